```python
import jax, jax.numpy as jnp
from jax import lax
import numpy as np

D_MODEL = 1024
BATCH = 8
SEQ = 8192
DEPTH = 2

CHUNK = 64
GMLP_BLOCK = 128
A_HEADS = 4
A_HEAD_DIM = 128
D_A = A_HEADS * A_HEAD_DIM
B_GROUPS = 4
D_B = 512
CONV_WIDTH = 3
D_FF = 2816
N_BRANCH = 2
D_IN = 2 * D_A + 3 * D_B + N_BRANCH * D_MODEL
RMS_EPS = 1e-6
LN_EPS = 1e-5

kernel_name = "hybrid_gmlp_shortconv_gated_encoder"


def rms_norm(x, g):
    xf = x.astype(jnp.float32)
    y = xf * lax.rsqrt(jnp.mean(xf * xf, axis=-1, keepdims=True) + RMS_EPS)
    return (y * g.astype(jnp.float32)).astype(x.dtype)


def layer_norm(x, g, b):
    xf = x.astype(jnp.float32)
    mu = jnp.mean(xf, axis=-1, keepdims=True)
    var = jnp.mean(jnp.square(xf - mu), axis=-1, keepdims=True)
    y = (xf - mu) * lax.rsqrt(var + LN_EPS)
    return (y * g.astype(jnp.float32) + b.astype(jnp.float32)).astype(x.dtype)


def causal_dwconv(x, w):
    k, c = w.shape
    return lax.conv_general_dilated(
        x, w[:, None, :].astype(x.dtype), window_strides=(1,), padding=[(k - 1, 0)],
        dimension_numbers=("NWC", "WIO", "NWC"), feature_group_count=c)


def spatial_mask():
    idx = jnp.arange(GMLP_BLOCK) // CHUNK
    return idx[None, :] <= idx[:, None]


def gmlp_branch(u, v, ln_g, ln_b, w_s, b_s, mask):
    bsz, s, _ = u.shape
    u = jax.nn.gelu(u)
    v = layer_norm(jax.nn.gelu(v), ln_g, ln_b)
    vb = v.reshape(bsz, s // GMLP_BLOCK, GMLP_BLOCK, A_HEADS, A_HEAD_DIM)
    w_m = jnp.where(mask[None], w_s, jnp.zeros((), w_s.dtype))
    f = jnp.einsum("hij,bnjhd->bnihd", w_m, vb) + b_s.T[None, None, :, :, None]
    return u * f.reshape(bsz, s, D_A)


def _fwd_setup_inputs(seed: int = 0) -> dict:
    key = jax.random.key(seed)
    ks = jax.random.split(key, 20)
    n = jax.random.normal
    f32 = jnp.float32
    return {
        "x": n(ks[0], (BATCH, SEQ, D_MODEL), f32),
        "norm1_g": 1.0 + 0.02 * n(ks[1], (DEPTH, D_MODEL), f32),
        "w_in": n(ks[2], (DEPTH, D_MODEL, D_IN), f32) * D_MODEL ** -0.5,
        "b_gate": 0.02 * n(ks[3], (DEPTH, N_BRANCH * D_MODEL), f32),
        "gmlp_ln_g": 1.0 + 0.02 * n(ks[4], (DEPTH, D_A), f32),
        "gmlp_ln_b": 0.02 * n(ks[5], (DEPTH, D_A), f32),
        "w_spatial": n(ks[6], (DEPTH, A_HEADS, GMLP_BLOCK, GMLP_BLOCK), f32) * (0.5 * GMLP_BLOCK ** -0.5),
        "b_spatial": 1.0 + 0.02 * n(ks[7], (DEPTH, A_HEADS, GMLP_BLOCK), f32),
        "w_shortconv": n(ks[8], (DEPTH, CONV_WIDTH, D_B), f32) * CONV_WIDTH ** -0.5,
        "w_branch": n(ks[9], (DEPTH, N_BRANCH, D_A, D_MODEL), f32) * D_A ** -0.5,
        "w_out": n(ks[10], (DEPTH, D_MODEL, D_MODEL), f32) * D_MODEL ** -0.5,
        "norm2_g": 1.0 + 0.02 * n(ks[11], (DEPTH, D_MODEL), f32),
        "w_ffn_up": n(ks[12], (DEPTH, D_MODEL, 2 * D_FF), f32) * D_MODEL ** -0.5,
        "w_ffn_conv": n(ks[13], (DEPTH, CONV_WIDTH, D_FF), f32) * CONV_WIDTH ** -0.5,
        "b_ffn_conv": 0.02 * n(ks[14], (DEPTH, D_FF), f32),
        "w_ffn_down": n(ks[15], (DEPTH, D_FF, D_MODEL), f32) * D_FF ** -0.5,
        "final_g": 1.0 + 0.02 * n(ks[16], (D_MODEL,), f32),
    }


def _fwd_reference(x, norm1_g, w_in, b_gate, gmlp_ln_g, gmlp_ln_b, w_spatial, b_spatial,
              w_shortconv, w_branch, w_out, norm2_g, w_ffn_up, w_ffn_conv, b_ffn_conv,
              w_ffn_down, final_g):
    mask = spatial_mask()
    cuts = [D_A, 2 * D_A, 2 * D_A + D_B, 2 * D_A + 2 * D_B, 2 * D_A + 3 * D_B,
            2 * D_A + 3 * D_B + D_MODEL]
    for l in range(DEPTH):
        h = rms_norm(x, norm1_g[l])
        z = h @ w_in[l]
        u, v, bg, cg, hb, ga, gb = jnp.split(z, cuts, axis=-1)
        y_a = gmlp_branch(u, v, gmlp_ln_g[l], gmlp_ln_b[l], w_spatial[l], b_spatial[l], mask)
        y_b = bg * causal_dwconv(cg * hb, w_shortconv[l])
        g_a = jax.nn.sigmoid(ga + b_gate[l, :D_MODEL])
        g_b = jax.nn.sigmoid(gb + b_gate[l, D_MODEL:])
        merged = g_a * (y_a @ w_branch[l, 0]) + g_b * (y_b @ w_branch[l, 1])
        x = x + merged @ w_out[l]
        h = rms_norm(x, norm2_g[l])
        up = h @ w_ffn_up[l]
        gate, val = up[..., :D_FF], up[..., D_FF:]
        gate = causal_dwconv(gate, w_ffn_conv[l]) + b_ffn_conv[l]
        x = x + (jax.nn.silu(gate) * val) @ w_ffn_down[l]
    return rms_norm(x, final_g)


import jax as _jax
import jax.numpy as _jnp

TWIN_FORMAT = 'train_step'
FWD_PARAMS = ['x', 'norm1_g', 'w_in', 'b_gate', 'gmlp_ln_g', 'gmlp_ln_b', 'w_spatial', 'b_spatial', 'w_shortconv', 'w_branch', 'w_out', 'norm2_g', 'w_ffn_up', 'w_ffn_conv', 'b_ffn_conv', 'w_ffn_down', 'final_g']
TWIN_WEIGHTS = ['norm1_g', 'w_in', 'b_gate', 'gmlp_ln_g', 'gmlp_ln_b', 'w_spatial', 'b_spatial', 'w_shortconv', 'w_branch', 'w_out', 'norm2_g', 'w_ffn_up', 'w_ffn_conv', 'b_ffn_conv', 'w_ffn_down', 'final_g']
TWIN_DIFF_INPUT = 'x'
TWIN_INPUTS = ['x', 'norm1_g', 'w_in', 'b_gate', 'gmlp_ln_g', 'gmlp_ln_b', 'w_spatial', 'b_spatial', 'w_shortconv', 'w_branch', 'w_out', 'norm2_g', 'w_ffn_up', 'w_ffn_conv', 'b_ffn_conv', 'w_ffn_down', 'final_g', 'loss_target', 'm_norm1_g', 'm_w_in', 'm_b_gate', 'm_gmlp_ln_g', 'm_gmlp_ln_b', 'm_w_spatial', 'm_b_spatial', 'm_w_shortconv', 'm_w_branch', 'm_w_out', 'm_norm2_g', 'm_w_ffn_up', 'm_w_ffn_conv', 'm_b_ffn_conv', 'm_w_ffn_down', 'm_final_g', 'v_norm1_g', 'v_w_in', 'v_b_gate', 'v_gmlp_ln_g', 'v_gmlp_ln_b', 'v_w_spatial', 'v_b_spatial', 'v_w_shortconv', 'v_w_branch', 'v_w_out', 'v_norm2_g', 'v_w_ffn_up', 'v_w_ffn_conv', 'v_b_ffn_conv', 'v_w_ffn_down', 'v_final_g']
TWIN_OUTPUTS = ['loss', 'grad_x', 'grad_norm1_g', 'grad_w_in', 'grad_b_gate', 'grad_gmlp_ln_g', 'grad_gmlp_ln_b', 'grad_w_spatial', 'grad_b_spatial', 'grad_w_shortconv', 'grad_w_branch', 'grad_w_out', 'grad_norm2_g', 'grad_w_ffn_up', 'grad_w_ffn_conv', 'grad_b_ffn_conv', 'grad_w_ffn_down', 'grad_final_g', 'delta_norm1_g', 'delta_w_in', 'delta_b_gate', 'delta_gmlp_ln_g', 'delta_gmlp_ln_b', 'delta_w_spatial', 'delta_b_spatial', 'delta_w_shortconv', 'delta_w_branch', 'delta_w_out', 'delta_norm2_g', 'delta_w_ffn_up', 'delta_w_ffn_conv', 'delta_b_ffn_conv', 'delta_w_ffn_down', 'delta_final_g', 'new_m_norm1_g', 'new_m_w_in', 'new_m_b_gate', 'new_m_gmlp_ln_g', 'new_m_gmlp_ln_b', 'new_m_w_spatial', 'new_m_b_spatial', 'new_m_w_shortconv', 'new_m_w_branch', 'new_m_w_out', 'new_m_norm2_g', 'new_m_w_ffn_up', 'new_m_w_ffn_conv', 'new_m_b_ffn_conv', 'new_m_w_ffn_down', 'new_m_final_g', 'new_v_norm1_g', 'new_v_w_in', 'new_v_b_gate', 'new_v_gmlp_ln_g', 'new_v_gmlp_ln_b', 'new_v_w_spatial', 'new_v_b_spatial', 'new_v_w_shortconv', 'new_v_w_branch', 'new_v_w_out', 'new_v_norm2_g', 'new_v_w_ffn_up', 'new_v_w_ffn_conv', 'new_v_b_ffn_conv', 'new_v_w_ffn_down', 'new_v_final_g']
TWIN_LEAF_KINDS = {'loss': 'loss', 'grad_x': 'grad_x', 'grad_norm1_g': 'grad_w', 'grad_w_in': 'grad_w', 'grad_b_gate': 'grad_w', 'grad_gmlp_ln_g': 'grad_w', 'grad_gmlp_ln_b': 'grad_w', 'grad_w_spatial': 'grad_w', 'grad_b_spatial': 'grad_w', 'grad_w_shortconv': 'grad_w', 'grad_w_branch': 'grad_w', 'grad_w_out': 'grad_w', 'grad_norm2_g': 'grad_w', 'grad_w_ffn_up': 'grad_w', 'grad_w_ffn_conv': 'grad_w', 'grad_b_ffn_conv': 'grad_w', 'grad_w_ffn_down': 'grad_w', 'grad_final_g': 'grad_w', 'delta_norm1_g': 'delta_w', 'delta_w_in': 'delta_w', 'delta_b_gate': 'delta_w', 'delta_gmlp_ln_g': 'delta_w', 'delta_gmlp_ln_b': 'delta_w', 'delta_w_spatial': 'delta_w', 'delta_b_spatial': 'delta_w', 'delta_w_shortconv': 'delta_w', 'delta_w_branch': 'delta_w', 'delta_w_out': 'delta_w', 'delta_norm2_g': 'delta_w', 'delta_w_ffn_up': 'delta_w', 'delta_w_ffn_conv': 'delta_w', 'delta_b_ffn_conv': 'delta_w', 'delta_w_ffn_down': 'delta_w', 'delta_final_g': 'delta_w', 'new_m_norm1_g': 'new_m', 'new_m_w_in': 'new_m', 'new_m_b_gate': 'new_m', 'new_m_gmlp_ln_g': 'new_m', 'new_m_gmlp_ln_b': 'new_m', 'new_m_w_spatial': 'new_m', 'new_m_b_spatial': 'new_m', 'new_m_w_shortconv': 'new_m', 'new_m_w_branch': 'new_m', 'new_m_w_out': 'new_m', 'new_m_norm2_g': 'new_m', 'new_m_w_ffn_up': 'new_m', 'new_m_w_ffn_conv': 'new_m', 'new_m_b_ffn_conv': 'new_m', 'new_m_w_ffn_down': 'new_m', 'new_m_final_g': 'new_m', 'new_v_norm1_g': 'new_v', 'new_v_w_in': 'new_v', 'new_v_b_gate': 'new_v', 'new_v_gmlp_ln_g': 'new_v', 'new_v_gmlp_ln_b': 'new_v', 'new_v_w_spatial': 'new_v', 'new_v_b_spatial': 'new_v', 'new_v_w_shortconv': 'new_v', 'new_v_w_branch': 'new_v', 'new_v_w_out': 'new_v', 'new_v_norm2_g': 'new_v', 'new_v_w_ffn_up': 'new_v', 'new_v_w_ffn_conv': 'new_v', 'new_v_b_ffn_conv': 'new_v', 'new_v_w_ffn_down': 'new_v', 'new_v_final_g': 'new_v'}


def _forward(args):
    return _fwd_reference(*[args[k] for k in FWD_PARAMS])


def _output_shape():
    out = _jax.eval_shape(lambda: _forward(_fwd_setup_inputs(0)))
    return out.shape, out.dtype

N_MICROBATCH = 1
ADAM_LR = 0.001
ADAM_B1 = 0.9
ADAM_B2 = 0.999
ADAM_EPS = 1e-08
ADAM_WD = 0.01
ADAM_STEP = 10
PER_EXAMPLE_BATCH_AXIS = {'x': 0, 'loss_target': 0}
SHARED_INPUTS = []
_WEIGHT_DTYPES = {'norm1_g': _jnp.float32, 'w_in': _jnp.float32, 'b_gate': _jnp.float32, 'gmlp_ln_g': _jnp.float32, 'gmlp_ln_b': _jnp.float32, 'w_spatial': _jnp.float32, 'b_spatial': _jnp.float32, 'w_shortconv': _jnp.float32, 'w_branch': _jnp.float32, 'w_out': _jnp.float32, 'norm2_g': _jnp.float32, 'w_ffn_up': _jnp.float32, 'w_ffn_conv': _jnp.float32, 'b_ffn_conv': _jnp.float32, 'w_ffn_down': _jnp.float32, 'final_g': _jnp.float32}
MOMENT_SCALE = {'norm1_g': 2.530991e-01, 'w_in': 1.178925e-01, 'b_gate': 4.420814e-02, 'gmlp_ln_g': 5.106313e-02, 'gmlp_ln_b': 5.904875e-02, 'w_spatial': 1.035505e-01, 'b_spatial': 1.240392e-01, 'w_shortconv': 1.848784e-01, 'w_branch': 1.106867e-01, 'w_out': 1.569608e-01, 'norm2_g': 1.806465e-01, 'w_ffn_up': 7.455952e-02, 'w_ffn_conv': 7.485196e-02, 'b_ffn_conv': 7.121968e-02, 'w_ffn_down': 1.212540e-01, 'final_g': 6.406670e+01}


def _to_microbatches(a, axis):
    t = _jnp.moveaxis(a, axis, 0)
    t = t.reshape((N_MICROBATCH, t.shape[0] // N_MICROBATCH) + t.shape[1:])
    return _jnp.moveaxis(t, 1, axis + 1)


def setup_inputs(seed: int = 0) -> dict:
    inp = _fwd_setup_inputs(seed)
    key = _jax.random.fold_in(_jax.random.key(seed), 7919)
    shape, _ = _output_shape()
    out = dict(inp)
    out["loss_target"] = _jax.random.normal(_jax.random.fold_in(key, 0), shape, _jnp.float32)
    for i, name in enumerate(TWIN_WEIGHTS):
        w = inp[name].astype(_jnp.float32)
        if MOMENT_SCALE is None:
            s = _jnp.sqrt(_jnp.mean(_jnp.square(w)) + 1e-30)
        else:
            s = MOMENT_SCALE[name]
        km, kv = _jax.random.split(_jax.random.fold_in(key, i + 1))
        out[name] = w
        out["m_" + name] = s * _jax.random.normal(km, w.shape, _jnp.float32)
        out["v_" + name] = (s * s) * _jax.random.uniform(kv, w.shape, _jnp.float32, 0.5, 1.5)
    if N_MICROBATCH > 1:
        for name, axis in PER_EXAMPLE_BATCH_AXIS.items():
            out[name] = _to_microbatches(out[name], axis)
    return {'x': out['x'], 'norm1_g': out['norm1_g'], 'w_in': out['w_in'], 'b_gate': out['b_gate'], 'gmlp_ln_g': out['gmlp_ln_g'], 'gmlp_ln_b': out['gmlp_ln_b'], 'w_spatial': out['w_spatial'], 'b_spatial': out['b_spatial'], 'w_shortconv': out['w_shortconv'], 'w_branch': out['w_branch'], 'w_out': out['w_out'], 'norm2_g': out['norm2_g'], 'w_ffn_up': out['w_ffn_up'], 'w_ffn_conv': out['w_ffn_conv'], 'b_ffn_conv': out['b_ffn_conv'], 'w_ffn_down': out['w_ffn_down'], 'final_g': out['final_g'], 'loss_target': out['loss_target'], 'm_norm1_g': out['m_norm1_g'], 'm_w_in': out['m_w_in'], 'm_b_gate': out['m_b_gate'], 'm_gmlp_ln_g': out['m_gmlp_ln_g'], 'm_gmlp_ln_b': out['m_gmlp_ln_b'], 'm_w_spatial': out['m_w_spatial'], 'm_b_spatial': out['m_b_spatial'], 'm_w_shortconv': out['m_w_shortconv'], 'm_w_branch': out['m_w_branch'], 'm_w_out': out['m_w_out'], 'm_norm2_g': out['m_norm2_g'], 'm_w_ffn_up': out['m_w_ffn_up'], 'm_w_ffn_conv': out['m_w_ffn_conv'], 'm_b_ffn_conv': out['m_b_ffn_conv'], 'm_w_ffn_down': out['m_w_ffn_down'], 'm_final_g': out['m_final_g'], 'v_norm1_g': out['v_norm1_g'], 'v_w_in': out['v_w_in'], 'v_b_gate': out['v_b_gate'], 'v_gmlp_ln_g': out['v_gmlp_ln_g'], 'v_gmlp_ln_b': out['v_gmlp_ln_b'], 'v_w_spatial': out['v_w_spatial'], 'v_b_spatial': out['v_b_spatial'], 'v_w_shortconv': out['v_w_shortconv'], 'v_w_branch': out['v_w_branch'], 'v_w_out': out['v_w_out'], 'v_norm2_g': out['v_norm2_g'], 'v_w_ffn_up': out['v_w_ffn_up'], 'v_w_ffn_conv': out['v_w_ffn_conv'], 'v_b_ffn_conv': out['v_b_ffn_conv'], 'v_w_ffn_down': out['v_w_ffn_down'], 'v_final_g': out['v_final_g']}


def _loss(weights, diff, rest, loss_target):
    with _jax.named_scope("forward"):
        args = {**rest, TWIN_DIFF_INPUT: diff, **{k: w.astype(_WEIGHT_DTYPES[k]) for k, w in weights.items()}}
        y = _forward(args)
    with _jax.named_scope("loss_head"):
        err = _jnp.square(y.astype(_jnp.float32) - loss_target)
        return 0.5 * _jnp.sum(_jnp.mean(err, axis=-1)) if err.ndim else 0.5 * err


def _adamw(w, g, m, v):
    m = ADAM_B1 * m + (1.0 - ADAM_B1) * g
    v = ADAM_B2 * v + (1.0 - ADAM_B2) * _jnp.square(g)
    m_hat = m / (1.0 - ADAM_B1 ** ADAM_STEP)
    v_hat = v / (1.0 - ADAM_B2 ** ADAM_STEP)
    delta = -ADAM_LR * (m_hat / (_jnp.sqrt(v_hat) + ADAM_EPS) + ADAM_WD * w)
    return delta, m, v


def reference(x, norm1_g, w_in, b_gate, gmlp_ln_g, gmlp_ln_b, w_spatial, b_spatial, w_shortconv, w_branch, w_out, norm2_g, w_ffn_up, w_ffn_conv, b_ffn_conv, w_ffn_down, final_g, loss_target, m_norm1_g, m_w_in, m_b_gate, m_gmlp_ln_g, m_gmlp_ln_b, m_w_spatial, m_b_spatial, m_w_shortconv, m_w_branch, m_w_out, m_norm2_g, m_w_ffn_up, m_w_ffn_conv, m_b_ffn_conv, m_w_ffn_down, m_final_g, v_norm1_g, v_w_in, v_b_gate, v_gmlp_ln_g, v_gmlp_ln_b, v_w_spatial, v_b_spatial, v_w_shortconv, v_w_branch, v_w_out, v_norm2_g, v_w_ffn_up, v_w_ffn_conv, v_b_ffn_conv, v_w_ffn_down, v_final_g):
    given = dict(x=x, norm1_g=norm1_g, w_in=w_in, b_gate=b_gate, gmlp_ln_g=gmlp_ln_g, gmlp_ln_b=gmlp_ln_b, w_spatial=w_spatial, b_spatial=b_spatial, w_shortconv=w_shortconv, w_branch=w_branch, w_out=w_out, norm2_g=norm2_g, w_ffn_up=w_ffn_up, w_ffn_conv=w_ffn_conv, b_ffn_conv=b_ffn_conv, w_ffn_down=w_ffn_down, final_g=final_g, loss_target=loss_target, m_norm1_g=m_norm1_g, m_w_in=m_w_in, m_b_gate=m_b_gate, m_gmlp_ln_g=m_gmlp_ln_g, m_gmlp_ln_b=m_gmlp_ln_b, m_w_spatial=m_w_spatial, m_b_spatial=m_b_spatial, m_w_shortconv=m_w_shortconv, m_w_branch=m_w_branch, m_w_out=m_w_out, m_norm2_g=m_norm2_g, m_w_ffn_up=m_w_ffn_up, m_w_ffn_conv=m_w_ffn_conv, m_b_ffn_conv=m_b_ffn_conv, m_w_ffn_down=m_w_ffn_down, m_final_g=m_final_g, v_norm1_g=v_norm1_g, v_w_in=v_w_in, v_b_gate=v_b_gate, v_gmlp_ln_g=v_gmlp_ln_g, v_gmlp_ln_b=v_gmlp_ln_b, v_w_spatial=v_w_spatial, v_b_spatial=v_b_spatial, v_w_shortconv=v_w_shortconv, v_w_branch=v_w_branch, v_w_out=v_w_out, v_norm2_g=v_norm2_g, v_w_ffn_up=v_w_ffn_up, v_w_ffn_conv=v_w_ffn_conv, v_b_ffn_conv=v_b_ffn_conv, v_w_ffn_down=v_w_ffn_down, v_final_g=v_final_g)
    weights = {n: given[n] for n in TWIN_WEIGHTS}
    shared = {n: given[n] for n in SHARED_INPUTS}
    per_example = {n: given[n] for n in ['x']}
    grad_fn = _jax.value_and_grad(_loss, argnums=(0, 1))

    def one_microbatch(ex, loss_target):
        ex = dict(ex)
        diff = ex.pop(TWIN_DIFF_INPUT)
        return grad_fn(weights, diff, {**shared, **ex}, loss_target)

    if N_MICROBATCH == 1:
        loss, (grad_w, grad_x) = one_microbatch(per_example, given["loss_target"])
    else:
        def body(carry, xs):
            loss_sum, grad_sum = carry
            l_k, (gw_k, gx_k) = one_microbatch(xs[0], xs[1])
            with _jax.named_scope("update"):
                return (loss_sum + l_k, _jax.tree.map(_jnp.add, grad_sum, gw_k)), gx_k

        init = (_jnp.zeros((), _jnp.float32), _jax.tree.map(_jnp.zeros_like, weights))
        (loss, grad_w), grad_x = _jax.lax.scan(body, init, (per_example, given["loss_target"]))
    with _jax.named_scope("update"):
        delta_w, new_m, new_v = {}, {}, {}
        for n in TWIN_WEIGHTS:
            delta_w[n], new_m[n], new_v[n] = _adamw(weights[n], grad_w[n], given["m_" + n], given["v_" + n])
    return (loss, grad_x, *[grad_w[n] for n in TWIN_WEIGHTS], *[delta_w[n] for n in TWIN_WEIGHTS],
            *[new_m[n] for n in TWIN_WEIGHTS], *[new_v[n] for n in TWIN_WEIGHTS])
```

```python
import functools

import jax
import jax.numpy as jnp
from jax import lax
from jax.experimental import pallas as pl
from jax.experimental.pallas import tpu as pltpu

F32 = jnp.float32
BF16 = jnp.bfloat16
MESH = pl.DeviceIdType.MESH
ANY = pl.BlockSpec(memory_space=pl.ANY)

D_MODEL = 1024
D_A = 512
D_B = 512
D_IN = 4608
D_FF = 2816
GMLP_BLOCK = 128
CHUNK = 64
A_HEADS = 4
N_LAYERS = 2
N_CHIPS = 4
RMS_EPS = 1e-6
LN_EPS = 1e-5
ADAM_LR = 0.001
ADAM_B1 = 0.9
ADAM_B2 = 0.999
ADAM_EPS = 1e-08
ADAM_WD = 0.01
ADAM_STEP = 10

C_U, C_V, C_BG, C_CG, C_HB, C_GA, C_GB = 0, 512, 1024, 1536, 2048, 2560, 3584

V7X_VMEM_LIMIT = 60 * 1024 * 1024
TM_MIX = 256
TM_FFN = 256
TM_EW = 512
GELU_C0 = 0.7978845608028654
GELU_C1 = 0.044715

BIG = {
    "w_in": (1024, 1152),
    "w_branch": (1024, 256),
    "w_out": (256, 1024),
    "w_ffn_up": (1024, 1408),
    "w_ffn_down": (704, 1024),
}
BIG_NAMES = tuple(BIG)


def _params(sem=("arbitrary",), vmem=V7X_VMEM_LIMIT):
    return pltpu.CompilerParams(dimension_semantics=sem, vmem_limit_bytes=vmem)


def _gelu(x):
    x2 = x * x
    t = jnp.tanh(GELU_C0 * x * (1.0 + GELU_C1 * x2))
    return 0.5 * x * (1.0 + t), t


def _gelu_grad(x, t):
    return 0.5 * (1.0 + t) + 0.5 * x * (1.0 - t * t) * GELU_C0 * (1.0 + 3.0 * GELU_C1 * x * x)


def _colsum8(v):
    r, n = v.shape
    return v.reshape(r // 8, 8, n).sum(axis=0)


def _dot(a, b):
    return jnp.dot(a, b, preferred_element_type=F32)


def _dot_nt(a, b):
    return lax.dot_general(a, b, (((1,), (1,)), ((), ())), preferred_element_type=F32)


def _dot_tn(a, b):
    return lax.dot_general(a, b, (((0,), (0,)), ((), ())), preferred_element_type=F32)


def _shift_down(v, carry, n):
    rows = lax.broadcasted_iota(jnp.int32, v.shape, 0)
    out = pltpu.roll(v, n, 0)
    for r in range(n):
        out = jnp.where(rows == r, carry[8 - n + r:8 - n + r + 1, :], out)
    return out


def _shift_up(v, carry, n):
    tm = v.shape[0]
    rows = lax.broadcasted_iota(jnp.int32, v.shape, 0)
    out = pltpu.roll(v, tm - n, 0)
    for r in range(n):
        out = jnp.where(rows == tm - n + r, carry[r:r + 1, :], out)
    return out


def _start_all(copies):
    for cp in copies:
        cp.start()


def _wait_all(copies):
    for cp in copies:
        cp.wait()


def _load_col_sharded(src, layer, dst, sems, first):
    cs = src.shape[-1]
    return [pltpu.make_async_copy(src.at[layer, k], dst.at[:, k * cs:(k + 1) * cs], sems.at[first + k])
            for k in range(N_CHIPS)]


def _load_row_sharded(src, layer, dst, sems, first):
    rs = src.shape[-2]
    return [pltpu.make_async_copy(src.at[layer, k], dst.at[k * rs:(k + 1) * rs, :], sems.at[first + k])
            for k in range(N_CHIPS)]


def _load_branch(src, layer, dst, sems, first):
    return [pltpu.make_async_copy(src.at[layer, k, pl.ds(m * D_A, D_A), :], dst.at[m, :, k * 256:(k + 1) * 256],
                                  sems.at[first + 2 * k + m])
            for k in range(N_CHIPS) for m in range(2)]


def _row_spec(tm, n, rev=None):
    if rev is None:
        return pl.BlockSpec((tm, n), lambda i: (i, 0))
    return pl.BlockSpec((tm, n), lambda i: (rev - 1 - i, 0))


def _const_spec(shape):
    nd = len(shape)
    return pl.BlockSpec(shape, lambda i: (0,) * nd)


def _mixer_fwd(layer, x, g1, bgate, lng, lnb, wm, bsf, wsc, win_g, wb_g, wout_g):
    t_len = x.shape[0]
    tm = min(TM_MIX, t_len)
    nt = t_len // tm
    nb = tm // GMLP_BLOCK

    def body(x_ref, g1_ref, bgate_ref, lng_ref, lnb_ref, wm_ref, bsf_ref, wsc_ref, win_hbm, wb_hbm, wout_hbm,
             z_ref, ya_ref, yb_ref, a_ref, b_ref, mg_ref, h_ref, x2_ref,
             win_v, wb_v, wout_v, carry, vn_s, f_s, sems):
        i = pl.program_id(0)

        @pl.when(i == 0)
        def _():
            cps = (_load_col_sharded(win_hbm, layer, win_v, sems, 0) + _load_branch(wb_hbm, layer, wb_v, sems, 4)
                   + _load_row_sharded(wout_hbm, layer, wout_v, sems, 12))
            _start_all(cps)
            carry[...] = jnp.zeros_like(carry)
            _wait_all(cps)

        xv = x_ref[...]
        r = lax.rsqrt(jnp.mean(xv * xv, axis=-1, keepdims=True) + RMS_EPS)
        h_ref[...] = (xv * r * g1_ref[...]).astype(BF16)

        def zcols(c0, c1):
            zc = _dot(h_ref[...], win_v[:, c0:c1])
            z_ref[:, c0:c1] = zc.astype(BF16)
            return zc

        vg, _ = _gelu(zcols(C_V, C_V + D_A))
        mu = jnp.mean(vg, axis=-1, keepdims=True)
        vc = vg - mu
        rstd = lax.rsqrt(jnp.mean(vc * vc, axis=-1, keepdims=True) + LN_EPS)
        vn_s[...] = (vc * rstd * lng_ref[...] + lnb_ref[...]).astype(BF16)
        for hd in range(A_HEADS):
            cols = slice(hd * 128, (hd + 1) * 128)
            vcat = jnp.concatenate([vn_s[b * 128:(b + 1) * 128, cols] for b in range(nb)], axis=1)
            fcat = _dot(wm_ref[hd], vcat)
            for b in range(nb):
                f_s[b * 128:(b + 1) * 128, cols] = fcat[:, b * 128:(b + 1) * 128]
        ug, _ = _gelu(zcols(C_U, C_U + D_A))
        bias = jnp.concatenate([bsf_ref[...]] * nb, axis=0)
        ya_ref[...] = (ug * (f_s[...] + bias)).astype(BF16)

        p = zcols(C_CG, C_CG + D_B) * zcols(C_HB, C_HB + D_B)
        cr = carry[...]
        q = wsc_ref[0:1, :] * _shift_down(p, cr, 2) + wsc_ref[1:2, :] * _shift_down(p, cr, 1) + wsc_ref[2:3, :] * p
        carry[...] = p[tm - 8:tm, :]
        yb_ref[...] = (zcols(C_BG, C_BG + D_B) * q).astype(BF16)

        av = _dot(ya_ref[...], wb_v[0])
        a_ref[...] = av.astype(BF16)
        mg = jax.nn.sigmoid(zcols(C_GA, C_GA + D_MODEL) + bgate_ref[:, 0:D_MODEL]) * av
        bv = _dot(yb_ref[...], wb_v[1])
        b_ref[...] = bv.astype(BF16)
        mg = mg + jax.nn.sigmoid(zcols(C_GB, C_GB + D_MODEL) + bgate_ref[:, D_MODEL:2 * D_MODEL]) * bv
        mg_ref[...] = mg.astype(BF16)
        x2_ref[...] = x_ref[...] + _dot(mg_ref[...], wout_v[...])

    outs = [
        jax.ShapeDtypeStruct((t_len, D_IN), BF16),
        jax.ShapeDtypeStruct((t_len, D_A), BF16),
        jax.ShapeDtypeStruct((t_len, D_B), BF16),
        jax.ShapeDtypeStruct((t_len, D_MODEL), BF16),
        jax.ShapeDtypeStruct((t_len, D_MODEL), BF16),
        jax.ShapeDtypeStruct((t_len, D_MODEL), BF16),
        jax.ShapeDtypeStruct((t_len, D_MODEL), BF16),
        jax.ShapeDtypeStruct((t_len, D_MODEL), F32),
    ]
    return pl.pallas_call(
        body, name=f"mixer_fwd_l{layer}", grid=(nt,),
        in_specs=[_row_spec(tm, D_MODEL), _const_spec((1, D_MODEL)), _const_spec((1, 2 * D_MODEL)),
                  _const_spec((1, D_A)), _const_spec((1, D_A)), _const_spec((A_HEADS, 128, 128)),
                  _const_spec((128, D_A)), _const_spec((8, D_B)), ANY, ANY, ANY],
        out_specs=[_row_spec(tm, o.shape[1]) for o in outs],
        out_shape=outs,
        scratch_shapes=[pltpu.VMEM((D_MODEL, D_IN), BF16), pltpu.VMEM((2, D_A, D_MODEL), BF16),
                        pltpu.VMEM((D_MODEL, D_MODEL), BF16), pltpu.VMEM((8, D_B), F32),
                        pltpu.VMEM((tm, D_A), BF16), pltpu.VMEM((tm, D_A), F32), pltpu.SemaphoreType.DMA((16,))],
        compiler_params=_params(),
    )(x, g1, bgate, lng, lnb, wm, bsf, wsc, win_g, wb_g, wout_g)


def _ffn_fwd(layer, x2, g2, wfc, bfc, wup_g, wdown_g):
    t_len = x2.shape[0]
    tm = min(TM_FFN, t_len)
    nt = t_len // tm

    def body(x_ref, g2_ref, wfc_ref, bfc_ref, wup_hbm, wdown_hbm, up_ref, act_ref, h_ref, x3_ref,
             wup_v, wdown_v, carry, sems):
        i = pl.program_id(0)

        @pl.when(i == 0)
        def _():
            cps = _load_col_sharded(wup_hbm, layer, wup_v, sems, 0) + _load_row_sharded(wdown_hbm, layer, wdown_v, sems, 4)
            _start_all(cps)
            carry[...] = jnp.zeros_like(carry)
            _wait_all(cps)

        xv = x_ref[...]
        r = lax.rsqrt(jnp.mean(xv * xv, axis=-1, keepdims=True) + RMS_EPS)
        h_ref[...] = (xv * r * g2_ref[...]).astype(BF16)
        gate = _dot(h_ref[...], wup_v[:, 0:D_FF])
        up_ref[:, 0:D_FF] = gate.astype(BF16)
        cr = carry[...]
        gc = (wfc_ref[0:1, :] * _shift_down(gate, cr, 2) + wfc_ref[1:2, :] * _shift_down(gate, cr, 1)
              + wfc_ref[2:3, :] * gate + bfc_ref[...])
        carry[...] = gate[tm - 8:tm, :]
        val = _dot(h_ref[...], wup_v[:, D_FF:2 * D_FF])
        up_ref[:, D_FF:2 * D_FF] = val.astype(BF16)
        act_ref[...] = (gc * jax.nn.sigmoid(gc) * val).astype(BF16)
        x3_ref[...] = x_ref[...] + _dot(act_ref[...], wdown_v[...])

    outs = [
        jax.ShapeDtypeStruct((t_len, 2 * D_FF), BF16),
        jax.ShapeDtypeStruct((t_len, D_FF), BF16),
        jax.ShapeDtypeStruct((t_len, D_MODEL), BF16),
        jax.ShapeDtypeStruct((t_len, D_MODEL), F32),
    ]
    return pl.pallas_call(
        body, name=f"ffn_fwd_l{layer}", grid=(nt,),
        in_specs=[_row_spec(tm, D_MODEL), _const_spec((1, D_MODEL)), _const_spec((8, D_FF)), _const_spec((1, D_FF)), ANY, ANY],
        out_specs=[_row_spec(tm, o.shape[1]) for o in outs],
        out_shape=outs,
        scratch_shapes=[pltpu.VMEM((D_MODEL, 2 * D_FF), BF16), pltpu.VMEM((D_FF, D_MODEL), BF16),
                        pltpu.VMEM((8, D_FF), F32), pltpu.SemaphoreType.DMA((8,))],
        compiler_params=_params(),
    )(x2, g2, wfc, bfc, wup_g, wdown_g)


def _loss_head(x3, target, gf):
    t_len = x3.shape[0]
    tm = min(TM_EW, t_len)
    nt = t_len // tm

    def body(x_ref, t_ref, gf_ref, dx_ref, dgf_ref, loss_ref):
        i = pl.program_id(0)

        @pl.when(i == 0)
        def _():
            dgf_ref[...] = jnp.zeros_like(dgf_ref)
            loss_ref[...] = jnp.zeros_like(loss_ref)

        xv = x_ref[...]
        r = lax.rsqrt(jnp.mean(xv * xv, axis=-1, keepdims=True) + RMS_EPS)
        xh = xv * r
        err = xh * gf_ref[...] - t_ref[...]
        loss_ref[...] += _colsum8(err * err)
        dy = err * (1.0 / D_MODEL)
        dgf_ref[...] += _colsum8(dy * xh)
        dxh = dy * gf_ref[...]
        dx_ref[...] = r * (dxh - xh * jnp.mean(dxh * xh, axis=-1, keepdims=True))

    return pl.pallas_call(
        body, name="loss_head", grid=(nt,),
        in_specs=[_row_spec(tm, D_MODEL), _row_spec(tm, D_MODEL), _const_spec((1, D_MODEL))],
        out_specs=[_row_spec(tm, D_MODEL), _const_spec((8, D_MODEL)), _const_spec((8, D_MODEL))],
        out_shape=[jax.ShapeDtypeStruct((t_len, D_MODEL), F32), jax.ShapeDtypeStruct((8, D_MODEL), F32),
                   jax.ShapeDtypeStruct((8, D_MODEL), F32)],
        compiler_params=_params(),
    )(x3, target, gf)


def _ffn_bwd(layer, dx3, x2, up, g2, wfc, bfc, wup_g, wdown_g):
    t_len = x2.shape[0]
    tm = min(TM_FFN, t_len)
    nt = t_len // tm
    hb = tm // 16

    def body(dx3_ref, x_ref, up_ref, halo_ref, g2_ref, wfc_ref, bfc_ref, wup_hbm, wdown_hbm,
             dx2_ref, dup_ref, dx3b_ref, dg2_ref, dbfc_ref, dwfc_ref,
             wup_v, wdown_v, carry, sems):
        i = pl.program_id(0)

        @pl.when(i == 0)
        def _():
            cps = _load_col_sharded(wup_hbm, layer, wup_v, sems, 0) + _load_row_sharded(wdown_hbm, layer, wdown_v, sems, 4)
            _start_all(cps)
            carry[...] = jnp.zeros_like(carry)
            dg2_ref[...] = jnp.zeros_like(dg2_ref)
            dbfc_ref[...] = jnp.zeros_like(dbfc_ref)
            dwfc_ref[...] = jnp.zeros_like(dwfc_ref)
            _wait_all(cps)

        dx3b_ref[...] = dx3_ref[...].astype(BF16)
        da = _dot_nt(dx3b_ref[...], wdown_v[...])
        gate = up_ref[:, 0:D_FF].astype(F32)
        first_tile = i == nt - 1
        halo = jnp.where(first_tile, 0.0, halo_ref[...].astype(F32)[8:16, :])
        g1s = _shift_down(gate, halo, 1)
        g2s = _shift_down(gate, halo, 2)
        gc = wfc_ref[0:1, :] * g2s + wfc_ref[1:2, :] * g1s + wfc_ref[2:3, :] * gate + bfc_ref[...]
        sg = jax.nn.sigmoid(gc)
        val = up_ref[:, D_FF:2 * D_FF].astype(F32)
        dup_ref[:, D_FF:2 * D_FF] = (da * gc * sg).astype(BF16)
        dgc = da * val * sg * (1.0 + gc * (1.0 - sg))
        dbfc_ref[...] += _colsum8(dgc)
        dwfc_ref[0] += _colsum8(dgc * g2s)
        dwfc_ref[1] += _colsum8(dgc * g1s)
        dwfc_ref[2] += _colsum8(dgc * gate)
        cr = carry[...]
        dgate = wfc_ref[2:3, :] * dgc + wfc_ref[1:2, :] * _shift_up(dgc, cr, 1) + wfc_ref[0:1, :] * _shift_up(dgc, cr, 2)
        carry[...] = dgc[0:8, :]
        dup_ref[:, 0:D_FF] = dgate.astype(BF16)
        dh = _dot_nt(dup_ref[...], wup_v[...])
        xv = x_ref[...]
        r = lax.rsqrt(jnp.mean(xv * xv, axis=-1, keepdims=True) + RMS_EPS)
        xh = xv * r
        dg2_ref[...] += _colsum8(dh * xh)
        dxh = dh * g2_ref[...]
        dx2_ref[...] = dx3_ref[...] + r * (dxh - xh * jnp.mean(dxh * xh, axis=-1, keepdims=True))

    outs = [
        jax.ShapeDtypeStruct((t_len, D_MODEL), F32),
        jax.ShapeDtypeStruct((t_len, 2 * D_FF), BF16),
        jax.ShapeDtypeStruct((t_len, D_MODEL), BF16),
        jax.ShapeDtypeStruct((8, D_MODEL), F32),
        jax.ShapeDtypeStruct((8, D_FF), F32),
        jax.ShapeDtypeStruct((3, 8, D_FF), F32),
    ]
    halo_spec = pl.BlockSpec((16, D_FF), lambda i: (jnp.maximum((nt - 1 - i) * hb - 1, 0), 0))
    return pl.pallas_call(
        body, name=f"ffn_bwd_l{layer}", grid=(nt,),
        in_specs=[_row_spec(tm, D_MODEL, nt), _row_spec(tm, D_MODEL, nt), _row_spec(tm, 2 * D_FF, nt), halo_spec,
                  _const_spec((1, D_MODEL)), _const_spec((8, D_FF)), _const_spec((1, D_FF)), ANY, ANY],
        out_specs=[_row_spec(tm, D_MODEL, nt), _row_spec(tm, 2 * D_FF, nt), _row_spec(tm, D_MODEL, nt),
                   _const_spec((8, D_MODEL)), _const_spec((8, D_FF)), _const_spec((3, 8, D_FF))],
        out_shape=outs,
        scratch_shapes=[pltpu.VMEM((D_MODEL, 2 * D_FF), BF16), pltpu.VMEM((D_FF, D_MODEL), BF16),
                        pltpu.VMEM((8, D_FF), F32), pltpu.SemaphoreType.DMA((8,))],
        compiler_params=_params(),
    )(dx3, x2, up, up, g2, wfc, bfc, wup_g, wdown_g)


def _mixer_bwd(layer, dx2, x, z, av, bv, g1, bgate, lng, lnb, wm, wmt, bsf, wsc, win_g, wb_g, wout_g):
    t_len = x.shape[0]
    tm = min(TM_MIX, t_len)
    nt = t_len // tm
    nb = tm // GMLP_BLOCK
    hb = tm // 16

    def body(dx2_ref, x_ref, z_ref, cg_halo_ref, hb_halo_ref, a_ref, b_ref, g1_ref, bgate_ref, lng_ref, lnb_ref,
             wm_ref, wmt_ref, bsf_ref, wsc_ref, win_hbm, wb_hbm, wout_hbm,
             dx_ref, dz_ref, da_ref, db_ref, dx2b_ref, dg1_ref, dbgate_ref, dlng_ref, dlnb_ref, dwm_ref, dbsf_ref, dwsc_ref,
             win_v, wb_v, wout_v, carry, vn_s, f_s, df_s, dvn_s, sems):
        i = pl.program_id(0)

        @pl.when(i == 0)
        def _():
            cps = (_load_col_sharded(win_hbm, layer, win_v, sems, 0) + _load_branch(wb_hbm, layer, wb_v, sems, 4)
                   + _load_row_sharded(wout_hbm, layer, wout_v, sems, 12))
            _start_all(cps)
            carry[...] = jnp.zeros_like(carry)
            for ref in (dg1_ref, dbgate_ref, dlng_ref, dlnb_ref, dwm_ref, dbsf_ref, dwsc_ref):
                ref[...] = jnp.zeros_like(ref)
            _wait_all(cps)

        def zc(c0, n):
            return z_ref[:, c0:c0 + n].astype(F32)

        dx2b_ref[...] = dx2_ref[...].astype(BF16)
        dm = _dot_nt(dx2b_ref[...], wout_v[...])
        sa = jax.nn.sigmoid(zc(C_GA, D_MODEL) + bgate_ref[:, 0:D_MODEL])
        da_ref[...] = (dm * sa).astype(BF16)
        dga = dm * a_ref[...].astype(F32) * sa * (1.0 - sa)
        dz_ref[:, C_GA:C_GA + D_MODEL] = dga.astype(BF16)
        dbgate_ref[:, 0:D_MODEL] += _colsum8(dga)
        sb = jax.nn.sigmoid(zc(C_GB, D_MODEL) + bgate_ref[:, D_MODEL:2 * D_MODEL])
        db_ref[...] = (dm * sb).astype(BF16)
        dgb = dm * b_ref[...].astype(F32) * sb * (1.0 - sb)
        dz_ref[:, C_GB:C_GB + D_MODEL] = dgb.astype(BF16)
        dbgate_ref[:, D_MODEL:2 * D_MODEL] += _colsum8(dgb)
        dya = _dot_nt(da_ref[...], wb_v[0])
        dyb = _dot_nt(db_ref[...], wb_v[1])

        v = zc(C_V, D_A)
        vg, tv = _gelu(v)
        mu = jnp.mean(vg, axis=-1, keepdims=True)
        vc = vg - mu
        rstd = lax.rsqrt(jnp.mean(vc * vc, axis=-1, keepdims=True) + LN_EPS)
        xh = vc * rstd
        vn_s[...] = (xh * lng_ref[...] + lnb_ref[...]).astype(BF16)
        u = zc(C_U, D_A)
        ug, tu = _gelu(u)
        df = dya * ug
        df_s[...] = df.astype(BF16)
        dbsf_acc = df[0:128, :]
        for b in range(1, nb):
            dbsf_acc = dbsf_acc + df[b * 128:(b + 1) * 128, :]
        dbsf_ref[...] += dbsf_acc
        for hd in range(A_HEADS):
            cols = slice(hd * 128, (hd + 1) * 128)
            vcat = jnp.concatenate([vn_s[b * 128:(b + 1) * 128, cols] for b in range(nb)], axis=1)
            dcat = jnp.concatenate([df_s[b * 128:(b + 1) * 128, cols] for b in range(nb)], axis=1)
            fcat = _dot(wm_ref[hd], vcat)
            gcat = _dot(wmt_ref[hd], dcat)
            dwm_ref[hd] += _dot_nt(dcat, vcat)
            for b in range(nb):
                f_s[b * 128:(b + 1) * 128, cols] = fcat[:, b * 128:(b + 1) * 128]
                dvn_s[b * 128:(b + 1) * 128, cols] = gcat[:, b * 128:(b + 1) * 128]
        bias = jnp.concatenate([bsf_ref[...]] * nb, axis=0)
        dz_ref[:, C_U:C_U + D_A] = (dya * (f_s[...] + bias) * _gelu_grad(u, tu)).astype(BF16)
        dvn = dvn_s[...]
        dlng_ref[...] += _colsum8(dvn * xh)
        dlnb_ref[...] += _colsum8(dvn)
        dxh = dvn * lng_ref[...]
        dvg = rstd * (dxh - jnp.mean(dxh, axis=-1, keepdims=True) - xh * jnp.mean(dxh * xh, axis=-1, keepdims=True))
        dz_ref[:, C_V:C_V + D_A] = (dvg * _gelu_grad(v, tv)).astype(BF16)

        first_tile = i == nt - 1
        halo = jnp.where(first_tile, 0.0, (cg_halo_ref[...].astype(F32) * hb_halo_ref[...].astype(F32))[8:16, :])
        cg = zc(C_CG, D_B)
        hbv = zc(C_HB, D_B)
        bg = zc(C_BG, D_B)
        p = cg * hbv
        p1 = _shift_down(p, halo, 1)
        p2 = _shift_down(p, halo, 2)
        q = wsc_ref[0:1, :] * p2 + wsc_ref[1:2, :] * p1 + wsc_ref[2:3, :] * p
        dz_ref[:, C_BG:C_BG + D_B] = (dyb * q).astype(BF16)
        dq = dyb * bg
        dwsc_ref[0] += _colsum8(dq * p2)
        dwsc_ref[1] += _colsum8(dq * p1)
        dwsc_ref[2] += _colsum8(dq * p)
        cr = carry[...]
        dp = wsc_ref[2:3, :] * dq + wsc_ref[1:2, :] * _shift_up(dq, cr, 1) + wsc_ref[0:1, :] * _shift_up(dq, cr, 2)
        carry[...] = dq[0:8, :]
        dz_ref[:, C_CG:C_CG + D_B] = (dp * hbv).astype(BF16)
        dz_ref[:, C_HB:C_HB + D_B] = (dp * cg).astype(BF16)

        dh = _dot_nt(dz_ref[...], win_v[...])
        xv = x_ref[...]
        r = lax.rsqrt(jnp.mean(xv * xv, axis=-1, keepdims=True) + RMS_EPS)
        xn = xv * r
        dg1_ref[...] += _colsum8(dh * xn)
        dxn = dh * g1_ref[...]
        dx_ref[...] = dx2_ref[...] + r * (dxn - xn * jnp.mean(dxn * xn, axis=-1, keepdims=True))

    outs = [
        jax.ShapeDtypeStruct((t_len, D_MODEL), F32),
        jax.ShapeDtypeStruct((t_len, D_IN), BF16),
        jax.ShapeDtypeStruct((t_len, D_MODEL), BF16),
        jax.ShapeDtypeStruct((t_len, D_MODEL), BF16),
        jax.ShapeDtypeStruct((t_len, D_MODEL), BF16),
        jax.ShapeDtypeStruct((8, D_MODEL), F32),
        jax.ShapeDtypeStruct((8, 2 * D_MODEL), F32),
        jax.ShapeDtypeStruct((8, D_A), F32),
        jax.ShapeDtypeStruct((8, D_A), F32),
        jax.ShapeDtypeStruct((A_HEADS, 128, 128), F32),
        jax.ShapeDtypeStruct((128, D_A), F32),
        jax.ShapeDtypeStruct((3, 8, D_B), F32),
    ]

    def halo_spec(col):
        return pl.BlockSpec((16, D_B), lambda i: (jnp.maximum((nt - 1 - i) * hb - 1, 0), col))

    return pl.pallas_call(
        body, name=f"mixer_bwd_l{layer}", grid=(nt,),
        in_specs=[_row_spec(tm, D_MODEL, nt), _row_spec(tm, D_MODEL, nt), _row_spec(tm, D_IN, nt),
                  halo_spec(C_CG // D_B), halo_spec(C_HB // D_B),
                  _row_spec(tm, D_MODEL, nt), _row_spec(tm, D_MODEL, nt),
                  _const_spec((1, D_MODEL)), _const_spec((1, 2 * D_MODEL)), _const_spec((1, D_A)), _const_spec((1, D_A)),
                  _const_spec((A_HEADS, 128, 128)), _const_spec((A_HEADS, 128, 128)), _const_spec((128, D_A)),
                  _const_spec((8, D_B)), ANY, ANY, ANY],
        out_specs=[_row_spec(tm, D_MODEL, nt), _row_spec(tm, D_IN, nt), _row_spec(tm, D_MODEL, nt),
                   _row_spec(tm, D_MODEL, nt), _row_spec(tm, D_MODEL, nt),
                   _const_spec((8, D_MODEL)), _const_spec((8, 2 * D_MODEL)), _const_spec((8, D_A)), _const_spec((8, D_A)),
                   _const_spec((A_HEADS, 128, 128)), _const_spec((128, D_A)), _const_spec((3, 8, D_B))],
        out_shape=outs,
        scratch_shapes=[pltpu.VMEM((D_MODEL, D_IN), BF16), pltpu.VMEM((2, D_A, D_MODEL), BF16),
                        pltpu.VMEM((D_MODEL, D_MODEL), BF16), pltpu.VMEM((8, D_B), F32),
                        pltpu.VMEM((tm, D_A), BF16), pltpu.VMEM((tm, D_A), F32), pltpu.VMEM((tm, D_A), BF16),
                        pltpu.VMEM((tm, D_A), F32), pltpu.SemaphoreType.DMA((16,))],
        compiler_params=_params(),
    )(dx2, x, z, z, z, av, bv, g1, bgate, lng, lnb, wm, wmt, bsf, wsc, win_g, wb_g, wout_g)


def _wgrad(name, layer, a, b, rows, cols, row_blk, col_blk, row_off, prev):
    t_len, m = a.shape
    n = b.shape[1]
    tk = min(1024, t_len)
    nk = t_len // tk
    col_sharded = n == N_CHIPS * cols
    grid = (m // row_blk, n // col_blk, nk)
    per_shard_c = cols // col_blk
    off_blk = row_off // row_blk

    if col_sharded:
        out_shape = (N_LAYERS, N_CHIPS, rows, cols)
        out_spec = pl.BlockSpec((None, None, row_blk, col_blk),
                                lambda i, j, k: (layer, j // per_shard_c, off_blk + i, j % per_shard_c))
    else:
        out_shape = (N_LAYERS, N_CHIPS * rows, cols)
        out_spec = pl.BlockSpec((None, row_blk, col_blk), lambda i, j, k: (layer, i, j))

    def body(*refs):
        a_ref, b_ref, o_ref = refs[0], refs[1], refs[-1]
        k = pl.program_id(2)

        @pl.when(k == 0)
        def _():
            o_ref[...] = jnp.zeros_like(o_ref)

        o_ref[...] += _dot_tn(a_ref[...], b_ref[...])

    in_specs = [pl.BlockSpec((tk, row_blk), lambda i, j, k: (k, i)), pl.BlockSpec((tk, col_blk), lambda i, j, k: (k, j))]
    args = [a, b]
    aliases = {}
    if prev is not None:
        in_specs.append(ANY)
        args.append(prev)
        aliases = {2: 0}
    return pl.pallas_call(
        body, name=f"wgrad_{name}_l{layer}_r{row_off}", grid=grid,
        in_specs=in_specs,
        out_specs=out_spec,
        out_shape=jax.ShapeDtypeStruct(out_shape, F32),
        input_output_aliases=aliases,
        compiler_params=_params(("parallel", "parallel", "arbitrary")),
    )(*args)


def _mesh_pos():
    return lax.axis_index("x"), lax.axis_index("y"), lax.axis_index("c")


def _other_chips(x, y):
    return [(1 - x, y, 2 * (1 - x) + y), (x, 1 - y, 2 * x + (1 - y)), (1 - x, 1 - y, 2 * (1 - x) + (1 - y))]


def _all_gather_weights(shards):
    nw = len(shards)

    def body(*refs):
        src = refs[:nw]
        dst = refs[nw:2 * nw]
        ici_send, ici_recv, d2d_send, d2d_recv, loc = refs[2 * nw:]
        x, y, c = _mesh_pos()
        me = 2 * x + y
        sibling = (x, y, 1 - c)
        chips = _other_chips(x, y)

        def remote(s_ref, d_ref, ssem, rsem, to):
            return pltpu.make_async_remote_copy(src_ref=s_ref, dst_ref=d_ref, send_sem=ssem, recv_sem=rsem,
                                                device_id=to, device_id_type=MESH)

        local, own, ici, fwd = [], [], [], []
        for w in range(nw):
            for la in range(N_LAYERS):
                local.append(pltpu.make_async_copy(src[w].at[la], dst[w].at[la, me], loc.at[w, la]))
            for j, (px, py, pk) in enumerate(chips):
                ici.append(remote(src[w].at[c], dst[w].at[c, me], ici_send.at[w, j], ici_recv.at[w, j], (px, py, c)))
        _start_all(local)
        _start_all(ici)
        for w in range(nw):
            for j, (px, py, pk) in enumerate(chips):
                remote(src[w].at[c], dst[w].at[c, pk], ici_send.at[w, j], ici_recv.at[w, j], (px, py, c)).wait_recv()
                cp = remote(dst[w].at[c, pk], dst[w].at[c, pk], d2d_send.at[w, j], d2d_recv.at[w, j], sibling)
                cp.start()
                fwd.append(cp)
        for w in range(nw):
            for j, (px, py, pk) in enumerate(chips):
                remote(dst[w].at[1 - c, pk], dst[w].at[1 - c, pk], d2d_send.at[w, j], d2d_recv.at[w, j], sibling).wait_recv()
        for cp in ici + fwd:
            cp.wait_send()
        _wait_all(local)

    out_shape = [jax.ShapeDtypeStruct((N_LAYERS, N_CHIPS) + s.shape[1:], s.dtype) for s in shards]
    return pl.pallas_call(
        body, name="all_gather_weights",
        in_specs=[ANY] * nw, out_specs=[ANY] * nw, out_shape=out_shape,
        scratch_shapes=[pltpu.SemaphoreType.DMA((nw, 3)), pltpu.SemaphoreType.DMA((nw, 3)),
                        pltpu.SemaphoreType.DMA((nw, 3)), pltpu.SemaphoreType.DMA((nw, 3)),
                        pltpu.SemaphoreType.DMA((nw, N_LAYERS))],
            )(*shards)


def _pair_exchange(grads):
    nw = len(grads)

    def body(*refs):
        src = refs[:nw]
        dst = refs[nw:2 * nw]
        send, recv = refs[2 * nw:]
        x, y, c = _mesh_pos()
        cps = [pltpu.make_async_remote_copy(src_ref=src[w].at[1 - c], dst_ref=dst[w], send_sem=send.at[w],
                                            recv_sem=recv.at[w], device_id=(x, y, 1 - c), device_id_type=MESH)
               for w in range(nw)]
        _start_all(cps)
        _wait_all(cps)

    return pl.pallas_call(
        body, name="grad_pair_exchange",
        in_specs=[ANY] * nw, out_specs=[ANY] * nw,
        out_shape=[jax.ShapeDtypeStruct(g.shape[1:], g.dtype) for g in grads],
        scratch_shapes=[pltpu.SemaphoreType.DMA((nw,)), pltpu.SemaphoreType.DMA((nw,))],
            )(*grads)


def _chip_exchange(sums):
    nw = len(sums)

    def body(*refs):
        src = refs[:nw]
        dst = refs[nw:2 * nw]
        send, recv = refs[2 * nw:]
        x, y, c = _mesh_pos()
        cps = [pltpu.make_async_remote_copy(src_ref=src[w].at[pk], dst_ref=dst[w].at[j], send_sem=send.at[w, j],
                                            recv_sem=recv.at[w, j], device_id=(px, py, c), device_id_type=MESH)
               for w in range(nw) for j, (px, py, pk) in enumerate(_other_chips(x, y))]
        _start_all(cps)
        _wait_all(cps)

    return pl.pallas_call(
        body, name="grad_chip_exchange",
        in_specs=[ANY] * nw, out_specs=[ANY] * nw,
        out_shape=[jax.ShapeDtypeStruct((3,) + s.shape[1:], s.dtype) for s in sums],
        scratch_shapes=[pltpu.SemaphoreType.DMA((nw, 3)), pltpu.SemaphoreType.DMA((nw, 3))],
            )(*sums)


def _pair_gather(finals):
    nw = len(finals)

    def body(*refs):
        src = refs[:nw]
        dst = refs[nw:2 * nw]
        send, recv, loc = refs[2 * nw:]
        x, y, c = _mesh_pos()
        local = [pltpu.make_async_copy(src[w], dst[w].at[c], loc.at[w]) for w in range(nw)]
        cps = [pltpu.make_async_remote_copy(src_ref=src[w], dst_ref=dst[w].at[c], send_sem=send.at[w],
                                            recv_sem=recv.at[w], device_id=(x, y, 1 - c), device_id_type=MESH)
               for w in range(nw)]
        _start_all(local)
        _start_all(cps)
        _wait_all(cps)
        _wait_all(local)

    return pl.pallas_call(
        body, name="grad_pair_gather",
        in_specs=[ANY] * nw, out_specs=[ANY] * nw,
        out_shape=[jax.ShapeDtypeStruct((N_LAYERS,) + f.shape, f.dtype) for f in finals],
        scratch_shapes=[pltpu.SemaphoreType.DMA((nw,)), pltpu.SemaphoreType.DMA((nw,)), pltpu.SemaphoreType.DMA((nw,))],
            )(*finals)


def _all_reduce_small(name, packed):
    rows = packed.shape[0]

    def body(src_ref, out_ref, slots, send, recv):
        x, y, c = _mesh_pos()
        me = 4 * x + 2 * y + c
        cps = []
        for d in range(1, 8):
            peer = me ^ d
            cps.append(pltpu.make_async_remote_copy(
                src_ref=src_ref, dst_ref=slots.at[me], send_sem=send.at[d - 1], recv_sem=recv.at[d - 1],
                device_id=(peer // 4, (peer // 2) % 2, peer % 2), device_id_type=MESH))
        _start_all(cps)
        slots[me] = src_ref[...]
        _wait_all(cps)
        acc = slots[0]
        for d in range(1, 8):
            acc = acc + slots[d]
        out_ref[...] = acc

    return pl.pallas_call(
        body, name=f"all_reduce_{name}",
        in_specs=[pl.BlockSpec(memory_space=pltpu.VMEM)], out_specs=pl.BlockSpec(memory_space=pltpu.VMEM),
        out_shape=jax.ShapeDtypeStruct(packed.shape, F32),
        scratch_shapes=[pltpu.VMEM((8, rows, 128), F32), pltpu.SemaphoreType.DMA((7,)), pltpu.SemaphoreType.DMA((7,))],
        compiler_params=pltpu.CompilerParams(vmem_limit_bytes=V7X_VMEM_LIMIT),
    )(packed)


def _flat_blk(rows, cols):
    blk = rows
    while blk * cols * 4 > 2 * 1024 * 1024 and blk % 16 == 0:
        blk //= 2
    return blk


def _pair_sum(name, grad, other, c):
    _, _, rows, cols = grad.shape
    blk = _flat_blk(rows, cols)

    def body(c_ref, g_ref, o_ref, s32_ref, s16_ref):
        s = g_ref[...] + o_ref[...]
        s32_ref[...] = s
        s16_ref[...] = s.astype(BF16)

    spec3 = pl.BlockSpec((None, blk, cols), lambda k, i, c_ref: (k, i, 0))
    return pl.pallas_call(
        body, name=f"pair_sum_{name}",
        grid_spec=pltpu.PrefetchScalarGridSpec(
            num_scalar_prefetch=1, grid=(N_CHIPS, rows // blk),
            in_specs=[pl.BlockSpec((None, None, blk, cols), lambda k, i, c_ref: (c_ref[0], k, i, 0)), spec3],
            out_specs=[spec3, spec3]),
        out_shape=[jax.ShapeDtypeStruct((N_CHIPS, rows, cols), F32), jax.ShapeDtypeStruct((N_CHIPS, rows, cols), BF16)],
        compiler_params=_params(("parallel", "parallel")),
    )(c, grad, other)


def _chip_sum(name, s32, got, me):
    _, rows, cols = s32.shape
    blk = _flat_blk(rows, cols)

    def body(me_ref, s_ref, g_ref, o_ref):
        o_ref[...] = ((s_ref[...] + g_ref[0].astype(F32)) + g_ref[1].astype(F32)) + g_ref[2].astype(F32)

    return pl.pallas_call(
        body, name=f"chip_sum_{name}",
        grid_spec=pltpu.PrefetchScalarGridSpec(
            num_scalar_prefetch=1, grid=(rows // blk,),
            in_specs=[pl.BlockSpec((None, blk, cols), lambda i, me_ref: (me_ref[0], i, 0)),
                      pl.BlockSpec((3, blk, cols), lambda i, me_ref: (0, i, 0))],
            out_specs=pl.BlockSpec((blk, cols), lambda i, me_ref: (i, 0))),
        out_shape=jax.ShapeDtypeStruct((rows, cols), F32),
        compiler_params=_params(("parallel",)),
    )(me, s32, got)


def _adamw_math(w, g, m, v):
    m2 = ADAM_B1 * m + (1.0 - ADAM_B1) * g
    v2 = ADAM_B2 * v + (1.0 - ADAM_B2) * (g * g)
    m_hat = m2 / (1.0 - ADAM_B1 ** ADAM_STEP)
    v_hat = v2 / (1.0 - ADAM_B2 ** ADAM_STEP)
    delta = -ADAM_LR * (m_hat / (jnp.sqrt(v_hat) + ADAM_EPS) + ADAM_WD * w)
    return delta, m2, v2


def _adamw(name, w, g, m, v):
    rows, cols = w.shape
    blk = _flat_blk(rows, cols)

    def body(w_ref, g_ref, m_ref, v_ref, d_ref, m2_ref, v2_ref):
        d, m2, v2 = _adamw_math(w_ref[...], g_ref[...], m_ref[...], v_ref[...])
        d_ref[...] = d
        m2_ref[...] = m2
        v2_ref[...] = v2

    spec = pl.BlockSpec((blk, cols), lambda i: (i, 0))
    return pl.pallas_call(
        body, name=f"adamw_{name}", grid=(rows // blk,),
        in_specs=[spec] * 4, out_specs=[spec] * 3,
        out_shape=[jax.ShapeDtypeStruct((rows, cols), F32)] * 3,
        compiler_params=_params(("parallel",)),
    )(w, g, m, v)


SMALL = ("norm1_g", "b_gate", "gmlp_ln_g", "gmlp_ln_b", "w_spatial", "b_spatial", "w_shortconv", "norm2_g",
         "w_ffn_conv", "b_ffn_conv", "final_g")
ALL_WEIGHTS = ("norm1_g", "w_in", "b_gate", "gmlp_ln_g", "gmlp_ln_b", "w_spatial", "b_spatial", "w_shortconv",
               "w_branch", "w_out", "norm2_g", "w_ffn_up", "w_ffn_conv", "b_ffn_conv", "w_ffn_down", "final_g")


def _pack(arrays):
    flat = jnp.concatenate([a.reshape(-1) for a in arrays])
    n = flat.shape[0]
    rows = -(-n // 1024) * 8
    return jnp.pad(flat, (0, rows * 128 - n)).reshape(rows, 128)


def _unpack(packed, like):
    flat = packed.reshape(-1)
    out, off = [], 0
    for a in like:
        out.append(flat[off:off + a.size].reshape(a.shape))
        off += a.size
    return out


def _pad8(w):
    return jnp.pad(w, ((0, 5), (0, 0)))


def kernel(x, norm1_g, w_in, b_gate, gmlp_ln_g, gmlp_ln_b, w_spatial, b_spatial, w_shortconv, w_branch, w_out, norm2_g, w_ffn_up, w_ffn_conv, b_ffn_conv, w_ffn_down, final_g, loss_target, m_norm1_g, m_w_in, m_b_gate, m_gmlp_ln_g, m_gmlp_ln_b, m_w_spatial, m_b_spatial, m_w_shortconv, m_w_branch, m_w_out, m_norm2_g, m_w_ffn_up, m_w_ffn_conv, m_b_ffn_conv, m_w_ffn_down, m_final_g, v_norm1_g, v_w_in, v_b_gate, v_gmlp_ln_g, v_gmlp_ln_b, v_w_spatial, v_b_spatial, v_w_shortconv, v_w_branch, v_w_out, v_norm2_g, v_w_ffn_up, v_w_ffn_conv, v_b_ffn_conv, v_w_ffn_down, v_final_g):
    weights = dict(norm1_g=norm1_g, w_in=w_in, b_gate=b_gate, gmlp_ln_g=gmlp_ln_g, gmlp_ln_b=gmlp_ln_b,
                   w_spatial=w_spatial, b_spatial=b_spatial, w_shortconv=w_shortconv, w_branch=w_branch, w_out=w_out,
                   norm2_g=norm2_g, w_ffn_up=w_ffn_up, w_ffn_conv=w_ffn_conv, b_ffn_conv=b_ffn_conv,
                   w_ffn_down=w_ffn_down, final_g=final_g)
    mom = dict(norm1_g=m_norm1_g, w_in=m_w_in, b_gate=m_b_gate, gmlp_ln_g=m_gmlp_ln_g, gmlp_ln_b=m_gmlp_ln_b,
               w_spatial=m_w_spatial, b_spatial=m_b_spatial, w_shortconv=m_w_shortconv, w_branch=m_w_branch,
               w_out=m_w_out, norm2_g=m_norm2_g, w_ffn_up=m_w_ffn_up, w_ffn_conv=m_w_ffn_conv,
               b_ffn_conv=m_b_ffn_conv, w_ffn_down=m_w_ffn_down, final_g=m_final_g)
    vel = dict(norm1_g=v_norm1_g, w_in=v_w_in, b_gate=v_b_gate, gmlp_ln_g=v_gmlp_ln_g, gmlp_ln_b=v_gmlp_ln_b,
               w_spatial=v_w_spatial, b_spatial=v_b_spatial, w_shortconv=v_w_shortconv, w_branch=v_w_branch,
               w_out=v_w_out, norm2_g=v_norm2_g, w_ffn_up=v_w_ffn_up, w_ffn_conv=v_w_ffn_conv,
               b_ffn_conv=v_b_ffn_conv, w_ffn_down=v_w_ffn_down, final_g=v_final_g)

    cx, cy, cc = _mesh_pos()
    chip = 2 * cx + cy
    t_len = x.shape[1]
    xs = x.reshape(t_len, D_MODEL)
    target = loss_target.reshape(t_len, D_MODEL)

    shards = [weights[n].astype(BF16).reshape((N_LAYERS,) + BIG[n]) for n in BIG_NAMES]
    win_g, wb_g, wout_g, wup_g, wdown_g = _all_gather_weights(shards)

    idx = jnp.arange(GMLP_BLOCK) // CHUNK
    mask = idx[None, :] <= idx[:, None]
    wm_all = jnp.where(mask[None, None], w_spatial, 0.0)
    wm_bf = wm_all.astype(BF16)
    wmt_bf = jnp.swapaxes(wm_all, -1, -2).astype(BF16)
    bsf = jnp.repeat(jnp.swapaxes(b_spatial, -1, -2), 128, axis=-1)
    wsc_full = lax.dynamic_update_slice(jnp.zeros((N_LAYERS, 3, D_B), F32), w_shortconv, (0, 0, chip * (D_B // 4)))
    wfc_full = lax.dynamic_update_slice(jnp.zeros((N_LAYERS, 3, D_FF), F32), w_ffn_conv, (0, 0, chip * (D_FF // 4)))
    taps = _all_reduce_small("conv_taps", _pack([wsc_full, wfc_full]))
    wsc_full, wfc_full = _unpack(taps * 0.5, [wsc_full, wfc_full])

    def row(a):
        return a.reshape(1, -1)

    saved = []
    h_in = xs
    for la in range(N_LAYERS):
        z, ya, yb, av, bv, mg, h1, x2 = _mixer_fwd(
            la, h_in, row(norm1_g[la]), row(b_gate[la]), row(gmlp_ln_g[la]), row(gmlp_ln_b[la]), wm_bf[la], bsf[la],
            _pad8(wsc_full[la]), win_g, wb_g, wout_g)
        up, act, h2, x3 = _ffn_fwd(la, x2, row(norm2_g[la]), _pad8(wfc_full[la]), row(b_ffn_conv[la]), wup_g, wdown_g)
        saved.append(dict(x=h_in, z=z, ya=ya, yb=yb, av=av, bv=bv, mg=mg, h1=h1, x2=x2, up=up, act=act, h2=h2))
        h_in = x3

    dx, dgf8, loss8 = _loss_head(h_in, target, row(final_g))
    small = {n: [None] * N_LAYERS for n in SMALL}
    big = {n: None for n in BIG_NAMES}
    for la in reversed(range(N_LAYERS)):
        s = saved[la]
        dx3 = dx
        dx2, dup, dx3b, dg2, dbfc, dwfc = _ffn_bwd(la, dx3, s["x2"], s["up"], row(norm2_g[la]), _pad8(wfc_full[la]),
                                                   row(b_ffn_conv[la]), wup_g, wdown_g)
        big["w_ffn_down"] = _wgrad("w_ffn_down", la, s["act"], dx3b, 704, 1024, 1408, 1024, 0, big["w_ffn_down"])
        big["w_ffn_up"] = _wgrad("w_ffn_up", la, s["h2"], dup, 1024, 1408, 1024, 1408, 0, big["w_ffn_up"])
        dxl, dz, da, db, dx2b, dg1, dbg, dlng, dlnb, dwm, dbsf, dwsc = _mixer_bwd(
            la, dx2, s["x"], s["z"], s["av"], s["bv"], row(norm1_g[la]), row(b_gate[la]), row(gmlp_ln_g[la]),
            row(gmlp_ln_b[la]), wm_bf[la], wmt_bf[la], bsf[la], _pad8(wsc_full[la]), win_g, wb_g, wout_g)
        big["w_out"] = _wgrad("w_out", la, s["mg"], dx2b, 256, 1024, 1024, 1024, 0, big["w_out"])
        big["w_branch"] = _wgrad("w_branch", la, s["ya"], da, 1024, 256, 512, 256, 0, big["w_branch"])
        big["w_branch"] = _wgrad("w_branch", la, s["yb"], db, 1024, 256, 512, 256, 512, big["w_branch"])
        big["w_in"] = _wgrad("w_in", la, s["h1"], dz, 1024, 1152, 1024, 1152, 0, big["w_in"])
        small["norm1_g"][la] = dg1.sum(0)
        small["b_gate"][la] = dbg.sum(0)
        small["gmlp_ln_g"][la] = dlng.sum(0)
        small["gmlp_ln_b"][la] = dlnb.sum(0)
        small["w_spatial"][la] = jnp.where(mask[None], dwm, 0.0)
        small["b_spatial"][la] = dbsf.reshape(128, A_HEADS, 128).sum(-1).T
        small["w_shortconv"][la] = dwsc.sum(1)
        small["norm2_g"][la] = dg2.sum(0)
        small["w_ffn_conv"][la] = dwfc.sum(1)
        small["b_ffn_conv"][la] = dbfc.sum(0)
        dx = dxl
    grad_x = dx.reshape(x.shape)

    small_local = [jnp.stack(small[n]) for n in SMALL[:-1]] + [dgf8.sum(0), 0.5 * loss8.sum().reshape(1) / D_MODEL]
    reduced = _unpack(_all_reduce_small("small_grads", _pack(small_local)), small_local)
    loss = reduced[-1].reshape(())
    grads = dict(zip(SMALL, reduced[:-1]))
    grads["w_shortconv"] = lax.dynamic_slice(grads["w_shortconv"], (0, 0, chip * (D_B // 4)), (N_LAYERS, 3, D_B // 4))
    grads["w_ffn_conv"] = lax.dynamic_slice(grads["w_ffn_conv"], (0, 0, chip * (D_FF // 4)), (N_LAYERS, 3, D_FF // 4))

    partial = [big[n].reshape((N_LAYERS, N_CHIPS) + BIG[n]) for n in BIG_NAMES]
    from_sibling = _pair_exchange(partial)
    c_arr = cc.astype(jnp.int32).reshape(1)
    chip_arr = chip.astype(jnp.int32).reshape(1)
    sums = [_pair_sum(n, g, o, c_arr) for n, g, o in zip(BIG_NAMES, partial, from_sibling)]
    got = _chip_exchange([s16 for _, s16 in sums])
    finals = [_chip_sum(n, s32, r, chip_arr) for n, (s32, _), r in zip(BIG_NAMES, sums, got)]
    for n, g in zip(BIG_NAMES, _pair_gather(finals)):
        grads[n] = g.reshape(weights[n].shape)

    delta, new_m, new_v = {}, {}, {}
    for n in BIG_NAMES:
        rows, cols = BIG[n]
        shape2 = (N_LAYERS * rows, cols)
        d, m2, v2 = _adamw(n, weights[n].reshape(shape2), grads[n].reshape(shape2), mom[n].reshape(shape2),
                           vel[n].reshape(shape2))
        delta[n], new_m[n], new_v[n] = (a.reshape(weights[n].shape) for a in (d, m2, v2))
    small_w = [weights[n] for n in SMALL]
    packed = [_pack([src[n] for n in SMALL]) for src in (weights, grads, mom, vel)]
    for dst, res in zip((delta, new_m, new_v), _adamw("small", *packed)):
        dst.update(zip(SMALL, _unpack(res, small_w)))

    return (loss, grad_x, *[grads[n] for n in ALL_WEIGHTS], *[delta[n] for n in ALL_WEIGHTS],
            *[new_m[n] for n in ALL_WEIGHTS], *[new_v[n] for n in ALL_WEIGHTS])
```

```python
import functools

import jax
import jax.numpy as jnp
from jax import lax
from jax.experimental import pallas as pl
from jax.experimental.pallas import tpu as pltpu

F32 = jnp.float32
BF16 = jnp.bfloat16
MESH = pl.DeviceIdType.MESH
ANY = pl.BlockSpec(memory_space=pl.ANY)

D_MODEL = 1024
D_A = 512
D_B = 512
D_IN = 4608
D_FF = 2816
GMLP_BLOCK = 128
CHUNK = 64
A_HEADS = 4
N_LAYERS = 2
N_CHIPS = 4
RMS_EPS = 1e-6
LN_EPS = 1e-5
ADAM_LR = 0.001
ADAM_B1 = 0.9
ADAM_B2 = 0.999
ADAM_EPS = 1e-08
ADAM_WD = 0.01
ADAM_STEP = 10

C_U, C_V, C_BG, C_CG, C_HB, C_GA, C_GB = 0, 512, 1024, 1536, 2048, 2560, 3584

V7X_VMEM_LIMIT = 60 * 1024 * 1024
TM_MIX = 256
TM_FFN = 256
TM_EW = 512
GELU_C0 = 0.7978845608028654
GELU_C1 = 0.044715

BIG = {
    "w_in": (1024, 1152),
    "w_branch": (1024, 256),
    "w_out": (256, 1024),
    "w_ffn_up": (1024, 1408),
    "w_ffn_down": (704, 1024),
}
BIG_NAMES = tuple(BIG)


def _params(sem=("arbitrary",), vmem=V7X_VMEM_LIMIT):
    return pltpu.CompilerParams(dimension_semantics=sem, vmem_limit_bytes=vmem)


def _gelu(x):
    x2 = x * x
    t = jnp.tanh(GELU_C0 * x * (1.0 + GELU_C1 * x2))
    return 0.5 * x * (1.0 + t), t


def _gelu_grad(x, t):
    return 0.5 * (1.0 + t) + 0.5 * x * (1.0 - t * t) * GELU_C0 * (1.0 + 3.0 * GELU_C1 * x * x)


def _colsum8(v):
    r, n = v.shape
    return v.reshape(r // 8, 8, n).sum(axis=0)


def _dot(a, b):
    return jnp.dot(a, b, preferred_element_type=F32)


def _dot_nt(a, b):
    return lax.dot_general(a, b, (((1,), (1,)), ((), ())), preferred_element_type=F32)


def _dot_tn(a, b):
    return lax.dot_general(a, b, (((0,), (0,)), ((), ())), preferred_element_type=F32)


def _shift_down(v, carry, n):
    rows = lax.broadcasted_iota(jnp.int32, v.shape, 0)
    out = pltpu.roll(v, n, 0)
    for r in range(n):
        out = jnp.where(rows == r, carry[8 - n + r:8 - n + r + 1, :], out)
    return out


def _shift_up(v, carry, n):
    tm = v.shape[0]
    rows = lax.broadcasted_iota(jnp.int32, v.shape, 0)
    out = pltpu.roll(v, tm - n, 0)
    for r in range(n):
        out = jnp.where(rows == tm - n + r, carry[r:r + 1, :], out)
    return out


def _start_all(copies):
    for cp in copies:
        cp.start()


def _wait_all(copies):
    for cp in copies:
        cp.wait()


def _load_col_sharded(src, layer, dst, sems, first):
    cs = src.shape[-1]
    return [pltpu.make_async_copy(src.at[layer, k], dst.at[:, k * cs:(k + 1) * cs], sems.at[first + k])
            for k in range(N_CHIPS)]


def _load_row_sharded(src, layer, dst, sems, first):
    rs = src.shape[-2]
    return [pltpu.make_async_copy(src.at[layer, k], dst.at[k * rs:(k + 1) * rs, :], sems.at[first + k])
            for k in range(N_CHIPS)]


def _load_branch(src, layer, dst, sems, first):
    return [pltpu.make_async_copy(src.at[layer, k, pl.ds(m * D_A, D_A), :], dst.at[m, :, k * 256:(k + 1) * 256],
                                  sems.at[first + 2 * k + m])
            for k in range(N_CHIPS) for m in range(2)]


def _row_spec(tm, n, rev=None):
    if rev is None:
        return pl.BlockSpec((tm, n), lambda i: (i, 0))
    return pl.BlockSpec((tm, n), lambda i: (rev - 1 - i, 0))


def _const_spec(shape):
    nd = len(shape)
    return pl.BlockSpec(shape, lambda i: (0,) * nd)


def _mixer_fwd(layer, x, g1, bgate, lng, lnb, wm, bsf, wsc, win_g, wb_g, wout_g):
    t_len = x.shape[0]
    tm = min(TM_MIX, t_len)
    nt = t_len // tm
    nb = tm // GMLP_BLOCK

    def body(x_ref, g1_ref, bgate_ref, lng_ref, lnb_ref, wm_ref, bsf_ref, wsc_ref, win_hbm, wb_hbm, wout_hbm,
             z_ref, ya_ref, yb_ref, a_ref, b_ref, mg_ref, h_ref, x2_ref,
             win_v, wb_v, wout_v, carry, vn_s, f_s, sems):
        i = pl.program_id(0)

        @pl.when(i == 0)
        def _():
            cps = (_load_col_sharded(win_hbm, layer, win_v, sems, 0) + _load_branch(wb_hbm, layer, wb_v, sems, 4)
                   + _load_row_sharded(wout_hbm, layer, wout_v, sems, 12))
            _start_all(cps)
            carry[...] = jnp.zeros_like(carry)
            _wait_all(cps)

        xv = x_ref[...]
        r = lax.rsqrt(jnp.mean(xv * xv, axis=-1, keepdims=True) + RMS_EPS)
        h_ref[...] = (xv * r * g1_ref[...]).astype(BF16)

        def zcols(c0, c1):
            zc = _dot(h_ref[...], win_v[:, c0:c1])
            z_ref[:, c0:c1] = zc.astype(BF16)
            return zc

        vg, _ = _gelu(zcols(C_V, C_V + D_A))
        mu = jnp.mean(vg, axis=-1, keepdims=True)
        vc = vg - mu
        rstd = lax.rsqrt(jnp.mean(vc * vc, axis=-1, keepdims=True) + LN_EPS)
        vn_s[...] = (vc * rstd * lng_ref[...] + lnb_ref[...]).astype(BF16)
        for hd in range(A_HEADS):
            cols = slice(hd * 128, (hd + 1) * 128)
            vcat = jnp.concatenate([vn_s[b * 128:(b + 1) * 128, cols] for b in range(nb)], axis=1)
            fcat = _dot(wm_ref[hd], vcat)
            for b in range(nb):
                f_s[b * 128:(b + 1) * 128, cols] = fcat[:, b * 128:(b + 1) * 128]
        ug, _ = _gelu(zcols(C_U, C_U + D_A))
        bias = jnp.concatenate([bsf_ref[...]] * nb, axis=0)
        ya_ref[...] = (ug * (f_s[...] + bias)).astype(BF16)

        p = zcols(C_CG, C_CG + D_B) * zcols(C_HB, C_HB + D_B)
        cr = carry[...]
        q = wsc_ref[0:1, :] * _shift_down(p, cr, 2) + wsc_ref[1:2, :] * _shift_down(p, cr, 1) + wsc_ref[2:3, :] * p
        carry[...] = p[tm - 8:tm, :]
        yb_ref[...] = (zcols(C_BG, C_BG + D_B) * q).astype(BF16)

        av = _dot(ya_ref[...], wb_v[0])
        a_ref[...] = av.astype(BF16)
        mg = jax.nn.sigmoid(zcols(C_GA, C_GA + D_MODEL) + bgate_ref[:, 0:D_MODEL]) * av
        bv = _dot(yb_ref[...], wb_v[1])
        b_ref[...] = bv.astype(BF16)
        mg = mg + jax.nn.sigmoid(zcols(C_GB, C_GB + D_MODEL) + bgate_ref[:, D_MODEL:2 * D_MODEL]) * bv
        mg_ref[...] = mg.astype(BF16)
        x2_ref[...] = x_ref[...] + _dot(mg_ref[...], wout_v[...])

    outs = [
        jax.ShapeDtypeStruct((t_len, D_IN), BF16),
        jax.ShapeDtypeStruct((t_len, D_A), BF16),
        jax.ShapeDtypeStruct((t_len, D_B), BF16),
        jax.ShapeDtypeStruct((t_len, D_MODEL), BF16),
        jax.ShapeDtypeStruct((t_len, D_MODEL), BF16),
        jax.ShapeDtypeStruct((t_len, D_MODEL), BF16),
        jax.ShapeDtypeStruct((t_len, D_MODEL), BF16),
        jax.ShapeDtypeStruct((t_len, D_MODEL), F32),
    ]
    return pl.pallas_call(
        body, name=f"mixer_fwd_l{layer}", grid=(nt,),
        in_specs=[_row_spec(tm, D_MODEL), _const_spec((1, D_MODEL)), _const_spec((1, 2 * D_MODEL)),
                  _const_spec((1, D_A)), _const_spec((1, D_A)), _const_spec((A_HEADS, 128, 128)),
                  _const_spec((128, D_A)), _const_spec((8, D_B)), ANY, ANY, ANY],
        out_specs=[_row_spec(tm, o.shape[1]) for o in outs],
        out_shape=outs,
        scratch_shapes=[pltpu.VMEM((D_MODEL, D_IN), BF16), pltpu.VMEM((2, D_A, D_MODEL), BF16),
                        pltpu.VMEM((D_MODEL, D_MODEL), BF16), pltpu.VMEM((8, D_B), F32),
                        pltpu.VMEM((tm, D_A), BF16), pltpu.VMEM((tm, D_A), F32), pltpu.SemaphoreType.DMA((16,))],
        compiler_params=_params(),
    )(x, g1, bgate, lng, lnb, wm, bsf, wsc, win_g, wb_g, wout_g)


def _ffn_fwd(layer, x2, g2, wfc, bfc, wup_g, wdown_g):
    t_len = x2.shape[0]
    tm = min(TM_FFN, t_len)
    nt = t_len // tm

    def body(x_ref, g2_ref, wfc_ref, bfc_ref, wup_hbm, wdown_hbm, up_ref, act_ref, h_ref, x3_ref,
             wup_v, wdown_v, carry, sems):
        i = pl.program_id(0)

        @pl.when(i == 0)
        def _():
            cps = _load_col_sharded(wup_hbm, layer, wup_v, sems, 0) + _load_row_sharded(wdown_hbm, layer, wdown_v, sems, 4)
            _start_all(cps)
            carry[...] = jnp.zeros_like(carry)
            _wait_all(cps)

        xv = x_ref[...]
        r = lax.rsqrt(jnp.mean(xv * xv, axis=-1, keepdims=True) + RMS_EPS)
        h_ref[...] = (xv * r * g2_ref[...]).astype(BF16)
        gate = _dot(h_ref[...], wup_v[:, 0:D_FF])
        up_ref[:, 0:D_FF] = gate.astype(BF16)
        cr = carry[...]
        gc = (wfc_ref[0:1, :] * _shift_down(gate, cr, 2) + wfc_ref[1:2, :] * _shift_down(gate, cr, 1)
              + wfc_ref[2:3, :] * gate + bfc_ref[...])
        carry[...] = gate[tm - 8:tm, :]
        val = _dot(h_ref[...], wup_v[:, D_FF:2 * D_FF])
        up_ref[:, D_FF:2 * D_FF] = val.astype(BF16)
        act_ref[...] = (gc * jax.nn.sigmoid(gc) * val).astype(BF16)
        x3_ref[...] = x_ref[...] + _dot(act_ref[...], wdown_v[...])

    outs = [
        jax.ShapeDtypeStruct((t_len, 2 * D_FF), BF16),
        jax.ShapeDtypeStruct((t_len, D_FF), BF16),
        jax.ShapeDtypeStruct((t_len, D_MODEL), BF16),
        jax.ShapeDtypeStruct((t_len, D_MODEL), F32),
    ]
    return pl.pallas_call(
        body, name=f"ffn_fwd_l{layer}", grid=(nt,),
        in_specs=[_row_spec(tm, D_MODEL), _const_spec((1, D_MODEL)), _const_spec((8, D_FF)), _const_spec((1, D_FF)), ANY, ANY],
        out_specs=[_row_spec(tm, o.shape[1]) for o in outs],
        out_shape=outs,
        scratch_shapes=[pltpu.VMEM((D_MODEL, 2 * D_FF), BF16), pltpu.VMEM((D_FF, D_MODEL), BF16),
                        pltpu.VMEM((8, D_FF), F32), pltpu.SemaphoreType.DMA((8,))],
        compiler_params=_params(),
    )(x2, g2, wfc, bfc, wup_g, wdown_g)


def _loss_head(x3, target, gf):
    t_len = x3.shape[0]
    tm = min(TM_EW, t_len)
    nt = t_len // tm

    def body(x_ref, t_ref, gf_ref, dx_ref, dgf_ref, loss_ref):
        i = pl.program_id(0)

        @pl.when(i == 0)
        def _():
            dgf_ref[...] = jnp.zeros_like(dgf_ref)
            loss_ref[...] = jnp.zeros_like(loss_ref)

        xv = x_ref[...]
        r = lax.rsqrt(jnp.mean(xv * xv, axis=-1, keepdims=True) + RMS_EPS)
        xh = xv * r
        err = xh * gf_ref[...] - t_ref[...]
        loss_ref[...] += _colsum8(err * err)
        dy = err * (1.0 / D_MODEL)
        dgf_ref[...] += _colsum8(dy * xh)
        dxh = dy * gf_ref[...]
        dx_ref[...] = r * (dxh - xh * jnp.mean(dxh * xh, axis=-1, keepdims=True))

    return pl.pallas_call(
        body, name="loss_head", grid=(nt,),
        in_specs=[_row_spec(tm, D_MODEL), _row_spec(tm, D_MODEL), _const_spec((1, D_MODEL))],
        out_specs=[_row_spec(tm, D_MODEL), _const_spec((8, D_MODEL)), _const_spec((8, D_MODEL))],
        out_shape=[jax.ShapeDtypeStruct((t_len, D_MODEL), F32), jax.ShapeDtypeStruct((8, D_MODEL), F32),
                   jax.ShapeDtypeStruct((8, D_MODEL), F32)],
        compiler_params=_params(),
    )(x3, target, gf)


def _ffn_bwd(layer, dx3, x2, up, g2, wfc, bfc, wup_g, wdown_g):
    t_len = x2.shape[0]
    tm = min(TM_FFN, t_len)
    nt = t_len // tm
    hb = tm // 16

    def body(dx3_ref, x_ref, up_ref, halo_ref, g2_ref, wfc_ref, bfc_ref, wup_hbm, wdown_hbm,
             dx2_ref, dup_ref, dx3b_ref, dg2_ref, dbfc_ref, dwfc_ref,
             wup_v, wdown_v, carry, sems):
        i = pl.program_id(0)

        @pl.when(i == 0)
        def _():
            cps = _load_col_sharded(wup_hbm, layer, wup_v, sems, 0) + _load_row_sharded(wdown_hbm, layer, wdown_v, sems, 4)
            _start_all(cps)
            carry[...] = jnp.zeros_like(carry)
            dg2_ref[...] = jnp.zeros_like(dg2_ref)
            dbfc_ref[...] = jnp.zeros_like(dbfc_ref)
            dwfc_ref[...] = jnp.zeros_like(dwfc_ref)
            _wait_all(cps)

        dx3b_ref[...] = dx3_ref[...].astype(BF16)
        da = _dot_nt(dx3b_ref[...], wdown_v[...])
        gate = up_ref[:, 0:D_FF].astype(F32)
        first_tile = i == nt - 1
        halo = jnp.where(first_tile, 0.0, halo_ref[...].astype(F32)[8:16, :])
        g1s = _shift_down(gate, halo, 1)
        g2s = _shift_down(gate, halo, 2)
        gc = wfc_ref[0:1, :] * g2s + wfc_ref[1:2, :] * g1s + wfc_ref[2:3, :] * gate + bfc_ref[...]
        sg = jax.nn.sigmoid(gc)
        val = up_ref[:, D_FF:2 * D_FF].astype(F32)
        dup_ref[:, D_FF:2 * D_FF] = (da * gc * sg).astype(BF16)
        dgc = da * val * sg * (1.0 + gc * (1.0 - sg))
        dbfc_ref[...] += _colsum8(dgc)
        dwfc_ref[0] += _colsum8(dgc * g2s)
        dwfc_ref[1] += _colsum8(dgc * g1s)
        dwfc_ref[2] += _colsum8(dgc * gate)
        cr = carry[...]
        dgate = wfc_ref[2:3, :] * dgc + wfc_ref[1:2, :] * _shift_up(dgc, cr, 1) + wfc_ref[0:1, :] * _shift_up(dgc, cr, 2)
        carry[...] = dgc[0:8, :]
        dup_ref[:, 0:D_FF] = dgate.astype(BF16)
        dh = _dot_nt(dup_ref[...], wup_v[...])
        xv = x_ref[...]
        r = lax.rsqrt(jnp.mean(xv * xv, axis=-1, keepdims=True) + RMS_EPS)
        xh = xv * r
        dg2_ref[...] += _colsum8(dh * xh)
        dxh = dh * g2_ref[...]
        dx2_ref[...] = dx3_ref[...] + r * (dxh - xh * jnp.mean(dxh * xh, axis=-1, keepdims=True))

    outs = [
        jax.ShapeDtypeStruct((t_len, D_MODEL), F32),
        jax.ShapeDtypeStruct((t_len, 2 * D_FF), BF16),
        jax.ShapeDtypeStruct((t_len, D_MODEL), BF16),
        jax.ShapeDtypeStruct((8, D_MODEL), F32),
        jax.ShapeDtypeStruct((8, D_FF), F32),
        jax.ShapeDtypeStruct((3, 8, D_FF), F32),
    ]
    halo_spec = pl.BlockSpec((16, D_FF), lambda i: (jnp.maximum((nt - 1 - i) * hb - 1, 0), 0))
    return pl.pallas_call(
        body, name=f"ffn_bwd_l{layer}", grid=(nt,),
        in_specs=[_row_spec(tm, D_MODEL, nt), _row_spec(tm, D_MODEL, nt), _row_spec(tm, 2 * D_FF, nt), halo_spec,
                  _const_spec((1, D_MODEL)), _const_spec((8, D_FF)), _const_spec((1, D_FF)), ANY, ANY],
        out_specs=[_row_spec(tm, D_MODEL, nt), _row_spec(tm, 2 * D_FF, nt), _row_spec(tm, D_MODEL, nt),
                   _const_spec((8, D_MODEL)), _const_spec((8, D_FF)), _const_spec((3, 8, D_FF))],
        out_shape=outs,
        scratch_shapes=[pltpu.VMEM((D_MODEL, 2 * D_FF), BF16), pltpu.VMEM((D_FF, D_MODEL), BF16),
                        pltpu.VMEM((8, D_FF), F32), pltpu.SemaphoreType.DMA((8,))],
        compiler_params=_params(),
    )(dx3, x2, up, up, g2, wfc, bfc, wup_g, wdown_g)


def _mixer_bwd(layer, dx2, x, z, av, bv, g1, bgate, lng, lnb, wm, wmt, bsf, wsc, win_g, wb_g, wout_g):
    t_len = x.shape[0]
    tm = min(TM_MIX, t_len)
    nt = t_len // tm
    nb = tm // GMLP_BLOCK
    hb = tm // 16

    def body(dx2_ref, x_ref, z_ref, cg_halo_ref, hb_halo_ref, a_ref, b_ref, g1_ref, bgate_ref, lng_ref, lnb_ref,
             wm_ref, wmt_ref, bsf_ref, wsc_ref, win_hbm, wb_hbm, wout_hbm,
             dx_ref, dz_ref, da_ref, db_ref, dx2b_ref, dg1_ref, dbgate_ref, dlng_ref, dlnb_ref, dwm_ref, dbsf_ref, dwsc_ref,
             win_v, wb_v, wout_v, carry, vn_s, f_s, df_s, dvn_s, sems):
        i = pl.program_id(0)

        @pl.when(i == 0)
        def _():
            cps = (_load_col_sharded(win_hbm, layer, win_v, sems, 0) + _load_branch(wb_hbm, layer, wb_v, sems, 4)
                   + _load_row_sharded(wout_hbm, layer, wout_v, sems, 12))
            _start_all(cps)
            carry[...] = jnp.zeros_like(carry)
            for ref in (dg1_ref, dbgate_ref, dlng_ref, dlnb_ref, dwm_ref, dbsf_ref, dwsc_ref):
                ref[...] = jnp.zeros_like(ref)
            _wait_all(cps)

        def zc(c0, n):
            return z_ref[:, c0:c0 + n].astype(F32)

        dx2b_ref[...] = dx2_ref[...].astype(BF16)
        dm = _dot_nt(dx2b_ref[...], wout_v[...])
        sa = jax.nn.sigmoid(zc(C_GA, D_MODEL) + bgate_ref[:, 0:D_MODEL])
        da_ref[...] = (dm * sa).astype(BF16)
        dga = dm * a_ref[...].astype(F32) * sa * (1.0 - sa)
        dz_ref[:, C_GA:C_GA + D_MODEL] = dga.astype(BF16)
        dbgate_ref[:, 0:D_MODEL] += _colsum8(dga)
        sb = jax.nn.sigmoid(zc(C_GB, D_MODEL) + bgate_ref[:, D_MODEL:2 * D_MODEL])
        db_ref[...] = (dm * sb).astype(BF16)
        dgb = dm * b_ref[...].astype(F32) * sb * (1.0 - sb)
        dz_ref[:, C_GB:C_GB + D_MODEL] = dgb.astype(BF16)
        dbgate_ref[:, D_MODEL:2 * D_MODEL] += _colsum8(dgb)
        dya = _dot_nt(da_ref[...], wb_v[0])
        dyb = _dot_nt(db_ref[...], wb_v[1])

        v = zc(C_V, D_A)
        vg, tv = _gelu(v)
        mu = jnp.mean(vg, axis=-1, keepdims=True)
        vc = vg - mu
        rstd = lax.rsqrt(jnp.mean(vc * vc, axis=-1, keepdims=True) + LN_EPS)
        xh = vc * rstd
        vn_s[...] = (xh * lng_ref[...] + lnb_ref[...]).astype(BF16)
        u = zc(C_U, D_A)
        ug, tu = _gelu(u)
        df = dya * ug
        df_s[...] = df.astype(BF16)
        dbsf_acc = df[0:128, :]
        for b in range(1, nb):
            dbsf_acc = dbsf_acc + df[b * 128:(b + 1) * 128, :]
        dbsf_ref[...] += dbsf_acc
        for hd in range(A_HEADS):
            cols = slice(hd * 128, (hd + 1) * 128)
            vcat = jnp.concatenate([vn_s[b * 128:(b + 1) * 128, cols] for b in range(nb)], axis=1)
            dcat = jnp.concatenate([df_s[b * 128:(b + 1) * 128, cols] for b in range(nb)], axis=1)
            fcat = _dot(wm_ref[hd], vcat)
            gcat = _dot(wmt_ref[hd], dcat)
            dwm_ref[hd] += _dot_nt(dcat, vcat)
            for b in range(nb):
                f_s[b * 128:(b + 1) * 128, cols] = fcat[:, b * 128:(b + 1) * 128]
                dvn_s[b * 128:(b + 1) * 128, cols] = gcat[:, b * 128:(b + 1) * 128]
        bias = jnp.concatenate([bsf_ref[...]] * nb, axis=0)
        dz_ref[:, C_U:C_U + D_A] = (dya * (f_s[...] + bias) * _gelu_grad(u, tu)).astype(BF16)
        dvn = dvn_s[...]
        dlng_ref[...] += _colsum8(dvn * xh)
        dlnb_ref[...] += _colsum8(dvn)
        dxh = dvn * lng_ref[...]
        dvg = rstd * (dxh - jnp.mean(dxh, axis=-1, keepdims=True) - xh * jnp.mean(dxh * xh, axis=-1, keepdims=True))
        dz_ref[:, C_V:C_V + D_A] = (dvg * _gelu_grad(v, tv)).astype(BF16)

        first_tile = i == nt - 1
        halo = jnp.where(first_tile, 0.0, (cg_halo_ref[...].astype(F32) * hb_halo_ref[...].astype(F32))[8:16, :])
        cg = zc(C_CG, D_B)
        hbv = zc(C_HB, D_B)
        bg = zc(C_BG, D_B)
        p = cg * hbv
        p1 = _shift_down(p, halo, 1)
        p2 = _shift_down(p, halo, 2)
        q = wsc_ref[0:1, :] * p2 + wsc_ref[1:2, :] * p1 + wsc_ref[2:3, :] * p
        dz_ref[:, C_BG:C_BG + D_B] = (dyb * q).astype(BF16)
        dq = dyb * bg
        dwsc_ref[0] += _colsum8(dq * p2)
        dwsc_ref[1] += _colsum8(dq * p1)
        dwsc_ref[2] += _colsum8(dq * p)
        cr = carry[...]
        dp = wsc_ref[2:3, :] * dq + wsc_ref[1:2, :] * _shift_up(dq, cr, 1) + wsc_ref[0:1, :] * _shift_up(dq, cr, 2)
        carry[...] = dq[0:8, :]
        dz_ref[:, C_CG:C_CG + D_B] = (dp * hbv).astype(BF16)
        dz_ref[:, C_HB:C_HB + D_B] = (dp * cg).astype(BF16)

        dh = _dot_nt(dz_ref[...], win_v[...])
        xv = x_ref[...]
        r = lax.rsqrt(jnp.mean(xv * xv, axis=-1, keepdims=True) + RMS_EPS)
        xn = xv * r
        dg1_ref[...] += _colsum8(dh * xn)
        dxn = dh * g1_ref[...]
        dx_ref[...] = dx2_ref[...] + r * (dxn - xn * jnp.mean(dxn * xn, axis=-1, keepdims=True))

    outs = [
        jax.ShapeDtypeStruct((t_len, D_MODEL), F32),
        jax.ShapeDtypeStruct((t_len, D_IN), BF16),
        jax.ShapeDtypeStruct((t_len, D_MODEL), BF16),
        jax.ShapeDtypeStruct((t_len, D_MODEL), BF16),
        jax.ShapeDtypeStruct((t_len, D_MODEL), BF16),
        jax.ShapeDtypeStruct((8, D_MODEL), F32),
        jax.ShapeDtypeStruct((8, 2 * D_MODEL), F32),
        jax.ShapeDtypeStruct((8, D_A), F32),
        jax.ShapeDtypeStruct((8, D_A), F32),
        jax.ShapeDtypeStruct((A_HEADS, 128, 128), F32),
        jax.ShapeDtypeStruct((128, D_A), F32),
        jax.ShapeDtypeStruct((3, 8, D_B), F32),
    ]

    def halo_spec(col):
        return pl.BlockSpec((16, D_B), lambda i: (jnp.maximum((nt - 1 - i) * hb - 1, 0), col))

    return pl.pallas_call(
        body, name=f"mixer_bwd_l{layer}", grid=(nt,),
        in_specs=[_row_spec(tm, D_MODEL, nt), _row_spec(tm, D_MODEL, nt), _row_spec(tm, D_IN, nt),
                  halo_spec(C_CG // D_B), halo_spec(C_HB // D_B),
                  _row_spec(tm, D_MODEL, nt), _row_spec(tm, D_MODEL, nt),
                  _const_spec((1, D_MODEL)), _const_spec((1, 2 * D_MODEL)), _const_spec((1, D_A)), _const_spec((1, D_A)),
                  _const_spec((A_HEADS, 128, 128)), _const_spec((A_HEADS, 128, 128)), _const_spec((128, D_A)),
                  _const_spec((8, D_B)), ANY, ANY, ANY],
        out_specs=[_row_spec(tm, D_MODEL, nt), _row_spec(tm, D_IN, nt), _row_spec(tm, D_MODEL, nt),
                   _row_spec(tm, D_MODEL, nt), _row_spec(tm, D_MODEL, nt),
                   _const_spec((8, D_MODEL)), _const_spec((8, 2 * D_MODEL)), _const_spec((8, D_A)), _const_spec((8, D_A)),
                   _const_spec((A_HEADS, 128, 128)), _const_spec((128, D_A)), _const_spec((3, 8, D_B))],
        out_shape=outs,
        scratch_shapes=[pltpu.VMEM((D_MODEL, D_IN), BF16), pltpu.VMEM((2, D_A, D_MODEL), BF16),
                        pltpu.VMEM((D_MODEL, D_MODEL), BF16), pltpu.VMEM((8, D_B), F32),
                        pltpu.VMEM((tm, D_A), BF16), pltpu.VMEM((tm, D_A), F32), pltpu.VMEM((tm, D_A), BF16),
                        pltpu.VMEM((tm, D_A), F32), pltpu.SemaphoreType.DMA((16,))],
        compiler_params=_params(),
    )(dx2, x, z, z, z, av, bv, g1, bgate, lng, lnb, wm, wmt, bsf, wsc, win_g, wb_g, wout_g)


def _wgrad(name, layer, a, b, rows, cols, row_blk, col_blk, row_off, prev):
    t_len, m = a.shape
    n = b.shape[1]
    tk = min(1024, t_len)
    nk = t_len // tk
    col_sharded = n == N_CHIPS * cols
    grid = (m // row_blk, n // col_blk, nk)
    per_shard_c = cols // col_blk
    off_blk = row_off // row_blk

    if col_sharded:
        out_shape = (N_LAYERS, N_CHIPS, rows, cols)
        out_spec = pl.BlockSpec((None, None, row_blk, col_blk),
                                lambda i, j, k: (layer, j // per_shard_c, off_blk + i, j % per_shard_c))
    else:
        out_shape = (N_LAYERS, N_CHIPS * rows, cols)
        out_spec = pl.BlockSpec((None, row_blk, col_blk), lambda i, j, k: (layer, i, j))

    def body(*refs):
        a_ref, b_ref, o_ref = refs[0], refs[1], refs[-1]
        k = pl.program_id(2)

        @pl.when(k == 0)
        def _():
            o_ref[...] = jnp.zeros_like(o_ref)

        o_ref[...] += _dot_tn(a_ref[...], b_ref[...])

    in_specs = [pl.BlockSpec((tk, row_blk), lambda i, j, k: (k, i)), pl.BlockSpec((tk, col_blk), lambda i, j, k: (k, j))]
    args = [a, b]
    aliases = {}
    if prev is not None:
        in_specs.append(ANY)
        args.append(prev)
        aliases = {2: 0}
    return pl.pallas_call(
        body, name=f"wgrad_{name}_l{layer}_r{row_off}", grid=grid,
        in_specs=in_specs,
        out_specs=out_spec,
        out_shape=jax.ShapeDtypeStruct(out_shape, F32),
        input_output_aliases=aliases,
        compiler_params=_params(("parallel", "parallel", "arbitrary")),
    )(*args)


def _mesh_pos():
    return lax.axis_index("x"), lax.axis_index("y"), lax.axis_index("c")


def _other_chips(x, y):
    return [(1 - x, y, 2 * (1 - x) + y), (x, 1 - y, 2 * x + (1 - y)), (1 - x, 1 - y, 2 * (1 - x) + (1 - y))]


def _cast_into_slot(name, w, chip):
    n_l, rows, cols = w.shape
    blk = _flat_blk(rows, cols)

    def body(chip_ref, w_ref, o_ref):
        o_ref[...] = w_ref[...].astype(BF16)

    return pl.pallas_call(
        body, name=f"cast_{name}",
        grid_spec=pltpu.PrefetchScalarGridSpec(
            num_scalar_prefetch=1, grid=(n_l, rows // blk),
            in_specs=[pl.BlockSpec((None, blk, cols), lambda la, i, chip_ref: (la, i, 0))],
            out_specs=pl.BlockSpec((None, None, blk, cols), lambda la, i, chip_ref: (la, chip_ref[0], i, 0))),
        out_shape=jax.ShapeDtypeStruct((n_l, N_CHIPS, rows, cols), BF16),
        compiler_params=_params(("parallel", "parallel")),
    )(chip, w)


def _all_gather_weights(slots):
    nw = len(slots)

    def body(*refs):
        buf = refs[nw:2 * nw]
        ici_send, ici_recv, d2d_send, d2d_recv = refs[2 * nw:]
        x, y, c = _mesh_pos()
        me = 2 * x + y
        sibling = (x, y, 1 - c)
        chips = _other_chips(x, y)

        def remote(ref, ssem, rsem, to):
            return pltpu.make_async_remote_copy(src_ref=ref, dst_ref=ref, send_sem=ssem, recv_sem=rsem,
                                                device_id=to, device_id_type=MESH)

        ici, fwd = [], []
        for w in range(nw):
            for j, (px, py, pk) in enumerate(chips):
                ici.append(remote(buf[w].at[c, me], ici_send.at[w, j], ici_recv.at[w, j], (px, py, c)))
        _start_all(ici)
        for w in range(nw):
            for j, (px, py, pk) in enumerate(chips):
                remote(buf[w].at[c, pk], ici_send.at[w, j], ici_recv.at[w, j], (px, py, c)).wait_recv()
                cp = remote(buf[w].at[c, pk], d2d_send.at[w, j], d2d_recv.at[w, j], sibling)
                cp.start()
                fwd.append(cp)
        for w in range(nw):
            for j, (px, py, pk) in enumerate(chips):
                remote(buf[w].at[1 - c, pk], d2d_send.at[w, j], d2d_recv.at[w, j], sibling).wait_recv()
        for cp in ici + fwd:
            cp.wait_send()

    return pl.pallas_call(
        body, name="all_gather_weights",
        in_specs=[ANY] * nw, out_specs=[ANY] * nw,
        out_shape=[jax.ShapeDtypeStruct(s.shape, s.dtype) for s in slots],
        input_output_aliases={w: w for w in range(nw)},
        scratch_shapes=[pltpu.SemaphoreType.DMA((nw, 3)), pltpu.SemaphoreType.DMA((nw, 3)),
                        pltpu.SemaphoreType.DMA((nw, 3)), pltpu.SemaphoreType.DMA((nw, 3))],
    )(*slots)


def _pair_exchange(grads):
    nw = len(grads)

    def body(*refs):
        src = refs[:nw]
        dst = refs[nw:2 * nw]
        send, recv = refs[2 * nw:]
        x, y, c = _mesh_pos()
        cps = [pltpu.make_async_remote_copy(src_ref=src[w].at[1 - c], dst_ref=dst[w], send_sem=send.at[w],
                                            recv_sem=recv.at[w], device_id=(x, y, 1 - c), device_id_type=MESH)
               for w in range(nw)]
        _start_all(cps)
        _wait_all(cps)

    return pl.pallas_call(
        body, name="grad_pair_exchange",
        in_specs=[ANY] * nw, out_specs=[ANY] * nw,
        out_shape=[jax.ShapeDtypeStruct(g.shape[1:], g.dtype) for g in grads],
        scratch_shapes=[pltpu.SemaphoreType.DMA((nw,)), pltpu.SemaphoreType.DMA((nw,))],
            )(*grads)


def _chip_exchange(sums):
    nw = len(sums)

    def body(*refs):
        src = refs[:nw]
        dst = refs[nw:2 * nw]
        send, recv = refs[2 * nw:]
        x, y, c = _mesh_pos()
        cps = [pltpu.make_async_remote_copy(src_ref=src[w].at[pk], dst_ref=dst[w].at[j], send_sem=send.at[w, j],
                                            recv_sem=recv.at[w, j], device_id=(px, py, c), device_id_type=MESH)
               for w in range(nw) for j, (px, py, pk) in enumerate(_other_chips(x, y))]
        _start_all(cps)
        _wait_all(cps)

    return pl.pallas_call(
        body, name="grad_chip_exchange",
        in_specs=[ANY] * nw, out_specs=[ANY] * nw,
        out_shape=[jax.ShapeDtypeStruct((3,) + s.shape[1:], s.dtype) for s in sums],
        scratch_shapes=[pltpu.SemaphoreType.DMA((nw, 3)), pltpu.SemaphoreType.DMA((nw, 3))],
            )(*sums)


def _pair_gather(finals):
    nw = len(finals)

    def body(*refs):
        buf = refs[nw:2 * nw]
        send, recv = refs[2 * nw:]
        x, y, c = _mesh_pos()
        cps = [pltpu.make_async_remote_copy(src_ref=buf[w].at[c], dst_ref=buf[w].at[c], send_sem=send.at[w],
                                            recv_sem=recv.at[w], device_id=(x, y, 1 - c), device_id_type=MESH)
               for w in range(nw)]
        _start_all(cps)
        _wait_all(cps)

    return pl.pallas_call(
        body, name="grad_pair_gather",
        in_specs=[ANY] * nw, out_specs=[ANY] * nw,
        out_shape=[jax.ShapeDtypeStruct(f.shape, f.dtype) for f in finals],
        input_output_aliases={w: w for w in range(nw)},
        scratch_shapes=[pltpu.SemaphoreType.DMA((nw,)), pltpu.SemaphoreType.DMA((nw,))],
    )(*finals)


def _all_reduce_small(name, packed):
    rows = packed.shape[0]

    def body(src_ref, out_ref, slots, send, recv):
        x, y, c = _mesh_pos()
        me = 4 * x + 2 * y + c
        cps = []
        for d in range(1, 8):
            peer = me ^ d
            cps.append(pltpu.make_async_remote_copy(
                src_ref=src_ref, dst_ref=slots.at[me], send_sem=send.at[d - 1], recv_sem=recv.at[d - 1],
                device_id=(peer // 4, (peer // 2) % 2, peer % 2), device_id_type=MESH))
        _start_all(cps)
        slots[me] = src_ref[...]
        _wait_all(cps)
        acc = slots[0]
        for d in range(1, 8):
            acc = acc + slots[d]
        out_ref[...] = acc

    return pl.pallas_call(
        body, name=f"all_reduce_{name}",
        in_specs=[pl.BlockSpec(memory_space=pltpu.VMEM)], out_specs=pl.BlockSpec(memory_space=pltpu.VMEM),
        out_shape=jax.ShapeDtypeStruct(packed.shape, F32),
        scratch_shapes=[pltpu.VMEM((8, rows, 128), F32), pltpu.SemaphoreType.DMA((7,)), pltpu.SemaphoreType.DMA((7,))],
        compiler_params=pltpu.CompilerParams(vmem_limit_bytes=V7X_VMEM_LIMIT),
    )(packed)


def _flat_blk(rows, cols):
    blk = rows
    while blk * cols * 4 > 2 * 1024 * 1024 and blk % 16 == 0:
        blk //= 2
    return blk


def _pair_sum(name, grad, other, c):
    _, _, rows, cols = grad.shape
    blk = _flat_blk(rows, cols)

    def body(c_ref, g_ref, o_ref, s32_ref, s16_ref):
        s = g_ref[...] + o_ref[...]
        s32_ref[...] = s
        s16_ref[...] = s.astype(BF16)

    spec3 = pl.BlockSpec((None, blk, cols), lambda k, i, c_ref: (k, i, 0))
    return pl.pallas_call(
        body, name=f"pair_sum_{name}",
        grid_spec=pltpu.PrefetchScalarGridSpec(
            num_scalar_prefetch=1, grid=(N_CHIPS, rows // blk),
            in_specs=[pl.BlockSpec((None, None, blk, cols), lambda k, i, c_ref: (c_ref[0], k, i, 0)), spec3],
            out_specs=[spec3, spec3]),
        out_shape=[jax.ShapeDtypeStruct((N_CHIPS, rows, cols), F32), jax.ShapeDtypeStruct((N_CHIPS, rows, cols), BF16)],
        compiler_params=_params(("parallel", "parallel")),
    )(c, grad, other)


def _chip_sum(name, s32, got, pos):
    _, rows, cols = s32.shape
    blk = _flat_blk(rows, cols)

    def body(pos_ref, s_ref, g_ref, o_ref):
        o_ref[...] = ((s_ref[...] + g_ref[0].astype(F32)) + g_ref[1].astype(F32)) + g_ref[2].astype(F32)

    return pl.pallas_call(
        body, name=f"chip_sum_{name}",
        grid_spec=pltpu.PrefetchScalarGridSpec(
            num_scalar_prefetch=1, grid=(rows // blk,),
            in_specs=[pl.BlockSpec((None, blk, cols), lambda i, pos_ref: (pos_ref[0], i, 0)),
                      pl.BlockSpec((3, blk, cols), lambda i, pos_ref: (0, i, 0))],
            out_specs=pl.BlockSpec((None, blk, cols), lambda i, pos_ref: (pos_ref[1], i, 0))),
        out_shape=jax.ShapeDtypeStruct((N_LAYERS, rows, cols), F32),
        compiler_params=_params(("parallel",)),
    )(pos, s32, got)


def _adamw_math(w, g, m, v):
    m2 = ADAM_B1 * m + (1.0 - ADAM_B1) * g
    v2 = ADAM_B2 * v + (1.0 - ADAM_B2) * (g * g)
    m_hat = m2 / (1.0 - ADAM_B1 ** ADAM_STEP)
    v_hat = v2 / (1.0 - ADAM_B2 ** ADAM_STEP)
    delta = -ADAM_LR * (m_hat / (jnp.sqrt(v_hat) + ADAM_EPS) + ADAM_WD * w)
    return delta, m2, v2


def _adamw(name, w, g, m, v):
    rows, cols = w.shape
    blk = _flat_blk(rows, cols)

    def body(w_ref, g_ref, m_ref, v_ref, d_ref, m2_ref, v2_ref):
        d, m2, v2 = _adamw_math(w_ref[...], g_ref[...], m_ref[...], v_ref[...])
        d_ref[...] = d
        m2_ref[...] = m2
        v2_ref[...] = v2

    spec = pl.BlockSpec((blk, cols), lambda i: (i, 0))
    return pl.pallas_call(
        body, name=f"adamw_{name}", grid=(rows // blk,),
        in_specs=[spec] * 4, out_specs=[spec] * 3,
        out_shape=[jax.ShapeDtypeStruct((rows, cols), F32)] * 3,
        compiler_params=_params(("parallel",)),
    )(w, g, m, v)


SMALL = ("norm1_g", "b_gate", "gmlp_ln_g", "gmlp_ln_b", "w_spatial", "b_spatial", "w_shortconv", "norm2_g",
         "w_ffn_conv", "b_ffn_conv", "final_g")
ALL_WEIGHTS = ("norm1_g", "w_in", "b_gate", "gmlp_ln_g", "gmlp_ln_b", "w_spatial", "b_spatial", "w_shortconv",
               "w_branch", "w_out", "norm2_g", "w_ffn_up", "w_ffn_conv", "b_ffn_conv", "w_ffn_down", "final_g")


def _pack(arrays):
    flat = jnp.concatenate([a.reshape(-1) for a in arrays])
    n = flat.shape[0]
    rows = -(-n // 1024) * 8
    return jnp.pad(flat, (0, rows * 128 - n)).reshape(rows, 128)


def _unpack(packed, like):
    flat = packed.reshape(-1)
    out, off = [], 0
    for a in like:
        out.append(flat[off:off + a.size].reshape(a.shape))
        off += a.size
    return out


def _pad8(w):
    return jnp.pad(w, ((0, 5), (0, 0)))


def kernel(x, norm1_g, w_in, b_gate, gmlp_ln_g, gmlp_ln_b, w_spatial, b_spatial, w_shortconv, w_branch, w_out, norm2_g, w_ffn_up, w_ffn_conv, b_ffn_conv, w_ffn_down, final_g, loss_target, m_norm1_g, m_w_in, m_b_gate, m_gmlp_ln_g, m_gmlp_ln_b, m_w_spatial, m_b_spatial, m_w_shortconv, m_w_branch, m_w_out, m_norm2_g, m_w_ffn_up, m_w_ffn_conv, m_b_ffn_conv, m_w_ffn_down, m_final_g, v_norm1_g, v_w_in, v_b_gate, v_gmlp_ln_g, v_gmlp_ln_b, v_w_spatial, v_b_spatial, v_w_shortconv, v_w_branch, v_w_out, v_norm2_g, v_w_ffn_up, v_w_ffn_conv, v_b_ffn_conv, v_w_ffn_down, v_final_g):
    weights = dict(norm1_g=norm1_g, w_in=w_in, b_gate=b_gate, gmlp_ln_g=gmlp_ln_g, gmlp_ln_b=gmlp_ln_b,
                   w_spatial=w_spatial, b_spatial=b_spatial, w_shortconv=w_shortconv, w_branch=w_branch, w_out=w_out,
                   norm2_g=norm2_g, w_ffn_up=w_ffn_up, w_ffn_conv=w_ffn_conv, b_ffn_conv=b_ffn_conv,
                   w_ffn_down=w_ffn_down, final_g=final_g)
    mom = dict(norm1_g=m_norm1_g, w_in=m_w_in, b_gate=m_b_gate, gmlp_ln_g=m_gmlp_ln_g, gmlp_ln_b=m_gmlp_ln_b,
               w_spatial=m_w_spatial, b_spatial=m_b_spatial, w_shortconv=m_w_shortconv, w_branch=m_w_branch,
               w_out=m_w_out, norm2_g=m_norm2_g, w_ffn_up=m_w_ffn_up, w_ffn_conv=m_w_ffn_conv,
               b_ffn_conv=m_b_ffn_conv, w_ffn_down=m_w_ffn_down, final_g=m_final_g)
    vel = dict(norm1_g=v_norm1_g, w_in=v_w_in, b_gate=v_b_gate, gmlp_ln_g=v_gmlp_ln_g, gmlp_ln_b=v_gmlp_ln_b,
               w_spatial=v_w_spatial, b_spatial=v_b_spatial, w_shortconv=v_w_shortconv, w_branch=v_w_branch,
               w_out=v_w_out, norm2_g=v_norm2_g, w_ffn_up=v_w_ffn_up, w_ffn_conv=v_w_ffn_conv,
               b_ffn_conv=v_b_ffn_conv, w_ffn_down=v_w_ffn_down, final_g=v_final_g)

    cx, cy, cc = _mesh_pos()
    chip = 2 * cx + cy
    t_len = x.shape[1]
    xs = x.reshape(t_len, D_MODEL)
    target = loss_target.reshape(t_len, D_MODEL)

    c_arr = cc.astype(jnp.int32).reshape(1)
    chip_arr = chip.astype(jnp.int32).reshape(1)
    slots = [_cast_into_slot(n, weights[n].reshape((N_LAYERS,) + BIG[n]), chip_arr) for n in BIG_NAMES]
    win_g, wb_g, wout_g, wup_g, wdown_g = _all_gather_weights(slots)

    idx = jnp.arange(GMLP_BLOCK) // CHUNK
    mask = idx[None, :] <= idx[:, None]
    wm_all = jnp.where(mask[None, None], w_spatial, 0.0)
    wm_bf = wm_all.astype(BF16)
    wmt_bf = jnp.swapaxes(wm_all, -1, -2).astype(BF16)
    bsf = jnp.repeat(jnp.swapaxes(b_spatial, -1, -2), 128, axis=-1)
    wsc_full = lax.dynamic_update_slice(jnp.zeros((N_LAYERS, 3, D_B), F32), w_shortconv, (0, 0, chip * (D_B // 4)))
    wfc_full = lax.dynamic_update_slice(jnp.zeros((N_LAYERS, 3, D_FF), F32), w_ffn_conv, (0, 0, chip * (D_FF // 4)))
    taps = _all_reduce_small("conv_taps", _pack([wsc_full, wfc_full]))
    wsc_full, wfc_full = _unpack(taps * 0.5, [wsc_full, wfc_full])

    def row(a):
        return a.reshape(1, -1)

    saved = []
    h_in = xs
    for la in range(N_LAYERS):
        z, ya, yb, av, bv, mg, h1, x2 = _mixer_fwd(
            la, h_in, row(norm1_g[la]), row(b_gate[la]), row(gmlp_ln_g[la]), row(gmlp_ln_b[la]), wm_bf[la], bsf[la],
            _pad8(wsc_full[la]), win_g, wb_g, wout_g)
        up, act, h2, x3 = _ffn_fwd(la, x2, row(norm2_g[la]), _pad8(wfc_full[la]), row(b_ffn_conv[la]), wup_g, wdown_g)
        saved.append(dict(x=h_in, z=z, ya=ya, yb=yb, av=av, bv=bv, mg=mg, h1=h1, x2=x2, up=up, act=act, h2=h2))
        h_in = x3

    dx, dgf8, loss8 = _loss_head(h_in, target, row(final_g))
    small = {n: [None] * N_LAYERS for n in SMALL}
    big = {n: None for n in BIG_NAMES}
    for la in reversed(range(N_LAYERS)):
        s = saved[la]
        dx3 = dx
        dx2, dup, dx3b, dg2, dbfc, dwfc = _ffn_bwd(la, dx3, s["x2"], s["up"], row(norm2_g[la]), _pad8(wfc_full[la]),
                                                   row(b_ffn_conv[la]), wup_g, wdown_g)
        big["w_ffn_down"] = _wgrad("w_ffn_down", la, s["act"], dx3b, 704, 1024, 1408, 1024, 0, big["w_ffn_down"])
        big["w_ffn_up"] = _wgrad("w_ffn_up", la, s["h2"], dup, 1024, 1408, 1024, 1408, 0, big["w_ffn_up"])
        dxl, dz, da, db, dx2b, dg1, dbg, dlng, dlnb, dwm, dbsf, dwsc = _mixer_bwd(
            la, dx2, s["x"], s["z"], s["av"], s["bv"], row(norm1_g[la]), row(b_gate[la]), row(gmlp_ln_g[la]),
            row(gmlp_ln_b[la]), wm_bf[la], wmt_bf[la], bsf[la], _pad8(wsc_full[la]), win_g, wb_g, wout_g)
        big["w_out"] = _wgrad("w_out", la, s["mg"], dx2b, 256, 1024, 1024, 1024, 0, big["w_out"])
        big["w_branch"] = _wgrad("w_branch", la, s["ya"], da, 1024, 256, 512, 256, 0, big["w_branch"])
        big["w_branch"] = _wgrad("w_branch", la, s["yb"], db, 1024, 256, 512, 256, 512, big["w_branch"])
        big["w_in"] = _wgrad("w_in", la, s["h1"], dz, 1024, 1152, 1024, 1152, 0, big["w_in"])
        small["norm1_g"][la] = dg1.sum(0)
        small["b_gate"][la] = dbg.sum(0)
        small["gmlp_ln_g"][la] = dlng.sum(0)
        small["gmlp_ln_b"][la] = dlnb.sum(0)
        small["w_spatial"][la] = jnp.where(mask[None], dwm, 0.0)
        small["b_spatial"][la] = dbsf.reshape(128, A_HEADS, 128).sum(-1).T
        small["w_shortconv"][la] = dwsc.sum(1)
        small["norm2_g"][la] = dg2.sum(0)
        small["w_ffn_conv"][la] = dwfc.sum(1)
        small["b_ffn_conv"][la] = dbfc.sum(0)
        dx = dxl
    grad_x = dx.reshape(x.shape)

    small_local = [jnp.stack(small[n]) for n in SMALL[:-1]] + [dgf8.sum(0), 0.5 * loss8.sum().reshape(1) / D_MODEL]
    reduced = _unpack(_all_reduce_small("small_grads", _pack(small_local)), small_local)
    loss = reduced[-1].reshape(())
    grads = dict(zip(SMALL, reduced[:-1]))
    grads["w_shortconv"] = lax.dynamic_slice(grads["w_shortconv"], (0, 0, chip * (D_B // 4)), (N_LAYERS, 3, D_B // 4))
    grads["w_ffn_conv"] = lax.dynamic_slice(grads["w_ffn_conv"], (0, 0, chip * (D_FF // 4)), (N_LAYERS, 3, D_FF // 4))

    partial = [big[n].reshape((N_LAYERS, N_CHIPS) + BIG[n]) for n in BIG_NAMES]
    from_sibling = _pair_exchange(partial)
    pos_arr = jnp.stack([chip, cc]).astype(jnp.int32)
    sums = [_pair_sum(n, g, o, c_arr) for n, g, o in zip(BIG_NAMES, partial, from_sibling)]
    got = _chip_exchange([s16 for _, s16 in sums])
    finals = [_chip_sum(n, s32, r, pos_arr) for n, (s32, _), r in zip(BIG_NAMES, sums, got)]
    for n, g in zip(BIG_NAMES, _pair_gather(finals)):
        grads[n] = g.reshape(weights[n].shape)

    delta, new_m, new_v = {}, {}, {}
    for n in BIG_NAMES:
        rows, cols = BIG[n]
        shape2 = (N_LAYERS * rows, cols)
        d, m2, v2 = _adamw(n, weights[n].reshape(shape2), grads[n].reshape(shape2), mom[n].reshape(shape2),
                           vel[n].reshape(shape2))
        delta[n], new_m[n], new_v[n] = (a.reshape(weights[n].shape) for a in (d, m2, v2))
    small_w = [weights[n] for n in SMALL]
    packed = [_pack([src[n] for n in SMALL]) for src in (weights, grads, mom, vel)]
    for dst, res in zip((delta, new_m, new_v), _adamw("small", *packed)):
        dst.update(zip(SMALL, _unpack(res, small_w)))

    return (loss, grad_x, *[grads[n] for n in ALL_WEIGHTS], *[delta[n] for n in ALL_WEIGHTS],
            *[new_m[n] for n in ALL_WEIGHTS], *[new_v[n] for n in ALL_WEIGHTS])
```

```python
import jax
import jax.numpy as jnp
from jax import lax
from jax.experimental import pallas as pl
from jax.experimental.pallas import tpu as pltpu

F32 = jnp.float32
BF16 = jnp.bfloat16
MESH = pl.DeviceIdType.MESH
ANY = pl.BlockSpec(memory_space=pl.ANY)

D_MODEL = 1024
D_A = 512
D_B = 512
D_IN = 4608
D_FF = 2816
GMLP_BLOCK = 128
CHUNK = 64
A_HEADS = 4
N_LAYERS = 2
N_CHIPS = 4
N_DEVICES = 8
RMS_EPS = 1e-6
LN_EPS = 1e-5
ADAM_LR = 0.001
ADAM_B1 = 0.9
ADAM_B2 = 0.999
ADAM_EPS = 1e-08
ADAM_WD = 0.01
ADAM_STEP = 10

C_U, C_V, C_BG, C_CG, C_HB, C_GA, C_GB = 0, 512, 1024, 1536, 2048, 2560, 3584

V7X_VMEM_LIMIT = 60 * 1024 * 1024
TM_MIX = 256
TM_FFN = 256
TM_EW = 512
GELU_C0 = 0.7978845608028654
GELU_C1 = 0.044715

BIG = {
    "w_in": (1024, 1152),
    "w_branch": (1024, 256),
    "w_out": (256, 1024),
    "w_ffn_up": (1024, 1408),
    "w_ffn_down": (704, 1024),
}
BIG_NAMES = tuple(BIG)


def _params(sem=("arbitrary",), vmem=V7X_VMEM_LIMIT):
    return pltpu.CompilerParams(dimension_semantics=sem, vmem_limit_bytes=vmem)


def _gelu(x):
    x2 = x * x
    t = jnp.tanh(GELU_C0 * x * (1.0 + GELU_C1 * x2))
    return 0.5 * x * (1.0 + t), t


def _gelu_grad(x, t):
    return 0.5 * (1.0 + t) + 0.5 * x * (1.0 - t * t) * GELU_C0 * (1.0 + 3.0 * GELU_C1 * x * x)


def _colsum8(v):
    r, n = v.shape
    return v.reshape(r // 8, 8, n).sum(axis=0)


def _dot(a, b):
    return jnp.dot(a, b, preferred_element_type=F32)


def _dot_nt(a, b):
    return lax.dot_general(a, b, (((1,), (1,)), ((), ())), preferred_element_type=F32)


def _dot_tn(a, b):
    return lax.dot_general(a, b, (((0,), (0,)), ((), ())), preferred_element_type=F32)


def _shift_down(v, carry, n):
    rows = lax.broadcasted_iota(jnp.int32, v.shape, 0)
    out = pltpu.roll(v, n, 0)
    for r in range(n):
        out = jnp.where(rows == r, carry[8 - n + r:8 - n + r + 1, :], out)
    return out


def _shift_up(v, carry, n):
    tm = v.shape[0]
    rows = lax.broadcasted_iota(jnp.int32, v.shape, 0)
    out = pltpu.roll(v, tm - n, 0)
    for r in range(n):
        out = jnp.where(rows == tm - n + r, carry[r:r + 1, :], out)
    return out


def _start_all(copies):
    for cp in copies:
        cp.start()


def _wait_all(copies):
    for cp in copies:
        cp.wait()


def _load_col_sharded(src, dst, sems, first):
    cs = src.shape[-1]
    return [pltpu.make_async_copy(src.at[k], dst.at[:, k * cs:(k + 1) * cs], sems.at[first + k])
            for k in range(N_CHIPS)]


def _load_row_sharded(src, dst, sems, first):
    rs = src.shape[-2]
    return [pltpu.make_async_copy(src.at[k], dst.at[k * rs:(k + 1) * rs, :], sems.at[first + k])
            for k in range(N_CHIPS)]


def _load_branch(src, dst, sems, first):
    return [pltpu.make_async_copy(src.at[k, pl.ds(m * D_A, D_A), :], dst.at[m, :, k * 256:(k + 1) * 256],
                                  sems.at[first + 2 * k + m])
            for k in range(N_CHIPS) for m in range(2)]


def _row_spec(tm, n, rev=None):
    if rev is None:
        return pl.BlockSpec((tm, n), lambda i: (i, 0))
    return pl.BlockSpec((tm, n), lambda i: (rev - 1 - i, 0))


def _const_spec(shape):
    nd = len(shape)
    return pl.BlockSpec(shape, lambda i: (0,) * nd)


def _mesh_pos():
    return lax.axis_index("x"), lax.axis_index("y"), lax.axis_index("c")


def _other_chips(x, y):
    return [(1 - x, y, 2 * (1 - x) + y), (x, 1 - y, 2 * x + (1 - y)), (1 - x, 1 - y, 2 * (1 - x) + (1 - y))]


def _remote(src, dst, ssem, rsem, to):
    return pltpu.make_async_remote_copy(src_ref=src, dst_ref=dst, send_sem=ssem, recv_sem=rsem, device_id=to,
                                        device_id_type=MESH)


def _half(ref, which, h):
    start = pl.multiple_of(which * h, 8)
    if len(ref.shape) == 2:
        return ref.at[pl.ds(start, h), :]
    return ref.at[:, pl.ds(start, h), :]


class _Stage:
    def __init__(self, ins=(), inouts=(), outs=(), n_sems=0, start=None, mid=None, finish=None, then=None):
        self.ins, self.inouts, self.outs = list(ins), list(inouts), list(outs)
        self.n_sems, self.start, self.mid, self.finish, self.then = n_sems, start, mid, finish, then


def _gather_stage(bufs, then):
    n = len(bufs)

    def copies(io, sem):
        x, y, c = _mesh_pos()
        me = 2 * x + y
        ici, fwd, got = [], [], []
        for w in range(n):
            h = io[w].shape[1] // 2
            for j, (px, py, pk) in enumerate(_other_chips(x, y)):
                mine = _half(io[w].at[me], c, h)
                theirs = _half(io[w].at[pk], c, h)
                ici.append(_remote(mine, mine, sem(12 * w + j), sem(12 * w + 3 + j), (px, py, c)))
                got.append(_remote(theirs, theirs, sem(12 * w + j), sem(12 * w + 3 + j), (px, py, c)))
                fwd.append(_remote(theirs, theirs, sem(12 * w + 6 + j), sem(12 * w + 9 + j), (x, y, 1 - c)))
        return ici, got, fwd

    def start(ins, io, outs, sem):
        _start_all(copies(io, sem)[0])

    def mid(ins, io, outs, sem):
        _, got, fwd = copies(io, sem)
        for g, f in zip(got, fwd):
            g.wait_recv()
            f.start()

    def finish(ins, io, outs, sem):
        x, y, c = _mesh_pos()
        ici, _, fwd = copies(io, sem)
        for w in range(n):
            h = io[w].shape[1] // 2
            for j, (px, py, pk) in enumerate(_other_chips(x, y)):
                other = _half(io[w].at[pk], 1 - c, h)
                _remote(other, other, sem(12 * w + 6 + j), sem(12 * w + 9 + j), (x, y, 1 - c)).wait_recv()
        for cp in ici + fwd:
            cp.wait_send()

    return _Stage(inouts=bufs, n_sems=12 * n, start=start, mid=mid, finish=finish, then=then)


def _pair_send_stage(grad, then):
    h = grad.shape[1] // 2

    def copy(ins, outs, sem):
        x, y, c = _mesh_pos()
        return _remote(_half(ins[0], 1 - c, h), outs[0], sem(0), sem(1), (x, y, 1 - c))

    return _Stage(ins=[grad], outs=[jax.ShapeDtypeStruct((N_CHIPS, h, grad.shape[2]), F32)], n_sems=2,
                  start=lambda ins, io, outs, sem: copy(ins, outs, sem).start(),
                  finish=lambda ins, io, outs, sem: copy(ins, outs, sem).wait(), then=then)


def _chip_send_stage(psum, then):
    def copies(ins, outs, sem):
        x, y, c = _mesh_pos()
        return [_remote(ins[0].at[pk], outs[0].at[j], sem(j), sem(3 + j), (px, py, c))
                for j, (px, py, pk) in enumerate(_other_chips(x, y))]

    return _Stage(ins=[psum], outs=[jax.ShapeDtypeStruct((3,) + psum.shape[1:], BF16)], n_sems=6,
                  start=lambda ins, io, outs, sem: _start_all(copies(ins, outs, sem)),
                  finish=lambda ins, io, outs, sem: _wait_all(copies(ins, outs, sem)), then=then)


def _pair_fill_stage(final, then):
    h = final.shape[0] // 2

    def copy(io, sem):
        x, y, c = _mesh_pos()
        mine = _half(io[0], c, h)
        return _remote(mine, mine, sem(0), sem(1), (x, y, 1 - c))

    return _Stage(inouts=[final], n_sems=2,
                  start=lambda ins, io, outs, sem: copy(io, sem).start(),
                  finish=lambda ins, io, outs, sem: copy(io, sem).wait(), then=then)


def _spread_stage(packed, then):
    def copies(ins, outs, sem):
        x, y, c = _mesh_pos()
        me = 4 * x + 2 * y + c
        cps = []
        for d in range(1, N_DEVICES):
            peer = me ^ d
            cps.append(_remote(ins[0], outs[0].at[me], sem(d), sem(7 + d), (peer // 4, (peer // 2) % 2, peer % 2)))
        return cps, pltpu.make_async_copy(ins[0], outs[0].at[me], sem(0))

    def start(ins, io, outs, sem):
        cps, own = copies(ins, outs, sem)
        own.start()
        _start_all(cps)

    def finish(ins, io, outs, sem):
        cps, own = copies(ins, outs, sem)
        _wait_all(cps)
        own.wait()

    return _Stage(ins=[packed], outs=[jax.ShapeDtypeStruct((N_DEVICES,) + packed.shape, F32)], n_sems=15,
                  start=start, finish=finish, then=then)


def _staged_call(core, *, name, grid, in_specs, out_specs, out_shape, scratch_shapes, args, stages):
    n_in, n_out, n_scr = len(args), len(out_shape), len(scratch_shapes)
    s_args, s_outs, aliases, layout = [], [], {}, []
    n_sems = 0
    for st in stages:
        i0, o0 = len(s_args), len(s_outs)
        s_args += st.ins + st.inouts
        for q in range(len(st.inouts)):
            aliases[n_in + i0 + len(st.ins) + q] = n_out + o0 + q
        s_outs += [jax.ShapeDtypeStruct(a.shape, a.dtype) for a in st.inouts] + st.outs
        layout.append((i0, o0, n_sems))
        n_sems += st.n_sems
    steps = 1
    for g in grid:
        steps *= g

    def body(*refs):
        own_in = refs[:n_in]
        s_in = refs[n_in:n_in + len(s_args)]
        rest = refs[n_in + len(s_args):]
        own_out = rest[:n_out]
        s_out = rest[n_out:n_out + len(s_outs)]
        scr = rest[n_out + len(s_outs):]

        def run(which):
            for st, (i0, o0, s0) in zip(stages, layout):
                fn = getattr(st, which)
                if fn is not None:
                    fn(s_in[i0:i0 + len(st.ins)], s_out[o0:o0 + len(st.inouts)],
                       s_out[o0 + len(st.inouts):o0 + len(st.inouts) + len(st.outs)],
                       lambda k, s0=s0: scr[n_scr].at[s0 + k])

        if not stages:
            core(*own_in, *own_out, *scr[:n_scr])
            return
        step = 0
        for d, g in enumerate(grid):
            step = step * g + pl.program_id(d)
        if steps == 1:
            run("start")
            core(*own_in, *own_out, *scr[:n_scr])
            run("mid")
            run("finish")
            return
        pl.when(step == 0)(lambda: run("start"))
        core(*own_in, *own_out, *scr[:n_scr])
        pl.when(step == (3 * steps) // 4)(lambda: run("mid"))
        pl.when(step == steps - 1)(lambda: run("finish"))

    sem = ("arbitrary",) * len(grid) if stages else ("parallel",) * max(len(grid) - 1, 0) + ("arbitrary",) * min(len(grid), 1)
    res = pl.pallas_call(
        body, name=name, grid=grid,
        in_specs=list(in_specs) + [ANY] * len(s_args),
        out_specs=list(out_specs) + [ANY] * len(s_outs),
        out_shape=list(out_shape) + s_outs,
        input_output_aliases=aliases,
        scratch_shapes=list(scratch_shapes) + ([pltpu.SemaphoreType.DMA((n_sems,))] if stages else []),
        compiler_params=_params(sem) if grid else pltpu.CompilerParams(vmem_limit_bytes=V7X_VMEM_LIMIT),
    )(*args, *s_args)
    return list(res[:n_out]), list(res[n_out:])


class _Pipe:
    def __init__(self):
        self.ready = []
        self.flushes = 0

    def add(self, stage):
        self.ready.append(stage)

    def carry(self, call):
        stages, self.ready = self.ready, []
        own, outs = call(stages)
        k = 0
        for st in stages:
            n = len(st.inouts) + len(st.outs)
            st.then(*outs[k:k + n])
            k += n
        return own

    def flush(self):
        while self.ready:
            self.flushes += 1
            self.carry(lambda stages: _staged_call(
                lambda *refs: None, name=f"comm_tail_{self.flushes}", grid=(), in_specs=[], out_specs=[], out_shape=[],
                scratch_shapes=[], args=[], stages=stages))


def _mixer_fwd(layer, x, g1, bgate, lng, lnb, wm, bsf, wsc, win_g, wb_g, wout_g, stages):
    t_len = x.shape[0]
    tm = min(TM_MIX, t_len)
    nt = t_len // tm
    nb = tm // GMLP_BLOCK

    def core(x_ref, g1_ref, bgate_ref, lng_ref, lnb_ref, wm_ref, bsf_ref, wsc_ref, win_hbm, wb_hbm, wout_hbm,
             z_ref, ya_ref, yb_ref, a_ref, b_ref, mg_ref, h_ref, x2_ref,
             win_v, wb_v, wout_v, carry, vn_s, f_s, sems):
        i = pl.program_id(0)

        @pl.when(i == 0)
        def _():
            cps = (_load_col_sharded(win_hbm, win_v, sems, 0) + _load_branch(wb_hbm, wb_v, sems, 4)
                   + _load_row_sharded(wout_hbm, wout_v, sems, 12))
            _start_all(cps)
            carry[...] = jnp.zeros_like(carry)
            _wait_all(cps)

        xv = x_ref[...]
        r = lax.rsqrt(jnp.mean(xv * xv, axis=-1, keepdims=True) + RMS_EPS)
        h_ref[...] = (xv * r * g1_ref[...]).astype(BF16)

        def zcols(c0, c1):
            zc = _dot(h_ref[...], win_v[:, c0:c1])
            z_ref[:, c0:c1] = zc.astype(BF16)
            return zc

        vg, _ = _gelu(zcols(C_V, C_V + D_A))
        mu = jnp.mean(vg, axis=-1, keepdims=True)
        vc = vg - mu
        rstd = lax.rsqrt(jnp.mean(vc * vc, axis=-1, keepdims=True) + LN_EPS)
        vn_s[...] = (vc * rstd * lng_ref[...] + lnb_ref[...]).astype(BF16)
        for hd in range(A_HEADS):
            cols = slice(hd * 128, (hd + 1) * 128)
            vcat = jnp.concatenate([vn_s[b * 128:(b + 1) * 128, cols] for b in range(nb)], axis=1)
            fcat = _dot(wm_ref[hd], vcat)
            for b in range(nb):
                f_s[b * 128:(b + 1) * 128, cols] = fcat[:, b * 128:(b + 1) * 128]
        ug, _ = _gelu(zcols(C_U, C_U + D_A))
        bias = jnp.concatenate([bsf_ref[...]] * nb, axis=0)
        ya_ref[...] = (ug * (f_s[...] + bias)).astype(BF16)

        p = zcols(C_CG, C_CG + D_B) * zcols(C_HB, C_HB + D_B)
        cr = carry[...]
        q = wsc_ref[0:1, :] * _shift_down(p, cr, 2) + wsc_ref[1:2, :] * _shift_down(p, cr, 1) + wsc_ref[2:3, :] * p
        carry[...] = p[tm - 8:tm, :]
        yb_ref[...] = (zcols(C_BG, C_BG + D_B) * q).astype(BF16)

        av = _dot(ya_ref[...], wb_v[0])
        a_ref[...] = av.astype(BF16)
        mg = jax.nn.sigmoid(zcols(C_GA, C_GA + D_MODEL) + bgate_ref[:, 0:D_MODEL]) * av
        bv = _dot(yb_ref[...], wb_v[1])
        b_ref[...] = bv.astype(BF16)
        mg = mg + jax.nn.sigmoid(zcols(C_GB, C_GB + D_MODEL) + bgate_ref[:, D_MODEL:2 * D_MODEL]) * bv
        mg_ref[...] = mg.astype(BF16)
        x2_ref[...] = x_ref[...] + _dot(mg_ref[...], wout_v[...])

    outs = [
        jax.ShapeDtypeStruct((t_len, D_IN), BF16),
        jax.ShapeDtypeStruct((t_len, D_A), BF16),
        jax.ShapeDtypeStruct((t_len, D_B), BF16),
        jax.ShapeDtypeStruct((t_len, D_MODEL), BF16),
        jax.ShapeDtypeStruct((t_len, D_MODEL), BF16),
        jax.ShapeDtypeStruct((t_len, D_MODEL), BF16),
        jax.ShapeDtypeStruct((t_len, D_MODEL), BF16),
        jax.ShapeDtypeStruct((t_len, D_MODEL), F32),
    ]
    return _staged_call(
        core, name=f"mixer_fwd_l{layer}", grid=(nt,),
        in_specs=[_row_spec(tm, D_MODEL), _const_spec((1, D_MODEL)), _const_spec((1, 2 * D_MODEL)),
                  _const_spec((1, D_A)), _const_spec((1, D_A)), _const_spec((A_HEADS, 128, 128)),
                  _const_spec((128, D_A)), _const_spec((8, D_B)), ANY, ANY, ANY],
        out_specs=[_row_spec(tm, o.shape[1]) for o in outs],
        out_shape=outs,
        scratch_shapes=[pltpu.VMEM((D_MODEL, D_IN), BF16), pltpu.VMEM((2, D_A, D_MODEL), BF16),
                        pltpu.VMEM((D_MODEL, D_MODEL), BF16), pltpu.VMEM((8, D_B), F32),
                        pltpu.VMEM((tm, D_A), BF16), pltpu.VMEM((tm, D_A), F32), pltpu.SemaphoreType.DMA((16,))],
        args=[x, g1, bgate, lng, lnb, wm, bsf, wsc, win_g, wb_g, wout_g], stages=stages)


def _ffn_fwd(layer, x2, g2, wfc, bfc, wup_g, wdown_g, stages):
    t_len = x2.shape[0]
    tm = min(TM_FFN, t_len)
    nt = t_len // tm

    def core(x_ref, g2_ref, wfc_ref, bfc_ref, wup_hbm, wdown_hbm, up_ref, act_ref, h_ref, x3_ref,
             wup_v, wdown_v, carry, sems):
        i = pl.program_id(0)

        @pl.when(i == 0)
        def _():
            cps = _load_col_sharded(wup_hbm, wup_v, sems, 0) + _load_row_sharded(wdown_hbm, wdown_v, sems, 4)
            _start_all(cps)
            carry[...] = jnp.zeros_like(carry)
            _wait_all(cps)

        xv = x_ref[...]
        r = lax.rsqrt(jnp.mean(xv * xv, axis=-1, keepdims=True) + RMS_EPS)
        h_ref[...] = (xv * r * g2_ref[...]).astype(BF16)
        gate = _dot(h_ref[...], wup_v[:, 0:D_FF])
        up_ref[:, 0:D_FF] = gate.astype(BF16)
        cr = carry[...]
        gc = (wfc_ref[0:1, :] * _shift_down(gate, cr, 2) + wfc_ref[1:2, :] * _shift_down(gate, cr, 1)
              + wfc_ref[2:3, :] * gate + bfc_ref[...])
        carry[...] = gate[tm - 8:tm, :]
        val = _dot(h_ref[...], wup_v[:, D_FF:2 * D_FF])
        up_ref[:, D_FF:2 * D_FF] = val.astype(BF16)
        act_ref[...] = (gc * jax.nn.sigmoid(gc) * val).astype(BF16)
        x3_ref[...] = x_ref[...] + _dot(act_ref[...], wdown_v[...])

    outs = [
        jax.ShapeDtypeStruct((t_len, 2 * D_FF), BF16),
        jax.ShapeDtypeStruct((t_len, D_FF), BF16),
        jax.ShapeDtypeStruct((t_len, D_MODEL), BF16),
        jax.ShapeDtypeStruct((t_len, D_MODEL), F32),
    ]
    return _staged_call(
        core, name=f"ffn_fwd_l{layer}", grid=(nt,),
        in_specs=[_row_spec(tm, D_MODEL), _const_spec((1, D_MODEL)), _const_spec((8, D_FF)), _const_spec((1, D_FF)), ANY, ANY],
        out_specs=[_row_spec(tm, o.shape[1]) for o in outs],
        out_shape=outs,
        scratch_shapes=[pltpu.VMEM((D_MODEL, 2 * D_FF), BF16), pltpu.VMEM((D_FF, D_MODEL), BF16),
                        pltpu.VMEM((8, D_FF), F32), pltpu.SemaphoreType.DMA((8,))],
        args=[x2, g2, wfc, bfc, wup_g, wdown_g], stages=stages)


def _loss_head(x3, target, gf):
    t_len = x3.shape[0]
    tm = min(TM_EW, t_len)
    nt = t_len // tm

    def body(x_ref, t_ref, gf_ref, dx_ref, dgf_ref, loss_ref):
        i = pl.program_id(0)

        @pl.when(i == 0)
        def _():
            dgf_ref[...] = jnp.zeros_like(dgf_ref)
            loss_ref[...] = jnp.zeros_like(loss_ref)

        xv = x_ref[...]
        r = lax.rsqrt(jnp.mean(xv * xv, axis=-1, keepdims=True) + RMS_EPS)
        xh = xv * r
        err = xh * gf_ref[...] - t_ref[...]
        loss_ref[...] += _colsum8(err * err)
        dy = err * (1.0 / D_MODEL)
        dgf_ref[...] += _colsum8(dy * xh)
        dxh = dy * gf_ref[...]
        dx_ref[...] = r * (dxh - xh * jnp.mean(dxh * xh, axis=-1, keepdims=True))

    return pl.pallas_call(
        body, name="loss_head", grid=(nt,),
        in_specs=[_row_spec(tm, D_MODEL), _row_spec(tm, D_MODEL), _const_spec((1, D_MODEL))],
        out_specs=[_row_spec(tm, D_MODEL), _const_spec((8, D_MODEL)), _const_spec((8, D_MODEL))],
        out_shape=[jax.ShapeDtypeStruct((t_len, D_MODEL), F32), jax.ShapeDtypeStruct((8, D_MODEL), F32),
                   jax.ShapeDtypeStruct((8, D_MODEL), F32)],
        compiler_params=_params(),
    )(x3, target, gf)


def _ffn_bwd(layer, dx3, x2, up, g2, wfc, bfc, wup_g, wdown_g, stages):
    t_len = x2.shape[0]
    tm = min(TM_FFN, t_len)
    nt = t_len // tm
    hb = tm // 16

    def core(dx3_ref, x_ref, up_ref, halo_ref, g2_ref, wfc_ref, bfc_ref, wup_hbm, wdown_hbm,
             dx2_ref, dup_ref, dx3b_ref, dg2_ref, dbfc_ref, dwfc_ref,
             wup_v, wdown_v, carry, sems):
        i = pl.program_id(0)

        @pl.when(i == 0)
        def _():
            cps = _load_col_sharded(wup_hbm, wup_v, sems, 0) + _load_row_sharded(wdown_hbm, wdown_v, sems, 4)
            _start_all(cps)
            carry[...] = jnp.zeros_like(carry)
            dg2_ref[...] = jnp.zeros_like(dg2_ref)
            dbfc_ref[...] = jnp.zeros_like(dbfc_ref)
            dwfc_ref[...] = jnp.zeros_like(dwfc_ref)
            _wait_all(cps)

        dx3b_ref[...] = dx3_ref[...].astype(BF16)
        da = _dot_nt(dx3b_ref[...], wdown_v[...])
        gate = up_ref[:, 0:D_FF].astype(F32)
        first_tile = i == nt - 1
        halo = jnp.where(first_tile, 0.0, halo_ref[...].astype(F32)[8:16, :])
        g1s = _shift_down(gate, halo, 1)
        g2s = _shift_down(gate, halo, 2)
        gc = wfc_ref[0:1, :] * g2s + wfc_ref[1:2, :] * g1s + wfc_ref[2:3, :] * gate + bfc_ref[...]
        sg = jax.nn.sigmoid(gc)
        val = up_ref[:, D_FF:2 * D_FF].astype(F32)
        dup_ref[:, D_FF:2 * D_FF] = (da * gc * sg).astype(BF16)
        dgc = da * val * sg * (1.0 + gc * (1.0 - sg))
        dbfc_ref[...] += _colsum8(dgc)
        dwfc_ref[0] += _colsum8(dgc * g2s)
        dwfc_ref[1] += _colsum8(dgc * g1s)
        dwfc_ref[2] += _colsum8(dgc * gate)
        cr = carry[...]
        dgate = wfc_ref[2:3, :] * dgc + wfc_ref[1:2, :] * _shift_up(dgc, cr, 1) + wfc_ref[0:1, :] * _shift_up(dgc, cr, 2)
        carry[...] = dgc[0:8, :]
        dup_ref[:, 0:D_FF] = dgate.astype(BF16)
        dh = _dot_nt(dup_ref[...], wup_v[...])
        xv = x_ref[...]
        r = lax.rsqrt(jnp.mean(xv * xv, axis=-1, keepdims=True) + RMS_EPS)
        xh = xv * r
        dg2_ref[...] += _colsum8(dh * xh)
        dxh = dh * g2_ref[...]
        dx2_ref[...] = dx3_ref[...] + r * (dxh - xh * jnp.mean(dxh * xh, axis=-1, keepdims=True))

    outs = [
        jax.ShapeDtypeStruct((t_len, D_MODEL), F32),
        jax.ShapeDtypeStruct((t_len, 2 * D_FF), BF16),
        jax.ShapeDtypeStruct((t_len, D_MODEL), BF16),
        jax.ShapeDtypeStruct((8, D_MODEL), F32),
        jax.ShapeDtypeStruct((8, D_FF), F32),
        jax.ShapeDtypeStruct((3, 8, D_FF), F32),
    ]
    halo_spec = pl.BlockSpec((16, D_FF), lambda i: (jnp.maximum((nt - 1 - i) * hb - 1, 0), 0))
    return _staged_call(
        core, name=f"ffn_bwd_l{layer}", grid=(nt,),
        in_specs=[_row_spec(tm, D_MODEL, nt), _row_spec(tm, D_MODEL, nt), _row_spec(tm, 2 * D_FF, nt), halo_spec,
                  _const_spec((1, D_MODEL)), _const_spec((8, D_FF)), _const_spec((1, D_FF)), ANY, ANY],
        out_specs=[_row_spec(tm, D_MODEL, nt), _row_spec(tm, 2 * D_FF, nt), _row_spec(tm, D_MODEL, nt),
                   _const_spec((8, D_MODEL)), _const_spec((8, D_FF)), _const_spec((3, 8, D_FF))],
        out_shape=outs,
        scratch_shapes=[pltpu.VMEM((D_MODEL, 2 * D_FF), BF16), pltpu.VMEM((D_FF, D_MODEL), BF16),
                        pltpu.VMEM((8, D_FF), F32), pltpu.SemaphoreType.DMA((8,))],
        args=[dx3, x2, up, up, g2, wfc, bfc, wup_g, wdown_g], stages=stages)


def _mixer_bwd(layer, dx2, x, z, av, bv, g1, bgate, lng, lnb, wm, wmt, bsf, wsc, win_g, wb_g, wout_g, stages):
    t_len = x.shape[0]
    tm = min(TM_MIX, t_len)
    nt = t_len // tm
    nb = tm // GMLP_BLOCK
    hb = tm // 16

    def core(dx2_ref, x_ref, z_ref, cg_halo_ref, hb_halo_ref, a_ref, b_ref, g1_ref, bgate_ref, lng_ref, lnb_ref,
             wm_ref, wmt_ref, bsf_ref, wsc_ref, win_hbm, wb_hbm, wout_hbm,
             dx_ref, dz_ref, da_ref, db_ref, dx2b_ref, dg1_ref, dbgate_ref, dlng_ref, dlnb_ref, dwm_ref, dbsf_ref, dwsc_ref,
             win_v, wb_v, wout_v, carry, vn_s, f_s, df_s, dvn_s, sems):
        i = pl.program_id(0)

        @pl.when(i == 0)
        def _():
            cps = (_load_col_sharded(win_hbm, win_v, sems, 0) + _load_branch(wb_hbm, wb_v, sems, 4)
                   + _load_row_sharded(wout_hbm, wout_v, sems, 12))
            _start_all(cps)
            carry[...] = jnp.zeros_like(carry)
            for ref in (dg1_ref, dbgate_ref, dlng_ref, dlnb_ref, dwm_ref, dbsf_ref, dwsc_ref):
                ref[...] = jnp.zeros_like(ref)
            _wait_all(cps)

        def zc(c0, n):
            return z_ref[:, c0:c0 + n].astype(F32)

        dx2b_ref[...] = dx2_ref[...].astype(BF16)
        dm = _dot_nt(dx2b_ref[...], wout_v[...])
        sa = jax.nn.sigmoid(zc(C_GA, D_MODEL) + bgate_ref[:, 0:D_MODEL])
        da_ref[...] = (dm * sa).astype(BF16)
        dga = dm * a_ref[...].astype(F32) * sa * (1.0 - sa)
        dz_ref[:, C_GA:C_GA + D_MODEL] = dga.astype(BF16)
        dbgate_ref[:, 0:D_MODEL] += _colsum8(dga)
        sb = jax.nn.sigmoid(zc(C_GB, D_MODEL) + bgate_ref[:, D_MODEL:2 * D_MODEL])
        db_ref[...] = (dm * sb).astype(BF16)
        dgb = dm * b_ref[...].astype(F32) * sb * (1.0 - sb)
        dz_ref[:, C_GB:C_GB + D_MODEL] = dgb.astype(BF16)
        dbgate_ref[:, D_MODEL:2 * D_MODEL] += _colsum8(dgb)
        dya = _dot_nt(da_ref[...], wb_v[0])
        dyb = _dot_nt(db_ref[...], wb_v[1])

        v = zc(C_V, D_A)
        vg, tv = _gelu(v)
        mu = jnp.mean(vg, axis=-1, keepdims=True)
        vc = vg - mu
        rstd = lax.rsqrt(jnp.mean(vc * vc, axis=-1, keepdims=True) + LN_EPS)
        xh = vc * rstd
        vn_s[...] = (xh * lng_ref[...] + lnb_ref[...]).astype(BF16)
        u = zc(C_U, D_A)
        ug, tu = _gelu(u)
        df = dya * ug
        df_s[...] = df.astype(BF16)
        dbsf_acc = df[0:128, :]
        for b in range(1, nb):
            dbsf_acc = dbsf_acc + df[b * 128:(b + 1) * 128, :]
        dbsf_ref[...] += dbsf_acc
        for hd in range(A_HEADS):
            cols = slice(hd * 128, (hd + 1) * 128)
            vcat = jnp.concatenate([vn_s[b * 128:(b + 1) * 128, cols] for b in range(nb)], axis=1)
            dcat = jnp.concatenate([df_s[b * 128:(b + 1) * 128, cols] for b in range(nb)], axis=1)
            fcat = _dot(wm_ref[hd], vcat)
            gcat = _dot(wmt_ref[hd], dcat)
            dwm_ref[hd] += _dot_nt(dcat, vcat)
            for b in range(nb):
                f_s[b * 128:(b + 1) * 128, cols] = fcat[:, b * 128:(b + 1) * 128]
                dvn_s[b * 128:(b + 1) * 128, cols] = gcat[:, b * 128:(b + 1) * 128]
        bias = jnp.concatenate([bsf_ref[...]] * nb, axis=0)
        dz_ref[:, C_U:C_U + D_A] = (dya * (f_s[...] + bias) * _gelu_grad(u, tu)).astype(BF16)
        dvn = dvn_s[...]
        dlng_ref[...] += _colsum8(dvn * xh)
        dlnb_ref[...] += _colsum8(dvn)
        dxh = dvn * lng_ref[...]
        dvg = rstd * (dxh - jnp.mean(dxh, axis=-1, keepdims=True) - xh * jnp.mean(dxh * xh, axis=-1, keepdims=True))
        dz_ref[:, C_V:C_V + D_A] = (dvg * _gelu_grad(v, tv)).astype(BF16)

        first_tile = i == nt - 1
        halo = jnp.where(first_tile, 0.0, (cg_halo_ref[...].astype(F32) * hb_halo_ref[...].astype(F32))[8:16, :])
        cg = zc(C_CG, D_B)
        hbv = zc(C_HB, D_B)
        bg = zc(C_BG, D_B)
        p = cg * hbv
        p1 = _shift_down(p, halo, 1)
        p2 = _shift_down(p, halo, 2)
        q = wsc_ref[0:1, :] * p2 + wsc_ref[1:2, :] * p1 + wsc_ref[2:3, :] * p
        dz_ref[:, C_BG:C_BG + D_B] = (dyb * q).astype(BF16)
        dq = dyb * bg
        dwsc_ref[0] += _colsum8(dq * p2)
        dwsc_ref[1] += _colsum8(dq * p1)
        dwsc_ref[2] += _colsum8(dq * p)
        cr = carry[...]
        dp = wsc_ref[2:3, :] * dq + wsc_ref[1:2, :] * _shift_up(dq, cr, 1) + wsc_ref[0:1, :] * _shift_up(dq, cr, 2)
        carry[...] = dq[0:8, :]
        dz_ref[:, C_CG:C_CG + D_B] = (dp * hbv).astype(BF16)
        dz_ref[:, C_HB:C_HB + D_B] = (dp * cg).astype(BF16)

        dh = _dot_nt(dz_ref[...], win_v[...])
        xv = x_ref[...]
        r = lax.rsqrt(jnp.mean(xv * xv, axis=-1, keepdims=True) + RMS_EPS)
        xn = xv * r
        dg1_ref[...] += _colsum8(dh * xn)
        dxn = dh * g1_ref[...]
        dx_ref[...] = dx2_ref[...] + r * (dxn - xn * jnp.mean(dxn * xn, axis=-1, keepdims=True))

    outs = [
        jax.ShapeDtypeStruct((t_len, D_MODEL), F32),
        jax.ShapeDtypeStruct((t_len, D_IN), BF16),
        jax.ShapeDtypeStruct((t_len, D_MODEL), BF16),
        jax.ShapeDtypeStruct((t_len, D_MODEL), BF16),
        jax.ShapeDtypeStruct((t_len, D_MODEL), BF16),
        jax.ShapeDtypeStruct((8, D_MODEL), F32),
        jax.ShapeDtypeStruct((8, 2 * D_MODEL), F32),
        jax.ShapeDtypeStruct((8, D_A), F32),
        jax.ShapeDtypeStruct((8, D_A), F32),
        jax.ShapeDtypeStruct((A_HEADS, 128, 128), F32),
        jax.ShapeDtypeStruct((128, D_A), F32),
        jax.ShapeDtypeStruct((3, 8, D_B), F32),
    ]

    def halo_spec(col):
        return pl.BlockSpec((16, D_B), lambda i: (jnp.maximum((nt - 1 - i) * hb - 1, 0), col))

    return _staged_call(
        core, name=f"mixer_bwd_l{layer}", grid=(nt,),
        in_specs=[_row_spec(tm, D_MODEL, nt), _row_spec(tm, D_MODEL, nt), _row_spec(tm, D_IN, nt),
                  halo_spec(C_CG // D_B), halo_spec(C_HB // D_B),
                  _row_spec(tm, D_MODEL, nt), _row_spec(tm, D_MODEL, nt),
                  _const_spec((1, D_MODEL)), _const_spec((1, 2 * D_MODEL)), _const_spec((1, D_A)), _const_spec((1, D_A)),
                  _const_spec((A_HEADS, 128, 128)), _const_spec((A_HEADS, 128, 128)), _const_spec((128, D_A)),
                  _const_spec((8, D_B)), ANY, ANY, ANY],
        out_specs=[_row_spec(tm, D_MODEL, nt), _row_spec(tm, D_IN, nt), _row_spec(tm, D_MODEL, nt),
                   _row_spec(tm, D_MODEL, nt), _row_spec(tm, D_MODEL, nt),
                   _const_spec((8, D_MODEL)), _const_spec((8, 2 * D_MODEL)), _const_spec((8, D_A)), _const_spec((8, D_A)),
                   _const_spec((A_HEADS, 128, 128)), _const_spec((128, D_A)), _const_spec((3, 8, D_B))],
        out_shape=outs,
        scratch_shapes=[pltpu.VMEM((D_MODEL, D_IN), BF16), pltpu.VMEM((2, D_A, D_MODEL), BF16),
                        pltpu.VMEM((D_MODEL, D_MODEL), BF16), pltpu.VMEM((8, D_B), F32),
                        pltpu.VMEM((tm, D_A), BF16), pltpu.VMEM((tm, D_A), F32), pltpu.VMEM((tm, D_A), BF16),
                        pltpu.VMEM((tm, D_A), F32), pltpu.SemaphoreType.DMA((16,))],
        args=[dx2, x, z, z, z, av, bv, g1, bgate, lng, lnb, wm, wmt, bsf, wsc, win_g, wb_g, wout_g], stages=stages)


def _wgrad(name, layer, a, b, rows, cols, row_blk, col_blk, stages):
    t_len, m = a.shape
    n = b.shape[1]
    tk = min(1024, t_len)
    col_sharded = n == N_CHIPS * cols
    grid = (m // row_blk, n // col_blk, t_len // tk)
    per_shard_c = cols // col_blk

    if col_sharded:
        out_shape = (N_CHIPS, rows, cols)
        out_spec = pl.BlockSpec((None, row_blk, col_blk), lambda i, j, k: (j // per_shard_c, i, j % per_shard_c))
    else:
        out_shape = (N_CHIPS * rows, cols)
        out_spec = pl.BlockSpec((row_blk, col_blk), lambda i, j, k: (i, j))

    def core(a_ref, b_ref, o_ref):
        @pl.when(pl.program_id(2) == 0)
        def _():
            o_ref[...] = jnp.zeros_like(o_ref)

        o_ref[...] += _dot_tn(a_ref[...], b_ref[...])

    own, outs = _staged_call(
        core, name=f"wgrad_{name}_l{layer}", grid=grid,
        in_specs=[pl.BlockSpec((tk, row_blk), lambda i, j, k: (k, i)), pl.BlockSpec((tk, col_blk), lambda i, j, k: (k, j))],
        out_specs=[out_spec], out_shape=[jax.ShapeDtypeStruct(out_shape, F32)], scratch_shapes=[],
        args=[a, b], stages=stages)
    return [own[0].reshape(N_CHIPS, rows, cols)], outs


def _wgrad_branch(layer, ya, da, yb, db, stages):
    t_len = ya.shape[0]
    tk = min(1024, t_len)

    def core(ya_ref, da_ref, yb_ref, db_ref, o_ref):
        @pl.when(pl.program_id(1) == 0)
        def _():
            o_ref[...] = jnp.zeros_like(o_ref)

        o_ref[0:D_A, :] += _dot_tn(ya_ref[...], da_ref[...])
        o_ref[D_A:2 * D_A, :] += _dot_tn(yb_ref[...], db_ref[...])

    a_spec = pl.BlockSpec((tk, D_A), lambda j, k: (k, 0))
    d_spec = pl.BlockSpec((tk, 256), lambda j, k: (k, j))
    return _staged_call(
        core, name=f"wgrad_w_branch_l{layer}", grid=(N_CHIPS, t_len // tk),
        in_specs=[a_spec, d_spec, a_spec, d_spec],
        out_specs=[pl.BlockSpec((None, 2 * D_A, 256), lambda j, k: (j, 0, 0))],
        out_shape=[jax.ShapeDtypeStruct((N_CHIPS, 2 * D_A, 256), F32)], scratch_shapes=[],
        args=[ya, da, yb, db], stages=stages)


def _all_reduce_small(name, packed):
    rows = packed.shape[0]

    def body(src_ref, out_ref, slots, send, recv):
        x, y, c = _mesh_pos()
        me = 4 * x + 2 * y + c
        cps = []
        for d in range(1, N_DEVICES):
            peer = me ^ d
            cps.append(_remote(src_ref, slots.at[me], send.at[d - 1], recv.at[d - 1],
                               (peer // 4, (peer // 2) % 2, peer % 2)))
        _start_all(cps)
        slots[me] = src_ref[...]
        _wait_all(cps)
        acc = slots[0]
        for d in range(1, N_DEVICES):
            acc = acc + slots[d]
        out_ref[...] = acc

    return pl.pallas_call(
        body, name=f"all_reduce_{name}",
        in_specs=[pl.BlockSpec(memory_space=pltpu.VMEM)], out_specs=pl.BlockSpec(memory_space=pltpu.VMEM),
        out_shape=jax.ShapeDtypeStruct(packed.shape, F32),
        scratch_shapes=[pltpu.VMEM((N_DEVICES, rows, 128), F32), pltpu.SemaphoreType.DMA((7,)),
                        pltpu.SemaphoreType.DMA((7,))],
        compiler_params=pltpu.CompilerParams(vmem_limit_bytes=V7X_VMEM_LIMIT),
    )(packed)


def _flat_blk(rows, cols):
    blk = rows
    while blk * cols * 4 > 2 * 1024 * 1024 and blk % 16 == 0:
        blk //= 2
    return blk


def _cast_into_slot(name, layer, w, chip):
    _, rows, cols = w.shape
    blk = _flat_blk(rows, cols)

    def body(chip_ref, w_ref, o_ref):
        o_ref[...] = w_ref[...].astype(BF16)

    return pl.pallas_call(
        body, name=f"cast_{name}_l{layer}",
        grid_spec=pltpu.PrefetchScalarGridSpec(
            num_scalar_prefetch=1, grid=(rows // blk,),
            in_specs=[pl.BlockSpec((None, blk, cols), lambda i, chip_ref: (layer, i, 0))],
            out_specs=pl.BlockSpec((None, blk, cols), lambda i, chip_ref: (chip_ref[0], i, 0))),
        out_shape=jax.ShapeDtypeStruct((N_CHIPS, rows, cols), BF16),
        compiler_params=_params(("parallel",)),
    )(chip, w)


def _pair_sum(name, grad, other, core):
    _, h, cols = other.shape
    blk = _flat_blk(h, cols)
    nblk = h // blk

    def body(core_ref, g_ref, o_ref, s_ref):
        s_ref[...] = (g_ref[...] + o_ref[...]).astype(BF16)

    spec = pl.BlockSpec((None, blk, cols), lambda k, i, core_ref: (k, i, 0))
    return pl.pallas_call(
        body, name=f"pair_sum_{name}",
        grid_spec=pltpu.PrefetchScalarGridSpec(
            num_scalar_prefetch=1, grid=(N_CHIPS, nblk),
            in_specs=[pl.BlockSpec((None, blk, cols), lambda k, i, core_ref: (k, core_ref[0] * nblk + i, 0)), spec],
            out_specs=spec),
        out_shape=jax.ShapeDtypeStruct((N_CHIPS, h, cols), BF16),
        compiler_params=_params(("parallel", "parallel")),
    )(core, grad, other)


def _chip_sum(name, grad, other, got, pos):
    _, rows, cols = grad.shape
    h = rows // 2
    blk = _flat_blk(h, cols)
    nblk = h // blk

    def body(pos_ref, g_ref, o_ref, r_ref, f_ref):
        f_ref[...] = (((g_ref[...] + o_ref[...]) + r_ref[0].astype(F32)) + r_ref[1].astype(F32)) + r_ref[2].astype(F32)

    return pl.pallas_call(
        body, name=f"chip_sum_{name}",
        grid_spec=pltpu.PrefetchScalarGridSpec(
            num_scalar_prefetch=1, grid=(nblk,),
            in_specs=[pl.BlockSpec((None, blk, cols), lambda i, pos_ref: (pos_ref[0], pos_ref[1] * nblk + i, 0)),
                      pl.BlockSpec((None, blk, cols), lambda i, pos_ref: (pos_ref[0], i, 0)),
                      pl.BlockSpec((3, blk, cols), lambda i, pos_ref: (0, i, 0))],
            out_specs=pl.BlockSpec((blk, cols), lambda i, pos_ref: (pos_ref[1] * nblk + i, 0))),
        out_shape=jax.ShapeDtypeStruct((rows, cols), F32),
        compiler_params=_params(("parallel",)),
    )(pos, grad, other, got)


def _sum_slots(name, slots):
    _, rows, _ = slots.shape

    def body(s_ref, o_ref):
        acc = s_ref[0]
        for d in range(1, N_DEVICES):
            acc = acc + s_ref[d]
        o_ref[...] = acc

    return pl.pallas_call(
        body, name=f"sum_slots_{name}", grid=(1,),
        in_specs=[pl.BlockSpec((N_DEVICES, rows, 128), lambda i: (0, 0, 0))],
        out_specs=pl.BlockSpec((rows, 128), lambda i: (0, 0)),
        out_shape=jax.ShapeDtypeStruct((rows, 128), F32),
        compiler_params=_params(),
    )(slots)


def _adamw_math(w, g, m, v):
    m2 = ADAM_B1 * m + (1.0 - ADAM_B1) * g
    v2 = ADAM_B2 * v + (1.0 - ADAM_B2) * (g * g)
    m_hat = m2 / (1.0 - ADAM_B1 ** ADAM_STEP)
    v_hat = v2 / (1.0 - ADAM_B2 ** ADAM_STEP)
    delta = -ADAM_LR * (m_hat / (jnp.sqrt(v_hat) + ADAM_EPS) + ADAM_WD * w)
    return delta, m2, v2


def _adamw_big(name, w, g0, g1, m, v):
    _, rows, cols = w.shape
    blk = _flat_blk(rows, cols) // 2

    def body(w_ref, g0_ref, g1_ref, m_ref, v_ref, g_ref, d_ref, m2_ref, v2_ref):
        g = jnp.where(pl.program_id(0) == 0, g0_ref[...], g1_ref[...])
        d, m2, v2 = _adamw_math(w_ref[...], g, m_ref[...], v_ref[...])
        g_ref[...] = g
        d_ref[...] = d
        m2_ref[...] = m2
        v2_ref[...] = v2

    spec = pl.BlockSpec((None, blk, cols), lambda la, i: (la, i, 0))
    return pl.pallas_call(
        body, name=f"adamw_{name}", grid=(N_LAYERS, rows // blk),
        in_specs=[spec, pl.BlockSpec((blk, cols), lambda la, i: (i * (1 - la), 0)),
                  pl.BlockSpec((blk, cols), lambda la, i: (i * la, 0)), spec, spec],
        out_specs=[spec] * 4,
        out_shape=[jax.ShapeDtypeStruct(w.shape, F32)] * 4,
        compiler_params=_params(("parallel", "parallel")),
    )(w, g0, g1, m, v)


def _adamw(name, w, g, m, v):
    rows, cols = w.shape
    blk = _flat_blk(rows, cols)

    def body(w_ref, g_ref, m_ref, v_ref, d_ref, m2_ref, v2_ref):
        d, m2, v2 = _adamw_math(w_ref[...], g_ref[...], m_ref[...], v_ref[...])
        d_ref[...] = d
        m2_ref[...] = m2
        v2_ref[...] = v2

    spec = pl.BlockSpec((blk, cols), lambda i: (i, 0))
    return pl.pallas_call(
        body, name=f"adamw_{name}", grid=(rows // blk,),
        in_specs=[spec] * 4, out_specs=[spec] * 3,
        out_shape=[jax.ShapeDtypeStruct((rows, cols), F32)] * 3,
        compiler_params=_params(("parallel",)),
    )(w, g, m, v)


SMALL = ("norm1_g", "b_gate", "gmlp_ln_g", "gmlp_ln_b", "w_spatial", "b_spatial", "w_shortconv", "norm2_g",
         "w_ffn_conv", "b_ffn_conv", "final_g")
ALL_WEIGHTS = ("norm1_g", "w_in", "b_gate", "gmlp_ln_g", "gmlp_ln_b", "w_spatial", "b_spatial", "w_shortconv",
               "w_branch", "w_out", "norm2_g", "w_ffn_up", "w_ffn_conv", "b_ffn_conv", "w_ffn_down", "final_g")


def _pack(arrays):
    flat = jnp.concatenate([a.reshape(-1) for a in arrays])
    n = flat.shape[0]
    rows = -(-n // 1024) * 8
    return jnp.pad(flat, (0, rows * 128 - n)).reshape(rows, 128)


def _unpack(packed, like):
    flat = packed.reshape(-1)
    out, off = [], 0
    for a in like:
        out.append(flat[off:off + a.size].reshape(a.shape))
        off += a.size
    return out


def _pad8(w):
    return jnp.pad(w, ((0, 5), (0, 0)))


def kernel(x, norm1_g, w_in, b_gate, gmlp_ln_g, gmlp_ln_b, w_spatial, b_spatial, w_shortconv, w_branch, w_out, norm2_g, w_ffn_up, w_ffn_conv, b_ffn_conv, w_ffn_down, final_g, loss_target, m_norm1_g, m_w_in, m_b_gate, m_gmlp_ln_g, m_gmlp_ln_b, m_w_spatial, m_b_spatial, m_w_shortconv, m_w_branch, m_w_out, m_norm2_g, m_w_ffn_up, m_w_ffn_conv, m_b_ffn_conv, m_w_ffn_down, m_final_g, v_norm1_g, v_w_in, v_b_gate, v_gmlp_ln_g, v_gmlp_ln_b, v_w_spatial, v_b_spatial, v_w_shortconv, v_w_branch, v_w_out, v_norm2_g, v_w_ffn_up, v_w_ffn_conv, v_b_ffn_conv, v_w_ffn_down, v_final_g):
    weights = dict(norm1_g=norm1_g, w_in=w_in, b_gate=b_gate, gmlp_ln_g=gmlp_ln_g, gmlp_ln_b=gmlp_ln_b,
                   w_spatial=w_spatial, b_spatial=b_spatial, w_shortconv=w_shortconv, w_branch=w_branch, w_out=w_out,
                   norm2_g=norm2_g, w_ffn_up=w_ffn_up, w_ffn_conv=w_ffn_conv, b_ffn_conv=b_ffn_conv,
                   w_ffn_down=w_ffn_down, final_g=final_g)
    mom = dict(norm1_g=m_norm1_g, w_in=m_w_in, b_gate=m_b_gate, gmlp_ln_g=m_gmlp_ln_g, gmlp_ln_b=m_gmlp_ln_b,
               w_spatial=m_w_spatial, b_spatial=m_b_spatial, w_shortconv=m_w_shortconv, w_branch=m_w_branch,
               w_out=m_w_out, norm2_g=m_norm2_g, w_ffn_up=m_w_ffn_up, w_ffn_conv=m_w_ffn_conv,
               b_ffn_conv=m_b_ffn_conv, w_ffn_down=m_w_ffn_down, final_g=m_final_g)
    vel = dict(norm1_g=v_norm1_g, w_in=v_w_in, b_gate=v_b_gate, gmlp_ln_g=v_gmlp_ln_g, gmlp_ln_b=v_gmlp_ln_b,
               w_spatial=v_w_spatial, b_spatial=v_b_spatial, w_shortconv=v_w_shortconv, w_branch=v_w_branch,
               w_out=v_w_out, norm2_g=v_norm2_g, w_ffn_up=v_w_ffn_up, w_ffn_conv=v_w_ffn_conv,
               b_ffn_conv=v_b_ffn_conv, w_ffn_down=v_w_ffn_down, final_g=v_final_g)

    cx, cy, cc = _mesh_pos()
    chip = 2 * cx + cy
    core_arr = cc.astype(jnp.int32).reshape(1)
    chip_arr = chip.astype(jnp.int32).reshape(1)
    pos_arr = jnp.stack([chip, cc]).astype(jnp.int32)
    t_len = x.shape[1]
    xs = x.reshape(t_len, D_MODEL)
    target = loss_target.reshape(t_len, D_MODEL)
    pipe = _Pipe()

    full = {}

    def gather(keys):
        slots = [_cast_into_slot(n, la, weights[n].reshape((N_LAYERS,) + BIG[n]), chip_arr) for n, la in keys]

        def then(*bufs):
            full.update(zip(keys, bufs))

        pipe.add(_gather_stage(slots, then))

    mixer_w = ("w_in", "w_branch", "w_out")
    ffn_w = ("w_ffn_up", "w_ffn_down")
    gather([(n, 0) for n in mixer_w])
    pipe.flush()

    idx = jnp.arange(GMLP_BLOCK) // CHUNK
    mask = idx[None, :] <= idx[:, None]
    wm_all = jnp.where(mask[None, None], w_spatial, 0.0)
    wm_bf = wm_all.astype(BF16)
    wmt_bf = jnp.swapaxes(wm_all, -1, -2).astype(BF16)
    bsf = jnp.repeat(jnp.swapaxes(b_spatial, -1, -2), 128, axis=-1)
    wsc_full = lax.dynamic_update_slice(jnp.zeros((N_LAYERS, 3, D_B), F32), w_shortconv, (0, 0, chip * (D_B // 4)))
    wfc_full = lax.dynamic_update_slice(jnp.zeros((N_LAYERS, 3, D_FF), F32), w_ffn_conv, (0, 0, chip * (D_FF // 4)))
    taps = _all_reduce_small("conv_taps", _pack([wsc_full, wfc_full]))
    wsc_full, wfc_full = _unpack(taps * 0.5, [wsc_full, wfc_full])

    def row(a):
        return a.reshape(1, -1)

    def mixer_args(la):
        return (row(norm1_g[la]), row(b_gate[la]), row(gmlp_ln_g[la]), row(gmlp_ln_b[la]))

    def mixer_weights(la):
        return tuple(full[(n, la)] for n in mixer_w)

    def ffn_weights(la):
        return tuple(full[(n, la)] for n in ffn_w)

    saved = []
    h_in = xs
    for la in range(N_LAYERS):
        gather([(n, la) for n in ffn_w])
        z, ya, yb, av, bv, mg, h1, x2 = pipe.carry(lambda st: _mixer_fwd(
            la, h_in, *mixer_args(la), wm_bf[la], bsf[la], _pad8(wsc_full[la]), *mixer_weights(la), st))
        if la + 1 < N_LAYERS:
            gather([(n, la + 1) for n in mixer_w])
        up, act, h2, x3 = pipe.carry(lambda st: _ffn_fwd(
            la, x2, row(norm2_g[la]), _pad8(wfc_full[la]), row(b_ffn_conv[la]), *ffn_weights(la), st))
        saved.append(dict(x=h_in, z=z, ya=ya, yb=yb, av=av, bv=bv, mg=mg, h1=h1, x2=x2, up=up, act=act, h2=h2))
        h_in = x3

    reduced_big = {}

    def reduce_big(name, la, grad):
        tag = f"{name}_l{la}"

        def after_pair(other):
            psum = _pair_sum(tag, grad, other, core_arr)

            def after_chips(got):
                final = _chip_sum(tag, grad, other, got, pos_arr)
                pipe.add(_pair_fill_stage(final, lambda done: reduced_big.__setitem__((name, la), done)))

            pipe.add(_chip_send_stage(psum, after_chips))

        pipe.add(_pair_send_stage(grad, after_pair))

    dx, dgf8, loss8 = _loss_head(h_in, target, row(final_g))
    small = {n: [None] * N_LAYERS for n in SMALL}
    deferred = []
    for la in reversed(range(N_LAYERS)):
        s = saved[la]
        dx3 = dx
        dx2, dup, dx3b, dg2, dbfc, dwfc = pipe.carry(lambda st: _ffn_bwd(
            la, dx3, s["x2"], s["up"], row(norm2_g[la]), _pad8(wfc_full[la]), row(b_ffn_conv[la]), *ffn_weights(la), st))
        g, = pipe.carry(lambda st: _wgrad("w_ffn_down", la, s["act"], dx3b, 704, 1024, 1408, 1024, st))
        reduce_big("w_ffn_down", la, g)
        g, = pipe.carry(lambda st: _wgrad("w_ffn_up", la, s["h2"], dup, 1024, 1408, 1024, 1408, st))
        reduce_big("w_ffn_up", la, g)
        dxl, dz, da, db, dx2b, dg1, dbg, dlng, dlnb, dwm, dbsf, dwsc = pipe.carry(lambda st: _mixer_bwd(
            la, dx2, s["x"], s["z"], s["av"], s["bv"], *mixer_args(la), wm_bf[la], wmt_bf[la], bsf[la],
            _pad8(wsc_full[la]), *mixer_weights(la), st))
        g, = pipe.carry(lambda st: _wgrad("w_in", la, s["h1"], dz, 1024, 1152, 1024, 1152, st))
        reduce_big("w_in", la, g)
        deferred.append((la, s["mg"], dx2b, s["ya"], da, s["yb"], db))
        small["norm1_g"][la] = dg1.sum(0)
        small["b_gate"][la] = dbg.sum(0)
        small["gmlp_ln_g"][la] = dlng.sum(0)
        small["gmlp_ln_b"][la] = dlnb.sum(0)
        small["w_spatial"][la] = jnp.where(mask[None], dwm, 0.0)
        small["b_spatial"][la] = dbsf.reshape(128, A_HEADS, 128).sum(-1).T
        small["w_shortconv"][la] = dwsc.sum(1)
        small["norm2_g"][la] = dg2.sum(0)
        small["w_ffn_conv"][la] = dwfc.sum(1)
        small["b_ffn_conv"][la] = dbfc.sum(0)
        dx = dxl
    grad_x = dx.reshape(x.shape)

    small_local = [jnp.stack(small[n]) for n in SMALL[:-1]] + [dgf8.sum(0), 0.5 * loss8.sum().reshape(1) / D_MODEL]
    spread = {}
    pipe.add(_spread_stage(_pack(small_local), lambda slots: spread.__setitem__("slots", slots)))

    for la, mg, dx2b, ya, da, yb, db in deferred:
        g, = pipe.carry(lambda st: _wgrad("w_out", la, mg, dx2b, 256, 1024, 1024, 1024, st))
        reduce_big("w_out", la, g)
    for la, mg, dx2b, ya, da, yb, db in deferred:
        g, = pipe.carry(lambda st: _wgrad_branch(la, ya, da, yb, db, st))
        reduce_big("w_branch", la, g)
    pipe.flush()

    reduced = _unpack(_sum_slots("small_grads", spread["slots"]), small_local)
    loss = reduced[-1].reshape(())
    grads = dict(zip(SMALL, reduced[:-1]))
    grads["w_shortconv"] = lax.dynamic_slice(grads["w_shortconv"], (0, 0, chip * (D_B // 4)), (N_LAYERS, 3, D_B // 4))
    grads["w_ffn_conv"] = lax.dynamic_slice(grads["w_ffn_conv"], (0, 0, chip * (D_FF // 4)), (N_LAYERS, 3, D_FF // 4))

    delta, new_m, new_v = {}, {}, {}
    for n in BIG_NAMES:
        shape3 = (N_LAYERS,) + BIG[n]
        res = _adamw_big(n, weights[n].reshape(shape3), reduced_big[(n, 0)], reduced_big[(n, 1)],
                         mom[n].reshape(shape3), vel[n].reshape(shape3))
        grads[n], delta[n], new_m[n], new_v[n] = (a.reshape(weights[n].shape) for a in res)
    small_w = [weights[n] for n in SMALL]
    packed = [_pack([src[n] for n in SMALL]) for src in (weights, grads, mom, vel)]
    for dst, res in zip((delta, new_m, new_v), _adamw("small", *packed)):
        dst.update(zip(SMALL, _unpack(res, small_w)))

    return (loss, grad_x, *[grads[n] for n in ALL_WEIGHTS], *[delta[n] for n in ALL_WEIGHTS],
            *[new_m[n] for n in ALL_WEIGHTS], *[new_v[n] for n in ALL_WEIGHTS])
```

```python
import jax
import jax.numpy as jnp
from jax import lax
from jax.experimental import pallas as pl
from jax.experimental.pallas import tpu as pltpu

F32 = jnp.float32
BF16 = jnp.bfloat16
MESH = pl.DeviceIdType.MESH
ANY = pl.BlockSpec(memory_space=pl.ANY)

D_MODEL = 1024
D_A = 512
D_B = 512
D_IN = 4608
D_FF = 2816
GMLP_BLOCK = 128
CHUNK = 64
A_HEADS = 4
N_LAYERS = 2
N_CHIPS = 4
N_DEVICES = 8
RMS_EPS = 1e-6
LN_EPS = 1e-5
ADAM_LR = 0.001
ADAM_B1 = 0.9
ADAM_B2 = 0.999
ADAM_EPS = 1e-08
ADAM_WD = 0.01
ADAM_STEP = 10

C_U, C_V, C_BG, C_CG, C_HB, C_GA, C_GB = 0, 512, 1024, 1536, 2048, 2560, 3584

V7X_VMEM_LIMIT = 60 * 1024 * 1024
TM_MIX = 256
TM_FFN = 256
TM_EW = 512
FF_CHUNKS = ((0, 768), (768, 1536), (1536, 2304), (2304, 2816))
GELU_C0 = 0.7978845608028654
GELU_C1 = 0.044715

BIG = {
    "w_in": (1024, 1152),
    "w_branch": (1024, 256),
    "w_out": (256, 1024),
    "w_ffn_up": (1024, 1408),
    "w_ffn_down": (704, 1024),
}
BIG_NAMES = tuple(BIG)


def _params(sem=("arbitrary",), vmem=V7X_VMEM_LIMIT):
    return pltpu.CompilerParams(dimension_semantics=sem, vmem_limit_bytes=vmem)


def _gelu(x):
    x2 = x * x
    t = jnp.tanh(GELU_C0 * x * (1.0 + GELU_C1 * x2))
    return 0.5 * x * (1.0 + t), t


def _gelu_grad(x, t):
    return 0.5 * (1.0 + t) + 0.5 * x * (1.0 - t * t) * GELU_C0 * (1.0 + 3.0 * GELU_C1 * x * x)


def _colsum8(v):
    r, n = v.shape
    return v.reshape(r // 8, 8, n).sum(axis=0)


def _dot(a, b):
    return jnp.dot(a, b, preferred_element_type=F32)


def _dot_nt(a, b):
    return lax.dot_general(a, b, (((1,), (1,)), ((), ())), preferred_element_type=F32)


def _dot_tn(a, b):
    return lax.dot_general(a, b, (((0,), (0,)), ((), ())), preferred_element_type=F32)


def _shift_down(v, carry, n):
    rows = lax.broadcasted_iota(jnp.int32, (8, v.shape[1]), 0)
    out = pltpu.roll(v, n, 0)
    head = out[0:8, :]
    for r in range(n):
        head = jnp.where(rows == r, carry[8 - n + r:8 - n + r + 1, :], head)
    return jnp.concatenate([head, out[8:, :]], axis=0)


def _shift_up(v, carry, n):
    tm = v.shape[0]
    rows = lax.broadcasted_iota(jnp.int32, (8, v.shape[1]), 0)
    out = pltpu.roll(v, tm - n, 0)
    tail = out[tm - 8:tm, :]
    for r in range(n):
        tail = jnp.where(rows == 8 - n + r, carry[r:r + 1, :], tail)
    return jnp.concatenate([out[0:tm - 8, :], tail], axis=0)


def _sigmoid(x):
    return 0.5 * jnp.tanh(0.5 * x) + 0.5


def _start_all(copies):
    for cp in copies:
        cp.start()


def _wait_all(copies):
    for cp in copies:
        cp.wait()


def _load_col_sharded(src, dst, sems, first):
    cs = src.shape[-1]
    return [pltpu.make_async_copy(src.at[k], dst.at[:, k * cs:(k + 1) * cs], sems.at[first + k])
            for k in range(N_CHIPS)]


def _load_row_sharded(src, dst, sems, first):
    rs = src.shape[-2]
    return [pltpu.make_async_copy(src.at[k], dst.at[k * rs:(k + 1) * rs, :], sems.at[first + k])
            for k in range(N_CHIPS)]


def _load_branch(src, dst, sems, first):
    return [pltpu.make_async_copy(src.at[k, pl.ds(m * D_A, D_A), :], dst.at[m, :, k * 256:(k + 1) * 256],
                                  sems.at[first + 2 * k + m])
            for k in range(N_CHIPS) for m in range(2)]


def _row_spec(tm, n, rev=None):
    if rev is None:
        return pl.BlockSpec((tm, n), lambda i: (i, 0))
    return pl.BlockSpec((tm, n), lambda i: (rev - 1 - i, 0))


def _const_spec(shape):
    nd = len(shape)
    return pl.BlockSpec(shape, lambda i: (0,) * nd)


def _mesh_pos():
    return lax.axis_index("x"), lax.axis_index("y"), lax.axis_index("c")


def _other_chips(x, y):
    return [(1 - x, y, 2 * (1 - x) + y), (x, 1 - y, 2 * x + (1 - y)), (1 - x, 1 - y, 2 * (1 - x) + (1 - y))]


def _remote(src, dst, ssem, rsem, to):
    return pltpu.make_async_remote_copy(src_ref=src, dst_ref=dst, send_sem=ssem, recv_sem=rsem, device_id=to,
                                        device_id_type=MESH)


def _half(ref, which, h):
    start = pl.multiple_of(which * h, 8)
    if len(ref.shape) == 2:
        return ref.at[pl.ds(start, h), :]
    return ref.at[:, pl.ds(start, h), :]


class _Stage:
    def __init__(self, ins=(), inouts=(), outs=(), n_sems=0, start=None, mid=None, finish=None, then=None):
        self.ins, self.inouts, self.outs = list(ins), list(inouts), list(outs)
        self.n_sems, self.start, self.mid, self.finish, self.then = n_sems, start, mid, finish, then


def _gather_stage(bufs, then):
    n = len(bufs)

    def copies(io, sem):
        x, y, c = _mesh_pos()
        me = 2 * x + y
        ici, fwd, got = [], [], []
        for w in range(n):
            h = io[w].shape[1] // 2
            for j, (px, py, pk) in enumerate(_other_chips(x, y)):
                mine = _half(io[w].at[me], c, h)
                theirs = _half(io[w].at[pk], c, h)
                ici.append(_remote(mine, mine, sem(12 * w + j), sem(12 * w + 3 + j), (px, py, c)))
                got.append(_remote(theirs, theirs, sem(12 * w + j), sem(12 * w + 3 + j), (px, py, c)))
                fwd.append(_remote(theirs, theirs, sem(12 * w + 6 + j), sem(12 * w + 9 + j), (x, y, 1 - c)))
        return ici, got, fwd

    def start(ins, io, outs, sem):
        _start_all(copies(io, sem)[0])

    def mid(ins, io, outs, sem):
        _, got, fwd = copies(io, sem)
        for g, f in zip(got, fwd):
            g.wait_recv()
            f.start()

    def finish(ins, io, outs, sem):
        x, y, c = _mesh_pos()
        ici, _, fwd = copies(io, sem)
        for w in range(n):
            h = io[w].shape[1] // 2
            for j, (px, py, pk) in enumerate(_other_chips(x, y)):
                other = _half(io[w].at[pk], 1 - c, h)
                _remote(other, other, sem(12 * w + 6 + j), sem(12 * w + 9 + j), (x, y, 1 - c)).wait_recv()
        for cp in ici + fwd:
            cp.wait_send()

    return _Stage(inouts=bufs, n_sems=12 * n, start=start, mid=mid, finish=finish, then=then)


def _pair_send_stage(grad, then):
    h = grad.shape[1] // 2

    def copy(ins, outs, sem):
        x, y, c = _mesh_pos()
        return _remote(_half(ins[0], 1 - c, h), outs[0], sem(0), sem(1), (x, y, 1 - c))

    return _Stage(ins=[grad], outs=[jax.ShapeDtypeStruct((N_CHIPS, h, grad.shape[2]), F32)], n_sems=2,
                  start=lambda ins, io, outs, sem: copy(ins, outs, sem).start(),
                  finish=lambda ins, io, outs, sem: copy(ins, outs, sem).wait(), then=then)


def _chip_send_stage(psum, then):
    def copies(ins, outs, sem):
        x, y, c = _mesh_pos()
        return [_remote(ins[0].at[pk], outs[0].at[j], sem(j), sem(3 + j), (px, py, c))
                for j, (px, py, pk) in enumerate(_other_chips(x, y))]

    return _Stage(ins=[psum], outs=[jax.ShapeDtypeStruct((3,) + psum.shape[1:], BF16)], n_sems=6,
                  start=lambda ins, io, outs, sem: _start_all(copies(ins, outs, sem)),
                  finish=lambda ins, io, outs, sem: _wait_all(copies(ins, outs, sem)), then=then)


def _pair_fill_stage(final, then):
    h = final.shape[0] // 2

    def copy(io, sem):
        x, y, c = _mesh_pos()
        mine = _half(io[0], c, h)
        return _remote(mine, mine, sem(0), sem(1), (x, y, 1 - c))

    return _Stage(inouts=[final], n_sems=2,
                  start=lambda ins, io, outs, sem: copy(io, sem).start(),
                  finish=lambda ins, io, outs, sem: copy(io, sem).wait(), then=then)


def _spread_stage(packed, then):
    def copies(ins, outs, sem):
        x, y, c = _mesh_pos()
        me = 4 * x + 2 * y + c
        cps = []
        for d in range(1, N_DEVICES):
            peer = me ^ d
            cps.append(_remote(ins[0], outs[0].at[me], sem(d), sem(7 + d), (peer // 4, (peer // 2) % 2, peer % 2)))
        return cps, pltpu.make_async_copy(ins[0], outs[0].at[me], sem(0))

    def start(ins, io, outs, sem):
        cps, own = copies(ins, outs, sem)
        own.start()
        _start_all(cps)

    def finish(ins, io, outs, sem):
        cps, own = copies(ins, outs, sem)
        _wait_all(cps)
        own.wait()

    return _Stage(ins=[packed], outs=[jax.ShapeDtypeStruct((N_DEVICES,) + packed.shape, F32)], n_sems=15,
                  start=start, finish=finish, then=then)


def _staged_call(core, *, name, grid, in_specs, out_specs, out_shape, scratch_shapes, args, stages):
    n_in, n_out, n_scr = len(args), len(out_shape), len(scratch_shapes)
    s_args, s_outs, aliases, layout = [], [], {}, []
    n_sems = 0
    for st in stages:
        i0, o0 = len(s_args), len(s_outs)
        s_args += st.ins + st.inouts
        for q in range(len(st.inouts)):
            aliases[n_in + i0 + len(st.ins) + q] = n_out + o0 + q
        s_outs += [jax.ShapeDtypeStruct(a.shape, a.dtype) for a in st.inouts] + st.outs
        layout.append((i0, o0, n_sems))
        n_sems += st.n_sems
    steps = 1
    for g in grid:
        steps *= g

    def body(*refs):
        own_in = refs[:n_in]
        s_in = refs[n_in:n_in + len(s_args)]
        rest = refs[n_in + len(s_args):]
        own_out = rest[:n_out]
        s_out = rest[n_out:n_out + len(s_outs)]
        scr = rest[n_out + len(s_outs):]

        def run(which):
            for st, (i0, o0, s0) in zip(stages, layout):
                fn = getattr(st, which)
                if fn is not None:
                    fn(s_in[i0:i0 + len(st.ins)], s_out[o0:o0 + len(st.inouts)],
                       s_out[o0 + len(st.inouts):o0 + len(st.inouts) + len(st.outs)],
                       lambda k, s0=s0: scr[n_scr].at[s0 + k])

        if not stages:
            core(*own_in, *own_out, *scr[:n_scr])
            return
        step = 0
        for d, g in enumerate(grid):
            step = step * g + pl.program_id(d)
        if steps == 1:
            run("start")
            core(*own_in, *own_out, *scr[:n_scr])
            run("mid")
            run("finish")
            return
        pl.when(step == 0)(lambda: run("start"))
        core(*own_in, *own_out, *scr[:n_scr])
        pl.when(step == (3 * steps) // 4)(lambda: run("mid"))
        pl.when(step == steps - 1)(lambda: run("finish"))

    sem = ("arbitrary",) * len(grid) if stages else ("parallel",) * max(len(grid) - 1, 0) + ("arbitrary",) * min(len(grid), 1)
    res = pl.pallas_call(
        body, name=name, grid=grid,
        in_specs=list(in_specs) + [ANY] * len(s_args),
        out_specs=list(out_specs) + [ANY] * len(s_outs),
        out_shape=list(out_shape) + s_outs,
        input_output_aliases=aliases,
        scratch_shapes=list(scratch_shapes) + ([pltpu.SemaphoreType.DMA((n_sems,))] if stages else []),
        compiler_params=_params(sem) if grid else pltpu.CompilerParams(vmem_limit_bytes=V7X_VMEM_LIMIT),
    )(*args, *s_args)
    return list(res[:n_out]), list(res[n_out:])


class _Pipe:
    def __init__(self):
        self.ready = []
        self.flushes = 0

    def add(self, stage):
        self.ready.append(stage)

    def carry(self, call):
        stages, self.ready = self.ready, []
        own, outs = call(stages)
        k = 0
        for st in stages:
            n = len(st.inouts) + len(st.outs)
            st.then(*outs[k:k + n])
            k += n
        return own

    def flush(self):
        while self.ready:
            self.flushes += 1
            self.carry(lambda stages: _staged_call(
                lambda *refs: None, name=f"comm_tail_{self.flushes}", grid=(), in_specs=[], out_specs=[], out_shape=[],
                scratch_shapes=[], args=[], stages=stages))


def _mixer_fwd(layer, x, g1, bgate, lng, lnb, wm, bsf, wsc, win_g, wb_g, wout_g, stages):
    t_len = x.shape[0]
    tm = min(TM_MIX, t_len)
    nt = t_len // tm
    nb = tm // GMLP_BLOCK

    def core(x_ref, g1_ref, bgate_ref, lng_ref, lnb_ref, wm_ref, bsf_ref, wsc_ref, win_hbm, wb_hbm, wout_hbm,
             z_ref, ya_ref, yb_ref, q_ref, a_ref, b_ref, mg_ref, h_ref, x2_ref,
             win_v, wb_v, wout_v, carry, vn_s, f_s, sems):
        i = pl.program_id(0)

        @pl.when(i == 0)
        def _():
            cps = (_load_col_sharded(win_hbm, win_v, sems, 0) + _load_branch(wb_hbm, wb_v, sems, 4)
                   + _load_row_sharded(wout_hbm, wout_v, sems, 12))
            _start_all(cps)
            carry[...] = jnp.zeros_like(carry)
            _wait_all(cps)

        xv = x_ref[...]
        r = lax.rsqrt(jnp.mean(xv * xv, axis=-1, keepdims=True) + RMS_EPS)
        h_ref[...] = (xv * r * g1_ref[...]).astype(BF16)

        def zcols(c0, c1):
            zc = _dot(h_ref[...], win_v[:, c0:c1])
            z_ref[:, c0:c1] = zc.astype(BF16)
            return zc

        vg, _ = _gelu(zcols(C_V, C_V + D_A))
        mu = jnp.mean(vg, axis=-1, keepdims=True)
        vc = vg - mu
        rstd = lax.rsqrt(jnp.mean(vc * vc, axis=-1, keepdims=True) + LN_EPS)
        vn_s[...] = (vc * rstd * lng_ref[...] + lnb_ref[...]).astype(BF16)
        for hd in range(A_HEADS):
            cols = slice(hd * 128, (hd + 1) * 128)
            vcat = jnp.concatenate([vn_s[b * 128:(b + 1) * 128, cols] for b in range(nb)], axis=1)
            fcat = _dot(wm_ref[hd], vcat)
            for b in range(nb):
                f_s[b * 128:(b + 1) * 128, cols] = fcat[:, b * 128:(b + 1) * 128]
        ug, _ = _gelu(zcols(C_U, C_U + D_A))
        bias = jnp.concatenate([bsf_ref[...]] * nb, axis=0)
        ya_ref[...] = (ug * (f_s[...] + bias)).astype(BF16)

        p = zcols(C_CG, C_CG + D_B) * zcols(C_HB, C_HB + D_B)
        cr = carry[...]
        q = wsc_ref[0:1, :] * _shift_down(p, cr, 2) + wsc_ref[1:2, :] * _shift_down(p, cr, 1) + wsc_ref[2:3, :] * p
        carry[...] = p[tm - 8:tm, :]
        q_ref[...] = q.astype(BF16)
        yb_ref[...] = (zcols(C_BG, C_BG + D_B) * q).astype(BF16)

        av = _dot(ya_ref[...], wb_v[0])
        a_ref[...] = av.astype(BF16)
        mg = _sigmoid(zcols(C_GA, C_GA + D_MODEL) + bgate_ref[:, 0:D_MODEL]) * av
        bv = _dot(yb_ref[...], wb_v[1])
        b_ref[...] = bv.astype(BF16)
        mg = mg + _sigmoid(zcols(C_GB, C_GB + D_MODEL) + bgate_ref[:, D_MODEL:2 * D_MODEL]) * bv
        mg_ref[...] = mg.astype(BF16)
        x2_ref[...] = x_ref[...] + _dot(mg_ref[...], wout_v[...])

    outs = [
        jax.ShapeDtypeStruct((t_len, D_IN), BF16),
        jax.ShapeDtypeStruct((t_len, D_A), BF16),
        jax.ShapeDtypeStruct((t_len, D_B), BF16),
        jax.ShapeDtypeStruct((t_len, D_B), BF16),
        jax.ShapeDtypeStruct((t_len, D_MODEL), BF16),
        jax.ShapeDtypeStruct((t_len, D_MODEL), BF16),
        jax.ShapeDtypeStruct((t_len, D_MODEL), BF16),
        jax.ShapeDtypeStruct((t_len, D_MODEL), BF16),
        jax.ShapeDtypeStruct((t_len, D_MODEL), F32),
    ]
    return _staged_call(
        core, name=f"mixer_fwd_l{layer}", grid=(nt,),
        in_specs=[_row_spec(tm, D_MODEL), _const_spec((1, D_MODEL)), _const_spec((1, 2 * D_MODEL)),
                  _const_spec((1, D_A)), _const_spec((1, D_A)), _const_spec((A_HEADS, 128, 128)),
                  _const_spec((128, D_A)), _const_spec((8, D_B)), ANY, ANY, ANY],
        out_specs=[_row_spec(tm, o.shape[1]) for o in outs],
        out_shape=outs,
        scratch_shapes=[pltpu.VMEM((D_MODEL, D_IN), BF16), pltpu.VMEM((2, D_A, D_MODEL), BF16),
                        pltpu.VMEM((D_MODEL, D_MODEL), BF16), pltpu.VMEM((8, D_B), F32),
                        pltpu.VMEM((tm, D_A), BF16), pltpu.VMEM((tm, D_A), F32), pltpu.SemaphoreType.DMA((16,))],
        args=[x, g1, bgate, lng, lnb, wm, bsf, wsc, win_g, wb_g, wout_g], stages=stages)


def _ffn_fwd(layer, x2, g2, wfc, bfc, wup_g, wdown_g, stages):
    t_len = x2.shape[0]
    tm = min(TM_FFN, t_len)
    nt = t_len // tm

    def core(x_ref, g2_ref, wfc_ref, bfc_ref, wup_hbm, wdown_hbm, up_ref, gc_ref, act_ref, h_ref, x3_ref,
             wup_v, wdown_v, carry, sems):
        i = pl.program_id(0)

        @pl.when(i == 0)
        def _():
            cps = _load_col_sharded(wup_hbm, wup_v, sems, 0) + _load_row_sharded(wdown_hbm, wdown_v, sems, 4)
            _start_all(cps)
            carry[...] = jnp.zeros_like(carry)
            _wait_all(cps)

        xv = x_ref[...]
        r = lax.rsqrt(jnp.mean(xv * xv, axis=-1, keepdims=True) + RMS_EPS)
        h_ref[...] = (xv * r * g2_ref[...]).astype(BF16)
        acc = xv
        for c0, c1 in FF_CHUNKS:
            gate = _dot(h_ref[...], wup_v[:, c0:c1])
            up_ref[:, c0:c1] = gate.astype(BF16)
            cr = carry[:, c0:c1]
            gc = (wfc_ref[0:1, c0:c1] * _shift_down(gate, cr, 2) + wfc_ref[1:2, c0:c1] * _shift_down(gate, cr, 1)
                  + wfc_ref[2:3, c0:c1] * gate + bfc_ref[:, c0:c1])
            carry[:, c0:c1] = gate[tm - 8:tm, :]
            gc_ref[:, c0:c1] = gc.astype(BF16)
            val = _dot(h_ref[...], wup_v[:, D_FF + c0:D_FF + c1])
            up_ref[:, D_FF + c0:D_FF + c1] = val.astype(BF16)
            act_ref[:, c0:c1] = (gc * _sigmoid(gc) * val).astype(BF16)
            acc = acc + _dot(act_ref[:, c0:c1], wdown_v[c0:c1, :])
        x3_ref[...] = acc

    outs = [
        jax.ShapeDtypeStruct((t_len, 2 * D_FF), BF16),
        jax.ShapeDtypeStruct((t_len, D_FF), BF16),
        jax.ShapeDtypeStruct((t_len, D_FF), BF16),
        jax.ShapeDtypeStruct((t_len, D_MODEL), BF16),
        jax.ShapeDtypeStruct((t_len, D_MODEL), F32),
    ]
    return _staged_call(
        core, name=f"ffn_fwd_l{layer}", grid=(nt,),
        in_specs=[_row_spec(tm, D_MODEL), _const_spec((1, D_MODEL)), _const_spec((8, D_FF)), _const_spec((1, D_FF)), ANY, ANY],
        out_specs=[_row_spec(tm, o.shape[1]) for o in outs],
        out_shape=outs,
        scratch_shapes=[pltpu.VMEM((D_MODEL, 2 * D_FF), BF16), pltpu.VMEM((D_FF, D_MODEL), BF16),
                        pltpu.VMEM((8, D_FF), F32), pltpu.SemaphoreType.DMA((8,))],
        args=[x2, g2, wfc, bfc, wup_g, wdown_g], stages=stages)


def _loss_head(x3, target, gf):
    t_len = x3.shape[0]
    tm = min(TM_EW, t_len)
    nt = t_len // tm

    def body(x_ref, t_ref, gf_ref, dx_ref, dgf_ref, loss_ref):
        i = pl.program_id(0)

        @pl.when(i == 0)
        def _():
            dgf_ref[...] = jnp.zeros_like(dgf_ref)
            loss_ref[...] = jnp.zeros_like(loss_ref)

        xv = x_ref[...]
        r = lax.rsqrt(jnp.mean(xv * xv, axis=-1, keepdims=True) + RMS_EPS)
        xh = xv * r
        err = xh * gf_ref[...] - t_ref[...]
        loss_ref[...] += _colsum8(err * err)
        dy = err * (1.0 / D_MODEL)
        dgf_ref[...] += _colsum8(dy * xh)
        dxh = dy * gf_ref[...]
        dx_ref[...] = r * (dxh - xh * jnp.mean(dxh * xh, axis=-1, keepdims=True))

    return pl.pallas_call(
        body, name="loss_head", grid=(nt,),
        in_specs=[_row_spec(tm, D_MODEL), _row_spec(tm, D_MODEL), _const_spec((1, D_MODEL))],
        out_specs=[_row_spec(tm, D_MODEL), _const_spec((8, D_MODEL)), _const_spec((8, D_MODEL))],
        out_shape=[jax.ShapeDtypeStruct((t_len, D_MODEL), F32), jax.ShapeDtypeStruct((8, D_MODEL), F32),
                   jax.ShapeDtypeStruct((8, D_MODEL), F32)],
        compiler_params=_params(),
    )(x3, target, gf)


def _ffn_bwd(layer, dx3, x2, up, gcs, g2, wfc, wup_g, wdown_g, stages):
    t_len = x2.shape[0]
    tm = min(TM_FFN, t_len)
    nt = t_len // tm

    def core(dx3_ref, x_ref, up_ref, gc_ref, g2_ref, wfc_ref, wup_hbm, wdown_hbm,
             dx2_ref, dup_ref, dx3b_ref, dg2_ref, dbfc_ref, dwfc_ref,
             wup_v, wdown_v, carry, sems):
        i = pl.program_id(0)

        @pl.when(i == 0)
        def _():
            cps = _load_col_sharded(wup_hbm, wup_v, sems, 0) + _load_row_sharded(wdown_hbm, wdown_v, sems, 4)
            _start_all(cps)
            carry[...] = jnp.zeros_like(carry)
            dg2_ref[...] = jnp.zeros_like(dg2_ref)
            dbfc_ref[...] = jnp.zeros_like(dbfc_ref)
            dwfc_ref[...] = jnp.zeros_like(dwfc_ref)
            _wait_all(cps)

        dx3b_ref[...] = dx3_ref[...].astype(BF16)
        dh = jnp.zeros((tm, D_MODEL), F32)
        for c0, c1 in FF_CHUNKS:
            v0, v1 = D_FF + c0, D_FF + c1
            da = _dot_nt(dx3b_ref[...], wdown_v[c0:c1, :])
            gc = gc_ref[:, c0:c1].astype(F32)
            sg = _sigmoid(gc)
            dup_ref[:, v0:v1] = (da * gc * sg).astype(BF16)
            dgc = da * up_ref[:, v0:v1].astype(F32) * sg * (1.0 + gc * (1.0 - sg))
            cr = carry[:, c0:c1]
            dgc1 = _shift_up(dgc, cr, 1)
            dgc2 = _shift_up(dgc, cr, 2)
            carry[:, c0:c1] = dgc[0:8, :]
            gate = up_ref[:, c0:c1].astype(F32)
            dbfc_ref[:, c0:c1] += _colsum8(dgc)
            dwfc_ref[0, :, c0:c1] += _colsum8(dgc2 * gate)
            dwfc_ref[1, :, c0:c1] += _colsum8(dgc1 * gate)
            dwfc_ref[2, :, c0:c1] += _colsum8(dgc * gate)
            dgate = wfc_ref[2:3, c0:c1] * dgc + wfc_ref[1:2, c0:c1] * dgc1 + wfc_ref[0:1, c0:c1] * dgc2
            dup_ref[:, c0:c1] = dgate.astype(BF16)
            dh = dh + _dot_nt(dup_ref[:, c0:c1], wup_v[:, c0:c1]) + _dot_nt(dup_ref[:, v0:v1], wup_v[:, v0:v1])
        xv = x_ref[...]
        r = lax.rsqrt(jnp.mean(xv * xv, axis=-1, keepdims=True) + RMS_EPS)
        xh = xv * r
        dg2_ref[...] += _colsum8(dh * xh)
        dxh = dh * g2_ref[...]
        dx2_ref[...] = dx3_ref[...] + r * (dxh - xh * jnp.mean(dxh * xh, axis=-1, keepdims=True))

    outs = [
        jax.ShapeDtypeStruct((t_len, D_MODEL), F32),
        jax.ShapeDtypeStruct((t_len, 2 * D_FF), BF16),
        jax.ShapeDtypeStruct((t_len, D_MODEL), BF16),
        jax.ShapeDtypeStruct((8, D_MODEL), F32),
        jax.ShapeDtypeStruct((8, D_FF), F32),
        jax.ShapeDtypeStruct((3, 8, D_FF), F32),
    ]
    return _staged_call(
        core, name=f"ffn_bwd_l{layer}", grid=(nt,),
        in_specs=[_row_spec(tm, D_MODEL, nt), _row_spec(tm, D_MODEL, nt), _row_spec(tm, 2 * D_FF, nt),
                  _row_spec(tm, D_FF, nt), _const_spec((1, D_MODEL)), _const_spec((8, D_FF)), ANY, ANY],
        out_specs=[_row_spec(tm, D_MODEL, nt), _row_spec(tm, 2 * D_FF, nt), _row_spec(tm, D_MODEL, nt),
                   _const_spec((8, D_MODEL)), _const_spec((8, D_FF)), _const_spec((3, 8, D_FF))],
        out_shape=outs,
        scratch_shapes=[pltpu.VMEM((D_MODEL, 2 * D_FF), BF16), pltpu.VMEM((D_FF, D_MODEL), BF16),
                        pltpu.VMEM((8, D_FF), F32), pltpu.SemaphoreType.DMA((8,))],
        args=[dx3, x2, up, gcs, g2, wfc, wup_g, wdown_g], stages=stages)


def _mixer_bwd(layer, dx2, x, z, qs, av, bv, g1, bgate, lng, lnb, wm, wmt, bsf, wsc, win_g, wb_g, wout_g, stages):
    t_len = x.shape[0]
    tm = min(TM_MIX, t_len)
    nt = t_len // tm
    nb = tm // GMLP_BLOCK

    def core(dx2_ref, x_ref, z_ref, q_ref, a_ref, b_ref, g1_ref, bgate_ref, lng_ref, lnb_ref,
             wm_ref, wmt_ref, bsf_ref, wsc_ref, win_hbm, wb_hbm, wout_hbm,
             dx_ref, dz_ref, da_ref, db_ref, dx2b_ref, dg1_ref, dbgate_ref, dlng_ref, dlnb_ref, dwm_ref, dbsf_ref, dwsc_ref,
             win_v, wb_v, wout_v, carry, vn_s, f_s, df_s, dvn_s, sems):
        i = pl.program_id(0)

        @pl.when(i == 0)
        def _():
            cps = (_load_col_sharded(win_hbm, win_v, sems, 0) + _load_branch(wb_hbm, wb_v, sems, 4)
                   + _load_row_sharded(wout_hbm, wout_v, sems, 12))
            _start_all(cps)
            carry[...] = jnp.zeros_like(carry)
            for ref in (dg1_ref, dbgate_ref, dlng_ref, dlnb_ref, dwm_ref, dbsf_ref, dwsc_ref):
                ref[...] = jnp.zeros_like(ref)
            _wait_all(cps)

        def zc(c0, n):
            return z_ref[:, c0:c0 + n].astype(F32)

        dx2b_ref[...] = dx2_ref[...].astype(BF16)
        dm = _dot_nt(dx2b_ref[...], wout_v[...])
        def dz_cols(c0, n, val):
            dz_ref[:, c0:c0 + n] = val.astype(BF16)
            return _dot_nt(dz_ref[:, c0:c0 + n], win_v[:, c0:c0 + n])

        sa = _sigmoid(zc(C_GA, D_MODEL) + bgate_ref[:, 0:D_MODEL])
        da_ref[...] = (dm * sa).astype(BF16)
        dga = dm * a_ref[...].astype(F32) * sa * (1.0 - sa)
        dh = dz_cols(C_GA, D_MODEL, dga)
        dbgate_ref[:, 0:D_MODEL] += _colsum8(dga)
        dya = _dot_nt(da_ref[...], wb_v[0])
        sb = _sigmoid(zc(C_GB, D_MODEL) + bgate_ref[:, D_MODEL:2 * D_MODEL])
        db_ref[...] = (dm * sb).astype(BF16)
        dgb = dm * b_ref[...].astype(F32) * sb * (1.0 - sb)
        dh = dh + dz_cols(C_GB, D_MODEL, dgb)
        dbgate_ref[:, D_MODEL:2 * D_MODEL] += _colsum8(dgb)
        dyb = _dot_nt(db_ref[...], wb_v[1])

        v = zc(C_V, D_A)
        vg, tv = _gelu(v)
        mu = jnp.mean(vg, axis=-1, keepdims=True)
        vc = vg - mu
        rstd = lax.rsqrt(jnp.mean(vc * vc, axis=-1, keepdims=True) + LN_EPS)
        xh = vc * rstd
        vn_s[...] = (xh * lng_ref[...] + lnb_ref[...]).astype(BF16)
        u = zc(C_U, D_A)
        ug, tu = _gelu(u)
        df = dya * ug
        df_s[...] = df.astype(BF16)
        dbsf_acc = df[0:128, :]
        for b in range(1, nb):
            dbsf_acc = dbsf_acc + df[b * 128:(b + 1) * 128, :]
        dbsf_ref[...] += dbsf_acc
        for hd in range(A_HEADS):
            cols = slice(hd * 128, (hd + 1) * 128)
            vcat = jnp.concatenate([vn_s[b * 128:(b + 1) * 128, cols] for b in range(nb)], axis=1)
            dcat = jnp.concatenate([df_s[b * 128:(b + 1) * 128, cols] for b in range(nb)], axis=1)
            fcat = _dot(wm_ref[hd], vcat)
            gcat = _dot(wmt_ref[hd], dcat)
            dwm_ref[hd] += _dot_nt(dcat, vcat)
            for b in range(nb):
                f_s[b * 128:(b + 1) * 128, cols] = fcat[:, b * 128:(b + 1) * 128]
                dvn_s[b * 128:(b + 1) * 128, cols] = gcat[:, b * 128:(b + 1) * 128]
        bias = jnp.concatenate([bsf_ref[...]] * nb, axis=0)
        dh = dh + dz_cols(C_U, D_A, dya * (f_s[...] + bias) * _gelu_grad(u, tu))
        dvn = dvn_s[...]
        dlng_ref[...] += _colsum8(dvn * xh)
        dlnb_ref[...] += _colsum8(dvn)
        dxh = dvn * lng_ref[...]
        dvg = rstd * (dxh - jnp.mean(dxh, axis=-1, keepdims=True) - xh * jnp.mean(dxh * xh, axis=-1, keepdims=True))
        dh = dh + dz_cols(C_V, D_A, dvg * _gelu_grad(v, tv))

        cg = zc(C_CG, D_B)
        hbv = zc(C_HB, D_B)
        p = cg * hbv
        dh = dh + dz_cols(C_BG, D_B, dyb * q_ref[...].astype(F32))
        dq = dyb * zc(C_BG, D_B)
        cr = carry[...]
        dq1 = _shift_up(dq, cr, 1)
        dq2 = _shift_up(dq, cr, 2)
        carry[...] = dq[0:8, :]
        dwsc_ref[0] += _colsum8(dq2 * p)
        dwsc_ref[1] += _colsum8(dq1 * p)
        dwsc_ref[2] += _colsum8(dq * p)
        dp = wsc_ref[2:3, :] * dq + wsc_ref[1:2, :] * dq1 + wsc_ref[0:1, :] * dq2
        dh = dh + dz_cols(C_CG, D_B, dp * hbv)
        dh = dh + dz_cols(C_HB, D_B, dp * cg)

        xv = x_ref[...]
        r = lax.rsqrt(jnp.mean(xv * xv, axis=-1, keepdims=True) + RMS_EPS)
        xn = xv * r
        dg1_ref[...] += _colsum8(dh * xn)
        dxn = dh * g1_ref[...]
        dx_ref[...] = dx2_ref[...] + r * (dxn - xn * jnp.mean(dxn * xn, axis=-1, keepdims=True))

    outs = [
        jax.ShapeDtypeStruct((t_len, D_MODEL), F32),
        jax.ShapeDtypeStruct((t_len, D_IN), BF16),
        jax.ShapeDtypeStruct((t_len, D_MODEL), BF16),
        jax.ShapeDtypeStruct((t_len, D_MODEL), BF16),
        jax.ShapeDtypeStruct((t_len, D_MODEL), BF16),
        jax.ShapeDtypeStruct((8, D_MODEL), F32),
        jax.ShapeDtypeStruct((8, 2 * D_MODEL), F32),
        jax.ShapeDtypeStruct((8, D_A), F32),
        jax.ShapeDtypeStruct((8, D_A), F32),
        jax.ShapeDtypeStruct((A_HEADS, 128, 128), F32),
        jax.ShapeDtypeStruct((128, D_A), F32),
        jax.ShapeDtypeStruct((3, 8, D_B), F32),
    ]

    return _staged_call(
        core, name=f"mixer_bwd_l{layer}", grid=(nt,),
        in_specs=[_row_spec(tm, D_MODEL, nt), _row_spec(tm, D_MODEL, nt), _row_spec(tm, D_IN, nt),
                  _row_spec(tm, D_B, nt), _row_spec(tm, D_MODEL, nt), _row_spec(tm, D_MODEL, nt),
                  _const_spec((1, D_MODEL)), _const_spec((1, 2 * D_MODEL)), _const_spec((1, D_A)), _const_spec((1, D_A)),
                  _const_spec((A_HEADS, 128, 128)), _const_spec((A_HEADS, 128, 128)), _const_spec((128, D_A)),
                  _const_spec((8, D_B)), ANY, ANY, ANY],
        out_specs=[_row_spec(tm, D_MODEL, nt), _row_spec(tm, D_IN, nt), _row_spec(tm, D_MODEL, nt),
                   _row_spec(tm, D_MODEL, nt), _row_spec(tm, D_MODEL, nt),
                   _const_spec((8, D_MODEL)), _const_spec((8, 2 * D_MODEL)), _const_spec((8, D_A)), _const_spec((8, D_A)),
                   _const_spec((A_HEADS, 128, 128)), _const_spec((128, D_A)), _const_spec((3, 8, D_B))],
        out_shape=outs,
        scratch_shapes=[pltpu.VMEM((D_MODEL, D_IN), BF16), pltpu.VMEM((2, D_A, D_MODEL), BF16),
                        pltpu.VMEM((D_MODEL, D_MODEL), BF16), pltpu.VMEM((8, D_B), F32),
                        pltpu.VMEM((tm, D_A), BF16), pltpu.VMEM((tm, D_A), F32), pltpu.VMEM((tm, D_A), BF16),
                        pltpu.VMEM((tm, D_A), F32), pltpu.SemaphoreType.DMA((16,))],
        args=[dx2, x, z, qs, av, bv, g1, bgate, lng, lnb, wm, wmt, bsf, wsc, win_g, wb_g, wout_g], stages=stages)


def _wgrad(name, layer, a, b, rows, cols, row_blk, col_blk, stages):
    t_len, m = a.shape
    n = b.shape[1]
    tk = min(1024, t_len)
    col_sharded = n == N_CHIPS * cols
    grid = (m // row_blk, n // col_blk, t_len // tk)
    per_shard_c = cols // col_blk

    if col_sharded:
        out_shape = (N_CHIPS, rows, cols)
        out_spec = pl.BlockSpec((None, row_blk, col_blk), lambda i, j, k: (j // per_shard_c, i, j % per_shard_c))
    else:
        out_shape = (N_CHIPS * rows, cols)
        out_spec = pl.BlockSpec((row_blk, col_blk), lambda i, j, k: (i, j))

    def core(a_ref, b_ref, o_ref):
        @pl.when(pl.program_id(2) == 0)
        def _():
            o_ref[...] = jnp.zeros_like(o_ref)

        o_ref[...] += _dot_tn(a_ref[...], b_ref[...])

    own, outs = _staged_call(
        core, name=f"wgrad_{name}_l{layer}", grid=grid,
        in_specs=[pl.BlockSpec((tk, row_blk), lambda i, j, k: (k, i)), pl.BlockSpec((tk, col_blk), lambda i, j, k: (k, j))],
        out_specs=[out_spec], out_shape=[jax.ShapeDtypeStruct(out_shape, F32)], scratch_shapes=[],
        args=[a, b], stages=stages)
    return [own[0].reshape(N_CHIPS, rows, cols)], outs


def _wgrad_branch(layer, ya, da, yb, db, stages):
    t_len = ya.shape[0]
    tk = min(1024, t_len)

    def core(ya_ref, da_ref, yb_ref, db_ref, o_ref):
        @pl.when(pl.program_id(1) == 0)
        def _():
            o_ref[...] = jnp.zeros_like(o_ref)

        o_ref[0:D_A, :] += _dot_tn(ya_ref[...], da_ref[...])
        o_ref[D_A:2 * D_A, :] += _dot_tn(yb_ref[...], db_ref[...])

    a_spec = pl.BlockSpec((tk, D_A), lambda j, k: (k, 0))
    d_spec = pl.BlockSpec((tk, 256), lambda j, k: (k, j))
    return _staged_call(
        core, name=f"wgrad_w_branch_l{layer}", grid=(N_CHIPS, t_len // tk),
        in_specs=[a_spec, d_spec, a_spec, d_spec],
        out_specs=[pl.BlockSpec((None, 2 * D_A, 256), lambda j, k: (j, 0, 0))],
        out_shape=[jax.ShapeDtypeStruct((N_CHIPS, 2 * D_A, 256), F32)], scratch_shapes=[],
        args=[ya, da, yb, db], stages=stages)


def _all_reduce_small(name, packed):
    rows = packed.shape[0]

    def body(src_ref, out_ref, slots, send, recv):
        x, y, c = _mesh_pos()
        me = 4 * x + 2 * y + c
        cps = []
        for d in range(1, N_DEVICES):
            peer = me ^ d
            cps.append(_remote(src_ref, slots.at[me], send.at[d - 1], recv.at[d - 1],
                               (peer // 4, (peer // 2) % 2, peer % 2)))
        _start_all(cps)
        slots[me] = src_ref[...]
        _wait_all(cps)
        acc = slots[0]
        for d in range(1, N_DEVICES):
            acc = acc + slots[d]
        out_ref[...] = acc

    return pl.pallas_call(
        body, name=f"all_reduce_{name}",
        in_specs=[pl.BlockSpec(memory_space=pltpu.VMEM)], out_specs=pl.BlockSpec(memory_space=pltpu.VMEM),
        out_shape=jax.ShapeDtypeStruct(packed.shape, F32),
        scratch_shapes=[pltpu.VMEM((N_DEVICES, rows, 128), F32), pltpu.SemaphoreType.DMA((7,)),
                        pltpu.SemaphoreType.DMA((7,))],
        compiler_params=pltpu.CompilerParams(vmem_limit_bytes=V7X_VMEM_LIMIT),
    )(packed)


def _flat_blk(rows, cols):
    blk = rows
    while blk * cols * 4 > 2 * 1024 * 1024 and blk % 16 == 0:
        blk //= 2
    return blk


def _cast_into_slot(name, layer, w, chip):
    _, rows, cols = w.shape
    blk = _flat_blk(rows, cols)

    def body(chip_ref, w_ref, o_ref):
        o_ref[...] = w_ref[...].astype(BF16)

    return pl.pallas_call(
        body, name=f"cast_{name}_l{layer}",
        grid_spec=pltpu.PrefetchScalarGridSpec(
            num_scalar_prefetch=1, grid=(rows // blk,),
            in_specs=[pl.BlockSpec((None, blk, cols), lambda i, chip_ref: (layer, i, 0))],
            out_specs=pl.BlockSpec((None, blk, cols), lambda i, chip_ref: (chip_ref[0], i, 0))),
        out_shape=jax.ShapeDtypeStruct((N_CHIPS, rows, cols), BF16),
        compiler_params=_params(("parallel",)),
    )(chip, w)


def _pair_sum(name, grad, other, core):
    _, h, cols = other.shape
    blk = _flat_blk(h, cols)
    nblk = h // blk

    def body(core_ref, g_ref, o_ref, s_ref):
        s_ref[...] = (g_ref[...] + o_ref[...]).astype(BF16)

    spec = pl.BlockSpec((None, blk, cols), lambda k, i, core_ref: (k, i, 0))
    return pl.pallas_call(
        body, name=f"pair_sum_{name}",
        grid_spec=pltpu.PrefetchScalarGridSpec(
            num_scalar_prefetch=1, grid=(N_CHIPS, nblk),
            in_specs=[pl.BlockSpec((None, blk, cols), lambda k, i, core_ref: (k, core_ref[0] * nblk + i, 0)), spec],
            out_specs=spec),
        out_shape=jax.ShapeDtypeStruct((N_CHIPS, h, cols), BF16),
        compiler_params=_params(("parallel", "parallel")),
    )(core, grad, other)


def _chip_sum(name, grad, other, got, pos):
    _, rows, cols = grad.shape
    h = rows // 2
    blk = _flat_blk(h, cols)
    nblk = h // blk

    def body(pos_ref, g_ref, o_ref, r_ref, f_ref):
        f_ref[...] = (((g_ref[...] + o_ref[...]) + r_ref[0].astype(F32)) + r_ref[1].astype(F32)) + r_ref[2].astype(F32)

    return pl.pallas_call(
        body, name=f"chip_sum_{name}",
        grid_spec=pltpu.PrefetchScalarGridSpec(
            num_scalar_prefetch=1, grid=(nblk,),
            in_specs=[pl.BlockSpec((None, blk, cols), lambda i, pos_ref: (pos_ref[0], pos_ref[1] * nblk + i, 0)),
                      pl.BlockSpec((None, blk, cols), lambda i, pos_ref: (pos_ref[0], i, 0)),
                      pl.BlockSpec((3, blk, cols), lambda i, pos_ref: (0, i, 0))],
            out_specs=pl.BlockSpec((blk, cols), lambda i, pos_ref: (pos_ref[1] * nblk + i, 0))),
        out_shape=jax.ShapeDtypeStruct((rows, cols), F32),
        compiler_params=_params(("parallel",)),
    )(pos, grad, other, got)


def _sum_slots(name, slots):
    _, rows, _ = slots.shape

    def body(s_ref, o_ref):
        acc = s_ref[0]
        for d in range(1, N_DEVICES):
            acc = acc + s_ref[d]
        o_ref[...] = acc

    return pl.pallas_call(
        body, name=f"sum_slots_{name}", grid=(1,),
        in_specs=[pl.BlockSpec((N_DEVICES, rows, 128), lambda i: (0, 0, 0))],
        out_specs=pl.BlockSpec((rows, 128), lambda i: (0, 0)),
        out_shape=jax.ShapeDtypeStruct((rows, 128), F32),
        compiler_params=_params(),
    )(slots)


def _adamw_math(w, g, m, v):
    m2 = ADAM_B1 * m + (1.0 - ADAM_B1) * g
    v2 = ADAM_B2 * v + (1.0 - ADAM_B2) * (g * g)
    m_hat = m2 / (1.0 - ADAM_B1 ** ADAM_STEP)
    v_hat = v2 / (1.0 - ADAM_B2 ** ADAM_STEP)
    delta = -ADAM_LR * (m_hat / (jnp.sqrt(v_hat) + ADAM_EPS) + ADAM_WD * w)
    return delta, m2, v2


def _adamw_big(name, w, g0, g1, m, v):
    _, rows, cols = w.shape
    blk = _flat_blk(rows, cols) // 2

    def body(w_ref, g0_ref, g1_ref, m_ref, v_ref, g_ref, d_ref, m2_ref, v2_ref):
        g = jnp.where(pl.program_id(0) == 0, g0_ref[...], g1_ref[...])
        d, m2, v2 = _adamw_math(w_ref[...], g, m_ref[...], v_ref[...])
        g_ref[...] = g
        d_ref[...] = d
        m2_ref[...] = m2
        v2_ref[...] = v2

    spec = pl.BlockSpec((None, blk, cols), lambda la, i: (la, i, 0))
    return pl.pallas_call(
        body, name=f"adamw_{name}", grid=(N_LAYERS, rows // blk),
        in_specs=[spec, pl.BlockSpec((blk, cols), lambda la, i: (i * (1 - la), 0)),
                  pl.BlockSpec((blk, cols), lambda la, i: (i * la, 0)), spec, spec],
        out_specs=[spec] * 4,
        out_shape=[jax.ShapeDtypeStruct(w.shape, F32)] * 4,
        compiler_params=_params(("parallel", "parallel")),
    )(w, g0, g1, m, v)


def _adamw(name, w, g, m, v):
    rows, cols = w.shape
    blk = _flat_blk(rows, cols)

    def body(w_ref, g_ref, m_ref, v_ref, d_ref, m2_ref, v2_ref):
        d, m2, v2 = _adamw_math(w_ref[...], g_ref[...], m_ref[...], v_ref[...])
        d_ref[...] = d
        m2_ref[...] = m2
        v2_ref[...] = v2

    spec = pl.BlockSpec((blk, cols), lambda i: (i, 0))
    return pl.pallas_call(
        body, name=f"adamw_{name}", grid=(rows // blk,),
        in_specs=[spec] * 4, out_specs=[spec] * 3,
        out_shape=[jax.ShapeDtypeStruct((rows, cols), F32)] * 3,
        compiler_params=_params(("parallel",)),
    )(w, g, m, v)


SMALL = ("norm1_g", "b_gate", "gmlp_ln_g", "gmlp_ln_b", "w_spatial", "b_spatial", "w_shortconv", "norm2_g",
         "w_ffn_conv", "b_ffn_conv", "final_g")
ALL_WEIGHTS = ("norm1_g", "w_in", "b_gate", "gmlp_ln_g", "gmlp_ln_b", "w_spatial", "b_spatial", "w_shortconv",
               "w_branch", "w_out", "norm2_g", "w_ffn_up", "w_ffn_conv", "b_ffn_conv", "w_ffn_down", "final_g")


def _pack(arrays):
    flat = jnp.concatenate([a.reshape(-1) for a in arrays])
    n = flat.shape[0]
    rows = -(-n // 1024) * 8
    return jnp.pad(flat, (0, rows * 128 - n)).reshape(rows, 128)


def _unpack(packed, like):
    flat = packed.reshape(-1)
    out, off = [], 0
    for a in like:
        out.append(flat[off:off + a.size].reshape(a.shape))
        off += a.size
    return out


def _pad8(w):
    return jnp.pad(w, ((0, 5), (0, 0)))


def kernel(x, norm1_g, w_in, b_gate, gmlp_ln_g, gmlp_ln_b, w_spatial, b_spatial, w_shortconv, w_branch, w_out, norm2_g, w_ffn_up, w_ffn_conv, b_ffn_conv, w_ffn_down, final_g, loss_target, m_norm1_g, m_w_in, m_b_gate, m_gmlp_ln_g, m_gmlp_ln_b, m_w_spatial, m_b_spatial, m_w_shortconv, m_w_branch, m_w_out, m_norm2_g, m_w_ffn_up, m_w_ffn_conv, m_b_ffn_conv, m_w_ffn_down, m_final_g, v_norm1_g, v_w_in, v_b_gate, v_gmlp_ln_g, v_gmlp_ln_b, v_w_spatial, v_b_spatial, v_w_shortconv, v_w_branch, v_w_out, v_norm2_g, v_w_ffn_up, v_w_ffn_conv, v_b_ffn_conv, v_w_ffn_down, v_final_g):
    weights = dict(norm1_g=norm1_g, w_in=w_in, b_gate=b_gate, gmlp_ln_g=gmlp_ln_g, gmlp_ln_b=gmlp_ln_b,
                   w_spatial=w_spatial, b_spatial=b_spatial, w_shortconv=w_shortconv, w_branch=w_branch, w_out=w_out,
                   norm2_g=norm2_g, w_ffn_up=w_ffn_up, w_ffn_conv=w_ffn_conv, b_ffn_conv=b_ffn_conv,
                   w_ffn_down=w_ffn_down, final_g=final_g)
    mom = dict(norm1_g=m_norm1_g, w_in=m_w_in, b_gate=m_b_gate, gmlp_ln_g=m_gmlp_ln_g, gmlp_ln_b=m_gmlp_ln_b,
               w_spatial=m_w_spatial, b_spatial=m_b_spatial, w_shortconv=m_w_shortconv, w_branch=m_w_branch,
               w_out=m_w_out, norm2_g=m_norm2_g, w_ffn_up=m_w_ffn_up, w_ffn_conv=m_w_ffn_conv,
               b_ffn_conv=m_b_ffn_conv, w_ffn_down=m_w_ffn_down, final_g=m_final_g)
    vel = dict(norm1_g=v_norm1_g, w_in=v_w_in, b_gate=v_b_gate, gmlp_ln_g=v_gmlp_ln_g, gmlp_ln_b=v_gmlp_ln_b,
               w_spatial=v_w_spatial, b_spatial=v_b_spatial, w_shortconv=v_w_shortconv, w_branch=v_w_branch,
               w_out=v_w_out, norm2_g=v_norm2_g, w_ffn_up=v_w_ffn_up, w_ffn_conv=v_w_ffn_conv,
               b_ffn_conv=v_b_ffn_conv, w_ffn_down=v_w_ffn_down, final_g=v_final_g)

    cx, cy, cc = _mesh_pos()
    chip = 2 * cx + cy
    core_arr = cc.astype(jnp.int32).reshape(1)
    chip_arr = chip.astype(jnp.int32).reshape(1)
    pos_arr = jnp.stack([chip, cc]).astype(jnp.int32)
    t_len = x.shape[1]
    xs = x.reshape(t_len, D_MODEL)
    target = loss_target.reshape(t_len, D_MODEL)
    pipe = _Pipe()

    full = {}

    def gather(keys):
        slots = [_cast_into_slot(n, la, weights[n].reshape((N_LAYERS,) + BIG[n]), chip_arr) for n, la in keys]

        def then(*bufs):
            full.update(zip(keys, bufs))

        pipe.add(_gather_stage(slots, then))

    mixer_w = ("w_in", "w_branch", "w_out")
    ffn_w = ("w_ffn_up", "w_ffn_down")
    gather([(n, 0) for n in mixer_w])
    pipe.flush()

    idx = jnp.arange(GMLP_BLOCK) // CHUNK
    mask = idx[None, :] <= idx[:, None]
    wm_all = jnp.where(mask[None, None], w_spatial, 0.0)
    wm_bf = wm_all.astype(BF16)
    wmt_bf = jnp.swapaxes(wm_all, -1, -2).astype(BF16)
    bsf = jnp.repeat(jnp.swapaxes(b_spatial, -1, -2), 128, axis=-1)
    wsc_full = lax.dynamic_update_slice(jnp.zeros((N_LAYERS, 3, D_B), F32), w_shortconv, (0, 0, chip * (D_B // 4)))
    wfc_full = lax.dynamic_update_slice(jnp.zeros((N_LAYERS, 3, D_FF), F32), w_ffn_conv, (0, 0, chip * (D_FF // 4)))
    taps = _all_reduce_small("conv_taps", _pack([wsc_full, wfc_full]))
    wsc_full, wfc_full = _unpack(taps * 0.5, [wsc_full, wfc_full])

    def row(a):
        return a.reshape(1, -1)

    def mixer_args(la):
        return (row(norm1_g[la]), row(b_gate[la]), row(gmlp_ln_g[la]), row(gmlp_ln_b[la]))

    def mixer_weights(la):
        return tuple(full[(n, la)] for n in mixer_w)

    def ffn_weights(la):
        return tuple(full[(n, la)] for n in ffn_w)

    saved = []
    h_in = xs
    for la in range(N_LAYERS):
        gather([(n, la) for n in ffn_w])
        z, ya, yb, qs, av, bv, mg, h1, x2 = pipe.carry(lambda st: _mixer_fwd(
            la, h_in, *mixer_args(la), wm_bf[la], bsf[la], _pad8(wsc_full[la]), *mixer_weights(la), st))
        if la + 1 < N_LAYERS:
            gather([(n, la + 1) for n in mixer_w])
        up, gcs, act, h2, x3 = pipe.carry(lambda st: _ffn_fwd(
            la, x2, row(norm2_g[la]), _pad8(wfc_full[la]), row(b_ffn_conv[la]), *ffn_weights(la), st))
        saved.append(dict(x=h_in, z=z, ya=ya, yb=yb, q=qs, av=av, bv=bv, mg=mg, h1=h1, x2=x2, up=up, gc=gcs, act=act,
                          h2=h2))
        h_in = x3

    reduced_big = {}

    def reduce_big(name, la, grad):
        tag = f"{name}_l{la}"

        def after_pair(other):
            psum = _pair_sum(tag, grad, other, core_arr)

            def after_chips(got):
                final = _chip_sum(tag, grad, other, got, pos_arr)
                pipe.add(_pair_fill_stage(final, lambda done: reduced_big.__setitem__((name, la), done)))

            pipe.add(_chip_send_stage(psum, after_chips))

        pipe.add(_pair_send_stage(grad, after_pair))

    dx, dgf8, loss8 = _loss_head(h_in, target, row(final_g))
    small = {n: [None] * N_LAYERS for n in SMALL}
    deferred = []
    for la in reversed(range(N_LAYERS)):
        s = saved[la]
        dx3 = dx
        dx2, dup, dx3b, dg2, dbfc, dwfc = pipe.carry(lambda st: _ffn_bwd(
            la, dx3, s["x2"], s["up"], s["gc"], row(norm2_g[la]), _pad8(wfc_full[la]), *ffn_weights(la), st))
        g, = pipe.carry(lambda st: _wgrad("w_ffn_down", la, s["act"], dx3b, 704, 1024, 1408, 1024, st))
        reduce_big("w_ffn_down", la, g)
        g, = pipe.carry(lambda st: _wgrad("w_ffn_up", la, s["h2"], dup, 1024, 1408, 1024, 1408, st))
        reduce_big("w_ffn_up", la, g)
        dxl, dz, da, db, dx2b, dg1, dbg, dlng, dlnb, dwm, dbsf, dwsc = pipe.carry(lambda st: _mixer_bwd(
            la, dx2, s["x"], s["z"], s["q"], s["av"], s["bv"], *mixer_args(la), wm_bf[la], wmt_bf[la], bsf[la],
            _pad8(wsc_full[la]), *mixer_weights(la), st))
        g, = pipe.carry(lambda st: _wgrad("w_in", la, s["h1"], dz, 1024, 1152, 1024, 1152, st))
        reduce_big("w_in", la, g)
        deferred.append((la, s["mg"], dx2b, s["ya"], da, s["yb"], db))
        small["norm1_g"][la] = dg1.sum(0)
        small["b_gate"][la] = dbg.sum(0)
        small["gmlp_ln_g"][la] = dlng.sum(0)
        small["gmlp_ln_b"][la] = dlnb.sum(0)
        small["w_spatial"][la] = jnp.where(mask[None], dwm, 0.0)
        small["b_spatial"][la] = dbsf.reshape(128, A_HEADS, 128).sum(-1).T
        small["w_shortconv"][la] = dwsc.sum(1)
        small["norm2_g"][la] = dg2.sum(0)
        small["w_ffn_conv"][la] = dwfc.sum(1)
        small["b_ffn_conv"][la] = dbfc.sum(0)
        dx = dxl
    grad_x = dx.reshape(x.shape)

    small_local = [jnp.stack(small[n]) for n in SMALL[:-1]] + [dgf8.sum(0), 0.5 * loss8.sum().reshape(1) / D_MODEL]
    spread = {}
    pipe.add(_spread_stage(_pack(small_local), lambda slots: spread.__setitem__("slots", slots)))

    for la, mg, dx2b, ya, da, yb, db in deferred:
        g, = pipe.carry(lambda st: _wgrad("w_out", la, mg, dx2b, 256, 1024, 1024, 1024, st))
        reduce_big("w_out", la, g)
    for la, mg, dx2b, ya, da, yb, db in deferred:
        g, = pipe.carry(lambda st: _wgrad_branch(la, ya, da, yb, db, st))
        reduce_big("w_branch", la, g)
    pipe.flush()

    reduced = _unpack(_sum_slots("small_grads", spread["slots"]), small_local)
    loss = reduced[-1].reshape(())
    grads = dict(zip(SMALL, reduced[:-1]))
    grads["w_shortconv"] = lax.dynamic_slice(grads["w_shortconv"], (0, 0, chip * (D_B // 4)), (N_LAYERS, 3, D_B // 4))
    grads["w_ffn_conv"] = lax.dynamic_slice(grads["w_ffn_conv"], (0, 0, chip * (D_FF // 4)), (N_LAYERS, 3, D_FF // 4))

    delta, new_m, new_v = {}, {}, {}
    for n in BIG_NAMES:
        shape3 = (N_LAYERS,) + BIG[n]
        res = _adamw_big(n, weights[n].reshape(shape3), reduced_big[(n, 0)], reduced_big[(n, 1)],
                         mom[n].reshape(shape3), vel[n].reshape(shape3))
        grads[n], delta[n], new_m[n], new_v[n] = (a.reshape(weights[n].shape) for a in res)
    small_w = [weights[n] for n in SMALL]
    packed = [_pack([src[n] for n in SMALL]) for src in (weights, grads, mom, vel)]
    for dst, res in zip((delta, new_m, new_v), _adamw("small", *packed)):
        dst.update(zip(SMALL, _unpack(res, small_w)))

    return (loss, grad_x, *[grads[n] for n in ALL_WEIGHTS], *[delta[n] for n in ALL_WEIGHTS],
            *[new_m[n] for n in ALL_WEIGHTS], *[new_v[n] for n in ALL_WEIGHTS])
```

```python
import jax
import jax.numpy as jnp
from jax import lax
from jax.experimental import pallas as pl
from jax.experimental.pallas import tpu as pltpu

F32 = jnp.float32
BF16 = jnp.bfloat16
MESH = pl.DeviceIdType.MESH
ANY = pl.BlockSpec(memory_space=pl.ANY)

D_MODEL = 1024
D_A = 512
D_B = 512
D_IN = 4608
D_FF = 2816
GMLP_BLOCK = 128
CHUNK = 64
A_HEADS = 4
N_LAYERS = 2
N_CHIPS = 4
N_DEVICES = 8
RMS_EPS = 1e-6
LN_EPS = 1e-5
ADAM_LR = 0.001
ADAM_B1 = 0.9
ADAM_B2 = 0.999
ADAM_EPS = 1e-08
ADAM_WD = 0.01
ADAM_STEP = 10

C_U, C_V, C_BG, C_CG, C_HB, C_GA, C_GB = 0, 512, 1024, 1536, 2048, 2560, 3584

V7X_VMEM_LIMIT = 60 * 1024 * 1024
TM_MIX = 256
TM_FFN = 256
TM_EW = 512
TK_WGRAD = 2048
FF_CHUNKS = ((0, 768), (768, 1536), (1536, 2304), (2304, 2816))
GELU_C0 = 0.7978845608028654
GELU_C1 = 0.044715

BIG = {
    "w_in": (1024, 1152),
    "w_branch": (1024, 256),
    "w_out": (256, 1024),
    "w_ffn_up": (1024, 1408),
    "w_ffn_down": (704, 1024),
}
BIG_NAMES = tuple(BIG)


def _params(sem=("arbitrary",), vmem=V7X_VMEM_LIMIT):
    return pltpu.CompilerParams(dimension_semantics=sem, vmem_limit_bytes=vmem)


def _gelu(x):
    x2 = x * x
    t = jnp.tanh(GELU_C0 * x * (1.0 + GELU_C1 * x2))
    return 0.5 * x * (1.0 + t), t


def _gelu_grad(x, t):
    return 0.5 * (1.0 + t) + 0.5 * x * (1.0 - t * t) * GELU_C0 * (1.0 + 3.0 * GELU_C1 * x * x)


def _colsum8(v):
    r, n = v.shape
    return v.reshape(r // 8, 8, n).sum(axis=0)


def _dot(a, b):
    return jnp.dot(a, b, preferred_element_type=F32)


def _dot_nt(a, b):
    return lax.dot_general(a, b, (((1,), (1,)), ((), ())), preferred_element_type=F32)


def _dot_tn(a, b):
    return lax.dot_general(a, b, (((0,), (0,)), ((), ())), preferred_element_type=F32)


def _shift_down(v, carry, n):
    rows = lax.broadcasted_iota(jnp.int32, (8, v.shape[1]), 0)
    out = pltpu.roll(v, n, 0)
    head = out[0:8, :]
    for r in range(n):
        head = jnp.where(rows == r, carry[8 - n + r:8 - n + r + 1, :], head)
    return jnp.concatenate([head, out[8:, :]], axis=0)


def _shift_up(v, carry, n):
    tm = v.shape[0]
    rows = lax.broadcasted_iota(jnp.int32, (8, v.shape[1]), 0)
    out = pltpu.roll(v, tm - n, 0)
    tail = out[tm - 8:tm, :]
    for r in range(n):
        tail = jnp.where(rows == 8 - n + r, carry[r:r + 1, :], tail)
    return jnp.concatenate([out[0:tm - 8, :], tail], axis=0)


def _sigmoid(x):
    return 0.5 * jnp.tanh(0.5 * x) + 0.5


def _start_all(copies):
    for cp in copies:
        cp.start()


def _wait_all(copies):
    for cp in copies:
        cp.wait()


def _load_col_sharded(src, dst, sems, first):
    cs = src.shape[-1]
    return [pltpu.make_async_copy(src.at[k], dst.at[:, k * cs:(k + 1) * cs], sems.at[first + k])
            for k in range(N_CHIPS)]


def _load_row_sharded(src, dst, sems, first):
    rs = src.shape[-2]
    return [pltpu.make_async_copy(src.at[k], dst.at[k * rs:(k + 1) * rs, :], sems.at[first + k])
            for k in range(N_CHIPS)]


def _load_branch(src, dst, sems, first):
    return [pltpu.make_async_copy(src.at[k, pl.ds(m * D_A, D_A), :], dst.at[m, :, k * 256:(k + 1) * 256],
                                  sems.at[first + 2 * k + m])
            for k in range(N_CHIPS) for m in range(2)]


def _row_spec(tm, n, rev=None):
    if rev is None:
        return pl.BlockSpec((tm, n), lambda i: (i, 0))
    return pl.BlockSpec((tm, n), lambda i: (rev - 1 - i, 0))


def _const_spec(shape):
    nd = len(shape)
    return pl.BlockSpec(shape, lambda i: (0,) * nd)


def _mesh_pos():
    return lax.axis_index("x"), lax.axis_index("y"), lax.axis_index("c")


def _other_chips(x, y):
    return [(1 - x, y, 2 * (1 - x) + y), (x, 1 - y, 2 * x + (1 - y)), (1 - x, 1 - y, 2 * (1 - x) + (1 - y))]


def _remote(src, dst, ssem, rsem, to):
    return pltpu.make_async_remote_copy(src_ref=src, dst_ref=dst, send_sem=ssem, recv_sem=rsem, device_id=to,
                                        device_id_type=MESH)


def _half(ref, which, h):
    start = pl.multiple_of(which * h, 8)
    if len(ref.shape) == 2:
        return ref.at[pl.ds(start, h), :]
    return ref.at[:, pl.ds(start, h), :]


class _Stage:
    def __init__(self, ins=(), inouts=(), outs=(), n_sems=0, start=None, mid=None, finish=None, then=None):
        self.ins, self.inouts, self.outs = list(ins), list(inouts), list(outs)
        self.n_sems, self.start, self.mid, self.finish, self.then = n_sems, start, mid, finish, then


def _gather_stage(bufs, then):
    n = len(bufs)

    def copies(io, sem):
        x, y, c = _mesh_pos()
        me = 2 * x + y
        ici, fwd, got = [], [], []
        for w in range(n):
            h = io[w].shape[1] // 2
            for j, (px, py, pk) in enumerate(_other_chips(x, y)):
                mine = _half(io[w].at[me], c, h)
                theirs = _half(io[w].at[pk], c, h)
                ici.append(_remote(mine, mine, sem(12 * w + j), sem(12 * w + 3 + j), (px, py, c)))
                got.append(_remote(theirs, theirs, sem(12 * w + j), sem(12 * w + 3 + j), (px, py, c)))
                fwd.append(_remote(theirs, theirs, sem(12 * w + 6 + j), sem(12 * w + 9 + j), (x, y, 1 - c)))
        return ici, got, fwd

    def start(ins, io, outs, sem):
        _start_all(copies(io, sem)[0])

    def mid(ins, io, outs, sem):
        _, got, fwd = copies(io, sem)
        for g, f in zip(got, fwd):
            g.wait_recv()
            f.start()

    def finish(ins, io, outs, sem):
        x, y, c = _mesh_pos()
        ici, _, fwd = copies(io, sem)
        for w in range(n):
            h = io[w].shape[1] // 2
            for j, (px, py, pk) in enumerate(_other_chips(x, y)):
                other = _half(io[w].at[pk], 1 - c, h)
                _remote(other, other, sem(12 * w + 6 + j), sem(12 * w + 9 + j), (x, y, 1 - c)).wait_recv()
        for cp in ici + fwd:
            cp.wait_send()

    return _Stage(inouts=bufs, n_sems=12 * n, start=start, mid=mid, finish=finish, then=then)


def _pair_send_stage(grad, then):
    h = grad.shape[1] // 2

    def copy(ins, outs, sem):
        x, y, c = _mesh_pos()
        return _remote(_half(ins[0], 1 - c, h), outs[0], sem(0), sem(1), (x, y, 1 - c))

    return _Stage(ins=[grad], outs=[jax.ShapeDtypeStruct((N_CHIPS, h, grad.shape[2]), F32)], n_sems=2,
                  start=lambda ins, io, outs, sem: copy(ins, outs, sem).start(),
                  finish=lambda ins, io, outs, sem: copy(ins, outs, sem).wait(), then=then)


def _chip_send_stage(psum, then):
    def copies(ins, outs, sem):
        x, y, c = _mesh_pos()
        return [_remote(ins[0].at[pk], outs[0].at[j], sem(j), sem(3 + j), (px, py, c))
                for j, (px, py, pk) in enumerate(_other_chips(x, y))]

    return _Stage(ins=[psum], outs=[jax.ShapeDtypeStruct((3,) + psum.shape[1:], BF16)], n_sems=6,
                  start=lambda ins, io, outs, sem: _start_all(copies(ins, outs, sem)),
                  finish=lambda ins, io, outs, sem: _wait_all(copies(ins, outs, sem)), then=then)


def _pair_fill_stage(final, then):
    h = final.shape[0] // 2

    def copy(io, sem):
        x, y, c = _mesh_pos()
        mine = _half(io[0], c, h)
        return _remote(mine, mine, sem(0), sem(1), (x, y, 1 - c))

    return _Stage(inouts=[final], n_sems=2,
                  start=lambda ins, io, outs, sem: copy(io, sem).start(),
                  finish=lambda ins, io, outs, sem: copy(io, sem).wait(), then=then)


def _spread_stage(packed, then):
    def copies(ins, outs, sem):
        x, y, c = _mesh_pos()
        me = 4 * x + 2 * y + c
        cps = []
        for d in range(1, N_DEVICES):
            peer = me ^ d
            cps.append(_remote(ins[0], outs[0].at[me], sem(d), sem(7 + d), (peer // 4, (peer // 2) % 2, peer % 2)))
        return cps, pltpu.make_async_copy(ins[0], outs[0].at[me], sem(0))

    def start(ins, io, outs, sem):
        cps, own = copies(ins, outs, sem)
        own.start()
        _start_all(cps)

    def finish(ins, io, outs, sem):
        cps, own = copies(ins, outs, sem)
        _wait_all(cps)
        own.wait()

    return _Stage(ins=[packed], outs=[jax.ShapeDtypeStruct((N_DEVICES,) + packed.shape, F32)], n_sems=15,
                  start=start, finish=finish, then=then)


def _staged_call(core, *, name, grid, in_specs, out_specs, out_shape, scratch_shapes, args, stages):
    n_in, n_out, n_scr = len(args), len(out_shape), len(scratch_shapes)
    s_args, s_outs, aliases, layout = [], [], {}, []
    n_sems = 0
    for st in stages:
        i0, o0 = len(s_args), len(s_outs)
        s_args += st.ins + st.inouts
        for q in range(len(st.inouts)):
            aliases[n_in + i0 + len(st.ins) + q] = n_out + o0 + q
        s_outs += [jax.ShapeDtypeStruct(a.shape, a.dtype) for a in st.inouts] + st.outs
        layout.append((i0, o0, n_sems))
        n_sems += st.n_sems
    steps = 1
    for g in grid:
        steps *= g

    def body(*refs):
        own_in = refs[:n_in]
        s_in = refs[n_in:n_in + len(s_args)]
        rest = refs[n_in + len(s_args):]
        own_out = rest[:n_out]
        s_out = rest[n_out:n_out + len(s_outs)]
        scr = rest[n_out + len(s_outs):]

        def run(which):
            for st, (i0, o0, s0) in zip(stages, layout):
                fn = getattr(st, which)
                if fn is not None:
                    fn(s_in[i0:i0 + len(st.ins)], s_out[o0:o0 + len(st.inouts)],
                       s_out[o0 + len(st.inouts):o0 + len(st.inouts) + len(st.outs)],
                       lambda k, s0=s0: scr[n_scr].at[s0 + k])

        if not stages:
            core(*own_in, *own_out, *scr[:n_scr])
            return
        step = 0
        for d, g in enumerate(grid):
            step = step * g + pl.program_id(d)
        if steps == 1:
            run("start")
            core(*own_in, *own_out, *scr[:n_scr])
            run("mid")
            run("finish")
            return
        pl.when(step == 0)(lambda: run("start"))
        core(*own_in, *own_out, *scr[:n_scr])
        pl.when(step == (3 * steps) // 4)(lambda: run("mid"))
        pl.when(step == steps - 1)(lambda: run("finish"))

    sem = ("arbitrary",) * len(grid) if stages else ("parallel",) * max(len(grid) - 1, 0) + ("arbitrary",) * min(len(grid), 1)
    res = pl.pallas_call(
        body, name=name, grid=grid,
        in_specs=list(in_specs) + [ANY] * len(s_args),
        out_specs=list(out_specs) + [ANY] * len(s_outs),
        out_shape=list(out_shape) + s_outs,
        input_output_aliases=aliases,
        scratch_shapes=list(scratch_shapes) + ([pltpu.SemaphoreType.DMA((n_sems,))] if stages else []),
        compiler_params=_params(sem) if grid else pltpu.CompilerParams(vmem_limit_bytes=V7X_VMEM_LIMIT),
    )(*args, *s_args)
    return list(res[:n_out]), list(res[n_out:])


class _Pipe:
    def __init__(self):
        self.ready = []
        self.flushes = 0

    def add(self, stage):
        self.ready.append(stage)

    def carry(self, call):
        stages, self.ready = self.ready, []
        own, outs = call(stages)
        k = 0
        for st in stages:
            n = len(st.inouts) + len(st.outs)
            st.then(*outs[k:k + n])
            k += n
        return own

    def flush(self):
        while self.ready:
            self.flushes += 1
            self.carry(lambda stages: _staged_call(
                lambda *refs: None, name=f"comm_tail_{self.flushes}", grid=(), in_specs=[], out_specs=[], out_shape=[],
                scratch_shapes=[], args=[], stages=stages))


def _mixer_fwd(layer, x, g1, bgate, lng, lnb, wm, bsf, wsc, win_g, wb_g, wout_g, stages):
    t_len = x.shape[0]
    tm = min(TM_MIX, t_len)
    nt = t_len // tm
    nb = tm // GMLP_BLOCK

    def core(x_ref, g1_ref, bgate_ref, lng_ref, lnb_ref, wm_ref, bsf_ref, wsc_ref, win_hbm, wb_hbm, wout_hbm,
             z_ref, ya_ref, yb_ref, q_ref, a_ref, b_ref, mg_ref, h_ref, x2_ref,
             win_v, wb_v, wout_v, carry, vn_s, f_s, sems):
        i = pl.program_id(0)

        @pl.when(i == 0)
        def _():
            cps = (_load_col_sharded(win_hbm, win_v, sems, 0) + _load_branch(wb_hbm, wb_v, sems, 4)
                   + _load_row_sharded(wout_hbm, wout_v, sems, 12))
            _start_all(cps)
            carry[...] = jnp.zeros_like(carry)
            _wait_all(cps)

        xv = x_ref[...]
        r = lax.rsqrt(jnp.mean(xv * xv, axis=-1, keepdims=True) + RMS_EPS)
        h_ref[...] = (xv * r * g1_ref[...]).astype(BF16)

        def zcols(c0, c1):
            zc = _dot(h_ref[...], win_v[:, c0:c1])
            z_ref[:, c0:c1] = zc.astype(BF16)
            return zc

        vg, _ = _gelu(zcols(C_V, C_V + D_A))
        mu = jnp.mean(vg, axis=-1, keepdims=True)
        vc = vg - mu
        rstd = lax.rsqrt(jnp.mean(vc * vc, axis=-1, keepdims=True) + LN_EPS)
        vn_s[...] = (vc * rstd * lng_ref[...] + lnb_ref[...]).astype(BF16)
        for hd in range(A_HEADS):
            cols = slice(hd * 128, (hd + 1) * 128)
            vcat = jnp.concatenate([vn_s[b * 128:(b + 1) * 128, cols] for b in range(nb)], axis=1)
            fcat = _dot(wm_ref[hd], vcat)
            for b in range(nb):
                f_s[b * 128:(b + 1) * 128, cols] = fcat[:, b * 128:(b + 1) * 128]
        ug, _ = _gelu(zcols(C_U, C_U + D_A))
        bias = jnp.concatenate([bsf_ref[...]] * nb, axis=0)
        ya_ref[...] = (ug * (f_s[...] + bias)).astype(BF16)

        p = zcols(C_CG, C_CG + D_B) * zcols(C_HB, C_HB + D_B)
        cr = carry[...]
        q = wsc_ref[0:1, :] * _shift_down(p, cr, 2) + wsc_ref[1:2, :] * _shift_down(p, cr, 1) + wsc_ref[2:3, :] * p
        carry[...] = p[tm - 8:tm, :]
        q_ref[...] = q.astype(BF16)
        yb_ref[...] = (zcols(C_BG, C_BG + D_B) * q).astype(BF16)

        av = _dot(ya_ref[...], wb_v[0])
        a_ref[...] = av.astype(BF16)
        mg = _sigmoid(zcols(C_GA, C_GA + D_MODEL) + bgate_ref[:, 0:D_MODEL]) * av
        bv = _dot(yb_ref[...], wb_v[1])
        b_ref[...] = bv.astype(BF16)
        mg = mg + _sigmoid(zcols(C_GB, C_GB + D_MODEL) + bgate_ref[:, D_MODEL:2 * D_MODEL]) * bv
        mg_ref[...] = mg.astype(BF16)
        x2_ref[...] = x_ref[...] + _dot(mg_ref[...], wout_v[...])

    outs = [
        jax.ShapeDtypeStruct((t_len, D_IN), BF16),
        jax.ShapeDtypeStruct((t_len, D_A), BF16),
        jax.ShapeDtypeStruct((t_len, D_B), BF16),
        jax.ShapeDtypeStruct((t_len, D_B), BF16),
        jax.ShapeDtypeStruct((t_len, D_MODEL), BF16),
        jax.ShapeDtypeStruct((t_len, D_MODEL), BF16),
        jax.ShapeDtypeStruct((t_len, D_MODEL), BF16),
        jax.ShapeDtypeStruct((t_len, D_MODEL), BF16),
        jax.ShapeDtypeStruct((t_len, D_MODEL), F32),
    ]
    return _staged_call(
        core, name=f"mixer_fwd_l{layer}", grid=(nt,),
        in_specs=[_row_spec(tm, D_MODEL), _const_spec((1, D_MODEL)), _const_spec((1, 2 * D_MODEL)),
                  _const_spec((1, D_A)), _const_spec((1, D_A)), _const_spec((A_HEADS, 128, 128)),
                  _const_spec((128, D_A)), _const_spec((8, D_B)), ANY, ANY, ANY],
        out_specs=[_row_spec(tm, o.shape[1]) for o in outs],
        out_shape=outs,
        scratch_shapes=[pltpu.VMEM((D_MODEL, D_IN), BF16), pltpu.VMEM((2, D_A, D_MODEL), BF16),
                        pltpu.VMEM((D_MODEL, D_MODEL), BF16), pltpu.VMEM((8, D_B), F32),
                        pltpu.VMEM((tm, D_A), BF16), pltpu.VMEM((tm, D_A), F32), pltpu.SemaphoreType.DMA((16,))],
        args=[x, g1, bgate, lng, lnb, wm, bsf, wsc, win_g, wb_g, wout_g], stages=stages)


def _ffn_fwd(layer, x2, g2, wfc, bfc, wup_g, wdown_g, stages):
    t_len = x2.shape[0]
    tm = min(TM_FFN, t_len)
    nt = t_len // tm

    def core(x_ref, g2_ref, wfc_ref, bfc_ref, wup_hbm, wdown_hbm, up_ref, silu_ref, dsilu_ref, act_ref, h_ref, x3_ref,
             wup_v, wdown_v, carry, sems):
        i = pl.program_id(0)

        @pl.when(i == 0)
        def _():
            cps = _load_col_sharded(wup_hbm, wup_v, sems, 0) + _load_row_sharded(wdown_hbm, wdown_v, sems, 4)
            _start_all(cps)
            carry[...] = jnp.zeros_like(carry)
            _wait_all(cps)

        xv = x_ref[...]
        r = lax.rsqrt(jnp.mean(xv * xv, axis=-1, keepdims=True) + RMS_EPS)
        h_ref[...] = (xv * r * g2_ref[...]).astype(BF16)
        acc = xv
        for c0, c1 in FF_CHUNKS:
            gate = _dot(h_ref[...], wup_v[:, c0:c1])
            up_ref[:, c0:c1] = gate.astype(BF16)
            cr = carry[:, c0:c1]
            gc = (wfc_ref[0:1, c0:c1] * _shift_down(gate, cr, 2) + wfc_ref[1:2, c0:c1] * _shift_down(gate, cr, 1)
                  + wfc_ref[2:3, c0:c1] * gate + bfc_ref[:, c0:c1])
            carry[:, c0:c1] = gate[tm - 8:tm, :]
            sg = _sigmoid(gc)
            silu = gc * sg
            silu_ref[:, c0:c1] = silu.astype(BF16)
            dsilu_ref[:, c0:c1] = (sg + silu * (1.0 - sg)).astype(BF16)
            val = _dot(h_ref[...], wup_v[:, D_FF + c0:D_FF + c1])
            up_ref[:, D_FF + c0:D_FF + c1] = val.astype(BF16)
            act_ref[:, c0:c1] = (silu * val).astype(BF16)
            acc = acc + _dot(act_ref[:, c0:c1], wdown_v[c0:c1, :])
        x3_ref[...] = acc

    outs = [
        jax.ShapeDtypeStruct((t_len, 2 * D_FF), BF16),
        jax.ShapeDtypeStruct((t_len, D_FF), BF16),
        jax.ShapeDtypeStruct((t_len, D_FF), BF16),
        jax.ShapeDtypeStruct((t_len, D_FF), BF16),
        jax.ShapeDtypeStruct((t_len, D_MODEL), BF16),
        jax.ShapeDtypeStruct((t_len, D_MODEL), F32),
    ]
    return _staged_call(
        core, name=f"ffn_fwd_l{layer}", grid=(nt,),
        in_specs=[_row_spec(tm, D_MODEL), _const_spec((1, D_MODEL)), _const_spec((8, D_FF)), _const_spec((1, D_FF)), ANY, ANY],
        out_specs=[_row_spec(tm, o.shape[1]) for o in outs],
        out_shape=outs,
        scratch_shapes=[pltpu.VMEM((D_MODEL, 2 * D_FF), BF16), pltpu.VMEM((D_FF, D_MODEL), BF16),
                        pltpu.VMEM((8, D_FF), F32), pltpu.SemaphoreType.DMA((8,))],
        args=[x2, g2, wfc, bfc, wup_g, wdown_g], stages=stages)


def _loss_head(x3, target, gf):
    t_len = x3.shape[0]
    tm = min(TM_EW, t_len)
    nt = t_len // tm

    def body(x_ref, t_ref, gf_ref, dx_ref, dgf_ref, loss_ref):
        i = pl.program_id(0)

        @pl.when(i == 0)
        def _():
            dgf_ref[...] = jnp.zeros_like(dgf_ref)
            loss_ref[...] = jnp.zeros_like(loss_ref)

        xv = x_ref[...]
        r = lax.rsqrt(jnp.mean(xv * xv, axis=-1, keepdims=True) + RMS_EPS)
        xh = xv * r
        err = xh * gf_ref[...] - t_ref[...]
        loss_ref[...] += _colsum8(err * err)
        dy = err * (1.0 / D_MODEL)
        dgf_ref[...] += _colsum8(dy * xh)
        dxh = dy * gf_ref[...]
        dx_ref[...] = r * (dxh - xh * jnp.mean(dxh * xh, axis=-1, keepdims=True))

    return pl.pallas_call(
        body, name="loss_head", grid=(nt,),
        in_specs=[_row_spec(tm, D_MODEL), _row_spec(tm, D_MODEL), _const_spec((1, D_MODEL))],
        out_specs=[_row_spec(tm, D_MODEL), _const_spec((8, D_MODEL)), _const_spec((8, D_MODEL))],
        out_shape=[jax.ShapeDtypeStruct((t_len, D_MODEL), F32), jax.ShapeDtypeStruct((8, D_MODEL), F32),
                   jax.ShapeDtypeStruct((8, D_MODEL), F32)],
        compiler_params=_params(),
    )(x3, target, gf)


def _ffn_bwd(layer, dx3, x2, up, silu, dsilu, g2, wfc, wup_g, wdown_g, stages):
    t_len = x2.shape[0]
    tm = min(TM_FFN, t_len)
    nt = t_len // tm

    def core(dx3_ref, x_ref, up_ref, silu_ref, dsilu_ref, g2_ref, wfc_ref, wup_hbm, wdown_hbm,
             dx2_ref, dup_ref, dx3b_ref, dg2_ref, dbfc_ref, dwfc_ref,
             wup_v, wdown_v, carry, sems):
        i = pl.program_id(0)

        @pl.when(i == 0)
        def _():
            cps = _load_col_sharded(wup_hbm, wup_v, sems, 0) + _load_row_sharded(wdown_hbm, wdown_v, sems, 4)
            _start_all(cps)
            carry[...] = jnp.zeros_like(carry)
            dg2_ref[...] = jnp.zeros_like(dg2_ref)
            dbfc_ref[...] = jnp.zeros_like(dbfc_ref)
            dwfc_ref[...] = jnp.zeros_like(dwfc_ref)
            _wait_all(cps)

        dx3b_ref[...] = dx3_ref[...].astype(BF16)
        dh = jnp.zeros((tm, D_MODEL), F32)
        for c0, c1 in FF_CHUNKS:
            v0, v1 = D_FF + c0, D_FF + c1
            da = _dot_nt(dx3b_ref[...], wdown_v[c0:c1, :])
            dup_ref[:, v0:v1] = (da * silu_ref[:, c0:c1].astype(F32)).astype(BF16)
            dgc = da * up_ref[:, v0:v1].astype(F32) * dsilu_ref[:, c0:c1].astype(F32)
            cr = carry[:, c0:c1]
            dgc1 = _shift_up(dgc, cr, 1)
            dgc2 = _shift_up(dgc, cr, 2)
            carry[:, c0:c1] = dgc[0:8, :]
            gate = up_ref[:, c0:c1].astype(F32)
            dbfc_ref[:, c0:c1] += _colsum8(dgc)
            dwfc_ref[0, :, c0:c1] += _colsum8(dgc2 * gate)
            dwfc_ref[1, :, c0:c1] += _colsum8(dgc1 * gate)
            dwfc_ref[2, :, c0:c1] += _colsum8(dgc * gate)
            dgate = wfc_ref[2:3, c0:c1] * dgc + wfc_ref[1:2, c0:c1] * dgc1 + wfc_ref[0:1, c0:c1] * dgc2
            dup_ref[:, c0:c1] = dgate.astype(BF16)
            dh = dh + _dot_nt(dup_ref[:, c0:c1], wup_v[:, c0:c1]) + _dot_nt(dup_ref[:, v0:v1], wup_v[:, v0:v1])
        xv = x_ref[...]
        r = lax.rsqrt(jnp.mean(xv * xv, axis=-1, keepdims=True) + RMS_EPS)
        xh = xv * r
        dg2_ref[...] += _colsum8(dh * xh)
        dxh = dh * g2_ref[...]
        dx2_ref[...] = dx3_ref[...] + r * (dxh - xh * jnp.mean(dxh * xh, axis=-1, keepdims=True))

    outs = [
        jax.ShapeDtypeStruct((t_len, D_MODEL), F32),
        jax.ShapeDtypeStruct((t_len, 2 * D_FF), BF16),
        jax.ShapeDtypeStruct((t_len, D_MODEL), BF16),
        jax.ShapeDtypeStruct((8, D_MODEL), F32),
        jax.ShapeDtypeStruct((8, D_FF), F32),
        jax.ShapeDtypeStruct((3, 8, D_FF), F32),
    ]
    return _staged_call(
        core, name=f"ffn_bwd_l{layer}", grid=(nt,),
        in_specs=[_row_spec(tm, D_MODEL, nt), _row_spec(tm, D_MODEL, nt), _row_spec(tm, 2 * D_FF, nt),
                  _row_spec(tm, D_FF, nt), _row_spec(tm, D_FF, nt), _const_spec((1, D_MODEL)), _const_spec((8, D_FF)),
                  ANY, ANY],
        out_specs=[_row_spec(tm, D_MODEL, nt), _row_spec(tm, 2 * D_FF, nt), _row_spec(tm, D_MODEL, nt),
                   _const_spec((8, D_MODEL)), _const_spec((8, D_FF)), _const_spec((3, 8, D_FF))],
        out_shape=outs,
        scratch_shapes=[pltpu.VMEM((D_MODEL, 2 * D_FF), BF16), pltpu.VMEM((D_FF, D_MODEL), BF16),
                        pltpu.VMEM((8, D_FF), F32), pltpu.SemaphoreType.DMA((8,))],
        args=[dx3, x2, up, silu, dsilu, g2, wfc, wup_g, wdown_g], stages=stages)


def _mixer_bwd(layer, dx2, x, z, qs, av, bv, g1, bgate, lng, lnb, wm, wmt, bsf, wsc, win_g, wb_g, wout_g, stages):
    t_len = x.shape[0]
    tm = min(TM_MIX, t_len)
    nt = t_len // tm
    nb = tm // GMLP_BLOCK

    def core(dx2_ref, x_ref, z_ref, q_ref, a_ref, b_ref, g1_ref, bgate_ref, lng_ref, lnb_ref,
             wm_ref, wmt_ref, bsf_ref, wsc_ref, win_hbm, wb_hbm, wout_hbm,
             dx_ref, dz_ref, da_ref, db_ref, dx2b_ref, dg1_ref, dbgate_ref, dlng_ref, dlnb_ref, dwm_ref, dbsf_ref, dwsc_ref,
             win_v, wb_v, wout_v, carry, vn_s, f_s, df_s, dvn_s, sems):
        i = pl.program_id(0)

        @pl.when(i == 0)
        def _():
            cps = (_load_col_sharded(win_hbm, win_v, sems, 0) + _load_branch(wb_hbm, wb_v, sems, 4)
                   + _load_row_sharded(wout_hbm, wout_v, sems, 12))
            _start_all(cps)
            carry[...] = jnp.zeros_like(carry)
            for ref in (dg1_ref, dbgate_ref, dlng_ref, dlnb_ref, dwm_ref, dbsf_ref, dwsc_ref):
                ref[...] = jnp.zeros_like(ref)
            _wait_all(cps)

        def zc(c0, n):
            return z_ref[:, c0:c0 + n].astype(F32)

        dx2b_ref[...] = dx2_ref[...].astype(BF16)
        dm = _dot_nt(dx2b_ref[...], wout_v[...])
        def dz_cols(c0, n, val):
            dz_ref[:, c0:c0 + n] = val.astype(BF16)
            return _dot_nt(dz_ref[:, c0:c0 + n], win_v[:, c0:c0 + n])

        sa = _sigmoid(zc(C_GA, D_MODEL) + bgate_ref[:, 0:D_MODEL])
        da_ref[...] = (dm * sa).astype(BF16)
        dga = dm * a_ref[...].astype(F32) * sa * (1.0 - sa)
        dh = dz_cols(C_GA, D_MODEL, dga)
        dbgate_ref[:, 0:D_MODEL] += _colsum8(dga)
        dya = _dot_nt(da_ref[...], wb_v[0])
        sb = _sigmoid(zc(C_GB, D_MODEL) + bgate_ref[:, D_MODEL:2 * D_MODEL])
        db_ref[...] = (dm * sb).astype(BF16)
        dgb = dm * b_ref[...].astype(F32) * sb * (1.0 - sb)
        dh = dh + dz_cols(C_GB, D_MODEL, dgb)
        dbgate_ref[:, D_MODEL:2 * D_MODEL] += _colsum8(dgb)
        dyb = _dot_nt(db_ref[...], wb_v[1])

        v = zc(C_V, D_A)
        vg, tv = _gelu(v)
        mu = jnp.mean(vg, axis=-1, keepdims=True)
        vc = vg - mu
        rstd = lax.rsqrt(jnp.mean(vc * vc, axis=-1, keepdims=True) + LN_EPS)
        xh = vc * rstd
        vn_s[...] = (xh * lng_ref[...] + lnb_ref[...]).astype(BF16)
        u = zc(C_U, D_A)
        ug, tu = _gelu(u)
        df = dya * ug
        df_s[...] = df.astype(BF16)
        dbsf_acc = df[0:128, :]
        for b in range(1, nb):
            dbsf_acc = dbsf_acc + df[b * 128:(b + 1) * 128, :]
        dbsf_ref[...] += dbsf_acc
        for hd in range(A_HEADS):
            cols = slice(hd * 128, (hd + 1) * 128)
            vcat = jnp.concatenate([vn_s[b * 128:(b + 1) * 128, cols] for b in range(nb)], axis=1)
            dcat = jnp.concatenate([df_s[b * 128:(b + 1) * 128, cols] for b in range(nb)], axis=1)
            fcat = _dot(wm_ref[hd], vcat)
            gcat = _dot(wmt_ref[hd], dcat)
            dwm_ref[hd] += _dot_nt(dcat, vcat)
            for b in range(nb):
                f_s[b * 128:(b + 1) * 128, cols] = fcat[:, b * 128:(b + 1) * 128]
                dvn_s[b * 128:(b + 1) * 128, cols] = gcat[:, b * 128:(b + 1) * 128]
        bias = jnp.concatenate([bsf_ref[...]] * nb, axis=0)
        dh = dh + dz_cols(C_U, D_A, dya * (f_s[...] + bias) * _gelu_grad(u, tu))
        dvn = dvn_s[...]
        dlng_ref[...] += _colsum8(dvn * xh)
        dlnb_ref[...] += _colsum8(dvn)
        dxh = dvn * lng_ref[...]
        dvg = rstd * (dxh - jnp.mean(dxh, axis=-1, keepdims=True) - xh * jnp.mean(dxh * xh, axis=-1, keepdims=True))
        dh = dh + dz_cols(C_V, D_A, dvg * _gelu_grad(v, tv))

        cg = zc(C_CG, D_B)
        hbv = zc(C_HB, D_B)
        p = cg * hbv
        dh = dh + dz_cols(C_BG, D_B, dyb * q_ref[...].astype(F32))
        dq = dyb * zc(C_BG, D_B)
        cr = carry[...]
        dq1 = _shift_up(dq, cr, 1)
        dq2 = _shift_up(dq, cr, 2)
        carry[...] = dq[0:8, :]
        dwsc_ref[0] += _colsum8(dq2 * p)
        dwsc_ref[1] += _colsum8(dq1 * p)
        dwsc_ref[2] += _colsum8(dq * p)
        dp = wsc_ref[2:3, :] * dq + wsc_ref[1:2, :] * dq1 + wsc_ref[0:1, :] * dq2
        dh = dh + dz_cols(C_CG, D_B, dp * hbv)
        dh = dh + dz_cols(C_HB, D_B, dp * cg)

        xv = x_ref[...]
        r = lax.rsqrt(jnp.mean(xv * xv, axis=-1, keepdims=True) + RMS_EPS)
        xn = xv * r
        dg1_ref[...] += _colsum8(dh * xn)
        dxn = dh * g1_ref[...]
        dx_ref[...] = dx2_ref[...] + r * (dxn - xn * jnp.mean(dxn * xn, axis=-1, keepdims=True))

    outs = [
        jax.ShapeDtypeStruct((t_len, D_MODEL), F32),
        jax.ShapeDtypeStruct((t_len, D_IN), BF16),
        jax.ShapeDtypeStruct((t_len, D_MODEL), BF16),
        jax.ShapeDtypeStruct((t_len, D_MODEL), BF16),
        jax.ShapeDtypeStruct((t_len, D_MODEL), BF16),
        jax.ShapeDtypeStruct((8, D_MODEL), F32),
        jax.ShapeDtypeStruct((8, 2 * D_MODEL), F32),
        jax.ShapeDtypeStruct((8, D_A), F32),
        jax.ShapeDtypeStruct((8, D_A), F32),
        jax.ShapeDtypeStruct((A_HEADS, 128, 128), F32),
        jax.ShapeDtypeStruct((128, D_A), F32),
        jax.ShapeDtypeStruct((3, 8, D_B), F32),
    ]

    return _staged_call(
        core, name=f"mixer_bwd_l{layer}", grid=(nt,),
        in_specs=[_row_spec(tm, D_MODEL, nt), _row_spec(tm, D_MODEL, nt), _row_spec(tm, D_IN, nt),
                  _row_spec(tm, D_B, nt), _row_spec(tm, D_MODEL, nt), _row_spec(tm, D_MODEL, nt),
                  _const_spec((1, D_MODEL)), _const_spec((1, 2 * D_MODEL)), _const_spec((1, D_A)), _const_spec((1, D_A)),
                  _const_spec((A_HEADS, 128, 128)), _const_spec((A_HEADS, 128, 128)), _const_spec((128, D_A)),
                  _const_spec((8, D_B)), ANY, ANY, ANY],
        out_specs=[_row_spec(tm, D_MODEL, nt), _row_spec(tm, D_IN, nt), _row_spec(tm, D_MODEL, nt),
                   _row_spec(tm, D_MODEL, nt), _row_spec(tm, D_MODEL, nt),
                   _const_spec((8, D_MODEL)), _const_spec((8, 2 * D_MODEL)), _const_spec((8, D_A)), _const_spec((8, D_A)),
                   _const_spec((A_HEADS, 128, 128)), _const_spec((128, D_A)), _const_spec((3, 8, D_B))],
        out_shape=outs,
        scratch_shapes=[pltpu.VMEM((D_MODEL, D_IN), BF16), pltpu.VMEM((2, D_A, D_MODEL), BF16),
                        pltpu.VMEM((D_MODEL, D_MODEL), BF16), pltpu.VMEM((8, D_B), F32),
                        pltpu.VMEM((tm, D_A), BF16), pltpu.VMEM((tm, D_A), F32), pltpu.VMEM((tm, D_A), BF16),
                        pltpu.VMEM((tm, D_A), F32), pltpu.SemaphoreType.DMA((16,))],
        args=[dx2, x, z, qs, av, bv, g1, bgate, lng, lnb, wm, wmt, bsf, wsc, win_g, wb_g, wout_g], stages=stages)


def _wgrad(name, layer, a, b, rows, cols, row_blk, col_blk, stages):
    t_len, m = a.shape
    n = b.shape[1]
    tk = min(TK_WGRAD, t_len)
    col_sharded = n == N_CHIPS * cols
    grid = (m // row_blk, n // col_blk, t_len // tk)
    per_shard_c = cols // col_blk

    if col_sharded:
        out_shape = (N_CHIPS, rows, cols)
        out_spec = pl.BlockSpec((None, row_blk, col_blk), lambda i, j, k: (j // per_shard_c, i, j % per_shard_c))
    else:
        out_shape = (N_CHIPS * rows, cols)
        out_spec = pl.BlockSpec((row_blk, col_blk), lambda i, j, k: (i, j))

    def core(a_ref, b_ref, o_ref):
        @pl.when(pl.program_id(2) == 0)
        def _():
            o_ref[...] = jnp.zeros_like(o_ref)

        o_ref[...] += _dot_tn(a_ref[...], b_ref[...])

    own, outs = _staged_call(
        core, name=f"wgrad_{name}_l{layer}", grid=grid,
        in_specs=[pl.BlockSpec((tk, row_blk), lambda i, j, k: (k, i)), pl.BlockSpec((tk, col_blk), lambda i, j, k: (k, j))],
        out_specs=[out_spec], out_shape=[jax.ShapeDtypeStruct(out_shape, F32)], scratch_shapes=[],
        args=[a, b], stages=stages)
    return [own[0].reshape(N_CHIPS, rows, cols)], outs


def _wgrad_branch(layer, ya, da, yb, db, stages):
    t_len = ya.shape[0]
    tk = min(TK_WGRAD, t_len)

    def core(ya_ref, da_ref, yb_ref, db_ref, o_ref):
        @pl.when(pl.program_id(1) == 0)
        def _():
            o_ref[...] = jnp.zeros_like(o_ref)

        o_ref[0:D_A, :] += _dot_tn(ya_ref[...], da_ref[...])
        o_ref[D_A:2 * D_A, :] += _dot_tn(yb_ref[...], db_ref[...])

    a_spec = pl.BlockSpec((tk, D_A), lambda j, k: (k, 0))
    d_spec = pl.BlockSpec((tk, 256), lambda j, k: (k, j))
    return _staged_call(
        core, name=f"wgrad_w_branch_l{layer}", grid=(N_CHIPS, t_len // tk),
        in_specs=[a_spec, d_spec, a_spec, d_spec],
        out_specs=[pl.BlockSpec((None, 2 * D_A, 256), lambda j, k: (j, 0, 0))],
        out_shape=[jax.ShapeDtypeStruct((N_CHIPS, 2 * D_A, 256), F32)], scratch_shapes=[],
        args=[ya, da, yb, db], stages=stages)


def _all_reduce_small(name, packed):
    rows = packed.shape[0]

    def body(src_ref, out_ref, slots, send, recv):
        x, y, c = _mesh_pos()
        me = 4 * x + 2 * y + c
        cps = []
        for d in range(1, N_DEVICES):
            peer = me ^ d
            cps.append(_remote(src_ref, slots.at[me], send.at[d - 1], recv.at[d - 1],
                               (peer // 4, (peer // 2) % 2, peer % 2)))
        _start_all(cps)
        slots[me] = src_ref[...]
        _wait_all(cps)
        acc = slots[0]
        for d in range(1, N_DEVICES):
            acc = acc + slots[d]
        out_ref[...] = acc

    return pl.pallas_call(
        body, name=f"all_reduce_{name}",
        in_specs=[pl.BlockSpec(memory_space=pltpu.VMEM)], out_specs=pl.BlockSpec(memory_space=pltpu.VMEM),
        out_shape=jax.ShapeDtypeStruct(packed.shape, F32),
        scratch_shapes=[pltpu.VMEM((N_DEVICES, rows, 128), F32), pltpu.SemaphoreType.DMA((7,)),
                        pltpu.SemaphoreType.DMA((7,))],
        compiler_params=pltpu.CompilerParams(vmem_limit_bytes=V7X_VMEM_LIMIT),
    )(packed)


def _flat_blk(rows, cols):
    blk = rows
    while blk * cols * 4 > 2 * 1024 * 1024 and blk % 16 == 0:
        blk //= 2
    return blk


def _cast_into_slot(name, layer, w, chip):
    _, rows, cols = w.shape
    blk = _flat_blk(rows, cols)

    def body(chip_ref, w_ref, o_ref):
        o_ref[...] = w_ref[...].astype(BF16)

    return pl.pallas_call(
        body, name=f"cast_{name}_l{layer}",
        grid_spec=pltpu.PrefetchScalarGridSpec(
            num_scalar_prefetch=1, grid=(rows // blk,),
            in_specs=[pl.BlockSpec((None, blk, cols), lambda i, chip_ref: (layer, i, 0))],
            out_specs=pl.BlockSpec((None, blk, cols), lambda i, chip_ref: (chip_ref[0], i, 0))),
        out_shape=jax.ShapeDtypeStruct((N_CHIPS, rows, cols), BF16),
        compiler_params=_params(("parallel",)),
    )(chip, w)


def _pair_sum(name, grad, other, core):
    _, h, cols = other.shape
    blk = _flat_blk(h, cols)
    nblk = h // blk

    def body(core_ref, g_ref, o_ref, s_ref):
        s_ref[...] = (g_ref[...] + o_ref[...]).astype(BF16)

    spec = pl.BlockSpec((None, blk, cols), lambda k, i, core_ref: (k, i, 0))
    return pl.pallas_call(
        body, name=f"pair_sum_{name}",
        grid_spec=pltpu.PrefetchScalarGridSpec(
            num_scalar_prefetch=1, grid=(N_CHIPS, nblk),
            in_specs=[pl.BlockSpec((None, blk, cols), lambda k, i, core_ref: (k, core_ref[0] * nblk + i, 0)), spec],
            out_specs=spec),
        out_shape=jax.ShapeDtypeStruct((N_CHIPS, h, cols), BF16),
        compiler_params=_params(("parallel", "parallel")),
    )(core, grad, other)


def _chip_sum(name, grad, other, got, pos):
    _, rows, cols = grad.shape
    h = rows // 2
    blk = _flat_blk(h, cols)
    nblk = h // blk

    def body(pos_ref, g_ref, o_ref, r_ref, f_ref):
        f_ref[...] = (((g_ref[...] + o_ref[...]) + r_ref[0].astype(F32)) + r_ref[1].astype(F32)) + r_ref[2].astype(F32)

    return pl.pallas_call(
        body, name=f"chip_sum_{name}",
        grid_spec=pltpu.PrefetchScalarGridSpec(
            num_scalar_prefetch=1, grid=(nblk,),
            in_specs=[pl.BlockSpec((None, blk, cols), lambda i, pos_ref: (pos_ref[0], pos_ref[1] * nblk + i, 0)),
                      pl.BlockSpec((None, blk, cols), lambda i, pos_ref: (pos_ref[0], i, 0)),
                      pl.BlockSpec((3, blk, cols), lambda i, pos_ref: (0, i, 0))],
            out_specs=pl.BlockSpec((blk, cols), lambda i, pos_ref: (pos_ref[1] * nblk + i, 0))),
        out_shape=jax.ShapeDtypeStruct((rows, cols), F32),
        compiler_params=_params(("parallel",)),
    )(pos, grad, other, got)


def _sum_slots(name, slots):
    _, rows, _ = slots.shape

    def body(s_ref, o_ref):
        acc = s_ref[0]
        for d in range(1, N_DEVICES):
            acc = acc + s_ref[d]
        o_ref[...] = acc

    return pl.pallas_call(
        body, name=f"sum_slots_{name}", grid=(1,),
        in_specs=[pl.BlockSpec((N_DEVICES, rows, 128), lambda i: (0, 0, 0))],
        out_specs=pl.BlockSpec((rows, 128), lambda i: (0, 0)),
        out_shape=jax.ShapeDtypeStruct((rows, 128), F32),
        compiler_params=_params(),
    )(slots)


def _adamw_math(w, g, m, v):
    m2 = ADAM_B1 * m + (1.0 - ADAM_B1) * g
    v2 = ADAM_B2 * v + (1.0 - ADAM_B2) * (g * g)
    m_hat = m2 / (1.0 - ADAM_B1 ** ADAM_STEP)
    v_hat = v2 / (1.0 - ADAM_B2 ** ADAM_STEP)
    delta = -ADAM_LR * (m_hat / (jnp.sqrt(v_hat) + ADAM_EPS) + ADAM_WD * w)
    return delta, m2, v2


def _adamw_big(name, w, g0, g1, m, v):
    _, rows, cols = w.shape
    blk = _flat_blk(rows, cols) // 2

    def body(w_ref, g0_ref, g1_ref, m_ref, v_ref, g_ref, d_ref, m2_ref, v2_ref):
        g = jnp.where(pl.program_id(0) == 0, g0_ref[...], g1_ref[...])
        d, m2, v2 = _adamw_math(w_ref[...], g, m_ref[...], v_ref[...])
        g_ref[...] = g
        d_ref[...] = d
        m2_ref[...] = m2
        v2_ref[...] = v2

    spec = pl.BlockSpec((None, blk, cols), lambda la, i: (la, i, 0))
    return pl.pallas_call(
        body, name=f"adamw_{name}", grid=(N_LAYERS, rows // blk),
        in_specs=[spec, pl.BlockSpec((blk, cols), lambda la, i: (i * (1 - la), 0)),
                  pl.BlockSpec((blk, cols), lambda la, i: (i * la, 0)), spec, spec],
        out_specs=[spec] * 4,
        out_shape=[jax.ShapeDtypeStruct(w.shape, F32)] * 4,
        compiler_params=_params(("parallel", "parallel")),
    )(w, g0, g1, m, v)


def _adamw(name, w, g, m, v):
    rows, cols = w.shape
    blk = _flat_blk(rows, cols)

    def body(w_ref, g_ref, m_ref, v_ref, d_ref, m2_ref, v2_ref):
        d, m2, v2 = _adamw_math(w_ref[...], g_ref[...], m_ref[...], v_ref[...])
        d_ref[...] = d
        m2_ref[...] = m2
        v2_ref[...] = v2

    spec = pl.BlockSpec((blk, cols), lambda i: (i, 0))
    return pl.pallas_call(
        body, name=f"adamw_{name}", grid=(rows // blk,),
        in_specs=[spec] * 4, out_specs=[spec] * 3,
        out_shape=[jax.ShapeDtypeStruct((rows, cols), F32)] * 3,
        compiler_params=_params(("parallel",)),
    )(w, g, m, v)


SMALL = ("norm1_g", "b_gate", "gmlp_ln_g", "gmlp_ln_b", "w_spatial", "b_spatial", "w_shortconv", "norm2_g",
         "w_ffn_conv", "b_ffn_conv", "final_g")
ALL_WEIGHTS = ("norm1_g", "w_in", "b_gate", "gmlp_ln_g", "gmlp_ln_b", "w_spatial", "b_spatial", "w_shortconv",
               "w_branch", "w_out", "norm2_g", "w_ffn_up", "w_ffn_conv", "b_ffn_conv", "w_ffn_down", "final_g")


def _pack(arrays):
    flat = jnp.concatenate([a.reshape(-1) for a in arrays])
    n = flat.shape[0]
    rows = -(-n // 1024) * 8
    return jnp.pad(flat, (0, rows * 128 - n)).reshape(rows, 128)


def _unpack(packed, like):
    flat = packed.reshape(-1)
    out, off = [], 0
    for a in like:
        out.append(flat[off:off + a.size].reshape(a.shape))
        off += a.size
    return out


def _pad8(w):
    return jnp.pad(w, ((0, 5), (0, 0)))


def kernel(x, norm1_g, w_in, b_gate, gmlp_ln_g, gmlp_ln_b, w_spatial, b_spatial, w_shortconv, w_branch, w_out, norm2_g, w_ffn_up, w_ffn_conv, b_ffn_conv, w_ffn_down, final_g, loss_target, m_norm1_g, m_w_in, m_b_gate, m_gmlp_ln_g, m_gmlp_ln_b, m_w_spatial, m_b_spatial, m_w_shortconv, m_w_branch, m_w_out, m_norm2_g, m_w_ffn_up, m_w_ffn_conv, m_b_ffn_conv, m_w_ffn_down, m_final_g, v_norm1_g, v_w_in, v_b_gate, v_gmlp_ln_g, v_gmlp_ln_b, v_w_spatial, v_b_spatial, v_w_shortconv, v_w_branch, v_w_out, v_norm2_g, v_w_ffn_up, v_w_ffn_conv, v_b_ffn_conv, v_w_ffn_down, v_final_g):
    weights = dict(norm1_g=norm1_g, w_in=w_in, b_gate=b_gate, gmlp_ln_g=gmlp_ln_g, gmlp_ln_b=gmlp_ln_b,
                   w_spatial=w_spatial, b_spatial=b_spatial, w_shortconv=w_shortconv, w_branch=w_branch, w_out=w_out,
                   norm2_g=norm2_g, w_ffn_up=w_ffn_up, w_ffn_conv=w_ffn_conv, b_ffn_conv=b_ffn_conv,
                   w_ffn_down=w_ffn_down, final_g=final_g)
    mom = dict(norm1_g=m_norm1_g, w_in=m_w_in, b_gate=m_b_gate, gmlp_ln_g=m_gmlp_ln_g, gmlp_ln_b=m_gmlp_ln_b,
               w_spatial=m_w_spatial, b_spatial=m_b_spatial, w_shortconv=m_w_shortconv, w_branch=m_w_branch,
               w_out=m_w_out, norm2_g=m_norm2_g, w_ffn_up=m_w_ffn_up, w_ffn_conv=m_w_ffn_conv,
               b_ffn_conv=m_b_ffn_conv, w_ffn_down=m_w_ffn_down, final_g=m_final_g)
    vel = dict(norm1_g=v_norm1_g, w_in=v_w_in, b_gate=v_b_gate, gmlp_ln_g=v_gmlp_ln_g, gmlp_ln_b=v_gmlp_ln_b,
               w_spatial=v_w_spatial, b_spatial=v_b_spatial, w_shortconv=v_w_shortconv, w_branch=v_w_branch,
               w_out=v_w_out, norm2_g=v_norm2_g, w_ffn_up=v_w_ffn_up, w_ffn_conv=v_w_ffn_conv,
               b_ffn_conv=v_b_ffn_conv, w_ffn_down=v_w_ffn_down, final_g=v_final_g)

    cx, cy, cc = _mesh_pos()
    chip = 2 * cx + cy
    core_arr = cc.astype(jnp.int32).reshape(1)
    chip_arr = chip.astype(jnp.int32).reshape(1)
    pos_arr = jnp.stack([chip, cc]).astype(jnp.int32)
    t_len = x.shape[1]
    xs = x.reshape(t_len, D_MODEL)
    target = loss_target.reshape(t_len, D_MODEL)
    pipe = _Pipe()

    full = {}

    def gather(keys):
        slots = [_cast_into_slot(n, la, weights[n].reshape((N_LAYERS,) + BIG[n]), chip_arr) for n, la in keys]

        def then(*bufs):
            full.update(zip(keys, bufs))

        pipe.add(_gather_stage(slots, then))

    mixer_w = ("w_in", "w_branch", "w_out")
    ffn_w = ("w_ffn_up", "w_ffn_down")
    gather([(n, 0) for n in mixer_w])
    pipe.flush()

    idx = jnp.arange(GMLP_BLOCK) // CHUNK
    mask = idx[None, :] <= idx[:, None]
    wm_all = jnp.where(mask[None, None], w_spatial, 0.0)
    wm_bf = wm_all.astype(BF16)
    wmt_bf = jnp.swapaxes(wm_all, -1, -2).astype(BF16)
    bsf = jnp.repeat(jnp.swapaxes(b_spatial, -1, -2), 128, axis=-1)
    wsc_full = lax.dynamic_update_slice(jnp.zeros((N_LAYERS, 3, D_B), F32), w_shortconv, (0, 0, chip * (D_B // 4)))
    wfc_full = lax.dynamic_update_slice(jnp.zeros((N_LAYERS, 3, D_FF), F32), w_ffn_conv, (0, 0, chip * (D_FF // 4)))
    taps = _all_reduce_small("conv_taps", _pack([wsc_full, wfc_full]))
    wsc_full, wfc_full = _unpack(taps * 0.5, [wsc_full, wfc_full])

    def row(a):
        return a.reshape(1, -1)

    def mixer_args(la):
        return (row(norm1_g[la]), row(b_gate[la]), row(gmlp_ln_g[la]), row(gmlp_ln_b[la]))

    def mixer_weights(la):
        return tuple(full[(n, la)] for n in mixer_w)

    def ffn_weights(la):
        return tuple(full[(n, la)] for n in ffn_w)

    saved = []
    h_in = xs
    for la in range(N_LAYERS):
        gather([(n, la) for n in ffn_w])
        z, ya, yb, qs, av, bv, mg, h1, x2 = pipe.carry(lambda st: _mixer_fwd(
            la, h_in, *mixer_args(la), wm_bf[la], bsf[la], _pad8(wsc_full[la]), *mixer_weights(la), st))
        if la + 1 < N_LAYERS:
            gather([(n, la + 1) for n in mixer_w])
        up, silu, dsilu, act, h2, x3 = pipe.carry(lambda st: _ffn_fwd(
            la, x2, row(norm2_g[la]), _pad8(wfc_full[la]), row(b_ffn_conv[la]), *ffn_weights(la), st))
        saved.append(dict(x=h_in, z=z, ya=ya, yb=yb, q=qs, av=av, bv=bv, mg=mg, h1=h1, x2=x2, up=up, silu=silu, dsilu=dsilu, act=act,
                          h2=h2))
        h_in = x3

    reduced_big = {}

    def reduce_big(name, la, grad):
        tag = f"{name}_l{la}"

        def after_pair(other):
            psum = _pair_sum(tag, grad, other, core_arr)

            def after_chips(got):
                final = _chip_sum(tag, grad, other, got, pos_arr)
                pipe.add(_pair_fill_stage(final, lambda done: reduced_big.__setitem__((name, la), done)))

            pipe.add(_chip_send_stage(psum, after_chips))

        pipe.add(_pair_send_stage(grad, after_pair))

    dx, dgf8, loss8 = _loss_head(h_in, target, row(final_g))
    small = {n: [None] * N_LAYERS for n in SMALL}
    spread = {}
    for la in reversed(range(N_LAYERS)):
        s = saved[la]
        dx3 = dx
        dx2, dup, dx3b, dg2, dbfc, dwfc = pipe.carry(lambda st: _ffn_bwd(
            la, dx3, s["x2"], s["up"], s["silu"], s["dsilu"], row(norm2_g[la]), _pad8(wfc_full[la]),
            *ffn_weights(la), st))
        g, = pipe.carry(lambda st: _wgrad("w_ffn_down", la, s["act"], dx3b, 704, 1024, 1408, 1024, st))
        reduce_big("w_ffn_down", la, g)
        g, = pipe.carry(lambda st: _wgrad("w_ffn_up", la, s["h2"], dup, 1024, 1408, 1024, 1408, st))
        reduce_big("w_ffn_up", la, g)
        dxl, dz, da, db, dx2b, dg1, dbg, dlng, dlnb, dwm, dbsf, dwsc = pipe.carry(lambda st: _mixer_bwd(
            la, dx2, s["x"], s["z"], s["q"], s["av"], s["bv"], *mixer_args(la), wm_bf[la], wmt_bf[la], bsf[la],
            _pad8(wsc_full[la]), *mixer_weights(la), st))
        small["norm1_g"][la] = dg1.sum(0)
        small["b_gate"][la] = dbg.sum(0)
        small["gmlp_ln_g"][la] = dlng.sum(0)
        small["gmlp_ln_b"][la] = dlnb.sum(0)
        small["w_spatial"][la] = jnp.where(mask[None], dwm, 0.0)
        small["b_spatial"][la] = dbsf.reshape(128, A_HEADS, 128).sum(-1).T
        small["w_shortconv"][la] = dwsc.sum(1)
        small["norm2_g"][la] = dg2.sum(0)
        small["w_ffn_conv"][la] = dwfc.sum(1)
        small["b_ffn_conv"][la] = dbfc.sum(0)
        if la == 0:
            small_local = ([jnp.stack(small[n]) for n in SMALL[:-1]]
                           + [dgf8.sum(0), 0.5 * loss8.sum().reshape(1) / D_MODEL])
            pipe.add(_spread_stage(_pack(small_local), lambda slots: spread.__setitem__("slots", slots)))
        g, = pipe.carry(lambda st: _wgrad("w_in", la, s["h1"], dz, 1024, 1152, 1024, 1152, st))
        reduce_big("w_in", la, g)
        g, = pipe.carry(lambda st: _wgrad("w_out", la, s["mg"], dx2b, 256, 1024, 1024, 1024, st))
        reduce_big("w_out", la, g)
        g, = pipe.carry(lambda st: _wgrad_branch(la, s["ya"], da, s["yb"], db, st))
        reduce_big("w_branch", la, g)
        dx = dxl
    grad_x = dx.reshape(x.shape)
    pipe.flush()

    reduced = _unpack(_sum_slots("small_grads", spread["slots"]), small_local)
    loss = reduced[-1].reshape(())
    grads = dict(zip(SMALL, reduced[:-1]))
    grads["w_shortconv"] = lax.dynamic_slice(grads["w_shortconv"], (0, 0, chip * (D_B // 4)), (N_LAYERS, 3, D_B // 4))
    grads["w_ffn_conv"] = lax.dynamic_slice(grads["w_ffn_conv"], (0, 0, chip * (D_FF // 4)), (N_LAYERS, 3, D_FF // 4))

    delta, new_m, new_v = {}, {}, {}
    for n in BIG_NAMES:
        shape3 = (N_LAYERS,) + BIG[n]
        res = _adamw_big(n, weights[n].reshape(shape3), reduced_big[(n, 0)], reduced_big[(n, 1)],
                         mom[n].reshape(shape3), vel[n].reshape(shape3))
        grads[n], delta[n], new_m[n], new_v[n] = (a.reshape(weights[n].shape) for a in res)
    small_w = [weights[n] for n in SMALL]
    packed = [_pack([src[n] for n in SMALL]) for src in (weights, grads, mom, vel)]
    for dst, res in zip((delta, new_m, new_v), _adamw("small", *packed)):
        dst.update(zip(SMALL, _unpack(res, small_w)))

    return (loss, grad_x, *[grads[n] for n in ALL_WEIGHTS], *[delta[n] for n in ALL_WEIGHTS],
            *[new_m[n] for n in ALL_WEIGHTS], *[new_v[n] for n in ALL_WEIGHTS])
```

```python
import jax
import jax.numpy as jnp
from jax import lax
from jax.experimental import pallas as pl
from jax.experimental.pallas import tpu as pltpu

F32 = jnp.float32
BF16 = jnp.bfloat16
MESH = pl.DeviceIdType.MESH
ANY = pl.BlockSpec(memory_space=pl.ANY)

D_MODEL = 1024
D_A = 512
D_B = 512
D_IN = 4608
D_FF = 2816
GMLP_BLOCK = 128
CHUNK = 64
A_HEADS = 4
N_LAYERS = 2
N_CHIPS = 4
N_DEVICES = 8
RMS_EPS = 1e-6
LN_EPS = 1e-5
ADAM_LR = 0.001
ADAM_B1 = 0.9
ADAM_B2 = 0.999
ADAM_EPS = 1e-08
ADAM_WD = 0.01
ADAM_STEP = 10

C_U, C_V, C_BG, C_CG, C_HB, C_GA, C_GB = 0, 512, 1024, 1536, 2048, 2560, 3584

V7X_VMEM_LIMIT = 60 * 1024 * 1024
TM_MIX = 256
TM_FFN = 256
TM_EW = 512
TK_WGRAD = 2048
SLOW_COPY_BYTES = 640 * 1024
FF_CHUNKS = ((0, 768), (768, 1536), (1536, 2304), (2304, 2816))
GELU_C0 = 0.7978845608028654
GELU_C1 = 0.044715

BIG = {
    "w_in": (1024, 1152),
    "w_branch": (1024, 256),
    "w_out": (256, 1024),
    "w_ffn_up": (1024, 1408),
    "w_ffn_down": (704, 1024),
}
BIG_NAMES = tuple(BIG)


def _params(sem=("arbitrary",), vmem=V7X_VMEM_LIMIT):
    return pltpu.CompilerParams(dimension_semantics=sem, vmem_limit_bytes=vmem)


def _gelu(x):
    x2 = x * x
    t = jnp.tanh(GELU_C0 * x * (1.0 + GELU_C1 * x2))
    return 0.5 * x * (1.0 + t), t


def _gelu_grad(x, t):
    return 0.5 * (1.0 + t) + 0.5 * x * (1.0 - t * t) * GELU_C0 * (1.0 + 3.0 * GELU_C1 * x * x)


def _colsum8(v):
    r, n = v.shape
    return v.reshape(r // 8, 8, n).sum(axis=0)


def _dot(a, b):
    return jnp.dot(a, b, preferred_element_type=F32)


def _dot_nt(a, b):
    return lax.dot_general(a, b, (((1,), (1,)), ((), ())), preferred_element_type=F32)


def _dot_tn(a, b):
    return lax.dot_general(a, b, (((0,), (0,)), ((), ())), preferred_element_type=F32)


def _shift_down(v, carry, n):
    rows = lax.broadcasted_iota(jnp.int32, (8, v.shape[1]), 0)
    out = pltpu.roll(v, n, 0)
    head = out[0:8, :]
    for r in range(n):
        head = jnp.where(rows == r, carry[8 - n + r:8 - n + r + 1, :], head)
    return jnp.concatenate([head, out[8:, :]], axis=0)


def _shift_up(v, carry, n):
    tm = v.shape[0]
    rows = lax.broadcasted_iota(jnp.int32, (8, v.shape[1]), 0)
    out = pltpu.roll(v, tm - n, 0)
    tail = out[tm - 8:tm, :]
    for r in range(n):
        tail = jnp.where(rows == 8 - n + r, carry[r:r + 1, :], tail)
    return jnp.concatenate([out[0:tm - 8, :], tail], axis=0)


def _sigmoid(x):
    return 0.5 * jnp.tanh(0.5 * x) + 0.5


def _start_all(copies):
    for cp in copies:
        cp.start()


def _wait_all(copies):
    for cp in copies:
        cp.wait()


def _load_col_sharded(src, dst, sems, first):
    cs = src.shape[-1]
    return [pltpu.make_async_copy(src.at[k], dst.at[:, k * cs:(k + 1) * cs], sems.at[first + k])
            for k in range(N_CHIPS)]


def _load_row_sharded(src, dst, sems, first):
    rs = src.shape[-2]
    return [pltpu.make_async_copy(src.at[k], dst.at[k * rs:(k + 1) * rs, :], sems.at[first + k])
            for k in range(N_CHIPS)]


def _load_branch(src, dst, sems, first):
    return [pltpu.make_async_copy(src.at[k, pl.ds(m * D_A, D_A), :], dst.at[m, :, k * 256:(k + 1) * 256],
                                  sems.at[first + 2 * k + m])
            for k in range(N_CHIPS) for m in range(2)]


def _row_spec(tm, n, rev=None):
    if rev is None:
        return pl.BlockSpec((tm, n), lambda i: (i, 0))
    return pl.BlockSpec((tm, n), lambda i: (rev - 1 - i, 0))


def _const_spec(shape):
    nd = len(shape)
    return pl.BlockSpec(shape, lambda i: (0,) * nd)


def _mesh_pos():
    return lax.axis_index("x"), lax.axis_index("y"), lax.axis_index("c")


def _other_chips(x, y):
    return [(1 - x, y, 2 * (1 - x) + y), (x, 1 - y, 2 * x + (1 - y)), (1 - x, 1 - y, 2 * (1 - x) + (1 - y))]


def _remote(src, dst, ssem, rsem, to):
    return pltpu.make_async_remote_copy(src_ref=src, dst_ref=dst, send_sem=ssem, recv_sem=rsem, device_id=to,
                                        device_id_type=MESH)


def _half(ref, which, h):
    start = pl.multiple_of(which * h, 8)
    if len(ref.shape) == 2:
        return ref.at[pl.ds(start, h), :]
    return ref.at[:, pl.ds(start, h), :]


class _Stage:
    def __init__(self, ins=(), inouts=(), outs=(), n_sems=0, start=None, mid=None, finish=None, then=None, slow=False):
        self.ins, self.inouts, self.outs = list(ins), list(inouts), list(outs)
        self.n_sems, self.start, self.mid, self.finish, self.then = n_sems, start, mid, finish, then
        self.slow = slow


def _gather_stage(bufs, then):
    n = len(bufs)

    def copies(io, sem):
        x, y, c = _mesh_pos()
        me = 2 * x + y
        ici, fwd, got = [], [], []
        for w in range(n):
            h = io[w].shape[1] // 2
            for j, (px, py, pk) in enumerate(_other_chips(x, y)):
                mine = _half(io[w].at[me], c, h)
                theirs = _half(io[w].at[pk], c, h)
                ici.append(_remote(mine, mine, sem(12 * w + j), sem(12 * w + 3 + j), (px, py, c)))
                got.append(_remote(theirs, theirs, sem(12 * w + j), sem(12 * w + 3 + j), (px, py, c)))
                fwd.append(_remote(theirs, theirs, sem(12 * w + 6 + j), sem(12 * w + 9 + j), (x, y, 1 - c)))
        return ici, got, fwd

    def start(ins, io, outs, sem):
        _start_all(copies(io, sem)[0])

    def mid(ins, io, outs, sem):
        _, got, fwd = copies(io, sem)
        for g, f in zip(got, fwd):
            g.wait_recv()
            f.start()

    def finish(ins, io, outs, sem):
        x, y, c = _mesh_pos()
        ici, _, fwd = copies(io, sem)
        for w in range(n):
            h = io[w].shape[1] // 2
            for j, (px, py, pk) in enumerate(_other_chips(x, y)):
                other = _half(io[w].at[pk], 1 - c, h)
                _remote(other, other, sem(12 * w + 6 + j), sem(12 * w + 9 + j), (x, y, 1 - c)).wait_recv()
        for cp in ici + fwd:
            cp.wait_send()

    return _Stage(inouts=bufs, n_sems=12 * n, start=start, mid=mid, finish=finish, then=then)


def _pair_send_stage(grad, then):
    h = grad.shape[1] // 2

    def copy(ins, outs, sem):
        x, y, c = _mesh_pos()
        return _remote(_half(ins[0], 1 - c, h), outs[0], sem(0), sem(1), (x, y, 1 - c))

    return _Stage(ins=[grad], outs=[jax.ShapeDtypeStruct((N_CHIPS, h, grad.shape[2]), F32)], n_sems=2,
                  start=lambda ins, io, outs, sem: copy(ins, outs, sem).start(),
                  finish=lambda ins, io, outs, sem: copy(ins, outs, sem).wait(), then=then)


def _chip_send_stage(psum, then):
    def copies(ins, outs, sem):
        x, y, c = _mesh_pos()
        return [_remote(ins[0].at[pk], outs[0].at[j], sem(j), sem(3 + j), (px, py, c))
                for j, (px, py, pk) in enumerate(_other_chips(x, y))]

    return _Stage(ins=[psum], outs=[jax.ShapeDtypeStruct((3,) + psum.shape[1:], BF16)], n_sems=6,
                  start=lambda ins, io, outs, sem: _start_all(copies(ins, outs, sem)),
                  finish=lambda ins, io, outs, sem: _wait_all(copies(ins, outs, sem)), then=then,
                  slow=psum.shape[1] * psum.shape[2] * 2 > SLOW_COPY_BYTES)


def _pair_fill_stage(final, then):
    h = final.shape[0] // 2

    def copy(io, sem):
        x, y, c = _mesh_pos()
        mine = _half(io[0], c, h)
        return _remote(mine, mine, sem(0), sem(1), (x, y, 1 - c))

    return _Stage(inouts=[final], n_sems=2,
                  start=lambda ins, io, outs, sem: copy(io, sem).start(),
                  finish=lambda ins, io, outs, sem: copy(io, sem).wait(), then=then)


def _pair_swap_stage(packed, then):
    def copy(ins, outs, sem):
        x, y, c = _mesh_pos()
        return _remote(ins[0], outs[0], sem(0), sem(1), (x, y, 1 - c))

    return _Stage(ins=[packed], outs=[jax.ShapeDtypeStruct(packed.shape, F32)], n_sems=2,
                  start=lambda ins, io, outs, sem: copy(ins, outs, sem).start(),
                  finish=lambda ins, io, outs, sem: copy(ins, outs, sem).wait(), then=then)


def _chip_spread_stage(psum, then):
    def copies(ins, outs, sem):
        x, y, c = _mesh_pos()
        me = 2 * x + y
        cps = [_remote(ins[0], outs[0].at[me], sem(j), sem(3 + j), (px, py, c))
               for j, (px, py, pk) in enumerate(_other_chips(x, y))]
        return cps, pltpu.make_async_copy(ins[0], outs[0].at[me], sem(6))

    def start(ins, io, outs, sem):
        cps, own = copies(ins, outs, sem)
        own.start()
        _start_all(cps)

    def finish(ins, io, outs, sem):
        cps, own = copies(ins, outs, sem)
        _wait_all(cps)
        own.wait()

    return _Stage(ins=[psum], outs=[jax.ShapeDtypeStruct((N_CHIPS,) + psum.shape, F32)], n_sems=7,
                  start=start, finish=finish, then=then)


def _staged_call(core, *, name, grid, in_specs, out_specs, out_shape, scratch_shapes, args, stages):
    n_in, n_out, n_scr = len(args), len(out_shape), len(scratch_shapes)
    s_args, s_outs, aliases, layout = [], [], {}, []
    n_sems = 0
    for st in stages:
        i0, o0 = len(s_args), len(s_outs)
        s_args += st.ins + st.inouts
        for q in range(len(st.inouts)):
            aliases[n_in + i0 + len(st.ins) + q] = n_out + o0 + q
        s_outs += [jax.ShapeDtypeStruct(a.shape, a.dtype) for a in st.inouts] + st.outs
        layout.append((i0, o0, n_sems))
        n_sems += st.n_sems
    steps = 1
    for g in grid:
        steps *= g

    def body(*refs):
        own_in = refs[:n_in]
        s_in = refs[n_in:n_in + len(s_args)]
        rest = refs[n_in + len(s_args):]
        own_out = rest[:n_out]
        s_out = rest[n_out:n_out + len(s_outs)]
        scr = rest[n_out + len(s_outs):]

        def run(which):
            for st, (i0, o0, s0) in zip(stages, layout):
                fn = getattr(st, which)
                if fn is not None:
                    fn(s_in[i0:i0 + len(st.ins)], s_out[o0:o0 + len(st.inouts)],
                       s_out[o0 + len(st.inouts):o0 + len(st.inouts) + len(st.outs)],
                       lambda k, s0=s0: scr[n_scr].at[s0 + k])

        if not stages:
            core(*own_in, *own_out, *scr[:n_scr])
            return
        step = 0
        for d, g in enumerate(grid):
            step = step * g + pl.program_id(d)
        if steps == 1:
            run("start")
            core(*own_in, *own_out, *scr[:n_scr])
            run("mid")
            run("finish")
            return
        pl.when(step == 0)(lambda: run("start"))
        core(*own_in, *own_out, *scr[:n_scr])
        pl.when(step == (3 * steps) // 4)(lambda: run("mid"))
        pl.when(step == steps - 1)(lambda: run("finish"))

    sem = ("arbitrary",) * len(grid) if stages else ("parallel",) * max(len(grid) - 1, 0) + ("arbitrary",) * min(len(grid), 1)
    res = pl.pallas_call(
        body, name=name, grid=grid,
        in_specs=list(in_specs) + [ANY] * len(s_args),
        out_specs=list(out_specs) + [ANY] * len(s_outs),
        out_shape=list(out_shape) + s_outs,
        input_output_aliases=aliases,
        scratch_shapes=list(scratch_shapes) + ([pltpu.SemaphoreType.DMA((n_sems,))] if stages else []),
        compiler_params=_params(sem) if grid else pltpu.CompilerParams(vmem_limit_bytes=V7X_VMEM_LIMIT),
    )(*args, *s_args)
    return list(res[:n_out]), list(res[n_out:])


class _Pipe:
    def __init__(self):
        self.ready = []
        self.flushes = 0

    def add(self, stage):
        self.ready.append(stage)

    def carry(self, call, long=True):
        stages = [st for st in self.ready if long or not st.slow]
        self.ready = [st for st in self.ready if not (long or not st.slow)]
        own, outs = call(stages)
        k = 0
        for st in stages:
            n = len(st.inouts) + len(st.outs)
            st.then(*outs[k:k + n])
            k += n
        return own

    def flush(self):
        while self.ready:
            self.flushes += 1
            self.carry(lambda stages: _staged_call(
                lambda *refs: None, name=f"comm_tail_{self.flushes}", grid=(), in_specs=[], out_specs=[], out_shape=[],
                scratch_shapes=[], args=[], stages=stages))


def _mixer_fwd(layer, x, g1, bgate, lng, lnb, wm, bsf, wsc, win_g, wb_g, wout_g, stages):
    t_len = x.shape[0]
    tm = min(TM_MIX, t_len)
    nt = t_len // tm
    nb = tm // GMLP_BLOCK

    def core(x_ref, g1_ref, bgate_ref, lng_ref, lnb_ref, wm_ref, bsf_ref, wsc_ref, win_hbm, wb_hbm, wout_hbm,
             z_ref, ya_ref, yb_ref, q_ref, a_ref, b_ref, mg_ref, h_ref, x2_ref,
             win_v, wb_v, wout_v, carry, vn_s, f_s, sems):
        i = pl.program_id(0)

        @pl.when(i == 0)
        def _():
            cps = (_load_col_sharded(win_hbm, win_v, sems, 0) + _load_branch(wb_hbm, wb_v, sems, 4)
                   + _load_row_sharded(wout_hbm, wout_v, sems, 12))
            _start_all(cps)
            carry[...] = jnp.zeros_like(carry)
            _wait_all(cps)

        xv = x_ref[...]
        r = lax.rsqrt(jnp.mean(xv * xv, axis=-1, keepdims=True) + RMS_EPS)
        h_ref[...] = (xv * r * g1_ref[...]).astype(BF16)

        def zcols(c0, c1):
            zc = _dot(h_ref[...], win_v[:, c0:c1])
            z_ref[:, c0:c1] = zc.astype(BF16)
            return zc

        vg, _ = _gelu(zcols(C_V, C_V + D_A))
        mu = jnp.mean(vg, axis=-1, keepdims=True)
        vc = vg - mu
        rstd = lax.rsqrt(jnp.mean(vc * vc, axis=-1, keepdims=True) + LN_EPS)
        vn_s[...] = (vc * rstd * lng_ref[...] + lnb_ref[...]).astype(BF16)
        for hd in range(A_HEADS):
            cols = slice(hd * 128, (hd + 1) * 128)
            vcat = jnp.concatenate([vn_s[b * 128:(b + 1) * 128, cols] for b in range(nb)], axis=1)
            fcat = _dot(wm_ref[hd], vcat)
            for b in range(nb):
                f_s[b * 128:(b + 1) * 128, cols] = fcat[:, b * 128:(b + 1) * 128]
        ug, _ = _gelu(zcols(C_U, C_U + D_A))
        bias = jnp.concatenate([bsf_ref[...]] * nb, axis=0)
        ya_ref[...] = (ug * (f_s[...] + bias)).astype(BF16)

        p = zcols(C_CG, C_CG + D_B) * zcols(C_HB, C_HB + D_B)
        cr = carry[...]
        q = wsc_ref[0:1, :] * _shift_down(p, cr, 2) + wsc_ref[1:2, :] * _shift_down(p, cr, 1) + wsc_ref[2:3, :] * p
        carry[...] = p[tm - 8:tm, :]
        q_ref[...] = q.astype(BF16)
        yb_ref[...] = (zcols(C_BG, C_BG + D_B) * q).astype(BF16)

        av = _dot(ya_ref[...], wb_v[0])
        a_ref[...] = av.astype(BF16)
        mg = _sigmoid(zcols(C_GA, C_GA + D_MODEL) + bgate_ref[:, 0:D_MODEL]) * av
        bv = _dot(yb_ref[...], wb_v[1])
        b_ref[...] = bv.astype(BF16)
        mg = mg + _sigmoid(zcols(C_GB, C_GB + D_MODEL) + bgate_ref[:, D_MODEL:2 * D_MODEL]) * bv
        mg_ref[...] = mg.astype(BF16)
        x2_ref[...] = x_ref[...] + _dot(mg_ref[...], wout_v[...])

    outs = [
        jax.ShapeDtypeStruct((t_len, D_IN), BF16),
        jax.ShapeDtypeStruct((t_len, D_A), BF16),
        jax.ShapeDtypeStruct((t_len, D_B), BF16),
        jax.ShapeDtypeStruct((t_len, D_B), BF16),
        jax.ShapeDtypeStruct((t_len, D_MODEL), BF16),
        jax.ShapeDtypeStruct((t_len, D_MODEL), BF16),
        jax.ShapeDtypeStruct((t_len, D_MODEL), BF16),
        jax.ShapeDtypeStruct((t_len, D_MODEL), BF16),
        jax.ShapeDtypeStruct((t_len, D_MODEL), F32),
    ]
    return _staged_call(
        core, name=f"mixer_fwd_l{layer}", grid=(nt,),
        in_specs=[_row_spec(tm, D_MODEL), _const_spec((1, D_MODEL)), _const_spec((1, 2 * D_MODEL)),
                  _const_spec((1, D_A)), _const_spec((1, D_A)), _const_spec((A_HEADS, 128, 128)),
                  _const_spec((128, D_A)), _const_spec((8, D_B)), ANY, ANY, ANY],
        out_specs=[_row_spec(tm, o.shape[1]) for o in outs],
        out_shape=outs,
        scratch_shapes=[pltpu.VMEM((D_MODEL, D_IN), BF16), pltpu.VMEM((2, D_A, D_MODEL), BF16),
                        pltpu.VMEM((D_MODEL, D_MODEL), BF16), pltpu.VMEM((8, D_B), F32),
                        pltpu.VMEM((tm, D_A), BF16), pltpu.VMEM((tm, D_A), F32), pltpu.SemaphoreType.DMA((16,))],
        args=[x, g1, bgate, lng, lnb, wm, bsf, wsc, win_g, wb_g, wout_g], stages=stages)


def _ffn_fwd(layer, x2, g2, wfc, bfc, wup_g, wdown_g, stages):
    t_len = x2.shape[0]
    tm = min(TM_FFN, t_len)
    nt = t_len // tm

    def core(x_ref, g2_ref, wfc_ref, bfc_ref, wup_hbm, wdown_hbm, up_ref, silu_ref, dsilu_ref, act_ref, h_ref, x3_ref,
             wup_v, wdown_v, carry, sems):
        i = pl.program_id(0)

        @pl.when(i == 0)
        def _():
            cps = _load_col_sharded(wup_hbm, wup_v, sems, 0) + _load_row_sharded(wdown_hbm, wdown_v, sems, 4)
            _start_all(cps)
            carry[...] = jnp.zeros_like(carry)
            _wait_all(cps)

        xv = x_ref[...]
        r = lax.rsqrt(jnp.mean(xv * xv, axis=-1, keepdims=True) + RMS_EPS)
        h_ref[...] = (xv * r * g2_ref[...]).astype(BF16)
        gate = _dot(h_ref[...], wup_v[:, 0:D_FF])
        up_ref[:, 0:D_FF] = gate.astype(BF16)
        cr = carry[...]
        gc = (wfc_ref[0:1, :] * _shift_down(gate, cr, 2) + wfc_ref[1:2, :] * _shift_down(gate, cr, 1)
              + wfc_ref[2:3, :] * gate + bfc_ref[...])
        carry[...] = gate[tm - 8:tm, :]
        sg = _sigmoid(gc)
        silu = gc * sg
        silu_ref[...] = silu.astype(BF16)
        dsilu_ref[...] = (sg + silu * (1.0 - sg)).astype(BF16)
        val = _dot(h_ref[...], wup_v[:, D_FF:2 * D_FF])
        up_ref[:, D_FF:2 * D_FF] = val.astype(BF16)
        act_ref[...] = (silu * val).astype(BF16)
        x3_ref[...] = x_ref[...] + _dot(act_ref[...], wdown_v[...])

    outs = [
        jax.ShapeDtypeStruct((t_len, 2 * D_FF), BF16),
        jax.ShapeDtypeStruct((t_len, D_FF), BF16),
        jax.ShapeDtypeStruct((t_len, D_FF), BF16),
        jax.ShapeDtypeStruct((t_len, D_FF), BF16),
        jax.ShapeDtypeStruct((t_len, D_MODEL), BF16),
        jax.ShapeDtypeStruct((t_len, D_MODEL), F32),
    ]
    return _staged_call(
        core, name=f"ffn_fwd_l{layer}", grid=(nt,),
        in_specs=[_row_spec(tm, D_MODEL), _const_spec((1, D_MODEL)), _const_spec((8, D_FF)), _const_spec((1, D_FF)), ANY, ANY],
        out_specs=[_row_spec(tm, o.shape[1]) for o in outs],
        out_shape=outs,
        scratch_shapes=[pltpu.VMEM((D_MODEL, 2 * D_FF), BF16), pltpu.VMEM((D_FF, D_MODEL), BF16),
                        pltpu.VMEM((8, D_FF), F32), pltpu.SemaphoreType.DMA((8,))],
        args=[x2, g2, wfc, bfc, wup_g, wdown_g], stages=stages)


def _loss_head(x3, target, gf):
    t_len = x3.shape[0]
    tm = min(TM_EW, t_len)
    nt = t_len // tm

    def body(x_ref, t_ref, gf_ref, dx_ref, dgf_ref, loss_ref):
        i = pl.program_id(0)

        @pl.when(i == 0)
        def _():
            dgf_ref[...] = jnp.zeros_like(dgf_ref)
            loss_ref[...] = jnp.zeros_like(loss_ref)

        xv = x_ref[...]
        r = lax.rsqrt(jnp.mean(xv * xv, axis=-1, keepdims=True) + RMS_EPS)
        xh = xv * r
        err = xh * gf_ref[...] - t_ref[...]
        loss_ref[...] += _colsum8(err * err)
        dy = err * (1.0 / D_MODEL)
        dgf_ref[...] += _colsum8(dy * xh)
        dxh = dy * gf_ref[...]
        dx_ref[...] = r * (dxh - xh * jnp.mean(dxh * xh, axis=-1, keepdims=True))

    return pl.pallas_call(
        body, name="loss_head", grid=(nt,),
        in_specs=[_row_spec(tm, D_MODEL), _row_spec(tm, D_MODEL), _const_spec((1, D_MODEL))],
        out_specs=[_row_spec(tm, D_MODEL), _const_spec((8, D_MODEL)), _const_spec((8, D_MODEL))],
        out_shape=[jax.ShapeDtypeStruct((t_len, D_MODEL), F32), jax.ShapeDtypeStruct((8, D_MODEL), F32),
                   jax.ShapeDtypeStruct((8, D_MODEL), F32)],
        compiler_params=_params(),
    )(x3, target, gf)


def _ffn_bwd(layer, dx3, x2, up, silu, dsilu, g2, wfc, wup_g, wdown_g, stages):
    t_len = x2.shape[0]
    tm = min(TM_FFN, t_len)
    nt = t_len // tm

    def core(dx3_ref, x_ref, up_ref, silu_ref, dsilu_ref, g2_ref, wfc_ref, wup_hbm, wdown_hbm,
             dx2_ref, dup_ref, dx3b_ref, dg2_ref, dbfc_ref, dwfc_ref,
             wup_v, wdown_v, carry, sems):
        i = pl.program_id(0)

        @pl.when(i == 0)
        def _():
            cps = _load_col_sharded(wup_hbm, wup_v, sems, 0) + _load_row_sharded(wdown_hbm, wdown_v, sems, 4)
            _start_all(cps)
            carry[...] = jnp.zeros_like(carry)
            dg2_ref[...] = jnp.zeros_like(dg2_ref)
            dbfc_ref[...] = jnp.zeros_like(dbfc_ref)
            dwfc_ref[...] = jnp.zeros_like(dwfc_ref)
            _wait_all(cps)

        dx3b_ref[...] = dx3_ref[...].astype(BF16)
        dh = jnp.zeros((tm, D_MODEL), F32)
        for c0, c1 in FF_CHUNKS:
            v0, v1 = D_FF + c0, D_FF + c1
            da = _dot_nt(dx3b_ref[...], wdown_v[c0:c1, :])
            dup_ref[:, v0:v1] = (da * silu_ref[:, c0:c1].astype(F32)).astype(BF16)
            dgc = da * up_ref[:, v0:v1].astype(F32) * dsilu_ref[:, c0:c1].astype(F32)
            cr = carry[:, c0:c1]
            dgc1 = _shift_up(dgc, cr, 1)
            dgc2 = _shift_up(dgc, cr, 2)
            carry[:, c0:c1] = dgc[0:8, :]
            gate = up_ref[:, c0:c1].astype(F32)
            dbfc_ref[:, c0:c1] += _colsum8(dgc)
            dwfc_ref[0, :, c0:c1] += _colsum8(dgc2 * gate)
            dwfc_ref[1, :, c0:c1] += _colsum8(dgc1 * gate)
            dwfc_ref[2, :, c0:c1] += _colsum8(dgc * gate)
            dgate = wfc_ref[2:3, c0:c1] * dgc + wfc_ref[1:2, c0:c1] * dgc1 + wfc_ref[0:1, c0:c1] * dgc2
            dup_ref[:, c0:c1] = dgate.astype(BF16)
            dh = dh + _dot_nt(dup_ref[:, c0:c1], wup_v[:, c0:c1]) + _dot_nt(dup_ref[:, v0:v1], wup_v[:, v0:v1])
        xv = x_ref[...]
        r = lax.rsqrt(jnp.mean(xv * xv, axis=-1, keepdims=True) + RMS_EPS)
        xh = xv * r
        dg2_ref[...] += _colsum8(dh * xh)
        dxh = dh * g2_ref[...]
        dx2_ref[...] = dx3_ref[...] + r * (dxh - xh * jnp.mean(dxh * xh, axis=-1, keepdims=True))

    outs = [
        jax.ShapeDtypeStruct((t_len, D_MODEL), F32),
        jax.ShapeDtypeStruct((t_len, 2 * D_FF), BF16),
        jax.ShapeDtypeStruct((t_len, D_MODEL), BF16),
        jax.ShapeDtypeStruct((8, D_MODEL), F32),
        jax.ShapeDtypeStruct((8, D_FF), F32),
        jax.ShapeDtypeStruct((3, 8, D_FF), F32),
    ]
    return _staged_call(
        core, name=f"ffn_bwd_l{layer}", grid=(nt,),
        in_specs=[_row_spec(tm, D_MODEL, nt), _row_spec(tm, D_MODEL, nt), _row_spec(tm, 2 * D_FF, nt),
                  _row_spec(tm, D_FF, nt), _row_spec(tm, D_FF, nt), _const_spec((1, D_MODEL)), _const_spec((8, D_FF)),
                  ANY, ANY],
        out_specs=[_row_spec(tm, D_MODEL, nt), _row_spec(tm, 2 * D_FF, nt), _row_spec(tm, D_MODEL, nt),
                   _const_spec((8, D_MODEL)), _const_spec((8, D_FF)), _const_spec((3, 8, D_FF))],
        out_shape=outs,
        scratch_shapes=[pltpu.VMEM((D_MODEL, 2 * D_FF), BF16), pltpu.VMEM((D_FF, D_MODEL), BF16),
                        pltpu.VMEM((8, D_FF), F32), pltpu.SemaphoreType.DMA((8,))],
        args=[dx3, x2, up, silu, dsilu, g2, wfc, wup_g, wdown_g], stages=stages)


def _mixer_bwd(layer, dx2, x, z, qs, av, bv, g1, bgate, lng, lnb, wm, wmt, bsf, wsc, win_g, wb_g, wout_g, stages):
    t_len = x.shape[0]
    tm = min(TM_MIX, t_len)
    nt = t_len // tm
    nb = tm // GMLP_BLOCK

    def core(dx2_ref, x_ref, z_ref, q_ref, a_ref, b_ref, g1_ref, bgate_ref, lng_ref, lnb_ref,
             wm_ref, wmt_ref, bsf_ref, wsc_ref, win_hbm, wb_hbm, wout_hbm,
             dx_ref, dz_ref, da_ref, db_ref, dx2b_ref, dg1_ref, dbgate_ref, dlng_ref, dlnb_ref, dwm_ref, dbsf_ref, dwsc_ref,
             win_v, wb_v, wout_v, carry, vn_s, f_s, df_s, dvn_s, sems):
        i = pl.program_id(0)

        @pl.when(i == 0)
        def _():
            cps = (_load_col_sharded(win_hbm, win_v, sems, 0) + _load_branch(wb_hbm, wb_v, sems, 4)
                   + _load_row_sharded(wout_hbm, wout_v, sems, 12))
            _start_all(cps)
            carry[...] = jnp.zeros_like(carry)
            for ref in (dg1_ref, dbgate_ref, dlng_ref, dlnb_ref, dwm_ref, dbsf_ref, dwsc_ref):
                ref[...] = jnp.zeros_like(ref)
            _wait_all(cps)

        def zc(c0, n):
            return z_ref[:, c0:c0 + n].astype(F32)

        dx2b_ref[...] = dx2_ref[...].astype(BF16)
        dm = _dot_nt(dx2b_ref[...], wout_v[...])
        def dz_cols(c0, n, val):
            dz_ref[:, c0:c0 + n] = val.astype(BF16)
            return _dot_nt(dz_ref[:, c0:c0 + n], win_v[:, c0:c0 + n])

        sa = _sigmoid(zc(C_GA, D_MODEL) + bgate_ref[:, 0:D_MODEL])
        da_ref[...] = (dm * sa).astype(BF16)
        dga = dm * a_ref[...].astype(F32) * sa * (1.0 - sa)
        dh = dz_cols(C_GA, D_MODEL, dga)
        dbgate_ref[:, 0:D_MODEL] += _colsum8(dga)
        dya = _dot_nt(da_ref[...], wb_v[0])
        sb = _sigmoid(zc(C_GB, D_MODEL) + bgate_ref[:, D_MODEL:2 * D_MODEL])
        db_ref[...] = (dm * sb).astype(BF16)
        dgb = dm * b_ref[...].astype(F32) * sb * (1.0 - sb)
        dh = dh + dz_cols(C_GB, D_MODEL, dgb)
        dbgate_ref[:, D_MODEL:2 * D_MODEL] += _colsum8(dgb)
        dyb = _dot_nt(db_ref[...], wb_v[1])

        v = zc(C_V, D_A)
        vg, tv = _gelu(v)
        mu = jnp.mean(vg, axis=-1, keepdims=True)
        vc = vg - mu
        rstd = lax.rsqrt(jnp.mean(vc * vc, axis=-1, keepdims=True) + LN_EPS)
        xh = vc * rstd
        vn_s[...] = (xh * lng_ref[...] + lnb_ref[...]).astype(BF16)
        u = zc(C_U, D_A)
        ug, tu = _gelu(u)
        df = dya * ug
        df_s[...] = df.astype(BF16)
        dbsf_acc = df[0:128, :]
        for b in range(1, nb):
            dbsf_acc = dbsf_acc + df[b * 128:(b + 1) * 128, :]
        dbsf_ref[...] += dbsf_acc
        for hd in range(A_HEADS):
            cols = slice(hd * 128, (hd + 1) * 128)
            vcat = jnp.concatenate([vn_s[b * 128:(b + 1) * 128, cols] for b in range(nb)], axis=1)
            dcat = jnp.concatenate([df_s[b * 128:(b + 1) * 128, cols] for b in range(nb)], axis=1)
            fcat = _dot(wm_ref[hd], vcat)
            gcat = _dot(wmt_ref[hd], dcat)
            dwm_ref[hd] += _dot_nt(dcat, vcat)
            for b in range(nb):
                f_s[b * 128:(b + 1) * 128, cols] = fcat[:, b * 128:(b + 1) * 128]
                dvn_s[b * 128:(b + 1) * 128, cols] = gcat[:, b * 128:(b + 1) * 128]
        bias = jnp.concatenate([bsf_ref[...]] * nb, axis=0)
        dh = dh + dz_cols(C_U, D_A, dya * (f_s[...] + bias) * _gelu_grad(u, tu))
        dvn = dvn_s[...]
        dlng_ref[...] += _colsum8(dvn * xh)
        dlnb_ref[...] += _colsum8(dvn)
        dxh = dvn * lng_ref[...]
        dvg = rstd * (dxh - jnp.mean(dxh, axis=-1, keepdims=True) - xh * jnp.mean(dxh * xh, axis=-1, keepdims=True))
        dh = dh + dz_cols(C_V, D_A, dvg * _gelu_grad(v, tv))

        cg = zc(C_CG, D_B)
        hbv = zc(C_HB, D_B)
        p = cg * hbv
        dh = dh + dz_cols(C_BG, D_B, dyb * q_ref[...].astype(F32))
        dq = dyb * zc(C_BG, D_B)
        cr = carry[...]
        dq1 = _shift_up(dq, cr, 1)
        dq2 = _shift_up(dq, cr, 2)
        carry[...] = dq[0:8, :]
        dwsc_ref[0] += _colsum8(dq2 * p)
        dwsc_ref[1] += _colsum8(dq1 * p)
        dwsc_ref[2] += _colsum8(dq * p)
        dp = wsc_ref[2:3, :] * dq + wsc_ref[1:2, :] * dq1 + wsc_ref[0:1, :] * dq2
        dh = dh + dz_cols(C_CG, D_B, dp * hbv)
        dh = dh + dz_cols(C_HB, D_B, dp * cg)

        xv = x_ref[...]
        r = lax.rsqrt(jnp.mean(xv * xv, axis=-1, keepdims=True) + RMS_EPS)
        xn = xv * r
        dg1_ref[...] += _colsum8(dh * xn)
        dxn = dh * g1_ref[...]
        dx_ref[...] = dx2_ref[...] + r * (dxn - xn * jnp.mean(dxn * xn, axis=-1, keepdims=True))

    outs = [
        jax.ShapeDtypeStruct((t_len, D_MODEL), F32),
        jax.ShapeDtypeStruct((t_len, D_IN), BF16),
        jax.ShapeDtypeStruct((t_len, D_MODEL), BF16),
        jax.ShapeDtypeStruct((t_len, D_MODEL), BF16),
        jax.ShapeDtypeStruct((t_len, D_MODEL), BF16),
        jax.ShapeDtypeStruct((8, D_MODEL), F32),
        jax.ShapeDtypeStruct((8, 2 * D_MODEL), F32),
        jax.ShapeDtypeStruct((8, D_A), F32),
        jax.ShapeDtypeStruct((8, D_A), F32),
        jax.ShapeDtypeStruct((A_HEADS, 128, 128), F32),
        jax.ShapeDtypeStruct((128, D_A), F32),
        jax.ShapeDtypeStruct((3, 8, D_B), F32),
    ]

    return _staged_call(
        core, name=f"mixer_bwd_l{layer}", grid=(nt,),
        in_specs=[_row_spec(tm, D_MODEL, nt), _row_spec(tm, D_MODEL, nt), _row_spec(tm, D_IN, nt),
                  _row_spec(tm, D_B, nt), _row_spec(tm, D_MODEL, nt), _row_spec(tm, D_MODEL, nt),
                  _const_spec((1, D_MODEL)), _const_spec((1, 2 * D_MODEL)), _const_spec((1, D_A)), _const_spec((1, D_A)),
                  _const_spec((A_HEADS, 128, 128)), _const_spec((A_HEADS, 128, 128)), _const_spec((128, D_A)),
                  _const_spec((8, D_B)), ANY, ANY, ANY],
        out_specs=[_row_spec(tm, D_MODEL, nt), _row_spec(tm, D_IN, nt), _row_spec(tm, D_MODEL, nt),
                   _row_spec(tm, D_MODEL, nt), _row_spec(tm, D_MODEL, nt),
                   _const_spec((8, D_MODEL)), _const_spec((8, 2 * D_MODEL)), _const_spec((8, D_A)), _const_spec((8, D_A)),
                   _const_spec((A_HEADS, 128, 128)), _const_spec((128, D_A)), _const_spec((3, 8, D_B))],
        out_shape=outs,
        scratch_shapes=[pltpu.VMEM((D_MODEL, D_IN), BF16), pltpu.VMEM((2, D_A, D_MODEL), BF16),
                        pltpu.VMEM((D_MODEL, D_MODEL), BF16), pltpu.VMEM((8, D_B), F32),
                        pltpu.VMEM((tm, D_A), BF16), pltpu.VMEM((tm, D_A), F32), pltpu.VMEM((tm, D_A), BF16),
                        pltpu.VMEM((tm, D_A), F32), pltpu.SemaphoreType.DMA((16,))],
        args=[dx2, x, z, qs, av, bv, g1, bgate, lng, lnb, wm, wmt, bsf, wsc, win_g, wb_g, wout_g], stages=stages)


def _wgrad(name, layer, a, b, rows, cols, row_blk, col_blk, stages, a_first=0):
    t_len = a.shape[0]
    n = b.shape[1]
    tk = min(TK_WGRAD, t_len)
    col_sharded = n == N_CHIPS * cols
    m = rows if col_sharded else a.shape[1]
    grid = (m // row_blk, n // col_blk, t_len // tk)
    per_shard_c = cols // col_blk

    if col_sharded:
        out_shape = (N_CHIPS, rows, cols)
        out_spec = pl.BlockSpec((None, row_blk, col_blk), lambda i, j, k: (j // per_shard_c, i, j % per_shard_c))
    else:
        out_shape = (N_CHIPS * rows, cols)
        out_spec = pl.BlockSpec((row_blk, col_blk), lambda i, j, k: (i, j))

    def core(a_ref, b_ref, o_ref):
        @pl.when(pl.program_id(2) == 0)
        def _():
            o_ref[...] = jnp.zeros_like(o_ref)

        o_ref[...] += _dot_tn(a_ref[...], b_ref[...])

    own, outs = _staged_call(
        core, name=f"wgrad_{name}_l{layer}", grid=grid,
        in_specs=[pl.BlockSpec((tk, row_blk), lambda i, j, k: (k, a_first + i)),
                  pl.BlockSpec((tk, col_blk), lambda i, j, k: (k, j))],
        out_specs=[out_spec], out_shape=[jax.ShapeDtypeStruct(out_shape, F32)], scratch_shapes=[],
        args=[a, b], stages=stages)
    return [own[0].reshape(N_CHIPS, rows, cols)], outs


def _wgrad_branch(layer, ya, da, yb, db, stages):
    t_len = ya.shape[0]
    tk = min(TK_WGRAD, t_len)

    def core(ya_ref, da_ref, yb_ref, db_ref, o_ref):
        @pl.when(pl.program_id(1) == 0)
        def _():
            o_ref[...] = jnp.zeros_like(o_ref)

        o_ref[0:D_A, :] += _dot_tn(ya_ref[...], da_ref[...])
        o_ref[D_A:2 * D_A, :] += _dot_tn(yb_ref[...], db_ref[...])

    a_spec = pl.BlockSpec((tk, D_A), lambda j, k: (k, 0))
    d_spec = pl.BlockSpec((tk, 256), lambda j, k: (k, j))
    return _staged_call(
        core, name=f"wgrad_w_branch_l{layer}", grid=(N_CHIPS, t_len // tk),
        in_specs=[a_spec, d_spec, a_spec, d_spec],
        out_specs=[pl.BlockSpec((None, 2 * D_A, 256), lambda j, k: (j, 0, 0))],
        out_shape=[jax.ShapeDtypeStruct((N_CHIPS, 2 * D_A, 256), F32)], scratch_shapes=[],
        args=[ya, da, yb, db], stages=stages)


def _all_reduce_small(name, packed):
    rows = packed.shape[0]

    def body(src_ref, out_ref, slots, send, recv):
        x, y, c = _mesh_pos()
        me = 4 * x + 2 * y + c
        cps = []
        for d in range(1, N_DEVICES):
            peer = me ^ d
            cps.append(_remote(src_ref, slots.at[me], send.at[d - 1], recv.at[d - 1],
                               (peer // 4, (peer // 2) % 2, peer % 2)))
        _start_all(cps)
        slots[me] = src_ref[...]
        _wait_all(cps)
        acc = slots[0]
        for d in range(1, N_DEVICES):
            acc = acc + slots[d]
        out_ref[...] = acc

    return pl.pallas_call(
        body, name=f"all_reduce_{name}",
        in_specs=[pl.BlockSpec(memory_space=pltpu.VMEM)], out_specs=pl.BlockSpec(memory_space=pltpu.VMEM),
        out_shape=jax.ShapeDtypeStruct(packed.shape, F32),
        scratch_shapes=[pltpu.VMEM((N_DEVICES, rows, 128), F32), pltpu.SemaphoreType.DMA((7,)),
                        pltpu.SemaphoreType.DMA((7,))],
        compiler_params=pltpu.CompilerParams(vmem_limit_bytes=V7X_VMEM_LIMIT),
    )(packed)


def _flat_blk(rows, cols):
    blk = rows
    while blk * cols * 4 > 2 * 1024 * 1024 and blk % 16 == 0:
        blk //= 2
    return blk


def _cast_into_slot(name, layer, w, chip):
    _, rows, cols = w.shape
    blk = _flat_blk(rows, cols)

    def body(chip_ref, w_ref, o_ref):
        o_ref[...] = w_ref[...].astype(BF16)

    return pl.pallas_call(
        body, name=f"cast_{name}_l{layer}",
        grid_spec=pltpu.PrefetchScalarGridSpec(
            num_scalar_prefetch=1, grid=(rows // blk,),
            in_specs=[pl.BlockSpec((None, blk, cols), lambda i, chip_ref: (layer, i, 0))],
            out_specs=pl.BlockSpec((None, blk, cols), lambda i, chip_ref: (chip_ref[0], i, 0))),
        out_shape=jax.ShapeDtypeStruct((N_CHIPS, rows, cols), BF16),
        compiler_params=_params(("parallel",)),
    )(chip, w)


def _pair_sum(name, grad, other, core):
    _, h, cols = other.shape
    blk = _flat_blk(h, cols)
    nblk = h // blk

    def body(core_ref, g_ref, o_ref, s_ref):
        s_ref[...] = (g_ref[...] + o_ref[...]).astype(BF16)

    spec = pl.BlockSpec((None, blk, cols), lambda k, i, core_ref: (k, i, 0))
    return pl.pallas_call(
        body, name=f"pair_sum_{name}",
        grid_spec=pltpu.PrefetchScalarGridSpec(
            num_scalar_prefetch=1, grid=(N_CHIPS, nblk),
            in_specs=[pl.BlockSpec((None, blk, cols), lambda k, i, core_ref: (k, core_ref[0] * nblk + i, 0)), spec],
            out_specs=spec),
        out_shape=jax.ShapeDtypeStruct((N_CHIPS, h, cols), BF16),
        compiler_params=_params(("parallel", "parallel")),
    )(core, grad, other)


def _chip_sum(name, grad, other, got, pos):
    _, rows, cols = grad.shape
    h = rows // 2
    blk = _flat_blk(h, cols)
    nblk = h // blk

    def body(pos_ref, g_ref, o_ref, r_ref, f_ref):
        f_ref[...] = (((g_ref[...] + o_ref[...]) + r_ref[0].astype(F32)) + r_ref[1].astype(F32)) + r_ref[2].astype(F32)

    return pl.pallas_call(
        body, name=f"chip_sum_{name}",
        grid_spec=pltpu.PrefetchScalarGridSpec(
            num_scalar_prefetch=1, grid=(nblk,),
            in_specs=[pl.BlockSpec((None, blk, cols), lambda i, pos_ref: (pos_ref[0], pos_ref[1] * nblk + i, 0)),
                      pl.BlockSpec((None, blk, cols), lambda i, pos_ref: (pos_ref[0], i, 0)),
                      pl.BlockSpec((3, blk, cols), lambda i, pos_ref: (0, i, 0))],
            out_specs=pl.BlockSpec((blk, cols), lambda i, pos_ref: (pos_ref[1] * nblk + i, 0))),
        out_shape=jax.ShapeDtypeStruct((rows, cols), F32),
        compiler_params=_params(("parallel",)),
    )(pos, grad, other, got)


def _sum_slots(name, slots):
    n, rows, _ = slots.shape

    def body(s_ref, o_ref):
        acc = s_ref[0]
        for d in range(1, n):
            acc = acc + s_ref[d]
        o_ref[...] = acc

    return pl.pallas_call(
        body, name=f"sum_slots_{name}", grid=(1,),
        in_specs=[pl.BlockSpec((n, rows, 128), lambda i: (0, 0, 0))],
        out_specs=pl.BlockSpec((rows, 128), lambda i: (0, 0)),
        out_shape=jax.ShapeDtypeStruct((rows, 128), F32),
        compiler_params=_params(),
    )(slots)


def _adamw_math(w, g, m, v):
    m2 = ADAM_B1 * m + (1.0 - ADAM_B1) * g
    v2 = ADAM_B2 * v + (1.0 - ADAM_B2) * (g * g)
    m_hat = m2 / (1.0 - ADAM_B1 ** ADAM_STEP)
    v_hat = v2 / (1.0 - ADAM_B2 ** ADAM_STEP)
    delta = -ADAM_LR * (m_hat / (jnp.sqrt(v_hat) + ADAM_EPS) + ADAM_WD * w)
    return delta, m2, v2


def _adamw_big(name, w, g0, g1, m, v):
    _, rows, cols = w.shape
    blk = _flat_blk(rows, cols) // 2

    def body(w_ref, g0_ref, g1_ref, m_ref, v_ref, g_ref, d_ref, m2_ref, v2_ref):
        g = jnp.where(pl.program_id(0) == 0, g0_ref[...], g1_ref[...])
        d, m2, v2 = _adamw_math(w_ref[...], g, m_ref[...], v_ref[...])
        g_ref[...] = g
        d_ref[...] = d
        m2_ref[...] = m2
        v2_ref[...] = v2

    spec = pl.BlockSpec((None, blk, cols), lambda la, i: (la, i, 0))
    return pl.pallas_call(
        body, name=f"adamw_{name}", grid=(N_LAYERS, rows // blk),
        in_specs=[spec, pl.BlockSpec((blk, cols), lambda la, i: (i * (1 - la), 0)),
                  pl.BlockSpec((blk, cols), lambda la, i: (i * la, 0)), spec, spec],
        out_specs=[spec] * 4,
        out_shape=[jax.ShapeDtypeStruct(w.shape, F32)] * 4,
        compiler_params=_params(("parallel", "parallel")),
    )(w, g0, g1, m, v)


def _adamw(name, w, g, m, v):
    rows, cols = w.shape
    blk = _flat_blk(rows, cols)

    def body(w_ref, g_ref, m_ref, v_ref, d_ref, m2_ref, v2_ref):
        d, m2, v2 = _adamw_math(w_ref[...], g_ref[...], m_ref[...], v_ref[...])
        d_ref[...] = d
        m2_ref[...] = m2
        v2_ref[...] = v2

    spec = pl.BlockSpec((blk, cols), lambda i: (i, 0))
    return pl.pallas_call(
        body, name=f"adamw_{name}", grid=(rows // blk,),
        in_specs=[spec] * 4, out_specs=[spec] * 3,
        out_shape=[jax.ShapeDtypeStruct((rows, cols), F32)] * 3,
        compiler_params=_params(("parallel",)),
    )(w, g, m, v)


SMALL = ("norm1_g", "b_gate", "gmlp_ln_g", "gmlp_ln_b", "w_spatial", "b_spatial", "w_shortconv", "norm2_g",
         "w_ffn_conv", "b_ffn_conv", "final_g")
ALL_WEIGHTS = ("norm1_g", "w_in", "b_gate", "gmlp_ln_g", "gmlp_ln_b", "w_spatial", "b_spatial", "w_shortconv",
               "w_branch", "w_out", "norm2_g", "w_ffn_up", "w_ffn_conv", "b_ffn_conv", "w_ffn_down", "final_g")


def _pack(arrays):
    flat = jnp.concatenate([a.reshape(-1) for a in arrays])
    n = flat.shape[0]
    rows = -(-n // 1024) * 8
    return jnp.pad(flat, (0, rows * 128 - n)).reshape(rows, 128)


def _unpack(packed, like):
    flat = packed.reshape(-1)
    out, off = [], 0
    for a in like:
        out.append(flat[off:off + a.size].reshape(a.shape))
        off += a.size
    return out


def _pad8(w):
    return jnp.pad(w, ((0, 5), (0, 0)))


def kernel(x, norm1_g, w_in, b_gate, gmlp_ln_g, gmlp_ln_b, w_spatial, b_spatial, w_shortconv, w_branch, w_out, norm2_g, w_ffn_up, w_ffn_conv, b_ffn_conv, w_ffn_down, final_g, loss_target, m_norm1_g, m_w_in, m_b_gate, m_gmlp_ln_g, m_gmlp_ln_b, m_w_spatial, m_b_spatial, m_w_shortconv, m_w_branch, m_w_out, m_norm2_g, m_w_ffn_up, m_w_ffn_conv, m_b_ffn_conv, m_w_ffn_down, m_final_g, v_norm1_g, v_w_in, v_b_gate, v_gmlp_ln_g, v_gmlp_ln_b, v_w_spatial, v_b_spatial, v_w_shortconv, v_w_branch, v_w_out, v_norm2_g, v_w_ffn_up, v_w_ffn_conv, v_b_ffn_conv, v_w_ffn_down, v_final_g):
    weights = dict(norm1_g=norm1_g, w_in=w_in, b_gate=b_gate, gmlp_ln_g=gmlp_ln_g, gmlp_ln_b=gmlp_ln_b,
                   w_spatial=w_spatial, b_spatial=b_spatial, w_shortconv=w_shortconv, w_branch=w_branch, w_out=w_out,
                   norm2_g=norm2_g, w_ffn_up=w_ffn_up, w_ffn_conv=w_ffn_conv, b_ffn_conv=b_ffn_conv,
                   w_ffn_down=w_ffn_down, final_g=final_g)
    mom = dict(norm1_g=m_norm1_g, w_in=m_w_in, b_gate=m_b_gate, gmlp_ln_g=m_gmlp_ln_g, gmlp_ln_b=m_gmlp_ln_b,
               w_spatial=m_w_spatial, b_spatial=m_b_spatial, w_shortconv=m_w_shortconv, w_branch=m_w_branch,
               w_out=m_w_out, norm2_g=m_norm2_g, w_ffn_up=m_w_ffn_up, w_ffn_conv=m_w_ffn_conv,
               b_ffn_conv=m_b_ffn_conv, w_ffn_down=m_w_ffn_down, final_g=m_final_g)
    vel = dict(norm1_g=v_norm1_g, w_in=v_w_in, b_gate=v_b_gate, gmlp_ln_g=v_gmlp_ln_g, gmlp_ln_b=v_gmlp_ln_b,
               w_spatial=v_w_spatial, b_spatial=v_b_spatial, w_shortconv=v_w_shortconv, w_branch=v_w_branch,
               w_out=v_w_out, norm2_g=v_norm2_g, w_ffn_up=v_w_ffn_up, w_ffn_conv=v_w_ffn_conv,
               b_ffn_conv=v_b_ffn_conv, w_ffn_down=v_w_ffn_down, final_g=v_final_g)

    cx, cy, cc = _mesh_pos()
    chip = 2 * cx + cy
    core_arr = cc.astype(jnp.int32).reshape(1)
    chip_arr = chip.astype(jnp.int32).reshape(1)
    pos_arr = jnp.stack([chip, cc]).astype(jnp.int32)
    t_len = x.shape[1]
    xs = x.reshape(t_len, D_MODEL)
    target = loss_target.reshape(t_len, D_MODEL)
    pipe = _Pipe()

    full = {}

    def gather(keys):
        slots = [_cast_into_slot(n, la, weights[n].reshape((N_LAYERS,) + BIG[n]), chip_arr) for n, la in keys]

        def then(*bufs):
            full.update(zip(keys, bufs))

        pipe.add(_gather_stage(slots, then))

    mixer_w = ("w_in", "w_branch", "w_out")
    ffn_w = ("w_ffn_up", "w_ffn_down")
    gather([(n, 0) for n in mixer_w])
    pipe.flush()

    idx = jnp.arange(GMLP_BLOCK) // CHUNK
    mask = idx[None, :] <= idx[:, None]
    wm_all = jnp.where(mask[None, None], w_spatial, 0.0)
    wm_bf = wm_all.astype(BF16)
    wmt_bf = jnp.swapaxes(wm_all, -1, -2).astype(BF16)
    bsf = jnp.repeat(jnp.swapaxes(b_spatial, -1, -2), 128, axis=-1)
    wsc_full = lax.dynamic_update_slice(jnp.zeros((N_LAYERS, 3, D_B), F32), w_shortconv, (0, 0, chip * (D_B // 4)))
    wfc_full = lax.dynamic_update_slice(jnp.zeros((N_LAYERS, 3, D_FF), F32), w_ffn_conv, (0, 0, chip * (D_FF // 4)))
    taps = _all_reduce_small("conv_taps", _pack([wsc_full, wfc_full]))
    wsc_full, wfc_full = _unpack(taps * 0.5, [wsc_full, wfc_full])

    def row(a):
        return a.reshape(1, -1)

    def mixer_args(la):
        return (row(norm1_g[la]), row(b_gate[la]), row(gmlp_ln_g[la]), row(gmlp_ln_b[la]))

    def mixer_weights(la):
        return tuple(full[(n, la)] for n in mixer_w)

    def ffn_weights(la):
        return tuple(full[(n, la)] for n in ffn_w)

    saved = []
    h_in = xs
    for la in range(N_LAYERS):
        gather([(n, la) for n in ffn_w])
        z, ya, yb, qs, av, bv, mg, h1, x2 = pipe.carry(lambda st: _mixer_fwd(
            la, h_in, *mixer_args(la), wm_bf[la], bsf[la], _pad8(wsc_full[la]), *mixer_weights(la), st))
        if la + 1 < N_LAYERS:
            gather([(n, la + 1) for n in mixer_w])
        up, silu, dsilu, act, h2, x3 = pipe.carry(lambda st: _ffn_fwd(
            la, x2, row(norm2_g[la]), _pad8(wfc_full[la]), row(b_ffn_conv[la]), *ffn_weights(la), st))
        saved.append(dict(x=h_in, z=z, ya=ya, yb=yb, q=qs, av=av, bv=bv, mg=mg, h1=h1, x2=x2, up=up, silu=silu, dsilu=dsilu, act=act,
                          h2=h2))
        h_in = x3

    reduced_big = {}

    def reduce_big(name, la, grad):
        tag = f"{name}_l{la}"

        def after_pair(other):
            psum = _pair_sum(tag, grad, other, core_arr)

            def after_chips(got):
                final = _chip_sum(tag, grad, other, got, pos_arr)
                pipe.add(_pair_fill_stage(final, lambda done: reduced_big.__setitem__((name, la), done)))

            pipe.add(_chip_send_stage(psum, after_chips))

        pipe.add(_pair_send_stage(grad, after_pair))

    dx, dgf8, loss8 = _loss_head(h_in, target, row(final_g))
    small = {n: [None] * N_LAYERS for n in SMALL}
    spread = {}
    for la in reversed(range(N_LAYERS)):
        s = saved[la]
        dx3 = dx
        dx2, dup, dx3b, dg2, dbfc, dwfc = pipe.carry(lambda st: _ffn_bwd(
            la, dx3, s["x2"], s["up"], s["silu"], s["dsilu"], row(norm2_g[la]), _pad8(wfc_full[la]),
            *ffn_weights(la), st))
        g, = pipe.carry(lambda st: _wgrad("w_ffn_down", la, s["act"], dx3b, 704, 1024, 1408, 1024, st))
        reduce_big("w_ffn_down", la, g)
        g, = pipe.carry(lambda st: _wgrad("w_ffn_up", la, s["h2"], dup, 1024, 1408, 1024, 1408, st))
        reduce_big("w_ffn_up", la, g)
        dxl, dz, da, db, dx2b, dg1, dbg, dlng, dlnb, dwm, dbsf, dwsc = pipe.carry(lambda st: _mixer_bwd(
            la, dx2, s["x"], s["z"], s["q"], s["av"], s["bv"], *mixer_args(la), wm_bf[la], wmt_bf[la], bsf[la],
            _pad8(wsc_full[la]), *mixer_weights(la), st))
        small["norm1_g"][la] = dg1.sum(0)
        small["b_gate"][la] = dbg.sum(0)
        small["gmlp_ln_g"][la] = dlng.sum(0)
        small["gmlp_ln_b"][la] = dlnb.sum(0)
        small["w_spatial"][la] = jnp.where(mask[None], dwm, 0.0)
        small["b_spatial"][la] = dbsf.reshape(128, A_HEADS, 128).sum(-1).T
        small["w_shortconv"][la] = dwsc.sum(1)
        small["norm2_g"][la] = dg2.sum(0)
        small["w_ffn_conv"][la] = dwfc.sum(1)
        small["b_ffn_conv"][la] = dbfc.sum(0)
        if la == 0:
            small_local = ([jnp.stack(small[n]) for n in SMALL[:-1]]
                           + [dgf8.sum(0), 0.5 * loss8.sum().reshape(1) / D_MODEL])
            mine = _pack(small_local)

            def after_swap(other, mine=mine):
                pair = _sum_slots("small_pair", jnp.stack([mine, other]))
                pipe.add(_chip_spread_stage(pair, lambda slots: spread.__setitem__("slots", slots)))

            pipe.add(_pair_swap_stage(mine, after_swap))
        for part, tag in enumerate(("w_in_a", "w_in_b")):
            g, = pipe.carry(lambda st: _wgrad(tag, la, s["h1"], dz, 512, 1152, 512, 1152, st, a_first=part))
            reduce_big(tag, la, g)
        g, = pipe.carry(lambda st: _wgrad("w_out", la, s["mg"], dx2b, 256, 1024, 1024, 1024, st), long=False)
        reduce_big("w_out", la, g)
        g, = pipe.carry(lambda st: _wgrad_branch(la, s["ya"], da, s["yb"], db, st), long=False)
        reduce_big("w_branch", la, g)
        dx = dxl
    grad_x = dx.reshape(x.shape)
    pipe.flush()

    for la in range(N_LAYERS):
        reduced_big[("w_in", la)] = jnp.concatenate([reduced_big[("w_in_a", la)], reduced_big[("w_in_b", la)]], axis=0)
    reduced = _unpack(_sum_slots("small_grads", spread["slots"]), small_local)
    loss = reduced[-1].reshape(())
    grads = dict(zip(SMALL, reduced[:-1]))
    grads["w_shortconv"] = lax.dynamic_slice(grads["w_shortconv"], (0, 0, chip * (D_B // 4)), (N_LAYERS, 3, D_B // 4))
    grads["w_ffn_conv"] = lax.dynamic_slice(grads["w_ffn_conv"], (0, 0, chip * (D_FF // 4)), (N_LAYERS, 3, D_FF // 4))

    delta, new_m, new_v = {}, {}, {}
    for n in BIG_NAMES:
        shape3 = (N_LAYERS,) + BIG[n]
        res = _adamw_big(n, weights[n].reshape(shape3), reduced_big[(n, 0)], reduced_big[(n, 1)],
                         mom[n].reshape(shape3), vel[n].reshape(shape3))
        grads[n], delta[n], new_m[n], new_v[n] = (a.reshape(weights[n].shape) for a in res)
    small_w = [weights[n] for n in SMALL]
    packed = [_pack([src[n] for n in SMALL]) for src in (weights, grads, mom, vel)]
    for dst, res in zip((delta, new_m, new_v), _adamw("small", *packed)):
        dst.update(zip(SMALL, _unpack(res, small_w)))

    return (loss, grad_x, *[grads[n] for n in ALL_WEIGHTS], *[delta[n] for n in ALL_WEIGHTS],
            *[new_m[n] for n in ALL_WEIGHTS], *[new_v[n] for n in ALL_WEIGHTS])
```

```python
import jax
import jax.numpy as jnp
from jax import lax
from jax.experimental import pallas as pl
from jax.experimental.pallas import tpu as pltpu

F32 = jnp.float32
BF16 = jnp.bfloat16
MESH = pl.DeviceIdType.MESH
ANY = pl.BlockSpec(memory_space=pl.ANY)

D_MODEL = 1024
D_A = 512
D_B = 512
D_IN = 4608
D_FF = 2816
GMLP_BLOCK = 128
CHUNK = 64
A_HEADS = 4
N_LAYERS = 2
N_CHIPS = 4
N_DEVICES = 8
RMS_EPS = 1e-6
LN_EPS = 1e-5
ADAM_LR = 0.001
ADAM_B1 = 0.9
ADAM_B2 = 0.999
ADAM_EPS = 1e-08
ADAM_WD = 0.01
ADAM_STEP = 10

C_U, C_V, C_BG, C_CG, C_HB, C_GA, C_GB = 0, 512, 1024, 1536, 2048, 2560, 3584

V7X_VMEM_LIMIT = 60 * 1024 * 1024
TM_MIX = 256
TM_FFN = 256
TM_EW = 512
TK_WGRAD = 2048
SLOW_COPY_BYTES = 640 * 1024
FF_CHUNKS = ((0, 768), (768, 1536), (1536, 2304), (2304, 2816))
GELU_C0 = 0.7978845608028654
GELU_C1 = 0.044715

BIG = {
    "w_in": (1024, 1152),
    "w_branch": (1024, 256),
    "w_out": (256, 1024),
    "w_ffn_up": (1024, 1408),
    "w_ffn_down": (704, 1024),
}
BIG_NAMES = tuple(BIG)


def _params(sem=("arbitrary",), vmem=V7X_VMEM_LIMIT):
    return pltpu.CompilerParams(dimension_semantics=sem, vmem_limit_bytes=vmem)


def _gelu(x):
    x2 = x * x
    t = jnp.tanh(GELU_C0 * x * (1.0 + GELU_C1 * x2))
    return 0.5 * x * (1.0 + t), t


def _gelu_grad(x, t):
    return 0.5 * (1.0 + t) + 0.5 * x * (1.0 - t * t) * GELU_C0 * (1.0 + 3.0 * GELU_C1 * x * x)


def _colsum8(v):
    r, n = v.shape
    return v.reshape(r // 8, 8, n).sum(axis=0)


def _dot(a, b):
    return jnp.dot(a, b, preferred_element_type=F32)


def _dot_nt(a, b):
    return lax.dot_general(a, b, (((1,), (1,)), ((), ())), preferred_element_type=F32)


def _dot_tn(a, b):
    return lax.dot_general(a, b, (((0,), (0,)), ((), ())), preferred_element_type=F32)


def _shift_down(v, carry, n):
    rows = lax.broadcasted_iota(jnp.int32, (8, v.shape[1]), 0)
    out = pltpu.roll(v, n, 0)
    head = out[0:8, :]
    for r in range(n):
        head = jnp.where(rows == r, carry[8 - n + r:8 - n + r + 1, :], head)
    return jnp.concatenate([head, out[8:, :]], axis=0)


def _shift_up(v, carry, n):
    tm = v.shape[0]
    rows = lax.broadcasted_iota(jnp.int32, (8, v.shape[1]), 0)
    out = pltpu.roll(v, tm - n, 0)
    tail = out[tm - 8:tm, :]
    for r in range(n):
        tail = jnp.where(rows == 8 - n + r, carry[r:r + 1, :], tail)
    return jnp.concatenate([out[0:tm - 8, :], tail], axis=0)


def _sigmoid(x):
    return 0.5 * jnp.tanh(0.5 * x) + 0.5


def _start_all(copies):
    for cp in copies:
        cp.start()


def _wait_all(copies):
    for cp in copies:
        cp.wait()


def _load_col_sharded(src, dst, sems, first):
    cs = src.shape[-1]
    return [pltpu.make_async_copy(src.at[k], dst.at[:, k * cs:(k + 1) * cs], sems.at[first + k])
            for k in range(N_CHIPS)]


def _load_row_sharded(src, dst, sems, first):
    rs = src.shape[-2]
    return [pltpu.make_async_copy(src.at[k], dst.at[k * rs:(k + 1) * rs, :], sems.at[first + k])
            for k in range(N_CHIPS)]


def _load_branch(src, dst, sems, first):
    return [pltpu.make_async_copy(src.at[k, pl.ds(m * D_A, D_A), :], dst.at[m, :, k * 256:(k + 1) * 256],
                                  sems.at[first + 2 * k + m])
            for k in range(N_CHIPS) for m in range(2)]


def _row_spec(tm, n, rev=None):
    if rev is None:
        return pl.BlockSpec((tm, n), lambda i: (i, 0))
    return pl.BlockSpec((tm, n), lambda i: (rev - 1 - i, 0))


def _const_spec(shape):
    nd = len(shape)
    return pl.BlockSpec(shape, lambda i: (0,) * nd)


def _mesh_pos():
    return lax.axis_index("x"), lax.axis_index("y"), lax.axis_index("c")


def _other_chips(x, y):
    return [(1 - x, y, 2 * (1 - x) + y), (x, 1 - y, 2 * x + (1 - y)), (1 - x, 1 - y, 2 * (1 - x) + (1 - y))]


def _remote(src, dst, ssem, rsem, to):
    return pltpu.make_async_remote_copy(src_ref=src, dst_ref=dst, send_sem=ssem, recv_sem=rsem, device_id=to,
                                        device_id_type=MESH)


def _half(ref, which, h):
    start = pl.multiple_of(which * h, 8)
    if len(ref.shape) == 2:
        return ref.at[pl.ds(start, h), :]
    return ref.at[:, pl.ds(start, h), :]


class _Stage:
    def __init__(self, ins=(), inouts=(), outs=(), n_sems=0, start=None, mid=None, finish=None, then=None, slow=False):
        self.ins, self.inouts, self.outs = list(ins), list(inouts), list(outs)
        self.n_sems, self.start, self.mid, self.finish, self.then = n_sems, start, mid, finish, then
        self.slow = slow


def _gather_stage(bufs, then):
    n = len(bufs)

    def copies(io, sem):
        x, y, c = _mesh_pos()
        me = 2 * x + y
        ici, fwd, got = [], [], []
        for w in range(n):
            h = io[w].shape[1] // 2
            for j, (px, py, pk) in enumerate(_other_chips(x, y)):
                mine = _half(io[w].at[me], c, h)
                theirs = _half(io[w].at[pk], c, h)
                ici.append(_remote(mine, mine, sem(12 * w + j), sem(12 * w + 3 + j), (px, py, c)))
                got.append(_remote(theirs, theirs, sem(12 * w + j), sem(12 * w + 3 + j), (px, py, c)))
                fwd.append(_remote(theirs, theirs, sem(12 * w + 6 + j), sem(12 * w + 9 + j), (x, y, 1 - c)))
        return ici, got, fwd

    def start(ins, io, outs, sem):
        _start_all(copies(io, sem)[0])

    def mid(ins, io, outs, sem):
        _, got, fwd = copies(io, sem)
        for g, f in zip(got, fwd):
            g.wait_recv()
            f.start()

    def finish(ins, io, outs, sem):
        x, y, c = _mesh_pos()
        ici, _, fwd = copies(io, sem)
        for w in range(n):
            h = io[w].shape[1] // 2
            for j, (px, py, pk) in enumerate(_other_chips(x, y)):
                other = _half(io[w].at[pk], 1 - c, h)
                _remote(other, other, sem(12 * w + 6 + j), sem(12 * w + 9 + j), (x, y, 1 - c)).wait_recv()
        for cp in ici + fwd:
            cp.wait_send()

    return _Stage(inouts=bufs, n_sems=12 * n, start=start, mid=mid, finish=finish, then=then)


def _pair_send_stage(grad, then):
    h = grad.shape[1] // 2

    def copy(ins, outs, sem):
        x, y, c = _mesh_pos()
        return _remote(_half(ins[0], 1 - c, h), outs[0], sem(0), sem(1), (x, y, 1 - c))

    return _Stage(ins=[grad], outs=[jax.ShapeDtypeStruct((N_CHIPS, h, grad.shape[2]), F32)], n_sems=2,
                  start=lambda ins, io, outs, sem: copy(ins, outs, sem).start(),
                  finish=lambda ins, io, outs, sem: copy(ins, outs, sem).wait(), then=then)


def _chip_send_stage(psum, then):
    def copies(ins, outs, sem):
        x, y, c = _mesh_pos()
        return [_remote(ins[0].at[pk], outs[0].at[j], sem(j), sem(3 + j), (px, py, c))
                for j, (px, py, pk) in enumerate(_other_chips(x, y))]

    return _Stage(ins=[psum], outs=[jax.ShapeDtypeStruct((3,) + psum.shape[1:], BF16)], n_sems=6,
                  start=lambda ins, io, outs, sem: _start_all(copies(ins, outs, sem)),
                  finish=lambda ins, io, outs, sem: _wait_all(copies(ins, outs, sem)), then=then,
                  slow=psum.shape[1] * psum.shape[2] * 2 > SLOW_COPY_BYTES)


def _pair_fill_stage(final, then):
    h = final.shape[0] // 2

    def copy(io, sem):
        x, y, c = _mesh_pos()
        mine = _half(io[0], c, h)
        return _remote(mine, mine, sem(0), sem(1), (x, y, 1 - c))

    return _Stage(inouts=[final], n_sems=2,
                  start=lambda ins, io, outs, sem: copy(io, sem).start(),
                  finish=lambda ins, io, outs, sem: copy(io, sem).wait(), then=then)


def _pair_swap_stage(packed, then):
    def copy(ins, outs, sem):
        x, y, c = _mesh_pos()
        return _remote(ins[0], outs[0], sem(0), sem(1), (x, y, 1 - c))

    return _Stage(ins=[packed], outs=[jax.ShapeDtypeStruct(packed.shape, F32)], n_sems=2,
                  start=lambda ins, io, outs, sem: copy(ins, outs, sem).start(),
                  finish=lambda ins, io, outs, sem: copy(ins, outs, sem).wait(), then=then)


def _chip_spread_stage(psum, then):
    def copies(ins, outs, sem):
        x, y, c = _mesh_pos()
        me = 2 * x + y
        cps = [_remote(ins[0], outs[0].at[me], sem(j), sem(3 + j), (px, py, c))
               for j, (px, py, pk) in enumerate(_other_chips(x, y))]
        return cps, pltpu.make_async_copy(ins[0], outs[0].at[me], sem(6))

    def start(ins, io, outs, sem):
        cps, own = copies(ins, outs, sem)
        own.start()
        _start_all(cps)

    def finish(ins, io, outs, sem):
        cps, own = copies(ins, outs, sem)
        _wait_all(cps)
        own.wait()

    return _Stage(ins=[psum], outs=[jax.ShapeDtypeStruct((N_CHIPS,) + psum.shape, F32)], n_sems=7,
                  start=start, finish=finish, then=then)


def _staged_call(core, *, name, grid, in_specs, out_specs, out_shape, scratch_shapes, args, stages):
    n_in, n_out, n_scr = len(args), len(out_shape), len(scratch_shapes)
    s_args, s_outs, aliases, layout = [], [], {}, []
    n_sems = 0
    for st in stages:
        i0, o0 = len(s_args), len(s_outs)
        s_args += st.ins + st.inouts
        for q in range(len(st.inouts)):
            aliases[n_in + i0 + len(st.ins) + q] = n_out + o0 + q
        s_outs += [jax.ShapeDtypeStruct(a.shape, a.dtype) for a in st.inouts] + st.outs
        layout.append((i0, o0, n_sems))
        n_sems += st.n_sems
    steps = 1
    for g in grid:
        steps *= g

    def body(*refs):
        own_in = refs[:n_in]
        s_in = refs[n_in:n_in + len(s_args)]
        rest = refs[n_in + len(s_args):]
        own_out = rest[:n_out]
        s_out = rest[n_out:n_out + len(s_outs)]
        scr = rest[n_out + len(s_outs):]

        def run(which):
            for st, (i0, o0, s0) in zip(stages, layout):
                fn = getattr(st, which)
                if fn is not None:
                    fn(s_in[i0:i0 + len(st.ins)], s_out[o0:o0 + len(st.inouts)],
                       s_out[o0 + len(st.inouts):o0 + len(st.inouts) + len(st.outs)],
                       lambda k, s0=s0: scr[n_scr].at[s0 + k])

        if not stages:
            core(*own_in, *own_out, *scr[:n_scr])
            return
        step = 0
        for d, g in enumerate(grid):
            step = step * g + pl.program_id(d)
        if steps == 1:
            run("start")
            core(*own_in, *own_out, *scr[:n_scr])
            run("mid")
            run("finish")
            return
        pl.when(step == 0)(lambda: run("start"))
        core(*own_in, *own_out, *scr[:n_scr])
        pl.when(step == (3 * steps) // 4)(lambda: run("mid"))
        pl.when(step == steps - 1)(lambda: run("finish"))

    sem = ("arbitrary",) * len(grid) if stages else ("parallel",) * max(len(grid) - 1, 0) + ("arbitrary",) * min(len(grid), 1)
    res = pl.pallas_call(
        body, name=name, grid=grid,
        in_specs=list(in_specs) + [ANY] * len(s_args),
        out_specs=list(out_specs) + [ANY] * len(s_outs),
        out_shape=list(out_shape) + s_outs,
        input_output_aliases=aliases,
        scratch_shapes=list(scratch_shapes) + ([pltpu.SemaphoreType.DMA((n_sems,))] if stages else []),
        compiler_params=_params(sem) if grid else pltpu.CompilerParams(vmem_limit_bytes=V7X_VMEM_LIMIT),
    )(*args, *s_args)
    return list(res[:n_out]), list(res[n_out:])


class _Pipe:
    def __init__(self):
        self.ready = []
        self.flushes = 0

    def add(self, stage):
        self.ready.append(stage)

    def carry(self, call, long=True):
        stages = [st for st in self.ready if long or not st.slow]
        self.ready = [st for st in self.ready if not (long or not st.slow)]
        own, outs = call(stages)
        k = 0
        for st in stages:
            n = len(st.inouts) + len(st.outs)
            st.then(*outs[k:k + n])
            k += n
        return own

    def flush(self):
        while self.ready:
            self.flushes += 1
            self.carry(lambda stages: _staged_call(
                lambda *refs: None, name=f"comm_tail_{self.flushes}", grid=(), in_specs=[], out_specs=[], out_shape=[],
                scratch_shapes=[], args=[], stages=stages))


def _mixer_fwd(layer, x, g1, bgate, lng, lnb, wm, bsf, wsc, win_g, wb_g, wout_g, stages):
    t_len = x.shape[0]
    tm = min(TM_MIX, t_len)
    nt = t_len // tm
    nb = tm // GMLP_BLOCK

    def core(x_ref, g1_ref, bgate_ref, lng_ref, lnb_ref, wm_ref, bsf_ref, wsc_ref, win_hbm, wb_hbm, wout_hbm,
             z_ref, ya_ref, yb_ref, q_ref, a_ref, b_ref, mg_ref, h_ref, x2_ref,
             win_v, wb_v, wout_v, carry, vn_s, f_s, sems):
        i = pl.program_id(0)

        @pl.when(i == 0)
        def _():
            cps = (_load_col_sharded(win_hbm, win_v, sems, 0) + _load_branch(wb_hbm, wb_v, sems, 4)
                   + _load_row_sharded(wout_hbm, wout_v, sems, 12))
            _start_all(cps)
            carry[...] = jnp.zeros_like(carry)
            _wait_all(cps)

        xv = x_ref[...]
        r = lax.rsqrt(jnp.mean(xv * xv, axis=-1, keepdims=True) + RMS_EPS)
        h_ref[...] = (xv * r * g1_ref[...]).astype(BF16)

        def zcols(c0, c1):
            zc = _dot(h_ref[...], win_v[:, c0:c1])
            z_ref[:, c0:c1] = zc.astype(BF16)
            return zc

        vg, _ = _gelu(zcols(C_V, C_V + D_A))
        mu = jnp.mean(vg, axis=-1, keepdims=True)
        vc = vg - mu
        rstd = lax.rsqrt(jnp.mean(vc * vc, axis=-1, keepdims=True) + LN_EPS)
        vn_s[...] = (vc * rstd * lng_ref[...] + lnb_ref[...]).astype(BF16)
        for hd in range(A_HEADS):
            cols = slice(hd * 128, (hd + 1) * 128)
            vcat = jnp.concatenate([vn_s[b * 128:(b + 1) * 128, cols] for b in range(nb)], axis=1)
            fcat = _dot(wm_ref[hd], vcat)
            for b in range(nb):
                f_s[b * 128:(b + 1) * 128, cols] = fcat[:, b * 128:(b + 1) * 128]
        ug, _ = _gelu(zcols(C_U, C_U + D_A))
        bias = jnp.concatenate([bsf_ref[...]] * nb, axis=0)
        ya_ref[...] = (ug * (f_s[...] + bias)).astype(BF16)

        p = zcols(C_CG, C_CG + D_B) * zcols(C_HB, C_HB + D_B)
        cr = carry[...]
        q = wsc_ref[0:1, :] * _shift_down(p, cr, 2) + wsc_ref[1:2, :] * _shift_down(p, cr, 1) + wsc_ref[2:3, :] * p
        carry[...] = p[tm - 8:tm, :]
        q_ref[...] = q.astype(BF16)
        yb_ref[...] = (zcols(C_BG, C_BG + D_B) * q).astype(BF16)

        av = _dot(ya_ref[...], wb_v[0])
        a_ref[...] = av.astype(BF16)
        mg = _sigmoid(zcols(C_GA, C_GA + D_MODEL) + bgate_ref[:, 0:D_MODEL]) * av
        bv = _dot(yb_ref[...], wb_v[1])
        b_ref[...] = bv.astype(BF16)
        mg = mg + _sigmoid(zcols(C_GB, C_GB + D_MODEL) + bgate_ref[:, D_MODEL:2 * D_MODEL]) * bv
        mg_ref[...] = mg.astype(BF16)
        x2_ref[...] = x_ref[...] + _dot(mg_ref[...], wout_v[...])

    outs = [
        jax.ShapeDtypeStruct((t_len, D_IN), BF16),
        jax.ShapeDtypeStruct((t_len, D_A), BF16),
        jax.ShapeDtypeStruct((t_len, D_B), BF16),
        jax.ShapeDtypeStruct((t_len, D_B), BF16),
        jax.ShapeDtypeStruct((t_len, D_MODEL), BF16),
        jax.ShapeDtypeStruct((t_len, D_MODEL), BF16),
        jax.ShapeDtypeStruct((t_len, D_MODEL), BF16),
        jax.ShapeDtypeStruct((t_len, D_MODEL), BF16),
        jax.ShapeDtypeStruct((t_len, D_MODEL), F32),
    ]
    return _staged_call(
        core, name=f"mixer_fwd_l{layer}", grid=(nt,),
        in_specs=[_row_spec(tm, D_MODEL), _const_spec((1, D_MODEL)), _const_spec((1, 2 * D_MODEL)),
                  _const_spec((1, D_A)), _const_spec((1, D_A)), _const_spec((A_HEADS, 128, 128)),
                  _const_spec((128, D_A)), _const_spec((8, D_B)), ANY, ANY, ANY],
        out_specs=[_row_spec(tm, o.shape[1]) for o in outs],
        out_shape=outs,
        scratch_shapes=[pltpu.VMEM((D_MODEL, D_IN), BF16), pltpu.VMEM((2, D_A, D_MODEL), BF16),
                        pltpu.VMEM((D_MODEL, D_MODEL), BF16), pltpu.VMEM((8, D_B), F32),
                        pltpu.VMEM((tm, D_A), BF16), pltpu.VMEM((tm, D_A), F32), pltpu.SemaphoreType.DMA((16,))],
        args=[x, g1, bgate, lng, lnb, wm, bsf, wsc, win_g, wb_g, wout_g], stages=stages)


def _ffn_fwd(layer, x2, g2, wfc, bfc, wup_g, wdown_g, stages):
    t_len = x2.shape[0]
    tm = min(TM_FFN, t_len)
    nt = t_len // tm

    def core(x_ref, g2_ref, wfc_ref, bfc_ref, wup_hbm, wdown_hbm, up_ref, silu_ref, dsilu_ref, act_ref, h_ref, x3_ref,
             wup_v, wdown_v, carry, sems):
        i = pl.program_id(0)

        @pl.when(i == 0)
        def _():
            cps = _load_col_sharded(wup_hbm, wup_v, sems, 0) + _load_row_sharded(wdown_hbm, wdown_v, sems, 4)
            _start_all(cps)
            carry[...] = jnp.zeros_like(carry)
            _wait_all(cps)

        xv = x_ref[...]
        r = lax.rsqrt(jnp.mean(xv * xv, axis=-1, keepdims=True) + RMS_EPS)
        h_ref[...] = (xv * r * g2_ref[...]).astype(BF16)
        gate = _dot(h_ref[...], wup_v[:, 0:D_FF])
        up_ref[:, 0:D_FF] = gate.astype(BF16)
        cr = carry[...]
        gc = (wfc_ref[0:1, :] * _shift_down(gate, cr, 2) + wfc_ref[1:2, :] * _shift_down(gate, cr, 1)
              + wfc_ref[2:3, :] * gate + bfc_ref[...])
        carry[...] = gate[tm - 8:tm, :]
        sg = _sigmoid(gc)
        silu = gc * sg
        silu_ref[...] = silu.astype(BF16)
        dsilu_ref[...] = (sg + silu * (1.0 - sg)).astype(BF16)
        val = _dot(h_ref[...], wup_v[:, D_FF:2 * D_FF])
        up_ref[:, D_FF:2 * D_FF] = val.astype(BF16)
        act_ref[...] = (silu * val).astype(BF16)
        x3_ref[...] = x_ref[...] + _dot(act_ref[...], wdown_v[...])

    outs = [
        jax.ShapeDtypeStruct((t_len, 2 * D_FF), BF16),
        jax.ShapeDtypeStruct((t_len, D_FF), BF16),
        jax.ShapeDtypeStruct((t_len, D_FF), BF16),
        jax.ShapeDtypeStruct((t_len, D_FF), BF16),
        jax.ShapeDtypeStruct((t_len, D_MODEL), BF16),
        jax.ShapeDtypeStruct((t_len, D_MODEL), F32),
    ]
    return _staged_call(
        core, name=f"ffn_fwd_l{layer}", grid=(nt,),
        in_specs=[_row_spec(tm, D_MODEL), _const_spec((1, D_MODEL)), _const_spec((8, D_FF)), _const_spec((1, D_FF)), ANY, ANY],
        out_specs=[_row_spec(tm, o.shape[1]) for o in outs],
        out_shape=outs,
        scratch_shapes=[pltpu.VMEM((D_MODEL, 2 * D_FF), BF16), pltpu.VMEM((D_FF, D_MODEL), BF16),
                        pltpu.VMEM((8, D_FF), F32), pltpu.SemaphoreType.DMA((8,))],
        args=[x2, g2, wfc, bfc, wup_g, wdown_g], stages=stages)


def _loss_head(x3, target, gf):
    t_len = x3.shape[0]
    tm = min(TM_EW, t_len)
    nt = t_len // tm

    def body(x_ref, t_ref, gf_ref, dx_ref, dgf_ref, loss_ref):
        i = pl.program_id(0)

        @pl.when(i == 0)
        def _():
            dgf_ref[...] = jnp.zeros_like(dgf_ref)
            loss_ref[...] = jnp.zeros_like(loss_ref)

        xv = x_ref[...]
        r = lax.rsqrt(jnp.mean(xv * xv, axis=-1, keepdims=True) + RMS_EPS)
        xh = xv * r
        err = xh * gf_ref[...] - t_ref[...]
        loss_ref[...] += _colsum8(err * err)
        dy = err * (1.0 / D_MODEL)
        dgf_ref[...] += _colsum8(dy * xh)
        dxh = dy * gf_ref[...]
        dx_ref[...] = r * (dxh - xh * jnp.mean(dxh * xh, axis=-1, keepdims=True))

    return pl.pallas_call(
        body, name="loss_head", grid=(nt,),
        in_specs=[_row_spec(tm, D_MODEL), _row_spec(tm, D_MODEL), _const_spec((1, D_MODEL))],
        out_specs=[_row_spec(tm, D_MODEL), _const_spec((8, D_MODEL)), _const_spec((8, D_MODEL))],
        out_shape=[jax.ShapeDtypeStruct((t_len, D_MODEL), F32), jax.ShapeDtypeStruct((8, D_MODEL), F32),
                   jax.ShapeDtypeStruct((8, D_MODEL), F32)],
        compiler_params=_params(),
    )(x3, target, gf)


def _ffn_bwd(layer, dx3, x2, up, silu, dsilu, g2, wfc, wup_g, wdown_g, stages):
    t_len = x2.shape[0]
    tm = min(TM_FFN, t_len)
    nt = t_len // tm

    def core(dx3_ref, dx3_late_ref, x_ref, up_ref, silu_ref, dsilu_ref, g2_ref, wfc_ref, wup_hbm, wdown_hbm,
             dx2_ref, dup_ref, dx3b_ref, dg2_ref, dbfc_ref, dwfc_ref,
             wup_v, wdown_v, carry, da_s, dup_s, sems):
        i = pl.program_id(0)

        @pl.when(i == 0)
        def _():
            cps = _load_col_sharded(wup_hbm, wup_v, sems, 0) + _load_row_sharded(wdown_hbm, wdown_v, sems, 4)
            _start_all(cps)
            for ref in (carry, da_s, dup_s, dg2_ref, dbfc_ref, dwfc_ref):
                ref[...] = jnp.zeros_like(ref)
            _wait_all(cps)

        live = (i <= nt).astype(F32)
        dx3b_ref[...] = dx3_ref[...].astype(BF16)
        dh = jnp.zeros((tm, D_MODEL), F32)
        for c0, c1 in FF_CHUNKS:
            v0, v1 = D_FF + c0, D_FF + c1
            dh = dh + _dot_nt(dup_s[:, c0:c1], wup_v[:, c0:c1]) + _dot_nt(dup_s[:, v0:v1], wup_v[:, v0:v1])
            da = da_s[:, c0:c1]
            dval = (da * silu_ref[:, c0:c1].astype(F32)).astype(BF16)
            dup_ref[:, v0:v1] = dval
            dup_s[:, v0:v1] = dval
            dgc = da * up_ref[:, v0:v1].astype(F32) * dsilu_ref[:, c0:c1].astype(F32)
            cr = carry[:, c0:c1]
            dgc1 = _shift_up(dgc, cr, 1)
            dgc2 = _shift_up(dgc, cr, 2)
            carry[:, c0:c1] = jnp.where(i < nt, dgc[0:8, :], cr)
            gate = up_ref[:, c0:c1].astype(F32)
            dbfc_ref[:, c0:c1] += live * _colsum8(dgc)
            dwfc_ref[0, :, c0:c1] += live * _colsum8(dgc2 * gate)
            dwfc_ref[1, :, c0:c1] += live * _colsum8(dgc1 * gate)
            dwfc_ref[2, :, c0:c1] += live * _colsum8(dgc * gate)
            dgate = (wfc_ref[2:3, c0:c1] * dgc + wfc_ref[1:2, c0:c1] * dgc1 + wfc_ref[0:1, c0:c1] * dgc2).astype(BF16)
            dup_ref[:, c0:c1] = dgate
            dup_s[:, c0:c1] = dgate
            da_s[:, c0:c1] = _dot_nt(dx3b_ref[...], wdown_v[c0:c1, :])
        xv = x_ref[...]
        r = lax.rsqrt(jnp.mean(xv * xv, axis=-1, keepdims=True) + RMS_EPS)
        xh = xv * r
        dg2_ref[...] += _colsum8(dh * xh)
        dxh = dh * g2_ref[...]
        dx2_ref[...] = dx3_late_ref[...] + r * (dxh - xh * jnp.mean(dxh * xh, axis=-1, keepdims=True))

    def tile(n, lag):
        return pl.BlockSpec((tm, n), lambda i: (nt - 1 - jnp.clip(i - lag, 0, nt - 1), 0))

    outs = [
        jax.ShapeDtypeStruct((t_len, D_MODEL), F32),
        jax.ShapeDtypeStruct((t_len, 2 * D_FF), BF16),
        jax.ShapeDtypeStruct((t_len, D_MODEL), BF16),
        jax.ShapeDtypeStruct((8, D_MODEL), F32),
        jax.ShapeDtypeStruct((8, D_FF), F32),
        jax.ShapeDtypeStruct((3, 8, D_FF), F32),
    ]
    return _staged_call(
        core, name=f"ffn_bwd_l{layer}", grid=(nt + 2,),
        in_specs=[tile(D_MODEL, 0), tile(D_MODEL, 2), tile(D_MODEL, 2), tile(2 * D_FF, 1), tile(D_FF, 1), tile(D_FF, 1),
                  _const_spec((1, D_MODEL)), _const_spec((8, D_FF)), ANY, ANY],
        out_specs=[tile(D_MODEL, 2), tile(2 * D_FF, 1), tile(D_MODEL, 0),
                   _const_spec((8, D_MODEL)), _const_spec((8, D_FF)), _const_spec((3, 8, D_FF))],
        out_shape=outs,
        scratch_shapes=[pltpu.VMEM((D_MODEL, 2 * D_FF), BF16), pltpu.VMEM((D_FF, D_MODEL), BF16),
                        pltpu.VMEM((8, D_FF), F32), pltpu.VMEM((tm, D_FF), F32), pltpu.VMEM((tm, 2 * D_FF), BF16),
                        pltpu.SemaphoreType.DMA((8,))],
        args=[dx3, dx3, x2, up, silu, dsilu, g2, wfc, wup_g, wdown_g], stages=stages)


def _mixer_bwd(layer, dx2, x, z, qs, av, bv, g1, bgate, lng, lnb, wm, wmt, bsf, wsc, win_g, wb_g, wout_g, stages):
    t_len = x.shape[0]
    tm = min(TM_MIX, t_len)
    nt = t_len // tm
    nb = tm // GMLP_BLOCK

    def core(dx2_ref, x_ref, z_ref, q_ref, a_ref, b_ref, g1_ref, bgate_ref, lng_ref, lnb_ref,
             wm_ref, wmt_ref, bsf_ref, wsc_ref, win_hbm, wb_hbm, wout_hbm,
             dx_ref, dz_ref, da_ref, db_ref, dx2b_ref, dg1_ref, dbgate_ref, dlng_ref, dlnb_ref, dwm_ref, dbsf_ref, dwsc_ref,
             win_v, wb_v, wout_v, carry, vn_s, f_s, df_s, dvn_s, sems):
        i = pl.program_id(0)

        @pl.when(i == 0)
        def _():
            cps = (_load_col_sharded(win_hbm, win_v, sems, 0) + _load_branch(wb_hbm, wb_v, sems, 4)
                   + _load_row_sharded(wout_hbm, wout_v, sems, 12))
            _start_all(cps)
            carry[...] = jnp.zeros_like(carry)
            for ref in (dg1_ref, dbgate_ref, dlng_ref, dlnb_ref, dwm_ref, dbsf_ref, dwsc_ref):
                ref[...] = jnp.zeros_like(ref)
            _wait_all(cps)

        def zc(c0, n):
            return z_ref[:, c0:c0 + n].astype(F32)

        dx2b_ref[...] = dx2_ref[...].astype(BF16)
        dm = _dot_nt(dx2b_ref[...], wout_v[...])
        def dz_cols(c0, n, val):
            dz_ref[:, c0:c0 + n] = val.astype(BF16)
            return _dot_nt(dz_ref[:, c0:c0 + n], win_v[:, c0:c0 + n])

        sa = _sigmoid(zc(C_GA, D_MODEL) + bgate_ref[:, 0:D_MODEL])
        da_ref[...] = (dm * sa).astype(BF16)
        dga = dm * a_ref[...].astype(F32) * sa * (1.0 - sa)
        dh = dz_cols(C_GA, D_MODEL, dga)
        dbgate_ref[:, 0:D_MODEL] += _colsum8(dga)
        dya = _dot_nt(da_ref[...], wb_v[0])
        sb = _sigmoid(zc(C_GB, D_MODEL) + bgate_ref[:, D_MODEL:2 * D_MODEL])
        db_ref[...] = (dm * sb).astype(BF16)
        dgb = dm * b_ref[...].astype(F32) * sb * (1.0 - sb)
        dh = dh + dz_cols(C_GB, D_MODEL, dgb)
        dbgate_ref[:, D_MODEL:2 * D_MODEL] += _colsum8(dgb)
        dyb = _dot_nt(db_ref[...], wb_v[1])

        v = zc(C_V, D_A)
        vg, tv = _gelu(v)
        mu = jnp.mean(vg, axis=-1, keepdims=True)
        vc = vg - mu
        rstd = lax.rsqrt(jnp.mean(vc * vc, axis=-1, keepdims=True) + LN_EPS)
        xh = vc * rstd
        vn_s[...] = (xh * lng_ref[...] + lnb_ref[...]).astype(BF16)
        u = zc(C_U, D_A)
        ug, tu = _gelu(u)
        df = dya * ug
        df_s[...] = df.astype(BF16)
        dbsf_acc = df[0:128, :]
        for b in range(1, nb):
            dbsf_acc = dbsf_acc + df[b * 128:(b + 1) * 128, :]
        dbsf_ref[...] += dbsf_acc
        for hd in range(A_HEADS):
            cols = slice(hd * 128, (hd + 1) * 128)
            vcat = jnp.concatenate([vn_s[b * 128:(b + 1) * 128, cols] for b in range(nb)], axis=1)
            dcat = jnp.concatenate([df_s[b * 128:(b + 1) * 128, cols] for b in range(nb)], axis=1)
            fcat = _dot(wm_ref[hd], vcat)
            gcat = _dot(wmt_ref[hd], dcat)
            dwm_ref[hd] += _dot_nt(dcat, vcat)
            for b in range(nb):
                f_s[b * 128:(b + 1) * 128, cols] = fcat[:, b * 128:(b + 1) * 128]
                dvn_s[b * 128:(b + 1) * 128, cols] = gcat[:, b * 128:(b + 1) * 128]
        bias = jnp.concatenate([bsf_ref[...]] * nb, axis=0)
        dh = dh + dz_cols(C_U, D_A, dya * (f_s[...] + bias) * _gelu_grad(u, tu))
        dvn = dvn_s[...]
        dlng_ref[...] += _colsum8(dvn * xh)
        dlnb_ref[...] += _colsum8(dvn)
        dxh = dvn * lng_ref[...]
        dvg = rstd * (dxh - jnp.mean(dxh, axis=-1, keepdims=True) - xh * jnp.mean(dxh * xh, axis=-1, keepdims=True))
        dh = dh + dz_cols(C_V, D_A, dvg * _gelu_grad(v, tv))

        cg = zc(C_CG, D_B)
        hbv = zc(C_HB, D_B)
        p = cg * hbv
        dh = dh + dz_cols(C_BG, D_B, dyb * q_ref[...].astype(F32))
        dq = dyb * zc(C_BG, D_B)
        cr = carry[...]
        dq1 = _shift_up(dq, cr, 1)
        dq2 = _shift_up(dq, cr, 2)
        carry[...] = dq[0:8, :]
        dwsc_ref[0] += _colsum8(dq2 * p)
        dwsc_ref[1] += _colsum8(dq1 * p)
        dwsc_ref[2] += _colsum8(dq * p)
        dp = wsc_ref[2:3, :] * dq + wsc_ref[1:2, :] * dq1 + wsc_ref[0:1, :] * dq2
        dh = dh + dz_cols(C_CG, D_B, dp * hbv)
        dh = dh + dz_cols(C_HB, D_B, dp * cg)

        xv = x_ref[...]
        r = lax.rsqrt(jnp.mean(xv * xv, axis=-1, keepdims=True) + RMS_EPS)
        xn = xv * r
        dg1_ref[...] += _colsum8(dh * xn)
        dxn = dh * g1_ref[...]
        dx_ref[...] = dx2_ref[...] + r * (dxn - xn * jnp.mean(dxn * xn, axis=-1, keepdims=True))

    outs = [
        jax.ShapeDtypeStruct((t_len, D_MODEL), F32),
        jax.ShapeDtypeStruct((t_len, D_IN), BF16),
        jax.ShapeDtypeStruct((t_len, D_MODEL), BF16),
        jax.ShapeDtypeStruct((t_len, D_MODEL), BF16),
        jax.ShapeDtypeStruct((t_len, D_MODEL), BF16),
        jax.ShapeDtypeStruct((8, D_MODEL), F32),
        jax.ShapeDtypeStruct((8, 2 * D_MODEL), F32),
        jax.ShapeDtypeStruct((8, D_A), F32),
        jax.ShapeDtypeStruct((8, D_A), F32),
        jax.ShapeDtypeStruct((A_HEADS, 128, 128), F32),
        jax.ShapeDtypeStruct((128, D_A), F32),
        jax.ShapeDtypeStruct((3, 8, D_B), F32),
    ]

    return _staged_call(
        core, name=f"mixer_bwd_l{layer}", grid=(nt,),
        in_specs=[_row_spec(tm, D_MODEL, nt), _row_spec(tm, D_MODEL, nt), _row_spec(tm, D_IN, nt),
                  _row_spec(tm, D_B, nt), _row_spec(tm, D_MODEL, nt), _row_spec(tm, D_MODEL, nt),
                  _const_spec((1, D_MODEL)), _const_spec((1, 2 * D_MODEL)), _const_spec((1, D_A)), _const_spec((1, D_A)),
                  _const_spec((A_HEADS, 128, 128)), _const_spec((A_HEADS, 128, 128)), _const_spec((128, D_A)),
                  _const_spec((8, D_B)), ANY, ANY, ANY],
        out_specs=[_row_spec(tm, D_MODEL, nt), _row_spec(tm, D_IN, nt), _row_spec(tm, D_MODEL, nt),
                   _row_spec(tm, D_MODEL, nt), _row_spec(tm, D_MODEL, nt),
                   _const_spec((8, D_MODEL)), _const_spec((8, 2 * D_MODEL)), _const_spec((8, D_A)), _const_spec((8, D_A)),
                   _const_spec((A_HEADS, 128, 128)), _const_spec((128, D_A)), _const_spec((3, 8, D_B))],
        out_shape=outs,
        scratch_shapes=[pltpu.VMEM((D_MODEL, D_IN), BF16), pltpu.VMEM((2, D_A, D_MODEL), BF16),
                        pltpu.VMEM((D_MODEL, D_MODEL), BF16), pltpu.VMEM((8, D_B), F32),
                        pltpu.VMEM((tm, D_A), BF16), pltpu.VMEM((tm, D_A), F32), pltpu.VMEM((tm, D_A), BF16),
                        pltpu.VMEM((tm, D_A), F32), pltpu.SemaphoreType.DMA((16,))],
        args=[dx2, x, z, qs, av, bv, g1, bgate, lng, lnb, wm, wmt, bsf, wsc, win_g, wb_g, wout_g], stages=stages)


def _wgrad(name, layer, a, b, rows, cols, row_blk, col_blk, stages, a_first=0):
    t_len = a.shape[0]
    n = b.shape[1]
    tk = min(TK_WGRAD, t_len)
    col_sharded = n == N_CHIPS * cols
    m = rows if col_sharded else a.shape[1]
    grid = (m // row_blk, n // col_blk, t_len // tk)
    per_shard_c = cols // col_blk

    if col_sharded:
        out_shape = (N_CHIPS, rows, cols)
        out_spec = pl.BlockSpec((None, row_blk, col_blk), lambda i, j, k: (j // per_shard_c, i, j % per_shard_c))
    else:
        out_shape = (N_CHIPS * rows, cols)
        out_spec = pl.BlockSpec((row_blk, col_blk), lambda i, j, k: (i, j))

    def core(a_ref, b_ref, o_ref):
        @pl.when(pl.program_id(2) == 0)
        def _():
            o_ref[...] = jnp.zeros_like(o_ref)

        o_ref[...] += _dot_tn(a_ref[...], b_ref[...])

    own, outs = _staged_call(
        core, name=f"wgrad_{name}_l{layer}", grid=grid,
        in_specs=[pl.BlockSpec((tk, row_blk), lambda i, j, k: (k, a_first + i)),
                  pl.BlockSpec((tk, col_blk), lambda i, j, k: (k, j))],
        out_specs=[out_spec], out_shape=[jax.ShapeDtypeStruct(out_shape, F32)], scratch_shapes=[],
        args=[a, b], stages=stages)
    return [own[0].reshape(N_CHIPS, rows, cols)], outs


def _wgrad_branch(layer, ya, da, yb, db, stages):
    t_len = ya.shape[0]
    tk = min(TK_WGRAD, t_len)

    def core(ya_ref, da_ref, yb_ref, db_ref, o_ref):
        @pl.when(pl.program_id(1) == 0)
        def _():
            o_ref[...] = jnp.zeros_like(o_ref)

        o_ref[0:D_A, :] += _dot_tn(ya_ref[...], da_ref[...])
        o_ref[D_A:2 * D_A, :] += _dot_tn(yb_ref[...], db_ref[...])

    a_spec = pl.BlockSpec((tk, D_A), lambda j, k: (k, 0))
    d_spec = pl.BlockSpec((tk, 256), lambda j, k: (k, j))
    return _staged_call(
        core, name=f"wgrad_w_branch_l{layer}", grid=(N_CHIPS, t_len // tk),
        in_specs=[a_spec, d_spec, a_spec, d_spec],
        out_specs=[pl.BlockSpec((None, 2 * D_A, 256), lambda j, k: (j, 0, 0))],
        out_shape=[jax.ShapeDtypeStruct((N_CHIPS, 2 * D_A, 256), F32)], scratch_shapes=[],
        args=[ya, da, yb, db], stages=stages)


def _all_reduce_small(name, packed):
    rows = packed.shape[0]

    def body(src_ref, out_ref, slots, send, recv):
        x, y, c = _mesh_pos()
        me = 4 * x + 2 * y + c
        cps = []
        for d in range(1, N_DEVICES):
            peer = me ^ d
            cps.append(_remote(src_ref, slots.at[me], send.at[d - 1], recv.at[d - 1],
                               (peer // 4, (peer // 2) % 2, peer % 2)))
        _start_all(cps)
        slots[me] = src_ref[...]
        _wait_all(cps)
        acc = slots[0]
        for d in range(1, N_DEVICES):
            acc = acc + slots[d]
        out_ref[...] = acc

    return pl.pallas_call(
        body, name=f"all_reduce_{name}",
        in_specs=[pl.BlockSpec(memory_space=pltpu.VMEM)], out_specs=pl.BlockSpec(memory_space=pltpu.VMEM),
        out_shape=jax.ShapeDtypeStruct(packed.shape, F32),
        scratch_shapes=[pltpu.VMEM((N_DEVICES, rows, 128), F32), pltpu.SemaphoreType.DMA((7,)),
                        pltpu.SemaphoreType.DMA((7,))],
        compiler_params=pltpu.CompilerParams(vmem_limit_bytes=V7X_VMEM_LIMIT),
    )(packed)


def _flat_blk(rows, cols):
    blk = rows
    while blk * cols * 4 > 2 * 1024 * 1024 and blk % 16 == 0:
        blk //= 2
    return blk


def _cast_into_slot(name, layer, w, chip):
    _, rows, cols = w.shape
    blk = _flat_blk(rows, cols)

    def body(chip_ref, w_ref, o_ref):
        o_ref[...] = w_ref[...].astype(BF16)

    return pl.pallas_call(
        body, name=f"cast_{name}_l{layer}",
        grid_spec=pltpu.PrefetchScalarGridSpec(
            num_scalar_prefetch=1, grid=(rows // blk,),
            in_specs=[pl.BlockSpec((None, blk, cols), lambda i, chip_ref: (layer, i, 0))],
            out_specs=pl.BlockSpec((None, blk, cols), lambda i, chip_ref: (chip_ref[0], i, 0))),
        out_shape=jax.ShapeDtypeStruct((N_CHIPS, rows, cols), BF16),
        compiler_params=_params(("parallel",)),
    )(chip, w)


def _pair_sum(name, grad, other, core):
    _, h, cols = other.shape
    blk = _flat_blk(h, cols)
    nblk = h // blk

    def body(core_ref, g_ref, o_ref, s_ref):
        s_ref[...] = (g_ref[...] + o_ref[...]).astype(BF16)

    spec = pl.BlockSpec((None, blk, cols), lambda k, i, core_ref: (k, i, 0))
    return pl.pallas_call(
        body, name=f"pair_sum_{name}",
        grid_spec=pltpu.PrefetchScalarGridSpec(
            num_scalar_prefetch=1, grid=(N_CHIPS, nblk),
            in_specs=[pl.BlockSpec((None, blk, cols), lambda k, i, core_ref: (k, core_ref[0] * nblk + i, 0)), spec],
            out_specs=spec),
        out_shape=jax.ShapeDtypeStruct((N_CHIPS, h, cols), BF16),
        compiler_params=_params(("parallel", "parallel")),
    )(core, grad, other)


def _chip_sum(name, grad, other, got, pos):
    _, rows, cols = grad.shape
    h = rows // 2
    blk = _flat_blk(h, cols)
    nblk = h // blk

    def body(pos_ref, g_ref, o_ref, r_ref, f_ref):
        f_ref[...] = (((g_ref[...] + o_ref[...]) + r_ref[0].astype(F32)) + r_ref[1].astype(F32)) + r_ref[2].astype(F32)

    return pl.pallas_call(
        body, name=f"chip_sum_{name}",
        grid_spec=pltpu.PrefetchScalarGridSpec(
            num_scalar_prefetch=1, grid=(nblk,),
            in_specs=[pl.BlockSpec((None, blk, cols), lambda i, pos_ref: (pos_ref[0], pos_ref[1] * nblk + i, 0)),
                      pl.BlockSpec((None, blk, cols), lambda i, pos_ref: (pos_ref[0], i, 0)),
                      pl.BlockSpec((3, blk, cols), lambda i, pos_ref: (0, i, 0))],
            out_specs=pl.BlockSpec((blk, cols), lambda i, pos_ref: (pos_ref[1] * nblk + i, 0))),
        out_shape=jax.ShapeDtypeStruct((rows, cols), F32),
        compiler_params=_params(("parallel",)),
    )(pos, grad, other, got)


def _sum_slots(name, slots):
    n, rows, _ = slots.shape

    def body(s_ref, o_ref):
        acc = s_ref[0]
        for d in range(1, n):
            acc = acc + s_ref[d]
        o_ref[...] = acc

    return pl.pallas_call(
        body, name=f"sum_slots_{name}", grid=(1,),
        in_specs=[pl.BlockSpec((n, rows, 128), lambda i: (0, 0, 0))],
        out_specs=pl.BlockSpec((rows, 128), lambda i: (0, 0)),
        out_shape=jax.ShapeDtypeStruct((rows, 128), F32),
        compiler_params=_params(),
    )(slots)


def _adamw_math(w, g, m, v):
    m2 = ADAM_B1 * m + (1.0 - ADAM_B1) * g
    v2 = ADAM_B2 * v + (1.0 - ADAM_B2) * (g * g)
    m_hat = m2 / (1.0 - ADAM_B1 ** ADAM_STEP)
    v_hat = v2 / (1.0 - ADAM_B2 ** ADAM_STEP)
    delta = -ADAM_LR * (m_hat / (jnp.sqrt(v_hat) + ADAM_EPS) + ADAM_WD * w)
    return delta, m2, v2


def _adamw_big(name, w, g0, g1, m, v):
    _, rows, cols = w.shape
    blk = _flat_blk(rows, cols) // 2

    def body(w_ref, g0_ref, g1_ref, m_ref, v_ref, g_ref, d_ref, m2_ref, v2_ref):
        g = jnp.where(pl.program_id(0) == 0, g0_ref[...], g1_ref[...])
        d, m2, v2 = _adamw_math(w_ref[...], g, m_ref[...], v_ref[...])
        g_ref[...] = g
        d_ref[...] = d
        m2_ref[...] = m2
        v2_ref[...] = v2

    spec = pl.BlockSpec((None, blk, cols), lambda la, i: (la, i, 0))
    return pl.pallas_call(
        body, name=f"adamw_{name}", grid=(N_LAYERS, rows // blk),
        in_specs=[spec, pl.BlockSpec((blk, cols), lambda la, i: (i * (1 - la), 0)),
                  pl.BlockSpec((blk, cols), lambda la, i: (i * la, 0)), spec, spec],
        out_specs=[spec] * 4,
        out_shape=[jax.ShapeDtypeStruct(w.shape, F32)] * 4,
        compiler_params=_params(("parallel", "parallel")),
    )(w, g0, g1, m, v)


def _adamw(name, w, g, m, v):
    rows, cols = w.shape
    blk = _flat_blk(rows, cols)

    def body(w_ref, g_ref, m_ref, v_ref, d_ref, m2_ref, v2_ref):
        d, m2, v2 = _adamw_math(w_ref[...], g_ref[...], m_ref[...], v_ref[...])
        d_ref[...] = d
        m2_ref[...] = m2
        v2_ref[...] = v2

    spec = pl.BlockSpec((blk, cols), lambda i: (i, 0))
    return pl.pallas_call(
        body, name=f"adamw_{name}", grid=(rows // blk,),
        in_specs=[spec] * 4, out_specs=[spec] * 3,
        out_shape=[jax.ShapeDtypeStruct((rows, cols), F32)] * 3,
        compiler_params=_params(("parallel",)),
    )(w, g, m, v)


SMALL = ("norm1_g", "b_gate", "gmlp_ln_g", "gmlp_ln_b", "w_spatial", "b_spatial", "w_shortconv", "norm2_g",
         "w_ffn_conv", "b_ffn_conv", "final_g")
ALL_WEIGHTS = ("norm1_g", "w_in", "b_gate", "gmlp_ln_g", "gmlp_ln_b", "w_spatial", "b_spatial", "w_shortconv",
               "w_branch", "w_out", "norm2_g", "w_ffn_up", "w_ffn_conv", "b_ffn_conv", "w_ffn_down", "final_g")


def _pack(arrays):
    flat = jnp.concatenate([a.reshape(-1) for a in arrays])
    n = flat.shape[0]
    rows = -(-n // 1024) * 8
    return jnp.pad(flat, (0, rows * 128 - n)).reshape(rows, 128)


def _unpack(packed, like):
    flat = packed.reshape(-1)
    out, off = [], 0
    for a in like:
        out.append(flat[off:off + a.size].reshape(a.shape))
        off += a.size
    return out


def _pad8(w):
    return jnp.pad(w, ((0, 5), (0, 0)))


def kernel(x, norm1_g, w_in, b_gate, gmlp_ln_g, gmlp_ln_b, w_spatial, b_spatial, w_shortconv, w_branch, w_out, norm2_g, w_ffn_up, w_ffn_conv, b_ffn_conv, w_ffn_down, final_g, loss_target, m_norm1_g, m_w_in, m_b_gate, m_gmlp_ln_g, m_gmlp_ln_b, m_w_spatial, m_b_spatial, m_w_shortconv, m_w_branch, m_w_out, m_norm2_g, m_w_ffn_up, m_w_ffn_conv, m_b_ffn_conv, m_w_ffn_down, m_final_g, v_norm1_g, v_w_in, v_b_gate, v_gmlp_ln_g, v_gmlp_ln_b, v_w_spatial, v_b_spatial, v_w_shortconv, v_w_branch, v_w_out, v_norm2_g, v_w_ffn_up, v_w_ffn_conv, v_b_ffn_conv, v_w_ffn_down, v_final_g):
    weights = dict(norm1_g=norm1_g, w_in=w_in, b_gate=b_gate, gmlp_ln_g=gmlp_ln_g, gmlp_ln_b=gmlp_ln_b,
                   w_spatial=w_spatial, b_spatial=b_spatial, w_shortconv=w_shortconv, w_branch=w_branch, w_out=w_out,
                   norm2_g=norm2_g, w_ffn_up=w_ffn_up, w_ffn_conv=w_ffn_conv, b_ffn_conv=b_ffn_conv,
                   w_ffn_down=w_ffn_down, final_g=final_g)
    mom = dict(norm1_g=m_norm1_g, w_in=m_w_in, b_gate=m_b_gate, gmlp_ln_g=m_gmlp_ln_g, gmlp_ln_b=m_gmlp_ln_b,
               w_spatial=m_w_spatial, b_spatial=m_b_spatial, w_shortconv=m_w_shortconv, w_branch=m_w_branch,
               w_out=m_w_out, norm2_g=m_norm2_g, w_ffn_up=m_w_ffn_up, w_ffn_conv=m_w_ffn_conv,
               b_ffn_conv=m_b_ffn_conv, w_ffn_down=m_w_ffn_down, final_g=m_final_g)
    vel = dict(norm1_g=v_norm1_g, w_in=v_w_in, b_gate=v_b_gate, gmlp_ln_g=v_gmlp_ln_g, gmlp_ln_b=v_gmlp_ln_b,
               w_spatial=v_w_spatial, b_spatial=v_b_spatial, w_shortconv=v_w_shortconv, w_branch=v_w_branch,
               w_out=v_w_out, norm2_g=v_norm2_g, w_ffn_up=v_w_ffn_up, w_ffn_conv=v_w_ffn_conv,
               b_ffn_conv=v_b_ffn_conv, w_ffn_down=v_w_ffn_down, final_g=v_final_g)

    cx, cy, cc = _mesh_pos()
    chip = 2 * cx + cy
    core_arr = cc.astype(jnp.int32).reshape(1)
    chip_arr = chip.astype(jnp.int32).reshape(1)
    pos_arr = jnp.stack([chip, cc]).astype(jnp.int32)
    t_len = x.shape[1]
    xs = x.reshape(t_len, D_MODEL)
    target = loss_target.reshape(t_len, D_MODEL)
    pipe = _Pipe()

    full = {}

    def gather(keys):
        slots = [_cast_into_slot(n, la, weights[n].reshape((N_LAYERS,) + BIG[n]), chip_arr) for n, la in keys]

        def then(*bufs):
            full.update(zip(keys, bufs))

        pipe.add(_gather_stage(slots, then))

    mixer_w = ("w_in", "w_branch", "w_out")
    ffn_w = ("w_ffn_up", "w_ffn_down")
    gather([(n, 0) for n in mixer_w])
    pipe.flush()

    idx = jnp.arange(GMLP_BLOCK) // CHUNK
    mask = idx[None, :] <= idx[:, None]
    wm_all = jnp.where(mask[None, None], w_spatial, 0.0)
    wm_bf = wm_all.astype(BF16)
    wmt_bf = jnp.swapaxes(wm_all, -1, -2).astype(BF16)
    bsf = jnp.repeat(jnp.swapaxes(b_spatial, -1, -2), 128, axis=-1)
    wsc_full = lax.dynamic_update_slice(jnp.zeros((N_LAYERS, 3, D_B), F32), w_shortconv, (0, 0, chip * (D_B // 4)))
    wfc_full = lax.dynamic_update_slice(jnp.zeros((N_LAYERS, 3, D_FF), F32), w_ffn_conv, (0, 0, chip * (D_FF // 4)))
    taps = _all_reduce_small("conv_taps", _pack([wsc_full, wfc_full]))
    wsc_full, wfc_full = _unpack(taps * 0.5, [wsc_full, wfc_full])

    def row(a):
        return a.reshape(1, -1)

    def mixer_args(la):
        return (row(norm1_g[la]), row(b_gate[la]), row(gmlp_ln_g[la]), row(gmlp_ln_b[la]))

    def mixer_weights(la):
        return tuple(full[(n, la)] for n in mixer_w)

    def ffn_weights(la):
        return tuple(full[(n, la)] for n in ffn_w)

    saved = []
    h_in = xs
    for la in range(N_LAYERS):
        gather([(n, la) for n in ffn_w])
        z, ya, yb, qs, av, bv, mg, h1, x2 = pipe.carry(lambda st: _mixer_fwd(
            la, h_in, *mixer_args(la), wm_bf[la], bsf[la], _pad8(wsc_full[la]), *mixer_weights(la), st))
        if la + 1 < N_LAYERS:
            gather([(n, la + 1) for n in mixer_w])
        up, silu, dsilu, act, h2, x3 = pipe.carry(lambda st: _ffn_fwd(
            la, x2, row(norm2_g[la]), _pad8(wfc_full[la]), row(b_ffn_conv[la]), *ffn_weights(la), st))
        saved.append(dict(x=h_in, z=z, ya=ya, yb=yb, q=qs, av=av, bv=bv, mg=mg, h1=h1, x2=x2, up=up, silu=silu, dsilu=dsilu, act=act,
                          h2=h2))
        h_in = x3

    reduced_big = {}

    def reduce_big(name, la, grad):
        tag = f"{name}_l{la}"

        def after_pair(other):
            psum = _pair_sum(tag, grad, other, core_arr)

            def after_chips(got):
                final = _chip_sum(tag, grad, other, got, pos_arr)
                pipe.add(_pair_fill_stage(final, lambda done: reduced_big.__setitem__((name, la), done)))

            pipe.add(_chip_send_stage(psum, after_chips))

        pipe.add(_pair_send_stage(grad, after_pair))

    dx, dgf8, loss8 = _loss_head(h_in, target, row(final_g))
    small = {n: [None] * N_LAYERS for n in SMALL}
    spread = {}
    for la in reversed(range(N_LAYERS)):
        s = saved[la]
        dx3 = dx
        dx2, dup, dx3b, dg2, dbfc, dwfc = pipe.carry(lambda st: _ffn_bwd(
            la, dx3, s["x2"], s["up"], s["silu"], s["dsilu"], row(norm2_g[la]), _pad8(wfc_full[la]),
            *ffn_weights(la), st))
        g, = pipe.carry(lambda st: _wgrad("w_ffn_down", la, s["act"], dx3b, 704, 1024, 1408, 1024, st))
        reduce_big("w_ffn_down", la, g)
        g, = pipe.carry(lambda st: _wgrad("w_ffn_up", la, s["h2"], dup, 1024, 1408, 1024, 1408, st))
        reduce_big("w_ffn_up", la, g)
        dxl, dz, da, db, dx2b, dg1, dbg, dlng, dlnb, dwm, dbsf, dwsc = pipe.carry(lambda st: _mixer_bwd(
            la, dx2, s["x"], s["z"], s["q"], s["av"], s["bv"], *mixer_args(la), wm_bf[la], wmt_bf[la], bsf[la],
            _pad8(wsc_full[la]), *mixer_weights(la), st))
        small["norm1_g"][la] = dg1.sum(0)
        small["b_gate"][la] = dbg.sum(0)
        small["gmlp_ln_g"][la] = dlng.sum(0)
        small["gmlp_ln_b"][la] = dlnb.sum(0)
        small["w_spatial"][la] = jnp.where(mask[None], dwm, 0.0)
        small["b_spatial"][la] = dbsf.reshape(128, A_HEADS, 128).sum(-1).T
        small["w_shortconv"][la] = dwsc.sum(1)
        small["norm2_g"][la] = dg2.sum(0)
        small["w_ffn_conv"][la] = dwfc.sum(1)
        small["b_ffn_conv"][la] = dbfc.sum(0)
        if la == 0:
            small_local = ([jnp.stack(small[n]) for n in SMALL[:-1]]
                           + [dgf8.sum(0), 0.5 * loss8.sum().reshape(1) / D_MODEL])
            mine = _pack(small_local)

            def after_swap(other, mine=mine):
                pair = _sum_slots("small_pair", jnp.stack([mine, other]))
                pipe.add(_chip_spread_stage(pair, lambda slots: spread.__setitem__("slots", slots)))

            pipe.add(_pair_swap_stage(mine, after_swap))
        for part, tag in enumerate(("w_in_a", "w_in_b")):
            g, = pipe.carry(lambda st: _wgrad(tag, la, s["h1"], dz, 512, 1152, 512, 1152, st, a_first=part))
            reduce_big(tag, la, g)
        g, = pipe.carry(lambda st: _wgrad("w_out", la, s["mg"], dx2b, 256, 1024, 1024, 1024, st), long=False)
        reduce_big("w_out", la, g)
        g, = pipe.carry(lambda st: _wgrad_branch(la, s["ya"], da, s["yb"], db, st), long=False)
        reduce_big("w_branch", la, g)
        dx = dxl
    grad_x = dx.reshape(x.shape)
    pipe.flush()

    for la in range(N_LAYERS):
        reduced_big[("w_in", la)] = jnp.concatenate([reduced_big[("w_in_a", la)], reduced_big[("w_in_b", la)]], axis=0)
    reduced = _unpack(_sum_slots("small_grads", spread["slots"]), small_local)
    loss = reduced[-1].reshape(())
    grads = dict(zip(SMALL, reduced[:-1]))
    grads["w_shortconv"] = lax.dynamic_slice(grads["w_shortconv"], (0, 0, chip * (D_B // 4)), (N_LAYERS, 3, D_B // 4))
    grads["w_ffn_conv"] = lax.dynamic_slice(grads["w_ffn_conv"], (0, 0, chip * (D_FF // 4)), (N_LAYERS, 3, D_FF // 4))

    delta, new_m, new_v = {}, {}, {}
    for n in BIG_NAMES:
        shape3 = (N_LAYERS,) + BIG[n]
        res = _adamw_big(n, weights[n].reshape(shape3), reduced_big[(n, 0)], reduced_big[(n, 1)],
                         mom[n].reshape(shape3), vel[n].reshape(shape3))
        grads[n], delta[n], new_m[n], new_v[n] = (a.reshape(weights[n].shape) for a in res)
    small_w = [weights[n] for n in SMALL]
    packed = [_pack([src[n] for n in SMALL]) for src in (weights, grads, mom, vel)]
    for dst, res in zip((delta, new_m, new_v), _adamw("small", *packed)):
        dst.update(zip(SMALL, _unpack(res, small_w)))

    return (loss, grad_x, *[grads[n] for n in ALL_WEIGHTS], *[delta[n] for n in ALL_WEIGHTS],
            *[new_m[n] for n in ALL_WEIGHTS], *[new_v[n] for n in ALL_WEIGHTS])
```

```python
import jax
import jax.numpy as jnp
from jax import lax
from jax.experimental import pallas as pl
from jax.experimental.pallas import tpu as pltpu

F32 = jnp.float32
BF16 = jnp.bfloat16
MESH = pl.DeviceIdType.MESH
ANY = pl.BlockSpec(memory_space=pl.ANY)

D_MODEL = 1024
D_A = 512
D_B = 512
D_IN = 4608
D_FF = 2816
GMLP_BLOCK = 128
CHUNK = 64
A_HEADS = 4
N_LAYERS = 2
N_CHIPS = 4
N_DEVICES = 8
RMS_EPS = 1e-6
LN_EPS = 1e-5
ADAM_LR = 0.001
ADAM_B1 = 0.9
ADAM_B2 = 0.999
ADAM_EPS = 1e-08
ADAM_WD = 0.01
ADAM_STEP = 10

C_U, C_V, C_BG, C_CG, C_HB, C_GA, C_GB = 0, 512, 1024, 1536, 2048, 2560, 3584

V7X_VMEM_LIMIT = 60 * 1024 * 1024
TM_MIX = 256
TM_FFN = 256
TK_WGRAD = 2048
SLOW_COPY_BYTES = 640 * 1024
FF_CHUNKS = ((0, 768), (768, 1536), (1536, 2304), (2304, 2816))
GELU_C0 = 0.7978845608028654
GELU_C1 = 0.044715

BIG = {
    "w_in": (1024, 1152),
    "w_branch": (1024, 256),
    "w_out": (256, 1024),
    "w_ffn_up": (1024, 1408),
    "w_ffn_down": (704, 1024),
}
BIG_NAMES = tuple(BIG)


def _params(sem=("arbitrary",), vmem=V7X_VMEM_LIMIT):
    return pltpu.CompilerParams(dimension_semantics=sem, vmem_limit_bytes=vmem)


def _gelu(x):
    x2 = x * x
    t = jnp.tanh(GELU_C0 * x * (1.0 + GELU_C1 * x2))
    return 0.5 * x * (1.0 + t), t


def _gelu_grad(x, t):
    return 0.5 * (1.0 + t) + 0.5 * x * (1.0 - t * t) * GELU_C0 * (1.0 + 3.0 * GELU_C1 * x * x)


def _colsum8(v):
    r, n = v.shape
    return v.reshape(r // 8, 8, n).sum(axis=0)


def _dot(a, b):
    return jnp.dot(a, b, preferred_element_type=F32)


def _dot_nt(a, b):
    return lax.dot_general(a, b, (((1,), (1,)), ((), ())), preferred_element_type=F32)


def _dot_tn(a, b):
    return lax.dot_general(a, b, (((0,), (0,)), ((), ())), preferred_element_type=F32)


def _shift_down(v, carry, n):
    rows = lax.broadcasted_iota(jnp.int32, (8, v.shape[1]), 0)
    out = pltpu.roll(v, n, 0)
    head = out[0:8, :]
    for r in range(n):
        head = jnp.where(rows == r, carry[8 - n + r:8 - n + r + 1, :], head)
    return jnp.concatenate([head, out[8:, :]], axis=0)


def _shift_up(v, carry, n):
    tm = v.shape[0]
    rows = lax.broadcasted_iota(jnp.int32, (8, v.shape[1]), 0)
    out = pltpu.roll(v, tm - n, 0)
    tail = out[tm - 8:tm, :]
    for r in range(n):
        tail = jnp.where(rows == 8 - n + r, carry[r:r + 1, :], tail)
    return jnp.concatenate([out[0:tm - 8, :], tail], axis=0)


def _sigmoid(x):
    return 0.5 * jnp.tanh(0.5 * x) + 0.5


def _start_all(copies):
    for cp in copies:
        cp.start()


def _wait_all(copies):
    for cp in copies:
        cp.wait()


def _load_col_sharded(src, dst, sems, first):
    cs = src.shape[-1]
    return [pltpu.make_async_copy(src.at[k], dst.at[:, k * cs:(k + 1) * cs], sems.at[first + k])
            for k in range(N_CHIPS)]


def _load_row_sharded(src, dst, sems, first):
    rs = src.shape[-2]
    return [pltpu.make_async_copy(src.at[k], dst.at[k * rs:(k + 1) * rs, :], sems.at[first + k])
            for k in range(N_CHIPS)]


def _load_branch(src, dst, sems, first):
    return [pltpu.make_async_copy(src.at[k, pl.ds(m * D_A, D_A), :], dst.at[m, :, k * 256:(k + 1) * 256],
                                  sems.at[first + 2 * k + m])
            for k in range(N_CHIPS) for m in range(2)]


def _row_spec(tm, n, rev=None):
    if rev is None:
        return pl.BlockSpec((tm, n), lambda i: (i, 0))
    return pl.BlockSpec((tm, n), lambda i: (rev - 1 - i, 0))


def _const_spec(shape):
    nd = len(shape)
    return pl.BlockSpec(shape, lambda i: (0,) * nd)


def _mesh_pos():
    return lax.axis_index("x"), lax.axis_index("y"), lax.axis_index("c")


def _other_chips(x, y):
    return [(1 - x, y, 2 * (1 - x) + y), (x, 1 - y, 2 * x + (1 - y)), (1 - x, 1 - y, 2 * (1 - x) + (1 - y))]


def _remote(src, dst, ssem, rsem, to):
    return pltpu.make_async_remote_copy(src_ref=src, dst_ref=dst, send_sem=ssem, recv_sem=rsem, device_id=to,
                                        device_id_type=MESH)


def _half(ref, which, h):
    start = pl.multiple_of(which * h, 8)
    if len(ref.shape) == 2:
        return ref.at[pl.ds(start, h), :]
    return ref.at[:, pl.ds(start, h), :]


class _Stage:
    def __init__(self, ins=(), inouts=(), outs=(), n_sems=0, start=None, mid=None, finish=None, then=None, slow=False):
        self.ins, self.inouts, self.outs = list(ins), list(inouts), list(outs)
        self.n_sems, self.start, self.mid, self.finish, self.then = n_sems, start, mid, finish, then
        self.slow = slow


def _gather_stage(bufs, then):
    n = len(bufs)

    def copies(io, sem):
        x, y, c = _mesh_pos()
        me = 2 * x + y
        ici, fwd, got = [], [], []
        for w in range(n):
            h = io[w].shape[1] // 2
            for j, (px, py, pk) in enumerate(_other_chips(x, y)):
                mine = _half(io[w].at[me], c, h)
                theirs = _half(io[w].at[pk], c, h)
                ici.append(_remote(mine, mine, sem(12 * w + j), sem(12 * w + 3 + j), (px, py, c)))
                got.append(_remote(theirs, theirs, sem(12 * w + j), sem(12 * w + 3 + j), (px, py, c)))
                fwd.append(_remote(theirs, theirs, sem(12 * w + 6 + j), sem(12 * w + 9 + j), (x, y, 1 - c)))
        return ici, got, fwd

    def start(ins, io, outs, sem):
        _start_all(copies(io, sem)[0])

    def mid(ins, io, outs, sem):
        _, got, fwd = copies(io, sem)
        for g, f in zip(got, fwd):
            g.wait_recv()
            f.start()

    def finish(ins, io, outs, sem):
        x, y, c = _mesh_pos()
        ici, _, fwd = copies(io, sem)
        for w in range(n):
            h = io[w].shape[1] // 2
            for j, (px, py, pk) in enumerate(_other_chips(x, y)):
                other = _half(io[w].at[pk], 1 - c, h)
                _remote(other, other, sem(12 * w + 6 + j), sem(12 * w + 9 + j), (x, y, 1 - c)).wait_recv()
        for cp in ici + fwd:
            cp.wait_send()

    return _Stage(inouts=bufs, n_sems=12 * n, start=start, mid=mid, finish=finish, then=then)


def _pair_send_stage(grad, then):
    h = grad.shape[1] // 2

    def copy(ins, outs, sem):
        x, y, c = _mesh_pos()
        return _remote(_half(ins[0], 1 - c, h), outs[0], sem(0), sem(1), (x, y, 1 - c))

    return _Stage(ins=[grad], outs=[jax.ShapeDtypeStruct((N_CHIPS, h, grad.shape[2]), F32)], n_sems=2,
                  start=lambda ins, io, outs, sem: copy(ins, outs, sem).start(),
                  finish=lambda ins, io, outs, sem: copy(ins, outs, sem).wait(), then=then)


def _chip_send_stage(psum, then):
    def copies(ins, outs, sem):
        x, y, c = _mesh_pos()
        return [_remote(ins[0].at[pk], outs[0].at[j], sem(j), sem(3 + j), (px, py, c))
                for j, (px, py, pk) in enumerate(_other_chips(x, y))]

    return _Stage(ins=[psum], outs=[jax.ShapeDtypeStruct((3,) + psum.shape[1:], BF16)], n_sems=6,
                  start=lambda ins, io, outs, sem: _start_all(copies(ins, outs, sem)),
                  finish=lambda ins, io, outs, sem: _wait_all(copies(ins, outs, sem)), then=then,
                  slow=psum.shape[1] * psum.shape[2] * 2 > SLOW_COPY_BYTES)


def _pair_fill_stage(final, then):
    h = final.shape[0] // 2

    def copy(io, sem):
        x, y, c = _mesh_pos()
        mine = _half(io[0], c, h)
        return _remote(mine, mine, sem(0), sem(1), (x, y, 1 - c))

    return _Stage(inouts=[final], n_sems=2,
                  start=lambda ins, io, outs, sem: copy(io, sem).start(),
                  finish=lambda ins, io, outs, sem: copy(io, sem).wait(), then=then)


def _pair_swap_stage(packed, then):
    def copy(ins, outs, sem):
        x, y, c = _mesh_pos()
        return _remote(ins[0], outs[0], sem(0), sem(1), (x, y, 1 - c))

    return _Stage(ins=[packed], outs=[jax.ShapeDtypeStruct(packed.shape, F32)], n_sems=2,
                  start=lambda ins, io, outs, sem: copy(ins, outs, sem).start(),
                  finish=lambda ins, io, outs, sem: copy(ins, outs, sem).wait(), then=then)


def _chip_spread_stage(psum, then):
    def copies(ins, outs, sem):
        x, y, c = _mesh_pos()
        me = 2 * x + y
        cps = [_remote(ins[0], outs[0].at[me], sem(j), sem(3 + j), (px, py, c))
               for j, (px, py, pk) in enumerate(_other_chips(x, y))]
        return cps, pltpu.make_async_copy(ins[0], outs[0].at[me], sem(6))

    def start(ins, io, outs, sem):
        cps, own = copies(ins, outs, sem)
        own.start()
        _start_all(cps)

    def finish(ins, io, outs, sem):
        cps, own = copies(ins, outs, sem)
        _wait_all(cps)
        own.wait()

    return _Stage(ins=[psum], outs=[jax.ShapeDtypeStruct((N_CHIPS,) + psum.shape, F32)], n_sems=7,
                  start=start, finish=finish, then=then)


def _staged_call(core, *, name, grid, in_specs, out_specs, out_shape, scratch_shapes, args, stages):
    n_in, n_out, n_scr = len(args), len(out_shape), len(scratch_shapes)
    s_args, s_outs, aliases, layout = [], [], {}, []
    n_sems = 0
    for st in stages:
        i0, o0 = len(s_args), len(s_outs)
        s_args += st.ins + st.inouts
        for q in range(len(st.inouts)):
            aliases[n_in + i0 + len(st.ins) + q] = n_out + o0 + q
        s_outs += [jax.ShapeDtypeStruct(a.shape, a.dtype) for a in st.inouts] + st.outs
        layout.append((i0, o0, n_sems))
        n_sems += st.n_sems
    steps = 1
    for g in grid:
        steps *= g

    def body(*refs):
        own_in = refs[:n_in]
        s_in = refs[n_in:n_in + len(s_args)]
        rest = refs[n_in + len(s_args):]
        own_out = rest[:n_out]
        s_out = rest[n_out:n_out + len(s_outs)]
        scr = rest[n_out + len(s_outs):]

        def run(which):
            for st, (i0, o0, s0) in zip(stages, layout):
                fn = getattr(st, which)
                if fn is not None:
                    fn(s_in[i0:i0 + len(st.ins)], s_out[o0:o0 + len(st.inouts)],
                       s_out[o0 + len(st.inouts):o0 + len(st.inouts) + len(st.outs)],
                       lambda k, s0=s0: scr[n_scr].at[s0 + k])

        if not stages:
            core(*own_in, *own_out, *scr[:n_scr])
            return
        step = 0
        for d, g in enumerate(grid):
            step = step * g + pl.program_id(d)
        if steps == 1:
            run("start")
            core(*own_in, *own_out, *scr[:n_scr])
            run("mid")
            run("finish")
            return
        pl.when(step == 0)(lambda: run("start"))
        core(*own_in, *own_out, *scr[:n_scr])
        pl.when(step == (3 * steps) // 4)(lambda: run("mid"))
        pl.when(step == steps - 1)(lambda: run("finish"))

    sem = ("arbitrary",) * len(grid) if stages else ("parallel",) * max(len(grid) - 1, 0) + ("arbitrary",) * min(len(grid), 1)
    res = pl.pallas_call(
        body, name=name, grid=grid,
        in_specs=list(in_specs) + [ANY] * len(s_args),
        out_specs=list(out_specs) + [ANY] * len(s_outs),
        out_shape=list(out_shape) + s_outs,
        input_output_aliases=aliases,
        scratch_shapes=list(scratch_shapes) + ([pltpu.SemaphoreType.DMA((n_sems,))] if stages else []),
        compiler_params=_params(sem) if grid else pltpu.CompilerParams(vmem_limit_bytes=V7X_VMEM_LIMIT),
    )(*args, *s_args)
    return list(res[:n_out]), list(res[n_out:])


class _Pipe:
    def __init__(self):
        self.ready = []
        self.flushes = 0

    def add(self, stage):
        self.ready.append(stage)

    def carry(self, call, long=True):
        stages = [st for st in self.ready if long or not st.slow]
        self.ready = [st for st in self.ready if not (long or not st.slow)]
        own, outs = call(stages)
        k = 0
        for st in stages:
            n = len(st.inouts) + len(st.outs)
            st.then(*outs[k:k + n])
            k += n
        return own

    def flush(self):
        while self.ready:
            self.flushes += 1
            self.carry(lambda stages: _staged_call(
                lambda *refs: None, name=f"comm_tail_{self.flushes}", grid=(), in_specs=[], out_specs=[], out_shape=[],
                scratch_shapes=[], args=[], stages=stages))


def _mixer_fwd(layer, x, g1, bgate, lng, lnb, wm, bsf, wsc, win_g, wb_g, wout_g, stages):
    t_len = x.shape[0]
    tm = min(TM_MIX, t_len)
    nt = t_len // tm
    nb = tm // GMLP_BLOCK

    def core(x_ref, g1_ref, bgate_ref, lng_ref, lnb_ref, wm_ref, bsf_ref, wsc_ref, win_hbm, wb_hbm, wout_hbm,
             z_ref, ya_ref, yb_ref, q_ref, a_ref, b_ref, mg_ref, h_ref, x2_ref,
             win_v, wb_v, wout_v, carry, vn_s, f_s, sems):
        i = pl.program_id(0)

        @pl.when(i == 0)
        def _():
            cps = (_load_col_sharded(win_hbm, win_v, sems, 0) + _load_branch(wb_hbm, wb_v, sems, 4)
                   + _load_row_sharded(wout_hbm, wout_v, sems, 12))
            _start_all(cps)
            carry[...] = jnp.zeros_like(carry)
            _wait_all(cps)

        xv = x_ref[...]
        r = lax.rsqrt(jnp.mean(xv * xv, axis=-1, keepdims=True) + RMS_EPS)
        h_ref[...] = (xv * r * g1_ref[...]).astype(BF16)

        def zcols(c0, c1):
            zc = _dot(h_ref[...], win_v[:, c0:c1])
            z_ref[:, c0:c1] = zc.astype(BF16)
            return zc

        vg, _ = _gelu(zcols(C_V, C_V + D_A))
        mu = jnp.mean(vg, axis=-1, keepdims=True)
        vc = vg - mu
        rstd = lax.rsqrt(jnp.mean(vc * vc, axis=-1, keepdims=True) + LN_EPS)
        vn_s[...] = (vc * rstd * lng_ref[...] + lnb_ref[...]).astype(BF16)
        for hd in range(A_HEADS):
            cols = slice(hd * 128, (hd + 1) * 128)
            vcat = jnp.concatenate([vn_s[b * 128:(b + 1) * 128, cols] for b in range(nb)], axis=1)
            fcat = _dot(wm_ref[hd], vcat)
            for b in range(nb):
                f_s[b * 128:(b + 1) * 128, cols] = fcat[:, b * 128:(b + 1) * 128]
        ug, _ = _gelu(zcols(C_U, C_U + D_A))
        bias = jnp.concatenate([bsf_ref[...]] * nb, axis=0)
        ya_ref[...] = (ug * (f_s[...] + bias)).astype(BF16)

        p = zcols(C_CG, C_CG + D_B) * zcols(C_HB, C_HB + D_B)
        cr = carry[...]
        q = wsc_ref[0:1, :] * _shift_down(p, cr, 2) + wsc_ref[1:2, :] * _shift_down(p, cr, 1) + wsc_ref[2:3, :] * p
        carry[...] = p[tm - 8:tm, :]
        q_ref[...] = q.astype(BF16)
        yb_ref[...] = (zcols(C_BG, C_BG + D_B) * q).astype(BF16)

        av = _dot(ya_ref[...], wb_v[0])
        a_ref[...] = av.astype(BF16)
        mg = _sigmoid(zcols(C_GA, C_GA + D_MODEL) + bgate_ref[:, 0:D_MODEL]) * av
        bv = _dot(yb_ref[...], wb_v[1])
        b_ref[...] = bv.astype(BF16)
        mg = mg + _sigmoid(zcols(C_GB, C_GB + D_MODEL) + bgate_ref[:, D_MODEL:2 * D_MODEL]) * bv
        mg_ref[...] = mg.astype(BF16)
        x2_ref[...] = x_ref[...] + _dot(mg_ref[...], wout_v[...])

    outs = [
        jax.ShapeDtypeStruct((t_len, D_IN), BF16),
        jax.ShapeDtypeStruct((t_len, D_A), BF16),
        jax.ShapeDtypeStruct((t_len, D_B), BF16),
        jax.ShapeDtypeStruct((t_len, D_B), BF16),
        jax.ShapeDtypeStruct((t_len, D_MODEL), BF16),
        jax.ShapeDtypeStruct((t_len, D_MODEL), BF16),
        jax.ShapeDtypeStruct((t_len, D_MODEL), BF16),
        jax.ShapeDtypeStruct((t_len, D_MODEL), BF16),
        jax.ShapeDtypeStruct((t_len, D_MODEL), F32),
    ]
    return _staged_call(
        core, name=f"mixer_fwd_l{layer}", grid=(nt,),
        in_specs=[_row_spec(tm, D_MODEL), _const_spec((1, D_MODEL)), _const_spec((1, 2 * D_MODEL)),
                  _const_spec((1, D_A)), _const_spec((1, D_A)), _const_spec((A_HEADS, 128, 128)),
                  _const_spec((128, D_A)), _const_spec((8, D_B)), ANY, ANY, ANY],
        out_specs=[_row_spec(tm, o.shape[1]) for o in outs],
        out_shape=outs,
        scratch_shapes=[pltpu.VMEM((D_MODEL, D_IN), BF16), pltpu.VMEM((2, D_A, D_MODEL), BF16),
                        pltpu.VMEM((D_MODEL, D_MODEL), BF16), pltpu.VMEM((8, D_B), F32),
                        pltpu.VMEM((tm, D_A), BF16), pltpu.VMEM((tm, D_A), F32), pltpu.SemaphoreType.DMA((16,))],
        args=[x, g1, bgate, lng, lnb, wm, bsf, wsc, win_g, wb_g, wout_g], stages=stages)


def _ffn_fwd(layer, x2, g2, wfc, bfc, wup_g, wdown_g, stages, head=None):
    t_len = x2.shape[0]
    tm = min(TM_FFN, t_len)
    nt = t_len // tm

    def core(*refs):
        if head is None:
            (x_ref, g2_ref, wfc_ref, bfc_ref, wup_hbm, wdown_hbm, up_ref, silu_ref, dsilu_ref, act_ref, h_ref, x3_ref,
             wup_v, wdown_v, carry, sems) = refs
        else:
            (x_ref, g2_ref, wfc_ref, bfc_ref, t_ref, gf_ref, wup_hbm, wdown_hbm, up_ref, silu_ref, dsilu_ref, act_ref,
             h_ref, dx_ref, dgf_ref, loss_ref, wup_v, wdown_v, carry, sems) = refs
        i = pl.program_id(0)

        @pl.when(i == 0)
        def _():
            cps = _load_col_sharded(wup_hbm, wup_v, sems, 0) + _load_row_sharded(wdown_hbm, wdown_v, sems, 4)
            _start_all(cps)
            carry[...] = jnp.zeros_like(carry)
            if head is not None:
                dgf_ref[...] = jnp.zeros_like(dgf_ref)
                loss_ref[...] = jnp.zeros_like(loss_ref)
            _wait_all(cps)

        xv = x_ref[...]
        r = lax.rsqrt(jnp.mean(xv * xv, axis=-1, keepdims=True) + RMS_EPS)
        h_ref[...] = (xv * r * g2_ref[...]).astype(BF16)
        gate = _dot(h_ref[...], wup_v[:, 0:D_FF])
        up_ref[:, 0:D_FF] = gate.astype(BF16)
        cr = carry[...]
        gc = (wfc_ref[0:1, :] * _shift_down(gate, cr, 2) + wfc_ref[1:2, :] * _shift_down(gate, cr, 1)
              + wfc_ref[2:3, :] * gate + bfc_ref[...])
        carry[...] = gate[tm - 8:tm, :]
        sg = _sigmoid(gc)
        silu = gc * sg
        silu_ref[...] = silu.astype(BF16)
        dsilu_ref[...] = (sg + silu * (1.0 - sg)).astype(BF16)
        val = _dot(h_ref[...], wup_v[:, D_FF:2 * D_FF])
        up_ref[:, D_FF:2 * D_FF] = val.astype(BF16)
        act_ref[...] = (silu * val).astype(BF16)
        x3 = x_ref[...] + _dot(act_ref[...], wdown_v[...])
        if head is None:
            x3_ref[...] = x3
        else:
            r3 = lax.rsqrt(jnp.mean(x3 * x3, axis=-1, keepdims=True) + RMS_EPS)
            xh = x3 * r3
            err = xh * gf_ref[...] - t_ref[...]
            loss_ref[...] += _colsum8(err * err)
            dy = err * (1.0 / D_MODEL)
            dgf_ref[...] += _colsum8(dy * xh)
            dxh = dy * gf_ref[...]
            dx_ref[...] = r3 * (dxh - xh * jnp.mean(dxh * xh, axis=-1, keepdims=True))

    outs = [
        jax.ShapeDtypeStruct((t_len, 2 * D_FF), BF16),
        jax.ShapeDtypeStruct((t_len, D_FF), BF16),
        jax.ShapeDtypeStruct((t_len, D_FF), BF16),
        jax.ShapeDtypeStruct((t_len, D_FF), BF16),
        jax.ShapeDtypeStruct((t_len, D_MODEL), BF16),
        jax.ShapeDtypeStruct((t_len, D_MODEL), F32),
    ]
    in_specs = [_row_spec(tm, D_MODEL), _const_spec((1, D_MODEL)), _const_spec((8, D_FF)), _const_spec((1, D_FF))]
    out_specs = [_row_spec(tm, o.shape[1]) for o in outs]
    args = [x2, g2, wfc, bfc]
    if head is not None:
        in_specs += [_row_spec(tm, D_MODEL), _const_spec((1, D_MODEL))]
        args += list(head)
        outs += [jax.ShapeDtypeStruct((8, D_MODEL), F32)] * 2
        out_specs += [_const_spec((8, D_MODEL))] * 2
    return _staged_call(
        core, name=f"ffn_fwd_l{layer}", grid=(nt,),
        in_specs=in_specs + [ANY, ANY], out_specs=out_specs, out_shape=outs,
        scratch_shapes=[pltpu.VMEM((D_MODEL, 2 * D_FF), BF16), pltpu.VMEM((D_FF, D_MODEL), BF16),
                        pltpu.VMEM((8, D_FF), F32), pltpu.SemaphoreType.DMA((8,))],
        args=args + [wup_g, wdown_g], stages=stages)


def _ffn_bwd(layer, dx3, x2, up, silu, dsilu, g2, wfc, wup_g, wdown_g, stages):
    t_len = x2.shape[0]
    tm = min(TM_FFN, t_len)
    nt = t_len // tm

    def core(dx3_ref, dx3_late_ref, x_ref, up_ref, silu_ref, dsilu_ref, g2_ref, wfc_ref, wup_hbm, wdown_hbm,
             dx2_ref, dup_ref, dx3b_ref, dg2_ref, dbfc_ref, dwfc_ref,
             wup_v, wdown_v, carry, da_s, dup_s, sems):
        i = pl.program_id(0)

        @pl.when(i == 0)
        def _():
            cps = _load_col_sharded(wup_hbm, wup_v, sems, 0) + _load_row_sharded(wdown_hbm, wdown_v, sems, 4)
            _start_all(cps)
            for ref in (carry, da_s, dup_s, dg2_ref, dbfc_ref, dwfc_ref):
                ref[...] = jnp.zeros_like(ref)
            _wait_all(cps)

        live = (i <= nt).astype(F32)
        dx3b_ref[...] = dx3_ref[...].astype(BF16)
        dh = jnp.zeros((tm, D_MODEL), F32)
        for c0, c1 in FF_CHUNKS:
            v0, v1 = D_FF + c0, D_FF + c1
            dh = dh + _dot_nt(dup_s[:, c0:c1], wup_v[:, c0:c1]) + _dot_nt(dup_s[:, v0:v1], wup_v[:, v0:v1])
            da = da_s[:, c0:c1]
            dval = (da * silu_ref[:, c0:c1].astype(F32)).astype(BF16)
            dup_ref[:, v0:v1] = dval
            dup_s[:, v0:v1] = dval
            dgc = da * up_ref[:, v0:v1].astype(F32) * dsilu_ref[:, c0:c1].astype(F32)
            cr = carry[:, c0:c1]
            dgc1 = _shift_up(dgc, cr, 1)
            dgc2 = _shift_up(dgc, cr, 2)
            carry[:, c0:c1] = jnp.where(i < nt, dgc[0:8, :], cr)
            gate = up_ref[:, c0:c1].astype(F32)
            dbfc_ref[:, c0:c1] += live * _colsum8(dgc)
            dwfc_ref[0, :, c0:c1] += live * _colsum8(dgc2 * gate)
            dwfc_ref[1, :, c0:c1] += live * _colsum8(dgc1 * gate)
            dwfc_ref[2, :, c0:c1] += live * _colsum8(dgc * gate)
            dgate = (wfc_ref[2:3, c0:c1] * dgc + wfc_ref[1:2, c0:c1] * dgc1 + wfc_ref[0:1, c0:c1] * dgc2).astype(BF16)
            dup_ref[:, c0:c1] = dgate
            dup_s[:, c0:c1] = dgate
            da_s[:, c0:c1] = _dot_nt(dx3b_ref[...], wdown_v[c0:c1, :])
        xv = x_ref[...]
        r = lax.rsqrt(jnp.mean(xv * xv, axis=-1, keepdims=True) + RMS_EPS)
        xh = xv * r
        dg2_ref[...] += _colsum8(dh * xh)
        dxh = dh * g2_ref[...]
        dx2_ref[...] = dx3_late_ref[...] + r * (dxh - xh * jnp.mean(dxh * xh, axis=-1, keepdims=True))

    def tile(n, lag):
        return pl.BlockSpec((tm, n), lambda i: (nt - 1 - jnp.clip(i - lag, 0, nt - 1), 0))

    outs = [
        jax.ShapeDtypeStruct((t_len, D_MODEL), F32),
        jax.ShapeDtypeStruct((t_len, 2 * D_FF), BF16),
        jax.ShapeDtypeStruct((t_len, D_MODEL), BF16),
        jax.ShapeDtypeStruct((8, D_MODEL), F32),
        jax.ShapeDtypeStruct((8, D_FF), F32),
        jax.ShapeDtypeStruct((3, 8, D_FF), F32),
    ]
    return _staged_call(
        core, name=f"ffn_bwd_l{layer}", grid=(nt + 2,),
        in_specs=[tile(D_MODEL, 0), tile(D_MODEL, 2), tile(D_MODEL, 2), tile(2 * D_FF, 1), tile(D_FF, 1), tile(D_FF, 1),
                  _const_spec((1, D_MODEL)), _const_spec((8, D_FF)), ANY, ANY],
        out_specs=[tile(D_MODEL, 2), tile(2 * D_FF, 1), tile(D_MODEL, 0),
                   _const_spec((8, D_MODEL)), _const_spec((8, D_FF)), _const_spec((3, 8, D_FF))],
        out_shape=outs,
        scratch_shapes=[pltpu.VMEM((D_MODEL, 2 * D_FF), BF16), pltpu.VMEM((D_FF, D_MODEL), BF16),
                        pltpu.VMEM((8, D_FF), F32), pltpu.VMEM((tm, D_FF), F32), pltpu.VMEM((tm, 2 * D_FF), BF16),
                        pltpu.SemaphoreType.DMA((8,))],
        args=[dx3, dx3, x2, up, silu, dsilu, g2, wfc, wup_g, wdown_g], stages=stages)


def _mixer_bwd(layer, dx2, x, z, qs, av, bv, g1, bgate, lng, lnb, wm, wmt, bsf, wsc, win_g, wb_g, wout_g, stages):
    t_len = x.shape[0]
    tm = min(TM_MIX, t_len)
    nt = t_len // tm
    nb = tm // GMLP_BLOCK

    def core(dx2_ref, x_ref, z_ref, q_ref, a_ref, b_ref, g1_ref, bgate_ref, lng_ref, lnb_ref,
             wm_ref, wmt_ref, bsf_ref, wsc_ref, win_hbm, wb_hbm, wout_hbm,
             dx_ref, dz_ref, da_ref, db_ref, dx2b_ref, dg1_ref, dbgate_ref, dlng_ref, dlnb_ref, dwm_ref, dbsf_ref, dwsc_ref,
             win_v, wb_v, wout_v, carry, vn_s, f_s, df_s, dvn_s, sems):
        i = pl.program_id(0)

        @pl.when(i == 0)
        def _():
            cps = (_load_col_sharded(win_hbm, win_v, sems, 0) + _load_branch(wb_hbm, wb_v, sems, 4)
                   + _load_row_sharded(wout_hbm, wout_v, sems, 12))
            _start_all(cps)
            for ref in (carry, dg1_ref, dbgate_ref, dlng_ref, dlnb_ref, dwm_ref, dbsf_ref, dwsc_ref):
                ref[...] = jnp.zeros_like(ref)
            _wait_all(cps)

        def zc(c0, n):
            return z_ref[:, c0:c0 + n].astype(F32)

        def dz_cols(c0, n, val):
            dz_ref[:, c0:c0 + n] = val.astype(BF16)
            return _dot_nt(dz_ref[:, c0:c0 + n], win_v[:, c0:c0 + n])

        dx2b_ref[...] = dx2_ref[...].astype(BF16)
        dm = _dot_nt(dx2b_ref[...], wout_v[...])
        sa = _sigmoid(zc(C_GA, D_MODEL) + bgate_ref[:, 0:D_MODEL])
        da_ref[...] = (dm * sa).astype(BF16)
        dga = dm * a_ref[...].astype(F32) * sa * (1.0 - sa)
        dh = dz_cols(C_GA, D_MODEL, dga)
        dbgate_ref[:, 0:D_MODEL] += _colsum8(dga)
        dya = _dot_nt(da_ref[...], wb_v[0])
        sb = _sigmoid(zc(C_GB, D_MODEL) + bgate_ref[:, D_MODEL:2 * D_MODEL])
        db_ref[...] = (dm * sb).astype(BF16)
        dgb = dm * b_ref[...].astype(F32) * sb * (1.0 - sb)
        dh = dh + dz_cols(C_GB, D_MODEL, dgb)
        dbgate_ref[:, D_MODEL:2 * D_MODEL] += _colsum8(dgb)
        dyb = _dot_nt(db_ref[...], wb_v[1])

        v = zc(C_V, D_A)
        vg, tv = _gelu(v)
        mu = jnp.mean(vg, axis=-1, keepdims=True)
        vc = vg - mu
        rstd = lax.rsqrt(jnp.mean(vc * vc, axis=-1, keepdims=True) + LN_EPS)
        xh = vc * rstd
        vn_s[...] = (xh * lng_ref[...] + lnb_ref[...]).astype(BF16)
        u = zc(C_U, D_A)
        ug, tu = _gelu(u)
        df = dya * ug
        df_s[...] = df.astype(BF16)
        dbsf_acc = df[0:128, :]
        for b in range(1, nb):
            dbsf_acc = dbsf_acc + df[b * 128:(b + 1) * 128, :]
        dbsf_ref[...] += dbsf_acc
        for hd in range(A_HEADS):
            cols = slice(hd * 128, (hd + 1) * 128)
            vcat = jnp.concatenate([vn_s[b * 128:(b + 1) * 128, cols] for b in range(nb)], axis=1)
            dcat = jnp.concatenate([df_s[b * 128:(b + 1) * 128, cols] for b in range(nb)], axis=1)
            fcat = _dot(wm_ref[hd], vcat)
            gcat = _dot(wmt_ref[hd], dcat)
            dwm_ref[hd] += _dot_nt(dcat, vcat)
            for b in range(nb):
                f_s[b * 128:(b + 1) * 128, cols] = fcat[:, b * 128:(b + 1) * 128]
                dvn_s[b * 128:(b + 1) * 128, cols] = gcat[:, b * 128:(b + 1) * 128]
        bias = jnp.concatenate([bsf_ref[...]] * nb, axis=0)
        dh = dh + dz_cols(C_U, D_A, dya * (f_s[...] + bias) * _gelu_grad(u, tu))
        dvn = dvn_s[...]
        dlng_ref[...] += _colsum8(dvn * xh)
        dlnb_ref[...] += _colsum8(dvn)
        dxh = dvn * lng_ref[...]
        dvg = rstd * (dxh - jnp.mean(dxh, axis=-1, keepdims=True) - xh * jnp.mean(dxh * xh, axis=-1, keepdims=True))
        dh = dh + dz_cols(C_V, D_A, dvg * _gelu_grad(v, tv))

        cg = zc(C_CG, D_B)
        hbv = zc(C_HB, D_B)
        p = cg * hbv
        dh = dh + dz_cols(C_BG, D_B, dyb * q_ref[...].astype(F32))
        dq = dyb * zc(C_BG, D_B)
        cr = carry[...]
        dq1 = _shift_up(dq, cr, 1)
        dq2 = _shift_up(dq, cr, 2)
        carry[...] = dq[0:8, :]
        dwsc_ref[0] += _colsum8(dq2 * p)
        dwsc_ref[1] += _colsum8(dq1 * p)
        dwsc_ref[2] += _colsum8(dq * p)
        dp = wsc_ref[2:3, :] * dq + wsc_ref[1:2, :] * dq1 + wsc_ref[0:1, :] * dq2
        dh = dh + dz_cols(C_CG, D_B, dp * hbv)
        dh = dh + dz_cols(C_HB, D_B, dp * cg)

        xv = x_ref[...]
        r = lax.rsqrt(jnp.mean(xv * xv, axis=-1, keepdims=True) + RMS_EPS)
        xn = xv * r
        dg1_ref[...] += _colsum8(dh * xn)
        dxn = dh * g1_ref[...]
        dx_ref[...] = dx2_ref[...] + r * (dxn - xn * jnp.mean(dxn * xn, axis=-1, keepdims=True))

    outs = [
        jax.ShapeDtypeStruct((t_len, D_MODEL), F32),
        jax.ShapeDtypeStruct((t_len, D_IN), BF16),
        jax.ShapeDtypeStruct((t_len, D_MODEL), BF16),
        jax.ShapeDtypeStruct((t_len, D_MODEL), BF16),
        jax.ShapeDtypeStruct((t_len, D_MODEL), BF16),
        jax.ShapeDtypeStruct((8, D_MODEL), F32),
        jax.ShapeDtypeStruct((8, 2 * D_MODEL), F32),
        jax.ShapeDtypeStruct((8, D_A), F32),
        jax.ShapeDtypeStruct((8, D_A), F32),
        jax.ShapeDtypeStruct((A_HEADS, 128, 128), F32),
        jax.ShapeDtypeStruct((128, D_A), F32),
        jax.ShapeDtypeStruct((3, 8, D_B), F32),
    ]

    return _staged_call(
        core, name=f"mixer_bwd_l{layer}", grid=(nt,),
        in_specs=[_row_spec(tm, D_MODEL, nt), _row_spec(tm, D_MODEL, nt), _row_spec(tm, D_IN, nt),
                  _row_spec(tm, D_B, nt), _row_spec(tm, D_MODEL, nt), _row_spec(tm, D_MODEL, nt),
                  _const_spec((1, D_MODEL)), _const_spec((1, 2 * D_MODEL)), _const_spec((1, D_A)), _const_spec((1, D_A)),
                  _const_spec((A_HEADS, 128, 128)), _const_spec((A_HEADS, 128, 128)), _const_spec((128, D_A)),
                  _const_spec((8, D_B)), ANY, ANY, ANY],
        out_specs=[_row_spec(tm, D_MODEL, nt), _row_spec(tm, D_IN, nt), _row_spec(tm, D_MODEL, nt),
                   _row_spec(tm, D_MODEL, nt), _row_spec(tm, D_MODEL, nt),
                   _const_spec((8, D_MODEL)), _const_spec((8, 2 * D_MODEL)), _const_spec((8, D_A)), _const_spec((8, D_A)),
                   _const_spec((A_HEADS, 128, 128)), _const_spec((128, D_A)), _const_spec((3, 8, D_B))],
        out_shape=outs,
        scratch_shapes=[pltpu.VMEM((D_MODEL, D_IN), BF16), pltpu.VMEM((2, D_A, D_MODEL), BF16),
                        pltpu.VMEM((D_MODEL, D_MODEL), BF16), pltpu.VMEM((8, D_B), F32),
                        pltpu.VMEM((tm, D_A), BF16), pltpu.VMEM((tm, D_A), F32), pltpu.VMEM((tm, D_A), BF16),
                        pltpu.VMEM((tm, D_A), F32), pltpu.SemaphoreType.DMA((16,))],
        args=[dx2, x, z, qs, av, bv, g1, bgate, lng, lnb, wm, wmt, bsf, wsc, win_g, wb_g, wout_g], stages=stages)


def _wgrad(name, layer, a, b, rows, cols, row_blk, col_blk, stages, a_first=0):
    t_len = a.shape[0]
    n = b.shape[1]
    tk = min(TK_WGRAD, t_len)
    col_sharded = n == N_CHIPS * cols
    m = rows if col_sharded else a.shape[1]
    grid = (m // row_blk, n // col_blk, t_len // tk)
    per_shard_c = cols // col_blk

    if col_sharded:
        out_shape = (N_CHIPS, rows, cols)
        out_spec = pl.BlockSpec((None, row_blk, col_blk), lambda i, j, k: (j // per_shard_c, i, j % per_shard_c))
    else:
        out_shape = (N_CHIPS * rows, cols)
        out_spec = pl.BlockSpec((row_blk, col_blk), lambda i, j, k: (i, j))

    def core(a_ref, b_ref, o_ref):
        @pl.when(pl.program_id(2) == 0)
        def _():
            o_ref[...] = jnp.zeros_like(o_ref)

        o_ref[...] += _dot_tn(a_ref[...], b_ref[...])

    own, outs = _staged_call(
        core, name=f"wgrad_{name}_l{layer}", grid=grid,
        in_specs=[pl.BlockSpec((tk, row_blk), lambda i, j, k: (k, a_first + i)),
                  pl.BlockSpec((tk, col_blk), lambda i, j, k: (k, j))],
        out_specs=[out_spec], out_shape=[jax.ShapeDtypeStruct(out_shape, F32)], scratch_shapes=[],
        args=[a, b], stages=stages)
    return [own[0].reshape(N_CHIPS, rows, cols)], outs


def _wgrad_branch(layer, ya, da, yb, db, stages):
    t_len = ya.shape[0]
    tk = min(TK_WGRAD, t_len)

    def core(ya_ref, da_ref, yb_ref, db_ref, o_ref):
        @pl.when(pl.program_id(1) == 0)
        def _():
            o_ref[...] = jnp.zeros_like(o_ref)

        o_ref[0:D_A, :] += _dot_tn(ya_ref[...], da_ref[...])
        o_ref[D_A:2 * D_A, :] += _dot_tn(yb_ref[...], db_ref[...])

    a_spec = pl.BlockSpec((tk, D_A), lambda j, k: (k, 0))
    d_spec = pl.BlockSpec((tk, 256), lambda j, k: (k, j))
    return _staged_call(
        core, name=f"wgrad_w_branch_l{layer}", grid=(N_CHIPS, t_len // tk),
        in_specs=[a_spec, d_spec, a_spec, d_spec],
        out_specs=[pl.BlockSpec((None, 2 * D_A, 256), lambda j, k: (j, 0, 0))],
        out_shape=[jax.ShapeDtypeStruct((N_CHIPS, 2 * D_A, 256), F32)], scratch_shapes=[],
        args=[ya, da, yb, db], stages=stages)


def _all_reduce_small(name, packed):
    rows = packed.shape[0]

    def body(src_ref, out_ref, slots, send, recv):
        x, y, c = _mesh_pos()
        me = 4 * x + 2 * y + c
        cps = []
        for d in range(1, N_DEVICES):
            peer = me ^ d
            cps.append(_remote(src_ref, slots.at[me], send.at[d - 1], recv.at[d - 1],
                               (peer // 4, (peer // 2) % 2, peer % 2)))
        _start_all(cps)
        slots[me] = src_ref[...]
        _wait_all(cps)
        acc = slots[0]
        for d in range(1, N_DEVICES):
            acc = acc + slots[d]
        out_ref[...] = acc

    return pl.pallas_call(
        body, name=f"all_reduce_{name}",
        in_specs=[pl.BlockSpec(memory_space=pltpu.VMEM)], out_specs=pl.BlockSpec(memory_space=pltpu.VMEM),
        out_shape=jax.ShapeDtypeStruct(packed.shape, F32),
        scratch_shapes=[pltpu.VMEM((N_DEVICES, rows, 128), F32), pltpu.SemaphoreType.DMA((7,)),
                        pltpu.SemaphoreType.DMA((7,))],
        compiler_params=pltpu.CompilerParams(vmem_limit_bytes=V7X_VMEM_LIMIT),
    )(packed)


def _flat_blk(rows, cols):
    blk = rows
    while blk * cols * 4 > 2 * 1024 * 1024 and blk % 16 == 0:
        blk //= 2
    return blk


def _cast_into_slot(name, layer, w, chip):
    _, rows, cols = w.shape
    blk = _flat_blk(rows, cols)

    def body(chip_ref, w_ref, o_ref):
        o_ref[...] = w_ref[...].astype(BF16)

    return pl.pallas_call(
        body, name=f"cast_{name}_l{layer}",
        grid_spec=pltpu.PrefetchScalarGridSpec(
            num_scalar_prefetch=1, grid=(rows // blk,),
            in_specs=[pl.BlockSpec((None, blk, cols), lambda i, chip_ref: (layer, i, 0))],
            out_specs=pl.BlockSpec((None, blk, cols), lambda i, chip_ref: (chip_ref[0], i, 0))),
        out_shape=jax.ShapeDtypeStruct((N_CHIPS, rows, cols), BF16),
        compiler_params=_params(("parallel",)),
    )(chip, w)


def _pair_sum(name, grad, other, core):
    _, h, cols = other.shape
    blk = _flat_blk(h, cols)
    nblk = h // blk

    def body(core_ref, g_ref, o_ref, s_ref):
        s_ref[...] = (g_ref[...] + o_ref[...]).astype(BF16)

    spec = pl.BlockSpec((None, blk, cols), lambda k, i, core_ref: (k, i, 0))
    return pl.pallas_call(
        body, name=f"pair_sum_{name}",
        grid_spec=pltpu.PrefetchScalarGridSpec(
            num_scalar_prefetch=1, grid=(N_CHIPS, nblk),
            in_specs=[pl.BlockSpec((None, blk, cols), lambda k, i, core_ref: (k, core_ref[0] * nblk + i, 0)), spec],
            out_specs=spec),
        out_shape=jax.ShapeDtypeStruct((N_CHIPS, h, cols), BF16),
        compiler_params=_params(("parallel", "parallel")),
    )(core, grad, other)


def _chip_sum(name, grad, other, got, pos):
    _, rows, cols = grad.shape
    h = rows // 2
    blk = _flat_blk(h, cols)
    nblk = h // blk

    def body(pos_ref, g_ref, o_ref, r_ref, f_ref):
        f_ref[...] = (((g_ref[...] + o_ref[...]) + r_ref[0].astype(F32)) + r_ref[1].astype(F32)) + r_ref[2].astype(F32)

    return pl.pallas_call(
        body, name=f"chip_sum_{name}",
        grid_spec=pltpu.PrefetchScalarGridSpec(
            num_scalar_prefetch=1, grid=(nblk,),
            in_specs=[pl.BlockSpec((None, blk, cols), lambda i, pos_ref: (pos_ref[0], pos_ref[1] * nblk + i, 0)),
                      pl.BlockSpec((None, blk, cols), lambda i, pos_ref: (pos_ref[0], i, 0)),
                      pl.BlockSpec((3, blk, cols), lambda i, pos_ref: (0, i, 0))],
            out_specs=pl.BlockSpec((blk, cols), lambda i, pos_ref: (pos_ref[1] * nblk + i, 0))),
        out_shape=jax.ShapeDtypeStruct((rows, cols), F32),
        compiler_params=_params(("parallel",)),
    )(pos, grad, other, got)


def _sum_slots(name, slots):
    n, rows, _ = slots.shape

    def body(s_ref, o_ref):
        acc = s_ref[0]
        for d in range(1, n):
            acc = acc + s_ref[d]
        o_ref[...] = acc

    return pl.pallas_call(
        body, name=f"sum_slots_{name}", grid=(1,),
        in_specs=[pl.BlockSpec((n, rows, 128), lambda i: (0, 0, 0))],
        out_specs=pl.BlockSpec((rows, 128), lambda i: (0, 0)),
        out_shape=jax.ShapeDtypeStruct((rows, 128), F32),
        compiler_params=_params(),
    )(slots)


def _adamw_math(w, g, m, v):
    m2 = ADAM_B1 * m + (1.0 - ADAM_B1) * g
    v2 = ADAM_B2 * v + (1.0 - ADAM_B2) * (g * g)
    m_hat = m2 / (1.0 - ADAM_B1 ** ADAM_STEP)
    v_hat = v2 / (1.0 - ADAM_B2 ** ADAM_STEP)
    delta = -ADAM_LR * (m_hat / (jnp.sqrt(v_hat) + ADAM_EPS) + ADAM_WD * w)
    return delta, m2, v2


def _adamw_big(name, w, g0, g1, m, v):
    _, rows, cols = w.shape
    blk = _flat_blk(rows, cols) // 2

    def body(w_ref, g0_ref, g1_ref, m_ref, v_ref, g_ref, d_ref, m2_ref, v2_ref):
        g = jnp.where(pl.program_id(0) == 0, g0_ref[...], g1_ref[...])
        d, m2, v2 = _adamw_math(w_ref[...], g, m_ref[...], v_ref[...])
        g_ref[...] = g
        d_ref[...] = d
        m2_ref[...] = m2
        v2_ref[...] = v2

    spec = pl.BlockSpec((None, blk, cols), lambda la, i: (la, i, 0))
    return pl.pallas_call(
        body, name=f"adamw_{name}", grid=(N_LAYERS, rows // blk),
        in_specs=[spec, pl.BlockSpec((blk, cols), lambda la, i: (i * (1 - la), 0)),
                  pl.BlockSpec((blk, cols), lambda la, i: (i * la, 0)), spec, spec],
        out_specs=[spec] * 4,
        out_shape=[jax.ShapeDtypeStruct(w.shape, F32)] * 4,
        compiler_params=_params(("parallel", "parallel")),
    )(w, g0, g1, m, v)


def _adamw(name, w, g, m, v):
    rows, cols = w.shape
    blk = _flat_blk(rows, cols)

    def body(w_ref, g_ref, m_ref, v_ref, d_ref, m2_ref, v2_ref):
        d, m2, v2 = _adamw_math(w_ref[...], g_ref[...], m_ref[...], v_ref[...])
        d_ref[...] = d
        m2_ref[...] = m2
        v2_ref[...] = v2

    spec = pl.BlockSpec((blk, cols), lambda i: (i, 0))
    return pl.pallas_call(
        body, name=f"adamw_{name}", grid=(rows // blk,),
        in_specs=[spec] * 4, out_specs=[spec] * 3,
        out_shape=[jax.ShapeDtypeStruct((rows, cols), F32)] * 3,
        compiler_params=_params(("parallel",)),
    )(w, g, m, v)


SMALL = ("norm1_g", "b_gate", "gmlp_ln_g", "gmlp_ln_b", "w_spatial", "b_spatial", "w_shortconv", "norm2_g",
         "w_ffn_conv", "b_ffn_conv", "final_g")
ALL_WEIGHTS = ("norm1_g", "w_in", "b_gate", "gmlp_ln_g", "gmlp_ln_b", "w_spatial", "b_spatial", "w_shortconv",
               "w_branch", "w_out", "norm2_g", "w_ffn_up", "w_ffn_conv", "b_ffn_conv", "w_ffn_down", "final_g")


def _pack(arrays):
    flat = jnp.concatenate([a.reshape(-1) for a in arrays])
    n = flat.shape[0]
    rows = -(-n // 1024) * 8
    return jnp.pad(flat, (0, rows * 128 - n)).reshape(rows, 128)


def _unpack(packed, like):
    flat = packed.reshape(-1)
    out, off = [], 0
    for a in like:
        out.append(flat[off:off + a.size].reshape(a.shape))
        off += a.size
    return out


def _pad8(w):
    return jnp.pad(w, ((0, 5), (0, 0)))


def kernel(x, norm1_g, w_in, b_gate, gmlp_ln_g, gmlp_ln_b, w_spatial, b_spatial, w_shortconv, w_branch, w_out, norm2_g, w_ffn_up, w_ffn_conv, b_ffn_conv, w_ffn_down, final_g, loss_target, m_norm1_g, m_w_in, m_b_gate, m_gmlp_ln_g, m_gmlp_ln_b, m_w_spatial, m_b_spatial, m_w_shortconv, m_w_branch, m_w_out, m_norm2_g, m_w_ffn_up, m_w_ffn_conv, m_b_ffn_conv, m_w_ffn_down, m_final_g, v_norm1_g, v_w_in, v_b_gate, v_gmlp_ln_g, v_gmlp_ln_b, v_w_spatial, v_b_spatial, v_w_shortconv, v_w_branch, v_w_out, v_norm2_g, v_w_ffn_up, v_w_ffn_conv, v_b_ffn_conv, v_w_ffn_down, v_final_g):
    weights = dict(norm1_g=norm1_g, w_in=w_in, b_gate=b_gate, gmlp_ln_g=gmlp_ln_g, gmlp_ln_b=gmlp_ln_b,
                   w_spatial=w_spatial, b_spatial=b_spatial, w_shortconv=w_shortconv, w_branch=w_branch, w_out=w_out,
                   norm2_g=norm2_g, w_ffn_up=w_ffn_up, w_ffn_conv=w_ffn_conv, b_ffn_conv=b_ffn_conv,
                   w_ffn_down=w_ffn_down, final_g=final_g)
    mom = dict(norm1_g=m_norm1_g, w_in=m_w_in, b_gate=m_b_gate, gmlp_ln_g=m_gmlp_ln_g, gmlp_ln_b=m_gmlp_ln_b,
               w_spatial=m_w_spatial, b_spatial=m_b_spatial, w_shortconv=m_w_shortconv, w_branch=m_w_branch,
               w_out=m_w_out, norm2_g=m_norm2_g, w_ffn_up=m_w_ffn_up, w_ffn_conv=m_w_ffn_conv,
               b_ffn_conv=m_b_ffn_conv, w_ffn_down=m_w_ffn_down, final_g=m_final_g)
    vel = dict(norm1_g=v_norm1_g, w_in=v_w_in, b_gate=v_b_gate, gmlp_ln_g=v_gmlp_ln_g, gmlp_ln_b=v_gmlp_ln_b,
               w_spatial=v_w_spatial, b_spatial=v_b_spatial, w_shortconv=v_w_shortconv, w_branch=v_w_branch,
               w_out=v_w_out, norm2_g=v_norm2_g, w_ffn_up=v_w_ffn_up, w_ffn_conv=v_w_ffn_conv,
               b_ffn_conv=v_b_ffn_conv, w_ffn_down=v_w_ffn_down, final_g=v_final_g)

    cx, cy, cc = _mesh_pos()
    chip = 2 * cx + cy
    core_arr = cc.astype(jnp.int32).reshape(1)
    chip_arr = chip.astype(jnp.int32).reshape(1)
    pos_arr = jnp.stack([chip, cc]).astype(jnp.int32)
    t_len = x.shape[1]
    xs = x.reshape(t_len, D_MODEL)
    target = loss_target.reshape(t_len, D_MODEL)
    pipe = _Pipe()

    full = {}

    def gather(keys):
        slots = [_cast_into_slot(n, la, weights[n].reshape((N_LAYERS,) + BIG[n]), chip_arr) for n, la in keys]

        def then(*bufs):
            full.update(zip(keys, bufs))

        pipe.add(_gather_stage(slots, then))

    mixer_w = ("w_in", "w_branch", "w_out")
    ffn_w = ("w_ffn_up", "w_ffn_down")
    gather([(n, 0) for n in mixer_w])
    pipe.flush()

    idx = jnp.arange(GMLP_BLOCK) // CHUNK
    mask = idx[None, :] <= idx[:, None]
    wm_all = jnp.where(mask[None, None], w_spatial, 0.0)
    wm_bf = wm_all.astype(BF16)
    wmt_bf = jnp.swapaxes(wm_all, -1, -2).astype(BF16)
    bsf = jnp.repeat(jnp.swapaxes(b_spatial, -1, -2), 128, axis=-1)
    wsc_full = lax.dynamic_update_slice(jnp.zeros((N_LAYERS, 3, D_B), F32), w_shortconv, (0, 0, chip * (D_B // 4)))
    wfc_full = lax.dynamic_update_slice(jnp.zeros((N_LAYERS, 3, D_FF), F32), w_ffn_conv, (0, 0, chip * (D_FF // 4)))
    taps = _all_reduce_small("conv_taps", _pack([wsc_full, wfc_full]))
    wsc_full, wfc_full = _unpack(taps * 0.5, [wsc_full, wfc_full])

    def row(a):
        return a.reshape(1, -1)

    def mixer_args(la):
        return (row(norm1_g[la]), row(b_gate[la]), row(gmlp_ln_g[la]), row(gmlp_ln_b[la]))

    def mixer_weights(la):
        return tuple(full[(n, la)] for n in mixer_w)

    def ffn_weights(la):
        return tuple(full[(n, la)] for n in ffn_w)

    saved = []
    h_in = xs
    for la in range(N_LAYERS):
        gather([(n, la) for n in ffn_w])
        z, ya, yb, qs, av, bv, mg, h1, x2 = pipe.carry(lambda st: _mixer_fwd(
            la, h_in, *mixer_args(la), wm_bf[la], bsf[la], _pad8(wsc_full[la]), *mixer_weights(la), st))
        if la + 1 < N_LAYERS:
            gather([(n, la + 1) for n in mixer_w])
        head = (target, row(final_g)) if la == N_LAYERS - 1 else None
        up, silu, dsilu, act, h2, *rest = pipe.carry(lambda st: _ffn_fwd(
            la, x2, row(norm2_g[la]), _pad8(wfc_full[la]), row(b_ffn_conv[la]), *ffn_weights(la), st, head=head))
        saved.append(dict(x=h_in, z=z, ya=ya, yb=yb, q=qs, av=av, bv=bv, mg=mg, h1=h1, x2=x2, up=up, silu=silu,
                          dsilu=dsilu, act=act, h2=h2))
        h_in = rest[0]
    dx, dgf8, loss8 = rest

    reduced_big = {}

    def reduce_big(name, la, grad):
        tag = f"{name}_l{la}"

        def after_pair(other):
            psum = _pair_sum(tag, grad, other, core_arr)

            def after_chips(got):
                final = _chip_sum(tag, grad, other, got, pos_arr)
                pipe.add(_pair_fill_stage(final, lambda done: reduced_big.__setitem__((name, la), done)))

            pipe.add(_chip_send_stage(psum, after_chips))

        pipe.add(_pair_send_stage(grad, after_pair))

    small = {n: [None] * N_LAYERS for n in SMALL}
    spread = {}
    for la in reversed(range(N_LAYERS)):
        s = saved[la]
        dx3 = dx
        dx2, dup, dx3b, dg2, dbfc, dwfc = pipe.carry(lambda st: _ffn_bwd(
            la, dx3, s["x2"], s["up"], s["silu"], s["dsilu"], row(norm2_g[la]), _pad8(wfc_full[la]),
            *ffn_weights(la), st))
        g, = pipe.carry(lambda st: _wgrad("w_ffn_down", la, s["act"], dx3b, 704, 1024, 1408, 1024, st))
        reduce_big("w_ffn_down", la, g)
        g, = pipe.carry(lambda st: _wgrad("w_ffn_up", la, s["h2"], dup, 1024, 1408, 1024, 1408, st))
        reduce_big("w_ffn_up", la, g)
        dxl, dz, da, db, dx2b, dg1, dbg, dlng, dlnb, dwm, dbsf, dwsc = pipe.carry(lambda st: _mixer_bwd(
            la, dx2, s["x"], s["z"], s["q"], s["av"], s["bv"], *mixer_args(la), wm_bf[la], wmt_bf[la], bsf[la],
            _pad8(wsc_full[la]), *mixer_weights(la), st))
        small["norm1_g"][la] = dg1.sum(0)
        small["b_gate"][la] = dbg.sum(0)
        small["gmlp_ln_g"][la] = dlng.sum(0)
        small["gmlp_ln_b"][la] = dlnb.sum(0)
        small["w_spatial"][la] = jnp.where(mask[None], dwm, 0.0)
        small["b_spatial"][la] = dbsf.reshape(128, A_HEADS, 128).sum(-1).T
        small["w_shortconv"][la] = dwsc.sum(1)
        small["norm2_g"][la] = dg2.sum(0)
        small["w_ffn_conv"][la] = dwfc.sum(1)
        small["b_ffn_conv"][la] = dbfc.sum(0)
        if la == 0:
            small_local = ([jnp.stack(small[n]) for n in SMALL[:-1]]
                           + [dgf8.sum(0), 0.5 * loss8.sum().reshape(1) / D_MODEL])
            mine = _pack(small_local)

            def after_swap(other, mine=mine):
                pair = _sum_slots("small_pair", jnp.stack([mine, other]))
                pipe.add(_chip_spread_stage(pair, lambda slots: spread.__setitem__("slots", slots)))

            pipe.add(_pair_swap_stage(mine, after_swap))
        for part, tag in enumerate(("w_in_a", "w_in_b")):
            g, = pipe.carry(lambda st: _wgrad(tag, la, s["h1"], dz, 512, 1152, 512, 1152, st, a_first=part))
            reduce_big(tag, la, g)
        g, = pipe.carry(lambda st: _wgrad("w_out", la, s["mg"], dx2b, 256, 1024, 1024, 1024, st), long=False)
        reduce_big("w_out", la, g)
        g, = pipe.carry(lambda st: _wgrad_branch(la, s["ya"], da, s["yb"], db, st), long=False)
        reduce_big("w_branch", la, g)
        dx = dxl
    grad_x = dx.reshape(x.shape)
    pipe.flush()

    for la in range(N_LAYERS):
        reduced_big[("w_in", la)] = jnp.concatenate([reduced_big[("w_in_a", la)], reduced_big[("w_in_b", la)]], axis=0)
    reduced = _unpack(_sum_slots("small_grads", spread["slots"]), small_local)
    loss = reduced[-1].reshape(())
    grads = dict(zip(SMALL, reduced[:-1]))
    grads["w_shortconv"] = lax.dynamic_slice(grads["w_shortconv"], (0, 0, chip * (D_B // 4)), (N_LAYERS, 3, D_B // 4))
    grads["w_ffn_conv"] = lax.dynamic_slice(grads["w_ffn_conv"], (0, 0, chip * (D_FF // 4)), (N_LAYERS, 3, D_FF // 4))

    delta, new_m, new_v = {}, {}, {}
    for n in BIG_NAMES:
        shape3 = (N_LAYERS,) + BIG[n]
        res = _adamw_big(n, weights[n].reshape(shape3), reduced_big[(n, 0)], reduced_big[(n, 1)],
                         mom[n].reshape(shape3), vel[n].reshape(shape3))
        grads[n], delta[n], new_m[n], new_v[n] = (a.reshape(weights[n].shape) for a in res)
    small_w = [weights[n] for n in SMALL]
    packed = [_pack([src[n] for n in SMALL]) for src in (weights, grads, mom, vel)]
    for dst, res in zip((delta, new_m, new_v), _adamw("small", *packed)):
        dst.update(zip(SMALL, _unpack(res, small_w)))

    return (loss, grad_x, *[grads[n] for n in ALL_WEIGHTS], *[delta[n] for n in ALL_WEIGHTS],
            *[new_m[n] for n in ALL_WEIGHTS], *[new_v[n] for n in ALL_WEIGHTS])
```

```python
import jax
import jax.numpy as jnp
from jax import lax
from jax.experimental import pallas as pl
from jax.experimental.pallas import tpu as pltpu

F32 = jnp.float32
BF16 = jnp.bfloat16
MESH = pl.DeviceIdType.MESH
ANY = pl.BlockSpec(memory_space=pl.ANY)

D_MODEL = 1024
D_A = 512
D_B = 512
D_IN = 4608
D_FF = 2816
GMLP_BLOCK = 128
CHUNK = 64
A_HEADS = 4
N_LAYERS = 2
N_CHIPS = 4
N_DEVICES = 8
RMS_EPS = 1e-6
LN_EPS = 1e-5
ADAM_LR = 0.001
ADAM_B1 = 0.9
ADAM_B2 = 0.999
ADAM_EPS = 1e-08
ADAM_WD = 0.01
ADAM_STEP = 10

C_U, C_V, C_BG, C_CG, C_HB, C_GA, C_GB = 0, 512, 1024, 1536, 2048, 2560, 3584

V7X_VMEM_LIMIT = 60 * 1024 * 1024
TM_MIX = 256
TM_FFN = 256
TK_WGRAD = 2048
SLOW_COPY_BYTES = 640 * 1024
FF_CHUNKS = ((0, 768), (768, 1536), (1536, 2304), (2304, 2816))
GELU_C0 = 0.7978845608028654
GELU_C1 = 0.044715

BIG = {
    "w_in": (1024, 1152),
    "w_branch": (1024, 256),
    "w_out": (256, 1024),
    "w_ffn_up": (1024, 1408),
    "w_ffn_down": (704, 1024),
}
BIG_NAMES = tuple(BIG)


def _params(sem=("arbitrary",), vmem=V7X_VMEM_LIMIT):
    return pltpu.CompilerParams(dimension_semantics=sem, vmem_limit_bytes=vmem)


def _gelu(x):
    x2 = x * x
    t = jnp.tanh(GELU_C0 * x * (1.0 + GELU_C1 * x2))
    return 0.5 * x * (1.0 + t), t


def _gelu_grad(x, t):
    return 0.5 * (1.0 + t) + 0.5 * x * (1.0 - t * t) * GELU_C0 * (1.0 + 3.0 * GELU_C1 * x * x)


def _colsum8(v):
    r, n = v.shape
    return v.reshape(r // 8, 8, n).sum(axis=0)


def _dot(a, b):
    return jnp.dot(a, b, preferred_element_type=F32)


def _dot_nt(a, b):
    return lax.dot_general(a, b, (((1,), (1,)), ((), ())), preferred_element_type=F32)


def _dot_tn(a, b):
    return lax.dot_general(a, b, (((0,), (0,)), ((), ())), preferred_element_type=F32)


def _shift_down(v, carry, n):
    rows = lax.broadcasted_iota(jnp.int32, (8, v.shape[1]), 0)
    out = pltpu.roll(v, n, 0)
    head = out[0:8, :]
    for r in range(n):
        head = jnp.where(rows == r, carry[8 - n + r:8 - n + r + 1, :], head)
    return jnp.concatenate([head, out[8:, :]], axis=0)


def _shift_up(v, carry, n):
    tm = v.shape[0]
    rows = lax.broadcasted_iota(jnp.int32, (8, v.shape[1]), 0)
    out = pltpu.roll(v, tm - n, 0)
    tail = out[tm - 8:tm, :]
    for r in range(n):
        tail = jnp.where(rows == 8 - n + r, carry[r:r + 1, :], tail)
    return jnp.concatenate([out[0:tm - 8, :], tail], axis=0)


def _sigmoid(x):
    return 0.5 * jnp.tanh(0.5 * x) + 0.5


def _start_all(copies):
    for cp in copies:
        cp.start()


def _wait_all(copies):
    for cp in copies:
        cp.wait()


def _load_col_sharded(src, dst, sems, first):
    cs = src.shape[-1]
    return [pltpu.make_async_copy(src.at[k], dst.at[:, k * cs:(k + 1) * cs], sems.at[first + k])
            for k in range(N_CHIPS)]


def _load_row_sharded(src, dst, sems, first):
    rs = src.shape[-2]
    return [pltpu.make_async_copy(src.at[k], dst.at[k * rs:(k + 1) * rs, :], sems.at[first + k])
            for k in range(N_CHIPS)]


def _load_branch(src, dst, sems, first):
    return [pltpu.make_async_copy(src.at[k, pl.ds(m * D_A, D_A), :], dst.at[m, :, k * 256:(k + 1) * 256],
                                  sems.at[first + 2 * k + m])
            for k in range(N_CHIPS) for m in range(2)]


def _row_spec(tm, n, rev=None):
    if rev is None:
        return pl.BlockSpec((tm, n), lambda i: (i, 0))
    return pl.BlockSpec((tm, n), lambda i: (rev - 1 - i, 0))


def _const_spec(shape):
    nd = len(shape)
    return pl.BlockSpec(shape, lambda i: (0,) * nd)


def _mesh_pos():
    return lax.axis_index("x"), lax.axis_index("y"), lax.axis_index("c")


def _other_chips(x, y):
    return [(1 - x, y, 2 * (1 - x) + y), (x, 1 - y, 2 * x + (1 - y)), (1 - x, 1 - y, 2 * (1 - x) + (1 - y))]


def _remote(src, dst, ssem, rsem, to):
    return pltpu.make_async_remote_copy(src_ref=src, dst_ref=dst, send_sem=ssem, recv_sem=rsem, device_id=to,
                                        device_id_type=MESH)


def _half(ref, which, h):
    start = pl.multiple_of(which * h, 8)
    if len(ref.shape) == 2:
        return ref.at[pl.ds(start, h), :]
    return ref.at[:, pl.ds(start, h), :]


class _Stage:
    def __init__(self, ins=(), inouts=(), outs=(), n_sems=0, start=None, mid=None, finish=None, then=None, slow=False):
        self.ins, self.inouts, self.outs = list(ins), list(inouts), list(outs)
        self.n_sems, self.start, self.mid, self.finish, self.then = n_sems, start, mid, finish, then
        self.slow = slow


def _gather_stage(bufs, then):
    n = len(bufs)

    def copies(io, sem):
        x, y, c = _mesh_pos()
        me = 2 * x + y
        ici, fwd, got = [], [], []
        for w in range(n):
            h = io[w].shape[1] // 2
            for j, (px, py, pk) in enumerate(_other_chips(x, y)):
                mine = _half(io[w].at[me], c, h)
                theirs = _half(io[w].at[pk], c, h)
                ici.append(_remote(mine, mine, sem(12 * w + j), sem(12 * w + 3 + j), (px, py, c)))
                got.append(_remote(theirs, theirs, sem(12 * w + j), sem(12 * w + 3 + j), (px, py, c)))
                fwd.append(_remote(theirs, theirs, sem(12 * w + 6 + j), sem(12 * w + 9 + j), (x, y, 1 - c)))
        return ici, got, fwd

    def start(ins, io, outs, sem):
        _start_all(copies(io, sem)[0])

    def mid(ins, io, outs, sem):
        _, got, fwd = copies(io, sem)
        for g, f in zip(got, fwd):
            g.wait_recv()
            f.start()

    def finish(ins, io, outs, sem):
        x, y, c = _mesh_pos()
        ici, _, fwd = copies(io, sem)
        for w in range(n):
            h = io[w].shape[1] // 2
            for j, (px, py, pk) in enumerate(_other_chips(x, y)):
                other = _half(io[w].at[pk], 1 - c, h)
                _remote(other, other, sem(12 * w + 6 + j), sem(12 * w + 9 + j), (x, y, 1 - c)).wait_recv()
        for cp in ici + fwd:
            cp.wait_send()

    return _Stage(inouts=bufs, n_sems=12 * n, start=start, mid=mid, finish=finish, then=then)


def _pair_send_stage(grad, then):
    h = grad.shape[1] // 2

    def copy(ins, outs, sem):
        x, y, c = _mesh_pos()
        return _remote(_half(ins[0], 1 - c, h), outs[0], sem(0), sem(1), (x, y, 1 - c))

    return _Stage(ins=[grad], outs=[jax.ShapeDtypeStruct((N_CHIPS, h, grad.shape[2]), F32)], n_sems=2,
                  start=lambda ins, io, outs, sem: copy(ins, outs, sem).start(),
                  finish=lambda ins, io, outs, sem: copy(ins, outs, sem).wait(), then=then)


def _chip_send_stage(psum, then):
    def copies(ins, outs, sem):
        x, y, c = _mesh_pos()
        return [_remote(ins[0].at[pk], outs[0].at[j], sem(j), sem(3 + j), (px, py, c))
                for j, (px, py, pk) in enumerate(_other_chips(x, y))]

    return _Stage(ins=[psum], outs=[jax.ShapeDtypeStruct((3,) + psum.shape[1:], BF16)], n_sems=6,
                  start=lambda ins, io, outs, sem: _start_all(copies(ins, outs, sem)),
                  finish=lambda ins, io, outs, sem: _wait_all(copies(ins, outs, sem)), then=then,
                  slow=psum.shape[1] * psum.shape[2] * 2 > SLOW_COPY_BYTES)


def _pair_fill_stage(final, then):
    h = final.shape[0] // 2

    def copy(io, sem):
        x, y, c = _mesh_pos()
        mine = _half(io[0], c, h)
        return _remote(mine, mine, sem(0), sem(1), (x, y, 1 - c))

    return _Stage(inouts=[final], n_sems=2,
                  start=lambda ins, io, outs, sem: copy(io, sem).start(),
                  finish=lambda ins, io, outs, sem: copy(io, sem).wait(), then=then)


def _pair_swap_stage(packed, then):
    def copy(ins, outs, sem):
        x, y, c = _mesh_pos()
        return _remote(ins[0], outs[0], sem(0), sem(1), (x, y, 1 - c))

    return _Stage(ins=[packed], outs=[jax.ShapeDtypeStruct(packed.shape, F32)], n_sems=2,
                  start=lambda ins, io, outs, sem: copy(ins, outs, sem).start(),
                  finish=lambda ins, io, outs, sem: copy(ins, outs, sem).wait(), then=then)


def _chip_spread_stage(psum, then):
    def copies(ins, outs, sem):
        x, y, c = _mesh_pos()
        me = 2 * x + y
        cps = [_remote(ins[0], outs[0].at[me], sem(j), sem(3 + j), (px, py, c))
               for j, (px, py, pk) in enumerate(_other_chips(x, y))]
        return cps, pltpu.make_async_copy(ins[0], outs[0].at[me], sem(6))

    def start(ins, io, outs, sem):
        cps, own = copies(ins, outs, sem)
        own.start()
        _start_all(cps)

    def finish(ins, io, outs, sem):
        cps, own = copies(ins, outs, sem)
        _wait_all(cps)
        own.wait()

    return _Stage(ins=[psum], outs=[jax.ShapeDtypeStruct((N_CHIPS,) + psum.shape, F32)], n_sems=7,
                  start=start, finish=finish, then=then)


def _staged_call(core, *, name, grid, in_specs, out_specs, out_shape, scratch_shapes, args, stages):
    n_in, n_out, n_scr = len(args), len(out_shape), len(scratch_shapes)
    s_args, s_outs, aliases, layout = [], [], {}, []
    n_sems = 0
    for st in stages:
        i0, o0 = len(s_args), len(s_outs)
        s_args += st.ins + st.inouts
        for q in range(len(st.inouts)):
            aliases[n_in + i0 + len(st.ins) + q] = n_out + o0 + q
        s_outs += [jax.ShapeDtypeStruct(a.shape, a.dtype) for a in st.inouts] + st.outs
        layout.append((i0, o0, n_sems))
        n_sems += st.n_sems
    steps = 1
    for g in grid:
        steps *= g

    def body(*refs):
        own_in = refs[:n_in]
        s_in = refs[n_in:n_in + len(s_args)]
        rest = refs[n_in + len(s_args):]
        own_out = rest[:n_out]
        s_out = rest[n_out:n_out + len(s_outs)]
        scr = rest[n_out + len(s_outs):]

        def run(which):
            for st, (i0, o0, s0) in zip(stages, layout):
                fn = getattr(st, which)
                if fn is not None:
                    fn(s_in[i0:i0 + len(st.ins)], s_out[o0:o0 + len(st.inouts)],
                       s_out[o0 + len(st.inouts):o0 + len(st.inouts) + len(st.outs)],
                       lambda k, s0=s0: scr[n_scr].at[s0 + k])

        if not stages:
            core(*own_in, *own_out, *scr[:n_scr])
            return
        step = 0
        for d, g in enumerate(grid):
            step = step * g + pl.program_id(d)
        if steps == 1:
            run("start")
            core(*own_in, *own_out, *scr[:n_scr])
            run("mid")
            run("finish")
            return
        pl.when(step == 0)(lambda: run("start"))
        core(*own_in, *own_out, *scr[:n_scr])
        pl.when(step == (3 * steps) // 4)(lambda: run("mid"))
        pl.when(step == steps - 1)(lambda: run("finish"))

    sem = ("arbitrary",) * len(grid) if stages else ("parallel",) * max(len(grid) - 1, 0) + ("arbitrary",) * min(len(grid), 1)
    res = pl.pallas_call(
        body, name=name, grid=grid,
        in_specs=list(in_specs) + [ANY] * len(s_args),
        out_specs=list(out_specs) + [ANY] * len(s_outs),
        out_shape=list(out_shape) + s_outs,
        input_output_aliases=aliases,
        scratch_shapes=list(scratch_shapes) + ([pltpu.SemaphoreType.DMA((n_sems,))] if stages else []),
        compiler_params=_params(sem) if grid else pltpu.CompilerParams(vmem_limit_bytes=V7X_VMEM_LIMIT),
    )(*args, *s_args)
    return list(res[:n_out]), list(res[n_out:])


class _Pipe:
    def __init__(self):
        self.ready = []
        self.flushes = 0

    def add(self, stage):
        self.ready.append(stage)

    def carry(self, call, long=True):
        stages = [st for st in self.ready if long or not st.slow]
        self.ready = [st for st in self.ready if not (long or not st.slow)]
        own, outs = call(stages)
        k = 0
        for st in stages:
            n = len(st.inouts) + len(st.outs)
            st.then(*outs[k:k + n])
            k += n
        return own

    def flush(self):
        while self.ready:
            self.flushes += 1
            self.carry(lambda stages: _staged_call(
                lambda *refs: None, name=f"comm_tail_{self.flushes}", grid=(), in_specs=[], out_specs=[], out_shape=[],
                scratch_shapes=[], args=[], stages=stages))


def _mixer_fwd(layer, x, g1, bgate, lng, lnb, wm, bsf, wsc, win_g, wb_g, wout_g, stages):
    t_len = x.shape[0]
    tm = min(TM_MIX, t_len)
    nt = t_len // tm
    nb = tm // GMLP_BLOCK

    def core(x_ref, x_late_ref, g1_ref, bgate_ref, lng_ref, lnb_ref, wm_ref, bsf_ref, wsc_ref, win_hbm, wb_hbm, wout_hbm,
             zc_ref, ya_ref, yb_ref, q_ref, sa_ref, ca_ref, sb_ref, cb_ref, ug_ref, fu_ref, xh_ref, cv_ref,
             mg_ref, h_ref, x2_ref,
             win_v, wb_v, wout_v, carry, vn_s, f_s, z_s, sems):
        i = pl.program_id(0)

        @pl.when(i == 0)
        def _():
            cps = (_load_col_sharded(win_hbm, win_v, sems, 0) + _load_branch(wb_hbm, wb_v, sems, 4)
                   + _load_row_sharded(wout_hbm, wout_v, sems, 12))
            _start_all(cps)
            carry[...] = jnp.zeros_like(carry)
            z_s[...] = jnp.zeros_like(z_s)
            _wait_all(cps)

        xv = x_ref[...]
        r = lax.rsqrt(jnp.mean(xv * xv, axis=-1, keepdims=True) + RMS_EPS)
        h_ref[...] = (xv * r * g1_ref[...]).astype(BF16)

        def zcols(c0, n, keep=None):
            zv = z_s[:, c0:c0 + n]
            z_s[:, c0:c0 + n] = _dot(h_ref[...], win_v[:, c0:c0 + n])
            if keep is not None:
                zc_ref[:, keep * D_B:(keep + 1) * D_B] = zv.astype(BF16)
            return zv

        v = zcols(C_V, D_A)
        vg, tv = _gelu(v)
        mu = jnp.mean(vg, axis=-1, keepdims=True)
        vc = vg - mu
        rstd = lax.rsqrt(jnp.mean(vc * vc, axis=-1, keepdims=True) + LN_EPS)
        xh = vc * rstd
        xh_ref[...] = xh.astype(BF16)
        cv_ref[...] = (rstd * _gelu_grad(v, tv)).astype(BF16)
        vn_s[...] = (xh * lng_ref[...] + lnb_ref[...]).astype(BF16)
        for hd in range(A_HEADS):
            cols = slice(hd * 128, (hd + 1) * 128)
            vcat = jnp.concatenate([vn_s[b * 128:(b + 1) * 128, cols] for b in range(nb)], axis=1)
            fcat = _dot(wm_ref[hd], vcat)
            for b in range(nb):
                f_s[b * 128:(b + 1) * 128, cols] = fcat[:, b * 128:(b + 1) * 128]
        u = zcols(C_U, D_A)
        ug, tu = _gelu(u)
        ug_ref[...] = ug.astype(BF16)
        fb = f_s[...] + jnp.concatenate([bsf_ref[...]] * nb, axis=0)
        fu_ref[...] = (fb * _gelu_grad(u, tu)).astype(BF16)
        ya_ref[...] = (ug * fb).astype(BF16)

        p = zcols(C_CG, D_B, keep=1) * zcols(C_HB, D_B, keep=2)
        cr = carry[...]
        q = wsc_ref[0:1, :] * _shift_down(p, cr, 2) + wsc_ref[1:2, :] * _shift_down(p, cr, 1) + wsc_ref[2:3, :] * p
        carry[...] = p[tm - 8:tm, :]
        q_ref[...] = q.astype(BF16)
        yb_ref[...] = (zcols(C_BG, D_B, keep=0) * q).astype(BF16)

        av = _dot(ya_ref[...], wb_v[0])
        sa = _sigmoid(zcols(C_GA, D_MODEL) + bgate_ref[:, 0:D_MODEL])
        sa_ref[...] = sa.astype(BF16)
        mg = sa * av
        ca_ref[...] = (mg * (1.0 - sa)).astype(BF16)
        bv = _dot(yb_ref[...], wb_v[1])
        sb = _sigmoid(zcols(C_GB, D_MODEL) + bgate_ref[:, D_MODEL:2 * D_MODEL])
        sb_ref[...] = sb.astype(BF16)
        mb = sb * bv
        cb_ref[...] = (mb * (1.0 - sb)).astype(BF16)
        mg_ref[...] = (mg + mb).astype(BF16)
        x2_ref[...] = x_late_ref[...] + _dot(mg_ref[...], wout_v[...])

    def tile(n, lag):
        return pl.BlockSpec((tm, n), lambda i: (jnp.clip(i - lag, 0, nt - 1), 0))

    outs = [
        jax.ShapeDtypeStruct((t_len, 3 * D_B), BF16),
        jax.ShapeDtypeStruct((t_len, D_A), BF16),
        jax.ShapeDtypeStruct((t_len, D_B), BF16),
        jax.ShapeDtypeStruct((t_len, D_B), BF16),
        jax.ShapeDtypeStruct((t_len, D_MODEL), BF16),
        jax.ShapeDtypeStruct((t_len, D_MODEL), BF16),
        jax.ShapeDtypeStruct((t_len, D_MODEL), BF16),
        jax.ShapeDtypeStruct((t_len, D_MODEL), BF16),
        jax.ShapeDtypeStruct((t_len, D_A), BF16),
        jax.ShapeDtypeStruct((t_len, D_A), BF16),
        jax.ShapeDtypeStruct((t_len, D_A), BF16),
        jax.ShapeDtypeStruct((t_len, D_A), BF16),
        jax.ShapeDtypeStruct((t_len, D_MODEL), BF16),
        jax.ShapeDtypeStruct((t_len, D_MODEL), BF16),
        jax.ShapeDtypeStruct((t_len, D_MODEL), F32),
    ]
    return _staged_call(
        core, name=f"mixer_fwd_l{layer}", grid=(nt + 1,),
        in_specs=[tile(D_MODEL, 0), tile(D_MODEL, 1), _const_spec((1, D_MODEL)), _const_spec((1, 2 * D_MODEL)),
                  _const_spec((1, D_A)), _const_spec((1, D_A)), _const_spec((A_HEADS, 128, 128)),
                  _const_spec((128, D_A)), _const_spec((8, D_B)), ANY, ANY, ANY],
        out_specs=[tile(o.shape[1], 0 if k == len(outs) - 2 else 1) for k, o in enumerate(outs)],
        out_shape=outs,
        scratch_shapes=[pltpu.VMEM((D_MODEL, D_IN), BF16), pltpu.VMEM((2, D_A, D_MODEL), BF16),
                        pltpu.VMEM((D_MODEL, D_MODEL), BF16), pltpu.VMEM((8, D_B), F32),
                        pltpu.VMEM((tm, D_A), BF16), pltpu.VMEM((tm, D_A), F32), pltpu.VMEM((tm, D_IN), F32),
                        pltpu.SemaphoreType.DMA((16,))],
        args=[x, x, g1, bgate, lng, lnb, wm, bsf, wsc, win_g, wb_g, wout_g], stages=stages)


def _ffn_fwd(layer, x2, g2, wfc, bfc, wup_g, wdown_g, stages, head=None):
    t_len = x2.shape[0]
    tm = min(TM_FFN, t_len)
    nt = t_len // tm

    def core(*refs):
        if head is None:
            (x_ref, g2_ref, wfc_ref, bfc_ref, wup_hbm, wdown_hbm, up_ref, silu_ref, dsilu_ref, act_ref, h_ref, x3_ref,
             wup_v, wdown_v, carry, sems) = refs
        else:
            (x_ref, g2_ref, wfc_ref, bfc_ref, t_ref, gf_ref, wup_hbm, wdown_hbm, up_ref, silu_ref, dsilu_ref, act_ref,
             h_ref, dx_ref, dgf_ref, loss_ref, wup_v, wdown_v, carry, sems) = refs
        i = pl.program_id(0)

        @pl.when(i == 0)
        def _():
            cps = _load_col_sharded(wup_hbm, wup_v, sems, 0) + _load_row_sharded(wdown_hbm, wdown_v, sems, 4)
            _start_all(cps)
            carry[...] = jnp.zeros_like(carry)
            if head is not None:
                dgf_ref[...] = jnp.zeros_like(dgf_ref)
                loss_ref[...] = jnp.zeros_like(loss_ref)
            _wait_all(cps)

        xv = x_ref[...]
        r = lax.rsqrt(jnp.mean(xv * xv, axis=-1, keepdims=True) + RMS_EPS)
        h_ref[...] = (xv * r * g2_ref[...]).astype(BF16)
        gate = _dot(h_ref[...], wup_v[:, 0:D_FF])
        up_ref[:, 0:D_FF] = gate.astype(BF16)
        cr = carry[...]
        gc = (wfc_ref[0:1, :] * _shift_down(gate, cr, 2) + wfc_ref[1:2, :] * _shift_down(gate, cr, 1)
              + wfc_ref[2:3, :] * gate + bfc_ref[...])
        carry[...] = gate[tm - 8:tm, :]
        sg = _sigmoid(gc)
        silu = gc * sg
        silu_ref[...] = silu.astype(BF16)
        dsilu_ref[...] = (sg + silu * (1.0 - sg)).astype(BF16)
        val = _dot(h_ref[...], wup_v[:, D_FF:2 * D_FF])
        up_ref[:, D_FF:2 * D_FF] = val.astype(BF16)
        act_ref[...] = (silu * val).astype(BF16)
        x3 = x_ref[...] + _dot(act_ref[...], wdown_v[...])
        if head is None:
            x3_ref[...] = x3
        else:
            r3 = lax.rsqrt(jnp.mean(x3 * x3, axis=-1, keepdims=True) + RMS_EPS)
            xh = x3 * r3
            err = xh * gf_ref[...] - t_ref[...]
            loss_ref[...] += _colsum8(err * err)
            dy = err * (1.0 / D_MODEL)
            dgf_ref[...] += _colsum8(dy * xh)
            dxh = dy * gf_ref[...]
            dx_ref[...] = r3 * (dxh - xh * jnp.mean(dxh * xh, axis=-1, keepdims=True))

    outs = [
        jax.ShapeDtypeStruct((t_len, 2 * D_FF), BF16),
        jax.ShapeDtypeStruct((t_len, D_FF), BF16),
        jax.ShapeDtypeStruct((t_len, D_FF), BF16),
        jax.ShapeDtypeStruct((t_len, D_FF), BF16),
        jax.ShapeDtypeStruct((t_len, D_MODEL), BF16),
        jax.ShapeDtypeStruct((t_len, D_MODEL), F32),
    ]
    in_specs = [_row_spec(tm, D_MODEL), _const_spec((1, D_MODEL)), _const_spec((8, D_FF)), _const_spec((1, D_FF))]
    out_specs = [_row_spec(tm, o.shape[1]) for o in outs]
    args = [x2, g2, wfc, bfc]
    if head is not None:
        in_specs += [_row_spec(tm, D_MODEL), _const_spec((1, D_MODEL))]
        args += list(head)
        outs += [jax.ShapeDtypeStruct((8, D_MODEL), F32)] * 2
        out_specs += [_const_spec((8, D_MODEL))] * 2
    return _staged_call(
        core, name=f"ffn_fwd_l{layer}", grid=(nt,),
        in_specs=in_specs + [ANY, ANY], out_specs=out_specs, out_shape=outs,
        scratch_shapes=[pltpu.VMEM((D_MODEL, 2 * D_FF), BF16), pltpu.VMEM((D_FF, D_MODEL), BF16),
                        pltpu.VMEM((8, D_FF), F32), pltpu.SemaphoreType.DMA((8,))],
        args=args + [wup_g, wdown_g], stages=stages)


def _ffn_bwd(layer, dx3, x2, up, silu, dsilu, g2, wfc, wup_g, wdown_g, stages):
    t_len = x2.shape[0]
    tm = min(TM_FFN, t_len)
    nt = t_len // tm

    def core(dx3_ref, dx3_late_ref, x_ref, up_ref, silu_ref, dsilu_ref, g2_ref, wfc_ref, wup_hbm, wdown_hbm,
             dx2_ref, dup_ref, dx3b_ref, dg2_ref, dbfc_ref, dwfc_ref,
             wup_v, wdown_v, carry, da_s, dup_s, sems):
        i = pl.program_id(0)

        @pl.when(i == 0)
        def _():
            cps = _load_col_sharded(wup_hbm, wup_v, sems, 0) + _load_row_sharded(wdown_hbm, wdown_v, sems, 4)
            _start_all(cps)
            for ref in (carry, da_s, dup_s, dg2_ref, dbfc_ref, dwfc_ref):
                ref[...] = jnp.zeros_like(ref)
            _wait_all(cps)

        live = (i <= nt).astype(F32)
        dx3b_ref[...] = dx3_ref[...].astype(BF16)
        dh = jnp.zeros((tm, D_MODEL), F32)
        for c0, c1 in FF_CHUNKS:
            v0, v1 = D_FF + c0, D_FF + c1
            dh = dh + _dot_nt(dup_s[:, c0:c1], wup_v[:, c0:c1]) + _dot_nt(dup_s[:, v0:v1], wup_v[:, v0:v1])
            da = da_s[:, c0:c1]
            dval = (da * silu_ref[:, c0:c1].astype(F32)).astype(BF16)
            dup_ref[:, v0:v1] = dval
            dup_s[:, v0:v1] = dval
            dgc = da * up_ref[:, v0:v1].astype(F32) * dsilu_ref[:, c0:c1].astype(F32)
            cr = carry[:, c0:c1]
            dgc1 = _shift_up(dgc, cr, 1)
            dgc2 = _shift_up(dgc, cr, 2)
            carry[:, c0:c1] = jnp.where(i < nt, dgc[0:8, :], cr)
            gate = up_ref[:, c0:c1].astype(F32)
            dbfc_ref[:, c0:c1] += live * _colsum8(dgc)
            dwfc_ref[0, :, c0:c1] += live * _colsum8(dgc2 * gate)
            dwfc_ref[1, :, c0:c1] += live * _colsum8(dgc1 * gate)
            dwfc_ref[2, :, c0:c1] += live * _colsum8(dgc * gate)
            dgate = (wfc_ref[2:3, c0:c1] * dgc + wfc_ref[1:2, c0:c1] * dgc1 + wfc_ref[0:1, c0:c1] * dgc2).astype(BF16)
            dup_ref[:, c0:c1] = dgate
            dup_s[:, c0:c1] = dgate
            da_s[:, c0:c1] = _dot_nt(dx3b_ref[...], wdown_v[c0:c1, :])
        xv = x_ref[...]
        r = lax.rsqrt(jnp.mean(xv * xv, axis=-1, keepdims=True) + RMS_EPS)
        xh = xv * r
        dg2_ref[...] += _colsum8(dh * xh)
        dxh = dh * g2_ref[...]
        dx2_ref[...] = dx3_late_ref[...] + r * (dxh - xh * jnp.mean(dxh * xh, axis=-1, keepdims=True))

    def tile(n, lag):
        return pl.BlockSpec((tm, n), lambda i: (nt - 1 - jnp.clip(i - lag, 0, nt - 1), 0))

    outs = [
        jax.ShapeDtypeStruct((t_len, D_MODEL), F32),
        jax.ShapeDtypeStruct((t_len, 2 * D_FF), BF16),
        jax.ShapeDtypeStruct((t_len, D_MODEL), BF16),
        jax.ShapeDtypeStruct((8, D_MODEL), F32),
        jax.ShapeDtypeStruct((8, D_FF), F32),
        jax.ShapeDtypeStruct((3, 8, D_FF), F32),
    ]
    return _staged_call(
        core, name=f"ffn_bwd_l{layer}", grid=(nt + 2,),
        in_specs=[tile(D_MODEL, 0), tile(D_MODEL, 2), tile(D_MODEL, 2), tile(2 * D_FF, 1), tile(D_FF, 1), tile(D_FF, 1),
                  _const_spec((1, D_MODEL)), _const_spec((8, D_FF)), ANY, ANY],
        out_specs=[tile(D_MODEL, 2), tile(2 * D_FF, 1), tile(D_MODEL, 0),
                   _const_spec((8, D_MODEL)), _const_spec((8, D_FF)), _const_spec((3, 8, D_FF))],
        out_shape=outs,
        scratch_shapes=[pltpu.VMEM((D_MODEL, 2 * D_FF), BF16), pltpu.VMEM((D_FF, D_MODEL), BF16),
                        pltpu.VMEM((8, D_FF), F32), pltpu.VMEM((tm, D_FF), F32), pltpu.VMEM((tm, 2 * D_FF), BF16),
                        pltpu.SemaphoreType.DMA((8,))],
        args=[dx3, dx3, x2, up, silu, dsilu, g2, wfc, wup_g, wdown_g], stages=stages)


def _mixer_bwd(layer, dx2, x, zc, qs, sa, ca, sb, cb, ug, fu, xhs, cv, g1, lng, lnb, wmt, wsc, win_g, wb_g, wout_g,
               stages):
    t_len = x.shape[0]
    tm = min(TM_MIX, t_len)
    nt = t_len // tm
    nb = tm // GMLP_BLOCK

    def core(dx2_ref, x_ref, zc_ref, q_ref, sa_ref, ca_ref, sb_ref, cb_ref, ug_ref, fu_ref, xh_ref, cv_ref,
             g1_ref, lng_ref, lnb_ref, wmt_ref, wsc_ref, win_hbm, wb_hbm, wout_hbm,
             dx_ref, dz_ref, da_ref, db_ref, dx2b_ref, dg1_ref, dbgate_ref, dlng_ref, dlnb_ref, dwm_ref, dbsf_ref, dwsc_ref,
             win_v, wb_v, wout_v, carry, vn_s, df_s, dvn_s, sems):
        i = pl.program_id(0)

        @pl.when(i == 0)
        def _():
            cps = (_load_col_sharded(win_hbm, win_v, sems, 0) + _load_branch(wb_hbm, wb_v, sems, 4)
                   + _load_row_sharded(wout_hbm, wout_v, sems, 12))
            _start_all(cps)
            for ref in (carry, dg1_ref, dbgate_ref, dlng_ref, dlnb_ref, dwm_ref, dbsf_ref, dwsc_ref):
                ref[...] = jnp.zeros_like(ref)
            _wait_all(cps)

        def kept(k):
            return zc_ref[:, k * D_B:(k + 1) * D_B].astype(F32)

        def dz_cols(c0, n, val):
            dz_ref[:, c0:c0 + n] = val.astype(BF16)
            return _dot_nt(dz_ref[:, c0:c0 + n], win_v[:, c0:c0 + n])

        dx2b_ref[...] = dx2_ref[...].astype(BF16)
        dm = _dot_nt(dx2b_ref[...], wout_v[...])
        da_ref[...] = (dm * sa_ref[...].astype(F32)).astype(BF16)
        dga = dm * ca_ref[...].astype(F32)
        dh = dz_cols(C_GA, D_MODEL, dga)
        dbgate_ref[:, 0:D_MODEL] += _colsum8(dga)
        dya = _dot_nt(da_ref[...], wb_v[0])
        db_ref[...] = (dm * sb_ref[...].astype(F32)).astype(BF16)
        dgb = dm * cb_ref[...].astype(F32)
        dh = dh + dz_cols(C_GB, D_MODEL, dgb)
        dbgate_ref[:, D_MODEL:2 * D_MODEL] += _colsum8(dgb)
        dyb = _dot_nt(db_ref[...], wb_v[1])

        xh = xh_ref[...].astype(F32)
        vn_s[...] = (xh * lng_ref[...] + lnb_ref[...]).astype(BF16)
        df = dya * ug_ref[...].astype(F32)
        df_s[...] = df.astype(BF16)
        dbsf_acc = df[0:128, :]
        for b in range(1, nb):
            dbsf_acc = dbsf_acc + df[b * 128:(b + 1) * 128, :]
        dbsf_ref[...] += dbsf_acc
        for hd in range(A_HEADS):
            cols = slice(hd * 128, (hd + 1) * 128)
            vcat = jnp.concatenate([vn_s[b * 128:(b + 1) * 128, cols] for b in range(nb)], axis=1)
            dcat = jnp.concatenate([df_s[b * 128:(b + 1) * 128, cols] for b in range(nb)], axis=1)
            gcat = _dot(wmt_ref[hd], dcat)
            dwm_ref[hd] += _dot_nt(dcat, vcat)
            for b in range(nb):
                dvn_s[b * 128:(b + 1) * 128, cols] = gcat[:, b * 128:(b + 1) * 128]
        dh = dh + dz_cols(C_U, D_A, dya * fu_ref[...].astype(F32))
        dvn = dvn_s[...]
        dlng_ref[...] += _colsum8(dvn * xh)
        dlnb_ref[...] += _colsum8(dvn)
        dxh = dvn * lng_ref[...]
        dvc = dxh - jnp.mean(dxh, axis=-1, keepdims=True) - xh * jnp.mean(dxh * xh, axis=-1, keepdims=True)
        dh = dh + dz_cols(C_V, D_A, dvc * cv_ref[...].astype(F32))

        cg = kept(1)
        hbv = kept(2)
        p = cg * hbv
        dh = dh + dz_cols(C_BG, D_B, dyb * q_ref[...].astype(F32))
        dq = dyb * kept(0)
        cr = carry[...]
        dq1 = _shift_up(dq, cr, 1)
        dq2 = _shift_up(dq, cr, 2)
        carry[...] = dq[0:8, :]
        dwsc_ref[0] += _colsum8(dq2 * p)
        dwsc_ref[1] += _colsum8(dq1 * p)
        dwsc_ref[2] += _colsum8(dq * p)
        dp = wsc_ref[2:3, :] * dq + wsc_ref[1:2, :] * dq1 + wsc_ref[0:1, :] * dq2
        dh = dh + dz_cols(C_CG, D_B, dp * hbv)
        dh = dh + dz_cols(C_HB, D_B, dp * cg)

        xv = x_ref[...]
        r = lax.rsqrt(jnp.mean(xv * xv, axis=-1, keepdims=True) + RMS_EPS)
        xn = xv * r
        dg1_ref[...] += _colsum8(dh * xn)
        dxn = dh * g1_ref[...]
        dx_ref[...] = dx2_ref[...] + r * (dxn - xn * jnp.mean(dxn * xn, axis=-1, keepdims=True))

    outs = [
        jax.ShapeDtypeStruct((t_len, D_MODEL), F32),
        jax.ShapeDtypeStruct((t_len, D_IN), BF16),
        jax.ShapeDtypeStruct((t_len, D_MODEL), BF16),
        jax.ShapeDtypeStruct((t_len, D_MODEL), BF16),
        jax.ShapeDtypeStruct((t_len, D_MODEL), BF16),
        jax.ShapeDtypeStruct((8, D_MODEL), F32),
        jax.ShapeDtypeStruct((8, 2 * D_MODEL), F32),
        jax.ShapeDtypeStruct((8, D_A), F32),
        jax.ShapeDtypeStruct((8, D_A), F32),
        jax.ShapeDtypeStruct((A_HEADS, 128, 128), F32),
        jax.ShapeDtypeStruct((128, D_A), F32),
        jax.ShapeDtypeStruct((3, 8, D_B), F32),
    ]

    return _staged_call(
        core, name=f"mixer_bwd_l{layer}", grid=(nt,),
        in_specs=[_row_spec(tm, D_MODEL, nt), _row_spec(tm, D_MODEL, nt), _row_spec(tm, 3 * D_B, nt),
                  _row_spec(tm, D_B, nt), _row_spec(tm, D_MODEL, nt), _row_spec(tm, D_MODEL, nt),
                  _row_spec(tm, D_MODEL, nt), _row_spec(tm, D_MODEL, nt), _row_spec(tm, D_A, nt), _row_spec(tm, D_A, nt),
                  _row_spec(tm, D_A, nt), _row_spec(tm, D_A, nt),
                  _const_spec((1, D_MODEL)), _const_spec((1, D_A)), _const_spec((1, D_A)),
                  _const_spec((A_HEADS, 128, 128)), _const_spec((8, D_B)), ANY, ANY, ANY],
        out_specs=[_row_spec(tm, D_MODEL, nt), _row_spec(tm, D_IN, nt), _row_spec(tm, D_MODEL, nt),
                   _row_spec(tm, D_MODEL, nt), _row_spec(tm, D_MODEL, nt),
                   _const_spec((8, D_MODEL)), _const_spec((8, 2 * D_MODEL)), _const_spec((8, D_A)), _const_spec((8, D_A)),
                   _const_spec((A_HEADS, 128, 128)), _const_spec((128, D_A)), _const_spec((3, 8, D_B))],
        out_shape=outs,
        scratch_shapes=[pltpu.VMEM((D_MODEL, D_IN), BF16), pltpu.VMEM((2, D_A, D_MODEL), BF16),
                        pltpu.VMEM((D_MODEL, D_MODEL), BF16), pltpu.VMEM((8, D_B), F32),
                        pltpu.VMEM((tm, D_A), BF16), pltpu.VMEM((tm, D_A), BF16), pltpu.VMEM((tm, D_A), F32),
                        pltpu.SemaphoreType.DMA((16,))],
        args=[dx2, x, zc, qs, sa, ca, sb, cb, ug, fu, xhs, cv, g1, lng, lnb, wmt, wsc, win_g, wb_g, wout_g],
        stages=stages)


def _wgrad(name, layer, a, b, rows, cols, row_blk, col_blk, stages, a_first=0):
    t_len = a.shape[0]
    n = b.shape[1]
    tk = min(TK_WGRAD, t_len)
    col_sharded = n == N_CHIPS * cols
    m = rows if col_sharded else a.shape[1]
    grid = (m // row_blk, n // col_blk, t_len // tk)
    per_shard_c = cols // col_blk

    if col_sharded:
        out_shape = (N_CHIPS, rows, cols)
        out_spec = pl.BlockSpec((None, row_blk, col_blk), lambda i, j, k: (j // per_shard_c, i, j % per_shard_c))
    else:
        out_shape = (N_CHIPS * rows, cols)
        out_spec = pl.BlockSpec((row_blk, col_blk), lambda i, j, k: (i, j))

    def core(a_ref, b_ref, o_ref):
        @pl.when(pl.program_id(2) == 0)
        def _():
            o_ref[...] = jnp.zeros_like(o_ref)

        o_ref[...] += _dot_tn(a_ref[...], b_ref[...])

    own, outs = _staged_call(
        core, name=f"wgrad_{name}_l{layer}", grid=grid,
        in_specs=[pl.BlockSpec((tk, row_blk), lambda i, j, k: (k, a_first + i)),
                  pl.BlockSpec((tk, col_blk), lambda i, j, k: (k, j))],
        out_specs=[out_spec], out_shape=[jax.ShapeDtypeStruct(out_shape, F32)], scratch_shapes=[],
        args=[a, b], stages=stages)
    return [own[0].reshape(N_CHIPS, rows, cols)], outs


def _wgrad_branch(layer, ya, da, yb, db, stages):
    t_len = ya.shape[0]
    tk = min(TK_WGRAD, t_len)

    def core(ya_ref, da_ref, yb_ref, db_ref, o_ref):
        @pl.when(pl.program_id(1) == 0)
        def _():
            o_ref[...] = jnp.zeros_like(o_ref)

        o_ref[0:D_A, :] += _dot_tn(ya_ref[...], da_ref[...])
        o_ref[D_A:2 * D_A, :] += _dot_tn(yb_ref[...], db_ref[...])

    a_spec = pl.BlockSpec((tk, D_A), lambda j, k: (k, 0))
    d_spec = pl.BlockSpec((tk, 256), lambda j, k: (k, j))
    return _staged_call(
        core, name=f"wgrad_w_branch_l{layer}", grid=(N_CHIPS, t_len // tk),
        in_specs=[a_spec, d_spec, a_spec, d_spec],
        out_specs=[pl.BlockSpec((None, 2 * D_A, 256), lambda j, k: (j, 0, 0))],
        out_shape=[jax.ShapeDtypeStruct((N_CHIPS, 2 * D_A, 256), F32)], scratch_shapes=[],
        args=[ya, da, yb, db], stages=stages)


def _all_reduce_small(name, packed):
    rows = packed.shape[0]

    def body(src_ref, out_ref, slots, send, recv):
        x, y, c = _mesh_pos()
        me = 4 * x + 2 * y + c
        cps = []
        for d in range(1, N_DEVICES):
            peer = me ^ d
            cps.append(_remote(src_ref, slots.at[me], send.at[d - 1], recv.at[d - 1],
                               (peer // 4, (peer // 2) % 2, peer % 2)))
        _start_all(cps)
        slots[me] = src_ref[...]
        _wait_all(cps)
        acc = slots[0]
        for d in range(1, N_DEVICES):
            acc = acc + slots[d]
        out_ref[...] = acc

    return pl.pallas_call(
        body, name=f"all_reduce_{name}",
        in_specs=[pl.BlockSpec(memory_space=pltpu.VMEM)], out_specs=pl.BlockSpec(memory_space=pltpu.VMEM),
        out_shape=jax.ShapeDtypeStruct(packed.shape, F32),
        scratch_shapes=[pltpu.VMEM((N_DEVICES, rows, 128), F32), pltpu.SemaphoreType.DMA((7,)),
                        pltpu.SemaphoreType.DMA((7,))],
        compiler_params=pltpu.CompilerParams(vmem_limit_bytes=V7X_VMEM_LIMIT),
    )(packed)


def _flat_blk(rows, cols):
    blk = rows
    while blk * cols * 4 > 2 * 1024 * 1024 and blk % 16 == 0:
        blk //= 2
    return blk


def _cast_into_slot(name, layer, w, chip):
    _, rows, cols = w.shape
    blk = _flat_blk(rows, cols)

    def body(chip_ref, w_ref, o_ref):
        o_ref[...] = w_ref[...].astype(BF16)

    return pl.pallas_call(
        body, name=f"cast_{name}_l{layer}",
        grid_spec=pltpu.PrefetchScalarGridSpec(
            num_scalar_prefetch=1, grid=(rows // blk,),
            in_specs=[pl.BlockSpec((None, blk, cols), lambda i, chip_ref: (layer, i, 0))],
            out_specs=pl.BlockSpec((None, blk, cols), lambda i, chip_ref: (chip_ref[0], i, 0))),
        out_shape=jax.ShapeDtypeStruct((N_CHIPS, rows, cols), BF16),
        compiler_params=_params(("parallel",)),
    )(chip, w)


def _pair_sum(name, grad, other, core):
    _, h, cols = other.shape
    blk = _flat_blk(h, cols)
    nblk = h // blk

    def body(core_ref, g_ref, o_ref, s_ref):
        s_ref[...] = (g_ref[...] + o_ref[...]).astype(BF16)

    spec = pl.BlockSpec((None, blk, cols), lambda k, i, core_ref: (k, i, 0))
    return pl.pallas_call(
        body, name=f"pair_sum_{name}",
        grid_spec=pltpu.PrefetchScalarGridSpec(
            num_scalar_prefetch=1, grid=(N_CHIPS, nblk),
            in_specs=[pl.BlockSpec((None, blk, cols), lambda k, i, core_ref: (k, core_ref[0] * nblk + i, 0)), spec],
            out_specs=spec),
        out_shape=jax.ShapeDtypeStruct((N_CHIPS, h, cols), BF16),
        compiler_params=_params(("parallel", "parallel")),
    )(core, grad, other)


def _chip_sum(name, grad, other, got, pos):
    _, rows, cols = grad.shape
    h = rows // 2
    blk = _flat_blk(h, cols)
    nblk = h // blk

    def body(pos_ref, g_ref, o_ref, r_ref, f_ref):
        f_ref[...] = (((g_ref[...] + o_ref[...]) + r_ref[0].astype(F32)) + r_ref[1].astype(F32)) + r_ref[2].astype(F32)

    return pl.pallas_call(
        body, name=f"chip_sum_{name}",
        grid_spec=pltpu.PrefetchScalarGridSpec(
            num_scalar_prefetch=1, grid=(nblk,),
            in_specs=[pl.BlockSpec((None, blk, cols), lambda i, pos_ref: (pos_ref[0], pos_ref[1] * nblk + i, 0)),
                      pl.BlockSpec((None, blk, cols), lambda i, pos_ref: (pos_ref[0], i, 0)),
                      pl.BlockSpec((3, blk, cols), lambda i, pos_ref: (0, i, 0))],
            out_specs=pl.BlockSpec((blk, cols), lambda i, pos_ref: (pos_ref[1] * nblk + i, 0))),
        out_shape=jax.ShapeDtypeStruct((rows, cols), F32),
        compiler_params=_params(("parallel",)),
    )(pos, grad, other, got)


def _sum_slots(name, slots):
    n, rows, _ = slots.shape

    def body(s_ref, o_ref):
        acc = s_ref[0]
        for d in range(1, n):
            acc = acc + s_ref[d]
        o_ref[...] = acc

    return pl.pallas_call(
        body, name=f"sum_slots_{name}", grid=(1,),
        in_specs=[pl.BlockSpec((n, rows, 128), lambda i: (0, 0, 0))],
        out_specs=pl.BlockSpec((rows, 128), lambda i: (0, 0)),
        out_shape=jax.ShapeDtypeStruct((rows, 128), F32),
        compiler_params=_params(),
    )(slots)


def _adamw_math(w, g, m, v):
    m2 = ADAM_B1 * m + (1.0 - ADAM_B1) * g
    v2 = ADAM_B2 * v + (1.0 - ADAM_B2) * (g * g)
    m_hat = m2 / (1.0 - ADAM_B1 ** ADAM_STEP)
    v_hat = v2 / (1.0 - ADAM_B2 ** ADAM_STEP)
    delta = -ADAM_LR * (m_hat / (jnp.sqrt(v_hat) + ADAM_EPS) + ADAM_WD * w)
    return delta, m2, v2


def _adamw_big(name, w, g0, g1, m, v):
    _, rows, cols = w.shape
    blk = _flat_blk(rows, cols) // 2

    def body(w_ref, g0_ref, g1_ref, m_ref, v_ref, g_ref, d_ref, m2_ref, v2_ref):
        g = jnp.where(pl.program_id(0) == 0, g0_ref[...], g1_ref[...])
        d, m2, v2 = _adamw_math(w_ref[...], g, m_ref[...], v_ref[...])
        g_ref[...] = g
        d_ref[...] = d
        m2_ref[...] = m2
        v2_ref[...] = v2

    spec = pl.BlockSpec((None, blk, cols), lambda la, i: (la, i, 0))
    return pl.pallas_call(
        body, name=f"adamw_{name}", grid=(N_LAYERS, rows // blk),
        in_specs=[spec, pl.BlockSpec((blk, cols), lambda la, i: (i * (1 - la), 0)),
                  pl.BlockSpec((blk, cols), lambda la, i: (i * la, 0)), spec, spec],
        out_specs=[spec] * 4,
        out_shape=[jax.ShapeDtypeStruct(w.shape, F32)] * 4,
        compiler_params=_params(("parallel", "parallel")),
    )(w, g0, g1, m, v)


def _adamw(name, w, g, m, v):
    rows, cols = w.shape
    blk = _flat_blk(rows, cols)

    def body(w_ref, g_ref, m_ref, v_ref, d_ref, m2_ref, v2_ref):
        d, m2, v2 = _adamw_math(w_ref[...], g_ref[...], m_ref[...], v_ref[...])
        d_ref[...] = d
        m2_ref[...] = m2
        v2_ref[...] = v2

    spec = pl.BlockSpec((blk, cols), lambda i: (i, 0))
    return pl.pallas_call(
        body, name=f"adamw_{name}", grid=(rows // blk,),
        in_specs=[spec] * 4, out_specs=[spec] * 3,
        out_shape=[jax.ShapeDtypeStruct((rows, cols), F32)] * 3,
        compiler_params=_params(("parallel",)),
    )(w, g, m, v)


SMALL = ("norm1_g", "b_gate", "gmlp_ln_g", "gmlp_ln_b", "w_spatial", "b_spatial", "w_shortconv", "norm2_g",
         "w_ffn_conv", "b_ffn_conv", "final_g")
ALL_WEIGHTS = ("norm1_g", "w_in", "b_gate", "gmlp_ln_g", "gmlp_ln_b", "w_spatial", "b_spatial", "w_shortconv",
               "w_branch", "w_out", "norm2_g", "w_ffn_up", "w_ffn_conv", "b_ffn_conv", "w_ffn_down", "final_g")


def _pack(arrays):
    flat = jnp.concatenate([a.reshape(-1) for a in arrays])
    n = flat.shape[0]
    rows = -(-n // 1024) * 8
    return jnp.pad(flat, (0, rows * 128 - n)).reshape(rows, 128)


def _unpack(packed, like):
    flat = packed.reshape(-1)
    out, off = [], 0
    for a in like:
        out.append(flat[off:off + a.size].reshape(a.shape))
        off += a.size
    return out


def _pad8(w):
    return jnp.pad(w, ((0, 5), (0, 0)))


def kernel(x, norm1_g, w_in, b_gate, gmlp_ln_g, gmlp_ln_b, w_spatial, b_spatial, w_shortconv, w_branch, w_out, norm2_g, w_ffn_up, w_ffn_conv, b_ffn_conv, w_ffn_down, final_g, loss_target, m_norm1_g, m_w_in, m_b_gate, m_gmlp_ln_g, m_gmlp_ln_b, m_w_spatial, m_b_spatial, m_w_shortconv, m_w_branch, m_w_out, m_norm2_g, m_w_ffn_up, m_w_ffn_conv, m_b_ffn_conv, m_w_ffn_down, m_final_g, v_norm1_g, v_w_in, v_b_gate, v_gmlp_ln_g, v_gmlp_ln_b, v_w_spatial, v_b_spatial, v_w_shortconv, v_w_branch, v_w_out, v_norm2_g, v_w_ffn_up, v_w_ffn_conv, v_b_ffn_conv, v_w_ffn_down, v_final_g):
    weights = dict(norm1_g=norm1_g, w_in=w_in, b_gate=b_gate, gmlp_ln_g=gmlp_ln_g, gmlp_ln_b=gmlp_ln_b,
                   w_spatial=w_spatial, b_spatial=b_spatial, w_shortconv=w_shortconv, w_branch=w_branch, w_out=w_out,
                   norm2_g=norm2_g, w_ffn_up=w_ffn_up, w_ffn_conv=w_ffn_conv, b_ffn_conv=b_ffn_conv,
                   w_ffn_down=w_ffn_down, final_g=final_g)
    mom = dict(norm1_g=m_norm1_g, w_in=m_w_in, b_gate=m_b_gate, gmlp_ln_g=m_gmlp_ln_g, gmlp_ln_b=m_gmlp_ln_b,
               w_spatial=m_w_spatial, b_spatial=m_b_spatial, w_shortconv=m_w_shortconv, w_branch=m_w_branch,
               w_out=m_w_out, norm2_g=m_norm2_g, w_ffn_up=m_w_ffn_up, w_ffn_conv=m_w_ffn_conv,
               b_ffn_conv=m_b_ffn_conv, w_ffn_down=m_w_ffn_down, final_g=m_final_g)
    vel = dict(norm1_g=v_norm1_g, w_in=v_w_in, b_gate=v_b_gate, gmlp_ln_g=v_gmlp_ln_g, gmlp_ln_b=v_gmlp_ln_b,
               w_spatial=v_w_spatial, b_spatial=v_b_spatial, w_shortconv=v_w_shortconv, w_branch=v_w_branch,
               w_out=v_w_out, norm2_g=v_norm2_g, w_ffn_up=v_w_ffn_up, w_ffn_conv=v_w_ffn_conv,
               b_ffn_conv=v_b_ffn_conv, w_ffn_down=v_w_ffn_down, final_g=v_final_g)

    cx, cy, cc = _mesh_pos()
    chip = 2 * cx + cy
    core_arr = cc.astype(jnp.int32).reshape(1)
    chip_arr = chip.astype(jnp.int32).reshape(1)
    pos_arr = jnp.stack([chip, cc]).astype(jnp.int32)
    t_len = x.shape[1]
    xs = x.reshape(t_len, D_MODEL)
    target = loss_target.reshape(t_len, D_MODEL)
    pipe = _Pipe()

    full = {}

    def gather(keys):
        slots = [_cast_into_slot(n, la, weights[n].reshape((N_LAYERS,) + BIG[n]), chip_arr) for n, la in keys]

        def then(*bufs):
            full.update(zip(keys, bufs))

        pipe.add(_gather_stage(slots, then))

    mixer_w = ("w_in", "w_branch", "w_out")
    ffn_w = ("w_ffn_up", "w_ffn_down")
    gather([(n, 0) for n in mixer_w])
    pipe.flush()

    idx = jnp.arange(GMLP_BLOCK) // CHUNK
    mask = idx[None, :] <= idx[:, None]
    wm_all = jnp.where(mask[None, None], w_spatial, 0.0)
    wm_bf = wm_all.astype(BF16)
    wmt_bf = jnp.swapaxes(wm_all, -1, -2).astype(BF16)
    bsf = jnp.repeat(jnp.swapaxes(b_spatial, -1, -2), 128, axis=-1)
    wsc_full = lax.dynamic_update_slice(jnp.zeros((N_LAYERS, 3, D_B), F32), w_shortconv, (0, 0, chip * (D_B // 4)))
    wfc_full = lax.dynamic_update_slice(jnp.zeros((N_LAYERS, 3, D_FF), F32), w_ffn_conv, (0, 0, chip * (D_FF // 4)))
    taps = _all_reduce_small("conv_taps", _pack([wsc_full, wfc_full]))
    wsc_full, wfc_full = _unpack(taps * 0.5, [wsc_full, wfc_full])

    def row(a):
        return a.reshape(1, -1)

    def mixer_args(la):
        return (row(norm1_g[la]), row(b_gate[la]), row(gmlp_ln_g[la]), row(gmlp_ln_b[la]))

    def mixer_weights(la):
        return tuple(full[(n, la)] for n in mixer_w)

    def ffn_weights(la):
        return tuple(full[(n, la)] for n in ffn_w)

    saved = []
    h_in = xs
    for la in range(N_LAYERS):
        gather([(n, la) for n in ffn_w])
        *kept, mg, h1, x2 = pipe.carry(lambda st: _mixer_fwd(
            la, h_in, *mixer_args(la), wm_bf[la], bsf[la], _pad8(wsc_full[la]), *mixer_weights(la), st))
        ya, yb = kept[1], kept[2]
        if la + 1 < N_LAYERS:
            gather([(n, la + 1) for n in mixer_w])
        head = (target, row(final_g)) if la == N_LAYERS - 1 else None
        up, silu, dsilu, act, h2, *rest = pipe.carry(lambda st: _ffn_fwd(
            la, x2, row(norm2_g[la]), _pad8(wfc_full[la]), row(b_ffn_conv[la]), *ffn_weights(la), st, head=head))
        saved.append(dict(x=h_in, ya=ya, yb=yb, mixer=[kept[0]] + kept[3:], mg=mg, h1=h1, x2=x2, up=up, silu=silu,
                          dsilu=dsilu, act=act, h2=h2))
        h_in = rest[0]
    dx, dgf8, loss8 = rest

    reduced_big = {}

    def reduce_big(name, la, grad):
        tag = f"{name}_l{la}"

        def after_pair(other):
            psum = _pair_sum(tag, grad, other, core_arr)

            def after_chips(got):
                final = _chip_sum(tag, grad, other, got, pos_arr)
                pipe.add(_pair_fill_stage(final, lambda done: reduced_big.__setitem__((name, la), done)))

            pipe.add(_chip_send_stage(psum, after_chips))

        pipe.add(_pair_send_stage(grad, after_pair))

    small = {n: [None] * N_LAYERS for n in SMALL}
    spread = {}
    for la in reversed(range(N_LAYERS)):
        s = saved[la]
        dx3 = dx
        dx2, dup, dx3b, dg2, dbfc, dwfc = pipe.carry(lambda st: _ffn_bwd(
            la, dx3, s["x2"], s["up"], s["silu"], s["dsilu"], row(norm2_g[la]), _pad8(wfc_full[la]),
            *ffn_weights(la), st))
        g, = pipe.carry(lambda st: _wgrad("w_ffn_down", la, s["act"], dx3b, 704, 1024, 1408, 1024, st))
        reduce_big("w_ffn_down", la, g)
        g, = pipe.carry(lambda st: _wgrad("w_ffn_up", la, s["h2"], dup, 1024, 1408, 1024, 1408, st))
        reduce_big("w_ffn_up", la, g)
        dxl, dz, da, db, dx2b, dg1, dbg, dlng, dlnb, dwm, dbsf, dwsc = pipe.carry(lambda st: _mixer_bwd(
            la, dx2, s["x"], *s["mixer"], row(norm1_g[la]), row(gmlp_ln_g[la]), row(gmlp_ln_b[la]), wmt_bf[la],
            _pad8(wsc_full[la]), *mixer_weights(la), st))
        small["norm1_g"][la] = dg1.sum(0)
        small["b_gate"][la] = dbg.sum(0)
        small["gmlp_ln_g"][la] = dlng.sum(0)
        small["gmlp_ln_b"][la] = dlnb.sum(0)
        small["w_spatial"][la] = jnp.where(mask[None], dwm, 0.0)
        small["b_spatial"][la] = dbsf.reshape(128, A_HEADS, 128).sum(-1).T
        small["w_shortconv"][la] = dwsc.sum(1)
        small["norm2_g"][la] = dg2.sum(0)
        small["w_ffn_conv"][la] = dwfc.sum(1)
        small["b_ffn_conv"][la] = dbfc.sum(0)
        if la == 0:
            small_local = ([jnp.stack(small[n]) for n in SMALL[:-1]]
                           + [dgf8.sum(0), 0.5 * loss8.sum().reshape(1) / D_MODEL])
            mine = _pack(small_local)

            def after_swap(other, mine=mine):
                pair = _sum_slots("small_pair", jnp.stack([mine, other]))
                pipe.add(_chip_spread_stage(pair, lambda slots: spread.__setitem__("slots", slots)))

            pipe.add(_pair_swap_stage(mine, after_swap))
        for part, tag in enumerate(("w_in_a", "w_in_b")):
            g, = pipe.carry(lambda st: _wgrad(tag, la, s["h1"], dz, 512, 1152, 512, 1152, st, a_first=part))
            reduce_big(tag, la, g)
        g, = pipe.carry(lambda st: _wgrad("w_out", la, s["mg"], dx2b, 256, 1024, 1024, 1024, st), long=False)
        reduce_big("w_out", la, g)
        g, = pipe.carry(lambda st: _wgrad_branch(la, s["ya"], da, s["yb"], db, st), long=False)
        reduce_big("w_branch", la, g)
        dx = dxl
    grad_x = dx.reshape(x.shape)
    pipe.flush()

    for la in range(N_LAYERS):
        reduced_big[("w_in", la)] = jnp.concatenate([reduced_big[("w_in_a", la)], reduced_big[("w_in_b", la)]], axis=0)
    reduced = _unpack(_sum_slots("small_grads", spread["slots"]), small_local)
    loss = reduced[-1].reshape(())
    grads = dict(zip(SMALL, reduced[:-1]))
    grads["w_shortconv"] = lax.dynamic_slice(grads["w_shortconv"], (0, 0, chip * (D_B // 4)), (N_LAYERS, 3, D_B // 4))
    grads["w_ffn_conv"] = lax.dynamic_slice(grads["w_ffn_conv"], (0, 0, chip * (D_FF // 4)), (N_LAYERS, 3, D_FF // 4))

    delta, new_m, new_v = {}, {}, {}
    for n in BIG_NAMES:
        shape3 = (N_LAYERS,) + BIG[n]
        res = _adamw_big(n, weights[n].reshape(shape3), reduced_big[(n, 0)], reduced_big[(n, 1)],
                         mom[n].reshape(shape3), vel[n].reshape(shape3))
        grads[n], delta[n], new_m[n], new_v[n] = (a.reshape(weights[n].shape) for a in res)
    small_w = [weights[n] for n in SMALL]
    packed = [_pack([src[n] for n in SMALL]) for src in (weights, grads, mom, vel)]
    for dst, res in zip((delta, new_m, new_v), _adamw("small", *packed)):
        dst.update(zip(SMALL, _unpack(res, small_w)))

    return (loss, grad_x, *[grads[n] for n in ALL_WEIGHTS], *[delta[n] for n in ALL_WEIGHTS],
            *[new_m[n] for n in ALL_WEIGHTS], *[new_v[n] for n in ALL_WEIGHTS])
```

```python
import jax
import jax.numpy as jnp
from jax import lax
from jax.experimental import pallas as pl
from jax.experimental.pallas import tpu as pltpu

F32 = jnp.float32
BF16 = jnp.bfloat16
MESH = pl.DeviceIdType.MESH
ANY = pl.BlockSpec(memory_space=pl.ANY)

D_MODEL = 1024
D_A = 512
D_B = 512
D_IN = 4608
D_FF = 2816
GMLP_BLOCK = 128
CHUNK = 64
A_HEADS = 4
N_LAYERS = 2
N_CHIPS = 4
N_DEVICES = 8
RMS_EPS = 1e-6
LN_EPS = 1e-5
ADAM_LR = 0.001
ADAM_B1 = 0.9
ADAM_B2 = 0.999
ADAM_EPS = 1e-08
ADAM_WD = 0.01
ADAM_STEP = 10

C_U, C_V, C_BG, C_CG, C_HB, C_GA, C_GB = 0, 512, 1024, 1536, 2048, 2560, 3584

V7X_VMEM_LIMIT = 60 * 1024 * 1024
TM_MIX = 256
TM_FFN = 256
TK_WGRAD = 2048
SLOW_COPY_BYTES = 640 * 1024
FF_CHUNKS = ((0, 768), (768, 1536), (1536, 2304), (2304, 2816))
GELU_C0 = 0.7978845608028654
GELU_C1 = 0.044715

BIG = {
    "w_in": (1024, 1152),
    "w_branch": (1024, 256),
    "w_out": (256, 1024),
    "w_ffn_up": (1024, 1408),
    "w_ffn_down": (704, 1024),
}
BIG_NAMES = tuple(BIG)


def _params(sem=("arbitrary",), vmem=V7X_VMEM_LIMIT):
    return pltpu.CompilerParams(dimension_semantics=sem, vmem_limit_bytes=vmem)


def _gelu(x):
    x2 = x * x
    t = jnp.tanh(GELU_C0 * x * (1.0 + GELU_C1 * x2))
    return 0.5 * x * (1.0 + t), t


def _gelu_grad(x, t):
    return 0.5 * (1.0 + t) + 0.5 * x * (1.0 - t * t) * GELU_C0 * (1.0 + 3.0 * GELU_C1 * x * x)


def _colsum8(v):
    r, n = v.shape
    return v.reshape(r // 8, 8, n).sum(axis=0)


def _dot(a, b):
    return jnp.dot(a, b, preferred_element_type=F32)


def _dot_nt(a, b):
    return lax.dot_general(a, b, (((1,), (1,)), ((), ())), preferred_element_type=F32)


def _dot_tn(a, b):
    return lax.dot_general(a, b, (((0,), (0,)), ((), ())), preferred_element_type=F32)


def _shift_down(v, carry, n):
    rows = lax.broadcasted_iota(jnp.int32, (8, v.shape[1]), 0)
    out = pltpu.roll(v, n, 0)
    head = out[0:8, :]
    for r in range(n):
        head = jnp.where(rows == r, carry[8 - n + r:8 - n + r + 1, :], head)
    return jnp.concatenate([head, out[8:, :]], axis=0)


def _shift_up(v, carry, n):
    tm = v.shape[0]
    rows = lax.broadcasted_iota(jnp.int32, (8, v.shape[1]), 0)
    out = pltpu.roll(v, tm - n, 0)
    tail = out[tm - 8:tm, :]
    for r in range(n):
        tail = jnp.where(rows == 8 - n + r, carry[r:r + 1, :], tail)
    return jnp.concatenate([out[0:tm - 8, :], tail], axis=0)


def _sigmoid(x):
    return 0.5 * jnp.tanh(0.5 * x) + 0.5


def _start_all(copies):
    for cp in copies:
        cp.start()


def _wait_all(copies):
    for cp in copies:
        cp.wait()


def _load_col_sharded(src, dst, sems, first):
    cs = src.shape[-1]
    return [pltpu.make_async_copy(src.at[k], dst.at[:, k * cs:(k + 1) * cs], sems.at[first + k])
            for k in range(N_CHIPS)]


def _load_row_sharded(src, dst, sems, first):
    rs = src.shape[-2]
    return [pltpu.make_async_copy(src.at[k], dst.at[k * rs:(k + 1) * rs, :], sems.at[first + k])
            for k in range(N_CHIPS)]


def _load_branch(src, dst, sems, first):
    return [pltpu.make_async_copy(src.at[k, pl.ds(m * D_A, D_A), :], dst.at[m, :, k * 256:(k + 1) * 256],
                                  sems.at[first + 2 * k + m])
            for k in range(N_CHIPS) for m in range(2)]


def _row_spec(tm, n, rev=None):
    if rev is None:
        return pl.BlockSpec((tm, n), lambda i: (i, 0))
    return pl.BlockSpec((tm, n), lambda i: (rev - 1 - i, 0))


def _const_spec(shape):
    nd = len(shape)
    return pl.BlockSpec(shape, lambda i: (0,) * nd)


def _mesh_pos():
    return lax.axis_index("x"), lax.axis_index("y"), lax.axis_index("c")


def _other_chips(x, y):
    return [(1 - x, y, 2 * (1 - x) + y), (x, 1 - y, 2 * x + (1 - y)), (1 - x, 1 - y, 2 * (1 - x) + (1 - y))]


def _remote(src, dst, ssem, rsem, to):
    return pltpu.make_async_remote_copy(src_ref=src, dst_ref=dst, send_sem=ssem, recv_sem=rsem, device_id=to,
                                        device_id_type=MESH)


def _half(ref, which, h):
    start = pl.multiple_of(which * h, 8)
    if len(ref.shape) == 2:
        return ref.at[pl.ds(start, h), :]
    return ref.at[:, pl.ds(start, h), :]


class _Stage:
    def __init__(self, ins=(), inouts=(), outs=(), n_sems=0, start=None, mid=None, finish=None, then=None, slow=False):
        self.ins, self.inouts, self.outs = list(ins), list(inouts), list(outs)
        self.n_sems, self.start, self.mid, self.finish, self.then = n_sems, start, mid, finish, then
        self.slow = slow


def _gather_stage(bufs, then):
    n = len(bufs)

    def copies(io, sem):
        x, y, c = _mesh_pos()
        me = 2 * x + y
        ici, fwd, got = [], [], []
        for w in range(n):
            h = io[w].shape[1] // 2
            for j, (px, py, pk) in enumerate(_other_chips(x, y)):
                mine = _half(io[w].at[me], c, h)
                theirs = _half(io[w].at[pk], c, h)
                ici.append(_remote(mine, mine, sem(12 * w + j), sem(12 * w + 3 + j), (px, py, c)))
                got.append(_remote(theirs, theirs, sem(12 * w + j), sem(12 * w + 3 + j), (px, py, c)))
                fwd.append(_remote(theirs, theirs, sem(12 * w + 6 + j), sem(12 * w + 9 + j), (x, y, 1 - c)))
        return ici, got, fwd

    def start(ins, io, outs, sem):
        _start_all(copies(io, sem)[0])

    def mid(ins, io, outs, sem):
        _, got, fwd = copies(io, sem)
        for g, f in zip(got, fwd):
            g.wait_recv()
            f.start()

    def finish(ins, io, outs, sem):
        x, y, c = _mesh_pos()
        ici, _, fwd = copies(io, sem)
        for w in range(n):
            h = io[w].shape[1] // 2
            for j, (px, py, pk) in enumerate(_other_chips(x, y)):
                other = _half(io[w].at[pk], 1 - c, h)
                _remote(other, other, sem(12 * w + 6 + j), sem(12 * w + 9 + j), (x, y, 1 - c)).wait_recv()
        for cp in ici + fwd:
            cp.wait_send()

    return _Stage(inouts=bufs, n_sems=12 * n, start=start, mid=mid, finish=finish, then=then)


def _pair_send_stage(grad, then):
    h = grad.shape[1] // 2

    def copy(ins, outs, sem):
        x, y, c = _mesh_pos()
        return _remote(_half(ins[0], 1 - c, h), outs[0], sem(0), sem(1), (x, y, 1 - c))

    return _Stage(ins=[grad], outs=[jax.ShapeDtypeStruct((N_CHIPS, h, grad.shape[2]), F32)], n_sems=2,
                  start=lambda ins, io, outs, sem: copy(ins, outs, sem).start(),
                  finish=lambda ins, io, outs, sem: copy(ins, outs, sem).wait(), then=then)


def _chip_send_stage(psum, then):
    def copies(ins, outs, sem):
        x, y, c = _mesh_pos()
        return [_remote(ins[0].at[pk], outs[0].at[j], sem(j), sem(3 + j), (px, py, c))
                for j, (px, py, pk) in enumerate(_other_chips(x, y))]

    return _Stage(ins=[psum], outs=[jax.ShapeDtypeStruct((3,) + psum.shape[1:], BF16)], n_sems=6,
                  start=lambda ins, io, outs, sem: _start_all(copies(ins, outs, sem)),
                  finish=lambda ins, io, outs, sem: _wait_all(copies(ins, outs, sem)), then=then,
                  slow=psum.shape[1] * psum.shape[2] * 2 > SLOW_COPY_BYTES)


def _pair_fill_stage(final, then):
    h = final.shape[0] // 2

    def copy(io, sem):
        x, y, c = _mesh_pos()
        mine = _half(io[0], c, h)
        return _remote(mine, mine, sem(0), sem(1), (x, y, 1 - c))

    return _Stage(inouts=[final], n_sems=2,
                  start=lambda ins, io, outs, sem: copy(io, sem).start(),
                  finish=lambda ins, io, outs, sem: copy(io, sem).wait(), then=then)


def _pair_swap_stage(packed, then):
    def copy(ins, outs, sem):
        x, y, c = _mesh_pos()
        return _remote(ins[0], outs[0], sem(0), sem(1), (x, y, 1 - c))

    return _Stage(ins=[packed], outs=[jax.ShapeDtypeStruct(packed.shape, F32)], n_sems=2,
                  start=lambda ins, io, outs, sem: copy(ins, outs, sem).start(),
                  finish=lambda ins, io, outs, sem: copy(ins, outs, sem).wait(), then=then)


def _chip_spread_stage(psum, then):
    def copies(ins, outs, sem):
        x, y, c = _mesh_pos()
        me = 2 * x + y
        cps = [_remote(ins[0], outs[0].at[me], sem(j), sem(3 + j), (px, py, c))
               for j, (px, py, pk) in enumerate(_other_chips(x, y))]
        return cps, pltpu.make_async_copy(ins[0], outs[0].at[me], sem(6))

    def start(ins, io, outs, sem):
        cps, own = copies(ins, outs, sem)
        own.start()
        _start_all(cps)

    def finish(ins, io, outs, sem):
        cps, own = copies(ins, outs, sem)
        _wait_all(cps)
        own.wait()

    return _Stage(ins=[psum], outs=[jax.ShapeDtypeStruct((N_CHIPS,) + psum.shape, F32)], n_sems=7,
                  start=start, finish=finish, then=then)


def _staged_call(core, *, name, grid, in_specs, out_specs, out_shape, scratch_shapes, args, stages):
    n_in, n_out, n_scr = len(args), len(out_shape), len(scratch_shapes)
    s_args, s_outs, aliases, layout = [], [], {}, []
    n_sems = 0
    for st in stages:
        i0, o0 = len(s_args), len(s_outs)
        s_args += st.ins + st.inouts
        for q in range(len(st.inouts)):
            aliases[n_in + i0 + len(st.ins) + q] = n_out + o0 + q
        s_outs += [jax.ShapeDtypeStruct(a.shape, a.dtype) for a in st.inouts] + st.outs
        layout.append((i0, o0, n_sems))
        n_sems += st.n_sems
    steps = 1
    for g in grid:
        steps *= g

    def body(*refs):
        own_in = refs[:n_in]
        s_in = refs[n_in:n_in + len(s_args)]
        rest = refs[n_in + len(s_args):]
        own_out = rest[:n_out]
        s_out = rest[n_out:n_out + len(s_outs)]
        scr = rest[n_out + len(s_outs):]

        def run(which):
            for st, (i0, o0, s0) in zip(stages, layout):
                fn = getattr(st, which)
                if fn is not None:
                    fn(s_in[i0:i0 + len(st.ins)], s_out[o0:o0 + len(st.inouts)],
                       s_out[o0 + len(st.inouts):o0 + len(st.inouts) + len(st.outs)],
                       lambda k, s0=s0: scr[n_scr].at[s0 + k])

        if not stages:
            core(*own_in, *own_out, *scr[:n_scr])
            return
        step = 0
        for d, g in enumerate(grid):
            step = step * g + pl.program_id(d)
        if steps == 1:
            run("start")
            core(*own_in, *own_out, *scr[:n_scr])
            run("mid")
            run("finish")
            return
        pl.when(step == 0)(lambda: run("start"))
        core(*own_in, *own_out, *scr[:n_scr])
        pl.when(step == (3 * steps) // 4)(lambda: run("mid"))
        pl.when(step == steps - 1)(lambda: run("finish"))

    sem = ("arbitrary",) * len(grid) if stages else ("parallel",) * max(len(grid) - 1, 0) + ("arbitrary",) * min(len(grid), 1)
    res = pl.pallas_call(
        body, name=name, grid=grid,
        in_specs=list(in_specs) + [ANY] * len(s_args),
        out_specs=list(out_specs) + [ANY] * len(s_outs),
        out_shape=list(out_shape) + s_outs,
        input_output_aliases=aliases,
        scratch_shapes=list(scratch_shapes) + ([pltpu.SemaphoreType.DMA((n_sems,))] if stages else []),
        compiler_params=_params(sem) if grid else pltpu.CompilerParams(vmem_limit_bytes=V7X_VMEM_LIMIT),
    )(*args, *s_args)
    return list(res[:n_out]), list(res[n_out:])


class _Pipe:
    def __init__(self):
        self.ready = []
        self.flushes = 0

    def add(self, stage):
        self.ready.append(stage)

    def carry(self, call, long=True):
        stages = [st for st in self.ready if long or not st.slow]
        self.ready = [st for st in self.ready if not (long or not st.slow)]
        own, outs = call(stages)
        k = 0
        for st in stages:
            n = len(st.inouts) + len(st.outs)
            st.then(*outs[k:k + n])
            k += n
        return own

    def flush(self):
        while self.ready:
            self.flushes += 1
            self.carry(lambda stages: _staged_call(
                lambda *refs: None, name=f"comm_tail_{self.flushes}", grid=(), in_specs=[], out_specs=[], out_shape=[],
                scratch_shapes=[], args=[], stages=stages))


def _mixer_fwd(layer, x, g1, bgate, lng, lnb, wm, bsf, wsc, win_g, wb_g, wout_g, stages):
    t_len = x.shape[0]
    tm = min(TM_MIX, t_len)
    nt = t_len // tm
    nb = tm // GMLP_BLOCK

    def core(x_ref, x_late_ref, g1_ref, bgate_ref, lng_ref, lnb_ref, wm_ref, bsf_ref, wsc_ref, win_hbm, wb_hbm, wout_hbm,
             zc_ref, ya_ref, yb_ref, q_ref, sa_ref, ca_ref, sb_ref, cb_ref, ug_ref, fu_ref, xh_ref, cv_ref,
             mg_ref, h_ref, x2_ref,
             win_v, wb_v, wout_v, carry, vn_s, f_s, z_s, sems):
        i = pl.program_id(0)

        @pl.when(i == 0)
        def _():
            cps = (_load_col_sharded(win_hbm, win_v, sems, 0) + _load_branch(wb_hbm, wb_v, sems, 4)
                   + _load_row_sharded(wout_hbm, wout_v, sems, 12))
            _start_all(cps)
            carry[...] = jnp.zeros_like(carry)
            z_s[...] = jnp.zeros_like(z_s)
            _wait_all(cps)

        xv = x_ref[...]
        r = lax.rsqrt(jnp.mean(xv * xv, axis=-1, keepdims=True) + RMS_EPS)
        h_ref[...] = (xv * r * g1_ref[...]).astype(BF16)

        def zcols(c0, n, keep=None):
            zv = z_s[:, c0:c0 + n]
            z_s[:, c0:c0 + n] = _dot(h_ref[...], win_v[:, c0:c0 + n])
            if keep is not None:
                zc_ref[:, keep * D_B:(keep + 1) * D_B] = zv.astype(BF16)
            return zv

        v = zcols(C_V, D_A)
        vg, tv = _gelu(v)
        mu = jnp.mean(vg, axis=-1, keepdims=True)
        vc = vg - mu
        rstd = lax.rsqrt(jnp.mean(vc * vc, axis=-1, keepdims=True) + LN_EPS)
        xh = vc * rstd
        xh_ref[...] = xh.astype(BF16)
        cv_ref[...] = (rstd * _gelu_grad(v, tv)).astype(BF16)
        vn_s[...] = (xh * lng_ref[...] + lnb_ref[...]).astype(BF16)
        for hd in range(A_HEADS):
            cols = slice(hd * 128, (hd + 1) * 128)
            vcat = jnp.concatenate([vn_s[b * 128:(b + 1) * 128, cols] for b in range(nb)], axis=1)
            fcat = _dot(wm_ref[hd], vcat)
            for b in range(nb):
                f_s[b * 128:(b + 1) * 128, cols] = fcat[:, b * 128:(b + 1) * 128]
        u = zcols(C_U, D_A)
        ug, tu = _gelu(u)
        ug_ref[...] = ug.astype(BF16)
        fb = f_s[...] + jnp.concatenate([bsf_ref[...]] * nb, axis=0)
        fu_ref[...] = (fb * _gelu_grad(u, tu)).astype(BF16)
        ya_ref[...] = (ug * fb).astype(BF16)

        p = zcols(C_CG, D_B, keep=1) * zcols(C_HB, D_B, keep=2)
        cr = carry[...]
        q = wsc_ref[0:1, :] * _shift_down(p, cr, 2) + wsc_ref[1:2, :] * _shift_down(p, cr, 1) + wsc_ref[2:3, :] * p
        carry[...] = p[tm - 8:tm, :]
        q_ref[...] = q.astype(BF16)
        yb_ref[...] = (zcols(C_BG, D_B, keep=0) * q).astype(BF16)

        av = _dot(ya_ref[...], wb_v[0])
        sa = _sigmoid(zcols(C_GA, D_MODEL) + bgate_ref[:, 0:D_MODEL])
        sa_ref[...] = sa.astype(BF16)
        mg = sa * av
        ca_ref[...] = (mg * (1.0 - sa)).astype(BF16)
        bv = _dot(yb_ref[...], wb_v[1])
        sb = _sigmoid(zcols(C_GB, D_MODEL) + bgate_ref[:, D_MODEL:2 * D_MODEL])
        sb_ref[...] = sb.astype(BF16)
        mb = sb * bv
        cb_ref[...] = (mb * (1.0 - sb)).astype(BF16)
        mg_ref[...] = (mg + mb).astype(BF16)
        x2_ref[...] = x_late_ref[...] + _dot(mg_ref[...], wout_v[...])

    def tile(n, lag):
        return pl.BlockSpec((tm, n), lambda i: (jnp.clip(i - lag, 0, nt - 1), 0))

    outs = [
        jax.ShapeDtypeStruct((t_len, 3 * D_B), BF16),
        jax.ShapeDtypeStruct((t_len, D_A), BF16),
        jax.ShapeDtypeStruct((t_len, D_B), BF16),
        jax.ShapeDtypeStruct((t_len, D_B), BF16),
        jax.ShapeDtypeStruct((t_len, D_MODEL), BF16),
        jax.ShapeDtypeStruct((t_len, D_MODEL), BF16),
        jax.ShapeDtypeStruct((t_len, D_MODEL), BF16),
        jax.ShapeDtypeStruct((t_len, D_MODEL), BF16),
        jax.ShapeDtypeStruct((t_len, D_A), BF16),
        jax.ShapeDtypeStruct((t_len, D_A), BF16),
        jax.ShapeDtypeStruct((t_len, D_A), BF16),
        jax.ShapeDtypeStruct((t_len, D_A), BF16),
        jax.ShapeDtypeStruct((t_len, D_MODEL), BF16),
        jax.ShapeDtypeStruct((t_len, D_MODEL), BF16),
        jax.ShapeDtypeStruct((t_len, D_MODEL), F32),
    ]
    return _staged_call(
        core, name=f"mixer_fwd_l{layer}", grid=(nt + 1,),
        in_specs=[tile(D_MODEL, 0), tile(D_MODEL, 1), _const_spec((1, D_MODEL)), _const_spec((1, 2 * D_MODEL)),
                  _const_spec((1, D_A)), _const_spec((1, D_A)), _const_spec((A_HEADS, 128, 128)),
                  _const_spec((128, D_A)), _const_spec((8, D_B)), ANY, ANY, ANY],
        out_specs=[tile(o.shape[1], 0 if k == len(outs) - 2 else 1) for k, o in enumerate(outs)],
        out_shape=outs,
        scratch_shapes=[pltpu.VMEM((D_MODEL, D_IN), BF16), pltpu.VMEM((2, D_A, D_MODEL), BF16),
                        pltpu.VMEM((D_MODEL, D_MODEL), BF16), pltpu.VMEM((8, D_B), F32),
                        pltpu.VMEM((tm, D_A), BF16), pltpu.VMEM((tm, D_A), F32), pltpu.VMEM((tm, D_IN), F32),
                        pltpu.SemaphoreType.DMA((16,))],
        args=[x, x, g1, bgate, lng, lnb, wm, bsf, wsc, win_g, wb_g, wout_g], stages=stages)


def _ffn_fwd(layer, x2, g2, wfc, bfc, wup_g, wdown_g, stages, head=None):
    t_len = x2.shape[0]
    tm = min(TM_FFN, t_len)
    nt = t_len // tm

    def core(*refs):
        if head is None:
            (x_ref, g2_ref, wfc_ref, bfc_ref, wup_hbm, wdown_hbm, up_ref, silu_ref, dsilu_ref, act_ref, h_ref, x3_ref,
             wup_v, wdown_v, carry, sems) = refs
        else:
            (x_ref, g2_ref, wfc_ref, bfc_ref, t_ref, gf_ref, wup_hbm, wdown_hbm, up_ref, silu_ref, dsilu_ref, act_ref,
             h_ref, dx_ref, dgf_ref, loss_ref, wup_v, wdown_v, carry, sems) = refs
        i = pl.program_id(0)

        @pl.when(i == 0)
        def _():
            cps = _load_col_sharded(wup_hbm, wup_v, sems, 0) + _load_row_sharded(wdown_hbm, wdown_v, sems, 4)
            _start_all(cps)
            carry[...] = jnp.zeros_like(carry)
            if head is not None:
                dgf_ref[...] = jnp.zeros_like(dgf_ref)
                loss_ref[...] = jnp.zeros_like(loss_ref)
            _wait_all(cps)

        xv = x_ref[...]
        r = lax.rsqrt(jnp.mean(xv * xv, axis=-1, keepdims=True) + RMS_EPS)
        h_ref[...] = (xv * r * g2_ref[...]).astype(BF16)
        gate = _dot(h_ref[...], wup_v[:, 0:D_FF])
        up_ref[:, 0:D_FF] = gate.astype(BF16)
        cr = carry[...]
        gc = (wfc_ref[0:1, :] * _shift_down(gate, cr, 2) + wfc_ref[1:2, :] * _shift_down(gate, cr, 1)
              + wfc_ref[2:3, :] * gate + bfc_ref[...])
        carry[...] = gate[tm - 8:tm, :]
        sg = _sigmoid(gc)
        silu = gc * sg
        silu_ref[...] = silu.astype(BF16)
        dsilu_ref[...] = (sg + silu * (1.0 - sg)).astype(BF16)
        val = _dot(h_ref[...], wup_v[:, D_FF:2 * D_FF])
        up_ref[:, D_FF:2 * D_FF] = val.astype(BF16)
        act_ref[...] = (silu * val).astype(BF16)
        x3 = x_ref[...] + _dot(act_ref[...], wdown_v[...])
        if head is None:
            x3_ref[...] = x3
        else:
            r3 = lax.rsqrt(jnp.mean(x3 * x3, axis=-1, keepdims=True) + RMS_EPS)
            xh = x3 * r3
            err = xh * gf_ref[...] - t_ref[...]
            loss_ref[...] += _colsum8(err * err)
            dy = err * (1.0 / D_MODEL)
            dgf_ref[...] += _colsum8(dy * xh)
            dxh = dy * gf_ref[...]
            dx_ref[...] = r3 * (dxh - xh * jnp.mean(dxh * xh, axis=-1, keepdims=True))

    outs = [
        jax.ShapeDtypeStruct((t_len, 2 * D_FF), BF16),
        jax.ShapeDtypeStruct((t_len, D_FF), BF16),
        jax.ShapeDtypeStruct((t_len, D_FF), BF16),
        jax.ShapeDtypeStruct((t_len, D_FF), BF16),
        jax.ShapeDtypeStruct((t_len, D_MODEL), BF16),
        jax.ShapeDtypeStruct((t_len, D_MODEL), F32),
    ]
    in_specs = [_row_spec(tm, D_MODEL), _const_spec((1, D_MODEL)), _const_spec((8, D_FF)), _const_spec((1, D_FF))]
    out_specs = [_row_spec(tm, o.shape[1]) for o in outs]
    args = [x2, g2, wfc, bfc]
    if head is not None:
        in_specs += [_row_spec(tm, D_MODEL), _const_spec((1, D_MODEL))]
        args += list(head)
        outs += [jax.ShapeDtypeStruct((8, D_MODEL), F32)] * 2
        out_specs += [_const_spec((8, D_MODEL))] * 2
    return _staged_call(
        core, name=f"ffn_fwd_l{layer}", grid=(nt,),
        in_specs=in_specs + [ANY, ANY], out_specs=out_specs, out_shape=outs,
        scratch_shapes=[pltpu.VMEM((D_MODEL, 2 * D_FF), BF16), pltpu.VMEM((D_FF, D_MODEL), BF16),
                        pltpu.VMEM((8, D_FF), F32), pltpu.SemaphoreType.DMA((8,))],
        args=args + [wup_g, wdown_g], stages=stages)


def _ffn_bwd(layer, dx3, x2, up, silu, dsilu, g2, wfc, wup_g, wdown_g, stages):
    t_len = x2.shape[0]
    tm = min(TM_FFN, t_len)
    nt = t_len // tm

    def core(dx3_ref, dx3_late_ref, x_ref, up_ref, silu_ref, dsilu_ref, g2_ref, wfc_ref, wup_hbm, wdown_hbm,
             dx2_ref, dup_ref, dx3b_ref, dg2_ref, dbfc_ref, dwfc_ref,
             wup_v, wdown_v, carry, da_s, dup_s, sems):
        i = pl.program_id(0)

        @pl.when(i == 0)
        def _():
            cps = _load_col_sharded(wup_hbm, wup_v, sems, 0) + _load_row_sharded(wdown_hbm, wdown_v, sems, 4)
            _start_all(cps)
            for ref in (carry, da_s, dup_s, dg2_ref, dbfc_ref, dwfc_ref):
                ref[...] = jnp.zeros_like(ref)
            _wait_all(cps)

        live = (i <= nt).astype(F32)
        dx3b_ref[...] = dx3_ref[...].astype(BF16)
        dh = jnp.zeros((tm, D_MODEL), F32)
        for c0, c1 in FF_CHUNKS:
            v0, v1 = D_FF + c0, D_FF + c1
            dh = dh + _dot_nt(dup_s[:, c0:c1], wup_v[:, c0:c1]) + _dot_nt(dup_s[:, v0:v1], wup_v[:, v0:v1])
            da = da_s[:, c0:c1]
            dval = (da * silu_ref[:, c0:c1].astype(F32)).astype(BF16)
            dup_ref[:, v0:v1] = dval
            dup_s[:, v0:v1] = dval
            dgc = da * up_ref[:, v0:v1].astype(F32) * dsilu_ref[:, c0:c1].astype(F32)
            cr = carry[:, c0:c1]
            dgc1 = _shift_up(dgc, cr, 1)
            dgc2 = _shift_up(dgc, cr, 2)
            carry[:, c0:c1] = jnp.where(i < nt, dgc[0:8, :], cr)
            gate = up_ref[:, c0:c1].astype(F32)
            dbfc_ref[:, c0:c1] += live * _colsum8(dgc)
            dwfc_ref[0, :, c0:c1] += live * _colsum8(dgc2 * gate)
            dwfc_ref[1, :, c0:c1] += live * _colsum8(dgc1 * gate)
            dwfc_ref[2, :, c0:c1] += live * _colsum8(dgc * gate)
            dgate = (wfc_ref[2:3, c0:c1] * dgc + wfc_ref[1:2, c0:c1] * dgc1 + wfc_ref[0:1, c0:c1] * dgc2).astype(BF16)
            dup_ref[:, c0:c1] = dgate
            dup_s[:, c0:c1] = dgate
            da_s[:, c0:c1] = _dot_nt(dx3b_ref[...], wdown_v[c0:c1, :])
        xv = x_ref[...]
        r = lax.rsqrt(jnp.mean(xv * xv, axis=-1, keepdims=True) + RMS_EPS)
        xh = xv * r
        dg2_ref[...] += _colsum8(dh * xh)
        dxh = dh * g2_ref[...]
        dx2_ref[...] = dx3_late_ref[...] + r * (dxh - xh * jnp.mean(dxh * xh, axis=-1, keepdims=True))

    def tile(n, lag):
        return pl.BlockSpec((tm, n), lambda i: (nt - 1 - jnp.clip(i - lag, 0, nt - 1), 0))

    outs = [
        jax.ShapeDtypeStruct((t_len, D_MODEL), F32),
        jax.ShapeDtypeStruct((t_len, 2 * D_FF), BF16),
        jax.ShapeDtypeStruct((t_len, D_MODEL), BF16),
        jax.ShapeDtypeStruct((8, D_MODEL), F32),
        jax.ShapeDtypeStruct((8, D_FF), F32),
        jax.ShapeDtypeStruct((3, 8, D_FF), F32),
    ]
    return _staged_call(
        core, name=f"ffn_bwd_l{layer}", grid=(nt + 2,),
        in_specs=[tile(D_MODEL, 0), tile(D_MODEL, 2), tile(D_MODEL, 2), tile(2 * D_FF, 1), tile(D_FF, 1), tile(D_FF, 1),
                  _const_spec((1, D_MODEL)), _const_spec((8, D_FF)), ANY, ANY],
        out_specs=[tile(D_MODEL, 2), tile(2 * D_FF, 1), tile(D_MODEL, 0),
                   _const_spec((8, D_MODEL)), _const_spec((8, D_FF)), _const_spec((3, 8, D_FF))],
        out_shape=outs,
        scratch_shapes=[pltpu.VMEM((D_MODEL, 2 * D_FF), BF16), pltpu.VMEM((D_FF, D_MODEL), BF16),
                        pltpu.VMEM((8, D_FF), F32), pltpu.VMEM((tm, D_FF), F32), pltpu.VMEM((tm, 2 * D_FF), BF16),
                        pltpu.SemaphoreType.DMA((8,))],
        args=[dx3, dx3, x2, up, silu, dsilu, g2, wfc, wup_g, wdown_g], stages=stages)


def _mixer_bwd(layer, dx2, x, zc, qs, sa, ca, sb, cb, ug, fu, xhs, cv, g1, lng, lnb, wmt, wsc, win_g, wb_g, wout_g,
               stages):
    t_len = x.shape[0]
    tm = min(TM_MIX, t_len)
    nt = t_len // tm
    nb = tm // GMLP_BLOCK

    def core(dx2_ref, x_ref, zc_ref, q_ref, sa_ref, ca_ref, sb_ref, cb_ref, ug_ref, fu_ref, xh_ref, cv_ref,
             g1_ref, lng_ref, lnb_ref, wmt_ref, wsc_ref, win_hbm, wb_hbm, wout_hbm,
             dx_ref, dz_ref, da_ref, db_ref, dx2b_ref, dg1_ref, dbgate_ref, dlng_ref, dlnb_ref, dwm_ref, dbsf_ref, dwsc_ref,
             win_v, wb_v, wout_v, carry, vn_s, df_s, dvn_s, sems):
        i = pl.program_id(0)

        @pl.when(i == 0)
        def _():
            cps = (_load_col_sharded(win_hbm, win_v, sems, 0) + _load_branch(wb_hbm, wb_v, sems, 4)
                   + _load_row_sharded(wout_hbm, wout_v, sems, 12))
            _start_all(cps)
            for ref in (carry, dg1_ref, dbgate_ref, dlng_ref, dlnb_ref, dwm_ref, dbsf_ref, dwsc_ref):
                ref[...] = jnp.zeros_like(ref)
            _wait_all(cps)

        def kept(k):
            return zc_ref[:, k * D_B:(k + 1) * D_B].astype(F32)

        def dz_cols(c0, n, val):
            dz_ref[:, c0:c0 + n] = val.astype(BF16)
            return _dot_nt(dz_ref[:, c0:c0 + n], win_v[:, c0:c0 + n])

        dx2b_ref[...] = dx2_ref[...].astype(BF16)
        dm = _dot_nt(dx2b_ref[...], wout_v[...])
        da_ref[...] = (dm * sa_ref[...].astype(F32)).astype(BF16)
        dga = dm * ca_ref[...].astype(F32)
        dh = dz_cols(C_GA, D_MODEL, dga)
        dbgate_ref[:, 0:D_MODEL] += _colsum8(dga)
        dya = _dot_nt(da_ref[...], wb_v[0])
        db_ref[...] = (dm * sb_ref[...].astype(F32)).astype(BF16)
        dgb = dm * cb_ref[...].astype(F32)
        dh = dh + dz_cols(C_GB, D_MODEL, dgb)
        dbgate_ref[:, D_MODEL:2 * D_MODEL] += _colsum8(dgb)
        dyb = _dot_nt(db_ref[...], wb_v[1])

        xh = xh_ref[...].astype(F32)
        vn_s[...] = (xh * lng_ref[...] + lnb_ref[...]).astype(BF16)
        df = dya * ug_ref[...].astype(F32)
        df_s[...] = df.astype(BF16)
        dbsf_acc = df[0:128, :]
        for b in range(1, nb):
            dbsf_acc = dbsf_acc + df[b * 128:(b + 1) * 128, :]
        dbsf_ref[...] += dbsf_acc
        for hd in range(A_HEADS):
            cols = slice(hd * 128, (hd + 1) * 128)
            vcat = jnp.concatenate([vn_s[b * 128:(b + 1) * 128, cols] for b in range(nb)], axis=1)
            dcat = jnp.concatenate([df_s[b * 128:(b + 1) * 128, cols] for b in range(nb)], axis=1)
            gcat = _dot(wmt_ref[hd], dcat)
            dwm_ref[hd] += _dot_nt(dcat, vcat)
            for b in range(nb):
                dvn_s[b * 128:(b + 1) * 128, cols] = gcat[:, b * 128:(b + 1) * 128]
        dh = dh + dz_cols(C_U, D_A, dya * fu_ref[...].astype(F32))
        dvn = dvn_s[...]
        dlng_ref[...] += _colsum8(dvn * xh)
        dlnb_ref[...] += _colsum8(dvn)
        dxh = dvn * lng_ref[...]
        dvc = dxh - jnp.mean(dxh, axis=-1, keepdims=True) - xh * jnp.mean(dxh * xh, axis=-1, keepdims=True)
        dh = dh + dz_cols(C_V, D_A, dvc * cv_ref[...].astype(F32))

        cg = kept(1)
        hbv = kept(2)
        p = cg * hbv
        dh = dh + dz_cols(C_BG, D_B, dyb * q_ref[...].astype(F32))
        dq = dyb * kept(0)
        cr = carry[...]
        dq1 = _shift_up(dq, cr, 1)
        dq2 = _shift_up(dq, cr, 2)
        carry[...] = dq[0:8, :]
        dwsc_ref[0] += _colsum8(dq2 * p)
        dwsc_ref[1] += _colsum8(dq1 * p)
        dwsc_ref[2] += _colsum8(dq * p)
        dp = wsc_ref[2:3, :] * dq + wsc_ref[1:2, :] * dq1 + wsc_ref[0:1, :] * dq2
        dh = dh + dz_cols(C_CG, D_B, dp * hbv)
        dh = dh + dz_cols(C_HB, D_B, dp * cg)

        xv = x_ref[...]
        r = lax.rsqrt(jnp.mean(xv * xv, axis=-1, keepdims=True) + RMS_EPS)
        xn = xv * r
        dg1_ref[...] += _colsum8(dh * xn)
        dxn = dh * g1_ref[...]
        dx_ref[...] = dx2_ref[...] + r * (dxn - xn * jnp.mean(dxn * xn, axis=-1, keepdims=True))

    outs = [
        jax.ShapeDtypeStruct((t_len, D_MODEL), F32),
        jax.ShapeDtypeStruct((t_len, D_IN), BF16),
        jax.ShapeDtypeStruct((t_len, D_MODEL), BF16),
        jax.ShapeDtypeStruct((t_len, D_MODEL), BF16),
        jax.ShapeDtypeStruct((t_len, D_MODEL), BF16),
        jax.ShapeDtypeStruct((8, D_MODEL), F32),
        jax.ShapeDtypeStruct((8, 2 * D_MODEL), F32),
        jax.ShapeDtypeStruct((8, D_A), F32),
        jax.ShapeDtypeStruct((8, D_A), F32),
        jax.ShapeDtypeStruct((A_HEADS, 128, 128), F32),
        jax.ShapeDtypeStruct((128, D_A), F32),
        jax.ShapeDtypeStruct((3, 8, D_B), F32),
    ]

    return _staged_call(
        core, name=f"mixer_bwd_l{layer}", grid=(nt,),
        in_specs=[_row_spec(tm, D_MODEL, nt), _row_spec(tm, D_MODEL, nt), _row_spec(tm, 3 * D_B, nt),
                  _row_spec(tm, D_B, nt), _row_spec(tm, D_MODEL, nt), _row_spec(tm, D_MODEL, nt),
                  _row_spec(tm, D_MODEL, nt), _row_spec(tm, D_MODEL, nt), _row_spec(tm, D_A, nt), _row_spec(tm, D_A, nt),
                  _row_spec(tm, D_A, nt), _row_spec(tm, D_A, nt),
                  _const_spec((1, D_MODEL)), _const_spec((1, D_A)), _const_spec((1, D_A)),
                  _const_spec((A_HEADS, 128, 128)), _const_spec((8, D_B)), ANY, ANY, ANY],
        out_specs=[_row_spec(tm, D_MODEL, nt), _row_spec(tm, D_IN, nt), _row_spec(tm, D_MODEL, nt),
                   _row_spec(tm, D_MODEL, nt), _row_spec(tm, D_MODEL, nt),
                   _const_spec((8, D_MODEL)), _const_spec((8, 2 * D_MODEL)), _const_spec((8, D_A)), _const_spec((8, D_A)),
                   _const_spec((A_HEADS, 128, 128)), _const_spec((128, D_A)), _const_spec((3, 8, D_B))],
        out_shape=outs,
        scratch_shapes=[pltpu.VMEM((D_MODEL, D_IN), BF16), pltpu.VMEM((2, D_A, D_MODEL), BF16),
                        pltpu.VMEM((D_MODEL, D_MODEL), BF16), pltpu.VMEM((8, D_B), F32),
                        pltpu.VMEM((tm, D_A), BF16), pltpu.VMEM((tm, D_A), BF16), pltpu.VMEM((tm, D_A), F32),
                        pltpu.SemaphoreType.DMA((16,))],
        args=[dx2, x, zc, qs, sa, ca, sb, cb, ug, fu, xhs, cv, g1, lng, lnb, wmt, wsc, win_g, wb_g, wout_g],
        stages=stages)


def _wgrad(name, layer, a, b, rows, cols, row_blk, col_blk, stages, a_first=0):
    t_len = a.shape[0]
    n = b.shape[1]
    tk = min(TK_WGRAD, t_len)
    col_sharded = n == N_CHIPS * cols
    m = rows if col_sharded else a.shape[1]
    grid = (m // row_blk, n // col_blk, t_len // tk)
    per_shard_c = cols // col_blk

    if col_sharded:
        out_shape = (N_CHIPS, rows, cols)
        out_spec = pl.BlockSpec((None, row_blk, col_blk), lambda i, j, k: (j // per_shard_c, i, j % per_shard_c))
    else:
        out_shape = (N_CHIPS * rows, cols)
        out_spec = pl.BlockSpec((row_blk, col_blk), lambda i, j, k: (i, j))

    def core(a_ref, b_ref, o_ref):
        @pl.when(pl.program_id(2) == 0)
        def _():
            o_ref[...] = jnp.zeros_like(o_ref)

        o_ref[...] += _dot_tn(a_ref[...], b_ref[...])

    own, outs = _staged_call(
        core, name=f"wgrad_{name}_l{layer}", grid=grid,
        in_specs=[pl.BlockSpec((tk, row_blk), lambda i, j, k: (k, a_first + i)),
                  pl.BlockSpec((tk, col_blk), lambda i, j, k: (k, j))],
        out_specs=[out_spec], out_shape=[jax.ShapeDtypeStruct(out_shape, F32)], scratch_shapes=[],
        args=[a, b], stages=stages)
    return [own[0].reshape(N_CHIPS, rows, cols)], outs


def _wgrad_branch(layer, ya, da, yb, db, stages):
    t_len = ya.shape[0]
    tk = min(TK_WGRAD, t_len)

    def core(ya_ref, da_ref, yb_ref, db_ref, o_ref):
        @pl.when(pl.program_id(1) == 0)
        def _():
            o_ref[...] = jnp.zeros_like(o_ref)

        o_ref[0:D_A, :] += _dot_tn(ya_ref[...], da_ref[...])
        o_ref[D_A:2 * D_A, :] += _dot_tn(yb_ref[...], db_ref[...])

    a_spec = pl.BlockSpec((tk, D_A), lambda j, k: (k, 0))
    d_spec = pl.BlockSpec((tk, 256), lambda j, k: (k, j))
    return _staged_call(
        core, name=f"wgrad_w_branch_l{layer}", grid=(N_CHIPS, t_len // tk),
        in_specs=[a_spec, d_spec, a_spec, d_spec],
        out_specs=[pl.BlockSpec((None, 2 * D_A, 256), lambda j, k: (j, 0, 0))],
        out_shape=[jax.ShapeDtypeStruct((N_CHIPS, 2 * D_A, 256), F32)], scratch_shapes=[],
        args=[ya, da, yb, db], stages=stages)


def _all_reduce_small(name, packed):
    rows = packed.shape[0]

    def body(src_ref, out_ref, slots, send, recv):
        x, y, c = _mesh_pos()
        me = 4 * x + 2 * y + c
        cps = []
        for d in range(1, N_DEVICES):
            peer = me ^ d
            cps.append(_remote(src_ref, slots.at[me], send.at[d - 1], recv.at[d - 1],
                               (peer // 4, (peer // 2) % 2, peer % 2)))
        _start_all(cps)
        slots[me] = src_ref[...]
        _wait_all(cps)
        acc = slots[0]
        for d in range(1, N_DEVICES):
            acc = acc + slots[d]
        out_ref[...] = acc

    return pl.pallas_call(
        body, name=f"all_reduce_{name}",
        in_specs=[pl.BlockSpec(memory_space=pltpu.VMEM)], out_specs=pl.BlockSpec(memory_space=pltpu.VMEM),
        out_shape=jax.ShapeDtypeStruct(packed.shape, F32),
        scratch_shapes=[pltpu.VMEM((N_DEVICES, rows, 128), F32), pltpu.SemaphoreType.DMA((7,)),
                        pltpu.SemaphoreType.DMA((7,))],
        compiler_params=pltpu.CompilerParams(vmem_limit_bytes=V7X_VMEM_LIMIT),
    )(packed)


def _flat_blk(rows, cols):
    blk = rows
    while blk * cols * 4 > 2 * 1024 * 1024 and blk % 16 == 0:
        blk //= 2
    return blk


def _cast_into_slot(name, layer, w, chip):
    _, rows, cols = w.shape
    blk = _flat_blk(rows, cols)

    def body(chip_ref, w_ref, o_ref):
        o_ref[...] = w_ref[...].astype(BF16)

    return pl.pallas_call(
        body, name=f"cast_{name}_l{layer}",
        grid_spec=pltpu.PrefetchScalarGridSpec(
            num_scalar_prefetch=1, grid=(rows // blk,),
            in_specs=[pl.BlockSpec((None, blk, cols), lambda i, chip_ref: (layer, i, 0))],
            out_specs=pl.BlockSpec((None, blk, cols), lambda i, chip_ref: (chip_ref[0], i, 0))),
        out_shape=jax.ShapeDtypeStruct((N_CHIPS, rows, cols), BF16),
        compiler_params=_params(("parallel",)),
    )(chip, w)


def _pair_sum(name, grad, other, core):
    _, h, cols = other.shape
    blk = _flat_blk(h, cols)
    nblk = h // blk

    def body(core_ref, g_ref, o_ref, s_ref):
        s_ref[...] = (g_ref[...] + o_ref[...]).astype(BF16)

    spec = pl.BlockSpec((None, blk, cols), lambda k, i, core_ref: (k, i, 0))
    return pl.pallas_call(
        body, name=f"pair_sum_{name}",
        grid_spec=pltpu.PrefetchScalarGridSpec(
            num_scalar_prefetch=1, grid=(N_CHIPS, nblk),
            in_specs=[pl.BlockSpec((None, blk, cols), lambda k, i, core_ref: (k, core_ref[0] * nblk + i, 0)), spec],
            out_specs=spec),
        out_shape=jax.ShapeDtypeStruct((N_CHIPS, h, cols), BF16),
        compiler_params=_params(("parallel", "parallel")),
    )(core, grad, other)


def _chip_sum(name, grad, other, got, pos):
    _, rows, cols = grad.shape
    h = rows // 2
    blk = _flat_blk(h, cols)
    nblk = h // blk

    def body(pos_ref, g_ref, o_ref, r_ref, f_ref):
        f_ref[...] = (((g_ref[...] + o_ref[...]) + r_ref[0].astype(F32)) + r_ref[1].astype(F32)) + r_ref[2].astype(F32)

    return pl.pallas_call(
        body, name=f"chip_sum_{name}",
        grid_spec=pltpu.PrefetchScalarGridSpec(
            num_scalar_prefetch=1, grid=(nblk,),
            in_specs=[pl.BlockSpec((None, blk, cols), lambda i, pos_ref: (pos_ref[0], pos_ref[1] * nblk + i, 0)),
                      pl.BlockSpec((None, blk, cols), lambda i, pos_ref: (pos_ref[0], i, 0)),
                      pl.BlockSpec((3, blk, cols), lambda i, pos_ref: (0, i, 0))],
            out_specs=pl.BlockSpec((blk, cols), lambda i, pos_ref: (pos_ref[1] * nblk + i, 0))),
        out_shape=jax.ShapeDtypeStruct((rows, cols), F32),
        compiler_params=_params(("parallel",)),
    )(pos, grad, other, got)


def _sum_slots(name, slots):
    n, rows, _ = slots.shape

    def body(s_ref, o_ref):
        acc = s_ref[0]
        for d in range(1, n):
            acc = acc + s_ref[d]
        o_ref[...] = acc

    return pl.pallas_call(
        body, name=f"sum_slots_{name}", grid=(1,),
        in_specs=[pl.BlockSpec((n, rows, 128), lambda i: (0, 0, 0))],
        out_specs=pl.BlockSpec((rows, 128), lambda i: (0, 0)),
        out_shape=jax.ShapeDtypeStruct((rows, 128), F32),
        compiler_params=_params(),
    )(slots)


def _adamw_math(w, g, m, v):
    m2 = ADAM_B1 * m + (1.0 - ADAM_B1) * g
    v2 = ADAM_B2 * v + (1.0 - ADAM_B2) * (g * g)
    m_hat = m2 / (1.0 - ADAM_B1 ** ADAM_STEP)
    v_hat = v2 / (1.0 - ADAM_B2 ** ADAM_STEP)
    delta = -ADAM_LR * (m_hat / (jnp.sqrt(v_hat) + ADAM_EPS) + ADAM_WD * w)
    return delta, m2, v2


def _adamw_big(name, w, g0, g1, m, v):
    _, rows, cols = w.shape
    blk = _flat_blk(rows, cols) // 2

    def body(w_ref, g0_ref, g1_ref, m_ref, v_ref, g_ref, d_ref, m2_ref, v2_ref):
        g = jnp.where(pl.program_id(0) == 0, g0_ref[...], g1_ref[...])
        d, m2, v2 = _adamw_math(w_ref[...], g, m_ref[...], v_ref[...])
        g_ref[...] = g
        d_ref[...] = d
        m2_ref[...] = m2
        v2_ref[...] = v2

    spec = pl.BlockSpec((None, blk, cols), lambda la, i: (la, i, 0))
    return pl.pallas_call(
        body, name=f"adamw_{name}", grid=(N_LAYERS, rows // blk),
        in_specs=[spec, pl.BlockSpec((blk, cols), lambda la, i: (i * (1 - la), 0)),
                  pl.BlockSpec((blk, cols), lambda la, i: (i * la, 0)), spec, spec],
        out_specs=[spec] * 4,
        out_shape=[jax.ShapeDtypeStruct(w.shape, F32)] * 4,
        compiler_params=_params(("parallel", "parallel")),
    )(w, g0, g1, m, v)


def _adamw_small(ws, gs, ms, vs):
    n = len(ws)

    def body(*refs):
        ins, outs = refs[:4 * n], refs[4 * n:]
        for k in range(n):
            d, m2, v2 = _adamw_math(ins[k][...], ins[n + k][...], ins[2 * n + k][...], ins[3 * n + k][...])
            outs[k][...] = d
            outs[n + k][...] = m2
            outs[2 * n + k][...] = v2

    vmem = pl.BlockSpec(memory_space=pltpu.VMEM)
    return pl.pallas_call(
        body, name="adamw_small",
        in_specs=[vmem] * (4 * n), out_specs=[vmem] * (3 * n),
        out_shape=[jax.ShapeDtypeStruct(w.shape, F32) for w in ws] * 3,
        compiler_params=pltpu.CompilerParams(vmem_limit_bytes=V7X_VMEM_LIMIT),
    )(*ws, *gs, *ms, *vs)


SMALL = ("norm1_g", "b_gate", "gmlp_ln_g", "gmlp_ln_b", "w_spatial", "b_spatial", "w_shortconv", "norm2_g",
         "w_ffn_conv", "b_ffn_conv", "final_g")
ALL_WEIGHTS = ("norm1_g", "w_in", "b_gate", "gmlp_ln_g", "gmlp_ln_b", "w_spatial", "b_spatial", "w_shortconv",
               "w_branch", "w_out", "norm2_g", "w_ffn_up", "w_ffn_conv", "b_ffn_conv", "w_ffn_down", "final_g")


def _pack(arrays):
    flat = jnp.concatenate([a.reshape(-1) for a in arrays])
    n = flat.shape[0]
    rows = -(-n // 1024) * 8
    return jnp.pad(flat, (0, rows * 128 - n)).reshape(rows, 128)


def _unpack(packed, like):
    flat = packed.reshape(-1)
    out, off = [], 0
    for a in like:
        out.append(flat[off:off + a.size].reshape(a.shape))
        off += a.size
    return out


def _pad8(w):
    return jnp.pad(w, ((0, 5), (0, 0)))


def kernel(x, norm1_g, w_in, b_gate, gmlp_ln_g, gmlp_ln_b, w_spatial, b_spatial, w_shortconv, w_branch, w_out, norm2_g, w_ffn_up, w_ffn_conv, b_ffn_conv, w_ffn_down, final_g, loss_target, m_norm1_g, m_w_in, m_b_gate, m_gmlp_ln_g, m_gmlp_ln_b, m_w_spatial, m_b_spatial, m_w_shortconv, m_w_branch, m_w_out, m_norm2_g, m_w_ffn_up, m_w_ffn_conv, m_b_ffn_conv, m_w_ffn_down, m_final_g, v_norm1_g, v_w_in, v_b_gate, v_gmlp_ln_g, v_gmlp_ln_b, v_w_spatial, v_b_spatial, v_w_shortconv, v_w_branch, v_w_out, v_norm2_g, v_w_ffn_up, v_w_ffn_conv, v_b_ffn_conv, v_w_ffn_down, v_final_g):
    weights = dict(norm1_g=norm1_g, w_in=w_in, b_gate=b_gate, gmlp_ln_g=gmlp_ln_g, gmlp_ln_b=gmlp_ln_b,
                   w_spatial=w_spatial, b_spatial=b_spatial, w_shortconv=w_shortconv, w_branch=w_branch, w_out=w_out,
                   norm2_g=norm2_g, w_ffn_up=w_ffn_up, w_ffn_conv=w_ffn_conv, b_ffn_conv=b_ffn_conv,
                   w_ffn_down=w_ffn_down, final_g=final_g)
    mom = dict(norm1_g=m_norm1_g, w_in=m_w_in, b_gate=m_b_gate, gmlp_ln_g=m_gmlp_ln_g, gmlp_ln_b=m_gmlp_ln_b,
               w_spatial=m_w_spatial, b_spatial=m_b_spatial, w_shortconv=m_w_shortconv, w_branch=m_w_branch,
               w_out=m_w_out, norm2_g=m_norm2_g, w_ffn_up=m_w_ffn_up, w_ffn_conv=m_w_ffn_conv,
               b_ffn_conv=m_b_ffn_conv, w_ffn_down=m_w_ffn_down, final_g=m_final_g)
    vel = dict(norm1_g=v_norm1_g, w_in=v_w_in, b_gate=v_b_gate, gmlp_ln_g=v_gmlp_ln_g, gmlp_ln_b=v_gmlp_ln_b,
               w_spatial=v_w_spatial, b_spatial=v_b_spatial, w_shortconv=v_w_shortconv, w_branch=v_w_branch,
               w_out=v_w_out, norm2_g=v_norm2_g, w_ffn_up=v_w_ffn_up, w_ffn_conv=v_w_ffn_conv,
               b_ffn_conv=v_b_ffn_conv, w_ffn_down=v_w_ffn_down, final_g=v_final_g)

    cx, cy, cc = _mesh_pos()
    chip = 2 * cx + cy
    core_arr = cc.astype(jnp.int32).reshape(1)
    chip_arr = chip.astype(jnp.int32).reshape(1)
    pos_arr = jnp.stack([chip, cc]).astype(jnp.int32)
    t_len = x.shape[1]
    xs = x.reshape(t_len, D_MODEL)
    target = loss_target.reshape(t_len, D_MODEL)
    pipe = _Pipe()

    full = {}

    def gather(keys):
        slots = [_cast_into_slot(n, la, weights[n].reshape((N_LAYERS,) + BIG[n]), chip_arr) for n, la in keys]

        def then(*bufs):
            full.update(zip(keys, bufs))

        pipe.add(_gather_stage(slots, then))

    mixer_w = ("w_in", "w_branch", "w_out")
    ffn_w = ("w_ffn_up", "w_ffn_down")
    gather([(n, 0) for n in mixer_w])
    pipe.flush()

    idx = jnp.arange(GMLP_BLOCK) // CHUNK
    mask = idx[None, :] <= idx[:, None]
    wm_all = jnp.where(mask[None, None], w_spatial, 0.0)
    wm_bf = wm_all.astype(BF16)
    wmt_bf = jnp.swapaxes(wm_all, -1, -2).astype(BF16)
    bsf = jnp.repeat(jnp.swapaxes(b_spatial, -1, -2), 128, axis=-1)
    wsc_full = lax.dynamic_update_slice(jnp.zeros((N_LAYERS, 3, D_B), F32), w_shortconv, (0, 0, chip * (D_B // 4)))
    wfc_full = lax.dynamic_update_slice(jnp.zeros((N_LAYERS, 3, D_FF), F32), w_ffn_conv, (0, 0, chip * (D_FF // 4)))
    taps = _all_reduce_small("conv_taps", _pack([wsc_full, wfc_full]))
    wsc_full, wfc_full = _unpack(taps * 0.5, [wsc_full, wfc_full])

    def row(a):
        return a.reshape(1, -1)

    def mixer_args(la):
        return (row(norm1_g[la]), row(b_gate[la]), row(gmlp_ln_g[la]), row(gmlp_ln_b[la]))

    def mixer_weights(la):
        return tuple(full[(n, la)] for n in mixer_w)

    def ffn_weights(la):
        return tuple(full[(n, la)] for n in ffn_w)

    saved = []
    h_in = xs
    for la in range(N_LAYERS):
        gather([(n, la) for n in ffn_w])
        *kept, mg, h1, x2 = pipe.carry(lambda st: _mixer_fwd(
            la, h_in, *mixer_args(la), wm_bf[la], bsf[la], _pad8(wsc_full[la]), *mixer_weights(la), st))
        ya, yb = kept[1], kept[2]
        if la + 1 < N_LAYERS:
            gather([(n, la + 1) for n in mixer_w])
        head = (target, row(final_g)) if la == N_LAYERS - 1 else None
        up, silu, dsilu, act, h2, *rest = pipe.carry(lambda st: _ffn_fwd(
            la, x2, row(norm2_g[la]), _pad8(wfc_full[la]), row(b_ffn_conv[la]), *ffn_weights(la), st, head=head))
        saved.append(dict(x=h_in, ya=ya, yb=yb, mixer=[kept[0]] + kept[3:], mg=mg, h1=h1, x2=x2, up=up, silu=silu,
                          dsilu=dsilu, act=act, h2=h2))
        h_in = rest[0]
    dx, dgf8, loss8 = rest

    reduced_big = {}

    def reduce_big(name, la, grad):
        tag = f"{name}_l{la}"

        def after_pair(other):
            psum = _pair_sum(tag, grad, other, core_arr)

            def after_chips(got):
                final = _chip_sum(tag, grad, other, got, pos_arr)
                pipe.add(_pair_fill_stage(final, lambda done: reduced_big.__setitem__((name, la), done)))

            pipe.add(_chip_send_stage(psum, after_chips))

        pipe.add(_pair_send_stage(grad, after_pair))

    small = {n: [None] * N_LAYERS for n in SMALL}
    spread = {}
    for la in reversed(range(N_LAYERS)):
        s = saved[la]
        dx3 = dx
        dx2, dup, dx3b, dg2, dbfc, dwfc = pipe.carry(lambda st: _ffn_bwd(
            la, dx3, s["x2"], s["up"], s["silu"], s["dsilu"], row(norm2_g[la]), _pad8(wfc_full[la]),
            *ffn_weights(la), st))
        g, = pipe.carry(lambda st: _wgrad("w_ffn_down", la, s["act"], dx3b, 704, 1024, 1408, 1024, st))
        reduce_big("w_ffn_down", la, g)
        g, = pipe.carry(lambda st: _wgrad("w_ffn_up", la, s["h2"], dup, 1024, 1408, 1024, 1408, st))
        reduce_big("w_ffn_up", la, g)
        run = pipe.carry if la > 0 else (lambda call: call([])[0])
        dxl, dz, da, db, dx2b, dg1, dbg, dlng, dlnb, dwm, dbsf, dwsc = run(lambda st: _mixer_bwd(
            la, dx2, s["x"], *s["mixer"], row(norm1_g[la]), row(gmlp_ln_g[la]), row(gmlp_ln_b[la]), wmt_bf[la],
            _pad8(wsc_full[la]), *mixer_weights(la), st))
        small["norm1_g"][la] = dg1.sum(0)
        small["b_gate"][la] = dbg.sum(0)
        small["gmlp_ln_g"][la] = dlng.sum(0)
        small["gmlp_ln_b"][la] = dlnb.sum(0)
        small["w_spatial"][la] = jnp.where(mask[None], dwm, 0.0)
        small["b_spatial"][la] = dbsf.reshape(128, A_HEADS, 128).sum(-1).T
        small["w_shortconv"][la] = dwsc.sum(1)
        small["norm2_g"][la] = dg2.sum(0)
        small["w_ffn_conv"][la] = dwfc.sum(1)
        small["b_ffn_conv"][la] = dbfc.sum(0)
        if la == 0:
            small_local = ([jnp.stack(small[n]) for n in SMALL[:-1]]
                           + [dgf8.sum(0), 0.5 * loss8.sum().reshape(1) / D_MODEL])
            mine = _pack(small_local)

            def after_swap(other, mine=mine):
                pair = _sum_slots("small_pair", jnp.stack([mine, other]))
                pipe.add(_chip_spread_stage(pair, lambda slots: spread.__setitem__("slots", slots)))

            pipe.add(_pair_swap_stage(mine, after_swap))
        for part, tag in enumerate(("w_in_a", "w_in_b")):
            g, = pipe.carry(lambda st: _wgrad(tag, la, s["h1"], dz, 512, 1152, 512, 1152, st, a_first=part))
            reduce_big(tag, la, g)
        g, = pipe.carry(lambda st: _wgrad("w_out", la, s["mg"], dx2b, 256, 1024, 1024, 1024, st), long=False)
        reduce_big("w_out", la, g)
        g, = pipe.carry(lambda st: _wgrad_branch(la, s["ya"], da, s["yb"], db, st), long=False)
        reduce_big("w_branch", la, g)
        dx = dxl
    grad_x = dx.reshape(x.shape)
    pipe.flush()

    for la in range(N_LAYERS):
        reduced_big[("w_in", la)] = jnp.concatenate([reduced_big[("w_in_a", la)], reduced_big[("w_in_b", la)]], axis=0)
    reduced = _unpack(_sum_slots("small_grads", spread["slots"]), small_local)
    loss = reduced[-1].reshape(())
    grads = dict(zip(SMALL, reduced[:-1]))
    grads["w_shortconv"] = lax.dynamic_slice(grads["w_shortconv"], (0, 0, chip * (D_B // 4)), (N_LAYERS, 3, D_B // 4))
    grads["w_ffn_conv"] = lax.dynamic_slice(grads["w_ffn_conv"], (0, 0, chip * (D_FF // 4)), (N_LAYERS, 3, D_FF // 4))

    delta, new_m, new_v = {}, {}, {}
    for n in BIG_NAMES:
        shape3 = (N_LAYERS,) + BIG[n]
        res = _adamw_big(n, weights[n].reshape(shape3), reduced_big[(n, 0)], reduced_big[(n, 1)],
                         mom[n].reshape(shape3), vel[n].reshape(shape3))
        grads[n], delta[n], new_m[n], new_v[n] = (a.reshape(weights[n].shape) for a in res)
    res = _adamw_small(*[[src[n].reshape(-1, src[n].shape[-1]) for n in SMALL] for src in (weights, grads, mom, vel)])
    for k, n in enumerate(SMALL):
        delta[n], new_m[n], new_v[n] = (res[j * len(SMALL) + k].reshape(weights[n].shape) for j in range(3))

    return (loss, grad_x, *[grads[n] for n in ALL_WEIGHTS], *[delta[n] for n in ALL_WEIGHTS],
            *[new_m[n] for n in ALL_WEIGHTS], *[new_v[n] for n in ALL_WEIGHTS])
```

```python
import jax
import jax.numpy as jnp
from jax import lax
from jax.experimental import pallas as pl
from jax.experimental.pallas import tpu as pltpu

F32 = jnp.float32
BF16 = jnp.bfloat16
MESH = pl.DeviceIdType.MESH
ANY = pl.BlockSpec(memory_space=pl.ANY)

D_MODEL = 1024
D_A = 512
D_B = 512
D_IN = 4608
D_FF = 2816
GMLP_BLOCK = 128
CHUNK = 64
A_HEADS = 4
N_LAYERS = 2
N_CHIPS = 4
RMS_EPS = 1e-6
LN_EPS = 1e-5
ADAM_LR = 0.001
ADAM_B1 = 0.9
ADAM_B2 = 0.999
ADAM_EPS = 1e-08
ADAM_WD = 0.01
ADAM_STEP = 10

C_U, C_V, C_BG, C_CG, C_HB, C_GA, C_GB = 0, 512, 1024, 1536, 2048, 2560, 3584

V7X_VMEM_LIMIT = 60 * 1024 * 1024
TM_MIX = 256
TM_FFN = 256
TK_WGRAD = 2048
SLOW_COPY_BYTES = 640 * 1024
FF_CHUNKS = ((0, 768), (768, 1536), (1536, 2304), (2304, 2816))
GELU_C0 = 0.7978845608028654
GELU_C1 = 0.044715

BIG = {
    "w_in": (1024, 1152),
    "w_branch": (1024, 256),
    "w_out": (256, 1024),
    "w_ffn_up": (1024, 1408),
    "w_ffn_down": (704, 1024),
}
BIG_NAMES = tuple(BIG)


def _params(sem=("arbitrary",), vmem=V7X_VMEM_LIMIT):
    return pltpu.CompilerParams(dimension_semantics=sem, vmem_limit_bytes=vmem)


def _gelu(x):
    x2 = x * x
    t = jnp.tanh(GELU_C0 * x * (1.0 + GELU_C1 * x2))
    return 0.5 * x * (1.0 + t), t


def _gelu_grad(x, t):
    return 0.5 * (1.0 + t) + 0.5 * x * (1.0 - t * t) * GELU_C0 * (1.0 + 3.0 * GELU_C1 * x * x)


def _colsum8(v):
    r, n = v.shape
    return v.reshape(r // 8, 8, n).sum(axis=0)


def _dot(a, b):
    return jnp.dot(a, b, preferred_element_type=F32)


def _dot_nt(a, b):
    return lax.dot_general(a, b, (((1,), (1,)), ((), ())), preferred_element_type=F32)


def _dot_tn(a, b):
    return lax.dot_general(a, b, (((0,), (0,)), ((), ())), preferred_element_type=F32)


def _shift_down(v, carry, n):
    rows = lax.broadcasted_iota(jnp.int32, (8, v.shape[1]), 0)
    out = pltpu.roll(v, n, 0)
    head = out[0:8, :]
    for r in range(n):
        head = jnp.where(rows == r, carry[8 - n + r:8 - n + r + 1, :], head)
    return jnp.concatenate([head, out[8:, :]], axis=0)


def _shift_up(v, carry, n):
    tm = v.shape[0]
    rows = lax.broadcasted_iota(jnp.int32, (8, v.shape[1]), 0)
    out = pltpu.roll(v, tm - n, 0)
    tail = out[tm - 8:tm, :]
    for r in range(n):
        tail = jnp.where(rows == 8 - n + r, carry[r:r + 1, :], tail)
    return jnp.concatenate([out[0:tm - 8, :], tail], axis=0)


def _sigmoid(x):
    return 0.5 * jnp.tanh(0.5 * x) + 0.5


def _start_all(copies):
    for cp in copies:
        cp.start()


def _wait_all(copies):
    for cp in copies:
        cp.wait()


def _load_col_sharded(src, dst, sems, first):
    cs = src.shape[-1]
    return [pltpu.make_async_copy(src.at[k], dst.at[:, k * cs:(k + 1) * cs], sems.at[first + k])
            for k in range(N_CHIPS)]


def _load_row_sharded(src, dst, sems, first):
    rs = src.shape[-2]
    return [pltpu.make_async_copy(src.at[k], dst.at[k * rs:(k + 1) * rs, :], sems.at[first + k])
            for k in range(N_CHIPS)]


def _load_branch(src, dst, sems, first):
    return [pltpu.make_async_copy(src.at[k, pl.ds(m * D_A, D_A), :], dst.at[m, :, k * 256:(k + 1) * 256],
                                  sems.at[first + 2 * k + m])
            for k in range(N_CHIPS) for m in range(2)]


def _row_spec(tm, n, rev=None):
    if rev is None:
        return pl.BlockSpec((tm, n), lambda i: (i, 0))
    return pl.BlockSpec((tm, n), lambda i: (rev - 1 - i, 0))


def _const_spec(shape):
    nd = len(shape)
    return pl.BlockSpec(shape, lambda i: (0,) * nd)


def _mesh_pos():
    return lax.axis_index("x"), lax.axis_index("y"), lax.axis_index("c")


def _other_chips(x, y):
    return [(1 - x, y, 2 * (1 - x) + y), (x, 1 - y, 2 * x + (1 - y)), (1 - x, 1 - y, 2 * (1 - x) + (1 - y))]


def _remote(src, dst, ssem, rsem, to):
    return pltpu.make_async_remote_copy(src_ref=src, dst_ref=dst, send_sem=ssem, recv_sem=rsem, device_id=to,
                                        device_id_type=MESH)


def _half(ref, which, h):
    start = pl.multiple_of(which * h, 8)
    if len(ref.shape) == 2:
        return ref.at[pl.ds(start, h), :]
    return ref.at[:, pl.ds(start, h), :]


class _Stage:
    def __init__(self, ins=(), inouts=(), outs=(), n_sems=0, start=None, mid=None, finish=None, then=None, slow=False):
        self.ins, self.inouts, self.outs = list(ins), list(inouts), list(outs)
        self.n_sems, self.start, self.mid, self.finish, self.then = n_sems, start, mid, finish, then
        self.slow = slow


def _gather_stage(bufs, then):
    n = len(bufs)

    def copies(io, sem):
        x, y, c = _mesh_pos()
        me = 2 * x + y
        ici, fwd, got = [], [], []
        for w in range(n):
            h = io[w].shape[1] // 2
            for j, (px, py, pk) in enumerate(_other_chips(x, y)):
                mine = _half(io[w].at[me], c, h)
                theirs = _half(io[w].at[pk], c, h)
                ici.append(_remote(mine, mine, sem(12 * w + j), sem(12 * w + 3 + j), (px, py, c)))
                got.append(_remote(theirs, theirs, sem(12 * w + j), sem(12 * w + 3 + j), (px, py, c)))
                fwd.append(_remote(theirs, theirs, sem(12 * w + 6 + j), sem(12 * w + 9 + j), (x, y, 1 - c)))
        return ici, got, fwd

    def start(ins, io, outs, sem):
        _start_all(copies(io, sem)[0])

    def mid(ins, io, outs, sem):
        _, got, fwd = copies(io, sem)
        for g, f in zip(got, fwd):
            g.wait_recv()
            f.start()

    def finish(ins, io, outs, sem):
        x, y, c = _mesh_pos()
        ici, _, fwd = copies(io, sem)
        for w in range(n):
            h = io[w].shape[1] // 2
            for j, (px, py, pk) in enumerate(_other_chips(x, y)):
                other = _half(io[w].at[pk], 1 - c, h)
                _remote(other, other, sem(12 * w + 6 + j), sem(12 * w + 9 + j), (x, y, 1 - c)).wait_recv()
        for cp in ici + fwd:
            cp.wait_send()

    return _Stage(inouts=bufs, n_sems=12 * n, start=start, mid=mid, finish=finish, then=then)


def _pair_send_stage(grad, then):
    h = grad.shape[1] // 2

    def copy(ins, outs, sem):
        x, y, c = _mesh_pos()
        return _remote(_half(ins[0], 1 - c, h), outs[0], sem(0), sem(1), (x, y, 1 - c))

    return _Stage(ins=[grad], outs=[jax.ShapeDtypeStruct((N_CHIPS, h, grad.shape[2]), F32)], n_sems=2,
                  start=lambda ins, io, outs, sem: copy(ins, outs, sem).start(),
                  finish=lambda ins, io, outs, sem: copy(ins, outs, sem).wait(), then=then)


def _chip_send_stage(psum, then):
    def copies(ins, outs, sem):
        x, y, c = _mesh_pos()
        return [_remote(ins[0].at[pk], outs[0].at[j], sem(j), sem(3 + j), (px, py, c))
                for j, (px, py, pk) in enumerate(_other_chips(x, y))]

    return _Stage(ins=[psum], outs=[jax.ShapeDtypeStruct((3,) + psum.shape[1:], BF16)], n_sems=6,
                  start=lambda ins, io, outs, sem: _start_all(copies(ins, outs, sem)),
                  finish=lambda ins, io, outs, sem: _wait_all(copies(ins, outs, sem)), then=then,
                  slow=psum.shape[1] * psum.shape[2] * 2 > SLOW_COPY_BYTES)


def _pair_fill_stage(final, then):
    h = final.shape[0] // 2

    def copy(io, sem):
        x, y, c = _mesh_pos()
        mine = _half(io[0], c, h)
        return _remote(mine, mine, sem(0), sem(1), (x, y, 1 - c))

    return _Stage(inouts=[final], n_sems=2,
                  start=lambda ins, io, outs, sem: copy(io, sem).start(),
                  finish=lambda ins, io, outs, sem: copy(io, sem).wait(), then=then)


def _pair_swap_stage(packed, then):
    def copy(ins, outs, sem):
        x, y, c = _mesh_pos()
        return _remote(ins[0], outs[0], sem(0), sem(1), (x, y, 1 - c))

    return _Stage(ins=[packed], outs=[jax.ShapeDtypeStruct(packed.shape, F32)], n_sems=2,
                  start=lambda ins, io, outs, sem: copy(ins, outs, sem).start(),
                  finish=lambda ins, io, outs, sem: copy(ins, outs, sem).wait(), then=then)


def _chip_spread_stage(psum, then):
    def copies(ins, outs, sem):
        x, y, c = _mesh_pos()
        me = 2 * x + y
        cps = [_remote(ins[0], outs[0].at[me], sem(j), sem(3 + j), (px, py, c))
               for j, (px, py, pk) in enumerate(_other_chips(x, y))]
        return cps, pltpu.make_async_copy(ins[0], outs[0].at[me], sem(6))

    def start(ins, io, outs, sem):
        cps, own = copies(ins, outs, sem)
        own.start()
        _start_all(cps)

    def finish(ins, io, outs, sem):
        cps, own = copies(ins, outs, sem)
        _wait_all(cps)
        own.wait()

    return _Stage(ins=[psum], outs=[jax.ShapeDtypeStruct((N_CHIPS,) + psum.shape, F32)], n_sems=7,
                  start=start, finish=finish, then=then)


def _staged_call(core, *, name, grid, in_specs, out_specs, out_shape, scratch_shapes, args, stages):
    n_in, n_out, n_scr = len(args), len(out_shape), len(scratch_shapes)
    s_args, s_outs, aliases, layout = [], [], {}, []
    n_sems = 0
    for st in stages:
        i0, o0 = len(s_args), len(s_outs)
        s_args += st.ins + st.inouts
        for q in range(len(st.inouts)):
            aliases[n_in + i0 + len(st.ins) + q] = n_out + o0 + q
        s_outs += [jax.ShapeDtypeStruct(a.shape, a.dtype) for a in st.inouts] + st.outs
        layout.append((i0, o0, n_sems))
        n_sems += st.n_sems
    steps = 1
    for g in grid:
        steps *= g

    def body(*refs):
        own_in = refs[:n_in]
        s_in = refs[n_in:n_in + len(s_args)]
        rest = refs[n_in + len(s_args):]
        own_out = rest[:n_out]
        s_out = rest[n_out:n_out + len(s_outs)]
        scr = rest[n_out + len(s_outs):]

        def run(which):
            for st, (i0, o0, s0) in zip(stages, layout):
                fn = getattr(st, which)
                if fn is not None:
                    fn(s_in[i0:i0 + len(st.ins)], s_out[o0:o0 + len(st.inouts)],
                       s_out[o0 + len(st.inouts):o0 + len(st.inouts) + len(st.outs)],
                       lambda k, s0=s0: scr[n_scr].at[s0 + k])

        if not stages:
            core(*own_in, *own_out, *scr[:n_scr])
            return
        step = 0
        for d, g in enumerate(grid):
            step = step * g + pl.program_id(d)
        if steps == 1:
            run("start")
            core(*own_in, *own_out, *scr[:n_scr])
            run("mid")
            run("finish")
            return
        pl.when(step == 0)(lambda: run("start"))
        core(*own_in, *own_out, *scr[:n_scr])
        pl.when(step == (3 * steps) // 4)(lambda: run("mid"))
        pl.when(step == steps - 1)(lambda: run("finish"))

    sem = ("arbitrary",) * len(grid) if stages else ("parallel",) * max(len(grid) - 1, 0) + ("arbitrary",) * min(len(grid), 1)
    res = pl.pallas_call(
        body, name=name, grid=grid,
        in_specs=list(in_specs) + [ANY] * len(s_args),
        out_specs=list(out_specs) + [ANY] * len(s_outs),
        out_shape=list(out_shape) + s_outs,
        input_output_aliases=aliases,
        scratch_shapes=list(scratch_shapes) + ([pltpu.SemaphoreType.DMA((n_sems,))] if stages else []),
        compiler_params=_params(sem) if grid else pltpu.CompilerParams(vmem_limit_bytes=V7X_VMEM_LIMIT),
    )(*args, *s_args)
    return list(res[:n_out]), list(res[n_out:])


class _Pipe:
    def __init__(self):
        self.ready = []
        self.flushes = 0

    def add(self, stage):
        self.ready.append(stage)

    def carry(self, call, long=True):
        stages = [st for st in self.ready if long or not st.slow]
        self.ready = [st for st in self.ready if not (long or not st.slow)]
        own, outs = call(stages)
        k = 0
        for st in stages:
            n = len(st.inouts) + len(st.outs)
            st.then(*outs[k:k + n])
            k += n
        return own

    def flush(self):
        while self.ready:
            self.flushes += 1
            self.carry(lambda stages: _staged_call(
                lambda *refs: None, name=f"comm_tail_{self.flushes}", grid=(), in_specs=[], out_specs=[], out_shape=[],
                scratch_shapes=[], args=[], stages=stages))


def _mixer_fwd(layer, x, g1, bgate, lng, lnb, wm, bsf, wsc, win_g, wb_g, wout_g, stages):
    t_len = x.shape[0]
    tm = min(TM_MIX, t_len)
    nt = t_len // tm
    nb = tm // GMLP_BLOCK

    def core(x_ref, x_late_ref, g1_ref, bgate_ref, lng_ref, lnb_ref, wm_ref, bsf_ref, wsc_ref, win_hbm, wb_hbm, wout_hbm,
             zc_ref, ya_ref, yb_ref, q_ref, sa_ref, ca_ref, sb_ref, cb_ref, ug_ref, fu_ref, xh_ref, cv_ref,
             mg_ref, h_ref, x2_ref,
             win_v, wb_v, wout_v, carry, vn_s, f_s, z_s, sems):
        i = pl.program_id(0)

        @pl.when(i == 0)
        def _():
            cps = (_load_col_sharded(win_hbm, win_v, sems, 0) + _load_branch(wb_hbm, wb_v, sems, 4)
                   + _load_row_sharded(wout_hbm, wout_v, sems, 12))
            _start_all(cps)
            carry[...] = jnp.zeros_like(carry)
            z_s[...] = jnp.zeros_like(z_s)
            _wait_all(cps)

        xv = x_ref[...]
        r = lax.rsqrt(jnp.mean(xv * xv, axis=-1, keepdims=True) + RMS_EPS)
        h_ref[...] = (xv * r * g1_ref[...]).astype(BF16)

        def zcols(c0, n, keep=None):
            zv = z_s[:, c0:c0 + n]
            z_s[:, c0:c0 + n] = _dot(h_ref[...], win_v[:, c0:c0 + n])
            if keep is not None:
                zc_ref[:, keep * D_B:(keep + 1) * D_B] = zv.astype(BF16)
            return zv

        v = zcols(C_V, D_A)
        vg, tv = _gelu(v)
        mu = jnp.mean(vg, axis=-1, keepdims=True)
        vc = vg - mu
        rstd = lax.rsqrt(jnp.mean(vc * vc, axis=-1, keepdims=True) + LN_EPS)
        xh = vc * rstd
        xh_ref[...] = xh.astype(BF16)
        cv_ref[...] = (rstd * _gelu_grad(v, tv)).astype(BF16)
        vn_s[...] = (xh * lng_ref[...] + lnb_ref[...]).astype(BF16)
        for hd in range(A_HEADS):
            cols = slice(hd * 128, (hd + 1) * 128)
            vcat = jnp.concatenate([vn_s[b * 128:(b + 1) * 128, cols] for b in range(nb)], axis=1)
            fcat = _dot(wm_ref[hd], vcat)
            for b in range(nb):
                f_s[b * 128:(b + 1) * 128, cols] = fcat[:, b * 128:(b + 1) * 128]
        u = zcols(C_U, D_A)
        ug, tu = _gelu(u)
        ug_ref[...] = ug.astype(BF16)
        fb = f_s[...] + jnp.concatenate([bsf_ref[...]] * nb, axis=0)
        fu_ref[...] = (fb * _gelu_grad(u, tu)).astype(BF16)
        ya_ref[...] = (ug * fb).astype(BF16)

        p = zcols(C_CG, D_B, keep=1) * zcols(C_HB, D_B, keep=2)
        cr = carry[...]
        q = wsc_ref[0:1, :] * _shift_down(p, cr, 2) + wsc_ref[1:2, :] * _shift_down(p, cr, 1) + wsc_ref[2:3, :] * p
        carry[...] = p[tm - 8:tm, :]
        q_ref[...] = q.astype(BF16)
        yb_ref[...] = (zcols(C_BG, D_B, keep=0) * q).astype(BF16)

        av = _dot(ya_ref[...], wb_v[0])
        sa = _sigmoid(zcols(C_GA, D_MODEL) + bgate_ref[:, 0:D_MODEL])
        sa_ref[...] = sa.astype(BF16)
        mg = sa * av
        ca_ref[...] = (mg * (1.0 - sa)).astype(BF16)
        bv = _dot(yb_ref[...], wb_v[1])
        sb = _sigmoid(zcols(C_GB, D_MODEL) + bgate_ref[:, D_MODEL:2 * D_MODEL])
        sb_ref[...] = sb.astype(BF16)
        mb = sb * bv
        cb_ref[...] = (mb * (1.0 - sb)).astype(BF16)
        mg_ref[...] = (mg + mb).astype(BF16)
        x2_ref[...] = x_late_ref[...] + _dot(mg_ref[...], wout_v[...])

    def tile(n, lag):
        return pl.BlockSpec((tm, n), lambda i: (jnp.clip(i - lag, 0, nt - 1), 0))

    outs = [
        jax.ShapeDtypeStruct((t_len, 3 * D_B), BF16),
        jax.ShapeDtypeStruct((t_len, D_A), BF16),
        jax.ShapeDtypeStruct((t_len, D_B), BF16),
        jax.ShapeDtypeStruct((t_len, D_B), BF16),
        jax.ShapeDtypeStruct((t_len, D_MODEL), BF16),
        jax.ShapeDtypeStruct((t_len, D_MODEL), BF16),
        jax.ShapeDtypeStruct((t_len, D_MODEL), BF16),
        jax.ShapeDtypeStruct((t_len, D_MODEL), BF16),
        jax.ShapeDtypeStruct((t_len, D_A), BF16),
        jax.ShapeDtypeStruct((t_len, D_A), BF16),
        jax.ShapeDtypeStruct((t_len, D_A), BF16),
        jax.ShapeDtypeStruct((t_len, D_A), BF16),
        jax.ShapeDtypeStruct((t_len, D_MODEL), BF16),
        jax.ShapeDtypeStruct((t_len, D_MODEL), BF16),
        jax.ShapeDtypeStruct((t_len, D_MODEL), F32),
    ]
    return _staged_call(
        core, name=f"mixer_fwd_l{layer}", grid=(nt + 1,),
        in_specs=[tile(D_MODEL, 0), tile(D_MODEL, 1), _const_spec((1, D_MODEL)), _const_spec((1, 2 * D_MODEL)),
                  _const_spec((1, D_A)), _const_spec((1, D_A)), _const_spec((A_HEADS, 128, 128)),
                  _const_spec((128, D_A)), _const_spec((8, D_B)), ANY, ANY, ANY],
        out_specs=[tile(o.shape[1], 0 if k == len(outs) - 2 else 1) for k, o in enumerate(outs)],
        out_shape=outs,
        scratch_shapes=[pltpu.VMEM((D_MODEL, D_IN), BF16), pltpu.VMEM((2, D_A, D_MODEL), BF16),
                        pltpu.VMEM((D_MODEL, D_MODEL), BF16), pltpu.VMEM((8, D_B), F32),
                        pltpu.VMEM((tm, D_A), BF16), pltpu.VMEM((tm, D_A), F32), pltpu.VMEM((tm, D_IN), F32),
                        pltpu.SemaphoreType.DMA((16,))],
        args=[x, x, g1, bgate, lng, lnb, wm, bsf, wsc, win_g, wb_g, wout_g], stages=stages)


def _ffn_fwd(layer, x2, g2, wfc, bfc, wup_g, wdown_g, stages, head=None):
    t_len = x2.shape[0]
    tm = min(TM_FFN, t_len)
    nt = t_len // tm

    def core(*refs):
        if head is None:
            (x_ref, g2_ref, wfc_ref, bfc_ref, wup_hbm, wdown_hbm, up_ref, silu_ref, dsilu_ref, act_ref, h_ref, x3_ref,
             wup_v, wdown_v, carry, sems) = refs
        else:
            (x_ref, g2_ref, wfc_ref, bfc_ref, t_ref, gf_ref, wup_hbm, wdown_hbm, up_ref, silu_ref, dsilu_ref, act_ref,
             h_ref, dx_ref, dgf_ref, loss_ref, wup_v, wdown_v, carry, sems) = refs
        i = pl.program_id(0)

        @pl.when(i == 0)
        def _():
            cps = _load_col_sharded(wup_hbm, wup_v, sems, 0) + _load_row_sharded(wdown_hbm, wdown_v, sems, 4)
            _start_all(cps)
            carry[...] = jnp.zeros_like(carry)
            if head is not None:
                dgf_ref[...] = jnp.zeros_like(dgf_ref)
                loss_ref[...] = jnp.zeros_like(loss_ref)
            _wait_all(cps)

        xv = x_ref[...]
        r = lax.rsqrt(jnp.mean(xv * xv, axis=-1, keepdims=True) + RMS_EPS)
        h_ref[...] = (xv * r * g2_ref[...]).astype(BF16)
        gate = _dot(h_ref[...], wup_v[:, 0:D_FF])
        up_ref[:, 0:D_FF] = gate.astype(BF16)
        cr = carry[...]
        gc = (wfc_ref[0:1, :] * _shift_down(gate, cr, 2) + wfc_ref[1:2, :] * _shift_down(gate, cr, 1)
              + wfc_ref[2:3, :] * gate + bfc_ref[...])
        carry[...] = gate[tm - 8:tm, :]
        sg = _sigmoid(gc)
        silu = gc * sg
        silu_ref[...] = silu.astype(BF16)
        dsilu_ref[...] = (sg + silu * (1.0 - sg)).astype(BF16)
        val = _dot(h_ref[...], wup_v[:, D_FF:2 * D_FF])
        up_ref[:, D_FF:2 * D_FF] = val.astype(BF16)
        act_ref[...] = (silu * val).astype(BF16)
        x3 = x_ref[...] + _dot(act_ref[...], wdown_v[...])
        if head is None:
            x3_ref[...] = x3
        else:
            r3 = lax.rsqrt(jnp.mean(x3 * x3, axis=-1, keepdims=True) + RMS_EPS)
            xh = x3 * r3
            err = xh * gf_ref[...] - t_ref[...]
            loss_ref[...] += _colsum8(err * err)
            dy = err * (1.0 / D_MODEL)
            dgf_ref[...] += _colsum8(dy * xh)
            dxh = dy * gf_ref[...]
            dx_ref[...] = r3 * (dxh - xh * jnp.mean(dxh * xh, axis=-1, keepdims=True))

    outs = [
        jax.ShapeDtypeStruct((t_len, 2 * D_FF), BF16),
        jax.ShapeDtypeStruct((t_len, D_FF), BF16),
        jax.ShapeDtypeStruct((t_len, D_FF), BF16),
        jax.ShapeDtypeStruct((t_len, D_FF), BF16),
        jax.ShapeDtypeStruct((t_len, D_MODEL), BF16),
        jax.ShapeDtypeStruct((t_len, D_MODEL), F32),
    ]
    in_specs = [_row_spec(tm, D_MODEL), _const_spec((1, D_MODEL)), _const_spec((8, D_FF)), _const_spec((1, D_FF))]
    out_specs = [_row_spec(tm, o.shape[1]) for o in outs]
    args = [x2, g2, wfc, bfc]
    if head is not None:
        in_specs += [_row_spec(tm, D_MODEL), _const_spec((1, D_MODEL))]
        args += list(head)
        outs += [jax.ShapeDtypeStruct((8, D_MODEL), F32)] * 2
        out_specs += [_const_spec((8, D_MODEL))] * 2
    return _staged_call(
        core, name=f"ffn_fwd_l{layer}", grid=(nt,),
        in_specs=in_specs + [ANY, ANY], out_specs=out_specs, out_shape=outs,
        scratch_shapes=[pltpu.VMEM((D_MODEL, 2 * D_FF), BF16), pltpu.VMEM((D_FF, D_MODEL), BF16),
                        pltpu.VMEM((8, D_FF), F32), pltpu.SemaphoreType.DMA((8,))],
        args=args + [wup_g, wdown_g], stages=stages)


def _ffn_bwd(layer, dx3, x2, up, silu, dsilu, g2, wfc, wup_g, wdown_g, stages):
    t_len = x2.shape[0]
    tm = min(TM_FFN, t_len)
    nt = t_len // tm

    def core(dx3_ref, dx3_late_ref, x_ref, up_ref, silu_ref, dsilu_ref, g2_ref, wfc_ref, wup_hbm, wdown_hbm,
             dx2_ref, dup_ref, dx3b_ref, dg2_ref, dbfc_ref, dwfc_ref,
             wup_v, wdown_v, carry, da_s, dup_s, sems):
        i = pl.program_id(0)

        @pl.when(i == 0)
        def _():
            cps = _load_col_sharded(wup_hbm, wup_v, sems, 0) + _load_row_sharded(wdown_hbm, wdown_v, sems, 4)
            _start_all(cps)
            for ref in (carry, da_s, dup_s, dg2_ref, dbfc_ref, dwfc_ref):
                ref[...] = jnp.zeros_like(ref)
            _wait_all(cps)

        live = (i <= nt).astype(F32)
        dx3b_ref[...] = dx3_ref[...].astype(BF16)
        dh = jnp.zeros((tm, D_MODEL), F32)
        for c0, c1 in FF_CHUNKS:
            v0, v1 = D_FF + c0, D_FF + c1
            dh = dh + _dot_nt(dup_s[:, c0:c1], wup_v[:, c0:c1]) + _dot_nt(dup_s[:, v0:v1], wup_v[:, v0:v1])
            da = da_s[:, c0:c1]
            dval = (da * silu_ref[:, c0:c1].astype(F32)).astype(BF16)
            dup_ref[:, v0:v1] = dval
            dup_s[:, v0:v1] = dval
            dgc = da * up_ref[:, v0:v1].astype(F32) * dsilu_ref[:, c0:c1].astype(F32)
            cr = carry[:, c0:c1]
            dgc1 = _shift_up(dgc, cr, 1)
            dgc2 = _shift_up(dgc, cr, 2)
            carry[:, c0:c1] = jnp.where(i < nt, dgc[0:8, :], cr)
            gate = up_ref[:, c0:c1].astype(F32)
            dbfc_ref[:, c0:c1] += live * _colsum8(dgc)
            dwfc_ref[0, :, c0:c1] += live * _colsum8(dgc2 * gate)
            dwfc_ref[1, :, c0:c1] += live * _colsum8(dgc1 * gate)
            dwfc_ref[2, :, c0:c1] += live * _colsum8(dgc * gate)
            dgate = (wfc_ref[2:3, c0:c1] * dgc + wfc_ref[1:2, c0:c1] * dgc1 + wfc_ref[0:1, c0:c1] * dgc2).astype(BF16)
            dup_ref[:, c0:c1] = dgate
            dup_s[:, c0:c1] = dgate
            da_s[:, c0:c1] = _dot_nt(dx3b_ref[...], wdown_v[c0:c1, :])
        xv = x_ref[...]
        r = lax.rsqrt(jnp.mean(xv * xv, axis=-1, keepdims=True) + RMS_EPS)
        xh = xv * r
        dg2_ref[...] += _colsum8(dh * xh)
        dxh = dh * g2_ref[...]
        dx2_ref[...] = dx3_late_ref[...] + r * (dxh - xh * jnp.mean(dxh * xh, axis=-1, keepdims=True))

    def tile(n, lag):
        return pl.BlockSpec((tm, n), lambda i: (nt - 1 - jnp.clip(i - lag, 0, nt - 1), 0))

    outs = [
        jax.ShapeDtypeStruct((t_len, D_MODEL), F32),
        jax.ShapeDtypeStruct((t_len, 2 * D_FF), BF16),
        jax.ShapeDtypeStruct((t_len, D_MODEL), BF16),
        jax.ShapeDtypeStruct((8, D_MODEL), F32),
        jax.ShapeDtypeStruct((8, D_FF), F32),
        jax.ShapeDtypeStruct((3, 8, D_FF), F32),
    ]
    return _staged_call(
        core, name=f"ffn_bwd_l{layer}", grid=(nt + 2,),
        in_specs=[tile(D_MODEL, 0), tile(D_MODEL, 2), tile(D_MODEL, 2), tile(2 * D_FF, 1), tile(D_FF, 1), tile(D_FF, 1),
                  _const_spec((1, D_MODEL)), _const_spec((8, D_FF)), ANY, ANY],
        out_specs=[tile(D_MODEL, 2), tile(2 * D_FF, 1), tile(D_MODEL, 0),
                   _const_spec((8, D_MODEL)), _const_spec((8, D_FF)), _const_spec((3, 8, D_FF))],
        out_shape=outs,
        scratch_shapes=[pltpu.VMEM((D_MODEL, 2 * D_FF), BF16), pltpu.VMEM((D_FF, D_MODEL), BF16),
                        pltpu.VMEM((8, D_FF), F32), pltpu.VMEM((tm, D_FF), F32), pltpu.VMEM((tm, 2 * D_FF), BF16),
                        pltpu.SemaphoreType.DMA((8,))],
        args=[dx3, dx3, x2, up, silu, dsilu, g2, wfc, wup_g, wdown_g], stages=stages)


def _mixer_bwd(layer, dx2, x, zc, qs, sa, ca, sb, cb, ug, fu, xhs, cv, g1, lng, lnb, wmt, wsc, win_g, wb_g, wout_g,
               stages):
    t_len = x.shape[0]
    tm = min(TM_MIX, t_len)
    nt = t_len // tm
    nb = tm // GMLP_BLOCK

    def core(dx2_ref, x_ref, zc_ref, q_ref, sa_ref, ca_ref, sb_ref, cb_ref, ug_ref, fu_ref, xh_ref, cv_ref,
             g1_ref, lng_ref, lnb_ref, wmt_ref, wsc_ref, win_hbm, wb_hbm, wout_hbm,
             dx_ref, dz_ref, da_ref, db_ref, dx2b_ref, dg1_ref, dbgate_ref, dlng_ref, dlnb_ref, dwm_ref, dbsf_ref, dwsc_ref,
             win_v, wb_v, wout_v, carry, vn_s, df_s, dvn_s, sems):
        i = pl.program_id(0)

        @pl.when(i == 0)
        def _():
            cps = (_load_col_sharded(win_hbm, win_v, sems, 0) + _load_branch(wb_hbm, wb_v, sems, 4)
                   + _load_row_sharded(wout_hbm, wout_v, sems, 12))
            _start_all(cps)
            for ref in (carry, dg1_ref, dbgate_ref, dlng_ref, dlnb_ref, dwm_ref, dbsf_ref, dwsc_ref):
                ref[...] = jnp.zeros_like(ref)
            _wait_all(cps)

        def kept(k):
            return zc_ref[:, k * D_B:(k + 1) * D_B].astype(F32)

        def dz_cols(c0, n, val):
            dz_ref[:, c0:c0 + n] = val.astype(BF16)
            return _dot_nt(dz_ref[:, c0:c0 + n], win_v[:, c0:c0 + n])

        dx2b_ref[...] = dx2_ref[...].astype(BF16)
        dm = _dot_nt(dx2b_ref[...], wout_v[...])
        da_ref[...] = (dm * sa_ref[...].astype(F32)).astype(BF16)
        dga = dm * ca_ref[...].astype(F32)
        dh = dz_cols(C_GA, D_MODEL, dga)
        dbgate_ref[:, 0:D_MODEL] += _colsum8(dga)
        dya = _dot_nt(da_ref[...], wb_v[0])
        db_ref[...] = (dm * sb_ref[...].astype(F32)).astype(BF16)
        dgb = dm * cb_ref[...].astype(F32)
        dh = dh + dz_cols(C_GB, D_MODEL, dgb)
        dbgate_ref[:, D_MODEL:2 * D_MODEL] += _colsum8(dgb)
        dyb = _dot_nt(db_ref[...], wb_v[1])

        xh = xh_ref[...].astype(F32)
        vn_s[...] = (xh * lng_ref[...] + lnb_ref[...]).astype(BF16)
        df = dya * ug_ref[...].astype(F32)
        df_s[...] = df.astype(BF16)
        dbsf_acc = df[0:128, :]
        for b in range(1, nb):
            dbsf_acc = dbsf_acc + df[b * 128:(b + 1) * 128, :]
        dbsf_ref[...] += dbsf_acc
        for hd in range(A_HEADS):
            cols = slice(hd * 128, (hd + 1) * 128)
            vcat = jnp.concatenate([vn_s[b * 128:(b + 1) * 128, cols] for b in range(nb)], axis=1)
            dcat = jnp.concatenate([df_s[b * 128:(b + 1) * 128, cols] for b in range(nb)], axis=1)
            gcat = _dot(wmt_ref[hd], dcat)
            dwm_ref[hd] += _dot_nt(dcat, vcat)
            for b in range(nb):
                dvn_s[b * 128:(b + 1) * 128, cols] = gcat[:, b * 128:(b + 1) * 128]
        dh = dh + dz_cols(C_U, D_A, dya * fu_ref[...].astype(F32))
        dvn = dvn_s[...]
        dlng_ref[...] += _colsum8(dvn * xh)
        dlnb_ref[...] += _colsum8(dvn)
        dxh = dvn * lng_ref[...]
        dvc = dxh - jnp.mean(dxh, axis=-1, keepdims=True) - xh * jnp.mean(dxh * xh, axis=-1, keepdims=True)
        dh = dh + dz_cols(C_V, D_A, dvc * cv_ref[...].astype(F32))

        cg = kept(1)
        hbv = kept(2)
        p = cg * hbv
        dh = dh + dz_cols(C_BG, D_B, dyb * q_ref[...].astype(F32))
        dq = dyb * kept(0)
        cr = carry[...]
        dq1 = _shift_up(dq, cr, 1)
        dq2 = _shift_up(dq, cr, 2)
        carry[...] = dq[0:8, :]
        dwsc_ref[0] += _colsum8(dq2 * p)
        dwsc_ref[1] += _colsum8(dq1 * p)
        dwsc_ref[2] += _colsum8(dq * p)
        dp = wsc_ref[2:3, :] * dq + wsc_ref[1:2, :] * dq1 + wsc_ref[0:1, :] * dq2
        dh = dh + dz_cols(C_CG, D_B, dp * hbv)
        dh = dh + dz_cols(C_HB, D_B, dp * cg)

        xv = x_ref[...]
        r = lax.rsqrt(jnp.mean(xv * xv, axis=-1, keepdims=True) + RMS_EPS)
        xn = xv * r
        dg1_ref[...] += _colsum8(dh * xn)
        dxn = dh * g1_ref[...]
        dx_ref[...] = dx2_ref[...] + r * (dxn - xn * jnp.mean(dxn * xn, axis=-1, keepdims=True))

    outs = [
        jax.ShapeDtypeStruct((t_len, D_MODEL), F32),
        jax.ShapeDtypeStruct((t_len, D_IN), BF16),
        jax.ShapeDtypeStruct((t_len, D_MODEL), BF16),
        jax.ShapeDtypeStruct((t_len, D_MODEL), BF16),
        jax.ShapeDtypeStruct((t_len, D_MODEL), BF16),
        jax.ShapeDtypeStruct((8, D_MODEL), F32),
        jax.ShapeDtypeStruct((8, 2 * D_MODEL), F32),
        jax.ShapeDtypeStruct((8, D_A), F32),
        jax.ShapeDtypeStruct((8, D_A), F32),
        jax.ShapeDtypeStruct((A_HEADS, 128, 128), F32),
        jax.ShapeDtypeStruct((128, D_A), F32),
        jax.ShapeDtypeStruct((3, 8, D_B), F32),
    ]

    return _staged_call(
        core, name=f"mixer_bwd_l{layer}", grid=(nt,),
        in_specs=[_row_spec(tm, D_MODEL, nt), _row_spec(tm, D_MODEL, nt), _row_spec(tm, 3 * D_B, nt),
                  _row_spec(tm, D_B, nt), _row_spec(tm, D_MODEL, nt), _row_spec(tm, D_MODEL, nt),
                  _row_spec(tm, D_MODEL, nt), _row_spec(tm, D_MODEL, nt), _row_spec(tm, D_A, nt), _row_spec(tm, D_A, nt),
                  _row_spec(tm, D_A, nt), _row_spec(tm, D_A, nt),
                  _const_spec((1, D_MODEL)), _const_spec((1, D_A)), _const_spec((1, D_A)),
                  _const_spec((A_HEADS, 128, 128)), _const_spec((8, D_B)), ANY, ANY, ANY],
        out_specs=[_row_spec(tm, D_MODEL, nt), _row_spec(tm, D_IN, nt), _row_spec(tm, D_MODEL, nt),
                   _row_spec(tm, D_MODEL, nt), _row_spec(tm, D_MODEL, nt),
                   _const_spec((8, D_MODEL)), _const_spec((8, 2 * D_MODEL)), _const_spec((8, D_A)), _const_spec((8, D_A)),
                   _const_spec((A_HEADS, 128, 128)), _const_spec((128, D_A)), _const_spec((3, 8, D_B))],
        out_shape=outs,
        scratch_shapes=[pltpu.VMEM((D_MODEL, D_IN), BF16), pltpu.VMEM((2, D_A, D_MODEL), BF16),
                        pltpu.VMEM((D_MODEL, D_MODEL), BF16), pltpu.VMEM((8, D_B), F32),
                        pltpu.VMEM((tm, D_A), BF16), pltpu.VMEM((tm, D_A), BF16), pltpu.VMEM((tm, D_A), F32),
                        pltpu.SemaphoreType.DMA((16,))],
        args=[dx2, x, zc, qs, sa, ca, sb, cb, ug, fu, xhs, cv, g1, lng, lnb, wmt, wsc, win_g, wb_g, wout_g],
        stages=stages)


def _wgrad(name, layer, a, b, rows, cols, row_blk, col_blk, stages, a_first=0):
    t_len = a.shape[0]
    n = b.shape[1]
    tk = min(TK_WGRAD, t_len)
    col_sharded = n == N_CHIPS * cols
    m = rows if col_sharded else a.shape[1]
    grid = (m // row_blk, n // col_blk, t_len // tk)
    per_shard_c = cols // col_blk

    if col_sharded:
        out_shape = (N_CHIPS, rows, cols)
        out_spec = pl.BlockSpec((None, row_blk, col_blk), lambda i, j, k: (j // per_shard_c, i, j % per_shard_c))
    else:
        out_shape = (N_CHIPS * rows, cols)
        out_spec = pl.BlockSpec((row_blk, col_blk), lambda i, j, k: (i, j))

    def core(a_ref, b_ref, o_ref):
        @pl.when(pl.program_id(2) == 0)
        def _():
            o_ref[...] = jnp.zeros_like(o_ref)

        o_ref[...] += _dot_tn(a_ref[...], b_ref[...])

    own, outs = _staged_call(
        core, name=f"wgrad_{name}_l{layer}", grid=grid,
        in_specs=[pl.BlockSpec((tk, row_blk), lambda i, j, k: (k, a_first + i)),
                  pl.BlockSpec((tk, col_blk), lambda i, j, k: (k, j))],
        out_specs=[out_spec], out_shape=[jax.ShapeDtypeStruct(out_shape, F32)], scratch_shapes=[],
        args=[a, b], stages=stages)
    return [own[0].reshape(N_CHIPS, rows, cols)], outs


def _wgrad_branch(layer, ya, da, yb, db, stages):
    t_len = ya.shape[0]
    tk = min(TK_WGRAD, t_len)

    def core(ya_ref, da_ref, yb_ref, db_ref, o_ref):
        @pl.when(pl.program_id(1) == 0)
        def _():
            o_ref[...] = jnp.zeros_like(o_ref)

        o_ref[0:D_A, :] += _dot_tn(ya_ref[...], da_ref[...])
        o_ref[D_A:2 * D_A, :] += _dot_tn(yb_ref[...], db_ref[...])

    a_spec = pl.BlockSpec((tk, D_A), lambda j, k: (k, 0))
    d_spec = pl.BlockSpec((tk, 256), lambda j, k: (k, j))
    return _staged_call(
        core, name=f"wgrad_w_branch_l{layer}", grid=(N_CHIPS, t_len // tk),
        in_specs=[a_spec, d_spec, a_spec, d_spec],
        out_specs=[pl.BlockSpec((None, 2 * D_A, 256), lambda j, k: (j, 0, 0))],
        out_shape=[jax.ShapeDtypeStruct((N_CHIPS, 2 * D_A, 256), F32)], scratch_shapes=[],
        args=[ya, da, yb, db], stages=stages)


def _flat_blk(rows, cols):
    blk = rows
    while blk * cols * 4 > 2 * 1024 * 1024 and blk % 16 == 0:
        blk //= 2
    return blk


def _cast_into_slots(name, layer, ws, chip):
    blks = [_flat_blk(w.shape[1], w.shape[2]) for w in ws]
    nblks = [w.shape[1] // b for w, b in zip(ws, blks)]
    n = len(ws)

    def body(chip_ref, *refs):
        for w_ref, o_ref in zip(refs[:n], refs[n:]):
            o_ref[...] = w_ref[...].astype(BF16)

    def in_spec(w, blk, nblk):
        return pl.BlockSpec((None, blk, w.shape[2]), lambda i, chip_ref: (layer, jnp.minimum(i, nblk - 1), 0))

    def out_spec(w, blk, nblk):
        return pl.BlockSpec((None, blk, w.shape[2]), lambda i, chip_ref: (chip_ref[0], jnp.minimum(i, nblk - 1), 0))

    return pl.pallas_call(
        body, name=f"cast_{name}_l{layer}",
        grid_spec=pltpu.PrefetchScalarGridSpec(
            num_scalar_prefetch=1, grid=(max(nblks),),
            in_specs=[in_spec(w, b, k) for w, b, k in zip(ws, blks, nblks)],
            out_specs=[out_spec(w, b, k) for w, b, k in zip(ws, blks, nblks)]),
        out_shape=[jax.ShapeDtypeStruct((N_CHIPS,) + w.shape[1:], BF16) for w in ws],
        compiler_params=_params(),
    )(chip, *ws)


def _pair_sum(name, grad, other, core):
    _, h, cols = other.shape
    blk = _flat_blk(h, cols)
    nblk = h // blk

    def body(core_ref, g_ref, o_ref, s_ref):
        s_ref[...] = (g_ref[...] + o_ref[...]).astype(BF16)

    spec = pl.BlockSpec((None, blk, cols), lambda k, i, core_ref: (k, i, 0))
    return pl.pallas_call(
        body, name=f"pair_sum_{name}",
        grid_spec=pltpu.PrefetchScalarGridSpec(
            num_scalar_prefetch=1, grid=(N_CHIPS, nblk),
            in_specs=[pl.BlockSpec((None, blk, cols), lambda k, i, core_ref: (k, core_ref[0] * nblk + i, 0)), spec],
            out_specs=spec),
        out_shape=jax.ShapeDtypeStruct((N_CHIPS, h, cols), BF16),
        compiler_params=_params(("parallel", "parallel")),
    )(core, grad, other)


def _chip_sum(name, grad, other, got, pos):
    _, rows, cols = grad.shape
    h = rows // 2
    blk = _flat_blk(h, cols)
    nblk = h // blk

    def body(pos_ref, g_ref, o_ref, r_ref, f_ref):
        f_ref[...] = (((g_ref[...] + o_ref[...]) + r_ref[0].astype(F32)) + r_ref[1].astype(F32)) + r_ref[2].astype(F32)

    return pl.pallas_call(
        body, name=f"chip_sum_{name}",
        grid_spec=pltpu.PrefetchScalarGridSpec(
            num_scalar_prefetch=1, grid=(nblk,),
            in_specs=[pl.BlockSpec((None, blk, cols), lambda i, pos_ref: (pos_ref[0], pos_ref[1] * nblk + i, 0)),
                      pl.BlockSpec((None, blk, cols), lambda i, pos_ref: (pos_ref[0], i, 0)),
                      pl.BlockSpec((3, blk, cols), lambda i, pos_ref: (0, i, 0))],
            out_specs=pl.BlockSpec((blk, cols), lambda i, pos_ref: (pos_ref[1] * nblk + i, 0))),
        out_shape=jax.ShapeDtypeStruct((rows, cols), F32),
        compiler_params=_params(("parallel",)),
    )(pos, grad, other, got)


def _sum_slots(name, slots):
    n, rows, _ = slots.shape

    def body(s_ref, o_ref):
        acc = s_ref[0]
        for d in range(1, n):
            acc = acc + s_ref[d]
        o_ref[...] = acc

    return pl.pallas_call(
        body, name=f"sum_slots_{name}", grid=(1,),
        in_specs=[pl.BlockSpec((n, rows, 128), lambda i: (0, 0, 0))],
        out_specs=pl.BlockSpec((rows, 128), lambda i: (0, 0)),
        out_shape=jax.ShapeDtypeStruct((rows, 128), F32),
        compiler_params=_params(),
    )(slots)


def _adamw_math(w, g, m, v):
    m2 = ADAM_B1 * m + (1.0 - ADAM_B1) * g
    v2 = ADAM_B2 * v + (1.0 - ADAM_B2) * (g * g)
    m_hat = m2 / (1.0 - ADAM_B1 ** ADAM_STEP)
    v_hat = v2 / (1.0 - ADAM_B2 ** ADAM_STEP)
    delta = -ADAM_LR * (m_hat / (jnp.sqrt(v_hat) + ADAM_EPS) + ADAM_WD * w)
    return delta, m2, v2


def _adamw_big(name, w, g0, g1, m, v):
    _, rows, cols = w.shape
    blk = _flat_blk(rows, cols) // 2

    def body(w_ref, g0_ref, g1_ref, m_ref, v_ref, g_ref, d_ref, m2_ref, v2_ref):
        g = jnp.where(pl.program_id(0) == 0, g0_ref[...], g1_ref[...])
        d, m2, v2 = _adamw_math(w_ref[...], g, m_ref[...], v_ref[...])
        g_ref[...] = g
        d_ref[...] = d
        m2_ref[...] = m2
        v2_ref[...] = v2

    spec = pl.BlockSpec((None, blk, cols), lambda la, i: (la, i, 0))
    return pl.pallas_call(
        body, name=f"adamw_{name}", grid=(N_LAYERS, rows // blk),
        in_specs=[spec, pl.BlockSpec((blk, cols), lambda la, i: (i * (1 - la), 0)),
                  pl.BlockSpec((blk, cols), lambda la, i: (i * la, 0)), spec, spec],
        out_specs=[spec] * 4,
        out_shape=[jax.ShapeDtypeStruct(w.shape, F32)] * 4,
        compiler_params=_params(("parallel", "parallel")),
    )(w, g0, g1, m, v)


def _adamw_small(ws, gs, ms, vs):
    n = len(ws)

    def body(*refs):
        ins, outs = refs[:4 * n], refs[4 * n:]
        for k in range(n):
            d, m2, v2 = _adamw_math(ins[k][...], ins[n + k][...], ins[2 * n + k][...], ins[3 * n + k][...])
            outs[k][...] = d
            outs[n + k][...] = m2
            outs[2 * n + k][...] = v2

    vmem = pl.BlockSpec(memory_space=pltpu.VMEM)
    return pl.pallas_call(
        body, name="adamw_small",
        in_specs=[vmem] * (4 * n), out_specs=[vmem] * (3 * n),
        out_shape=[jax.ShapeDtypeStruct(w.shape, F32) for w in ws] * 3,
        compiler_params=pltpu.CompilerParams(vmem_limit_bytes=V7X_VMEM_LIMIT),
    )(*ws, *gs, *ms, *vs)


SMALL = ("norm1_g", "b_gate", "gmlp_ln_g", "gmlp_ln_b", "w_spatial", "b_spatial", "w_shortconv", "norm2_g",
         "w_ffn_conv", "b_ffn_conv", "final_g")
ALL_WEIGHTS = ("norm1_g", "w_in", "b_gate", "gmlp_ln_g", "gmlp_ln_b", "w_spatial", "b_spatial", "w_shortconv",
               "w_branch", "w_out", "norm2_g", "w_ffn_up", "w_ffn_conv", "b_ffn_conv", "w_ffn_down", "final_g")


def _pack(arrays):
    flat = jnp.concatenate([a.reshape(-1) for a in arrays])
    n = flat.shape[0]
    rows = -(-n // 1024) * 8
    return jnp.pad(flat, (0, rows * 128 - n)).reshape(rows, 128)


def _unpack(packed, like):
    flat = packed.reshape(-1)
    out, off = [], 0
    for a in like:
        out.append(flat[off:off + a.size].reshape(a.shape))
        off += a.size
    return out


def _pad8(w):
    return jnp.pad(w, ((0, 5), (0, 0)))


def kernel(x, norm1_g, w_in, b_gate, gmlp_ln_g, gmlp_ln_b, w_spatial, b_spatial, w_shortconv, w_branch, w_out, norm2_g, w_ffn_up, w_ffn_conv, b_ffn_conv, w_ffn_down, final_g, loss_target, m_norm1_g, m_w_in, m_b_gate, m_gmlp_ln_g, m_gmlp_ln_b, m_w_spatial, m_b_spatial, m_w_shortconv, m_w_branch, m_w_out, m_norm2_g, m_w_ffn_up, m_w_ffn_conv, m_b_ffn_conv, m_w_ffn_down, m_final_g, v_norm1_g, v_w_in, v_b_gate, v_gmlp_ln_g, v_gmlp_ln_b, v_w_spatial, v_b_spatial, v_w_shortconv, v_w_branch, v_w_out, v_norm2_g, v_w_ffn_up, v_w_ffn_conv, v_b_ffn_conv, v_w_ffn_down, v_final_g):
    weights = dict(norm1_g=norm1_g, w_in=w_in, b_gate=b_gate, gmlp_ln_g=gmlp_ln_g, gmlp_ln_b=gmlp_ln_b,
                   w_spatial=w_spatial, b_spatial=b_spatial, w_shortconv=w_shortconv, w_branch=w_branch, w_out=w_out,
                   norm2_g=norm2_g, w_ffn_up=w_ffn_up, w_ffn_conv=w_ffn_conv, b_ffn_conv=b_ffn_conv,
                   w_ffn_down=w_ffn_down, final_g=final_g)
    mom = dict(norm1_g=m_norm1_g, w_in=m_w_in, b_gate=m_b_gate, gmlp_ln_g=m_gmlp_ln_g, gmlp_ln_b=m_gmlp_ln_b,
               w_spatial=m_w_spatial, b_spatial=m_b_spatial, w_shortconv=m_w_shortconv, w_branch=m_w_branch,
               w_out=m_w_out, norm2_g=m_norm2_g, w_ffn_up=m_w_ffn_up, w_ffn_conv=m_w_ffn_conv,
               b_ffn_conv=m_b_ffn_conv, w_ffn_down=m_w_ffn_down, final_g=m_final_g)
    vel = dict(norm1_g=v_norm1_g, w_in=v_w_in, b_gate=v_b_gate, gmlp_ln_g=v_gmlp_ln_g, gmlp_ln_b=v_gmlp_ln_b,
               w_spatial=v_w_spatial, b_spatial=v_b_spatial, w_shortconv=v_w_shortconv, w_branch=v_w_branch,
               w_out=v_w_out, norm2_g=v_norm2_g, w_ffn_up=v_w_ffn_up, w_ffn_conv=v_w_ffn_conv,
               b_ffn_conv=v_b_ffn_conv, w_ffn_down=v_w_ffn_down, final_g=v_final_g)

    cx, cy, cc = _mesh_pos()
    chip = 2 * cx + cy
    core_arr = cc.astype(jnp.int32).reshape(1)
    chip_arr = chip.astype(jnp.int32).reshape(1)
    pos_arr = jnp.stack([chip, cc]).astype(jnp.int32)
    t_len = x.shape[1]
    xs = x.reshape(t_len, D_MODEL)
    target = loss_target.reshape(t_len, D_MODEL)
    pipe = _Pipe()

    full = {}

    def gather(group, names, la):
        slots = _cast_into_slots(group, la, [weights[n].reshape((N_LAYERS,) + BIG[n]) for n in names], chip_arr)

        def then(*bufs):
            full.update(zip([(n, la) for n in names], bufs))

        pipe.add(_gather_stage(slots, then))

    mixer_w = ("w_in", "w_branch", "w_out")
    ffn_w = ("w_ffn_up", "w_ffn_down")
    gather("mixer", mixer_w, 0)
    tap_slots = {}
    pipe.add(_chip_spread_stage(_pack([w_shortconv, w_ffn_conv]), lambda slots: tap_slots.__setitem__("all", slots)))
    pipe.flush()
    by_chip = [_unpack(tap_slots["all"][k], [w_shortconv, w_ffn_conv]) for k in range(N_CHIPS)]
    wsc_full = jnp.concatenate([t[0] for t in by_chip], axis=-1)
    wfc_full = jnp.concatenate([t[1] for t in by_chip], axis=-1)

    idx = jnp.arange(GMLP_BLOCK) // CHUNK
    mask = idx[None, :] <= idx[:, None]
    wm_all = jnp.where(mask[None, None], w_spatial, 0.0)
    wm_bf = wm_all.astype(BF16)
    wmt_bf = jnp.swapaxes(wm_all, -1, -2).astype(BF16)
    bsf = jnp.repeat(jnp.swapaxes(b_spatial, -1, -2), 128, axis=-1)

    def row(a):
        return a.reshape(1, -1)

    def mixer_args(la):
        return (row(norm1_g[la]), row(b_gate[la]), row(gmlp_ln_g[la]), row(gmlp_ln_b[la]))

    def mixer_weights(la):
        return tuple(full[(n, la)] for n in mixer_w)

    def ffn_weights(la):
        return tuple(full[(n, la)] for n in ffn_w)

    saved = []
    h_in = xs
    for la in range(N_LAYERS):
        gather("ffn", ffn_w, la)
        *kept, mg, h1, x2 = pipe.carry(lambda st: _mixer_fwd(
            la, h_in, *mixer_args(la), wm_bf[la], bsf[la], _pad8(wsc_full[la]), *mixer_weights(la), st))
        ya, yb = kept[1], kept[2]
        if la + 1 < N_LAYERS:
            gather("mixer", mixer_w, la + 1)
        head = (target, row(final_g)) if la == N_LAYERS - 1 else None
        up, silu, dsilu, act, h2, *rest = pipe.carry(lambda st: _ffn_fwd(
            la, x2, row(norm2_g[la]), _pad8(wfc_full[la]), row(b_ffn_conv[la]), *ffn_weights(la), st, head=head))
        saved.append(dict(x=h_in, ya=ya, yb=yb, mixer=[kept[0]] + kept[3:], mg=mg, h1=h1, x2=x2, up=up, silu=silu,
                          dsilu=dsilu, act=act, h2=h2))
        h_in = rest[0]
    dx, dgf8, loss8 = rest

    reduced_big = {}

    def reduce_big(name, la, grad):
        tag = f"{name}_l{la}"

        def after_pair(other):
            psum = _pair_sum(tag, grad, other, core_arr)

            def after_chips(got):
                final = _chip_sum(tag, grad, other, got, pos_arr)
                pipe.add(_pair_fill_stage(final, lambda done: reduced_big.__setitem__((name, la), done)))

            pipe.add(_chip_send_stage(psum, after_chips))

        pipe.add(_pair_send_stage(grad, after_pair))

    small = {n: [None] * N_LAYERS for n in SMALL}
    spread = {}
    for la in reversed(range(N_LAYERS)):
        s = saved[la]
        dx3 = dx
        dx2, dup, dx3b, dg2, dbfc, dwfc = pipe.carry(lambda st: _ffn_bwd(
            la, dx3, s["x2"], s["up"], s["silu"], s["dsilu"], row(norm2_g[la]), _pad8(wfc_full[la]),
            *ffn_weights(la), st))
        g, = pipe.carry(lambda st: _wgrad("w_ffn_down", la, s["act"], dx3b, 704, 1024, 1408, 1024, st))
        reduce_big("w_ffn_down", la, g)
        g, = pipe.carry(lambda st: _wgrad("w_ffn_up", la, s["h2"], dup, 1024, 1408, 1024, 1408, st))
        reduce_big("w_ffn_up", la, g)
        run = pipe.carry if la > 0 else (lambda call: call([])[0])
        dxl, dz, da, db, dx2b, dg1, dbg, dlng, dlnb, dwm, dbsf, dwsc = run(lambda st: _mixer_bwd(
            la, dx2, s["x"], *s["mixer"], row(norm1_g[la]), row(gmlp_ln_g[la]), row(gmlp_ln_b[la]), wmt_bf[la],
            _pad8(wsc_full[la]), *mixer_weights(la), st))
        small["norm1_g"][la] = dg1.sum(0)
        small["b_gate"][la] = dbg.sum(0)
        small["gmlp_ln_g"][la] = dlng.sum(0)
        small["gmlp_ln_b"][la] = dlnb.sum(0)
        small["w_spatial"][la] = jnp.where(mask[None], dwm, 0.0)
        small["b_spatial"][la] = dbsf.reshape(128, A_HEADS, 128).sum(-1).T
        small["w_shortconv"][la] = dwsc.sum(1)
        small["norm2_g"][la] = dg2.sum(0)
        small["w_ffn_conv"][la] = dwfc.sum(1)
        small["b_ffn_conv"][la] = dbfc.sum(0)
        if la == 0:
            small_local = ([jnp.stack(small[n]) for n in SMALL[:-1]]
                           + [dgf8.sum(0), 0.5 * loss8.sum().reshape(1) / D_MODEL])
            mine = _pack(small_local)

            def after_swap(other, mine=mine):
                pair = _sum_slots("small_pair", jnp.stack([mine, other]))
                pipe.add(_chip_spread_stage(pair, lambda slots: spread.__setitem__("slots", slots)))

            pipe.add(_pair_swap_stage(mine, after_swap))
        for part, tag in enumerate(("w_in_a", "w_in_b")):
            g, = pipe.carry(lambda st: _wgrad(tag, la, s["h1"], dz, 512, 1152, 512, 1152, st, a_first=part))
            reduce_big(tag, la, g)
        g, = pipe.carry(lambda st: _wgrad("w_out", la, s["mg"], dx2b, 256, 1024, 1024, 1024, st), long=False)
        reduce_big("w_out", la, g)
        g, = pipe.carry(lambda st: _wgrad_branch(la, s["ya"], da, s["yb"], db, st), long=False)
        reduce_big("w_branch", la, g)
        dx = dxl
    grad_x = dx.reshape(x.shape)
    pipe.flush()

    for la in range(N_LAYERS):
        reduced_big[("w_in", la)] = jnp.concatenate([reduced_big[("w_in_a", la)], reduced_big[("w_in_b", la)]], axis=0)
    reduced = _unpack(_sum_slots("small_grads", spread["slots"]), small_local)
    loss = reduced[-1].reshape(())
    grads = dict(zip(SMALL, reduced[:-1]))
    grads["w_shortconv"] = lax.dynamic_slice(grads["w_shortconv"], (0, 0, chip * (D_B // 4)), (N_LAYERS, 3, D_B // 4))
    grads["w_ffn_conv"] = lax.dynamic_slice(grads["w_ffn_conv"], (0, 0, chip * (D_FF // 4)), (N_LAYERS, 3, D_FF // 4))

    delta, new_m, new_v = {}, {}, {}
    for n in BIG_NAMES:
        shape3 = (N_LAYERS,) + BIG[n]
        res = _adamw_big(n, weights[n].reshape(shape3), reduced_big[(n, 0)], reduced_big[(n, 1)],
                         mom[n].reshape(shape3), vel[n].reshape(shape3))
        grads[n], delta[n], new_m[n], new_v[n] = (a.reshape(weights[n].shape) for a in res)
    res = _adamw_small(*[[src[n].reshape(-1, src[n].shape[-1]) for n in SMALL] for src in (weights, grads, mom, vel)])
    for k, n in enumerate(SMALL):
        delta[n], new_m[n], new_v[n] = (res[j * len(SMALL) + k].reshape(weights[n].shape) for j in range(3))

    return (loss, grad_x, *[grads[n] for n in ALL_WEIGHTS], *[delta[n] for n in ALL_WEIGHTS],
            *[new_m[n] for n in ALL_WEIGHTS], *[new_v[n] for n in ALL_WEIGHTS])
```

```python
import jax
import jax.numpy as jnp
from jax import lax
from jax.experimental import pallas as pl
from jax.experimental.pallas import tpu as pltpu

F32 = jnp.float32
BF16 = jnp.bfloat16
MESH = pl.DeviceIdType.MESH
ANY = pl.BlockSpec(memory_space=pl.ANY)

D_MODEL = 1024
D_A = 512
D_B = 512
D_IN = 4608
D_FF = 2816
GMLP_BLOCK = 128
CHUNK = 64
A_HEADS = 4
N_LAYERS = 2
N_CHIPS = 4
RMS_EPS = 1e-6
LN_EPS = 1e-5
ADAM_LR = 0.001
ADAM_B1 = 0.9
ADAM_B2 = 0.999
ADAM_EPS = 1e-08
ADAM_WD = 0.01
ADAM_STEP = 10

C_U, C_V, C_BG, C_CG, C_HB, C_GA, C_GB = 0, 512, 1024, 1536, 2048, 2560, 3584

V7X_VMEM_LIMIT = 60 * 1024 * 1024
TM_MIX = 256
TM_FFN = 256
TK_WGRAD = 2048
SLOW_COPY_BYTES = 640 * 1024
FF_CHUNKS = ((0, 768), (768, 1536), (1536, 2304), (2304, 2816))
GELU_C0 = 0.7978845608028654
GELU_C1 = 0.044715

BIG = {
    "w_in": (1024, 1152),
    "w_branch": (1024, 256),
    "w_out": (256, 1024),
    "w_ffn_up": (1024, 1408),
    "w_ffn_down": (704, 1024),
}
BIG_NAMES = tuple(BIG)


def _params(sem=("arbitrary",), vmem=V7X_VMEM_LIMIT):
    return pltpu.CompilerParams(dimension_semantics=sem, vmem_limit_bytes=vmem)


def _gelu(x):
    x2 = x * x
    t = jnp.tanh(GELU_C0 * x * (1.0 + GELU_C1 * x2))
    return 0.5 * x * (1.0 + t), t


def _gelu_grad(x, t):
    return 0.5 * (1.0 + t) + 0.5 * x * (1.0 - t * t) * GELU_C0 * (1.0 + 3.0 * GELU_C1 * x * x)


def _colsum8(v):
    r, n = v.shape
    return v.reshape(r // 8, 8, n).sum(axis=0)


def _dot(a, b):
    return jnp.dot(a, b, preferred_element_type=F32)


def _dot_nt(a, b):
    return lax.dot_general(a, b, (((1,), (1,)), ((), ())), preferred_element_type=F32)


def _dot_tn(a, b):
    return lax.dot_general(a, b, (((0,), (0,)), ((), ())), preferred_element_type=F32)


def _shift_down(v, carry, n):
    rows = lax.broadcasted_iota(jnp.int32, (8, v.shape[1]), 0)
    out = pltpu.roll(v, n, 0)
    head = out[0:8, :]
    for r in range(n):
        head = jnp.where(rows == r, carry[8 - n + r:8 - n + r + 1, :], head)
    return jnp.concatenate([head, out[8:, :]], axis=0)


def _shift_up(v, carry, n):
    tm = v.shape[0]
    rows = lax.broadcasted_iota(jnp.int32, (8, v.shape[1]), 0)
    out = pltpu.roll(v, tm - n, 0)
    tail = out[tm - 8:tm, :]
    for r in range(n):
        tail = jnp.where(rows == 8 - n + r, carry[r:r + 1, :], tail)
    return jnp.concatenate([out[0:tm - 8, :], tail], axis=0)


def _sigmoid(x):
    return 0.5 * jnp.tanh(0.5 * x) + 0.5


def _start_all(copies):
    for cp in copies:
        cp.start()


def _wait_all(copies):
    for cp in copies:
        cp.wait()


def _load_col_sharded(src, dst, sems, first):
    cs = src.shape[-1]
    return [pltpu.make_async_copy(src.at[k], dst.at[:, k * cs:(k + 1) * cs], sems.at[first + k])
            for k in range(N_CHIPS)]


def _load_row_sharded(src, dst, sems, first):
    rs = src.shape[-2]
    return [pltpu.make_async_copy(src.at[k], dst.at[k * rs:(k + 1) * rs, :], sems.at[first + k])
            for k in range(N_CHIPS)]


def _load_branch(src, dst, sems, first):
    return [pltpu.make_async_copy(src.at[k, pl.ds(m * D_A, D_A), :], dst.at[m, :, k * 256:(k + 1) * 256],
                                  sems.at[first + 2 * k + m])
            for k in range(N_CHIPS) for m in range(2)]


def _row_spec(tm, n, rev=None):
    if rev is None:
        return pl.BlockSpec((tm, n), lambda i: (i, 0))
    return pl.BlockSpec((tm, n), lambda i: (rev - 1 - i, 0))


def _const_spec(shape):
    nd = len(shape)
    return pl.BlockSpec(shape, lambda i: (0,) * nd)


def _mesh_pos():
    return lax.axis_index("x"), lax.axis_index("y"), lax.axis_index("c")


def _other_chips(x, y):
    return [(1 - x, y, 2 * (1 - x) + y), (x, 1 - y, 2 * x + (1 - y)), (1 - x, 1 - y, 2 * (1 - x) + (1 - y))]


def _remote(src, dst, ssem, rsem, to):
    return pltpu.make_async_remote_copy(src_ref=src, dst_ref=dst, send_sem=ssem, recv_sem=rsem, device_id=to,
                                        device_id_type=MESH)


def _half(ref, which, h):
    start = pl.multiple_of(which * h, 8)
    if len(ref.shape) == 2:
        return ref.at[pl.ds(start, h), :]
    return ref.at[:, pl.ds(start, h), :]


class _Stage:
    def __init__(self, ins=(), inouts=(), outs=(), n_sems=0, start=None, mid=None, finish=None, then=None, slow=False):
        self.ins, self.inouts, self.outs = list(ins), list(inouts), list(outs)
        self.n_sems, self.start, self.mid, self.finish, self.then = n_sems, start, mid, finish, then
        self.slow = slow


def _gather_stage(bufs, then):
    n = len(bufs)

    def copies(io, sem):
        x, y, c = _mesh_pos()
        me = 2 * x + y
        ici, fwd, got = [], [], []
        for w in range(n):
            h = io[w].shape[1] // 2
            for j, (px, py, pk) in enumerate(_other_chips(x, y)):
                mine = _half(io[w].at[me], c, h)
                theirs = _half(io[w].at[pk], c, h)
                ici.append(_remote(mine, mine, sem(12 * w + j), sem(12 * w + 3 + j), (px, py, c)))
                got.append(_remote(theirs, theirs, sem(12 * w + j), sem(12 * w + 3 + j), (px, py, c)))
                fwd.append(_remote(theirs, theirs, sem(12 * w + 6 + j), sem(12 * w + 9 + j), (x, y, 1 - c)))
        return ici, got, fwd

    def start(ins, io, outs, sem):
        _start_all(copies(io, sem)[0])

    def mid(ins, io, outs, sem):
        _, got, fwd = copies(io, sem)
        for g, f in zip(got, fwd):
            g.wait_recv()
            f.start()

    def finish(ins, io, outs, sem):
        x, y, c = _mesh_pos()
        ici, _, fwd = copies(io, sem)
        for w in range(n):
            h = io[w].shape[1] // 2
            for j, (px, py, pk) in enumerate(_other_chips(x, y)):
                other = _half(io[w].at[pk], 1 - c, h)
                _remote(other, other, sem(12 * w + 6 + j), sem(12 * w + 9 + j), (x, y, 1 - c)).wait_recv()
        for cp in ici + fwd:
            cp.wait_send()

    return _Stage(inouts=bufs, n_sems=12 * n, start=start, mid=mid, finish=finish, then=then)


def _pair_send_stage(grad, then):
    h = grad.shape[1] // 2

    def copy(ins, outs, sem):
        x, y, c = _mesh_pos()
        return _remote(_half(ins[0], 1 - c, h), outs[0], sem(0), sem(1), (x, y, 1 - c))

    return _Stage(ins=[grad], outs=[jax.ShapeDtypeStruct((N_CHIPS, h, grad.shape[2]), F32)], n_sems=2,
                  start=lambda ins, io, outs, sem: copy(ins, outs, sem).start(),
                  finish=lambda ins, io, outs, sem: copy(ins, outs, sem).wait(), then=then)


def _chip_send_stage(psum, then):
    def copies(ins, outs, sem):
        x, y, c = _mesh_pos()
        return [_remote(ins[0].at[pk], outs[0].at[j], sem(j), sem(3 + j), (px, py, c))
                for j, (px, py, pk) in enumerate(_other_chips(x, y))]

    return _Stage(ins=[psum], outs=[jax.ShapeDtypeStruct((3,) + psum.shape[1:], BF16)], n_sems=6,
                  start=lambda ins, io, outs, sem: _start_all(copies(ins, outs, sem)),
                  finish=lambda ins, io, outs, sem: _wait_all(copies(ins, outs, sem)), then=then,
                  slow=psum.shape[1] * psum.shape[2] * 2 > SLOW_COPY_BYTES)


def _pair_fill_stage(final, then):
    h = final.shape[0] // 2

    def copy(io, sem):
        x, y, c = _mesh_pos()
        mine = _half(io[0], c, h)
        return _remote(mine, mine, sem(0), sem(1), (x, y, 1 - c))

    return _Stage(inouts=[final], n_sems=2,
                  start=lambda ins, io, outs, sem: copy(io, sem).start(),
                  finish=lambda ins, io, outs, sem: copy(io, sem).wait(), then=then)


def _pair_swap_stage(packed, then):
    def copy(ins, outs, sem):
        x, y, c = _mesh_pos()
        return _remote(ins[0], outs[0], sem(0), sem(1), (x, y, 1 - c))

    return _Stage(ins=[packed], outs=[jax.ShapeDtypeStruct(packed.shape, F32)], n_sems=2,
                  start=lambda ins, io, outs, sem: copy(ins, outs, sem).start(),
                  finish=lambda ins, io, outs, sem: copy(ins, outs, sem).wait(), then=then)


def _chip_spread_stage(psum, then):
    def copies(ins, outs, sem):
        x, y, c = _mesh_pos()
        me = 2 * x + y
        cps = [_remote(ins[0], outs[0].at[me], sem(j), sem(3 + j), (px, py, c))
               for j, (px, py, pk) in enumerate(_other_chips(x, y))]
        return cps, pltpu.make_async_copy(ins[0], outs[0].at[me], sem(6))

    def start(ins, io, outs, sem):
        cps, own = copies(ins, outs, sem)
        own.start()
        _start_all(cps)

    def finish(ins, io, outs, sem):
        cps, own = copies(ins, outs, sem)
        _wait_all(cps)
        own.wait()

    return _Stage(ins=[psum], outs=[jax.ShapeDtypeStruct((N_CHIPS,) + psum.shape, F32)], n_sems=7,
                  start=start, finish=finish, then=then)


def _staged_call(core, *, name, grid, in_specs, out_specs, out_shape, scratch_shapes, args, stages):
    n_in, n_out, n_scr = len(args), len(out_shape), len(scratch_shapes)
    s_args, s_outs, aliases, layout = [], [], {}, []
    n_sems = 0
    for st in stages:
        i0, o0 = len(s_args), len(s_outs)
        s_args += st.ins + st.inouts
        for q in range(len(st.inouts)):
            aliases[n_in + i0 + len(st.ins) + q] = n_out + o0 + q
        s_outs += [jax.ShapeDtypeStruct(a.shape, a.dtype) for a in st.inouts] + st.outs
        layout.append((i0, o0, n_sems))
        n_sems += st.n_sems
    steps = 1
    for g in grid:
        steps *= g

    def body(*refs):
        own_in = refs[:n_in]
        s_in = refs[n_in:n_in + len(s_args)]
        rest = refs[n_in + len(s_args):]
        own_out = rest[:n_out]
        s_out = rest[n_out:n_out + len(s_outs)]
        scr = rest[n_out + len(s_outs):]

        def run(which):
            for st, (i0, o0, s0) in zip(stages, layout):
                fn = getattr(st, which)
                if fn is not None:
                    fn(s_in[i0:i0 + len(st.ins)], s_out[o0:o0 + len(st.inouts)],
                       s_out[o0 + len(st.inouts):o0 + len(st.inouts) + len(st.outs)],
                       lambda k, s0=s0: scr[n_scr].at[s0 + k])

        if not stages:
            core(*own_in, *own_out, *scr[:n_scr])
            return
        step = 0
        for d, g in enumerate(grid):
            step = step * g + pl.program_id(d)
        if steps == 1:
            run("start")
            core(*own_in, *own_out, *scr[:n_scr])
            run("mid")
            run("finish")
            return
        pl.when(step == 0)(lambda: run("start"))
        core(*own_in, *own_out, *scr[:n_scr])
        pl.when(step == (3 * steps) // 4)(lambda: run("mid"))
        pl.when(step == steps - 1)(lambda: run("finish"))

    sem = ("arbitrary",) * len(grid) if stages else ("parallel",) * max(len(grid) - 1, 0) + ("arbitrary",) * min(len(grid), 1)
    res = pl.pallas_call(
        body, name=name, grid=grid,
        in_specs=list(in_specs) + [ANY] * len(s_args),
        out_specs=list(out_specs) + [ANY] * len(s_outs),
        out_shape=list(out_shape) + s_outs,
        input_output_aliases=aliases,
        scratch_shapes=list(scratch_shapes) + ([pltpu.SemaphoreType.DMA((n_sems,))] if stages else []),
        compiler_params=_params(sem) if grid else pltpu.CompilerParams(vmem_limit_bytes=V7X_VMEM_LIMIT),
    )(*args, *s_args)
    return list(res[:n_out]), list(res[n_out:])


class _Pipe:
    def __init__(self):
        self.ready = []
        self.flushes = 0
        self.after = None

    def add(self, stage):
        self.ready.append(stage)

    def carry(self, call, long=True):
        stages = [st for st in self.ready if long or not st.slow]
        self.ready = [st for st in self.ready if not (long or not st.slow)]
        own, outs = call(stages)
        k = 0
        for st in stages:
            n = len(st.inouts) + len(st.outs)
            st.then(*outs[k:k + n])
            k += n
        if self.after is not None:
            self.after()
        return own

    def flush(self):
        while self.ready:
            self.flushes += 1
            self.carry(lambda stages: _staged_call(
                lambda *refs: None, name=f"comm_tail_{self.flushes}", grid=(), in_specs=[], out_specs=[], out_shape=[],
                scratch_shapes=[], args=[], stages=stages))


def _mixer_fwd(layer, x, g1, bgate, lng, lnb, wm, bsf, wsc, win_g, wb_g, wout_g, stages):
    t_len = x.shape[0]
    tm = min(TM_MIX, t_len)
    nt = t_len // tm
    nb = tm // GMLP_BLOCK

    def core(x_ref, x_late_ref, g1_ref, bgate_ref, lng_ref, lnb_ref, wm_ref, bsf_ref, wsc_ref, win_hbm, wb_hbm, wout_hbm,
             zc_ref, ya_ref, yb_ref, q_ref, sa_ref, ca_ref, sb_ref, cb_ref, ug_ref, fu_ref, xh_ref, cv_ref,
             mg_ref, h_ref, x2_ref,
             win_v, wb_v, wout_v, carry, vn_s, f_s, z_s, sems):
        i = pl.program_id(0)

        @pl.when(i == 0)
        def _():
            cps = (_load_col_sharded(win_hbm, win_v, sems, 0) + _load_branch(wb_hbm, wb_v, sems, 4)
                   + _load_row_sharded(wout_hbm, wout_v, sems, 12))
            _start_all(cps)
            carry[...] = jnp.zeros_like(carry)
            z_s[...] = jnp.zeros_like(z_s)
            _wait_all(cps)

        xv = x_ref[...]
        r = lax.rsqrt(jnp.mean(xv * xv, axis=-1, keepdims=True) + RMS_EPS)
        h_ref[...] = (xv * r * g1_ref[...]).astype(BF16)

        def zcols(c0, n, keep=None):
            zv = z_s[:, c0:c0 + n]
            z_s[:, c0:c0 + n] = _dot(h_ref[...], win_v[:, c0:c0 + n])
            if keep is not None:
                zc_ref[:, keep * D_B:(keep + 1) * D_B] = zv.astype(BF16)
            return zv

        v = zcols(C_V, D_A)
        vg, tv = _gelu(v)
        mu = jnp.mean(vg, axis=-1, keepdims=True)
        vc = vg - mu
        rstd = lax.rsqrt(jnp.mean(vc * vc, axis=-1, keepdims=True) + LN_EPS)
        xh = vc * rstd
        xh_ref[...] = xh.astype(BF16)
        cv_ref[...] = (rstd * _gelu_grad(v, tv)).astype(BF16)
        vn_s[...] = (xh * lng_ref[...] + lnb_ref[...]).astype(BF16)
        for hd in range(A_HEADS):
            cols = slice(hd * 128, (hd + 1) * 128)
            vcat = jnp.concatenate([vn_s[b * 128:(b + 1) * 128, cols] for b in range(nb)], axis=1)
            fcat = _dot(wm_ref[hd], vcat)
            for b in range(nb):
                f_s[b * 128:(b + 1) * 128, cols] = fcat[:, b * 128:(b + 1) * 128]
        u = zcols(C_U, D_A)
        ug, tu = _gelu(u)
        ug_ref[...] = ug.astype(BF16)
        fb = f_s[...] + jnp.concatenate([bsf_ref[...]] * nb, axis=0)
        fu_ref[...] = (fb * _gelu_grad(u, tu)).astype(BF16)
        ya_ref[...] = (ug * fb).astype(BF16)

        p = zcols(C_CG, D_B, keep=1) * zcols(C_HB, D_B, keep=2)
        cr = carry[...]
        q = wsc_ref[0:1, :] * _shift_down(p, cr, 2) + wsc_ref[1:2, :] * _shift_down(p, cr, 1) + wsc_ref[2:3, :] * p
        carry[...] = p[tm - 8:tm, :]
        q_ref[...] = q.astype(BF16)
        yb_ref[...] = (zcols(C_BG, D_B, keep=0) * q).astype(BF16)

        av = _dot(ya_ref[...], wb_v[0])
        sa = _sigmoid(zcols(C_GA, D_MODEL) + bgate_ref[:, 0:D_MODEL])
        sa_ref[...] = sa.astype(BF16)
        mg = sa * av
        ca_ref[...] = (mg * (1.0 - sa)).astype(BF16)
        bv = _dot(yb_ref[...], wb_v[1])
        sb = _sigmoid(zcols(C_GB, D_MODEL) + bgate_ref[:, D_MODEL:2 * D_MODEL])
        sb_ref[...] = sb.astype(BF16)
        mb = sb * bv
        cb_ref[...] = (mb * (1.0 - sb)).astype(BF16)
        mg_ref[...] = (mg + mb).astype(BF16)
        x2_ref[...] = x_late_ref[...] + _dot(mg_ref[...], wout_v[...])

    def tile(n, lag):
        return pl.BlockSpec((tm, n), lambda i: (jnp.clip(i - lag, 0, nt - 1), 0))

    outs = [
        jax.ShapeDtypeStruct((t_len, 3 * D_B), BF16),
        jax.ShapeDtypeStruct((t_len, D_A), BF16),
        jax.ShapeDtypeStruct((t_len, D_B), BF16),
        jax.ShapeDtypeStruct((t_len, D_B), BF16),
        jax.ShapeDtypeStruct((t_len, D_MODEL), BF16),
        jax.ShapeDtypeStruct((t_len, D_MODEL), BF16),
        jax.ShapeDtypeStruct((t_len, D_MODEL), BF16),
        jax.ShapeDtypeStruct((t_len, D_MODEL), BF16),
        jax.ShapeDtypeStruct((t_len, D_A), BF16),
        jax.ShapeDtypeStruct((t_len, D_A), BF16),
        jax.ShapeDtypeStruct((t_len, D_A), BF16),
        jax.ShapeDtypeStruct((t_len, D_A), BF16),
        jax.ShapeDtypeStruct((t_len, D_MODEL), BF16),
        jax.ShapeDtypeStruct((t_len, D_MODEL), BF16),
        jax.ShapeDtypeStruct((t_len, D_MODEL), F32),
    ]
    return _staged_call(
        core, name=f"mixer_fwd_l{layer}", grid=(nt + 1,),
        in_specs=[tile(D_MODEL, 0), tile(D_MODEL, 1), _const_spec((1, D_MODEL)), _const_spec((1, 2 * D_MODEL)),
                  _const_spec((1, D_A)), _const_spec((1, D_A)), _const_spec((A_HEADS, 128, 128)),
                  _const_spec((128, D_A)), _const_spec((8, D_B)), ANY, ANY, ANY],
        out_specs=[tile(o.shape[1], 0 if k == len(outs) - 2 else 1) for k, o in enumerate(outs)],
        out_shape=outs,
        scratch_shapes=[pltpu.VMEM((D_MODEL, D_IN), BF16), pltpu.VMEM((2, D_A, D_MODEL), BF16),
                        pltpu.VMEM((D_MODEL, D_MODEL), BF16), pltpu.VMEM((8, D_B), F32),
                        pltpu.VMEM((tm, D_A), BF16), pltpu.VMEM((tm, D_A), F32), pltpu.VMEM((tm, D_IN), F32),
                        pltpu.SemaphoreType.DMA((16,))],
        args=[x, x, g1, bgate, lng, lnb, wm, bsf, wsc, win_g, wb_g, wout_g], stages=stages)


def _ffn_fwd(layer, x2, g2, wfc, bfc, wup_g, wdown_g, stages, head=None):
    t_len = x2.shape[0]
    tm = min(TM_FFN, t_len)
    nt = t_len // tm

    def core(*refs):
        if head is None:
            (x_ref, g2_ref, wfc_ref, bfc_ref, wup_hbm, wdown_hbm, up_ref, silu_ref, dsilu_ref, act_ref, h_ref, x3_ref,
             wup_v, wdown_v, carry, sems) = refs
        else:
            (x_ref, g2_ref, wfc_ref, bfc_ref, t_ref, gf_ref, wup_hbm, wdown_hbm, up_ref, silu_ref, dsilu_ref, act_ref,
             h_ref, dx_ref, dgf_ref, loss_ref, wup_v, wdown_v, carry, sems) = refs
        i = pl.program_id(0)

        @pl.when(i == 0)
        def _():
            cps = _load_col_sharded(wup_hbm, wup_v, sems, 0) + _load_row_sharded(wdown_hbm, wdown_v, sems, 4)
            _start_all(cps)
            carry[...] = jnp.zeros_like(carry)
            if head is not None:
                dgf_ref[...] = jnp.zeros_like(dgf_ref)
                loss_ref[...] = jnp.zeros_like(loss_ref)
            _wait_all(cps)

        xv = x_ref[...]
        r = lax.rsqrt(jnp.mean(xv * xv, axis=-1, keepdims=True) + RMS_EPS)
        h_ref[...] = (xv * r * g2_ref[...]).astype(BF16)
        gate = _dot(h_ref[...], wup_v[:, 0:D_FF])
        up_ref[:, 0:D_FF] = gate.astype(BF16)
        cr = carry[...]
        gc = (wfc_ref[0:1, :] * _shift_down(gate, cr, 2) + wfc_ref[1:2, :] * _shift_down(gate, cr, 1)
              + wfc_ref[2:3, :] * gate + bfc_ref[...])
        carry[...] = gate[tm - 8:tm, :]
        sg = _sigmoid(gc)
        silu = gc * sg
        silu_ref[...] = silu.astype(BF16)
        dsilu_ref[...] = (sg + silu * (1.0 - sg)).astype(BF16)
        val = _dot(h_ref[...], wup_v[:, D_FF:2 * D_FF])
        up_ref[:, D_FF:2 * D_FF] = val.astype(BF16)
        act_ref[...] = (silu * val).astype(BF16)
        x3 = x_ref[...] + _dot(act_ref[...], wdown_v[...])
        if head is None:
            x3_ref[...] = x3
        else:
            r3 = lax.rsqrt(jnp.mean(x3 * x3, axis=-1, keepdims=True) + RMS_EPS)
            xh = x3 * r3
            err = xh * gf_ref[...] - t_ref[...]
            loss_ref[...] += _colsum8(err * err)
            dy = err * (1.0 / D_MODEL)
            dgf_ref[...] += _colsum8(dy * xh)
            dxh = dy * gf_ref[...]
            dx_ref[...] = r3 * (dxh - xh * jnp.mean(dxh * xh, axis=-1, keepdims=True))

    outs = [
        jax.ShapeDtypeStruct((t_len, 2 * D_FF), BF16),
        jax.ShapeDtypeStruct((t_len, D_FF), BF16),
        jax.ShapeDtypeStruct((t_len, D_FF), BF16),
        jax.ShapeDtypeStruct((t_len, D_FF), BF16),
        jax.ShapeDtypeStruct((t_len, D_MODEL), BF16),
        jax.ShapeDtypeStruct((t_len, D_MODEL), F32),
    ]
    in_specs = [_row_spec(tm, D_MODEL), _const_spec((1, D_MODEL)), _const_spec((8, D_FF)), _const_spec((1, D_FF))]
    out_specs = [_row_spec(tm, o.shape[1]) for o in outs]
    args = [x2, g2, wfc, bfc]
    if head is not None:
        in_specs += [_row_spec(tm, D_MODEL), _const_spec((1, D_MODEL))]
        args += list(head)
        outs += [jax.ShapeDtypeStruct((8, D_MODEL), F32)] * 2
        out_specs += [_const_spec((8, D_MODEL))] * 2
    return _staged_call(
        core, name=f"ffn_fwd_l{layer}", grid=(nt,),
        in_specs=in_specs + [ANY, ANY], out_specs=out_specs, out_shape=outs,
        scratch_shapes=[pltpu.VMEM((D_MODEL, 2 * D_FF), BF16), pltpu.VMEM((D_FF, D_MODEL), BF16),
                        pltpu.VMEM((8, D_FF), F32), pltpu.SemaphoreType.DMA((8,))],
        args=args + [wup_g, wdown_g], stages=stages)


def _ffn_bwd(layer, dx3, x2, up, silu, dsilu, g2, wfc, wup_g, wdown_g, stages):
    t_len = x2.shape[0]
    tm = min(TM_FFN, t_len)
    nt = t_len // tm

    def core(dx3_ref, dx3_late_ref, x_ref, up_ref, silu_ref, dsilu_ref, g2_ref, wfc_ref, wup_hbm, wdown_hbm,
             dx2_ref, dup_ref, dx3b_ref, dg2_ref, dbfc_ref, dwfc_ref,
             wup_v, wdown_v, carry, da_s, dup_s, sems):
        i = pl.program_id(0)

        @pl.when(i == 0)
        def _():
            cps = _load_col_sharded(wup_hbm, wup_v, sems, 0) + _load_row_sharded(wdown_hbm, wdown_v, sems, 4)
            _start_all(cps)
            for ref in (carry, da_s, dup_s, dg2_ref, dbfc_ref, dwfc_ref):
                ref[...] = jnp.zeros_like(ref)
            _wait_all(cps)

        live = (i <= nt).astype(F32)
        dx3b_ref[...] = dx3_ref[...].astype(BF16)
        dh = jnp.zeros((tm, D_MODEL), F32)
        for c0, c1 in FF_CHUNKS:
            v0, v1 = D_FF + c0, D_FF + c1
            dh = dh + _dot_nt(dup_s[:, c0:c1], wup_v[:, c0:c1]) + _dot_nt(dup_s[:, v0:v1], wup_v[:, v0:v1])
            da = da_s[:, c0:c1]
            dval = (da * silu_ref[:, c0:c1].astype(F32)).astype(BF16)
            dup_ref[:, v0:v1] = dval
            dup_s[:, v0:v1] = dval
            dgc = da * up_ref[:, v0:v1].astype(F32) * dsilu_ref[:, c0:c1].astype(F32)
            cr = carry[:, c0:c1]
            dgc1 = _shift_up(dgc, cr, 1)
            dgc2 = _shift_up(dgc, cr, 2)
            carry[:, c0:c1] = jnp.where(i < nt, dgc[0:8, :], cr)
            gate = up_ref[:, c0:c1].astype(F32)
            dbfc_ref[:, c0:c1] += live * _colsum8(dgc)
            dwfc_ref[0, :, c0:c1] += live * _colsum8(dgc2 * gate)
            dwfc_ref[1, :, c0:c1] += live * _colsum8(dgc1 * gate)
            dwfc_ref[2, :, c0:c1] += live * _colsum8(dgc * gate)
            dgate = (wfc_ref[2:3, c0:c1] * dgc + wfc_ref[1:2, c0:c1] * dgc1 + wfc_ref[0:1, c0:c1] * dgc2).astype(BF16)
            dup_ref[:, c0:c1] = dgate
            dup_s[:, c0:c1] = dgate
            da_s[:, c0:c1] = _dot_nt(dx3b_ref[...], wdown_v[c0:c1, :])
        xv = x_ref[...]
        r = lax.rsqrt(jnp.mean(xv * xv, axis=-1, keepdims=True) + RMS_EPS)
        xh = xv * r
        dg2_ref[...] += _colsum8(dh * xh)
        dxh = dh * g2_ref[...]
        dx2_ref[...] = dx3_late_ref[...] + r * (dxh - xh * jnp.mean(dxh * xh, axis=-1, keepdims=True))

    def tile(n, lag):
        return pl.BlockSpec((tm, n), lambda i: (nt - 1 - jnp.clip(i - lag, 0, nt - 1), 0))

    outs = [
        jax.ShapeDtypeStruct((t_len, D_MODEL), F32),
        jax.ShapeDtypeStruct((t_len, 2 * D_FF), BF16),
        jax.ShapeDtypeStruct((t_len, D_MODEL), BF16),
        jax.ShapeDtypeStruct((8, D_MODEL), F32),
        jax.ShapeDtypeStruct((8, D_FF), F32),
        jax.ShapeDtypeStruct((3, 8, D_FF), F32),
    ]
    return _staged_call(
        core, name=f"ffn_bwd_l{layer}", grid=(nt + 2,),
        in_specs=[tile(D_MODEL, 0), tile(D_MODEL, 2), tile(D_MODEL, 2), tile(2 * D_FF, 1), tile(D_FF, 1), tile(D_FF, 1),
                  _const_spec((1, D_MODEL)), _const_spec((8, D_FF)), ANY, ANY],
        out_specs=[tile(D_MODEL, 2), tile(2 * D_FF, 1), tile(D_MODEL, 0),
                   _const_spec((8, D_MODEL)), _const_spec((8, D_FF)), _const_spec((3, 8, D_FF))],
        out_shape=outs,
        scratch_shapes=[pltpu.VMEM((D_MODEL, 2 * D_FF), BF16), pltpu.VMEM((D_FF, D_MODEL), BF16),
                        pltpu.VMEM((8, D_FF), F32), pltpu.VMEM((tm, D_FF), F32), pltpu.VMEM((tm, 2 * D_FF), BF16),
                        pltpu.SemaphoreType.DMA((8,))],
        args=[dx3, dx3, x2, up, silu, dsilu, g2, wfc, wup_g, wdown_g], stages=stages)


def _mixer_bwd(layer, dx2, x, zc, qs, sa, ca, sb, cb, ug, fu, xhs, cv, g1, lng, lnb, wmt, wsc, win_g, wb_g, wout_g,
               stages):
    t_len = x.shape[0]
    tm = min(TM_MIX, t_len)
    nt = t_len // tm
    nb = tm // GMLP_BLOCK

    def core(dx2_ref, x_ref, zc_ref, q_ref, sa_ref, ca_ref, sb_ref, cb_ref, ug_ref, fu_ref, xh_ref, cv_ref,
             g1_ref, lng_ref, lnb_ref, wmt_ref, wsc_ref, win_hbm, wb_hbm, wout_hbm,
             dx_ref, dz_ref, da_ref, db_ref, dx2b_ref, dg1_ref, dbgate_ref, dlng_ref, dlnb_ref, dwm_ref, dbsf_ref, dwsc_ref,
             win_v, wb_v, wout_v, carry, vn_s, df_s, dvn_s, sems):
        i = pl.program_id(0)

        @pl.when(i == 0)
        def _():
            cps = (_load_col_sharded(win_hbm, win_v, sems, 0) + _load_branch(wb_hbm, wb_v, sems, 4)
                   + _load_row_sharded(wout_hbm, wout_v, sems, 12))
            _start_all(cps)
            for ref in (carry, dg1_ref, dbgate_ref, dlng_ref, dlnb_ref, dwm_ref, dbsf_ref, dwsc_ref):
                ref[...] = jnp.zeros_like(ref)
            _wait_all(cps)

        def kept(k):
            return zc_ref[:, k * D_B:(k + 1) * D_B].astype(F32)

        def dz_cols(c0, n, val):
            dz_ref[:, c0:c0 + n] = val.astype(BF16)
            return _dot_nt(dz_ref[:, c0:c0 + n], win_v[:, c0:c0 + n])

        dx2b_ref[...] = dx2_ref[...].astype(BF16)
        dm = _dot_nt(dx2b_ref[...], wout_v[...])
        da_ref[...] = (dm * sa_ref[...].astype(F32)).astype(BF16)
        dga = dm * ca_ref[...].astype(F32)
        dh = dz_cols(C_GA, D_MODEL, dga)
        dbgate_ref[:, 0:D_MODEL] += _colsum8(dga)
        dya = _dot_nt(da_ref[...], wb_v[0])
        db_ref[...] = (dm * sb_ref[...].astype(F32)).astype(BF16)
        dgb = dm * cb_ref[...].astype(F32)
        dh = dh + dz_cols(C_GB, D_MODEL, dgb)
        dbgate_ref[:, D_MODEL:2 * D_MODEL] += _colsum8(dgb)
        dyb = _dot_nt(db_ref[...], wb_v[1])

        xh = xh_ref[...].astype(F32)
        vn_s[...] = (xh * lng_ref[...] + lnb_ref[...]).astype(BF16)
        df = dya * ug_ref[...].astype(F32)
        df_s[...] = df.astype(BF16)
        dbsf_acc = df[0:128, :]
        for b in range(1, nb):
            dbsf_acc = dbsf_acc + df[b * 128:(b + 1) * 128, :]
        dbsf_ref[...] += dbsf_acc
        for hd in range(A_HEADS):
            cols = slice(hd * 128, (hd + 1) * 128)
            vcat = jnp.concatenate([vn_s[b * 128:(b + 1) * 128, cols] for b in range(nb)], axis=1)
            dcat = jnp.concatenate([df_s[b * 128:(b + 1) * 128, cols] for b in range(nb)], axis=1)
            gcat = _dot(wmt_ref[hd], dcat)
            dwm_ref[hd] += _dot_nt(dcat, vcat)
            for b in range(nb):
                dvn_s[b * 128:(b + 1) * 128, cols] = gcat[:, b * 128:(b + 1) * 128]
        dh = dh + dz_cols(C_U, D_A, dya * fu_ref[...].astype(F32))
        dvn = dvn_s[...]
        dlng_ref[...] += _colsum8(dvn * xh)
        dlnb_ref[...] += _colsum8(dvn)
        dxh = dvn * lng_ref[...]
        dvc = dxh - jnp.mean(dxh, axis=-1, keepdims=True) - xh * jnp.mean(dxh * xh, axis=-1, keepdims=True)
        dh = dh + dz_cols(C_V, D_A, dvc * cv_ref[...].astype(F32))

        cg = kept(1)
        hbv = kept(2)
        p = cg * hbv
        dh = dh + dz_cols(C_BG, D_B, dyb * q_ref[...].astype(F32))
        dq = dyb * kept(0)
        cr = carry[...]
        dq1 = _shift_up(dq, cr, 1)
        dq2 = _shift_up(dq, cr, 2)
        carry[...] = dq[0:8, :]
        dwsc_ref[0] += _colsum8(dq2 * p)
        dwsc_ref[1] += _colsum8(dq1 * p)
        dwsc_ref[2] += _colsum8(dq * p)
        dp = wsc_ref[2:3, :] * dq + wsc_ref[1:2, :] * dq1 + wsc_ref[0:1, :] * dq2
        dh = dh + dz_cols(C_CG, D_B, dp * hbv)
        dh = dh + dz_cols(C_HB, D_B, dp * cg)

        xv = x_ref[...]
        r = lax.rsqrt(jnp.mean(xv * xv, axis=-1, keepdims=True) + RMS_EPS)
        xn = xv * r
        dg1_ref[...] += _colsum8(dh * xn)
        dxn = dh * g1_ref[...]
        dx_ref[...] = dx2_ref[...] + r * (dxn - xn * jnp.mean(dxn * xn, axis=-1, keepdims=True))

    outs = [
        jax.ShapeDtypeStruct((t_len, D_MODEL), F32),
        jax.ShapeDtypeStruct((t_len, D_IN), BF16),
        jax.ShapeDtypeStruct((t_len, D_MODEL), BF16),
        jax.ShapeDtypeStruct((t_len, D_MODEL), BF16),
        jax.ShapeDtypeStruct((t_len, D_MODEL), BF16),
        jax.ShapeDtypeStruct((8, D_MODEL), F32),
        jax.ShapeDtypeStruct((8, 2 * D_MODEL), F32),
        jax.ShapeDtypeStruct((8, D_A), F32),
        jax.ShapeDtypeStruct((8, D_A), F32),
        jax.ShapeDtypeStruct((A_HEADS, 128, 128), F32),
        jax.ShapeDtypeStruct((128, D_A), F32),
        jax.ShapeDtypeStruct((3, 8, D_B), F32),
    ]

    return _staged_call(
        core, name=f"mixer_bwd_l{layer}", grid=(nt,),
        in_specs=[_row_spec(tm, D_MODEL, nt), _row_spec(tm, D_MODEL, nt), _row_spec(tm, 3 * D_B, nt),
                  _row_spec(tm, D_B, nt), _row_spec(tm, D_MODEL, nt), _row_spec(tm, D_MODEL, nt),
                  _row_spec(tm, D_MODEL, nt), _row_spec(tm, D_MODEL, nt), _row_spec(tm, D_A, nt), _row_spec(tm, D_A, nt),
                  _row_spec(tm, D_A, nt), _row_spec(tm, D_A, nt),
                  _const_spec((1, D_MODEL)), _const_spec((1, D_A)), _const_spec((1, D_A)),
                  _const_spec((A_HEADS, 128, 128)), _const_spec((8, D_B)), ANY, ANY, ANY],
        out_specs=[_row_spec(tm, D_MODEL, nt), _row_spec(tm, D_IN, nt), _row_spec(tm, D_MODEL, nt),
                   _row_spec(tm, D_MODEL, nt), _row_spec(tm, D_MODEL, nt),
                   _const_spec((8, D_MODEL)), _const_spec((8, 2 * D_MODEL)), _const_spec((8, D_A)), _const_spec((8, D_A)),
                   _const_spec((A_HEADS, 128, 128)), _const_spec((128, D_A)), _const_spec((3, 8, D_B))],
        out_shape=outs,
        scratch_shapes=[pltpu.VMEM((D_MODEL, D_IN), BF16), pltpu.VMEM((2, D_A, D_MODEL), BF16),
                        pltpu.VMEM((D_MODEL, D_MODEL), BF16), pltpu.VMEM((8, D_B), F32),
                        pltpu.VMEM((tm, D_A), BF16), pltpu.VMEM((tm, D_A), BF16), pltpu.VMEM((tm, D_A), F32),
                        pltpu.SemaphoreType.DMA((16,))],
        args=[dx2, x, zc, qs, sa, ca, sb, cb, ug, fu, xhs, cv, g1, lng, lnb, wmt, wsc, win_g, wb_g, wout_g],
        stages=stages)


def _wgrad(name, layer, a, b, rows, cols, row_blk, col_blk, stages, a_first=0):
    t_len = a.shape[0]
    n = b.shape[1]
    tk = min(TK_WGRAD, t_len)
    col_sharded = n == N_CHIPS * cols
    m = rows if col_sharded else a.shape[1]
    grid = (m // row_blk, n // col_blk, t_len // tk)
    per_shard_c = cols // col_blk

    if col_sharded:
        out_shape = (N_CHIPS, rows, cols)
        out_spec = pl.BlockSpec((None, row_blk, col_blk), lambda i, j, k: (j // per_shard_c, i, j % per_shard_c))
    else:
        out_shape = (N_CHIPS * rows, cols)
        out_spec = pl.BlockSpec((row_blk, col_blk), lambda i, j, k: (i, j))

    def core(a_ref, b_ref, o_ref):
        @pl.when(pl.program_id(2) == 0)
        def _():
            o_ref[...] = jnp.zeros_like(o_ref)

        o_ref[...] += _dot_tn(a_ref[...], b_ref[...])

    own, outs = _staged_call(
        core, name=f"wgrad_{name}_l{layer}", grid=grid,
        in_specs=[pl.BlockSpec((tk, row_blk), lambda i, j, k: (k, a_first + i)),
                  pl.BlockSpec((tk, col_blk), lambda i, j, k: (k, j))],
        out_specs=[out_spec], out_shape=[jax.ShapeDtypeStruct(out_shape, F32)], scratch_shapes=[],
        args=[a, b], stages=stages)
    return [own[0].reshape(N_CHIPS, rows, cols)], outs


def _wgrad_branch(layer, ya, da, yb, db, stages):
    t_len = ya.shape[0]
    tk = min(TK_WGRAD, t_len)

    def core(ya_ref, da_ref, yb_ref, db_ref, o_ref):
        @pl.when(pl.program_id(1) == 0)
        def _():
            o_ref[...] = jnp.zeros_like(o_ref)

        o_ref[0:D_A, :] += _dot_tn(ya_ref[...], da_ref[...])
        o_ref[D_A:2 * D_A, :] += _dot_tn(yb_ref[...], db_ref[...])

    a_spec = pl.BlockSpec((tk, D_A), lambda j, k: (k, 0))
    d_spec = pl.BlockSpec((tk, 256), lambda j, k: (k, j))
    return _staged_call(
        core, name=f"wgrad_w_branch_l{layer}", grid=(N_CHIPS, t_len // tk),
        in_specs=[a_spec, d_spec, a_spec, d_spec],
        out_specs=[pl.BlockSpec((None, 2 * D_A, 256), lambda j, k: (j, 0, 0))],
        out_shape=[jax.ShapeDtypeStruct((N_CHIPS, 2 * D_A, 256), F32)], scratch_shapes=[],
        args=[ya, da, yb, db], stages=stages)


def _flat_blk(rows, cols):
    blk = rows
    while blk * cols * 4 > 2 * 1024 * 1024 and blk % 16 == 0:
        blk //= 2
    return blk


def _cast_into_slots(name, layer, ws, chip):
    blks = [_flat_blk(w.shape[1], w.shape[2]) for w in ws]
    nblks = [w.shape[1] // b for w, b in zip(ws, blks)]
    n = len(ws)

    def body(chip_ref, *refs):
        for w_ref, o_ref in zip(refs[:n], refs[n:]):
            o_ref[...] = w_ref[...].astype(BF16)

    def in_spec(w, blk, nblk):
        return pl.BlockSpec((None, blk, w.shape[2]), lambda i, chip_ref: (layer, jnp.minimum(i, nblk - 1), 0))

    def out_spec(w, blk, nblk):
        return pl.BlockSpec((None, blk, w.shape[2]), lambda i, chip_ref: (chip_ref[0], jnp.minimum(i, nblk - 1), 0))

    return pl.pallas_call(
        body, name=f"cast_{name}_l{layer}",
        grid_spec=pltpu.PrefetchScalarGridSpec(
            num_scalar_prefetch=1, grid=(max(nblks),),
            in_specs=[in_spec(w, b, k) for w, b, k in zip(ws, blks, nblks)],
            out_specs=[out_spec(w, b, k) for w, b, k in zip(ws, blks, nblks)]),
        out_shape=[jax.ShapeDtypeStruct((N_CHIPS,) + w.shape[1:], BF16) for w in ws],
        compiler_params=_params(),
    )(chip, *ws)


def _reduction_sums(name, jobs, pos):
    in_specs, out_specs, out_shape, args, bodies, counts = [], [], [], [], [], []
    for job in jobs:
        kind, grad, other = job[0], job[1], job[2]
        _, h, cols = other.shape
        blk = _flat_blk(h, cols)
        nblk = h // blk
        if kind == "pair":
            total = N_CHIPS * nblk

            def block(s, total=total, nblk=nblk):
                b = jnp.minimum(s, total - 1)
                return b // nblk, b % nblk

            spec = pl.BlockSpec((None, blk, cols), lambda s, p, block=block: (block(s)[0], block(s)[1], 0))
            in_specs += [pl.BlockSpec((None, blk, cols), lambda s, p, block=block, nblk=nblk:
                                      (block(s)[0], p[1] * nblk + block(s)[1], 0)), spec]
            out_specs.append(spec)
            out_shape.append(jax.ShapeDtypeStruct((N_CHIPS, h, cols), BF16))
            args += [grad, other]
            bodies.append((2, lambda g, o, out: out.__setitem__(..., (g[...] + o[...]).astype(BF16))))
        else:
            total = nblk

            def block(s, total=total):
                return jnp.minimum(s, total - 1)

            in_specs += [pl.BlockSpec((None, blk, cols), lambda s, p, block=block, nblk=nblk:
                                      (p[0], p[1] * nblk + block(s), 0)),
                         pl.BlockSpec((None, blk, cols), lambda s, p, block=block: (p[0], block(s), 0)),
                         pl.BlockSpec((3, blk, cols), lambda s, p, block=block: (0, block(s), 0))]
            out_specs.append(pl.BlockSpec((blk, cols), lambda s, p, block=block, nblk=nblk: (p[1] * nblk + block(s), 0)))
            out_shape.append(jax.ShapeDtypeStruct((2 * h, cols), F32))
            args += [grad, other, job[3]]
            bodies.append((3, lambda g, o, r, out: out.__setitem__(
                ..., (((g[...] + o[...]) + r[0].astype(F32)) + r[1].astype(F32)) + r[2].astype(F32))))
        counts.append(total)

    def body(pos_ref, *refs):
        ins, outs = refs[:len(args)], refs[len(args):]
        k = 0
        for (n_in, fn), out in zip(bodies, outs):
            fn(*ins[k:k + n_in], out)
            k += n_in

    return pl.pallas_call(
        body, name=f"reduction_sums_{name}",
        grid_spec=pltpu.PrefetchScalarGridSpec(num_scalar_prefetch=1, grid=(max(counts),), in_specs=in_specs,
                                               out_specs=out_specs),
        out_shape=out_shape,
        compiler_params=_params(),
    )(pos, *args)


def _sum_slots(name, slots):
    n, rows, _ = slots.shape

    def body(s_ref, o_ref):
        acc = s_ref[0]
        for d in range(1, n):
            acc = acc + s_ref[d]
        o_ref[...] = acc

    return pl.pallas_call(
        body, name=f"sum_slots_{name}", grid=(1,),
        in_specs=[pl.BlockSpec((n, rows, 128), lambda i: (0, 0, 0))],
        out_specs=pl.BlockSpec((rows, 128), lambda i: (0, 0)),
        out_shape=jax.ShapeDtypeStruct((rows, 128), F32),
        compiler_params=_params(),
    )(slots)


def _adamw_math(w, g, m, v):
    m2 = ADAM_B1 * m + (1.0 - ADAM_B1) * g
    v2 = ADAM_B2 * v + (1.0 - ADAM_B2) * (g * g)
    m_hat = m2 / (1.0 - ADAM_B1 ** ADAM_STEP)
    v_hat = v2 / (1.0 - ADAM_B2 ** ADAM_STEP)
    delta = -ADAM_LR * (m_hat / (jnp.sqrt(v_hat) + ADAM_EPS) + ADAM_WD * w)
    return delta, m2, v2


def _adamw_big(name, w, g0, g1, m, v):
    _, rows, cols = w.shape
    blk = _flat_blk(rows, cols) // 2

    def body(w_ref, g0_ref, g1_ref, m_ref, v_ref, g_ref, d_ref, m2_ref, v2_ref):
        g = jnp.where(pl.program_id(0) == 0, g0_ref[...], g1_ref[...])
        d, m2, v2 = _adamw_math(w_ref[...], g, m_ref[...], v_ref[...])
        g_ref[...] = g
        d_ref[...] = d
        m2_ref[...] = m2
        v2_ref[...] = v2

    spec = pl.BlockSpec((None, blk, cols), lambda la, i: (la, i, 0))
    return pl.pallas_call(
        body, name=f"adamw_{name}", grid=(N_LAYERS, rows // blk),
        in_specs=[spec, pl.BlockSpec((blk, cols), lambda la, i: (i * (1 - la), 0)),
                  pl.BlockSpec((blk, cols), lambda la, i: (i * la, 0)), spec, spec],
        out_specs=[spec] * 4,
        out_shape=[jax.ShapeDtypeStruct(w.shape, F32)] * 4,
        compiler_params=_params(("parallel", "parallel")),
    )(w, g0, g1, m, v)


def _adamw_small(ws, gs, ms, vs):
    n = len(ws)

    def body(*refs):
        ins, outs = refs[:4 * n], refs[4 * n:]
        for k in range(n):
            d, m2, v2 = _adamw_math(ins[k][...], ins[n + k][...], ins[2 * n + k][...], ins[3 * n + k][...])
            outs[k][...] = d
            outs[n + k][...] = m2
            outs[2 * n + k][...] = v2

    vmem = pl.BlockSpec(memory_space=pltpu.VMEM)
    return pl.pallas_call(
        body, name="adamw_small",
        in_specs=[vmem] * (4 * n), out_specs=[vmem] * (3 * n),
        out_shape=[jax.ShapeDtypeStruct(w.shape, F32) for w in ws] * 3,
        compiler_params=pltpu.CompilerParams(vmem_limit_bytes=V7X_VMEM_LIMIT),
    )(*ws, *gs, *ms, *vs)


SMALL = ("norm1_g", "b_gate", "gmlp_ln_g", "gmlp_ln_b", "w_spatial", "b_spatial", "w_shortconv", "norm2_g",
         "w_ffn_conv", "b_ffn_conv", "final_g")
ALL_WEIGHTS = ("norm1_g", "w_in", "b_gate", "gmlp_ln_g", "gmlp_ln_b", "w_spatial", "b_spatial", "w_shortconv",
               "w_branch", "w_out", "norm2_g", "w_ffn_up", "w_ffn_conv", "b_ffn_conv", "w_ffn_down", "final_g")


def _pack(arrays):
    flat = jnp.concatenate([a.reshape(-1) for a in arrays])
    n = flat.shape[0]
    rows = -(-n // 1024) * 8
    return jnp.pad(flat, (0, rows * 128 - n)).reshape(rows, 128)


def _unpack(packed, like):
    flat = packed.reshape(-1)
    out, off = [], 0
    for a in like:
        out.append(flat[off:off + a.size].reshape(a.shape))
        off += a.size
    return out


def _pad8(w):
    return jnp.pad(w, ((0, 5), (0, 0)))


def kernel(x, norm1_g, w_in, b_gate, gmlp_ln_g, gmlp_ln_b, w_spatial, b_spatial, w_shortconv, w_branch, w_out, norm2_g, w_ffn_up, w_ffn_conv, b_ffn_conv, w_ffn_down, final_g, loss_target, m_norm1_g, m_w_in, m_b_gate, m_gmlp_ln_g, m_gmlp_ln_b, m_w_spatial, m_b_spatial, m_w_shortconv, m_w_branch, m_w_out, m_norm2_g, m_w_ffn_up, m_w_ffn_conv, m_b_ffn_conv, m_w_ffn_down, m_final_g, v_norm1_g, v_w_in, v_b_gate, v_gmlp_ln_g, v_gmlp_ln_b, v_w_spatial, v_b_spatial, v_w_shortconv, v_w_branch, v_w_out, v_norm2_g, v_w_ffn_up, v_w_ffn_conv, v_b_ffn_conv, v_w_ffn_down, v_final_g):
    weights = dict(norm1_g=norm1_g, w_in=w_in, b_gate=b_gate, gmlp_ln_g=gmlp_ln_g, gmlp_ln_b=gmlp_ln_b,
                   w_spatial=w_spatial, b_spatial=b_spatial, w_shortconv=w_shortconv, w_branch=w_branch, w_out=w_out,
                   norm2_g=norm2_g, w_ffn_up=w_ffn_up, w_ffn_conv=w_ffn_conv, b_ffn_conv=b_ffn_conv,
                   w_ffn_down=w_ffn_down, final_g=final_g)
    mom = dict(norm1_g=m_norm1_g, w_in=m_w_in, b_gate=m_b_gate, gmlp_ln_g=m_gmlp_ln_g, gmlp_ln_b=m_gmlp_ln_b,
               w_spatial=m_w_spatial, b_spatial=m_b_spatial, w_shortconv=m_w_shortconv, w_branch=m_w_branch,
               w_out=m_w_out, norm2_g=m_norm2_g, w_ffn_up=m_w_ffn_up, w_ffn_conv=m_w_ffn_conv,
               b_ffn_conv=m_b_ffn_conv, w_ffn_down=m_w_ffn_down, final_g=m_final_g)
    vel = dict(norm1_g=v_norm1_g, w_in=v_w_in, b_gate=v_b_gate, gmlp_ln_g=v_gmlp_ln_g, gmlp_ln_b=v_gmlp_ln_b,
               w_spatial=v_w_spatial, b_spatial=v_b_spatial, w_shortconv=v_w_shortconv, w_branch=v_w_branch,
               w_out=v_w_out, norm2_g=v_norm2_g, w_ffn_up=v_w_ffn_up, w_ffn_conv=v_w_ffn_conv,
               b_ffn_conv=v_b_ffn_conv, w_ffn_down=v_w_ffn_down, final_g=v_final_g)

    cx, cy, cc = _mesh_pos()
    chip = 2 * cx + cy
    chip_arr = chip.astype(jnp.int32).reshape(1)
    pos_arr = jnp.stack([chip, cc]).astype(jnp.int32)
    t_len = x.shape[1]
    xs = x.reshape(t_len, D_MODEL)
    target = loss_target.reshape(t_len, D_MODEL)
    pipe = _Pipe()

    full = {}

    def gather(group, names, la):
        slots = _cast_into_slots(group, la, [weights[n].reshape((N_LAYERS,) + BIG[n]) for n in names], chip_arr)

        def then(*bufs):
            full.update(zip([(n, la) for n in names], bufs))

        pipe.add(_gather_stage(slots, then))

    mixer_w = ("w_in", "w_branch", "w_out")
    ffn_w = ("w_ffn_up", "w_ffn_down")
    gather("mixer", mixer_w, 0)
    tap_slots = {}
    pipe.add(_chip_spread_stage(_pack([w_shortconv, w_ffn_conv]), lambda slots: tap_slots.__setitem__("all", slots)))
    pipe.flush()
    by_chip = [_unpack(tap_slots["all"][k], [w_shortconv, w_ffn_conv]) for k in range(N_CHIPS)]
    wsc_full = jnp.concatenate([t[0] for t in by_chip], axis=-1)
    wfc_full = jnp.concatenate([t[1] for t in by_chip], axis=-1)

    idx = jnp.arange(GMLP_BLOCK) // CHUNK
    mask = idx[None, :] <= idx[:, None]
    wm_all = jnp.where(mask[None, None], w_spatial, 0.0)
    wm_bf = wm_all.astype(BF16)
    wmt_bf = jnp.swapaxes(wm_all, -1, -2).astype(BF16)
    bsf = jnp.repeat(jnp.swapaxes(b_spatial, -1, -2), 128, axis=-1)

    def row(a):
        return a.reshape(1, -1)

    def mixer_args(la):
        return (row(norm1_g[la]), row(b_gate[la]), row(gmlp_ln_g[la]), row(gmlp_ln_b[la]))

    def mixer_weights(la):
        return tuple(full[(n, la)] for n in mixer_w)

    def ffn_weights(la):
        return tuple(full[(n, la)] for n in ffn_w)

    saved = []
    h_in = xs
    for la in range(N_LAYERS):
        gather("ffn", ffn_w, la)
        *kept, mg, h1, x2 = pipe.carry(lambda st: _mixer_fwd(
            la, h_in, *mixer_args(la), wm_bf[la], bsf[la], _pad8(wsc_full[la]), *mixer_weights(la), st))
        ya, yb = kept[1], kept[2]
        if la + 1 < N_LAYERS:
            gather("mixer", mixer_w, la + 1)
        head = (target, row(final_g)) if la == N_LAYERS - 1 else None
        up, silu, dsilu, act, h2, *rest = pipe.carry(lambda st: _ffn_fwd(
            la, x2, row(norm2_g[la]), _pad8(wfc_full[la]), row(b_ffn_conv[la]), *ffn_weights(la), st, head=head))
        saved.append(dict(x=h_in, ya=ya, yb=yb, mixer=[kept[0]] + kept[3:], mg=mg, h1=h1, x2=x2, up=up, silu=silu,
                          dsilu=dsilu, act=act, h2=h2))
        h_in = rest[0]
    dx, dgf8, loss8 = rest

    reduced_big = {}

    sums_due = []

    def run_sums():
        if sums_due:
            due = list(sums_due)
            sums_due.clear()
            run_sums.calls += 1
            for (_, then), res in zip(due, _reduction_sums(str(run_sums.calls), [job for job, _ in due], pos_arr)):
                then(res)

    run_sums.calls = 0
    pipe.after = run_sums

    def reduce_big(name, la, grad):
        def after_pair(other):
            def after_chips(got):
                sums_due.append((("chip", grad, other, got), lambda final: pipe.add(_pair_fill_stage(
                    final, lambda done: reduced_big.__setitem__((name, la), done)))))

            sums_due.append((("pair", grad, other), lambda psum: pipe.add(_chip_send_stage(psum, after_chips))))

        pipe.add(_pair_send_stage(grad, after_pair))

    small = {n: [None] * N_LAYERS for n in SMALL}
    spread = {}
    for la in reversed(range(N_LAYERS)):
        s = saved[la]
        dx3 = dx
        dx2, dup, dx3b, dg2, dbfc, dwfc = pipe.carry(lambda st: _ffn_bwd(
            la, dx3, s["x2"], s["up"], s["silu"], s["dsilu"], row(norm2_g[la]), _pad8(wfc_full[la]),
            *ffn_weights(la), st))
        g, = pipe.carry(lambda st: _wgrad("w_ffn_down", la, s["act"], dx3b, 704, 1024, 1408, 1024, st))
        reduce_big("w_ffn_down", la, g)
        g, = pipe.carry(lambda st: _wgrad("w_ffn_up", la, s["h2"], dup, 1024, 1408, 1024, 1408, st))
        reduce_big("w_ffn_up", la, g)
        run = pipe.carry if la > 0 else (lambda call: call([])[0])
        dxl, dz, da, db, dx2b, dg1, dbg, dlng, dlnb, dwm, dbsf, dwsc = run(lambda st: _mixer_bwd(
            la, dx2, s["x"], *s["mixer"], row(norm1_g[la]), row(gmlp_ln_g[la]), row(gmlp_ln_b[la]), wmt_bf[la],
            _pad8(wsc_full[la]), *mixer_weights(la), st))
        small["norm1_g"][la] = dg1.sum(0)
        small["b_gate"][la] = dbg.sum(0)
        small["gmlp_ln_g"][la] = dlng.sum(0)
        small["gmlp_ln_b"][la] = dlnb.sum(0)
        small["w_spatial"][la] = jnp.where(mask[None], dwm, 0.0)
        small["b_spatial"][la] = dbsf.reshape(128, A_HEADS, 128).sum(-1).T
        small["w_shortconv"][la] = dwsc.sum(1)
        small["norm2_g"][la] = dg2.sum(0)
        small["w_ffn_conv"][la] = dwfc.sum(1)
        small["b_ffn_conv"][la] = dbfc.sum(0)
        if la == 0:
            small_local = ([jnp.stack(small[n]) for n in SMALL[:-1]]
                           + [dgf8.sum(0), 0.5 * loss8.sum().reshape(1) / D_MODEL])
            mine = _pack(small_local)

            def after_swap(other, mine=mine):
                pair = _sum_slots("small_pair", jnp.stack([mine, other]))
                pipe.add(_chip_spread_stage(pair, lambda slots: spread.__setitem__("slots", slots)))

            pipe.add(_pair_swap_stage(mine, after_swap))
        for part, tag in enumerate(("w_in_a", "w_in_b")):
            g, = pipe.carry(lambda st: _wgrad(tag, la, s["h1"], dz, 512, 1152, 512, 1152, st, a_first=part))
            reduce_big(tag, la, g)
        g, = pipe.carry(lambda st: _wgrad("w_out", la, s["mg"], dx2b, 256, 1024, 1024, 1024, st), long=False)
        reduce_big("w_out", la, g)
        g, = pipe.carry(lambda st: _wgrad_branch(la, s["ya"], da, s["yb"], db, st), long=False)
        reduce_big("w_branch", la, g)
        dx = dxl
    grad_x = dx.reshape(x.shape)
    pipe.flush()

    for la in range(N_LAYERS):
        reduced_big[("w_in", la)] = jnp.concatenate([reduced_big[("w_in_a", la)], reduced_big[("w_in_b", la)]], axis=0)
    reduced = _unpack(_sum_slots("small_grads", spread["slots"]), small_local)
    loss = reduced[-1].reshape(())
    grads = dict(zip(SMALL, reduced[:-1]))
    grads["w_shortconv"] = lax.dynamic_slice(grads["w_shortconv"], (0, 0, chip * (D_B // 4)), (N_LAYERS, 3, D_B // 4))
    grads["w_ffn_conv"] = lax.dynamic_slice(grads["w_ffn_conv"], (0, 0, chip * (D_FF // 4)), (N_LAYERS, 3, D_FF // 4))

    delta, new_m, new_v = {}, {}, {}
    for n in BIG_NAMES:
        shape3 = (N_LAYERS,) + BIG[n]
        res = _adamw_big(n, weights[n].reshape(shape3), reduced_big[(n, 0)], reduced_big[(n, 1)],
                         mom[n].reshape(shape3), vel[n].reshape(shape3))
        grads[n], delta[n], new_m[n], new_v[n] = (a.reshape(weights[n].shape) for a in res)
    res = _adamw_small(*[[src[n].reshape(-1, src[n].shape[-1]) for n in SMALL] for src in (weights, grads, mom, vel)])
    for k, n in enumerate(SMALL):
        delta[n], new_m[n], new_v[n] = (res[j * len(SMALL) + k].reshape(weights[n].shape) for j in range(3))

    return (loss, grad_x, *[grads[n] for n in ALL_WEIGHTS], *[delta[n] for n in ALL_WEIGHTS],
            *[new_m[n] for n in ALL_WEIGHTS], *[new_v[n] for n in ALL_WEIGHTS])
```

```python
import jax
import jax.numpy as jnp
from jax import lax
from jax.experimental import pallas as pl
from jax.experimental.pallas import tpu as pltpu

F32 = jnp.float32
BF16 = jnp.bfloat16
MESH = pl.DeviceIdType.MESH
ANY = pl.BlockSpec(memory_space=pl.ANY)

D_MODEL = 1024
D_A = 512
D_B = 512
D_IN = 4608
D_FF = 2816
GMLP_BLOCK = 128
CHUNK = 64
A_HEADS = 4
N_LAYERS = 2
N_CHIPS = 4
RMS_EPS = 1e-6
LN_EPS = 1e-5
ADAM_LR = 0.001
ADAM_B1 = 0.9
ADAM_B2 = 0.999
ADAM_EPS = 1e-08
ADAM_WD = 0.01
ADAM_STEP = 10

C_U, C_V, C_BG, C_CG, C_HB, C_GA, C_GB = 0, 512, 1024, 1536, 2048, 2560, 3584

V7X_VMEM_LIMIT = 60 * 1024 * 1024
TM_MIX = 256
TM_FFN = 256
TK_WGRAD = 2048
SLOW_COPY_BYTES = 640 * 1024
FF_CHUNKS = ((0, 768), (768, 1536), (1536, 2304), (2304, 2816))
GELU_C0 = 0.7978845608028654
GELU_C1 = 0.044715

BIG = {
    "w_in": (1024, 1152),
    "w_branch": (1024, 256),
    "w_out": (256, 1024),
    "w_ffn_up": (1024, 1408),
    "w_ffn_down": (704, 1024),
}
BIG_NAMES = tuple(BIG)


def _params(sem=("arbitrary",), vmem=V7X_VMEM_LIMIT):
    return pltpu.CompilerParams(dimension_semantics=sem, vmem_limit_bytes=vmem)


def _gelu(x):
    x2 = x * x
    t = jnp.tanh(GELU_C0 * x * (1.0 + GELU_C1 * x2))
    return 0.5 * x * (1.0 + t), t


def _gelu_grad(x, t):
    return 0.5 * (1.0 + t) + 0.5 * x * (1.0 - t * t) * GELU_C0 * (1.0 + 3.0 * GELU_C1 * x * x)


def _colsum8(v):
    r, n = v.shape
    return v.reshape(r // 8, 8, n).sum(axis=0)


def _dot(a, b):
    return jnp.dot(a, b, preferred_element_type=F32)


def _dot_nt(a, b):
    return lax.dot_general(a, b, (((1,), (1,)), ((), ())), preferred_element_type=F32)


def _dot_tn(a, b):
    return lax.dot_general(a, b, (((0,), (0,)), ((), ())), preferred_element_type=F32)


def _shift_down(v, carry, n):
    rows = lax.broadcasted_iota(jnp.int32, (8, v.shape[1]), 0)
    out = pltpu.roll(v, n, 0)
    head = out[0:8, :]
    for r in range(n):
        head = jnp.where(rows == r, carry[8 - n + r:8 - n + r + 1, :], head)
    return jnp.concatenate([head, out[8:, :]], axis=0)


def _shift_up(v, carry, n):
    tm = v.shape[0]
    rows = lax.broadcasted_iota(jnp.int32, (8, v.shape[1]), 0)
    out = pltpu.roll(v, tm - n, 0)
    tail = out[tm - 8:tm, :]
    for r in range(n):
        tail = jnp.where(rows == 8 - n + r, carry[r:r + 1, :], tail)
    return jnp.concatenate([out[0:tm - 8, :], tail], axis=0)


def _sigmoid(x):
    return 0.5 * jnp.tanh(0.5 * x) + 0.5


def _start_all(copies):
    for cp in copies:
        cp.start()


def _wait_all(copies):
    for cp in copies:
        cp.wait()


def _load_col_sharded(src, dst, sems, first):
    cs = src.shape[-1]
    return [pltpu.make_async_copy(src.at[k], dst.at[:, k * cs:(k + 1) * cs], sems.at[first + k])
            for k in range(N_CHIPS)]


def _load_row_sharded(src, dst, sems, first):
    rs = src.shape[-2]
    return [pltpu.make_async_copy(src.at[k], dst.at[k * rs:(k + 1) * rs, :], sems.at[first + k])
            for k in range(N_CHIPS)]


def _load_branch(src, dst, sems, first):
    return [pltpu.make_async_copy(src.at[k, pl.ds(m * D_A, D_A), :], dst.at[m, :, k * 256:(k + 1) * 256],
                                  sems.at[first + 2 * k + m])
            for k in range(N_CHIPS) for m in range(2)]


def _row_spec(tm, n, rev=None):
    if rev is None:
        return pl.BlockSpec((tm, n), lambda i: (i, 0))
    return pl.BlockSpec((tm, n), lambda i: (rev - 1 - i, 0))


def _const_spec(shape):
    nd = len(shape)
    return pl.BlockSpec(shape, lambda i: (0,) * nd)


def _mesh_pos():
    return lax.axis_index("x"), lax.axis_index("y"), lax.axis_index("c")


def _other_chips(x, y):
    return [(1 - x, y, 2 * (1 - x) + y), (x, 1 - y, 2 * x + (1 - y)), (1 - x, 1 - y, 2 * (1 - x) + (1 - y))]


def _remote(src, dst, ssem, rsem, to):
    return pltpu.make_async_remote_copy(src_ref=src, dst_ref=dst, send_sem=ssem, recv_sem=rsem, device_id=to,
                                        device_id_type=MESH)


def _half(ref, which, h):
    start = pl.multiple_of(which * h, 8)
    if len(ref.shape) == 2:
        return ref.at[pl.ds(start, h), :]
    return ref.at[:, pl.ds(start, h), :]


class _Stage:
    def __init__(self, ins=(), inouts=(), outs=(), n_sems=0, start=None, mid=None, finish=None, then=None, slow=False):
        self.ins, self.inouts, self.outs = list(ins), list(inouts), list(outs)
        self.n_sems, self.start, self.mid, self.finish, self.then = n_sems, start, mid, finish, then
        self.slow = slow


def _gather_stage(bufs, then):
    n = len(bufs)

    def copies(io, sem):
        x, y, c = _mesh_pos()
        me = 2 * x + y
        ici, fwd, got = [], [], []
        for w in range(n):
            h = io[w].shape[1] // 2
            for j, (px, py, pk) in enumerate(_other_chips(x, y)):
                mine = _half(io[w].at[me], c, h)
                theirs = _half(io[w].at[pk], c, h)
                ici.append(_remote(mine, mine, sem(12 * w + j), sem(12 * w + 3 + j), (px, py, c)))
                got.append(_remote(theirs, theirs, sem(12 * w + j), sem(12 * w + 3 + j), (px, py, c)))
                fwd.append(_remote(theirs, theirs, sem(12 * w + 6 + j), sem(12 * w + 9 + j), (x, y, 1 - c)))
        return ici, got, fwd

    def start(ins, io, outs, sem):
        _start_all(copies(io, sem)[0])

    def mid(ins, io, outs, sem):
        _, got, fwd = copies(io, sem)
        for g, f in zip(got, fwd):
            g.wait_recv()
            f.start()

    def finish(ins, io, outs, sem):
        x, y, c = _mesh_pos()
        ici, _, fwd = copies(io, sem)
        for w in range(n):
            h = io[w].shape[1] // 2
            for j, (px, py, pk) in enumerate(_other_chips(x, y)):
                other = _half(io[w].at[pk], 1 - c, h)
                _remote(other, other, sem(12 * w + 6 + j), sem(12 * w + 9 + j), (x, y, 1 - c)).wait_recv()
        for cp in ici + fwd:
            cp.wait_send()

    return _Stage(inouts=bufs, n_sems=12 * n, start=start, mid=mid, finish=finish, then=then)


def _pair_send_stage(grad, then):
    h = grad.shape[1] // 2

    def copy(ins, outs, sem):
        x, y, c = _mesh_pos()
        return _remote(_half(ins[0], 1 - c, h), outs[0], sem(0), sem(1), (x, y, 1 - c))

    return _Stage(ins=[grad], outs=[jax.ShapeDtypeStruct((N_CHIPS, h, grad.shape[2]), F32)], n_sems=2,
                  start=lambda ins, io, outs, sem: copy(ins, outs, sem).start(),
                  finish=lambda ins, io, outs, sem: copy(ins, outs, sem).wait(), then=then)


def _chip_send_stage(psum, then):
    def copies(ins, outs, sem):
        x, y, c = _mesh_pos()
        return [_remote(ins[0].at[pk], outs[0].at[j], sem(j), sem(3 + j), (px, py, c))
                for j, (px, py, pk) in enumerate(_other_chips(x, y))]

    return _Stage(ins=[psum], outs=[jax.ShapeDtypeStruct((3,) + psum.shape[1:], BF16)], n_sems=6,
                  start=lambda ins, io, outs, sem: _start_all(copies(ins, outs, sem)),
                  finish=lambda ins, io, outs, sem: _wait_all(copies(ins, outs, sem)), then=then,
                  slow=psum.shape[1] * psum.shape[2] * 2 > SLOW_COPY_BYTES)


def _pair_fill_stage(final, then):
    h = final.shape[0] // 2

    def copy(io, sem):
        x, y, c = _mesh_pos()
        mine = _half(io[0], c, h)
        return _remote(mine, mine, sem(0), sem(1), (x, y, 1 - c))

    return _Stage(inouts=[final], n_sems=2,
                  start=lambda ins, io, outs, sem: copy(io, sem).start(),
                  finish=lambda ins, io, outs, sem: copy(io, sem).wait(), then=then)


def _pair_swap_stage(packed, then):
    def copy(ins, outs, sem):
        x, y, c = _mesh_pos()
        return _remote(ins[0], outs[0], sem(0), sem(1), (x, y, 1 - c))

    return _Stage(ins=[packed], outs=[jax.ShapeDtypeStruct(packed.shape, F32)], n_sems=2,
                  start=lambda ins, io, outs, sem: copy(ins, outs, sem).start(),
                  finish=lambda ins, io, outs, sem: copy(ins, outs, sem).wait(), then=then)


def _chip_spread_stage(psum, then):
    def copies(ins, outs, sem):
        x, y, c = _mesh_pos()
        me = 2 * x + y
        cps = [_remote(ins[0], outs[0].at[me], sem(j), sem(3 + j), (px, py, c))
               for j, (px, py, pk) in enumerate(_other_chips(x, y))]
        return cps, pltpu.make_async_copy(ins[0], outs[0].at[me], sem(6))

    def start(ins, io, outs, sem):
        cps, own = copies(ins, outs, sem)
        own.start()
        _start_all(cps)

    def finish(ins, io, outs, sem):
        cps, own = copies(ins, outs, sem)
        _wait_all(cps)
        own.wait()

    return _Stage(ins=[psum], outs=[jax.ShapeDtypeStruct((N_CHIPS,) + psum.shape, F32)], n_sems=7,
                  start=start, finish=finish, then=then)


def _staged_call(core, *, name, grid, in_specs, out_specs, out_shape, scratch_shapes, args, stages):
    n_in, n_out, n_scr = len(args), len(out_shape), len(scratch_shapes)
    s_args, s_outs, aliases, layout = [], [], {}, []
    n_sems = 0
    for st in stages:
        i0, o0 = len(s_args), len(s_outs)
        s_args += st.ins + st.inouts
        for q in range(len(st.inouts)):
            aliases[n_in + i0 + len(st.ins) + q] = n_out + o0 + q
        s_outs += [jax.ShapeDtypeStruct(a.shape, a.dtype) for a in st.inouts] + st.outs
        layout.append((i0, o0, n_sems))
        n_sems += st.n_sems
    steps = 1
    for g in grid:
        steps *= g

    def body(*refs):
        own_in = refs[:n_in]
        s_in = refs[n_in:n_in + len(s_args)]
        rest = refs[n_in + len(s_args):]
        own_out = rest[:n_out]
        s_out = rest[n_out:n_out + len(s_outs)]
        scr = rest[n_out + len(s_outs):]

        def run(which):
            for st, (i0, o0, s0) in zip(stages, layout):
                fn = getattr(st, which)
                if fn is not None:
                    fn(s_in[i0:i0 + len(st.ins)], s_out[o0:o0 + len(st.inouts)],
                       s_out[o0 + len(st.inouts):o0 + len(st.inouts) + len(st.outs)],
                       lambda k, s0=s0: scr[n_scr].at[s0 + k])

        if not stages:
            core(*own_in, *own_out, *scr[:n_scr])
            return
        step = 0
        for d, g in enumerate(grid):
            step = step * g + pl.program_id(d)
        if steps == 1:
            run("start")
            core(*own_in, *own_out, *scr[:n_scr])
            run("mid")
            run("finish")
            return
        pl.when(step == 0)(lambda: run("start"))
        core(*own_in, *own_out, *scr[:n_scr])
        pl.when(step == (3 * steps) // 4)(lambda: run("mid"))
        pl.when(step == steps - 1)(lambda: run("finish"))

    sem = ("arbitrary",) * len(grid) if stages else ("parallel",) * max(len(grid) - 1, 0) + ("arbitrary",) * min(len(grid), 1)
    res = pl.pallas_call(
        body, name=name, grid=grid,
        in_specs=list(in_specs) + [ANY] * len(s_args),
        out_specs=list(out_specs) + [ANY] * len(s_outs),
        out_shape=list(out_shape) + s_outs,
        input_output_aliases=aliases,
        scratch_shapes=list(scratch_shapes) + ([pltpu.SemaphoreType.DMA((n_sems,))] if stages else []),
        compiler_params=_params(sem) if grid else pltpu.CompilerParams(vmem_limit_bytes=V7X_VMEM_LIMIT),
    )(*args, *s_args)
    return list(res[:n_out]), list(res[n_out:])


class _Pipe:
    def __init__(self):
        self.ready = []
        self.flushes = 0
        self.after = None

    def add(self, stage):
        self.ready.append(stage)

    def carry(self, call, long=True):
        stages = [st for st in self.ready if long or not st.slow]
        self.ready = [st for st in self.ready if not (long or not st.slow)]
        own, outs = call(stages)
        k = 0
        for st in stages:
            n = len(st.inouts) + len(st.outs)
            st.then(*outs[k:k + n])
            k += n
        if self.after is not None:
            self.after()
        return own

    def flush(self):
        while self.ready:
            self.flushes += 1
            self.carry(lambda stages: _staged_call(
                lambda *refs: None, name=f"comm_tail_{self.flushes}", grid=(), in_specs=[], out_specs=[], out_shape=[],
                scratch_shapes=[], args=[], stages=stages))


def _mixer_fwd(layer, x, g1, bgate, lng, lnb, wm, bsf, wsc, win_g, wb_g, wout_g, stages):
    t_len = x.shape[0]
    tm = min(TM_MIX, t_len)
    nt = t_len // tm
    nb = tm // GMLP_BLOCK

    def core(x_ref, x_late_ref, g1_ref, bgate_ref, lng_ref, lnb_ref, wm_ref, bsf_ref, wsc_ref, win_hbm, wb_hbm, wout_hbm,
             zc_ref, ya_ref, yb_ref, q_ref, sa_ref, ca_ref, sb_ref, cb_ref, ug_ref, fu_ref, xh_ref, cv_ref,
             mg_ref, h_ref, x2_ref,
             win_v, wb_v, wout_v, carry, vn_s, f_s, z_s, sems):
        i = pl.program_id(0)

        @pl.when(i == 0)
        def _():
            cps = (_load_col_sharded(win_hbm, win_v, sems, 0) + _load_branch(wb_hbm, wb_v, sems, 4)
                   + _load_row_sharded(wout_hbm, wout_v, sems, 12))
            _start_all(cps)
            carry[...] = jnp.zeros_like(carry)
            z_s[...] = jnp.zeros_like(z_s)
            _wait_all(cps)

        xv = x_ref[...]
        r = lax.rsqrt(jnp.mean(xv * xv, axis=-1, keepdims=True) + RMS_EPS)
        h_ref[...] = (xv * r * g1_ref[...]).astype(BF16)

        def zcols(c0, n, keep=None):
            zv = z_s[:, c0:c0 + n]
            z_s[:, c0:c0 + n] = _dot(h_ref[...], win_v[:, c0:c0 + n])
            if keep is not None:
                zc_ref[:, keep * D_B:(keep + 1) * D_B] = zv.astype(BF16)
            return zv

        v = zcols(C_V, D_A)
        vg, tv = _gelu(v)
        mu = jnp.mean(vg, axis=-1, keepdims=True)
        vc = vg - mu
        rstd = lax.rsqrt(jnp.mean(vc * vc, axis=-1, keepdims=True) + LN_EPS)
        xh = vc * rstd
        xh_ref[...] = xh.astype(BF16)
        cv_ref[...] = (rstd * _gelu_grad(v, tv)).astype(BF16)
        vn_s[...] = (xh * lng_ref[...] + lnb_ref[...]).astype(BF16)
        for hd in range(A_HEADS):
            cols = slice(hd * 128, (hd + 1) * 128)
            vcat = jnp.concatenate([vn_s[b * 128:(b + 1) * 128, cols] for b in range(nb)], axis=1)
            fcat = _dot(wm_ref[hd], vcat)
            for b in range(nb):
                f_s[b * 128:(b + 1) * 128, cols] = fcat[:, b * 128:(b + 1) * 128]
        u = zcols(C_U, D_A)
        ug, tu = _gelu(u)
        ug_ref[...] = ug.astype(BF16)
        fb = f_s[...] + jnp.concatenate([bsf_ref[...]] * nb, axis=0)
        fu_ref[...] = (fb * _gelu_grad(u, tu)).astype(BF16)
        ya_ref[...] = (ug * fb).astype(BF16)

        p = zcols(C_CG, D_B, keep=1) * zcols(C_HB, D_B, keep=2)
        cr = carry[...]
        q = wsc_ref[0:1, :] * _shift_down(p, cr, 2) + wsc_ref[1:2, :] * _shift_down(p, cr, 1) + wsc_ref[2:3, :] * p
        carry[...] = p[tm - 8:tm, :]
        q_ref[...] = q.astype(BF16)
        yb_ref[...] = (zcols(C_BG, D_B, keep=0) * q).astype(BF16)

        av = _dot(ya_ref[...], wb_v[0])
        sa = _sigmoid(zcols(C_GA, D_MODEL) + bgate_ref[:, 0:D_MODEL])
        sa_ref[...] = sa.astype(BF16)
        mg = sa * av
        ca_ref[...] = (mg * (1.0 - sa)).astype(BF16)
        bv = _dot(yb_ref[...], wb_v[1])
        sb = _sigmoid(zcols(C_GB, D_MODEL) + bgate_ref[:, D_MODEL:2 * D_MODEL])
        sb_ref[...] = sb.astype(BF16)
        mb = sb * bv
        cb_ref[...] = (mb * (1.0 - sb)).astype(BF16)
        mg_ref[...] = (mg + mb).astype(BF16)
        x2_ref[...] = x_late_ref[...] + _dot(mg_ref[...], wout_v[...])

    def tile(n, lag):
        return pl.BlockSpec((tm, n), lambda i: (jnp.clip(i - lag, 0, nt - 1), 0))

    outs = [
        jax.ShapeDtypeStruct((t_len, 3 * D_B), BF16),
        jax.ShapeDtypeStruct((t_len, D_A), BF16),
        jax.ShapeDtypeStruct((t_len, D_B), BF16),
        jax.ShapeDtypeStruct((t_len, D_B), BF16),
        jax.ShapeDtypeStruct((t_len, D_MODEL), BF16),
        jax.ShapeDtypeStruct((t_len, D_MODEL), BF16),
        jax.ShapeDtypeStruct((t_len, D_MODEL), BF16),
        jax.ShapeDtypeStruct((t_len, D_MODEL), BF16),
        jax.ShapeDtypeStruct((t_len, D_A), BF16),
        jax.ShapeDtypeStruct((t_len, D_A), BF16),
        jax.ShapeDtypeStruct((t_len, D_A), BF16),
        jax.ShapeDtypeStruct((t_len, D_A), BF16),
        jax.ShapeDtypeStruct((t_len, D_MODEL), BF16),
        jax.ShapeDtypeStruct((t_len, D_MODEL), BF16),
        jax.ShapeDtypeStruct((t_len, D_MODEL), F32),
    ]
    return _staged_call(
        core, name=f"mixer_fwd_l{layer}", grid=(nt + 1,),
        in_specs=[tile(D_MODEL, 0), tile(D_MODEL, 1), _const_spec((1, D_MODEL)), _const_spec((1, 2 * D_MODEL)),
                  _const_spec((1, D_A)), _const_spec((1, D_A)), _const_spec((A_HEADS, 128, 128)),
                  _const_spec((128, D_A)), _const_spec((8, D_B)), ANY, ANY, ANY],
        out_specs=[tile(o.shape[1], 0 if k == len(outs) - 2 else 1) for k, o in enumerate(outs)],
        out_shape=outs,
        scratch_shapes=[pltpu.VMEM((D_MODEL, D_IN), BF16), pltpu.VMEM((2, D_A, D_MODEL), BF16),
                        pltpu.VMEM((D_MODEL, D_MODEL), BF16), pltpu.VMEM((8, D_B), F32),
                        pltpu.VMEM((tm, D_A), BF16), pltpu.VMEM((tm, D_A), F32), pltpu.VMEM((tm, D_IN), F32),
                        pltpu.SemaphoreType.DMA((16,))],
        args=[x, x, g1, bgate, lng, lnb, wm, bsf, wsc, win_g, wb_g, wout_g], stages=stages)


def _ffn_fwd(layer, x2, g2, wfc, bfc, wup_g, wdown_g, stages, head=None):
    t_len = x2.shape[0]
    tm = min(TM_FFN, t_len)
    nt = t_len // tm

    def core(*refs):
        if head is None:
            (x_ref, g2_ref, wfc_ref, bfc_ref, wup_hbm, wdown_hbm, up_ref, silu_ref, dsilu_ref, act_ref, h_ref, x3_ref,
             wup_v, wdown_v, carry, sems) = refs
        else:
            (x_ref, g2_ref, wfc_ref, bfc_ref, t_ref, gf_ref, wup_hbm, wdown_hbm, up_ref, silu_ref, dsilu_ref, act_ref,
             h_ref, dx_ref, dgf_ref, loss_ref, wup_v, wdown_v, carry, sems) = refs
        i = pl.program_id(0)

        @pl.when(i == 0)
        def _():
            cps = _load_col_sharded(wup_hbm, wup_v, sems, 0) + _load_row_sharded(wdown_hbm, wdown_v, sems, 4)
            _start_all(cps)
            carry[...] = jnp.zeros_like(carry)
            if head is not None:
                dgf_ref[...] = jnp.zeros_like(dgf_ref)
                loss_ref[...] = jnp.zeros_like(loss_ref)
            _wait_all(cps)

        xv = x_ref[...]
        r = lax.rsqrt(jnp.mean(xv * xv, axis=-1, keepdims=True) + RMS_EPS)
        h_ref[...] = (xv * r * g2_ref[...]).astype(BF16)
        gate = _dot(h_ref[...], wup_v[:, 0:D_FF])
        up_ref[:, 0:D_FF] = gate.astype(BF16)
        cr = carry[...]
        gc = (wfc_ref[0:1, :] * _shift_down(gate, cr, 2) + wfc_ref[1:2, :] * _shift_down(gate, cr, 1)
              + wfc_ref[2:3, :] * gate + bfc_ref[...])
        carry[...] = gate[tm - 8:tm, :]
        sg = _sigmoid(gc)
        silu = gc * sg
        silu_ref[...] = silu.astype(BF16)
        dsilu_ref[...] = (sg + silu * (1.0 - sg)).astype(BF16)
        val = _dot(h_ref[...], wup_v[:, D_FF:2 * D_FF])
        up_ref[:, D_FF:2 * D_FF] = val.astype(BF16)
        act_ref[...] = (silu * val).astype(BF16)
        x3 = x_ref[...] + _dot(act_ref[...], wdown_v[...])
        if head is None:
            x3_ref[...] = x3
        else:
            r3 = lax.rsqrt(jnp.mean(x3 * x3, axis=-1, keepdims=True) + RMS_EPS)
            xh = x3 * r3
            err = xh * gf_ref[...] - t_ref[...]
            loss_ref[...] += _colsum8(err * err)
            dy = err * (1.0 / D_MODEL)
            dgf_ref[...] += _colsum8(dy * xh)
            dxh = dy * gf_ref[...]
            dx_ref[...] = r3 * (dxh - xh * jnp.mean(dxh * xh, axis=-1, keepdims=True))

    outs = [
        jax.ShapeDtypeStruct((t_len, 2 * D_FF), BF16),
        jax.ShapeDtypeStruct((t_len, D_FF), BF16),
        jax.ShapeDtypeStruct((t_len, D_FF), BF16),
        jax.ShapeDtypeStruct((t_len, D_FF), BF16),
        jax.ShapeDtypeStruct((t_len, D_MODEL), BF16),
        jax.ShapeDtypeStruct((t_len, D_MODEL), F32),
    ]
    in_specs = [_row_spec(tm, D_MODEL), _const_spec((1, D_MODEL)), _const_spec((8, D_FF)), _const_spec((1, D_FF))]
    out_specs = [_row_spec(tm, o.shape[1]) for o in outs]
    args = [x2, g2, wfc, bfc]
    if head is not None:
        in_specs += [_row_spec(tm, D_MODEL), _const_spec((1, D_MODEL))]
        args += list(head)
        outs += [jax.ShapeDtypeStruct((8, D_MODEL), F32)] * 2
        out_specs += [_const_spec((8, D_MODEL))] * 2
    return _staged_call(
        core, name=f"ffn_fwd_l{layer}", grid=(nt,),
        in_specs=in_specs + [ANY, ANY], out_specs=out_specs, out_shape=outs,
        scratch_shapes=[pltpu.VMEM((D_MODEL, 2 * D_FF), BF16), pltpu.VMEM((D_FF, D_MODEL), BF16),
                        pltpu.VMEM((8, D_FF), F32), pltpu.SemaphoreType.DMA((8,))],
        args=args + [wup_g, wdown_g], stages=stages)


def _ffn_bwd(layer, dx3, x2, up, silu, dsilu, g2, wfc, wup_g, wdown_g, stages):
    t_len = x2.shape[0]
    tm = min(TM_FFN, t_len)
    nt = t_len // tm

    def core(dx3_ref, dx3_late_ref, x_ref, up_ref, silu_ref, dsilu_ref, g2_ref, wfc_ref, wup_hbm, wdown_hbm,
             dx2_ref, dup_ref, dx3b_ref, dg2_ref, dbfc_ref, dwfc_ref,
             wup_v, wdown_v, carry, da_s, dup_s, sems):
        i = pl.program_id(0)

        @pl.when(i == 0)
        def _():
            cps = _load_col_sharded(wup_hbm, wup_v, sems, 0) + _load_row_sharded(wdown_hbm, wdown_v, sems, 4)
            _start_all(cps)
            for ref in (carry, da_s, dup_s, dg2_ref, dbfc_ref, dwfc_ref):
                ref[...] = jnp.zeros_like(ref)
            _wait_all(cps)

        live = (i <= nt).astype(F32)
        dx3b_ref[...] = dx3_ref[...].astype(BF16)
        dh = jnp.zeros((tm, D_MODEL), F32)
        for c0, c1 in FF_CHUNKS:
            v0, v1 = D_FF + c0, D_FF + c1
            dh = dh + _dot_nt(dup_s[:, c0:c1], wup_v[:, c0:c1]) + _dot_nt(dup_s[:, v0:v1], wup_v[:, v0:v1])
            da = da_s[:, c0:c1]
            dval = (da * silu_ref[:, c0:c1].astype(F32)).astype(BF16)
            dup_ref[:, v0:v1] = dval
            dup_s[:, v0:v1] = dval
            dgc = da * up_ref[:, v0:v1].astype(F32) * dsilu_ref[:, c0:c1].astype(F32)
            cr = carry[:, c0:c1]
            dgc1 = _shift_up(dgc, cr, 1)
            dgc2 = _shift_up(dgc, cr, 2)
            carry[:, c0:c1] = jnp.where(i < nt, dgc[0:8, :], cr)
            gate = up_ref[:, c0:c1].astype(F32)
            dbfc_ref[:, c0:c1] += live * _colsum8(dgc)
            dwfc_ref[0, :, c0:c1] += live * _colsum8(dgc2 * gate)
            dwfc_ref[1, :, c0:c1] += live * _colsum8(dgc1 * gate)
            dwfc_ref[2, :, c0:c1] += live * _colsum8(dgc * gate)
            dgate = (wfc_ref[2:3, c0:c1] * dgc + wfc_ref[1:2, c0:c1] * dgc1 + wfc_ref[0:1, c0:c1] * dgc2).astype(BF16)
            dup_ref[:, c0:c1] = dgate
            dup_s[:, c0:c1] = dgate
            da_s[:, c0:c1] = _dot_nt(dx3b_ref[...], wdown_v[c0:c1, :])
        xv = x_ref[...]
        r = lax.rsqrt(jnp.mean(xv * xv, axis=-1, keepdims=True) + RMS_EPS)
        xh = xv * r
        dg2_ref[...] += _colsum8(dh * xh)
        dxh = dh * g2_ref[...]
        dx2_ref[...] = dx3_late_ref[...] + r * (dxh - xh * jnp.mean(dxh * xh, axis=-1, keepdims=True))

    def tile(n, lag):
        return pl.BlockSpec((tm, n), lambda i: (nt - 1 - jnp.clip(i - lag, 0, nt - 1), 0))

    outs = [
        jax.ShapeDtypeStruct((t_len, D_MODEL), F32),
        jax.ShapeDtypeStruct((t_len, 2 * D_FF), BF16),
        jax.ShapeDtypeStruct((t_len, D_MODEL), BF16),
        jax.ShapeDtypeStruct((8, D_MODEL), F32),
        jax.ShapeDtypeStruct((8, D_FF), F32),
        jax.ShapeDtypeStruct((3, 8, D_FF), F32),
    ]
    return _staged_call(
        core, name=f"ffn_bwd_l{layer}", grid=(nt + 2,),
        in_specs=[tile(D_MODEL, 0), tile(D_MODEL, 2), tile(D_MODEL, 2), tile(2 * D_FF, 1), tile(D_FF, 1), tile(D_FF, 1),
                  _const_spec((1, D_MODEL)), _const_spec((8, D_FF)), ANY, ANY],
        out_specs=[tile(D_MODEL, 2), tile(2 * D_FF, 1), tile(D_MODEL, 0),
                   _const_spec((8, D_MODEL)), _const_spec((8, D_FF)), _const_spec((3, 8, D_FF))],
        out_shape=outs,
        scratch_shapes=[pltpu.VMEM((D_MODEL, 2 * D_FF), BF16), pltpu.VMEM((D_FF, D_MODEL), BF16),
                        pltpu.VMEM((8, D_FF), F32), pltpu.VMEM((tm, D_FF), F32), pltpu.VMEM((tm, 2 * D_FF), BF16),
                        pltpu.SemaphoreType.DMA((8,))],
        args=[dx3, dx3, x2, up, silu, dsilu, g2, wfc, wup_g, wdown_g], stages=stages)


def _mixer_bwd(layer, dx2, x, zc, qs, sa, ca, sb, cb, ug, fu, xhs, cv, g1, lng, lnb, wmt, wsc, win_g, wb_g, wout_g,
               stages):
    t_len = x.shape[0]
    tm = min(TM_MIX, t_len)
    nt = t_len // tm
    nb = tm // GMLP_BLOCK

    def core(dx2_ref, x_ref, zc_ref, q_ref, sa_ref, ca_ref, sb_ref, cb_ref, ug_ref, fu_ref, xh_ref, cv_ref,
             g1_ref, lng_ref, lnb_ref, wmt_ref, wsc_ref, win_hbm, wb_hbm, wout_hbm,
             dx_ref, dz_ref, da_ref, db_ref, dx2b_ref, dg1_ref, dbgate_ref, dlng_ref, dlnb_ref, dwm_ref, dbsf_ref, dwsc_ref,
             win_v, wb_v, wout_v, carry, vn_s, df_s, dvn_s, sems):
        i = pl.program_id(0)

        @pl.when(i == 0)
        def _():
            cps = (_load_col_sharded(win_hbm, win_v, sems, 0) + _load_branch(wb_hbm, wb_v, sems, 4)
                   + _load_row_sharded(wout_hbm, wout_v, sems, 12))
            _start_all(cps)
            for ref in (carry, dg1_ref, dbgate_ref, dlng_ref, dlnb_ref, dwm_ref, dbsf_ref, dwsc_ref):
                ref[...] = jnp.zeros_like(ref)
            _wait_all(cps)

        def kept(k):
            return zc_ref[:, k * D_B:(k + 1) * D_B].astype(F32)

        def dz_cols(c0, n, val):
            dz_ref[:, c0:c0 + n] = val.astype(BF16)
            return _dot_nt(dz_ref[:, c0:c0 + n], win_v[:, c0:c0 + n])

        dx2b_ref[...] = dx2_ref[...].astype(BF16)
        dm = _dot_nt(dx2b_ref[...], wout_v[...])
        da_ref[...] = (dm * sa_ref[...].astype(F32)).astype(BF16)
        dga = dm * ca_ref[...].astype(F32)
        dh = dz_cols(C_GA, D_MODEL, dga)
        dbgate_ref[:, 0:D_MODEL] += _colsum8(dga)
        dya = _dot_nt(da_ref[...], wb_v[0])
        db_ref[...] = (dm * sb_ref[...].astype(F32)).astype(BF16)
        dgb = dm * cb_ref[...].astype(F32)
        dh = dh + dz_cols(C_GB, D_MODEL, dgb)
        dbgate_ref[:, D_MODEL:2 * D_MODEL] += _colsum8(dgb)
        dyb = _dot_nt(db_ref[...], wb_v[1])

        xh = xh_ref[...].astype(F32)
        vn_s[...] = (xh * lng_ref[...] + lnb_ref[...]).astype(BF16)
        df = dya * ug_ref[...].astype(F32)
        df_s[...] = df.astype(BF16)
        dbsf_acc = df[0:128, :]
        for b in range(1, nb):
            dbsf_acc = dbsf_acc + df[b * 128:(b + 1) * 128, :]
        dbsf_ref[...] += dbsf_acc
        for hd in range(A_HEADS):
            cols = slice(hd * 128, (hd + 1) * 128)
            vcat = jnp.concatenate([vn_s[b * 128:(b + 1) * 128, cols] for b in range(nb)], axis=1)
            dcat = jnp.concatenate([df_s[b * 128:(b + 1) * 128, cols] for b in range(nb)], axis=1)
            gcat = _dot(wmt_ref[hd], dcat)
            dwm_ref[hd] += _dot_nt(dcat, vcat)
            for b in range(nb):
                dvn_s[b * 128:(b + 1) * 128, cols] = gcat[:, b * 128:(b + 1) * 128]
        dh = dh + dz_cols(C_U, D_A, dya * fu_ref[...].astype(F32))
        dvn = dvn_s[...]
        dlng_ref[...] += _colsum8(dvn * xh)
        dlnb_ref[...] += _colsum8(dvn)
        dxh = dvn * lng_ref[...]
        dvc = dxh - jnp.mean(dxh, axis=-1, keepdims=True) - xh * jnp.mean(dxh * xh, axis=-1, keepdims=True)
        dh = dh + dz_cols(C_V, D_A, dvc * cv_ref[...].astype(F32))

        cg = kept(1)
        hbv = kept(2)
        p = cg * hbv
        dh = dh + dz_cols(C_BG, D_B, dyb * q_ref[...].astype(F32))
        dq = dyb * kept(0)
        cr = carry[...]
        dq1 = _shift_up(dq, cr, 1)
        dq2 = _shift_up(dq, cr, 2)
        carry[...] = dq[0:8, :]
        dwsc_ref[0] += _colsum8(dq2 * p)
        dwsc_ref[1] += _colsum8(dq1 * p)
        dwsc_ref[2] += _colsum8(dq * p)
        dp = wsc_ref[2:3, :] * dq + wsc_ref[1:2, :] * dq1 + wsc_ref[0:1, :] * dq2
        dh = dh + dz_cols(C_CG, D_B, dp * hbv)
        dh = dh + dz_cols(C_HB, D_B, dp * cg)

        xv = x_ref[...]
        r = lax.rsqrt(jnp.mean(xv * xv, axis=-1, keepdims=True) + RMS_EPS)
        xn = xv * r
        dg1_ref[...] += _colsum8(dh * xn)
        dxn = dh * g1_ref[...]
        dx_ref[...] = dx2_ref[...] + r * (dxn - xn * jnp.mean(dxn * xn, axis=-1, keepdims=True))

    outs = [
        jax.ShapeDtypeStruct((t_len, D_MODEL), F32),
        jax.ShapeDtypeStruct((t_len, D_IN), BF16),
        jax.ShapeDtypeStruct((t_len, D_MODEL), BF16),
        jax.ShapeDtypeStruct((t_len, D_MODEL), BF16),
        jax.ShapeDtypeStruct((t_len, D_MODEL), BF16),
        jax.ShapeDtypeStruct((8, D_MODEL), F32),
        jax.ShapeDtypeStruct((8, 2 * D_MODEL), F32),
        jax.ShapeDtypeStruct((8, D_A), F32),
        jax.ShapeDtypeStruct((8, D_A), F32),
        jax.ShapeDtypeStruct((A_HEADS, 128, 128), F32),
        jax.ShapeDtypeStruct((128, D_A), F32),
        jax.ShapeDtypeStruct((3, 8, D_B), F32),
    ]

    return _staged_call(
        core, name=f"mixer_bwd_l{layer}", grid=(nt,),
        in_specs=[_row_spec(tm, D_MODEL, nt), _row_spec(tm, D_MODEL, nt), _row_spec(tm, 3 * D_B, nt),
                  _row_spec(tm, D_B, nt), _row_spec(tm, D_MODEL, nt), _row_spec(tm, D_MODEL, nt),
                  _row_spec(tm, D_MODEL, nt), _row_spec(tm, D_MODEL, nt), _row_spec(tm, D_A, nt), _row_spec(tm, D_A, nt),
                  _row_spec(tm, D_A, nt), _row_spec(tm, D_A, nt),
                  _const_spec((1, D_MODEL)), _const_spec((1, D_A)), _const_spec((1, D_A)),
                  _const_spec((A_HEADS, 128, 128)), _const_spec((8, D_B)), ANY, ANY, ANY],
        out_specs=[_row_spec(tm, D_MODEL, nt), _row_spec(tm, D_IN, nt), _row_spec(tm, D_MODEL, nt),
                   _row_spec(tm, D_MODEL, nt), _row_spec(tm, D_MODEL, nt),
                   _const_spec((8, D_MODEL)), _const_spec((8, 2 * D_MODEL)), _const_spec((8, D_A)), _const_spec((8, D_A)),
                   _const_spec((A_HEADS, 128, 128)), _const_spec((128, D_A)), _const_spec((3, 8, D_B))],
        out_shape=outs,
        scratch_shapes=[pltpu.VMEM((D_MODEL, D_IN), BF16), pltpu.VMEM((2, D_A, D_MODEL), BF16),
                        pltpu.VMEM((D_MODEL, D_MODEL), BF16), pltpu.VMEM((8, D_B), F32),
                        pltpu.VMEM((tm, D_A), BF16), pltpu.VMEM((tm, D_A), BF16), pltpu.VMEM((tm, D_A), F32),
                        pltpu.SemaphoreType.DMA((16,))],
        args=[dx2, x, zc, qs, sa, ca, sb, cb, ug, fu, xhs, cv, g1, lng, lnb, wmt, wsc, win_g, wb_g, wout_g],
        stages=stages)


def _wgrad(name, layer, a, b, rows, cols, row_blk, col_blk, stages, a_first=0):
    t_len = a.shape[0]
    n = b.shape[1]
    tk = min(TK_WGRAD, t_len)
    col_sharded = n == N_CHIPS * cols
    m = rows if col_sharded else a.shape[1]
    grid = (m // row_blk, n // col_blk, t_len // tk)
    per_shard_c = cols // col_blk

    if col_sharded:
        out_shape = (N_CHIPS, rows, cols)
        out_spec = pl.BlockSpec((None, row_blk, col_blk), lambda i, j, k: (j // per_shard_c, i, j % per_shard_c))
    else:
        out_shape = (N_CHIPS * rows, cols)
        out_spec = pl.BlockSpec((row_blk, col_blk), lambda i, j, k: (i, j))

    def core(a_ref, b_ref, o_ref):
        @pl.when(pl.program_id(2) == 0)
        def _():
            o_ref[...] = jnp.zeros_like(o_ref)

        o_ref[...] += _dot_tn(a_ref[...], b_ref[...])

    own, outs = _staged_call(
        core, name=f"wgrad_{name}_l{layer}", grid=grid,
        in_specs=[pl.BlockSpec((tk, row_blk), lambda i, j, k: (k, a_first + i)),
                  pl.BlockSpec((tk, col_blk), lambda i, j, k: (k, j))],
        out_specs=[out_spec], out_shape=[jax.ShapeDtypeStruct(out_shape, F32)], scratch_shapes=[],
        args=[a, b], stages=stages)
    return [own[0].reshape(N_CHIPS, rows, cols)], outs


def _wgrad_branch(layer, ya, da, yb, db, stages):
    t_len = ya.shape[0]
    tk = min(TK_WGRAD, t_len)

    def core(ya_ref, da_ref, yb_ref, db_ref, o_ref):
        @pl.when(pl.program_id(1) == 0)
        def _():
            o_ref[...] = jnp.zeros_like(o_ref)

        o_ref[0:D_A, :] += _dot_tn(ya_ref[...], da_ref[...])
        o_ref[D_A:2 * D_A, :] += _dot_tn(yb_ref[...], db_ref[...])

    a_spec = pl.BlockSpec((tk, D_A), lambda j, k: (k, 0))
    d_spec = pl.BlockSpec((tk, 256), lambda j, k: (k, j))
    return _staged_call(
        core, name=f"wgrad_w_branch_l{layer}", grid=(N_CHIPS, t_len // tk),
        in_specs=[a_spec, d_spec, a_spec, d_spec],
        out_specs=[pl.BlockSpec((None, 2 * D_A, 256), lambda j, k: (j, 0, 0))],
        out_shape=[jax.ShapeDtypeStruct((N_CHIPS, 2 * D_A, 256), F32)], scratch_shapes=[],
        args=[ya, da, yb, db], stages=stages)


def _flat_blk(rows, cols):
    blk = rows
    while blk * cols * 4 > 2 * 1024 * 1024 and blk % 16 == 0:
        blk //= 2
    return blk


def _cast_into_slots(name, layer, ws, chip):
    blks = [_flat_blk(w.shape[1], w.shape[2]) for w in ws]
    nblks = [w.shape[1] // b for w, b in zip(ws, blks)]
    n = len(ws)

    def body(chip_ref, *refs):
        for w_ref, o_ref in zip(refs[:n], refs[n:]):
            o_ref[...] = w_ref[...].astype(BF16)

    def in_spec(w, blk, nblk):
        return pl.BlockSpec((None, blk, w.shape[2]), lambda i, chip_ref: (layer, jnp.minimum(i, nblk - 1), 0))

    def out_spec(w, blk, nblk):
        return pl.BlockSpec((None, blk, w.shape[2]), lambda i, chip_ref: (chip_ref[0], jnp.minimum(i, nblk - 1), 0))

    return pl.pallas_call(
        body, name=f"cast_{name}_l{layer}",
        grid_spec=pltpu.PrefetchScalarGridSpec(
            num_scalar_prefetch=1, grid=(max(nblks),),
            in_specs=[in_spec(w, b, k) for w, b, k in zip(ws, blks, nblks)],
            out_specs=[out_spec(w, b, k) for w, b, k in zip(ws, blks, nblks)]),
        out_shape=[jax.ShapeDtypeStruct((N_CHIPS,) + w.shape[1:], BF16) for w in ws],
        compiler_params=_params(),
    )(chip, *ws)


def _reduction_sums(name, jobs, pos):
    in_specs, out_specs, out_shape, args, bodies, counts = [], [], [], [], [], []
    for job in jobs:
        kind, grad, other = job[0], job[1], job[2]
        _, h, cols = other.shape
        blk = _flat_blk(h, cols)
        nblk = h // blk
        if kind == "pair":
            total = N_CHIPS * nblk

            def block(s, total=total, nblk=nblk):
                b = jnp.minimum(s, total - 1)
                return b // nblk, b % nblk

            spec = pl.BlockSpec((None, blk, cols), lambda s, p, block=block: (block(s)[0], block(s)[1], 0))
            in_specs += [pl.BlockSpec((None, blk, cols), lambda s, p, block=block, nblk=nblk:
                                      (block(s)[0], p[1] * nblk + block(s)[1], 0)), spec]
            out_specs.append(spec)
            out_shape.append(jax.ShapeDtypeStruct((N_CHIPS, h, cols), BF16))
            args += [grad, other]
            bodies.append((2, lambda g, o, out: out.__setitem__(..., (g[...] + o[...]).astype(BF16))))
        else:
            total = nblk

            def block(s, total=total):
                return jnp.minimum(s, total - 1)

            in_specs += [pl.BlockSpec((None, blk, cols), lambda s, p, block=block, nblk=nblk:
                                      (p[0], p[1] * nblk + block(s), 0)),
                         pl.BlockSpec((None, blk, cols), lambda s, p, block=block: (p[0], block(s), 0)),
                         pl.BlockSpec((3, blk, cols), lambda s, p, block=block: (0, block(s), 0))]
            out_specs.append(pl.BlockSpec((blk, cols), lambda s, p, block=block, nblk=nblk: (p[1] * nblk + block(s), 0)))
            out_shape.append(jax.ShapeDtypeStruct((2 * h, cols), F32))
            args += [grad, other, job[3]]
            bodies.append((3, lambda g, o, r, out: out.__setitem__(
                ..., (((g[...] + o[...]) + r[0].astype(F32)) + r[1].astype(F32)) + r[2].astype(F32))))
        counts.append(total)

    def body(pos_ref, *refs):
        ins, outs = refs[:len(args)], refs[len(args):]
        k = 0
        for (n_in, fn), out in zip(bodies, outs):
            fn(*ins[k:k + n_in], out)
            k += n_in

    return pl.pallas_call(
        body, name=f"reduction_sums_{name}",
        grid_spec=pltpu.PrefetchScalarGridSpec(num_scalar_prefetch=1, grid=(max(counts),), in_specs=in_specs,
                                               out_specs=out_specs),
        out_shape=out_shape,
        compiler_params=_params(),
    )(pos, *args)


def _sum_slots(name, slots):
    n, rows, _ = slots.shape

    def body(s_ref, o_ref):
        acc = s_ref[0]
        for d in range(1, n):
            acc = acc + s_ref[d]
        o_ref[...] = acc

    return pl.pallas_call(
        body, name=f"sum_slots_{name}", grid=(1,),
        in_specs=[pl.BlockSpec((n, rows, 128), lambda i: (0, 0, 0))],
        out_specs=pl.BlockSpec((rows, 128), lambda i: (0, 0)),
        out_shape=jax.ShapeDtypeStruct((rows, 128), F32),
        compiler_params=_params(),
    )(slots)


def _adamw_math(w, g, m, v):
    m2 = ADAM_B1 * m + (1.0 - ADAM_B1) * g
    v2 = ADAM_B2 * v + (1.0 - ADAM_B2) * (g * g)
    m_hat = m2 / (1.0 - ADAM_B1 ** ADAM_STEP)
    v_hat = v2 / (1.0 - ADAM_B2 ** ADAM_STEP)
    delta = -ADAM_LR * (m_hat / (jnp.sqrt(v_hat) + ADAM_EPS) + ADAM_WD * w)
    return delta, m2, v2


def _adamw_big(name, w, g0, g1, m, v):
    _, rows, cols = w.shape
    blk = _flat_blk(rows, cols) // 2

    def body(w_ref, g0_ref, g1_ref, m_ref, v_ref, g_ref, d_ref, m2_ref, v2_ref):
        g = jnp.where(pl.program_id(0) == 0, g0_ref[...], g1_ref[...])
        d, m2, v2 = _adamw_math(w_ref[...], g, m_ref[...], v_ref[...])
        g_ref[...] = g
        d_ref[...] = d
        m2_ref[...] = m2
        v2_ref[...] = v2

    spec = pl.BlockSpec((None, blk, cols), lambda la, i: (la, i, 0))
    return pl.pallas_call(
        body, name=f"adamw_{name}", grid=(N_LAYERS, rows // blk),
        in_specs=[spec, pl.BlockSpec((blk, cols), lambda la, i: (i * (1 - la), 0)),
                  pl.BlockSpec((blk, cols), lambda la, i: (i * la, 0)), spec, spec],
        out_specs=[spec] * 4,
        out_shape=[jax.ShapeDtypeStruct(w.shape, F32)] * 4,
        compiler_params=_params(("parallel", "parallel")),
    )(w, g0, g1, m, v)


def _adamw_small(ws, gs, ms, vs):
    n = len(ws)

    def body(*refs):
        ins, outs = refs[:4 * n], refs[4 * n:]
        for k in range(n):
            d, m2, v2 = _adamw_math(ins[k][...], ins[n + k][...], ins[2 * n + k][...], ins[3 * n + k][...])
            outs[k][...] = d
            outs[n + k][...] = m2
            outs[2 * n + k][...] = v2

    vmem = pl.BlockSpec(memory_space=pltpu.VMEM)
    return pl.pallas_call(
        body, name="adamw_small",
        in_specs=[vmem] * (4 * n), out_specs=[vmem] * (3 * n),
        out_shape=[jax.ShapeDtypeStruct(w.shape, F32) for w in ws] * 3,
        compiler_params=pltpu.CompilerParams(vmem_limit_bytes=V7X_VMEM_LIMIT),
    )(*ws, *gs, *ms, *vs)


SMALL = ("norm1_g", "b_gate", "gmlp_ln_g", "gmlp_ln_b", "w_spatial", "b_spatial", "w_shortconv", "norm2_g",
         "w_ffn_conv", "b_ffn_conv", "final_g")
ALL_WEIGHTS = ("norm1_g", "w_in", "b_gate", "gmlp_ln_g", "gmlp_ln_b", "w_spatial", "b_spatial", "w_shortconv",
               "w_branch", "w_out", "norm2_g", "w_ffn_up", "w_ffn_conv", "b_ffn_conv", "w_ffn_down", "final_g")


def _pack(arrays):
    flat = jnp.concatenate([a.reshape(-1) for a in arrays])
    n = flat.shape[0]
    rows = -(-n // 1024) * 8
    return jnp.pad(flat, (0, rows * 128 - n)).reshape(rows, 128)


def _unpack(packed, like):
    flat = packed.reshape(-1)
    out, off = [], 0
    for a in like:
        out.append(flat[off:off + a.size].reshape(a.shape))
        off += a.size
    return out


def _pad8(w):
    return jnp.pad(w, ((0, 5), (0, 0)))


def kernel(x, norm1_g, w_in, b_gate, gmlp_ln_g, gmlp_ln_b, w_spatial, b_spatial, w_shortconv, w_branch, w_out, norm2_g, w_ffn_up, w_ffn_conv, b_ffn_conv, w_ffn_down, final_g, loss_target, m_norm1_g, m_w_in, m_b_gate, m_gmlp_ln_g, m_gmlp_ln_b, m_w_spatial, m_b_spatial, m_w_shortconv, m_w_branch, m_w_out, m_norm2_g, m_w_ffn_up, m_w_ffn_conv, m_b_ffn_conv, m_w_ffn_down, m_final_g, v_norm1_g, v_w_in, v_b_gate, v_gmlp_ln_g, v_gmlp_ln_b, v_w_spatial, v_b_spatial, v_w_shortconv, v_w_branch, v_w_out, v_norm2_g, v_w_ffn_up, v_w_ffn_conv, v_b_ffn_conv, v_w_ffn_down, v_final_g):
    weights = dict(norm1_g=norm1_g, w_in=w_in, b_gate=b_gate, gmlp_ln_g=gmlp_ln_g, gmlp_ln_b=gmlp_ln_b,
                   w_spatial=w_spatial, b_spatial=b_spatial, w_shortconv=w_shortconv, w_branch=w_branch, w_out=w_out,
                   norm2_g=norm2_g, w_ffn_up=w_ffn_up, w_ffn_conv=w_ffn_conv, b_ffn_conv=b_ffn_conv,
                   w_ffn_down=w_ffn_down, final_g=final_g)
    mom = dict(norm1_g=m_norm1_g, w_in=m_w_in, b_gate=m_b_gate, gmlp_ln_g=m_gmlp_ln_g, gmlp_ln_b=m_gmlp_ln_b,
               w_spatial=m_w_spatial, b_spatial=m_b_spatial, w_shortconv=m_w_shortconv, w_branch=m_w_branch,
               w_out=m_w_out, norm2_g=m_norm2_g, w_ffn_up=m_w_ffn_up, w_ffn_conv=m_w_ffn_conv,
               b_ffn_conv=m_b_ffn_conv, w_ffn_down=m_w_ffn_down, final_g=m_final_g)
    vel = dict(norm1_g=v_norm1_g, w_in=v_w_in, b_gate=v_b_gate, gmlp_ln_g=v_gmlp_ln_g, gmlp_ln_b=v_gmlp_ln_b,
               w_spatial=v_w_spatial, b_spatial=v_b_spatial, w_shortconv=v_w_shortconv, w_branch=v_w_branch,
               w_out=v_w_out, norm2_g=v_norm2_g, w_ffn_up=v_w_ffn_up, w_ffn_conv=v_w_ffn_conv,
               b_ffn_conv=v_b_ffn_conv, w_ffn_down=v_w_ffn_down, final_g=v_final_g)

    cx, cy, cc = _mesh_pos()
    chip = 2 * cx + cy
    chip_arr = chip.astype(jnp.int32).reshape(1)
    pos_arr = jnp.stack([chip, cc]).astype(jnp.int32)
    t_len = x.shape[1]
    xs = x.reshape(t_len, D_MODEL)
    target = loss_target.reshape(t_len, D_MODEL)
    pipe = _Pipe()

    full = {}

    def gather(group, names, la):
        slots = _cast_into_slots(group, la, [weights[n].reshape((N_LAYERS,) + BIG[n]) for n in names], chip_arr)

        def then(*bufs):
            full.update(zip([(n, la) for n in names], bufs))

        pipe.add(_gather_stage(slots, then))

    mixer_w = ("w_in", "w_branch", "w_out")
    ffn_w = ("w_ffn_up", "w_ffn_down")
    gather("mixer", mixer_w, 0)
    tap_slots = {}
    pipe.add(_chip_spread_stage(_pack([w_shortconv, w_ffn_conv]), lambda slots: tap_slots.__setitem__("all", slots)))
    pipe.flush()
    by_chip = [_unpack(tap_slots["all"][k], [w_shortconv, w_ffn_conv]) for k in range(N_CHIPS)]
    wsc_full = jnp.concatenate([t[0] for t in by_chip], axis=-1)
    wfc_full = jnp.concatenate([t[1] for t in by_chip], axis=-1)

    idx = jnp.arange(GMLP_BLOCK) // CHUNK
    mask = idx[None, :] <= idx[:, None]
    wm_all = jnp.where(mask[None, None], w_spatial, 0.0)
    wm_bf = wm_all.astype(BF16)
    wmt_bf = jnp.swapaxes(wm_all, -1, -2).astype(BF16)
    bsf = jnp.repeat(jnp.swapaxes(b_spatial, -1, -2), 128, axis=-1)

    def row(a):
        return a.reshape(1, -1)

    def mixer_args(la):
        return (row(norm1_g[la]), row(b_gate[la]), row(gmlp_ln_g[la]), row(gmlp_ln_b[la]))

    def mixer_weights(la):
        return tuple(full[(n, la)] for n in mixer_w)

    def ffn_weights(la):
        return tuple(full[(n, la)] for n in ffn_w)

    saved = []
    h_in = xs
    for la in range(N_LAYERS):
        gather("ffn", ffn_w, la)
        *kept, mg, h1, x2 = pipe.carry(lambda st: _mixer_fwd(
            la, h_in, *mixer_args(la), wm_bf[la], bsf[la], _pad8(wsc_full[la]), *mixer_weights(la), st))
        ya, yb = kept[1], kept[2]
        if la + 1 < N_LAYERS:
            gather("mixer", mixer_w, la + 1)
        head = (target, row(final_g)) if la == N_LAYERS - 1 else None
        up, silu, dsilu, act, h2, *rest = pipe.carry(lambda st: _ffn_fwd(
            la, x2, row(norm2_g[la]), _pad8(wfc_full[la]), row(b_ffn_conv[la]), *ffn_weights(la), st, head=head))
        saved.append(dict(x=h_in, ya=ya, yb=yb, mixer=[kept[0]] + kept[3:], mg=mg, h1=h1, x2=x2, up=up, silu=silu,
                          dsilu=dsilu, act=act, h2=h2))
        h_in = rest[0]
    dx, dgf8, loss8 = rest

    reduced_big = {}

    sums_due = []

    def run_sums():
        if sums_due:
            due = list(sums_due)
            sums_due.clear()
            run_sums.calls += 1
            for (_, then), res in zip(due, _reduction_sums(str(run_sums.calls), [job for job, _ in due], pos_arr)):
                then(res)

    run_sums.calls = 0
    pipe.after = run_sums

    def reduce_big(name, la, grad):
        def after_pair(other):
            def after_chips(got):
                sums_due.append((("chip", grad, other, got), lambda final: pipe.add(_pair_fill_stage(
                    final, lambda done: reduced_big.__setitem__((name, la), done)))))

            sums_due.append((("pair", grad, other), lambda psum: pipe.add(_chip_send_stage(psum, after_chips))))

        pipe.add(_pair_send_stage(grad, after_pair))

    small = {n: [None] * N_LAYERS for n in SMALL}
    spread = {}
    for la in reversed(range(N_LAYERS)):
        s = saved[la]
        dx3 = dx
        dx2, dup, dx3b, dg2, dbfc, dwfc = pipe.carry(lambda st: _ffn_bwd(
            la, dx3, s["x2"], s["up"], s["silu"], s["dsilu"], row(norm2_g[la]), _pad8(wfc_full[la]),
            *ffn_weights(la), st))
        g, = pipe.carry(lambda st: _wgrad("w_ffn_down", la, s["act"], dx3b, 704, 1024, 1408, 1024, st))
        reduce_big("w_ffn_down", la, g)
        g, = pipe.carry(lambda st: _wgrad("w_ffn_up", la, s["h2"], dup, 1024, 1408, 1024, 1408, st))
        reduce_big("w_ffn_up", la, g)
        run = pipe.carry if la > 0 else (lambda call: call([])[0])
        dxl, dz, da, db, dx2b, dg1, dbg, dlng, dlnb, dwm, dbsf, dwsc = run(lambda st: _mixer_bwd(
            la, dx2, s["x"], *s["mixer"], row(norm1_g[la]), row(gmlp_ln_g[la]), row(gmlp_ln_b[la]), wmt_bf[la],
            _pad8(wsc_full[la]), *mixer_weights(la), st))
        small["norm1_g"][la] = dg1.sum(0)
        small["b_gate"][la] = dbg.sum(0)
        small["gmlp_ln_g"][la] = dlng.sum(0)
        small["gmlp_ln_b"][la] = dlnb.sum(0)
        small["w_spatial"][la] = jnp.where(mask[None], dwm, 0.0)
        small["b_spatial"][la] = dbsf.reshape(128, A_HEADS, 128).sum(-1).T
        small["w_shortconv"][la] = dwsc.sum(1)
        small["norm2_g"][la] = dg2.sum(0)
        small["w_ffn_conv"][la] = dwfc.sum(1)
        small["b_ffn_conv"][la] = dbfc.sum(0)
        if la == 0:
            small_local = ([jnp.stack(small[n]) for n in SMALL[:-1]]
                           + [dgf8.sum(0), 0.5 * loss8.sum().reshape(1) / D_MODEL])
            mine = _pack(small_local)

            def after_swap(other, mine=mine):
                pair = _sum_slots("small_pair", jnp.stack([mine, other]))
                pipe.add(_chip_spread_stage(pair, lambda slots: spread.__setitem__("slots", slots)))

            pipe.add(_pair_swap_stage(mine, after_swap))
        if la > 0:
            g, = pipe.carry(lambda st: _wgrad("w_in", la, s["h1"], dz, 1024, 1152, 1024, 1152, st))
            reduce_big("w_in", la, g)
        else:
            for part, tag in enumerate(("w_in_a", "w_in_b")):
                g, = pipe.carry(lambda st: _wgrad(tag, la, s["h1"], dz, 512, 1152, 512, 1152, st, a_first=part))
                reduce_big(tag, la, g)
        g, = pipe.carry(lambda st: _wgrad("w_out", la, s["mg"], dx2b, 256, 1024, 1024, 1024, st), long=False)
        reduce_big("w_out", la, g)
        g, = pipe.carry(lambda st: _wgrad_branch(la, s["ya"], da, s["yb"], db, st), long=False)
        reduce_big("w_branch", la, g)
        dx = dxl
    grad_x = dx.reshape(x.shape)
    pipe.flush()

    reduced_big[("w_in", 0)] = jnp.concatenate([reduced_big[("w_in_a", 0)], reduced_big[("w_in_b", 0)]], axis=0)
    reduced = _unpack(_sum_slots("small_grads", spread["slots"]), small_local)
    loss = reduced[-1].reshape(())
    grads = dict(zip(SMALL, reduced[:-1]))
    grads["w_shortconv"] = lax.dynamic_slice(grads["w_shortconv"], (0, 0, chip * (D_B // 4)), (N_LAYERS, 3, D_B // 4))
    grads["w_ffn_conv"] = lax.dynamic_slice(grads["w_ffn_conv"], (0, 0, chip * (D_FF // 4)), (N_LAYERS, 3, D_FF // 4))

    delta, new_m, new_v = {}, {}, {}
    for n in BIG_NAMES:
        shape3 = (N_LAYERS,) + BIG[n]
        res = _adamw_big(n, weights[n].reshape(shape3), reduced_big[(n, 0)], reduced_big[(n, 1)],
                         mom[n].reshape(shape3), vel[n].reshape(shape3))
        grads[n], delta[n], new_m[n], new_v[n] = (a.reshape(weights[n].shape) for a in res)
    res = _adamw_small(*[[src[n].reshape(-1, src[n].shape[-1]) for n in SMALL] for src in (weights, grads, mom, vel)])
    for k, n in enumerate(SMALL):
        delta[n], new_m[n], new_v[n] = (res[j * len(SMALL) + k].reshape(weights[n].shape) for j in range(3))

    return (loss, grad_x, *[grads[n] for n in ALL_WEIGHTS], *[delta[n] for n in ALL_WEIGHTS],
            *[new_m[n] for n in ALL_WEIGHTS], *[new_v[n] for n in ALL_WEIGHTS])
```

```python
import jax
import jax.numpy as jnp
from jax import lax
from jax.experimental import pallas as pl
from jax.experimental.pallas import tpu as pltpu

F32 = jnp.float32
BF16 = jnp.bfloat16
MESH = pl.DeviceIdType.MESH
ANY = pl.BlockSpec(memory_space=pl.ANY)

D_MODEL = 1024
D_A = 512
D_B = 512
D_IN = 4608
D_FF = 2816
GMLP_BLOCK = 128
CHUNK = 64
A_HEADS = 4
N_LAYERS = 2
N_CHIPS = 4
RMS_EPS = 1e-6
LN_EPS = 1e-5
ADAM_LR = 0.001
ADAM_B1 = 0.9
ADAM_B2 = 0.999
ADAM_EPS = 1e-08
ADAM_WD = 0.01
ADAM_STEP = 10

C_U, C_V, C_BG, C_CG, C_HB, C_GA, C_GB = 0, 512, 1024, 1536, 2048, 2560, 3584

V7X_VMEM_LIMIT = 60 * 1024 * 1024
TM_MIX = 256
TM_FFN = 256
TK_WGRAD = 2048
SLOW_COPY_BYTES = 640 * 1024
FF_CHUNKS = ((0, 768), (768, 1536), (1536, 2304), (2304, 2816))
GELU_C0 = 0.7978845608028654
GELU_C1 = 0.044715

BIG = {
    "w_in": (1024, 1152),
    "w_branch": (1024, 256),
    "w_out": (256, 1024),
    "w_ffn_up": (1024, 1408),
    "w_ffn_down": (704, 1024),
}
BIG_NAMES = tuple(BIG)


def _params(sem=("arbitrary",), vmem=V7X_VMEM_LIMIT):
    return pltpu.CompilerParams(dimension_semantics=sem, vmem_limit_bytes=vmem)


def _gelu(x):
    x2 = x * x
    t = jnp.tanh(GELU_C0 * x * (1.0 + GELU_C1 * x2))
    return 0.5 * x * (1.0 + t), t


def _gelu_grad(x, t):
    return 0.5 * (1.0 + t) + 0.5 * x * (1.0 - t * t) * GELU_C0 * (1.0 + 3.0 * GELU_C1 * x * x)


def _colsum8(v):
    r, n = v.shape
    return v.reshape(r // 8, 8, n).sum(axis=0)


def _dot(a, b):
    return jnp.dot(a, b, preferred_element_type=F32)


def _dot_nt(a, b):
    return lax.dot_general(a, b, (((1,), (1,)), ((), ())), preferred_element_type=F32)


def _dot_tn(a, b):
    return lax.dot_general(a, b, (((0,), (0,)), ((), ())), preferred_element_type=F32)


def _shift_down(v, carry, n):
    rows = lax.broadcasted_iota(jnp.int32, (8, v.shape[1]), 0)
    out = pltpu.roll(v, n, 0)
    head = out[0:8, :]
    for r in range(n):
        head = jnp.where(rows == r, carry[8 - n + r:8 - n + r + 1, :], head)
    return jnp.concatenate([head, out[8:, :]], axis=0)


def _shift_up(v, carry, n):
    tm = v.shape[0]
    rows = lax.broadcasted_iota(jnp.int32, (8, v.shape[1]), 0)
    out = pltpu.roll(v, tm - n, 0)
    tail = out[tm - 8:tm, :]
    for r in range(n):
        tail = jnp.where(rows == 8 - n + r, carry[r:r + 1, :], tail)
    return jnp.concatenate([out[0:tm - 8, :], tail], axis=0)


def _sigmoid(x):
    return 0.5 * jnp.tanh(0.5 * x) + 0.5


def _start_all(copies):
    for cp in copies:
        cp.start()


def _wait_all(copies):
    for cp in copies:
        cp.wait()


def _load_col_sharded(src, dst, sems, first):
    cs = src.shape[-1]
    return [pltpu.make_async_copy(src.at[k], dst.at[:, k * cs:(k + 1) * cs], sems.at[first + k])
            for k in range(N_CHIPS)]


def _load_row_sharded(src, dst, sems, first):
    rs = src.shape[-2]
    return [pltpu.make_async_copy(src.at[k], dst.at[k * rs:(k + 1) * rs, :], sems.at[first + k])
            for k in range(N_CHIPS)]


def _load_branch(src, dst, sems, first):
    return [pltpu.make_async_copy(src.at[k, pl.ds(m * D_A, D_A), :], dst.at[m, :, k * 256:(k + 1) * 256],
                                  sems.at[first + 2 * k + m])
            for k in range(N_CHIPS) for m in range(2)]


def _row_spec(tm, n, rev=None):
    if rev is None:
        return pl.BlockSpec((tm, n), lambda i: (i, 0))
    return pl.BlockSpec((tm, n), lambda i: (rev - 1 - i, 0))


def _const_spec(shape):
    nd = len(shape)
    return pl.BlockSpec(shape, lambda i: (0,) * nd)


def _mesh_pos():
    return lax.axis_index("x"), lax.axis_index("y"), lax.axis_index("c")


def _other_chips(x, y):
    return [(1 - x, y, 2 * (1 - x) + y), (x, 1 - y, 2 * x + (1 - y)), (1 - x, 1 - y, 2 * (1 - x) + (1 - y))]


def _remote(src, dst, ssem, rsem, to):
    return pltpu.make_async_remote_copy(src_ref=src, dst_ref=dst, send_sem=ssem, recv_sem=rsem, device_id=to,
                                        device_id_type=MESH)


def _half(ref, which, h):
    start = pl.multiple_of(which * h, 8)
    if len(ref.shape) == 2:
        return ref.at[pl.ds(start, h), :]
    return ref.at[:, pl.ds(start, h), :]


class _Stage:
    def __init__(self, ins=(), inouts=(), outs=(), n_sems=0, start=None, mid=None, finish=None, then=None, slow=False):
        self.ins, self.inouts, self.outs = list(ins), list(inouts), list(outs)
        self.n_sems, self.start, self.mid, self.finish, self.then = n_sems, start, mid, finish, then
        self.slow = slow


def _gather_stage(bufs, then):
    n = len(bufs)

    def copies(io, sem):
        x, y, c = _mesh_pos()
        me = 2 * x + y
        ici, fwd, got = [], [], []
        for w in range(n):
            h = io[w].shape[1] // 2
            for j, (px, py, pk) in enumerate(_other_chips(x, y)):
                mine = _half(io[w].at[me], c, h)
                theirs = _half(io[w].at[pk], c, h)
                ici.append(_remote(mine, mine, sem(12 * w + j), sem(12 * w + 3 + j), (px, py, c)))
                got.append(_remote(theirs, theirs, sem(12 * w + j), sem(12 * w + 3 + j), (px, py, c)))
                fwd.append(_remote(theirs, theirs, sem(12 * w + 6 + j), sem(12 * w + 9 + j), (x, y, 1 - c)))
        return ici, got, fwd

    def start(ins, io, outs, sem):
        _start_all(copies(io, sem)[0])

    def mid(ins, io, outs, sem):
        _, got, fwd = copies(io, sem)
        for g, f in zip(got, fwd):
            g.wait_recv()
            f.start()

    def finish(ins, io, outs, sem):
        x, y, c = _mesh_pos()
        ici, _, fwd = copies(io, sem)
        for w in range(n):
            h = io[w].shape[1] // 2
            for j, (px, py, pk) in enumerate(_other_chips(x, y)):
                other = _half(io[w].at[pk], 1 - c, h)
                _remote(other, other, sem(12 * w + 6 + j), sem(12 * w + 9 + j), (x, y, 1 - c)).wait_recv()
        for cp in ici + fwd:
            cp.wait_send()

    return _Stage(inouts=bufs, n_sems=12 * n, start=start, mid=mid, finish=finish, then=then)


def _pair_send_stage(grad, then):
    h = grad.shape[1] // 2

    def copy(ins, outs, sem):
        x, y, c = _mesh_pos()
        return _remote(_half(ins[0], 1 - c, h), outs[0], sem(0), sem(1), (x, y, 1 - c))

    return _Stage(ins=[grad], outs=[jax.ShapeDtypeStruct((N_CHIPS, h, grad.shape[2]), F32)], n_sems=2,
                  start=lambda ins, io, outs, sem: copy(ins, outs, sem).start(),
                  finish=lambda ins, io, outs, sem: copy(ins, outs, sem).wait(), then=then)


def _chip_send_stage(psum, then):
    def copies(ins, outs, sem):
        x, y, c = _mesh_pos()
        return [_remote(ins[0].at[pk], outs[0].at[j], sem(j), sem(3 + j), (px, py, c))
                for j, (px, py, pk) in enumerate(_other_chips(x, y))]

    return _Stage(ins=[psum], outs=[jax.ShapeDtypeStruct((3,) + psum.shape[1:], BF16)], n_sems=6,
                  start=lambda ins, io, outs, sem: _start_all(copies(ins, outs, sem)),
                  finish=lambda ins, io, outs, sem: _wait_all(copies(ins, outs, sem)), then=then,
                  slow=psum.shape[1] * psum.shape[2] * 2 > SLOW_COPY_BYTES)


def _pair_fill_stage(final, then):
    h = final.shape[0] // 2

    def copy(io, sem):
        x, y, c = _mesh_pos()
        mine = _half(io[0], c, h)
        return _remote(mine, mine, sem(0), sem(1), (x, y, 1 - c))

    return _Stage(inouts=[final], n_sems=2,
                  start=lambda ins, io, outs, sem: copy(io, sem).start(),
                  finish=lambda ins, io, outs, sem: copy(io, sem).wait(), then=then)


def _pair_swap_stage(packed, then):
    def copy(ins, outs, sem):
        x, y, c = _mesh_pos()
        return _remote(ins[0], outs[0], sem(0), sem(1), (x, y, 1 - c))

    return _Stage(ins=[packed], outs=[jax.ShapeDtypeStruct(packed.shape, F32)], n_sems=2,
                  start=lambda ins, io, outs, sem: copy(ins, outs, sem).start(),
                  finish=lambda ins, io, outs, sem: copy(ins, outs, sem).wait(), then=then)


def _chip_spread_stage(psum, then):
    def copies(ins, outs, sem):
        x, y, c = _mesh_pos()
        me = 2 * x + y
        cps = [_remote(ins[0], outs[0].at[me], sem(j), sem(3 + j), (px, py, c))
               for j, (px, py, pk) in enumerate(_other_chips(x, y))]
        return cps, pltpu.make_async_copy(ins[0], outs[0].at[me], sem(6))

    def start(ins, io, outs, sem):
        cps, own = copies(ins, outs, sem)
        own.start()
        _start_all(cps)

    def finish(ins, io, outs, sem):
        cps, own = copies(ins, outs, sem)
        _wait_all(cps)
        own.wait()

    return _Stage(ins=[psum], outs=[jax.ShapeDtypeStruct((N_CHIPS,) + psum.shape, F32)], n_sems=7,
                  start=start, finish=finish, then=then)


def _staged_call(core, *, name, grid, in_specs, out_specs, out_shape, scratch_shapes, args, stages):
    n_in, n_out, n_scr = len(args), len(out_shape), len(scratch_shapes)
    s_args, s_outs, aliases, layout = [], [], {}, []
    n_sems = 0
    for st in stages:
        i0, o0 = len(s_args), len(s_outs)
        s_args += st.ins + st.inouts
        for q in range(len(st.inouts)):
            aliases[n_in + i0 + len(st.ins) + q] = n_out + o0 + q
        s_outs += [jax.ShapeDtypeStruct(a.shape, a.dtype) for a in st.inouts] + st.outs
        layout.append((i0, o0, n_sems))
        n_sems += st.n_sems
    steps = 1
    for g in grid:
        steps *= g

    def body(*refs):
        own_in = refs[:n_in]
        s_in = refs[n_in:n_in + len(s_args)]
        rest = refs[n_in + len(s_args):]
        own_out = rest[:n_out]
        s_out = rest[n_out:n_out + len(s_outs)]
        scr = rest[n_out + len(s_outs):]

        def run(which):
            for st, (i0, o0, s0) in zip(stages, layout):
                fn = getattr(st, which)
                if fn is not None:
                    fn(s_in[i0:i0 + len(st.ins)], s_out[o0:o0 + len(st.inouts)],
                       s_out[o0 + len(st.inouts):o0 + len(st.inouts) + len(st.outs)],
                       lambda k, s0=s0: scr[n_scr].at[s0 + k])

        if not stages:
            core(*own_in, *own_out, *scr[:n_scr])
            return
        step = 0
        for d, g in enumerate(grid):
            step = step * g + pl.program_id(d)
        if steps == 1:
            run("start")
            core(*own_in, *own_out, *scr[:n_scr])
            run("mid")
            run("finish")
            return
        pl.when(step == 0)(lambda: run("start"))
        core(*own_in, *own_out, *scr[:n_scr])
        pl.when(step == (3 * steps) // 4)(lambda: run("mid"))
        pl.when(step == steps - 1)(lambda: run("finish"))

    sem = ("arbitrary",) * len(grid) if stages else ("parallel",) * max(len(grid) - 1, 0) + ("arbitrary",) * min(len(grid), 1)
    res = pl.pallas_call(
        body, name=name, grid=grid,
        in_specs=list(in_specs) + [ANY] * len(s_args),
        out_specs=list(out_specs) + [ANY] * len(s_outs),
        out_shape=list(out_shape) + s_outs,
        input_output_aliases=aliases,
        scratch_shapes=list(scratch_shapes) + ([pltpu.SemaphoreType.DMA((n_sems,))] if stages else []),
        compiler_params=_params(sem) if grid else pltpu.CompilerParams(vmem_limit_bytes=V7X_VMEM_LIMIT),
    )(*args, *s_args)
    return list(res[:n_out]), list(res[n_out:])


class _Pipe:
    def __init__(self):
        self.ready = []
        self.flushes = 0
        self.after = None

    def add(self, stage):
        self.ready.append(stage)

    def carry(self, call, long=True):
        stages = [st for st in self.ready if long or not st.slow]
        self.ready = [st for st in self.ready if not (long or not st.slow)]
        own, outs = call(stages)
        k = 0
        for st in stages:
            n = len(st.inouts) + len(st.outs)
            st.then(*outs[k:k + n])
            k += n
        if self.after is not None:
            self.after()
        return own

    def flush(self):
        while self.ready:
            self.flushes += 1
            self.carry(lambda stages: _staged_call(
                lambda *refs: None, name=f"comm_tail_{self.flushes}", grid=(), in_specs=[], out_specs=[], out_shape=[],
                scratch_shapes=[], args=[], stages=stages))


def _mixer_fwd(layer, x, g1, bgate, lng, lnb, wm, bsf, wsc, win_g, wb_g, wout_g, stages):
    t_len = x.shape[0]
    tm = min(TM_MIX, t_len)
    nt = t_len // tm
    nb = tm // GMLP_BLOCK

    def core(x_ref, x_late_ref, g1_ref, bgate_ref, lng_ref, lnb_ref, wm_ref, bsf_ref, wsc_ref, win_hbm, wb_hbm, wout_hbm,
             zc_ref, ya_ref, yb_ref, q_ref, sa_ref, ca_ref, sb_ref, cb_ref, ug_ref, fu_ref, xh_ref, cv_ref,
             mg_ref, h_ref, x2_ref,
             win_v, wb_v, wout_v, carry, vn_s, f_s, z_s, sems):
        i = pl.program_id(0)

        @pl.when(i == 0)
        def _():
            cps = (_load_col_sharded(win_hbm, win_v, sems, 0) + _load_branch(wb_hbm, wb_v, sems, 4)
                   + _load_row_sharded(wout_hbm, wout_v, sems, 12))
            _start_all(cps)
            carry[...] = jnp.zeros_like(carry)
            z_s[...] = jnp.zeros_like(z_s)
            _wait_all(cps)

        xv = x_ref[...]
        r = lax.rsqrt(jnp.mean(xv * xv, axis=-1, keepdims=True) + RMS_EPS)
        h_ref[...] = (xv * r * g1_ref[...]).astype(BF16)

        def zcols(c0, n, keep=None):
            zv = z_s[:, c0:c0 + n]
            z_s[:, c0:c0 + n] = _dot(h_ref[...], win_v[:, c0:c0 + n])
            if keep is not None:
                zc_ref[:, keep * D_B:(keep + 1) * D_B] = zv.astype(BF16)
            return zv

        v = zcols(C_V, D_A)
        vg, tv = _gelu(v)
        mu = jnp.mean(vg, axis=-1, keepdims=True)
        vc = vg - mu
        rstd = lax.rsqrt(jnp.mean(vc * vc, axis=-1, keepdims=True) + LN_EPS)
        xh = vc * rstd
        xh_ref[...] = xh.astype(BF16)
        cv_ref[...] = (rstd * _gelu_grad(v, tv)).astype(BF16)
        vn_s[...] = (xh * lng_ref[...] + lnb_ref[...]).astype(BF16)
        for hd in range(A_HEADS):
            cols = slice(hd * 128, (hd + 1) * 128)
            vcat = jnp.concatenate([vn_s[b * 128:(b + 1) * 128, cols] for b in range(nb)], axis=1)
            fcat = _dot(wm_ref[hd], vcat)
            for b in range(nb):
                f_s[b * 128:(b + 1) * 128, cols] = fcat[:, b * 128:(b + 1) * 128]
        u = zcols(C_U, D_A)
        ug, tu = _gelu(u)
        ug_ref[...] = ug.astype(BF16)
        fb = f_s[...] + jnp.concatenate([bsf_ref[...]] * nb, axis=0)
        fu_ref[...] = (fb * _gelu_grad(u, tu)).astype(BF16)
        ya_ref[...] = (ug * fb).astype(BF16)

        p = zcols(C_CG, D_B, keep=1) * zcols(C_HB, D_B, keep=2)
        cr = carry[...]
        q = wsc_ref[0:1, :] * _shift_down(p, cr, 2) + wsc_ref[1:2, :] * _shift_down(p, cr, 1) + wsc_ref[2:3, :] * p
        carry[...] = p[tm - 8:tm, :]
        q_ref[...] = q.astype(BF16)
        yb_ref[...] = (zcols(C_BG, D_B, keep=0) * q).astype(BF16)

        av = _dot(ya_ref[...], wb_v[0])
        sa = _sigmoid(zcols(C_GA, D_MODEL) + bgate_ref[:, 0:D_MODEL])
        sa_ref[...] = sa.astype(BF16)
        mg = sa * av
        ca_ref[...] = (mg * (1.0 - sa)).astype(BF16)
        bv = _dot(yb_ref[...], wb_v[1])
        sb = _sigmoid(zcols(C_GB, D_MODEL) + bgate_ref[:, D_MODEL:2 * D_MODEL])
        sb_ref[...] = sb.astype(BF16)
        mb = sb * bv
        cb_ref[...] = (mb * (1.0 - sb)).astype(BF16)
        mg_ref[...] = (mg + mb).astype(BF16)
        x2_ref[...] = x_late_ref[...] + _dot(mg_ref[...], wout_v[...])

    def tile(n, lag):
        return pl.BlockSpec((tm, n), lambda i: (jnp.clip(i - lag, 0, nt - 1), 0))

    outs = [
        jax.ShapeDtypeStruct((t_len, 3 * D_B), BF16),
        jax.ShapeDtypeStruct((t_len, D_A), BF16),
        jax.ShapeDtypeStruct((t_len, D_B), BF16),
        jax.ShapeDtypeStruct((t_len, D_B), BF16),
        jax.ShapeDtypeStruct((t_len, D_MODEL), BF16),
        jax.ShapeDtypeStruct((t_len, D_MODEL), BF16),
        jax.ShapeDtypeStruct((t_len, D_MODEL), BF16),
        jax.ShapeDtypeStruct((t_len, D_MODEL), BF16),
        jax.ShapeDtypeStruct((t_len, D_A), BF16),
        jax.ShapeDtypeStruct((t_len, D_A), BF16),
        jax.ShapeDtypeStruct((t_len, D_A), BF16),
        jax.ShapeDtypeStruct((t_len, D_A), BF16),
        jax.ShapeDtypeStruct((t_len, D_MODEL), BF16),
        jax.ShapeDtypeStruct((t_len, D_MODEL), BF16),
        jax.ShapeDtypeStruct((t_len, D_MODEL), F32),
    ]
    return _staged_call(
        core, name=f"mixer_fwd_l{layer}", grid=(nt + 1,),
        in_specs=[tile(D_MODEL, 0), tile(D_MODEL, 1), _const_spec((1, D_MODEL)), _const_spec((1, 2 * D_MODEL)),
                  _const_spec((1, D_A)), _const_spec((1, D_A)), _const_spec((A_HEADS, 128, 128)),
                  _const_spec((128, D_A)), _const_spec((8, D_B)), ANY, ANY, ANY],
        out_specs=[tile(o.shape[1], 0 if k == len(outs) - 2 else 1) for k, o in enumerate(outs)],
        out_shape=outs,
        scratch_shapes=[pltpu.VMEM((D_MODEL, D_IN), BF16), pltpu.VMEM((2, D_A, D_MODEL), BF16),
                        pltpu.VMEM((D_MODEL, D_MODEL), BF16), pltpu.VMEM((8, D_B), F32),
                        pltpu.VMEM((tm, D_A), BF16), pltpu.VMEM((tm, D_A), F32), pltpu.VMEM((tm, D_IN), F32),
                        pltpu.SemaphoreType.DMA((16,))],
        args=[x, x, g1, bgate, lng, lnb, wm, bsf, wsc, win_g, wb_g, wout_g], stages=stages)


def _ffn_fwd(layer, x2, g2, wfc, bfc, wup_g, wdown_g, stages, head=None):
    t_len = x2.shape[0]
    tm = min(TM_FFN, t_len)
    nt = t_len // tm

    def core(*refs):
        if head is None:
            (x_ref, g2_ref, wfc_ref, bfc_ref, wup_hbm, wdown_hbm, up_ref, silu_ref, dsilu_ref, act_ref, h_ref, x3_ref,
             wup_v, wdown_v, carry, sems) = refs
        else:
            (x_ref, g2_ref, wfc_ref, bfc_ref, t_ref, gf_ref, wup_hbm, wdown_hbm, up_ref, silu_ref, dsilu_ref, act_ref,
             h_ref, dx_ref, dgf_ref, loss_ref, wup_v, wdown_v, carry, sems) = refs
        i = pl.program_id(0)

        @pl.when(i == 0)
        def _():
            cps = _load_col_sharded(wup_hbm, wup_v, sems, 0) + _load_row_sharded(wdown_hbm, wdown_v, sems, 4)
            _start_all(cps)
            carry[...] = jnp.zeros_like(carry)
            if head is not None:
                dgf_ref[...] = jnp.zeros_like(dgf_ref)
                loss_ref[...] = jnp.zeros_like(loss_ref)
            _wait_all(cps)

        xv = x_ref[...]
        r = lax.rsqrt(jnp.mean(xv * xv, axis=-1, keepdims=True) + RMS_EPS)
        h_ref[...] = (xv * r * g2_ref[...]).astype(BF16)
        gate = _dot(h_ref[...], wup_v[:, 0:D_FF])
        up_ref[:, 0:D_FF] = gate.astype(BF16)
        cr = carry[...]
        gc = (wfc_ref[0:1, :] * _shift_down(gate, cr, 2) + wfc_ref[1:2, :] * _shift_down(gate, cr, 1)
              + wfc_ref[2:3, :] * gate + bfc_ref[...])
        carry[...] = gate[tm - 8:tm, :]
        sg = _sigmoid(gc)
        silu = gc * sg
        silu_ref[...] = silu.astype(BF16)
        dsilu_ref[...] = (sg + silu * (1.0 - sg)).astype(BF16)
        val = _dot(h_ref[...], wup_v[:, D_FF:2 * D_FF])
        up_ref[:, D_FF:2 * D_FF] = val.astype(BF16)
        act_ref[...] = (silu * val).astype(BF16)
        x3 = x_ref[...] + _dot(act_ref[...], wdown_v[...])
        if head is None:
            x3_ref[...] = x3
        else:
            r3 = lax.rsqrt(jnp.mean(x3 * x3, axis=-1, keepdims=True) + RMS_EPS)
            xh = x3 * r3
            err = xh * gf_ref[...] - t_ref[...]
            loss_ref[...] += _colsum8(err * err)
            dy = err * (1.0 / D_MODEL)
            dgf_ref[...] += _colsum8(dy * xh)
            dxh = dy * gf_ref[...]
            dx_ref[...] = r3 * (dxh - xh * jnp.mean(dxh * xh, axis=-1, keepdims=True))

    outs = [
        jax.ShapeDtypeStruct((t_len, 2 * D_FF), BF16),
        jax.ShapeDtypeStruct((t_len, D_FF), BF16),
        jax.ShapeDtypeStruct((t_len, D_FF), BF16),
        jax.ShapeDtypeStruct((t_len, D_FF), BF16),
        jax.ShapeDtypeStruct((t_len, D_MODEL), BF16),
        jax.ShapeDtypeStruct((t_len, D_MODEL), F32),
    ]
    in_specs = [_row_spec(tm, D_MODEL), _const_spec((1, D_MODEL)), _const_spec((8, D_FF)), _const_spec((1, D_FF))]
    out_specs = [_row_spec(tm, o.shape[1]) for o in outs]
    args = [x2, g2, wfc, bfc]
    if head is not None:
        in_specs += [_row_spec(tm, D_MODEL), _const_spec((1, D_MODEL))]
        args += list(head)
        outs += [jax.ShapeDtypeStruct((8, D_MODEL), F32)] * 2
        out_specs += [_const_spec((8, D_MODEL))] * 2
    return _staged_call(
        core, name=f"ffn_fwd_l{layer}", grid=(nt,),
        in_specs=in_specs + [ANY, ANY], out_specs=out_specs, out_shape=outs,
        scratch_shapes=[pltpu.VMEM((D_MODEL, 2 * D_FF), BF16), pltpu.VMEM((D_FF, D_MODEL), BF16),
                        pltpu.VMEM((8, D_FF), F32), pltpu.SemaphoreType.DMA((8,))],
        args=args + [wup_g, wdown_g], stages=stages)


def _ffn_bwd(layer, dx3, x2, up, silu, dsilu, g2, wfc, wup_g, wdown_g, stages):
    t_len = x2.shape[0]
    tm = min(TM_FFN, t_len)
    nt = t_len // tm

    def core(dx3_ref, dx3_late_ref, x_ref, up_ref, silu_ref, dsilu_ref, g2_ref, wfc_ref, wup_hbm, wdown_hbm,
             dx2_ref, dup_ref, dx3b_ref, dg2_ref, dbfc_ref, dwfc_ref,
             wup_v, wdown_v, carry, da_s, dup_s, sems):
        i = pl.program_id(0)

        @pl.when(i == 0)
        def _():
            cps = _load_col_sharded(wup_hbm, wup_v, sems, 0) + _load_row_sharded(wdown_hbm, wdown_v, sems, 4)
            _start_all(cps)
            for ref in (carry, da_s, dup_s, dg2_ref, dbfc_ref, dwfc_ref):
                ref[...] = jnp.zeros_like(ref)
            _wait_all(cps)

        live = (i <= nt).astype(F32)
        dx3b_ref[...] = dx3_ref[...].astype(BF16)
        dh = jnp.zeros((tm, D_MODEL), F32)
        for c0, c1 in FF_CHUNKS:
            v0, v1 = D_FF + c0, D_FF + c1
            dh = dh + _dot_nt(dup_s[:, c0:c1], wup_v[:, c0:c1]) + _dot_nt(dup_s[:, v0:v1], wup_v[:, v0:v1])
            da = da_s[:, c0:c1]
            dval = (da * silu_ref[:, c0:c1].astype(F32)).astype(BF16)
            dup_ref[:, v0:v1] = dval
            dup_s[:, v0:v1] = dval
            dgc = da * up_ref[:, v0:v1].astype(F32) * dsilu_ref[:, c0:c1].astype(F32)
            cr = carry[:, c0:c1]
            dgc1 = _shift_up(dgc, cr, 1)
            dgc2 = _shift_up(dgc, cr, 2)
            carry[:, c0:c1] = jnp.where(i < nt, dgc[0:8, :], cr)
            gate = up_ref[:, c0:c1].astype(F32)
            dbfc_ref[:, c0:c1] += live * _colsum8(dgc)
            dwfc_ref[0, :, c0:c1] += live * _colsum8(dgc2 * gate)
            dwfc_ref[1, :, c0:c1] += live * _colsum8(dgc1 * gate)
            dwfc_ref[2, :, c0:c1] += live * _colsum8(dgc * gate)
            dgate = (wfc_ref[2:3, c0:c1] * dgc + wfc_ref[1:2, c0:c1] * dgc1 + wfc_ref[0:1, c0:c1] * dgc2).astype(BF16)
            dup_ref[:, c0:c1] = dgate
            dup_s[:, c0:c1] = dgate
            da_s[:, c0:c1] = _dot_nt(dx3b_ref[...], wdown_v[c0:c1, :])
        xv = x_ref[...]
        r = lax.rsqrt(jnp.mean(xv * xv, axis=-1, keepdims=True) + RMS_EPS)
        xh = xv * r
        dg2_ref[...] += _colsum8(dh * xh)
        dxh = dh * g2_ref[...]
        dx2_ref[...] = dx3_late_ref[...] + r * (dxh - xh * jnp.mean(dxh * xh, axis=-1, keepdims=True))

    def tile(n, lag):
        return pl.BlockSpec((tm, n), lambda i: (nt - 1 - jnp.clip(i - lag, 0, nt - 1), 0))

    outs = [
        jax.ShapeDtypeStruct((t_len, D_MODEL), F32),
        jax.ShapeDtypeStruct((t_len, 2 * D_FF), BF16),
        jax.ShapeDtypeStruct((t_len, D_MODEL), BF16),
        jax.ShapeDtypeStruct((8, D_MODEL), F32),
        jax.ShapeDtypeStruct((8, D_FF), F32),
        jax.ShapeDtypeStruct((3, 8, D_FF), F32),
    ]
    return _staged_call(
        core, name=f"ffn_bwd_l{layer}", grid=(nt + 2,),
        in_specs=[tile(D_MODEL, 0), tile(D_MODEL, 2), tile(D_MODEL, 2), tile(2 * D_FF, 1), tile(D_FF, 1), tile(D_FF, 1),
                  _const_spec((1, D_MODEL)), _const_spec((8, D_FF)), ANY, ANY],
        out_specs=[tile(D_MODEL, 2), tile(2 * D_FF, 1), tile(D_MODEL, 0),
                   _const_spec((8, D_MODEL)), _const_spec((8, D_FF)), _const_spec((3, 8, D_FF))],
        out_shape=outs,
        scratch_shapes=[pltpu.VMEM((D_MODEL, 2 * D_FF), BF16), pltpu.VMEM((D_FF, D_MODEL), BF16),
                        pltpu.VMEM((8, D_FF), F32), pltpu.VMEM((tm, D_FF), F32), pltpu.VMEM((tm, 2 * D_FF), BF16),
                        pltpu.SemaphoreType.DMA((8,))],
        args=[dx3, dx3, x2, up, silu, dsilu, g2, wfc, wup_g, wdown_g], stages=stages)


def _mixer_bwd(layer, dx2, x, zc, qs, sa, ca, sb, cb, ug, fu, xhs, cv, g1, lng, lnb, wmt, wsc, win_g, wb_g, wout_g,
               stages):
    t_len = x.shape[0]
    tm = min(TM_MIX, t_len)
    nt = t_len // tm
    nb = tm // GMLP_BLOCK

    def core(dx2_ref, x_ref, zc_ref, q_ref, sa_ref, ca_ref, sb_ref, cb_ref, ug_ref, fu_ref, xh_ref, cv_ref,
             g1_ref, lng_ref, lnb_ref, wmt_ref, wsc_ref, win_hbm, wb_hbm, wout_hbm,
             dx_ref, dz_ref, da_ref, db_ref, dx2b_ref, dg1_ref, dbgate_ref, dlng_ref, dlnb_ref, dwm_ref, dbsf_ref, dwsc_ref,
             win_v, wb_v, wout_v, carry, vn_s, df_s, dvn_s, sems):
        i = pl.program_id(0)

        @pl.when(i == 0)
        def _():
            cps = (_load_col_sharded(win_hbm, win_v, sems, 0) + _load_branch(wb_hbm, wb_v, sems, 4)
                   + _load_row_sharded(wout_hbm, wout_v, sems, 12))
            _start_all(cps)
            for ref in (carry, dg1_ref, dbgate_ref, dlng_ref, dlnb_ref, dwm_ref, dbsf_ref, dwsc_ref):
                ref[...] = jnp.zeros_like(ref)
            _wait_all(cps)

        def kept(k):
            return zc_ref[:, k * D_B:(k + 1) * D_B].astype(F32)

        def dz_cols(c0, n, val):
            dz_ref[:, c0:c0 + n] = val.astype(BF16)
            return _dot_nt(dz_ref[:, c0:c0 + n], win_v[:, c0:c0 + n])

        dx2b_ref[...] = dx2_ref[...].astype(BF16)
        dm = _dot_nt(dx2b_ref[...], wout_v[...])
        da_ref[...] = (dm * sa_ref[...].astype(F32)).astype(BF16)
        dga = dm * ca_ref[...].astype(F32)
        dh = dz_cols(C_GA, D_MODEL, dga)
        dbgate_ref[:, 0:D_MODEL] += _colsum8(dga)
        dya = _dot_nt(da_ref[...], wb_v[0])
        db_ref[...] = (dm * sb_ref[...].astype(F32)).astype(BF16)
        dgb = dm * cb_ref[...].astype(F32)
        dh = dh + dz_cols(C_GB, D_MODEL, dgb)
        dbgate_ref[:, D_MODEL:2 * D_MODEL] += _colsum8(dgb)
        dyb = _dot_nt(db_ref[...], wb_v[1])

        xh = xh_ref[...].astype(F32)
        vn_s[...] = (xh * lng_ref[...] + lnb_ref[...]).astype(BF16)
        df = dya * ug_ref[...].astype(F32)
        df_s[...] = df.astype(BF16)
        dbsf_acc = df[0:128, :]
        for b in range(1, nb):
            dbsf_acc = dbsf_acc + df[b * 128:(b + 1) * 128, :]
        dbsf_ref[...] += dbsf_acc
        for hd in range(A_HEADS):
            cols = slice(hd * 128, (hd + 1) * 128)
            vcat = jnp.concatenate([vn_s[b * 128:(b + 1) * 128, cols] for b in range(nb)], axis=1)
            dcat = jnp.concatenate([df_s[b * 128:(b + 1) * 128, cols] for b in range(nb)], axis=1)
            gcat = _dot(wmt_ref[hd], dcat)
            dwm_ref[hd] += _dot_nt(dcat, vcat)
            for b in range(nb):
                dvn_s[b * 128:(b + 1) * 128, cols] = gcat[:, b * 128:(b + 1) * 128]
        dh = dh + dz_cols(C_U, D_A, dya * fu_ref[...].astype(F32))
        dvn = dvn_s[...]
        dlng_ref[...] += _colsum8(dvn * xh)
        dlnb_ref[...] += _colsum8(dvn)
        dxh = dvn * lng_ref[...]
        dvc = dxh - jnp.mean(dxh, axis=-1, keepdims=True) - xh * jnp.mean(dxh * xh, axis=-1, keepdims=True)
        dh = dh + dz_cols(C_V, D_A, dvc * cv_ref[...].astype(F32))

        cg = kept(1)
        hbv = kept(2)
        p = cg * hbv
        dh = dh + dz_cols(C_BG, D_B, dyb * q_ref[...].astype(F32))
        dq = dyb * kept(0)
        cr = carry[...]
        dq1 = _shift_up(dq, cr, 1)
        dq2 = _shift_up(dq, cr, 2)
        carry[...] = dq[0:8, :]
        dwsc_ref[0] += _colsum8(dq2 * p)
        dwsc_ref[1] += _colsum8(dq1 * p)
        dwsc_ref[2] += _colsum8(dq * p)
        dp = wsc_ref[2:3, :] * dq + wsc_ref[1:2, :] * dq1 + wsc_ref[0:1, :] * dq2
        dh = dh + dz_cols(C_CG, D_B, dp * hbv)
        dh = dh + dz_cols(C_HB, D_B, dp * cg)

        xv = x_ref[...]
        r = lax.rsqrt(jnp.mean(xv * xv, axis=-1, keepdims=True) + RMS_EPS)
        xn = xv * r
        dg1_ref[...] += _colsum8(dh * xn)
        dxn = dh * g1_ref[...]
        dx_ref[...] = dx2_ref[...] + r * (dxn - xn * jnp.mean(dxn * xn, axis=-1, keepdims=True))

    outs = [
        jax.ShapeDtypeStruct((t_len, D_MODEL), F32),
        jax.ShapeDtypeStruct((t_len, D_IN), BF16),
        jax.ShapeDtypeStruct((t_len, D_MODEL), BF16),
        jax.ShapeDtypeStruct((t_len, D_MODEL), BF16),
        jax.ShapeDtypeStruct((t_len, D_MODEL), BF16),
        jax.ShapeDtypeStruct((8, D_MODEL), F32),
        jax.ShapeDtypeStruct((8, 2 * D_MODEL), F32),
        jax.ShapeDtypeStruct((8, D_A), F32),
        jax.ShapeDtypeStruct((8, D_A), F32),
        jax.ShapeDtypeStruct((A_HEADS, 128, 128), F32),
        jax.ShapeDtypeStruct((128, D_A), F32),
        jax.ShapeDtypeStruct((3, 8, D_B), F32),
    ]

    return _staged_call(
        core, name=f"mixer_bwd_l{layer}", grid=(nt,),
        in_specs=[_row_spec(tm, D_MODEL, nt), _row_spec(tm, D_MODEL, nt), _row_spec(tm, 3 * D_B, nt),
                  _row_spec(tm, D_B, nt), _row_spec(tm, D_MODEL, nt), _row_spec(tm, D_MODEL, nt),
                  _row_spec(tm, D_MODEL, nt), _row_spec(tm, D_MODEL, nt), _row_spec(tm, D_A, nt), _row_spec(tm, D_A, nt),
                  _row_spec(tm, D_A, nt), _row_spec(tm, D_A, nt),
                  _const_spec((1, D_MODEL)), _const_spec((1, D_A)), _const_spec((1, D_A)),
                  _const_spec((A_HEADS, 128, 128)), _const_spec((8, D_B)), ANY, ANY, ANY],
        out_specs=[_row_spec(tm, D_MODEL, nt), _row_spec(tm, D_IN, nt), _row_spec(tm, D_MODEL, nt),
                   _row_spec(tm, D_MODEL, nt), _row_spec(tm, D_MODEL, nt),
                   _const_spec((8, D_MODEL)), _const_spec((8, 2 * D_MODEL)), _const_spec((8, D_A)), _const_spec((8, D_A)),
                   _const_spec((A_HEADS, 128, 128)), _const_spec((128, D_A)), _const_spec((3, 8, D_B))],
        out_shape=outs,
        scratch_shapes=[pltpu.VMEM((D_MODEL, D_IN), BF16), pltpu.VMEM((2, D_A, D_MODEL), BF16),
                        pltpu.VMEM((D_MODEL, D_MODEL), BF16), pltpu.VMEM((8, D_B), F32),
                        pltpu.VMEM((tm, D_A), BF16), pltpu.VMEM((tm, D_A), BF16), pltpu.VMEM((tm, D_A), F32),
                        pltpu.SemaphoreType.DMA((16,))],
        args=[dx2, x, zc, qs, sa, ca, sb, cb, ug, fu, xhs, cv, g1, lng, lnb, wmt, wsc, win_g, wb_g, wout_g],
        stages=stages)


def _wgrad(name, layer, a, b, rows, cols, row_blk, col_blk, stages):
    t_len, m = a.shape
    n = b.shape[1]
    tk = min(TK_WGRAD, t_len)
    col_sharded = n == N_CHIPS * cols
    grid = (m // row_blk, n // col_blk, t_len // tk)
    per_shard_c = cols // col_blk

    if col_sharded:
        out_shape = (N_CHIPS, rows, cols)
        out_spec = pl.BlockSpec((None, row_blk, col_blk), lambda i, j, k: (j // per_shard_c, i, j % per_shard_c))
    else:
        out_shape = (N_CHIPS * rows, cols)
        out_spec = pl.BlockSpec((row_blk, col_blk), lambda i, j, k: (i, j))

    def core(a_ref, b_ref, o_ref):
        @pl.when(pl.program_id(2) == 0)
        def _():
            o_ref[...] = jnp.zeros_like(o_ref)

        o_ref[...] += _dot_tn(a_ref[...], b_ref[...])

    own, outs = _staged_call(
        core, name=f"wgrad_{name}_l{layer}", grid=grid,
        in_specs=[pl.BlockSpec((tk, row_blk), lambda i, j, k: (k, i)), pl.BlockSpec((tk, col_blk), lambda i, j, k: (k, j))],
        out_specs=[out_spec], out_shape=[jax.ShapeDtypeStruct(out_shape, F32)], scratch_shapes=[],
        args=[a, b], stages=stages)
    return [own[0].reshape(N_CHIPS, rows, cols)], outs


def _wgrad_branch(layer, ya, da, yb, db, stages):
    t_len = ya.shape[0]
    tk = min(TK_WGRAD, t_len)

    def core(ya_ref, da_ref, yb_ref, db_ref, o_ref):
        @pl.when(pl.program_id(1) == 0)
        def _():
            o_ref[...] = jnp.zeros_like(o_ref)

        o_ref[0:D_A, :] += _dot_tn(ya_ref[...], da_ref[...])
        o_ref[D_A:2 * D_A, :] += _dot_tn(yb_ref[...], db_ref[...])

    a_spec = pl.BlockSpec((tk, D_A), lambda j, k: (k, 0))
    d_spec = pl.BlockSpec((tk, 256), lambda j, k: (k, j))
    return _staged_call(
        core, name=f"wgrad_w_branch_l{layer}", grid=(N_CHIPS, t_len // tk),
        in_specs=[a_spec, d_spec, a_spec, d_spec],
        out_specs=[pl.BlockSpec((None, 2 * D_A, 256), lambda j, k: (j, 0, 0))],
        out_shape=[jax.ShapeDtypeStruct((N_CHIPS, 2 * D_A, 256), F32)], scratch_shapes=[],
        args=[ya, da, yb, db], stages=stages)


def _flat_blk(rows, cols):
    blk = rows
    while blk * cols * 4 > 2 * 1024 * 1024 and blk % 16 == 0:
        blk //= 2
    return blk


def _cast_into_slots(name, layer, ws, chip):
    blks = [_flat_blk(w.shape[1], w.shape[2]) for w in ws]
    nblks = [w.shape[1] // b for w, b in zip(ws, blks)]
    n = len(ws)

    def body(chip_ref, *refs):
        for w_ref, o_ref in zip(refs[:n], refs[n:]):
            o_ref[...] = w_ref[...].astype(BF16)

    def in_spec(w, blk, nblk):
        return pl.BlockSpec((None, blk, w.shape[2]), lambda i, chip_ref: (layer, jnp.minimum(i, nblk - 1), 0))

    def out_spec(w, blk, nblk):
        return pl.BlockSpec((None, blk, w.shape[2]), lambda i, chip_ref: (chip_ref[0], jnp.minimum(i, nblk - 1), 0))

    return pl.pallas_call(
        body, name=f"cast_{name}_l{layer}",
        grid_spec=pltpu.PrefetchScalarGridSpec(
            num_scalar_prefetch=1, grid=(max(nblks),),
            in_specs=[in_spec(w, b, k) for w, b, k in zip(ws, blks, nblks)],
            out_specs=[out_spec(w, b, k) for w, b, k in zip(ws, blks, nblks)]),
        out_shape=[jax.ShapeDtypeStruct((N_CHIPS,) + w.shape[1:], BF16) for w in ws],
        compiler_params=_params(),
    )(chip, *ws)


def _reduction_sums(name, jobs, pos):
    in_specs, out_specs, out_shape, args, bodies, counts = [], [], [], [], [], []
    for job in jobs:
        kind, grad, other = job[0], job[1], job[2]
        _, h, cols = other.shape
        blk = _flat_blk(h, cols)
        nblk = h // blk
        if kind == "pair":
            total = N_CHIPS * nblk

            def block(s, total=total, nblk=nblk):
                b = jnp.minimum(s, total - 1)
                return b // nblk, b % nblk

            spec = pl.BlockSpec((None, blk, cols), lambda s, p, block=block: (block(s)[0], block(s)[1], 0))
            in_specs += [pl.BlockSpec((None, blk, cols), lambda s, p, block=block, nblk=nblk:
                                      (block(s)[0], p[1] * nblk + block(s)[1], 0)), spec]
            out_specs.append(spec)
            out_shape.append(jax.ShapeDtypeStruct((N_CHIPS, h, cols), BF16))
            args += [grad, other]
            bodies.append((2, lambda g, o, out: out.__setitem__(..., (g[...] + o[...]).astype(BF16))))
        else:
            total = nblk

            def block(s, total=total):
                return jnp.minimum(s, total - 1)

            in_specs += [pl.BlockSpec((None, blk, cols), lambda s, p, block=block, nblk=nblk:
                                      (p[0], p[1] * nblk + block(s), 0)),
                         pl.BlockSpec((None, blk, cols), lambda s, p, block=block: (p[0], block(s), 0)),
                         pl.BlockSpec((3, blk, cols), lambda s, p, block=block: (0, block(s), 0))]
            out_specs.append(pl.BlockSpec((blk, cols), lambda s, p, block=block, nblk=nblk: (p[1] * nblk + block(s), 0)))
            out_shape.append(jax.ShapeDtypeStruct((2 * h, cols), F32))
            args += [grad, other, job[3]]
            bodies.append((3, lambda g, o, r, out: out.__setitem__(
                ..., (((g[...] + o[...]) + r[0].astype(F32)) + r[1].astype(F32)) + r[2].astype(F32))))
        counts.append(total)

    def body(pos_ref, *refs):
        ins, outs = refs[:len(args)], refs[len(args):]
        k = 0
        for (n_in, fn), out in zip(bodies, outs):
            fn(*ins[k:k + n_in], out)
            k += n_in

    return pl.pallas_call(
        body, name=f"reduction_sums_{name}",
        grid_spec=pltpu.PrefetchScalarGridSpec(num_scalar_prefetch=1, grid=(max(counts),), in_specs=in_specs,
                                               out_specs=out_specs),
        out_shape=out_shape,
        compiler_params=_params(),
    )(pos, *args)


def _sum_slots(name, slots):
    n, rows, _ = slots.shape

    def body(s_ref, o_ref):
        acc = s_ref[0]
        for d in range(1, n):
            acc = acc + s_ref[d]
        o_ref[...] = acc

    return pl.pallas_call(
        body, name=f"sum_slots_{name}", grid=(1,),
        in_specs=[pl.BlockSpec((n, rows, 128), lambda i: (0, 0, 0))],
        out_specs=pl.BlockSpec((rows, 128), lambda i: (0, 0)),
        out_shape=jax.ShapeDtypeStruct((rows, 128), F32),
        compiler_params=_params(),
    )(slots)


def _adamw_math(w, g, m, v):
    m2 = ADAM_B1 * m + (1.0 - ADAM_B1) * g
    v2 = ADAM_B2 * v + (1.0 - ADAM_B2) * (g * g)
    m_hat = m2 / (1.0 - ADAM_B1 ** ADAM_STEP)
    v_hat = v2 / (1.0 - ADAM_B2 ** ADAM_STEP)
    delta = -ADAM_LR * (m_hat / (jnp.sqrt(v_hat) + ADAM_EPS) + ADAM_WD * w)
    return delta, m2, v2


def _adamw_big(name, w, g0, g1, m, v):
    _, rows, cols = w.shape
    blk = _flat_blk(rows, cols) // 2

    def body(w_ref, g0_ref, g1_ref, m_ref, v_ref, g_ref, d_ref, m2_ref, v2_ref):
        g = jnp.where(pl.program_id(0) == 0, g0_ref[...], g1_ref[...])
        d, m2, v2 = _adamw_math(w_ref[...], g, m_ref[...], v_ref[...])
        g_ref[...] = g
        d_ref[...] = d
        m2_ref[...] = m2
        v2_ref[...] = v2

    spec = pl.BlockSpec((None, blk, cols), lambda la, i: (la, i, 0))
    return pl.pallas_call(
        body, name=f"adamw_{name}", grid=(N_LAYERS, rows // blk),
        in_specs=[spec, pl.BlockSpec((blk, cols), lambda la, i: (i * (1 - la), 0)),
                  pl.BlockSpec((blk, cols), lambda la, i: (i * la, 0)), spec, spec],
        out_specs=[spec] * 4,
        out_shape=[jax.ShapeDtypeStruct(w.shape, F32)] * 4,
        compiler_params=_params(("parallel", "parallel")),
    )(w, g0, g1, m, v)


def _adamw_small(ws, gs, ms, vs):
    n = len(ws)

    def body(*refs):
        ins, outs = refs[:4 * n], refs[4 * n:]
        for k in range(n):
            d, m2, v2 = _adamw_math(ins[k][...], ins[n + k][...], ins[2 * n + k][...], ins[3 * n + k][...])
            outs[k][...] = d
            outs[n + k][...] = m2
            outs[2 * n + k][...] = v2

    vmem = pl.BlockSpec(memory_space=pltpu.VMEM)
    return pl.pallas_call(
        body, name="adamw_small",
        in_specs=[vmem] * (4 * n), out_specs=[vmem] * (3 * n),
        out_shape=[jax.ShapeDtypeStruct(w.shape, F32) for w in ws] * 3,
        compiler_params=pltpu.CompilerParams(vmem_limit_bytes=V7X_VMEM_LIMIT),
    )(*ws, *gs, *ms, *vs)


SMALL = ("norm1_g", "b_gate", "gmlp_ln_g", "gmlp_ln_b", "w_spatial", "b_spatial", "w_shortconv", "norm2_g",
         "w_ffn_conv", "b_ffn_conv", "final_g")
ALL_WEIGHTS = ("norm1_g", "w_in", "b_gate", "gmlp_ln_g", "gmlp_ln_b", "w_spatial", "b_spatial", "w_shortconv",
               "w_branch", "w_out", "norm2_g", "w_ffn_up", "w_ffn_conv", "b_ffn_conv", "w_ffn_down", "final_g")


def _pack(arrays):
    flat = jnp.concatenate([a.reshape(-1) for a in arrays])
    n = flat.shape[0]
    rows = -(-n // 1024) * 8
    return jnp.pad(flat, (0, rows * 128 - n)).reshape(rows, 128)


def _unpack(packed, like):
    flat = packed.reshape(-1)
    out, off = [], 0
    for a in like:
        out.append(flat[off:off + a.size].reshape(a.shape))
        off += a.size
    return out


def _pad8(w):
    return jnp.pad(w, ((0, 5), (0, 0)))


def kernel(x, norm1_g, w_in, b_gate, gmlp_ln_g, gmlp_ln_b, w_spatial, b_spatial, w_shortconv, w_branch, w_out, norm2_g, w_ffn_up, w_ffn_conv, b_ffn_conv, w_ffn_down, final_g, loss_target, m_norm1_g, m_w_in, m_b_gate, m_gmlp_ln_g, m_gmlp_ln_b, m_w_spatial, m_b_spatial, m_w_shortconv, m_w_branch, m_w_out, m_norm2_g, m_w_ffn_up, m_w_ffn_conv, m_b_ffn_conv, m_w_ffn_down, m_final_g, v_norm1_g, v_w_in, v_b_gate, v_gmlp_ln_g, v_gmlp_ln_b, v_w_spatial, v_b_spatial, v_w_shortconv, v_w_branch, v_w_out, v_norm2_g, v_w_ffn_up, v_w_ffn_conv, v_b_ffn_conv, v_w_ffn_down, v_final_g):
    weights = dict(norm1_g=norm1_g, w_in=w_in, b_gate=b_gate, gmlp_ln_g=gmlp_ln_g, gmlp_ln_b=gmlp_ln_b,
                   w_spatial=w_spatial, b_spatial=b_spatial, w_shortconv=w_shortconv, w_branch=w_branch, w_out=w_out,
                   norm2_g=norm2_g, w_ffn_up=w_ffn_up, w_ffn_conv=w_ffn_conv, b_ffn_conv=b_ffn_conv,
                   w_ffn_down=w_ffn_down, final_g=final_g)
    mom = dict(norm1_g=m_norm1_g, w_in=m_w_in, b_gate=m_b_gate, gmlp_ln_g=m_gmlp_ln_g, gmlp_ln_b=m_gmlp_ln_b,
               w_spatial=m_w_spatial, b_spatial=m_b_spatial, w_shortconv=m_w_shortconv, w_branch=m_w_branch,
               w_out=m_w_out, norm2_g=m_norm2_g, w_ffn_up=m_w_ffn_up, w_ffn_conv=m_w_ffn_conv,
               b_ffn_conv=m_b_ffn_conv, w_ffn_down=m_w_ffn_down, final_g=m_final_g)
    vel = dict(norm1_g=v_norm1_g, w_in=v_w_in, b_gate=v_b_gate, gmlp_ln_g=v_gmlp_ln_g, gmlp_ln_b=v_gmlp_ln_b,
               w_spatial=v_w_spatial, b_spatial=v_b_spatial, w_shortconv=v_w_shortconv, w_branch=v_w_branch,
               w_out=v_w_out, norm2_g=v_norm2_g, w_ffn_up=v_w_ffn_up, w_ffn_conv=v_w_ffn_conv,
               b_ffn_conv=v_b_ffn_conv, w_ffn_down=v_w_ffn_down, final_g=v_final_g)

    cx, cy, cc = _mesh_pos()
    chip = 2 * cx + cy
    chip_arr = chip.astype(jnp.int32).reshape(1)
    pos_arr = jnp.stack([chip, cc]).astype(jnp.int32)
    t_len = x.shape[1]
    xs = x.reshape(t_len, D_MODEL)
    target = loss_target.reshape(t_len, D_MODEL)
    pipe = _Pipe()

    full = {}

    def gather(group, names, la):
        slots = _cast_into_slots(group, la, [weights[n].reshape((N_LAYERS,) + BIG[n]) for n in names], chip_arr)

        def then(*bufs):
            full.update(zip([(n, la) for n in names], bufs))

        pipe.add(_gather_stage(slots, then))

    mixer_w = ("w_in", "w_branch", "w_out")
    ffn_w = ("w_ffn_up", "w_ffn_down")
    gather("mixer", mixer_w, 0)
    tap_slots = {}
    pipe.add(_chip_spread_stage(_pack([w_shortconv, w_ffn_conv]), lambda slots: tap_slots.__setitem__("all", slots)))
    pipe.flush()
    by_chip = [_unpack(tap_slots["all"][k], [w_shortconv, w_ffn_conv]) for k in range(N_CHIPS)]
    wsc_full = jnp.concatenate([t[0] for t in by_chip], axis=-1)
    wfc_full = jnp.concatenate([t[1] for t in by_chip], axis=-1)

    idx = jnp.arange(GMLP_BLOCK) // CHUNK
    mask = idx[None, :] <= idx[:, None]
    wm_all = jnp.where(mask[None, None], w_spatial, 0.0)
    wm_bf = wm_all.astype(BF16)
    wmt_bf = jnp.swapaxes(wm_all, -1, -2).astype(BF16)
    bsf = jnp.repeat(jnp.swapaxes(b_spatial, -1, -2), 128, axis=-1)

    def row(a):
        return a.reshape(1, -1)

    def mixer_args(la):
        return (row(norm1_g[la]), row(b_gate[la]), row(gmlp_ln_g[la]), row(gmlp_ln_b[la]))

    def mixer_weights(la):
        return tuple(full[(n, la)] for n in mixer_w)

    def ffn_weights(la):
        return tuple(full[(n, la)] for n in ffn_w)

    saved = []
    h_in = xs
    for la in range(N_LAYERS):
        gather("ffn", ffn_w, la)
        *kept, mg, h1, x2 = pipe.carry(lambda st: _mixer_fwd(
            la, h_in, *mixer_args(la), wm_bf[la], bsf[la], _pad8(wsc_full[la]), *mixer_weights(la), st))
        ya, yb = kept[1], kept[2]
        if la + 1 < N_LAYERS:
            gather("mixer", mixer_w, la + 1)
        head = (target, row(final_g)) if la == N_LAYERS - 1 else None
        up, silu, dsilu, act, h2, *rest = pipe.carry(lambda st: _ffn_fwd(
            la, x2, row(norm2_g[la]), _pad8(wfc_full[la]), row(b_ffn_conv[la]), *ffn_weights(la), st, head=head))
        saved.append(dict(x=h_in, ya=ya, yb=yb, mixer=[kept[0]] + kept[3:], mg=mg, h1=h1, x2=x2, up=up, silu=silu,
                          dsilu=dsilu, act=act, h2=h2))
        h_in = rest[0]
    dx, dgf8, loss8 = rest

    reduced_big = {}

    sums_due = []

    def run_sums():
        if sums_due:
            due = list(sums_due)
            sums_due.clear()
            run_sums.calls += 1
            for (_, then), res in zip(due, _reduction_sums(str(run_sums.calls), [job for job, _ in due], pos_arr)):
                then(res)

    run_sums.calls = 0
    pipe.after = run_sums

    def reduce_big(name, la, grad):
        def after_pair(other):
            def after_chips(got):
                sums_due.append((("chip", grad, other, got), lambda final: pipe.add(_pair_fill_stage(
                    final, lambda done: reduced_big.__setitem__((name, la), done)))))

            sums_due.append((("pair", grad, other), lambda psum: pipe.add(_chip_send_stage(psum, after_chips))))

        pipe.add(_pair_send_stage(grad, after_pair))

    small = {n: [None] * N_LAYERS for n in SMALL}
    spread = {}
    for la in reversed(range(N_LAYERS)):
        s = saved[la]
        dx3 = dx
        dx2, dup, dx3b, dg2, dbfc, dwfc = pipe.carry(lambda st: _ffn_bwd(
            la, dx3, s["x2"], s["up"], s["silu"], s["dsilu"], row(norm2_g[la]), _pad8(wfc_full[la]),
            *ffn_weights(la), st))
        g, = pipe.carry(lambda st: _wgrad("w_ffn_down", la, s["act"], dx3b, 704, 1024, 1408, 1024, st))
        reduce_big("w_ffn_down", la, g)
        g, = pipe.carry(lambda st: _wgrad("w_ffn_up", la, s["h2"], dup, 1024, 1408, 1024, 1408, st))
        reduce_big("w_ffn_up", la, g)
        run = pipe.carry if la > 0 else (lambda call: call([])[0])
        dxl, dz, da, db, dx2b, dg1, dbg, dlng, dlnb, dwm, dbsf, dwsc = run(lambda st: _mixer_bwd(
            la, dx2, s["x"], *s["mixer"], row(norm1_g[la]), row(gmlp_ln_g[la]), row(gmlp_ln_b[la]), wmt_bf[la],
            _pad8(wsc_full[la]), *mixer_weights(la), st))
        small["norm1_g"][la] = dg1.sum(0)
        small["b_gate"][la] = dbg.sum(0)
        small["gmlp_ln_g"][la] = dlng.sum(0)
        small["gmlp_ln_b"][la] = dlnb.sum(0)
        small["w_spatial"][la] = jnp.where(mask[None], dwm, 0.0)
        small["b_spatial"][la] = dbsf.reshape(128, A_HEADS, 128).sum(-1).T
        small["w_shortconv"][la] = dwsc.sum(1)
        small["norm2_g"][la] = dg2.sum(0)
        small["w_ffn_conv"][la] = dwfc.sum(1)
        small["b_ffn_conv"][la] = dbfc.sum(0)
        if la == 0:
            small_local = ([jnp.stack(small[n]) for n in SMALL[:-1]]
                           + [dgf8.sum(0), 0.5 * loss8.sum().reshape(1) / D_MODEL])
            mine = _pack(small_local)

            def after_swap(other, mine=mine):
                pair = _sum_slots("small_pair", jnp.stack([mine, other]))
                pipe.add(_chip_spread_stage(pair, lambda slots: spread.__setitem__("slots", slots)))

            pipe.add(_pair_swap_stage(mine, after_swap))
        g, = pipe.carry(lambda st: _wgrad("w_in", la, s["h1"], dz, 1024, 1152, 1024, 1152, st))
        reduce_big("w_in", la, g)
        g, = pipe.carry(lambda st: _wgrad("w_out", la, s["mg"], dx2b, 256, 1024, 1024, 1024, st), long=False)
        reduce_big("w_out", la, g)
        g, = pipe.carry(lambda st: _wgrad_branch(la, s["ya"], da, s["yb"], db, st), long=False)
        reduce_big("w_branch", la, g)
        dx = dxl
    grad_x = dx.reshape(x.shape)
    pipe.flush()

    reduced = _unpack(_sum_slots("small_grads", spread["slots"]), small_local)
    loss = reduced[-1].reshape(())
    grads = dict(zip(SMALL, reduced[:-1]))
    grads["w_shortconv"] = lax.dynamic_slice(grads["w_shortconv"], (0, 0, chip * (D_B // 4)), (N_LAYERS, 3, D_B // 4))
    grads["w_ffn_conv"] = lax.dynamic_slice(grads["w_ffn_conv"], (0, 0, chip * (D_FF // 4)), (N_LAYERS, 3, D_FF // 4))

    delta, new_m, new_v = {}, {}, {}
    for n in BIG_NAMES:
        shape3 = (N_LAYERS,) + BIG[n]
        res = _adamw_big(n, weights[n].reshape(shape3), reduced_big[(n, 0)], reduced_big[(n, 1)],
                         mom[n].reshape(shape3), vel[n].reshape(shape3))
        grads[n], delta[n], new_m[n], new_v[n] = (a.reshape(weights[n].shape) for a in res)
    res = _adamw_small(*[[src[n].reshape(-1, src[n].shape[-1]) for n in SMALL] for src in (weights, grads, mom, vel)])
    for k, n in enumerate(SMALL):
        delta[n], new_m[n], new_v[n] = (res[j * len(SMALL) + k].reshape(weights[n].shape) for j in range(3))

    return (loss, grad_x, *[grads[n] for n in ALL_WEIGHTS], *[delta[n] for n in ALL_WEIGHTS],
            *[new_m[n] for n in ALL_WEIGHTS], *[new_v[n] for n in ALL_WEIGHTS])
```

```python
import jax
import jax.numpy as jnp
from jax import lax
from jax.experimental import pallas as pl
from jax.experimental.pallas import tpu as pltpu

F32 = jnp.float32
BF16 = jnp.bfloat16
MESH = pl.DeviceIdType.MESH
ANY = pl.BlockSpec(memory_space=pl.ANY)

D_MODEL = 1024
D_A = 512
D_B = 512
D_IN = 4608
D_FF = 2816
GMLP_BLOCK = 128
CHUNK = 64
A_HEADS = 4
N_LAYERS = 2
N_CHIPS = 4
RMS_EPS = 1e-6
LN_EPS = 1e-5
ADAM_LR = 0.001
ADAM_B1 = 0.9
ADAM_B2 = 0.999
ADAM_EPS = 1e-08
ADAM_WD = 0.01
ADAM_STEP = 10

C_U, C_V, C_BG, C_CG, C_HB, C_GA, C_GB = 0, 512, 1024, 1536, 2048, 2560, 3584

V7X_VMEM_LIMIT = 60 * 1024 * 1024
TM_MIX = 256
TM_FFN = 256
TK_WGRAD = 2048
SLOW_COPY_BYTES = 640 * 1024
FF_CHUNKS = ((0, 768), (768, 1536), (1536, 2304), (2304, 2816))
GELU_C0 = 0.7978845608028654
GELU_C1 = 0.044715

BIG = {
    "w_in": (1024, 1152),
    "w_branch": (1024, 256),
    "w_out": (256, 1024),
    "w_ffn_up": (1024, 1408),
    "w_ffn_down": (704, 1024),
}
BIG_NAMES = tuple(BIG)


def _params(sem=("arbitrary",), vmem=V7X_VMEM_LIMIT):
    return pltpu.CompilerParams(dimension_semantics=sem, vmem_limit_bytes=vmem)


def _gelu(x):
    x2 = x * x
    t = jnp.tanh(GELU_C0 * x * (1.0 + GELU_C1 * x2))
    return 0.5 * x * (1.0 + t), t


def _gelu_grad(x, t):
    return 0.5 * (1.0 + t) + 0.5 * x * (1.0 - t * t) * GELU_C0 * (1.0 + 3.0 * GELU_C1 * x * x)


def _colsum8(v):
    r, n = v.shape
    return v.reshape(r // 8, 8, n).sum(axis=0)


def _dot(a, b):
    return jnp.dot(a, b, preferred_element_type=F32)


def _dot_nt(a, b):
    return lax.dot_general(a, b, (((1,), (1,)), ((), ())), preferred_element_type=F32)


def _dot_tn(a, b):
    return lax.dot_general(a, b, (((0,), (0,)), ((), ())), preferred_element_type=F32)


def _shift_down(v, carry, n):
    rows = lax.broadcasted_iota(jnp.int32, (8, v.shape[1]), 0)
    out = pltpu.roll(v, n, 0)
    head = out[0:8, :]
    for r in range(n):
        head = jnp.where(rows == r, carry[8 - n + r:8 - n + r + 1, :], head)
    return jnp.concatenate([head, out[8:, :]], axis=0)


def _shift_up(v, carry, n):
    tm = v.shape[0]
    rows = lax.broadcasted_iota(jnp.int32, (8, v.shape[1]), 0)
    out = pltpu.roll(v, tm - n, 0)
    tail = out[tm - 8:tm, :]
    for r in range(n):
        tail = jnp.where(rows == 8 - n + r, carry[r:r + 1, :], tail)
    return jnp.concatenate([out[0:tm - 8, :], tail], axis=0)


def _sigmoid(x):
    return 0.5 * jnp.tanh(0.5 * x) + 0.5


def _start_all(copies):
    for cp in copies:
        cp.start()


def _wait_all(copies):
    for cp in copies:
        cp.wait()


def _load_col_sharded(src, dst, sems, first):
    cs = src.shape[-1]
    return [pltpu.make_async_copy(src.at[k], dst.at[:, k * cs:(k + 1) * cs], sems.at[first + k])
            for k in range(N_CHIPS)]


def _load_row_sharded(src, dst, sems, first):
    rs = src.shape[-2]
    return [pltpu.make_async_copy(src.at[k], dst.at[k * rs:(k + 1) * rs, :], sems.at[first + k])
            for k in range(N_CHIPS)]


def _load_branch(src, dst, sems, first):
    return [pltpu.make_async_copy(src.at[k, pl.ds(m * D_A, D_A), :], dst.at[m, :, k * 256:(k + 1) * 256],
                                  sems.at[first + 2 * k + m])
            for k in range(N_CHIPS) for m in range(2)]


def _row_spec(tm, n, rev=None):
    if rev is None:
        return pl.BlockSpec((tm, n), lambda i: (i, 0))
    return pl.BlockSpec((tm, n), lambda i: (rev - 1 - i, 0))


def _const_spec(shape):
    nd = len(shape)
    return pl.BlockSpec(shape, lambda i: (0,) * nd)


def _mesh_pos():
    return lax.axis_index("x"), lax.axis_index("y"), lax.axis_index("c")


def _other_chips(x, y):
    return [(1 - x, y, 2 * (1 - x) + y), (x, 1 - y, 2 * x + (1 - y)), (1 - x, 1 - y, 2 * (1 - x) + (1 - y))]


def _remote(src, dst, ssem, rsem, to):
    return pltpu.make_async_remote_copy(src_ref=src, dst_ref=dst, send_sem=ssem, recv_sem=rsem, device_id=to,
                                        device_id_type=MESH)


def _half(ref, which, h):
    start = pl.multiple_of(which * h, 8)
    if len(ref.shape) == 2:
        return ref.at[pl.ds(start, h), :]
    return ref.at[:, pl.ds(start, h), :]


class _Stage:
    def __init__(self, ins=(), inouts=(), outs=(), n_sems=0, start=None, mid=None, finish=None, then=None, slow=False):
        self.ins, self.inouts, self.outs = list(ins), list(inouts), list(outs)
        self.n_sems, self.start, self.mid, self.finish, self.then = n_sems, start, mid, finish, then
        self.slow = slow


def _gather_stage(bufs, then):
    n = len(bufs)

    def copies(io, sem):
        x, y, c = _mesh_pos()
        me = 2 * x + y
        ici, fwd, got = [], [], []
        for w in range(n):
            h = io[w].shape[1] // 2
            for j, (px, py, pk) in enumerate(_other_chips(x, y)):
                mine = _half(io[w].at[me], c, h)
                theirs = _half(io[w].at[pk], c, h)
                ici.append(_remote(mine, mine, sem(12 * w + j), sem(12 * w + 3 + j), (px, py, c)))
                got.append(_remote(theirs, theirs, sem(12 * w + j), sem(12 * w + 3 + j), (px, py, c)))
                fwd.append(_remote(theirs, theirs, sem(12 * w + 6 + j), sem(12 * w + 9 + j), (x, y, 1 - c)))
        return ici, got, fwd

    def start(ins, io, outs, sem):
        _start_all(copies(io, sem)[0])

    def mid(ins, io, outs, sem):
        _, got, fwd = copies(io, sem)
        for g, f in zip(got, fwd):
            g.wait_recv()
            f.start()

    def finish(ins, io, outs, sem):
        x, y, c = _mesh_pos()
        ici, _, fwd = copies(io, sem)
        for w in range(n):
            h = io[w].shape[1] // 2
            for j, (px, py, pk) in enumerate(_other_chips(x, y)):
                other = _half(io[w].at[pk], 1 - c, h)
                _remote(other, other, sem(12 * w + 6 + j), sem(12 * w + 9 + j), (x, y, 1 - c)).wait_recv()
        for cp in ici + fwd:
            cp.wait_send()

    return _Stage(inouts=bufs, n_sems=12 * n, start=start, mid=mid, finish=finish, then=then)


def _pair_send_stage(grad, then):
    h = grad.shape[1] // 2

    def copy(ins, outs, sem):
        x, y, c = _mesh_pos()
        return _remote(_half(ins[0], 1 - c, h), outs[0], sem(0), sem(1), (x, y, 1 - c))

    return _Stage(ins=[grad], outs=[jax.ShapeDtypeStruct((N_CHIPS, h, grad.shape[2]), F32)], n_sems=2,
                  start=lambda ins, io, outs, sem: copy(ins, outs, sem).start(),
                  finish=lambda ins, io, outs, sem: copy(ins, outs, sem).wait(), then=then)


def _chip_send_stage(psum, then):
    def copies(ins, outs, sem):
        x, y, c = _mesh_pos()
        return [_remote(ins[0].at[pk], outs[0].at[j], sem(j), sem(3 + j), (px, py, c))
                for j, (px, py, pk) in enumerate(_other_chips(x, y))]

    return _Stage(ins=[psum], outs=[jax.ShapeDtypeStruct((3,) + psum.shape[1:], BF16)], n_sems=6,
                  start=lambda ins, io, outs, sem: _start_all(copies(ins, outs, sem)),
                  finish=lambda ins, io, outs, sem: _wait_all(copies(ins, outs, sem)), then=then,
                  slow=psum.shape[1] * psum.shape[2] * 2 > SLOW_COPY_BYTES)


def _pair_fill_stage(final, then):
    h = final.shape[0] // 2

    def copy(io, sem):
        x, y, c = _mesh_pos()
        mine = _half(io[0], c, h)
        return _remote(mine, mine, sem(0), sem(1), (x, y, 1 - c))

    return _Stage(inouts=[final], n_sems=2,
                  start=lambda ins, io, outs, sem: copy(io, sem).start(),
                  finish=lambda ins, io, outs, sem: copy(io, sem).wait(), then=then)


def _pair_swap_stage(packed, then):
    def copy(ins, outs, sem):
        x, y, c = _mesh_pos()
        return _remote(ins[0], outs[0], sem(0), sem(1), (x, y, 1 - c))

    return _Stage(ins=[packed], outs=[jax.ShapeDtypeStruct(packed.shape, F32)], n_sems=2,
                  start=lambda ins, io, outs, sem: copy(ins, outs, sem).start(),
                  finish=lambda ins, io, outs, sem: copy(ins, outs, sem).wait(), then=then)


def _chip_spread_stage(psum, then):
    def copies(ins, outs, sem):
        x, y, c = _mesh_pos()
        me = 2 * x + y
        cps = [_remote(ins[0], outs[0].at[me], sem(j), sem(3 + j), (px, py, c))
               for j, (px, py, pk) in enumerate(_other_chips(x, y))]
        return cps, pltpu.make_async_copy(ins[0], outs[0].at[me], sem(6))

    def start(ins, io, outs, sem):
        cps, own = copies(ins, outs, sem)
        own.start()
        _start_all(cps)

    def finish(ins, io, outs, sem):
        cps, own = copies(ins, outs, sem)
        _wait_all(cps)
        own.wait()

    return _Stage(ins=[psum], outs=[jax.ShapeDtypeStruct((N_CHIPS,) + psum.shape, F32)], n_sems=7,
                  start=start, finish=finish, then=then)


def _staged_call(core, *, name, grid, in_specs, out_specs, out_shape, scratch_shapes, args, stages):
    n_in, n_out, n_scr = len(args), len(out_shape), len(scratch_shapes)
    s_args, s_outs, aliases, layout = [], [], {}, []
    n_sems = 0
    for st in stages:
        i0, o0 = len(s_args), len(s_outs)
        s_args += st.ins + st.inouts
        for q in range(len(st.inouts)):
            aliases[n_in + i0 + len(st.ins) + q] = n_out + o0 + q
        s_outs += [jax.ShapeDtypeStruct(a.shape, a.dtype) for a in st.inouts] + st.outs
        layout.append((i0, o0, n_sems))
        n_sems += st.n_sems
    steps = 1
    for g in grid:
        steps *= g

    def body(*refs):
        own_in = refs[:n_in]
        s_in = refs[n_in:n_in + len(s_args)]
        rest = refs[n_in + len(s_args):]
        own_out = rest[:n_out]
        s_out = rest[n_out:n_out + len(s_outs)]
        scr = rest[n_out + len(s_outs):]

        def run(which):
            for st, (i0, o0, s0) in zip(stages, layout):
                fn = getattr(st, which)
                if fn is not None:
                    fn(s_in[i0:i0 + len(st.ins)], s_out[o0:o0 + len(st.inouts)],
                       s_out[o0 + len(st.inouts):o0 + len(st.inouts) + len(st.outs)],
                       lambda k, s0=s0: scr[n_scr].at[s0 + k])

        if not stages:
            core(*own_in, *own_out, *scr[:n_scr])
            return
        step = 0
        for d, g in enumerate(grid):
            step = step * g + pl.program_id(d)
        if steps == 1:
            run("start")
            core(*own_in, *own_out, *scr[:n_scr])
            run("mid")
            run("finish")
            return
        pl.when(step == 0)(lambda: run("start"))
        core(*own_in, *own_out, *scr[:n_scr])
        pl.when(step == (3 * steps) // 4)(lambda: run("mid"))
        pl.when(step == steps - 1)(lambda: run("finish"))

    sem = ("arbitrary",) * len(grid) if stages else ("parallel",) * max(len(grid) - 1, 0) + ("arbitrary",) * min(len(grid), 1)
    res = pl.pallas_call(
        body, name=name, grid=grid,
        in_specs=list(in_specs) + [ANY] * len(s_args),
        out_specs=list(out_specs) + [ANY] * len(s_outs),
        out_shape=list(out_shape) + s_outs,
        input_output_aliases=aliases,
        scratch_shapes=list(scratch_shapes) + ([pltpu.SemaphoreType.DMA((n_sems,))] if stages else []),
        compiler_params=_params(sem) if grid else pltpu.CompilerParams(vmem_limit_bytes=V7X_VMEM_LIMIT),
    )(*args, *s_args)
    return list(res[:n_out]), list(res[n_out:])


class _Pipe:
    def __init__(self):
        self.ready = []
        self.flushes = 0
        self.after = None

    def add(self, stage):
        self.ready.append(stage)

    def carry(self, call, long=True):
        stages = [st for st in self.ready if long or not st.slow]
        self.ready = [st for st in self.ready if not (long or not st.slow)]
        own, outs = call(stages)
        k = 0
        for st in stages:
            n = len(st.inouts) + len(st.outs)
            st.then(*outs[k:k + n])
            k += n
        if self.after is not None:
            self.after()
        return own

    def flush(self):
        while self.ready:
            self.flushes += 1
            self.carry(lambda stages: _staged_call(
                lambda *refs: None, name=f"comm_tail_{self.flushes}", grid=(), in_specs=[], out_specs=[], out_shape=[],
                scratch_shapes=[], args=[], stages=stages))


def _mixer_fwd(layer, x, g1, bgate, lng, lnb, wm, bsf, wsc, win_g, wb_g, wout_g, stages):
    t_len = x.shape[0]
    tm = min(TM_MIX, t_len)
    nt = t_len // tm
    nb = tm // GMLP_BLOCK

    def core(x_ref, x_late_ref, g1_ref, bgate_ref, lng_ref, lnb_ref, wm_ref, bsf_ref, wsc_ref, win_hbm, wb_hbm, wout_hbm,
             zc_ref, ya_ref, yb_ref, q_ref, sa_ref, ca_ref, sb_ref, cb_ref, ug_ref, fu_ref, xh_ref, cv_ref,
             mg_ref, h_ref, x2_ref,
             win_v, wb_v, wout_v, carry, vn_s, f_s, z_s, sems):
        i = pl.program_id(0)

        @pl.when(i == 0)
        def _():
            cps = (_load_col_sharded(win_hbm, win_v, sems, 0) + _load_branch(wb_hbm, wb_v, sems, 4)
                   + _load_row_sharded(wout_hbm, wout_v, sems, 12))
            _start_all(cps)
            carry[...] = jnp.zeros_like(carry)
            z_s[...] = jnp.zeros_like(z_s)
            _wait_all(cps)

        xv = x_ref[...]
        r = lax.rsqrt(jnp.mean(xv * xv, axis=-1, keepdims=True) + RMS_EPS)
        h_ref[...] = (xv * r * g1_ref[...]).astype(BF16)

        def zcols(c0, n, keep=None):
            zv = z_s[:, c0:c0 + n]
            z_s[:, c0:c0 + n] = _dot(h_ref[...], win_v[:, c0:c0 + n])
            if keep is not None:
                zc_ref[:, keep * D_B:(keep + 1) * D_B] = zv.astype(BF16)
            return zv

        v = zcols(C_V, D_A)
        vg, tv = _gelu(v)
        mu = jnp.mean(vg, axis=-1, keepdims=True)
        vc = vg - mu
        rstd = lax.rsqrt(jnp.mean(vc * vc, axis=-1, keepdims=True) + LN_EPS)
        xh = vc * rstd
        xh_ref[...] = xh.astype(BF16)
        cv_ref[...] = (rstd * _gelu_grad(v, tv)).astype(BF16)
        vn_s[...] = (xh * lng_ref[...] + lnb_ref[...]).astype(BF16)
        for hd in range(A_HEADS):
            cols = slice(hd * 128, (hd + 1) * 128)
            vcat = jnp.concatenate([vn_s[b * 128:(b + 1) * 128, cols] for b in range(nb)], axis=1)
            fcat = _dot(wm_ref[hd], vcat)
            for b in range(nb):
                f_s[b * 128:(b + 1) * 128, cols] = fcat[:, b * 128:(b + 1) * 128]
        u = zcols(C_U, D_A)
        ug, tu = _gelu(u)
        ug_ref[...] = ug.astype(BF16)
        fb = f_s[...] + jnp.concatenate([bsf_ref[...]] * nb, axis=0)
        fu_ref[...] = (fb * _gelu_grad(u, tu)).astype(BF16)
        ya_ref[...] = (ug * fb).astype(BF16)

        p = zcols(C_CG, D_B, keep=1) * zcols(C_HB, D_B, keep=2)
        cr = carry[...]
        q = wsc_ref[0:1, :] * _shift_down(p, cr, 2) + wsc_ref[1:2, :] * _shift_down(p, cr, 1) + wsc_ref[2:3, :] * p
        carry[...] = p[tm - 8:tm, :]
        q_ref[...] = q.astype(BF16)
        yb_ref[...] = (zcols(C_BG, D_B, keep=0) * q).astype(BF16)

        av = _dot(ya_ref[...], wb_v[0])
        sa = _sigmoid(zcols(C_GA, D_MODEL) + bgate_ref[:, 0:D_MODEL])
        sa_ref[...] = sa.astype(BF16)
        mg = sa * av
        ca_ref[...] = (mg * (1.0 - sa)).astype(BF16)
        bv = _dot(yb_ref[...], wb_v[1])
        sb = _sigmoid(zcols(C_GB, D_MODEL) + bgate_ref[:, D_MODEL:2 * D_MODEL])
        sb_ref[...] = sb.astype(BF16)
        mb = sb * bv
        cb_ref[...] = (mb * (1.0 - sb)).astype(BF16)
        mg_ref[...] = (mg + mb).astype(BF16)
        x2_ref[...] = x_late_ref[...] + _dot(mg_ref[...], wout_v[...])

    def tile(n, lag):
        return pl.BlockSpec((tm, n), lambda i: (jnp.clip(i - lag, 0, nt - 1), 0))

    outs = [
        jax.ShapeDtypeStruct((t_len, 3 * D_B), BF16),
        jax.ShapeDtypeStruct((t_len, D_A), BF16),
        jax.ShapeDtypeStruct((t_len, D_B), BF16),
        jax.ShapeDtypeStruct((t_len, D_B), BF16),
        jax.ShapeDtypeStruct((t_len, D_MODEL), BF16),
        jax.ShapeDtypeStruct((t_len, D_MODEL), BF16),
        jax.ShapeDtypeStruct((t_len, D_MODEL), BF16),
        jax.ShapeDtypeStruct((t_len, D_MODEL), BF16),
        jax.ShapeDtypeStruct((t_len, D_A), BF16),
        jax.ShapeDtypeStruct((t_len, D_A), BF16),
        jax.ShapeDtypeStruct((t_len, D_A), BF16),
        jax.ShapeDtypeStruct((t_len, D_A), BF16),
        jax.ShapeDtypeStruct((t_len, D_MODEL), BF16),
        jax.ShapeDtypeStruct((t_len, D_MODEL), BF16),
        jax.ShapeDtypeStruct((t_len, D_MODEL), F32),
    ]
    return _staged_call(
        core, name=f"mixer_fwd_l{layer}", grid=(nt + 1,),
        in_specs=[tile(D_MODEL, 0), tile(D_MODEL, 1), _const_spec((1, D_MODEL)), _const_spec((1, 2 * D_MODEL)),
                  _const_spec((1, D_A)), _const_spec((1, D_A)), _const_spec((A_HEADS, 128, 128)),
                  _const_spec((128, D_A)), _const_spec((8, D_B)), ANY, ANY, ANY],
        out_specs=[tile(o.shape[1], 0 if k == len(outs) - 2 else 1) for k, o in enumerate(outs)],
        out_shape=outs,
        scratch_shapes=[pltpu.VMEM((D_MODEL, D_IN), BF16), pltpu.VMEM((2, D_A, D_MODEL), BF16),
                        pltpu.VMEM((D_MODEL, D_MODEL), BF16), pltpu.VMEM((8, D_B), F32),
                        pltpu.VMEM((tm, D_A), BF16), pltpu.VMEM((tm, D_A), F32), pltpu.VMEM((tm, D_IN), F32),
                        pltpu.SemaphoreType.DMA((16,))],
        args=[x, x, g1, bgate, lng, lnb, wm, bsf, wsc, win_g, wb_g, wout_g], stages=stages)


def _ffn_fwd(layer, x2, g2, wfc, bfc, wup_g, wdown_g, stages, head=None):
    t_len = x2.shape[0]
    tm = min(TM_FFN, t_len)
    nt = t_len // tm

    def core(*refs):
        if head is None:
            (x_ref, g2_ref, wfc_ref, bfc_ref, wup_hbm, wdown_hbm, up_ref, silu_ref, dsilu_ref, act_ref, h_ref, x3_ref,
             wup_v, wdown_v, carry, sems) = refs
        else:
            (x_ref, g2_ref, wfc_ref, bfc_ref, t_ref, gf_ref, wup_hbm, wdown_hbm, up_ref, silu_ref, dsilu_ref, act_ref,
             h_ref, dx_ref, dgf_ref, loss_ref, wup_v, wdown_v, carry, sems) = refs
        i = pl.program_id(0)

        @pl.when(i == 0)
        def _():
            cps = _load_col_sharded(wup_hbm, wup_v, sems, 0) + _load_row_sharded(wdown_hbm, wdown_v, sems, 4)
            _start_all(cps)
            carry[...] = jnp.zeros_like(carry)
            if head is not None:
                dgf_ref[...] = jnp.zeros_like(dgf_ref)
                loss_ref[...] = jnp.zeros_like(loss_ref)
            _wait_all(cps)

        xv = x_ref[...]
        r = lax.rsqrt(jnp.mean(xv * xv, axis=-1, keepdims=True) + RMS_EPS)
        h_ref[...] = (xv * r * g2_ref[...]).astype(BF16)
        gate = _dot(h_ref[...], wup_v[:, 0:D_FF])
        up_ref[:, 0:D_FF] = gate.astype(BF16)
        cr = carry[...]
        gc = (wfc_ref[0:1, :] * _shift_down(gate, cr, 2) + wfc_ref[1:2, :] * _shift_down(gate, cr, 1)
              + wfc_ref[2:3, :] * gate + bfc_ref[...])
        carry[...] = gate[tm - 8:tm, :]
        sg = _sigmoid(gc)
        silu = gc * sg
        silu_ref[...] = silu.astype(BF16)
        dsilu_ref[...] = (sg + silu * (1.0 - sg)).astype(BF16)
        val = _dot(h_ref[...], wup_v[:, D_FF:2 * D_FF])
        up_ref[:, D_FF:2 * D_FF] = val.astype(BF16)
        act_ref[...] = (silu * val).astype(BF16)
        x3 = x_ref[...] + _dot(act_ref[...], wdown_v[...])
        if head is None:
            x3_ref[...] = x3
        else:
            r3 = lax.rsqrt(jnp.mean(x3 * x3, axis=-1, keepdims=True) + RMS_EPS)
            xh = x3 * r3
            err = xh * gf_ref[...] - t_ref[...]
            loss_ref[...] += _colsum8(err * err)
            dy = err * (1.0 / D_MODEL)
            dgf_ref[...] += _colsum8(dy * xh)
            dxh = dy * gf_ref[...]
            dx_ref[...] = r3 * (dxh - xh * jnp.mean(dxh * xh, axis=-1, keepdims=True))

    outs = [
        jax.ShapeDtypeStruct((t_len, 2 * D_FF), BF16),
        jax.ShapeDtypeStruct((t_len, D_FF), BF16),
        jax.ShapeDtypeStruct((t_len, D_FF), BF16),
        jax.ShapeDtypeStruct((t_len, D_FF), BF16),
        jax.ShapeDtypeStruct((t_len, D_MODEL), BF16),
        jax.ShapeDtypeStruct((t_len, D_MODEL), F32),
    ]
    in_specs = [_row_spec(tm, D_MODEL), _const_spec((1, D_MODEL)), _const_spec((8, D_FF)), _const_spec((1, D_FF))]
    out_specs = [_row_spec(tm, o.shape[1]) for o in outs]
    args = [x2, g2, wfc, bfc]
    if head is not None:
        in_specs += [_row_spec(tm, D_MODEL), _const_spec((1, D_MODEL))]
        args += list(head)
        outs += [jax.ShapeDtypeStruct((8, D_MODEL), F32)] * 2
        out_specs += [_const_spec((8, D_MODEL))] * 2
    return _staged_call(
        core, name=f"ffn_fwd_l{layer}", grid=(nt,),
        in_specs=in_specs + [ANY, ANY], out_specs=out_specs, out_shape=outs,
        scratch_shapes=[pltpu.VMEM((D_MODEL, 2 * D_FF), BF16), pltpu.VMEM((D_FF, D_MODEL), BF16),
                        pltpu.VMEM((8, D_FF), F32), pltpu.SemaphoreType.DMA((8,))],
        args=args + [wup_g, wdown_g], stages=stages)


def _ffn_bwd(layer, dx3, x2, up, silu, dsilu, g2, wfc, wup_g, wdown_g, stages):
    t_len = x2.shape[0]
    tm = min(TM_FFN, t_len)
    nt = t_len // tm

    def core(dx3_ref, dx3_late_ref, x_ref, up_ref, silu_ref, dsilu_ref, g2_ref, wfc_ref, wup_hbm, wdown_hbm,
             dx2_ref, dup_ref, dx3b_ref, dg2_ref, dbfc_ref, dwfc_ref,
             wup_v, wdown_v, carry, da_s, dup_s, sems):
        i = pl.program_id(0)

        @pl.when(i == 0)
        def _():
            cps = _load_col_sharded(wup_hbm, wup_v, sems, 0) + _load_row_sharded(wdown_hbm, wdown_v, sems, 4)
            _start_all(cps)
            for ref in (carry, da_s, dup_s, dg2_ref, dbfc_ref, dwfc_ref):
                ref[...] = jnp.zeros_like(ref)
            _wait_all(cps)

        live = (i <= nt).astype(F32)
        dx3b_ref[...] = dx3_ref[...].astype(BF16)
        dh = jnp.zeros((tm, D_MODEL), F32)
        for c0, c1 in FF_CHUNKS:
            v0, v1 = D_FF + c0, D_FF + c1
            dh = dh + _dot_nt(dup_s[:, c0:c1], wup_v[:, c0:c1]) + _dot_nt(dup_s[:, v0:v1], wup_v[:, v0:v1])
            da = da_s[:, c0:c1]
            dval = (da * silu_ref[:, c0:c1].astype(F32)).astype(BF16)
            dup_ref[:, v0:v1] = dval
            dup_s[:, v0:v1] = dval
            dgc = da * up_ref[:, v0:v1].astype(F32) * dsilu_ref[:, c0:c1].astype(F32)
            cr = carry[:, c0:c1]
            dgc1 = _shift_up(dgc, cr, 1)
            dgc2 = _shift_up(dgc, cr, 2)
            carry[:, c0:c1] = jnp.where(i < nt, dgc[0:8, :], cr)
            gate = up_ref[:, c0:c1].astype(F32)
            dbfc_ref[:, c0:c1] += live * _colsum8(dgc)
            dwfc_ref[0, :, c0:c1] += live * _colsum8(dgc2 * gate)
            dwfc_ref[1, :, c0:c1] += live * _colsum8(dgc1 * gate)
            dwfc_ref[2, :, c0:c1] += live * _colsum8(dgc * gate)
            dgate = (wfc_ref[2:3, c0:c1] * dgc + wfc_ref[1:2, c0:c1] * dgc1 + wfc_ref[0:1, c0:c1] * dgc2).astype(BF16)
            dup_ref[:, c0:c1] = dgate
            dup_s[:, c0:c1] = dgate
            da_s[:, c0:c1] = _dot_nt(dx3b_ref[...], wdown_v[c0:c1, :])
        xv = x_ref[...]
        r = lax.rsqrt(jnp.mean(xv * xv, axis=-1, keepdims=True) + RMS_EPS)
        xh = xv * r
        dg2_ref[...] += _colsum8(dh * xh)
        dxh = dh * g2_ref[...]
        dx2_ref[...] = dx3_late_ref[...] + r * (dxh - xh * jnp.mean(dxh * xh, axis=-1, keepdims=True))

    def tile(n, lag):
        return pl.BlockSpec((tm, n), lambda i: (nt - 1 - jnp.clip(i - lag, 0, nt - 1), 0))

    outs = [
        jax.ShapeDtypeStruct((t_len, D_MODEL), F32),
        jax.ShapeDtypeStruct((t_len, 2 * D_FF), BF16),
        jax.ShapeDtypeStruct((t_len, D_MODEL), BF16),
        jax.ShapeDtypeStruct((8, D_MODEL), F32),
        jax.ShapeDtypeStruct((8, D_FF), F32),
        jax.ShapeDtypeStruct((3, 8, D_FF), F32),
    ]
    return _staged_call(
        core, name=f"ffn_bwd_l{layer}", grid=(nt + 2,),
        in_specs=[tile(D_MODEL, 0), tile(D_MODEL, 2), tile(D_MODEL, 2), tile(2 * D_FF, 1), tile(D_FF, 1), tile(D_FF, 1),
                  _const_spec((1, D_MODEL)), _const_spec((8, D_FF)), ANY, ANY],
        out_specs=[tile(D_MODEL, 2), tile(2 * D_FF, 1), tile(D_MODEL, 0),
                   _const_spec((8, D_MODEL)), _const_spec((8, D_FF)), _const_spec((3, 8, D_FF))],
        out_shape=outs,
        scratch_shapes=[pltpu.VMEM((D_MODEL, 2 * D_FF), BF16), pltpu.VMEM((D_FF, D_MODEL), BF16),
                        pltpu.VMEM((8, D_FF), F32), pltpu.VMEM((tm, D_FF), F32), pltpu.VMEM((tm, 2 * D_FF), BF16),
                        pltpu.SemaphoreType.DMA((8,))],
        args=[dx3, dx3, x2, up, silu, dsilu, g2, wfc, wup_g, wdown_g], stages=stages)


def _mixer_bwd(layer, dx2, x, zc, qs, sa, ca, sb, cb, ug, fu, xhs, cv, g1, lng, lnb, wmt, wsc, win_g, wb_g, wout_g,
               stages):
    t_len = x.shape[0]
    tm = min(TM_MIX, t_len)
    nt = t_len // tm
    nb = tm // GMLP_BLOCK

    def core(dx2_ref, x_ref, zc_ref, q_ref, sa_ref, ca_ref, sb_ref, cb_ref, ug_ref, fu_ref, xh_ref, cv_ref,
             g1_ref, lng_ref, lnb_ref, wmt_ref, wsc_ref, win_hbm, wb_hbm, wout_hbm,
             dx_ref, dz_ref, da_ref, db_ref, dx2b_ref, dg1_ref, dbgate_ref, dlng_ref, dlnb_ref, dwm_ref, dbsf_ref, dwsc_ref,
             win_v, wb_v, wout_v, carry, vn_s, df_s, dvn_s, sems):
        i = pl.program_id(0)

        @pl.when(i == 0)
        def _():
            cps = (_load_col_sharded(win_hbm, win_v, sems, 0) + _load_branch(wb_hbm, wb_v, sems, 4)
                   + _load_row_sharded(wout_hbm, wout_v, sems, 12))
            _start_all(cps)
            for ref in (carry, dg1_ref, dbgate_ref, dlng_ref, dlnb_ref, dwm_ref, dbsf_ref, dwsc_ref):
                ref[...] = jnp.zeros_like(ref)
            _wait_all(cps)

        def kept(k):
            return zc_ref[:, k * D_B:(k + 1) * D_B].astype(F32)

        def dz_cols(c0, n, val):
            dz_ref[:, c0:c0 + n] = val.astype(BF16)
            return _dot_nt(dz_ref[:, c0:c0 + n], win_v[:, c0:c0 + n])

        dx2b_ref[...] = dx2_ref[...].astype(BF16)
        dm = _dot_nt(dx2b_ref[...], wout_v[...])
        da_ref[...] = (dm * sa_ref[...].astype(F32)).astype(BF16)
        dga = dm * ca_ref[...].astype(F32)
        dh = dz_cols(C_GA, D_MODEL, dga)
        dbgate_ref[:, 0:D_MODEL] += _colsum8(dga)
        dya = _dot_nt(da_ref[...], wb_v[0])
        db_ref[...] = (dm * sb_ref[...].astype(F32)).astype(BF16)
        dgb = dm * cb_ref[...].astype(F32)
        dh = dh + dz_cols(C_GB, D_MODEL, dgb)
        dbgate_ref[:, D_MODEL:2 * D_MODEL] += _colsum8(dgb)
        dyb = _dot_nt(db_ref[...], wb_v[1])

        xh = xh_ref[...].astype(F32)
        vn_s[...] = (xh * lng_ref[...] + lnb_ref[...]).astype(BF16)
        df = dya * ug_ref[...].astype(F32)
        df_s[...] = df.astype(BF16)
        dbsf_acc = df[0:128, :]
        for b in range(1, nb):
            dbsf_acc = dbsf_acc + df[b * 128:(b + 1) * 128, :]
        dbsf_ref[...] += dbsf_acc
        for hd in range(A_HEADS):
            cols = slice(hd * 128, (hd + 1) * 128)
            vcat = jnp.concatenate([vn_s[b * 128:(b + 1) * 128, cols] for b in range(nb)], axis=1)
            dcat = jnp.concatenate([df_s[b * 128:(b + 1) * 128, cols] for b in range(nb)], axis=1)
            gcat = _dot(wmt_ref[hd], dcat)
            dwm_ref[hd] += _dot_nt(dcat, vcat)
            for b in range(nb):
                dvn_s[b * 128:(b + 1) * 128, cols] = gcat[:, b * 128:(b + 1) * 128]
        dh = dh + dz_cols(C_U, D_A, dya * fu_ref[...].astype(F32))
        dvn = dvn_s[...]
        dlng_ref[...] += _colsum8(dvn * xh)
        dlnb_ref[...] += _colsum8(dvn)
        dxh = dvn * lng_ref[...]
        dvc = dxh - jnp.mean(dxh, axis=-1, keepdims=True) - xh * jnp.mean(dxh * xh, axis=-1, keepdims=True)
        dh = dh + dz_cols(C_V, D_A, dvc * cv_ref[...].astype(F32))

        cg = kept(1)
        hbv = kept(2)
        p = cg * hbv
        dh = dh + dz_cols(C_BG, D_B, dyb * q_ref[...].astype(F32))
        dq = dyb * kept(0)
        cr = carry[...]
        dq1 = _shift_up(dq, cr, 1)
        dq2 = _shift_up(dq, cr, 2)
        carry[...] = dq[0:8, :]
        dwsc_ref[0] += _colsum8(dq2 * p)
        dwsc_ref[1] += _colsum8(dq1 * p)
        dwsc_ref[2] += _colsum8(dq * p)
        dp = wsc_ref[2:3, :] * dq + wsc_ref[1:2, :] * dq1 + wsc_ref[0:1, :] * dq2
        dh = dh + dz_cols(C_CG, D_B, dp * hbv)
        dh = dh + dz_cols(C_HB, D_B, dp * cg)

        xv = x_ref[...]
        r = lax.rsqrt(jnp.mean(xv * xv, axis=-1, keepdims=True) + RMS_EPS)
        xn = xv * r
        dg1_ref[...] += _colsum8(dh * xn)
        dxn = dh * g1_ref[...]
        dx_ref[...] = dx2_ref[...] + r * (dxn - xn * jnp.mean(dxn * xn, axis=-1, keepdims=True))

    outs = [
        jax.ShapeDtypeStruct((t_len, D_MODEL), F32),
        jax.ShapeDtypeStruct((t_len, D_IN), BF16),
        jax.ShapeDtypeStruct((t_len, D_MODEL), BF16),
        jax.ShapeDtypeStruct((t_len, D_MODEL), BF16),
        jax.ShapeDtypeStruct((t_len, D_MODEL), BF16),
        jax.ShapeDtypeStruct((8, D_MODEL), F32),
        jax.ShapeDtypeStruct((8, 2 * D_MODEL), F32),
        jax.ShapeDtypeStruct((8, D_A), F32),
        jax.ShapeDtypeStruct((8, D_A), F32),
        jax.ShapeDtypeStruct((A_HEADS, 128, 128), F32),
        jax.ShapeDtypeStruct((128, D_A), F32),
        jax.ShapeDtypeStruct((3, 8, D_B), F32),
    ]

    return _staged_call(
        core, name=f"mixer_bwd_l{layer}", grid=(nt,),
        in_specs=[_row_spec(tm, D_MODEL, nt), _row_spec(tm, D_MODEL, nt), _row_spec(tm, 3 * D_B, nt),
                  _row_spec(tm, D_B, nt), _row_spec(tm, D_MODEL, nt), _row_spec(tm, D_MODEL, nt),
                  _row_spec(tm, D_MODEL, nt), _row_spec(tm, D_MODEL, nt), _row_spec(tm, D_A, nt), _row_spec(tm, D_A, nt),
                  _row_spec(tm, D_A, nt), _row_spec(tm, D_A, nt),
                  _const_spec((1, D_MODEL)), _const_spec((1, D_A)), _const_spec((1, D_A)),
                  _const_spec((A_HEADS, 128, 128)), _const_spec((8, D_B)), ANY, ANY, ANY],
        out_specs=[_row_spec(tm, D_MODEL, nt), _row_spec(tm, D_IN, nt), _row_spec(tm, D_MODEL, nt),
                   _row_spec(tm, D_MODEL, nt), _row_spec(tm, D_MODEL, nt),
                   _const_spec((8, D_MODEL)), _const_spec((8, 2 * D_MODEL)), _const_spec((8, D_A)), _const_spec((8, D_A)),
                   _const_spec((A_HEADS, 128, 128)), _const_spec((128, D_A)), _const_spec((3, 8, D_B))],
        out_shape=outs,
        scratch_shapes=[pltpu.VMEM((D_MODEL, D_IN), BF16), pltpu.VMEM((2, D_A, D_MODEL), BF16),
                        pltpu.VMEM((D_MODEL, D_MODEL), BF16), pltpu.VMEM((8, D_B), F32),
                        pltpu.VMEM((tm, D_A), BF16), pltpu.VMEM((tm, D_A), BF16), pltpu.VMEM((tm, D_A), F32),
                        pltpu.SemaphoreType.DMA((16,))],
        args=[dx2, x, zc, qs, sa, ca, sb, cb, ug, fu, xhs, cv, g1, lng, lnb, wmt, wsc, win_g, wb_g, wout_g],
        stages=stages)


def _wgrad(name, layer, a, b, rows, cols, row_blk, col_blk, stages, a_first=0):
    t_len = a.shape[0]
    n = b.shape[1]
    tk = min(TK_WGRAD, t_len)
    col_sharded = n == N_CHIPS * cols
    m = rows if col_sharded else a.shape[1]
    grid = (m // row_blk, n // col_blk, t_len // tk)
    per_shard_c = cols // col_blk

    if col_sharded:
        out_shape = (N_CHIPS, rows, cols)
        out_spec = pl.BlockSpec((None, row_blk, col_blk), lambda i, j, k: (j // per_shard_c, i, j % per_shard_c))
    else:
        out_shape = (N_CHIPS * rows, cols)
        out_spec = pl.BlockSpec((row_blk, col_blk), lambda i, j, k: (i, j))

    def core(a_ref, b_ref, o_ref):
        @pl.when(pl.program_id(2) == 0)
        def _():
            o_ref[...] = jnp.zeros_like(o_ref)

        o_ref[...] += _dot_tn(a_ref[...], b_ref[...])

    own, outs = _staged_call(
        core, name=f"wgrad_{name}_l{layer}", grid=grid,
        in_specs=[pl.BlockSpec((tk, row_blk), lambda i, j, k: (k, a_first + i)),
                  pl.BlockSpec((tk, col_blk), lambda i, j, k: (k, j))],
        out_specs=[out_spec], out_shape=[jax.ShapeDtypeStruct(out_shape, F32)], scratch_shapes=[],
        args=[a, b], stages=stages)
    return [own[0].reshape(N_CHIPS, rows, cols)], outs


def _wgrad_branch(layer, ya, da, yb, db, stages):
    t_len = ya.shape[0]
    tk = min(TK_WGRAD, t_len)

    def core(ya_ref, da_ref, yb_ref, db_ref, o_ref):
        @pl.when(pl.program_id(1) == 0)
        def _():
            o_ref[...] = jnp.zeros_like(o_ref)

        o_ref[0:D_A, :] += _dot_tn(ya_ref[...], da_ref[...])
        o_ref[D_A:2 * D_A, :] += _dot_tn(yb_ref[...], db_ref[...])

    a_spec = pl.BlockSpec((tk, D_A), lambda j, k: (k, 0))
    d_spec = pl.BlockSpec((tk, 256), lambda j, k: (k, j))
    return _staged_call(
        core, name=f"wgrad_w_branch_l{layer}", grid=(N_CHIPS, t_len // tk),
        in_specs=[a_spec, d_spec, a_spec, d_spec],
        out_specs=[pl.BlockSpec((None, 2 * D_A, 256), lambda j, k: (j, 0, 0))],
        out_shape=[jax.ShapeDtypeStruct((N_CHIPS, 2 * D_A, 256), F32)], scratch_shapes=[],
        args=[ya, da, yb, db], stages=stages)


def _flat_blk(rows, cols):
    blk = rows
    while blk * cols * 4 > 2 * 1024 * 1024 and blk % 16 == 0:
        blk //= 2
    return blk


def _cast_into_slots(name, layer, ws, chip):
    blks = [_flat_blk(w.shape[1], w.shape[2]) for w in ws]
    nblks = [w.shape[1] // b for w, b in zip(ws, blks)]
    n = len(ws)

    def body(chip_ref, *refs):
        for w_ref, o_ref in zip(refs[:n], refs[n:]):
            o_ref[...] = w_ref[...].astype(BF16)

    def in_spec(w, blk, nblk):
        return pl.BlockSpec((None, blk, w.shape[2]), lambda i, chip_ref: (layer, jnp.minimum(i, nblk - 1), 0))

    def out_spec(w, blk, nblk):
        return pl.BlockSpec((None, blk, w.shape[2]), lambda i, chip_ref: (chip_ref[0], jnp.minimum(i, nblk - 1), 0))

    return pl.pallas_call(
        body, name=f"cast_{name}_l{layer}",
        grid_spec=pltpu.PrefetchScalarGridSpec(
            num_scalar_prefetch=1, grid=(max(nblks),),
            in_specs=[in_spec(w, b, k) for w, b, k in zip(ws, blks, nblks)],
            out_specs=[out_spec(w, b, k) for w, b, k in zip(ws, blks, nblks)]),
        out_shape=[jax.ShapeDtypeStruct((N_CHIPS,) + w.shape[1:], BF16) for w in ws],
        compiler_params=_params(),
    )(chip, *ws)


def _reduction_sums(name, jobs, pos):
    in_specs, out_specs, out_shape, args, bodies, counts = [], [], [], [], [], []
    for job in jobs:
        kind, grad, other = job[0], job[1], job[2]
        _, h, cols = other.shape
        blk = _flat_blk(h, cols)
        nblk = h // blk
        if kind == "pair":
            total = N_CHIPS * nblk

            def block(s, total=total, nblk=nblk):
                b = jnp.minimum(s, total - 1)
                return b // nblk, b % nblk

            spec = pl.BlockSpec((None, blk, cols), lambda s, p, block=block: (block(s)[0], block(s)[1], 0))
            in_specs += [pl.BlockSpec((None, blk, cols), lambda s, p, block=block, nblk=nblk:
                                      (block(s)[0], p[1] * nblk + block(s)[1], 0)), spec]
            out_specs.append(spec)
            out_shape.append(jax.ShapeDtypeStruct((N_CHIPS, h, cols), BF16))
            args += [grad, other]
            bodies.append((2, lambda g, o, out: out.__setitem__(..., (g[...] + o[...]).astype(BF16))))
        else:
            total = nblk

            def block(s, total=total):
                return jnp.minimum(s, total - 1)

            in_specs += [pl.BlockSpec((None, blk, cols), lambda s, p, block=block, nblk=nblk:
                                      (p[0], p[1] * nblk + block(s), 0)),
                         pl.BlockSpec((None, blk, cols), lambda s, p, block=block: (p[0], block(s), 0)),
                         pl.BlockSpec((3, blk, cols), lambda s, p, block=block: (0, block(s), 0))]
            out_specs.append(pl.BlockSpec((blk, cols), lambda s, p, block=block, nblk=nblk: (p[1] * nblk + block(s), 0)))
            out_shape.append(jax.ShapeDtypeStruct((2 * h, cols), F32))
            args += [grad, other, job[3]]
            bodies.append((3, lambda g, o, r, out: out.__setitem__(
                ..., (((g[...] + o[...]) + r[0].astype(F32)) + r[1].astype(F32)) + r[2].astype(F32))))
        counts.append(total)

    def body(pos_ref, *refs):
        ins, outs = refs[:len(args)], refs[len(args):]
        k = 0
        for (n_in, fn), out in zip(bodies, outs):
            fn(*ins[k:k + n_in], out)
            k += n_in

    return pl.pallas_call(
        body, name=f"reduction_sums_{name}",
        grid_spec=pltpu.PrefetchScalarGridSpec(num_scalar_prefetch=1, grid=(max(counts),), in_specs=in_specs,
                                               out_specs=out_specs),
        out_shape=out_shape,
        compiler_params=_params(),
    )(pos, *args)


def _sum_slots(name, slots):
    n, rows, _ = slots.shape

    def body(s_ref, o_ref):
        acc = s_ref[0]
        for d in range(1, n):
            acc = acc + s_ref[d]
        o_ref[...] = acc

    return pl.pallas_call(
        body, name=f"sum_slots_{name}", grid=(1,),
        in_specs=[pl.BlockSpec((n, rows, 128), lambda i: (0, 0, 0))],
        out_specs=pl.BlockSpec((rows, 128), lambda i: (0, 0)),
        out_shape=jax.ShapeDtypeStruct((rows, 128), F32),
        compiler_params=_params(),
    )(slots)


def _adamw_math(w, g, m, v):
    m2 = ADAM_B1 * m + (1.0 - ADAM_B1) * g
    v2 = ADAM_B2 * v + (1.0 - ADAM_B2) * (g * g)
    m_hat = m2 / (1.0 - ADAM_B1 ** ADAM_STEP)
    v_hat = v2 / (1.0 - ADAM_B2 ** ADAM_STEP)
    delta = -ADAM_LR * (m_hat / (jnp.sqrt(v_hat) + ADAM_EPS) + ADAM_WD * w)
    return delta, m2, v2


def _adamw_big(name, w, g0, g1, m, v):
    _, rows, cols = w.shape
    blk = _flat_blk(rows, cols) // 2

    def body(w_ref, g0_ref, g1_ref, m_ref, v_ref, g_ref, d_ref, m2_ref, v2_ref):
        g = jnp.where(pl.program_id(0) == 0, g0_ref[...], g1_ref[...])
        d, m2, v2 = _adamw_math(w_ref[...], g, m_ref[...], v_ref[...])
        g_ref[...] = g
        d_ref[...] = d
        m2_ref[...] = m2
        v2_ref[...] = v2

    spec = pl.BlockSpec((None, blk, cols), lambda la, i: (la, i, 0))
    return pl.pallas_call(
        body, name=f"adamw_{name}", grid=(N_LAYERS, rows // blk),
        in_specs=[spec, pl.BlockSpec((blk, cols), lambda la, i: (i * (1 - la), 0)),
                  pl.BlockSpec((blk, cols), lambda la, i: (i * la, 0)), spec, spec],
        out_specs=[spec] * 4,
        out_shape=[jax.ShapeDtypeStruct(w.shape, F32)] * 4,
        compiler_params=_params(("parallel", "parallel")),
    )(w, g0, g1, m, v)


def _adamw_small(ws, gs, ms, vs):
    n = len(ws)

    def body(*refs):
        ins, outs = refs[:4 * n], refs[4 * n:]
        for k in range(n):
            d, m2, v2 = _adamw_math(ins[k][...], ins[n + k][...], ins[2 * n + k][...], ins[3 * n + k][...])
            outs[k][...] = d
            outs[n + k][...] = m2
            outs[2 * n + k][...] = v2

    vmem = pl.BlockSpec(memory_space=pltpu.VMEM)
    return pl.pallas_call(
        body, name="adamw_small",
        in_specs=[vmem] * (4 * n), out_specs=[vmem] * (3 * n),
        out_shape=[jax.ShapeDtypeStruct(w.shape, F32) for w in ws] * 3,
        compiler_params=pltpu.CompilerParams(vmem_limit_bytes=V7X_VMEM_LIMIT),
    )(*ws, *gs, *ms, *vs)


SMALL = ("norm1_g", "b_gate", "gmlp_ln_g", "gmlp_ln_b", "w_spatial", "b_spatial", "w_shortconv", "norm2_g",
         "w_ffn_conv", "b_ffn_conv", "final_g")
ALL_WEIGHTS = ("norm1_g", "w_in", "b_gate", "gmlp_ln_g", "gmlp_ln_b", "w_spatial", "b_spatial", "w_shortconv",
               "w_branch", "w_out", "norm2_g", "w_ffn_up", "w_ffn_conv", "b_ffn_conv", "w_ffn_down", "final_g")


def _pack(arrays):
    flat = jnp.concatenate([a.reshape(-1) for a in arrays])
    n = flat.shape[0]
    rows = -(-n // 1024) * 8
    return jnp.pad(flat, (0, rows * 128 - n)).reshape(rows, 128)


def _unpack(packed, like):
    flat = packed.reshape(-1)
    out, off = [], 0
    for a in like:
        out.append(flat[off:off + a.size].reshape(a.shape))
        off += a.size
    return out


def _pad8(w):
    return jnp.pad(w, ((0, 5), (0, 0)))


def kernel(x, norm1_g, w_in, b_gate, gmlp_ln_g, gmlp_ln_b, w_spatial, b_spatial, w_shortconv, w_branch, w_out, norm2_g, w_ffn_up, w_ffn_conv, b_ffn_conv, w_ffn_down, final_g, loss_target, m_norm1_g, m_w_in, m_b_gate, m_gmlp_ln_g, m_gmlp_ln_b, m_w_spatial, m_b_spatial, m_w_shortconv, m_w_branch, m_w_out, m_norm2_g, m_w_ffn_up, m_w_ffn_conv, m_b_ffn_conv, m_w_ffn_down, m_final_g, v_norm1_g, v_w_in, v_b_gate, v_gmlp_ln_g, v_gmlp_ln_b, v_w_spatial, v_b_spatial, v_w_shortconv, v_w_branch, v_w_out, v_norm2_g, v_w_ffn_up, v_w_ffn_conv, v_b_ffn_conv, v_w_ffn_down, v_final_g):
    weights = dict(norm1_g=norm1_g, w_in=w_in, b_gate=b_gate, gmlp_ln_g=gmlp_ln_g, gmlp_ln_b=gmlp_ln_b,
                   w_spatial=w_spatial, b_spatial=b_spatial, w_shortconv=w_shortconv, w_branch=w_branch, w_out=w_out,
                   norm2_g=norm2_g, w_ffn_up=w_ffn_up, w_ffn_conv=w_ffn_conv, b_ffn_conv=b_ffn_conv,
                   w_ffn_down=w_ffn_down, final_g=final_g)
    mom = dict(norm1_g=m_norm1_g, w_in=m_w_in, b_gate=m_b_gate, gmlp_ln_g=m_gmlp_ln_g, gmlp_ln_b=m_gmlp_ln_b,
               w_spatial=m_w_spatial, b_spatial=m_b_spatial, w_shortconv=m_w_shortconv, w_branch=m_w_branch,
               w_out=m_w_out, norm2_g=m_norm2_g, w_ffn_up=m_w_ffn_up, w_ffn_conv=m_w_ffn_conv,
               b_ffn_conv=m_b_ffn_conv, w_ffn_down=m_w_ffn_down, final_g=m_final_g)
    vel = dict(norm1_g=v_norm1_g, w_in=v_w_in, b_gate=v_b_gate, gmlp_ln_g=v_gmlp_ln_g, gmlp_ln_b=v_gmlp_ln_b,
               w_spatial=v_w_spatial, b_spatial=v_b_spatial, w_shortconv=v_w_shortconv, w_branch=v_w_branch,
               w_out=v_w_out, norm2_g=v_norm2_g, w_ffn_up=v_w_ffn_up, w_ffn_conv=v_w_ffn_conv,
               b_ffn_conv=v_b_ffn_conv, w_ffn_down=v_w_ffn_down, final_g=v_final_g)

    cx, cy, cc = _mesh_pos()
    chip = 2 * cx + cy
    chip_arr = chip.astype(jnp.int32).reshape(1)
    pos_arr = jnp.stack([chip, cc]).astype(jnp.int32)
    t_len = x.shape[1]
    xs = x.reshape(t_len, D_MODEL)
    target = loss_target.reshape(t_len, D_MODEL)
    pipe = _Pipe()

    full = {}

    def gather(group, names, la):
        slots = _cast_into_slots(group, la, [weights[n].reshape((N_LAYERS,) + BIG[n]) for n in names], chip_arr)

        def then(*bufs):
            full.update(zip([(n, la) for n in names], bufs))

        pipe.add(_gather_stage(slots, then))

    mixer_w = ("w_in", "w_branch", "w_out")
    ffn_w = ("w_ffn_up", "w_ffn_down")
    gather("mixer", mixer_w, 0)
    tap_slots = {}
    pipe.add(_chip_spread_stage(_pack([w_shortconv, w_ffn_conv]), lambda slots: tap_slots.__setitem__("all", slots)))
    pipe.flush()
    by_chip = [_unpack(tap_slots["all"][k], [w_shortconv, w_ffn_conv]) for k in range(N_CHIPS)]
    wsc_full = jnp.concatenate([t[0] for t in by_chip], axis=-1)
    wfc_full = jnp.concatenate([t[1] for t in by_chip], axis=-1)

    idx = jnp.arange(GMLP_BLOCK) // CHUNK
    mask = idx[None, :] <= idx[:, None]
    wm_all = jnp.where(mask[None, None], w_spatial, 0.0)
    wm_bf = wm_all.astype(BF16)
    wmt_bf = jnp.swapaxes(wm_all, -1, -2).astype(BF16)
    bsf = jnp.repeat(jnp.swapaxes(b_spatial, -1, -2), 128, axis=-1)

    def row(a):
        return a.reshape(1, -1)

    def mixer_args(la):
        return (row(norm1_g[la]), row(b_gate[la]), row(gmlp_ln_g[la]), row(gmlp_ln_b[la]))

    def mixer_weights(la):
        return tuple(full[(n, la)] for n in mixer_w)

    def ffn_weights(la):
        return tuple(full[(n, la)] for n in ffn_w)

    saved = []
    h_in = xs
    for la in range(N_LAYERS):
        gather("ffn", ffn_w, la)
        *kept, mg, h1, x2 = pipe.carry(lambda st: _mixer_fwd(
            la, h_in, *mixer_args(la), wm_bf[la], bsf[la], _pad8(wsc_full[la]), *mixer_weights(la), st))
        ya, yb = kept[1], kept[2]
        if la + 1 < N_LAYERS:
            gather("mixer", mixer_w, la + 1)
        head = (target, row(final_g)) if la == N_LAYERS - 1 else None
        up, silu, dsilu, act, h2, *rest = pipe.carry(lambda st: _ffn_fwd(
            la, x2, row(norm2_g[la]), _pad8(wfc_full[la]), row(b_ffn_conv[la]), *ffn_weights(la), st, head=head))
        saved.append(dict(x=h_in, ya=ya, yb=yb, mixer=[kept[0]] + kept[3:], mg=mg, h1=h1, x2=x2, up=up, silu=silu,
                          dsilu=dsilu, act=act, h2=h2))
        h_in = rest[0]
    dx, dgf8, loss8 = rest

    reduced_big = {}

    sums_due = []

    def run_sums():
        if sums_due:
            due = list(sums_due)
            sums_due.clear()
            run_sums.calls += 1
            for (_, then), res in zip(due, _reduction_sums(str(run_sums.calls), [job for job, _ in due], pos_arr)):
                then(res)

    run_sums.calls = 0
    pipe.after = run_sums

    def reduce_big(name, la, grad):
        def after_pair(other):
            def after_chips(got):
                sums_due.append((("chip", grad, other, got), lambda final: pipe.add(_pair_fill_stage(
                    final, lambda done: reduced_big.__setitem__((name, la), done)))))

            sums_due.append((("pair", grad, other), lambda psum: pipe.add(_chip_send_stage(psum, after_chips))))

        pipe.add(_pair_send_stage(grad, after_pair))

    small = {n: [None] * N_LAYERS for n in SMALL}
    spread = {}
    for la in reversed(range(N_LAYERS)):
        s = saved[la]
        dx3 = dx
        dx2, dup, dx3b, dg2, dbfc, dwfc = pipe.carry(lambda st: _ffn_bwd(
            la, dx3, s["x2"], s["up"], s["silu"], s["dsilu"], row(norm2_g[la]), _pad8(wfc_full[la]),
            *ffn_weights(la), st))
        g, = pipe.carry(lambda st: _wgrad("w_ffn_up", la, s["h2"], dup, 1024, 1408, 1024, 1408, st))
        reduce_big("w_ffn_up", la, g)
        g, = pipe.carry(lambda st: _wgrad("w_ffn_down", la, s["act"], dx3b, 704, 1024, 1408, 1024, st))
        reduce_big("w_ffn_down", la, g)
        run = pipe.carry if la > 0 else (lambda call: call([])[0])
        dxl, dz, da, db, dx2b, dg1, dbg, dlng, dlnb, dwm, dbsf, dwsc = run(lambda st: _mixer_bwd(
            la, dx2, s["x"], *s["mixer"], row(norm1_g[la]), row(gmlp_ln_g[la]), row(gmlp_ln_b[la]), wmt_bf[la],
            _pad8(wsc_full[la]), *mixer_weights(la), st))
        small["norm1_g"][la] = dg1.sum(0)
        small["b_gate"][la] = dbg.sum(0)
        small["gmlp_ln_g"][la] = dlng.sum(0)
        small["gmlp_ln_b"][la] = dlnb.sum(0)
        small["w_spatial"][la] = jnp.where(mask[None], dwm, 0.0)
        small["b_spatial"][la] = dbsf.reshape(128, A_HEADS, 128).sum(-1).T
        small["w_shortconv"][la] = dwsc.sum(1)
        small["norm2_g"][la] = dg2.sum(0)
        small["w_ffn_conv"][la] = dwfc.sum(1)
        small["b_ffn_conv"][la] = dbfc.sum(0)
        if la == 0:
            small_local = ([jnp.stack(small[n]) for n in SMALL[:-1]]
                           + [dgf8.sum(0), 0.5 * loss8.sum().reshape(1) / D_MODEL])
            mine = _pack(small_local)

            def after_swap(other, mine=mine):
                pair = _sum_slots("small_pair", jnp.stack([mine, other]))
                pipe.add(_chip_spread_stage(pair, lambda slots: spread.__setitem__("slots", slots)))

            pipe.add(_pair_swap_stage(mine, after_swap))
        if la > 0:
            g, = pipe.carry(lambda st: _wgrad("w_in", la, s["h1"], dz, 1024, 1152, 1024, 1152, st))
            reduce_big("w_in", la, g)
        else:
            for part, tag in enumerate(("w_in_a", "w_in_b")):
                g, = pipe.carry(lambda st: _wgrad(tag, la, s["h1"], dz, 512, 1152, 512, 1152, st, a_first=part))
                reduce_big(tag, la, g)
        g, = pipe.carry(lambda st: _wgrad("w_out", la, s["mg"], dx2b, 256, 1024, 1024, 1024, st), long=False)
        reduce_big("w_out", la, g)
        g, = pipe.carry(lambda st: _wgrad_branch(la, s["ya"], da, s["yb"], db, st), long=False)
        reduce_big("w_branch", la, g)
        dx = dxl
    grad_x = dx.reshape(x.shape)
    pipe.flush()

    reduced_big[("w_in", 0)] = jnp.concatenate([reduced_big[("w_in_a", 0)], reduced_big[("w_in_b", 0)]], axis=0)
    reduced = _unpack(_sum_slots("small_grads", spread["slots"]), small_local)
    loss = reduced[-1].reshape(())
    grads = dict(zip(SMALL, reduced[:-1]))
    grads["w_shortconv"] = lax.dynamic_slice(grads["w_shortconv"], (0, 0, chip * (D_B // 4)), (N_LAYERS, 3, D_B // 4))
    grads["w_ffn_conv"] = lax.dynamic_slice(grads["w_ffn_conv"], (0, 0, chip * (D_FF // 4)), (N_LAYERS, 3, D_FF // 4))

    delta, new_m, new_v = {}, {}, {}
    for n in BIG_NAMES:
        shape3 = (N_LAYERS,) + BIG[n]
        res = _adamw_big(n, weights[n].reshape(shape3), reduced_big[(n, 0)], reduced_big[(n, 1)],
                         mom[n].reshape(shape3), vel[n].reshape(shape3))
        grads[n], delta[n], new_m[n], new_v[n] = (a.reshape(weights[n].shape) for a in res)
    res = _adamw_small(*[[src[n].reshape(-1, src[n].shape[-1]) for n in SMALL] for src in (weights, grads, mom, vel)])
    for k, n in enumerate(SMALL):
        delta[n], new_m[n], new_v[n] = (res[j * len(SMALL) + k].reshape(weights[n].shape) for j in range(3))

    return (loss, grad_x, *[grads[n] for n in ALL_WEIGHTS], *[delta[n] for n in ALL_WEIGHTS],
            *[new_m[n] for n in ALL_WEIGHTS], *[new_v[n] for n in ALL_WEIGHTS])
```

```python
import jax
import jax.numpy as jnp
from jax import lax
from jax.experimental import pallas as pl
from jax.experimental.pallas import tpu as pltpu

F32 = jnp.float32
BF16 = jnp.bfloat16
MESH = pl.DeviceIdType.MESH
ANY = pl.BlockSpec(memory_space=pl.ANY)

D_MODEL = 1024
D_A = 512
D_B = 512
D_IN = 4608
D_FF = 2816
GMLP_BLOCK = 128
CHUNK = 64
A_HEADS = 4
N_LAYERS = 2
N_CHIPS = 4
RMS_EPS = 1e-6
LN_EPS = 1e-5
ADAM_LR = 0.001
ADAM_B1 = 0.9
ADAM_B2 = 0.999
ADAM_EPS = 1e-08
ADAM_WD = 0.01
ADAM_STEP = 10

C_U, C_V, C_BG, C_CG, C_HB, C_GA, C_GB = 0, 512, 1024, 1536, 2048, 2560, 3584

V7X_VMEM_LIMIT = 60 * 1024 * 1024
TM_MIX = 256
TM_FFN = 256
TK_WGRAD = 2048
SLOW_COPY_BYTES = 640 * 1024
FF_CHUNKS = ((0, 768), (768, 1536), (1536, 2304), (2304, 2816))
GELU_C0 = 0.7978845608028654
GELU_C1 = 0.044715

BIG = {
    "w_in": (1024, 1152),
    "w_branch": (1024, 256),
    "w_out": (256, 1024),
    "w_ffn_up": (1024, 1408),
    "w_ffn_down": (704, 1024),
}
BIG_NAMES = tuple(BIG)


def _params(sem=("arbitrary",), vmem=V7X_VMEM_LIMIT):
    return pltpu.CompilerParams(dimension_semantics=sem, vmem_limit_bytes=vmem)


def _gelu(x):
    x2 = x * x
    t = jnp.tanh(GELU_C0 * x * (1.0 + GELU_C1 * x2))
    return 0.5 * x * (1.0 + t), t


def _gelu_grad(x, t):
    return 0.5 * (1.0 + t) + 0.5 * x * (1.0 - t * t) * GELU_C0 * (1.0 + 3.0 * GELU_C1 * x * x)


def _colsum8(v):
    r, n = v.shape
    return v.reshape(r // 8, 8, n).sum(axis=0)


def _dot(a, b):
    return jnp.dot(a, b, preferred_element_type=F32)


def _dot_nt(a, b):
    return lax.dot_general(a, b, (((1,), (1,)), ((), ())), preferred_element_type=F32)


def _dot_tn(a, b):
    return lax.dot_general(a, b, (((0,), (0,)), ((), ())), preferred_element_type=F32)


def _shift_down(v, carry, n):
    rows = lax.broadcasted_iota(jnp.int32, (8, v.shape[1]), 0)
    out = pltpu.roll(v, n, 0)
    head = out[0:8, :]
    for r in range(n):
        head = jnp.where(rows == r, carry[8 - n + r:8 - n + r + 1, :], head)
    return jnp.concatenate([head, out[8:, :]], axis=0)


def _shift_up(v, carry, n):
    tm = v.shape[0]
    rows = lax.broadcasted_iota(jnp.int32, (8, v.shape[1]), 0)
    out = pltpu.roll(v, tm - n, 0)
    tail = out[tm - 8:tm, :]
    for r in range(n):
        tail = jnp.where(rows == 8 - n + r, carry[r:r + 1, :], tail)
    return jnp.concatenate([out[0:tm - 8, :], tail], axis=0)


def _sigmoid(x):
    return 0.5 * jnp.tanh(0.5 * x) + 0.5


def _start_all(copies):
    for cp in copies:
        cp.start()


def _wait_all(copies):
    for cp in copies:
        cp.wait()


def _load_col_sharded(src, dst, sems, first):
    cs = src.shape[-1]
    return [pltpu.make_async_copy(src.at[k], dst.at[:, k * cs:(k + 1) * cs], sems.at[first + k])
            for k in range(N_CHIPS)]


def _load_row_sharded(src, dst, sems, first):
    rs = src.shape[-2]
    return [pltpu.make_async_copy(src.at[k], dst.at[k * rs:(k + 1) * rs, :], sems.at[first + k])
            for k in range(N_CHIPS)]


def _load_branch(src, dst, sems, first):
    return [pltpu.make_async_copy(src.at[k, pl.ds(m * D_A, D_A), :], dst.at[m, :, k * 256:(k + 1) * 256],
                                  sems.at[first + 2 * k + m])
            for k in range(N_CHIPS) for m in range(2)]


def _row_spec(tm, n, rev=None):
    if rev is None:
        return pl.BlockSpec((tm, n), lambda i: (i, 0))
    return pl.BlockSpec((tm, n), lambda i: (rev - 1 - i, 0))


def _const_spec(shape):
    nd = len(shape)
    return pl.BlockSpec(shape, lambda i: (0,) * nd)


def _mesh_pos():
    return lax.axis_index("x"), lax.axis_index("y"), lax.axis_index("c")


def _other_chips(x, y):
    return [(1 - x, y, 2 * (1 - x) + y), (x, 1 - y, 2 * x + (1 - y)), (1 - x, 1 - y, 2 * (1 - x) + (1 - y))]


def _remote(src, dst, ssem, rsem, to):
    return pltpu.make_async_remote_copy(src_ref=src, dst_ref=dst, send_sem=ssem, recv_sem=rsem, device_id=to,
                                        device_id_type=MESH)


def _half(ref, which, h):
    start = pl.multiple_of(which * h, 8)
    if len(ref.shape) == 2:
        return ref.at[pl.ds(start, h), :]
    return ref.at[:, pl.ds(start, h), :]


class _Stage:
    def __init__(self, ins=(), inouts=(), outs=(), n_sems=0, start=None, mid=None, finish=None, then=None, slow=False):
        self.ins, self.inouts, self.outs = list(ins), list(inouts), list(outs)
        self.n_sems, self.start, self.mid, self.finish, self.then = n_sems, start, mid, finish, then
        self.slow = slow


def _gather_stage(bufs, then):
    n = len(bufs)

    def copies(io, sem):
        x, y, c = _mesh_pos()
        me = 2 * x + y
        ici, fwd, got = [], [], []
        for w in range(n):
            h = io[w].shape[1] // 2
            for j, (px, py, pk) in enumerate(_other_chips(x, y)):
                mine = _half(io[w].at[me], c, h)
                theirs = _half(io[w].at[pk], c, h)
                ici.append(_remote(mine, mine, sem(12 * w + j), sem(12 * w + 3 + j), (px, py, c)))
                got.append(_remote(theirs, theirs, sem(12 * w + j), sem(12 * w + 3 + j), (px, py, c)))
                fwd.append(_remote(theirs, theirs, sem(12 * w + 6 + j), sem(12 * w + 9 + j), (x, y, 1 - c)))
        return ici, got, fwd

    def start(ins, io, outs, sem):
        _start_all(copies(io, sem)[0])

    def mid(ins, io, outs, sem):
        _, got, fwd = copies(io, sem)
        for g, f in zip(got, fwd):
            g.wait_recv()
            f.start()

    def finish(ins, io, outs, sem):
        x, y, c = _mesh_pos()
        ici, _, fwd = copies(io, sem)
        for w in range(n):
            h = io[w].shape[1] // 2
            for j, (px, py, pk) in enumerate(_other_chips(x, y)):
                other = _half(io[w].at[pk], 1 - c, h)
                _remote(other, other, sem(12 * w + 6 + j), sem(12 * w + 9 + j), (x, y, 1 - c)).wait_recv()
        for cp in ici + fwd:
            cp.wait_send()

    return _Stage(inouts=bufs, n_sems=12 * n, start=start, mid=mid, finish=finish, then=then)


def _pair_send_stage(grad, then):
    h = grad.shape[1] // 2

    def copy(ins, outs, sem):
        x, y, c = _mesh_pos()
        return _remote(_half(ins[0], 1 - c, h), outs[0], sem(0), sem(1), (x, y, 1 - c))

    return _Stage(ins=[grad], outs=[jax.ShapeDtypeStruct((N_CHIPS, h, grad.shape[2]), F32)], n_sems=2,
                  start=lambda ins, io, outs, sem: copy(ins, outs, sem).start(),
                  finish=lambda ins, io, outs, sem: copy(ins, outs, sem).wait(), then=then)


def _chip_send_stage(psum, then):
    def copies(ins, outs, sem):
        x, y, c = _mesh_pos()
        return [_remote(ins[0].at[pk], outs[0].at[j], sem(j), sem(3 + j), (px, py, c))
                for j, (px, py, pk) in enumerate(_other_chips(x, y))]

    return _Stage(ins=[psum], outs=[jax.ShapeDtypeStruct((3,) + psum.shape[1:], BF16)], n_sems=6,
                  start=lambda ins, io, outs, sem: _start_all(copies(ins, outs, sem)),
                  finish=lambda ins, io, outs, sem: _wait_all(copies(ins, outs, sem)), then=then,
                  slow=psum.shape[1] * psum.shape[2] * 2 > SLOW_COPY_BYTES)


def _pair_fill_stage(final, then):
    h = final.shape[0] // 2

    def copy(io, sem):
        x, y, c = _mesh_pos()
        mine = _half(io[0], c, h)
        return _remote(mine, mine, sem(0), sem(1), (x, y, 1 - c))

    return _Stage(inouts=[final], n_sems=2,
                  start=lambda ins, io, outs, sem: copy(io, sem).start(),
                  finish=lambda ins, io, outs, sem: copy(io, sem).wait(), then=then)


def _pair_swap_stage(packed, then):
    def copy(ins, outs, sem):
        x, y, c = _mesh_pos()
        return _remote(ins[0], outs[0], sem(0), sem(1), (x, y, 1 - c))

    return _Stage(ins=[packed], outs=[jax.ShapeDtypeStruct(packed.shape, F32)], n_sems=2,
                  start=lambda ins, io, outs, sem: copy(ins, outs, sem).start(),
                  finish=lambda ins, io, outs, sem: copy(ins, outs, sem).wait(), then=then)


def _chip_spread_stage(psum, then):
    def copies(ins, outs, sem):
        x, y, c = _mesh_pos()
        me = 2 * x + y
        cps = [_remote(ins[0], outs[0].at[me], sem(j), sem(3 + j), (px, py, c))
               for j, (px, py, pk) in enumerate(_other_chips(x, y))]
        return cps, pltpu.make_async_copy(ins[0], outs[0].at[me], sem(6))

    def start(ins, io, outs, sem):
        cps, own = copies(ins, outs, sem)
        own.start()
        _start_all(cps)

    def finish(ins, io, outs, sem):
        cps, own = copies(ins, outs, sem)
        _wait_all(cps)
        own.wait()

    return _Stage(ins=[psum], outs=[jax.ShapeDtypeStruct((N_CHIPS,) + psum.shape, F32)], n_sems=7,
                  start=start, finish=finish, then=then)


def _staged_call(core, *, name, grid, in_specs, out_specs, out_shape, scratch_shapes, args, stages):
    n_in, n_out, n_scr = len(args), len(out_shape), len(scratch_shapes)
    s_args, s_outs, aliases, layout = [], [], {}, []
    n_sems = 0
    for st in stages:
        i0, o0 = len(s_args), len(s_outs)
        s_args += st.ins + st.inouts
        for q in range(len(st.inouts)):
            aliases[n_in + i0 + len(st.ins) + q] = n_out + o0 + q
        s_outs += [jax.ShapeDtypeStruct(a.shape, a.dtype) for a in st.inouts] + st.outs
        layout.append((i0, o0, n_sems))
        n_sems += st.n_sems
    steps = 1
    for g in grid:
        steps *= g

    def body(*refs):
        own_in = refs[:n_in]
        s_in = refs[n_in:n_in + len(s_args)]
        rest = refs[n_in + len(s_args):]
        own_out = rest[:n_out]
        s_out = rest[n_out:n_out + len(s_outs)]
        scr = rest[n_out + len(s_outs):]

        def run(which):
            for st, (i0, o0, s0) in zip(stages, layout):
                fn = getattr(st, which)
                if fn is not None:
                    fn(s_in[i0:i0 + len(st.ins)], s_out[o0:o0 + len(st.inouts)],
                       s_out[o0 + len(st.inouts):o0 + len(st.inouts) + len(st.outs)],
                       lambda k, s0=s0: scr[n_scr].at[s0 + k])

        if not stages:
            core(*own_in, *own_out, *scr[:n_scr])
            return
        step = 0
        for d, g in enumerate(grid):
            step = step * g + pl.program_id(d)
        if steps == 1:
            run("start")
            core(*own_in, *own_out, *scr[:n_scr])
            run("mid")
            run("finish")
            return
        pl.when(step == 0)(lambda: run("start"))
        core(*own_in, *own_out, *scr[:n_scr])
        pl.when(step == (3 * steps) // 4)(lambda: run("mid"))
        pl.when(step == steps - 1)(lambda: run("finish"))

    sem = ("arbitrary",) * len(grid) if stages else ("parallel",) * max(len(grid) - 1, 0) + ("arbitrary",) * min(len(grid), 1)
    res = pl.pallas_call(
        body, name=name, grid=grid,
        in_specs=list(in_specs) + [ANY] * len(s_args),
        out_specs=list(out_specs) + [ANY] * len(s_outs),
        out_shape=list(out_shape) + s_outs,
        input_output_aliases=aliases,
        scratch_shapes=list(scratch_shapes) + ([pltpu.SemaphoreType.DMA((n_sems,))] if stages else []),
        compiler_params=_params(sem) if grid else pltpu.CompilerParams(vmem_limit_bytes=V7X_VMEM_LIMIT),
    )(*args, *s_args)
    return list(res[:n_out]), list(res[n_out:])


class _Pipe:
    def __init__(self):
        self.ready = []
        self.flushes = 0
        self.after = None

    def add(self, stage):
        self.ready.append(stage)

    def carry(self, call, long=True):
        stages = [st for st in self.ready if long or not st.slow]
        self.ready = [st for st in self.ready if not (long or not st.slow)]
        own, outs = call(stages)
        k = 0
        for st in stages:
            n = len(st.inouts) + len(st.outs)
            st.then(*outs[k:k + n])
            k += n
        if self.after is not None:
            self.after()
        return own

    def flush(self):
        while self.ready:
            self.flushes += 1
            self.carry(lambda stages: _staged_call(
                lambda *refs: None, name=f"comm_tail_{self.flushes}", grid=(), in_specs=[], out_specs=[], out_shape=[],
                scratch_shapes=[], args=[], stages=stages))


def _mixer_fwd(layer, x, g1, bgate, lng, lnb, wm, bsf, wsc, win_g, wb_g, wout_g, stages):
    t_len = x.shape[0]
    tm = min(TM_MIX, t_len)
    nt = t_len // tm
    nb = tm // GMLP_BLOCK

    def core(x_ref, x_late_ref, g1_ref, bgate_ref, lng_ref, lnb_ref, wm_ref, bsf_ref, wsc_ref, win_hbm, wb_hbm, wout_hbm,
             zc_ref, ya_ref, yb_ref, q_ref, sa_ref, ca_ref, sb_ref, cb_ref, ug_ref, fu_ref, xh_ref, cv_ref,
             mg_ref, h_ref, x2_ref,
             win_v, wb_v, wout_v, carry, vn_s, f_s, z_s, sems):
        i = pl.program_id(0)

        @pl.when(i == 0)
        def _():
            cps = (_load_col_sharded(win_hbm, win_v, sems, 0) + _load_branch(wb_hbm, wb_v, sems, 4)
                   + _load_row_sharded(wout_hbm, wout_v, sems, 12))
            _start_all(cps)
            carry[...] = jnp.zeros_like(carry)
            z_s[...] = jnp.zeros_like(z_s)
            _wait_all(cps)

        xv = x_ref[...]
        r = lax.rsqrt(jnp.mean(xv * xv, axis=-1, keepdims=True) + RMS_EPS)
        h_ref[...] = (xv * r * g1_ref[...]).astype(BF16)

        def zcols(c0, n, keep=None):
            zv = z_s[:, c0:c0 + n]
            z_s[:, c0:c0 + n] = _dot(h_ref[...], win_v[:, c0:c0 + n])
            if keep is not None:
                zc_ref[:, keep * D_B:(keep + 1) * D_B] = zv.astype(BF16)
            return zv

        v = zcols(C_V, D_A)
        vg, tv = _gelu(v)
        mu = jnp.mean(vg, axis=-1, keepdims=True)
        vc = vg - mu
        rstd = lax.rsqrt(jnp.mean(vc * vc, axis=-1, keepdims=True) + LN_EPS)
        xh = vc * rstd
        xh_ref[...] = xh.astype(BF16)
        cv_ref[...] = (rstd * _gelu_grad(v, tv)).astype(BF16)
        vn_s[...] = (xh * lng_ref[...] + lnb_ref[...]).astype(BF16)
        for hd in range(A_HEADS):
            cols = slice(hd * 128, (hd + 1) * 128)
            vcat = jnp.concatenate([vn_s[b * 128:(b + 1) * 128, cols] for b in range(nb)], axis=1)
            fcat = _dot(wm_ref[hd], vcat)
            for b in range(nb):
                f_s[b * 128:(b + 1) * 128, cols] = fcat[:, b * 128:(b + 1) * 128]
        u = zcols(C_U, D_A)
        ug, tu = _gelu(u)
        ug_ref[...] = ug.astype(BF16)
        fb = f_s[...] + jnp.concatenate([bsf_ref[...]] * nb, axis=0)
        fu_ref[...] = (fb * _gelu_grad(u, tu)).astype(BF16)
        ya_ref[...] = (ug * fb).astype(BF16)

        p = zcols(C_CG, D_B, keep=1) * zcols(C_HB, D_B, keep=2)
        cr = carry[...]
        q = wsc_ref[0:1, :] * _shift_down(p, cr, 2) + wsc_ref[1:2, :] * _shift_down(p, cr, 1) + wsc_ref[2:3, :] * p
        carry[...] = p[tm - 8:tm, :]
        q_ref[...] = q.astype(BF16)
        yb_ref[...] = (zcols(C_BG, D_B, keep=0) * q).astype(BF16)

        av = _dot(ya_ref[...], wb_v[0])
        sa = _sigmoid(zcols(C_GA, D_MODEL) + bgate_ref[:, 0:D_MODEL])
        sa_ref[...] = sa.astype(BF16)
        mg = sa * av
        ca_ref[...] = (mg * (1.0 - sa)).astype(BF16)
        bv = _dot(yb_ref[...], wb_v[1])
        sb = _sigmoid(zcols(C_GB, D_MODEL) + bgate_ref[:, D_MODEL:2 * D_MODEL])
        sb_ref[...] = sb.astype(BF16)
        mb = sb * bv
        cb_ref[...] = (mb * (1.0 - sb)).astype(BF16)
        mg_ref[...] = (mg + mb).astype(BF16)
        x2_ref[...] = x_late_ref[...] + _dot(mg_ref[...], wout_v[...])

    def tile(n, lag):
        return pl.BlockSpec((tm, n), lambda i: (jnp.clip(i - lag, 0, nt - 1), 0))

    outs = [
        jax.ShapeDtypeStruct((t_len, 3 * D_B), BF16),
        jax.ShapeDtypeStruct((t_len, D_A), BF16),
        jax.ShapeDtypeStruct((t_len, D_B), BF16),
        jax.ShapeDtypeStruct((t_len, D_B), BF16),
        jax.ShapeDtypeStruct((t_len, D_MODEL), BF16),
        jax.ShapeDtypeStruct((t_len, D_MODEL), BF16),
        jax.ShapeDtypeStruct((t_len, D_MODEL), BF16),
        jax.ShapeDtypeStruct((t_len, D_MODEL), BF16),
        jax.ShapeDtypeStruct((t_len, D_A), BF16),
        jax.ShapeDtypeStruct((t_len, D_A), BF16),
        jax.ShapeDtypeStruct((t_len, D_A), BF16),
        jax.ShapeDtypeStruct((t_len, D_A), BF16),
        jax.ShapeDtypeStruct((t_len, D_MODEL), BF16),
        jax.ShapeDtypeStruct((t_len, D_MODEL), BF16),
        jax.ShapeDtypeStruct((t_len, D_MODEL), F32),
    ]
    return _staged_call(
        core, name=f"mixer_fwd_l{layer}", grid=(nt + 1,),
        in_specs=[tile(D_MODEL, 0), tile(D_MODEL, 1), _const_spec((1, D_MODEL)), _const_spec((1, 2 * D_MODEL)),
                  _const_spec((1, D_A)), _const_spec((1, D_A)), _const_spec((A_HEADS, 128, 128)),
                  _const_spec((128, D_A)), _const_spec((8, D_B)), ANY, ANY, ANY],
        out_specs=[tile(o.shape[1], 0 if k == len(outs) - 2 else 1) for k, o in enumerate(outs)],
        out_shape=outs,
        scratch_shapes=[pltpu.VMEM((D_MODEL, D_IN), BF16), pltpu.VMEM((2, D_A, D_MODEL), BF16),
                        pltpu.VMEM((D_MODEL, D_MODEL), BF16), pltpu.VMEM((8, D_B), F32),
                        pltpu.VMEM((tm, D_A), BF16), pltpu.VMEM((tm, D_A), F32), pltpu.VMEM((tm, D_IN), F32),
                        pltpu.SemaphoreType.DMA((16,))],
        args=[x, x, g1, bgate, lng, lnb, wm, bsf, wsc, win_g, wb_g, wout_g], stages=stages)


def _ffn_fwd(layer, x2, g2, wfc, bfc, wup_g, wdown_g, stages, head=None):
    t_len = x2.shape[0]
    tm = min(TM_FFN, t_len)
    nt = t_len // tm

    def core(*refs):
        if head is None:
            (x_ref, g2_ref, wfc_ref, bfc_ref, wup_hbm, wdown_hbm, up_ref, silu_ref, dsilu_ref, act_ref, h_ref, x3_ref,
             wup_v, wdown_v, carry, sems) = refs
        else:
            (x_ref, g2_ref, wfc_ref, bfc_ref, t_ref, gf_ref, wup_hbm, wdown_hbm, up_ref, silu_ref, dsilu_ref, act_ref,
             h_ref, dx_ref, dgf_ref, loss_ref, wup_v, wdown_v, carry, sems) = refs
        i = pl.program_id(0)

        @pl.when(i == 0)
        def _():
            cps = _load_col_sharded(wup_hbm, wup_v, sems, 0) + _load_row_sharded(wdown_hbm, wdown_v, sems, 4)
            _start_all(cps)
            carry[...] = jnp.zeros_like(carry)
            if head is not None:
                dgf_ref[...] = jnp.zeros_like(dgf_ref)
                loss_ref[...] = jnp.zeros_like(loss_ref)
            _wait_all(cps)

        xv = x_ref[...]
        r = lax.rsqrt(jnp.mean(xv * xv, axis=-1, keepdims=True) + RMS_EPS)
        h_ref[...] = (xv * r * g2_ref[...]).astype(BF16)
        gate = _dot(h_ref[...], wup_v[:, 0:D_FF])
        up_ref[:, 0:D_FF] = gate.astype(BF16)
        cr = carry[...]
        gc = (wfc_ref[0:1, :] * _shift_down(gate, cr, 2) + wfc_ref[1:2, :] * _shift_down(gate, cr, 1)
              + wfc_ref[2:3, :] * gate + bfc_ref[...])
        carry[...] = gate[tm - 8:tm, :]
        sg = _sigmoid(gc)
        silu = gc * sg
        silu_ref[...] = silu.astype(BF16)
        dsilu_ref[...] = (sg + silu * (1.0 - sg)).astype(BF16)
        val = _dot(h_ref[...], wup_v[:, D_FF:2 * D_FF])
        up_ref[:, D_FF:2 * D_FF] = val.astype(BF16)
        act_ref[...] = (silu * val).astype(BF16)
        x3 = x_ref[...] + _dot(act_ref[...], wdown_v[...])
        if head is None:
            x3_ref[...] = x3
        else:
            r3 = lax.rsqrt(jnp.mean(x3 * x3, axis=-1, keepdims=True) + RMS_EPS)
            xh = x3 * r3
            err = xh * gf_ref[...] - t_ref[...]
            loss_ref[...] += _colsum8(err * err)
            dy = err * (1.0 / D_MODEL)
            dgf_ref[...] += _colsum8(dy * xh)
            dxh = dy * gf_ref[...]
            dx_ref[...] = r3 * (dxh - xh * jnp.mean(dxh * xh, axis=-1, keepdims=True))

    outs = [
        jax.ShapeDtypeStruct((t_len, 2 * D_FF), BF16),
        jax.ShapeDtypeStruct((t_len, D_FF), BF16),
        jax.ShapeDtypeStruct((t_len, D_FF), BF16),
        jax.ShapeDtypeStruct((t_len, D_FF), BF16),
        jax.ShapeDtypeStruct((t_len, D_MODEL), BF16),
        jax.ShapeDtypeStruct((t_len, D_MODEL), F32),
    ]
    in_specs = [_row_spec(tm, D_MODEL), _const_spec((1, D_MODEL)), _const_spec((8, D_FF)), _const_spec((1, D_FF))]
    out_specs = [_row_spec(tm, o.shape[1]) for o in outs]
    args = [x2, g2, wfc, bfc]
    if head is not None:
        in_specs += [_row_spec(tm, D_MODEL), _const_spec((1, D_MODEL))]
        args += list(head)
        outs += [jax.ShapeDtypeStruct((8, D_MODEL), F32)] * 2
        out_specs += [_const_spec((8, D_MODEL))] * 2
    return _staged_call(
        core, name=f"ffn_fwd_l{layer}", grid=(nt,),
        in_specs=in_specs + [ANY, ANY], out_specs=out_specs, out_shape=outs,
        scratch_shapes=[pltpu.VMEM((D_MODEL, 2 * D_FF), BF16), pltpu.VMEM((D_FF, D_MODEL), BF16),
                        pltpu.VMEM((8, D_FF), F32), pltpu.SemaphoreType.DMA((8,))],
        args=args + [wup_g, wdown_g], stages=stages)


def _ffn_bwd(layer, dx3, x2, up, silu, dsilu, g2, wfc, wup_g, wdown_g, stages):
    t_len = x2.shape[0]
    tm = min(TM_FFN, t_len)
    nt = t_len // tm

    def core(dx3_ref, dx3_late_ref, x_ref, up_ref, silu_ref, dsilu_ref, g2_ref, wfc_ref, wup_hbm, wdown_hbm,
             dx2_ref, dup_ref, dx3b_ref, dg2_ref, dbfc_ref, dwfc_ref,
             wup_v, wdown_v, carry, da_s, dup_s, sems):
        i = pl.program_id(0)

        @pl.when(i == 0)
        def _():
            cps = _load_col_sharded(wup_hbm, wup_v, sems, 0) + _load_row_sharded(wdown_hbm, wdown_v, sems, 4)
            _start_all(cps)
            for ref in (carry, da_s, dup_s, dg2_ref, dbfc_ref, dwfc_ref):
                ref[...] = jnp.zeros_like(ref)
            _wait_all(cps)

        live = (i <= nt).astype(F32)
        dx3b_ref[...] = dx3_ref[...].astype(BF16)
        dh = jnp.zeros((tm, D_MODEL), F32)
        for c0, c1 in FF_CHUNKS:
            v0, v1 = D_FF + c0, D_FF + c1
            dh = dh + _dot_nt(dup_s[:, c0:c1], wup_v[:, c0:c1]) + _dot_nt(dup_s[:, v0:v1], wup_v[:, v0:v1])
            da = da_s[:, c0:c1]
            dval = (da * silu_ref[:, c0:c1].astype(F32)).astype(BF16)
            dup_ref[:, v0:v1] = dval
            dup_s[:, v0:v1] = dval
            dgc = da * up_ref[:, v0:v1].astype(F32) * dsilu_ref[:, c0:c1].astype(F32)
            cr = carry[:, c0:c1]
            dgc1 = _shift_up(dgc, cr, 1)
            dgc2 = _shift_up(dgc, cr, 2)
            carry[:, c0:c1] = jnp.where(i < nt, dgc[0:8, :], cr)
            gate = up_ref[:, c0:c1].astype(F32)
            dbfc_ref[:, c0:c1] += live * _colsum8(dgc)
            dwfc_ref[0, :, c0:c1] += live * _colsum8(dgc2 * gate)
            dwfc_ref[1, :, c0:c1] += live * _colsum8(dgc1 * gate)
            dwfc_ref[2, :, c0:c1] += live * _colsum8(dgc * gate)
            dgate = (wfc_ref[2:3, c0:c1] * dgc + wfc_ref[1:2, c0:c1] * dgc1 + wfc_ref[0:1, c0:c1] * dgc2).astype(BF16)
            dup_ref[:, c0:c1] = dgate
            dup_s[:, c0:c1] = dgate
            da_s[:, c0:c1] = _dot_nt(dx3b_ref[...], wdown_v[c0:c1, :])
        xv = x_ref[...]
        r = lax.rsqrt(jnp.mean(xv * xv, axis=-1, keepdims=True) + RMS_EPS)
        xh = xv * r
        dg2_ref[...] += _colsum8(dh * xh)
        dxh = dh * g2_ref[...]
        dx2_ref[...] = dx3_late_ref[...] + r * (dxh - xh * jnp.mean(dxh * xh, axis=-1, keepdims=True))

    def tile(n, lag):
        return pl.BlockSpec((tm, n), lambda i: (nt - 1 - jnp.clip(i - lag, 0, nt - 1), 0))

    outs = [
        jax.ShapeDtypeStruct((t_len, D_MODEL), F32),
        jax.ShapeDtypeStruct((t_len, 2 * D_FF), BF16),
        jax.ShapeDtypeStruct((t_len, D_MODEL), BF16),
        jax.ShapeDtypeStruct((8, D_MODEL), F32),
        jax.ShapeDtypeStruct((8, D_FF), F32),
        jax.ShapeDtypeStruct((3, 8, D_FF), F32),
    ]
    return _staged_call(
        core, name=f"ffn_bwd_l{layer}", grid=(nt + 2,),
        in_specs=[tile(D_MODEL, 0), tile(D_MODEL, 2), tile(D_MODEL, 2), tile(2 * D_FF, 1), tile(D_FF, 1), tile(D_FF, 1),
                  _const_spec((1, D_MODEL)), _const_spec((8, D_FF)), ANY, ANY],
        out_specs=[tile(D_MODEL, 2), tile(2 * D_FF, 1), tile(D_MODEL, 0),
                   _const_spec((8, D_MODEL)), _const_spec((8, D_FF)), _const_spec((3, 8, D_FF))],
        out_shape=outs,
        scratch_shapes=[pltpu.VMEM((D_MODEL, 2 * D_FF), BF16), pltpu.VMEM((D_FF, D_MODEL), BF16),
                        pltpu.VMEM((8, D_FF), F32), pltpu.VMEM((tm, D_FF), F32), pltpu.VMEM((tm, 2 * D_FF), BF16),
                        pltpu.SemaphoreType.DMA((8,))],
        args=[dx3, dx3, x2, up, silu, dsilu, g2, wfc, wup_g, wdown_g], stages=stages)


def _mixer_bwd(layer, dx2, x, zc, qs, sa, ca, sb, cb, ug, fu, xhs, cv, g1, lng, lnb, wmt, wsc, win_g, wb_g, wout_g,
               stages):
    t_len = x.shape[0]
    tm = min(TM_MIX, t_len)
    nt = t_len // tm
    nb = tm // GMLP_BLOCK

    def core(dx2_ref, x_ref, zc_ref, q_ref, sa_ref, ca_ref, sb_ref, cb_ref, ug_ref, fu_ref, xh_ref, cv_ref,
             g1_ref, lng_ref, lnb_ref, wmt_ref, wsc_ref, win_hbm, wb_hbm, wout_hbm,
             dx_ref, dz_ref, da_ref, db_ref, dx2b_ref, dg1_ref, dbgate_ref, dlng_ref, dlnb_ref, dwm_ref, dbsf_ref, dwsc_ref,
             win_v, wb_v, wout_v, carry, vn_s, df_s, dvn_s, sems):
        i = pl.program_id(0)

        @pl.when(i == 0)
        def _():
            cps = (_load_col_sharded(win_hbm, win_v, sems, 0) + _load_branch(wb_hbm, wb_v, sems, 4)
                   + _load_row_sharded(wout_hbm, wout_v, sems, 12))
            _start_all(cps)
            for ref in (carry, dg1_ref, dbgate_ref, dlng_ref, dlnb_ref, dwm_ref, dbsf_ref, dwsc_ref):
                ref[...] = jnp.zeros_like(ref)
            _wait_all(cps)

        def kept(k):
            return zc_ref[:, k * D_B:(k + 1) * D_B].astype(F32)

        def dz_cols(c0, n, val):
            dz_ref[:, c0:c0 + n] = val.astype(BF16)
            return _dot_nt(dz_ref[:, c0:c0 + n], win_v[:, c0:c0 + n])

        dx2b_ref[...] = dx2_ref[...].astype(BF16)
        dm = _dot_nt(dx2b_ref[...], wout_v[...])
        da_ref[...] = (dm * sa_ref[...].astype(F32)).astype(BF16)
        dga = dm * ca_ref[...].astype(F32)
        dh = dz_cols(C_GA, D_MODEL, dga)
        dbgate_ref[:, 0:D_MODEL] += _colsum8(dga)
        dya = _dot_nt(da_ref[...], wb_v[0])
        db_ref[...] = (dm * sb_ref[...].astype(F32)).astype(BF16)
        dgb = dm * cb_ref[...].astype(F32)
        dh = dh + dz_cols(C_GB, D_MODEL, dgb)
        dbgate_ref[:, D_MODEL:2 * D_MODEL] += _colsum8(dgb)
        dyb = _dot_nt(db_ref[...], wb_v[1])

        xh = xh_ref[...].astype(F32)
        vn_s[...] = (xh * lng_ref[...] + lnb_ref[...]).astype(BF16)
        df = dya * ug_ref[...].astype(F32)
        df_s[...] = df.astype(BF16)
        dbsf_acc = df[0:128, :]
        for b in range(1, nb):
            dbsf_acc = dbsf_acc + df[b * 128:(b + 1) * 128, :]
        dbsf_ref[...] += dbsf_acc
        for hd in range(A_HEADS):
            cols = slice(hd * 128, (hd + 1) * 128)
            vcat = jnp.concatenate([vn_s[b * 128:(b + 1) * 128, cols] for b in range(nb)], axis=1)
            dcat = jnp.concatenate([df_s[b * 128:(b + 1) * 128, cols] for b in range(nb)], axis=1)
            gcat = _dot(wmt_ref[hd], dcat)
            dwm_ref[hd] += _dot_nt(dcat, vcat)
            for b in range(nb):
                dvn_s[b * 128:(b + 1) * 128, cols] = gcat[:, b * 128:(b + 1) * 128]
        dh = dh + dz_cols(C_U, D_A, dya * fu_ref[...].astype(F32))
        dvn = dvn_s[...]
        dlng_ref[...] += _colsum8(dvn * xh)
        dlnb_ref[...] += _colsum8(dvn)
        dxh = dvn * lng_ref[...]
        dvc = dxh - jnp.mean(dxh, axis=-1, keepdims=True) - xh * jnp.mean(dxh * xh, axis=-1, keepdims=True)
        dh = dh + dz_cols(C_V, D_A, dvc * cv_ref[...].astype(F32))

        cg = kept(1)
        hbv = kept(2)
        p = cg * hbv
        dh = dh + dz_cols(C_BG, D_B, dyb * q_ref[...].astype(F32))
        dq = dyb * kept(0)
        cr = carry[...]
        dq1 = _shift_up(dq, cr, 1)
        dq2 = _shift_up(dq, cr, 2)
        carry[...] = dq[0:8, :]
        dwsc_ref[0] += _colsum8(dq2 * p)
        dwsc_ref[1] += _colsum8(dq1 * p)
        dwsc_ref[2] += _colsum8(dq * p)
        dp = wsc_ref[2:3, :] * dq + wsc_ref[1:2, :] * dq1 + wsc_ref[0:1, :] * dq2
        dh = dh + dz_cols(C_CG, D_B, dp * hbv)
        dh = dh + dz_cols(C_HB, D_B, dp * cg)

        xv = x_ref[...]
        r = lax.rsqrt(jnp.mean(xv * xv, axis=-1, keepdims=True) + RMS_EPS)
        xn = xv * r
        dg1_ref[...] += _colsum8(dh * xn)
        dxn = dh * g1_ref[...]
        dx_ref[...] = dx2_ref[...] + r * (dxn - xn * jnp.mean(dxn * xn, axis=-1, keepdims=True))

    outs = [
        jax.ShapeDtypeStruct((t_len, D_MODEL), F32),
        jax.ShapeDtypeStruct((t_len, D_IN), BF16),
        jax.ShapeDtypeStruct((t_len, D_MODEL), BF16),
        jax.ShapeDtypeStruct((t_len, D_MODEL), BF16),
        jax.ShapeDtypeStruct((t_len, D_MODEL), BF16),
        jax.ShapeDtypeStruct((8, D_MODEL), F32),
        jax.ShapeDtypeStruct((8, 2 * D_MODEL), F32),
        jax.ShapeDtypeStruct((8, D_A), F32),
        jax.ShapeDtypeStruct((8, D_A), F32),
        jax.ShapeDtypeStruct((A_HEADS, 128, 128), F32),
        jax.ShapeDtypeStruct((128, D_A), F32),
        jax.ShapeDtypeStruct((3, 8, D_B), F32),
    ]

    return _staged_call(
        core, name=f"mixer_bwd_l{layer}", grid=(nt,),
        in_specs=[_row_spec(tm, D_MODEL, nt), _row_spec(tm, D_MODEL, nt), _row_spec(tm, 3 * D_B, nt),
                  _row_spec(tm, D_B, nt), _row_spec(tm, D_MODEL, nt), _row_spec(tm, D_MODEL, nt),
                  _row_spec(tm, D_MODEL, nt), _row_spec(tm, D_MODEL, nt), _row_spec(tm, D_A, nt), _row_spec(tm, D_A, nt),
                  _row_spec(tm, D_A, nt), _row_spec(tm, D_A, nt),
                  _const_spec((1, D_MODEL)), _const_spec((1, D_A)), _const_spec((1, D_A)),
                  _const_spec((A_HEADS, 128, 128)), _const_spec((8, D_B)), ANY, ANY, ANY],
        out_specs=[_row_spec(tm, D_MODEL, nt), _row_spec(tm, D_IN, nt), _row_spec(tm, D_MODEL, nt),
                   _row_spec(tm, D_MODEL, nt), _row_spec(tm, D_MODEL, nt),
                   _const_spec((8, D_MODEL)), _const_spec((8, 2 * D_MODEL)), _const_spec((8, D_A)), _const_spec((8, D_A)),
                   _const_spec((A_HEADS, 128, 128)), _const_spec((128, D_A)), _const_spec((3, 8, D_B))],
        out_shape=outs,
        scratch_shapes=[pltpu.VMEM((D_MODEL, D_IN), BF16), pltpu.VMEM((2, D_A, D_MODEL), BF16),
                        pltpu.VMEM((D_MODEL, D_MODEL), BF16), pltpu.VMEM((8, D_B), F32),
                        pltpu.VMEM((tm, D_A), BF16), pltpu.VMEM((tm, D_A), BF16), pltpu.VMEM((tm, D_A), F32),
                        pltpu.SemaphoreType.DMA((16,))],
        args=[dx2, x, zc, qs, sa, ca, sb, cb, ug, fu, xhs, cv, g1, lng, lnb, wmt, wsc, win_g, wb_g, wout_g],
        stages=stages)


def _wgrad(name, layer, a, b, rows, cols, row_blk, col_blk, stages, a_first=0):
    t_len = a.shape[0]
    n = b.shape[1]
    tk = min(TK_WGRAD, t_len)
    col_sharded = n == N_CHIPS * cols
    m = rows if col_sharded else a.shape[1]
    grid = (m // row_blk, n // col_blk, t_len // tk)
    per_shard_c = cols // col_blk

    if col_sharded:
        out_shape = (N_CHIPS, rows, cols)
        out_spec = pl.BlockSpec((None, row_blk, col_blk), lambda i, j, k: (j // per_shard_c, i, j % per_shard_c))
    else:
        out_shape = (N_CHIPS * rows, cols)
        out_spec = pl.BlockSpec((row_blk, col_blk), lambda i, j, k: (i, j))

    def core(a_ref, b_ref, o_ref):
        @pl.when(pl.program_id(2) == 0)
        def _():
            o_ref[...] = jnp.zeros_like(o_ref)

        o_ref[...] += _dot_tn(a_ref[...], b_ref[...])

    own, outs = _staged_call(
        core, name=f"wgrad_{name}_l{layer}", grid=grid,
        in_specs=[pl.BlockSpec((tk, row_blk), lambda i, j, k: (k, a_first + i)),
                  pl.BlockSpec((tk, col_blk), lambda i, j, k: (k, j))],
        out_specs=[out_spec], out_shape=[jax.ShapeDtypeStruct(out_shape, F32)], scratch_shapes=[],
        args=[a, b], stages=stages)
    return [own[0].reshape(N_CHIPS, rows, cols)], outs


def _wgrad_branch(layer, ya, da, yb, db, stages):
    t_len = ya.shape[0]
    tk = min(TK_WGRAD, t_len)

    cs = D_MODEL // N_CHIPS

    def core(ya_ref, da_ref, yb_ref, db_ref, o_ref):
        @pl.when(pl.program_id(0) == 0)
        def _():
            o_ref[...] = jnp.zeros_like(o_ref)

        ga = _dot_tn(ya_ref[...], da_ref[...])
        gb = _dot_tn(yb_ref[...], db_ref[...])
        for k in range(N_CHIPS):
            o_ref[k, 0:D_A, :] += ga[:, k * cs:(k + 1) * cs]
            o_ref[k, D_A:2 * D_A, :] += gb[:, k * cs:(k + 1) * cs]

    a_spec = pl.BlockSpec((tk, D_A), lambda k: (k, 0))
    d_spec = pl.BlockSpec((tk, D_MODEL), lambda k: (k, 0))
    return _staged_call(
        core, name=f"wgrad_w_branch_l{layer}", grid=(t_len // tk,),
        in_specs=[a_spec, d_spec, a_spec, d_spec],
        out_specs=[pl.BlockSpec((N_CHIPS, 2 * D_A, cs), lambda k: (0, 0, 0))],
        out_shape=[jax.ShapeDtypeStruct((N_CHIPS, 2 * D_A, cs), F32)], scratch_shapes=[],
        args=[ya, da, yb, db], stages=stages)


def _flat_blk(rows, cols):
    blk = rows
    while blk * cols * 4 > 2 * 1024 * 1024 and blk % 16 == 0:
        blk //= 2
    return blk


def _cast_into_slots(name, layer, ws, chip):
    blks = [_flat_blk(w.shape[1], w.shape[2]) for w in ws]
    nblks = [w.shape[1] // b for w, b in zip(ws, blks)]
    n = len(ws)

    def body(chip_ref, *refs):
        for w_ref, o_ref in zip(refs[:n], refs[n:]):
            o_ref[...] = w_ref[...].astype(BF16)

    def in_spec(w, blk, nblk):
        return pl.BlockSpec((None, blk, w.shape[2]), lambda i, chip_ref: (layer, jnp.minimum(i, nblk - 1), 0))

    def out_spec(w, blk, nblk):
        return pl.BlockSpec((None, blk, w.shape[2]), lambda i, chip_ref: (chip_ref[0], jnp.minimum(i, nblk - 1), 0))

    return pl.pallas_call(
        body, name=f"cast_{name}_l{layer}",
        grid_spec=pltpu.PrefetchScalarGridSpec(
            num_scalar_prefetch=1, grid=(max(nblks),),
            in_specs=[in_spec(w, b, k) for w, b, k in zip(ws, blks, nblks)],
            out_specs=[out_spec(w, b, k) for w, b, k in zip(ws, blks, nblks)]),
        out_shape=[jax.ShapeDtypeStruct((N_CHIPS,) + w.shape[1:], BF16) for w in ws],
        compiler_params=_params(),
    )(chip, *ws)


def _reduction_sums(name, jobs, pos):
    in_specs, out_specs, out_shape, args, bodies, counts = [], [], [], [], [], []
    for job in jobs:
        kind, grad, other = job[0], job[1], job[2]
        _, h, cols = other.shape
        blk = _flat_blk(h, cols)
        nblk = h // blk
        if kind == "pair":
            total = N_CHIPS * nblk

            def block(s, total=total, nblk=nblk):
                b = jnp.minimum(s, total - 1)
                return b // nblk, b % nblk

            spec = pl.BlockSpec((None, blk, cols), lambda s, p, block=block: (block(s)[0], block(s)[1], 0))
            in_specs += [pl.BlockSpec((None, blk, cols), lambda s, p, block=block, nblk=nblk:
                                      (block(s)[0], p[1] * nblk + block(s)[1], 0)), spec]
            out_specs.append(spec)
            out_shape.append(jax.ShapeDtypeStruct((N_CHIPS, h, cols), BF16))
            args += [grad, other]
            bodies.append((2, lambda g, o, out: out.__setitem__(..., (g[...] + o[...]).astype(BF16))))
        else:
            total = nblk

            def block(s, total=total):
                return jnp.minimum(s, total - 1)

            in_specs += [pl.BlockSpec((None, blk, cols), lambda s, p, block=block, nblk=nblk:
                                      (p[0], p[1] * nblk + block(s), 0)),
                         pl.BlockSpec((None, blk, cols), lambda s, p, block=block: (p[0], block(s), 0)),
                         pl.BlockSpec((3, blk, cols), lambda s, p, block=block: (0, block(s), 0))]
            out_specs.append(pl.BlockSpec((blk, cols), lambda s, p, block=block, nblk=nblk: (p[1] * nblk + block(s), 0)))
            out_shape.append(jax.ShapeDtypeStruct((2 * h, cols), F32))
            args += [grad, other, job[3]]
            bodies.append((3, lambda g, o, r, out: out.__setitem__(
                ..., (((g[...] + o[...]) + r[0].astype(F32)) + r[1].astype(F32)) + r[2].astype(F32))))
        counts.append(total)

    def body(pos_ref, *refs):
        ins, outs = refs[:len(args)], refs[len(args):]
        k = 0
        for (n_in, fn), out in zip(bodies, outs):
            fn(*ins[k:k + n_in], out)
            k += n_in

    return pl.pallas_call(
        body, name=f"reduction_sums_{name}",
        grid_spec=pltpu.PrefetchScalarGridSpec(num_scalar_prefetch=1, grid=(max(counts),), in_specs=in_specs,
                                               out_specs=out_specs),
        out_shape=out_shape,
        compiler_params=_params(),
    )(pos, *args)


def _sum_slots(name, slots):
    n, rows, _ = slots.shape

    def body(s_ref, o_ref):
        acc = s_ref[0]
        for d in range(1, n):
            acc = acc + s_ref[d]
        o_ref[...] = acc

    return pl.pallas_call(
        body, name=f"sum_slots_{name}", grid=(1,),
        in_specs=[pl.BlockSpec((n, rows, 128), lambda i: (0, 0, 0))],
        out_specs=pl.BlockSpec((rows, 128), lambda i: (0, 0)),
        out_shape=jax.ShapeDtypeStruct((rows, 128), F32),
        compiler_params=_params(),
    )(slots)


def _adamw_math(w, g, m, v):
    m2 = ADAM_B1 * m + (1.0 - ADAM_B1) * g
    v2 = ADAM_B2 * v + (1.0 - ADAM_B2) * (g * g)
    m_hat = m2 / (1.0 - ADAM_B1 ** ADAM_STEP)
    v_hat = v2 / (1.0 - ADAM_B2 ** ADAM_STEP)
    delta = -ADAM_LR * (m_hat / (jnp.sqrt(v_hat) + ADAM_EPS) + ADAM_WD * w)
    return delta, m2, v2


def _adamw_big(name, w, g0, g1, m, v):
    _, rows, cols = w.shape
    blk = _flat_blk(rows, cols) // 2

    def body(w_ref, g0_ref, g1_ref, m_ref, v_ref, g_ref, d_ref, m2_ref, v2_ref):
        g = jnp.where(pl.program_id(0) == 0, g0_ref[...], g1_ref[...])
        d, m2, v2 = _adamw_math(w_ref[...], g, m_ref[...], v_ref[...])
        g_ref[...] = g
        d_ref[...] = d
        m2_ref[...] = m2
        v2_ref[...] = v2

    spec = pl.BlockSpec((None, blk, cols), lambda la, i: (la, i, 0))
    return pl.pallas_call(
        body, name=f"adamw_{name}", grid=(N_LAYERS, rows // blk),
        in_specs=[spec, pl.BlockSpec((blk, cols), lambda la, i: (i * (1 - la), 0)),
                  pl.BlockSpec((blk, cols), lambda la, i: (i * la, 0)), spec, spec],
        out_specs=[spec] * 4,
        out_shape=[jax.ShapeDtypeStruct(w.shape, F32)] * 4,
        compiler_params=_params(("parallel", "parallel")),
    )(w, g0, g1, m, v)


def _adamw_small(ws, gs, ms, vs):
    n = len(ws)

    def body(*refs):
        ins, outs = refs[:4 * n], refs[4 * n:]
        for k in range(n):
            d, m2, v2 = _adamw_math(ins[k][...], ins[n + k][...], ins[2 * n + k][...], ins[3 * n + k][...])
            outs[k][...] = d
            outs[n + k][...] = m2
            outs[2 * n + k][...] = v2

    vmem = pl.BlockSpec(memory_space=pltpu.VMEM)
    return pl.pallas_call(
        body, name="adamw_small",
        in_specs=[vmem] * (4 * n), out_specs=[vmem] * (3 * n),
        out_shape=[jax.ShapeDtypeStruct(w.shape, F32) for w in ws] * 3,
        compiler_params=pltpu.CompilerParams(vmem_limit_bytes=V7X_VMEM_LIMIT),
    )(*ws, *gs, *ms, *vs)


SMALL = ("norm1_g", "b_gate", "gmlp_ln_g", "gmlp_ln_b", "w_spatial", "b_spatial", "w_shortconv", "norm2_g",
         "w_ffn_conv", "b_ffn_conv", "final_g")
ALL_WEIGHTS = ("norm1_g", "w_in", "b_gate", "gmlp_ln_g", "gmlp_ln_b", "w_spatial", "b_spatial", "w_shortconv",
               "w_branch", "w_out", "norm2_g", "w_ffn_up", "w_ffn_conv", "b_ffn_conv", "w_ffn_down", "final_g")


def _pack(arrays):
    flat = jnp.concatenate([a.reshape(-1) for a in arrays])
    n = flat.shape[0]
    rows = -(-n // 1024) * 8
    return jnp.pad(flat, (0, rows * 128 - n)).reshape(rows, 128)


def _unpack(packed, like):
    flat = packed.reshape(-1)
    out, off = [], 0
    for a in like:
        out.append(flat[off:off + a.size].reshape(a.shape))
        off += a.size
    return out


def _pad8(w):
    return jnp.pad(w, ((0, 5), (0, 0)))


def kernel(x, norm1_g, w_in, b_gate, gmlp_ln_g, gmlp_ln_b, w_spatial, b_spatial, w_shortconv, w_branch, w_out, norm2_g, w_ffn_up, w_ffn_conv, b_ffn_conv, w_ffn_down, final_g, loss_target, m_norm1_g, m_w_in, m_b_gate, m_gmlp_ln_g, m_gmlp_ln_b, m_w_spatial, m_b_spatial, m_w_shortconv, m_w_branch, m_w_out, m_norm2_g, m_w_ffn_up, m_w_ffn_conv, m_b_ffn_conv, m_w_ffn_down, m_final_g, v_norm1_g, v_w_in, v_b_gate, v_gmlp_ln_g, v_gmlp_ln_b, v_w_spatial, v_b_spatial, v_w_shortconv, v_w_branch, v_w_out, v_norm2_g, v_w_ffn_up, v_w_ffn_conv, v_b_ffn_conv, v_w_ffn_down, v_final_g):
    weights = dict(norm1_g=norm1_g, w_in=w_in, b_gate=b_gate, gmlp_ln_g=gmlp_ln_g, gmlp_ln_b=gmlp_ln_b,
                   w_spatial=w_spatial, b_spatial=b_spatial, w_shortconv=w_shortconv, w_branch=w_branch, w_out=w_out,
                   norm2_g=norm2_g, w_ffn_up=w_ffn_up, w_ffn_conv=w_ffn_conv, b_ffn_conv=b_ffn_conv,
                   w_ffn_down=w_ffn_down, final_g=final_g)
    mom = dict(norm1_g=m_norm1_g, w_in=m_w_in, b_gate=m_b_gate, gmlp_ln_g=m_gmlp_ln_g, gmlp_ln_b=m_gmlp_ln_b,
               w_spatial=m_w_spatial, b_spatial=m_b_spatial, w_shortconv=m_w_shortconv, w_branch=m_w_branch,
               w_out=m_w_out, norm2_g=m_norm2_g, w_ffn_up=m_w_ffn_up, w_ffn_conv=m_w_ffn_conv,
               b_ffn_conv=m_b_ffn_conv, w_ffn_down=m_w_ffn_down, final_g=m_final_g)
    vel = dict(norm1_g=v_norm1_g, w_in=v_w_in, b_gate=v_b_gate, gmlp_ln_g=v_gmlp_ln_g, gmlp_ln_b=v_gmlp_ln_b,
               w_spatial=v_w_spatial, b_spatial=v_b_spatial, w_shortconv=v_w_shortconv, w_branch=v_w_branch,
               w_out=v_w_out, norm2_g=v_norm2_g, w_ffn_up=v_w_ffn_up, w_ffn_conv=v_w_ffn_conv,
               b_ffn_conv=v_b_ffn_conv, w_ffn_down=v_w_ffn_down, final_g=v_final_g)

    cx, cy, cc = _mesh_pos()
    chip = 2 * cx + cy
    chip_arr = chip.astype(jnp.int32).reshape(1)
    pos_arr = jnp.stack([chip, cc]).astype(jnp.int32)
    t_len = x.shape[1]
    xs = x.reshape(t_len, D_MODEL)
    target = loss_target.reshape(t_len, D_MODEL)
    pipe = _Pipe()

    full = {}

    def gather(group, names, la):
        slots = _cast_into_slots(group, la, [weights[n].reshape((N_LAYERS,) + BIG[n]) for n in names], chip_arr)

        def then(*bufs):
            full.update(zip([(n, la) for n in names], bufs))

        pipe.add(_gather_stage(slots, then))

    mixer_w = ("w_in", "w_branch", "w_out")
    ffn_w = ("w_ffn_up", "w_ffn_down")
    gather("mixer", mixer_w, 0)
    tap_slots = {}
    pipe.add(_chip_spread_stage(_pack([w_shortconv, w_ffn_conv]), lambda slots: tap_slots.__setitem__("all", slots)))
    pipe.flush()
    by_chip = [_unpack(tap_slots["all"][k], [w_shortconv, w_ffn_conv]) for k in range(N_CHIPS)]
    wsc_full = jnp.concatenate([t[0] for t in by_chip], axis=-1)
    wfc_full = jnp.concatenate([t[1] for t in by_chip], axis=-1)

    idx = jnp.arange(GMLP_BLOCK) // CHUNK
    mask = idx[None, :] <= idx[:, None]
    wm_all = jnp.where(mask[None, None], w_spatial, 0.0)
    wm_bf = wm_all.astype(BF16)
    wmt_bf = jnp.swapaxes(wm_all, -1, -2).astype(BF16)
    bsf = jnp.repeat(jnp.swapaxes(b_spatial, -1, -2), 128, axis=-1)

    def row(a):
        return a.reshape(1, -1)

    def mixer_args(la):
        return (row(norm1_g[la]), row(b_gate[la]), row(gmlp_ln_g[la]), row(gmlp_ln_b[la]))

    def mixer_weights(la):
        return tuple(full[(n, la)] for n in mixer_w)

    def ffn_weights(la):
        return tuple(full[(n, la)] for n in ffn_w)

    saved = []
    h_in = xs
    for la in range(N_LAYERS):
        gather("ffn", ffn_w, la)
        *kept, mg, h1, x2 = pipe.carry(lambda st: _mixer_fwd(
            la, h_in, *mixer_args(la), wm_bf[la], bsf[la], _pad8(wsc_full[la]), *mixer_weights(la), st))
        ya, yb = kept[1], kept[2]
        if la + 1 < N_LAYERS:
            gather("mixer", mixer_w, la + 1)
        head = (target, row(final_g)) if la == N_LAYERS - 1 else None
        up, silu, dsilu, act, h2, *rest = pipe.carry(lambda st: _ffn_fwd(
            la, x2, row(norm2_g[la]), _pad8(wfc_full[la]), row(b_ffn_conv[la]), *ffn_weights(la), st, head=head))
        saved.append(dict(x=h_in, ya=ya, yb=yb, mixer=[kept[0]] + kept[3:], mg=mg, h1=h1, x2=x2, up=up, silu=silu,
                          dsilu=dsilu, act=act, h2=h2))
        h_in = rest[0]
    dx, dgf8, loss8 = rest

    reduced_big = {}

    sums_due = []

    def run_sums():
        if sums_due:
            due = list(sums_due)
            sums_due.clear()
            run_sums.calls += 1
            for (_, then), res in zip(due, _reduction_sums(str(run_sums.calls), [job for job, _ in due], pos_arr)):
                then(res)

    run_sums.calls = 0
    pipe.after = run_sums

    def reduce_big(name, la, grad):
        def after_pair(other):
            def after_chips(got):
                sums_due.append((("chip", grad, other, got), lambda final: pipe.add(_pair_fill_stage(
                    final, lambda done: reduced_big.__setitem__((name, la), done)))))

            sums_due.append((("pair", grad, other), lambda psum: pipe.add(_chip_send_stage(psum, after_chips))))

        pipe.add(_pair_send_stage(grad, after_pair))

    small = {n: [None] * N_LAYERS for n in SMALL}
    spread = {}
    for la in reversed(range(N_LAYERS)):
        s = saved[la]
        dx3 = dx
        dx2, dup, dx3b, dg2, dbfc, dwfc = pipe.carry(lambda st: _ffn_bwd(
            la, dx3, s["x2"], s["up"], s["silu"], s["dsilu"], row(norm2_g[la]), _pad8(wfc_full[la]),
            *ffn_weights(la), st))
        g, = pipe.carry(lambda st: _wgrad("w_ffn_up", la, s["h2"], dup, 1024, 1408, 1024, 1408, st))
        reduce_big("w_ffn_up", la, g)
        g, = pipe.carry(lambda st: _wgrad("w_ffn_down", la, s["act"], dx3b, 704, 1024, 1408, 1024, st))
        reduce_big("w_ffn_down", la, g)
        run = pipe.carry if la > 0 else (lambda call: call([])[0])
        dxl, dz, da, db, dx2b, dg1, dbg, dlng, dlnb, dwm, dbsf, dwsc = run(lambda st: _mixer_bwd(
            la, dx2, s["x"], *s["mixer"], row(norm1_g[la]), row(gmlp_ln_g[la]), row(gmlp_ln_b[la]), wmt_bf[la],
            _pad8(wsc_full[la]), *mixer_weights(la), st))
        small["norm1_g"][la] = dg1.sum(0)
        small["b_gate"][la] = dbg.sum(0)
        small["gmlp_ln_g"][la] = dlng.sum(0)
        small["gmlp_ln_b"][la] = dlnb.sum(0)
        small["w_spatial"][la] = jnp.where(mask[None], dwm, 0.0)
        small["b_spatial"][la] = dbsf.reshape(128, A_HEADS, 128).sum(-1).T
        small["w_shortconv"][la] = dwsc.sum(1)
        small["norm2_g"][la] = dg2.sum(0)
        small["w_ffn_conv"][la] = dwfc.sum(1)
        small["b_ffn_conv"][la] = dbfc.sum(0)
        if la == 0:
            small_local = ([jnp.stack(small[n]) for n in SMALL[:-1]]
                           + [dgf8.sum(0), 0.5 * loss8.sum().reshape(1) / D_MODEL])
            mine = _pack(small_local)

            def after_swap(other, mine=mine):
                pair = _sum_slots("small_pair", jnp.stack([mine, other]))
                pipe.add(_chip_spread_stage(pair, lambda slots: spread.__setitem__("slots", slots)))

            pipe.add(_pair_swap_stage(mine, after_swap))
        if la > 0:
            g, = pipe.carry(lambda st: _wgrad("w_in", la, s["h1"], dz, 1024, 1152, 1024, 1152, st))
            reduce_big("w_in", la, g)
        else:
            for part, tag in enumerate(("w_in_a", "w_in_b")):
                g, = pipe.carry(lambda st: _wgrad(tag, la, s["h1"], dz, 512, 1152, 512, 1152, st, a_first=part))
                reduce_big(tag, la, g)
        g, = pipe.carry(lambda st: _wgrad("w_out", la, s["mg"], dx2b, 256, 1024, 1024, 1024, st), long=False)
        reduce_big("w_out", la, g)
        g, = pipe.carry(lambda st: _wgrad_branch(la, s["ya"], da, s["yb"], db, st), long=False)
        reduce_big("w_branch", la, g)
        dx = dxl
    grad_x = dx.reshape(x.shape)
    pipe.flush()

    reduced_big[("w_in", 0)] = jnp.concatenate([reduced_big[("w_in_a", 0)], reduced_big[("w_in_b", 0)]], axis=0)
    reduced = _unpack(_sum_slots("small_grads", spread["slots"]), small_local)
    loss = reduced[-1].reshape(())
    grads = dict(zip(SMALL, reduced[:-1]))
    grads["w_shortconv"] = lax.dynamic_slice(grads["w_shortconv"], (0, 0, chip * (D_B // 4)), (N_LAYERS, 3, D_B // 4))
    grads["w_ffn_conv"] = lax.dynamic_slice(grads["w_ffn_conv"], (0, 0, chip * (D_FF // 4)), (N_LAYERS, 3, D_FF // 4))

    delta, new_m, new_v = {}, {}, {}
    for n in BIG_NAMES:
        shape3 = (N_LAYERS,) + BIG[n]
        res = _adamw_big(n, weights[n].reshape(shape3), reduced_big[(n, 0)], reduced_big[(n, 1)],
                         mom[n].reshape(shape3), vel[n].reshape(shape3))
        grads[n], delta[n], new_m[n], new_v[n] = (a.reshape(weights[n].shape) for a in res)
    res = _adamw_small(*[[src[n].reshape(-1, src[n].shape[-1]) for n in SMALL] for src in (weights, grads, mom, vel)])
    for k, n in enumerate(SMALL):
        delta[n], new_m[n], new_v[n] = (res[j * len(SMALL) + k].reshape(weights[n].shape) for j in range(3))

    return (loss, grad_x, *[grads[n] for n in ALL_WEIGHTS], *[delta[n] for n in ALL_WEIGHTS],
            *[new_m[n] for n in ALL_WEIGHTS], *[new_v[n] for n in ALL_WEIGHTS])
```

```python
import jax
import jax.numpy as jnp
from jax import lax
from jax.experimental import pallas as pl
from jax.experimental.pallas import tpu as pltpu

F32 = jnp.float32
BF16 = jnp.bfloat16
MESH = pl.DeviceIdType.MESH
ANY = pl.BlockSpec(memory_space=pl.ANY)

D_MODEL = 1024
D_A = 512
D_B = 512
D_IN = 4608
D_FF = 2816
GMLP_BLOCK = 128
CHUNK = 64
A_HEADS = 4
N_LAYERS = 2
N_CHIPS = 4
RMS_EPS = 1e-6
LN_EPS = 1e-5
ADAM_LR = 0.001
ADAM_B1 = 0.9
ADAM_B2 = 0.999
ADAM_EPS = 1e-08
ADAM_WD = 0.01
ADAM_STEP = 10

C_U, C_V, C_BG, C_CG, C_HB, C_GA, C_GB = 0, 512, 1024, 1536, 2048, 2560, 3584

V7X_VMEM_LIMIT = 60 * 1024 * 1024
TM_MIX = 256
TM_FFN = 256
TK_WGRAD = 2048
SLOW_COPY_BYTES = 640 * 1024
FF_CHUNKS = ((0, 768), (768, 1536), (1536, 2304), (2304, 2816))
GELU_C0 = 0.7978845608028654
GELU_C1 = 0.044715

BIG = {
    "w_in": (1024, 1152),
    "w_branch": (1024, 256),
    "w_out": (256, 1024),
    "w_ffn_up": (1024, 1408),
    "w_ffn_down": (704, 1024),
}
BIG_NAMES = tuple(BIG)


def _params(sem=("arbitrary",), vmem=V7X_VMEM_LIMIT):
    return pltpu.CompilerParams(dimension_semantics=sem, vmem_limit_bytes=vmem)


def _gelu(x):
    x2 = x * x
    t = jnp.tanh(GELU_C0 * x * (1.0 + GELU_C1 * x2))
    return 0.5 * x * (1.0 + t), t


def _gelu_grad(x, t):
    return 0.5 * (1.0 + t) + 0.5 * x * (1.0 - t * t) * GELU_C0 * (1.0 + 3.0 * GELU_C1 * x * x)


def _colsum8(v):
    r, n = v.shape
    return v.reshape(r // 8, 8, n).sum(axis=0)


def _dot(a, b):
    return jnp.dot(a, b, preferred_element_type=F32)


def _dot_nt(a, b):
    return lax.dot_general(a, b, (((1,), (1,)), ((), ())), preferred_element_type=F32)


def _dot_tn(a, b):
    return lax.dot_general(a, b, (((0,), (0,)), ((), ())), preferred_element_type=F32)


def _shift_down(v, carry, n):
    rows = lax.broadcasted_iota(jnp.int32, (8, v.shape[1]), 0)
    out = pltpu.roll(v, n, 0)
    head = out[0:8, :]
    for r in range(n):
        head = jnp.where(rows == r, carry[8 - n + r:8 - n + r + 1, :], head)
    return jnp.concatenate([head, out[8:, :]], axis=0)


def _shift_up(v, carry, n):
    tm = v.shape[0]
    rows = lax.broadcasted_iota(jnp.int32, (8, v.shape[1]), 0)
    out = pltpu.roll(v, tm - n, 0)
    tail = out[tm - 8:tm, :]
    for r in range(n):
        tail = jnp.where(rows == 8 - n + r, carry[r:r + 1, :], tail)
    return jnp.concatenate([out[0:tm - 8, :], tail], axis=0)


def _sigmoid(x):
    return 0.5 * jnp.tanh(0.5 * x) + 0.5


def _start_all(copies):
    for cp in copies:
        cp.start()


def _wait_all(copies):
    for cp in copies:
        cp.wait()


def _load_col_sharded(src, dst, sems, first):
    cs = src.shape[-1]
    return [pltpu.make_async_copy(src.at[k], dst.at[:, k * cs:(k + 1) * cs], sems.at[first + k])
            for k in range(N_CHIPS)]


def _load_row_sharded(src, dst, sems, first):
    rs = src.shape[-2]
    return [pltpu.make_async_copy(src.at[k], dst.at[k * rs:(k + 1) * rs, :], sems.at[first + k])
            for k in range(N_CHIPS)]


def _load_branch(src, dst, sems, first):
    return [pltpu.make_async_copy(src.at[k, pl.ds(m * D_A, D_A), :], dst.at[m, :, k * 256:(k + 1) * 256],
                                  sems.at[first + 2 * k + m])
            for k in range(N_CHIPS) for m in range(2)]


def _row_spec(tm, n, rev=None):
    if rev is None:
        return pl.BlockSpec((tm, n), lambda i: (i, 0))
    return pl.BlockSpec((tm, n), lambda i: (rev - 1 - i, 0))


def _const_spec(shape):
    nd = len(shape)
    return pl.BlockSpec(shape, lambda i: (0,) * nd)


def _mesh_pos():
    return lax.axis_index("x"), lax.axis_index("y"), lax.axis_index("c")


def _other_chips(x, y):
    return [(1 - x, y, 2 * (1 - x) + y), (x, 1 - y, 2 * x + (1 - y)), (1 - x, 1 - y, 2 * (1 - x) + (1 - y))]


def _remote(src, dst, ssem, rsem, to):
    return pltpu.make_async_remote_copy(src_ref=src, dst_ref=dst, send_sem=ssem, recv_sem=rsem, device_id=to,
                                        device_id_type=MESH)


def _half(ref, which, h):
    start = pl.multiple_of(which * h, 8)
    if len(ref.shape) == 2:
        return ref.at[pl.ds(start, h), :]
    return ref.at[:, pl.ds(start, h), :]


class _Stage:
    def __init__(self, ins=(), inouts=(), outs=(), n_sems=0, start=None, mid=None, finish=None, then=None, slow=False):
        self.ins, self.inouts, self.outs = list(ins), list(inouts), list(outs)
        self.n_sems, self.start, self.mid, self.finish, self.then = n_sems, start, mid, finish, then
        self.slow = slow


def _gather_stage(bufs, then):
    n = len(bufs)

    def copies(io, sem):
        x, y, c = _mesh_pos()
        me = 2 * x + y
        ici, fwd, got = [], [], []
        for w in range(n):
            h = io[w].shape[1] // 2
            for j, (px, py, pk) in enumerate(_other_chips(x, y)):
                mine = _half(io[w].at[me], c, h)
                theirs = _half(io[w].at[pk], c, h)
                ici.append(_remote(mine, mine, sem(12 * w + j), sem(12 * w + 3 + j), (px, py, c)))
                got.append(_remote(theirs, theirs, sem(12 * w + j), sem(12 * w + 3 + j), (px, py, c)))
                fwd.append(_remote(theirs, theirs, sem(12 * w + 6 + j), sem(12 * w + 9 + j), (x, y, 1 - c)))
        return ici, got, fwd

    def start(ins, io, outs, sem):
        _start_all(copies(io, sem)[0])

    def mid(ins, io, outs, sem):
        _, got, fwd = copies(io, sem)
        for g, f in zip(got, fwd):
            g.wait_recv()
            f.start()

    def finish(ins, io, outs, sem):
        x, y, c = _mesh_pos()
        ici, _, fwd = copies(io, sem)
        for w in range(n):
            h = io[w].shape[1] // 2
            for j, (px, py, pk) in enumerate(_other_chips(x, y)):
                other = _half(io[w].at[pk], 1 - c, h)
                _remote(other, other, sem(12 * w + 6 + j), sem(12 * w + 9 + j), (x, y, 1 - c)).wait_recv()
        for cp in ici + fwd:
            cp.wait_send()

    return _Stage(inouts=bufs, n_sems=12 * n, start=start, mid=mid, finish=finish, then=then)


def _pair_send_stage(grad, then):
    h = grad.shape[1] // 2

    def copy(ins, outs, sem):
        x, y, c = _mesh_pos()
        return _remote(_half(ins[0], 1 - c, h), outs[0], sem(0), sem(1), (x, y, 1 - c))

    return _Stage(ins=[grad], outs=[jax.ShapeDtypeStruct((N_CHIPS, h, grad.shape[2]), F32)], n_sems=2,
                  start=lambda ins, io, outs, sem: copy(ins, outs, sem).start(),
                  finish=lambda ins, io, outs, sem: copy(ins, outs, sem).wait(), then=then)


def _chip_send_stage(psum, then):
    def copies(ins, outs, sem):
        x, y, c = _mesh_pos()
        return [_remote(ins[0].at[pk], outs[0].at[j], sem(j), sem(3 + j), (px, py, c))
                for j, (px, py, pk) in enumerate(_other_chips(x, y))]

    return _Stage(ins=[psum], outs=[jax.ShapeDtypeStruct((3,) + psum.shape[1:], BF16)], n_sems=6,
                  start=lambda ins, io, outs, sem: _start_all(copies(ins, outs, sem)),
                  finish=lambda ins, io, outs, sem: _wait_all(copies(ins, outs, sem)), then=then,
                  slow=psum.shape[1] * psum.shape[2] * 2 > SLOW_COPY_BYTES)


def _pair_fill_stage(final, then):
    h = final.shape[0] // 2

    def copy(io, sem):
        x, y, c = _mesh_pos()
        mine = _half(io[0], c, h)
        return _remote(mine, mine, sem(0), sem(1), (x, y, 1 - c))

    return _Stage(inouts=[final], n_sems=2,
                  start=lambda ins, io, outs, sem: copy(io, sem).start(),
                  finish=lambda ins, io, outs, sem: copy(io, sem).wait(), then=then)


def _pair_swap_stage(packed, then):
    def copy(ins, outs, sem):
        x, y, c = _mesh_pos()
        return _remote(ins[0], outs[0], sem(0), sem(1), (x, y, 1 - c))

    return _Stage(ins=[packed], outs=[jax.ShapeDtypeStruct(packed.shape, F32)], n_sems=2,
                  start=lambda ins, io, outs, sem: copy(ins, outs, sem).start(),
                  finish=lambda ins, io, outs, sem: copy(ins, outs, sem).wait(), then=then)


def _chip_spread_stage(psum, then):
    def copies(ins, outs, sem):
        x, y, c = _mesh_pos()
        me = 2 * x + y
        cps = [_remote(ins[0], outs[0].at[me], sem(j), sem(3 + j), (px, py, c))
               for j, (px, py, pk) in enumerate(_other_chips(x, y))]
        return cps, pltpu.make_async_copy(ins[0], outs[0].at[me], sem(6))

    def start(ins, io, outs, sem):
        cps, own = copies(ins, outs, sem)
        own.start()
        _start_all(cps)

    def finish(ins, io, outs, sem):
        cps, own = copies(ins, outs, sem)
        _wait_all(cps)
        own.wait()

    return _Stage(ins=[psum], outs=[jax.ShapeDtypeStruct((N_CHIPS,) + psum.shape, F32)], n_sems=7,
                  start=start, finish=finish, then=then)


def _staged_call(core, *, name, grid, in_specs, out_specs, out_shape, scratch_shapes, args, stages):
    n_in, n_out, n_scr = len(args), len(out_shape), len(scratch_shapes)
    s_args, s_outs, aliases, layout = [], [], {}, []
    n_sems = 0
    for st in stages:
        i0, o0 = len(s_args), len(s_outs)
        s_args += st.ins + st.inouts
        for q in range(len(st.inouts)):
            aliases[n_in + i0 + len(st.ins) + q] = n_out + o0 + q
        s_outs += [jax.ShapeDtypeStruct(a.shape, a.dtype) for a in st.inouts] + st.outs
        layout.append((i0, o0, n_sems))
        n_sems += st.n_sems
    steps = 1
    for g in grid:
        steps *= g

    def body(*refs):
        own_in = refs[:n_in]
        s_in = refs[n_in:n_in + len(s_args)]
        rest = refs[n_in + len(s_args):]
        own_out = rest[:n_out]
        s_out = rest[n_out:n_out + len(s_outs)]
        scr = rest[n_out + len(s_outs):]

        def run(which):
            for st, (i0, o0, s0) in zip(stages, layout):
                fn = getattr(st, which)
                if fn is not None:
                    fn(s_in[i0:i0 + len(st.ins)], s_out[o0:o0 + len(st.inouts)],
                       s_out[o0 + len(st.inouts):o0 + len(st.inouts) + len(st.outs)],
                       lambda k, s0=s0: scr[n_scr].at[s0 + k])

        if not stages:
            core(*own_in, *own_out, *scr[:n_scr])
            return
        step = 0
        for d, g in enumerate(grid):
            step = step * g + pl.program_id(d)
        if steps == 1:
            run("start")
            core(*own_in, *own_out, *scr[:n_scr])
            run("mid")
            run("finish")
            return
        pl.when(step == 0)(lambda: run("start"))
        core(*own_in, *own_out, *scr[:n_scr])
        pl.when(step == (3 * steps) // 4)(lambda: run("mid"))
        pl.when(step == steps - 1)(lambda: run("finish"))

    sem = ("arbitrary",) * len(grid) if stages else ("parallel",) * max(len(grid) - 1, 0) + ("arbitrary",) * min(len(grid), 1)
    res = pl.pallas_call(
        body, name=name, grid=grid,
        in_specs=list(in_specs) + [ANY] * len(s_args),
        out_specs=list(out_specs) + [ANY] * len(s_outs),
        out_shape=list(out_shape) + s_outs,
        input_output_aliases=aliases,
        scratch_shapes=list(scratch_shapes) + ([pltpu.SemaphoreType.DMA((n_sems,))] if stages else []),
        compiler_params=_params(sem) if grid else pltpu.CompilerParams(vmem_limit_bytes=V7X_VMEM_LIMIT),
    )(*args, *s_args)
    return list(res[:n_out]), list(res[n_out:])


class _Pipe:
    def __init__(self):
        self.ready = []
        self.flushes = 0
        self.after = None

    def add(self, stage):
        self.ready.append(stage)

    def carry(self, call, long=True):
        stages = [st for st in self.ready if long or not st.slow]
        self.ready = [st for st in self.ready if not (long or not st.slow)]
        own, outs = call(stages)
        k = 0
        for st in stages:
            n = len(st.inouts) + len(st.outs)
            st.then(*outs[k:k + n])
            k += n
        if self.after is not None:
            self.after()
        return own

    def flush(self):
        while self.ready:
            self.flushes += 1
            self.carry(lambda stages: _staged_call(
                lambda *refs: None, name=f"comm_tail_{self.flushes}", grid=(), in_specs=[], out_specs=[], out_shape=[],
                scratch_shapes=[], args=[], stages=stages))


def _mixer_fwd(layer, x, g1, bgate, lng, lnb, wm, bsf, wsc, win_g, wb_g, wout_g, stages):
    t_len = x.shape[0]
    tm = min(TM_MIX, t_len)
    nt = t_len // tm
    nb = tm // GMLP_BLOCK

    def core(x_ref, x_late_ref, g1_ref, bgate_ref, lng_ref, lnb_ref, wm_ref, bsf_ref, wsc_ref, win_hbm, wb_hbm, wout_hbm,
             zc_ref, ya_ref, yb_ref, q_ref, sa_ref, ca_ref, sb_ref, cb_ref, ug_ref, fu_ref, xh_ref, cv_ref,
             mg_ref, h_ref, x2_ref,
             win_v, wb_v, wout_v, carry, vn_s, f_s, z_s, sems):
        i = pl.program_id(0)

        @pl.when(i == 0)
        def _():
            cps = (_load_col_sharded(win_hbm, win_v, sems, 0) + _load_branch(wb_hbm, wb_v, sems, 4)
                   + _load_row_sharded(wout_hbm, wout_v, sems, 12))
            _start_all(cps)
            carry[...] = jnp.zeros_like(carry)
            z_s[...] = jnp.zeros_like(z_s)
            _wait_all(cps)

        xv = x_ref[...]
        r = lax.rsqrt(jnp.mean(xv * xv, axis=-1, keepdims=True) + RMS_EPS)
        h_ref[...] = (xv * r * g1_ref[...]).astype(BF16)

        def zcols(c0, n, keep=None):
            zv = z_s[:, c0:c0 + n]
            z_s[:, c0:c0 + n] = _dot(h_ref[...], win_v[:, c0:c0 + n])
            if keep is not None:
                zc_ref[:, keep * D_B:(keep + 1) * D_B] = zv.astype(BF16)
            return zv

        v = zcols(C_V, D_A)
        vg, tv = _gelu(v)
        mu = jnp.mean(vg, axis=-1, keepdims=True)
        vc = vg - mu
        rstd = lax.rsqrt(jnp.mean(vc * vc, axis=-1, keepdims=True) + LN_EPS)
        xh = vc * rstd
        xh_ref[...] = xh.astype(BF16)
        cv_ref[...] = (rstd * _gelu_grad(v, tv)).astype(BF16)
        vn_s[...] = (xh * lng_ref[...] + lnb_ref[...]).astype(BF16)
        for hd in range(A_HEADS):
            cols = slice(hd * 128, (hd + 1) * 128)
            vcat = jnp.concatenate([vn_s[b * 128:(b + 1) * 128, cols] for b in range(nb)], axis=1)
            fcat = _dot(wm_ref[hd], vcat)
            for b in range(nb):
                f_s[b * 128:(b + 1) * 128, cols] = fcat[:, b * 128:(b + 1) * 128]
        u = zcols(C_U, D_A)
        ug, tu = _gelu(u)
        ug_ref[...] = ug.astype(BF16)
        fb = f_s[...] + jnp.concatenate([bsf_ref[...]] * nb, axis=0)
        fu_ref[...] = (fb * _gelu_grad(u, tu)).astype(BF16)
        ya_ref[...] = (ug * fb).astype(BF16)

        p = zcols(C_CG, D_B, keep=1) * zcols(C_HB, D_B, keep=2)
        cr = carry[...]
        q = wsc_ref[0:1, :] * _shift_down(p, cr, 2) + wsc_ref[1:2, :] * _shift_down(p, cr, 1) + wsc_ref[2:3, :] * p
        carry[...] = p[tm - 8:tm, :]
        q_ref[...] = q.astype(BF16)
        yb_ref[...] = (zcols(C_BG, D_B, keep=0) * q).astype(BF16)

        av = _dot(ya_ref[...], wb_v[0])
        sa = _sigmoid(zcols(C_GA, D_MODEL) + bgate_ref[:, 0:D_MODEL])
        sa_ref[...] = sa.astype(BF16)
        mg = sa * av
        ca_ref[...] = (mg * (1.0 - sa)).astype(BF16)
        bv = _dot(yb_ref[...], wb_v[1])
        sb = _sigmoid(zcols(C_GB, D_MODEL) + bgate_ref[:, D_MODEL:2 * D_MODEL])
        sb_ref[...] = sb.astype(BF16)
        mb = sb * bv
        cb_ref[...] = (mb * (1.0 - sb)).astype(BF16)
        mg_ref[...] = (mg + mb).astype(BF16)
        x2_ref[...] = x_late_ref[...] + _dot(mg_ref[...], wout_v[...])

    def tile(n, lag):
        return pl.BlockSpec((tm, n), lambda i: (jnp.clip(i - lag, 0, nt - 1), 0))

    outs = [
        jax.ShapeDtypeStruct((t_len, 3 * D_B), BF16),
        jax.ShapeDtypeStruct((t_len, D_A), BF16),
        jax.ShapeDtypeStruct((t_len, D_B), BF16),
        jax.ShapeDtypeStruct((t_len, D_B), BF16),
        jax.ShapeDtypeStruct((t_len, D_MODEL), BF16),
        jax.ShapeDtypeStruct((t_len, D_MODEL), BF16),
        jax.ShapeDtypeStruct((t_len, D_MODEL), BF16),
        jax.ShapeDtypeStruct((t_len, D_MODEL), BF16),
        jax.ShapeDtypeStruct((t_len, D_A), BF16),
        jax.ShapeDtypeStruct((t_len, D_A), BF16),
        jax.ShapeDtypeStruct((t_len, D_A), BF16),
        jax.ShapeDtypeStruct((t_len, D_A), BF16),
        jax.ShapeDtypeStruct((t_len, D_MODEL), BF16),
        jax.ShapeDtypeStruct((t_len, D_MODEL), BF16),
        jax.ShapeDtypeStruct((t_len, D_MODEL), F32),
    ]
    return _staged_call(
        core, name=f"mixer_fwd_l{layer}", grid=(nt + 1,),
        in_specs=[tile(D_MODEL, 0), tile(D_MODEL, 1), _const_spec((1, D_MODEL)), _const_spec((1, 2 * D_MODEL)),
                  _const_spec((1, D_A)), _const_spec((1, D_A)), _const_spec((A_HEADS, 128, 128)),
                  _const_spec((128, D_A)), _const_spec((8, D_B)), ANY, ANY, ANY],
        out_specs=[tile(o.shape[1], 0 if k == len(outs) - 2 else 1) for k, o in enumerate(outs)],
        out_shape=outs,
        scratch_shapes=[pltpu.VMEM((D_MODEL, D_IN), BF16), pltpu.VMEM((2, D_A, D_MODEL), BF16),
                        pltpu.VMEM((D_MODEL, D_MODEL), BF16), pltpu.VMEM((8, D_B), F32),
                        pltpu.VMEM((tm, D_A), BF16), pltpu.VMEM((tm, D_A), F32), pltpu.VMEM((tm, D_IN), F32),
                        pltpu.SemaphoreType.DMA((16,))],
        args=[x, x, g1, bgate, lng, lnb, wm, bsf, wsc, win_g, wb_g, wout_g], stages=stages)


def _ffn_fwd(layer, x2, g2, wfc, bfc, wup_g, wdown_g, stages, head=None):
    t_len = x2.shape[0]
    tm = min(TM_FFN, t_len)
    nt = t_len // tm

    def core(*refs):
        if head is None:
            (x_ref, g2_ref, wfc_ref, bfc_ref, wup_hbm, wdown_hbm, up_ref, silu_ref, dsilu_ref, act_ref, h_ref, x3_ref,
             wup_v, wdown_v, carry, sems) = refs
        else:
            (x_ref, g2_ref, wfc_ref, bfc_ref, t_ref, gf_ref, wup_hbm, wdown_hbm, up_ref, silu_ref, dsilu_ref, act_ref,
             h_ref, dx_ref, dgf_ref, loss_ref, wup_v, wdown_v, carry, sems) = refs
        i = pl.program_id(0)

        @pl.when(i == 0)
        def _():
            cps = _load_col_sharded(wup_hbm, wup_v, sems, 0) + _load_row_sharded(wdown_hbm, wdown_v, sems, 4)
            _start_all(cps)
            carry[...] = jnp.zeros_like(carry)
            if head is not None:
                dgf_ref[...] = jnp.zeros_like(dgf_ref)
                loss_ref[...] = jnp.zeros_like(loss_ref)
            _wait_all(cps)

        xv = x_ref[...]
        r = lax.rsqrt(jnp.mean(xv * xv, axis=-1, keepdims=True) + RMS_EPS)
        h_ref[...] = (xv * r * g2_ref[...]).astype(BF16)
        gate = _dot(h_ref[...], wup_v[:, 0:D_FF])
        up_ref[:, 0:D_FF] = gate.astype(BF16)
        cr = carry[...]
        gc = (wfc_ref[0:1, :] * _shift_down(gate, cr, 2) + wfc_ref[1:2, :] * _shift_down(gate, cr, 1)
              + wfc_ref[2:3, :] * gate + bfc_ref[...])
        carry[...] = gate[tm - 8:tm, :]
        sg = _sigmoid(gc)
        silu = gc * sg
        silu_ref[...] = silu.astype(BF16)
        dsilu_ref[...] = (sg + silu * (1.0 - sg)).astype(BF16)
        val = _dot(h_ref[...], wup_v[:, D_FF:2 * D_FF])
        up_ref[:, D_FF:2 * D_FF] = val.astype(BF16)
        act_ref[...] = (silu * val).astype(BF16)
        x3 = x_ref[...] + _dot(act_ref[...], wdown_v[...])
        if head is None:
            x3_ref[...] = x3
        else:
            r3 = lax.rsqrt(jnp.mean(x3 * x3, axis=-1, keepdims=True) + RMS_EPS)
            xh = x3 * r3
            err = xh * gf_ref[...] - t_ref[...]
            loss_ref[...] += _colsum8(err * err)
            dy = err * (1.0 / D_MODEL)
            dgf_ref[...] += _colsum8(dy * xh)
            dxh = dy * gf_ref[...]
            dx_ref[...] = r3 * (dxh - xh * jnp.mean(dxh * xh, axis=-1, keepdims=True))

    outs = [
        jax.ShapeDtypeStruct((t_len, 2 * D_FF), BF16),
        jax.ShapeDtypeStruct((t_len, D_FF), BF16),
        jax.ShapeDtypeStruct((t_len, D_FF), BF16),
        jax.ShapeDtypeStruct((t_len, D_FF), BF16),
        jax.ShapeDtypeStruct((t_len, D_MODEL), BF16),
        jax.ShapeDtypeStruct((t_len, D_MODEL), F32),
    ]
    in_specs = [_row_spec(tm, D_MODEL), _const_spec((1, D_MODEL)), _const_spec((8, D_FF)), _const_spec((1, D_FF))]
    out_specs = [_row_spec(tm, o.shape[1]) for o in outs]
    args = [x2, g2, wfc, bfc]
    if head is not None:
        in_specs += [_row_spec(tm, D_MODEL), _const_spec((1, D_MODEL))]
        args += list(head)
        outs += [jax.ShapeDtypeStruct((8, D_MODEL), F32)] * 2
        out_specs += [_const_spec((8, D_MODEL))] * 2
    return _staged_call(
        core, name=f"ffn_fwd_l{layer}", grid=(nt,),
        in_specs=in_specs + [ANY, ANY], out_specs=out_specs, out_shape=outs,
        scratch_shapes=[pltpu.VMEM((D_MODEL, 2 * D_FF), BF16), pltpu.VMEM((D_FF, D_MODEL), BF16),
                        pltpu.VMEM((8, D_FF), F32), pltpu.SemaphoreType.DMA((8,))],
        args=args + [wup_g, wdown_g], stages=stages)


def _ffn_bwd(layer, dx3, x2, up, silu, dsilu, g2, wfc, wup_g, wdown_g, stages):
    t_len = x2.shape[0]
    tm = min(TM_FFN, t_len)
    nt = t_len // tm

    def core(dx3_ref, dx3_late_ref, x_ref, up_ref, silu_ref, dsilu_ref, g2_ref, wfc_ref, wup_hbm, wdown_hbm,
             dx2_ref, dup_ref, dx3b_ref, dg2_ref, dbfc_ref, dwfc_ref,
             wup_v, wdown_v, carry, da_s, dup_s, sems):
        i = pl.program_id(0)

        @pl.when(i == 0)
        def _():
            cps = _load_col_sharded(wup_hbm, wup_v, sems, 0) + _load_row_sharded(wdown_hbm, wdown_v, sems, 4)
            _start_all(cps)
            for ref in (carry, da_s, dup_s, dg2_ref, dbfc_ref, dwfc_ref):
                ref[...] = jnp.zeros_like(ref)
            _wait_all(cps)

        live = (i <= nt).astype(F32)
        dx3b_ref[...] = dx3_ref[...].astype(BF16)
        dh = jnp.zeros((tm, D_MODEL), F32)
        for c0, c1 in FF_CHUNKS:
            v0, v1 = D_FF + c0, D_FF + c1
            dh = dh + _dot_nt(dup_s[:, c0:c1], wup_v[:, c0:c1]) + _dot_nt(dup_s[:, v0:v1], wup_v[:, v0:v1])
            da = da_s[:, c0:c1]
            dval = (da * silu_ref[:, c0:c1].astype(F32)).astype(BF16)
            dup_ref[:, v0:v1] = dval
            dup_s[:, v0:v1] = dval
            dgc = da * up_ref[:, v0:v1].astype(F32) * dsilu_ref[:, c0:c1].astype(F32)
            cr = carry[:, c0:c1]
            dgc1 = _shift_up(dgc, cr, 1)
            dgc2 = _shift_up(dgc, cr, 2)
            carry[:, c0:c1] = jnp.where(i < nt, dgc[0:8, :], cr)
            gate = up_ref[:, c0:c1].astype(F32)
            dbfc_ref[:, c0:c1] += live * _colsum8(dgc)
            dwfc_ref[0, :, c0:c1] += live * _colsum8(dgc2 * gate)
            dwfc_ref[1, :, c0:c1] += live * _colsum8(dgc1 * gate)
            dwfc_ref[2, :, c0:c1] += live * _colsum8(dgc * gate)
            dgate = (wfc_ref[2:3, c0:c1] * dgc + wfc_ref[1:2, c0:c1] * dgc1 + wfc_ref[0:1, c0:c1] * dgc2).astype(BF16)
            dup_ref[:, c0:c1] = dgate
            dup_s[:, c0:c1] = dgate
            da_s[:, c0:c1] = _dot_nt(dx3b_ref[...], wdown_v[c0:c1, :])
        xv = x_ref[...]
        r = lax.rsqrt(jnp.mean(xv * xv, axis=-1, keepdims=True) + RMS_EPS)
        xh = xv * r
        dg2_ref[...] += _colsum8(dh * xh)
        dxh = dh * g2_ref[...]
        dx2_ref[...] = dx3_late_ref[...] + r * (dxh - xh * jnp.mean(dxh * xh, axis=-1, keepdims=True))

    def tile(n, lag):
        return pl.BlockSpec((tm, n), lambda i: (nt - 1 - jnp.clip(i - lag, 0, nt - 1), 0))

    outs = [
        jax.ShapeDtypeStruct((t_len, D_MODEL), F32),
        jax.ShapeDtypeStruct((t_len, 2 * D_FF), BF16),
        jax.ShapeDtypeStruct((t_len, D_MODEL), BF16),
        jax.ShapeDtypeStruct((8, D_MODEL), F32),
        jax.ShapeDtypeStruct((8, D_FF), F32),
        jax.ShapeDtypeStruct((3, 8, D_FF), F32),
    ]
    return _staged_call(
        core, name=f"ffn_bwd_l{layer}", grid=(nt + 2,),
        in_specs=[tile(D_MODEL, 0), tile(D_MODEL, 2), tile(D_MODEL, 2), tile(2 * D_FF, 1), tile(D_FF, 1), tile(D_FF, 1),
                  _const_spec((1, D_MODEL)), _const_spec((8, D_FF)), ANY, ANY],
        out_specs=[tile(D_MODEL, 2), tile(2 * D_FF, 1), tile(D_MODEL, 0),
                   _const_spec((8, D_MODEL)), _const_spec((8, D_FF)), _const_spec((3, 8, D_FF))],
        out_shape=outs,
        scratch_shapes=[pltpu.VMEM((D_MODEL, 2 * D_FF), BF16), pltpu.VMEM((D_FF, D_MODEL), BF16),
                        pltpu.VMEM((8, D_FF), F32), pltpu.VMEM((tm, D_FF), F32), pltpu.VMEM((tm, 2 * D_FF), BF16),
                        pltpu.SemaphoreType.DMA((8,))],
        args=[dx3, dx3, x2, up, silu, dsilu, g2, wfc, wup_g, wdown_g], stages=stages)


def _mixer_bwd(layer, dx2, x, zc, qs, sa, ca, sb, cb, ug, fu, xhs, cv, g1, lng, lnb, wmt, wsc, win_g, wb_g, wout_g,
               stages):
    t_len = x.shape[0]
    tm = min(TM_MIX, t_len)
    nt = t_len // tm
    nb = tm // GMLP_BLOCK

    def core(dx2_ref, x_ref, zc_ref, q_ref, sa_ref, ca_ref, sb_ref, cb_ref, ug_ref, fu_ref, xh_ref, cv_ref,
             g1_ref, lng_ref, lnb_ref, wmt_ref, wsc_ref, win_hbm, wb_hbm, wout_hbm,
             dx_ref, dz_ref, da_ref, db_ref, dx2b_ref, dg1_ref, dbgate_ref, dlng_ref, dlnb_ref, dwm_ref, dbsf_ref, dwsc_ref,
             win_v, wb_v, wout_v, carry, vn_s, df_s, dvn_s, sems):
        i = pl.program_id(0)

        @pl.when(i == 0)
        def _():
            cps = (_load_col_sharded(win_hbm, win_v, sems, 0) + _load_branch(wb_hbm, wb_v, sems, 4)
                   + _load_row_sharded(wout_hbm, wout_v, sems, 12))
            _start_all(cps)
            for ref in (carry, dg1_ref, dbgate_ref, dlng_ref, dlnb_ref, dwm_ref, dbsf_ref, dwsc_ref):
                ref[...] = jnp.zeros_like(ref)
            _wait_all(cps)

        def kept(k):
            return zc_ref[:, k * D_B:(k + 1) * D_B].astype(F32)

        def dz_cols(c0, n, val):
            dz_ref[:, c0:c0 + n] = val.astype(BF16)
            return _dot_nt(dz_ref[:, c0:c0 + n], win_v[:, c0:c0 + n])

        dx2b_ref[...] = dx2_ref[...].astype(BF16)
        dm = _dot_nt(dx2b_ref[...], wout_v[...])
        da_ref[...] = (dm * sa_ref[...].astype(F32)).astype(BF16)
        dga = dm * ca_ref[...].astype(F32)
        dh = dz_cols(C_GA, D_MODEL, dga)
        dbgate_ref[:, 0:D_MODEL] += _colsum8(dga)
        dya = _dot_nt(da_ref[...], wb_v[0])
        db_ref[...] = (dm * sb_ref[...].astype(F32)).astype(BF16)
        dgb = dm * cb_ref[...].astype(F32)
        dh = dh + dz_cols(C_GB, D_MODEL, dgb)
        dbgate_ref[:, D_MODEL:2 * D_MODEL] += _colsum8(dgb)
        dyb = _dot_nt(db_ref[...], wb_v[1])

        xh = xh_ref[...].astype(F32)
        vn_s[...] = (xh * lng_ref[...] + lnb_ref[...]).astype(BF16)
        df = dya * ug_ref[...].astype(F32)
        df_s[...] = df.astype(BF16)
        dbsf_acc = df[0:128, :]
        for b in range(1, nb):
            dbsf_acc = dbsf_acc + df[b * 128:(b + 1) * 128, :]
        dbsf_ref[...] += dbsf_acc
        for hd in range(A_HEADS):
            cols = slice(hd * 128, (hd + 1) * 128)
            vcat = jnp.concatenate([vn_s[b * 128:(b + 1) * 128, cols] for b in range(nb)], axis=1)
            dcat = jnp.concatenate([df_s[b * 128:(b + 1) * 128, cols] for b in range(nb)], axis=1)
            gcat = _dot(wmt_ref[hd], dcat)
            dwm_ref[hd] += _dot_nt(dcat, vcat)
            for b in range(nb):
                dvn_s[b * 128:(b + 1) * 128, cols] = gcat[:, b * 128:(b + 1) * 128]
        dh = dh + dz_cols(C_U, D_A, dya * fu_ref[...].astype(F32))
        dvn = dvn_s[...]
        dlng_ref[...] += _colsum8(dvn * xh)
        dlnb_ref[...] += _colsum8(dvn)
        dxh = dvn * lng_ref[...]
        dvc = dxh - jnp.mean(dxh, axis=-1, keepdims=True) - xh * jnp.mean(dxh * xh, axis=-1, keepdims=True)
        dh = dh + dz_cols(C_V, D_A, dvc * cv_ref[...].astype(F32))

        cg = kept(1)
        hbv = kept(2)
        p = cg * hbv
        dh = dh + dz_cols(C_BG, D_B, dyb * q_ref[...].astype(F32))
        dq = dyb * kept(0)
        cr = carry[...]
        dq1 = _shift_up(dq, cr, 1)
        dq2 = _shift_up(dq, cr, 2)
        carry[...] = dq[0:8, :]
        dwsc_ref[0] += _colsum8(dq2 * p)
        dwsc_ref[1] += _colsum8(dq1 * p)
        dwsc_ref[2] += _colsum8(dq * p)
        dp = wsc_ref[2:3, :] * dq + wsc_ref[1:2, :] * dq1 + wsc_ref[0:1, :] * dq2
        dh = dh + dz_cols(C_CG, D_B, dp * hbv)
        dh = dh + dz_cols(C_HB, D_B, dp * cg)

        xv = x_ref[...]
        r = lax.rsqrt(jnp.mean(xv * xv, axis=-1, keepdims=True) + RMS_EPS)
        xn = xv * r
        dg1_ref[...] += _colsum8(dh * xn)
        dxn = dh * g1_ref[...]
        dx_ref[...] = dx2_ref[...] + r * (dxn - xn * jnp.mean(dxn * xn, axis=-1, keepdims=True))

    outs = [
        jax.ShapeDtypeStruct((t_len, D_MODEL), F32),
        jax.ShapeDtypeStruct((t_len, D_IN), BF16),
        jax.ShapeDtypeStruct((t_len, D_MODEL), BF16),
        jax.ShapeDtypeStruct((t_len, D_MODEL), BF16),
        jax.ShapeDtypeStruct((t_len, D_MODEL), BF16),
        jax.ShapeDtypeStruct((8, D_MODEL), F32),
        jax.ShapeDtypeStruct((8, 2 * D_MODEL), F32),
        jax.ShapeDtypeStruct((8, D_A), F32),
        jax.ShapeDtypeStruct((8, D_A), F32),
        jax.ShapeDtypeStruct((A_HEADS, 128, 128), F32),
        jax.ShapeDtypeStruct((128, D_A), F32),
        jax.ShapeDtypeStruct((3, 8, D_B), F32),
    ]

    return _staged_call(
        core, name=f"mixer_bwd_l{layer}", grid=(nt,),
        in_specs=[_row_spec(tm, D_MODEL, nt), _row_spec(tm, D_MODEL, nt), _row_spec(tm, 3 * D_B, nt),
                  _row_spec(tm, D_B, nt), _row_spec(tm, D_MODEL, nt), _row_spec(tm, D_MODEL, nt),
                  _row_spec(tm, D_MODEL, nt), _row_spec(tm, D_MODEL, nt), _row_spec(tm, D_A, nt), _row_spec(tm, D_A, nt),
                  _row_spec(tm, D_A, nt), _row_spec(tm, D_A, nt),
                  _const_spec((1, D_MODEL)), _const_spec((1, D_A)), _const_spec((1, D_A)),
                  _const_spec((A_HEADS, 128, 128)), _const_spec((8, D_B)), ANY, ANY, ANY],
        out_specs=[_row_spec(tm, D_MODEL, nt), _row_spec(tm, D_IN, nt), _row_spec(tm, D_MODEL, nt),
                   _row_spec(tm, D_MODEL, nt), _row_spec(tm, D_MODEL, nt),
                   _const_spec((8, D_MODEL)), _const_spec((8, 2 * D_MODEL)), _const_spec((8, D_A)), _const_spec((8, D_A)),
                   _const_spec((A_HEADS, 128, 128)), _const_spec((128, D_A)), _const_spec((3, 8, D_B))],
        out_shape=outs,
        scratch_shapes=[pltpu.VMEM((D_MODEL, D_IN), BF16), pltpu.VMEM((2, D_A, D_MODEL), BF16),
                        pltpu.VMEM((D_MODEL, D_MODEL), BF16), pltpu.VMEM((8, D_B), F32),
                        pltpu.VMEM((tm, D_A), BF16), pltpu.VMEM((tm, D_A), BF16), pltpu.VMEM((tm, D_A), F32),
                        pltpu.SemaphoreType.DMA((16,))],
        args=[dx2, x, zc, qs, sa, ca, sb, cb, ug, fu, xhs, cv, g1, lng, lnb, wmt, wsc, win_g, wb_g, wout_g],
        stages=stages)


def _wgrad(name, layer, a, b, rows, cols, row_blk, col_blk, stages, a_first=0):
    t_len = a.shape[0]
    n = b.shape[1]
    tk = min(TK_WGRAD, t_len)
    col_sharded = n == N_CHIPS * cols
    m = rows if col_sharded else a.shape[1]
    grid = (m // row_blk, n // col_blk, t_len // tk)
    shards = col_blk // cols if col_sharded else 1

    if col_sharded:
        out_shape = (N_CHIPS, rows, cols)
        out_spec = pl.BlockSpec((shards, row_blk, cols), lambda i, j, k: (j, i, 0))
    else:
        out_shape = (N_CHIPS * rows, cols)
        out_spec = pl.BlockSpec((row_blk, col_blk), lambda i, j, k: (i, j))

    def core(a_ref, b_ref, o_ref):
        @pl.when(pl.program_id(2) == 0)
        def _():
            o_ref[...] = jnp.zeros_like(o_ref)

        g = _dot_tn(a_ref[...], b_ref[...])
        if col_sharded:
            for q in range(shards):
                o_ref[q] += g[:, q * cols:(q + 1) * cols]
        else:
            o_ref[...] += g

    own, outs = _staged_call(
        core, name=f"wgrad_{name}_l{layer}", grid=grid,
        in_specs=[pl.BlockSpec((tk, row_blk), lambda i, j, k: (k, a_first + i)),
                  pl.BlockSpec((tk, col_blk), lambda i, j, k: (k, j))],
        out_specs=[out_spec], out_shape=[jax.ShapeDtypeStruct(out_shape, F32)], scratch_shapes=[],
        args=[a, b], stages=stages)
    return [own[0].reshape(N_CHIPS, rows, cols)], outs


def _wgrad_branch(layer, ya, da, yb, db, stages):
    t_len = ya.shape[0]
    tk = min(TK_WGRAD, t_len)

    cs = D_MODEL // N_CHIPS

    def core(ya_ref, da_ref, yb_ref, db_ref, o_ref):
        @pl.when(pl.program_id(0) == 0)
        def _():
            o_ref[...] = jnp.zeros_like(o_ref)

        ga = _dot_tn(ya_ref[...], da_ref[...])
        gb = _dot_tn(yb_ref[...], db_ref[...])
        for k in range(N_CHIPS):
            o_ref[k, 0:D_A, :] += ga[:, k * cs:(k + 1) * cs]
            o_ref[k, D_A:2 * D_A, :] += gb[:, k * cs:(k + 1) * cs]

    a_spec = pl.BlockSpec((tk, D_A), lambda k: (k, 0))
    d_spec = pl.BlockSpec((tk, D_MODEL), lambda k: (k, 0))
    return _staged_call(
        core, name=f"wgrad_w_branch_l{layer}", grid=(t_len // tk,),
        in_specs=[a_spec, d_spec, a_spec, d_spec],
        out_specs=[pl.BlockSpec((N_CHIPS, 2 * D_A, cs), lambda k: (0, 0, 0))],
        out_shape=[jax.ShapeDtypeStruct((N_CHIPS, 2 * D_A, cs), F32)], scratch_shapes=[],
        args=[ya, da, yb, db], stages=stages)


def _flat_blk(rows, cols):
    blk = rows
    while blk * cols * 4 > 2 * 1024 * 1024 and blk % 16 == 0:
        blk //= 2
    return blk


def _cast_into_slots(name, layer, ws, chip):
    blks = [_flat_blk(w.shape[1], w.shape[2]) for w in ws]
    nblks = [w.shape[1] // b for w, b in zip(ws, blks)]
    n = len(ws)

    def body(chip_ref, *refs):
        for w_ref, o_ref in zip(refs[:n], refs[n:]):
            o_ref[...] = w_ref[...].astype(BF16)

    def in_spec(w, blk, nblk):
        return pl.BlockSpec((None, blk, w.shape[2]), lambda i, chip_ref: (layer, jnp.minimum(i, nblk - 1), 0))

    def out_spec(w, blk, nblk):
        return pl.BlockSpec((None, blk, w.shape[2]), lambda i, chip_ref: (chip_ref[0], jnp.minimum(i, nblk - 1), 0))

    return pl.pallas_call(
        body, name=f"cast_{name}_l{layer}",
        grid_spec=pltpu.PrefetchScalarGridSpec(
            num_scalar_prefetch=1, grid=(max(nblks),),
            in_specs=[in_spec(w, b, k) for w, b, k in zip(ws, blks, nblks)],
            out_specs=[out_spec(w, b, k) for w, b, k in zip(ws, blks, nblks)]),
        out_shape=[jax.ShapeDtypeStruct((N_CHIPS,) + w.shape[1:], BF16) for w in ws],
        compiler_params=_params(),
    )(chip, *ws)


def _reduction_sums(name, jobs, pos):
    in_specs, out_specs, out_shape, args, bodies, counts = [], [], [], [], [], []
    for job in jobs:
        kind, grad, other = job[0], job[1], job[2]
        _, h, cols = other.shape
        blk = _flat_blk(h, cols)
        nblk = h // blk
        if kind == "pair":
            total = N_CHIPS * nblk

            def block(s, total=total, nblk=nblk):
                b = jnp.minimum(s, total - 1)
                return b // nblk, b % nblk

            spec = pl.BlockSpec((None, blk, cols), lambda s, p, block=block: (block(s)[0], block(s)[1], 0))
            in_specs += [pl.BlockSpec((None, blk, cols), lambda s, p, block=block, nblk=nblk:
                                      (block(s)[0], p[1] * nblk + block(s)[1], 0)), spec]
            out_specs.append(spec)
            out_shape.append(jax.ShapeDtypeStruct((N_CHIPS, h, cols), BF16))
            args += [grad, other]
            bodies.append((2, lambda g, o, out: out.__setitem__(..., (g[...] + o[...]).astype(BF16))))
        else:
            total = nblk

            def block(s, total=total):
                return jnp.minimum(s, total - 1)

            in_specs += [pl.BlockSpec((None, blk, cols), lambda s, p, block=block, nblk=nblk:
                                      (p[0], p[1] * nblk + block(s), 0)),
                         pl.BlockSpec((None, blk, cols), lambda s, p, block=block: (p[0], block(s), 0)),
                         pl.BlockSpec((3, blk, cols), lambda s, p, block=block: (0, block(s), 0))]
            out_specs.append(pl.BlockSpec((blk, cols), lambda s, p, block=block, nblk=nblk: (p[1] * nblk + block(s), 0)))
            out_shape.append(jax.ShapeDtypeStruct((2 * h, cols), F32))
            args += [grad, other, job[3]]
            bodies.append((3, lambda g, o, r, out: out.__setitem__(
                ..., (((g[...] + o[...]) + r[0].astype(F32)) + r[1].astype(F32)) + r[2].astype(F32))))
        counts.append(total)

    def body(pos_ref, *refs):
        ins, outs = refs[:len(args)], refs[len(args):]
        k = 0
        for (n_in, fn), out in zip(bodies, outs):
            fn(*ins[k:k + n_in], out)
            k += n_in

    return pl.pallas_call(
        body, name=f"reduction_sums_{name}",
        grid_spec=pltpu.PrefetchScalarGridSpec(num_scalar_prefetch=1, grid=(max(counts),), in_specs=in_specs,
                                               out_specs=out_specs),
        out_shape=out_shape,
        compiler_params=_params(),
    )(pos, *args)


def _sum_slots(name, slots):
    n, rows, _ = slots.shape

    def body(s_ref, o_ref):
        acc = s_ref[0]
        for d in range(1, n):
            acc = acc + s_ref[d]
        o_ref[...] = acc

    return pl.pallas_call(
        body, name=f"sum_slots_{name}", grid=(1,),
        in_specs=[pl.BlockSpec((n, rows, 128), lambda i: (0, 0, 0))],
        out_specs=pl.BlockSpec((rows, 128), lambda i: (0, 0)),
        out_shape=jax.ShapeDtypeStruct((rows, 128), F32),
        compiler_params=_params(),
    )(slots)


def _adamw_math(w, g, m, v):
    m2 = ADAM_B1 * m + (1.0 - ADAM_B1) * g
    v2 = ADAM_B2 * v + (1.0 - ADAM_B2) * (g * g)
    m_hat = m2 / (1.0 - ADAM_B1 ** ADAM_STEP)
    v_hat = v2 / (1.0 - ADAM_B2 ** ADAM_STEP)
    delta = -ADAM_LR * (m_hat / (jnp.sqrt(v_hat) + ADAM_EPS) + ADAM_WD * w)
    return delta, m2, v2


def _adamw_big(name, w, g0, g1, m, v):
    _, rows, cols = w.shape
    blk = _flat_blk(rows, cols) // 2

    def body(w_ref, g0_ref, g1_ref, m_ref, v_ref, g_ref, d_ref, m2_ref, v2_ref):
        g = jnp.where(pl.program_id(0) == 0, g0_ref[...], g1_ref[...])
        d, m2, v2 = _adamw_math(w_ref[...], g, m_ref[...], v_ref[...])
        g_ref[...] = g
        d_ref[...] = d
        m2_ref[...] = m2
        v2_ref[...] = v2

    spec = pl.BlockSpec((None, blk, cols), lambda la, i: (la, i, 0))
    return pl.pallas_call(
        body, name=f"adamw_{name}", grid=(N_LAYERS, rows // blk),
        in_specs=[spec, pl.BlockSpec((blk, cols), lambda la, i: (i * (1 - la), 0)),
                  pl.BlockSpec((blk, cols), lambda la, i: (i * la, 0)), spec, spec],
        out_specs=[spec] * 4,
        out_shape=[jax.ShapeDtypeStruct(w.shape, F32)] * 4,
        compiler_params=_params(("parallel", "parallel")),
    )(w, g0, g1, m, v)


def _adamw_small(ws, gs, ms, vs):
    n = len(ws)

    def body(*refs):
        ins, outs = refs[:4 * n], refs[4 * n:]
        for k in range(n):
            d, m2, v2 = _adamw_math(ins[k][...], ins[n + k][...], ins[2 * n + k][...], ins[3 * n + k][...])
            outs[k][...] = d
            outs[n + k][...] = m2
            outs[2 * n + k][...] = v2

    vmem = pl.BlockSpec(memory_space=pltpu.VMEM)
    return pl.pallas_call(
        body, name="adamw_small",
        in_specs=[vmem] * (4 * n), out_specs=[vmem] * (3 * n),
        out_shape=[jax.ShapeDtypeStruct(w.shape, F32) for w in ws] * 3,
        compiler_params=pltpu.CompilerParams(vmem_limit_bytes=V7X_VMEM_LIMIT),
    )(*ws, *gs, *ms, *vs)


SMALL = ("norm1_g", "b_gate", "gmlp_ln_g", "gmlp_ln_b", "w_spatial", "b_spatial", "w_shortconv", "norm2_g",
         "w_ffn_conv", "b_ffn_conv", "final_g")
ALL_WEIGHTS = ("norm1_g", "w_in", "b_gate", "gmlp_ln_g", "gmlp_ln_b", "w_spatial", "b_spatial", "w_shortconv",
               "w_branch", "w_out", "norm2_g", "w_ffn_up", "w_ffn_conv", "b_ffn_conv", "w_ffn_down", "final_g")


def _pack(arrays):
    flat = jnp.concatenate([a.reshape(-1) for a in arrays])
    n = flat.shape[0]
    rows = -(-n // 1024) * 8
    return jnp.pad(flat, (0, rows * 128 - n)).reshape(rows, 128)


def _unpack(packed, like):
    flat = packed.reshape(-1)
    out, off = [], 0
    for a in like:
        out.append(flat[off:off + a.size].reshape(a.shape))
        off += a.size
    return out


def _pad8(w):
    return jnp.pad(w, ((0, 5), (0, 0)))


def kernel(x, norm1_g, w_in, b_gate, gmlp_ln_g, gmlp_ln_b, w_spatial, b_spatial, w_shortconv, w_branch, w_out, norm2_g, w_ffn_up, w_ffn_conv, b_ffn_conv, w_ffn_down, final_g, loss_target, m_norm1_g, m_w_in, m_b_gate, m_gmlp_ln_g, m_gmlp_ln_b, m_w_spatial, m_b_spatial, m_w_shortconv, m_w_branch, m_w_out, m_norm2_g, m_w_ffn_up, m_w_ffn_conv, m_b_ffn_conv, m_w_ffn_down, m_final_g, v_norm1_g, v_w_in, v_b_gate, v_gmlp_ln_g, v_gmlp_ln_b, v_w_spatial, v_b_spatial, v_w_shortconv, v_w_branch, v_w_out, v_norm2_g, v_w_ffn_up, v_w_ffn_conv, v_b_ffn_conv, v_w_ffn_down, v_final_g):
    weights = dict(norm1_g=norm1_g, w_in=w_in, b_gate=b_gate, gmlp_ln_g=gmlp_ln_g, gmlp_ln_b=gmlp_ln_b,
                   w_spatial=w_spatial, b_spatial=b_spatial, w_shortconv=w_shortconv, w_branch=w_branch, w_out=w_out,
                   norm2_g=norm2_g, w_ffn_up=w_ffn_up, w_ffn_conv=w_ffn_conv, b_ffn_conv=b_ffn_conv,
                   w_ffn_down=w_ffn_down, final_g=final_g)
    mom = dict(norm1_g=m_norm1_g, w_in=m_w_in, b_gate=m_b_gate, gmlp_ln_g=m_gmlp_ln_g, gmlp_ln_b=m_gmlp_ln_b,
               w_spatial=m_w_spatial, b_spatial=m_b_spatial, w_shortconv=m_w_shortconv, w_branch=m_w_branch,
               w_out=m_w_out, norm2_g=m_norm2_g, w_ffn_up=m_w_ffn_up, w_ffn_conv=m_w_ffn_conv,
               b_ffn_conv=m_b_ffn_conv, w_ffn_down=m_w_ffn_down, final_g=m_final_g)
    vel = dict(norm1_g=v_norm1_g, w_in=v_w_in, b_gate=v_b_gate, gmlp_ln_g=v_gmlp_ln_g, gmlp_ln_b=v_gmlp_ln_b,
               w_spatial=v_w_spatial, b_spatial=v_b_spatial, w_shortconv=v_w_shortconv, w_branch=v_w_branch,
               w_out=v_w_out, norm2_g=v_norm2_g, w_ffn_up=v_w_ffn_up, w_ffn_conv=v_w_ffn_conv,
               b_ffn_conv=v_b_ffn_conv, w_ffn_down=v_w_ffn_down, final_g=v_final_g)

    cx, cy, cc = _mesh_pos()
    chip = 2 * cx + cy
    chip_arr = chip.astype(jnp.int32).reshape(1)
    pos_arr = jnp.stack([chip, cc]).astype(jnp.int32)
    t_len = x.shape[1]
    xs = x.reshape(t_len, D_MODEL)
    target = loss_target.reshape(t_len, D_MODEL)
    pipe = _Pipe()

    full = {}

    def gather(group, names, la):
        slots = _cast_into_slots(group, la, [weights[n].reshape((N_LAYERS,) + BIG[n]) for n in names], chip_arr)

        def then(*bufs):
            full.update(zip([(n, la) for n in names], bufs))

        pipe.add(_gather_stage(slots, then))

    mixer_w = ("w_in", "w_branch", "w_out")
    ffn_w = ("w_ffn_up", "w_ffn_down")
    gather("mixer", mixer_w, 0)
    tap_slots = {}
    pipe.add(_chip_spread_stage(_pack([w_shortconv, w_ffn_conv]), lambda slots: tap_slots.__setitem__("all", slots)))
    pipe.flush()
    by_chip = [_unpack(tap_slots["all"][k], [w_shortconv, w_ffn_conv]) for k in range(N_CHIPS)]
    wsc_full = jnp.concatenate([t[0] for t in by_chip], axis=-1)
    wfc_full = jnp.concatenate([t[1] for t in by_chip], axis=-1)

    idx = jnp.arange(GMLP_BLOCK) // CHUNK
    mask = idx[None, :] <= idx[:, None]
    wm_all = jnp.where(mask[None, None], w_spatial, 0.0)
    wm_bf = wm_all.astype(BF16)
    wmt_bf = jnp.swapaxes(wm_all, -1, -2).astype(BF16)
    bsf = jnp.repeat(jnp.swapaxes(b_spatial, -1, -2), 128, axis=-1)

    def row(a):
        return a.reshape(1, -1)

    def mixer_args(la):
        return (row(norm1_g[la]), row(b_gate[la]), row(gmlp_ln_g[la]), row(gmlp_ln_b[la]))

    def mixer_weights(la):
        return tuple(full[(n, la)] for n in mixer_w)

    def ffn_weights(la):
        return tuple(full[(n, la)] for n in ffn_w)

    saved = []
    h_in = xs
    for la in range(N_LAYERS):
        gather("ffn", ffn_w, la)
        *kept, mg, h1, x2 = pipe.carry(lambda st: _mixer_fwd(
            la, h_in, *mixer_args(la), wm_bf[la], bsf[la], _pad8(wsc_full[la]), *mixer_weights(la), st))
        ya, yb = kept[1], kept[2]
        if la + 1 < N_LAYERS:
            gather("mixer", mixer_w, la + 1)
        head = (target, row(final_g)) if la == N_LAYERS - 1 else None
        up, silu, dsilu, act, h2, *rest = pipe.carry(lambda st: _ffn_fwd(
            la, x2, row(norm2_g[la]), _pad8(wfc_full[la]), row(b_ffn_conv[la]), *ffn_weights(la), st, head=head))
        saved.append(dict(x=h_in, ya=ya, yb=yb, mixer=[kept[0]] + kept[3:], mg=mg, h1=h1, x2=x2, up=up, silu=silu,
                          dsilu=dsilu, act=act, h2=h2))
        h_in = rest[0]
    dx, dgf8, loss8 = rest

    reduced_big = {}

    sums_due = []

    def run_sums():
        if sums_due:
            due = list(sums_due)
            sums_due.clear()
            run_sums.calls += 1
            for (_, then), res in zip(due, _reduction_sums(str(run_sums.calls), [job for job, _ in due], pos_arr)):
                then(res)

    run_sums.calls = 0
    pipe.after = run_sums

    def reduce_big(name, la, grad):
        def after_pair(other):
            def after_chips(got):
                sums_due.append((("chip", grad, other, got), lambda final: pipe.add(_pair_fill_stage(
                    final, lambda done: reduced_big.__setitem__((name, la), done)))))

            sums_due.append((("pair", grad, other), lambda psum: pipe.add(_chip_send_stage(psum, after_chips))))

        pipe.add(_pair_send_stage(grad, after_pair))

    small = {n: [None] * N_LAYERS for n in SMALL}
    spread = {}
    for la in reversed(range(N_LAYERS)):
        s = saved[la]
        dx3 = dx
        dx2, dup, dx3b, dg2, dbfc, dwfc = pipe.carry(lambda st: _ffn_bwd(
            la, dx3, s["x2"], s["up"], s["silu"], s["dsilu"], row(norm2_g[la]), _pad8(wfc_full[la]),
            *ffn_weights(la), st))
        g, = pipe.carry(lambda st: _wgrad("w_ffn_up", la, s["h2"], dup, 1024, 1408, 512, 2816, st))
        reduce_big("w_ffn_up", la, g)
        g, = pipe.carry(lambda st: _wgrad("w_ffn_down", la, s["act"], dx3b, 704, 1024, 1408, 1024, st))
        reduce_big("w_ffn_down", la, g)
        run = pipe.carry if la > 0 else (lambda call: call([])[0])
        dxl, dz, da, db, dx2b, dg1, dbg, dlng, dlnb, dwm, dbsf, dwsc = run(lambda st: _mixer_bwd(
            la, dx2, s["x"], *s["mixer"], row(norm1_g[la]), row(gmlp_ln_g[la]), row(gmlp_ln_b[la]), wmt_bf[la],
            _pad8(wsc_full[la]), *mixer_weights(la), st))
        small["norm1_g"][la] = dg1.sum(0)
        small["b_gate"][la] = dbg.sum(0)
        small["gmlp_ln_g"][la] = dlng.sum(0)
        small["gmlp_ln_b"][la] = dlnb.sum(0)
        small["w_spatial"][la] = jnp.where(mask[None], dwm, 0.0)
        small["b_spatial"][la] = dbsf.reshape(128, A_HEADS, 128).sum(-1).T
        small["w_shortconv"][la] = dwsc.sum(1)
        small["norm2_g"][la] = dg2.sum(0)
        small["w_ffn_conv"][la] = dwfc.sum(1)
        small["b_ffn_conv"][la] = dbfc.sum(0)
        if la == 0:
            small_local = ([jnp.stack(small[n]) for n in SMALL[:-1]]
                           + [dgf8.sum(0), 0.5 * loss8.sum().reshape(1) / D_MODEL])
            mine = _pack(small_local)

            def after_swap(other, mine=mine):
                pair = _sum_slots("small_pair", jnp.stack([mine, other]))
                pipe.add(_chip_spread_stage(pair, lambda slots: spread.__setitem__("slots", slots)))

            pipe.add(_pair_swap_stage(mine, after_swap))
        if la > 0:
            g, = pipe.carry(lambda st: _wgrad("w_in", la, s["h1"], dz, 1024, 1152, 512, 2304, st))
            reduce_big("w_in", la, g)
        else:
            for part, tag in enumerate(("w_in_a", "w_in_b")):
                g, = pipe.carry(lambda st: _wgrad(tag, la, s["h1"], dz, 512, 1152, 512, 2304, st, a_first=part))
                reduce_big(tag, la, g)
        g, = pipe.carry(lambda st: _wgrad("w_out", la, s["mg"], dx2b, 256, 1024, 1024, 1024, st), long=False)
        reduce_big("w_out", la, g)
        g, = pipe.carry(lambda st: _wgrad_branch(la, s["ya"], da, s["yb"], db, st), long=False)
        reduce_big("w_branch", la, g)
        dx = dxl
    grad_x = dx.reshape(x.shape)
    pipe.flush()

    reduced_big[("w_in", 0)] = jnp.concatenate([reduced_big[("w_in_a", 0)], reduced_big[("w_in_b", 0)]], axis=0)
    reduced = _unpack(_sum_slots("small_grads", spread["slots"]), small_local)
    loss = reduced[-1].reshape(())
    grads = dict(zip(SMALL, reduced[:-1]))
    grads["w_shortconv"] = lax.dynamic_slice(grads["w_shortconv"], (0, 0, chip * (D_B // 4)), (N_LAYERS, 3, D_B // 4))
    grads["w_ffn_conv"] = lax.dynamic_slice(grads["w_ffn_conv"], (0, 0, chip * (D_FF // 4)), (N_LAYERS, 3, D_FF // 4))

    delta, new_m, new_v = {}, {}, {}
    for n in BIG_NAMES:
        shape3 = (N_LAYERS,) + BIG[n]
        res = _adamw_big(n, weights[n].reshape(shape3), reduced_big[(n, 0)], reduced_big[(n, 1)],
                         mom[n].reshape(shape3), vel[n].reshape(shape3))
        grads[n], delta[n], new_m[n], new_v[n] = (a.reshape(weights[n].shape) for a in res)
    res = _adamw_small(*[[src[n].reshape(-1, src[n].shape[-1]) for n in SMALL] for src in (weights, grads, mom, vel)])
    for k, n in enumerate(SMALL):
        delta[n], new_m[n], new_v[n] = (res[j * len(SMALL) + k].reshape(weights[n].shape) for j in range(3))

    return (loss, grad_x, *[grads[n] for n in ALL_WEIGHTS], *[delta[n] for n in ALL_WEIGHTS],
            *[new_m[n] for n in ALL_WEIGHTS], *[new_v[n] for n in ALL_WEIGHTS])
```

```python
import jax
import jax.numpy as jnp
from jax import lax
from jax.experimental import pallas as pl
from jax.experimental.pallas import tpu as pltpu

F32 = jnp.float32
BF16 = jnp.bfloat16
MESH = pl.DeviceIdType.MESH
ANY = pl.BlockSpec(memory_space=pl.ANY)

D_MODEL = 1024
D_A = 512
D_B = 512
D_IN = 4608
D_FF = 2816
GMLP_BLOCK = 128
CHUNK = 64
A_HEADS = 4
N_LAYERS = 2
N_CHIPS = 4
RMS_EPS = 1e-6
LN_EPS = 1e-5
ADAM_LR = 0.001
ADAM_B1 = 0.9
ADAM_B2 = 0.999
ADAM_EPS = 1e-08
ADAM_WD = 0.01
ADAM_STEP = 10

C_U, C_V, C_BG, C_CG, C_HB, C_GA, C_GB = 0, 512, 1024, 1536, 2048, 2560, 3584

V7X_VMEM_LIMIT = 60 * 1024 * 1024
TM_MIX = 256
TM_FFN = 256
TK_WGRAD = 2048
SLOW_COPY_BYTES = 640 * 1024
FF_CHUNKS = ((0, 768), (768, 1536), (1536, 2304), (2304, 2816))
GELU_C0 = 0.7978845608028654
GELU_C1 = 0.044715

BIG = {
    "w_in": (1024, 1152),
    "w_branch": (1024, 256),
    "w_out": (256, 1024),
    "w_ffn_up": (1024, 1408),
    "w_ffn_down": (704, 1024),
}
BIG_NAMES = tuple(BIG)


def _params(sem=("arbitrary",), vmem=V7X_VMEM_LIMIT):
    return pltpu.CompilerParams(dimension_semantics=sem, vmem_limit_bytes=vmem)


def _gelu(x):
    x2 = x * x
    t = jnp.tanh(GELU_C0 * x * (1.0 + GELU_C1 * x2))
    return 0.5 * x * (1.0 + t), t


def _gelu_grad(x, t):
    return 0.5 * (1.0 + t) + 0.5 * x * (1.0 - t * t) * GELU_C0 * (1.0 + 3.0 * GELU_C1 * x * x)


def _colsum8(v):
    r, n = v.shape
    return v.reshape(r // 8, 8, n).sum(axis=0)


def _dot(a, b):
    return jnp.dot(a, b, preferred_element_type=F32)


def _dot_nt(a, b):
    return lax.dot_general(a, b, (((1,), (1,)), ((), ())), preferred_element_type=F32)


def _dot_tn(a, b):
    return lax.dot_general(a, b, (((0,), (0,)), ((), ())), preferred_element_type=F32)


def _shift_down(v, carry, n):
    rows = lax.broadcasted_iota(jnp.int32, (8, v.shape[1]), 0)
    out = pltpu.roll(v, n, 0)
    head = out[0:8, :]
    for r in range(n):
        head = jnp.where(rows == r, carry[8 - n + r:8 - n + r + 1, :], head)
    return jnp.concatenate([head, out[8:, :]], axis=0)


def _shift_up(v, carry, n):
    tm = v.shape[0]
    rows = lax.broadcasted_iota(jnp.int32, (8, v.shape[1]), 0)
    out = pltpu.roll(v, tm - n, 0)
    tail = out[tm - 8:tm, :]
    for r in range(n):
        tail = jnp.where(rows == 8 - n + r, carry[r:r + 1, :], tail)
    return jnp.concatenate([out[0:tm - 8, :], tail], axis=0)


def _sigmoid(x):
    return 0.5 * jnp.tanh(0.5 * x) + 0.5


def _start_all(copies):
    for cp in copies:
        cp.start()


def _wait_all(copies):
    for cp in copies:
        cp.wait()


def _load_col_sharded(src, dst, sems, first):
    cs = src.shape[-1]
    return [pltpu.make_async_copy(src.at[k], dst.at[:, k * cs:(k + 1) * cs], sems.at[first + k])
            for k in range(N_CHIPS)]


def _load_row_sharded(src, dst, sems, first):
    rs = src.shape[-2]
    return [pltpu.make_async_copy(src.at[k], dst.at[k * rs:(k + 1) * rs, :], sems.at[first + k])
            for k in range(N_CHIPS)]


def _load_branch(src, dst, sems, first):
    return [pltpu.make_async_copy(src.at[k, pl.ds(m * D_A, D_A), :], dst.at[m, :, k * 256:(k + 1) * 256],
                                  sems.at[first + 2 * k + m])
            for k in range(N_CHIPS) for m in range(2)]


def _row_spec(tm, n, rev=None):
    if rev is None:
        return pl.BlockSpec((tm, n), lambda i: (i, 0))
    return pl.BlockSpec((tm, n), lambda i: (rev - 1 - i, 0))


def _const_spec(shape):
    nd = len(shape)
    return pl.BlockSpec(shape, lambda i: (0,) * nd)


def _mesh_pos():
    return lax.axis_index("x"), lax.axis_index("y"), lax.axis_index("c")


def _other_chips(x, y):
    return [(1 - x, y, 2 * (1 - x) + y), (x, 1 - y, 2 * x + (1 - y)), (1 - x, 1 - y, 2 * (1 - x) + (1 - y))]


def _remote(src, dst, ssem, rsem, to):
    return pltpu.make_async_remote_copy(src_ref=src, dst_ref=dst, send_sem=ssem, recv_sem=rsem, device_id=to,
                                        device_id_type=MESH)


def _half(ref, which, h):
    start = pl.multiple_of(which * h, 8)
    if len(ref.shape) == 2:
        return ref.at[pl.ds(start, h), :]
    return ref.at[:, pl.ds(start, h), :]


class _Stage:
    def __init__(self, ins=(), inouts=(), outs=(), n_sems=0, start=None, mid=None, finish=None, then=None, slow=False):
        self.ins, self.inouts, self.outs = list(ins), list(inouts), list(outs)
        self.n_sems, self.start, self.mid, self.finish, self.then = n_sems, start, mid, finish, then
        self.slow = slow


def _gather_stage(bufs, then):
    n = len(bufs)

    def copies(io, sem):
        x, y, c = _mesh_pos()
        me = 2 * x + y
        ici, fwd, got = [], [], []
        for w in range(n):
            h = io[w].shape[1] // 2
            for j, (px, py, pk) in enumerate(_other_chips(x, y)):
                mine = _half(io[w].at[me], c, h)
                theirs = _half(io[w].at[pk], c, h)
                ici.append(_remote(mine, mine, sem(12 * w + j), sem(12 * w + 3 + j), (px, py, c)))
                got.append(_remote(theirs, theirs, sem(12 * w + j), sem(12 * w + 3 + j), (px, py, c)))
                fwd.append(_remote(theirs, theirs, sem(12 * w + 6 + j), sem(12 * w + 9 + j), (x, y, 1 - c)))
        return ici, got, fwd

    def start(ins, io, outs, sem):
        _start_all(copies(io, sem)[0])

    def mid(ins, io, outs, sem):
        _, got, fwd = copies(io, sem)
        for g, f in zip(got, fwd):
            g.wait_recv()
            f.start()

    def finish(ins, io, outs, sem):
        x, y, c = _mesh_pos()
        ici, _, fwd = copies(io, sem)
        for w in range(n):
            h = io[w].shape[1] // 2
            for j, (px, py, pk) in enumerate(_other_chips(x, y)):
                other = _half(io[w].at[pk], 1 - c, h)
                _remote(other, other, sem(12 * w + 6 + j), sem(12 * w + 9 + j), (x, y, 1 - c)).wait_recv()
        for cp in ici + fwd:
            cp.wait_send()

    return _Stage(inouts=bufs, n_sems=12 * n, start=start, mid=mid, finish=finish, then=then)


def _pair_send_stage(grad, then):
    h = grad.shape[1] // 2

    def copy(ins, outs, sem):
        x, y, c = _mesh_pos()
        return _remote(_half(ins[0], 1 - c, h), outs[0], sem(0), sem(1), (x, y, 1 - c))

    return _Stage(ins=[grad], outs=[jax.ShapeDtypeStruct((N_CHIPS, h, grad.shape[2]), F32)], n_sems=2,
                  start=lambda ins, io, outs, sem: copy(ins, outs, sem).start(),
                  finish=lambda ins, io, outs, sem: copy(ins, outs, sem).wait(), then=then)


def _chip_send_stage(psum, then):
    def copies(ins, outs, sem):
        x, y, c = _mesh_pos()
        return [_remote(ins[0].at[pk], outs[0].at[j], sem(j), sem(3 + j), (px, py, c))
                for j, (px, py, pk) in enumerate(_other_chips(x, y))]

    return _Stage(ins=[psum], outs=[jax.ShapeDtypeStruct((3,) + psum.shape[1:], BF16)], n_sems=6,
                  start=lambda ins, io, outs, sem: _start_all(copies(ins, outs, sem)),
                  finish=lambda ins, io, outs, sem: _wait_all(copies(ins, outs, sem)), then=then,
                  slow=psum.shape[1] * psum.shape[2] * 2 > SLOW_COPY_BYTES)


def _pair_fill_stage(final, then):
    h = final.shape[0] // 2

    def copy(io, sem):
        x, y, c = _mesh_pos()
        mine = _half(io[0], c, h)
        return _remote(mine, mine, sem(0), sem(1), (x, y, 1 - c))

    return _Stage(inouts=[final], n_sems=2,
                  start=lambda ins, io, outs, sem: copy(io, sem).start(),
                  finish=lambda ins, io, outs, sem: copy(io, sem).wait(), then=then)


def _pair_swap_stage(packed, then):
    def copy(ins, outs, sem):
        x, y, c = _mesh_pos()
        return _remote(ins[0], outs[0], sem(0), sem(1), (x, y, 1 - c))

    return _Stage(ins=[packed], outs=[jax.ShapeDtypeStruct(packed.shape, F32)], n_sems=2,
                  start=lambda ins, io, outs, sem: copy(ins, outs, sem).start(),
                  finish=lambda ins, io, outs, sem: copy(ins, outs, sem).wait(), then=then)


def _chip_spread_stage(psum, then):
    def copies(ins, outs, sem):
        x, y, c = _mesh_pos()
        me = 2 * x + y
        cps = [_remote(ins[0], outs[0].at[me], sem(j), sem(3 + j), (px, py, c))
               for j, (px, py, pk) in enumerate(_other_chips(x, y))]
        return cps, pltpu.make_async_copy(ins[0], outs[0].at[me], sem(6))

    def start(ins, io, outs, sem):
        cps, own = copies(ins, outs, sem)
        own.start()
        _start_all(cps)

    def finish(ins, io, outs, sem):
        cps, own = copies(ins, outs, sem)
        _wait_all(cps)
        own.wait()

    return _Stage(ins=[psum], outs=[jax.ShapeDtypeStruct((N_CHIPS,) + psum.shape, F32)], n_sems=7,
                  start=start, finish=finish, then=then)


def _staged_call(core, *, name, grid, in_specs, out_specs, out_shape, scratch_shapes, args, stages):
    n_in, n_out, n_scr = len(args), len(out_shape), len(scratch_shapes)
    s_args, s_outs, aliases, layout = [], [], {}, []
    n_sems = 0
    for st in stages:
        i0, o0 = len(s_args), len(s_outs)
        s_args += st.ins + st.inouts
        for q in range(len(st.inouts)):
            aliases[n_in + i0 + len(st.ins) + q] = n_out + o0 + q
        s_outs += [jax.ShapeDtypeStruct(a.shape, a.dtype) for a in st.inouts] + st.outs
        layout.append((i0, o0, n_sems))
        n_sems += st.n_sems
    steps = 1
    for g in grid:
        steps *= g

    def body(*refs):
        own_in = refs[:n_in]
        s_in = refs[n_in:n_in + len(s_args)]
        rest = refs[n_in + len(s_args):]
        own_out = rest[:n_out]
        s_out = rest[n_out:n_out + len(s_outs)]
        scr = rest[n_out + len(s_outs):]

        def run(which):
            for st, (i0, o0, s0) in zip(stages, layout):
                fn = getattr(st, which)
                if fn is not None:
                    fn(s_in[i0:i0 + len(st.ins)], s_out[o0:o0 + len(st.inouts)],
                       s_out[o0 + len(st.inouts):o0 + len(st.inouts) + len(st.outs)],
                       lambda k, s0=s0: scr[n_scr].at[s0 + k])

        if not stages:
            core(*own_in, *own_out, *scr[:n_scr])
            return
        step = 0
        for d, g in enumerate(grid):
            step = step * g + pl.program_id(d)
        if steps == 1:
            run("start")
            core(*own_in, *own_out, *scr[:n_scr])
            run("mid")
            run("finish")
            return
        pl.when(step == 0)(lambda: run("start"))
        core(*own_in, *own_out, *scr[:n_scr])
        pl.when(step == (3 * steps) // 4)(lambda: run("mid"))
        pl.when(step == steps - 1)(lambda: run("finish"))

    sem = ("arbitrary",) * len(grid) if stages else ("parallel",) * max(len(grid) - 1, 0) + ("arbitrary",) * min(len(grid), 1)
    res = pl.pallas_call(
        body, name=name, grid=grid,
        in_specs=list(in_specs) + [ANY] * len(s_args),
        out_specs=list(out_specs) + [ANY] * len(s_outs),
        out_shape=list(out_shape) + s_outs,
        input_output_aliases=aliases,
        scratch_shapes=list(scratch_shapes) + ([pltpu.SemaphoreType.DMA((n_sems,))] if stages else []),
        compiler_params=_params(sem) if grid else pltpu.CompilerParams(vmem_limit_bytes=V7X_VMEM_LIMIT),
    )(*args, *s_args)
    return list(res[:n_out]), list(res[n_out:])


class _Pipe:
    def __init__(self):
        self.ready = []
        self.flushes = 0
        self.after = None

    def add(self, stage):
        self.ready.append(stage)

    def carry(self, call, long=True):
        stages = [st for st in self.ready if long or not st.slow]
        self.ready = [st for st in self.ready if not (long or not st.slow)]
        own, outs = call(stages)
        k = 0
        for st in stages:
            n = len(st.inouts) + len(st.outs)
            st.then(*outs[k:k + n])
            k += n
        if self.after is not None:
            self.after()
        return own

    def flush(self):
        while self.ready:
            self.flushes += 1
            self.carry(lambda stages: _staged_call(
                lambda *refs: None, name=f"comm_tail_{self.flushes}", grid=(), in_specs=[], out_specs=[], out_shape=[],
                scratch_shapes=[], args=[], stages=stages))


def _mixer_fwd(layer, x, g1, bgate, lng, lnb, wm, bsf, wsc, win_g, wb_g, wout_g, stages):
    t_len = x.shape[0]
    tm = min(TM_MIX, t_len)
    nt = t_len // tm
    nb = tm // GMLP_BLOCK

    def core(x_ref, x_late_ref, g1_ref, bgate_ref, lng_ref, lnb_ref, wm_ref, bsf_ref, wsc_ref, win_hbm, wb_hbm, wout_hbm,
             zc_ref, ya_ref, yb_ref, q_ref, sa_ref, ca_ref, sb_ref, cb_ref, ug_ref, fu_ref, xh_ref, cv_ref,
             mg_ref, h_ref, x2_ref,
             win_v, wb_v, wout_v, carry, vn_s, f_s, z_s, sems):
        i = pl.program_id(0)

        @pl.when(i == 0)
        def _():
            cps = (_load_col_sharded(win_hbm, win_v, sems, 0) + _load_branch(wb_hbm, wb_v, sems, 4)
                   + _load_row_sharded(wout_hbm, wout_v, sems, 12))
            _start_all(cps)
            carry[...] = jnp.zeros_like(carry)
            z_s[...] = jnp.zeros_like(z_s)
            _wait_all(cps)

        xv = x_ref[...]
        r = lax.rsqrt(jnp.mean(xv * xv, axis=-1, keepdims=True) + RMS_EPS)
        h_ref[...] = (xv * r * g1_ref[...]).astype(BF16)

        def zcols(c0, n, keep=None):
            zv = z_s[:, c0:c0 + n]
            z_s[:, c0:c0 + n] = _dot(h_ref[...], win_v[:, c0:c0 + n])
            if keep is not None:
                zc_ref[:, keep * D_B:(keep + 1) * D_B] = zv.astype(BF16)
            return zv

        v = zcols(C_V, D_A)
        vg, tv = _gelu(v)
        mu = jnp.mean(vg, axis=-1, keepdims=True)
        vc = vg - mu
        rstd = lax.rsqrt(jnp.mean(vc * vc, axis=-1, keepdims=True) + LN_EPS)
        xh = vc * rstd
        xh_ref[...] = xh.astype(BF16)
        cv_ref[...] = (rstd * _gelu_grad(v, tv)).astype(BF16)
        vn_s[...] = (xh * lng_ref[...] + lnb_ref[...]).astype(BF16)
        for hd in range(A_HEADS):
            cols = slice(hd * 128, (hd + 1) * 128)
            vcat = jnp.concatenate([vn_s[b * 128:(b + 1) * 128, cols] for b in range(nb)], axis=1)
            fcat = _dot(wm_ref[hd], vcat)
            for b in range(nb):
                f_s[b * 128:(b + 1) * 128, cols] = fcat[:, b * 128:(b + 1) * 128]
        u = zcols(C_U, D_A)
        ug, tu = _gelu(u)
        ug_ref[...] = ug.astype(BF16)
        fb = f_s[...] + jnp.concatenate([bsf_ref[...]] * nb, axis=0)
        fu_ref[...] = (fb * _gelu_grad(u, tu)).astype(BF16)
        ya_ref[...] = (ug * fb).astype(BF16)

        p = zcols(C_CG, D_B, keep=1) * zcols(C_HB, D_B, keep=2)
        cr = carry[...]
        q = wsc_ref[0:1, :] * _shift_down(p, cr, 2) + wsc_ref[1:2, :] * _shift_down(p, cr, 1) + wsc_ref[2:3, :] * p
        carry[...] = p[tm - 8:tm, :]
        q_ref[...] = q.astype(BF16)
        yb_ref[...] = (zcols(C_BG, D_B, keep=0) * q).astype(BF16)

        av = _dot(ya_ref[...], wb_v[0])
        sa = _sigmoid(zcols(C_GA, D_MODEL) + bgate_ref[:, 0:D_MODEL])
        sa_ref[...] = sa.astype(BF16)
        mg = sa * av
        ca_ref[...] = (mg * (1.0 - sa)).astype(BF16)
        bv = _dot(yb_ref[...], wb_v[1])
        sb = _sigmoid(zcols(C_GB, D_MODEL) + bgate_ref[:, D_MODEL:2 * D_MODEL])
        sb_ref[...] = sb.astype(BF16)
        mb = sb * bv
        cb_ref[...] = (mb * (1.0 - sb)).astype(BF16)
        mg_ref[...] = (mg + mb).astype(BF16)
        x2_ref[...] = x_late_ref[...] + _dot(mg_ref[...], wout_v[...])

    def tile(n, lag):
        return pl.BlockSpec((tm, n), lambda i: (jnp.clip(i - lag, 0, nt - 1), 0))

    outs = [
        jax.ShapeDtypeStruct((t_len, 3 * D_B), BF16),
        jax.ShapeDtypeStruct((t_len, D_A), BF16),
        jax.ShapeDtypeStruct((t_len, D_B), BF16),
        jax.ShapeDtypeStruct((t_len, D_B), BF16),
        jax.ShapeDtypeStruct((t_len, D_MODEL), BF16),
        jax.ShapeDtypeStruct((t_len, D_MODEL), BF16),
        jax.ShapeDtypeStruct((t_len, D_MODEL), BF16),
        jax.ShapeDtypeStruct((t_len, D_MODEL), BF16),
        jax.ShapeDtypeStruct((t_len, D_A), BF16),
        jax.ShapeDtypeStruct((t_len, D_A), BF16),
        jax.ShapeDtypeStruct((t_len, D_A), BF16),
        jax.ShapeDtypeStruct((t_len, D_A), BF16),
        jax.ShapeDtypeStruct((t_len, D_MODEL), BF16),
        jax.ShapeDtypeStruct((t_len, D_MODEL), BF16),
        jax.ShapeDtypeStruct((t_len, D_MODEL), F32),
    ]
    return _staged_call(
        core, name=f"mixer_fwd_l{layer}", grid=(nt + 1,),
        in_specs=[tile(D_MODEL, 0), tile(D_MODEL, 1), _const_spec((1, D_MODEL)), _const_spec((1, 2 * D_MODEL)),
                  _const_spec((1, D_A)), _const_spec((1, D_A)), _const_spec((A_HEADS, 128, 128)),
                  _const_spec((128, D_A)), _const_spec((8, D_B)), ANY, ANY, ANY],
        out_specs=[tile(o.shape[1], 0 if k == len(outs) - 2 else 1) for k, o in enumerate(outs)],
        out_shape=outs,
        scratch_shapes=[pltpu.VMEM((D_MODEL, D_IN), BF16), pltpu.VMEM((2, D_A, D_MODEL), BF16),
                        pltpu.VMEM((D_MODEL, D_MODEL), BF16), pltpu.VMEM((8, D_B), F32),
                        pltpu.VMEM((tm, D_A), BF16), pltpu.VMEM((tm, D_A), F32), pltpu.VMEM((tm, D_IN), F32),
                        pltpu.SemaphoreType.DMA((16,))],
        args=[x, x, g1, bgate, lng, lnb, wm, bsf, wsc, win_g, wb_g, wout_g], stages=stages)


def _ffn_fwd(layer, x2, g2, wfc, bfc, wup_g, wdown_g, stages, head=None):
    t_len = x2.shape[0]
    tm = min(TM_FFN, t_len)
    nt = t_len // tm

    def core(*refs):
        if head is None:
            (x_ref, g2_ref, wfc_ref, bfc_ref, wup_hbm, wdown_hbm, up_ref, silu_ref, dsilu_ref, act_ref, h_ref, x3_ref,
             wup_v, wdown_v, carry, sems) = refs
        else:
            (x_ref, g2_ref, wfc_ref, bfc_ref, t_ref, gf_ref, wup_hbm, wdown_hbm, up_ref, silu_ref, dsilu_ref, act_ref,
             h_ref, dx_ref, dgf_ref, loss_ref, wup_v, wdown_v, carry, sems) = refs
        i = pl.program_id(0)

        @pl.when(i == 0)
        def _():
            cps = _load_col_sharded(wup_hbm, wup_v, sems, 0) + _load_row_sharded(wdown_hbm, wdown_v, sems, 4)
            _start_all(cps)
            carry[...] = jnp.zeros_like(carry)
            if head is not None:
                dgf_ref[...] = jnp.zeros_like(dgf_ref)
                loss_ref[...] = jnp.zeros_like(loss_ref)
            _wait_all(cps)

        xv = x_ref[...]
        r = lax.rsqrt(jnp.mean(xv * xv, axis=-1, keepdims=True) + RMS_EPS)
        h_ref[...] = (xv * r * g2_ref[...]).astype(BF16)
        gate = _dot(h_ref[...], wup_v[:, 0:D_FF])
        up_ref[:, 0:D_FF] = gate.astype(BF16)
        cr = carry[...]
        gc = (wfc_ref[0:1, :] * _shift_down(gate, cr, 2) + wfc_ref[1:2, :] * _shift_down(gate, cr, 1)
              + wfc_ref[2:3, :] * gate + bfc_ref[...])
        carry[...] = gate[tm - 8:tm, :]
        sg = _sigmoid(gc)
        silu = gc * sg
        silu_ref[...] = silu.astype(BF16)
        dsilu_ref[...] = (sg + silu * (1.0 - sg)).astype(BF16)
        val = _dot(h_ref[...], wup_v[:, D_FF:2 * D_FF])
        up_ref[:, D_FF:2 * D_FF] = val.astype(BF16)
        act_ref[...] = (silu * val).astype(BF16)
        x3 = x_ref[...] + _dot(act_ref[...], wdown_v[...])
        if head is None:
            x3_ref[...] = x3
        else:
            r3 = lax.rsqrt(jnp.mean(x3 * x3, axis=-1, keepdims=True) + RMS_EPS)
            xh = x3 * r3
            err = xh * gf_ref[...] - t_ref[...]
            loss_ref[...] += _colsum8(err * err)
            dy = err * (1.0 / D_MODEL)
            dgf_ref[...] += _colsum8(dy * xh)
            dxh = dy * gf_ref[...]
            dx_ref[...] = r3 * (dxh - xh * jnp.mean(dxh * xh, axis=-1, keepdims=True))

    outs = [
        jax.ShapeDtypeStruct((t_len, 2 * D_FF), BF16),
        jax.ShapeDtypeStruct((t_len, D_FF), BF16),
        jax.ShapeDtypeStruct((t_len, D_FF), BF16),
        jax.ShapeDtypeStruct((t_len, D_FF), BF16),
        jax.ShapeDtypeStruct((t_len, D_MODEL), BF16),
        jax.ShapeDtypeStruct((t_len, D_MODEL), F32),
    ]
    in_specs = [_row_spec(tm, D_MODEL), _const_spec((1, D_MODEL)), _const_spec((8, D_FF)), _const_spec((1, D_FF))]
    out_specs = [_row_spec(tm, o.shape[1]) for o in outs]
    args = [x2, g2, wfc, bfc]
    if head is not None:
        in_specs += [_row_spec(tm, D_MODEL), _const_spec((1, D_MODEL))]
        args += list(head)
        outs += [jax.ShapeDtypeStruct((8, D_MODEL), F32)] * 2
        out_specs += [_const_spec((8, D_MODEL))] * 2
    return _staged_call(
        core, name=f"ffn_fwd_l{layer}", grid=(nt,),
        in_specs=in_specs + [ANY, ANY], out_specs=out_specs, out_shape=outs,
        scratch_shapes=[pltpu.VMEM((D_MODEL, 2 * D_FF), BF16), pltpu.VMEM((D_FF, D_MODEL), BF16),
                        pltpu.VMEM((8, D_FF), F32), pltpu.SemaphoreType.DMA((8,))],
        args=args + [wup_g, wdown_g], stages=stages)


def _ffn_bwd(layer, dx3, x2, up, silu, dsilu, g2, wfc, wup_g, wdown_g, stages):
    t_len = x2.shape[0]
    tm = min(TM_FFN, t_len)
    nt = t_len // tm

    def core(dx3_ref, dx3_late_ref, x_ref, up_ref, silu_ref, dsilu_ref, g2_ref, wfc_ref, wup_hbm, wdown_hbm,
             dx2_ref, dup_ref, dx3b_ref, dg2_ref, dbfc_ref, dwfc_ref,
             wup_v, wdown_v, carry, da_s, dup_s, sems):
        i = pl.program_id(0)

        @pl.when(i == 0)
        def _():
            cps = _load_col_sharded(wup_hbm, wup_v, sems, 0) + _load_row_sharded(wdown_hbm, wdown_v, sems, 4)
            _start_all(cps)
            for ref in (carry, da_s, dup_s, dg2_ref, dbfc_ref, dwfc_ref):
                ref[...] = jnp.zeros_like(ref)
            _wait_all(cps)

        live = (i <= nt).astype(F32)
        dx3b_ref[...] = dx3_ref[...].astype(BF16)
        dh = jnp.zeros((tm, D_MODEL), F32)
        for c0, c1 in FF_CHUNKS:
            v0, v1 = D_FF + c0, D_FF + c1
            dh = dh + _dot_nt(dup_s[:, c0:c1], wup_v[:, c0:c1]) + _dot_nt(dup_s[:, v0:v1], wup_v[:, v0:v1])
            da = da_s[:, c0:c1]
            dval = (da * silu_ref[:, c0:c1].astype(F32)).astype(BF16)
            dup_ref[:, v0:v1] = dval
            dup_s[:, v0:v1] = dval
            dgc = da * up_ref[:, v0:v1].astype(F32) * dsilu_ref[:, c0:c1].astype(F32)
            cr = carry[:, c0:c1]
            dgc1 = _shift_up(dgc, cr, 1)
            dgc2 = _shift_up(dgc, cr, 2)
            carry[:, c0:c1] = jnp.where(i < nt, dgc[0:8, :], cr)
            gate = up_ref[:, c0:c1].astype(F32)
            dbfc_ref[:, c0:c1] += live * _colsum8(dgc)
            dwfc_ref[0, :, c0:c1] += live * _colsum8(dgc2 * gate)
            dwfc_ref[1, :, c0:c1] += live * _colsum8(dgc1 * gate)
            dwfc_ref[2, :, c0:c1] += live * _colsum8(dgc * gate)
            dgate = (wfc_ref[2:3, c0:c1] * dgc + wfc_ref[1:2, c0:c1] * dgc1 + wfc_ref[0:1, c0:c1] * dgc2).astype(BF16)
            dup_ref[:, c0:c1] = dgate
            dup_s[:, c0:c1] = dgate
            da_s[:, c0:c1] = _dot_nt(dx3b_ref[...], wdown_v[c0:c1, :])
        xv = x_ref[...]
        r = lax.rsqrt(jnp.mean(xv * xv, axis=-1, keepdims=True) + RMS_EPS)
        xh = xv * r
        dg2_ref[...] += _colsum8(dh * xh)
        dxh = dh * g2_ref[...]
        dx2_ref[...] = dx3_late_ref[...] + r * (dxh - xh * jnp.mean(dxh * xh, axis=-1, keepdims=True))

    def tile(n, lag):
        return pl.BlockSpec((tm, n), lambda i: (nt - 1 - jnp.clip(i - lag, 0, nt - 1), 0))

    outs = [
        jax.ShapeDtypeStruct((t_len, D_MODEL), F32),
        jax.ShapeDtypeStruct((t_len, 2 * D_FF), BF16),
        jax.ShapeDtypeStruct((t_len, D_MODEL), BF16),
        jax.ShapeDtypeStruct((8, D_MODEL), F32),
        jax.ShapeDtypeStruct((8, D_FF), F32),
        jax.ShapeDtypeStruct((3, 8, D_FF), F32),
    ]
    return _staged_call(
        core, name=f"ffn_bwd_l{layer}", grid=(nt + 2,),
        in_specs=[tile(D_MODEL, 0), tile(D_MODEL, 2), tile(D_MODEL, 2), tile(2 * D_FF, 1), tile(D_FF, 1), tile(D_FF, 1),
                  _const_spec((1, D_MODEL)), _const_spec((8, D_FF)), ANY, ANY],
        out_specs=[tile(D_MODEL, 2), tile(2 * D_FF, 1), tile(D_MODEL, 0),
                   _const_spec((8, D_MODEL)), _const_spec((8, D_FF)), _const_spec((3, 8, D_FF))],
        out_shape=outs,
        scratch_shapes=[pltpu.VMEM((D_MODEL, 2 * D_FF), BF16), pltpu.VMEM((D_FF, D_MODEL), BF16),
                        pltpu.VMEM((8, D_FF), F32), pltpu.VMEM((tm, D_FF), F32), pltpu.VMEM((tm, 2 * D_FF), BF16),
                        pltpu.SemaphoreType.DMA((8,))],
        args=[dx3, dx3, x2, up, silu, dsilu, g2, wfc, wup_g, wdown_g], stages=stages)


def _mixer_bwd(layer, dx2, x, zc, qs, sa, ca, sb, cb, ug, fu, xhs, cv, g1, lng, lnb, wmt, wsc, win_g, wb_g, wout_g,
               stages):
    t_len = x.shape[0]
    tm = min(TM_MIX, t_len)
    nt = t_len // tm
    nb = tm // GMLP_BLOCK

    def core(dx2_ref, x_ref, zc_ref, q_ref, sa_ref, ca_ref, sb_ref, cb_ref, ug_ref, fu_ref, xh_ref, cv_ref,
             g1_ref, lng_ref, lnb_ref, wmt_ref, wsc_ref, win_hbm, wb_hbm, wout_hbm,
             dx_ref, dz_ref, da_ref, db_ref, dx2b_ref, dg1_ref, dbgate_ref, dlng_ref, dlnb_ref, dwm_ref, dbsf_ref, dwsc_ref,
             win_v, wb_v, wout_v, carry, vn_s, df_s, dvn_s, sems):
        i = pl.program_id(0)

        @pl.when(i == 0)
        def _():
            cps = (_load_col_sharded(win_hbm, win_v, sems, 0) + _load_branch(wb_hbm, wb_v, sems, 4)
                   + _load_row_sharded(wout_hbm, wout_v, sems, 12))
            _start_all(cps)
            for ref in (carry, dg1_ref, dbgate_ref, dlng_ref, dlnb_ref, dwm_ref, dbsf_ref, dwsc_ref):
                ref[...] = jnp.zeros_like(ref)
            _wait_all(cps)

        def kept(k):
            return zc_ref[:, k * D_B:(k + 1) * D_B].astype(F32)

        def dz_cols(c0, n, val):
            dz_ref[:, c0:c0 + n] = val.astype(BF16)
            return _dot_nt(dz_ref[:, c0:c0 + n], win_v[:, c0:c0 + n])

        dx2b_ref[...] = dx2_ref[...].astype(BF16)
        dm = _dot_nt(dx2b_ref[...], wout_v[...])
        da_ref[...] = (dm * sa_ref[...].astype(F32)).astype(BF16)
        dga = dm * ca_ref[...].astype(F32)
        dh = dz_cols(C_GA, D_MODEL, dga)
        dbgate_ref[:, 0:D_MODEL] += _colsum8(dga)
        dya = _dot_nt(da_ref[...], wb_v[0])
        db_ref[...] = (dm * sb_ref[...].astype(F32)).astype(BF16)
        dgb = dm * cb_ref[...].astype(F32)
        dh = dh + dz_cols(C_GB, D_MODEL, dgb)
        dbgate_ref[:, D_MODEL:2 * D_MODEL] += _colsum8(dgb)
        dyb = _dot_nt(db_ref[...], wb_v[1])

        xh = xh_ref[...].astype(F32)
        vn_s[...] = (xh * lng_ref[...] + lnb_ref[...]).astype(BF16)
        df = dya * ug_ref[...].astype(F32)
        df_s[...] = df.astype(BF16)
        dbsf_acc = df[0:128, :]
        for b in range(1, nb):
            dbsf_acc = dbsf_acc + df[b * 128:(b + 1) * 128, :]
        dbsf_ref[...] += dbsf_acc
        for hd in range(A_HEADS):
            cols = slice(hd * 128, (hd + 1) * 128)
            vcat = jnp.concatenate([vn_s[b * 128:(b + 1) * 128, cols] for b in range(nb)], axis=1)
            dcat = jnp.concatenate([df_s[b * 128:(b + 1) * 128, cols] for b in range(nb)], axis=1)
            gcat = _dot(wmt_ref[hd], dcat)
            dwm_ref[hd] += _dot_nt(dcat, vcat)
            for b in range(nb):
                dvn_s[b * 128:(b + 1) * 128, cols] = gcat[:, b * 128:(b + 1) * 128]
        dh = dh + dz_cols(C_U, D_A, dya * fu_ref[...].astype(F32))
        dvn = dvn_s[...]
        dlng_ref[...] += _colsum8(dvn * xh)
        dlnb_ref[...] += _colsum8(dvn)
        dxh = dvn * lng_ref[...]
        dvc = dxh - jnp.mean(dxh, axis=-1, keepdims=True) - xh * jnp.mean(dxh * xh, axis=-1, keepdims=True)
        dh = dh + dz_cols(C_V, D_A, dvc * cv_ref[...].astype(F32))

        cg = kept(1)
        hbv = kept(2)
        p = cg * hbv
        dh = dh + dz_cols(C_BG, D_B, dyb * q_ref[...].astype(F32))
        dq = dyb * kept(0)
        cr = carry[...]
        dq1 = _shift_up(dq, cr, 1)
        dq2 = _shift_up(dq, cr, 2)
        carry[...] = dq[0:8, :]
        dwsc_ref[0] += _colsum8(dq2 * p)
        dwsc_ref[1] += _colsum8(dq1 * p)
        dwsc_ref[2] += _colsum8(dq * p)
        dp = wsc_ref[2:3, :] * dq + wsc_ref[1:2, :] * dq1 + wsc_ref[0:1, :] * dq2
        dh = dh + dz_cols(C_CG, D_B, dp * hbv)
        dh = dh + dz_cols(C_HB, D_B, dp * cg)

        xv = x_ref[...]
        r = lax.rsqrt(jnp.mean(xv * xv, axis=-1, keepdims=True) + RMS_EPS)
        xn = xv * r
        dg1_ref[...] += _colsum8(dh * xn)
        dxn = dh * g1_ref[...]
        dx_ref[...] = dx2_ref[...] + r * (dxn - xn * jnp.mean(dxn * xn, axis=-1, keepdims=True))

    outs = [
        jax.ShapeDtypeStruct((t_len, D_MODEL), F32),
        jax.ShapeDtypeStruct((t_len, D_IN), BF16),
        jax.ShapeDtypeStruct((t_len, D_MODEL), BF16),
        jax.ShapeDtypeStruct((t_len, D_MODEL), BF16),
        jax.ShapeDtypeStruct((t_len, D_MODEL), BF16),
        jax.ShapeDtypeStruct((8, D_MODEL), F32),
        jax.ShapeDtypeStruct((8, 2 * D_MODEL), F32),
        jax.ShapeDtypeStruct((8, D_A), F32),
        jax.ShapeDtypeStruct((8, D_A), F32),
        jax.ShapeDtypeStruct((A_HEADS, 128, 128), F32),
        jax.ShapeDtypeStruct((128, D_A), F32),
        jax.ShapeDtypeStruct((3, 8, D_B), F32),
    ]

    return _staged_call(
        core, name=f"mixer_bwd_l{layer}", grid=(nt,),
        in_specs=[_row_spec(tm, D_MODEL, nt), _row_spec(tm, D_MODEL, nt), _row_spec(tm, 3 * D_B, nt),
                  _row_spec(tm, D_B, nt), _row_spec(tm, D_MODEL, nt), _row_spec(tm, D_MODEL, nt),
                  _row_spec(tm, D_MODEL, nt), _row_spec(tm, D_MODEL, nt), _row_spec(tm, D_A, nt), _row_spec(tm, D_A, nt),
                  _row_spec(tm, D_A, nt), _row_spec(tm, D_A, nt),
                  _const_spec((1, D_MODEL)), _const_spec((1, D_A)), _const_spec((1, D_A)),
                  _const_spec((A_HEADS, 128, 128)), _const_spec((8, D_B)), ANY, ANY, ANY],
        out_specs=[_row_spec(tm, D_MODEL, nt), _row_spec(tm, D_IN, nt), _row_spec(tm, D_MODEL, nt),
                   _row_spec(tm, D_MODEL, nt), _row_spec(tm, D_MODEL, nt),
                   _const_spec((8, D_MODEL)), _const_spec((8, 2 * D_MODEL)), _const_spec((8, D_A)), _const_spec((8, D_A)),
                   _const_spec((A_HEADS, 128, 128)), _const_spec((128, D_A)), _const_spec((3, 8, D_B))],
        out_shape=outs,
        scratch_shapes=[pltpu.VMEM((D_MODEL, D_IN), BF16), pltpu.VMEM((2, D_A, D_MODEL), BF16),
                        pltpu.VMEM((D_MODEL, D_MODEL), BF16), pltpu.VMEM((8, D_B), F32),
                        pltpu.VMEM((tm, D_A), BF16), pltpu.VMEM((tm, D_A), BF16), pltpu.VMEM((tm, D_A), F32),
                        pltpu.SemaphoreType.DMA((16,))],
        args=[dx2, x, zc, qs, sa, ca, sb, cb, ug, fu, xhs, cv, g1, lng, lnb, wmt, wsc, win_g, wb_g, wout_g],
        stages=stages)


def _wgrad(name, layer, a, b, rows, cols, row_blk, col_blk, stages, a_first=0):
    t_len = a.shape[0]
    n = b.shape[1]
    tk = min(TK_WGRAD, t_len)
    col_sharded = n == N_CHIPS * cols
    m = rows if col_sharded else a.shape[1]
    grid = (m // row_blk, n // col_blk, t_len // tk)
    shards = col_blk // cols if col_sharded else 1

    if col_sharded:
        out_shape = (N_CHIPS, rows, cols)
        out_spec = pl.BlockSpec((shards, row_blk, cols), lambda i, j, k: (j, i, 0))
    else:
        out_shape = (N_CHIPS * rows, cols)
        out_spec = pl.BlockSpec((row_blk, col_blk), lambda i, j, k: (i, j))

    def core(a_ref, b_ref, o_ref):
        @pl.when(pl.program_id(2) == 0)
        def _():
            o_ref[...] = jnp.zeros_like(o_ref)

        g = _dot_tn(a_ref[...], b_ref[...])
        if col_sharded:
            for q in range(shards):
                o_ref[q] += g[:, q * cols:(q + 1) * cols]
        else:
            o_ref[...] += g

    own, outs = _staged_call(
        core, name=f"wgrad_{name}_l{layer}", grid=grid,
        in_specs=[pl.BlockSpec((tk, row_blk), lambda i, j, k: (k, a_first + i)),
                  pl.BlockSpec((tk, col_blk), lambda i, j, k: (k, j))],
        out_specs=[out_spec], out_shape=[jax.ShapeDtypeStruct(out_shape, F32)], scratch_shapes=[],
        args=[a, b], stages=stages)
    return [own[0].reshape(N_CHIPS, rows, cols)], outs


def _wgrad_branch(layer, ya, da, yb, db, stages):
    t_len = ya.shape[0]
    tk = min(TK_WGRAD, t_len)

    cs = D_MODEL // N_CHIPS

    def core(ya_ref, da_ref, yb_ref, db_ref, o_ref):
        @pl.when(pl.program_id(0) == 0)
        def _():
            o_ref[...] = jnp.zeros_like(o_ref)

        ga = _dot_tn(ya_ref[...], da_ref[...])
        gb = _dot_tn(yb_ref[...], db_ref[...])
        for k in range(N_CHIPS):
            o_ref[k, 0:D_A, :] += ga[:, k * cs:(k + 1) * cs]
            o_ref[k, D_A:2 * D_A, :] += gb[:, k * cs:(k + 1) * cs]

    a_spec = pl.BlockSpec((tk, D_A), lambda k: (k, 0))
    d_spec = pl.BlockSpec((tk, D_MODEL), lambda k: (k, 0))
    return _staged_call(
        core, name=f"wgrad_w_branch_l{layer}", grid=(t_len // tk,),
        in_specs=[a_spec, d_spec, a_spec, d_spec],
        out_specs=[pl.BlockSpec((N_CHIPS, 2 * D_A, cs), lambda k: (0, 0, 0))],
        out_shape=[jax.ShapeDtypeStruct((N_CHIPS, 2 * D_A, cs), F32)], scratch_shapes=[],
        args=[ya, da, yb, db], stages=stages)


def _flat_blk(rows, cols):
    blk = rows
    while blk * cols * 4 > 2 * 1024 * 1024 and blk % 16 == 0:
        blk //= 2
    return blk


def _cast_into_slots(name, jobs, stages):
    blks = [_flat_blk(w.shape[1], w.shape[2]) for w, _ in jobs]
    nblks = [w.shape[1] // b for (w, _), b in zip(jobs, blks)]
    n = len(jobs)

    def core(*refs):
        for w_ref, o_ref in zip(refs[:n], refs[n:]):
            o_ref[...] = w_ref[...].astype(BF16)

    def in_spec(w, la, blk, nblk):
        return pl.BlockSpec((None, blk, w.shape[2]), lambda i: (la, jnp.minimum(i, nblk - 1), 0))

    def out_spec(w, blk, nblk):
        return pl.BlockSpec((None, blk, w.shape[2]),
                            lambda i: (2 * lax.axis_index("x") + lax.axis_index("y"), jnp.minimum(i, nblk - 1), 0))

    return _staged_call(
        core, name=f"cast_{name}", grid=(max(nblks),),
        in_specs=[in_spec(w, la, b, k) for (w, la), b, k in zip(jobs, blks, nblks)],
        out_specs=[out_spec(w, b, k) for (w, _), b, k in zip(jobs, blks, nblks)],
        out_shape=[jax.ShapeDtypeStruct((N_CHIPS,) + w.shape[1:], BF16) for w, _ in jobs], scratch_shapes=[],
        args=[w for w, _ in jobs], stages=stages)


def _reduction_sums(name, jobs, pos):
    in_specs, out_specs, out_shape, args, bodies, counts = [], [], [], [], [], []
    for job in jobs:
        kind, grad, other = job[0], job[1], job[2]
        _, h, cols = other.shape
        blk = _flat_blk(h, cols)
        nblk = h // blk
        if kind == "pair":
            total = N_CHIPS * nblk

            def block(s, total=total, nblk=nblk):
                b = jnp.minimum(s, total - 1)
                return b // nblk, b % nblk

            spec = pl.BlockSpec((None, blk, cols), lambda s, p, block=block: (block(s)[0], block(s)[1], 0))
            in_specs += [pl.BlockSpec((None, blk, cols), lambda s, p, block=block, nblk=nblk:
                                      (block(s)[0], p[1] * nblk + block(s)[1], 0)), spec]
            out_specs.append(spec)
            out_shape.append(jax.ShapeDtypeStruct((N_CHIPS, h, cols), BF16))
            args += [grad, other]
            bodies.append((2, lambda g, o, out: out.__setitem__(..., (g[...] + o[...]).astype(BF16))))
        else:
            total = nblk

            def block(s, total=total):
                return jnp.minimum(s, total - 1)

            in_specs += [pl.BlockSpec((None, blk, cols), lambda s, p, block=block, nblk=nblk:
                                      (p[0], p[1] * nblk + block(s), 0)),
                         pl.BlockSpec((None, blk, cols), lambda s, p, block=block: (p[0], block(s), 0)),
                         pl.BlockSpec((3, blk, cols), lambda s, p, block=block: (0, block(s), 0))]
            out_specs.append(pl.BlockSpec((blk, cols), lambda s, p, block=block, nblk=nblk: (p[1] * nblk + block(s), 0)))
            out_shape.append(jax.ShapeDtypeStruct((2 * h, cols), F32))
            args += [grad, other, job[3]]
            bodies.append((3, lambda g, o, r, out: out.__setitem__(
                ..., (((g[...] + o[...]) + r[0].astype(F32)) + r[1].astype(F32)) + r[2].astype(F32))))
        counts.append(total)

    def body(pos_ref, *refs):
        ins, outs = refs[:len(args)], refs[len(args):]
        k = 0
        for (n_in, fn), out in zip(bodies, outs):
            fn(*ins[k:k + n_in], out)
            k += n_in

    return pl.pallas_call(
        body, name=f"reduction_sums_{name}",
        grid_spec=pltpu.PrefetchScalarGridSpec(num_scalar_prefetch=1, grid=(max(counts),), in_specs=in_specs,
                                               out_specs=out_specs),
        out_shape=out_shape,
        compiler_params=_params(),
    )(pos, *args)


def _sum_slots(name, slots):
    n, rows, _ = slots.shape

    def body(s_ref, o_ref):
        acc = s_ref[0]
        for d in range(1, n):
            acc = acc + s_ref[d]
        o_ref[...] = acc

    return pl.pallas_call(
        body, name=f"sum_slots_{name}", grid=(1,),
        in_specs=[pl.BlockSpec((n, rows, 128), lambda i: (0, 0, 0))],
        out_specs=pl.BlockSpec((rows, 128), lambda i: (0, 0)),
        out_shape=jax.ShapeDtypeStruct((rows, 128), F32),
        compiler_params=_params(),
    )(slots)


def _adamw_math(w, g, m, v):
    m2 = ADAM_B1 * m + (1.0 - ADAM_B1) * g
    v2 = ADAM_B2 * v + (1.0 - ADAM_B2) * (g * g)
    m_hat = m2 / (1.0 - ADAM_B1 ** ADAM_STEP)
    v_hat = v2 / (1.0 - ADAM_B2 ** ADAM_STEP)
    delta = -ADAM_LR * (m_hat / (jnp.sqrt(v_hat) + ADAM_EPS) + ADAM_WD * w)
    return delta, m2, v2


def _adamw_big(name, w, g0, g1, m, v):
    _, rows, cols = w.shape
    blk = _flat_blk(rows, cols) // 2

    def body(w_ref, g0_ref, g1_ref, m_ref, v_ref, g_ref, d_ref, m2_ref, v2_ref):
        g = jnp.where(pl.program_id(0) == 0, g0_ref[...], g1_ref[...])
        d, m2, v2 = _adamw_math(w_ref[...], g, m_ref[...], v_ref[...])
        g_ref[...] = g
        d_ref[...] = d
        m2_ref[...] = m2
        v2_ref[...] = v2

    spec = pl.BlockSpec((None, blk, cols), lambda la, i: (la, i, 0))
    return pl.pallas_call(
        body, name=f"adamw_{name}", grid=(N_LAYERS, rows // blk),
        in_specs=[spec, pl.BlockSpec((blk, cols), lambda la, i: (i * (1 - la), 0)),
                  pl.BlockSpec((blk, cols), lambda la, i: (i * la, 0)), spec, spec],
        out_specs=[spec] * 4,
        out_shape=[jax.ShapeDtypeStruct(w.shape, F32)] * 4,
        compiler_params=_params(("parallel", "parallel")),
    )(w, g0, g1, m, v)


def _adamw_small(ws, gs, ms, vs):
    n = len(ws)

    def body(*refs):
        ins, outs = refs[:4 * n], refs[4 * n:]
        for k in range(n):
            d, m2, v2 = _adamw_math(ins[k][...], ins[n + k][...], ins[2 * n + k][...], ins[3 * n + k][...])
            outs[k][...] = d
            outs[n + k][...] = m2
            outs[2 * n + k][...] = v2

    vmem = pl.BlockSpec(memory_space=pltpu.VMEM)
    return pl.pallas_call(
        body, name="adamw_small",
        in_specs=[vmem] * (4 * n), out_specs=[vmem] * (3 * n),
        out_shape=[jax.ShapeDtypeStruct(w.shape, F32) for w in ws] * 3,
        compiler_params=pltpu.CompilerParams(vmem_limit_bytes=V7X_VMEM_LIMIT),
    )(*ws, *gs, *ms, *vs)


SMALL = ("norm1_g", "b_gate", "gmlp_ln_g", "gmlp_ln_b", "w_spatial", "b_spatial", "w_shortconv", "norm2_g",
         "w_ffn_conv", "b_ffn_conv", "final_g")
ALL_WEIGHTS = ("norm1_g", "w_in", "b_gate", "gmlp_ln_g", "gmlp_ln_b", "w_spatial", "b_spatial", "w_shortconv",
               "w_branch", "w_out", "norm2_g", "w_ffn_up", "w_ffn_conv", "b_ffn_conv", "w_ffn_down", "final_g")


def _pack(arrays):
    flat = jnp.concatenate([a.reshape(-1) for a in arrays])
    n = flat.shape[0]
    rows = -(-n // 1024) * 8
    return jnp.pad(flat, (0, rows * 128 - n)).reshape(rows, 128)


def _unpack(packed, like):
    flat = packed.reshape(-1)
    out, off = [], 0
    for a in like:
        out.append(flat[off:off + a.size].reshape(a.shape))
        off += a.size
    return out


def _pad8(w):
    return jnp.pad(w, ((0, 5), (0, 0)))


def kernel(x, norm1_g, w_in, b_gate, gmlp_ln_g, gmlp_ln_b, w_spatial, b_spatial, w_shortconv, w_branch, w_out, norm2_g, w_ffn_up, w_ffn_conv, b_ffn_conv, w_ffn_down, final_g, loss_target, m_norm1_g, m_w_in, m_b_gate, m_gmlp_ln_g, m_gmlp_ln_b, m_w_spatial, m_b_spatial, m_w_shortconv, m_w_branch, m_w_out, m_norm2_g, m_w_ffn_up, m_w_ffn_conv, m_b_ffn_conv, m_w_ffn_down, m_final_g, v_norm1_g, v_w_in, v_b_gate, v_gmlp_ln_g, v_gmlp_ln_b, v_w_spatial, v_b_spatial, v_w_shortconv, v_w_branch, v_w_out, v_norm2_g, v_w_ffn_up, v_w_ffn_conv, v_b_ffn_conv, v_w_ffn_down, v_final_g):
    weights = dict(norm1_g=norm1_g, w_in=w_in, b_gate=b_gate, gmlp_ln_g=gmlp_ln_g, gmlp_ln_b=gmlp_ln_b,
                   w_spatial=w_spatial, b_spatial=b_spatial, w_shortconv=w_shortconv, w_branch=w_branch, w_out=w_out,
                   norm2_g=norm2_g, w_ffn_up=w_ffn_up, w_ffn_conv=w_ffn_conv, b_ffn_conv=b_ffn_conv,
                   w_ffn_down=w_ffn_down, final_g=final_g)
    mom = dict(norm1_g=m_norm1_g, w_in=m_w_in, b_gate=m_b_gate, gmlp_ln_g=m_gmlp_ln_g, gmlp_ln_b=m_gmlp_ln_b,
               w_spatial=m_w_spatial, b_spatial=m_b_spatial, w_shortconv=m_w_shortconv, w_branch=m_w_branch,
               w_out=m_w_out, norm2_g=m_norm2_g, w_ffn_up=m_w_ffn_up, w_ffn_conv=m_w_ffn_conv,
               b_ffn_conv=m_b_ffn_conv, w_ffn_down=m_w_ffn_down, final_g=m_final_g)
    vel = dict(norm1_g=v_norm1_g, w_in=v_w_in, b_gate=v_b_gate, gmlp_ln_g=v_gmlp_ln_g, gmlp_ln_b=v_gmlp_ln_b,
               w_spatial=v_w_spatial, b_spatial=v_b_spatial, w_shortconv=v_w_shortconv, w_branch=v_w_branch,
               w_out=v_w_out, norm2_g=v_norm2_g, w_ffn_up=v_w_ffn_up, w_ffn_conv=v_w_ffn_conv,
               b_ffn_conv=v_b_ffn_conv, w_ffn_down=v_w_ffn_down, final_g=v_final_g)

    cx, cy, cc = _mesh_pos()
    chip = 2 * cx + cy
    pos_arr = jnp.stack([chip, cc]).astype(jnp.int32)
    t_len = x.shape[1]
    xs = x.reshape(t_len, D_MODEL)
    target = loss_target.reshape(t_len, D_MODEL)
    pipe = _Pipe()

    full = {}

    mixer_w = ("w_in", "w_branch", "w_out")
    ffn_w = ("w_ffn_up", "w_ffn_down")
    slots = {}

    def cast(name, keys, stages):
        own, outs = _cast_into_slots(name, [(weights[n].reshape((N_LAYERS,) + BIG[n]), la) for n, la in keys], stages)
        slots.update(zip(keys, own))
        return own, outs

    def gather(names, la):
        def then(*bufs):
            full.update(zip([(n, la) for n in names], bufs))

        pipe.add(_gather_stage([slots[(n, la)] for n in names], then))

    first = [(n, 0) for n in mixer_w]
    cast("first", first, [])
    gather(mixer_w, 0)
    tap_slots = {}
    pipe.add(_chip_spread_stage(_pack([w_shortconv, w_ffn_conv]), lambda got: tap_slots.__setitem__("all", got)))
    pipe.carry(lambda st: cast("rest", [(n, la) for la in range(N_LAYERS) for n in BIG_NAMES if (n, la) not in first], st))
    by_chip = [_unpack(tap_slots["all"][k], [w_shortconv, w_ffn_conv]) for k in range(N_CHIPS)]
    wsc_full = jnp.concatenate([t[0] for t in by_chip], axis=-1)
    wfc_full = jnp.concatenate([t[1] for t in by_chip], axis=-1)

    idx = jnp.arange(GMLP_BLOCK) // CHUNK
    mask = idx[None, :] <= idx[:, None]
    wm_all = jnp.where(mask[None, None], w_spatial, 0.0)
    wm_bf = wm_all.astype(BF16)
    wmt_bf = jnp.swapaxes(wm_all, -1, -2).astype(BF16)
    bsf = jnp.repeat(jnp.swapaxes(b_spatial, -1, -2), 128, axis=-1)

    def row(a):
        return a.reshape(1, -1)

    def mixer_args(la):
        return (row(norm1_g[la]), row(b_gate[la]), row(gmlp_ln_g[la]), row(gmlp_ln_b[la]))

    def mixer_weights(la):
        return tuple(full[(n, la)] for n in mixer_w)

    def ffn_weights(la):
        return tuple(full[(n, la)] for n in ffn_w)

    saved = []
    h_in = xs
    for la in range(N_LAYERS):
        gather(ffn_w, la)
        *kept, mg, h1, x2 = pipe.carry(lambda st: _mixer_fwd(
            la, h_in, *mixer_args(la), wm_bf[la], bsf[la], _pad8(wsc_full[la]), *mixer_weights(la), st))
        ya, yb = kept[1], kept[2]
        if la + 1 < N_LAYERS:
            gather(mixer_w, la + 1)
        head = (target, row(final_g)) if la == N_LAYERS - 1 else None
        up, silu, dsilu, act, h2, *rest = pipe.carry(lambda st: _ffn_fwd(
            la, x2, row(norm2_g[la]), _pad8(wfc_full[la]), row(b_ffn_conv[la]), *ffn_weights(la), st, head=head))
        saved.append(dict(x=h_in, ya=ya, yb=yb, mixer=[kept[0]] + kept[3:], mg=mg, h1=h1, x2=x2, up=up, silu=silu,
                          dsilu=dsilu, act=act, h2=h2))
        h_in = rest[0]
    dx, dgf8, loss8 = rest

    reduced_big = {}

    sums_due = []

    def run_sums():
        if sums_due:
            due = list(sums_due)
            sums_due.clear()
            run_sums.calls += 1
            for (_, then), res in zip(due, _reduction_sums(str(run_sums.calls), [job for job, _ in due], pos_arr)):
                then(res)

    run_sums.calls = 0
    pipe.after = run_sums

    def reduce_big(name, la, grad):
        def after_pair(other):
            def after_chips(got):
                sums_due.append((("chip", grad, other, got), lambda final: pipe.add(_pair_fill_stage(
                    final, lambda done: reduced_big.__setitem__((name, la), done)))))

            sums_due.append((("pair", grad, other), lambda psum: pipe.add(_chip_send_stage(psum, after_chips))))

        pipe.add(_pair_send_stage(grad, after_pair))

    small = {n: [None] * N_LAYERS for n in SMALL}
    spread = {}
    for la in reversed(range(N_LAYERS)):
        s = saved[la]
        dx3 = dx
        dx2, dup, dx3b, dg2, dbfc, dwfc = pipe.carry(lambda st: _ffn_bwd(
            la, dx3, s["x2"], s["up"], s["silu"], s["dsilu"], row(norm2_g[la]), _pad8(wfc_full[la]),
            *ffn_weights(la), st))
        g, = pipe.carry(lambda st: _wgrad("w_ffn_up", la, s["h2"], dup, 1024, 1408, 512, 2816, st))
        reduce_big("w_ffn_up", la, g)
        g, = pipe.carry(lambda st: _wgrad("w_ffn_down", la, s["act"], dx3b, 704, 1024, 1408, 1024, st))
        reduce_big("w_ffn_down", la, g)
        run = pipe.carry if la > 0 else (lambda call: call([])[0])
        dxl, dz, da, db, dx2b, dg1, dbg, dlng, dlnb, dwm, dbsf, dwsc = run(lambda st: _mixer_bwd(
            la, dx2, s["x"], *s["mixer"], row(norm1_g[la]), row(gmlp_ln_g[la]), row(gmlp_ln_b[la]), wmt_bf[la],
            _pad8(wsc_full[la]), *mixer_weights(la), st))
        small["norm1_g"][la] = dg1.sum(0)
        small["b_gate"][la] = dbg.sum(0)
        small["gmlp_ln_g"][la] = dlng.sum(0)
        small["gmlp_ln_b"][la] = dlnb.sum(0)
        small["w_spatial"][la] = jnp.where(mask[None], dwm, 0.0)
        small["b_spatial"][la] = dbsf.reshape(128, A_HEADS, 128).sum(-1).T
        small["w_shortconv"][la] = dwsc.sum(1)
        small["norm2_g"][la] = dg2.sum(0)
        small["w_ffn_conv"][la] = dwfc.sum(1)
        small["b_ffn_conv"][la] = dbfc.sum(0)
        if la == 0:
            small_local = ([jnp.stack(small[n]) for n in SMALL[:-1]]
                           + [dgf8.sum(0), 0.5 * loss8.sum().reshape(1) / D_MODEL])
            mine = _pack(small_local)

            def after_swap(other, mine=mine):
                pair = _sum_slots("small_pair", jnp.stack([mine, other]))
                pipe.add(_chip_spread_stage(pair, lambda slots: spread.__setitem__("slots", slots)))

            pipe.add(_pair_swap_stage(mine, after_swap))
        if la > 0:
            g, = pipe.carry(lambda st: _wgrad("w_in", la, s["h1"], dz, 1024, 1152, 512, 2304, st))
            reduce_big("w_in", la, g)
        else:
            for part, tag in enumerate(("w_in_a", "w_in_b")):
                g, = pipe.carry(lambda st: _wgrad(tag, la, s["h1"], dz, 512, 1152, 512, 2304, st, a_first=part))
                reduce_big(tag, la, g)
        g, = pipe.carry(lambda st: _wgrad("w_out", la, s["mg"], dx2b, 256, 1024, 1024, 1024, st), long=False)
        reduce_big("w_out", la, g)
        g, = pipe.carry(lambda st: _wgrad_branch(la, s["ya"], da, s["yb"], db, st), long=False)
        reduce_big("w_branch", la, g)
        dx = dxl
    grad_x = dx.reshape(x.shape)
    pipe.flush()

    reduced_big[("w_in", 0)] = jnp.concatenate([reduced_big[("w_in_a", 0)], reduced_big[("w_in_b", 0)]], axis=0)
    reduced = _unpack(_sum_slots("small_grads", spread["slots"]), small_local)
    loss = reduced[-1].reshape(())
    grads = dict(zip(SMALL, reduced[:-1]))
    grads["w_shortconv"] = lax.dynamic_slice(grads["w_shortconv"], (0, 0, chip * (D_B // 4)), (N_LAYERS, 3, D_B // 4))
    grads["w_ffn_conv"] = lax.dynamic_slice(grads["w_ffn_conv"], (0, 0, chip * (D_FF // 4)), (N_LAYERS, 3, D_FF // 4))

    delta, new_m, new_v = {}, {}, {}
    for n in BIG_NAMES:
        shape3 = (N_LAYERS,) + BIG[n]
        res = _adamw_big(n, weights[n].reshape(shape3), reduced_big[(n, 0)], reduced_big[(n, 1)],
                         mom[n].reshape(shape3), vel[n].reshape(shape3))
        grads[n], delta[n], new_m[n], new_v[n] = (a.reshape(weights[n].shape) for a in res)
    res = _adamw_small(*[[src[n].reshape(-1, src[n].shape[-1]) for n in SMALL] for src in (weights, grads, mom, vel)])
    for k, n in enumerate(SMALL):
        delta[n], new_m[n], new_v[n] = (res[j * len(SMALL) + k].reshape(weights[n].shape) for j in range(3))

    return (loss, grad_x, *[grads[n] for n in ALL_WEIGHTS], *[delta[n] for n in ALL_WEIGHTS],
            *[new_m[n] for n in ALL_WEIGHTS], *[new_v[n] for n in ALL_WEIGHTS])
```

```python
import jax
import jax.numpy as jnp
from jax import lax
from jax.experimental import pallas as pl
from jax.experimental.pallas import tpu as pltpu

F32 = jnp.float32
BF16 = jnp.bfloat16
MESH = pl.DeviceIdType.MESH
ANY = pl.BlockSpec(memory_space=pl.ANY)

D_MODEL = 1024
D_A = 512
D_B = 512
D_IN = 4608
D_FF = 2816
GMLP_BLOCK = 128
CHUNK = 64
A_HEADS = 4
N_LAYERS = 2
N_CHIPS = 4
RMS_EPS = 1e-6
LN_EPS = 1e-5
ADAM_LR = 0.001
ADAM_B1 = 0.9
ADAM_B2 = 0.999
ADAM_EPS = 1e-08
ADAM_WD = 0.01
ADAM_STEP = 10

C_U, C_V, C_BG, C_CG, C_HB, C_GA, C_GB = 0, 512, 1024, 1536, 2048, 2560, 3584

V7X_VMEM_LIMIT = 60 * 1024 * 1024
TM_MIX = 256
TM_FFN = 256
TK_WGRAD = 2048
SLOW_COPY_BYTES = 640 * 1024
FF_CHUNKS = ((0, 768), (768, 1536), (1536, 2304), (2304, 2816))
GELU_C0 = 0.7978845608028654
GELU_C1 = 0.044715

BIG = {
    "w_in": (1024, 1152),
    "w_branch": (1024, 256),
    "w_out": (256, 1024),
    "w_ffn_up": (1024, 1408),
    "w_ffn_down": (704, 1024),
}
BIG_NAMES = tuple(BIG)


def _params(sem=("arbitrary",), vmem=V7X_VMEM_LIMIT):
    return pltpu.CompilerParams(dimension_semantics=sem, vmem_limit_bytes=vmem)


def _gelu(x):
    x2 = x * x
    t = jnp.tanh(GELU_C0 * x * (1.0 + GELU_C1 * x2))
    return 0.5 * x * (1.0 + t), t


def _gelu_grad(x, t):
    return 0.5 * (1.0 + t) + 0.5 * x * (1.0 - t * t) * GELU_C0 * (1.0 + 3.0 * GELU_C1 * x * x)


def _colsum8(v):
    r, n = v.shape
    return v.reshape(r // 8, 8, n).sum(axis=0)


def _dot(a, b):
    return jnp.dot(a, b, preferred_element_type=F32)


def _dot_nt(a, b):
    return lax.dot_general(a, b, (((1,), (1,)), ((), ())), preferred_element_type=F32)


def _dot_tn(a, b):
    return lax.dot_general(a, b, (((0,), (0,)), ((), ())), preferred_element_type=F32)


def _shift_down(v, carry, n):
    rows = lax.broadcasted_iota(jnp.int32, (8, v.shape[1]), 0)
    out = pltpu.roll(v, n, 0)
    head = out[0:8, :]
    for r in range(n):
        head = jnp.where(rows == r, carry[8 - n + r:8 - n + r + 1, :], head)
    return jnp.concatenate([head, out[8:, :]], axis=0)


def _shift_up(v, carry, n):
    tm = v.shape[0]
    rows = lax.broadcasted_iota(jnp.int32, (8, v.shape[1]), 0)
    out = pltpu.roll(v, tm - n, 0)
    tail = out[tm - 8:tm, :]
    for r in range(n):
        tail = jnp.where(rows == 8 - n + r, carry[r:r + 1, :], tail)
    return jnp.concatenate([out[0:tm - 8, :], tail], axis=0)


def _sigmoid(x):
    return 0.5 * jnp.tanh(0.5 * x) + 0.5


def _start_all(copies):
    for cp in copies:
        cp.start()


def _wait_all(copies):
    for cp in copies:
        cp.wait()


def _load_col_sharded(src, dst, sems, first):
    cs = src.shape[-1]
    return [pltpu.make_async_copy(src.at[k], dst.at[:, k * cs:(k + 1) * cs], sems.at[first + k])
            for k in range(N_CHIPS)]


def _load_row_sharded(src, dst, sems, first):
    rs = src.shape[-2]
    return [pltpu.make_async_copy(src.at[k], dst.at[k * rs:(k + 1) * rs, :], sems.at[first + k])
            for k in range(N_CHIPS)]


def _load_branch(src, dst, sems, first):
    return [pltpu.make_async_copy(src.at[k, pl.ds(m * D_A, D_A), :], dst.at[m, :, k * 256:(k + 1) * 256],
                                  sems.at[first + 2 * k + m])
            for k in range(N_CHIPS) for m in range(2)]


def _row_spec(tm, n, rev=None):
    if rev is None:
        return pl.BlockSpec((tm, n), lambda i: (i, 0))
    return pl.BlockSpec((tm, n), lambda i: (rev - 1 - i, 0))


def _const_spec(shape):
    nd = len(shape)
    return pl.BlockSpec(shape, lambda i: (0,) * nd)


def _mesh_pos():
    return lax.axis_index("x"), lax.axis_index("y"), lax.axis_index("c")


def _other_chips(x, y):
    return [(1 - x, y, 2 * (1 - x) + y), (x, 1 - y, 2 * x + (1 - y)), (1 - x, 1 - y, 2 * (1 - x) + (1 - y))]


def _remote(src, dst, ssem, rsem, to):
    return pltpu.make_async_remote_copy(src_ref=src, dst_ref=dst, send_sem=ssem, recv_sem=rsem, device_id=to,
                                        device_id_type=MESH)


def _half(ref, which, h):
    start = pl.multiple_of(which * h, 8)
    if len(ref.shape) == 2:
        return ref.at[pl.ds(start, h), :]
    return ref.at[:, pl.ds(start, h), :]


class _Stage:
    def __init__(self, ins=(), inouts=(), outs=(), n_sems=0, start=None, mid=None, finish=None, then=None, slow=False):
        self.ins, self.inouts, self.outs = list(ins), list(inouts), list(outs)
        self.n_sems, self.start, self.mid, self.finish, self.then = n_sems, start, mid, finish, then
        self.slow = slow


def _gather_stage(bufs, then):
    n = len(bufs)

    def copies(io, sem):
        x, y, c = _mesh_pos()
        me = 2 * x + y
        ici, fwd, got = [], [], []
        for w in range(n):
            h = io[w].shape[1] // 2
            for j, (px, py, pk) in enumerate(_other_chips(x, y)):
                mine = _half(io[w].at[me], c, h)
                theirs = _half(io[w].at[pk], c, h)
                ici.append(_remote(mine, mine, sem(12 * w + j), sem(12 * w + 3 + j), (px, py, c)))
                got.append(_remote(theirs, theirs, sem(12 * w + j), sem(12 * w + 3 + j), (px, py, c)))
                fwd.append(_remote(theirs, theirs, sem(12 * w + 6 + j), sem(12 * w + 9 + j), (x, y, 1 - c)))
        return ici, got, fwd

    def start(ins, io, outs, sem):
        _start_all(copies(io, sem)[0])

    def mid(ins, io, outs, sem):
        _, got, fwd = copies(io, sem)
        for g, f in zip(got, fwd):
            g.wait_recv()
            f.start()

    def finish(ins, io, outs, sem):
        x, y, c = _mesh_pos()
        ici, _, fwd = copies(io, sem)
        for w in range(n):
            h = io[w].shape[1] // 2
            for j, (px, py, pk) in enumerate(_other_chips(x, y)):
                other = _half(io[w].at[pk], 1 - c, h)
                _remote(other, other, sem(12 * w + 6 + j), sem(12 * w + 9 + j), (x, y, 1 - c)).wait_recv()
        for cp in ici + fwd:
            cp.wait_send()

    return _Stage(inouts=bufs, n_sems=12 * n, start=start, mid=mid, finish=finish, then=then)


def _pair_send_stage(grad, then):
    h = grad.shape[1] // 2

    def copy(ins, outs, sem):
        x, y, c = _mesh_pos()
        return _remote(_half(ins[0], 1 - c, h), outs[0], sem(0), sem(1), (x, y, 1 - c))

    return _Stage(ins=[grad], outs=[jax.ShapeDtypeStruct((N_CHIPS, h, grad.shape[2]), F32)], n_sems=2,
                  start=lambda ins, io, outs, sem: copy(ins, outs, sem).start(),
                  finish=lambda ins, io, outs, sem: copy(ins, outs, sem).wait(), then=then)


def _chip_send_stage(psum, then):
    def copies(ins, outs, sem):
        x, y, c = _mesh_pos()
        return [_remote(ins[0].at[pk], outs[0].at[j], sem(j), sem(3 + j), (px, py, c))
                for j, (px, py, pk) in enumerate(_other_chips(x, y))]

    return _Stage(ins=[psum], outs=[jax.ShapeDtypeStruct((3,) + psum.shape[1:], BF16)], n_sems=6,
                  start=lambda ins, io, outs, sem: _start_all(copies(ins, outs, sem)),
                  finish=lambda ins, io, outs, sem: _wait_all(copies(ins, outs, sem)), then=then,
                  slow=psum.shape[1] * psum.shape[2] * 2 > SLOW_COPY_BYTES)


def _pair_fill_stage(final, then):
    h = final.shape[0] // 2

    def copy(io, sem):
        x, y, c = _mesh_pos()
        mine = _half(io[0], c, h)
        return _remote(mine, mine, sem(0), sem(1), (x, y, 1 - c))

    return _Stage(inouts=[final], n_sems=2,
                  start=lambda ins, io, outs, sem: copy(io, sem).start(),
                  finish=lambda ins, io, outs, sem: copy(io, sem).wait(), then=then)


def _pair_swap_stage(packed, then):
    def copy(ins, outs, sem):
        x, y, c = _mesh_pos()
        return _remote(ins[0], outs[0], sem(0), sem(1), (x, y, 1 - c))

    return _Stage(ins=[packed], outs=[jax.ShapeDtypeStruct(packed.shape, F32)], n_sems=2,
                  start=lambda ins, io, outs, sem: copy(ins, outs, sem).start(),
                  finish=lambda ins, io, outs, sem: copy(ins, outs, sem).wait(), then=then)


def _chip_spread_stage(psum, then):
    def copies(ins, outs, sem):
        x, y, c = _mesh_pos()
        me = 2 * x + y
        cps = [_remote(ins[0], outs[0].at[me], sem(j), sem(3 + j), (px, py, c))
               for j, (px, py, pk) in enumerate(_other_chips(x, y))]
        return cps, pltpu.make_async_copy(ins[0], outs[0].at[me], sem(6))

    def start(ins, io, outs, sem):
        cps, own = copies(ins, outs, sem)
        own.start()
        _start_all(cps)

    def finish(ins, io, outs, sem):
        cps, own = copies(ins, outs, sem)
        _wait_all(cps)
        own.wait()

    return _Stage(ins=[psum], outs=[jax.ShapeDtypeStruct((N_CHIPS,) + psum.shape, F32)], n_sems=7,
                  start=start, finish=finish, then=then)


def _staged_call(core, *, name, grid, in_specs, out_specs, out_shape, scratch_shapes, args, stages):
    n_in, n_out, n_scr = len(args), len(out_shape), len(scratch_shapes)
    s_args, s_outs, aliases, layout = [], [], {}, []
    n_sems = 0
    for st in stages:
        i0, o0 = len(s_args), len(s_outs)
        s_args += st.ins + st.inouts
        for q in range(len(st.inouts)):
            aliases[n_in + i0 + len(st.ins) + q] = n_out + o0 + q
        s_outs += [jax.ShapeDtypeStruct(a.shape, a.dtype) for a in st.inouts] + st.outs
        layout.append((i0, o0, n_sems))
        n_sems += st.n_sems
    steps = 1
    for g in grid:
        steps *= g

    def body(*refs):
        own_in = refs[:n_in]
        s_in = refs[n_in:n_in + len(s_args)]
        rest = refs[n_in + len(s_args):]
        own_out = rest[:n_out]
        s_out = rest[n_out:n_out + len(s_outs)]
        scr = rest[n_out + len(s_outs):]

        def run(which):
            for st, (i0, o0, s0) in zip(stages, layout):
                fn = getattr(st, which)
                if fn is not None:
                    fn(s_in[i0:i0 + len(st.ins)], s_out[o0:o0 + len(st.inouts)],
                       s_out[o0 + len(st.inouts):o0 + len(st.inouts) + len(st.outs)],
                       lambda k, s0=s0: scr[n_scr].at[s0 + k])

        if not stages:
            core(*own_in, *own_out, *scr[:n_scr])
            return
        step = 0
        for d, g in enumerate(grid):
            step = step * g + pl.program_id(d)
        if steps == 1:
            run("start")
            core(*own_in, *own_out, *scr[:n_scr])
            run("mid")
            run("finish")
            return
        pl.when(step == 0)(lambda: run("start"))
        core(*own_in, *own_out, *scr[:n_scr])
        pl.when(step == (3 * steps) // 4)(lambda: run("mid"))
        pl.when(step == steps - 1)(lambda: run("finish"))

    sem = ("arbitrary",) * len(grid) if stages else ("parallel",) * max(len(grid) - 1, 0) + ("arbitrary",) * min(len(grid), 1)
    res = pl.pallas_call(
        body, name=name, grid=grid,
        in_specs=list(in_specs) + [ANY] * len(s_args),
        out_specs=list(out_specs) + [ANY] * len(s_outs),
        out_shape=list(out_shape) + s_outs,
        input_output_aliases=aliases,
        scratch_shapes=list(scratch_shapes) + ([pltpu.SemaphoreType.DMA((n_sems,))] if stages else []),
        compiler_params=_params(sem) if grid else pltpu.CompilerParams(vmem_limit_bytes=V7X_VMEM_LIMIT),
    )(*args, *s_args)
    return list(res[:n_out]), list(res[n_out:])


class _Pipe:
    def __init__(self):
        self.ready = []
        self.flushes = 0
        self.after = None

    def add(self, stage):
        self.ready.append(stage)

    def carry(self, call, long=True):
        stages = [st for st in self.ready if long or not st.slow]
        self.ready = [st for st in self.ready if not (long or not st.slow)]
        own, outs = call(stages)
        k = 0
        for st in stages:
            n = len(st.inouts) + len(st.outs)
            st.then(*outs[k:k + n])
            k += n
        if self.after is not None:
            self.after()
        return own

    def flush(self):
        while self.ready:
            self.flushes += 1
            self.carry(lambda stages: _staged_call(
                lambda *refs: None, name=f"comm_tail_{self.flushes}", grid=(), in_specs=[], out_specs=[], out_shape=[],
                scratch_shapes=[], args=[], stages=stages))


def _mixer_fwd(layer, x, g1, bgate, lng, lnb, wm, bsf, wsc, win_g, wb_g, wout_g, stages):
    t_len = x.shape[0]
    tm = min(TM_MIX, t_len)
    nt = t_len // tm
    nb = tm // GMLP_BLOCK

    def core(x_ref, x_late_ref, g1_ref, bgate_ref, lng_ref, lnb_ref, wm_ref, bsf_ref, wsc_ref, win_hbm, wb_hbm, wout_hbm,
             zc_ref, ya_ref, yb_ref, q_ref, sa_ref, ca_ref, sb_ref, cb_ref, ug_ref, fu_ref, xh_ref, cv_ref,
             mg_ref, h_ref, x2_ref,
             win_v, wb_v, wout_v, carry, vn_s, f_s, z_s, sems):
        i = pl.program_id(0)

        @pl.when(i == 0)
        def _():
            cps = (_load_col_sharded(win_hbm, win_v, sems, 0) + _load_branch(wb_hbm, wb_v, sems, 4)
                   + _load_row_sharded(wout_hbm, wout_v, sems, 12))
            _start_all(cps)
            carry[...] = jnp.zeros_like(carry)
            z_s[...] = jnp.zeros_like(z_s)
            _wait_all(cps)

        xv = x_ref[...]
        r = lax.rsqrt(jnp.mean(xv * xv, axis=-1, keepdims=True) + RMS_EPS)
        h_ref[...] = (xv * r * g1_ref[...]).astype(BF16)

        def zcols(c0, n, keep=None):
            zv = z_s[:, c0:c0 + n]
            z_s[:, c0:c0 + n] = _dot(h_ref[...], win_v[:, c0:c0 + n])
            if keep is not None:
                zc_ref[:, keep * D_B:(keep + 1) * D_B] = zv.astype(BF16)
            return zv

        v = zcols(C_V, D_A)
        vg, tv = _gelu(v)
        mu = jnp.mean(vg, axis=-1, keepdims=True)
        vc = vg - mu
        rstd = lax.rsqrt(jnp.mean(vc * vc, axis=-1, keepdims=True) + LN_EPS)
        xh = vc * rstd
        xh_ref[...] = xh.astype(BF16)
        cv_ref[...] = (rstd * _gelu_grad(v, tv)).astype(BF16)
        vn_s[...] = (xh * lng_ref[...] + lnb_ref[...]).astype(BF16)
        for hd in range(A_HEADS):
            cols = slice(hd * 128, (hd + 1) * 128)
            vcat = jnp.concatenate([vn_s[b * 128:(b + 1) * 128, cols] for b in range(nb)], axis=1)
            fcat = _dot(wm_ref[hd], vcat)
            for b in range(nb):
                f_s[b * 128:(b + 1) * 128, cols] = fcat[:, b * 128:(b + 1) * 128]
        u = zcols(C_U, D_A)
        ug, tu = _gelu(u)
        ug_ref[...] = ug.astype(BF16)
        fb = f_s[...] + jnp.concatenate([bsf_ref[...]] * nb, axis=0)
        fu_ref[...] = (fb * _gelu_grad(u, tu)).astype(BF16)
        ya_ref[...] = (ug * fb).astype(BF16)

        p = zcols(C_CG, D_B, keep=1) * zcols(C_HB, D_B, keep=2)
        cr = carry[...]
        q = wsc_ref[0:1, :] * _shift_down(p, cr, 2) + wsc_ref[1:2, :] * _shift_down(p, cr, 1) + wsc_ref[2:3, :] * p
        carry[...] = p[tm - 8:tm, :]
        q_ref[...] = q.astype(BF16)
        yb_ref[...] = (zcols(C_BG, D_B, keep=0) * q).astype(BF16)

        av = _dot(ya_ref[...], wb_v[0])
        sa = _sigmoid(zcols(C_GA, D_MODEL) + bgate_ref[:, 0:D_MODEL])
        sa_ref[...] = sa.astype(BF16)
        mg = sa * av
        ca_ref[...] = (mg * (1.0 - sa)).astype(BF16)
        bv = _dot(yb_ref[...], wb_v[1])
        sb = _sigmoid(zcols(C_GB, D_MODEL) + bgate_ref[:, D_MODEL:2 * D_MODEL])
        sb_ref[...] = sb.astype(BF16)
        mb = sb * bv
        cb_ref[...] = (mb * (1.0 - sb)).astype(BF16)
        mg_ref[...] = (mg + mb).astype(BF16)
        x2_ref[...] = x_late_ref[...] + _dot(mg_ref[...], wout_v[...])

    def tile(n, lag):
        return pl.BlockSpec((tm, n), lambda i: (jnp.clip(i - lag, 0, nt - 1), 0))

    outs = [
        jax.ShapeDtypeStruct((t_len, 3 * D_B), BF16),
        jax.ShapeDtypeStruct((t_len, D_A), BF16),
        jax.ShapeDtypeStruct((t_len, D_B), BF16),
        jax.ShapeDtypeStruct((t_len, D_B), BF16),
        jax.ShapeDtypeStruct((t_len, D_MODEL), BF16),
        jax.ShapeDtypeStruct((t_len, D_MODEL), BF16),
        jax.ShapeDtypeStruct((t_len, D_MODEL), BF16),
        jax.ShapeDtypeStruct((t_len, D_MODEL), BF16),
        jax.ShapeDtypeStruct((t_len, D_A), BF16),
        jax.ShapeDtypeStruct((t_len, D_A), BF16),
        jax.ShapeDtypeStruct((t_len, D_A), BF16),
        jax.ShapeDtypeStruct((t_len, D_A), BF16),
        jax.ShapeDtypeStruct((t_len, D_MODEL), BF16),
        jax.ShapeDtypeStruct((t_len, D_MODEL), BF16),
        jax.ShapeDtypeStruct((t_len, D_MODEL), F32),
    ]
    return _staged_call(
        core, name=f"mixer_fwd_l{layer}", grid=(nt + 1,),
        in_specs=[tile(D_MODEL, 0), tile(D_MODEL, 1), _const_spec((1, D_MODEL)), _const_spec((1, 2 * D_MODEL)),
                  _const_spec((1, D_A)), _const_spec((1, D_A)), _const_spec((A_HEADS, 128, 128)),
                  _const_spec((128, D_A)), _const_spec((8, D_B)), ANY, ANY, ANY],
        out_specs=[tile(o.shape[1], 0 if k == len(outs) - 2 else 1) for k, o in enumerate(outs)],
        out_shape=outs,
        scratch_shapes=[pltpu.VMEM((D_MODEL, D_IN), BF16), pltpu.VMEM((2, D_A, D_MODEL), BF16),
                        pltpu.VMEM((D_MODEL, D_MODEL), BF16), pltpu.VMEM((8, D_B), F32),
                        pltpu.VMEM((tm, D_A), BF16), pltpu.VMEM((tm, D_A), F32), pltpu.VMEM((tm, D_IN), F32),
                        pltpu.SemaphoreType.DMA((16,))],
        args=[x, x, g1, bgate, lng, lnb, wm, bsf, wsc, win_g, wb_g, wout_g], stages=stages)


def _ffn_fwd(layer, x2, g2, wfc, bfc, wup_g, wdown_g, stages, head=None):
    t_len = x2.shape[0]
    tm = min(TM_FFN, t_len)
    nt = t_len // tm

    def core(*refs):
        if head is None:
            (x_ref, g2_ref, wfc_ref, bfc_ref, wup_hbm, wdown_hbm, up_ref, silu_ref, dsilu_ref, act_ref, h_ref, x3_ref,
             wup_v, wdown_v, carry, sems) = refs
        else:
            (x_ref, g2_ref, wfc_ref, bfc_ref, t_ref, gf_ref, wup_hbm, wdown_hbm, up_ref, silu_ref, dsilu_ref, act_ref,
             h_ref, dx_ref, dgf_ref, loss_ref, wup_v, wdown_v, carry, sems) = refs
        i = pl.program_id(0)

        @pl.when(i == 0)
        def _():
            cps = _load_col_sharded(wup_hbm, wup_v, sems, 0) + _load_row_sharded(wdown_hbm, wdown_v, sems, 4)
            _start_all(cps)
            carry[...] = jnp.zeros_like(carry)
            if head is not None:
                dgf_ref[...] = jnp.zeros_like(dgf_ref)
                loss_ref[...] = jnp.zeros_like(loss_ref)
            _wait_all(cps)

        xv = x_ref[...]
        r = lax.rsqrt(jnp.mean(xv * xv, axis=-1, keepdims=True) + RMS_EPS)
        h_ref[...] = (xv * r * g2_ref[...]).astype(BF16)
        gate = _dot(h_ref[...], wup_v[:, 0:D_FF])
        up_ref[:, 0:D_FF] = gate.astype(BF16)
        cr = carry[...]
        gc = (wfc_ref[0:1, :] * _shift_down(gate, cr, 2) + wfc_ref[1:2, :] * _shift_down(gate, cr, 1)
              + wfc_ref[2:3, :] * gate + bfc_ref[...])
        carry[...] = gate[tm - 8:tm, :]
        sg = _sigmoid(gc)
        silu = gc * sg
        silu_ref[...] = silu.astype(BF16)
        dsilu_ref[...] = (sg + silu * (1.0 - sg)).astype(BF16)
        val = _dot(h_ref[...], wup_v[:, D_FF:2 * D_FF])
        up_ref[:, D_FF:2 * D_FF] = val.astype(BF16)
        act_ref[...] = (silu * val).astype(BF16)
        x3 = x_ref[...] + _dot(act_ref[...], wdown_v[...])
        if head is None:
            x3_ref[...] = x3
        else:
            r3 = lax.rsqrt(jnp.mean(x3 * x3, axis=-1, keepdims=True) + RMS_EPS)
            xh = x3 * r3
            err = xh * gf_ref[...] - t_ref[...]
            loss_ref[...] += _colsum8(err * err)
            dy = err * (1.0 / D_MODEL)
            dgf_ref[...] += _colsum8(dy * xh)
            dxh = dy * gf_ref[...]
            dx_ref[...] = r3 * (dxh - xh * jnp.mean(dxh * xh, axis=-1, keepdims=True))

    outs = [
        jax.ShapeDtypeStruct((t_len, 2 * D_FF), BF16),
        jax.ShapeDtypeStruct((t_len, D_FF), BF16),
        jax.ShapeDtypeStruct((t_len, D_FF), BF16),
        jax.ShapeDtypeStruct((t_len, D_FF), BF16),
        jax.ShapeDtypeStruct((t_len, D_MODEL), BF16),
        jax.ShapeDtypeStruct((t_len, D_MODEL), F32),
    ]
    in_specs = [_row_spec(tm, D_MODEL), _const_spec((1, D_MODEL)), _const_spec((8, D_FF)), _const_spec((1, D_FF))]
    out_specs = [_row_spec(tm, o.shape[1]) for o in outs]
    args = [x2, g2, wfc, bfc]
    if head is not None:
        in_specs += [_row_spec(tm, D_MODEL), _const_spec((1, D_MODEL))]
        args += list(head)
        outs += [jax.ShapeDtypeStruct((8, D_MODEL), F32)] * 2
        out_specs += [_const_spec((8, D_MODEL))] * 2
    return _staged_call(
        core, name=f"ffn_fwd_l{layer}", grid=(nt,),
        in_specs=in_specs + [ANY, ANY], out_specs=out_specs, out_shape=outs,
        scratch_shapes=[pltpu.VMEM((D_MODEL, 2 * D_FF), BF16), pltpu.VMEM((D_FF, D_MODEL), BF16),
                        pltpu.VMEM((8, D_FF), F32), pltpu.SemaphoreType.DMA((8,))],
        args=args + [wup_g, wdown_g], stages=stages)


def _ffn_bwd(layer, dx3, x2, up, silu, dsilu, g2, wfc, wup_g, wdown_g, stages):
    t_len = x2.shape[0]
    tm = min(TM_FFN, t_len)
    nt = t_len // tm

    def core(dx3_ref, dx3_late_ref, x_ref, up_ref, silu_ref, dsilu_ref, g2_ref, wfc_ref, wup_hbm, wdown_hbm,
             dx2_ref, dup_ref, dx3b_ref, dg2_ref, dbfc_ref, dwfc_ref,
             wup_v, wdown_v, carry, da_s, dup_s, sems):
        i = pl.program_id(0)

        @pl.when(i == 0)
        def _():
            cps = _load_col_sharded(wup_hbm, wup_v, sems, 0) + _load_row_sharded(wdown_hbm, wdown_v, sems, 4)
            _start_all(cps)
            for ref in (carry, da_s, dup_s, dg2_ref, dbfc_ref, dwfc_ref):
                ref[...] = jnp.zeros_like(ref)
            _wait_all(cps)

        live = (i <= nt).astype(F32)
        dx3b_ref[...] = dx3_ref[...].astype(BF16)
        dh = jnp.zeros((tm, D_MODEL), F32)
        for c0, c1 in FF_CHUNKS:
            v0, v1 = D_FF + c0, D_FF + c1
            dh = dh + _dot_nt(dup_s[:, c0:c1], wup_v[:, c0:c1]) + _dot_nt(dup_s[:, v0:v1], wup_v[:, v0:v1])
            da = da_s[:, c0:c1]
            dval = (da * silu_ref[:, c0:c1].astype(F32)).astype(BF16)
            dup_ref[:, v0:v1] = dval
            dup_s[:, v0:v1] = dval
            dgc = da * up_ref[:, v0:v1].astype(F32) * dsilu_ref[:, c0:c1].astype(F32)
            cr = carry[:, c0:c1]
            dgc1 = _shift_up(dgc, cr, 1)
            dgc2 = _shift_up(dgc, cr, 2)
            carry[:, c0:c1] = jnp.where(i < nt, dgc[0:8, :], cr)
            gate = up_ref[:, c0:c1].astype(F32)
            dbfc_ref[:, c0:c1] += live * _colsum8(dgc)
            dwfc_ref[0, :, c0:c1] += live * _colsum8(dgc2 * gate)
            dwfc_ref[1, :, c0:c1] += live * _colsum8(dgc1 * gate)
            dwfc_ref[2, :, c0:c1] += live * _colsum8(dgc * gate)
            dgate = (wfc_ref[2:3, c0:c1] * dgc + wfc_ref[1:2, c0:c1] * dgc1 + wfc_ref[0:1, c0:c1] * dgc2).astype(BF16)
            dup_ref[:, c0:c1] = dgate
            dup_s[:, c0:c1] = dgate
            da_s[:, c0:c1] = _dot_nt(dx3b_ref[...], wdown_v[c0:c1, :])
        xv = x_ref[...]
        r = lax.rsqrt(jnp.mean(xv * xv, axis=-1, keepdims=True) + RMS_EPS)
        xh = xv * r
        dg2_ref[...] += _colsum8(dh * xh)
        dxh = dh * g2_ref[...]
        dx2_ref[...] = dx3_late_ref[...] + r * (dxh - xh * jnp.mean(dxh * xh, axis=-1, keepdims=True))

    def tile(n, lag):
        return pl.BlockSpec((tm, n), lambda i: (nt - 1 - jnp.clip(i - lag, 0, nt - 1), 0))

    outs = [
        jax.ShapeDtypeStruct((t_len, D_MODEL), F32),
        jax.ShapeDtypeStruct((t_len, 2 * D_FF), BF16),
        jax.ShapeDtypeStruct((t_len, D_MODEL), BF16),
        jax.ShapeDtypeStruct((8, D_MODEL), F32),
        jax.ShapeDtypeStruct((8, D_FF), F32),
        jax.ShapeDtypeStruct((3, 8, D_FF), F32),
    ]
    return _staged_call(
        core, name=f"ffn_bwd_l{layer}", grid=(nt + 2,),
        in_specs=[tile(D_MODEL, 0), tile(D_MODEL, 2), tile(D_MODEL, 2), tile(2 * D_FF, 1), tile(D_FF, 1), tile(D_FF, 1),
                  _const_spec((1, D_MODEL)), _const_spec((8, D_FF)), ANY, ANY],
        out_specs=[tile(D_MODEL, 2), tile(2 * D_FF, 1), tile(D_MODEL, 0),
                   _const_spec((8, D_MODEL)), _const_spec((8, D_FF)), _const_spec((3, 8, D_FF))],
        out_shape=outs,
        scratch_shapes=[pltpu.VMEM((D_MODEL, 2 * D_FF), BF16), pltpu.VMEM((D_FF, D_MODEL), BF16),
                        pltpu.VMEM((8, D_FF), F32), pltpu.VMEM((tm, D_FF), F32), pltpu.VMEM((tm, 2 * D_FF), BF16),
                        pltpu.SemaphoreType.DMA((8,))],
        args=[dx3, dx3, x2, up, silu, dsilu, g2, wfc, wup_g, wdown_g], stages=stages)


def _mixer_bwd(layer, dx2, x, zc, qs, sa, ca, sb, cb, ug, fu, xhs, cv, g1, lng, lnb, wmt, wsc, win_g, wb_g, wout_g,
               stages):
    t_len = x.shape[0]
    tm = min(TM_MIX, t_len)
    nt = t_len // tm
    nb = tm // GMLP_BLOCK

    def core(dx2_ref, x_ref, zc_ref, q_ref, sa_ref, ca_ref, sb_ref, cb_ref, ug_ref, fu_ref, xh_ref, cv_ref,
             g1_ref, lng_ref, lnb_ref, wmt_ref, wsc_ref, win_hbm, wb_hbm, wout_hbm,
             dx_ref, dz_ref, da_ref, db_ref, dx2b_ref, dg1_ref, dbgate_ref, dlng_ref, dlnb_ref, dwm_ref, dbsf_ref, dwsc_ref,
             win_v, wb_v, wout_v, carry, vn_s, df_s, dvn_s, sems):
        i = pl.program_id(0)

        @pl.when(i == 0)
        def _():
            cps = (_load_col_sharded(win_hbm, win_v, sems, 0) + _load_branch(wb_hbm, wb_v, sems, 4)
                   + _load_row_sharded(wout_hbm, wout_v, sems, 12))
            _start_all(cps)
            for ref in (carry, dg1_ref, dbgate_ref, dlng_ref, dlnb_ref, dwm_ref, dbsf_ref, dwsc_ref):
                ref[...] = jnp.zeros_like(ref)
            _wait_all(cps)

        def kept(k):
            return zc_ref[:, k * D_B:(k + 1) * D_B].astype(F32)

        def dz_cols(c0, n, val):
            dz_ref[:, c0:c0 + n] = val.astype(BF16)
            return _dot_nt(dz_ref[:, c0:c0 + n], win_v[:, c0:c0 + n])

        dx2b_ref[...] = dx2_ref[...].astype(BF16)
        dm = _dot_nt(dx2b_ref[...], wout_v[...])
        da_ref[...] = (dm * sa_ref[...].astype(F32)).astype(BF16)
        dga = dm * ca_ref[...].astype(F32)
        dh = dz_cols(C_GA, D_MODEL, dga)
        dbgate_ref[:, 0:D_MODEL] += _colsum8(dga)
        dya = _dot_nt(da_ref[...], wb_v[0])
        db_ref[...] = (dm * sb_ref[...].astype(F32)).astype(BF16)
        dgb = dm * cb_ref[...].astype(F32)
        dh = dh + dz_cols(C_GB, D_MODEL, dgb)
        dbgate_ref[:, D_MODEL:2 * D_MODEL] += _colsum8(dgb)
        dyb = _dot_nt(db_ref[...], wb_v[1])

        xh = xh_ref[...].astype(F32)
        vn_s[...] = (xh * lng_ref[...] + lnb_ref[...]).astype(BF16)
        df = dya * ug_ref[...].astype(F32)
        df_s[...] = df.astype(BF16)
        dbsf_acc = df[0:128, :]
        for b in range(1, nb):
            dbsf_acc = dbsf_acc + df[b * 128:(b + 1) * 128, :]
        dbsf_ref[...] += dbsf_acc
        for hd in range(A_HEADS):
            cols = slice(hd * 128, (hd + 1) * 128)
            vcat = jnp.concatenate([vn_s[b * 128:(b + 1) * 128, cols] for b in range(nb)], axis=1)
            dcat = jnp.concatenate([df_s[b * 128:(b + 1) * 128, cols] for b in range(nb)], axis=1)
            gcat = _dot(wmt_ref[hd], dcat)
            dwm_ref[hd] += _dot_nt(dcat, vcat)
            for b in range(nb):
                dvn_s[b * 128:(b + 1) * 128, cols] = gcat[:, b * 128:(b + 1) * 128]
        dh = dh + dz_cols(C_U, D_A, dya * fu_ref[...].astype(F32))
        dvn = dvn_s[...]
        dlng_ref[...] += _colsum8(dvn * xh)
        dlnb_ref[...] += _colsum8(dvn)
        dxh = dvn * lng_ref[...]
        dvc = dxh - jnp.mean(dxh, axis=-1, keepdims=True) - xh * jnp.mean(dxh * xh, axis=-1, keepdims=True)
        dh = dh + dz_cols(C_V, D_A, dvc * cv_ref[...].astype(F32))

        cg = kept(1)
        hbv = kept(2)
        p = cg * hbv
        dh = dh + dz_cols(C_BG, D_B, dyb * q_ref[...].astype(F32))
        dq = dyb * kept(0)
        cr = carry[...]
        dq1 = _shift_up(dq, cr, 1)
        dq2 = _shift_up(dq, cr, 2)
        carry[...] = dq[0:8, :]
        dwsc_ref[0] += _colsum8(dq2 * p)
        dwsc_ref[1] += _colsum8(dq1 * p)
        dwsc_ref[2] += _colsum8(dq * p)
        dp = wsc_ref[2:3, :] * dq + wsc_ref[1:2, :] * dq1 + wsc_ref[0:1, :] * dq2
        dh = dh + dz_cols(C_CG, D_B, dp * hbv)
        dh = dh + dz_cols(C_HB, D_B, dp * cg)

        xv = x_ref[...]
        r = lax.rsqrt(jnp.mean(xv * xv, axis=-1, keepdims=True) + RMS_EPS)
        xn = xv * r
        dg1_ref[...] += _colsum8(dh * xn)
        dxn = dh * g1_ref[...]
        dx_ref[...] = dx2_ref[...] + r * (dxn - xn * jnp.mean(dxn * xn, axis=-1, keepdims=True))

    outs = [
        jax.ShapeDtypeStruct((t_len, D_MODEL), F32),
        jax.ShapeDtypeStruct((t_len, D_IN), BF16),
        jax.ShapeDtypeStruct((t_len, D_MODEL), BF16),
        jax.ShapeDtypeStruct((t_len, D_MODEL), BF16),
        jax.ShapeDtypeStruct((t_len, D_MODEL), BF16),
        jax.ShapeDtypeStruct((8, D_MODEL), F32),
        jax.ShapeDtypeStruct((8, 2 * D_MODEL), F32),
        jax.ShapeDtypeStruct((8, D_A), F32),
        jax.ShapeDtypeStruct((8, D_A), F32),
        jax.ShapeDtypeStruct((A_HEADS, 128, 128), F32),
        jax.ShapeDtypeStruct((128, D_A), F32),
        jax.ShapeDtypeStruct((3, 8, D_B), F32),
    ]

    return _staged_call(
        core, name=f"mixer_bwd_l{layer}", grid=(nt,),
        in_specs=[_row_spec(tm, D_MODEL, nt), _row_spec(tm, D_MODEL, nt), _row_spec(tm, 3 * D_B, nt),
                  _row_spec(tm, D_B, nt), _row_spec(tm, D_MODEL, nt), _row_spec(tm, D_MODEL, nt),
                  _row_spec(tm, D_MODEL, nt), _row_spec(tm, D_MODEL, nt), _row_spec(tm, D_A, nt), _row_spec(tm, D_A, nt),
                  _row_spec(tm, D_A, nt), _row_spec(tm, D_A, nt),
                  _const_spec((1, D_MODEL)), _const_spec((1, D_A)), _const_spec((1, D_A)),
                  _const_spec((A_HEADS, 128, 128)), _const_spec((8, D_B)), ANY, ANY, ANY],
        out_specs=[_row_spec(tm, D_MODEL, nt), _row_spec(tm, D_IN, nt), _row_spec(tm, D_MODEL, nt),
                   _row_spec(tm, D_MODEL, nt), _row_spec(tm, D_MODEL, nt),
                   _const_spec((8, D_MODEL)), _const_spec((8, 2 * D_MODEL)), _const_spec((8, D_A)), _const_spec((8, D_A)),
                   _const_spec((A_HEADS, 128, 128)), _const_spec((128, D_A)), _const_spec((3, 8, D_B))],
        out_shape=outs,
        scratch_shapes=[pltpu.VMEM((D_MODEL, D_IN), BF16), pltpu.VMEM((2, D_A, D_MODEL), BF16),
                        pltpu.VMEM((D_MODEL, D_MODEL), BF16), pltpu.VMEM((8, D_B), F32),
                        pltpu.VMEM((tm, D_A), BF16), pltpu.VMEM((tm, D_A), BF16), pltpu.VMEM((tm, D_A), F32),
                        pltpu.SemaphoreType.DMA((16,))],
        args=[dx2, x, zc, qs, sa, ca, sb, cb, ug, fu, xhs, cv, g1, lng, lnb, wmt, wsc, win_g, wb_g, wout_g],
        stages=stages)


def _wgrad(name, layer, a, b, rows, cols, row_blk, col_blk, stages, a_first=0):
    t_len = a.shape[0]
    n = b.shape[1]
    tk = min(TK_WGRAD, t_len)
    col_sharded = n == N_CHIPS * cols
    m = rows if col_sharded else a.shape[1]
    grid = (m // row_blk, n // col_blk, t_len // tk)
    shards = col_blk // cols if col_sharded else 1

    if col_sharded:
        out_shape = (N_CHIPS, rows, cols)
        out_spec = pl.BlockSpec((shards, row_blk, cols), lambda i, j, k: (j, i, 0))
    else:
        out_shape = (N_CHIPS * rows, cols)
        out_spec = pl.BlockSpec((row_blk, col_blk), lambda i, j, k: (i, j))

    def core(a_ref, b_ref, o_ref):
        @pl.when(pl.program_id(2) == 0)
        def _():
            o_ref[...] = jnp.zeros_like(o_ref)

        g = _dot_tn(a_ref[...], b_ref[...])
        if col_sharded:
            for q in range(shards):
                o_ref[q] += g[:, q * cols:(q + 1) * cols]
        else:
            o_ref[...] += g

    own, outs = _staged_call(
        core, name=f"wgrad_{name}_l{layer}", grid=grid,
        in_specs=[pl.BlockSpec((tk, row_blk), lambda i, j, k: (k, a_first + i)),
                  pl.BlockSpec((tk, col_blk), lambda i, j, k: (k, j))],
        out_specs=[out_spec], out_shape=[jax.ShapeDtypeStruct(out_shape, F32)], scratch_shapes=[],
        args=[a, b], stages=stages)
    return [own[0].reshape(N_CHIPS, rows, cols)], outs


def _wgrad_branch(layer, ya, da, yb, db, stages):
    t_len = ya.shape[0]
    tk = min(TK_WGRAD, t_len)

    cs = D_MODEL // N_CHIPS

    def core(ya_ref, da_ref, yb_ref, db_ref, o_ref):
        @pl.when(pl.program_id(0) == 0)
        def _():
            o_ref[...] = jnp.zeros_like(o_ref)

        ga = _dot_tn(ya_ref[...], da_ref[...])
        gb = _dot_tn(yb_ref[...], db_ref[...])
        for k in range(N_CHIPS):
            o_ref[k, 0:D_A, :] += ga[:, k * cs:(k + 1) * cs]
            o_ref[k, D_A:2 * D_A, :] += gb[:, k * cs:(k + 1) * cs]

    a_spec = pl.BlockSpec((tk, D_A), lambda k: (k, 0))
    d_spec = pl.BlockSpec((tk, D_MODEL), lambda k: (k, 0))
    return _staged_call(
        core, name=f"wgrad_w_branch_l{layer}", grid=(t_len // tk,),
        in_specs=[a_spec, d_spec, a_spec, d_spec],
        out_specs=[pl.BlockSpec((N_CHIPS, 2 * D_A, cs), lambda k: (0, 0, 0))],
        out_shape=[jax.ShapeDtypeStruct((N_CHIPS, 2 * D_A, cs), F32)], scratch_shapes=[],
        args=[ya, da, yb, db], stages=stages)


def _flat_blk(rows, cols):
    blk = rows
    while blk * cols * 4 > 2 * 1024 * 1024 and blk % 16 == 0:
        blk //= 2
    return blk


def _cast_into_slots(name, jobs, chip, stages):
    blks = [_flat_blk(w.shape[1], w.shape[2]) for w, _ in jobs]
    nblks = [w.shape[1] // b for (w, _), b in zip(jobs, blks)]
    n = len(jobs)
    out_shape = [jax.ShapeDtypeStruct((N_CHIPS,) + w.shape[1:], BF16) for w, _ in jobs]

    def core(*refs):
        for w_ref, o_ref in zip(refs[-2 * n:-n], refs[-n:]):
            o_ref[...] = w_ref[...].astype(BF16)

    def slot(*scalars):
        return scalars[0][0] if scalars else 2 * lax.axis_index("x") + lax.axis_index("y")

    in_specs = [pl.BlockSpec((None, b, w.shape[2]), lambda i, *s, la=la, k=k: (la, jnp.minimum(i, k - 1), 0))
                for (w, la), b, k in zip(jobs, blks, nblks)]
    out_specs = [pl.BlockSpec((None, b, w.shape[2]), lambda i, *s, k=k: (slot(*s), jnp.minimum(i, k - 1), 0))
                 for (w, _), b, k in zip(jobs, blks, nblks)]
    args = [w for w, _ in jobs]
    if stages:
        return _staged_call(core, name=f"cast_{name}", grid=(max(nblks),), in_specs=in_specs, out_specs=out_specs,
                            out_shape=out_shape, scratch_shapes=[], args=args, stages=stages)
    own = pl.pallas_call(
        core, name=f"cast_{name}",
        grid_spec=pltpu.PrefetchScalarGridSpec(num_scalar_prefetch=1, grid=(max(nblks),), in_specs=in_specs,
                                               out_specs=out_specs),
        out_shape=out_shape, compiler_params=_params(),
    )(chip, *args)
    return list(own), []


def _reduction_sums(name, jobs, pos):
    in_specs, out_specs, out_shape, args, bodies, counts = [], [], [], [], [], []
    for job in jobs:
        kind, grad, other = job[0], job[1], job[2]
        _, h, cols = other.shape
        blk = _flat_blk(h, cols)
        nblk = h // blk
        if kind == "pair":
            total = N_CHIPS * nblk

            def block(s, total=total, nblk=nblk):
                b = jnp.minimum(s, total - 1)
                return b // nblk, b % nblk

            spec = pl.BlockSpec((None, blk, cols), lambda s, p, block=block: (block(s)[0], block(s)[1], 0))
            in_specs += [pl.BlockSpec((None, blk, cols), lambda s, p, block=block, nblk=nblk:
                                      (block(s)[0], p[1] * nblk + block(s)[1], 0)), spec]
            out_specs.append(spec)
            out_shape.append(jax.ShapeDtypeStruct((N_CHIPS, h, cols), BF16))
            args += [grad, other]
            bodies.append((2, lambda g, o, out: out.__setitem__(..., (g[...] + o[...]).astype(BF16))))
        else:
            total = nblk

            def block(s, total=total):
                return jnp.minimum(s, total - 1)

            in_specs += [pl.BlockSpec((None, blk, cols), lambda s, p, block=block, nblk=nblk:
                                      (p[0], p[1] * nblk + block(s), 0)),
                         pl.BlockSpec((None, blk, cols), lambda s, p, block=block: (p[0], block(s), 0)),
                         pl.BlockSpec((3, blk, cols), lambda s, p, block=block: (0, block(s), 0))]
            out_specs.append(pl.BlockSpec((blk, cols), lambda s, p, block=block, nblk=nblk: (p[1] * nblk + block(s), 0)))
            out_shape.append(jax.ShapeDtypeStruct((2 * h, cols), F32))
            args += [grad, other, job[3]]
            bodies.append((3, lambda g, o, r, out: out.__setitem__(
                ..., (((g[...] + o[...]) + r[0].astype(F32)) + r[1].astype(F32)) + r[2].astype(F32))))
        counts.append(total)

    def body(pos_ref, *refs):
        ins, outs = refs[:len(args)], refs[len(args):]
        k = 0
        for (n_in, fn), out in zip(bodies, outs):
            fn(*ins[k:k + n_in], out)
            k += n_in

    return pl.pallas_call(
        body, name=f"reduction_sums_{name}",
        grid_spec=pltpu.PrefetchScalarGridSpec(num_scalar_prefetch=1, grid=(max(counts),), in_specs=in_specs,
                                               out_specs=out_specs),
        out_shape=out_shape,
        compiler_params=_params(),
    )(pos, *args)


def _sum_slots(name, slots):
    n, rows, _ = slots.shape

    def body(s_ref, o_ref):
        acc = s_ref[0]
        for d in range(1, n):
            acc = acc + s_ref[d]
        o_ref[...] = acc

    return pl.pallas_call(
        body, name=f"sum_slots_{name}", grid=(1,),
        in_specs=[pl.BlockSpec((n, rows, 128), lambda i: (0, 0, 0))],
        out_specs=pl.BlockSpec((rows, 128), lambda i: (0, 0)),
        out_shape=jax.ShapeDtypeStruct((rows, 128), F32),
        compiler_params=_params(),
    )(slots)


def _adamw_math(w, g, m, v):
    m2 = ADAM_B1 * m + (1.0 - ADAM_B1) * g
    v2 = ADAM_B2 * v + (1.0 - ADAM_B2) * (g * g)
    m_hat = m2 / (1.0 - ADAM_B1 ** ADAM_STEP)
    v_hat = v2 / (1.0 - ADAM_B2 ** ADAM_STEP)
    delta = -ADAM_LR * (m_hat / (jnp.sqrt(v_hat) + ADAM_EPS) + ADAM_WD * w)
    return delta, m2, v2


def _adamw_big(name, w, g0, g1, m, v):
    _, rows, cols = w.shape
    blk = _flat_blk(rows, cols) // 2

    def body(w_ref, g0_ref, g1_ref, m_ref, v_ref, g_ref, d_ref, m2_ref, v2_ref):
        g = jnp.where(pl.program_id(0) == 0, g0_ref[...], g1_ref[...])
        d, m2, v2 = _adamw_math(w_ref[...], g, m_ref[...], v_ref[...])
        g_ref[...] = g
        d_ref[...] = d
        m2_ref[...] = m2
        v2_ref[...] = v2

    spec = pl.BlockSpec((None, blk, cols), lambda la, i: (la, i, 0))
    return pl.pallas_call(
        body, name=f"adamw_{name}", grid=(N_LAYERS, rows // blk),
        in_specs=[spec, pl.BlockSpec((blk, cols), lambda la, i: (i * (1 - la), 0)),
                  pl.BlockSpec((blk, cols), lambda la, i: (i * la, 0)), spec, spec],
        out_specs=[spec] * 4,
        out_shape=[jax.ShapeDtypeStruct(w.shape, F32)] * 4,
        compiler_params=_params(("parallel", "parallel")),
    )(w, g0, g1, m, v)


def _adamw_small(ws, gs, ms, vs):
    n = len(ws)

    def body(*refs):
        ins, outs = refs[:4 * n], refs[4 * n:]
        for k in range(n):
            d, m2, v2 = _adamw_math(ins[k][...], ins[n + k][...], ins[2 * n + k][...], ins[3 * n + k][...])
            outs[k][...] = d
            outs[n + k][...] = m2
            outs[2 * n + k][...] = v2

    vmem = pl.BlockSpec(memory_space=pltpu.VMEM)
    return pl.pallas_call(
        body, name="adamw_small",
        in_specs=[vmem] * (4 * n), out_specs=[vmem] * (3 * n),
        out_shape=[jax.ShapeDtypeStruct(w.shape, F32) for w in ws] * 3,
        compiler_params=pltpu.CompilerParams(vmem_limit_bytes=V7X_VMEM_LIMIT),
    )(*ws, *gs, *ms, *vs)


SMALL = ("norm1_g", "b_gate", "gmlp_ln_g", "gmlp_ln_b", "w_spatial", "b_spatial", "w_shortconv", "norm2_g",
         "w_ffn_conv", "b_ffn_conv", "final_g")
ALL_WEIGHTS = ("norm1_g", "w_in", "b_gate", "gmlp_ln_g", "gmlp_ln_b", "w_spatial", "b_spatial", "w_shortconv",
               "w_branch", "w_out", "norm2_g", "w_ffn_up", "w_ffn_conv", "b_ffn_conv", "w_ffn_down", "final_g")


def _pack(arrays):
    flat = jnp.concatenate([a.reshape(-1) for a in arrays])
    n = flat.shape[0]
    rows = -(-n // 1024) * 8
    return jnp.pad(flat, (0, rows * 128 - n)).reshape(rows, 128)


def _unpack(packed, like):
    flat = packed.reshape(-1)
    out, off = [], 0
    for a in like:
        out.append(flat[off:off + a.size].reshape(a.shape))
        off += a.size
    return out


def _pad8(w):
    return jnp.pad(w, ((0, 5), (0, 0)))


def kernel(x, norm1_g, w_in, b_gate, gmlp_ln_g, gmlp_ln_b, w_spatial, b_spatial, w_shortconv, w_branch, w_out, norm2_g, w_ffn_up, w_ffn_conv, b_ffn_conv, w_ffn_down, final_g, loss_target, m_norm1_g, m_w_in, m_b_gate, m_gmlp_ln_g, m_gmlp_ln_b, m_w_spatial, m_b_spatial, m_w_shortconv, m_w_branch, m_w_out, m_norm2_g, m_w_ffn_up, m_w_ffn_conv, m_b_ffn_conv, m_w_ffn_down, m_final_g, v_norm1_g, v_w_in, v_b_gate, v_gmlp_ln_g, v_gmlp_ln_b, v_w_spatial, v_b_spatial, v_w_shortconv, v_w_branch, v_w_out, v_norm2_g, v_w_ffn_up, v_w_ffn_conv, v_b_ffn_conv, v_w_ffn_down, v_final_g):
    weights = dict(norm1_g=norm1_g, w_in=w_in, b_gate=b_gate, gmlp_ln_g=gmlp_ln_g, gmlp_ln_b=gmlp_ln_b,
                   w_spatial=w_spatial, b_spatial=b_spatial, w_shortconv=w_shortconv, w_branch=w_branch, w_out=w_out,
                   norm2_g=norm2_g, w_ffn_up=w_ffn_up, w_ffn_conv=w_ffn_conv, b_ffn_conv=b_ffn_conv,
                   w_ffn_down=w_ffn_down, final_g=final_g)
    mom = dict(norm1_g=m_norm1_g, w_in=m_w_in, b_gate=m_b_gate, gmlp_ln_g=m_gmlp_ln_g, gmlp_ln_b=m_gmlp_ln_b,
               w_spatial=m_w_spatial, b_spatial=m_b_spatial, w_shortconv=m_w_shortconv, w_branch=m_w_branch,
               w_out=m_w_out, norm2_g=m_norm2_g, w_ffn_up=m_w_ffn_up, w_ffn_conv=m_w_ffn_conv,
               b_ffn_conv=m_b_ffn_conv, w_ffn_down=m_w_ffn_down, final_g=m_final_g)
    vel = dict(norm1_g=v_norm1_g, w_in=v_w_in, b_gate=v_b_gate, gmlp_ln_g=v_gmlp_ln_g, gmlp_ln_b=v_gmlp_ln_b,
               w_spatial=v_w_spatial, b_spatial=v_b_spatial, w_shortconv=v_w_shortconv, w_branch=v_w_branch,
               w_out=v_w_out, norm2_g=v_norm2_g, w_ffn_up=v_w_ffn_up, w_ffn_conv=v_w_ffn_conv,
               b_ffn_conv=v_b_ffn_conv, w_ffn_down=v_w_ffn_down, final_g=v_final_g)

    cx, cy, cc = _mesh_pos()
    chip = 2 * cx + cy
    pos_arr = jnp.stack([chip, cc]).astype(jnp.int32)
    t_len = x.shape[1]
    xs = x.reshape(t_len, D_MODEL)
    target = loss_target.reshape(t_len, D_MODEL)
    pipe = _Pipe()

    full = {}

    mixer_w = ("w_in", "w_branch", "w_out")
    ffn_w = ("w_ffn_up", "w_ffn_down")
    slots = {}

    def cast(name, keys, stages):
        own, outs = _cast_into_slots(name, [(weights[n].reshape((N_LAYERS,) + BIG[n]), la) for n, la in keys],
                                     chip.astype(jnp.int32).reshape(1), stages)
        slots.update(zip(keys, own))
        return own, outs

    def gather(names, la):
        def then(*bufs):
            full.update(zip([(n, la) for n in names], bufs))

        pipe.add(_gather_stage([slots[(n, la)] for n in names], then))

    first = [(n, 0) for n in mixer_w]
    cast("first", first, [])
    gather(mixer_w, 0)
    tap_slots = {}
    pipe.add(_chip_spread_stage(_pack([w_shortconv, w_ffn_conv]), lambda got: tap_slots.__setitem__("all", got)))
    pipe.carry(lambda st: cast("rest", [(n, la) for la in range(N_LAYERS) for n in BIG_NAMES if (n, la) not in first], st))
    by_chip = [_unpack(tap_slots["all"][k], [w_shortconv, w_ffn_conv]) for k in range(N_CHIPS)]
    wsc_full = jnp.concatenate([t[0] for t in by_chip], axis=-1)
    wfc_full = jnp.concatenate([t[1] for t in by_chip], axis=-1)

    idx = jnp.arange(GMLP_BLOCK) // CHUNK
    mask = idx[None, :] <= idx[:, None]
    wm_all = jnp.where(mask[None, None], w_spatial, 0.0)
    wm_bf = wm_all.astype(BF16)
    wmt_bf = jnp.swapaxes(wm_all, -1, -2).astype(BF16)
    bsf = jnp.repeat(jnp.swapaxes(b_spatial, -1, -2), 128, axis=-1)

    def row(a):
        return a.reshape(1, -1)

    def mixer_args(la):
        return (row(norm1_g[la]), row(b_gate[la]), row(gmlp_ln_g[la]), row(gmlp_ln_b[la]))

    def mixer_weights(la):
        return tuple(full[(n, la)] for n in mixer_w)

    def ffn_weights(la):
        return tuple(full[(n, la)] for n in ffn_w)

    saved = []
    h_in = xs
    for la in range(N_LAYERS):
        gather(ffn_w, la)
        *kept, mg, h1, x2 = pipe.carry(lambda st: _mixer_fwd(
            la, h_in, *mixer_args(la), wm_bf[la], bsf[la], _pad8(wsc_full[la]), *mixer_weights(la), st))
        ya, yb = kept[1], kept[2]
        if la + 1 < N_LAYERS:
            gather(mixer_w, la + 1)
        head = (target, row(final_g)) if la == N_LAYERS - 1 else None
        up, silu, dsilu, act, h2, *rest = pipe.carry(lambda st: _ffn_fwd(
            la, x2, row(norm2_g[la]), _pad8(wfc_full[la]), row(b_ffn_conv[la]), *ffn_weights(la), st, head=head))
        saved.append(dict(x=h_in, ya=ya, yb=yb, mixer=[kept[0]] + kept[3:], mg=mg, h1=h1, x2=x2, up=up, silu=silu,
                          dsilu=dsilu, act=act, h2=h2))
        h_in = rest[0]
    dx, dgf8, loss8 = rest

    reduced_big = {}

    sums_due = []

    def run_sums():
        if sums_due:
            due = list(sums_due)
            sums_due.clear()
            run_sums.calls += 1
            for (_, then), res in zip(due, _reduction_sums(str(run_sums.calls), [job for job, _ in due], pos_arr)):
                then(res)

    run_sums.calls = 0
    pipe.after = run_sums

    def reduce_big(name, la, grad):
        def after_pair(other):
            def after_chips(got):
                sums_due.append((("chip", grad, other, got), lambda final: pipe.add(_pair_fill_stage(
                    final, lambda done: reduced_big.__setitem__((name, la), done)))))

            sums_due.append((("pair", grad, other), lambda psum: pipe.add(_chip_send_stage(psum, after_chips))))

        pipe.add(_pair_send_stage(grad, after_pair))

    small = {n: [None] * N_LAYERS for n in SMALL}
    spread = {}
    for la in reversed(range(N_LAYERS)):
        s = saved[la]
        dx3 = dx
        dx2, dup, dx3b, dg2, dbfc, dwfc = pipe.carry(lambda st: _ffn_bwd(
            la, dx3, s["x2"], s["up"], s["silu"], s["dsilu"], row(norm2_g[la]), _pad8(wfc_full[la]),
            *ffn_weights(la), st))
        g, = pipe.carry(lambda st: _wgrad("w_ffn_up", la, s["h2"], dup, 1024, 1408, 512, 2816, st))
        reduce_big("w_ffn_up", la, g)
        g, = pipe.carry(lambda st: _wgrad("w_ffn_down", la, s["act"], dx3b, 704, 1024, 1408, 1024, st))
        reduce_big("w_ffn_down", la, g)
        run = pipe.carry if la > 0 else (lambda call: call([])[0])
        dxl, dz, da, db, dx2b, dg1, dbg, dlng, dlnb, dwm, dbsf, dwsc = run(lambda st: _mixer_bwd(
            la, dx2, s["x"], *s["mixer"], row(norm1_g[la]), row(gmlp_ln_g[la]), row(gmlp_ln_b[la]), wmt_bf[la],
            _pad8(wsc_full[la]), *mixer_weights(la), st))
        small["norm1_g"][la] = dg1.sum(0)
        small["b_gate"][la] = dbg.sum(0)
        small["gmlp_ln_g"][la] = dlng.sum(0)
        small["gmlp_ln_b"][la] = dlnb.sum(0)
        small["w_spatial"][la] = jnp.where(mask[None], dwm, 0.0)
        small["b_spatial"][la] = dbsf.reshape(128, A_HEADS, 128).sum(-1).T
        small["w_shortconv"][la] = dwsc.sum(1)
        small["norm2_g"][la] = dg2.sum(0)
        small["w_ffn_conv"][la] = dwfc.sum(1)
        small["b_ffn_conv"][la] = dbfc.sum(0)
        if la == 0:
            small_local = ([jnp.stack(small[n]) for n in SMALL[:-1]]
                           + [dgf8.sum(0), 0.5 * loss8.sum().reshape(1) / D_MODEL])
            mine = _pack(small_local)

            def after_swap(other, mine=mine):
                pair = _sum_slots("small_pair", jnp.stack([mine, other]))
                pipe.add(_chip_spread_stage(pair, lambda slots: spread.__setitem__("slots", slots)))

            pipe.add(_pair_swap_stage(mine, after_swap))
        if la > 0:
            g, = pipe.carry(lambda st: _wgrad("w_in", la, s["h1"], dz, 1024, 1152, 512, 2304, st))
            reduce_big("w_in", la, g)
        else:
            for part, tag in enumerate(("w_in_a", "w_in_b")):
                g, = pipe.carry(lambda st: _wgrad(tag, la, s["h1"], dz, 512, 1152, 512, 2304, st, a_first=part))
                reduce_big(tag, la, g)
        g, = pipe.carry(lambda st: _wgrad("w_out", la, s["mg"], dx2b, 256, 1024, 1024, 1024, st), long=False)
        reduce_big("w_out", la, g)
        g, = pipe.carry(lambda st: _wgrad_branch(la, s["ya"], da, s["yb"], db, st), long=False)
        reduce_big("w_branch", la, g)
        dx = dxl
    grad_x = dx.reshape(x.shape)
    pipe.flush()

    reduced_big[("w_in", 0)] = jnp.concatenate([reduced_big[("w_in_a", 0)], reduced_big[("w_in_b", 0)]], axis=0)
    reduced = _unpack(_sum_slots("small_grads", spread["slots"]), small_local)
    loss = reduced[-1].reshape(())
    grads = dict(zip(SMALL, reduced[:-1]))
    grads["w_shortconv"] = lax.dynamic_slice(grads["w_shortconv"], (0, 0, chip * (D_B // 4)), (N_LAYERS, 3, D_B // 4))
    grads["w_ffn_conv"] = lax.dynamic_slice(grads["w_ffn_conv"], (0, 0, chip * (D_FF // 4)), (N_LAYERS, 3, D_FF // 4))

    delta, new_m, new_v = {}, {}, {}
    for n in BIG_NAMES:
        shape3 = (N_LAYERS,) + BIG[n]
        res = _adamw_big(n, weights[n].reshape(shape3), reduced_big[(n, 0)], reduced_big[(n, 1)],
                         mom[n].reshape(shape3), vel[n].reshape(shape3))
        grads[n], delta[n], new_m[n], new_v[n] = (a.reshape(weights[n].shape) for a in res)
    res = _adamw_small(*[[src[n].reshape(-1, src[n].shape[-1]) for n in SMALL] for src in (weights, grads, mom, vel)])
    for k, n in enumerate(SMALL):
        delta[n], new_m[n], new_v[n] = (res[j * len(SMALL) + k].reshape(weights[n].shape) for j in range(3))

    return (loss, grad_x, *[grads[n] for n in ALL_WEIGHTS], *[delta[n] for n in ALL_WEIGHTS],
            *[new_m[n] for n in ALL_WEIGHTS], *[new_v[n] for n in ALL_WEIGHTS])
```

```python
import jax
import jax.numpy as jnp
from jax import lax
from jax.experimental import pallas as pl
from jax.experimental.pallas import tpu as pltpu

F32 = jnp.float32
BF16 = jnp.bfloat16
MESH = pl.DeviceIdType.MESH
ANY = pl.BlockSpec(memory_space=pl.ANY)

D_MODEL = 1024
D_A = 512
D_B = 512
D_IN = 4608
D_FF = 2816
GMLP_BLOCK = 128
CHUNK = 64
A_HEADS = 4
N_LAYERS = 2
N_CHIPS = 4
RMS_EPS = 1e-6
LN_EPS = 1e-5
ADAM_LR = 0.001
ADAM_B1 = 0.9
ADAM_B2 = 0.999
ADAM_EPS = 1e-08
ADAM_WD = 0.01
ADAM_STEP = 10

C_U, C_V, C_BG, C_CG, C_HB, C_GA, C_GB = 0, 512, 1024, 1536, 2048, 2560, 3584

V7X_VMEM_LIMIT = 60 * 1024 * 1024
TM_MIX = 256
TM_FFN = 256
TK_WGRAD = 2048
SLOW_COPY_BYTES = 768 * 1024
FF_CHUNKS = ((0, 768), (768, 1536), (1536, 2304), (2304, 2816))
GELU_C0 = 0.7978845608028654
GELU_C1 = 0.044715

BIG = {
    "w_in": (1024, 1152),
    "w_branch": (1024, 256),
    "w_out": (256, 1024),
    "w_ffn_up": (1024, 1408),
    "w_ffn_down": (704, 1024),
}
BIG_NAMES = tuple(BIG)


def _params(sem=("arbitrary",), vmem=V7X_VMEM_LIMIT):
    return pltpu.CompilerParams(dimension_semantics=sem, vmem_limit_bytes=vmem)


def _gelu(x):
    x2 = x * x
    t = jnp.tanh(GELU_C0 * x * (1.0 + GELU_C1 * x2))
    return 0.5 * x * (1.0 + t), t


def _gelu_grad(x, t):
    return 0.5 * (1.0 + t) + 0.5 * x * (1.0 - t * t) * GELU_C0 * (1.0 + 3.0 * GELU_C1 * x * x)


def _colsum8(v):
    r, n = v.shape
    return v.reshape(r // 8, 8, n).sum(axis=0)


def _dot(a, b):
    return jnp.dot(a, b, preferred_element_type=F32)


def _dot_nt(a, b):
    return lax.dot_general(a, b, (((1,), (1,)), ((), ())), preferred_element_type=F32)


def _dot_tn(a, b):
    return lax.dot_general(a, b, (((0,), (0,)), ((), ())), preferred_element_type=F32)


def _shift_down(v, carry, n):
    rows = lax.broadcasted_iota(jnp.int32, (8, v.shape[1]), 0)
    out = pltpu.roll(v, n, 0)
    head = out[0:8, :]
    for r in range(n):
        head = jnp.where(rows == r, carry[8 - n + r:8 - n + r + 1, :], head)
    return jnp.concatenate([head, out[8:, :]], axis=0)


def _shift_up(v, carry, n):
    tm = v.shape[0]
    rows = lax.broadcasted_iota(jnp.int32, (8, v.shape[1]), 0)
    out = pltpu.roll(v, tm - n, 0)
    tail = out[tm - 8:tm, :]
    for r in range(n):
        tail = jnp.where(rows == 8 - n + r, carry[r:r + 1, :], tail)
    return jnp.concatenate([out[0:tm - 8, :], tail], axis=0)


def _sigmoid(x):
    return 0.5 * jnp.tanh(0.5 * x) + 0.5


def _start_all(copies):
    for cp in copies:
        cp.start()


def _wait_all(copies):
    for cp in copies:
        cp.wait()


def _load_col_sharded(src, dst, sems, first):
    cs = src.shape[-1]
    return [pltpu.make_async_copy(src.at[k], dst.at[:, k * cs:(k + 1) * cs], sems.at[first + k])
            for k in range(N_CHIPS)]


def _load_row_sharded(src, dst, sems, first):
    rs = src.shape[-2]
    return [pltpu.make_async_copy(src.at[k], dst.at[k * rs:(k + 1) * rs, :], sems.at[first + k])
            for k in range(N_CHIPS)]


def _load_branch(src, dst, sems, first):
    return [pltpu.make_async_copy(src.at[k, pl.ds(m * D_A, D_A), :], dst.at[m, :, k * 256:(k + 1) * 256],
                                  sems.at[first + 2 * k + m])
            for k in range(N_CHIPS) for m in range(2)]


def _row_spec(tm, n, rev=None):
    if rev is None:
        return pl.BlockSpec((tm, n), lambda i: (i, 0))
    return pl.BlockSpec((tm, n), lambda i: (rev - 1 - i, 0))


def _const_spec(shape):
    nd = len(shape)
    return pl.BlockSpec(shape, lambda i: (0,) * nd)


def _mesh_pos():
    return lax.axis_index("x"), lax.axis_index("y"), lax.axis_index("c")


def _other_chips(x, y):
    return [(1 - x, y, 2 * (1 - x) + y), (x, 1 - y, 2 * x + (1 - y)), (1 - x, 1 - y, 2 * (1 - x) + (1 - y))]


def _remote(src, dst, ssem, rsem, to):
    return pltpu.make_async_remote_copy(src_ref=src, dst_ref=dst, send_sem=ssem, recv_sem=rsem, device_id=to,
                                        device_id_type=MESH)


def _half(ref, which, h):
    start = pl.multiple_of(which * h, 8)
    if len(ref.shape) == 2:
        return ref.at[pl.ds(start, h), :]
    return ref.at[:, pl.ds(start, h), :]


class _Stage:
    def __init__(self, ins=(), inouts=(), outs=(), n_sems=0, start=None, mid=None, finish=None, then=None, slow=False):
        self.ins, self.inouts, self.outs = list(ins), list(inouts), list(outs)
        self.n_sems, self.start, self.mid, self.finish, self.then = n_sems, start, mid, finish, then
        self.slow = slow


def _gather_stage(bufs, then):
    n = len(bufs)

    def copies(io, sem):
        x, y, c = _mesh_pos()
        me = 2 * x + y
        ici, fwd, got = [], [], []
        for w in range(n):
            h = io[w].shape[1] // 2
            for j, (px, py, pk) in enumerate(_other_chips(x, y)):
                mine = _half(io[w].at[me], c, h)
                theirs = _half(io[w].at[pk], c, h)
                ici.append(_remote(mine, mine, sem(12 * w + j), sem(12 * w + 3 + j), (px, py, c)))
                got.append(_remote(theirs, theirs, sem(12 * w + j), sem(12 * w + 3 + j), (px, py, c)))
                fwd.append(_remote(theirs, theirs, sem(12 * w + 6 + j), sem(12 * w + 9 + j), (x, y, 1 - c)))
        return ici, got, fwd

    def start(ins, io, outs, sem):
        _start_all(copies(io, sem)[0])

    def mid(ins, io, outs, sem):
        _, got, fwd = copies(io, sem)
        for g, f in zip(got, fwd):
            g.wait_recv()
            f.start()

    def finish(ins, io, outs, sem):
        x, y, c = _mesh_pos()
        ici, _, fwd = copies(io, sem)
        for w in range(n):
            h = io[w].shape[1] // 2
            for j, (px, py, pk) in enumerate(_other_chips(x, y)):
                other = _half(io[w].at[pk], 1 - c, h)
                _remote(other, other, sem(12 * w + 6 + j), sem(12 * w + 9 + j), (x, y, 1 - c)).wait_recv()
        for cp in ici + fwd:
            cp.wait_send()

    return _Stage(inouts=bufs, n_sems=12 * n, start=start, mid=mid, finish=finish, then=then)


def _pair_send_stage(grad, then):
    h = grad.shape[1] // 2

    def copy(ins, outs, sem):
        x, y, c = _mesh_pos()
        return _remote(_half(ins[0], 1 - c, h), outs[0], sem(0), sem(1), (x, y, 1 - c))

    return _Stage(ins=[grad], outs=[jax.ShapeDtypeStruct((N_CHIPS, h, grad.shape[2]), F32)], n_sems=2,
                  start=lambda ins, io, outs, sem: copy(ins, outs, sem).start(),
                  finish=lambda ins, io, outs, sem: copy(ins, outs, sem).wait(), then=then)


def _chip_send_stage(psum, then):
    def copies(ins, outs, sem):
        x, y, c = _mesh_pos()
        return [_remote(ins[0].at[pk], outs[0].at[j], sem(j), sem(3 + j), (px, py, c))
                for j, (px, py, pk) in enumerate(_other_chips(x, y))]

    return _Stage(ins=[psum], outs=[jax.ShapeDtypeStruct((3,) + psum.shape[1:], BF16)], n_sems=6,
                  start=lambda ins, io, outs, sem: _start_all(copies(ins, outs, sem)),
                  finish=lambda ins, io, outs, sem: _wait_all(copies(ins, outs, sem)), then=then,
                  slow=psum.shape[1] * psum.shape[2] * 2 > SLOW_COPY_BYTES)


def _pair_fill_stage(final, then):
    h = final.shape[0] // 2

    def copy(io, sem):
        x, y, c = _mesh_pos()
        mine = _half(io[0], c, h)
        return _remote(mine, mine, sem(0), sem(1), (x, y, 1 - c))

    return _Stage(inouts=[final], n_sems=2,
                  start=lambda ins, io, outs, sem: copy(io, sem).start(),
                  finish=lambda ins, io, outs, sem: copy(io, sem).wait(), then=then)


def _pair_swap_stage(packed, then):
    def copy(ins, outs, sem):
        x, y, c = _mesh_pos()
        return _remote(ins[0], outs[0], sem(0), sem(1), (x, y, 1 - c))

    return _Stage(ins=[packed], outs=[jax.ShapeDtypeStruct(packed.shape, F32)], n_sems=2,
                  start=lambda ins, io, outs, sem: copy(ins, outs, sem).start(),
                  finish=lambda ins, io, outs, sem: copy(ins, outs, sem).wait(), then=then)


def _chip_spread_stage(psum, then):
    def copies(ins, outs, sem):
        x, y, c = _mesh_pos()
        me = 2 * x + y
        cps = [_remote(ins[0], outs[0].at[me], sem(j), sem(3 + j), (px, py, c))
               for j, (px, py, pk) in enumerate(_other_chips(x, y))]
        return cps, pltpu.make_async_copy(ins[0], outs[0].at[me], sem(6))

    def start(ins, io, outs, sem):
        cps, own = copies(ins, outs, sem)
        own.start()
        _start_all(cps)

    def finish(ins, io, outs, sem):
        cps, own = copies(ins, outs, sem)
        _wait_all(cps)
        own.wait()

    return _Stage(ins=[psum], outs=[jax.ShapeDtypeStruct((N_CHIPS,) + psum.shape, F32)], n_sems=7,
                  start=start, finish=finish, then=then)


def _staged_call(core, *, name, grid, in_specs, out_specs, out_shape, scratch_shapes, args, stages):
    n_in, n_out, n_scr = len(args), len(out_shape), len(scratch_shapes)
    s_args, s_outs, aliases, layout = [], [], {}, []
    n_sems = 0
    for st in stages:
        i0, o0 = len(s_args), len(s_outs)
        s_args += st.ins + st.inouts
        for q in range(len(st.inouts)):
            aliases[n_in + i0 + len(st.ins) + q] = n_out + o0 + q
        s_outs += [jax.ShapeDtypeStruct(a.shape, a.dtype) for a in st.inouts] + st.outs
        layout.append((i0, o0, n_sems))
        n_sems += st.n_sems
    steps = 1
    for g in grid:
        steps *= g

    def body(*refs):
        own_in = refs[:n_in]
        s_in = refs[n_in:n_in + len(s_args)]
        rest = refs[n_in + len(s_args):]
        own_out = rest[:n_out]
        s_out = rest[n_out:n_out + len(s_outs)]
        scr = rest[n_out + len(s_outs):]

        def run(which):
            for st, (i0, o0, s0) in zip(stages, layout):
                fn = getattr(st, which)
                if fn is not None:
                    fn(s_in[i0:i0 + len(st.ins)], s_out[o0:o0 + len(st.inouts)],
                       s_out[o0 + len(st.inouts):o0 + len(st.inouts) + len(st.outs)],
                       lambda k, s0=s0: scr[n_scr].at[s0 + k])

        if not stages:
            core(*own_in, *own_out, *scr[:n_scr])
            return
        step = 0
        for d, g in enumerate(grid):
            step = step * g + pl.program_id(d)
        if steps == 1:
            run("start")
            core(*own_in, *own_out, *scr[:n_scr])
            run("mid")
            run("finish")
            return
        pl.when(step == 0)(lambda: run("start"))
        core(*own_in, *own_out, *scr[:n_scr])
        pl.when(step == (3 * steps) // 4)(lambda: run("mid"))
        pl.when(step == steps - 1)(lambda: run("finish"))

    sem = ("arbitrary",) * len(grid) if stages else ("parallel",) * max(len(grid) - 1, 0) + ("arbitrary",) * min(len(grid), 1)
    res = pl.pallas_call(
        body, name=name, grid=grid,
        in_specs=list(in_specs) + [ANY] * len(s_args),
        out_specs=list(out_specs) + [ANY] * len(s_outs),
        out_shape=list(out_shape) + s_outs,
        input_output_aliases=aliases,
        scratch_shapes=list(scratch_shapes) + ([pltpu.SemaphoreType.DMA((n_sems,))] if stages else []),
        compiler_params=_params(sem) if grid else pltpu.CompilerParams(vmem_limit_bytes=V7X_VMEM_LIMIT),
    )(*args, *s_args)
    return list(res[:n_out]), list(res[n_out:])


class _Pipe:
    def __init__(self):
        self.ready = []
        self.flushes = 0
        self.after = None

    def add(self, stage):
        self.ready.append(stage)

    def carry(self, call, long=True):
        stages = [st for st in self.ready if long or not st.slow]
        self.ready = [st for st in self.ready if not (long or not st.slow)]
        own, outs = call(stages)
        k = 0
        for st in stages:
            n = len(st.inouts) + len(st.outs)
            st.then(*outs[k:k + n])
            k += n
        if self.after is not None:
            self.after()
        return own

    def flush(self):
        while self.ready:
            self.flushes += 1
            self.carry(lambda stages: _staged_call(
                lambda *refs: None, name=f"comm_tail_{self.flushes}", grid=(), in_specs=[], out_specs=[], out_shape=[],
                scratch_shapes=[], args=[], stages=stages))


def _mixer_fwd(layer, x, g1, bgate, lng, lnb, wm, bsf, wsc, win_g, wb_g, wout_g, stages):
    t_len = x.shape[0]
    tm = min(TM_MIX, t_len)
    nt = t_len // tm
    nb = tm // GMLP_BLOCK

    def core(x_ref, x_late_ref, g1_ref, bgate_ref, lng_ref, lnb_ref, wm_ref, bsf_ref, wsc_ref, win_hbm, wb_hbm, wout_hbm,
             zc_ref, ya_ref, yb_ref, q_ref, sa_ref, ca_ref, sb_ref, cb_ref, ug_ref, fu_ref, xh_ref, cv_ref,
             mg_ref, h_ref, x2_ref,
             win_v, wb_v, wout_v, carry, vn_s, f_s, z_s, sems):
        i = pl.program_id(0)

        @pl.when(i == 0)
        def _():
            cps = (_load_col_sharded(win_hbm, win_v, sems, 0) + _load_branch(wb_hbm, wb_v, sems, 4)
                   + _load_row_sharded(wout_hbm, wout_v, sems, 12))
            _start_all(cps)
            carry[...] = jnp.zeros_like(carry)
            z_s[...] = jnp.zeros_like(z_s)
            _wait_all(cps)

        xv = x_ref[...]
        r = lax.rsqrt(jnp.mean(xv * xv, axis=-1, keepdims=True) + RMS_EPS)
        h_ref[...] = (xv * r * g1_ref[...]).astype(BF16)

        def zcols(c0, n, keep=None):
            zv = z_s[:, c0:c0 + n]
            z_s[:, c0:c0 + n] = _dot(h_ref[...], win_v[:, c0:c0 + n])
            if keep is not None:
                zc_ref[:, keep * D_B:(keep + 1) * D_B] = zv.astype(BF16)
            return zv

        v = zcols(C_V, D_A)
        vg, tv = _gelu(v)
        mu = jnp.mean(vg, axis=-1, keepdims=True)
        vc = vg - mu
        rstd = lax.rsqrt(jnp.mean(vc * vc, axis=-1, keepdims=True) + LN_EPS)
        xh = vc * rstd
        xh_ref[...] = xh.astype(BF16)
        cv_ref[...] = (rstd * _gelu_grad(v, tv)).astype(BF16)
        vn_s[...] = (xh * lng_ref[...] + lnb_ref[...]).astype(BF16)
        for hd in range(A_HEADS):
            cols = slice(hd * 128, (hd + 1) * 128)
            vcat = jnp.concatenate([vn_s[b * 128:(b + 1) * 128, cols] for b in range(nb)], axis=1)
            fcat = _dot(wm_ref[hd], vcat)
            for b in range(nb):
                f_s[b * 128:(b + 1) * 128, cols] = fcat[:, b * 128:(b + 1) * 128]
        u = zcols(C_U, D_A)
        ug, tu = _gelu(u)
        ug_ref[...] = ug.astype(BF16)
        fb = f_s[...] + jnp.concatenate([bsf_ref[...]] * nb, axis=0)
        fu_ref[...] = (fb * _gelu_grad(u, tu)).astype(BF16)
        ya_ref[...] = (ug * fb).astype(BF16)

        p = zcols(C_CG, D_B, keep=1) * zcols(C_HB, D_B, keep=2)
        cr = carry[...]
        q = wsc_ref[0:1, :] * _shift_down(p, cr, 2) + wsc_ref[1:2, :] * _shift_down(p, cr, 1) + wsc_ref[2:3, :] * p
        carry[...] = p[tm - 8:tm, :]
        q_ref[...] = q.astype(BF16)
        yb_ref[...] = (zcols(C_BG, D_B, keep=0) * q).astype(BF16)

        av = _dot(ya_ref[...], wb_v[0])
        sa = _sigmoid(zcols(C_GA, D_MODEL) + bgate_ref[:, 0:D_MODEL])
        sa_ref[...] = sa.astype(BF16)
        mg = sa * av
        ca_ref[...] = (mg * (1.0 - sa)).astype(BF16)
        bv = _dot(yb_ref[...], wb_v[1])
        sb = _sigmoid(zcols(C_GB, D_MODEL) + bgate_ref[:, D_MODEL:2 * D_MODEL])
        sb_ref[...] = sb.astype(BF16)
        mb = sb * bv
        cb_ref[...] = (mb * (1.0 - sb)).astype(BF16)
        mg_ref[...] = (mg + mb).astype(BF16)
        x2_ref[...] = x_late_ref[...] + _dot(mg_ref[...], wout_v[...])

    def tile(n, lag):
        return pl.BlockSpec((tm, n), lambda i: (jnp.clip(i - lag, 0, nt - 1), 0))

    outs = [
        jax.ShapeDtypeStruct((t_len, 3 * D_B), BF16),
        jax.ShapeDtypeStruct((t_len, D_A), BF16),
        jax.ShapeDtypeStruct((t_len, D_B), BF16),
        jax.ShapeDtypeStruct((t_len, D_B), BF16),
        jax.ShapeDtypeStruct((t_len, D_MODEL), BF16),
        jax.ShapeDtypeStruct((t_len, D_MODEL), BF16),
        jax.ShapeDtypeStruct((t_len, D_MODEL), BF16),
        jax.ShapeDtypeStruct((t_len, D_MODEL), BF16),
        jax.ShapeDtypeStruct((t_len, D_A), BF16),
        jax.ShapeDtypeStruct((t_len, D_A), BF16),
        jax.ShapeDtypeStruct((t_len, D_A), BF16),
        jax.ShapeDtypeStruct((t_len, D_A), BF16),
        jax.ShapeDtypeStruct((t_len, D_MODEL), BF16),
        jax.ShapeDtypeStruct((t_len, D_MODEL), BF16),
        jax.ShapeDtypeStruct((t_len, D_MODEL), F32),
    ]
    return _staged_call(
        core, name=f"mixer_fwd_l{layer}", grid=(nt + 1,),
        in_specs=[tile(D_MODEL, 0), tile(D_MODEL, 1), _const_spec((1, D_MODEL)), _const_spec((1, 2 * D_MODEL)),
                  _const_spec((1, D_A)), _const_spec((1, D_A)), _const_spec((A_HEADS, 128, 128)),
                  _const_spec((128, D_A)), _const_spec((8, D_B)), ANY, ANY, ANY],
        out_specs=[tile(o.shape[1], 0 if k == len(outs) - 2 else 1) for k, o in enumerate(outs)],
        out_shape=outs,
        scratch_shapes=[pltpu.VMEM((D_MODEL, D_IN), BF16), pltpu.VMEM((2, D_A, D_MODEL), BF16),
                        pltpu.VMEM((D_MODEL, D_MODEL), BF16), pltpu.VMEM((8, D_B), F32),
                        pltpu.VMEM((tm, D_A), BF16), pltpu.VMEM((tm, D_A), F32), pltpu.VMEM((tm, D_IN), F32),
                        pltpu.SemaphoreType.DMA((16,))],
        args=[x, x, g1, bgate, lng, lnb, wm, bsf, wsc, win_g, wb_g, wout_g], stages=stages)


def _ffn_fwd(layer, x2, g2, wfc, bfc, wup_g, wdown_g, stages, head=None):
    t_len = x2.shape[0]
    tm = min(TM_FFN, t_len)
    nt = t_len // tm

    def core(*refs):
        if head is None:
            (x_ref, g2_ref, wfc_ref, bfc_ref, wup_hbm, wdown_hbm, up_ref, silu_ref, dsilu_ref, act_ref, h_ref, x3_ref,
             wup_v, wdown_v, carry, sems) = refs
        else:
            (x_ref, g2_ref, wfc_ref, bfc_ref, t_ref, gf_ref, wup_hbm, wdown_hbm, up_ref, silu_ref, dsilu_ref, act_ref,
             h_ref, dx_ref, dgf_ref, loss_ref, wup_v, wdown_v, carry, sems) = refs
        i = pl.program_id(0)

        @pl.when(i == 0)
        def _():
            cps = _load_col_sharded(wup_hbm, wup_v, sems, 0) + _load_row_sharded(wdown_hbm, wdown_v, sems, 4)
            _start_all(cps)
            carry[...] = jnp.zeros_like(carry)
            if head is not None:
                dgf_ref[...] = jnp.zeros_like(dgf_ref)
                loss_ref[...] = jnp.zeros_like(loss_ref)
            _wait_all(cps)

        xv = x_ref[...]
        r = lax.rsqrt(jnp.mean(xv * xv, axis=-1, keepdims=True) + RMS_EPS)
        h_ref[...] = (xv * r * g2_ref[...]).astype(BF16)
        gate = _dot(h_ref[...], wup_v[:, 0:D_FF])
        up_ref[:, 0:D_FF] = gate.astype(BF16)
        cr = carry[...]
        gc = (wfc_ref[0:1, :] * _shift_down(gate, cr, 2) + wfc_ref[1:2, :] * _shift_down(gate, cr, 1)
              + wfc_ref[2:3, :] * gate + bfc_ref[...])
        carry[...] = gate[tm - 8:tm, :]
        sg = _sigmoid(gc)
        silu = gc * sg
        silu_ref[...] = silu.astype(BF16)
        dsilu_ref[...] = (sg + silu * (1.0 - sg)).astype(BF16)
        val = _dot(h_ref[...], wup_v[:, D_FF:2 * D_FF])
        up_ref[:, D_FF:2 * D_FF] = val.astype(BF16)
        act_ref[...] = (silu * val).astype(BF16)
        x3 = x_ref[...] + _dot(act_ref[...], wdown_v[...])
        if head is None:
            x3_ref[...] = x3
        else:
            r3 = lax.rsqrt(jnp.mean(x3 * x3, axis=-1, keepdims=True) + RMS_EPS)
            xh = x3 * r3
            err = xh * gf_ref[...] - t_ref[...]
            loss_ref[...] += _colsum8(err * err)
            dy = err * (1.0 / D_MODEL)
            dgf_ref[...] += _colsum8(dy * xh)
            dxh = dy * gf_ref[...]
            dx_ref[...] = r3 * (dxh - xh * jnp.mean(dxh * xh, axis=-1, keepdims=True))

    outs = [
        jax.ShapeDtypeStruct((t_len, 2 * D_FF), BF16),
        jax.ShapeDtypeStruct((t_len, D_FF), BF16),
        jax.ShapeDtypeStruct((t_len, D_FF), BF16),
        jax.ShapeDtypeStruct((t_len, D_FF), BF16),
        jax.ShapeDtypeStruct((t_len, D_MODEL), BF16),
        jax.ShapeDtypeStruct((t_len, D_MODEL), F32),
    ]
    in_specs = [_row_spec(tm, D_MODEL), _const_spec((1, D_MODEL)), _const_spec((8, D_FF)), _const_spec((1, D_FF))]
    out_specs = [_row_spec(tm, o.shape[1]) for o in outs]
    args = [x2, g2, wfc, bfc]
    if head is not None:
        in_specs += [_row_spec(tm, D_MODEL), _const_spec((1, D_MODEL))]
        args += list(head)
        outs += [jax.ShapeDtypeStruct((8, D_MODEL), F32)] * 2
        out_specs += [_const_spec((8, D_MODEL))] * 2
    return _staged_call(
        core, name=f"ffn_fwd_l{layer}", grid=(nt,),
        in_specs=in_specs + [ANY, ANY], out_specs=out_specs, out_shape=outs,
        scratch_shapes=[pltpu.VMEM((D_MODEL, 2 * D_FF), BF16), pltpu.VMEM((D_FF, D_MODEL), BF16),
                        pltpu.VMEM((8, D_FF), F32), pltpu.SemaphoreType.DMA((8,))],
        args=args + [wup_g, wdown_g], stages=stages)


def _ffn_bwd(layer, dx3, x2, up, silu, dsilu, g2, wfc, wup_g, wdown_g, stages):
    t_len = x2.shape[0]
    tm = min(TM_FFN, t_len)
    nt = t_len // tm

    def core(dx3_ref, dx3_late_ref, x_ref, up_ref, silu_ref, dsilu_ref, g2_ref, wfc_ref, wup_hbm, wdown_hbm,
             dx2_ref, dup_ref, dx3b_ref, dg2_ref, dbfc_ref, dwfc_ref,
             wup_v, wdown_v, carry, da_s, dup_s, sems):
        i = pl.program_id(0)

        @pl.when(i == 0)
        def _():
            cps = _load_col_sharded(wup_hbm, wup_v, sems, 0) + _load_row_sharded(wdown_hbm, wdown_v, sems, 4)
            _start_all(cps)
            for ref in (carry, da_s, dup_s, dg2_ref, dbfc_ref, dwfc_ref):
                ref[...] = jnp.zeros_like(ref)
            _wait_all(cps)

        live = (i <= nt).astype(F32)
        dx3b_ref[...] = dx3_ref[...].astype(BF16)
        dh = jnp.zeros((tm, D_MODEL), F32)
        for c0, c1 in FF_CHUNKS:
            v0, v1 = D_FF + c0, D_FF + c1
            dh = dh + _dot_nt(dup_s[:, c0:c1], wup_v[:, c0:c1]) + _dot_nt(dup_s[:, v0:v1], wup_v[:, v0:v1])
            da = da_s[:, c0:c1]
            dval = (da * silu_ref[:, c0:c1].astype(F32)).astype(BF16)
            dup_ref[:, v0:v1] = dval
            dup_s[:, v0:v1] = dval
            dgc = da * up_ref[:, v0:v1].astype(F32) * dsilu_ref[:, c0:c1].astype(F32)
            cr = carry[:, c0:c1]
            dgc1 = _shift_up(dgc, cr, 1)
            dgc2 = _shift_up(dgc, cr, 2)
            carry[:, c0:c1] = jnp.where(i < nt, dgc[0:8, :], cr)
            gate = up_ref[:, c0:c1].astype(F32)
            dbfc_ref[:, c0:c1] += live * _colsum8(dgc)
            dwfc_ref[0, :, c0:c1] += live * _colsum8(dgc2 * gate)
            dwfc_ref[1, :, c0:c1] += live * _colsum8(dgc1 * gate)
            dwfc_ref[2, :, c0:c1] += live * _colsum8(dgc * gate)
            dgate = (wfc_ref[2:3, c0:c1] * dgc + wfc_ref[1:2, c0:c1] * dgc1 + wfc_ref[0:1, c0:c1] * dgc2).astype(BF16)
            dup_ref[:, c0:c1] = dgate
            dup_s[:, c0:c1] = dgate
            da_s[:, c0:c1] = _dot_nt(dx3b_ref[...], wdown_v[c0:c1, :])
        xv = x_ref[...]
        r = lax.rsqrt(jnp.mean(xv * xv, axis=-1, keepdims=True) + RMS_EPS)
        xh = xv * r
        dg2_ref[...] += _colsum8(dh * xh)
        dxh = dh * g2_ref[...]
        dx2_ref[...] = dx3_late_ref[...] + r * (dxh - xh * jnp.mean(dxh * xh, axis=-1, keepdims=True))

    def tile(n, lag):
        return pl.BlockSpec((tm, n), lambda i: (nt - 1 - jnp.clip(i - lag, 0, nt - 1), 0))

    outs = [
        jax.ShapeDtypeStruct((t_len, D_MODEL), F32),
        jax.ShapeDtypeStruct((t_len, 2 * D_FF), BF16),
        jax.ShapeDtypeStruct((t_len, D_MODEL), BF16),
        jax.ShapeDtypeStruct((8, D_MODEL), F32),
        jax.ShapeDtypeStruct((8, D_FF), F32),
        jax.ShapeDtypeStruct((3, 8, D_FF), F32),
    ]
    return _staged_call(
        core, name=f"ffn_bwd_l{layer}", grid=(nt + 2,),
        in_specs=[tile(D_MODEL, 0), tile(D_MODEL, 2), tile(D_MODEL, 2), tile(2 * D_FF, 1), tile(D_FF, 1), tile(D_FF, 1),
                  _const_spec((1, D_MODEL)), _const_spec((8, D_FF)), ANY, ANY],
        out_specs=[tile(D_MODEL, 2), tile(2 * D_FF, 1), tile(D_MODEL, 0),
                   _const_spec((8, D_MODEL)), _const_spec((8, D_FF)), _const_spec((3, 8, D_FF))],
        out_shape=outs,
        scratch_shapes=[pltpu.VMEM((D_MODEL, 2 * D_FF), BF16), pltpu.VMEM((D_FF, D_MODEL), BF16),
                        pltpu.VMEM((8, D_FF), F32), pltpu.VMEM((tm, D_FF), F32), pltpu.VMEM((tm, 2 * D_FF), BF16),
                        pltpu.SemaphoreType.DMA((8,))],
        args=[dx3, dx3, x2, up, silu, dsilu, g2, wfc, wup_g, wdown_g], stages=stages)


def _mixer_bwd(layer, dx2, x, zc, qs, sa, ca, sb, cb, ug, fu, xhs, cv, g1, lng, lnb, wmt, wsc, win_g, wb_g, wout_g,
               stages):
    t_len = x.shape[0]
    tm = min(TM_MIX, t_len)
    nt = t_len // tm
    nb = tm // GMLP_BLOCK

    def core(dx2_ref, x_ref, zc_ref, q_ref, sa_ref, ca_ref, sb_ref, cb_ref, ug_ref, fu_ref, xh_ref, cv_ref,
             g1_ref, lng_ref, lnb_ref, wmt_ref, wsc_ref, win_hbm, wb_hbm, wout_hbm,
             dx_ref, dz_ref, da_ref, db_ref, dx2b_ref, dg1_ref, dbgate_ref, dlng_ref, dlnb_ref, dwm_ref, dbsf_ref, dwsc_ref,
             win_v, wb_v, wout_v, carry, vn_s, df_s, dvn_s, sems):
        i = pl.program_id(0)

        @pl.when(i == 0)
        def _():
            cps = (_load_col_sharded(win_hbm, win_v, sems, 0) + _load_branch(wb_hbm, wb_v, sems, 4)
                   + _load_row_sharded(wout_hbm, wout_v, sems, 12))
            _start_all(cps)
            for ref in (carry, dg1_ref, dbgate_ref, dlng_ref, dlnb_ref, dwm_ref, dbsf_ref, dwsc_ref):
                ref[...] = jnp.zeros_like(ref)
            _wait_all(cps)

        def kept(k):
            return zc_ref[:, k * D_B:(k + 1) * D_B].astype(F32)

        def dz_cols(c0, n, val):
            dz_ref[:, c0:c0 + n] = val.astype(BF16)
            return _dot_nt(dz_ref[:, c0:c0 + n], win_v[:, c0:c0 + n])

        dx2b_ref[...] = dx2_ref[...].astype(BF16)
        dm = _dot_nt(dx2b_ref[...], wout_v[...])
        da_ref[...] = (dm * sa_ref[...].astype(F32)).astype(BF16)
        dga = dm * ca_ref[...].astype(F32)
        dh = dz_cols(C_GA, D_MODEL, dga)
        dbgate_ref[:, 0:D_MODEL] += _colsum8(dga)
        dya = _dot_nt(da_ref[...], wb_v[0])
        db_ref[...] = (dm * sb_ref[...].astype(F32)).astype(BF16)
        dgb = dm * cb_ref[...].astype(F32)
        dh = dh + dz_cols(C_GB, D_MODEL, dgb)
        dbgate_ref[:, D_MODEL:2 * D_MODEL] += _colsum8(dgb)
        dyb = _dot_nt(db_ref[...], wb_v[1])

        xh = xh_ref[...].astype(F32)
        vn_s[...] = (xh * lng_ref[...] + lnb_ref[...]).astype(BF16)
        df = dya * ug_ref[...].astype(F32)
        df_s[...] = df.astype(BF16)
        dbsf_acc = df[0:128, :]
        for b in range(1, nb):
            dbsf_acc = dbsf_acc + df[b * 128:(b + 1) * 128, :]
        dbsf_ref[...] += dbsf_acc
        for hd in range(A_HEADS):
            cols = slice(hd * 128, (hd + 1) * 128)
            vcat = jnp.concatenate([vn_s[b * 128:(b + 1) * 128, cols] for b in range(nb)], axis=1)
            dcat = jnp.concatenate([df_s[b * 128:(b + 1) * 128, cols] for b in range(nb)], axis=1)
            gcat = _dot(wmt_ref[hd], dcat)
            dwm_ref[hd] += _dot_nt(dcat, vcat)
            for b in range(nb):
                dvn_s[b * 128:(b + 1) * 128, cols] = gcat[:, b * 128:(b + 1) * 128]
        dh = dh + dz_cols(C_U, D_A, dya * fu_ref[...].astype(F32))
        dvn = dvn_s[...]
        dlng_ref[...] += _colsum8(dvn * xh)
        dlnb_ref[...] += _colsum8(dvn)
        dxh = dvn * lng_ref[...]
        dvc = dxh - jnp.mean(dxh, axis=-1, keepdims=True) - xh * jnp.mean(dxh * xh, axis=-1, keepdims=True)
        dh = dh + dz_cols(C_V, D_A, dvc * cv_ref[...].astype(F32))

        cg = kept(1)
        hbv = kept(2)
        p = cg * hbv
        dh = dh + dz_cols(C_BG, D_B, dyb * q_ref[...].astype(F32))
        dq = dyb * kept(0)
        cr = carry[...]
        dq1 = _shift_up(dq, cr, 1)
        dq2 = _shift_up(dq, cr, 2)
        carry[...] = dq[0:8, :]
        dwsc_ref[0] += _colsum8(dq2 * p)
        dwsc_ref[1] += _colsum8(dq1 * p)
        dwsc_ref[2] += _colsum8(dq * p)
        dp = wsc_ref[2:3, :] * dq + wsc_ref[1:2, :] * dq1 + wsc_ref[0:1, :] * dq2
        dh = dh + dz_cols(C_CG, D_B, dp * hbv)
        dh = dh + dz_cols(C_HB, D_B, dp * cg)

        xv = x_ref[...]
        r = lax.rsqrt(jnp.mean(xv * xv, axis=-1, keepdims=True) + RMS_EPS)
        xn = xv * r
        dg1_ref[...] += _colsum8(dh * xn)
        dxn = dh * g1_ref[...]
        dx_ref[...] = dx2_ref[...] + r * (dxn - xn * jnp.mean(dxn * xn, axis=-1, keepdims=True))

    outs = [
        jax.ShapeDtypeStruct((t_len, D_MODEL), F32),
        jax.ShapeDtypeStruct((t_len, D_IN), BF16),
        jax.ShapeDtypeStruct((t_len, D_MODEL), BF16),
        jax.ShapeDtypeStruct((t_len, D_MODEL), BF16),
        jax.ShapeDtypeStruct((t_len, D_MODEL), BF16),
        jax.ShapeDtypeStruct((8, D_MODEL), F32),
        jax.ShapeDtypeStruct((8, 2 * D_MODEL), F32),
        jax.ShapeDtypeStruct((8, D_A), F32),
        jax.ShapeDtypeStruct((8, D_A), F32),
        jax.ShapeDtypeStruct((A_HEADS, 128, 128), F32),
        jax.ShapeDtypeStruct((128, D_A), F32),
        jax.ShapeDtypeStruct((3, 8, D_B), F32),
    ]

    return _staged_call(
        core, name=f"mixer_bwd_l{layer}", grid=(nt,),
        in_specs=[_row_spec(tm, D_MODEL, nt), _row_spec(tm, D_MODEL, nt), _row_spec(tm, 3 * D_B, nt),
                  _row_spec(tm, D_B, nt), _row_spec(tm, D_MODEL, nt), _row_spec(tm, D_MODEL, nt),
                  _row_spec(tm, D_MODEL, nt), _row_spec(tm, D_MODEL, nt), _row_spec(tm, D_A, nt), _row_spec(tm, D_A, nt),
                  _row_spec(tm, D_A, nt), _row_spec(tm, D_A, nt),
                  _const_spec((1, D_MODEL)), _const_spec((1, D_A)), _const_spec((1, D_A)),
                  _const_spec((A_HEADS, 128, 128)), _const_spec((8, D_B)), ANY, ANY, ANY],
        out_specs=[_row_spec(tm, D_MODEL, nt), _row_spec(tm, D_IN, nt), _row_spec(tm, D_MODEL, nt),
                   _row_spec(tm, D_MODEL, nt), _row_spec(tm, D_MODEL, nt),
                   _const_spec((8, D_MODEL)), _const_spec((8, 2 * D_MODEL)), _const_spec((8, D_A)), _const_spec((8, D_A)),
                   _const_spec((A_HEADS, 128, 128)), _const_spec((128, D_A)), _const_spec((3, 8, D_B))],
        out_shape=outs,
        scratch_shapes=[pltpu.VMEM((D_MODEL, D_IN), BF16), pltpu.VMEM((2, D_A, D_MODEL), BF16),
                        pltpu.VMEM((D_MODEL, D_MODEL), BF16), pltpu.VMEM((8, D_B), F32),
                        pltpu.VMEM((tm, D_A), BF16), pltpu.VMEM((tm, D_A), BF16), pltpu.VMEM((tm, D_A), F32),
                        pltpu.SemaphoreType.DMA((16,))],
        args=[dx2, x, zc, qs, sa, ca, sb, cb, ug, fu, xhs, cv, g1, lng, lnb, wmt, wsc, win_g, wb_g, wout_g],
        stages=stages)


def _wgrad(name, layer, a, b, rows, cols, row_blk, col_blk, stages):
    t_len, m = a.shape
    n = b.shape[1]
    tk = min(TK_WGRAD, t_len)
    col_sharded = n == N_CHIPS * cols
    grid = (m // row_blk, n // col_blk, t_len // tk)
    shards = col_blk // cols if col_sharded else 1

    if col_sharded:
        out_shape = (N_CHIPS, rows, cols)
        out_spec = pl.BlockSpec((shards, row_blk, cols), lambda i, j, k: (j, i, 0))
    else:
        out_shape = (N_CHIPS * rows, cols)
        out_spec = pl.BlockSpec((row_blk, col_blk), lambda i, j, k: (i, j))

    def core(a_ref, b_ref, o_ref):
        @pl.when(pl.program_id(2) == 0)
        def _():
            o_ref[...] = jnp.zeros_like(o_ref)

        g = _dot_tn(a_ref[...], b_ref[...])
        if col_sharded:
            for q in range(shards):
                o_ref[q] += g[:, q * cols:(q + 1) * cols]
        else:
            o_ref[...] += g

    own, outs = _staged_call(
        core, name=f"wgrad_{name}_l{layer}", grid=grid,
        in_specs=[pl.BlockSpec((tk, row_blk), lambda i, j, k: (k, i)), pl.BlockSpec((tk, col_blk), lambda i, j, k: (k, j))],
        out_specs=[out_spec], out_shape=[jax.ShapeDtypeStruct(out_shape, F32)], scratch_shapes=[],
        args=[a, b], stages=stages)
    return [own[0].reshape(N_CHIPS, rows, cols)], outs


def _wgrad_branch(layer, ya, da, yb, db, stages):
    t_len = ya.shape[0]
    tk = min(TK_WGRAD, t_len)

    cs = D_MODEL // N_CHIPS

    def core(ya_ref, da_ref, yb_ref, db_ref, o_ref):
        @pl.when(pl.program_id(0) == 0)
        def _():
            o_ref[...] = jnp.zeros_like(o_ref)

        ga = _dot_tn(ya_ref[...], da_ref[...])
        gb = _dot_tn(yb_ref[...], db_ref[...])
        for k in range(N_CHIPS):
            o_ref[k, 0:D_A, :] += ga[:, k * cs:(k + 1) * cs]
            o_ref[k, D_A:2 * D_A, :] += gb[:, k * cs:(k + 1) * cs]

    a_spec = pl.BlockSpec((tk, D_A), lambda k: (k, 0))
    d_spec = pl.BlockSpec((tk, D_MODEL), lambda k: (k, 0))
    return _staged_call(
        core, name=f"wgrad_w_branch_l{layer}", grid=(t_len // tk,),
        in_specs=[a_spec, d_spec, a_spec, d_spec],
        out_specs=[pl.BlockSpec((N_CHIPS, 2 * D_A, cs), lambda k: (0, 0, 0))],
        out_shape=[jax.ShapeDtypeStruct((N_CHIPS, 2 * D_A, cs), F32)], scratch_shapes=[],
        args=[ya, da, yb, db], stages=stages)


def _flat_blk(rows, cols):
    blk = rows
    while blk * cols * 4 > 2 * 1024 * 1024 and blk % 16 == 0:
        blk //= 2
    return blk


def _cast_into_slots(name, jobs, chip, stages):
    blks = [_flat_blk(w.shape[1], w.shape[2]) for w, _ in jobs]
    nblks = [w.shape[1] // b for (w, _), b in zip(jobs, blks)]
    n = len(jobs)
    out_shape = [jax.ShapeDtypeStruct((N_CHIPS,) + w.shape[1:], BF16) for w, _ in jobs]

    def core(*refs):
        for w_ref, o_ref in zip(refs[-2 * n:-n], refs[-n:]):
            o_ref[...] = w_ref[...].astype(BF16)

    def slot(*scalars):
        return scalars[0][0] if scalars else 2 * lax.axis_index("x") + lax.axis_index("y")

    in_specs = [pl.BlockSpec((None, b, w.shape[2]), lambda i, *s, la=la, k=k: (la, jnp.minimum(i, k - 1), 0))
                for (w, la), b, k in zip(jobs, blks, nblks)]
    out_specs = [pl.BlockSpec((None, b, w.shape[2]), lambda i, *s, k=k: (slot(*s), jnp.minimum(i, k - 1), 0))
                 for (w, _), b, k in zip(jobs, blks, nblks)]
    args = [w for w, _ in jobs]
    if stages:
        return _staged_call(core, name=f"cast_{name}", grid=(max(nblks),), in_specs=in_specs, out_specs=out_specs,
                            out_shape=out_shape, scratch_shapes=[], args=args, stages=stages)
    own = pl.pallas_call(
        core, name=f"cast_{name}",
        grid_spec=pltpu.PrefetchScalarGridSpec(num_scalar_prefetch=1, grid=(max(nblks),), in_specs=in_specs,
                                               out_specs=out_specs),
        out_shape=out_shape, compiler_params=_params(),
    )(chip, *args)
    return list(own), []


def _reduction_sums(name, jobs, pos):
    in_specs, out_specs, out_shape, args, bodies, counts = [], [], [], [], [], []
    for job in jobs:
        kind, grad, other = job[0], job[1], job[2]
        _, h, cols = other.shape
        blk = _flat_blk(h, cols)
        nblk = h // blk
        if kind == "pair":
            total = N_CHIPS * nblk

            def block(s, total=total, nblk=nblk):
                b = jnp.minimum(s, total - 1)
                return b // nblk, b % nblk

            spec = pl.BlockSpec((None, blk, cols), lambda s, p, block=block: (block(s)[0], block(s)[1], 0))
            in_specs += [pl.BlockSpec((None, blk, cols), lambda s, p, block=block, nblk=nblk:
                                      (block(s)[0], p[1] * nblk + block(s)[1], 0)), spec]
            out_specs.append(spec)
            out_shape.append(jax.ShapeDtypeStruct((N_CHIPS, h, cols), BF16))
            args += [grad, other]
            bodies.append((2, lambda g, o, out: out.__setitem__(..., (g[...] + o[...]).astype(BF16))))
        else:
            total = nblk

            def block(s, total=total):
                return jnp.minimum(s, total - 1)

            in_specs += [pl.BlockSpec((None, blk, cols), lambda s, p, block=block, nblk=nblk:
                                      (p[0], p[1] * nblk + block(s), 0)),
                         pl.BlockSpec((None, blk, cols), lambda s, p, block=block: (p[0], block(s), 0)),
                         pl.BlockSpec((3, blk, cols), lambda s, p, block=block: (0, block(s), 0))]
            out_specs.append(pl.BlockSpec((blk, cols), lambda s, p, block=block, nblk=nblk: (p[1] * nblk + block(s), 0)))
            out_shape.append(jax.ShapeDtypeStruct((2 * h, cols), F32))
            args += [grad, other, job[3]]
            bodies.append((3, lambda g, o, r, out: out.__setitem__(
                ..., (((g[...] + o[...]) + r[0].astype(F32)) + r[1].astype(F32)) + r[2].astype(F32))))
        counts.append(total)

    def body(pos_ref, *refs):
        ins, outs = refs[:len(args)], refs[len(args):]
        k = 0
        for (n_in, fn), out in zip(bodies, outs):
            fn(*ins[k:k + n_in], out)
            k += n_in

    return pl.pallas_call(
        body, name=f"reduction_sums_{name}",
        grid_spec=pltpu.PrefetchScalarGridSpec(num_scalar_prefetch=1, grid=(max(counts),), in_specs=in_specs,
                                               out_specs=out_specs),
        out_shape=out_shape,
        compiler_params=_params(),
    )(pos, *args)


def _sum_slots(name, slots):
    n, rows, _ = slots.shape

    def body(s_ref, o_ref):
        acc = s_ref[0]
        for d in range(1, n):
            acc = acc + s_ref[d]
        o_ref[...] = acc

    return pl.pallas_call(
        body, name=f"sum_slots_{name}", grid=(1,),
        in_specs=[pl.BlockSpec((n, rows, 128), lambda i: (0, 0, 0))],
        out_specs=pl.BlockSpec((rows, 128), lambda i: (0, 0)),
        out_shape=jax.ShapeDtypeStruct((rows, 128), F32),
        compiler_params=_params(),
    )(slots)


def _adamw_math(w, g, m, v):
    m2 = ADAM_B1 * m + (1.0 - ADAM_B1) * g
    v2 = ADAM_B2 * v + (1.0 - ADAM_B2) * (g * g)
    m_hat = m2 / (1.0 - ADAM_B1 ** ADAM_STEP)
    v_hat = v2 / (1.0 - ADAM_B2 ** ADAM_STEP)
    delta = -ADAM_LR * (m_hat / (jnp.sqrt(v_hat) + ADAM_EPS) + ADAM_WD * w)
    return delta, m2, v2


def _adamw_big(name, w, g0, g1, m, v):
    _, rows, cols = w.shape
    blk = _flat_blk(rows, cols) // 2

    def body(w_ref, g0_ref, g1_ref, m_ref, v_ref, g_ref, d_ref, m2_ref, v2_ref):
        g = jnp.where(pl.program_id(0) == 0, g0_ref[...], g1_ref[...])
        d, m2, v2 = _adamw_math(w_ref[...], g, m_ref[...], v_ref[...])
        g_ref[...] = g
        d_ref[...] = d
        m2_ref[...] = m2
        v2_ref[...] = v2

    spec = pl.BlockSpec((None, blk, cols), lambda la, i: (la, i, 0))
    return pl.pallas_call(
        body, name=f"adamw_{name}", grid=(N_LAYERS, rows // blk),
        in_specs=[spec, pl.BlockSpec((blk, cols), lambda la, i: (i * (1 - la), 0)),
                  pl.BlockSpec((blk, cols), lambda la, i: (i * la, 0)), spec, spec],
        out_specs=[spec] * 4,
        out_shape=[jax.ShapeDtypeStruct(w.shape, F32)] * 4,
        compiler_params=_params(("parallel", "parallel")),
    )(w, g0, g1, m, v)


def _adamw_small(ws, gs, ms, vs):
    n = len(ws)

    def body(*refs):
        ins, outs = refs[:4 * n], refs[4 * n:]
        for k in range(n):
            d, m2, v2 = _adamw_math(ins[k][...], ins[n + k][...], ins[2 * n + k][...], ins[3 * n + k][...])
            outs[k][...] = d
            outs[n + k][...] = m2
            outs[2 * n + k][...] = v2

    vmem = pl.BlockSpec(memory_space=pltpu.VMEM)
    return pl.pallas_call(
        body, name="adamw_small",
        in_specs=[vmem] * (4 * n), out_specs=[vmem] * (3 * n),
        out_shape=[jax.ShapeDtypeStruct(w.shape, F32) for w in ws] * 3,
        compiler_params=pltpu.CompilerParams(vmem_limit_bytes=V7X_VMEM_LIMIT),
    )(*ws, *gs, *ms, *vs)


SMALL = ("norm1_g", "b_gate", "gmlp_ln_g", "gmlp_ln_b", "w_spatial", "b_spatial", "w_shortconv", "norm2_g",
         "w_ffn_conv", "b_ffn_conv", "final_g")
ALL_WEIGHTS = ("norm1_g", "w_in", "b_gate", "gmlp_ln_g", "gmlp_ln_b", "w_spatial", "b_spatial", "w_shortconv",
               "w_branch", "w_out", "norm2_g", "w_ffn_up", "w_ffn_conv", "b_ffn_conv", "w_ffn_down", "final_g")


def _pack(arrays):
    flat = jnp.concatenate([a.reshape(-1) for a in arrays])
    n = flat.shape[0]
    rows = -(-n // 1024) * 8
    return jnp.pad(flat, (0, rows * 128 - n)).reshape(rows, 128)


def _unpack(packed, like):
    flat = packed.reshape(-1)
    out, off = [], 0
    for a in like:
        out.append(flat[off:off + a.size].reshape(a.shape))
        off += a.size
    return out


def _pad8(w):
    return jnp.pad(w, ((0, 5), (0, 0)))


def kernel(x, norm1_g, w_in, b_gate, gmlp_ln_g, gmlp_ln_b, w_spatial, b_spatial, w_shortconv, w_branch, w_out, norm2_g, w_ffn_up, w_ffn_conv, b_ffn_conv, w_ffn_down, final_g, loss_target, m_norm1_g, m_w_in, m_b_gate, m_gmlp_ln_g, m_gmlp_ln_b, m_w_spatial, m_b_spatial, m_w_shortconv, m_w_branch, m_w_out, m_norm2_g, m_w_ffn_up, m_w_ffn_conv, m_b_ffn_conv, m_w_ffn_down, m_final_g, v_norm1_g, v_w_in, v_b_gate, v_gmlp_ln_g, v_gmlp_ln_b, v_w_spatial, v_b_spatial, v_w_shortconv, v_w_branch, v_w_out, v_norm2_g, v_w_ffn_up, v_w_ffn_conv, v_b_ffn_conv, v_w_ffn_down, v_final_g):
    weights = dict(norm1_g=norm1_g, w_in=w_in, b_gate=b_gate, gmlp_ln_g=gmlp_ln_g, gmlp_ln_b=gmlp_ln_b,
                   w_spatial=w_spatial, b_spatial=b_spatial, w_shortconv=w_shortconv, w_branch=w_branch, w_out=w_out,
                   norm2_g=norm2_g, w_ffn_up=w_ffn_up, w_ffn_conv=w_ffn_conv, b_ffn_conv=b_ffn_conv,
                   w_ffn_down=w_ffn_down, final_g=final_g)
    mom = dict(norm1_g=m_norm1_g, w_in=m_w_in, b_gate=m_b_gate, gmlp_ln_g=m_gmlp_ln_g, gmlp_ln_b=m_gmlp_ln_b,
               w_spatial=m_w_spatial, b_spatial=m_b_spatial, w_shortconv=m_w_shortconv, w_branch=m_w_branch,
               w_out=m_w_out, norm2_g=m_norm2_g, w_ffn_up=m_w_ffn_up, w_ffn_conv=m_w_ffn_conv,
               b_ffn_conv=m_b_ffn_conv, w_ffn_down=m_w_ffn_down, final_g=m_final_g)
    vel = dict(norm1_g=v_norm1_g, w_in=v_w_in, b_gate=v_b_gate, gmlp_ln_g=v_gmlp_ln_g, gmlp_ln_b=v_gmlp_ln_b,
               w_spatial=v_w_spatial, b_spatial=v_b_spatial, w_shortconv=v_w_shortconv, w_branch=v_w_branch,
               w_out=v_w_out, norm2_g=v_norm2_g, w_ffn_up=v_w_ffn_up, w_ffn_conv=v_w_ffn_conv,
               b_ffn_conv=v_b_ffn_conv, w_ffn_down=v_w_ffn_down, final_g=v_final_g)

    cx, cy, cc = _mesh_pos()
    chip = 2 * cx + cy
    pos_arr = jnp.stack([chip, cc]).astype(jnp.int32)
    t_len = x.shape[1]
    xs = x.reshape(t_len, D_MODEL)
    target = loss_target.reshape(t_len, D_MODEL)
    pipe = _Pipe()

    full = {}

    mixer_w = ("w_in", "w_branch", "w_out")
    ffn_w = ("w_ffn_up", "w_ffn_down")
    slots = {}

    def cast(name, keys, stages):
        own, outs = _cast_into_slots(name, [(weights[n].reshape((N_LAYERS,) + BIG[n]), la) for n, la in keys],
                                     chip.astype(jnp.int32).reshape(1), stages)
        slots.update(zip(keys, own))
        return own, outs

    def gather(names, la):
        def then(*bufs):
            full.update(zip([(n, la) for n in names], bufs))

        pipe.add(_gather_stage([slots[(n, la)] for n in names], then))

    first = [(n, 0) for n in mixer_w]
    cast("first", first, [])
    gather(mixer_w, 0)
    tap_slots = {}
    pipe.add(_chip_spread_stage(_pack([w_shortconv, w_ffn_conv]), lambda got: tap_slots.__setitem__("all", got)))
    pipe.carry(lambda st: cast("rest", [(n, la) for la in range(N_LAYERS) for n in BIG_NAMES if (n, la) not in first], st))
    by_chip = [_unpack(tap_slots["all"][k], [w_shortconv, w_ffn_conv]) for k in range(N_CHIPS)]
    wsc_full = jnp.concatenate([t[0] for t in by_chip], axis=-1)
    wfc_full = jnp.concatenate([t[1] for t in by_chip], axis=-1)

    idx = jnp.arange(GMLP_BLOCK) // CHUNK
    mask = idx[None, :] <= idx[:, None]
    wm_all = jnp.where(mask[None, None], w_spatial, 0.0)
    wm_bf = wm_all.astype(BF16)
    wmt_bf = jnp.swapaxes(wm_all, -1, -2).astype(BF16)
    bsf = jnp.repeat(jnp.swapaxes(b_spatial, -1, -2), 128, axis=-1)

    def row(a):
        return a.reshape(1, -1)

    def mixer_args(la):
        return (row(norm1_g[la]), row(b_gate[la]), row(gmlp_ln_g[la]), row(gmlp_ln_b[la]))

    def mixer_weights(la):
        return tuple(full[(n, la)] for n in mixer_w)

    def ffn_weights(la):
        return tuple(full[(n, la)] for n in ffn_w)

    saved = []
    h_in = xs
    for la in range(N_LAYERS):
        gather(ffn_w, la)
        *kept, mg, h1, x2 = pipe.carry(lambda st: _mixer_fwd(
            la, h_in, *mixer_args(la), wm_bf[la], bsf[la], _pad8(wsc_full[la]), *mixer_weights(la), st))
        ya, yb = kept[1], kept[2]
        if la + 1 < N_LAYERS:
            gather(mixer_w, la + 1)
        head = (target, row(final_g)) if la == N_LAYERS - 1 else None
        up, silu, dsilu, act, h2, *rest = pipe.carry(lambda st: _ffn_fwd(
            la, x2, row(norm2_g[la]), _pad8(wfc_full[la]), row(b_ffn_conv[la]), *ffn_weights(la), st, head=head))
        saved.append(dict(x=h_in, ya=ya, yb=yb, mixer=[kept[0]] + kept[3:], mg=mg, h1=h1, x2=x2, up=up, silu=silu,
                          dsilu=dsilu, act=act, h2=h2))
        h_in = rest[0]
    dx, dgf8, loss8 = rest

    reduced_big = {}

    sums_due = []

    def run_sums():
        if sums_due:
            due = list(sums_due)
            sums_due.clear()
            run_sums.calls += 1
            for (_, then), res in zip(due, _reduction_sums(str(run_sums.calls), [job for job, _ in due], pos_arr)):
                then(res)

    run_sums.calls = 0
    pipe.after = run_sums

    def reduce_big(name, la, grad):
        def after_pair(other):
            def after_chips(got):
                sums_due.append((("chip", grad, other, got), lambda final: pipe.add(_pair_fill_stage(
                    final, lambda done: reduced_big.__setitem__((name, la), done)))))

            sums_due.append((("pair", grad, other), lambda psum: pipe.add(_chip_send_stage(psum, after_chips))))

        pipe.add(_pair_send_stage(grad, after_pair))

    small = {n: [None] * N_LAYERS for n in SMALL}
    spread = {}
    for la in reversed(range(N_LAYERS)):
        s = saved[la]
        dx3 = dx
        dx2, dup, dx3b, dg2, dbfc, dwfc = pipe.carry(lambda st: _ffn_bwd(
            la, dx3, s["x2"], s["up"], s["silu"], s["dsilu"], row(norm2_g[la]), _pad8(wfc_full[la]),
            *ffn_weights(la), st))
        g, = pipe.carry(lambda st: _wgrad("w_ffn_up", la, s["h2"], dup, 1024, 1408, 512, 2816, st))
        reduce_big("w_ffn_up", la, g)

        def wgrad_down():
            g, = pipe.carry(lambda st: _wgrad("w_ffn_down", la, s["act"], dx3b, 704, 1024, 1408, 1024, st))
            reduce_big("w_ffn_down", la, g)

        if la > 0:
            wgrad_down()
        run = pipe.carry if la > 0 else (lambda call: call([])[0])
        dxl, dz, da, db, dx2b, dg1, dbg, dlng, dlnb, dwm, dbsf, dwsc = run(lambda st: _mixer_bwd(
            la, dx2, s["x"], *s["mixer"], row(norm1_g[la]), row(gmlp_ln_g[la]), row(gmlp_ln_b[la]), wmt_bf[la],
            _pad8(wsc_full[la]), *mixer_weights(la), st))
        small["norm1_g"][la] = dg1.sum(0)
        small["b_gate"][la] = dbg.sum(0)
        small["gmlp_ln_g"][la] = dlng.sum(0)
        small["gmlp_ln_b"][la] = dlnb.sum(0)
        small["w_spatial"][la] = jnp.where(mask[None], dwm, 0.0)
        small["b_spatial"][la] = dbsf.reshape(128, A_HEADS, 128).sum(-1).T
        small["w_shortconv"][la] = dwsc.sum(1)
        small["norm2_g"][la] = dg2.sum(0)
        small["w_ffn_conv"][la] = dwfc.sum(1)
        small["b_ffn_conv"][la] = dbfc.sum(0)
        if la == 0:
            small_local = ([jnp.stack(small[n]) for n in SMALL[:-1]]
                           + [dgf8.sum(0), 0.5 * loss8.sum().reshape(1) / D_MODEL])
            mine = _pack(small_local)

            def after_swap(other, mine=mine):
                pair = _sum_slots("small_pair", jnp.stack([mine, other]))
                pipe.add(_chip_spread_stage(pair, lambda slots: spread.__setitem__("slots", slots)))

            pipe.add(_pair_swap_stage(mine, after_swap))
        g, = pipe.carry(lambda st: _wgrad("w_in", la, s["h1"], dz, 1024, 1152, 512, 2304, st))
        reduce_big("w_in", la, g)
        if la == 0:
            wgrad_down()
        g, = pipe.carry(lambda st: _wgrad("w_out", la, s["mg"], dx2b, 256, 1024, 1024, 1024, st), long=False)
        reduce_big("w_out", la, g)
        g, = pipe.carry(lambda st: _wgrad_branch(la, s["ya"], da, s["yb"], db, st), long=False)
        reduce_big("w_branch", la, g)
        dx = dxl
    grad_x = dx.reshape(x.shape)
    pipe.flush()

    reduced = _unpack(_sum_slots("small_grads", spread["slots"]), small_local)
    loss = reduced[-1].reshape(())
    grads = dict(zip(SMALL, reduced[:-1]))
    grads["w_shortconv"] = lax.dynamic_slice(grads["w_shortconv"], (0, 0, chip * (D_B // 4)), (N_LAYERS, 3, D_B // 4))
    grads["w_ffn_conv"] = lax.dynamic_slice(grads["w_ffn_conv"], (0, 0, chip * (D_FF // 4)), (N_LAYERS, 3, D_FF // 4))

    delta, new_m, new_v = {}, {}, {}
    for n in BIG_NAMES:
        shape3 = (N_LAYERS,) + BIG[n]
        res = _adamw_big(n, weights[n].reshape(shape3), reduced_big[(n, 0)], reduced_big[(n, 1)],
                         mom[n].reshape(shape3), vel[n].reshape(shape3))
        grads[n], delta[n], new_m[n], new_v[n] = (a.reshape(weights[n].shape) for a in res)
    res = _adamw_small(*[[src[n].reshape(-1, src[n].shape[-1]) for n in SMALL] for src in (weights, grads, mom, vel)])
    for k, n in enumerate(SMALL):
        delta[n], new_m[n], new_v[n] = (res[j * len(SMALL) + k].reshape(weights[n].shape) for j in range(3))

    return (loss, grad_x, *[grads[n] for n in ALL_WEIGHTS], *[delta[n] for n in ALL_WEIGHTS],
            *[new_m[n] for n in ALL_WEIGHTS], *[new_v[n] for n in ALL_WEIGHTS])
```

```python
import jax
import jax.numpy as jnp
from jax import lax
from jax.experimental import pallas as pl
from jax.experimental.pallas import tpu as pltpu

F32 = jnp.float32
BF16 = jnp.bfloat16
MESH = pl.DeviceIdType.MESH
ANY = pl.BlockSpec(memory_space=pl.ANY)

D_MODEL = 1024
D_A = 512
D_B = 512
D_IN = 4608
D_FF = 2816
GMLP_BLOCK = 128
CHUNK = 64
A_HEADS = 4
N_LAYERS = 2
N_CHIPS = 4
RMS_EPS = 1e-6
LN_EPS = 1e-5
ADAM_LR = 0.001
ADAM_B1 = 0.9
ADAM_B2 = 0.999
ADAM_EPS = 1e-08
ADAM_WD = 0.01
ADAM_STEP = 10

C_U, C_V, C_BG, C_CG, C_HB, C_GA, C_GB = 0, 512, 1024, 1536, 2048, 2560, 3584

V7X_VMEM_LIMIT = 60 * 1024 * 1024
TM_MIX = 256
TM_FFN = 256
TK_WGRAD = 2048
SLOW_COPY_BYTES = 640 * 1024
FF_CHUNKS = ((0, 768), (768, 1536), (1536, 2304), (2304, 2816))
GELU_C0 = 0.7978845608028654
GELU_C1 = 0.044715

BIG = {
    "w_in": (1024, 1152),
    "w_branch": (1024, 256),
    "w_out": (256, 1024),
    "w_ffn_up": (1024, 1408),
    "w_ffn_down": (704, 1024),
}
BIG_NAMES = tuple(BIG)


def _params(sem=("arbitrary",), vmem=V7X_VMEM_LIMIT):
    return pltpu.CompilerParams(dimension_semantics=sem, vmem_limit_bytes=vmem)


def _gelu(x):
    x2 = x * x
    t = jnp.tanh(GELU_C0 * x * (1.0 + GELU_C1 * x2))
    return 0.5 * x * (1.0 + t), t


def _gelu_grad(x, t):
    return 0.5 * (1.0 + t) + 0.5 * x * (1.0 - t * t) * GELU_C0 * (1.0 + 3.0 * GELU_C1 * x * x)


def _colsum8(v):
    r, n = v.shape
    return v.reshape(r // 8, 8, n).sum(axis=0)


def _dot(a, b):
    return jnp.dot(a, b, preferred_element_type=F32)


def _dot_nt(a, b):
    return lax.dot_general(a, b, (((1,), (1,)), ((), ())), preferred_element_type=F32)


def _dot_tn(a, b):
    return lax.dot_general(a, b, (((0,), (0,)), ((), ())), preferred_element_type=F32)


def _shift_down(v, carry, n):
    rows = lax.broadcasted_iota(jnp.int32, (8, v.shape[1]), 0)
    out = pltpu.roll(v, n, 0)
    head = out[0:8, :]
    for r in range(n):
        head = jnp.where(rows == r, carry[8 - n + r:8 - n + r + 1, :], head)
    return jnp.concatenate([head, out[8:, :]], axis=0)


def _shift_up(v, carry, n):
    tm = v.shape[0]
    rows = lax.broadcasted_iota(jnp.int32, (8, v.shape[1]), 0)
    out = pltpu.roll(v, tm - n, 0)
    tail = out[tm - 8:tm, :]
    for r in range(n):
        tail = jnp.where(rows == 8 - n + r, carry[r:r + 1, :], tail)
    return jnp.concatenate([out[0:tm - 8, :], tail], axis=0)


def _sigmoid(x):
    return 0.5 * jnp.tanh(0.5 * x) + 0.5


def _start_all(copies):
    for cp in copies:
        cp.start()


def _wait_all(copies):
    for cp in copies:
        cp.wait()


def _load_col_sharded(src, dst, sems, first):
    cs = src.shape[-1]
    return [pltpu.make_async_copy(src.at[k], dst.at[:, k * cs:(k + 1) * cs], sems.at[first + k])
            for k in range(N_CHIPS)]


def _load_row_sharded(src, dst, sems, first):
    rs = src.shape[-2]
    return [pltpu.make_async_copy(src.at[k], dst.at[k * rs:(k + 1) * rs, :], sems.at[first + k])
            for k in range(N_CHIPS)]


def _load_branch(src, dst, sems, first):
    return [pltpu.make_async_copy(src.at[k, pl.ds(m * D_A, D_A), :], dst.at[m, :, k * 256:(k + 1) * 256],
                                  sems.at[first + 2 * k + m])
            for k in range(N_CHIPS) for m in range(2)]


def _row_spec(tm, n, rev=None):
    if rev is None:
        return pl.BlockSpec((tm, n), lambda i: (i, 0))
    return pl.BlockSpec((tm, n), lambda i: (rev - 1 - i, 0))


def _const_spec(shape):
    nd = len(shape)
    return pl.BlockSpec(shape, lambda i: (0,) * nd)


def _mesh_pos():
    return lax.axis_index("x"), lax.axis_index("y"), lax.axis_index("c")


def _other_chips(x, y):
    return [(1 - x, y, 2 * (1 - x) + y), (x, 1 - y, 2 * x + (1 - y)), (1 - x, 1 - y, 2 * (1 - x) + (1 - y))]


def _remote(src, dst, ssem, rsem, to):
    return pltpu.make_async_remote_copy(src_ref=src, dst_ref=dst, send_sem=ssem, recv_sem=rsem, device_id=to,
                                        device_id_type=MESH)


def _half(ref, which, h):
    start = pl.multiple_of(which * h, 8)
    if len(ref.shape) == 2:
        return ref.at[pl.ds(start, h), :]
    return ref.at[:, pl.ds(start, h), :]


class _Stage:
    def __init__(self, ins=(), inouts=(), outs=(), n_sems=0, start=None, mid=None, finish=None, then=None, slow=False):
        self.ins, self.inouts, self.outs = list(ins), list(inouts), list(outs)
        self.n_sems, self.start, self.mid, self.finish, self.then = n_sems, start, mid, finish, then
        self.slow = slow


def _gather_stage(bufs, then):
    n = len(bufs)

    def copies(io, sem):
        x, y, c = _mesh_pos()
        me = 2 * x + y
        ici, fwd, got = [], [], []
        for w in range(n):
            h = io[w].shape[1] // 2
            for j, (px, py, pk) in enumerate(_other_chips(x, y)):
                mine = _half(io[w].at[me], c, h)
                theirs = _half(io[w].at[pk], c, h)
                ici.append(_remote(mine, mine, sem(12 * w + j), sem(12 * w + 3 + j), (px, py, c)))
                got.append(_remote(theirs, theirs, sem(12 * w + j), sem(12 * w + 3 + j), (px, py, c)))
                fwd.append(_remote(theirs, theirs, sem(12 * w + 6 + j), sem(12 * w + 9 + j), (x, y, 1 - c)))
        return ici, got, fwd

    def start(ins, io, outs, sem):
        _start_all(copies(io, sem)[0])

    def mid(ins, io, outs, sem):
        _, got, fwd = copies(io, sem)
        for g, f in zip(got, fwd):
            g.wait_recv()
            f.start()

    def finish(ins, io, outs, sem):
        x, y, c = _mesh_pos()
        ici, _, fwd = copies(io, sem)
        for w in range(n):
            h = io[w].shape[1] // 2
            for j, (px, py, pk) in enumerate(_other_chips(x, y)):
                other = _half(io[w].at[pk], 1 - c, h)
                _remote(other, other, sem(12 * w + 6 + j), sem(12 * w + 9 + j), (x, y, 1 - c)).wait_recv()
        for cp in ici + fwd:
            cp.wait_send()

    return _Stage(inouts=bufs, n_sems=12 * n, start=start, mid=mid, finish=finish, then=then)


def _pair_send_stage(grad, then):
    h = grad.shape[1] // 2

    def copy(ins, outs, sem):
        x, y, c = _mesh_pos()
        return _remote(_half(ins[0], 1 - c, h), outs[0], sem(0), sem(1), (x, y, 1 - c))

    return _Stage(ins=[grad], outs=[jax.ShapeDtypeStruct((N_CHIPS, h, grad.shape[2]), F32)], n_sems=2,
                  start=lambda ins, io, outs, sem: copy(ins, outs, sem).start(),
                  finish=lambda ins, io, outs, sem: copy(ins, outs, sem).wait(), then=then)


def _chip_send_stage(psum, then):
    def copies(ins, outs, sem):
        x, y, c = _mesh_pos()
        return [_remote(ins[0].at[pk], outs[0].at[j], sem(j), sem(3 + j), (px, py, c))
                for j, (px, py, pk) in enumerate(_other_chips(x, y))]

    return _Stage(ins=[psum], outs=[jax.ShapeDtypeStruct((3,) + psum.shape[1:], BF16)], n_sems=6,
                  start=lambda ins, io, outs, sem: _start_all(copies(ins, outs, sem)),
                  finish=lambda ins, io, outs, sem: _wait_all(copies(ins, outs, sem)), then=then,
                  slow=psum.shape[1] * psum.shape[2] * 2 > SLOW_COPY_BYTES)


def _pair_fill_stage(final, then):
    h = final.shape[0] // 2

    def copy(io, sem):
        x, y, c = _mesh_pos()
        mine = _half(io[0], c, h)
        return _remote(mine, mine, sem(0), sem(1), (x, y, 1 - c))

    return _Stage(inouts=[final], n_sems=2,
                  start=lambda ins, io, outs, sem: copy(io, sem).start(),
                  finish=lambda ins, io, outs, sem: copy(io, sem).wait(), then=then)


def _pair_swap_stage(packed, then):
    def copy(ins, outs, sem):
        x, y, c = _mesh_pos()
        return _remote(ins[0], outs[0], sem(0), sem(1), (x, y, 1 - c))

    return _Stage(ins=[packed], outs=[jax.ShapeDtypeStruct(packed.shape, F32)], n_sems=2,
                  start=lambda ins, io, outs, sem: copy(ins, outs, sem).start(),
                  finish=lambda ins, io, outs, sem: copy(ins, outs, sem).wait(), then=then)


def _chip_spread_stage(psum, then):
    def copies(ins, outs, sem):
        x, y, c = _mesh_pos()
        me = 2 * x + y
        cps = [_remote(ins[0], outs[0].at[me], sem(j), sem(3 + j), (px, py, c))
               for j, (px, py, pk) in enumerate(_other_chips(x, y))]
        return cps, pltpu.make_async_copy(ins[0], outs[0].at[me], sem(6))

    def start(ins, io, outs, sem):
        cps, own = copies(ins, outs, sem)
        own.start()
        _start_all(cps)

    def finish(ins, io, outs, sem):
        cps, own = copies(ins, outs, sem)
        _wait_all(cps)
        own.wait()

    return _Stage(ins=[psum], outs=[jax.ShapeDtypeStruct((N_CHIPS,) + psum.shape, F32)], n_sems=7,
                  start=start, finish=finish, then=then)


def _staged_call(core, *, name, grid, in_specs, out_specs, out_shape, scratch_shapes, args, stages):
    n_in, n_out, n_scr = len(args), len(out_shape), len(scratch_shapes)
    s_args, s_outs, aliases, layout = [], [], {}, []
    n_sems = 0
    for st in stages:
        i0, o0 = len(s_args), len(s_outs)
        s_args += st.ins + st.inouts
        for q in range(len(st.inouts)):
            aliases[n_in + i0 + len(st.ins) + q] = n_out + o0 + q
        s_outs += [jax.ShapeDtypeStruct(a.shape, a.dtype) for a in st.inouts] + st.outs
        layout.append((i0, o0, n_sems))
        n_sems += st.n_sems
    steps = 1
    for g in grid:
        steps *= g

    def body(*refs):
        own_in = refs[:n_in]
        s_in = refs[n_in:n_in + len(s_args)]
        rest = refs[n_in + len(s_args):]
        own_out = rest[:n_out]
        s_out = rest[n_out:n_out + len(s_outs)]
        scr = rest[n_out + len(s_outs):]

        def run(which):
            for st, (i0, o0, s0) in zip(stages, layout):
                fn = getattr(st, which)
                if fn is not None:
                    fn(s_in[i0:i0 + len(st.ins)], s_out[o0:o0 + len(st.inouts)],
                       s_out[o0 + len(st.inouts):o0 + len(st.inouts) + len(st.outs)],
                       lambda k, s0=s0: scr[n_scr].at[s0 + k])

        if not stages:
            core(*own_in, *own_out, *scr[:n_scr])
            return
        step = 0
        for d, g in enumerate(grid):
            step = step * g + pl.program_id(d)
        if steps == 1:
            run("start")
            core(*own_in, *own_out, *scr[:n_scr])
            run("mid")
            run("finish")
            return
        pl.when(step == 0)(lambda: run("start"))
        core(*own_in, *own_out, *scr[:n_scr])
        pl.when(step == (3 * steps) // 4)(lambda: run("mid"))
        pl.when(step == steps - 1)(lambda: run("finish"))

    sem = ("arbitrary",) * len(grid) if stages else ("parallel",) * max(len(grid) - 1, 0) + ("arbitrary",) * min(len(grid), 1)
    res = pl.pallas_call(
        body, name=name, grid=grid,
        in_specs=list(in_specs) + [ANY] * len(s_args),
        out_specs=list(out_specs) + [ANY] * len(s_outs),
        out_shape=list(out_shape) + s_outs,
        input_output_aliases=aliases,
        scratch_shapes=list(scratch_shapes) + ([pltpu.SemaphoreType.DMA((n_sems,))] if stages else []),
        compiler_params=_params(sem) if grid else pltpu.CompilerParams(vmem_limit_bytes=V7X_VMEM_LIMIT),
    )(*args, *s_args)
    return list(res[:n_out]), list(res[n_out:])


class _Pipe:
    def __init__(self):
        self.ready = []
        self.flushes = 0
        self.after = None

    def add(self, stage):
        self.ready.append(stage)

    def carry(self, call, long=True):
        stages = [st for st in self.ready if long or not st.slow]
        self.ready = [st for st in self.ready if not (long or not st.slow)]
        own, outs = call(stages)
        k = 0
        for st in stages:
            n = len(st.inouts) + len(st.outs)
            st.then(*outs[k:k + n])
            k += n
        if self.after is not None:
            self.after()
        return own

    def flush(self):
        while self.ready:
            self.flushes += 1
            self.carry(lambda stages: _staged_call(
                lambda *refs: None, name=f"comm_tail_{self.flushes}", grid=(), in_specs=[], out_specs=[], out_shape=[],
                scratch_shapes=[], args=[], stages=stages))


def _mixer_fwd(layer, x, g1, bgate, lng, lnb, wm, bsf, wsc, win_g, wb_g, wout_g, stages):
    t_len = x.shape[0]
    tm = min(TM_MIX, t_len)
    nt = t_len // tm
    nb = tm // GMLP_BLOCK

    def core(x_ref, x_late_ref, g1_ref, bgate_ref, lng_ref, lnb_ref, wm_ref, bsf_ref, wsc_ref, win_hbm, wb_hbm, wout_hbm,
             zc_ref, ya_ref, yb_ref, q_ref, sa_ref, ca_ref, sb_ref, cb_ref, ug_ref, fu_ref, xh_ref, cv_ref,
             mg_ref, h_ref, x2_ref,
             win_v, wb_v, wout_v, carry, vn_s, f_s, z_s, sems):
        i = pl.program_id(0)

        @pl.when(i == 0)
        def _():
            cps = (_load_col_sharded(win_hbm, win_v, sems, 0) + _load_branch(wb_hbm, wb_v, sems, 4)
                   + _load_row_sharded(wout_hbm, wout_v, sems, 12))
            _start_all(cps)
            carry[...] = jnp.zeros_like(carry)
            z_s[...] = jnp.zeros_like(z_s)
            _wait_all(cps)

        xv = x_ref[...]
        r = lax.rsqrt(jnp.mean(xv * xv, axis=-1, keepdims=True) + RMS_EPS)
        h_ref[...] = (xv * r * g1_ref[...]).astype(BF16)

        def zcols(c0, n, keep=None):
            zv = z_s[:, c0:c0 + n]
            z_s[:, c0:c0 + n] = _dot(h_ref[...], win_v[:, c0:c0 + n])
            if keep is not None:
                zc_ref[:, keep * D_B:(keep + 1) * D_B] = zv.astype(BF16)
            return zv

        v = zcols(C_V, D_A)
        vg, tv = _gelu(v)
        mu = jnp.mean(vg, axis=-1, keepdims=True)
        vc = vg - mu
        rstd = lax.rsqrt(jnp.mean(vc * vc, axis=-1, keepdims=True) + LN_EPS)
        xh = vc * rstd
        xh_ref[...] = xh.astype(BF16)
        cv_ref[...] = (rstd * _gelu_grad(v, tv)).astype(BF16)
        vn_s[...] = (xh * lng_ref[...] + lnb_ref[...]).astype(BF16)
        for hd in range(A_HEADS):
            cols = slice(hd * 128, (hd + 1) * 128)
            vcat = jnp.concatenate([vn_s[b * 128:(b + 1) * 128, cols] for b in range(nb)], axis=1)
            fcat = _dot(wm_ref[hd], vcat)
            for b in range(nb):
                f_s[b * 128:(b + 1) * 128, cols] = fcat[:, b * 128:(b + 1) * 128]
        u = zcols(C_U, D_A)
        ug, tu = _gelu(u)
        ug_ref[...] = ug.astype(BF16)
        fb = f_s[...] + jnp.concatenate([bsf_ref[...]] * nb, axis=0)
        fu_ref[...] = (fb * _gelu_grad(u, tu)).astype(BF16)
        ya_ref[...] = (ug * fb).astype(BF16)

        p = zcols(C_CG, D_B, keep=1) * zcols(C_HB, D_B, keep=2)
        cr = carry[...]
        q = wsc_ref[0:1, :] * _shift_down(p, cr, 2) + wsc_ref[1:2, :] * _shift_down(p, cr, 1) + wsc_ref[2:3, :] * p
        carry[...] = p[tm - 8:tm, :]
        q_ref[...] = q.astype(BF16)
        yb_ref[...] = (zcols(C_BG, D_B, keep=0) * q).astype(BF16)

        av = _dot(ya_ref[...], wb_v[0])
        sa = _sigmoid(zcols(C_GA, D_MODEL) + bgate_ref[:, 0:D_MODEL])
        sa_ref[...] = sa.astype(BF16)
        mg = sa * av
        ca_ref[...] = (mg * (1.0 - sa)).astype(BF16)
        bv = _dot(yb_ref[...], wb_v[1])
        sb = _sigmoid(zcols(C_GB, D_MODEL) + bgate_ref[:, D_MODEL:2 * D_MODEL])
        sb_ref[...] = sb.astype(BF16)
        mb = sb * bv
        cb_ref[...] = (mb * (1.0 - sb)).astype(BF16)
        mg_ref[...] = (mg + mb).astype(BF16)
        x2_ref[...] = x_late_ref[...] + _dot(mg_ref[...], wout_v[...])

    def tile(n, lag):
        return pl.BlockSpec((tm, n), lambda i: (jnp.clip(i - lag, 0, nt - 1), 0))

    outs = [
        jax.ShapeDtypeStruct((t_len, 3 * D_B), BF16),
        jax.ShapeDtypeStruct((t_len, D_A), BF16),
        jax.ShapeDtypeStruct((t_len, D_B), BF16),
        jax.ShapeDtypeStruct((t_len, D_B), BF16),
        jax.ShapeDtypeStruct((t_len, D_MODEL), BF16),
        jax.ShapeDtypeStruct((t_len, D_MODEL), BF16),
        jax.ShapeDtypeStruct((t_len, D_MODEL), BF16),
        jax.ShapeDtypeStruct((t_len, D_MODEL), BF16),
        jax.ShapeDtypeStruct((t_len, D_A), BF16),
        jax.ShapeDtypeStruct((t_len, D_A), BF16),
        jax.ShapeDtypeStruct((t_len, D_A), BF16),
        jax.ShapeDtypeStruct((t_len, D_A), BF16),
        jax.ShapeDtypeStruct((t_len, D_MODEL), BF16),
        jax.ShapeDtypeStruct((t_len, D_MODEL), BF16),
        jax.ShapeDtypeStruct((t_len, D_MODEL), F32),
    ]
    return _staged_call(
        core, name=f"mixer_fwd_l{layer}", grid=(nt + 1,),
        in_specs=[tile(D_MODEL, 0), tile(D_MODEL, 1), _const_spec((1, D_MODEL)), _const_spec((1, 2 * D_MODEL)),
                  _const_spec((1, D_A)), _const_spec((1, D_A)), _const_spec((A_HEADS, 128, 128)),
                  _const_spec((128, D_A)), _const_spec((8, D_B)), ANY, ANY, ANY],
        out_specs=[tile(o.shape[1], 0 if k == len(outs) - 2 else 1) for k, o in enumerate(outs)],
        out_shape=outs,
        scratch_shapes=[pltpu.VMEM((D_MODEL, D_IN), BF16), pltpu.VMEM((2, D_A, D_MODEL), BF16),
                        pltpu.VMEM((D_MODEL, D_MODEL), BF16), pltpu.VMEM((8, D_B), F32),
                        pltpu.VMEM((tm, D_A), BF16), pltpu.VMEM((tm, D_A), F32), pltpu.VMEM((tm, D_IN), F32),
                        pltpu.SemaphoreType.DMA((16,))],
        args=[x, x, g1, bgate, lng, lnb, wm, bsf, wsc, win_g, wb_g, wout_g], stages=stages)


def _ffn_fwd(layer, x2, g2, wfc, bfc, wup_g, wdown_g, stages, head=None):
    t_len = x2.shape[0]
    tm = min(TM_FFN, t_len)
    nt = t_len // tm

    def core(*refs):
        if head is None:
            (x_ref, g2_ref, wfc_ref, bfc_ref, wup_hbm, wdown_hbm, up_ref, silu_ref, dsilu_ref, act_ref, h_ref, x3_ref,
             wup_v, wdown_v, carry, sems) = refs
        else:
            (x_ref, g2_ref, wfc_ref, bfc_ref, t_ref, gf_ref, wup_hbm, wdown_hbm, up_ref, silu_ref, dsilu_ref, act_ref,
             h_ref, dx_ref, dgf_ref, loss_ref, wup_v, wdown_v, carry, sems) = refs
        i = pl.program_id(0)

        @pl.when(i == 0)
        def _():
            cps = _load_col_sharded(wup_hbm, wup_v, sems, 0) + _load_row_sharded(wdown_hbm, wdown_v, sems, 4)
            _start_all(cps)
            carry[...] = jnp.zeros_like(carry)
            if head is not None:
                dgf_ref[...] = jnp.zeros_like(dgf_ref)
                loss_ref[...] = jnp.zeros_like(loss_ref)
            _wait_all(cps)

        xv = x_ref[...]
        r = lax.rsqrt(jnp.mean(xv * xv, axis=-1, keepdims=True) + RMS_EPS)
        h_ref[...] = (xv * r * g2_ref[...]).astype(BF16)
        gate = _dot(h_ref[...], wup_v[:, 0:D_FF])
        up_ref[:, 0:D_FF] = gate.astype(BF16)
        cr = carry[...]
        gc = (wfc_ref[0:1, :] * _shift_down(gate, cr, 2) + wfc_ref[1:2, :] * _shift_down(gate, cr, 1)
              + wfc_ref[2:3, :] * gate + bfc_ref[...])
        carry[...] = gate[tm - 8:tm, :]
        sg = _sigmoid(gc)
        silu = gc * sg
        silu_ref[...] = silu.astype(BF16)
        dsilu_ref[...] = (sg + silu * (1.0 - sg)).astype(BF16)
        val = _dot(h_ref[...], wup_v[:, D_FF:2 * D_FF])
        up_ref[:, D_FF:2 * D_FF] = val.astype(BF16)
        act_ref[...] = (silu * val).astype(BF16)
        x3 = x_ref[...] + _dot(act_ref[...], wdown_v[...])
        if head is None:
            x3_ref[...] = x3
        else:
            r3 = lax.rsqrt(jnp.mean(x3 * x3, axis=-1, keepdims=True) + RMS_EPS)
            xh = x3 * r3
            err = xh * gf_ref[...] - t_ref[...]
            loss_ref[...] += _colsum8(err * err)
            dy = err * (1.0 / D_MODEL)
            dgf_ref[...] += _colsum8(dy * xh)
            dxh = dy * gf_ref[...]
            dx_ref[...] = r3 * (dxh - xh * jnp.mean(dxh * xh, axis=-1, keepdims=True))

    outs = [
        jax.ShapeDtypeStruct((t_len, 2 * D_FF), BF16),
        jax.ShapeDtypeStruct((t_len, D_FF), BF16),
        jax.ShapeDtypeStruct((t_len, D_FF), BF16),
        jax.ShapeDtypeStruct((t_len, D_FF), BF16),
        jax.ShapeDtypeStruct((t_len, D_MODEL), BF16),
        jax.ShapeDtypeStruct((t_len, D_MODEL), F32),
    ]
    in_specs = [_row_spec(tm, D_MODEL), _const_spec((1, D_MODEL)), _const_spec((8, D_FF)), _const_spec((1, D_FF))]
    out_specs = [_row_spec(tm, o.shape[1]) for o in outs]
    args = [x2, g2, wfc, bfc]
    if head is not None:
        in_specs += [_row_spec(tm, D_MODEL), _const_spec((1, D_MODEL))]
        args += list(head)
        outs += [jax.ShapeDtypeStruct((8, D_MODEL), F32)] * 2
        out_specs += [_const_spec((8, D_MODEL))] * 2
    return _staged_call(
        core, name=f"ffn_fwd_l{layer}", grid=(nt,),
        in_specs=in_specs + [ANY, ANY], out_specs=out_specs, out_shape=outs,
        scratch_shapes=[pltpu.VMEM((D_MODEL, 2 * D_FF), BF16), pltpu.VMEM((D_FF, D_MODEL), BF16),
                        pltpu.VMEM((8, D_FF), F32), pltpu.SemaphoreType.DMA((8,))],
        args=args + [wup_g, wdown_g], stages=stages)


def _ffn_bwd(layer, dx3, x2, up, silu, dsilu, g2, wfc, wup_g, wdown_g, stages):
    t_len = x2.shape[0]
    tm = min(TM_FFN, t_len)
    nt = t_len // tm

    def core(dx3_ref, dx3_late_ref, x_ref, up_ref, silu_ref, dsilu_ref, g2_ref, wfc_ref, wup_hbm, wdown_hbm,
             dx2_ref, dup_ref, dx3b_ref, dg2_ref, dbfc_ref, dwfc_ref,
             wup_v, wdown_v, carry, da_s, dup_s, sems):
        i = pl.program_id(0)

        @pl.when(i == 0)
        def _():
            cps = _load_col_sharded(wup_hbm, wup_v, sems, 0) + _load_row_sharded(wdown_hbm, wdown_v, sems, 4)
            _start_all(cps)
            for ref in (carry, da_s, dup_s, dg2_ref, dbfc_ref, dwfc_ref):
                ref[...] = jnp.zeros_like(ref)
            _wait_all(cps)

        live = (i <= nt).astype(F32)
        dx3b_ref[...] = dx3_ref[...].astype(BF16)
        dh = jnp.zeros((tm, D_MODEL), F32)
        for c0, c1 in FF_CHUNKS:
            v0, v1 = D_FF + c0, D_FF + c1
            dh = dh + _dot_nt(dup_s[:, c0:c1], wup_v[:, c0:c1]) + _dot_nt(dup_s[:, v0:v1], wup_v[:, v0:v1])
            da = da_s[:, c0:c1]
            dval = (da * silu_ref[:, c0:c1].astype(F32)).astype(BF16)
            dup_ref[:, v0:v1] = dval
            dup_s[:, v0:v1] = dval
            dgc = da * up_ref[:, v0:v1].astype(F32) * dsilu_ref[:, c0:c1].astype(F32)
            cr = carry[:, c0:c1]
            dgc1 = _shift_up(dgc, cr, 1)
            dgc2 = _shift_up(dgc, cr, 2)
            carry[:, c0:c1] = jnp.where(i < nt, dgc[0:8, :], cr)
            gate = up_ref[:, c0:c1].astype(F32)
            dbfc_ref[:, c0:c1] += live * _colsum8(dgc)
            dwfc_ref[0, :, c0:c1] += live * _colsum8(dgc2 * gate)
            dwfc_ref[1, :, c0:c1] += live * _colsum8(dgc1 * gate)
            dwfc_ref[2, :, c0:c1] += live * _colsum8(dgc * gate)
            dgate = (wfc_ref[2:3, c0:c1] * dgc + wfc_ref[1:2, c0:c1] * dgc1 + wfc_ref[0:1, c0:c1] * dgc2).astype(BF16)
            dup_ref[:, c0:c1] = dgate
            dup_s[:, c0:c1] = dgate
            da_s[:, c0:c1] = _dot_nt(dx3b_ref[...], wdown_v[c0:c1, :])
        xv = x_ref[...]
        r = lax.rsqrt(jnp.mean(xv * xv, axis=-1, keepdims=True) + RMS_EPS)
        xh = xv * r
        dg2_ref[...] += _colsum8(dh * xh)
        dxh = dh * g2_ref[...]
        dx2_ref[...] = dx3_late_ref[...] + r * (dxh - xh * jnp.mean(dxh * xh, axis=-1, keepdims=True))

    def tile(n, lag):
        return pl.BlockSpec((tm, n), lambda i: (nt - 1 - jnp.clip(i - lag, 0, nt - 1), 0))

    outs = [
        jax.ShapeDtypeStruct((t_len, D_MODEL), F32),
        jax.ShapeDtypeStruct((t_len, 2 * D_FF), BF16),
        jax.ShapeDtypeStruct((t_len, D_MODEL), BF16),
        jax.ShapeDtypeStruct((8, D_MODEL), F32),
        jax.ShapeDtypeStruct((8, D_FF), F32),
        jax.ShapeDtypeStruct((3, 8, D_FF), F32),
    ]
    return _staged_call(
        core, name=f"ffn_bwd_l{layer}", grid=(nt + 2,),
        in_specs=[tile(D_MODEL, 0), tile(D_MODEL, 2), tile(D_MODEL, 2), tile(2 * D_FF, 1), tile(D_FF, 1), tile(D_FF, 1),
                  _const_spec((1, D_MODEL)), _const_spec((8, D_FF)), ANY, ANY],
        out_specs=[tile(D_MODEL, 2), tile(2 * D_FF, 1), tile(D_MODEL, 0),
                   _const_spec((8, D_MODEL)), _const_spec((8, D_FF)), _const_spec((3, 8, D_FF))],
        out_shape=outs,
        scratch_shapes=[pltpu.VMEM((D_MODEL, 2 * D_FF), BF16), pltpu.VMEM((D_FF, D_MODEL), BF16),
                        pltpu.VMEM((8, D_FF), F32), pltpu.VMEM((tm, D_FF), F32), pltpu.VMEM((tm, 2 * D_FF), BF16),
                        pltpu.SemaphoreType.DMA((8,))],
        args=[dx3, dx3, x2, up, silu, dsilu, g2, wfc, wup_g, wdown_g], stages=stages)


def _mixer_bwd(layer, dx2, x, zc, qs, sa, ca, sb, cb, ug, fu, xhs, cv, g1, lng, lnb, wmt, wsc, win_g, wb_g, wout_g,
               stages):
    t_len = x.shape[0]
    tm = min(TM_MIX, t_len)
    nt = t_len // tm
    nb = tm // GMLP_BLOCK

    def core(dx2_ref, x_ref, zc_ref, q_ref, sa_ref, ca_ref, sb_ref, cb_ref, ug_ref, fu_ref, xh_ref, cv_ref,
             g1_ref, lng_ref, lnb_ref, wmt_ref, wsc_ref, win_hbm, wb_hbm, wout_hbm,
             dx_ref, dz_ref, da_ref, db_ref, dx2b_ref, dg1_ref, dbgate_ref, dlng_ref, dlnb_ref, dwm_ref, dbsf_ref, dwsc_ref,
             win_v, wb_v, wout_v, carry, vn_s, df_s, dvn_s, sems):
        i = pl.program_id(0)

        @pl.when(i == 0)
        def _():
            cps = (_load_col_sharded(win_hbm, win_v, sems, 0) + _load_branch(wb_hbm, wb_v, sems, 4)
                   + _load_row_sharded(wout_hbm, wout_v, sems, 12))
            _start_all(cps)
            for ref in (carry, dg1_ref, dbgate_ref, dlng_ref, dlnb_ref, dwm_ref, dbsf_ref, dwsc_ref):
                ref[...] = jnp.zeros_like(ref)
            _wait_all(cps)

        def kept(k):
            return zc_ref[:, k * D_B:(k + 1) * D_B].astype(F32)

        def dz_cols(c0, n, val):
            dz_ref[:, c0:c0 + n] = val.astype(BF16)
            return _dot_nt(dz_ref[:, c0:c0 + n], win_v[:, c0:c0 + n])

        dx2b_ref[...] = dx2_ref[...].astype(BF16)
        dm = _dot_nt(dx2b_ref[...], wout_v[...])
        da_ref[...] = (dm * sa_ref[...].astype(F32)).astype(BF16)
        dga = dm * ca_ref[...].astype(F32)
        dh = dz_cols(C_GA, D_MODEL, dga)
        dbgate_ref[:, 0:D_MODEL] += _colsum8(dga)
        dya = _dot_nt(da_ref[...], wb_v[0])
        db_ref[...] = (dm * sb_ref[...].astype(F32)).astype(BF16)
        dgb = dm * cb_ref[...].astype(F32)
        dh = dh + dz_cols(C_GB, D_MODEL, dgb)
        dbgate_ref[:, D_MODEL:2 * D_MODEL] += _colsum8(dgb)
        dyb = _dot_nt(db_ref[...], wb_v[1])

        xh = xh_ref[...].astype(F32)
        vn_s[...] = (xh * lng_ref[...] + lnb_ref[...]).astype(BF16)
        df = dya * ug_ref[...].astype(F32)
        df_s[...] = df.astype(BF16)
        dbsf_acc = df[0:128, :]
        for b in range(1, nb):
            dbsf_acc = dbsf_acc + df[b * 128:(b + 1) * 128, :]
        dbsf_ref[...] += dbsf_acc
        for hd in range(A_HEADS):
            cols = slice(hd * 128, (hd + 1) * 128)
            vcat = jnp.concatenate([vn_s[b * 128:(b + 1) * 128, cols] for b in range(nb)], axis=1)
            dcat = jnp.concatenate([df_s[b * 128:(b + 1) * 128, cols] for b in range(nb)], axis=1)
            gcat = _dot(wmt_ref[hd], dcat)
            dwm_ref[hd] += _dot_nt(dcat, vcat)
            for b in range(nb):
                dvn_s[b * 128:(b + 1) * 128, cols] = gcat[:, b * 128:(b + 1) * 128]
        dh = dh + dz_cols(C_U, D_A, dya * fu_ref[...].astype(F32))
        dvn = dvn_s[...]
        dlng_ref[...] += _colsum8(dvn * xh)
        dlnb_ref[...] += _colsum8(dvn)
        dxh = dvn * lng_ref[...]
        dvc = dxh - jnp.mean(dxh, axis=-1, keepdims=True) - xh * jnp.mean(dxh * xh, axis=-1, keepdims=True)
        dh = dh + dz_cols(C_V, D_A, dvc * cv_ref[...].astype(F32))

        cg = kept(1)
        hbv = kept(2)
        p = cg * hbv
        dh = dh + dz_cols(C_BG, D_B, dyb * q_ref[...].astype(F32))
        dq = dyb * kept(0)
        cr = carry[...]
        dq1 = _shift_up(dq, cr, 1)
        dq2 = _shift_up(dq, cr, 2)
        carry[...] = dq[0:8, :]
        dwsc_ref[0] += _colsum8(dq2 * p)
        dwsc_ref[1] += _colsum8(dq1 * p)
        dwsc_ref[2] += _colsum8(dq * p)
        dp = wsc_ref[2:3, :] * dq + wsc_ref[1:2, :] * dq1 + wsc_ref[0:1, :] * dq2
        dh = dh + dz_cols(C_CG, D_B, dp * hbv)
        dh = dh + dz_cols(C_HB, D_B, dp * cg)

        xv = x_ref[...]
        r = lax.rsqrt(jnp.mean(xv * xv, axis=-1, keepdims=True) + RMS_EPS)
        xn = xv * r
        dg1_ref[...] += _colsum8(dh * xn)
        dxn = dh * g1_ref[...]
        dx_ref[...] = dx2_ref[...] + r * (dxn - xn * jnp.mean(dxn * xn, axis=-1, keepdims=True))

    outs = [
        jax.ShapeDtypeStruct((t_len, D_MODEL), F32),
        jax.ShapeDtypeStruct((t_len, D_IN), BF16),
        jax.ShapeDtypeStruct((t_len, D_MODEL), BF16),
        jax.ShapeDtypeStruct((t_len, D_MODEL), BF16),
        jax.ShapeDtypeStruct((t_len, D_MODEL), BF16),
        jax.ShapeDtypeStruct((8, D_MODEL), F32),
        jax.ShapeDtypeStruct((8, 2 * D_MODEL), F32),
        jax.ShapeDtypeStruct((8, D_A), F32),
        jax.ShapeDtypeStruct((8, D_A), F32),
        jax.ShapeDtypeStruct((A_HEADS, 128, 128), F32),
        jax.ShapeDtypeStruct((128, D_A), F32),
        jax.ShapeDtypeStruct((3, 8, D_B), F32),
    ]

    return _staged_call(
        core, name=f"mixer_bwd_l{layer}", grid=(nt,),
        in_specs=[_row_spec(tm, D_MODEL, nt), _row_spec(tm, D_MODEL, nt), _row_spec(tm, 3 * D_B, nt),
                  _row_spec(tm, D_B, nt), _row_spec(tm, D_MODEL, nt), _row_spec(tm, D_MODEL, nt),
                  _row_spec(tm, D_MODEL, nt), _row_spec(tm, D_MODEL, nt), _row_spec(tm, D_A, nt), _row_spec(tm, D_A, nt),
                  _row_spec(tm, D_A, nt), _row_spec(tm, D_A, nt),
                  _const_spec((1, D_MODEL)), _const_spec((1, D_A)), _const_spec((1, D_A)),
                  _const_spec((A_HEADS, 128, 128)), _const_spec((8, D_B)), ANY, ANY, ANY],
        out_specs=[_row_spec(tm, D_MODEL, nt), _row_spec(tm, D_IN, nt), _row_spec(tm, D_MODEL, nt),
                   _row_spec(tm, D_MODEL, nt), _row_spec(tm, D_MODEL, nt),
                   _const_spec((8, D_MODEL)), _const_spec((8, 2 * D_MODEL)), _const_spec((8, D_A)), _const_spec((8, D_A)),
                   _const_spec((A_HEADS, 128, 128)), _const_spec((128, D_A)), _const_spec((3, 8, D_B))],
        out_shape=outs,
        scratch_shapes=[pltpu.VMEM((D_MODEL, D_IN), BF16), pltpu.VMEM((2, D_A, D_MODEL), BF16),
                        pltpu.VMEM((D_MODEL, D_MODEL), BF16), pltpu.VMEM((8, D_B), F32),
                        pltpu.VMEM((tm, D_A), BF16), pltpu.VMEM((tm, D_A), BF16), pltpu.VMEM((tm, D_A), F32),
                        pltpu.SemaphoreType.DMA((16,))],
        args=[dx2, x, zc, qs, sa, ca, sb, cb, ug, fu, xhs, cv, g1, lng, lnb, wmt, wsc, win_g, wb_g, wout_g],
        stages=stages)


def _wgrad(name, layer, a, b, rows, cols, row_blk, col_blk, stages, a_first=0):
    t_len = a.shape[0]
    n = b.shape[1]
    tk = min(TK_WGRAD, t_len)
    col_sharded = n == N_CHIPS * cols
    m = rows if col_sharded else a.shape[1]
    grid = (m // row_blk, n // col_blk, t_len // tk)
    shards = col_blk // cols if col_sharded else 1

    if col_sharded:
        out_shape = (N_CHIPS, rows, cols)
        out_spec = pl.BlockSpec((shards, row_blk, cols), lambda i, j, k: (j, i, 0))
    else:
        out_shape = (N_CHIPS * rows, cols)
        out_spec = pl.BlockSpec((row_blk, col_blk), lambda i, j, k: (i, j))

    def core(a_ref, b_ref, o_ref):
        @pl.when(pl.program_id(2) == 0)
        def _():
            o_ref[...] = jnp.zeros_like(o_ref)

        g = _dot_tn(a_ref[...], b_ref[...])
        if col_sharded:
            for q in range(shards):
                o_ref[q] += g[:, q * cols:(q + 1) * cols]
        else:
            o_ref[...] += g

    own, outs = _staged_call(
        core, name=f"wgrad_{name}_l{layer}", grid=grid,
        in_specs=[pl.BlockSpec((tk, row_blk), lambda i, j, k: (k, a_first + i)),
                  pl.BlockSpec((tk, col_blk), lambda i, j, k: (k, j))],
        out_specs=[out_spec], out_shape=[jax.ShapeDtypeStruct(out_shape, F32)], scratch_shapes=[],
        args=[a, b], stages=stages)
    return [own[0].reshape(N_CHIPS, rows, cols)], outs


def _wgrad_branch(layer, ya, da, yb, db, stages):
    t_len = ya.shape[0]
    tk = min(TK_WGRAD, t_len)

    cs = D_MODEL // N_CHIPS

    def core(ya_ref, da_ref, yb_ref, db_ref, o_ref):
        @pl.when(pl.program_id(0) == 0)
        def _():
            o_ref[...] = jnp.zeros_like(o_ref)

        ga = _dot_tn(ya_ref[...], da_ref[...])
        gb = _dot_tn(yb_ref[...], db_ref[...])
        for k in range(N_CHIPS):
            o_ref[k, 0:D_A, :] += ga[:, k * cs:(k + 1) * cs]
            o_ref[k, D_A:2 * D_A, :] += gb[:, k * cs:(k + 1) * cs]

    a_spec = pl.BlockSpec((tk, D_A), lambda k: (k, 0))
    d_spec = pl.BlockSpec((tk, D_MODEL), lambda k: (k, 0))
    return _staged_call(
        core, name=f"wgrad_w_branch_l{layer}", grid=(t_len // tk,),
        in_specs=[a_spec, d_spec, a_spec, d_spec],
        out_specs=[pl.BlockSpec((N_CHIPS, 2 * D_A, cs), lambda k: (0, 0, 0))],
        out_shape=[jax.ShapeDtypeStruct((N_CHIPS, 2 * D_A, cs), F32)], scratch_shapes=[],
        args=[ya, da, yb, db], stages=stages)


def _flat_blk(rows, cols):
    blk = rows
    while blk * cols * 4 > 2 * 1024 * 1024 and blk % 16 == 0:
        blk //= 2
    return blk


def _cast_into_slots(name, jobs, chip, stages):
    blks = [_flat_blk(w.shape[1], w.shape[2]) for w, _ in jobs]
    nblks = [w.shape[1] // b for (w, _), b in zip(jobs, blks)]
    n = len(jobs)
    out_shape = [jax.ShapeDtypeStruct((N_CHIPS,) + w.shape[1:], BF16) for w, _ in jobs]

    def core(*refs):
        for w_ref, o_ref in zip(refs[-2 * n:-n], refs[-n:]):
            o_ref[...] = w_ref[...].astype(BF16)

    def slot(*scalars):
        return scalars[0][0] if scalars else 2 * lax.axis_index("x") + lax.axis_index("y")

    in_specs = [pl.BlockSpec((None, b, w.shape[2]), lambda i, *s, la=la, k=k: (la, jnp.minimum(i, k - 1), 0))
                for (w, la), b, k in zip(jobs, blks, nblks)]
    out_specs = [pl.BlockSpec((None, b, w.shape[2]), lambda i, *s, k=k: (slot(*s), jnp.minimum(i, k - 1), 0))
                 for (w, _), b, k in zip(jobs, blks, nblks)]
    args = [w for w, _ in jobs]
    if stages:
        return _staged_call(core, name=f"cast_{name}", grid=(max(nblks),), in_specs=in_specs, out_specs=out_specs,
                            out_shape=out_shape, scratch_shapes=[], args=args, stages=stages)
    own = pl.pallas_call(
        core, name=f"cast_{name}",
        grid_spec=pltpu.PrefetchScalarGridSpec(num_scalar_prefetch=1, grid=(max(nblks),), in_specs=in_specs,
                                               out_specs=out_specs),
        out_shape=out_shape, compiler_params=_params(),
    )(chip, *args)
    return list(own), []


def _reduction_sums(name, jobs, pos):
    in_specs, out_specs, out_shape, args, bodies, counts = [], [], [], [], [], []
    for job in jobs:
        kind, grad, other = job[0], job[1], job[2]
        _, h, cols = other.shape
        blk = _flat_blk(h, cols)
        nblk = h // blk
        if kind == "pair":
            total = N_CHIPS * nblk

            def block(s, total=total, nblk=nblk):
                b = jnp.minimum(s, total - 1)
                return b // nblk, b % nblk

            spec = pl.BlockSpec((None, blk, cols), lambda s, p, block=block: (block(s)[0], block(s)[1], 0))
            in_specs += [pl.BlockSpec((None, blk, cols), lambda s, p, block=block, nblk=nblk:
                                      (block(s)[0], p[1] * nblk + block(s)[1], 0)), spec]
            out_specs.append(spec)
            out_shape.append(jax.ShapeDtypeStruct((N_CHIPS, h, cols), BF16))
            args += [grad, other]
            bodies.append((2, lambda g, o, out: out.__setitem__(..., (g[...] + o[...]).astype(BF16))))
        else:
            total = nblk

            def block(s, total=total):
                return jnp.minimum(s, total - 1)

            in_specs += [pl.BlockSpec((None, blk, cols), lambda s, p, block=block, nblk=nblk:
                                      (p[0], p[1] * nblk + block(s), 0)),
                         pl.BlockSpec((None, blk, cols), lambda s, p, block=block: (p[0], block(s), 0)),
                         pl.BlockSpec((3, blk, cols), lambda s, p, block=block: (0, block(s), 0))]
            out_specs.append(pl.BlockSpec((blk, cols), lambda s, p, block=block, nblk=nblk: (p[1] * nblk + block(s), 0)))
            out_shape.append(jax.ShapeDtypeStruct((2 * h, cols), F32))
            args += [grad, other, job[3]]
            bodies.append((3, lambda g, o, r, out: out.__setitem__(
                ..., (((g[...] + o[...]) + r[0].astype(F32)) + r[1].astype(F32)) + r[2].astype(F32))))
        counts.append(total)

    def body(pos_ref, *refs):
        ins, outs = refs[:len(args)], refs[len(args):]
        k = 0
        for (n_in, fn), out in zip(bodies, outs):
            fn(*ins[k:k + n_in], out)
            k += n_in

    return pl.pallas_call(
        body, name=f"reduction_sums_{name}",
        grid_spec=pltpu.PrefetchScalarGridSpec(num_scalar_prefetch=1, grid=(max(counts),), in_specs=in_specs,
                                               out_specs=out_specs),
        out_shape=out_shape,
        compiler_params=_params(),
    )(pos, *args)


def _sum_slots(name, slots):
    n, rows, _ = slots.shape

    def body(s_ref, o_ref):
        acc = s_ref[0]
        for d in range(1, n):
            acc = acc + s_ref[d]
        o_ref[...] = acc

    return pl.pallas_call(
        body, name=f"sum_slots_{name}", grid=(1,),
        in_specs=[pl.BlockSpec((n, rows, 128), lambda i: (0, 0, 0))],
        out_specs=pl.BlockSpec((rows, 128), lambda i: (0, 0)),
        out_shape=jax.ShapeDtypeStruct((rows, 128), F32),
        compiler_params=_params(),
    )(slots)


def _adamw_math(w, g, m, v):
    m2 = ADAM_B1 * m + (1.0 - ADAM_B1) * g
    v2 = ADAM_B2 * v + (1.0 - ADAM_B2) * (g * g)
    m_hat = m2 / (1.0 - ADAM_B1 ** ADAM_STEP)
    v_hat = v2 / (1.0 - ADAM_B2 ** ADAM_STEP)
    delta = -ADAM_LR * (m_hat / (jnp.sqrt(v_hat) + ADAM_EPS) + ADAM_WD * w)
    return delta, m2, v2


def _adamw_big(name, w, g0, g1, m, v):
    _, rows, cols = w.shape
    blk = _flat_blk(rows, cols) // 2

    def body(w_ref, g0_ref, g1_ref, m_ref, v_ref, g_ref, d_ref, m2_ref, v2_ref):
        g = jnp.where(pl.program_id(0) == 0, g0_ref[...], g1_ref[...])
        d, m2, v2 = _adamw_math(w_ref[...], g, m_ref[...], v_ref[...])
        g_ref[...] = g
        d_ref[...] = d
        m2_ref[...] = m2
        v2_ref[...] = v2

    spec = pl.BlockSpec((None, blk, cols), lambda la, i: (la, i, 0))
    return pl.pallas_call(
        body, name=f"adamw_{name}", grid=(N_LAYERS, rows // blk),
        in_specs=[spec, pl.BlockSpec((blk, cols), lambda la, i: (i * (1 - la), 0)),
                  pl.BlockSpec((blk, cols), lambda la, i: (i * la, 0)), spec, spec],
        out_specs=[spec] * 4,
        out_shape=[jax.ShapeDtypeStruct(w.shape, F32)] * 4,
        compiler_params=_params(("parallel", "parallel")),
    )(w, g0, g1, m, v)


def _adamw_small(ws, gs, ms, vs):
    n = len(ws)

    def body(*refs):
        ins, outs = refs[:4 * n], refs[4 * n:]
        for k in range(n):
            d, m2, v2 = _adamw_math(ins[k][...], ins[n + k][...], ins[2 * n + k][...], ins[3 * n + k][...])
            outs[k][...] = d
            outs[n + k][...] = m2
            outs[2 * n + k][...] = v2

    vmem = pl.BlockSpec(memory_space=pltpu.VMEM)
    return pl.pallas_call(
        body, name="adamw_small",
        in_specs=[vmem] * (4 * n), out_specs=[vmem] * (3 * n),
        out_shape=[jax.ShapeDtypeStruct(w.shape, F32) for w in ws] * 3,
        compiler_params=pltpu.CompilerParams(vmem_limit_bytes=V7X_VMEM_LIMIT),
    )(*ws, *gs, *ms, *vs)


SMALL = ("norm1_g", "b_gate", "gmlp_ln_g", "gmlp_ln_b", "w_spatial", "b_spatial", "w_shortconv", "norm2_g",
         "w_ffn_conv", "b_ffn_conv", "final_g")
ALL_WEIGHTS = ("norm1_g", "w_in", "b_gate", "gmlp_ln_g", "gmlp_ln_b", "w_spatial", "b_spatial", "w_shortconv",
               "w_branch", "w_out", "norm2_g", "w_ffn_up", "w_ffn_conv", "b_ffn_conv", "w_ffn_down", "final_g")


def _pack(arrays):
    flat = jnp.concatenate([a.reshape(-1) for a in arrays])
    n = flat.shape[0]
    rows = -(-n // 1024) * 8
    return jnp.pad(flat, (0, rows * 128 - n)).reshape(rows, 128)


def _unpack(packed, like):
    flat = packed.reshape(-1)
    out, off = [], 0
    for a in like:
        out.append(flat[off:off + a.size].reshape(a.shape))
        off += a.size
    return out


def _pad8(w):
    return jnp.pad(w, ((0, 5), (0, 0)))


def kernel(x, norm1_g, w_in, b_gate, gmlp_ln_g, gmlp_ln_b, w_spatial, b_spatial, w_shortconv, w_branch, w_out, norm2_g, w_ffn_up, w_ffn_conv, b_ffn_conv, w_ffn_down, final_g, loss_target, m_norm1_g, m_w_in, m_b_gate, m_gmlp_ln_g, m_gmlp_ln_b, m_w_spatial, m_b_spatial, m_w_shortconv, m_w_branch, m_w_out, m_norm2_g, m_w_ffn_up, m_w_ffn_conv, m_b_ffn_conv, m_w_ffn_down, m_final_g, v_norm1_g, v_w_in, v_b_gate, v_gmlp_ln_g, v_gmlp_ln_b, v_w_spatial, v_b_spatial, v_w_shortconv, v_w_branch, v_w_out, v_norm2_g, v_w_ffn_up, v_w_ffn_conv, v_b_ffn_conv, v_w_ffn_down, v_final_g):
    weights = dict(norm1_g=norm1_g, w_in=w_in, b_gate=b_gate, gmlp_ln_g=gmlp_ln_g, gmlp_ln_b=gmlp_ln_b,
                   w_spatial=w_spatial, b_spatial=b_spatial, w_shortconv=w_shortconv, w_branch=w_branch, w_out=w_out,
                   norm2_g=norm2_g, w_ffn_up=w_ffn_up, w_ffn_conv=w_ffn_conv, b_ffn_conv=b_ffn_conv,
                   w_ffn_down=w_ffn_down, final_g=final_g)
    mom = dict(norm1_g=m_norm1_g, w_in=m_w_in, b_gate=m_b_gate, gmlp_ln_g=m_gmlp_ln_g, gmlp_ln_b=m_gmlp_ln_b,
               w_spatial=m_w_spatial, b_spatial=m_b_spatial, w_shortconv=m_w_shortconv, w_branch=m_w_branch,
               w_out=m_w_out, norm2_g=m_norm2_g, w_ffn_up=m_w_ffn_up, w_ffn_conv=m_w_ffn_conv,
               b_ffn_conv=m_b_ffn_conv, w_ffn_down=m_w_ffn_down, final_g=m_final_g)
    vel = dict(norm1_g=v_norm1_g, w_in=v_w_in, b_gate=v_b_gate, gmlp_ln_g=v_gmlp_ln_g, gmlp_ln_b=v_gmlp_ln_b,
               w_spatial=v_w_spatial, b_spatial=v_b_spatial, w_shortconv=v_w_shortconv, w_branch=v_w_branch,
               w_out=v_w_out, norm2_g=v_norm2_g, w_ffn_up=v_w_ffn_up, w_ffn_conv=v_w_ffn_conv,
               b_ffn_conv=v_b_ffn_conv, w_ffn_down=v_w_ffn_down, final_g=v_final_g)

    cx, cy, cc = _mesh_pos()
    chip = 2 * cx + cy
    pos_arr = jnp.stack([chip, cc]).astype(jnp.int32)
    t_len = x.shape[1]
    xs = x.reshape(t_len, D_MODEL)
    target = loss_target.reshape(t_len, D_MODEL)
    pipe = _Pipe()

    full = {}

    mixer_w = ("w_in", "w_branch", "w_out")
    ffn_w = ("w_ffn_up", "w_ffn_down")
    slots = {}

    def cast(name, keys, stages):
        own, outs = _cast_into_slots(name, [(weights[n].reshape((N_LAYERS,) + BIG[n]), la) for n, la in keys],
                                     chip.astype(jnp.int32).reshape(1), stages)
        slots.update(zip(keys, own))
        return own, outs

    def gather(names, la):
        def then(*bufs):
            full.update(zip([(n, la) for n in names], bufs))

        pipe.add(_gather_stage([slots[(n, la)] for n in names], then))

    first = [(n, 0) for n in mixer_w]
    cast("first", first, [])
    gather(mixer_w, 0)
    tap_slots = {}
    pipe.add(_chip_spread_stage(_pack([w_shortconv, w_ffn_conv]), lambda got: tap_slots.__setitem__("all", got)))
    pipe.carry(lambda st: cast("rest", [(n, la) for la in range(N_LAYERS) for n in BIG_NAMES if (n, la) not in first], st))
    by_chip = [_unpack(tap_slots["all"][k], [w_shortconv, w_ffn_conv]) for k in range(N_CHIPS)]
    wsc_full = jnp.concatenate([t[0] for t in by_chip], axis=-1)
    wfc_full = jnp.concatenate([t[1] for t in by_chip], axis=-1)

    idx = jnp.arange(GMLP_BLOCK) // CHUNK
    mask = idx[None, :] <= idx[:, None]
    wm_all = jnp.where(mask[None, None], w_spatial, 0.0)
    wm_bf = wm_all.astype(BF16)
    wmt_bf = jnp.swapaxes(wm_all, -1, -2).astype(BF16)
    bsf = jnp.repeat(jnp.swapaxes(b_spatial, -1, -2), 128, axis=-1)

    def row(a):
        return a.reshape(1, -1)

    def mixer_args(la):
        return (row(norm1_g[la]), row(b_gate[la]), row(gmlp_ln_g[la]), row(gmlp_ln_b[la]))

    def mixer_weights(la):
        return tuple(full[(n, la)] for n in mixer_w)

    def ffn_weights(la):
        return tuple(full[(n, la)] for n in ffn_w)

    saved = []
    h_in = xs
    for la in range(N_LAYERS):
        gather(ffn_w, la)
        *kept, mg, h1, x2 = pipe.carry(lambda st: _mixer_fwd(
            la, h_in, *mixer_args(la), wm_bf[la], bsf[la], _pad8(wsc_full[la]), *mixer_weights(la), st))
        ya, yb = kept[1], kept[2]
        if la + 1 < N_LAYERS:
            gather(mixer_w, la + 1)
        head = (target, row(final_g)) if la == N_LAYERS - 1 else None
        up, silu, dsilu, act, h2, *rest = pipe.carry(lambda st: _ffn_fwd(
            la, x2, row(norm2_g[la]), _pad8(wfc_full[la]), row(b_ffn_conv[la]), *ffn_weights(la), st, head=head))
        saved.append(dict(x=h_in, ya=ya, yb=yb, mixer=[kept[0]] + kept[3:], mg=mg, h1=h1, x2=x2, up=up, silu=silu,
                          dsilu=dsilu, act=act, h2=h2))
        h_in = rest[0]
    dx, dgf8, loss8 = rest

    reduced_big = {}

    sums_due = []

    def run_sums():
        if sums_due:
            due = list(sums_due)
            sums_due.clear()
            run_sums.calls += 1
            for (_, then), res in zip(due, _reduction_sums(str(run_sums.calls), [job for job, _ in due], pos_arr)):
                then(res)

    run_sums.calls = 0
    pipe.after = run_sums

    def reduce_big(name, la, grad):
        def after_pair(other):
            def after_chips(got):
                sums_due.append((("chip", grad, other, got), lambda final: pipe.add(_pair_fill_stage(
                    final, lambda done: reduced_big.__setitem__((name, la), done)))))

            sums_due.append((("pair", grad, other), lambda psum: pipe.add(_chip_send_stage(psum, after_chips))))

        pipe.add(_pair_send_stage(grad, after_pair))

    small = {n: [None] * N_LAYERS for n in SMALL}
    spread = {}
    for la in reversed(range(N_LAYERS)):
        s = saved[la]
        dx3 = dx
        dx2, dup, dx3b, dg2, dbfc, dwfc = pipe.carry(lambda st: _ffn_bwd(
            la, dx3, s["x2"], s["up"], s["silu"], s["dsilu"], row(norm2_g[la]), _pad8(wfc_full[la]),
            *ffn_weights(la), st))
        g, = pipe.carry(lambda st: _wgrad("w_ffn_up", la, s["h2"], dup, 1024, 1408, 512, 2816, st))
        reduce_big("w_ffn_up", la, g)
        g, = pipe.carry(lambda st: _wgrad("w_ffn_down", la, s["act"], dx3b, 704, 1024, 1408, 1024, st))
        reduce_big("w_ffn_down", la, g)
        run = pipe.carry if la > 0 else (lambda call: call([])[0])
        dxl, dz, da, db, dx2b, dg1, dbg, dlng, dlnb, dwm, dbsf, dwsc = run(lambda st: _mixer_bwd(
            la, dx2, s["x"], *s["mixer"], row(norm1_g[la]), row(gmlp_ln_g[la]), row(gmlp_ln_b[la]), wmt_bf[la],
            _pad8(wsc_full[la]), *mixer_weights(la), st))
        small["norm1_g"][la] = dg1.sum(0)
        small["b_gate"][la] = dbg.sum(0)
        small["gmlp_ln_g"][la] = dlng.sum(0)
        small["gmlp_ln_b"][la] = dlnb.sum(0)
        small["w_spatial"][la] = jnp.where(mask[None], dwm, 0.0)
        small["b_spatial"][la] = dbsf.reshape(128, A_HEADS, 128).sum(-1).T
        small["w_shortconv"][la] = dwsc.sum(1)
        small["norm2_g"][la] = dg2.sum(0)
        small["w_ffn_conv"][la] = dwfc.sum(1)
        small["b_ffn_conv"][la] = dbfc.sum(0)
        if la == 0:
            small_local = ([jnp.stack(small[n]) for n in SMALL[:-1]]
                           + [dgf8.sum(0), 0.5 * loss8.sum().reshape(1) / D_MODEL])
            mine = _pack(small_local)

            def after_swap(other, mine=mine):
                pair = _sum_slots("small_pair", jnp.stack([mine, other]))
                pipe.add(_chip_spread_stage(pair, lambda slots: spread.__setitem__("slots", slots)))

            pipe.add(_pair_swap_stage(mine, after_swap))
        def small_wgrads():
            g, = pipe.carry(lambda st: _wgrad("w_out", la, s["mg"], dx2b, 256, 1024, 1024, 1024, st), long=False)
            reduce_big("w_out", la, g)
            g, = pipe.carry(lambda st: _wgrad_branch(la, s["ya"], da, s["yb"], db, st), long=False)
            reduce_big("w_branch", la, g)

        if la > 0:
            small_wgrads()
            g, = pipe.carry(lambda st: _wgrad("w_in", la, s["h1"], dz, 1024, 1152, 512, 2304, st))
            reduce_big("w_in", la, g)
        else:
            for part, tag in enumerate(("w_in_a", "w_in_b")):
                g, = pipe.carry(lambda st: _wgrad(tag, la, s["h1"], dz, 512, 1152, 512, 2304, st, a_first=part))
                reduce_big(tag, la, g)
            small_wgrads()
        dx = dxl
    grad_x = dx.reshape(x.shape)
    pipe.flush()

    reduced_big[("w_in", 0)] = jnp.concatenate([reduced_big[("w_in_a", 0)], reduced_big[("w_in_b", 0)]], axis=0)
    reduced = _unpack(_sum_slots("small_grads", spread["slots"]), small_local)
    loss = reduced[-1].reshape(())
    grads = dict(zip(SMALL, reduced[:-1]))
    grads["w_shortconv"] = lax.dynamic_slice(grads["w_shortconv"], (0, 0, chip * (D_B // 4)), (N_LAYERS, 3, D_B // 4))
    grads["w_ffn_conv"] = lax.dynamic_slice(grads["w_ffn_conv"], (0, 0, chip * (D_FF // 4)), (N_LAYERS, 3, D_FF // 4))

    delta, new_m, new_v = {}, {}, {}
    for n in BIG_NAMES:
        shape3 = (N_LAYERS,) + BIG[n]
        res = _adamw_big(n, weights[n].reshape(shape3), reduced_big[(n, 0)], reduced_big[(n, 1)],
                         mom[n].reshape(shape3), vel[n].reshape(shape3))
        grads[n], delta[n], new_m[n], new_v[n] = (a.reshape(weights[n].shape) for a in res)
    res = _adamw_small(*[[src[n].reshape(-1, src[n].shape[-1]) for n in SMALL] for src in (weights, grads, mom, vel)])
    for k, n in enumerate(SMALL):
        delta[n], new_m[n], new_v[n] = (res[j * len(SMALL) + k].reshape(weights[n].shape) for j in range(3))

    return (loss, grad_x, *[grads[n] for n in ALL_WEIGHTS], *[delta[n] for n in ALL_WEIGHTS],
            *[new_m[n] for n in ALL_WEIGHTS], *[new_v[n] for n in ALL_WEIGHTS])
```

```python
import jax
import jax.numpy as jnp
from jax import lax
from jax.experimental import pallas as pl
from jax.experimental.pallas import tpu as pltpu

F32 = jnp.float32
BF16 = jnp.bfloat16
MESH = pl.DeviceIdType.MESH
ANY = pl.BlockSpec(memory_space=pl.ANY)

D_MODEL = 1024
D_A = 512
D_B = 512
D_IN = 4608
D_FF = 2816
GMLP_BLOCK = 128
CHUNK = 64
A_HEADS = 4
N_LAYERS = 2
N_CHIPS = 4
RMS_EPS = 1e-6
LN_EPS = 1e-5
ADAM_LR = 0.001
ADAM_B1 = 0.9
ADAM_B2 = 0.999
ADAM_EPS = 1e-08
ADAM_WD = 0.01
ADAM_STEP = 10

C_U, C_V, C_BG, C_CG, C_HB, C_GA, C_GB = 0, 512, 1024, 1536, 2048, 2560, 3584

V7X_VMEM_LIMIT = 60 * 1024 * 1024
TM_MIX = 256
TM_FFN = 256
TK_WGRAD = 2048
SLOW_COPY_BYTES = 640 * 1024
FF_CHUNKS = ((0, 768), (768, 1536), (1536, 2304), (2304, 2816))
GELU_C0 = 0.7978845608028654
GELU_C1 = 0.044715

BIG = {
    "w_in": (1024, 1152),
    "w_branch": (1024, 256),
    "w_out": (256, 1024),
    "w_ffn_up": (1024, 1408),
    "w_ffn_down": (704, 1024),
}
BIG_NAMES = tuple(BIG)


def _params(sem=("arbitrary",), vmem=V7X_VMEM_LIMIT):
    return pltpu.CompilerParams(dimension_semantics=sem, vmem_limit_bytes=vmem)


def _gelu(x):
    x2 = x * x
    t = jnp.tanh(GELU_C0 * x * (1.0 + GELU_C1 * x2))
    return 0.5 * x * (1.0 + t), t


def _gelu_grad(x, t):
    return 0.5 * (1.0 + t) + 0.5 * x * (1.0 - t * t) * GELU_C0 * (1.0 + 3.0 * GELU_C1 * x * x)


def _colsum8(v):
    r, n = v.shape
    return v.reshape(r // 8, 8, n).sum(axis=0)


def _dot(a, b):
    return jnp.dot(a, b, preferred_element_type=F32)


def _dot_nt(a, b):
    return lax.dot_general(a, b, (((1,), (1,)), ((), ())), preferred_element_type=F32)


def _dot_tn(a, b):
    return lax.dot_general(a, b, (((0,), (0,)), ((), ())), preferred_element_type=F32)


def _shift_down(v, carry, n):
    rows = lax.broadcasted_iota(jnp.int32, (8, v.shape[1]), 0)
    out = pltpu.roll(v, n, 0)
    head = out[0:8, :]
    for r in range(n):
        head = jnp.where(rows == r, carry[8 - n + r:8 - n + r + 1, :], head)
    return jnp.concatenate([head, out[8:, :]], axis=0)


def _shift_up(v, carry, n):
    tm = v.shape[0]
    rows = lax.broadcasted_iota(jnp.int32, (8, v.shape[1]), 0)
    out = pltpu.roll(v, tm - n, 0)
    tail = out[tm - 8:tm, :]
    for r in range(n):
        tail = jnp.where(rows == 8 - n + r, carry[r:r + 1, :], tail)
    return jnp.concatenate([out[0:tm - 8, :], tail], axis=0)


def _sigmoid(x):
    return 0.5 * jnp.tanh(0.5 * x) + 0.5


def _start_all(copies):
    for cp in copies:
        cp.start()


def _wait_all(copies):
    for cp in copies:
        cp.wait()


def _load_col_sharded(src, dst, sems, first):
    cs = src.shape[-1]
    return [pltpu.make_async_copy(src.at[k], dst.at[:, k * cs:(k + 1) * cs], sems.at[first + k])
            for k in range(N_CHIPS)]


def _load_row_sharded(src, dst, sems, first):
    rs = src.shape[-2]
    return [pltpu.make_async_copy(src.at[k], dst.at[k * rs:(k + 1) * rs, :], sems.at[first + k])
            for k in range(N_CHIPS)]


def _load_branch(src, dst, sems, first):
    return [pltpu.make_async_copy(src.at[k, pl.ds(m * D_A, D_A), :], dst.at[m, :, k * 256:(k + 1) * 256],
                                  sems.at[first + 2 * k + m])
            for k in range(N_CHIPS) for m in range(2)]


def _row_spec(tm, n, rev=None):
    if rev is None:
        return pl.BlockSpec((tm, n), lambda i: (i, 0))
    return pl.BlockSpec((tm, n), lambda i: (rev - 1 - i, 0))


def _const_spec(shape):
    nd = len(shape)
    return pl.BlockSpec(shape, lambda i: (0,) * nd)


def _mesh_pos():
    return lax.axis_index("x"), lax.axis_index("y"), lax.axis_index("c")


def _other_chips(x, y):
    return [(1 - x, y, 2 * (1 - x) + y), (x, 1 - y, 2 * x + (1 - y)), (1 - x, 1 - y, 2 * (1 - x) + (1 - y))]


def _remote(src, dst, ssem, rsem, to):
    return pltpu.make_async_remote_copy(src_ref=src, dst_ref=dst, send_sem=ssem, recv_sem=rsem, device_id=to,
                                        device_id_type=MESH)


def _half(ref, which, h):
    start = pl.multiple_of(which * h, 8)
    if len(ref.shape) == 2:
        return ref.at[pl.ds(start, h), :]
    return ref.at[:, pl.ds(start, h), :]


class _Stage:
    def __init__(self, ins=(), inouts=(), outs=(), n_sems=0, start=None, mid=None, finish=None, then=None, slow=False):
        self.ins, self.inouts, self.outs = list(ins), list(inouts), list(outs)
        self.n_sems, self.start, self.mid, self.finish, self.then = n_sems, start, mid, finish, then
        self.slow = slow


def _gather_stage(bufs, then):
    n = len(bufs)

    def copies(io, sem):
        x, y, c = _mesh_pos()
        me = 2 * x + y
        ici, fwd, got = [], [], []
        for w in range(n):
            h = io[w].shape[1] // 2
            for j, (px, py, pk) in enumerate(_other_chips(x, y)):
                mine = _half(io[w].at[me], c, h)
                theirs = _half(io[w].at[pk], c, h)
                ici.append(_remote(mine, mine, sem(12 * w + j), sem(12 * w + 3 + j), (px, py, c)))
                got.append(_remote(theirs, theirs, sem(12 * w + j), sem(12 * w + 3 + j), (px, py, c)))
                fwd.append(_remote(theirs, theirs, sem(12 * w + 6 + j), sem(12 * w + 9 + j), (x, y, 1 - c)))
        return ici, got, fwd

    def start(ins, io, outs, sem):
        _start_all(copies(io, sem)[0])

    def mid(ins, io, outs, sem):
        _, got, fwd = copies(io, sem)
        for g, f in zip(got, fwd):
            g.wait_recv()
            f.start()

    def finish(ins, io, outs, sem):
        x, y, c = _mesh_pos()
        ici, _, fwd = copies(io, sem)
        for w in range(n):
            h = io[w].shape[1] // 2
            for j, (px, py, pk) in enumerate(_other_chips(x, y)):
                other = _half(io[w].at[pk], 1 - c, h)
                _remote(other, other, sem(12 * w + 6 + j), sem(12 * w + 9 + j), (x, y, 1 - c)).wait_recv()
        for cp in ici + fwd:
            cp.wait_send()

    return _Stage(inouts=bufs, n_sems=12 * n, start=start, mid=mid, finish=finish, then=then)


def _pair_send_stage(grad, then):
    h = grad.shape[1] // 2

    def copy(ins, outs, sem):
        x, y, c = _mesh_pos()
        return _remote(_half(ins[0], 1 - c, h), outs[0], sem(0), sem(1), (x, y, 1 - c))

    return _Stage(ins=[grad], outs=[jax.ShapeDtypeStruct((N_CHIPS, h, grad.shape[2]), F32)], n_sems=2,
                  start=lambda ins, io, outs, sem: copy(ins, outs, sem).start(),
                  finish=lambda ins, io, outs, sem: copy(ins, outs, sem).wait(), then=then)


def _chip_send_stage(psum, then):
    def copies(ins, outs, sem):
        x, y, c = _mesh_pos()
        return [_remote(ins[0].at[pk], outs[0].at[j], sem(j), sem(3 + j), (px, py, c))
                for j, (px, py, pk) in enumerate(_other_chips(x, y))]

    return _Stage(ins=[psum], outs=[jax.ShapeDtypeStruct((3,) + psum.shape[1:], BF16)], n_sems=6,
                  start=lambda ins, io, outs, sem: _start_all(copies(ins, outs, sem)),
                  finish=lambda ins, io, outs, sem: _wait_all(copies(ins, outs, sem)), then=then,
                  slow=psum.shape[1] * psum.shape[2] * 2 > SLOW_COPY_BYTES)


def _pair_fill_stage(final, then):
    h = final.shape[0] // 2

    def copy(io, sem):
        x, y, c = _mesh_pos()
        mine = _half(io[0], c, h)
        return _remote(mine, mine, sem(0), sem(1), (x, y, 1 - c))

    return _Stage(inouts=[final], n_sems=2,
                  start=lambda ins, io, outs, sem: copy(io, sem).start(),
                  finish=lambda ins, io, outs, sem: copy(io, sem).wait(), then=then)


def _pair_swap_stage(packed, then):
    def copy(ins, outs, sem):
        x, y, c = _mesh_pos()
        return _remote(ins[0], outs[0], sem(0), sem(1), (x, y, 1 - c))

    return _Stage(ins=[packed], outs=[jax.ShapeDtypeStruct(packed.shape, F32)], n_sems=2,
                  start=lambda ins, io, outs, sem: copy(ins, outs, sem).start(),
                  finish=lambda ins, io, outs, sem: copy(ins, outs, sem).wait(), then=then)


def _chip_spread_stage(psum, then):
    def copies(ins, outs, sem):
        x, y, c = _mesh_pos()
        me = 2 * x + y
        cps = [_remote(ins[0], outs[0].at[me], sem(j), sem(3 + j), (px, py, c))
               for j, (px, py, pk) in enumerate(_other_chips(x, y))]
        return cps, pltpu.make_async_copy(ins[0], outs[0].at[me], sem(6))

    def start(ins, io, outs, sem):
        cps, own = copies(ins, outs, sem)
        own.start()
        _start_all(cps)

    def finish(ins, io, outs, sem):
        cps, own = copies(ins, outs, sem)
        _wait_all(cps)
        own.wait()

    return _Stage(ins=[psum], outs=[jax.ShapeDtypeStruct((N_CHIPS,) + psum.shape, F32)], n_sems=7,
                  start=start, finish=finish, then=then)


def _staged_call(core, *, name, grid, in_specs, out_specs, out_shape, scratch_shapes, args, stages):
    n_in, n_out, n_scr = len(args), len(out_shape), len(scratch_shapes)
    s_args, s_outs, aliases, layout = [], [], {}, []
    n_sems = 0
    for st in stages:
        i0, o0 = len(s_args), len(s_outs)
        s_args += st.ins + st.inouts
        for q in range(len(st.inouts)):
            aliases[n_in + i0 + len(st.ins) + q] = n_out + o0 + q
        s_outs += [jax.ShapeDtypeStruct(a.shape, a.dtype) for a in st.inouts] + st.outs
        layout.append((i0, o0, n_sems))
        n_sems += st.n_sems
    steps = 1
    for g in grid:
        steps *= g

    def body(*refs):
        own_in = refs[:n_in]
        s_in = refs[n_in:n_in + len(s_args)]
        rest = refs[n_in + len(s_args):]
        own_out = rest[:n_out]
        s_out = rest[n_out:n_out + len(s_outs)]
        scr = rest[n_out + len(s_outs):]

        def run(which):
            for st, (i0, o0, s0) in zip(stages, layout):
                fn = getattr(st, which)
                if fn is not None:
                    fn(s_in[i0:i0 + len(st.ins)], s_out[o0:o0 + len(st.inouts)],
                       s_out[o0 + len(st.inouts):o0 + len(st.inouts) + len(st.outs)],
                       lambda k, s0=s0: scr[n_scr].at[s0 + k])

        if not stages:
            core(*own_in, *own_out, *scr[:n_scr])
            return
        step = 0
        for d, g in enumerate(grid):
            step = step * g + pl.program_id(d)
        if steps == 1:
            run("start")
            core(*own_in, *own_out, *scr[:n_scr])
            run("mid")
            run("finish")
            return
        pl.when(step == 0)(lambda: run("start"))
        core(*own_in, *own_out, *scr[:n_scr])
        pl.when(step == (3 * steps) // 4)(lambda: run("mid"))
        pl.when(step == steps - 1)(lambda: run("finish"))

    sem = ("arbitrary",) * len(grid) if stages else ("parallel",) * max(len(grid) - 1, 0) + ("arbitrary",) * min(len(grid), 1)
    res = pl.pallas_call(
        body, name=name, grid=grid,
        in_specs=list(in_specs) + [ANY] * len(s_args),
        out_specs=list(out_specs) + [ANY] * len(s_outs),
        out_shape=list(out_shape) + s_outs,
        input_output_aliases=aliases,
        scratch_shapes=list(scratch_shapes) + ([pltpu.SemaphoreType.DMA((n_sems,))] if stages else []),
        compiler_params=_params(sem) if grid else pltpu.CompilerParams(vmem_limit_bytes=V7X_VMEM_LIMIT),
    )(*args, *s_args)
    return list(res[:n_out]), list(res[n_out:])


class _Pipe:
    def __init__(self):
        self.ready = []
        self.flushes = 0
        self.after = None

    def add(self, stage):
        self.ready.append(stage)

    def carry(self, call, long=True):
        stages = [st for st in self.ready if long or not st.slow]
        self.ready = [st for st in self.ready if not (long or not st.slow)]
        own, outs = call(stages)
        k = 0
        for st in stages:
            n = len(st.inouts) + len(st.outs)
            st.then(*outs[k:k + n])
            k += n
        if self.after is not None:
            self.after()
        return own

    def flush(self):
        while self.ready:
            self.flushes += 1
            self.carry(lambda stages: _staged_call(
                lambda *refs: None, name=f"comm_tail_{self.flushes}", grid=(), in_specs=[], out_specs=[], out_shape=[],
                scratch_shapes=[], args=[], stages=stages))


def _mixer_fwd(layer, x, g1, bgate, lng, lnb, wm, bsf, wsc, win_g, wb_g, wout_g, stages):
    t_len = x.shape[0]
    tm = min(TM_MIX, t_len)
    nt = t_len // tm
    nb = tm // GMLP_BLOCK

    def core(x_ref, x_late_ref, g1_ref, bgate_ref, lng_ref, lnb_ref, wm_ref, bsf_ref, wsc_ref, win_hbm, wb_hbm, wout_hbm,
             zc_ref, ya_ref, yb_ref, q_ref, sa_ref, ca_ref, sb_ref, cb_ref, ug_ref, fu_ref, xh_ref, cv_ref,
             mg_ref, h_ref, x2_ref,
             win_v, wb_v, wout_v, carry, vn_s, f_s, z_s, sems):
        i = pl.program_id(0)

        @pl.when(i == 0)
        def _():
            cps = (_load_col_sharded(win_hbm, win_v, sems, 0) + _load_branch(wb_hbm, wb_v, sems, 4)
                   + _load_row_sharded(wout_hbm, wout_v, sems, 12))
            _start_all(cps)
            carry[...] = jnp.zeros_like(carry)
            z_s[...] = jnp.zeros_like(z_s)
            _wait_all(cps)

        xv = x_ref[...]
        r = lax.rsqrt(jnp.mean(xv * xv, axis=-1, keepdims=True) + RMS_EPS)
        h_ref[...] = (xv * r * g1_ref[...]).astype(BF16)

        def zcols(c0, n, keep=None):
            zv = z_s[:, c0:c0 + n]
            z_s[:, c0:c0 + n] = _dot(h_ref[...], win_v[:, c0:c0 + n])
            if keep is not None:
                zc_ref[:, keep * D_B:(keep + 1) * D_B] = zv.astype(BF16)
            return zv

        v = zcols(C_V, D_A)
        vg, tv = _gelu(v)
        mu = jnp.mean(vg, axis=-1, keepdims=True)
        vc = vg - mu
        rstd = lax.rsqrt(jnp.mean(vc * vc, axis=-1, keepdims=True) + LN_EPS)
        xh = vc * rstd
        xh_ref[...] = xh.astype(BF16)
        cv_ref[...] = (rstd * _gelu_grad(v, tv)).astype(BF16)
        vn_s[...] = (xh * lng_ref[...] + lnb_ref[...]).astype(BF16)
        for hd in range(A_HEADS):
            cols = slice(hd * 128, (hd + 1) * 128)
            vcat = jnp.concatenate([vn_s[b * 128:(b + 1) * 128, cols] for b in range(nb)], axis=1)
            fcat = _dot(wm_ref[hd], vcat)
            for b in range(nb):
                f_s[b * 128:(b + 1) * 128, cols] = fcat[:, b * 128:(b + 1) * 128]
        u = zcols(C_U, D_A)
        ug, tu = _gelu(u)
        ug_ref[...] = ug.astype(BF16)
        fb = f_s[...] + jnp.concatenate([bsf_ref[...]] * nb, axis=0)
        fu_ref[...] = (fb * _gelu_grad(u, tu)).astype(BF16)
        ya_ref[...] = (ug * fb).astype(BF16)

        p = zcols(C_CG, D_B, keep=1) * zcols(C_HB, D_B, keep=2)
        cr = carry[...]
        q = wsc_ref[0:1, :] * _shift_down(p, cr, 2) + wsc_ref[1:2, :] * _shift_down(p, cr, 1) + wsc_ref[2:3, :] * p
        carry[...] = p[tm - 8:tm, :]
        q_ref[...] = q.astype(BF16)
        yb_ref[...] = (zcols(C_BG, D_B, keep=0) * q).astype(BF16)

        av = _dot(ya_ref[...], wb_v[0])
        sa = _sigmoid(zcols(C_GA, D_MODEL) + bgate_ref[:, 0:D_MODEL])
        sa_ref[...] = sa.astype(BF16)
        mg = sa * av
        ca_ref[...] = (mg * (1.0 - sa)).astype(BF16)
        bv = _dot(yb_ref[...], wb_v[1])
        sb = _sigmoid(zcols(C_GB, D_MODEL) + bgate_ref[:, D_MODEL:2 * D_MODEL])
        sb_ref[...] = sb.astype(BF16)
        mb = sb * bv
        cb_ref[...] = (mb * (1.0 - sb)).astype(BF16)
        mg_ref[...] = (mg + mb).astype(BF16)
        x2_ref[...] = x_late_ref[...] + _dot(mg_ref[...], wout_v[...])

    def tile(n, lag):
        return pl.BlockSpec((tm, n), lambda i: (jnp.clip(i - lag, 0, nt - 1), 0))

    outs = [
        jax.ShapeDtypeStruct((t_len, 3 * D_B), BF16),
        jax.ShapeDtypeStruct((t_len, D_A), BF16),
        jax.ShapeDtypeStruct((t_len, D_B), BF16),
        jax.ShapeDtypeStruct((t_len, D_B), BF16),
        jax.ShapeDtypeStruct((t_len, D_MODEL), BF16),
        jax.ShapeDtypeStruct((t_len, D_MODEL), BF16),
        jax.ShapeDtypeStruct((t_len, D_MODEL), BF16),
        jax.ShapeDtypeStruct((t_len, D_MODEL), BF16),
        jax.ShapeDtypeStruct((t_len, D_A), BF16),
        jax.ShapeDtypeStruct((t_len, D_A), BF16),
        jax.ShapeDtypeStruct((t_len, D_A), BF16),
        jax.ShapeDtypeStruct((t_len, D_A), BF16),
        jax.ShapeDtypeStruct((t_len, D_MODEL), BF16),
        jax.ShapeDtypeStruct((t_len, D_MODEL), BF16),
        jax.ShapeDtypeStruct((t_len, D_MODEL), F32),
    ]
    return _staged_call(
        core, name=f"mixer_fwd_l{layer}", grid=(nt + 1,),
        in_specs=[tile(D_MODEL, 0), tile(D_MODEL, 1), _const_spec((1, D_MODEL)), _const_spec((1, 2 * D_MODEL)),
                  _const_spec((1, D_A)), _const_spec((1, D_A)), _const_spec((A_HEADS, 128, 128)),
                  _const_spec((128, D_A)), _const_spec((8, D_B)), ANY, ANY, ANY],
        out_specs=[tile(o.shape[1], 0 if k == len(outs) - 2 else 1) for k, o in enumerate(outs)],
        out_shape=outs,
        scratch_shapes=[pltpu.VMEM((D_MODEL, D_IN), BF16), pltpu.VMEM((2, D_A, D_MODEL), BF16),
                        pltpu.VMEM((D_MODEL, D_MODEL), BF16), pltpu.VMEM((8, D_B), F32),
                        pltpu.VMEM((tm, D_A), BF16), pltpu.VMEM((tm, D_A), F32), pltpu.VMEM((tm, D_IN), F32),
                        pltpu.SemaphoreType.DMA((16,))],
        args=[x, x, g1, bgate, lng, lnb, wm, bsf, wsc, win_g, wb_g, wout_g], stages=stages)


def _ffn_fwd(layer, x2, g2, wfc, bfc, wup_g, wdown_g, stages, head=None):
    t_len = x2.shape[0]
    tm = min(TM_FFN, t_len)
    nt = t_len // tm

    def core(*refs):
        if head is None:
            (x_ref, g2_ref, wfc_ref, bfc_ref, wup_hbm, wdown_hbm, up_ref, silu_ref, dsilu_ref, act_ref, h_ref, x3_ref,
             wup_v, wdown_v, carry, sems) = refs
        else:
            (x_ref, g2_ref, wfc_ref, bfc_ref, t_ref, gf_ref, wup_hbm, wdown_hbm, up_ref, silu_ref, dsilu_ref, act_ref,
             h_ref, dx_ref, dgf_ref, loss_ref, wup_v, wdown_v, carry, sems) = refs
        i = pl.program_id(0)

        @pl.when(i == 0)
        def _():
            cps = _load_col_sharded(wup_hbm, wup_v, sems, 0) + _load_row_sharded(wdown_hbm, wdown_v, sems, 4)
            _start_all(cps)
            carry[...] = jnp.zeros_like(carry)
            if head is not None:
                dgf_ref[...] = jnp.zeros_like(dgf_ref)
                loss_ref[...] = jnp.zeros_like(loss_ref)
            _wait_all(cps)

        xv = x_ref[...]
        r = lax.rsqrt(jnp.mean(xv * xv, axis=-1, keepdims=True) + RMS_EPS)
        h_ref[...] = (xv * r * g2_ref[...]).astype(BF16)
        gate = _dot(h_ref[...], wup_v[:, 0:D_FF])
        up_ref[:, 0:D_FF] = gate.astype(BF16)
        cr = carry[...]
        gc = (wfc_ref[0:1, :] * _shift_down(gate, cr, 2) + wfc_ref[1:2, :] * _shift_down(gate, cr, 1)
              + wfc_ref[2:3, :] * gate + bfc_ref[...])
        carry[...] = gate[tm - 8:tm, :]
        sg = _sigmoid(gc)
        silu = gc * sg
        silu_ref[...] = silu.astype(BF16)
        dsilu_ref[...] = (sg + silu * (1.0 - sg)).astype(BF16)
        val = _dot(h_ref[...], wup_v[:, D_FF:2 * D_FF])
        up_ref[:, D_FF:2 * D_FF] = val.astype(BF16)
        act_ref[...] = (silu * val).astype(BF16)
        x3 = x_ref[...] + _dot(act_ref[...], wdown_v[...])
        if head is None:
            x3_ref[...] = x3
        else:
            r3 = lax.rsqrt(jnp.mean(x3 * x3, axis=-1, keepdims=True) + RMS_EPS)
            xh = x3 * r3
            err = xh * gf_ref[...] - t_ref[...]
            loss_ref[...] += _colsum8(err * err)
            dy = err * (1.0 / D_MODEL)
            dgf_ref[...] += _colsum8(dy * xh)
            dxh = dy * gf_ref[...]
            dx_ref[...] = r3 * (dxh - xh * jnp.mean(dxh * xh, axis=-1, keepdims=True))

    outs = [
        jax.ShapeDtypeStruct((t_len, 2 * D_FF), BF16),
        jax.ShapeDtypeStruct((t_len, D_FF), BF16),
        jax.ShapeDtypeStruct((t_len, D_FF), BF16),
        jax.ShapeDtypeStruct((t_len, D_FF), BF16),
        jax.ShapeDtypeStruct((t_len, D_MODEL), BF16),
        jax.ShapeDtypeStruct((t_len, D_MODEL), F32),
    ]
    in_specs = [_row_spec(tm, D_MODEL), _const_spec((1, D_MODEL)), _const_spec((8, D_FF)), _const_spec((1, D_FF))]
    out_specs = [_row_spec(tm, o.shape[1]) for o in outs]
    args = [x2, g2, wfc, bfc]
    if head is not None:
        in_specs += [_row_spec(tm, D_MODEL), _const_spec((1, D_MODEL))]
        args += list(head)
        outs += [jax.ShapeDtypeStruct((8, D_MODEL), F32)] * 2
        out_specs += [_const_spec((8, D_MODEL))] * 2
    return _staged_call(
        core, name=f"ffn_fwd_l{layer}", grid=(nt,),
        in_specs=in_specs + [ANY, ANY], out_specs=out_specs, out_shape=outs,
        scratch_shapes=[pltpu.VMEM((D_MODEL, 2 * D_FF), BF16), pltpu.VMEM((D_FF, D_MODEL), BF16),
                        pltpu.VMEM((8, D_FF), F32), pltpu.SemaphoreType.DMA((8,))],
        args=args + [wup_g, wdown_g], stages=stages)


def _ffn_bwd(layer, dx3, x2, up, silu, dsilu, g2, wfc, wup_g, wdown_g, stages):
    t_len = x2.shape[0]
    tm = min(TM_FFN, t_len)
    nt = t_len // tm

    def core(dx3_ref, dx3_late_ref, x_ref, up_ref, silu_ref, dsilu_ref, g2_ref, wfc_ref, wup_hbm, wdown_hbm,
             dx2_ref, dup_ref, dx3b_ref, dg2_ref, dbfc_ref, dwfc_ref,
             wup_v, wdown_v, carry, da_s, dup_s, sems):
        i = pl.program_id(0)

        @pl.when(i == 0)
        def _():
            cps = _load_col_sharded(wup_hbm, wup_v, sems, 0) + _load_row_sharded(wdown_hbm, wdown_v, sems, 4)
            _start_all(cps)
            for ref in (carry, da_s, dup_s, dg2_ref, dbfc_ref, dwfc_ref):
                ref[...] = jnp.zeros_like(ref)
            _wait_all(cps)

        live = (i <= nt).astype(F32)
        dx3b_ref[...] = dx3_ref[...].astype(BF16)
        dh = jnp.zeros((tm, D_MODEL), F32)
        for c0, c1 in FF_CHUNKS:
            v0, v1 = D_FF + c0, D_FF + c1
            dh = dh + _dot_nt(dup_s[:, c0:c1], wup_v[:, c0:c1]) + _dot_nt(dup_s[:, v0:v1], wup_v[:, v0:v1])
            da = da_s[:, c0:c1]
            dval = (da * silu_ref[:, c0:c1].astype(F32)).astype(BF16)
            dup_ref[:, v0:v1] = dval
            dup_s[:, v0:v1] = dval
            dgc = da * up_ref[:, v0:v1].astype(F32) * dsilu_ref[:, c0:c1].astype(F32)
            cr = carry[:, c0:c1]
            dgc1 = _shift_up(dgc, cr, 1)
            dgc2 = _shift_up(dgc, cr, 2)
            carry[:, c0:c1] = jnp.where(i < nt, dgc[0:8, :], cr)
            gate = up_ref[:, c0:c1].astype(F32)
            dbfc_ref[:, c0:c1] += live * _colsum8(dgc)
            dwfc_ref[0, :, c0:c1] += live * _colsum8(dgc2 * gate)
            dwfc_ref[1, :, c0:c1] += live * _colsum8(dgc1 * gate)
            dwfc_ref[2, :, c0:c1] += live * _colsum8(dgc * gate)
            dgate = (wfc_ref[2:3, c0:c1] * dgc + wfc_ref[1:2, c0:c1] * dgc1 + wfc_ref[0:1, c0:c1] * dgc2).astype(BF16)
            dup_ref[:, c0:c1] = dgate
            dup_s[:, c0:c1] = dgate
            da_s[:, c0:c1] = _dot_nt(dx3b_ref[...], wdown_v[c0:c1, :])
        xv = x_ref[...]
        r = lax.rsqrt(jnp.mean(xv * xv, axis=-1, keepdims=True) + RMS_EPS)
        xh = xv * r
        dg2_ref[...] += _colsum8(dh * xh)
        dxh = dh * g2_ref[...]
        dx2_ref[...] = dx3_late_ref[...] + r * (dxh - xh * jnp.mean(dxh * xh, axis=-1, keepdims=True))

    def tile(n, lag):
        return pl.BlockSpec((tm, n), lambda i: (nt - 1 - jnp.clip(i - lag, 0, nt - 1), 0))

    outs = [
        jax.ShapeDtypeStruct((t_len, D_MODEL), F32),
        jax.ShapeDtypeStruct((t_len, 2 * D_FF), BF16),
        jax.ShapeDtypeStruct((t_len, D_MODEL), BF16),
        jax.ShapeDtypeStruct((8, D_MODEL), F32),
        jax.ShapeDtypeStruct((8, D_FF), F32),
        jax.ShapeDtypeStruct((3, 8, D_FF), F32),
    ]
    return _staged_call(
        core, name=f"ffn_bwd_l{layer}", grid=(nt + 2,),
        in_specs=[tile(D_MODEL, 0), tile(D_MODEL, 2), tile(D_MODEL, 2), tile(2 * D_FF, 1), tile(D_FF, 1), tile(D_FF, 1),
                  _const_spec((1, D_MODEL)), _const_spec((8, D_FF)), ANY, ANY],
        out_specs=[tile(D_MODEL, 2), tile(2 * D_FF, 1), tile(D_MODEL, 0),
                   _const_spec((8, D_MODEL)), _const_spec((8, D_FF)), _const_spec((3, 8, D_FF))],
        out_shape=outs,
        scratch_shapes=[pltpu.VMEM((D_MODEL, 2 * D_FF), BF16), pltpu.VMEM((D_FF, D_MODEL), BF16),
                        pltpu.VMEM((8, D_FF), F32), pltpu.VMEM((tm, D_FF), F32), pltpu.VMEM((tm, 2 * D_FF), BF16),
                        pltpu.SemaphoreType.DMA((8,))],
        args=[dx3, dx3, x2, up, silu, dsilu, g2, wfc, wup_g, wdown_g], stages=stages)


def _mixer_bwd(layer, dx2, x, zc, qs, sa, ca, sb, cb, ug, fu, xhs, cv, g1, lng, lnb, wmt, wsc, win_g, wb_g, wout_g,
               stages):
    t_len = x.shape[0]
    tm = min(TM_MIX, t_len)
    nt = t_len // tm
    nb = tm // GMLP_BLOCK

    def core(dx2_ref, x_ref, zc_ref, q_ref, sa_ref, ca_ref, sb_ref, cb_ref, ug_ref, fu_ref, xh_ref, cv_ref,
             g1_ref, lng_ref, lnb_ref, wmt_ref, wsc_ref, win_hbm, wb_hbm, wout_hbm,
             dx_ref, dz_ref, da_ref, db_ref, dx2b_ref, dg1_ref, dbgate_ref, dlng_ref, dlnb_ref, dwm_ref, dbsf_ref, dwsc_ref,
             win_v, wb_v, wout_v, carry, vn_s, df_s, dvn_s, sems):
        i = pl.program_id(0)

        @pl.when(i == 0)
        def _():
            cps = (_load_col_sharded(win_hbm, win_v, sems, 0) + _load_branch(wb_hbm, wb_v, sems, 4)
                   + _load_row_sharded(wout_hbm, wout_v, sems, 12))
            _start_all(cps)
            for ref in (carry, dg1_ref, dbgate_ref, dlng_ref, dlnb_ref, dwm_ref, dbsf_ref, dwsc_ref):
                ref[...] = jnp.zeros_like(ref)
            _wait_all(cps)

        def kept(k):
            return zc_ref[:, k * D_B:(k + 1) * D_B].astype(F32)

        def dz_cols(c0, n, val):
            dz_ref[:, c0:c0 + n] = val.astype(BF16)
            return _dot_nt(dz_ref[:, c0:c0 + n], win_v[:, c0:c0 + n])

        dx2b_ref[...] = dx2_ref[...].astype(BF16)
        dm = _dot_nt(dx2b_ref[...], wout_v[...])
        da_ref[...] = (dm * sa_ref[...].astype(F32)).astype(BF16)
        dga = dm * ca_ref[...].astype(F32)
        dh = dz_cols(C_GA, D_MODEL, dga)
        dbgate_ref[:, 0:D_MODEL] += _colsum8(dga)
        dya = _dot_nt(da_ref[...], wb_v[0])
        db_ref[...] = (dm * sb_ref[...].astype(F32)).astype(BF16)
        dgb = dm * cb_ref[...].astype(F32)
        dh = dh + dz_cols(C_GB, D_MODEL, dgb)
        dbgate_ref[:, D_MODEL:2 * D_MODEL] += _colsum8(dgb)
        dyb = _dot_nt(db_ref[...], wb_v[1])

        xh = xh_ref[...].astype(F32)
        vn_s[...] = (xh * lng_ref[...] + lnb_ref[...]).astype(BF16)
        df = dya * ug_ref[...].astype(F32)
        df_s[...] = df.astype(BF16)
        dbsf_acc = df[0:128, :]
        for b in range(1, nb):
            dbsf_acc = dbsf_acc + df[b * 128:(b + 1) * 128, :]
        dbsf_ref[...] += dbsf_acc
        for hd in range(A_HEADS):
            cols = slice(hd * 128, (hd + 1) * 128)
            vcat = jnp.concatenate([vn_s[b * 128:(b + 1) * 128, cols] for b in range(nb)], axis=1)
            dcat = jnp.concatenate([df_s[b * 128:(b + 1) * 128, cols] for b in range(nb)], axis=1)
            gcat = _dot(wmt_ref[hd], dcat)
            dwm_ref[hd] += _dot_nt(dcat, vcat)
            for b in range(nb):
                dvn_s[b * 128:(b + 1) * 128, cols] = gcat[:, b * 128:(b + 1) * 128]
        dh = dh + dz_cols(C_U, D_A, dya * fu_ref[...].astype(F32))
        dvn = dvn_s[...]
        dlng_ref[...] += _colsum8(dvn * xh)
        dlnb_ref[...] += _colsum8(dvn)
        dxh = dvn * lng_ref[...]
        dvc = dxh - jnp.mean(dxh, axis=-1, keepdims=True) - xh * jnp.mean(dxh * xh, axis=-1, keepdims=True)
        dh = dh + dz_cols(C_V, D_A, dvc * cv_ref[...].astype(F32))

        cg = kept(1)
        hbv = kept(2)
        p = cg * hbv
        dh = dh + dz_cols(C_BG, D_B, dyb * q_ref[...].astype(F32))
        dq = dyb * kept(0)
        cr = carry[...]
        dq1 = _shift_up(dq, cr, 1)
        dq2 = _shift_up(dq, cr, 2)
        carry[...] = dq[0:8, :]
        dwsc_ref[0] += _colsum8(dq2 * p)
        dwsc_ref[1] += _colsum8(dq1 * p)
        dwsc_ref[2] += _colsum8(dq * p)
        dp = wsc_ref[2:3, :] * dq + wsc_ref[1:2, :] * dq1 + wsc_ref[0:1, :] * dq2
        dh = dh + dz_cols(C_CG, D_B, dp * hbv)
        dh = dh + dz_cols(C_HB, D_B, dp * cg)

        xv = x_ref[...]
        r = lax.rsqrt(jnp.mean(xv * xv, axis=-1, keepdims=True) + RMS_EPS)
        xn = xv * r
        dg1_ref[...] += _colsum8(dh * xn)
        dxn = dh * g1_ref[...]
        dx_ref[...] = dx2_ref[...] + r * (dxn - xn * jnp.mean(dxn * xn, axis=-1, keepdims=True))

    outs = [
        jax.ShapeDtypeStruct((t_len, D_MODEL), F32),
        jax.ShapeDtypeStruct((t_len, D_IN), BF16),
        jax.ShapeDtypeStruct((t_len, D_MODEL), BF16),
        jax.ShapeDtypeStruct((t_len, D_MODEL), BF16),
        jax.ShapeDtypeStruct((t_len, D_MODEL), BF16),
        jax.ShapeDtypeStruct((8, D_MODEL), F32),
        jax.ShapeDtypeStruct((8, 2 * D_MODEL), F32),
        jax.ShapeDtypeStruct((8, D_A), F32),
        jax.ShapeDtypeStruct((8, D_A), F32),
        jax.ShapeDtypeStruct((A_HEADS, 128, 128), F32),
        jax.ShapeDtypeStruct((128, D_A), F32),
        jax.ShapeDtypeStruct((3, 8, D_B), F32),
    ]

    return _staged_call(
        core, name=f"mixer_bwd_l{layer}", grid=(nt,),
        in_specs=[_row_spec(tm, D_MODEL, nt), _row_spec(tm, D_MODEL, nt), _row_spec(tm, 3 * D_B, nt),
                  _row_spec(tm, D_B, nt), _row_spec(tm, D_MODEL, nt), _row_spec(tm, D_MODEL, nt),
                  _row_spec(tm, D_MODEL, nt), _row_spec(tm, D_MODEL, nt), _row_spec(tm, D_A, nt), _row_spec(tm, D_A, nt),
                  _row_spec(tm, D_A, nt), _row_spec(tm, D_A, nt),
                  _const_spec((1, D_MODEL)), _const_spec((1, D_A)), _const_spec((1, D_A)),
                  _const_spec((A_HEADS, 128, 128)), _const_spec((8, D_B)), ANY, ANY, ANY],
        out_specs=[_row_spec(tm, D_MODEL, nt), _row_spec(tm, D_IN, nt), _row_spec(tm, D_MODEL, nt),
                   _row_spec(tm, D_MODEL, nt), _row_spec(tm, D_MODEL, nt),
                   _const_spec((8, D_MODEL)), _const_spec((8, 2 * D_MODEL)), _const_spec((8, D_A)), _const_spec((8, D_A)),
                   _const_spec((A_HEADS, 128, 128)), _const_spec((128, D_A)), _const_spec((3, 8, D_B))],
        out_shape=outs,
        scratch_shapes=[pltpu.VMEM((D_MODEL, D_IN), BF16), pltpu.VMEM((2, D_A, D_MODEL), BF16),
                        pltpu.VMEM((D_MODEL, D_MODEL), BF16), pltpu.VMEM((8, D_B), F32),
                        pltpu.VMEM((tm, D_A), BF16), pltpu.VMEM((tm, D_A), BF16), pltpu.VMEM((tm, D_A), F32),
                        pltpu.SemaphoreType.DMA((16,))],
        args=[dx2, x, zc, qs, sa, ca, sb, cb, ug, fu, xhs, cv, g1, lng, lnb, wmt, wsc, win_g, wb_g, wout_g],
        stages=stages)


def _wgrad(name, layer, a, b, rows, cols, row_blk, col_blk, stages, a_first=0):
    t_len = a.shape[0]
    n = b.shape[1]
    tk = min(TK_WGRAD, t_len)
    col_sharded = n == N_CHIPS * cols
    m = rows if col_sharded else a.shape[1]
    grid = (m // row_blk, n // col_blk, t_len // tk)
    shards = col_blk // cols if col_sharded else 1

    if col_sharded:
        out_shape = (N_CHIPS, rows, cols)
        out_spec = pl.BlockSpec((shards, row_blk, cols), lambda i, j, k: (j, i, 0))
    else:
        out_shape = (N_CHIPS * rows, cols)
        out_spec = pl.BlockSpec((row_blk, col_blk), lambda i, j, k: (i, j))

    def core(a_ref, b_ref, o_ref):
        @pl.when(pl.program_id(2) == 0)
        def _():
            o_ref[...] = jnp.zeros_like(o_ref)

        g = _dot_tn(a_ref[...], b_ref[...])
        if col_sharded:
            for q in range(shards):
                o_ref[q] += g[:, q * cols:(q + 1) * cols]
        else:
            o_ref[...] += g

    own, outs = _staged_call(
        core, name=f"wgrad_{name}_l{layer}", grid=grid,
        in_specs=[pl.BlockSpec((tk, row_blk), lambda i, j, k: (k, a_first + i)),
                  pl.BlockSpec((tk, col_blk), lambda i, j, k: (k, j))],
        out_specs=[out_spec], out_shape=[jax.ShapeDtypeStruct(out_shape, F32)], scratch_shapes=[],
        args=[a, b], stages=stages)
    return [own[0].reshape(N_CHIPS, rows, cols)], outs


def _wgrad_branch(layer, ya, da, yb, db, stages):
    t_len = ya.shape[0]
    tk = min(TK_WGRAD, t_len)

    cs = D_MODEL // N_CHIPS

    def core(ya_ref, da_ref, yb_ref, db_ref, o_ref):
        @pl.when(pl.program_id(0) == 0)
        def _():
            o_ref[...] = jnp.zeros_like(o_ref)

        ga = _dot_tn(ya_ref[...], da_ref[...])
        gb = _dot_tn(yb_ref[...], db_ref[...])
        for k in range(N_CHIPS):
            o_ref[k, 0:D_A, :] += ga[:, k * cs:(k + 1) * cs]
            o_ref[k, D_A:2 * D_A, :] += gb[:, k * cs:(k + 1) * cs]

    a_spec = pl.BlockSpec((tk, D_A), lambda k: (k, 0))
    d_spec = pl.BlockSpec((tk, D_MODEL), lambda k: (k, 0))
    return _staged_call(
        core, name=f"wgrad_w_branch_l{layer}", grid=(t_len // tk,),
        in_specs=[a_spec, d_spec, a_spec, d_spec],
        out_specs=[pl.BlockSpec((N_CHIPS, 2 * D_A, cs), lambda k: (0, 0, 0))],
        out_shape=[jax.ShapeDtypeStruct((N_CHIPS, 2 * D_A, cs), F32)], scratch_shapes=[],
        args=[ya, da, yb, db], stages=stages)


def _flat_blk(rows, cols):
    blk = rows
    while blk * cols * 4 > 2 * 1024 * 1024 and blk % 16 == 0:
        blk //= 2
    return blk


def _cast_into_slots(name, jobs, chip, stages):
    blks = [_flat_blk(w.shape[1], w.shape[2]) for w, _ in jobs]
    nblks = [w.shape[1] // b for (w, _), b in zip(jobs, blks)]
    n = len(jobs)
    out_shape = [jax.ShapeDtypeStruct((N_CHIPS,) + w.shape[1:], BF16) for w, _ in jobs]

    def core(*refs):
        for w_ref, o_ref in zip(refs[-2 * n:-n], refs[-n:]):
            o_ref[...] = w_ref[...].astype(BF16)

    def slot(*scalars):
        return scalars[0][0] if scalars else 2 * lax.axis_index("x") + lax.axis_index("y")

    in_specs = [pl.BlockSpec((None, b, w.shape[2]), lambda i, *s, la=la, k=k: (la, jnp.minimum(i, k - 1), 0))
                for (w, la), b, k in zip(jobs, blks, nblks)]
    out_specs = [pl.BlockSpec((None, b, w.shape[2]), lambda i, *s, k=k: (slot(*s), jnp.minimum(i, k - 1), 0))
                 for (w, _), b, k in zip(jobs, blks, nblks)]
    args = [w for w, _ in jobs]
    if stages:
        return _staged_call(core, name=f"cast_{name}", grid=(max(nblks),), in_specs=in_specs, out_specs=out_specs,
                            out_shape=out_shape, scratch_shapes=[], args=args, stages=stages)
    own = pl.pallas_call(
        core, name=f"cast_{name}",
        grid_spec=pltpu.PrefetchScalarGridSpec(num_scalar_prefetch=1, grid=(max(nblks),), in_specs=in_specs,
                                               out_specs=out_specs),
        out_shape=out_shape, compiler_params=_params(),
    )(chip, *args)
    return list(own), []


def _reduction_sums(name, jobs, pos):
    in_specs, out_specs, out_shape, args, bodies, counts = [], [], [], [], [], []
    for job in jobs:
        kind, grad, other = job[0], job[1], job[2]
        _, h, cols = other.shape
        blk = _flat_blk(h, cols)
        nblk = h // blk
        if kind == "pair":
            total = N_CHIPS * nblk

            def block(s, total=total, nblk=nblk):
                b = jnp.minimum(s, total - 1)
                return b // nblk, b % nblk

            spec = pl.BlockSpec((None, blk, cols), lambda s, p, block=block: (block(s)[0], block(s)[1], 0))
            in_specs += [pl.BlockSpec((None, blk, cols), lambda s, p, block=block, nblk=nblk:
                                      (block(s)[0], p[1] * nblk + block(s)[1], 0)), spec]
            out_specs.append(spec)
            out_shape.append(jax.ShapeDtypeStruct((N_CHIPS, h, cols), BF16))
            args += [grad, other]
            bodies.append((2, lambda g, o, out: out.__setitem__(..., (g[...] + o[...]).astype(BF16))))
        else:
            total = nblk

            def block(s, total=total):
                return jnp.minimum(s, total - 1)

            in_specs += [pl.BlockSpec((None, blk, cols), lambda s, p, block=block, nblk=nblk:
                                      (p[0], p[1] * nblk + block(s), 0)),
                         pl.BlockSpec((None, blk, cols), lambda s, p, block=block: (p[0], block(s), 0)),
                         pl.BlockSpec((3, blk, cols), lambda s, p, block=block: (0, block(s), 0))]
            out_specs.append(pl.BlockSpec((blk, cols), lambda s, p, block=block, nblk=nblk: (p[1] * nblk + block(s), 0)))
            out_shape.append(jax.ShapeDtypeStruct((2 * h, cols), F32))
            args += [grad, other, job[3]]
            bodies.append((3, lambda g, o, r, out: out.__setitem__(
                ..., (((g[...] + o[...]) + r[0].astype(F32)) + r[1].astype(F32)) + r[2].astype(F32))))
        counts.append(total)

    def body(pos_ref, *refs):
        ins, outs = refs[:len(args)], refs[len(args):]
        k = 0
        for (n_in, fn), out in zip(bodies, outs):
            fn(*ins[k:k + n_in], out)
            k += n_in

    return pl.pallas_call(
        body, name=f"reduction_sums_{name}",
        grid_spec=pltpu.PrefetchScalarGridSpec(num_scalar_prefetch=1, grid=(max(counts),), in_specs=in_specs,
                                               out_specs=out_specs),
        out_shape=out_shape,
        compiler_params=_params(),
    )(pos, *args)


def _sum_slots(name, slots):
    n, rows, _ = slots.shape

    def body(s_ref, o_ref):
        acc = s_ref[0]
        for d in range(1, n):
            acc = acc + s_ref[d]
        o_ref[...] = acc

    return pl.pallas_call(
        body, name=f"sum_slots_{name}", grid=(1,),
        in_specs=[pl.BlockSpec((n, rows, 128), lambda i: (0, 0, 0))],
        out_specs=pl.BlockSpec((rows, 128), lambda i: (0, 0)),
        out_shape=jax.ShapeDtypeStruct((rows, 128), F32),
        compiler_params=_params(),
    )(slots)


def _adamw_math(w, g, m, v):
    m2 = ADAM_B1 * m + (1.0 - ADAM_B1) * g
    v2 = ADAM_B2 * v + (1.0 - ADAM_B2) * (g * g)
    m_hat = m2 / (1.0 - ADAM_B1 ** ADAM_STEP)
    v_hat = v2 / (1.0 - ADAM_B2 ** ADAM_STEP)
    delta = -ADAM_LR * (m_hat / (jnp.sqrt(v_hat) + ADAM_EPS) + ADAM_WD * w)
    return delta, m2, v2


def _adamw_big(name, w, g0, g1, m, v):
    _, rows, cols = w.shape
    blk = _flat_blk(rows, cols) // 2

    def body(w_ref, g0_ref, g1_ref, m_ref, v_ref, g_ref, d_ref, m2_ref, v2_ref):
        g = jnp.where(pl.program_id(0) == 0, g0_ref[...], g1_ref[...])
        d, m2, v2 = _adamw_math(w_ref[...], g, m_ref[...], v_ref[...])
        g_ref[...] = g
        d_ref[...] = d
        m2_ref[...] = m2
        v2_ref[...] = v2

    spec = pl.BlockSpec((None, blk, cols), lambda la, i: (la, i, 0))
    return pl.pallas_call(
        body, name=f"adamw_{name}", grid=(N_LAYERS, rows // blk),
        in_specs=[spec, pl.BlockSpec((blk, cols), lambda la, i: (i * (1 - la), 0)),
                  pl.BlockSpec((blk, cols), lambda la, i: (i * la, 0)), spec, spec],
        out_specs=[spec] * 4,
        out_shape=[jax.ShapeDtypeStruct(w.shape, F32)] * 4,
        compiler_params=_params(("parallel", "parallel")),
    )(w, g0, g1, m, v)


def _adamw_small(ws, gs, ms, vs):
    n = len(ws)

    def body(*refs):
        ins, outs = refs[:4 * n], refs[4 * n:]
        for k in range(n):
            d, m2, v2 = _adamw_math(ins[k][...], ins[n + k][...], ins[2 * n + k][...], ins[3 * n + k][...])
            outs[k][...] = d
            outs[n + k][...] = m2
            outs[2 * n + k][...] = v2

    vmem = pl.BlockSpec(memory_space=pltpu.VMEM)
    return pl.pallas_call(
        body, name="adamw_small",
        in_specs=[vmem] * (4 * n), out_specs=[vmem] * (3 * n),
        out_shape=[jax.ShapeDtypeStruct(w.shape, F32) for w in ws] * 3,
        compiler_params=pltpu.CompilerParams(vmem_limit_bytes=V7X_VMEM_LIMIT),
    )(*ws, *gs, *ms, *vs)


SMALL = ("norm1_g", "b_gate", "gmlp_ln_g", "gmlp_ln_b", "w_spatial", "b_spatial", "w_shortconv", "norm2_g",
         "w_ffn_conv", "b_ffn_conv", "final_g")
ALL_WEIGHTS = ("norm1_g", "w_in", "b_gate", "gmlp_ln_g", "gmlp_ln_b", "w_spatial", "b_spatial", "w_shortconv",
               "w_branch", "w_out", "norm2_g", "w_ffn_up", "w_ffn_conv", "b_ffn_conv", "w_ffn_down", "final_g")


def _pack(arrays):
    flat = jnp.concatenate([a.reshape(-1) for a in arrays])
    n = flat.shape[0]
    rows = -(-n // 1024) * 8
    return jnp.pad(flat, (0, rows * 128 - n)).reshape(rows, 128)


def _unpack(packed, like):
    flat = packed.reshape(-1)
    out, off = [], 0
    for a in like:
        out.append(flat[off:off + a.size].reshape(a.shape))
        off += a.size
    return out


def _pad8(w):
    return jnp.pad(w, ((0, 5), (0, 0)))


def kernel(x, norm1_g, w_in, b_gate, gmlp_ln_g, gmlp_ln_b, w_spatial, b_spatial, w_shortconv, w_branch, w_out, norm2_g, w_ffn_up, w_ffn_conv, b_ffn_conv, w_ffn_down, final_g, loss_target, m_norm1_g, m_w_in, m_b_gate, m_gmlp_ln_g, m_gmlp_ln_b, m_w_spatial, m_b_spatial, m_w_shortconv, m_w_branch, m_w_out, m_norm2_g, m_w_ffn_up, m_w_ffn_conv, m_b_ffn_conv, m_w_ffn_down, m_final_g, v_norm1_g, v_w_in, v_b_gate, v_gmlp_ln_g, v_gmlp_ln_b, v_w_spatial, v_b_spatial, v_w_shortconv, v_w_branch, v_w_out, v_norm2_g, v_w_ffn_up, v_w_ffn_conv, v_b_ffn_conv, v_w_ffn_down, v_final_g):
    weights = dict(norm1_g=norm1_g, w_in=w_in, b_gate=b_gate, gmlp_ln_g=gmlp_ln_g, gmlp_ln_b=gmlp_ln_b,
                   w_spatial=w_spatial, b_spatial=b_spatial, w_shortconv=w_shortconv, w_branch=w_branch, w_out=w_out,
                   norm2_g=norm2_g, w_ffn_up=w_ffn_up, w_ffn_conv=w_ffn_conv, b_ffn_conv=b_ffn_conv,
                   w_ffn_down=w_ffn_down, final_g=final_g)
    mom = dict(norm1_g=m_norm1_g, w_in=m_w_in, b_gate=m_b_gate, gmlp_ln_g=m_gmlp_ln_g, gmlp_ln_b=m_gmlp_ln_b,
               w_spatial=m_w_spatial, b_spatial=m_b_spatial, w_shortconv=m_w_shortconv, w_branch=m_w_branch,
               w_out=m_w_out, norm2_g=m_norm2_g, w_ffn_up=m_w_ffn_up, w_ffn_conv=m_w_ffn_conv,
               b_ffn_conv=m_b_ffn_conv, w_ffn_down=m_w_ffn_down, final_g=m_final_g)
    vel = dict(norm1_g=v_norm1_g, w_in=v_w_in, b_gate=v_b_gate, gmlp_ln_g=v_gmlp_ln_g, gmlp_ln_b=v_gmlp_ln_b,
               w_spatial=v_w_spatial, b_spatial=v_b_spatial, w_shortconv=v_w_shortconv, w_branch=v_w_branch,
               w_out=v_w_out, norm2_g=v_norm2_g, w_ffn_up=v_w_ffn_up, w_ffn_conv=v_w_ffn_conv,
               b_ffn_conv=v_b_ffn_conv, w_ffn_down=v_w_ffn_down, final_g=v_final_g)

    cx, cy, cc = _mesh_pos()
    chip = 2 * cx + cy
    pos_arr = jnp.stack([chip, cc]).astype(jnp.int32)
    t_len = x.shape[1]
    xs = x.reshape(t_len, D_MODEL)
    target = loss_target.reshape(t_len, D_MODEL)
    pipe = _Pipe()

    full = {}

    mixer_w = ("w_in", "w_branch", "w_out")
    ffn_w = ("w_ffn_up", "w_ffn_down")
    slots = {}

    def cast(name, keys, stages):
        own, outs = _cast_into_slots(name, [(weights[n].reshape((N_LAYERS,) + BIG[n]), la) for n, la in keys],
                                     chip.astype(jnp.int32).reshape(1), stages)
        slots.update(zip(keys, own))
        return own, outs

    def gather(names, la):
        def then(*bufs):
            full.update(zip([(n, la) for n in names], bufs))

        pipe.add(_gather_stage([slots[(n, la)] for n in names], then))

    first = [(n, 0) for n in mixer_w]
    cast("first", first, [])
    gather(mixer_w, 0)
    tap_slots = {}
    pipe.add(_chip_spread_stage(_pack([w_shortconv, w_ffn_conv]), lambda got: tap_slots.__setitem__("all", got)))
    pipe.carry(lambda st: cast("rest", [(n, la) for la in range(N_LAYERS) for n in BIG_NAMES if (n, la) not in first], st))
    by_chip = [_unpack(tap_slots["all"][k], [w_shortconv, w_ffn_conv]) for k in range(N_CHIPS)]
    wsc_full = jnp.concatenate([t[0] for t in by_chip], axis=-1)
    wfc_full = jnp.concatenate([t[1] for t in by_chip], axis=-1)

    idx = jnp.arange(GMLP_BLOCK) // CHUNK
    mask = idx[None, :] <= idx[:, None]
    wm_all = jnp.where(mask[None, None], w_spatial, 0.0)
    wm_bf = wm_all.astype(BF16)
    wmt_bf = jnp.swapaxes(wm_all, -1, -2).astype(BF16)
    bsf = jnp.repeat(jnp.swapaxes(b_spatial, -1, -2), 128, axis=-1)

    def row(a):
        return a.reshape(1, -1)

    def mixer_args(la):
        return (row(norm1_g[la]), row(b_gate[la]), row(gmlp_ln_g[la]), row(gmlp_ln_b[la]))

    def mixer_weights(la):
        return tuple(full[(n, la)] for n in mixer_w)

    def ffn_weights(la):
        return tuple(full[(n, la)] for n in ffn_w)

    saved = []
    h_in = xs
    for la in range(N_LAYERS):
        gather(ffn_w, la)
        *kept, mg, h1, x2 = pipe.carry(lambda st: _mixer_fwd(
            la, h_in, *mixer_args(la), wm_bf[la], bsf[la], _pad8(wsc_full[la]), *mixer_weights(la), st))
        ya, yb = kept[1], kept[2]
        if la + 1 < N_LAYERS:
            gather(mixer_w, la + 1)
        head = (target, row(final_g)) if la == N_LAYERS - 1 else None
        up, silu, dsilu, act, h2, *rest = pipe.carry(lambda st: _ffn_fwd(
            la, x2, row(norm2_g[la]), _pad8(wfc_full[la]), row(b_ffn_conv[la]), *ffn_weights(la), st, head=head))
        saved.append(dict(x=h_in, ya=ya, yb=yb, mixer=[kept[0]] + kept[3:], mg=mg, h1=h1, x2=x2, up=up, silu=silu,
                          dsilu=dsilu, act=act, h2=h2))
        h_in = rest[0]
    dx, dgf8, loss8 = rest

    reduced_big = {}

    sums_due = []

    def run_sums():
        if sums_due:
            due = list(sums_due)
            sums_due.clear()
            run_sums.calls += 1
            for (_, then), res in zip(due, _reduction_sums(str(run_sums.calls), [job for job, _ in due], pos_arr)):
                then(res)

    run_sums.calls = 0
    pipe.after = run_sums

    def reduce_big(name, la, grad):
        def after_pair(other):
            def after_chips(got):
                sums_due.append((("chip", grad, other, got), lambda final: pipe.add(_pair_fill_stage(
                    final, lambda done: reduced_big.__setitem__((name, la), done)))))

            sums_due.append((("pair", grad, other), lambda psum: pipe.add(_chip_send_stage(psum, after_chips))))

        pipe.add(_pair_send_stage(grad, after_pair))

    small = {n: [None] * N_LAYERS for n in SMALL}
    spread = {}
    for la in reversed(range(N_LAYERS)):
        s = saved[la]
        dx3 = dx
        run = pipe.carry if la > 0 else (lambda call: call([])[0])
        dx2, dup, dx3b, dg2, dbfc, dwfc = run(lambda st: _ffn_bwd(
            la, dx3, s["x2"], s["up"], s["silu"], s["dsilu"], row(norm2_g[la]), _pad8(wfc_full[la]),
            *ffn_weights(la), st))
        g, = pipe.carry(lambda st: _wgrad("w_ffn_up", la, s["h2"], dup, 1024, 1408, 512, 2816, st))
        reduce_big("w_ffn_up", la, g)
        g, = pipe.carry(lambda st: _wgrad("w_ffn_down", la, s["act"], dx3b, 704, 1024, 1408, 1024, st))
        reduce_big("w_ffn_down", la, g)
        dxl, dz, da, db, dx2b, dg1, dbg, dlng, dlnb, dwm, dbsf, dwsc = run(lambda st: _mixer_bwd(
            la, dx2, s["x"], *s["mixer"], row(norm1_g[la]), row(gmlp_ln_g[la]), row(gmlp_ln_b[la]), wmt_bf[la],
            _pad8(wsc_full[la]), *mixer_weights(la), st))
        small["norm1_g"][la] = dg1.sum(0)
        small["b_gate"][la] = dbg.sum(0)
        small["gmlp_ln_g"][la] = dlng.sum(0)
        small["gmlp_ln_b"][la] = dlnb.sum(0)
        small["w_spatial"][la] = jnp.where(mask[None], dwm, 0.0)
        small["b_spatial"][la] = dbsf.reshape(128, A_HEADS, 128).sum(-1).T
        small["w_shortconv"][la] = dwsc.sum(1)
        small["norm2_g"][la] = dg2.sum(0)
        small["w_ffn_conv"][la] = dwfc.sum(1)
        small["b_ffn_conv"][la] = dbfc.sum(0)
        if la == 0:
            small_local = ([jnp.stack(small[n]) for n in SMALL[:-1]]
                           + [dgf8.sum(0), 0.5 * loss8.sum().reshape(1) / D_MODEL])
            mine = _pack(small_local)

            def after_swap(other, mine=mine):
                pair = _sum_slots("small_pair", jnp.stack([mine, other]))
                pipe.add(_chip_spread_stage(pair, lambda slots: spread.__setitem__("slots", slots)))

            pipe.add(_pair_swap_stage(mine, after_swap))
        if la > 0:
            g, = pipe.carry(lambda st: _wgrad("w_in", la, s["h1"], dz, 1024, 1152, 512, 2304, st))
            reduce_big("w_in", la, g)
        else:
            for part, tag in enumerate(("w_in_a", "w_in_b")):
                g, = pipe.carry(lambda st: _wgrad(tag, la, s["h1"], dz, 512, 1152, 512, 2304, st, a_first=part))
                reduce_big(tag, la, g)
        g, = pipe.carry(lambda st: _wgrad("w_out", la, s["mg"], dx2b, 256, 1024, 1024, 1024, st), long=False)
        reduce_big("w_out", la, g)
        g, = pipe.carry(lambda st: _wgrad_branch(la, s["ya"], da, s["yb"], db, st), long=False)
        reduce_big("w_branch", la, g)
        dx = dxl
    grad_x = dx.reshape(x.shape)
    pipe.flush()

    reduced_big[("w_in", 0)] = jnp.concatenate([reduced_big[("w_in_a", 0)], reduced_big[("w_in_b", 0)]], axis=0)
    reduced = _unpack(_sum_slots("small_grads", spread["slots"]), small_local)
    loss = reduced[-1].reshape(())
    grads = dict(zip(SMALL, reduced[:-1]))
    grads["w_shortconv"] = lax.dynamic_slice(grads["w_shortconv"], (0, 0, chip * (D_B // 4)), (N_LAYERS, 3, D_B // 4))
    grads["w_ffn_conv"] = lax.dynamic_slice(grads["w_ffn_conv"], (0, 0, chip * (D_FF // 4)), (N_LAYERS, 3, D_FF // 4))

    delta, new_m, new_v = {}, {}, {}
    for n in BIG_NAMES:
        shape3 = (N_LAYERS,) + BIG[n]
        res = _adamw_big(n, weights[n].reshape(shape3), reduced_big[(n, 0)], reduced_big[(n, 1)],
                         mom[n].reshape(shape3), vel[n].reshape(shape3))
        grads[n], delta[n], new_m[n], new_v[n] = (a.reshape(weights[n].shape) for a in res)
    res = _adamw_small(*[[src[n].reshape(-1, src[n].shape[-1]) for n in SMALL] for src in (weights, grads, mom, vel)])
    for k, n in enumerate(SMALL):
        delta[n], new_m[n], new_v[n] = (res[j * len(SMALL) + k].reshape(weights[n].shape) for j in range(3))

    return (loss, grad_x, *[grads[n] for n in ALL_WEIGHTS], *[delta[n] for n in ALL_WEIGHTS],
            *[new_m[n] for n in ALL_WEIGHTS], *[new_v[n] for n in ALL_WEIGHTS])
```

```python
import jax
import jax.numpy as jnp
from jax import lax
from jax.experimental import pallas as pl
from jax.experimental.pallas import tpu as pltpu

F32 = jnp.float32
BF16 = jnp.bfloat16
MESH = pl.DeviceIdType.MESH
ANY = pl.BlockSpec(memory_space=pl.ANY)

D_MODEL = 1024
D_A = 512
D_B = 512
D_IN = 4608
D_FF = 2816
GMLP_BLOCK = 128
CHUNK = 64
A_HEADS = 4
N_LAYERS = 2
N_CHIPS = 4
RMS_EPS = 1e-6
LN_EPS = 1e-5
ADAM_LR = 0.001
ADAM_B1 = 0.9
ADAM_B2 = 0.999
ADAM_EPS = 1e-08
ADAM_WD = 0.01
ADAM_STEP = 10

C_U, C_V, C_BG, C_CG, C_HB, C_GA, C_GB = 0, 512, 1024, 1536, 2048, 2560, 3584

V7X_VMEM_LIMIT = 60 * 1024 * 1024
TM_MIX = 256
TM_FFN = 256
TK_WGRAD = 2048
SLOW_COPY_BYTES = 640 * 1024
FF_CHUNKS = ((0, 768), (768, 1536), (1536, 2304), (2304, 2816))
GELU_C0 = 0.7978845608028654
GELU_C1 = 0.044715

BIG = {
    "w_in": (1024, 1152),
    "w_branch": (1024, 256),
    "w_out": (256, 1024),
    "w_ffn_up": (1024, 1408),
    "w_ffn_down": (704, 1024),
}
BIG_NAMES = tuple(BIG)


def _params(sem=("arbitrary",), vmem=V7X_VMEM_LIMIT):
    return pltpu.CompilerParams(dimension_semantics=sem, vmem_limit_bytes=vmem)


def _gelu(x):
    x2 = x * x
    t = jnp.tanh(GELU_C0 * x * (1.0 + GELU_C1 * x2))
    return 0.5 * x * (1.0 + t), t


def _gelu_grad(x, t):
    return 0.5 * (1.0 + t) + 0.5 * x * (1.0 - t * t) * GELU_C0 * (1.0 + 3.0 * GELU_C1 * x * x)


def _colsum8(v):
    r, n = v.shape
    return v.reshape(r // 8, 8, n).sum(axis=0)


def _dot(a, b):
    return jnp.dot(a, b, preferred_element_type=F32)


def _dot_nt(a, b):
    return lax.dot_general(a, b, (((1,), (1,)), ((), ())), preferred_element_type=F32)


def _dot_tn(a, b):
    return lax.dot_general(a, b, (((0,), (0,)), ((), ())), preferred_element_type=F32)


def _shift_down(v, carry, n):
    rows = lax.broadcasted_iota(jnp.int32, (8, v.shape[1]), 0)
    out = pltpu.roll(v, n, 0)
    head = out[0:8, :]
    for r in range(n):
        head = jnp.where(rows == r, carry[8 - n + r:8 - n + r + 1, :], head)
    return jnp.concatenate([head, out[8:, :]], axis=0)


def _shift_up(v, carry, n):
    tm = v.shape[0]
    rows = lax.broadcasted_iota(jnp.int32, (8, v.shape[1]), 0)
    out = pltpu.roll(v, tm - n, 0)
    tail = out[tm - 8:tm, :]
    for r in range(n):
        tail = jnp.where(rows == 8 - n + r, carry[r:r + 1, :], tail)
    return jnp.concatenate([out[0:tm - 8, :], tail], axis=0)


def _sigmoid(x):
    return 0.5 * jnp.tanh(0.5 * x) + 0.5


def _start_all(copies):
    for cp in copies:
        cp.start()


def _wait_all(copies):
    for cp in copies:
        cp.wait()


def _load_col_sharded(src, dst, sems, first):
    cs = src.shape[-1]
    return [pltpu.make_async_copy(src.at[k], dst.at[:, k * cs:(k + 1) * cs], sems.at[first + k])
            for k in range(N_CHIPS)]


def _load_row_sharded(src, dst, sems, first):
    rs = src.shape[-2]
    return [pltpu.make_async_copy(src.at[k], dst.at[k * rs:(k + 1) * rs, :], sems.at[first + k])
            for k in range(N_CHIPS)]


def _load_branch(src, dst, sems, first):
    return [pltpu.make_async_copy(src.at[k, pl.ds(m * D_A, D_A), :], dst.at[m, :, k * 256:(k + 1) * 256],
                                  sems.at[first + 2 * k + m])
            for k in range(N_CHIPS) for m in range(2)]


def _row_spec(tm, n, rev=None):
    if rev is None:
        return pl.BlockSpec((tm, n), lambda i: (i, 0))
    return pl.BlockSpec((tm, n), lambda i: (rev - 1 - i, 0))


def _const_spec(shape):
    nd = len(shape)
    return pl.BlockSpec(shape, lambda i: (0,) * nd)


def _mesh_pos():
    return lax.axis_index("x"), lax.axis_index("y"), lax.axis_index("c")


def _other_chips(x, y):
    return [(1 - x, y, 2 * (1 - x) + y), (x, 1 - y, 2 * x + (1 - y)), (1 - x, 1 - y, 2 * (1 - x) + (1 - y))]


def _remote(src, dst, ssem, rsem, to):
    return pltpu.make_async_remote_copy(src_ref=src, dst_ref=dst, send_sem=ssem, recv_sem=rsem, device_id=to,
                                        device_id_type=MESH)


def _half(ref, which, h):
    start = pl.multiple_of(which * h, 8)
    if len(ref.shape) == 2:
        return ref.at[pl.ds(start, h), :]
    return ref.at[:, pl.ds(start, h), :]


class _Stage:
    def __init__(self, ins=(), inouts=(), outs=(), n_sems=0, start=None, mid=None, finish=None, then=None, slow=False):
        self.ins, self.inouts, self.outs = list(ins), list(inouts), list(outs)
        self.n_sems, self.start, self.mid, self.finish, self.then = n_sems, start, mid, finish, then
        self.slow = slow


def _gather_stage(bufs, then):
    n = len(bufs)

    def copies(io, sem):
        x, y, c = _mesh_pos()
        me = 2 * x + y
        ici, fwd, got = [], [], []
        for w in range(n):
            h = io[w].shape[1] // 2
            for j, (px, py, pk) in enumerate(_other_chips(x, y)):
                mine = _half(io[w].at[me], c, h)
                theirs = _half(io[w].at[pk], c, h)
                ici.append(_remote(mine, mine, sem(12 * w + j), sem(12 * w + 3 + j), (px, py, c)))
                got.append(_remote(theirs, theirs, sem(12 * w + j), sem(12 * w + 3 + j), (px, py, c)))
                fwd.append(_remote(theirs, theirs, sem(12 * w + 6 + j), sem(12 * w + 9 + j), (x, y, 1 - c)))
        return ici, got, fwd

    def start(ins, io, outs, sem):
        _start_all(copies(io, sem)[0])

    def mid(ins, io, outs, sem):
        _, got, fwd = copies(io, sem)
        for g, f in zip(got, fwd):
            g.wait_recv()
            f.start()

    def finish(ins, io, outs, sem):
        x, y, c = _mesh_pos()
        ici, _, fwd = copies(io, sem)
        for w in range(n):
            h = io[w].shape[1] // 2
            for j, (px, py, pk) in enumerate(_other_chips(x, y)):
                other = _half(io[w].at[pk], 1 - c, h)
                _remote(other, other, sem(12 * w + 6 + j), sem(12 * w + 9 + j), (x, y, 1 - c)).wait_recv()
        for cp in ici + fwd:
            cp.wait_send()

    return _Stage(inouts=bufs, n_sems=12 * n, start=start, mid=mid, finish=finish, then=then)


def _pair_send_stage(grad, then):
    h = grad.shape[1] // 2

    def copy(ins, outs, sem):
        x, y, c = _mesh_pos()
        return _remote(_half(ins[0], 1 - c, h), outs[0], sem(0), sem(1), (x, y, 1 - c))

    return _Stage(ins=[grad], outs=[jax.ShapeDtypeStruct((N_CHIPS, h, grad.shape[2]), F32)], n_sems=2,
                  start=lambda ins, io, outs, sem: copy(ins, outs, sem).start(),
                  finish=lambda ins, io, outs, sem: copy(ins, outs, sem).wait(), then=then)


def _chip_send_stage(psum, then):
    def copies(ins, outs, sem):
        x, y, c = _mesh_pos()
        return [_remote(ins[0].at[pk], outs[0].at[j], sem(j), sem(3 + j), (px, py, c))
                for j, (px, py, pk) in enumerate(_other_chips(x, y))]

    return _Stage(ins=[psum], outs=[jax.ShapeDtypeStruct((3,) + psum.shape[1:], BF16)], n_sems=6,
                  start=lambda ins, io, outs, sem: _start_all(copies(ins, outs, sem)),
                  finish=lambda ins, io, outs, sem: _wait_all(copies(ins, outs, sem)), then=then,
                  slow=psum.shape[1] * psum.shape[2] * 2 > SLOW_COPY_BYTES)


def _pair_fill_stage(final, then):
    h = final.shape[0] // 2

    def copy(io, sem):
        x, y, c = _mesh_pos()
        mine = _half(io[0], c, h)
        return _remote(mine, mine, sem(0), sem(1), (x, y, 1 - c))

    return _Stage(inouts=[final], n_sems=2,
                  start=lambda ins, io, outs, sem: copy(io, sem).start(),
                  finish=lambda ins, io, outs, sem: copy(io, sem).wait(), then=then)


def _pair_swap_stage(packed, then):
    def copy(ins, outs, sem):
        x, y, c = _mesh_pos()
        return _remote(ins[0], outs[0], sem(0), sem(1), (x, y, 1 - c))

    return _Stage(ins=[packed], outs=[jax.ShapeDtypeStruct(packed.shape, F32)], n_sems=2,
                  start=lambda ins, io, outs, sem: copy(ins, outs, sem).start(),
                  finish=lambda ins, io, outs, sem: copy(ins, outs, sem).wait(), then=then)


def _chip_spread_stage(psum, then):
    def copies(ins, outs, sem):
        x, y, c = _mesh_pos()
        me = 2 * x + y
        cps = [_remote(ins[0], outs[0].at[me], sem(j), sem(3 + j), (px, py, c))
               for j, (px, py, pk) in enumerate(_other_chips(x, y))]
        return cps, pltpu.make_async_copy(ins[0], outs[0].at[me], sem(6))

    def start(ins, io, outs, sem):
        cps, own = copies(ins, outs, sem)
        own.start()
        _start_all(cps)

    def finish(ins, io, outs, sem):
        cps, own = copies(ins, outs, sem)
        _wait_all(cps)
        own.wait()

    return _Stage(ins=[psum], outs=[jax.ShapeDtypeStruct((N_CHIPS,) + psum.shape, F32)], n_sems=7,
                  start=start, finish=finish, then=then)


def _staged_call(core, *, name, grid, in_specs, out_specs, out_shape, scratch_shapes, args, stages):
    n_in, n_out, n_scr = len(args), len(out_shape), len(scratch_shapes)
    s_args, s_outs, aliases, layout = [], [], {}, []
    n_sems = 0
    for st in stages:
        i0, o0 = len(s_args), len(s_outs)
        s_args += st.ins + st.inouts
        for q in range(len(st.inouts)):
            aliases[n_in + i0 + len(st.ins) + q] = n_out + o0 + q
        s_outs += [jax.ShapeDtypeStruct(a.shape, a.dtype) for a in st.inouts] + st.outs
        layout.append((i0, o0, n_sems))
        n_sems += st.n_sems
    steps = 1
    for g in grid:
        steps *= g

    def body(*refs):
        own_in = refs[:n_in]
        s_in = refs[n_in:n_in + len(s_args)]
        rest = refs[n_in + len(s_args):]
        own_out = rest[:n_out]
        s_out = rest[n_out:n_out + len(s_outs)]
        scr = rest[n_out + len(s_outs):]

        def run(which):
            for st, (i0, o0, s0) in zip(stages, layout):
                fn = getattr(st, which)
                if fn is not None:
                    fn(s_in[i0:i0 + len(st.ins)], s_out[o0:o0 + len(st.inouts)],
                       s_out[o0 + len(st.inouts):o0 + len(st.inouts) + len(st.outs)],
                       lambda k, s0=s0: scr[n_scr].at[s0 + k])

        if not stages:
            core(*own_in, *own_out, *scr[:n_scr])
            return
        step = 0
        for d, g in enumerate(grid):
            step = step * g + pl.program_id(d)
        if steps == 1:
            run("start")
            core(*own_in, *own_out, *scr[:n_scr])
            run("mid")
            run("finish")
            return
        pl.when(step == 0)(lambda: run("start"))
        core(*own_in, *own_out, *scr[:n_scr])
        pl.when(step == (3 * steps) // 4)(lambda: run("mid"))
        pl.when(step == steps - 1)(lambda: run("finish"))

    sem = ("arbitrary",) * len(grid) if stages else ("parallel",) * max(len(grid) - 1, 0) + ("arbitrary",) * min(len(grid), 1)
    res = pl.pallas_call(
        body, name=name, grid=grid,
        in_specs=list(in_specs) + [ANY] * len(s_args),
        out_specs=list(out_specs) + [ANY] * len(s_outs),
        out_shape=list(out_shape) + s_outs,
        input_output_aliases=aliases,
        scratch_shapes=list(scratch_shapes) + ([pltpu.SemaphoreType.DMA((n_sems,))] if stages else []),
        compiler_params=_params(sem) if grid else pltpu.CompilerParams(vmem_limit_bytes=V7X_VMEM_LIMIT),
    )(*args, *s_args)
    return list(res[:n_out]), list(res[n_out:])


class _Pipe:
    def __init__(self):
        self.ready = []
        self.flushes = 0
        self.after = None

    def add(self, stage):
        self.ready.append(stage)

    def carry(self, call, long=True):
        stages = [st for st in self.ready if long or not st.slow]
        self.ready = [st for st in self.ready if not (long or not st.slow)]
        own, outs = call(stages)
        k = 0
        for st in stages:
            n = len(st.inouts) + len(st.outs)
            st.then(*outs[k:k + n])
            k += n
        if self.after is not None:
            self.after()
        return own

    def flush(self):
        while self.ready:
            self.flushes += 1
            self.carry(lambda stages: _staged_call(
                lambda *refs: None, name=f"comm_tail_{self.flushes}", grid=(), in_specs=[], out_specs=[], out_shape=[],
                scratch_shapes=[], args=[], stages=stages))


def _mixer_fwd(layer, x, g1, bgate, lng, lnb, wm, bsf, wsc, win_g, wb_g, wout_g, stages):
    t_len = x.shape[0]
    tm = min(TM_MIX, t_len)
    nt = t_len // tm
    nb = tm // GMLP_BLOCK

    def core(x_ref, x_late_ref, g1_ref, bgate_ref, lng_ref, lnb_ref, wm_ref, bsf_ref, wsc_ref, win_hbm, wb_hbm, wout_hbm,
             zc_ref, ya_ref, yb_ref, q_ref, sa_ref, ca_ref, sb_ref, cb_ref, ug_ref, fu_ref, xh_ref, cv_ref,
             mg_ref, h_ref, x2_ref,
             win_v, wb_v, wout_v, carry, vn_s, f_s, z_s, sems):
        i = pl.program_id(0)

        @pl.when(i == 0)
        def _():
            cps = (_load_col_sharded(win_hbm, win_v, sems, 0) + _load_branch(wb_hbm, wb_v, sems, 4)
                   + _load_row_sharded(wout_hbm, wout_v, sems, 12))
            _start_all(cps)
            carry[...] = jnp.zeros_like(carry)
            z_s[...] = jnp.zeros_like(z_s)
            _wait_all(cps)

        xv = x_ref[...]
        r = lax.rsqrt(jnp.mean(xv * xv, axis=-1, keepdims=True) + RMS_EPS)
        h_ref[...] = (xv * r * g1_ref[...]).astype(BF16)

        def zcols(c0, n, keep=None):
            zv = z_s[:, c0:c0 + n]
            z_s[:, c0:c0 + n] = _dot(h_ref[...], win_v[:, c0:c0 + n])
            if keep is not None:
                zc_ref[:, keep * D_B:(keep + 1) * D_B] = zv.astype(BF16)
            return zv

        v = zcols(C_V, D_A)
        vg, tv = _gelu(v)
        mu = jnp.mean(vg, axis=-1, keepdims=True)
        vc = vg - mu
        rstd = lax.rsqrt(jnp.mean(vc * vc, axis=-1, keepdims=True) + LN_EPS)
        xh = vc * rstd
        xh_ref[...] = xh.astype(BF16)
        cv_ref[...] = (rstd * _gelu_grad(v, tv)).astype(BF16)
        vn_s[...] = (xh * lng_ref[...] + lnb_ref[...]).astype(BF16)
        for hd in range(A_HEADS):
            cols = slice(hd * 128, (hd + 1) * 128)
            vcat = jnp.concatenate([vn_s[b * 128:(b + 1) * 128, cols] for b in range(nb)], axis=1)
            fcat = _dot(wm_ref[hd], vcat)
            for b in range(nb):
                f_s[b * 128:(b + 1) * 128, cols] = fcat[:, b * 128:(b + 1) * 128]
        u = zcols(C_U, D_A)
        ug, tu = _gelu(u)
        ug_ref[...] = ug.astype(BF16)
        fb = f_s[...] + jnp.concatenate([bsf_ref[...]] * nb, axis=0)
        fu_ref[...] = (fb * _gelu_grad(u, tu)).astype(BF16)
        ya_ref[...] = (ug * fb).astype(BF16)

        p = zcols(C_CG, D_B, keep=1) * zcols(C_HB, D_B, keep=2)
        cr = carry[...]
        q = wsc_ref[0:1, :] * _shift_down(p, cr, 2) + wsc_ref[1:2, :] * _shift_down(p, cr, 1) + wsc_ref[2:3, :] * p
        carry[...] = p[tm - 8:tm, :]
        q_ref[...] = q.astype(BF16)
        yb_ref[...] = (zcols(C_BG, D_B, keep=0) * q).astype(BF16)

        av = _dot(ya_ref[...], wb_v[0])
        sa = _sigmoid(zcols(C_GA, D_MODEL) + bgate_ref[:, 0:D_MODEL])
        sa_ref[...] = sa.astype(BF16)
        mg = sa * av
        ca_ref[...] = (mg * (1.0 - sa)).astype(BF16)
        bv = _dot(yb_ref[...], wb_v[1])
        sb = _sigmoid(zcols(C_GB, D_MODEL) + bgate_ref[:, D_MODEL:2 * D_MODEL])
        sb_ref[...] = sb.astype(BF16)
        mb = sb * bv
        cb_ref[...] = (mb * (1.0 - sb)).astype(BF16)
        mg_ref[...] = (mg + mb).astype(BF16)
        x2_ref[...] = x_late_ref[...] + _dot(mg_ref[...], wout_v[...])

    def tile(n, lag):
        return pl.BlockSpec((tm, n), lambda i: (jnp.clip(i - lag, 0, nt - 1), 0))

    outs = [
        jax.ShapeDtypeStruct((t_len, 3 * D_B), BF16),
        jax.ShapeDtypeStruct((t_len, D_A), BF16),
        jax.ShapeDtypeStruct((t_len, D_B), BF16),
        jax.ShapeDtypeStruct((t_len, D_B), BF16),
        jax.ShapeDtypeStruct((t_len, D_MODEL), BF16),
        jax.ShapeDtypeStruct((t_len, D_MODEL), BF16),
        jax.ShapeDtypeStruct((t_len, D_MODEL), BF16),
        jax.ShapeDtypeStruct((t_len, D_MODEL), BF16),
        jax.ShapeDtypeStruct((t_len, D_A), BF16),
        jax.ShapeDtypeStruct((t_len, D_A), BF16),
        jax.ShapeDtypeStruct((t_len, D_A), BF16),
        jax.ShapeDtypeStruct((t_len, D_A), BF16),
        jax.ShapeDtypeStruct((t_len, D_MODEL), BF16),
        jax.ShapeDtypeStruct((t_len, D_MODEL), BF16),
        jax.ShapeDtypeStruct((t_len, D_MODEL), F32),
    ]
    return _staged_call(
        core, name=f"mixer_fwd_l{layer}", grid=(nt + 1,),
        in_specs=[tile(D_MODEL, 0), tile(D_MODEL, 1), _const_spec((1, D_MODEL)), _const_spec((1, 2 * D_MODEL)),
                  _const_spec((1, D_A)), _const_spec((1, D_A)), _const_spec((A_HEADS, 128, 128)),
                  _const_spec((128, D_A)), _const_spec((8, D_B)), ANY, ANY, ANY],
        out_specs=[tile(o.shape[1], 0 if k == len(outs) - 2 else 1) for k, o in enumerate(outs)],
        out_shape=outs,
        scratch_shapes=[pltpu.VMEM((D_MODEL, D_IN), BF16), pltpu.VMEM((2, D_A, D_MODEL), BF16),
                        pltpu.VMEM((D_MODEL, D_MODEL), BF16), pltpu.VMEM((8, D_B), F32),
                        pltpu.VMEM((tm, D_A), BF16), pltpu.VMEM((tm, D_A), F32), pltpu.VMEM((tm, D_IN), F32),
                        pltpu.SemaphoreType.DMA((16,))],
        args=[x, x, g1, bgate, lng, lnb, wm, bsf, wsc, win_g, wb_g, wout_g], stages=stages)


def _ffn_fwd(layer, x2, g2, wfc, bfc, wup_g, wdown_g, stages, head=None):
    t_len = x2.shape[0]
    tm = min(TM_FFN, t_len)
    nt = t_len // tm

    def core(*refs):
        if head is None:
            (x_ref, g2_ref, wfc_ref, bfc_ref, wup_hbm, wdown_hbm, up_ref, silu_ref, dsilu_ref, act_ref, h_ref, x3_ref,
             wup_v, wdown_v, carry, sems) = refs
        else:
            (x_ref, g2_ref, wfc_ref, bfc_ref, t_ref, gf_ref, wup_hbm, wdown_hbm, up_ref, silu_ref, dsilu_ref, act_ref,
             h_ref, dx_ref, dgf_ref, loss_ref, wup_v, wdown_v, carry, sems) = refs
        i = pl.program_id(0)

        @pl.when(i == 0)
        def _():
            cps = _load_col_sharded(wup_hbm, wup_v, sems, 0) + _load_row_sharded(wdown_hbm, wdown_v, sems, 4)
            _start_all(cps)
            carry[...] = jnp.zeros_like(carry)
            if head is not None:
                dgf_ref[...] = jnp.zeros_like(dgf_ref)
                loss_ref[...] = jnp.zeros_like(loss_ref)
            _wait_all(cps)

        xv = x_ref[...]
        r = lax.rsqrt(jnp.mean(xv * xv, axis=-1, keepdims=True) + RMS_EPS)
        h_ref[...] = (xv * r * g2_ref[...]).astype(BF16)
        gate = _dot(h_ref[...], wup_v[:, 0:D_FF])
        up_ref[:, 0:D_FF] = gate.astype(BF16)
        cr = carry[...]
        gc = (wfc_ref[0:1, :] * _shift_down(gate, cr, 2) + wfc_ref[1:2, :] * _shift_down(gate, cr, 1)
              + wfc_ref[2:3, :] * gate + bfc_ref[...])
        carry[...] = gate[tm - 8:tm, :]
        sg = _sigmoid(gc)
        silu = gc * sg
        silu_ref[...] = silu.astype(BF16)
        dsilu_ref[...] = (sg + silu * (1.0 - sg)).astype(BF16)
        val = _dot(h_ref[...], wup_v[:, D_FF:2 * D_FF])
        up_ref[:, D_FF:2 * D_FF] = val.astype(BF16)
        act_ref[...] = (silu * val).astype(BF16)
        x3 = x_ref[...] + _dot(act_ref[...], wdown_v[...])
        if head is None:
            x3_ref[...] = x3
        else:
            r3 = lax.rsqrt(jnp.mean(x3 * x3, axis=-1, keepdims=True) + RMS_EPS)
            xh = x3 * r3
            err = xh * gf_ref[...] - t_ref[...]
            loss_ref[...] += _colsum8(err * err)
            dy = err * (1.0 / D_MODEL)
            dgf_ref[...] += _colsum8(dy * xh)
            dxh = dy * gf_ref[...]
            dx_ref[...] = r3 * (dxh - xh * jnp.mean(dxh * xh, axis=-1, keepdims=True))

    outs = [
        jax.ShapeDtypeStruct((t_len, 2 * D_FF), BF16),
        jax.ShapeDtypeStruct((t_len, D_FF), BF16),
        jax.ShapeDtypeStruct((t_len, D_FF), BF16),
        jax.ShapeDtypeStruct((t_len, D_FF), BF16),
        jax.ShapeDtypeStruct((t_len, D_MODEL), BF16),
        jax.ShapeDtypeStruct((t_len, D_MODEL), F32),
    ]
    in_specs = [_row_spec(tm, D_MODEL), _const_spec((1, D_MODEL)), _const_spec((8, D_FF)), _const_spec((1, D_FF))]
    out_specs = [_row_spec(tm, o.shape[1]) for o in outs]
    args = [x2, g2, wfc, bfc]
    if head is not None:
        in_specs += [_row_spec(tm, D_MODEL), _const_spec((1, D_MODEL))]
        args += list(head)
        outs += [jax.ShapeDtypeStruct((8, D_MODEL), F32)] * 2
        out_specs += [_const_spec((8, D_MODEL))] * 2
    return _staged_call(
        core, name=f"ffn_fwd_l{layer}", grid=(nt,),
        in_specs=in_specs + [ANY, ANY], out_specs=out_specs, out_shape=outs,
        scratch_shapes=[pltpu.VMEM((D_MODEL, 2 * D_FF), BF16), pltpu.VMEM((D_FF, D_MODEL), BF16),
                        pltpu.VMEM((8, D_FF), F32), pltpu.SemaphoreType.DMA((8,))],
        args=args + [wup_g, wdown_g], stages=stages)


def _ffn_bwd(layer, dx3, x2, up, silu, dsilu, g2, wfc, wup_g, wdown_g, stages):
    t_len = x2.shape[0]
    tm = min(TM_FFN, t_len)
    nt = t_len // tm

    def core(dx3_ref, dx3_late_ref, x_ref, up_ref, silu_ref, dsilu_ref, g2_ref, wfc_ref, wup_hbm, wdown_hbm,
             dx2_ref, dup_ref, dx3b_ref, dg2_ref, dbfc_ref, dwfc_ref,
             wup_v, wdown_v, carry, da_s, dup_s, sems):
        i = pl.program_id(0)

        @pl.when(i == 0)
        def _():
            cps = _load_col_sharded(wup_hbm, wup_v, sems, 0) + _load_row_sharded(wdown_hbm, wdown_v, sems, 4)
            _start_all(cps)
            for ref in (carry, da_s, dup_s, dg2_ref, dbfc_ref, dwfc_ref):
                ref[...] = jnp.zeros_like(ref)
            _wait_all(cps)

        live = (i <= nt).astype(F32)
        dx3b_ref[...] = dx3_ref[...].astype(BF16)
        dh = jnp.zeros((tm, D_MODEL), F32)
        for c0, c1 in FF_CHUNKS:
            v0, v1 = D_FF + c0, D_FF + c1
            dh = dh + _dot_nt(dup_s[:, c0:c1], wup_v[:, c0:c1]) + _dot_nt(dup_s[:, v0:v1], wup_v[:, v0:v1])
            da = da_s[:, c0:c1]
            dval = (da * silu_ref[:, c0:c1].astype(F32)).astype(BF16)
            dup_ref[:, v0:v1] = dval
            dup_s[:, v0:v1] = dval
            dgc = da * up_ref[:, v0:v1].astype(F32) * dsilu_ref[:, c0:c1].astype(F32)
            cr = carry[:, c0:c1]
            dgc1 = _shift_up(dgc, cr, 1)
            dgc2 = _shift_up(dgc, cr, 2)
            carry[:, c0:c1] = jnp.where(i < nt, dgc[0:8, :], cr)
            gate = up_ref[:, c0:c1].astype(F32)
            dbfc_ref[:, c0:c1] += live * _colsum8(dgc)
            dwfc_ref[0, :, c0:c1] += live * _colsum8(dgc2 * gate)
            dwfc_ref[1, :, c0:c1] += live * _colsum8(dgc1 * gate)
            dwfc_ref[2, :, c0:c1] += live * _colsum8(dgc * gate)
            dgate = (wfc_ref[2:3, c0:c1] * dgc + wfc_ref[1:2, c0:c1] * dgc1 + wfc_ref[0:1, c0:c1] * dgc2).astype(BF16)
            dup_ref[:, c0:c1] = dgate
            dup_s[:, c0:c1] = dgate
            da_s[:, c0:c1] = _dot_nt(dx3b_ref[...], wdown_v[c0:c1, :])
        xv = x_ref[...]
        r = lax.rsqrt(jnp.mean(xv * xv, axis=-1, keepdims=True) + RMS_EPS)
        xh = xv * r
        dg2_ref[...] += _colsum8(dh * xh)
        dxh = dh * g2_ref[...]
        dx2_ref[...] = dx3_late_ref[...] + r * (dxh - xh * jnp.mean(dxh * xh, axis=-1, keepdims=True))

    def tile(n, lag):
        return pl.BlockSpec((tm, n), lambda i: (nt - 1 - jnp.clip(i - lag, 0, nt - 1), 0))

    outs = [
        jax.ShapeDtypeStruct((t_len, D_MODEL), F32),
        jax.ShapeDtypeStruct((t_len, 2 * D_FF), BF16),
        jax.ShapeDtypeStruct((t_len, D_MODEL), BF16),
        jax.ShapeDtypeStruct((8, D_MODEL), F32),
        jax.ShapeDtypeStruct((8, D_FF), F32),
        jax.ShapeDtypeStruct((3, 8, D_FF), F32),
    ]
    return _staged_call(
        core, name=f"ffn_bwd_l{layer}", grid=(nt + 2,),
        in_specs=[tile(D_MODEL, 0), tile(D_MODEL, 2), tile(D_MODEL, 2), tile(2 * D_FF, 1), tile(D_FF, 1), tile(D_FF, 1),
                  _const_spec((1, D_MODEL)), _const_spec((8, D_FF)), ANY, ANY],
        out_specs=[tile(D_MODEL, 2), tile(2 * D_FF, 1), tile(D_MODEL, 0),
                   _const_spec((8, D_MODEL)), _const_spec((8, D_FF)), _const_spec((3, 8, D_FF))],
        out_shape=outs,
        scratch_shapes=[pltpu.VMEM((D_MODEL, 2 * D_FF), BF16), pltpu.VMEM((D_FF, D_MODEL), BF16),
                        pltpu.VMEM((8, D_FF), F32), pltpu.VMEM((tm, D_FF), F32), pltpu.VMEM((tm, 2 * D_FF), BF16),
                        pltpu.SemaphoreType.DMA((8,))],
        args=[dx3, dx3, x2, up, silu, dsilu, g2, wfc, wup_g, wdown_g], stages=stages)


def _mixer_bwd(layer, dx2, x, zc, qs, sa, ca, sb, cb, ug, fu, xhs, cv, g1, lng, lnb, wmt, wsc, win_g, wb_g, wout_g,
               stages):
    t_len = x.shape[0]
    tm = min(TM_MIX, t_len)
    nt = t_len // tm
    nb = tm // GMLP_BLOCK

    def core(dx2_ref, x_ref, zc_ref, q_ref, sa_ref, ca_ref, sb_ref, cb_ref, ug_ref, fu_ref, xh_ref, cv_ref,
             g1_ref, lng_ref, lnb_ref, wmt_ref, wsc_ref, win_hbm, wb_hbm, wout_hbm,
             dx_ref, dz_ref, da_ref, db_ref, dx2b_ref, dg1_ref, dbgate_ref, dlng_ref, dlnb_ref, dwm_ref, dbsf_ref, dwsc_ref,
             win_v, wb_v, wout_v, carry, vn_s, df_s, dvn_s, sems):
        i = pl.program_id(0)

        @pl.when(i == 0)
        def _():
            cps = (_load_col_sharded(win_hbm, win_v, sems, 0) + _load_branch(wb_hbm, wb_v, sems, 4)
                   + _load_row_sharded(wout_hbm, wout_v, sems, 12))
            _start_all(cps)
            for ref in (carry, dg1_ref, dbgate_ref, dlng_ref, dlnb_ref, dwm_ref, dbsf_ref, dwsc_ref):
                ref[...] = jnp.zeros_like(ref)
            _wait_all(cps)

        def kept(k):
            return zc_ref[:, k * D_B:(k + 1) * D_B].astype(F32)

        def dz_cols(c0, n, val):
            dz_ref[:, c0:c0 + n] = val.astype(BF16)
            return _dot_nt(dz_ref[:, c0:c0 + n], win_v[:, c0:c0 + n])

        dx2b_ref[...] = dx2_ref[...].astype(BF16)
        dm = _dot_nt(dx2b_ref[...], wout_v[...])
        da_ref[...] = (dm * sa_ref[...].astype(F32)).astype(BF16)
        dga = dm * ca_ref[...].astype(F32)
        dh = dz_cols(C_GA, D_MODEL, dga)
        dbgate_ref[:, 0:D_MODEL] += _colsum8(dga)
        dya = _dot_nt(da_ref[...], wb_v[0])
        db_ref[...] = (dm * sb_ref[...].astype(F32)).astype(BF16)
        dgb = dm * cb_ref[...].astype(F32)
        dh = dh + dz_cols(C_GB, D_MODEL, dgb)
        dbgate_ref[:, D_MODEL:2 * D_MODEL] += _colsum8(dgb)
        dyb = _dot_nt(db_ref[...], wb_v[1])

        xh = xh_ref[...].astype(F32)
        vn_s[...] = (xh * lng_ref[...] + lnb_ref[...]).astype(BF16)
        df = dya * ug_ref[...].astype(F32)
        df_s[...] = df.astype(BF16)
        dbsf_acc = df[0:128, :]
        for b in range(1, nb):
            dbsf_acc = dbsf_acc + df[b * 128:(b + 1) * 128, :]
        dbsf_ref[...] += dbsf_acc
        for hd in range(A_HEADS):
            cols = slice(hd * 128, (hd + 1) * 128)
            vcat = jnp.concatenate([vn_s[b * 128:(b + 1) * 128, cols] for b in range(nb)], axis=1)
            dcat = jnp.concatenate([df_s[b * 128:(b + 1) * 128, cols] for b in range(nb)], axis=1)
            gcat = _dot(wmt_ref[hd], dcat)
            dwm_ref[hd] += _dot_nt(dcat, vcat)
            for b in range(nb):
                dvn_s[b * 128:(b + 1) * 128, cols] = gcat[:, b * 128:(b + 1) * 128]
        dh = dh + dz_cols(C_U, D_A, dya * fu_ref[...].astype(F32))
        dvn = dvn_s[...]
        dlng_ref[...] += _colsum8(dvn * xh)
        dlnb_ref[...] += _colsum8(dvn)
        dxh = dvn * lng_ref[...]
        dvc = dxh - jnp.mean(dxh, axis=-1, keepdims=True) - xh * jnp.mean(dxh * xh, axis=-1, keepdims=True)
        dh = dh + dz_cols(C_V, D_A, dvc * cv_ref[...].astype(F32))

        cg = kept(1)
        hbv = kept(2)
        p = cg * hbv
        dh = dh + dz_cols(C_BG, D_B, dyb * q_ref[...].astype(F32))
        dq = dyb * kept(0)
        cr = carry[...]
        dq1 = _shift_up(dq, cr, 1)
        dq2 = _shift_up(dq, cr, 2)
        carry[...] = dq[0:8, :]
        dwsc_ref[0] += _colsum8(dq2 * p)
        dwsc_ref[1] += _colsum8(dq1 * p)
        dwsc_ref[2] += _colsum8(dq * p)
        dp = wsc_ref[2:3, :] * dq + wsc_ref[1:2, :] * dq1 + wsc_ref[0:1, :] * dq2
        dh = dh + dz_cols(C_CG, D_B, dp * hbv)
        dh = dh + dz_cols(C_HB, D_B, dp * cg)

        xv = x_ref[...]
        r = lax.rsqrt(jnp.mean(xv * xv, axis=-1, keepdims=True) + RMS_EPS)
        xn = xv * r
        dg1_ref[...] += _colsum8(dh * xn)
        dxn = dh * g1_ref[...]
        dx_ref[...] = dx2_ref[...] + r * (dxn - xn * jnp.mean(dxn * xn, axis=-1, keepdims=True))

    outs = [
        jax.ShapeDtypeStruct((t_len, D_MODEL), F32),
        jax.ShapeDtypeStruct((t_len, D_IN), BF16),
        jax.ShapeDtypeStruct((t_len, D_MODEL), BF16),
        jax.ShapeDtypeStruct((t_len, D_MODEL), BF16),
        jax.ShapeDtypeStruct((t_len, D_MODEL), BF16),
        jax.ShapeDtypeStruct((8, D_MODEL), F32),
        jax.ShapeDtypeStruct((8, 2 * D_MODEL), F32),
        jax.ShapeDtypeStruct((8, D_A), F32),
        jax.ShapeDtypeStruct((8, D_A), F32),
        jax.ShapeDtypeStruct((A_HEADS, 128, 128), F32),
        jax.ShapeDtypeStruct((128, D_A), F32),
        jax.ShapeDtypeStruct((3, 8, D_B), F32),
    ]

    return _staged_call(
        core, name=f"mixer_bwd_l{layer}", grid=(nt,),
        in_specs=[_row_spec(tm, D_MODEL, nt), _row_spec(tm, D_MODEL, nt), _row_spec(tm, 3 * D_B, nt),
                  _row_spec(tm, D_B, nt), _row_spec(tm, D_MODEL, nt), _row_spec(tm, D_MODEL, nt),
                  _row_spec(tm, D_MODEL, nt), _row_spec(tm, D_MODEL, nt), _row_spec(tm, D_A, nt), _row_spec(tm, D_A, nt),
                  _row_spec(tm, D_A, nt), _row_spec(tm, D_A, nt),
                  _const_spec((1, D_MODEL)), _const_spec((1, D_A)), _const_spec((1, D_A)),
                  _const_spec((A_HEADS, 128, 128)), _const_spec((8, D_B)), ANY, ANY, ANY],
        out_specs=[_row_spec(tm, D_MODEL, nt), _row_spec(tm, D_IN, nt), _row_spec(tm, D_MODEL, nt),
                   _row_spec(tm, D_MODEL, nt), _row_spec(tm, D_MODEL, nt),
                   _const_spec((8, D_MODEL)), _const_spec((8, 2 * D_MODEL)), _const_spec((8, D_A)), _const_spec((8, D_A)),
                   _const_spec((A_HEADS, 128, 128)), _const_spec((128, D_A)), _const_spec((3, 8, D_B))],
        out_shape=outs,
        scratch_shapes=[pltpu.VMEM((D_MODEL, D_IN), BF16), pltpu.VMEM((2, D_A, D_MODEL), BF16),
                        pltpu.VMEM((D_MODEL, D_MODEL), BF16), pltpu.VMEM((8, D_B), F32),
                        pltpu.VMEM((tm, D_A), BF16), pltpu.VMEM((tm, D_A), BF16), pltpu.VMEM((tm, D_A), F32),
                        pltpu.SemaphoreType.DMA((16,))],
        args=[dx2, x, zc, qs, sa, ca, sb, cb, ug, fu, xhs, cv, g1, lng, lnb, wmt, wsc, win_g, wb_g, wout_g],
        stages=stages)


def _wgrad(name, layer, a, b, rows, cols, row_blk, col_blk, stages, a_first=0):
    t_len = a.shape[0]
    n = b.shape[1]
    tk = min(TK_WGRAD, t_len)
    col_sharded = n == N_CHIPS * cols
    m = rows if col_sharded else a.shape[1]
    grid = (m // row_blk, n // col_blk, t_len // tk)
    shards = col_blk // cols if col_sharded else 1

    if col_sharded:
        out_shape = (N_CHIPS, rows, cols)
        out_spec = pl.BlockSpec((shards, row_blk, cols), lambda i, j, k: (j, i, 0))
    else:
        out_shape = (N_CHIPS * rows, cols)
        out_spec = pl.BlockSpec((row_blk, col_blk), lambda i, j, k: (i, j))

    def core(a_ref, b_ref, o_ref):
        @pl.when(pl.program_id(2) == 0)
        def _():
            o_ref[...] = jnp.zeros_like(o_ref)

        g = _dot_tn(a_ref[...], b_ref[...])
        if col_sharded:
            for q in range(shards):
                o_ref[q] += g[:, q * cols:(q + 1) * cols]
        else:
            o_ref[...] += g

    own, outs = _staged_call(
        core, name=f"wgrad_{name}_l{layer}", grid=grid,
        in_specs=[pl.BlockSpec((tk, row_blk), lambda i, j, k: (k, a_first + i)),
                  pl.BlockSpec((tk, col_blk), lambda i, j, k: (k, j))],
        out_specs=[out_spec], out_shape=[jax.ShapeDtypeStruct(out_shape, F32)], scratch_shapes=[],
        args=[a, b], stages=stages)
    return [own[0].reshape(N_CHIPS, rows, cols)], outs


def _wgrad_branch(layer, ya, da, yb, db, stages):
    t_len = ya.shape[0]
    tk = min(TK_WGRAD, t_len)

    cs = D_MODEL // N_CHIPS

    def core(ya_ref, da_ref, yb_ref, db_ref, o_ref):
        @pl.when(pl.program_id(0) == 0)
        def _():
            o_ref[...] = jnp.zeros_like(o_ref)

        ga = _dot_tn(ya_ref[...], da_ref[...])
        gb = _dot_tn(yb_ref[...], db_ref[...])
        for k in range(N_CHIPS):
            o_ref[k, 0:D_A, :] += ga[:, k * cs:(k + 1) * cs]
            o_ref[k, D_A:2 * D_A, :] += gb[:, k * cs:(k + 1) * cs]

    a_spec = pl.BlockSpec((tk, D_A), lambda k: (k, 0))
    d_spec = pl.BlockSpec((tk, D_MODEL), lambda k: (k, 0))
    return _staged_call(
        core, name=f"wgrad_w_branch_l{layer}", grid=(t_len // tk,),
        in_specs=[a_spec, d_spec, a_spec, d_spec],
        out_specs=[pl.BlockSpec((N_CHIPS, 2 * D_A, cs), lambda k: (0, 0, 0))],
        out_shape=[jax.ShapeDtypeStruct((N_CHIPS, 2 * D_A, cs), F32)], scratch_shapes=[],
        args=[ya, da, yb, db], stages=stages)


def _flat_blk(rows, cols):
    blk = rows
    while blk * cols * 4 > 2 * 1024 * 1024 and blk % 16 == 0:
        blk //= 2
    return blk


def _cast_into_slots(name, jobs, chip, stages):
    blks = [_flat_blk(w.shape[1], w.shape[2]) for w, _ in jobs]
    nblks = [w.shape[1] // b for (w, _), b in zip(jobs, blks)]
    n = len(jobs)
    out_shape = [jax.ShapeDtypeStruct((N_CHIPS,) + w.shape[1:], BF16) for w, _ in jobs]

    def core(*refs):
        for w_ref, o_ref in zip(refs[-2 * n:-n], refs[-n:]):
            o_ref[...] = w_ref[...].astype(BF16)

    def slot(*scalars):
        return scalars[0][0] if scalars else 2 * lax.axis_index("x") + lax.axis_index("y")

    in_specs = [pl.BlockSpec((None, b, w.shape[2]), lambda i, *s, la=la, k=k: (la, jnp.minimum(i, k - 1), 0))
                for (w, la), b, k in zip(jobs, blks, nblks)]
    out_specs = [pl.BlockSpec((None, b, w.shape[2]), lambda i, *s, k=k: (slot(*s), jnp.minimum(i, k - 1), 0))
                 for (w, _), b, k in zip(jobs, blks, nblks)]
    args = [w for w, _ in jobs]
    if stages:
        return _staged_call(core, name=f"cast_{name}", grid=(max(nblks),), in_specs=in_specs, out_specs=out_specs,
                            out_shape=out_shape, scratch_shapes=[], args=args, stages=stages)
    own = pl.pallas_call(
        core, name=f"cast_{name}",
        grid_spec=pltpu.PrefetchScalarGridSpec(num_scalar_prefetch=1, grid=(max(nblks),), in_specs=in_specs,
                                               out_specs=out_specs),
        out_shape=out_shape, compiler_params=_params(),
    )(chip, *args)
    return list(own), []


def _reduction_sums(name, jobs, pos):
    in_specs, out_specs, out_shape, args, bodies, counts = [], [], [], [], [], []
    for job in jobs:
        kind, grad, other = job[0], job[1], job[2]
        _, h, cols = other.shape
        blk = _flat_blk(h, cols)
        nblk = h // blk
        if kind == "pair":
            total = N_CHIPS * nblk

            def block(s, total=total, nblk=nblk):
                b = jnp.minimum(s, total - 1)
                return b // nblk, b % nblk

            spec = pl.BlockSpec((None, blk, cols), lambda s, p, block=block: (block(s)[0], block(s)[1], 0))
            in_specs += [pl.BlockSpec((None, blk, cols), lambda s, p, block=block, nblk=nblk:
                                      (block(s)[0], p[1] * nblk + block(s)[1], 0)), spec]
            out_specs.append(spec)
            out_shape.append(jax.ShapeDtypeStruct((N_CHIPS, h, cols), BF16))
            args += [grad, other]
            bodies.append((2, lambda g, o, out: out.__setitem__(..., (g[...] + o[...]).astype(BF16))))
        else:
            total = nblk

            def block(s, total=total):
                return jnp.minimum(s, total - 1)

            in_specs += [pl.BlockSpec((None, blk, cols), lambda s, p, block=block, nblk=nblk:
                                      (p[0], p[1] * nblk + block(s), 0)),
                         pl.BlockSpec((None, blk, cols), lambda s, p, block=block: (p[0], block(s), 0)),
                         pl.BlockSpec((3, blk, cols), lambda s, p, block=block: (0, block(s), 0))]
            out_specs.append(pl.BlockSpec((blk, cols), lambda s, p, block=block, nblk=nblk: (p[1] * nblk + block(s), 0)))
            out_shape.append(jax.ShapeDtypeStruct((2 * h, cols), F32))
            args += [grad, other, job[3]]
            bodies.append((3, lambda g, o, r, out: out.__setitem__(
                ..., (((g[...] + o[...]) + r[0].astype(F32)) + r[1].astype(F32)) + r[2].astype(F32))))
        counts.append(total)

    def body(pos_ref, *refs):
        ins, outs = refs[:len(args)], refs[len(args):]
        k = 0
        for (n_in, fn), out in zip(bodies, outs):
            fn(*ins[k:k + n_in], out)
            k += n_in

    return pl.pallas_call(
        body, name=f"reduction_sums_{name}",
        grid_spec=pltpu.PrefetchScalarGridSpec(num_scalar_prefetch=1, grid=(max(counts),), in_specs=in_specs,
                                               out_specs=out_specs),
        out_shape=out_shape,
        compiler_params=_params(),
    )(pos, *args)


def _sum_slots(name, slots):
    n, rows, _ = slots.shape

    def body(s_ref, o_ref):
        acc = s_ref[0]
        for d in range(1, n):
            acc = acc + s_ref[d]
        o_ref[...] = acc

    return pl.pallas_call(
        body, name=f"sum_slots_{name}", grid=(1,),
        in_specs=[pl.BlockSpec((n, rows, 128), lambda i: (0, 0, 0))],
        out_specs=pl.BlockSpec((rows, 128), lambda i: (0, 0)),
        out_shape=jax.ShapeDtypeStruct((rows, 128), F32),
        compiler_params=_params(),
    )(slots)


def _adamw_math(w, g, m, v):
    m2 = ADAM_B1 * m + (1.0 - ADAM_B1) * g
    v2 = ADAM_B2 * v + (1.0 - ADAM_B2) * (g * g)
    m_hat = m2 / (1.0 - ADAM_B1 ** ADAM_STEP)
    v_hat = v2 / (1.0 - ADAM_B2 ** ADAM_STEP)
    delta = -ADAM_LR * (m_hat / (jnp.sqrt(v_hat) + ADAM_EPS) + ADAM_WD * w)
    return delta, m2, v2


def _adamw_big(name, w, g0, g1, m, v):
    _, rows, cols = w.shape
    blk = _flat_blk(rows, cols) // 2

    def body(w_ref, g0_ref, g1_ref, m_ref, v_ref, g_ref, d_ref, m2_ref, v2_ref):
        g = jnp.where(pl.program_id(0) == 0, g0_ref[...], g1_ref[...])
        d, m2, v2 = _adamw_math(w_ref[...], g, m_ref[...], v_ref[...])
        g_ref[...] = g
        d_ref[...] = d
        m2_ref[...] = m2
        v2_ref[...] = v2

    spec = pl.BlockSpec((None, blk, cols), lambda la, i: (la, i, 0))
    return pl.pallas_call(
        body, name=f"adamw_{name}", grid=(N_LAYERS, rows // blk),
        in_specs=[spec, pl.BlockSpec((blk, cols), lambda la, i: (i * (1 - la), 0)),
                  pl.BlockSpec((blk, cols), lambda la, i: (i * la, 0)), spec, spec],
        out_specs=[spec] * 4,
        out_shape=[jax.ShapeDtypeStruct(w.shape, F32)] * 4,
        compiler_params=_params(("parallel", "parallel")),
    )(w, g0, g1, m, v)


def _adamw_small(ws, gs, ms, vs):
    n = len(ws)

    def body(*refs):
        ins, outs = refs[:4 * n], refs[4 * n:]
        for k in range(n):
            d, m2, v2 = _adamw_math(ins[k][...], ins[n + k][...], ins[2 * n + k][...], ins[3 * n + k][...])
            outs[k][...] = d
            outs[n + k][...] = m2
            outs[2 * n + k][...] = v2

    vmem = pl.BlockSpec(memory_space=pltpu.VMEM)
    return pl.pallas_call(
        body, name="adamw_small",
        in_specs=[vmem] * (4 * n), out_specs=[vmem] * (3 * n),
        out_shape=[jax.ShapeDtypeStruct(w.shape, F32) for w in ws] * 3,
        compiler_params=pltpu.CompilerParams(vmem_limit_bytes=V7X_VMEM_LIMIT),
    )(*ws, *gs, *ms, *vs)


SMALL = ("norm1_g", "b_gate", "gmlp_ln_g", "gmlp_ln_b", "w_spatial", "b_spatial", "w_shortconv", "norm2_g",
         "w_ffn_conv", "b_ffn_conv", "final_g")
ALL_WEIGHTS = ("norm1_g", "w_in", "b_gate", "gmlp_ln_g", "gmlp_ln_b", "w_spatial", "b_spatial", "w_shortconv",
               "w_branch", "w_out", "norm2_g", "w_ffn_up", "w_ffn_conv", "b_ffn_conv", "w_ffn_down", "final_g")


def _pack(arrays):
    flat = jnp.concatenate([a.reshape(-1) for a in arrays])
    n = flat.shape[0]
    rows = -(-n // 1024) * 8
    return jnp.pad(flat, (0, rows * 128 - n)).reshape(rows, 128)


def _unpack(packed, like):
    flat = packed.reshape(-1)
    out, off = [], 0
    for a in like:
        out.append(flat[off:off + a.size].reshape(a.shape))
        off += a.size
    return out


def _pad8(w):
    return jnp.pad(w, ((0, 5), (0, 0)))


def kernel(x, norm1_g, w_in, b_gate, gmlp_ln_g, gmlp_ln_b, w_spatial, b_spatial, w_shortconv, w_branch, w_out, norm2_g, w_ffn_up, w_ffn_conv, b_ffn_conv, w_ffn_down, final_g, loss_target, m_norm1_g, m_w_in, m_b_gate, m_gmlp_ln_g, m_gmlp_ln_b, m_w_spatial, m_b_spatial, m_w_shortconv, m_w_branch, m_w_out, m_norm2_g, m_w_ffn_up, m_w_ffn_conv, m_b_ffn_conv, m_w_ffn_down, m_final_g, v_norm1_g, v_w_in, v_b_gate, v_gmlp_ln_g, v_gmlp_ln_b, v_w_spatial, v_b_spatial, v_w_shortconv, v_w_branch, v_w_out, v_norm2_g, v_w_ffn_up, v_w_ffn_conv, v_b_ffn_conv, v_w_ffn_down, v_final_g):
    weights = dict(norm1_g=norm1_g, w_in=w_in, b_gate=b_gate, gmlp_ln_g=gmlp_ln_g, gmlp_ln_b=gmlp_ln_b,
                   w_spatial=w_spatial, b_spatial=b_spatial, w_shortconv=w_shortconv, w_branch=w_branch, w_out=w_out,
                   norm2_g=norm2_g, w_ffn_up=w_ffn_up, w_ffn_conv=w_ffn_conv, b_ffn_conv=b_ffn_conv,
                   w_ffn_down=w_ffn_down, final_g=final_g)
    mom = dict(norm1_g=m_norm1_g, w_in=m_w_in, b_gate=m_b_gate, gmlp_ln_g=m_gmlp_ln_g, gmlp_ln_b=m_gmlp_ln_b,
               w_spatial=m_w_spatial, b_spatial=m_b_spatial, w_shortconv=m_w_shortconv, w_branch=m_w_branch,
               w_out=m_w_out, norm2_g=m_norm2_g, w_ffn_up=m_w_ffn_up, w_ffn_conv=m_w_ffn_conv,
               b_ffn_conv=m_b_ffn_conv, w_ffn_down=m_w_ffn_down, final_g=m_final_g)
    vel = dict(norm1_g=v_norm1_g, w_in=v_w_in, b_gate=v_b_gate, gmlp_ln_g=v_gmlp_ln_g, gmlp_ln_b=v_gmlp_ln_b,
               w_spatial=v_w_spatial, b_spatial=v_b_spatial, w_shortconv=v_w_shortconv, w_branch=v_w_branch,
               w_out=v_w_out, norm2_g=v_norm2_g, w_ffn_up=v_w_ffn_up, w_ffn_conv=v_w_ffn_conv,
               b_ffn_conv=v_b_ffn_conv, w_ffn_down=v_w_ffn_down, final_g=v_final_g)

    cx, cy, cc = _mesh_pos()
    chip = 2 * cx + cy
    pos_arr = jnp.stack([chip, cc]).astype(jnp.int32)
    t_len = x.shape[1]
    xs = x.reshape(t_len, D_MODEL)
    target = loss_target.reshape(t_len, D_MODEL)
    pipe = _Pipe()

    full = {}

    mixer_w = ("w_in", "w_branch", "w_out")
    ffn_w = ("w_ffn_up", "w_ffn_down")
    slots = {}

    def cast(name, keys, stages):
        own, outs = _cast_into_slots(name, [(weights[n].reshape((N_LAYERS,) + BIG[n]), la) for n, la in keys],
                                     chip.astype(jnp.int32).reshape(1), stages)
        slots.update(zip(keys, own))
        return own, outs

    def gather(names, la):
        def then(*bufs):
            full.update(zip([(n, la) for n in names], bufs))

        pipe.add(_gather_stage([slots[(n, la)] for n in names], then))

    first = [(n, 0) for n in mixer_w]
    cast("first", first, [])
    gather(mixer_w, 0)
    tap_slots = {}
    pipe.add(_chip_spread_stage(_pack([w_shortconv, w_ffn_conv]), lambda got: tap_slots.__setitem__("all", got)))
    pipe.carry(lambda st: cast("rest", [(n, la) for la in range(N_LAYERS) for n in BIG_NAMES if (n, la) not in first], st))
    by_chip = [_unpack(tap_slots["all"][k], [w_shortconv, w_ffn_conv]) for k in range(N_CHIPS)]
    wsc_full = jnp.concatenate([t[0] for t in by_chip], axis=-1)
    wfc_full = jnp.concatenate([t[1] for t in by_chip], axis=-1)

    idx = jnp.arange(GMLP_BLOCK) // CHUNK
    mask = idx[None, :] <= idx[:, None]
    wm_all = jnp.where(mask[None, None], w_spatial, 0.0)
    wm_bf = wm_all.astype(BF16)
    wmt_bf = jnp.swapaxes(wm_all, -1, -2).astype(BF16)
    bsf = jnp.repeat(jnp.swapaxes(b_spatial, -1, -2), 128, axis=-1)

    def row(a):
        return a.reshape(1, -1)

    def mixer_args(la):
        return (row(norm1_g[la]), row(b_gate[la]), row(gmlp_ln_g[la]), row(gmlp_ln_b[la]))

    def mixer_weights(la):
        return tuple(full[(n, la)] for n in mixer_w)

    def ffn_weights(la):
        return tuple(full[(n, la)] for n in ffn_w)

    saved = []
    h_in = xs
    for la in range(N_LAYERS):
        gather(ffn_w, la)
        *kept, mg, h1, x2 = pipe.carry(lambda st: _mixer_fwd(
            la, h_in, *mixer_args(la), wm_bf[la], bsf[la], _pad8(wsc_full[la]), *mixer_weights(la), st))
        ya, yb = kept[1], kept[2]
        if la + 1 < N_LAYERS:
            gather(mixer_w, la + 1)
        head = (target, row(final_g)) if la == N_LAYERS - 1 else None
        up, silu, dsilu, act, h2, *rest = pipe.carry(lambda st: _ffn_fwd(
            la, x2, row(norm2_g[la]), _pad8(wfc_full[la]), row(b_ffn_conv[la]), *ffn_weights(la), st, head=head))
        saved.append(dict(x=h_in, ya=ya, yb=yb, mixer=[kept[0]] + kept[3:], mg=mg, h1=h1, x2=x2, up=up, silu=silu,
                          dsilu=dsilu, act=act, h2=h2))
        h_in = rest[0]
    dx, dgf8, loss8 = rest

    reduced_big = {}

    sums_due = []

    def run_sums():
        if sums_due:
            due = list(sums_due)
            sums_due.clear()
            run_sums.calls += 1
            for (_, then), res in zip(due, _reduction_sums(str(run_sums.calls), [job for job, _ in due], pos_arr)):
                then(res)

    run_sums.calls = 0
    pipe.after = run_sums

    def reduce_big(name, la, grad):
        def after_pair(other):
            def after_chips(got):
                sums_due.append((("chip", grad, other, got), lambda final: pipe.add(_pair_fill_stage(
                    final, lambda done: reduced_big.__setitem__((name, la), done)))))

            sums_due.append((("pair", grad, other), lambda psum: pipe.add(_chip_send_stage(psum, after_chips))))

        pipe.add(_pair_send_stage(grad, after_pair))

    small = {n: [None] * N_LAYERS for n in SMALL}
    spread = {}
    for la in reversed(range(N_LAYERS)):
        s = saved[la]
        dx3 = dx
        run = pipe.carry if la > 0 else (lambda call: call([])[0])
        dx2, dup, dx3b, dg2, dbfc, dwfc = run(lambda st: _ffn_bwd(
            la, dx3, s["x2"], s["up"], s["silu"], s["dsilu"], row(norm2_g[la]), _pad8(wfc_full[la]),
            *ffn_weights(la), st))
        g, = pipe.carry(lambda st: _wgrad("w_ffn_up", la, s["h2"], dup, 1024, 1408, 512, 2816, st))
        reduce_big("w_ffn_up", la, g)
        g, = pipe.carry(lambda st: _wgrad("w_ffn_down", la, s["act"], dx3b, 704, 1024, 1408, 1024, st))
        reduce_big("w_ffn_down", la, g)
        dxl, dz, da, db, dx2b, dg1, dbg, dlng, dlnb, dwm, dbsf, dwsc = _mixer_bwd(
            la, dx2, s["x"], *s["mixer"], row(norm1_g[la]), row(gmlp_ln_g[la]), row(gmlp_ln_b[la]), wmt_bf[la],
            _pad8(wsc_full[la]), *mixer_weights(la), [])[0]
        small["norm1_g"][la] = dg1.sum(0)
        small["b_gate"][la] = dbg.sum(0)
        small["gmlp_ln_g"][la] = dlng.sum(0)
        small["gmlp_ln_b"][la] = dlnb.sum(0)
        small["w_spatial"][la] = jnp.where(mask[None], dwm, 0.0)
        small["b_spatial"][la] = dbsf.reshape(128, A_HEADS, 128).sum(-1).T
        small["w_shortconv"][la] = dwsc.sum(1)
        small["norm2_g"][la] = dg2.sum(0)
        small["w_ffn_conv"][la] = dwfc.sum(1)
        small["b_ffn_conv"][la] = dbfc.sum(0)
        if la == 0:
            small_local = ([jnp.stack(small[n]) for n in SMALL[:-1]]
                           + [dgf8.sum(0), 0.5 * loss8.sum().reshape(1) / D_MODEL])
            mine = _pack(small_local)

            def after_swap(other, mine=mine):
                pair = _sum_slots("small_pair", jnp.stack([mine, other]))
                pipe.add(_chip_spread_stage(pair, lambda slots: spread.__setitem__("slots", slots)))

            pipe.add(_pair_swap_stage(mine, after_swap))
        if la > 0:
            g, = pipe.carry(lambda st: _wgrad("w_in", la, s["h1"], dz, 1024, 1152, 512, 2304, st))
            reduce_big("w_in", la, g)
        else:
            for part, tag in enumerate(("w_in_a", "w_in_b")):
                g, = pipe.carry(lambda st: _wgrad(tag, la, s["h1"], dz, 512, 1152, 512, 2304, st, a_first=part))
                reduce_big(tag, la, g)
        g, = pipe.carry(lambda st: _wgrad("w_out", la, s["mg"], dx2b, 256, 1024, 1024, 1024, st), long=False)
        reduce_big("w_out", la, g)
        g, = pipe.carry(lambda st: _wgrad_branch(la, s["ya"], da, s["yb"], db, st), long=False)
        reduce_big("w_branch", la, g)
        dx = dxl
    grad_x = dx.reshape(x.shape)
    pipe.flush()

    reduced_big[("w_in", 0)] = jnp.concatenate([reduced_big[("w_in_a", 0)], reduced_big[("w_in_b", 0)]], axis=0)
    reduced = _unpack(_sum_slots("small_grads", spread["slots"]), small_local)
    loss = reduced[-1].reshape(())
    grads = dict(zip(SMALL, reduced[:-1]))
    grads["w_shortconv"] = lax.dynamic_slice(grads["w_shortconv"], (0, 0, chip * (D_B // 4)), (N_LAYERS, 3, D_B // 4))
    grads["w_ffn_conv"] = lax.dynamic_slice(grads["w_ffn_conv"], (0, 0, chip * (D_FF // 4)), (N_LAYERS, 3, D_FF // 4))

    delta, new_m, new_v = {}, {}, {}
    for n in BIG_NAMES:
        shape3 = (N_LAYERS,) + BIG[n]
        res = _adamw_big(n, weights[n].reshape(shape3), reduced_big[(n, 0)], reduced_big[(n, 1)],
                         mom[n].reshape(shape3), vel[n].reshape(shape3))
        grads[n], delta[n], new_m[n], new_v[n] = (a.reshape(weights[n].shape) for a in res)
    res = _adamw_small(*[[src[n].reshape(-1, src[n].shape[-1]) for n in SMALL] for src in (weights, grads, mom, vel)])
    for k, n in enumerate(SMALL):
        delta[n], new_m[n], new_v[n] = (res[j * len(SMALL) + k].reshape(weights[n].shape) for j in range(3))

    return (loss, grad_x, *[grads[n] for n in ALL_WEIGHTS], *[delta[n] for n in ALL_WEIGHTS],
            *[new_m[n] for n in ALL_WEIGHTS], *[new_v[n] for n in ALL_WEIGHTS])
```

```python
import jax
import jax.numpy as jnp
from jax import lax
from jax.experimental import pallas as pl
from jax.experimental.pallas import tpu as pltpu

F32 = jnp.float32
BF16 = jnp.bfloat16
MESH = pl.DeviceIdType.MESH
ANY = pl.BlockSpec(memory_space=pl.ANY)

D_MODEL = 1024
D_A = 512
D_B = 512
D_IN = 4608
D_FF = 2816
GMLP_BLOCK = 128
CHUNK = 64
A_HEADS = 4
N_LAYERS = 2
N_CHIPS = 4
RMS_EPS = 1e-6
LN_EPS = 1e-5
ADAM_LR = 0.001
ADAM_B1 = 0.9
ADAM_B2 = 0.999
ADAM_EPS = 1e-08
ADAM_WD = 0.01
ADAM_STEP = 10

C_U, C_V, C_BG, C_CG, C_HB, C_GA, C_GB = 0, 512, 1024, 1536, 2048, 2560, 3584

V7X_VMEM_LIMIT = 60 * 1024 * 1024
TM_MIX = 256
TM_FFN = 256
TK_WGRAD = 2048
SLOW_COPY_BYTES = 768 * 1024
FF_CHUNKS = ((0, 768), (768, 1536), (1536, 2304), (2304, 2816))
GELU_C0 = 0.7978845608028654
GELU_C1 = 0.044715

BIG = {
    "w_in": (1024, 1152),
    "w_branch": (1024, 256),
    "w_out": (256, 1024),
    "w_ffn_up": (1024, 1408),
    "w_ffn_down": (704, 1024),
}
BIG_NAMES = tuple(BIG)


def _params(sem=("arbitrary",), vmem=V7X_VMEM_LIMIT):
    return pltpu.CompilerParams(dimension_semantics=sem, vmem_limit_bytes=vmem)


def _gelu(x):
    x2 = x * x
    t = jnp.tanh(GELU_C0 * x * (1.0 + GELU_C1 * x2))
    return 0.5 * x * (1.0 + t), t


def _gelu_grad(x, t):
    return 0.5 * (1.0 + t) + 0.5 * x * (1.0 - t * t) * GELU_C0 * (1.0 + 3.0 * GELU_C1 * x * x)


def _colsum8(v):
    r, n = v.shape
    return v.reshape(r // 8, 8, n).sum(axis=0)


def _dot(a, b):
    return jnp.dot(a, b, preferred_element_type=F32)


def _dot_nt(a, b):
    return lax.dot_general(a, b, (((1,), (1,)), ((), ())), preferred_element_type=F32)


def _dot_tn(a, b):
    return lax.dot_general(a, b, (((0,), (0,)), ((), ())), preferred_element_type=F32)


def _shift_down(v, carry, n):
    rows = lax.broadcasted_iota(jnp.int32, (8, v.shape[1]), 0)
    out = pltpu.roll(v, n, 0)
    head = out[0:8, :]
    for r in range(n):
        head = jnp.where(rows == r, carry[8 - n + r:8 - n + r + 1, :], head)
    return jnp.concatenate([head, out[8:, :]], axis=0)


def _shift_up(v, carry, n):
    tm = v.shape[0]
    rows = lax.broadcasted_iota(jnp.int32, (8, v.shape[1]), 0)
    out = pltpu.roll(v, tm - n, 0)
    tail = out[tm - 8:tm, :]
    for r in range(n):
        tail = jnp.where(rows == 8 - n + r, carry[r:r + 1, :], tail)
    return jnp.concatenate([out[0:tm - 8, :], tail], axis=0)


def _sigmoid(x):
    return 0.5 * jnp.tanh(0.5 * x) + 0.5


def _start_all(copies):
    for cp in copies:
        cp.start()


def _wait_all(copies):
    for cp in copies:
        cp.wait()


def _load_col_sharded(src, dst, sems, first):
    cs = src.shape[-1]
    return [pltpu.make_async_copy(src.at[k], dst.at[:, k * cs:(k + 1) * cs], sems.at[first + k])
            for k in range(N_CHIPS)]


def _load_row_sharded(src, dst, sems, first):
    rs = src.shape[-2]
    return [pltpu.make_async_copy(src.at[k], dst.at[k * rs:(k + 1) * rs, :], sems.at[first + k])
            for k in range(N_CHIPS)]


def _load_branch(src, dst, sems, first):
    return [pltpu.make_async_copy(src.at[k, pl.ds(m * D_A, D_A), :], dst.at[m, :, k * 256:(k + 1) * 256],
                                  sems.at[first + 2 * k + m])
            for k in range(N_CHIPS) for m in range(2)]


def _row_spec(tm, n, rev=None):
    if rev is None:
        return pl.BlockSpec((tm, n), lambda i: (i, 0))
    return pl.BlockSpec((tm, n), lambda i: (rev - 1 - i, 0))


def _const_spec(shape):
    nd = len(shape)
    return pl.BlockSpec(shape, lambda i: (0,) * nd)


def _mesh_pos():
    return lax.axis_index("x"), lax.axis_index("y"), lax.axis_index("c")


def _other_chips(x, y):
    return [(1 - x, y, 2 * (1 - x) + y), (x, 1 - y, 2 * x + (1 - y)), (1 - x, 1 - y, 2 * (1 - x) + (1 - y))]


def _remote(src, dst, ssem, rsem, to):
    return pltpu.make_async_remote_copy(src_ref=src, dst_ref=dst, send_sem=ssem, recv_sem=rsem, device_id=to,
                                        device_id_type=MESH)


def _half(ref, which, h):
    start = pl.multiple_of(which * h, 8)
    if len(ref.shape) == 2:
        return ref.at[pl.ds(start, h), :]
    return ref.at[:, pl.ds(start, h), :]


class _Stage:
    def __init__(self, ins=(), inouts=(), outs=(), n_sems=0, start=None, mid=None, finish=None, then=None, slow=False):
        self.ins, self.inouts, self.outs = list(ins), list(inouts), list(outs)
        self.n_sems, self.start, self.mid, self.finish, self.then = n_sems, start, mid, finish, then
        self.slow = slow


def _gather_stage(bufs, then):
    n = len(bufs)

    def copies(io, sem):
        x, y, c = _mesh_pos()
        me = 2 * x + y
        ici, fwd, got = [], [], []
        for w in range(n):
            h = io[w].shape[1] // 2
            for j, (px, py, pk) in enumerate(_other_chips(x, y)):
                mine = _half(io[w].at[me], c, h)
                theirs = _half(io[w].at[pk], c, h)
                ici.append(_remote(mine, mine, sem(12 * w + j), sem(12 * w + 3 + j), (px, py, c)))
                got.append(_remote(theirs, theirs, sem(12 * w + j), sem(12 * w + 3 + j), (px, py, c)))
                fwd.append(_remote(theirs, theirs, sem(12 * w + 6 + j), sem(12 * w + 9 + j), (x, y, 1 - c)))
        return ici, got, fwd

    def start(ins, io, outs, sem):
        _start_all(copies(io, sem)[0])

    def mid(ins, io, outs, sem):
        _, got, fwd = copies(io, sem)
        for g, f in zip(got, fwd):
            g.wait_recv()
            f.start()

    def finish(ins, io, outs, sem):
        x, y, c = _mesh_pos()
        ici, _, fwd = copies(io, sem)
        for w in range(n):
            h = io[w].shape[1] // 2
            for j, (px, py, pk) in enumerate(_other_chips(x, y)):
                other = _half(io[w].at[pk], 1 - c, h)
                _remote(other, other, sem(12 * w + 6 + j), sem(12 * w + 9 + j), (x, y, 1 - c)).wait_recv()
        for cp in ici + fwd:
            cp.wait_send()

    return _Stage(inouts=bufs, n_sems=12 * n, start=start, mid=mid, finish=finish, then=then)


def _pair_send_stage(grad, then):
    h = grad.shape[1] // 2

    def copy(ins, outs, sem):
        x, y, c = _mesh_pos()
        return _remote(_half(ins[0], 1 - c, h), outs[0], sem(0), sem(1), (x, y, 1 - c))

    return _Stage(ins=[grad], outs=[jax.ShapeDtypeStruct((N_CHIPS, h, grad.shape[2]), F32)], n_sems=2,
                  start=lambda ins, io, outs, sem: copy(ins, outs, sem).start(),
                  finish=lambda ins, io, outs, sem: copy(ins, outs, sem).wait(), then=then)


def _chip_send_stage(psum, then):
    def copies(ins, outs, sem):
        x, y, c = _mesh_pos()
        return [_remote(ins[0].at[pk], outs[0].at[j], sem(j), sem(3 + j), (px, py, c))
                for j, (px, py, pk) in enumerate(_other_chips(x, y))]

    return _Stage(ins=[psum], outs=[jax.ShapeDtypeStruct((3,) + psum.shape[1:], BF16)], n_sems=6,
                  start=lambda ins, io, outs, sem: _start_all(copies(ins, outs, sem)),
                  finish=lambda ins, io, outs, sem: _wait_all(copies(ins, outs, sem)), then=then,
                  slow=psum.shape[1] * psum.shape[2] * 2 > SLOW_COPY_BYTES)


def _pair_fill_stage(final, then):
    h = final.shape[0] // 2

    def copy(io, sem):
        x, y, c = _mesh_pos()
        mine = _half(io[0], c, h)
        return _remote(mine, mine, sem(0), sem(1), (x, y, 1 - c))

    return _Stage(inouts=[final], n_sems=2,
                  start=lambda ins, io, outs, sem: copy(io, sem).start(),
                  finish=lambda ins, io, outs, sem: copy(io, sem).wait(), then=then)


def _pair_swap_stage(packed, then):
    def copy(ins, outs, sem):
        x, y, c = _mesh_pos()
        return _remote(ins[0], outs[0], sem(0), sem(1), (x, y, 1 - c))

    return _Stage(ins=[packed], outs=[jax.ShapeDtypeStruct(packed.shape, F32)], n_sems=2,
                  start=lambda ins, io, outs, sem: copy(ins, outs, sem).start(),
                  finish=lambda ins, io, outs, sem: copy(ins, outs, sem).wait(), then=then)


def _chip_spread_stage(psum, then):
    def copies(ins, outs, sem):
        x, y, c = _mesh_pos()
        me = 2 * x + y
        cps = [_remote(ins[0], outs[0].at[me], sem(j), sem(3 + j), (px, py, c))
               for j, (px, py, pk) in enumerate(_other_chips(x, y))]
        return cps, pltpu.make_async_copy(ins[0], outs[0].at[me], sem(6))

    def start(ins, io, outs, sem):
        cps, own = copies(ins, outs, sem)
        own.start()
        _start_all(cps)

    def finish(ins, io, outs, sem):
        cps, own = copies(ins, outs, sem)
        _wait_all(cps)
        own.wait()

    return _Stage(ins=[psum], outs=[jax.ShapeDtypeStruct((N_CHIPS,) + psum.shape, F32)], n_sems=7,
                  start=start, finish=finish, then=then)


def _staged_call(core, *, name, grid, in_specs, out_specs, out_shape, scratch_shapes, args, stages):
    n_in, n_out, n_scr = len(args), len(out_shape), len(scratch_shapes)
    s_args, s_outs, aliases, layout = [], [], {}, []
    n_sems = 0
    for st in stages:
        i0, o0 = len(s_args), len(s_outs)
        s_args += st.ins + st.inouts
        for q in range(len(st.inouts)):
            aliases[n_in + i0 + len(st.ins) + q] = n_out + o0 + q
        s_outs += [jax.ShapeDtypeStruct(a.shape, a.dtype) for a in st.inouts] + st.outs
        layout.append((i0, o0, n_sems))
        n_sems += st.n_sems
    steps = 1
    for g in grid:
        steps *= g

    def body(*refs):
        own_in = refs[:n_in]
        s_in = refs[n_in:n_in + len(s_args)]
        rest = refs[n_in + len(s_args):]
        own_out = rest[:n_out]
        s_out = rest[n_out:n_out + len(s_outs)]
        scr = rest[n_out + len(s_outs):]

        def run(which):
            for st, (i0, o0, s0) in zip(stages, layout):
                fn = getattr(st, which)
                if fn is not None:
                    fn(s_in[i0:i0 + len(st.ins)], s_out[o0:o0 + len(st.inouts)],
                       s_out[o0 + len(st.inouts):o0 + len(st.inouts) + len(st.outs)],
                       lambda k, s0=s0: scr[n_scr].at[s0 + k])

        if not stages:
            core(*own_in, *own_out, *scr[:n_scr])
            return
        step = 0
        for d, g in enumerate(grid):
            step = step * g + pl.program_id(d)
        if steps == 1:
            run("start")
            core(*own_in, *own_out, *scr[:n_scr])
            run("mid")
            run("finish")
            return
        pl.when(step == 0)(lambda: run("start"))
        core(*own_in, *own_out, *scr[:n_scr])
        pl.when(step == (3 * steps) // 4)(lambda: run("mid"))
        pl.when(step == steps - 1)(lambda: run("finish"))

    sem = ("arbitrary",) * len(grid) if stages else ("parallel",) * max(len(grid) - 1, 0) + ("arbitrary",) * min(len(grid), 1)
    res = pl.pallas_call(
        body, name=name, grid=grid,
        in_specs=list(in_specs) + [ANY] * len(s_args),
        out_specs=list(out_specs) + [ANY] * len(s_outs),
        out_shape=list(out_shape) + s_outs,
        input_output_aliases=aliases,
        scratch_shapes=list(scratch_shapes) + ([pltpu.SemaphoreType.DMA((n_sems,))] if stages else []),
        compiler_params=_params(sem) if grid else pltpu.CompilerParams(vmem_limit_bytes=V7X_VMEM_LIMIT),
    )(*args, *s_args)
    return list(res[:n_out]), list(res[n_out:])


class _Pipe:
    def __init__(self):
        self.ready = []
        self.flushes = 0
        self.after = None

    def add(self, stage):
        self.ready.append(stage)

    def carry(self, call, long=True):
        stages = [st for st in self.ready if long or not st.slow]
        self.ready = [st for st in self.ready if not (long or not st.slow)]
        own, outs = call(stages)
        k = 0
        for st in stages:
            n = len(st.inouts) + len(st.outs)
            st.then(*outs[k:k + n])
            k += n
        if self.after is not None:
            self.after()
        return own

    def flush(self):
        while self.ready:
            self.flushes += 1
            self.carry(lambda stages: _staged_call(
                lambda *refs: None, name=f"comm_tail_{self.flushes}", grid=(), in_specs=[], out_specs=[], out_shape=[],
                scratch_shapes=[], args=[], stages=stages))


def _mixer_fwd(layer, x, g1, bgate, lng, lnb, wm, bsf, wsc, win_g, wb_g, wout_g, stages):
    t_len = x.shape[0]
    tm = min(TM_MIX, t_len)
    nt = t_len // tm
    nb = tm // GMLP_BLOCK

    def core(x_ref, x_late_ref, g1_ref, bgate_ref, lng_ref, lnb_ref, wm_ref, bsf_ref, wsc_ref, win_hbm, wb_hbm, wout_hbm,
             zc_ref, ya_ref, yb_ref, q_ref, sa_ref, ca_ref, sb_ref, cb_ref, ug_ref, fu_ref, xh_ref, cv_ref,
             mg_ref, h_ref, x2_ref,
             win_v, wb_v, wout_v, carry, vn_s, f_s, z_s, sems):
        i = pl.program_id(0)

        @pl.when(i == 0)
        def _():
            cps = (_load_col_sharded(win_hbm, win_v, sems, 0) + _load_branch(wb_hbm, wb_v, sems, 4)
                   + _load_row_sharded(wout_hbm, wout_v, sems, 12))
            _start_all(cps)
            carry[...] = jnp.zeros_like(carry)
            z_s[...] = jnp.zeros_like(z_s)
            _wait_all(cps)

        xv = x_ref[...]
        r = lax.rsqrt(jnp.mean(xv * xv, axis=-1, keepdims=True) + RMS_EPS)
        h_ref[...] = (xv * r * g1_ref[...]).astype(BF16)

        def zcols(c0, n, keep=None):
            zv = z_s[:, c0:c0 + n]
            z_s[:, c0:c0 + n] = _dot(h_ref[...], win_v[:, c0:c0 + n])
            if keep is not None:
                zc_ref[:, keep * D_B:(keep + 1) * D_B] = zv.astype(BF16)
            return zv

        v = zcols(C_V, D_A)
        vg, tv = _gelu(v)
        mu = jnp.mean(vg, axis=-1, keepdims=True)
        vc = vg - mu
        rstd = lax.rsqrt(jnp.mean(vc * vc, axis=-1, keepdims=True) + LN_EPS)
        xh = vc * rstd
        xh_ref[...] = xh.astype(BF16)
        cv_ref[...] = (rstd * _gelu_grad(v, tv)).astype(BF16)
        vn_s[...] = (xh * lng_ref[...] + lnb_ref[...]).astype(BF16)
        for hd in range(A_HEADS):
            cols = slice(hd * 128, (hd + 1) * 128)
            vcat = jnp.concatenate([vn_s[b * 128:(b + 1) * 128, cols] for b in range(nb)], axis=1)
            fcat = _dot(wm_ref[hd], vcat)
            for b in range(nb):
                f_s[b * 128:(b + 1) * 128, cols] = fcat[:, b * 128:(b + 1) * 128]
        u = zcols(C_U, D_A)
        ug, tu = _gelu(u)
        ug_ref[...] = ug.astype(BF16)
        fb = f_s[...] + jnp.concatenate([bsf_ref[...]] * nb, axis=0)
        fu_ref[...] = (fb * _gelu_grad(u, tu)).astype(BF16)
        ya_ref[...] = (ug * fb).astype(BF16)

        p = zcols(C_CG, D_B, keep=1) * zcols(C_HB, D_B, keep=2)
        cr = carry[...]
        q = wsc_ref[0:1, :] * _shift_down(p, cr, 2) + wsc_ref[1:2, :] * _shift_down(p, cr, 1) + wsc_ref[2:3, :] * p
        carry[...] = p[tm - 8:tm, :]
        q_ref[...] = q.astype(BF16)
        yb_ref[...] = (zcols(C_BG, D_B, keep=0) * q).astype(BF16)

        av = _dot(ya_ref[...], wb_v[0])
        sa = _sigmoid(zcols(C_GA, D_MODEL) + bgate_ref[:, 0:D_MODEL])
        sa_ref[...] = sa.astype(BF16)
        mg = sa * av
        ca_ref[...] = (mg * (1.0 - sa)).astype(BF16)
        bv = _dot(yb_ref[...], wb_v[1])
        sb = _sigmoid(zcols(C_GB, D_MODEL) + bgate_ref[:, D_MODEL:2 * D_MODEL])
        sb_ref[...] = sb.astype(BF16)
        mb = sb * bv
        cb_ref[...] = (mb * (1.0 - sb)).astype(BF16)
        mg_ref[...] = (mg + mb).astype(BF16)
        x2_ref[...] = x_late_ref[...] + _dot(mg_ref[...], wout_v[...])

    def tile(n, lag):
        return pl.BlockSpec((tm, n), lambda i: (jnp.clip(i - lag, 0, nt - 1), 0))

    outs = [
        jax.ShapeDtypeStruct((t_len, 3 * D_B), BF16),
        jax.ShapeDtypeStruct((t_len, D_A), BF16),
        jax.ShapeDtypeStruct((t_len, D_B), BF16),
        jax.ShapeDtypeStruct((t_len, D_B), BF16),
        jax.ShapeDtypeStruct((t_len, D_MODEL), BF16),
        jax.ShapeDtypeStruct((t_len, D_MODEL), BF16),
        jax.ShapeDtypeStruct((t_len, D_MODEL), BF16),
        jax.ShapeDtypeStruct((t_len, D_MODEL), BF16),
        jax.ShapeDtypeStruct((t_len, D_A), BF16),
        jax.ShapeDtypeStruct((t_len, D_A), BF16),
        jax.ShapeDtypeStruct((t_len, D_A), BF16),
        jax.ShapeDtypeStruct((t_len, D_A), BF16),
        jax.ShapeDtypeStruct((t_len, D_MODEL), BF16),
        jax.ShapeDtypeStruct((t_len, D_MODEL), BF16),
        jax.ShapeDtypeStruct((t_len, D_MODEL), F32),
    ]
    return _staged_call(
        core, name=f"mixer_fwd_l{layer}", grid=(nt + 1,),
        in_specs=[tile(D_MODEL, 0), tile(D_MODEL, 1), _const_spec((1, D_MODEL)), _const_spec((1, 2 * D_MODEL)),
                  _const_spec((1, D_A)), _const_spec((1, D_A)), _const_spec((A_HEADS, 128, 128)),
                  _const_spec((128, D_A)), _const_spec((8, D_B)), ANY, ANY, ANY],
        out_specs=[tile(o.shape[1], 0 if k == len(outs) - 2 else 1) for k, o in enumerate(outs)],
        out_shape=outs,
        scratch_shapes=[pltpu.VMEM((D_MODEL, D_IN), BF16), pltpu.VMEM((2, D_A, D_MODEL), BF16),
                        pltpu.VMEM((D_MODEL, D_MODEL), BF16), pltpu.VMEM((8, D_B), F32),
                        pltpu.VMEM((tm, D_A), BF16), pltpu.VMEM((tm, D_A), F32), pltpu.VMEM((tm, D_IN), F32),
                        pltpu.SemaphoreType.DMA((16,))],
        args=[x, x, g1, bgate, lng, lnb, wm, bsf, wsc, win_g, wb_g, wout_g], stages=stages)


def _ffn_fwd(layer, x2, g2, wfc, bfc, wup_g, wdown_g, stages, head=None):
    t_len = x2.shape[0]
    tm = min(TM_FFN, t_len)
    nt = t_len // tm

    def core(*refs):
        if head is None:
            (x_ref, g2_ref, wfc_ref, bfc_ref, wup_hbm, wdown_hbm, up_ref, silu_ref, dsilu_ref, act_ref, h_ref, x3_ref,
             wup_v, wdown_v, carry, sems) = refs
        else:
            (x_ref, g2_ref, wfc_ref, bfc_ref, t_ref, gf_ref, wup_hbm, wdown_hbm, up_ref, silu_ref, dsilu_ref, act_ref,
             h_ref, dx_ref, dgf_ref, loss_ref, wup_v, wdown_v, carry, sems) = refs
        i = pl.program_id(0)

        @pl.when(i == 0)
        def _():
            cps = _load_col_sharded(wup_hbm, wup_v, sems, 0) + _load_row_sharded(wdown_hbm, wdown_v, sems, 4)
            _start_all(cps)
            carry[...] = jnp.zeros_like(carry)
            if head is not None:
                dgf_ref[...] = jnp.zeros_like(dgf_ref)
                loss_ref[...] = jnp.zeros_like(loss_ref)
            _wait_all(cps)

        xv = x_ref[...]
        r = lax.rsqrt(jnp.mean(xv * xv, axis=-1, keepdims=True) + RMS_EPS)
        h_ref[...] = (xv * r * g2_ref[...]).astype(BF16)
        gate = _dot(h_ref[...], wup_v[:, 0:D_FF])
        up_ref[:, 0:D_FF] = gate.astype(BF16)
        cr = carry[...]
        gc = (wfc_ref[0:1, :] * _shift_down(gate, cr, 2) + wfc_ref[1:2, :] * _shift_down(gate, cr, 1)
              + wfc_ref[2:3, :] * gate + bfc_ref[...])
        carry[...] = gate[tm - 8:tm, :]
        sg = _sigmoid(gc)
        silu = gc * sg
        silu_ref[...] = silu.astype(BF16)
        dsilu_ref[...] = (sg + silu * (1.0 - sg)).astype(BF16)
        val = _dot(h_ref[...], wup_v[:, D_FF:2 * D_FF])
        up_ref[:, D_FF:2 * D_FF] = val.astype(BF16)
        act_ref[...] = (silu * val).astype(BF16)
        x3 = x_ref[...] + _dot(act_ref[...], wdown_v[...])
        if head is None:
            x3_ref[...] = x3
        else:
            r3 = lax.rsqrt(jnp.mean(x3 * x3, axis=-1, keepdims=True) + RMS_EPS)
            xh = x3 * r3
            err = xh * gf_ref[...] - t_ref[...]
            loss_ref[...] += _colsum8(err * err)
            dy = err * (1.0 / D_MODEL)
            dgf_ref[...] += _colsum8(dy * xh)
            dxh = dy * gf_ref[...]
            dx_ref[...] = r3 * (dxh - xh * jnp.mean(dxh * xh, axis=-1, keepdims=True))

    outs = [
        jax.ShapeDtypeStruct((t_len, 2 * D_FF), BF16),
        jax.ShapeDtypeStruct((t_len, D_FF), BF16),
        jax.ShapeDtypeStruct((t_len, D_FF), BF16),
        jax.ShapeDtypeStruct((t_len, D_FF), BF16),
        jax.ShapeDtypeStruct((t_len, D_MODEL), BF16),
        jax.ShapeDtypeStruct((t_len, D_MODEL), F32),
    ]
    in_specs = [_row_spec(tm, D_MODEL), _const_spec((1, D_MODEL)), _const_spec((8, D_FF)), _const_spec((1, D_FF))]
    out_specs = [_row_spec(tm, o.shape[1]) for o in outs]
    args = [x2, g2, wfc, bfc]
    if head is not None:
        in_specs += [_row_spec(tm, D_MODEL), _const_spec((1, D_MODEL))]
        args += list(head)
        outs += [jax.ShapeDtypeStruct((8, D_MODEL), F32)] * 2
        out_specs += [_const_spec((8, D_MODEL))] * 2
    return _staged_call(
        core, name=f"ffn_fwd_l{layer}", grid=(nt,),
        in_specs=in_specs + [ANY, ANY], out_specs=out_specs, out_shape=outs,
        scratch_shapes=[pltpu.VMEM((D_MODEL, 2 * D_FF), BF16), pltpu.VMEM((D_FF, D_MODEL), BF16),
                        pltpu.VMEM((8, D_FF), F32), pltpu.SemaphoreType.DMA((8,))],
        args=args + [wup_g, wdown_g], stages=stages)


def _ffn_bwd(layer, dx3, x2, up, silu, dsilu, g2, wfc, wup_g, wdown_g, stages):
    t_len = x2.shape[0]
    tm = min(TM_FFN, t_len)
    nt = t_len // tm

    def core(dx3_ref, dx3_late_ref, x_ref, up_ref, silu_ref, dsilu_ref, g2_ref, wfc_ref, wup_hbm, wdown_hbm,
             dx2_ref, dup_ref, dx3b_ref, dg2_ref, dbfc_ref, dwfc_ref,
             wup_v, wdown_v, carry, da_s, dup_s, sems):
        i = pl.program_id(0)

        @pl.when(i == 0)
        def _():
            cps = _load_col_sharded(wup_hbm, wup_v, sems, 0) + _load_row_sharded(wdown_hbm, wdown_v, sems, 4)
            _start_all(cps)
            for ref in (carry, da_s, dup_s, dg2_ref, dbfc_ref, dwfc_ref):
                ref[...] = jnp.zeros_like(ref)
            _wait_all(cps)

        live = (i <= nt).astype(F32)
        dx3b_ref[...] = dx3_ref[...].astype(BF16)
        dh = jnp.zeros((tm, D_MODEL), F32)
        for c0, c1 in FF_CHUNKS:
            v0, v1 = D_FF + c0, D_FF + c1
            dh = dh + _dot_nt(dup_s[:, c0:c1], wup_v[:, c0:c1]) + _dot_nt(dup_s[:, v0:v1], wup_v[:, v0:v1])
            da = da_s[:, c0:c1]
            dval = (da * silu_ref[:, c0:c1].astype(F32)).astype(BF16)
            dup_ref[:, v0:v1] = dval
            dup_s[:, v0:v1] = dval
            dgc = da * up_ref[:, v0:v1].astype(F32) * dsilu_ref[:, c0:c1].astype(F32)
            cr = carry[:, c0:c1]
            dgc1 = _shift_up(dgc, cr, 1)
            dgc2 = _shift_up(dgc, cr, 2)
            carry[:, c0:c1] = jnp.where(i < nt, dgc[0:8, :], cr)
            gate = up_ref[:, c0:c1].astype(F32)
            dbfc_ref[:, c0:c1] += live * _colsum8(dgc)
            dwfc_ref[0, :, c0:c1] += live * _colsum8(dgc2 * gate)
            dwfc_ref[1, :, c0:c1] += live * _colsum8(dgc1 * gate)
            dwfc_ref[2, :, c0:c1] += live * _colsum8(dgc * gate)
            dgate = (wfc_ref[2:3, c0:c1] * dgc + wfc_ref[1:2, c0:c1] * dgc1 + wfc_ref[0:1, c0:c1] * dgc2).astype(BF16)
            dup_ref[:, c0:c1] = dgate
            dup_s[:, c0:c1] = dgate
            da_s[:, c0:c1] = _dot_nt(dx3b_ref[...], wdown_v[c0:c1, :])
        xv = x_ref[...]
        r = lax.rsqrt(jnp.mean(xv * xv, axis=-1, keepdims=True) + RMS_EPS)
        xh = xv * r
        dg2_ref[...] += _colsum8(dh * xh)
        dxh = dh * g2_ref[...]
        dx2_ref[...] = dx3_late_ref[...] + r * (dxh - xh * jnp.mean(dxh * xh, axis=-1, keepdims=True))

    def tile(n, lag):
        return pl.BlockSpec((tm, n), lambda i: (nt - 1 - jnp.clip(i - lag, 0, nt - 1), 0))

    outs = [
        jax.ShapeDtypeStruct((t_len, D_MODEL), F32),
        jax.ShapeDtypeStruct((t_len, 2 * D_FF), BF16),
        jax.ShapeDtypeStruct((t_len, D_MODEL), BF16),
        jax.ShapeDtypeStruct((8, D_MODEL), F32),
        jax.ShapeDtypeStruct((8, D_FF), F32),
        jax.ShapeDtypeStruct((3, 8, D_FF), F32),
    ]
    return _staged_call(
        core, name=f"ffn_bwd_l{layer}", grid=(nt + 2,),
        in_specs=[tile(D_MODEL, 0), tile(D_MODEL, 2), tile(D_MODEL, 2), tile(2 * D_FF, 1), tile(D_FF, 1), tile(D_FF, 1),
                  _const_spec((1, D_MODEL)), _const_spec((8, D_FF)), ANY, ANY],
        out_specs=[tile(D_MODEL, 2), tile(2 * D_FF, 1), tile(D_MODEL, 0),
                   _const_spec((8, D_MODEL)), _const_spec((8, D_FF)), _const_spec((3, 8, D_FF))],
        out_shape=outs,
        scratch_shapes=[pltpu.VMEM((D_MODEL, 2 * D_FF), BF16), pltpu.VMEM((D_FF, D_MODEL), BF16),
                        pltpu.VMEM((8, D_FF), F32), pltpu.VMEM((tm, D_FF), F32), pltpu.VMEM((tm, 2 * D_FF), BF16),
                        pltpu.SemaphoreType.DMA((8,))],
        args=[dx3, dx3, x2, up, silu, dsilu, g2, wfc, wup_g, wdown_g], stages=stages)


def _mixer_bwd(layer, dx2, x, zc, qs, sa, ca, sb, cb, ug, fu, xhs, cv, g1, lng, lnb, wmt, wsc, win_g, wb_g, wout_g,
               stages):
    t_len = x.shape[0]
    tm = min(TM_MIX, t_len)
    nt = t_len // tm
    nb = tm // GMLP_BLOCK

    def core(dx2_ref, x_ref, zc_ref, q_ref, sa_ref, ca_ref, sb_ref, cb_ref, ug_ref, fu_ref, xh_ref, cv_ref,
             g1_ref, lng_ref, lnb_ref, wmt_ref, wsc_ref, win_hbm, wb_hbm, wout_hbm,
             dx_ref, dz_ref, da_ref, db_ref, dx2b_ref, dg1_ref, dbgate_ref, dlng_ref, dlnb_ref, dwm_ref, dbsf_ref, dwsc_ref,
             win_v, wb_v, wout_v, carry, vn_s, df_s, dvn_s, sems):
        i = pl.program_id(0)

        @pl.when(i == 0)
        def _():
            cps = (_load_col_sharded(win_hbm, win_v, sems, 0) + _load_branch(wb_hbm, wb_v, sems, 4)
                   + _load_row_sharded(wout_hbm, wout_v, sems, 12))
            _start_all(cps)
            for ref in (carry, dg1_ref, dbgate_ref, dlng_ref, dlnb_ref, dwm_ref, dbsf_ref, dwsc_ref):
                ref[...] = jnp.zeros_like(ref)
            _wait_all(cps)

        def kept(k):
            return zc_ref[:, k * D_B:(k + 1) * D_B].astype(F32)

        def dz_cols(c0, n, val):
            dz_ref[:, c0:c0 + n] = val.astype(BF16)
            return _dot_nt(dz_ref[:, c0:c0 + n], win_v[:, c0:c0 + n])

        dx2b_ref[...] = dx2_ref[...].astype(BF16)
        dm = _dot_nt(dx2b_ref[...], wout_v[...])
        da_ref[...] = (dm * sa_ref[...].astype(F32)).astype(BF16)
        dga = dm * ca_ref[...].astype(F32)
        dh = dz_cols(C_GA, D_MODEL, dga)
        dbgate_ref[:, 0:D_MODEL] += _colsum8(dga)
        dya = _dot_nt(da_ref[...], wb_v[0])
        db_ref[...] = (dm * sb_ref[...].astype(F32)).astype(BF16)
        dgb = dm * cb_ref[...].astype(F32)
        dh = dh + dz_cols(C_GB, D_MODEL, dgb)
        dbgate_ref[:, D_MODEL:2 * D_MODEL] += _colsum8(dgb)
        dyb = _dot_nt(db_ref[...], wb_v[1])

        xh = xh_ref[...].astype(F32)
        vn_s[...] = (xh * lng_ref[...] + lnb_ref[...]).astype(BF16)
        df = dya * ug_ref[...].astype(F32)
        df_s[...] = df.astype(BF16)
        dbsf_acc = df[0:128, :]
        for b in range(1, nb):
            dbsf_acc = dbsf_acc + df[b * 128:(b + 1) * 128, :]
        dbsf_ref[...] += dbsf_acc
        for hd in range(A_HEADS):
            cols = slice(hd * 128, (hd + 1) * 128)
            vcat = jnp.concatenate([vn_s[b * 128:(b + 1) * 128, cols] for b in range(nb)], axis=1)
            dcat = jnp.concatenate([df_s[b * 128:(b + 1) * 128, cols] for b in range(nb)], axis=1)
            gcat = _dot(wmt_ref[hd], dcat)
            dwm_ref[hd] += _dot_nt(dcat, vcat)
            for b in range(nb):
                dvn_s[b * 128:(b + 1) * 128, cols] = gcat[:, b * 128:(b + 1) * 128]
        dh = dh + dz_cols(C_U, D_A, dya * fu_ref[...].astype(F32))
        dvn = dvn_s[...]
        dlng_ref[...] += _colsum8(dvn * xh)
        dlnb_ref[...] += _colsum8(dvn)
        dxh = dvn * lng_ref[...]
        dvc = dxh - jnp.mean(dxh, axis=-1, keepdims=True) - xh * jnp.mean(dxh * xh, axis=-1, keepdims=True)
        dh = dh + dz_cols(C_V, D_A, dvc * cv_ref[...].astype(F32))

        cg = kept(1)
        hbv = kept(2)
        p = cg * hbv
        dh = dh + dz_cols(C_BG, D_B, dyb * q_ref[...].astype(F32))
        dq = dyb * kept(0)
        cr = carry[...]
        dq1 = _shift_up(dq, cr, 1)
        dq2 = _shift_up(dq, cr, 2)
        carry[...] = dq[0:8, :]
        dwsc_ref[0] += _colsum8(dq2 * p)
        dwsc_ref[1] += _colsum8(dq1 * p)
        dwsc_ref[2] += _colsum8(dq * p)
        dp = wsc_ref[2:3, :] * dq + wsc_ref[1:2, :] * dq1 + wsc_ref[0:1, :] * dq2
        dh = dh + dz_cols(C_CG, D_B, dp * hbv)
        dh = dh + dz_cols(C_HB, D_B, dp * cg)

        xv = x_ref[...]
        r = lax.rsqrt(jnp.mean(xv * xv, axis=-1, keepdims=True) + RMS_EPS)
        xn = xv * r
        dg1_ref[...] += _colsum8(dh * xn)
        dxn = dh * g1_ref[...]
        dx_ref[...] = dx2_ref[...] + r * (dxn - xn * jnp.mean(dxn * xn, axis=-1, keepdims=True))

    outs = [
        jax.ShapeDtypeStruct((t_len, D_MODEL), F32),
        jax.ShapeDtypeStruct((t_len, D_IN), BF16),
        jax.ShapeDtypeStruct((t_len, D_MODEL), BF16),
        jax.ShapeDtypeStruct((t_len, D_MODEL), BF16),
        jax.ShapeDtypeStruct((t_len, D_MODEL), BF16),
        jax.ShapeDtypeStruct((8, D_MODEL), F32),
        jax.ShapeDtypeStruct((8, 2 * D_MODEL), F32),
        jax.ShapeDtypeStruct((8, D_A), F32),
        jax.ShapeDtypeStruct((8, D_A), F32),
        jax.ShapeDtypeStruct((A_HEADS, 128, 128), F32),
        jax.ShapeDtypeStruct((128, D_A), F32),
        jax.ShapeDtypeStruct((3, 8, D_B), F32),
    ]

    return _staged_call(
        core, name=f"mixer_bwd_l{layer}", grid=(nt,),
        in_specs=[_row_spec(tm, D_MODEL, nt), _row_spec(tm, D_MODEL, nt), _row_spec(tm, 3 * D_B, nt),
                  _row_spec(tm, D_B, nt), _row_spec(tm, D_MODEL, nt), _row_spec(tm, D_MODEL, nt),
                  _row_spec(tm, D_MODEL, nt), _row_spec(tm, D_MODEL, nt), _row_spec(tm, D_A, nt), _row_spec(tm, D_A, nt),
                  _row_spec(tm, D_A, nt), _row_spec(tm, D_A, nt),
                  _const_spec((1, D_MODEL)), _const_spec((1, D_A)), _const_spec((1, D_A)),
                  _const_spec((A_HEADS, 128, 128)), _const_spec((8, D_B)), ANY, ANY, ANY],
        out_specs=[_row_spec(tm, D_MODEL, nt), _row_spec(tm, D_IN, nt), _row_spec(tm, D_MODEL, nt),
                   _row_spec(tm, D_MODEL, nt), _row_spec(tm, D_MODEL, nt),
                   _const_spec((8, D_MODEL)), _const_spec((8, 2 * D_MODEL)), _const_spec((8, D_A)), _const_spec((8, D_A)),
                   _const_spec((A_HEADS, 128, 128)), _const_spec((128, D_A)), _const_spec((3, 8, D_B))],
        out_shape=outs,
        scratch_shapes=[pltpu.VMEM((D_MODEL, D_IN), BF16), pltpu.VMEM((2, D_A, D_MODEL), BF16),
                        pltpu.VMEM((D_MODEL, D_MODEL), BF16), pltpu.VMEM((8, D_B), F32),
                        pltpu.VMEM((tm, D_A), BF16), pltpu.VMEM((tm, D_A), BF16), pltpu.VMEM((tm, D_A), F32),
                        pltpu.SemaphoreType.DMA((16,))],
        args=[dx2, x, zc, qs, sa, ca, sb, cb, ug, fu, xhs, cv, g1, lng, lnb, wmt, wsc, win_g, wb_g, wout_g],
        stages=stages)


def _wgrad(name, layer, a, b, rows, cols, row_blk, col_blk, stages):
    t_len, m = a.shape
    n = b.shape[1]
    tk = min(TK_WGRAD, t_len)
    col_sharded = n == N_CHIPS * cols
    grid = (m // row_blk, n // col_blk, t_len // tk)
    shards = col_blk // cols if col_sharded else 1

    if col_sharded:
        out_shape = (N_CHIPS, rows, cols)
        out_spec = pl.BlockSpec((shards, row_blk, cols), lambda i, j, k: (j, i, 0))
    else:
        out_shape = (N_CHIPS * rows, cols)
        out_spec = pl.BlockSpec((row_blk, col_blk), lambda i, j, k: (i, j))

    def core(a_ref, b_ref, o_ref):
        @pl.when(pl.program_id(2) == 0)
        def _():
            o_ref[...] = jnp.zeros_like(o_ref)

        g = _dot_tn(a_ref[...], b_ref[...])
        if col_sharded:
            for q in range(shards):
                o_ref[q] += g[:, q * cols:(q + 1) * cols]
        else:
            o_ref[...] += g

    own, outs = _staged_call(
        core, name=f"wgrad_{name}_l{layer}", grid=grid,
        in_specs=[pl.BlockSpec((tk, row_blk), lambda i, j, k: (k, i)), pl.BlockSpec((tk, col_blk), lambda i, j, k: (k, j))],
        out_specs=[out_spec], out_shape=[jax.ShapeDtypeStruct(out_shape, F32)], scratch_shapes=[],
        args=[a, b], stages=stages)
    return [own[0].reshape(N_CHIPS, rows, cols)], outs


def _wgrad_branch(layer, ya, da, yb, db, stages):
    t_len = ya.shape[0]
    tk = min(TK_WGRAD, t_len)

    cs = D_MODEL // N_CHIPS

    def core(ya_ref, da_ref, yb_ref, db_ref, o_ref):
        @pl.when(pl.program_id(0) == 0)
        def _():
            o_ref[...] = jnp.zeros_like(o_ref)

        ga = _dot_tn(ya_ref[...], da_ref[...])
        gb = _dot_tn(yb_ref[...], db_ref[...])
        for k in range(N_CHIPS):
            o_ref[k, 0:D_A, :] += ga[:, k * cs:(k + 1) * cs]
            o_ref[k, D_A:2 * D_A, :] += gb[:, k * cs:(k + 1) * cs]

    a_spec = pl.BlockSpec((tk, D_A), lambda k: (k, 0))
    d_spec = pl.BlockSpec((tk, D_MODEL), lambda k: (k, 0))
    return _staged_call(
        core, name=f"wgrad_w_branch_l{layer}", grid=(t_len // tk,),
        in_specs=[a_spec, d_spec, a_spec, d_spec],
        out_specs=[pl.BlockSpec((N_CHIPS, 2 * D_A, cs), lambda k: (0, 0, 0))],
        out_shape=[jax.ShapeDtypeStruct((N_CHIPS, 2 * D_A, cs), F32)], scratch_shapes=[],
        args=[ya, da, yb, db], stages=stages)


def _flat_blk(rows, cols):
    blk = rows
    while blk * cols * 4 > 2 * 1024 * 1024 and blk % 16 == 0:
        blk //= 2
    return blk


def _cast_into_slots(name, jobs, chip, stages):
    blks = [_flat_blk(w.shape[1], w.shape[2]) for w, _ in jobs]
    nblks = [w.shape[1] // b for (w, _), b in zip(jobs, blks)]
    n = len(jobs)
    out_shape = [jax.ShapeDtypeStruct((N_CHIPS,) + w.shape[1:], BF16) for w, _ in jobs]

    def core(*refs):
        for w_ref, o_ref in zip(refs[-2 * n:-n], refs[-n:]):
            o_ref[...] = w_ref[...].astype(BF16)

    def slot(*scalars):
        return scalars[0][0] if scalars else 2 * lax.axis_index("x") + lax.axis_index("y")

    in_specs = [pl.BlockSpec((None, b, w.shape[2]), lambda i, *s, la=la, k=k: (la, jnp.minimum(i, k - 1), 0))
                for (w, la), b, k in zip(jobs, blks, nblks)]
    out_specs = [pl.BlockSpec((None, b, w.shape[2]), lambda i, *s, k=k: (slot(*s), jnp.minimum(i, k - 1), 0))
                 for (w, _), b, k in zip(jobs, blks, nblks)]
    args = [w for w, _ in jobs]
    if stages:
        return _staged_call(core, name=f"cast_{name}", grid=(max(nblks),), in_specs=in_specs, out_specs=out_specs,
                            out_shape=out_shape, scratch_shapes=[], args=args, stages=stages)
    own = pl.pallas_call(
        core, name=f"cast_{name}",
        grid_spec=pltpu.PrefetchScalarGridSpec(num_scalar_prefetch=1, grid=(max(nblks),), in_specs=in_specs,
                                               out_specs=out_specs),
        out_shape=out_shape, compiler_params=_params(),
    )(chip, *args)
    return list(own), []


def _reduction_sums(name, jobs, pos):
    in_specs, out_specs, out_shape, args, bodies, counts = [], [], [], [], [], []
    for job in jobs:
        kind, grad, other = job[0], job[1], job[2]
        _, h, cols = other.shape
        blk = _flat_blk(h, cols)
        nblk = h // blk
        if kind == "pair":
            total = N_CHIPS * nblk

            def block(s, total=total, nblk=nblk):
                b = jnp.minimum(s, total - 1)
                return b // nblk, b % nblk

            spec = pl.BlockSpec((None, blk, cols), lambda s, p, block=block: (block(s)[0], block(s)[1], 0))
            in_specs += [pl.BlockSpec((None, blk, cols), lambda s, p, block=block, nblk=nblk:
                                      (block(s)[0], p[1] * nblk + block(s)[1], 0)), spec]
            out_specs.append(spec)
            out_shape.append(jax.ShapeDtypeStruct((N_CHIPS, h, cols), BF16))
            args += [grad, other]
            bodies.append((2, lambda g, o, out: out.__setitem__(..., (g[...] + o[...]).astype(BF16))))
        else:
            total = nblk

            def block(s, total=total):
                return jnp.minimum(s, total - 1)

            in_specs += [pl.BlockSpec((None, blk, cols), lambda s, p, block=block, nblk=nblk:
                                      (p[0], p[1] * nblk + block(s), 0)),
                         pl.BlockSpec((None, blk, cols), lambda s, p, block=block: (p[0], block(s), 0)),
                         pl.BlockSpec((3, blk, cols), lambda s, p, block=block: (0, block(s), 0))]
            out_specs.append(pl.BlockSpec((blk, cols), lambda s, p, block=block, nblk=nblk: (p[1] * nblk + block(s), 0)))
            out_shape.append(jax.ShapeDtypeStruct((2 * h, cols), F32))
            args += [grad, other, job[3]]
            bodies.append((3, lambda g, o, r, out: out.__setitem__(
                ..., (((g[...] + o[...]) + r[0].astype(F32)) + r[1].astype(F32)) + r[2].astype(F32))))
        counts.append(total)

    def body(pos_ref, *refs):
        ins, outs = refs[:len(args)], refs[len(args):]
        k = 0
        for (n_in, fn), out in zip(bodies, outs):
            fn(*ins[k:k + n_in], out)
            k += n_in

    return pl.pallas_call(
        body, name=f"reduction_sums_{name}",
        grid_spec=pltpu.PrefetchScalarGridSpec(num_scalar_prefetch=1, grid=(max(counts),), in_specs=in_specs,
                                               out_specs=out_specs),
        out_shape=out_shape,
        compiler_params=_params(),
    )(pos, *args)


def _sum_slots(name, slots):
    n, rows, _ = slots.shape

    def body(s_ref, o_ref):
        acc = s_ref[0]
        for d in range(1, n):
            acc = acc + s_ref[d]
        o_ref[...] = acc

    return pl.pallas_call(
        body, name=f"sum_slots_{name}", grid=(1,),
        in_specs=[pl.BlockSpec((n, rows, 128), lambda i: (0, 0, 0))],
        out_specs=pl.BlockSpec((rows, 128), lambda i: (0, 0)),
        out_shape=jax.ShapeDtypeStruct((rows, 128), F32),
        compiler_params=_params(),
    )(slots)


def _adamw_math(w, g, m, v):
    m2 = ADAM_B1 * m + (1.0 - ADAM_B1) * g
    v2 = ADAM_B2 * v + (1.0 - ADAM_B2) * (g * g)
    m_hat = m2 / (1.0 - ADAM_B1 ** ADAM_STEP)
    v_hat = v2 / (1.0 - ADAM_B2 ** ADAM_STEP)
    delta = -ADAM_LR * (m_hat / (jnp.sqrt(v_hat) + ADAM_EPS) + ADAM_WD * w)
    return delta, m2, v2


def _adamw_big(name, w, g0, g1, m, v):
    _, rows, cols = w.shape
    blk = _flat_blk(rows, cols) // 2

    def body(w_ref, g0_ref, g1_ref, m_ref, v_ref, g_ref, d_ref, m2_ref, v2_ref):
        g = jnp.where(pl.program_id(0) == 0, g0_ref[...], g1_ref[...])
        d, m2, v2 = _adamw_math(w_ref[...], g, m_ref[...], v_ref[...])
        g_ref[...] = g
        d_ref[...] = d
        m2_ref[...] = m2
        v2_ref[...] = v2

    spec = pl.BlockSpec((None, blk, cols), lambda la, i: (la, i, 0))
    return pl.pallas_call(
        body, name=f"adamw_{name}", grid=(N_LAYERS, rows // blk),
        in_specs=[spec, pl.BlockSpec((blk, cols), lambda la, i: (i * (1 - la), 0)),
                  pl.BlockSpec((blk, cols), lambda la, i: (i * la, 0)), spec, spec],
        out_specs=[spec] * 4,
        out_shape=[jax.ShapeDtypeStruct(w.shape, F32)] * 4,
        compiler_params=_params(("parallel", "parallel")),
    )(w, g0, g1, m, v)


def _adamw_small(ws, gs, ms, vs):
    n = len(ws)

    def body(*refs):
        ins, outs = refs[:4 * n], refs[4 * n:]
        for k in range(n):
            d, m2, v2 = _adamw_math(ins[k][...], ins[n + k][...], ins[2 * n + k][...], ins[3 * n + k][...])
            outs[k][...] = d
            outs[n + k][...] = m2
            outs[2 * n + k][...] = v2

    vmem = pl.BlockSpec(memory_space=pltpu.VMEM)
    return pl.pallas_call(
        body, name="adamw_small",
        in_specs=[vmem] * (4 * n), out_specs=[vmem] * (3 * n),
        out_shape=[jax.ShapeDtypeStruct(w.shape, F32) for w in ws] * 3,
        compiler_params=pltpu.CompilerParams(vmem_limit_bytes=V7X_VMEM_LIMIT),
    )(*ws, *gs, *ms, *vs)


SMALL = ("norm1_g", "b_gate", "gmlp_ln_g", "gmlp_ln_b", "w_spatial", "b_spatial", "w_shortconv", "norm2_g",
         "w_ffn_conv", "b_ffn_conv", "final_g")
ALL_WEIGHTS = ("norm1_g", "w_in", "b_gate", "gmlp_ln_g", "gmlp_ln_b", "w_spatial", "b_spatial", "w_shortconv",
               "w_branch", "w_out", "norm2_g", "w_ffn_up", "w_ffn_conv", "b_ffn_conv", "w_ffn_down", "final_g")


def _pack(arrays):
    flat = jnp.concatenate([a.reshape(-1) for a in arrays])
    n = flat.shape[0]
    rows = -(-n // 1024) * 8
    return jnp.pad(flat, (0, rows * 128 - n)).reshape(rows, 128)


def _unpack(packed, like):
    flat = packed.reshape(-1)
    out, off = [], 0
    for a in like:
        out.append(flat[off:off + a.size].reshape(a.shape))
        off += a.size
    return out


def _pad8(w):
    return jnp.pad(w, ((0, 5), (0, 0)))


def kernel(x, norm1_g, w_in, b_gate, gmlp_ln_g, gmlp_ln_b, w_spatial, b_spatial, w_shortconv, w_branch, w_out, norm2_g, w_ffn_up, w_ffn_conv, b_ffn_conv, w_ffn_down, final_g, loss_target, m_norm1_g, m_w_in, m_b_gate, m_gmlp_ln_g, m_gmlp_ln_b, m_w_spatial, m_b_spatial, m_w_shortconv, m_w_branch, m_w_out, m_norm2_g, m_w_ffn_up, m_w_ffn_conv, m_b_ffn_conv, m_w_ffn_down, m_final_g, v_norm1_g, v_w_in, v_b_gate, v_gmlp_ln_g, v_gmlp_ln_b, v_w_spatial, v_b_spatial, v_w_shortconv, v_w_branch, v_w_out, v_norm2_g, v_w_ffn_up, v_w_ffn_conv, v_b_ffn_conv, v_w_ffn_down, v_final_g):
    weights = dict(norm1_g=norm1_g, w_in=w_in, b_gate=b_gate, gmlp_ln_g=gmlp_ln_g, gmlp_ln_b=gmlp_ln_b,
                   w_spatial=w_spatial, b_spatial=b_spatial, w_shortconv=w_shortconv, w_branch=w_branch, w_out=w_out,
                   norm2_g=norm2_g, w_ffn_up=w_ffn_up, w_ffn_conv=w_ffn_conv, b_ffn_conv=b_ffn_conv,
                   w_ffn_down=w_ffn_down, final_g=final_g)
    mom = dict(norm1_g=m_norm1_g, w_in=m_w_in, b_gate=m_b_gate, gmlp_ln_g=m_gmlp_ln_g, gmlp_ln_b=m_gmlp_ln_b,
               w_spatial=m_w_spatial, b_spatial=m_b_spatial, w_shortconv=m_w_shortconv, w_branch=m_w_branch,
               w_out=m_w_out, norm2_g=m_norm2_g, w_ffn_up=m_w_ffn_up, w_ffn_conv=m_w_ffn_conv,
               b_ffn_conv=m_b_ffn_conv, w_ffn_down=m_w_ffn_down, final_g=m_final_g)
    vel = dict(norm1_g=v_norm1_g, w_in=v_w_in, b_gate=v_b_gate, gmlp_ln_g=v_gmlp_ln_g, gmlp_ln_b=v_gmlp_ln_b,
               w_spatial=v_w_spatial, b_spatial=v_b_spatial, w_shortconv=v_w_shortconv, w_branch=v_w_branch,
               w_out=v_w_out, norm2_g=v_norm2_g, w_ffn_up=v_w_ffn_up, w_ffn_conv=v_w_ffn_conv,
               b_ffn_conv=v_b_ffn_conv, w_ffn_down=v_w_ffn_down, final_g=v_final_g)

    cx, cy, cc = _mesh_pos()
    chip = 2 * cx + cy
    pos_arr = jnp.stack([chip, cc]).astype(jnp.int32)
    t_len = x.shape[1]
    xs = x.reshape(t_len, D_MODEL)
    target = loss_target.reshape(t_len, D_MODEL)
    pipe = _Pipe()

    full = {}

    mixer_w = ("w_in", "w_branch", "w_out")
    ffn_w = ("w_ffn_up", "w_ffn_down")
    slots = {}

    def cast(name, keys, stages):
        own, outs = _cast_into_slots(name, [(weights[n].reshape((N_LAYERS,) + BIG[n]), la) for n, la in keys],
                                     chip.astype(jnp.int32).reshape(1), stages)
        slots.update(zip(keys, own))
        return own, outs

    def gather(names, la):
        def then(*bufs):
            full.update(zip([(n, la) for n in names], bufs))

        pipe.add(_gather_stage([slots[(n, la)] for n in names], then))

    first = [(n, 0) for n in mixer_w]
    cast("first", first, [])
    gather(mixer_w, 0)
    tap_slots = {}
    pipe.add(_chip_spread_stage(_pack([w_shortconv, w_ffn_conv]), lambda got: tap_slots.__setitem__("all", got)))
    pipe.carry(lambda st: cast("rest", [(n, la) for la in range(N_LAYERS) for n in BIG_NAMES if (n, la) not in first], st))
    by_chip = [_unpack(tap_slots["all"][k], [w_shortconv, w_ffn_conv]) for k in range(N_CHIPS)]
    wsc_full = jnp.concatenate([t[0] for t in by_chip], axis=-1)
    wfc_full = jnp.concatenate([t[1] for t in by_chip], axis=-1)

    idx = jnp.arange(GMLP_BLOCK) // CHUNK
    mask = idx[None, :] <= idx[:, None]
    wm_all = jnp.where(mask[None, None], w_spatial, 0.0)
    wm_bf = wm_all.astype(BF16)
    wmt_bf = jnp.swapaxes(wm_all, -1, -2).astype(BF16)
    bsf = jnp.repeat(jnp.swapaxes(b_spatial, -1, -2), 128, axis=-1)

    def row(a):
        return a.reshape(1, -1)

    def mixer_args(la):
        return (row(norm1_g[la]), row(b_gate[la]), row(gmlp_ln_g[la]), row(gmlp_ln_b[la]))

    def mixer_weights(la):
        return tuple(full[(n, la)] for n in mixer_w)

    def ffn_weights(la):
        return tuple(full[(n, la)] for n in ffn_w)

    saved = []
    h_in = xs
    for la in range(N_LAYERS):
        gather(ffn_w, la)
        *kept, mg, h1, x2 = pipe.carry(lambda st: _mixer_fwd(
            la, h_in, *mixer_args(la), wm_bf[la], bsf[la], _pad8(wsc_full[la]), *mixer_weights(la), st))
        ya, yb = kept[1], kept[2]
        if la + 1 < N_LAYERS:
            gather(mixer_w, la + 1)
        head = (target, row(final_g)) if la == N_LAYERS - 1 else None
        up, silu, dsilu, act, h2, *rest = pipe.carry(lambda st: _ffn_fwd(
            la, x2, row(norm2_g[la]), _pad8(wfc_full[la]), row(b_ffn_conv[la]), *ffn_weights(la), st, head=head))
        saved.append(dict(x=h_in, ya=ya, yb=yb, mixer=[kept[0]] + kept[3:], mg=mg, h1=h1, x2=x2, up=up, silu=silu,
                          dsilu=dsilu, act=act, h2=h2))
        h_in = rest[0]
    dx, dgf8, loss8 = rest

    reduced_big = {}

    sums_due = []

    def run_sums():
        if sums_due:
            due = list(sums_due)
            sums_due.clear()
            run_sums.calls += 1
            for (_, then), res in zip(due, _reduction_sums(str(run_sums.calls), [job for job, _ in due], pos_arr)):
                then(res)

    run_sums.calls = 0
    pipe.after = run_sums

    def reduce_big(name, la, grad):
        def after_pair(other):
            def after_chips(got):
                sums_due.append((("chip", grad, other, got), lambda final: pipe.add(_pair_fill_stage(
                    final, lambda done: reduced_big.__setitem__((name, la), done)))))

            sums_due.append((("pair", grad, other), lambda psum: pipe.add(_chip_send_stage(psum, after_chips))))

        pipe.add(_pair_send_stage(grad, after_pair))

    small = {n: [None] * N_LAYERS for n in SMALL}
    spread = {}
    wgrad_in = {}
    for la in reversed(range(N_LAYERS)):
        s = saved[la]
        dx3 = dx
        dx2, dup, dx3b, dg2, dbfc, dwfc = _ffn_bwd(
            la, dx3, s["x2"], s["up"], s["silu"], s["dsilu"], row(norm2_g[la]), _pad8(wfc_full[la]),
            *ffn_weights(la), [])[0]
        dxl, dz, da, db, dx2b, dg1, dbg, dlng, dlnb, dwm, dbsf, dwsc = _mixer_bwd(
            la, dx2, s["x"], *s["mixer"], row(norm1_g[la]), row(gmlp_ln_g[la]), row(gmlp_ln_b[la]), wmt_bf[la],
            _pad8(wsc_full[la]), *mixer_weights(la), [])[0]
        wgrad_in[la] = dict(s, dup=dup, dx3b=dx3b, dz=dz, da=da, db=db, dx2b=dx2b)
        small["norm1_g"][la] = dg1.sum(0)
        small["b_gate"][la] = dbg.sum(0)
        small["gmlp_ln_g"][la] = dlng.sum(0)
        small["gmlp_ln_b"][la] = dlnb.sum(0)
        small["w_spatial"][la] = jnp.where(mask[None], dwm, 0.0)
        small["b_spatial"][la] = dbsf.reshape(128, A_HEADS, 128).sum(-1).T
        small["w_shortconv"][la] = dwsc.sum(1)
        small["norm2_g"][la] = dg2.sum(0)
        small["w_ffn_conv"][la] = dwfc.sum(1)
        small["b_ffn_conv"][la] = dbfc.sum(0)
        dx = dxl
    grad_x = dx.reshape(x.shape)

    small_local = [jnp.stack(small[n]) for n in SMALL[:-1]] + [dgf8.sum(0), 0.5 * loss8.sum().reshape(1) / D_MODEL]
    mine = _pack(small_local)

    def after_swap(other):
        pair = _sum_slots("small_pair", jnp.stack([mine, other]))
        pipe.add(_chip_spread_stage(pair, lambda slots: spread.__setitem__("slots", slots)))

    pipe.add(_pair_swap_stage(mine, after_swap))

    def wgrad(name, la, st):
        w = wgrad_in[la]
        if name == "w_ffn_up":
            return _wgrad(name, la, w["h2"], w["dup"], 1024, 1408, 512, 2816, st)
        if name == "w_in":
            return _wgrad(name, la, w["h1"], w["dz"], 1024, 1152, 512, 2304, st)
        if name == "w_ffn_down":
            return _wgrad(name, la, w["act"], w["dx3b"], 704, 1024, 1408, 1024, st)
        if name == "w_out":
            return _wgrad(name, la, w["mg"], w["dx2b"], 256, 1024, 1024, 1024, st)
        return _wgrad_branch(la, w["ya"], w["da"], w["yb"], w["db"], st)

    for name in ("w_ffn_up", "w_in", "w_ffn_down", "w_out", "w_branch"):
        for la in reversed(range(N_LAYERS)):
            g, = pipe.carry(lambda st: wgrad(name, la, st), long=name not in ("w_out", "w_branch"))
            reduce_big(name, la, g)
    pipe.flush()

    reduced = _unpack(_sum_slots("small_grads", spread["slots"]), small_local)
    loss = reduced[-1].reshape(())
    grads = dict(zip(SMALL, reduced[:-1]))
    grads["w_shortconv"] = lax.dynamic_slice(grads["w_shortconv"], (0, 0, chip * (D_B // 4)), (N_LAYERS, 3, D_B // 4))
    grads["w_ffn_conv"] = lax.dynamic_slice(grads["w_ffn_conv"], (0, 0, chip * (D_FF // 4)), (N_LAYERS, 3, D_FF // 4))

    delta, new_m, new_v = {}, {}, {}
    for n in BIG_NAMES:
        shape3 = (N_LAYERS,) + BIG[n]
        res = _adamw_big(n, weights[n].reshape(shape3), reduced_big[(n, 0)], reduced_big[(n, 1)],
                         mom[n].reshape(shape3), vel[n].reshape(shape3))
        grads[n], delta[n], new_m[n], new_v[n] = (a.reshape(weights[n].shape) for a in res)
    res = _adamw_small(*[[src[n].reshape(-1, src[n].shape[-1]) for n in SMALL] for src in (weights, grads, mom, vel)])
    for k, n in enumerate(SMALL):
        delta[n], new_m[n], new_v[n] = (res[j * len(SMALL) + k].reshape(weights[n].shape) for j in range(3))

    return (loss, grad_x, *[grads[n] for n in ALL_WEIGHTS], *[delta[n] for n in ALL_WEIGHTS],
            *[new_m[n] for n in ALL_WEIGHTS], *[new_v[n] for n in ALL_WEIGHTS])
```

```python
import jax
import jax.numpy as jnp
from jax import lax
from jax.experimental import pallas as pl
from jax.experimental.pallas import tpu as pltpu

F32 = jnp.float32
BF16 = jnp.bfloat16
MESH = pl.DeviceIdType.MESH
ANY = pl.BlockSpec(memory_space=pl.ANY)

D_MODEL = 1024
D_A = 512
D_B = 512
D_IN = 4608
D_FF = 2816
GMLP_BLOCK = 128
CHUNK = 64
A_HEADS = 4
N_LAYERS = 2
N_CHIPS = 4
RMS_EPS = 1e-6
LN_EPS = 1e-5
ADAM_LR = 0.001
ADAM_B1 = 0.9
ADAM_B2 = 0.999
ADAM_EPS = 1e-08
ADAM_WD = 0.01
ADAM_STEP = 10

C_U, C_V, C_BG, C_CG, C_HB, C_GA, C_GB = 0, 512, 1024, 1536, 2048, 2560, 3584

V7X_VMEM_LIMIT = 60 * 1024 * 1024
TM_MIX = 256
TM_FFN = 256
TK_WGRAD = 2048
SLOW_COPY_BYTES = 768 * 1024
FF_CHUNKS = ((0, 768), (768, 1536), (1536, 2304), (2304, 2816))
GELU_C0 = 0.7978845608028654
GELU_C1 = 0.044715

BIG = {
    "w_in": (1024, 1152),
    "w_branch": (1024, 256),
    "w_out": (256, 1024),
    "w_ffn_up": (1024, 1408),
    "w_ffn_down": (704, 1024),
}
BIG_NAMES = tuple(BIG)


def _params(sem=("arbitrary",), vmem=V7X_VMEM_LIMIT):
    return pltpu.CompilerParams(dimension_semantics=sem, vmem_limit_bytes=vmem)


def _gelu(x):
    x2 = x * x
    t = jnp.tanh(GELU_C0 * x * (1.0 + GELU_C1 * x2))
    return 0.5 * x * (1.0 + t), t


def _gelu_grad(x, t):
    return 0.5 * (1.0 + t) + 0.5 * x * (1.0 - t * t) * GELU_C0 * (1.0 + 3.0 * GELU_C1 * x * x)


def _colsum8(v):
    r, n = v.shape
    return v.reshape(r // 8, 8, n).sum(axis=0)


def _dot(a, b):
    return jnp.dot(a, b, preferred_element_type=F32)


def _dot_nt(a, b):
    return lax.dot_general(a, b, (((1,), (1,)), ((), ())), preferred_element_type=F32)


def _dot_tn(a, b):
    return lax.dot_general(a, b, (((0,), (0,)), ((), ())), preferred_element_type=F32)


def _shift_down(v, carry, n):
    rows = lax.broadcasted_iota(jnp.int32, (8, v.shape[1]), 0)
    out = pltpu.roll(v, n, 0)
    head = out[0:8, :]
    for r in range(n):
        head = jnp.where(rows == r, carry[8 - n + r:8 - n + r + 1, :], head)
    return jnp.concatenate([head, out[8:, :]], axis=0)


def _shift_up(v, carry, n):
    tm = v.shape[0]
    rows = lax.broadcasted_iota(jnp.int32, (8, v.shape[1]), 0)
    out = pltpu.roll(v, tm - n, 0)
    tail = out[tm - 8:tm, :]
    for r in range(n):
        tail = jnp.where(rows == 8 - n + r, carry[r:r + 1, :], tail)
    return jnp.concatenate([out[0:tm - 8, :], tail], axis=0)


def _sigmoid(x):
    return 0.5 * jnp.tanh(0.5 * x) + 0.5


def _start_all(copies):
    for cp in copies:
        cp.start()


def _wait_all(copies):
    for cp in copies:
        cp.wait()


def _load_col_sharded(src, dst, sems, first):
    cs = src.shape[-1]
    return [pltpu.make_async_copy(src.at[k], dst.at[:, k * cs:(k + 1) * cs], sems.at[first + k])
            for k in range(N_CHIPS)]


def _load_row_sharded(src, dst, sems, first):
    rs = src.shape[-2]
    return [pltpu.make_async_copy(src.at[k], dst.at[k * rs:(k + 1) * rs, :], sems.at[first + k])
            for k in range(N_CHIPS)]


def _load_branch(src, dst, sems, first):
    return [pltpu.make_async_copy(src.at[k, pl.ds(m * D_A, D_A), :], dst.at[m, :, k * 256:(k + 1) * 256],
                                  sems.at[first + 2 * k + m])
            for k in range(N_CHIPS) for m in range(2)]


def _row_spec(tm, n, rev=None):
    if rev is None:
        return pl.BlockSpec((tm, n), lambda i: (i, 0))
    return pl.BlockSpec((tm, n), lambda i: (rev - 1 - i, 0))


def _const_spec(shape):
    nd = len(shape)
    return pl.BlockSpec(shape, lambda i: (0,) * nd)


def _mesh_pos():
    return lax.axis_index("x"), lax.axis_index("y"), lax.axis_index("c")


def _other_chips(x, y):
    return [(1 - x, y, 2 * (1 - x) + y), (x, 1 - y, 2 * x + (1 - y)), (1 - x, 1 - y, 2 * (1 - x) + (1 - y))]


def _remote(src, dst, ssem, rsem, to):
    return pltpu.make_async_remote_copy(src_ref=src, dst_ref=dst, send_sem=ssem, recv_sem=rsem, device_id=to,
                                        device_id_type=MESH)


def _half(ref, which, h):
    start = pl.multiple_of(which * h, 8)
    if len(ref.shape) == 2:
        return ref.at[pl.ds(start, h), :]
    return ref.at[:, pl.ds(start, h), :]


class _Stage:
    def __init__(self, ins=(), inouts=(), outs=(), n_sems=0, start=None, mid=None, finish=None, then=None, slow=False):
        self.ins, self.inouts, self.outs = list(ins), list(inouts), list(outs)
        self.n_sems, self.start, self.mid, self.finish, self.then = n_sems, start, mid, finish, then
        self.slow = slow


def _gather_stage(bufs, then):
    n = len(bufs)

    def copies(io, sem):
        x, y, c = _mesh_pos()
        me = 2 * x + y
        ici, fwd, got = [], [], []
        for w in range(n):
            h = io[w].shape[1] // 2
            for j, (px, py, pk) in enumerate(_other_chips(x, y)):
                mine = _half(io[w].at[me], c, h)
                theirs = _half(io[w].at[pk], c, h)
                ici.append(_remote(mine, mine, sem(12 * w + j), sem(12 * w + 3 + j), (px, py, c)))
                got.append(_remote(theirs, theirs, sem(12 * w + j), sem(12 * w + 3 + j), (px, py, c)))
                fwd.append(_remote(theirs, theirs, sem(12 * w + 6 + j), sem(12 * w + 9 + j), (x, y, 1 - c)))
        return ici, got, fwd

    def start(ins, io, outs, sem):
        _start_all(copies(io, sem)[0])

    def mid(ins, io, outs, sem):
        _, got, fwd = copies(io, sem)
        for g, f in zip(got, fwd):
            g.wait_recv()
            f.start()

    def finish(ins, io, outs, sem):
        x, y, c = _mesh_pos()
        ici, _, fwd = copies(io, sem)
        for w in range(n):
            h = io[w].shape[1] // 2
            for j, (px, py, pk) in enumerate(_other_chips(x, y)):
                other = _half(io[w].at[pk], 1 - c, h)
                _remote(other, other, sem(12 * w + 6 + j), sem(12 * w + 9 + j), (x, y, 1 - c)).wait_recv()
        for cp in ici + fwd:
            cp.wait_send()

    return _Stage(inouts=bufs, n_sems=12 * n, start=start, mid=mid, finish=finish, then=then)


def _pair_send_stage(grad, then):
    h = grad.shape[1] // 2

    def copy(ins, outs, sem):
        x, y, c = _mesh_pos()
        return _remote(_half(ins[0], 1 - c, h), outs[0], sem(0), sem(1), (x, y, 1 - c))

    return _Stage(ins=[grad], outs=[jax.ShapeDtypeStruct((N_CHIPS, h, grad.shape[2]), F32)], n_sems=2,
                  start=lambda ins, io, outs, sem: copy(ins, outs, sem).start(),
                  finish=lambda ins, io, outs, sem: copy(ins, outs, sem).wait(), then=then)


def _chip_send_stage(psum, then):
    def copies(ins, outs, sem):
        x, y, c = _mesh_pos()
        return [_remote(ins[0].at[pk], outs[0].at[j], sem(j), sem(3 + j), (px, py, c))
                for j, (px, py, pk) in enumerate(_other_chips(x, y))]

    return _Stage(ins=[psum], outs=[jax.ShapeDtypeStruct((3,) + psum.shape[1:], BF16)], n_sems=6,
                  start=lambda ins, io, outs, sem: _start_all(copies(ins, outs, sem)),
                  finish=lambda ins, io, outs, sem: _wait_all(copies(ins, outs, sem)), then=then,
                  slow=psum.shape[1] * psum.shape[2] * 2 > SLOW_COPY_BYTES)


def _pair_fill_stage(final, then):
    h = final.shape[0] // 2

    def copy(io, sem):
        x, y, c = _mesh_pos()
        mine = _half(io[0], c, h)
        return _remote(mine, mine, sem(0), sem(1), (x, y, 1 - c))

    return _Stage(inouts=[final], n_sems=2,
                  start=lambda ins, io, outs, sem: copy(io, sem).start(),
                  finish=lambda ins, io, outs, sem: copy(io, sem).wait(), then=then)


def _pair_swap_stage(packed, then):
    def copy(ins, outs, sem):
        x, y, c = _mesh_pos()
        return _remote(ins[0], outs[0], sem(0), sem(1), (x, y, 1 - c))

    return _Stage(ins=[packed], outs=[jax.ShapeDtypeStruct(packed.shape, F32)], n_sems=2,
                  start=lambda ins, io, outs, sem: copy(ins, outs, sem).start(),
                  finish=lambda ins, io, outs, sem: copy(ins, outs, sem).wait(), then=then)


def _chip_spread_stage(psum, then):
    def copies(ins, outs, sem):
        x, y, c = _mesh_pos()
        me = 2 * x + y
        cps = [_remote(ins[0], outs[0].at[me], sem(j), sem(3 + j), (px, py, c))
               for j, (px, py, pk) in enumerate(_other_chips(x, y))]
        return cps, pltpu.make_async_copy(ins[0], outs[0].at[me], sem(6))

    def start(ins, io, outs, sem):
        cps, own = copies(ins, outs, sem)
        own.start()
        _start_all(cps)

    def finish(ins, io, outs, sem):
        cps, own = copies(ins, outs, sem)
        _wait_all(cps)
        own.wait()

    return _Stage(ins=[psum], outs=[jax.ShapeDtypeStruct((N_CHIPS,) + psum.shape, F32)], n_sems=7,
                  start=start, finish=finish, then=then)


def _staged_call(core, *, name, grid, in_specs, out_specs, out_shape, scratch_shapes, args, stages):
    n_in, n_out, n_scr = len(args), len(out_shape), len(scratch_shapes)
    s_args, s_outs, aliases, layout = [], [], {}, []
    n_sems = 0
    for st in stages:
        i0, o0 = len(s_args), len(s_outs)
        s_args += st.ins + st.inouts
        for q in range(len(st.inouts)):
            aliases[n_in + i0 + len(st.ins) + q] = n_out + o0 + q
        s_outs += [jax.ShapeDtypeStruct(a.shape, a.dtype) for a in st.inouts] + st.outs
        layout.append((i0, o0, n_sems))
        n_sems += st.n_sems
    steps = 1
    for g in grid:
        steps *= g

    def body(*refs):
        own_in = refs[:n_in]
        s_in = refs[n_in:n_in + len(s_args)]
        rest = refs[n_in + len(s_args):]
        own_out = rest[:n_out]
        s_out = rest[n_out:n_out + len(s_outs)]
        scr = rest[n_out + len(s_outs):]

        def run(which):
            for st, (i0, o0, s0) in zip(stages, layout):
                fn = getattr(st, which)
                if fn is not None:
                    fn(s_in[i0:i0 + len(st.ins)], s_out[o0:o0 + len(st.inouts)],
                       s_out[o0 + len(st.inouts):o0 + len(st.inouts) + len(st.outs)],
                       lambda k, s0=s0: scr[n_scr].at[s0 + k])

        if not stages:
            core(*own_in, *own_out, *scr[:n_scr])
            return
        step = 0
        for d, g in enumerate(grid):
            step = step * g + pl.program_id(d)
        if steps == 1:
            run("start")
            core(*own_in, *own_out, *scr[:n_scr])
            run("mid")
            run("finish")
            return
        pl.when(step == 0)(lambda: run("start"))
        core(*own_in, *own_out, *scr[:n_scr])
        pl.when(step == (3 * steps) // 4)(lambda: run("mid"))
        pl.when(step == steps - 1)(lambda: run("finish"))

    sem = ("arbitrary",) * len(grid) if stages else ("parallel",) * max(len(grid) - 1, 0) + ("arbitrary",) * min(len(grid), 1)
    res = pl.pallas_call(
        body, name=name, grid=grid,
        in_specs=list(in_specs) + [ANY] * len(s_args),
        out_specs=list(out_specs) + [ANY] * len(s_outs),
        out_shape=list(out_shape) + s_outs,
        input_output_aliases=aliases,
        scratch_shapes=list(scratch_shapes) + ([pltpu.SemaphoreType.DMA((n_sems,))] if stages else []),
        compiler_params=_params(sem) if grid else pltpu.CompilerParams(vmem_limit_bytes=V7X_VMEM_LIMIT),
    )(*args, *s_args)
    return list(res[:n_out]), list(res[n_out:])


class _Pipe:
    def __init__(self):
        self.ready = []
        self.flushes = 0
        self.after = None

    def add(self, stage):
        self.ready.append(stage)

    def carry(self, call, long=True):
        stages = [st for st in self.ready if long or not st.slow]
        self.ready = [st for st in self.ready if not (long or not st.slow)]
        own, outs = call(stages)
        k = 0
        for st in stages:
            n = len(st.inouts) + len(st.outs)
            st.then(*outs[k:k + n])
            k += n
        if self.after is not None:
            self.after()
        return own

    def flush(self):
        while self.ready:
            self.flushes += 1
            self.carry(lambda stages: _staged_call(
                lambda *refs: None, name=f"comm_tail_{self.flushes}", grid=(), in_specs=[], out_specs=[], out_shape=[],
                scratch_shapes=[], args=[], stages=stages))


def _mixer_fwd(layer, x, g1, bgate, lng, lnb, wm, bsf, wsc, win_g, wb_g, wout_g, stages):
    t_len = x.shape[0]
    tm = min(TM_MIX, t_len)
    nt = t_len // tm
    nb = tm // GMLP_BLOCK

    def core(x_ref, x_late_ref, g1_ref, bgate_ref, lng_ref, lnb_ref, wm_ref, bsf_ref, wsc_ref, win_hbm, wb_hbm, wout_hbm,
             zc_ref, ya_ref, yb_ref, q_ref, sa_ref, ca_ref, sb_ref, cb_ref, ug_ref, fu_ref, xh_ref, cv_ref,
             mg_ref, h_ref, x2_ref,
             win_v, wb_v, wout_v, carry, vn_s, f_s, z_s, sems):
        i = pl.program_id(0)

        @pl.when(i == 0)
        def _():
            cps = (_load_col_sharded(win_hbm, win_v, sems, 0) + _load_branch(wb_hbm, wb_v, sems, 4)
                   + _load_row_sharded(wout_hbm, wout_v, sems, 12))
            _start_all(cps)
            carry[...] = jnp.zeros_like(carry)
            z_s[...] = jnp.zeros_like(z_s)
            _wait_all(cps)

        xv = x_ref[...]
        r = lax.rsqrt(jnp.mean(xv * xv, axis=-1, keepdims=True) + RMS_EPS)
        h_ref[...] = (xv * r * g1_ref[...]).astype(BF16)

        def zcols(c0, n, keep=None):
            zv = z_s[:, c0:c0 + n]
            z_s[:, c0:c0 + n] = _dot(h_ref[...], win_v[:, c0:c0 + n])
            if keep is not None:
                zc_ref[:, keep * D_B:(keep + 1) * D_B] = zv.astype(BF16)
            return zv

        v = zcols(C_V, D_A)
        vg, tv = _gelu(v)
        mu = jnp.mean(vg, axis=-1, keepdims=True)
        vc = vg - mu
        rstd = lax.rsqrt(jnp.mean(vc * vc, axis=-1, keepdims=True) + LN_EPS)
        xh = vc * rstd
        xh_ref[...] = xh.astype(BF16)
        cv_ref[...] = (rstd * _gelu_grad(v, tv)).astype(BF16)
        vn_s[...] = (xh * lng_ref[...] + lnb_ref[...]).astype(BF16)
        for hd in range(A_HEADS):
            cols = slice(hd * 128, (hd + 1) * 128)
            vcat = jnp.concatenate([vn_s[b * 128:(b + 1) * 128, cols] for b in range(nb)], axis=1)
            fcat = _dot(wm_ref[hd], vcat)
            for b in range(nb):
                f_s[b * 128:(b + 1) * 128, cols] = fcat[:, b * 128:(b + 1) * 128]
        u = zcols(C_U, D_A)
        ug, tu = _gelu(u)
        ug_ref[...] = ug.astype(BF16)
        fb = f_s[...] + jnp.concatenate([bsf_ref[...]] * nb, axis=0)
        fu_ref[...] = (fb * _gelu_grad(u, tu)).astype(BF16)
        ya_ref[...] = (ug * fb).astype(BF16)

        p = zcols(C_CG, D_B, keep=1) * zcols(C_HB, D_B, keep=2)
        cr = carry[...]
        q = wsc_ref[0:1, :] * _shift_down(p, cr, 2) + wsc_ref[1:2, :] * _shift_down(p, cr, 1) + wsc_ref[2:3, :] * p
        carry[...] = p[tm - 8:tm, :]
        q_ref[...] = q.astype(BF16)
        yb_ref[...] = (zcols(C_BG, D_B, keep=0) * q).astype(BF16)

        av = _dot(ya_ref[...], wb_v[0])
        sa = _sigmoid(zcols(C_GA, D_MODEL) + bgate_ref[:, 0:D_MODEL])
        sa_ref[...] = sa.astype(BF16)
        mg = sa * av
        ca_ref[...] = (mg * (1.0 - sa)).astype(BF16)
        bv = _dot(yb_ref[...], wb_v[1])
        sb = _sigmoid(zcols(C_GB, D_MODEL) + bgate_ref[:, D_MODEL:2 * D_MODEL])
        sb_ref[...] = sb.astype(BF16)
        mb = sb * bv
        cb_ref[...] = (mb * (1.0 - sb)).astype(BF16)
        mg_ref[...] = (mg + mb).astype(BF16)
        x2_ref[...] = x_late_ref[...] + _dot(mg_ref[...], wout_v[...])

    def tile(n, lag):
        return pl.BlockSpec((tm, n), lambda i: (jnp.clip(i - lag, 0, nt - 1), 0))

    outs = [
        jax.ShapeDtypeStruct((t_len, 3 * D_B), BF16),
        jax.ShapeDtypeStruct((t_len, D_A), BF16),
        jax.ShapeDtypeStruct((t_len, D_B), BF16),
        jax.ShapeDtypeStruct((t_len, D_B), BF16),
        jax.ShapeDtypeStruct((t_len, D_MODEL), BF16),
        jax.ShapeDtypeStruct((t_len, D_MODEL), BF16),
        jax.ShapeDtypeStruct((t_len, D_MODEL), BF16),
        jax.ShapeDtypeStruct((t_len, D_MODEL), BF16),
        jax.ShapeDtypeStruct((t_len, D_A), BF16),
        jax.ShapeDtypeStruct((t_len, D_A), BF16),
        jax.ShapeDtypeStruct((t_len, D_A), BF16),
        jax.ShapeDtypeStruct((t_len, D_A), BF16),
        jax.ShapeDtypeStruct((t_len, D_MODEL), BF16),
        jax.ShapeDtypeStruct((t_len, D_MODEL), BF16),
        jax.ShapeDtypeStruct((t_len, D_MODEL), F32),
    ]
    return _staged_call(
        core, name=f"mixer_fwd_l{layer}", grid=(nt + 1,),
        in_specs=[tile(D_MODEL, 0), tile(D_MODEL, 1), _const_spec((1, D_MODEL)), _const_spec((1, 2 * D_MODEL)),
                  _const_spec((1, D_A)), _const_spec((1, D_A)), _const_spec((A_HEADS, 128, 128)),
                  _const_spec((128, D_A)), _const_spec((8, D_B)), ANY, ANY, ANY],
        out_specs=[tile(o.shape[1], 0 if k == len(outs) - 2 else 1) for k, o in enumerate(outs)],
        out_shape=outs,
        scratch_shapes=[pltpu.VMEM((D_MODEL, D_IN), BF16), pltpu.VMEM((2, D_A, D_MODEL), BF16),
                        pltpu.VMEM((D_MODEL, D_MODEL), BF16), pltpu.VMEM((8, D_B), F32),
                        pltpu.VMEM((tm, D_A), BF16), pltpu.VMEM((tm, D_A), F32), pltpu.VMEM((tm, D_IN), F32),
                        pltpu.SemaphoreType.DMA((16,))],
        args=[x, x, g1, bgate, lng, lnb, wm, bsf, wsc, win_g, wb_g, wout_g], stages=stages)


def _ffn_fwd(layer, x2, g2, wfc, bfc, wup_g, wdown_g, stages, head=None):
    t_len = x2.shape[0]
    tm = min(TM_FFN, t_len)
    nt = t_len // tm

    def core(*refs):
        if head is None:
            (x_ref, g2_ref, wfc_ref, bfc_ref, wup_hbm, wdown_hbm, up_ref, silu_ref, dsilu_ref, act_ref, h_ref, x3_ref,
             wup_v, wdown_v, carry, sems) = refs
        else:
            (x_ref, g2_ref, wfc_ref, bfc_ref, t_ref, gf_ref, wup_hbm, wdown_hbm, up_ref, silu_ref, dsilu_ref, act_ref,
             h_ref, dx_ref, dgf_ref, loss_ref, wup_v, wdown_v, carry, sems) = refs
        i = pl.program_id(0)

        @pl.when(i == 0)
        def _():
            cps = _load_col_sharded(wup_hbm, wup_v, sems, 0) + _load_row_sharded(wdown_hbm, wdown_v, sems, 4)
            _start_all(cps)
            carry[...] = jnp.zeros_like(carry)
            if head is not None:
                dgf_ref[...] = jnp.zeros_like(dgf_ref)
                loss_ref[...] = jnp.zeros_like(loss_ref)
            _wait_all(cps)

        xv = x_ref[...]
        r = lax.rsqrt(jnp.mean(xv * xv, axis=-1, keepdims=True) + RMS_EPS)
        h_ref[...] = (xv * r * g2_ref[...]).astype(BF16)
        gate = _dot(h_ref[...], wup_v[:, 0:D_FF])
        up_ref[:, 0:D_FF] = gate.astype(BF16)
        cr = carry[...]
        gc = (wfc_ref[0:1, :] * _shift_down(gate, cr, 2) + wfc_ref[1:2, :] * _shift_down(gate, cr, 1)
              + wfc_ref[2:3, :] * gate + bfc_ref[...])
        carry[...] = gate[tm - 8:tm, :]
        sg = _sigmoid(gc)
        silu = gc * sg
        silu_ref[...] = silu.astype(BF16)
        dsilu_ref[...] = (sg + silu * (1.0 - sg)).astype(BF16)
        val = _dot(h_ref[...], wup_v[:, D_FF:2 * D_FF])
        up_ref[:, D_FF:2 * D_FF] = val.astype(BF16)
        act_ref[...] = (silu * val).astype(BF16)
        x3 = x_ref[...] + _dot(act_ref[...], wdown_v[...])
        if head is None:
            x3_ref[...] = x3
        else:
            r3 = lax.rsqrt(jnp.mean(x3 * x3, axis=-1, keepdims=True) + RMS_EPS)
            xh = x3 * r3
            err = xh * gf_ref[...] - t_ref[...]
            loss_ref[...] += _colsum8(err * err)
            dy = err * (1.0 / D_MODEL)
            dgf_ref[...] += _colsum8(dy * xh)
            dxh = dy * gf_ref[...]
            dx_ref[...] = r3 * (dxh - xh * jnp.mean(dxh * xh, axis=-1, keepdims=True))

    outs = [
        jax.ShapeDtypeStruct((t_len, 2 * D_FF), BF16),
        jax.ShapeDtypeStruct((t_len, D_FF), BF16),
        jax.ShapeDtypeStruct((t_len, D_FF), BF16),
        jax.ShapeDtypeStruct((t_len, D_FF), BF16),
        jax.ShapeDtypeStruct((t_len, D_MODEL), BF16),
        jax.ShapeDtypeStruct((t_len, D_MODEL), F32),
    ]
    in_specs = [_row_spec(tm, D_MODEL), _const_spec((1, D_MODEL)), _const_spec((8, D_FF)), _const_spec((1, D_FF))]
    out_specs = [_row_spec(tm, o.shape[1]) for o in outs]
    args = [x2, g2, wfc, bfc]
    if head is not None:
        in_specs += [_row_spec(tm, D_MODEL), _const_spec((1, D_MODEL))]
        args += list(head)
        outs += [jax.ShapeDtypeStruct((8, D_MODEL), F32)] * 2
        out_specs += [_const_spec((8, D_MODEL))] * 2
    return _staged_call(
        core, name=f"ffn_fwd_l{layer}", grid=(nt,),
        in_specs=in_specs + [ANY, ANY], out_specs=out_specs, out_shape=outs,
        scratch_shapes=[pltpu.VMEM((D_MODEL, 2 * D_FF), BF16), pltpu.VMEM((D_FF, D_MODEL), BF16),
                        pltpu.VMEM((8, D_FF), F32), pltpu.SemaphoreType.DMA((8,))],
        args=args + [wup_g, wdown_g], stages=stages)


def _ffn_bwd(layer, dx3, x2, up, silu, dsilu, g2, wfc, wup_g, wdown_g):
    t_len = x2.shape[0]
    tm = min(TM_FFN, t_len)
    nt = t_len // tm

    def core(dx3_ref, dx3_late_ref, x_ref, up_ref, silu_ref, dsilu_ref, g2_ref, wfc_ref, wup_hbm, wdown_hbm,
             dx2_ref, dup_ref, dx3b_ref, dg2_ref, dbfc_ref, dwfc_ref,
             wup_v, wdown_v, carry, da_s, dup_s, sems):
        i = pl.program_id(0)

        @pl.when(i == 0)
        def _():
            cps = _load_col_sharded(wup_hbm, wup_v, sems, 0) + _load_row_sharded(wdown_hbm, wdown_v, sems, 4)
            _start_all(cps)
            for ref in (carry, da_s, dup_s, dg2_ref, dbfc_ref, dwfc_ref):
                ref[...] = jnp.zeros_like(ref)
            _wait_all(cps)

        live = (i <= nt).astype(F32)
        dx3b_ref[...] = dx3_ref[...].astype(BF16)
        dh = jnp.zeros((tm, D_MODEL), F32)
        for c0, c1 in FF_CHUNKS:
            v0, v1 = D_FF + c0, D_FF + c1
            dh = dh + _dot_nt(dup_s[:, c0:c1], wup_v[:, c0:c1]) + _dot_nt(dup_s[:, v0:v1], wup_v[:, v0:v1])
            da = da_s[:, c0:c1]
            dval = (da * silu_ref[:, c0:c1].astype(F32)).astype(BF16)
            dup_ref[:, v0:v1] = dval
            dup_s[:, v0:v1] = dval
            dgc = da * up_ref[:, v0:v1].astype(F32) * dsilu_ref[:, c0:c1].astype(F32)
            cr = carry[:, c0:c1]
            dgc1 = _shift_up(dgc, cr, 1)
            dgc2 = _shift_up(dgc, cr, 2)
            carry[:, c0:c1] = jnp.where(i < nt, dgc[0:8, :], cr)
            gate = up_ref[:, c0:c1].astype(F32)
            dbfc_ref[:, c0:c1] += live * _colsum8(dgc)
            dwfc_ref[0, :, c0:c1] += live * _colsum8(dgc2 * gate)
            dwfc_ref[1, :, c0:c1] += live * _colsum8(dgc1 * gate)
            dwfc_ref[2, :, c0:c1] += live * _colsum8(dgc * gate)
            dgate = (wfc_ref[2:3, c0:c1] * dgc + wfc_ref[1:2, c0:c1] * dgc1 + wfc_ref[0:1, c0:c1] * dgc2).astype(BF16)
            dup_ref[:, c0:c1] = dgate
            dup_s[:, c0:c1] = dgate
            da_s[:, c0:c1] = _dot_nt(dx3b_ref[...], wdown_v[c0:c1, :])
        xv = x_ref[...]
        r = lax.rsqrt(jnp.mean(xv * xv, axis=-1, keepdims=True) + RMS_EPS)
        xh = xv * r
        dg2_ref[...] += _colsum8(dh * xh)
        dxh = dh * g2_ref[...]
        dx2_ref[...] = dx3_late_ref[...] + r * (dxh - xh * jnp.mean(dxh * xh, axis=-1, keepdims=True))

    def tile(n, lag):
        return pl.BlockSpec((tm, n), lambda i: (nt - 1 - jnp.clip(i - lag, 0, nt - 1), 0))

    outs = [
        jax.ShapeDtypeStruct((t_len, D_MODEL), F32),
        jax.ShapeDtypeStruct((t_len, 2 * D_FF), BF16),
        jax.ShapeDtypeStruct((t_len, D_MODEL), BF16),
        jax.ShapeDtypeStruct((8, D_MODEL), F32),
        jax.ShapeDtypeStruct((8, D_FF), F32),
        jax.ShapeDtypeStruct((3, 8, D_FF), F32),
    ]
    return _staged_call(
        core, name=f"ffn_bwd_l{layer}", grid=(nt + 2,),
        in_specs=[tile(D_MODEL, 0), tile(D_MODEL, 2), tile(D_MODEL, 2), tile(2 * D_FF, 1), tile(D_FF, 1), tile(D_FF, 1),
                  _const_spec((1, D_MODEL)), _const_spec((8, D_FF)), ANY, ANY],
        out_specs=[tile(D_MODEL, 2), tile(2 * D_FF, 1), tile(D_MODEL, 0),
                   _const_spec((8, D_MODEL)), _const_spec((8, D_FF)), _const_spec((3, 8, D_FF))],
        out_shape=outs,
        scratch_shapes=[pltpu.VMEM((D_MODEL, 2 * D_FF), BF16), pltpu.VMEM((D_FF, D_MODEL), BF16),
                        pltpu.VMEM((8, D_FF), F32), pltpu.VMEM((tm, D_FF), F32), pltpu.VMEM((tm, 2 * D_FF), BF16),
                        pltpu.SemaphoreType.DMA((8,))],
        args=[dx3, dx3, x2, up, silu, dsilu, g2, wfc, wup_g, wdown_g], stages=[])[0]


def _mixer_bwd(layer, dx2, x, zc, qs, sa, ca, sb, cb, ug, fu, xhs, cv, g1, lng, lnb, wmt, wsc, win_g, wb_g, wout_g):
    t_len = x.shape[0]
    tm = min(TM_MIX, t_len)
    nt = t_len // tm
    nb = tm // GMLP_BLOCK

    def core(dx2_ref, x_ref, zc_ref, q_ref, sa_ref, ca_ref, sb_ref, cb_ref, ug_ref, fu_ref, xh_ref, cv_ref,
             g1_ref, lng_ref, lnb_ref, wmt_ref, wsc_ref, win_hbm, wb_hbm, wout_hbm,
             dx_ref, dz_ref, da_ref, db_ref, dx2b_ref, dg1_ref, dbgate_ref, dlng_ref, dlnb_ref, dwm_ref, dbsf_ref, dwsc_ref,
             win_v, wb_v, wout_v, carry, vn_s, df_s, dvn_s, sems):
        i = pl.program_id(0)

        @pl.when(i == 0)
        def _():
            cps = (_load_col_sharded(win_hbm, win_v, sems, 0) + _load_branch(wb_hbm, wb_v, sems, 4)
                   + _load_row_sharded(wout_hbm, wout_v, sems, 12))
            _start_all(cps)
            for ref in (carry, dg1_ref, dbgate_ref, dlng_ref, dlnb_ref, dwm_ref, dbsf_ref, dwsc_ref):
                ref[...] = jnp.zeros_like(ref)
            _wait_all(cps)

        def kept(k):
            return zc_ref[:, k * D_B:(k + 1) * D_B].astype(F32)

        def dz_cols(c0, n, val):
            dz_ref[:, c0:c0 + n] = val.astype(BF16)
            return _dot_nt(dz_ref[:, c0:c0 + n], win_v[:, c0:c0 + n])

        dx2b_ref[...] = dx2_ref[...].astype(BF16)
        dm = _dot_nt(dx2b_ref[...], wout_v[...])
        da_ref[...] = (dm * sa_ref[...].astype(F32)).astype(BF16)
        dga = dm * ca_ref[...].astype(F32)
        dh = dz_cols(C_GA, D_MODEL, dga)
        dbgate_ref[:, 0:D_MODEL] += _colsum8(dga)
        dya = _dot_nt(da_ref[...], wb_v[0])
        db_ref[...] = (dm * sb_ref[...].astype(F32)).astype(BF16)
        dgb = dm * cb_ref[...].astype(F32)
        dh = dh + dz_cols(C_GB, D_MODEL, dgb)
        dbgate_ref[:, D_MODEL:2 * D_MODEL] += _colsum8(dgb)
        dyb = _dot_nt(db_ref[...], wb_v[1])

        xh = xh_ref[...].astype(F32)
        vn_s[...] = (xh * lng_ref[...] + lnb_ref[...]).astype(BF16)
        df = dya * ug_ref[...].astype(F32)
        df_s[...] = df.astype(BF16)
        dbsf_acc = df[0:128, :]
        for b in range(1, nb):
            dbsf_acc = dbsf_acc + df[b * 128:(b + 1) * 128, :]
        dbsf_ref[...] += dbsf_acc
        for hd in range(A_HEADS):
            cols = slice(hd * 128, (hd + 1) * 128)
            vcat = jnp.concatenate([vn_s[b * 128:(b + 1) * 128, cols] for b in range(nb)], axis=1)
            dcat = jnp.concatenate([df_s[b * 128:(b + 1) * 128, cols] for b in range(nb)], axis=1)
            gcat = _dot(wmt_ref[hd], dcat)
            dwm_ref[hd] += _dot_nt(dcat, vcat)
            for b in range(nb):
                dvn_s[b * 128:(b + 1) * 128, cols] = gcat[:, b * 128:(b + 1) * 128]
        dh = dh + dz_cols(C_U, D_A, dya * fu_ref[...].astype(F32))
        dvn = dvn_s[...]
        dlng_ref[...] += _colsum8(dvn * xh)
        dlnb_ref[...] += _colsum8(dvn)
        dxh = dvn * lng_ref[...]
        dvc = dxh - jnp.mean(dxh, axis=-1, keepdims=True) - xh * jnp.mean(dxh * xh, axis=-1, keepdims=True)
        dh = dh + dz_cols(C_V, D_A, dvc * cv_ref[...].astype(F32))

        cg = kept(1)
        hbv = kept(2)
        p = cg * hbv
        dh = dh + dz_cols(C_BG, D_B, dyb * q_ref[...].astype(F32))
        dq = dyb * kept(0)
        cr = carry[...]
        dq1 = _shift_up(dq, cr, 1)
        dq2 = _shift_up(dq, cr, 2)
        carry[...] = dq[0:8, :]
        dwsc_ref[0] += _colsum8(dq2 * p)
        dwsc_ref[1] += _colsum8(dq1 * p)
        dwsc_ref[2] += _colsum8(dq * p)
        dp = wsc_ref[2:3, :] * dq + wsc_ref[1:2, :] * dq1 + wsc_ref[0:1, :] * dq2
        dh = dh + dz_cols(C_CG, D_B, dp * hbv)
        dh = dh + dz_cols(C_HB, D_B, dp * cg)

        xv = x_ref[...]
        r = lax.rsqrt(jnp.mean(xv * xv, axis=-1, keepdims=True) + RMS_EPS)
        xn = xv * r
        dg1_ref[...] += _colsum8(dh * xn)
        dxn = dh * g1_ref[...]
        dx_ref[...] = dx2_ref[...] + r * (dxn - xn * jnp.mean(dxn * xn, axis=-1, keepdims=True))

    outs = [
        jax.ShapeDtypeStruct((t_len, D_MODEL), F32),
        jax.ShapeDtypeStruct((t_len, D_IN), BF16),
        jax.ShapeDtypeStruct((t_len, D_MODEL), BF16),
        jax.ShapeDtypeStruct((t_len, D_MODEL), BF16),
        jax.ShapeDtypeStruct((t_len, D_MODEL), BF16),
        jax.ShapeDtypeStruct((8, D_MODEL), F32),
        jax.ShapeDtypeStruct((8, 2 * D_MODEL), F32),
        jax.ShapeDtypeStruct((8, D_A), F32),
        jax.ShapeDtypeStruct((8, D_A), F32),
        jax.ShapeDtypeStruct((A_HEADS, 128, 128), F32),
        jax.ShapeDtypeStruct((128, D_A), F32),
        jax.ShapeDtypeStruct((3, 8, D_B), F32),
    ]

    return _staged_call(
        core, name=f"mixer_bwd_l{layer}", grid=(nt,),
        in_specs=[_row_spec(tm, D_MODEL, nt), _row_spec(tm, D_MODEL, nt), _row_spec(tm, 3 * D_B, nt),
                  _row_spec(tm, D_B, nt), _row_spec(tm, D_MODEL, nt), _row_spec(tm, D_MODEL, nt),
                  _row_spec(tm, D_MODEL, nt), _row_spec(tm, D_MODEL, nt), _row_spec(tm, D_A, nt), _row_spec(tm, D_A, nt),
                  _row_spec(tm, D_A, nt), _row_spec(tm, D_A, nt),
                  _const_spec((1, D_MODEL)), _const_spec((1, D_A)), _const_spec((1, D_A)),
                  _const_spec((A_HEADS, 128, 128)), _const_spec((8, D_B)), ANY, ANY, ANY],
        out_specs=[_row_spec(tm, D_MODEL, nt), _row_spec(tm, D_IN, nt), _row_spec(tm, D_MODEL, nt),
                   _row_spec(tm, D_MODEL, nt), _row_spec(tm, D_MODEL, nt),
                   _const_spec((8, D_MODEL)), _const_spec((8, 2 * D_MODEL)), _const_spec((8, D_A)), _const_spec((8, D_A)),
                   _const_spec((A_HEADS, 128, 128)), _const_spec((128, D_A)), _const_spec((3, 8, D_B))],
        out_shape=outs,
        scratch_shapes=[pltpu.VMEM((D_MODEL, D_IN), BF16), pltpu.VMEM((2, D_A, D_MODEL), BF16),
                        pltpu.VMEM((D_MODEL, D_MODEL), BF16), pltpu.VMEM((8, D_B), F32),
                        pltpu.VMEM((tm, D_A), BF16), pltpu.VMEM((tm, D_A), BF16), pltpu.VMEM((tm, D_A), F32),
                        pltpu.SemaphoreType.DMA((16,))],
        args=[dx2, x, zc, qs, sa, ca, sb, cb, ug, fu, xhs, cv, g1, lng, lnb, wmt, wsc, win_g, wb_g, wout_g],
        stages=[])[0]


def _wgrad(name, layer, a, b, rows, cols, row_blk, col_blk, stages):
    t_len, m = a.shape
    n = b.shape[1]
    tk = min(TK_WGRAD, t_len)
    col_sharded = n == N_CHIPS * cols
    grid = (m // row_blk, n // col_blk, t_len // tk)
    shards = col_blk // cols if col_sharded else 1

    if col_sharded:
        out_shape = (N_CHIPS, rows, cols)
        out_spec = pl.BlockSpec((shards, row_blk, cols), lambda i, j, k: (j, i, 0))
    else:
        out_shape = (N_CHIPS * rows, cols)
        out_spec = pl.BlockSpec((row_blk, col_blk), lambda i, j, k: (i, j))

    def core(a_ref, b_ref, o_ref):
        @pl.when(pl.program_id(2) == 0)
        def _():
            o_ref[...] = jnp.zeros_like(o_ref)

        g = _dot_tn(a_ref[...], b_ref[...])
        if col_sharded:
            for q in range(shards):
                o_ref[q] += g[:, q * cols:(q + 1) * cols]
        else:
            o_ref[...] += g

    own, outs = _staged_call(
        core, name=f"wgrad_{name}_l{layer}", grid=grid,
        in_specs=[pl.BlockSpec((tk, row_blk), lambda i, j, k: (k, i)), pl.BlockSpec((tk, col_blk), lambda i, j, k: (k, j))],
        out_specs=[out_spec], out_shape=[jax.ShapeDtypeStruct(out_shape, F32)], scratch_shapes=[],
        args=[a, b], stages=stages)
    return [own[0].reshape(N_CHIPS, rows, cols)], outs


def _wgrad_branch(layer, ya, da, yb, db, stages):
    t_len = ya.shape[0]
    tk = min(TK_WGRAD, t_len)

    cs = D_MODEL // N_CHIPS

    def core(ya_ref, da_ref, yb_ref, db_ref, o_ref):
        @pl.when(pl.program_id(0) == 0)
        def _():
            o_ref[...] = jnp.zeros_like(o_ref)

        ga = _dot_tn(ya_ref[...], da_ref[...])
        gb = _dot_tn(yb_ref[...], db_ref[...])
        for k in range(N_CHIPS):
            o_ref[k, 0:D_A, :] += ga[:, k * cs:(k + 1) * cs]
            o_ref[k, D_A:2 * D_A, :] += gb[:, k * cs:(k + 1) * cs]

    a_spec = pl.BlockSpec((tk, D_A), lambda k: (k, 0))
    d_spec = pl.BlockSpec((tk, D_MODEL), lambda k: (k, 0))
    return _staged_call(
        core, name=f"wgrad_w_branch_l{layer}", grid=(t_len // tk,),
        in_specs=[a_spec, d_spec, a_spec, d_spec],
        out_specs=[pl.BlockSpec((N_CHIPS, 2 * D_A, cs), lambda k: (0, 0, 0))],
        out_shape=[jax.ShapeDtypeStruct((N_CHIPS, 2 * D_A, cs), F32)], scratch_shapes=[],
        args=[ya, da, yb, db], stages=stages)


def _flat_blk(rows, cols):
    blk = rows
    while blk * cols * 4 > 2 * 1024 * 1024 and blk % 16 == 0:
        blk //= 2
    return blk


def _cast_into_slots(name, jobs, chip, stages):
    blks = [_flat_blk(w.shape[1], w.shape[2]) for w, _ in jobs]
    nblks = [w.shape[1] // b for (w, _), b in zip(jobs, blks)]
    n = len(jobs)
    out_shape = [jax.ShapeDtypeStruct((N_CHIPS,) + w.shape[1:], BF16) for w, _ in jobs]

    def core(*refs):
        for w_ref, o_ref in zip(refs[-2 * n:-n], refs[-n:]):
            o_ref[...] = w_ref[...].astype(BF16)

    def slot(*scalars):
        return scalars[0][0] if scalars else 2 * lax.axis_index("x") + lax.axis_index("y")

    in_specs = [pl.BlockSpec((None, b, w.shape[2]), lambda i, *s, la=la, k=k: (la, jnp.minimum(i, k - 1), 0))
                for (w, la), b, k in zip(jobs, blks, nblks)]
    out_specs = [pl.BlockSpec((None, b, w.shape[2]), lambda i, *s, k=k: (slot(*s), jnp.minimum(i, k - 1), 0))
                 for (w, _), b, k in zip(jobs, blks, nblks)]
    args = [w for w, _ in jobs]
    if stages:
        return _staged_call(core, name=f"cast_{name}", grid=(max(nblks),), in_specs=in_specs, out_specs=out_specs,
                            out_shape=out_shape, scratch_shapes=[], args=args, stages=stages)
    own = pl.pallas_call(
        core, name=f"cast_{name}",
        grid_spec=pltpu.PrefetchScalarGridSpec(num_scalar_prefetch=1, grid=(max(nblks),), in_specs=in_specs,
                                               out_specs=out_specs),
        out_shape=out_shape, compiler_params=_params(),
    )(chip, *args)
    return list(own), []


def _reduction_sums(name, jobs, pos):
    in_specs, out_specs, out_shape, args, bodies, counts = [], [], [], [], [], []
    for job in jobs:
        kind, grad, other = job[0], job[1], job[2]
        _, h, cols = other.shape
        blk = _flat_blk(h, cols)
        nblk = h // blk
        if kind == "pair":
            total = N_CHIPS * nblk

            def block(s, total=total, nblk=nblk):
                b = jnp.minimum(s, total - 1)
                return b // nblk, b % nblk

            spec = pl.BlockSpec((None, blk, cols), lambda s, p, block=block: (block(s)[0], block(s)[1], 0))
            in_specs += [pl.BlockSpec((None, blk, cols), lambda s, p, block=block, nblk=nblk:
                                      (block(s)[0], p[1] * nblk + block(s)[1], 0)), spec]
            out_specs.append(spec)
            out_shape.append(jax.ShapeDtypeStruct((N_CHIPS, h, cols), BF16))
            args += [grad, other]
            bodies.append((2, lambda g, o, out: out.__setitem__(..., (g[...] + o[...]).astype(BF16))))
        else:
            total = nblk

            def block(s, total=total):
                return jnp.minimum(s, total - 1)

            in_specs += [pl.BlockSpec((None, blk, cols), lambda s, p, block=block, nblk=nblk:
                                      (p[0], p[1] * nblk + block(s), 0)),
                         pl.BlockSpec((None, blk, cols), lambda s, p, block=block: (p[0], block(s), 0)),
                         pl.BlockSpec((3, blk, cols), lambda s, p, block=block: (0, block(s), 0))]
            out_specs.append(pl.BlockSpec((blk, cols), lambda s, p, block=block, nblk=nblk: (p[1] * nblk + block(s), 0)))
            out_shape.append(jax.ShapeDtypeStruct((2 * h, cols), F32))
            args += [grad, other, job[3]]
            bodies.append((3, lambda g, o, r, out: out.__setitem__(
                ..., (((g[...] + o[...]) + r[0].astype(F32)) + r[1].astype(F32)) + r[2].astype(F32))))
        counts.append(total)

    def body(pos_ref, *refs):
        ins, outs = refs[:len(args)], refs[len(args):]
        k = 0
        for (n_in, fn), out in zip(bodies, outs):
            fn(*ins[k:k + n_in], out)
            k += n_in

    return pl.pallas_call(
        body, name=f"reduction_sums_{name}",
        grid_spec=pltpu.PrefetchScalarGridSpec(num_scalar_prefetch=1, grid=(max(counts),), in_specs=in_specs,
                                               out_specs=out_specs),
        out_shape=out_shape,
        compiler_params=_params(),
    )(pos, *args)


def _sum_slots(name, slots):
    n, rows, _ = slots.shape

    def body(s_ref, o_ref):
        acc = s_ref[0]
        for d in range(1, n):
            acc = acc + s_ref[d]
        o_ref[...] = acc

    return pl.pallas_call(
        body, name=f"sum_slots_{name}", grid=(1,),
        in_specs=[pl.BlockSpec((n, rows, 128), lambda i: (0, 0, 0))],
        out_specs=pl.BlockSpec((rows, 128), lambda i: (0, 0)),
        out_shape=jax.ShapeDtypeStruct((rows, 128), F32),
        compiler_params=_params(),
    )(slots)


def _adamw_math(w, g, m, v):
    m2 = ADAM_B1 * m + (1.0 - ADAM_B1) * g
    v2 = ADAM_B2 * v + (1.0 - ADAM_B2) * (g * g)
    m_hat = m2 / (1.0 - ADAM_B1 ** ADAM_STEP)
    v_hat = v2 / (1.0 - ADAM_B2 ** ADAM_STEP)
    delta = -ADAM_LR * (m_hat / (jnp.sqrt(v_hat) + ADAM_EPS) + ADAM_WD * w)
    return delta, m2, v2


def _adamw_big(name, w, g0, g1, m, v):
    _, rows, cols = w.shape
    blk = _flat_blk(rows, cols) // 2

    def body(w_ref, g0_ref, g1_ref, m_ref, v_ref, g_ref, d_ref, m2_ref, v2_ref):
        g = jnp.where(pl.program_id(0) == 0, g0_ref[...], g1_ref[...])
        d, m2, v2 = _adamw_math(w_ref[...], g, m_ref[...], v_ref[...])
        g_ref[...] = g
        d_ref[...] = d
        m2_ref[...] = m2
        v2_ref[...] = v2

    spec = pl.BlockSpec((None, blk, cols), lambda la, i: (la, i, 0))
    return pl.pallas_call(
        body, name=f"adamw_{name}", grid=(N_LAYERS, rows // blk),
        in_specs=[spec, pl.BlockSpec((blk, cols), lambda la, i: (i * (1 - la), 0)),
                  pl.BlockSpec((blk, cols), lambda la, i: (i * la, 0)), spec, spec],
        out_specs=[spec] * 4,
        out_shape=[jax.ShapeDtypeStruct(w.shape, F32)] * 4,
        compiler_params=_params(("parallel", "parallel")),
    )(w, g0, g1, m, v)


def _adamw_small(ws, gs, ms, vs):
    n = len(ws)

    def body(*refs):
        ins, outs = refs[:4 * n], refs[4 * n:]
        for k in range(n):
            d, m2, v2 = _adamw_math(ins[k][...], ins[n + k][...], ins[2 * n + k][...], ins[3 * n + k][...])
            outs[k][...] = d
            outs[n + k][...] = m2
            outs[2 * n + k][...] = v2

    vmem = pl.BlockSpec(memory_space=pltpu.VMEM)
    return pl.pallas_call(
        body, name="adamw_small",
        in_specs=[vmem] * (4 * n), out_specs=[vmem] * (3 * n),
        out_shape=[jax.ShapeDtypeStruct(w.shape, F32) for w in ws] * 3,
        compiler_params=pltpu.CompilerParams(vmem_limit_bytes=V7X_VMEM_LIMIT),
    )(*ws, *gs, *ms, *vs)


SMALL = ("norm1_g", "b_gate", "gmlp_ln_g", "gmlp_ln_b", "w_spatial", "b_spatial", "w_shortconv", "norm2_g",
         "w_ffn_conv", "b_ffn_conv", "final_g")
ALL_WEIGHTS = ("norm1_g", "w_in", "b_gate", "gmlp_ln_g", "gmlp_ln_b", "w_spatial", "b_spatial", "w_shortconv",
               "w_branch", "w_out", "norm2_g", "w_ffn_up", "w_ffn_conv", "b_ffn_conv", "w_ffn_down", "final_g")


def _pack(arrays):
    flat = jnp.concatenate([a.reshape(-1) for a in arrays])
    n = flat.shape[0]
    rows = -(-n // 1024) * 8
    return jnp.pad(flat, (0, rows * 128 - n)).reshape(rows, 128)


def _unpack(packed, like):
    flat = packed.reshape(-1)
    out, off = [], 0
    for a in like:
        out.append(flat[off:off + a.size].reshape(a.shape))
        off += a.size
    return out


def _pad8(w):
    return jnp.pad(w, ((0, 5), (0, 0)))


def kernel(x, norm1_g, w_in, b_gate, gmlp_ln_g, gmlp_ln_b, w_spatial, b_spatial, w_shortconv, w_branch, w_out, norm2_g, w_ffn_up, w_ffn_conv, b_ffn_conv, w_ffn_down, final_g, loss_target, m_norm1_g, m_w_in, m_b_gate, m_gmlp_ln_g, m_gmlp_ln_b, m_w_spatial, m_b_spatial, m_w_shortconv, m_w_branch, m_w_out, m_norm2_g, m_w_ffn_up, m_w_ffn_conv, m_b_ffn_conv, m_w_ffn_down, m_final_g, v_norm1_g, v_w_in, v_b_gate, v_gmlp_ln_g, v_gmlp_ln_b, v_w_spatial, v_b_spatial, v_w_shortconv, v_w_branch, v_w_out, v_norm2_g, v_w_ffn_up, v_w_ffn_conv, v_b_ffn_conv, v_w_ffn_down, v_final_g):
    weights = dict(norm1_g=norm1_g, w_in=w_in, b_gate=b_gate, gmlp_ln_g=gmlp_ln_g, gmlp_ln_b=gmlp_ln_b,
                   w_spatial=w_spatial, b_spatial=b_spatial, w_shortconv=w_shortconv, w_branch=w_branch, w_out=w_out,
                   norm2_g=norm2_g, w_ffn_up=w_ffn_up, w_ffn_conv=w_ffn_conv, b_ffn_conv=b_ffn_conv,
                   w_ffn_down=w_ffn_down, final_g=final_g)
    mom = dict(norm1_g=m_norm1_g, w_in=m_w_in, b_gate=m_b_gate, gmlp_ln_g=m_gmlp_ln_g, gmlp_ln_b=m_gmlp_ln_b,
               w_spatial=m_w_spatial, b_spatial=m_b_spatial, w_shortconv=m_w_shortconv, w_branch=m_w_branch,
               w_out=m_w_out, norm2_g=m_norm2_g, w_ffn_up=m_w_ffn_up, w_ffn_conv=m_w_ffn_conv,
               b_ffn_conv=m_b_ffn_conv, w_ffn_down=m_w_ffn_down, final_g=m_final_g)
    vel = dict(norm1_g=v_norm1_g, w_in=v_w_in, b_gate=v_b_gate, gmlp_ln_g=v_gmlp_ln_g, gmlp_ln_b=v_gmlp_ln_b,
               w_spatial=v_w_spatial, b_spatial=v_b_spatial, w_shortconv=v_w_shortconv, w_branch=v_w_branch,
               w_out=v_w_out, norm2_g=v_norm2_g, w_ffn_up=v_w_ffn_up, w_ffn_conv=v_w_ffn_conv,
               b_ffn_conv=v_b_ffn_conv, w_ffn_down=v_w_ffn_down, final_g=v_final_g)

    cx, cy, cc = _mesh_pos()
    chip = 2 * cx + cy
    pos_arr = jnp.stack([chip, cc]).astype(jnp.int32)
    t_len = x.shape[1]
    xs = x.reshape(t_len, D_MODEL)
    target = loss_target.reshape(t_len, D_MODEL)
    pipe = _Pipe()

    full = {}

    mixer_w = ("w_in", "w_branch", "w_out")
    ffn_w = ("w_ffn_up", "w_ffn_down")
    slots = {}

    def cast(name, keys, stages):
        own, outs = _cast_into_slots(name, [(weights[n].reshape((N_LAYERS,) + BIG[n]), la) for n, la in keys],
                                     chip.astype(jnp.int32).reshape(1), stages)
        slots.update(zip(keys, own))
        return own, outs

    def gather(names, la):
        def then(*bufs):
            full.update(zip([(n, la) for n in names], bufs))

        pipe.add(_gather_stage([slots[(n, la)] for n in names], then))

    first = [(n, 0) for n in mixer_w]
    cast("first", first, [])
    gather(mixer_w, 0)
    tap_slots = {}
    pipe.add(_chip_spread_stage(_pack([w_shortconv, w_ffn_conv]), lambda got: tap_slots.__setitem__("all", got)))
    pipe.carry(lambda st: cast("rest", [(n, la) for la in range(N_LAYERS) for n in BIG_NAMES if (n, la) not in first], st))
    by_chip = [_unpack(tap_slots["all"][k], [w_shortconv, w_ffn_conv]) for k in range(N_CHIPS)]
    wsc_full = jnp.concatenate([t[0] for t in by_chip], axis=-1)
    wfc_full = jnp.concatenate([t[1] for t in by_chip], axis=-1)

    idx = jnp.arange(GMLP_BLOCK) // CHUNK
    mask = idx[None, :] <= idx[:, None]
    wm_all = jnp.where(mask[None, None], w_spatial, 0.0)
    wm_bf = wm_all.astype(BF16)
    wmt_bf = jnp.swapaxes(wm_all, -1, -2).astype(BF16)
    bsf = jnp.repeat(jnp.swapaxes(b_spatial, -1, -2), 128, axis=-1)

    def row(a):
        return a.reshape(1, -1)

    def mixer_args(la):
        return (row(norm1_g[la]), row(b_gate[la]), row(gmlp_ln_g[la]), row(gmlp_ln_b[la]))

    def mixer_weights(la):
        return tuple(full[(n, la)] for n in mixer_w)

    def ffn_weights(la):
        return tuple(full[(n, la)] for n in ffn_w)

    saved = []
    h_in = xs
    for la in range(N_LAYERS):
        if la == 0:
            gather(ffn_w, 0)
        *kept, mg, h1, x2 = pipe.carry(lambda st: _mixer_fwd(
            la, h_in, *mixer_args(la), wm_bf[la], bsf[la], _pad8(wsc_full[la]), *mixer_weights(la), st))
        ya, yb = kept[1], kept[2]
        if la + 1 < N_LAYERS:
            gather(mixer_w, la + 1)
            gather(ffn_w, la + 1)
        head = (target, row(final_g)) if la == N_LAYERS - 1 else None
        up, silu, dsilu, act, h2, *rest = pipe.carry(lambda st: _ffn_fwd(
            la, x2, row(norm2_g[la]), _pad8(wfc_full[la]), row(b_ffn_conv[la]), *ffn_weights(la), st, head=head))
        saved.append(dict(x=h_in, ya=ya, yb=yb, mixer=[kept[0]] + kept[3:], mg=mg, h1=h1, x2=x2, up=up, silu=silu,
                          dsilu=dsilu, act=act, h2=h2))
        h_in = rest[0]
    dx, dgf8, loss8 = rest

    reduced_big = {}

    sums_due = []

    def run_sums():
        if sums_due:
            due = list(sums_due)
            sums_due.clear()
            run_sums.calls += 1
            for (_, then), res in zip(due, _reduction_sums(str(run_sums.calls), [job for job, _ in due], pos_arr)):
                then(res)

    run_sums.calls = 0
    pipe.after = run_sums

    def reduce_big(name, la, grad):
        def after_pair(other):
            def after_chips(got):
                sums_due.append((("chip", grad, other, got), lambda final: pipe.add(_pair_fill_stage(
                    final, lambda done: reduced_big.__setitem__((name, la), done)))))

            sums_due.append((("pair", grad, other), lambda psum: pipe.add(_chip_send_stage(psum, after_chips))))

        pipe.add(_pair_send_stage(grad, after_pair))

    small = {n: [None] * N_LAYERS for n in SMALL}
    spread = {}
    wgrad_in = {}
    for la in reversed(range(N_LAYERS)):
        s = saved[la]
        dx3 = dx
        dx2, dup, dx3b, dg2, dbfc, dwfc = _ffn_bwd(
            la, dx3, s["x2"], s["up"], s["silu"], s["dsilu"], row(norm2_g[la]), _pad8(wfc_full[la]),
            *ffn_weights(la))
        dxl, dz, da, db, dx2b, dg1, dbg, dlng, dlnb, dwm, dbsf, dwsc = _mixer_bwd(
            la, dx2, s["x"], *s["mixer"], row(norm1_g[la]), row(gmlp_ln_g[la]), row(gmlp_ln_b[la]), wmt_bf[la],
            _pad8(wsc_full[la]), *mixer_weights(la))
        wgrad_in[la] = dict(s, dup=dup, dx3b=dx3b, dz=dz, da=da, db=db, dx2b=dx2b)
        small["norm1_g"][la] = dg1.sum(0)
        small["b_gate"][la] = dbg.sum(0)
        small["gmlp_ln_g"][la] = dlng.sum(0)
        small["gmlp_ln_b"][la] = dlnb.sum(0)
        small["w_spatial"][la] = jnp.where(mask[None], dwm, 0.0)
        small["b_spatial"][la] = dbsf.reshape(128, A_HEADS, 128).sum(-1).T
        small["w_shortconv"][la] = dwsc.sum(1)
        small["norm2_g"][la] = dg2.sum(0)
        small["w_ffn_conv"][la] = dwfc.sum(1)
        small["b_ffn_conv"][la] = dbfc.sum(0)
        dx = dxl
    grad_x = dx.reshape(x.shape)

    small_local = [jnp.stack(small[n]) for n in SMALL[:-1]] + [dgf8.sum(0), 0.5 * loss8.sum().reshape(1) / D_MODEL]
    mine = _pack(small_local)

    def after_swap(other):
        pair = _sum_slots("small_pair", jnp.stack([mine, other]))
        pipe.add(_chip_spread_stage(pair, lambda slots: spread.__setitem__("slots", slots)))

    pipe.add(_pair_swap_stage(mine, after_swap))

    def wgrad(name, la, st):
        w = wgrad_in[la]
        if name == "w_ffn_up":
            return _wgrad(name, la, w["h2"], w["dup"], 1024, 1408, 512, 2816, st)
        if name == "w_in":
            return _wgrad(name, la, w["h1"], w["dz"], 1024, 1152, 512, 2304, st)
        if name == "w_ffn_down":
            return _wgrad(name, la, w["act"], w["dx3b"], 704, 1024, 1408, 1024, st)
        if name == "w_out":
            return _wgrad(name, la, w["mg"], w["dx2b"], 256, 1024, 1024, 1024, st)
        return _wgrad_branch(la, w["ya"], w["da"], w["yb"], w["db"], st)

    for name in ("w_ffn_up", "w_in", "w_ffn_down", "w_out", "w_branch"):
        for la in reversed(range(N_LAYERS)):
            g, = pipe.carry(lambda st: wgrad(name, la, st), long=name not in ("w_out", "w_branch"))
            reduce_big(name, la, g)
    pipe.flush()

    reduced = _unpack(_sum_slots("small_grads", spread["slots"]), small_local)
    loss = reduced[-1].reshape(())
    grads = dict(zip(SMALL, reduced[:-1]))
    grads["w_shortconv"] = lax.dynamic_slice(grads["w_shortconv"], (0, 0, chip * (D_B // 4)), (N_LAYERS, 3, D_B // 4))
    grads["w_ffn_conv"] = lax.dynamic_slice(grads["w_ffn_conv"], (0, 0, chip * (D_FF // 4)), (N_LAYERS, 3, D_FF // 4))

    delta, new_m, new_v = {}, {}, {}
    for n in BIG_NAMES:
        shape3 = (N_LAYERS,) + BIG[n]
        res = _adamw_big(n, weights[n].reshape(shape3), reduced_big[(n, 0)], reduced_big[(n, 1)],
                         mom[n].reshape(shape3), vel[n].reshape(shape3))
        grads[n], delta[n], new_m[n], new_v[n] = (a.reshape(weights[n].shape) for a in res)
    res = _adamw_small(*[[src[n].reshape(-1, src[n].shape[-1]) for n in SMALL] for src in (weights, grads, mom, vel)])
    for k, n in enumerate(SMALL):
        delta[n], new_m[n], new_v[n] = (res[j * len(SMALL) + k].reshape(weights[n].shape) for j in range(3))

    return (loss, grad_x, *[grads[n] for n in ALL_WEIGHTS], *[delta[n] for n in ALL_WEIGHTS],
            *[new_m[n] for n in ALL_WEIGHTS], *[new_v[n] for n in ALL_WEIGHTS])
```

```python
import jax
import jax.numpy as jnp
from jax import lax
from jax.experimental import pallas as pl
from jax.experimental.pallas import tpu as pltpu
from jax.experimental.pallas import tpu_sc as plsc

F32 = jnp.float32
BF16 = jnp.bfloat16
MESH = pl.DeviceIdType.MESH
ANY = pl.BlockSpec(memory_space=pl.ANY)

D_MODEL = 1024
D_A = 512
D_B = 512
D_IN = 4608
D_FF = 2816
GMLP_BLOCK = 128
CHUNK = 64
A_HEADS = 4
N_LAYERS = 2
N_CHIPS = 4
RMS_EPS = 1e-6
LN_EPS = 1e-5
ADAM_LR = 0.001
ADAM_B1 = 0.9
ADAM_B2 = 0.999
ADAM_EPS = 1e-08
ADAM_WD = 0.01
ADAM_STEP = 10

C_U, C_V, C_BG, C_CG, C_HB, C_GA, C_GB = 0, 512, 1024, 1536, 2048, 2560, 3584

V7X_VMEM_LIMIT = 60 * 1024 * 1024
TM_MIX = 256
TM_FFN = 256
TK_WGRAD = 2048
SLOW_COPY_BYTES = 768 * 1024
FF_CHUNKS = ((0, 768), (768, 1536), (1536, 2304), (2304, 2816))
GELU_C0 = 0.7978845608028654
GELU_C1 = 0.044715

BIG = {
    "w_in": (1024, 1152),
    "w_branch": (1024, 256),
    "w_out": (256, 1024),
    "w_ffn_up": (1024, 1408),
    "w_ffn_down": (704, 1024),
}
BIG_NAMES = tuple(BIG)


def _params(sem=("arbitrary",), vmem=V7X_VMEM_LIMIT):
    return pltpu.CompilerParams(dimension_semantics=sem, vmem_limit_bytes=vmem)


def _gelu(x):
    x2 = x * x
    t = jnp.tanh(GELU_C0 * x * (1.0 + GELU_C1 * x2))
    return 0.5 * x * (1.0 + t), t


def _gelu_grad(x, t):
    return 0.5 * (1.0 + t) + 0.5 * x * (1.0 - t * t) * GELU_C0 * (1.0 + 3.0 * GELU_C1 * x * x)


def _colsum8(v):
    r, n = v.shape
    return v.reshape(r // 8, 8, n).sum(axis=0)


def _dot(a, b):
    return jnp.dot(a, b, preferred_element_type=F32)


def _dot_nt(a, b):
    return lax.dot_general(a, b, (((1,), (1,)), ((), ())), preferred_element_type=F32)


def _dot_tn(a, b):
    return lax.dot_general(a, b, (((0,), (0,)), ((), ())), preferred_element_type=F32)


def _shift_down(v, carry, n):
    rows = lax.broadcasted_iota(jnp.int32, (8, v.shape[1]), 0)
    out = pltpu.roll(v, n, 0)
    head = out[0:8, :]
    for r in range(n):
        head = jnp.where(rows == r, carry[8 - n + r:8 - n + r + 1, :], head)
    return jnp.concatenate([head, out[8:, :]], axis=0)


def _shift_up(v, carry, n):
    tm = v.shape[0]
    rows = lax.broadcasted_iota(jnp.int32, (8, v.shape[1]), 0)
    out = pltpu.roll(v, tm - n, 0)
    tail = out[tm - 8:tm, :]
    for r in range(n):
        tail = jnp.where(rows == 8 - n + r, carry[r:r + 1, :], tail)
    return jnp.concatenate([out[0:tm - 8, :], tail], axis=0)


def _sigmoid(x):
    return 0.5 * jnp.tanh(0.5 * x) + 0.5


def _start_all(copies):
    for cp in copies:
        cp.start()


def _wait_all(copies):
    for cp in copies:
        cp.wait()


def _load_col_sharded(src, dst, sems, first):
    cs = src.shape[-1]
    return [pltpu.make_async_copy(src.at[k], dst.at[:, k * cs:(k + 1) * cs], sems.at[first + k])
            for k in range(N_CHIPS)]


def _load_row_sharded(src, dst, sems, first):
    rs = src.shape[-2]
    return [pltpu.make_async_copy(src.at[k], dst.at[k * rs:(k + 1) * rs, :], sems.at[first + k])
            for k in range(N_CHIPS)]


def _load_branch(src, dst, sems, first):
    return [pltpu.make_async_copy(src.at[k, pl.ds(m * D_A, D_A), :], dst.at[m, :, k * 256:(k + 1) * 256],
                                  sems.at[first + 2 * k + m])
            for k in range(N_CHIPS) for m in range(2)]


def _row_spec(tm, n, rev=None):
    if rev is None:
        return pl.BlockSpec((tm, n), lambda i: (i, 0))
    return pl.BlockSpec((tm, n), lambda i: (rev - 1 - i, 0))


def _const_spec(shape):
    nd = len(shape)
    return pl.BlockSpec(shape, lambda i: (0,) * nd)


def _mesh_pos():
    return lax.axis_index("x"), lax.axis_index("y"), lax.axis_index("c")


def _other_chips(x, y):
    return [(1 - x, y, 2 * (1 - x) + y), (x, 1 - y, 2 * x + (1 - y)), (1 - x, 1 - y, 2 * (1 - x) + (1 - y))]


def _remote(src, dst, ssem, rsem, to):
    return pltpu.make_async_remote_copy(src_ref=src, dst_ref=dst, send_sem=ssem, recv_sem=rsem, device_id=to,
                                        device_id_type=MESH)


def _half(ref, which, h):
    start = pl.multiple_of(which * h, 8)
    if len(ref.shape) == 2:
        return ref.at[pl.ds(start, h), :]
    return ref.at[:, pl.ds(start, h), :]


class _Stage:
    def __init__(self, ins=(), inouts=(), outs=(), n_sems=0, start=None, mid=None, finish=None, then=None, slow=False):
        self.ins, self.inouts, self.outs = list(ins), list(inouts), list(outs)
        self.n_sems, self.start, self.mid, self.finish, self.then = n_sems, start, mid, finish, then
        self.slow = slow


def _gather_stage(bufs, then):
    n = len(bufs)

    def copies(io, sem):
        x, y, c = _mesh_pos()
        me = 2 * x + y
        ici, fwd, got = [], [], []
        for w in range(n):
            h = io[w].shape[1] // 2
            for j, (px, py, pk) in enumerate(_other_chips(x, y)):
                mine = _half(io[w].at[me], c, h)
                theirs = _half(io[w].at[pk], c, h)
                ici.append(_remote(mine, mine, sem(12 * w + j), sem(12 * w + 3 + j), (px, py, c)))
                got.append(_remote(theirs, theirs, sem(12 * w + j), sem(12 * w + 3 + j), (px, py, c)))
                fwd.append(_remote(theirs, theirs, sem(12 * w + 6 + j), sem(12 * w + 9 + j), (x, y, 1 - c)))
        return ici, got, fwd

    def start(ins, io, outs, sem):
        _start_all(copies(io, sem)[0])

    def mid(ins, io, outs, sem):
        _, got, fwd = copies(io, sem)
        for g, f in zip(got, fwd):
            g.wait_recv()
            f.start()

    def finish(ins, io, outs, sem):
        x, y, c = _mesh_pos()
        ici, _, fwd = copies(io, sem)
        for w in range(n):
            h = io[w].shape[1] // 2
            for j, (px, py, pk) in enumerate(_other_chips(x, y)):
                other = _half(io[w].at[pk], 1 - c, h)
                _remote(other, other, sem(12 * w + 6 + j), sem(12 * w + 9 + j), (x, y, 1 - c)).wait_recv()
        for cp in ici + fwd:
            cp.wait_send()

    return _Stage(inouts=bufs, n_sems=12 * n, start=start, mid=mid, finish=finish, then=then)


def _pair_send_stage(grad, then):
    h = grad.shape[1] // 2

    def copy(ins, outs, sem):
        x, y, c = _mesh_pos()
        return _remote(_half(ins[0], 1 - c, h), outs[0], sem(0), sem(1), (x, y, 1 - c))

    return _Stage(ins=[grad], outs=[jax.ShapeDtypeStruct((N_CHIPS, h, grad.shape[2]), F32)], n_sems=2,
                  start=lambda ins, io, outs, sem: copy(ins, outs, sem).start(),
                  finish=lambda ins, io, outs, sem: copy(ins, outs, sem).wait(), then=then)


def _chip_send_stage(psum, then):
    def copies(ins, outs, sem):
        x, y, c = _mesh_pos()
        return [_remote(ins[0].at[pk], outs[0].at[j], sem(j), sem(3 + j), (px, py, c))
                for j, (px, py, pk) in enumerate(_other_chips(x, y))]

    return _Stage(ins=[psum], outs=[jax.ShapeDtypeStruct((3,) + psum.shape[1:], BF16)], n_sems=6,
                  start=lambda ins, io, outs, sem: _start_all(copies(ins, outs, sem)),
                  finish=lambda ins, io, outs, sem: _wait_all(copies(ins, outs, sem)), then=then,
                  slow=psum.shape[1] * psum.shape[2] * 2 > SLOW_COPY_BYTES)


def _pair_fill_stage(final, then):
    h = final.shape[0] // 2

    def copy(io, sem):
        x, y, c = _mesh_pos()
        mine = _half(io[0], c, h)
        return _remote(mine, mine, sem(0), sem(1), (x, y, 1 - c))

    return _Stage(inouts=[final], n_sems=2,
                  start=lambda ins, io, outs, sem: copy(io, sem).start(),
                  finish=lambda ins, io, outs, sem: copy(io, sem).wait(), then=then)


def _pair_swap_stage(packed, then):
    def copy(ins, outs, sem):
        x, y, c = _mesh_pos()
        return _remote(ins[0], outs[0], sem(0), sem(1), (x, y, 1 - c))

    return _Stage(ins=[packed], outs=[jax.ShapeDtypeStruct(packed.shape, F32)], n_sems=2,
                  start=lambda ins, io, outs, sem: copy(ins, outs, sem).start(),
                  finish=lambda ins, io, outs, sem: copy(ins, outs, sem).wait(), then=then)


def _chip_spread_stage(psum, then):
    def copies(ins, outs, sem):
        x, y, c = _mesh_pos()
        me = 2 * x + y
        cps = [_remote(ins[0], outs[0].at[me], sem(j), sem(3 + j), (px, py, c))
               for j, (px, py, pk) in enumerate(_other_chips(x, y))]
        return cps, pltpu.make_async_copy(ins[0], outs[0].at[me], sem(6))

    def start(ins, io, outs, sem):
        cps, own = copies(ins, outs, sem)
        own.start()
        _start_all(cps)

    def finish(ins, io, outs, sem):
        cps, own = copies(ins, outs, sem)
        _wait_all(cps)
        own.wait()

    return _Stage(ins=[psum], outs=[jax.ShapeDtypeStruct((N_CHIPS,) + psum.shape, F32)], n_sems=7,
                  start=start, finish=finish, then=then)


def _staged_call(core, *, name, grid, in_specs, out_specs, out_shape, scratch_shapes, args, stages):
    n_in, n_out, n_scr = len(args), len(out_shape), len(scratch_shapes)
    s_args, s_outs, aliases, layout = [], [], {}, []
    n_sems = 0
    for st in stages:
        i0, o0 = len(s_args), len(s_outs)
        s_args += st.ins + st.inouts
        for q in range(len(st.inouts)):
            aliases[n_in + i0 + len(st.ins) + q] = n_out + o0 + q
        s_outs += [jax.ShapeDtypeStruct(a.shape, a.dtype) for a in st.inouts] + st.outs
        layout.append((i0, o0, n_sems))
        n_sems += st.n_sems
    steps = 1
    for g in grid:
        steps *= g

    def body(*refs):
        own_in = refs[:n_in]
        s_in = refs[n_in:n_in + len(s_args)]
        rest = refs[n_in + len(s_args):]
        own_out = rest[:n_out]
        s_out = rest[n_out:n_out + len(s_outs)]
        scr = rest[n_out + len(s_outs):]

        def run(which):
            for st, (i0, o0, s0) in zip(stages, layout):
                fn = getattr(st, which)
                if fn is not None:
                    fn(s_in[i0:i0 + len(st.ins)], s_out[o0:o0 + len(st.inouts)],
                       s_out[o0 + len(st.inouts):o0 + len(st.inouts) + len(st.outs)],
                       lambda k, s0=s0: scr[n_scr].at[s0 + k])

        if not stages:
            core(*own_in, *own_out, *scr[:n_scr])
            return
        step = 0
        for d, g in enumerate(grid):
            step = step * g + pl.program_id(d)
        if steps == 1:
            run("start")
            core(*own_in, *own_out, *scr[:n_scr])
            run("mid")
            run("finish")
            return
        pl.when(step == 0)(lambda: run("start"))
        core(*own_in, *own_out, *scr[:n_scr])
        pl.when(step == (3 * steps) // 4)(lambda: run("mid"))
        pl.when(step == steps - 1)(lambda: run("finish"))

    sem = ("arbitrary",) * len(grid) if stages else ("parallel",) * max(len(grid) - 1, 0) + ("arbitrary",) * min(len(grid), 1)
    res = pl.pallas_call(
        body, name=name, grid=grid,
        in_specs=list(in_specs) + [ANY] * len(s_args),
        out_specs=list(out_specs) + [ANY] * len(s_outs),
        out_shape=list(out_shape) + s_outs,
        input_output_aliases=aliases,
        scratch_shapes=list(scratch_shapes) + ([pltpu.SemaphoreType.DMA((n_sems,))] if stages else []),
        compiler_params=_params(sem) if grid else pltpu.CompilerParams(vmem_limit_bytes=V7X_VMEM_LIMIT),
    )(*args, *s_args)
    return list(res[:n_out]), list(res[n_out:])


class _Pipe:
    def __init__(self):
        self.ready = []
        self.flushes = 0
        self.after = None

    def add(self, stage):
        self.ready.append(stage)

    def carry(self, call, long=True):
        stages = [st for st in self.ready if long or not st.slow]
        self.ready = [st for st in self.ready if not (long or not st.slow)]
        own, outs = call(stages)
        k = 0
        for st in stages:
            n = len(st.inouts) + len(st.outs)
            st.then(*outs[k:k + n])
            k += n
        if self.after is not None:
            self.after()
        return own

    def flush(self):
        while self.ready:
            self.flushes += 1
            self.carry(lambda stages: _staged_call(
                lambda *refs: None, name=f"comm_tail_{self.flushes}", grid=(), in_specs=[], out_specs=[], out_shape=[],
                scratch_shapes=[], args=[], stages=stages))


def _mixer_fwd(layer, x, g1, bgate, lng, lnb, wm, bsf, wsc, win_g, wb_g, wout_g, stages):
    t_len = x.shape[0]
    tm = min(TM_MIX, t_len)
    nt = t_len // tm
    nb = tm // GMLP_BLOCK

    def core(x_ref, x_late_ref, g1_ref, bgate_ref, lng_ref, lnb_ref, wm_ref, bsf_ref, wsc_ref, win_hbm, wb_hbm, wout_hbm,
             zc_ref, ya_ref, yb_ref, q_ref, sa_ref, ca_ref, sb_ref, cb_ref, ug_ref, fu_ref, xh_ref, cv_ref,
             mg_ref, h_ref, x2_ref,
             win_v, wb_v, wout_v, carry, vn_s, f_s, z_s, sems):
        i = pl.program_id(0)

        @pl.when(i == 0)
        def _():
            cps = (_load_col_sharded(win_hbm, win_v, sems, 0) + _load_branch(wb_hbm, wb_v, sems, 4)
                   + _load_row_sharded(wout_hbm, wout_v, sems, 12))
            _start_all(cps)
            carry[...] = jnp.zeros_like(carry)
            z_s[...] = jnp.zeros_like(z_s)
            _wait_all(cps)

        xv = x_ref[...]
        r = lax.rsqrt(jnp.mean(xv * xv, axis=-1, keepdims=True) + RMS_EPS)
        h_ref[...] = (xv * r * g1_ref[...]).astype(BF16)

        def zcols(c0, n, keep=None):
            zv = z_s[:, c0:c0 + n]
            z_s[:, c0:c0 + n] = _dot(h_ref[...], win_v[:, c0:c0 + n])
            if keep is not None:
                zc_ref[:, keep * D_B:(keep + 1) * D_B] = zv.astype(BF16)
            return zv

        v = zcols(C_V, D_A)
        vg, tv = _gelu(v)
        mu = jnp.mean(vg, axis=-1, keepdims=True)
        vc = vg - mu
        rstd = lax.rsqrt(jnp.mean(vc * vc, axis=-1, keepdims=True) + LN_EPS)
        xh = vc * rstd
        xh_ref[...] = xh.astype(BF16)
        cv_ref[...] = (rstd * _gelu_grad(v, tv)).astype(BF16)
        vn_s[...] = (xh * lng_ref[...] + lnb_ref[...]).astype(BF16)
        for hd in range(A_HEADS):
            cols = slice(hd * 128, (hd + 1) * 128)
            vcat = jnp.concatenate([vn_s[b * 128:(b + 1) * 128, cols] for b in range(nb)], axis=1)
            fcat = _dot(wm_ref[hd], vcat)
            for b in range(nb):
                f_s[b * 128:(b + 1) * 128, cols] = fcat[:, b * 128:(b + 1) * 128]
        u = zcols(C_U, D_A)
        ug, tu = _gelu(u)
        ug_ref[...] = ug.astype(BF16)
        fb = f_s[...] + jnp.concatenate([bsf_ref[...]] * nb, axis=0)
        fu_ref[...] = (fb * _gelu_grad(u, tu)).astype(BF16)
        ya_ref[...] = (ug * fb).astype(BF16)

        p = zcols(C_CG, D_B, keep=1) * zcols(C_HB, D_B, keep=2)
        cr = carry[...]
        q = wsc_ref[0:1, :] * _shift_down(p, cr, 2) + wsc_ref[1:2, :] * _shift_down(p, cr, 1) + wsc_ref[2:3, :] * p
        carry[...] = p[tm - 8:tm, :]
        q_ref[...] = q.astype(BF16)
        yb_ref[...] = (zcols(C_BG, D_B, keep=0) * q).astype(BF16)

        av = _dot(ya_ref[...], wb_v[0])
        sa = _sigmoid(zcols(C_GA, D_MODEL) + bgate_ref[:, 0:D_MODEL])
        sa_ref[...] = sa.astype(BF16)
        mg = sa * av
        ca_ref[...] = (mg * (1.0 - sa)).astype(BF16)
        bv = _dot(yb_ref[...], wb_v[1])
        sb = _sigmoid(zcols(C_GB, D_MODEL) + bgate_ref[:, D_MODEL:2 * D_MODEL])
        sb_ref[...] = sb.astype(BF16)
        mb = sb * bv
        cb_ref[...] = (mb * (1.0 - sb)).astype(BF16)
        mg_ref[...] = (mg + mb).astype(BF16)
        x2_ref[...] = x_late_ref[...] + _dot(mg_ref[...], wout_v[...])

    def tile(n, lag):
        return pl.BlockSpec((tm, n), lambda i: (jnp.clip(i - lag, 0, nt - 1), 0))

    outs = [
        jax.ShapeDtypeStruct((t_len, 3 * D_B), BF16),
        jax.ShapeDtypeStruct((t_len, D_A), BF16),
        jax.ShapeDtypeStruct((t_len, D_B), BF16),
        jax.ShapeDtypeStruct((t_len, D_B), BF16),
        jax.ShapeDtypeStruct((t_len, D_MODEL), BF16),
        jax.ShapeDtypeStruct((t_len, D_MODEL), BF16),
        jax.ShapeDtypeStruct((t_len, D_MODEL), BF16),
        jax.ShapeDtypeStruct((t_len, D_MODEL), BF16),
        jax.ShapeDtypeStruct((t_len, D_A), BF16),
        jax.ShapeDtypeStruct((t_len, D_A), BF16),
        jax.ShapeDtypeStruct((t_len, D_A), BF16),
        jax.ShapeDtypeStruct((t_len, D_A), BF16),
        jax.ShapeDtypeStruct((t_len, D_MODEL), BF16),
        jax.ShapeDtypeStruct((t_len, D_MODEL), BF16),
        jax.ShapeDtypeStruct((t_len, D_MODEL), F32),
    ]
    return _staged_call(
        core, name=f"mixer_fwd_l{layer}", grid=(nt + 1,),
        in_specs=[tile(D_MODEL, 0), tile(D_MODEL, 1), _const_spec((1, D_MODEL)), _const_spec((1, 2 * D_MODEL)),
                  _const_spec((1, D_A)), _const_spec((1, D_A)), _const_spec((A_HEADS, 128, 128)),
                  _const_spec((128, D_A)), _const_spec((8, D_B)), ANY, ANY, ANY],
        out_specs=[tile(o.shape[1], 0 if k == len(outs) - 2 else 1) for k, o in enumerate(outs)],
        out_shape=outs,
        scratch_shapes=[pltpu.VMEM((D_MODEL, D_IN), BF16), pltpu.VMEM((2, D_A, D_MODEL), BF16),
                        pltpu.VMEM((D_MODEL, D_MODEL), BF16), pltpu.VMEM((8, D_B), F32),
                        pltpu.VMEM((tm, D_A), BF16), pltpu.VMEM((tm, D_A), F32), pltpu.VMEM((tm, D_IN), F32),
                        pltpu.SemaphoreType.DMA((16,))],
        args=[x, x, g1, bgate, lng, lnb, wm, bsf, wsc, win_g, wb_g, wout_g], stages=stages)


def _ffn_fwd(layer, x2, g2, wfc, bfc, wup_g, wdown_g, stages, head=None):
    t_len = x2.shape[0]
    tm = min(TM_FFN, t_len)
    nt = t_len // tm

    def core(*refs):
        if head is None:
            (x_ref, g2_ref, wfc_ref, bfc_ref, wup_hbm, wdown_hbm, up_ref, silu_ref, dsilu_ref, act_ref, h_ref, x3_ref,
             wup_v, wdown_v, carry, sems) = refs
        else:
            (x_ref, g2_ref, wfc_ref, bfc_ref, t_ref, gf_ref, wup_hbm, wdown_hbm, up_ref, silu_ref, dsilu_ref, act_ref,
             h_ref, dx_ref, dgf_ref, loss_ref, wup_v, wdown_v, carry, sems) = refs
        i = pl.program_id(0)

        @pl.when(i == 0)
        def _():
            cps = _load_col_sharded(wup_hbm, wup_v, sems, 0) + _load_row_sharded(wdown_hbm, wdown_v, sems, 4)
            _start_all(cps)
            carry[...] = jnp.zeros_like(carry)
            if head is not None:
                dgf_ref[...] = jnp.zeros_like(dgf_ref)
                loss_ref[...] = jnp.zeros_like(loss_ref)
            _wait_all(cps)

        xv = x_ref[...]
        r = lax.rsqrt(jnp.mean(xv * xv, axis=-1, keepdims=True) + RMS_EPS)
        h_ref[...] = (xv * r * g2_ref[...]).astype(BF16)
        gate = _dot(h_ref[...], wup_v[:, 0:D_FF])
        up_ref[:, 0:D_FF] = gate.astype(BF16)
        cr = carry[...]
        gc = (wfc_ref[0:1, :] * _shift_down(gate, cr, 2) + wfc_ref[1:2, :] * _shift_down(gate, cr, 1)
              + wfc_ref[2:3, :] * gate + bfc_ref[...])
        carry[...] = gate[tm - 8:tm, :]
        sg = _sigmoid(gc)
        silu = gc * sg
        silu_ref[...] = silu.astype(BF16)
        dsilu_ref[...] = (sg + silu * (1.0 - sg)).astype(BF16)
        val = _dot(h_ref[...], wup_v[:, D_FF:2 * D_FF])
        up_ref[:, D_FF:2 * D_FF] = val.astype(BF16)
        act_ref[...] = (silu * val).astype(BF16)
        x3 = x_ref[...] + _dot(act_ref[...], wdown_v[...])
        if head is None:
            x3_ref[...] = x3
        else:
            r3 = lax.rsqrt(jnp.mean(x3 * x3, axis=-1, keepdims=True) + RMS_EPS)
            xh = x3 * r3
            err = xh * gf_ref[...] - t_ref[...]
            loss_ref[...] += _colsum8(err * err)
            dy = err * (1.0 / D_MODEL)
            dgf_ref[...] += _colsum8(dy * xh)
            dxh = dy * gf_ref[...]
            dx_ref[...] = r3 * (dxh - xh * jnp.mean(dxh * xh, axis=-1, keepdims=True))

    outs = [
        jax.ShapeDtypeStruct((t_len, 2 * D_FF), BF16),
        jax.ShapeDtypeStruct((t_len, D_FF), BF16),
        jax.ShapeDtypeStruct((t_len, D_FF), BF16),
        jax.ShapeDtypeStruct((t_len, D_FF), BF16),
        jax.ShapeDtypeStruct((t_len, D_MODEL), BF16),
        jax.ShapeDtypeStruct((t_len, D_MODEL), F32),
    ]
    in_specs = [_row_spec(tm, D_MODEL), _const_spec((1, D_MODEL)), _const_spec((8, D_FF)), _const_spec((1, D_FF))]
    out_specs = [_row_spec(tm, o.shape[1]) for o in outs]
    args = [x2, g2, wfc, bfc]
    if head is not None:
        in_specs += [_row_spec(tm, D_MODEL), _const_spec((1, D_MODEL))]
        args += list(head)
        outs += [jax.ShapeDtypeStruct((8, D_MODEL), F32)] * 2
        out_specs += [_const_spec((8, D_MODEL))] * 2
    return _staged_call(
        core, name=f"ffn_fwd_l{layer}", grid=(nt,),
        in_specs=in_specs + [ANY, ANY], out_specs=out_specs, out_shape=outs,
        scratch_shapes=[pltpu.VMEM((D_MODEL, 2 * D_FF), BF16), pltpu.VMEM((D_FF, D_MODEL), BF16),
                        pltpu.VMEM((8, D_FF), F32), pltpu.SemaphoreType.DMA((8,))],
        args=args + [wup_g, wdown_g], stages=stages)


def _ffn_bwd(layer, dx3, x2, up, silu, dsilu, g2, wfc, wup_g, wdown_g):
    t_len = x2.shape[0]
    tm = min(TM_FFN, t_len)
    nt = t_len // tm

    def core(dx3_ref, dx3_late_ref, x_ref, up_ref, silu_ref, dsilu_ref, g2_ref, wfc_ref, wup_hbm, wdown_hbm,
             dx2_ref, dup_ref, dx3b_ref, dg2_ref, dbfc_ref, dwfc_ref,
             wup_v, wdown_v, carry, da_s, dup_s, sems):
        i = pl.program_id(0)

        @pl.when(i == 0)
        def _():
            cps = _load_col_sharded(wup_hbm, wup_v, sems, 0) + _load_row_sharded(wdown_hbm, wdown_v, sems, 4)
            _start_all(cps)
            for ref in (carry, da_s, dup_s, dg2_ref, dbfc_ref, dwfc_ref):
                ref[...] = jnp.zeros_like(ref)
            _wait_all(cps)

        live = (i <= nt).astype(F32)
        dx3b_ref[...] = dx3_ref[...].astype(BF16)
        dh = jnp.zeros((tm, D_MODEL), F32)
        for c0, c1 in FF_CHUNKS:
            v0, v1 = D_FF + c0, D_FF + c1
            dh = dh + _dot_nt(dup_s[:, c0:c1], wup_v[:, c0:c1]) + _dot_nt(dup_s[:, v0:v1], wup_v[:, v0:v1])
            da = da_s[:, c0:c1]
            dval = (da * silu_ref[:, c0:c1].astype(F32)).astype(BF16)
            dup_ref[:, v0:v1] = dval
            dup_s[:, v0:v1] = dval
            dgc = da * up_ref[:, v0:v1].astype(F32) * dsilu_ref[:, c0:c1].astype(F32)
            cr = carry[:, c0:c1]
            dgc1 = _shift_up(dgc, cr, 1)
            dgc2 = _shift_up(dgc, cr, 2)
            carry[:, c0:c1] = jnp.where(i < nt, dgc[0:8, :], cr)
            gate = up_ref[:, c0:c1].astype(F32)
            dbfc_ref[:, c0:c1] += live * _colsum8(dgc)
            dwfc_ref[0, :, c0:c1] += live * _colsum8(dgc2 * gate)
            dwfc_ref[1, :, c0:c1] += live * _colsum8(dgc1 * gate)
            dwfc_ref[2, :, c0:c1] += live * _colsum8(dgc * gate)
            dgate = (wfc_ref[2:3, c0:c1] * dgc + wfc_ref[1:2, c0:c1] * dgc1 + wfc_ref[0:1, c0:c1] * dgc2).astype(BF16)
            dup_ref[:, c0:c1] = dgate
            dup_s[:, c0:c1] = dgate
            da_s[:, c0:c1] = _dot_nt(dx3b_ref[...], wdown_v[c0:c1, :])
        xv = x_ref[...]
        r = lax.rsqrt(jnp.mean(xv * xv, axis=-1, keepdims=True) + RMS_EPS)
        xh = xv * r
        dg2_ref[...] += _colsum8(dh * xh)
        dxh = dh * g2_ref[...]
        dx2_ref[...] = dx3_late_ref[...] + r * (dxh - xh * jnp.mean(dxh * xh, axis=-1, keepdims=True))

    def tile(n, lag):
        return pl.BlockSpec((tm, n), lambda i: (nt - 1 - jnp.clip(i - lag, 0, nt - 1), 0))

    outs = [
        jax.ShapeDtypeStruct((t_len, D_MODEL), F32),
        jax.ShapeDtypeStruct((t_len, 2 * D_FF), BF16),
        jax.ShapeDtypeStruct((t_len, D_MODEL), BF16),
        jax.ShapeDtypeStruct((8, D_MODEL), F32),
        jax.ShapeDtypeStruct((8, D_FF), F32),
        jax.ShapeDtypeStruct((3, 8, D_FF), F32),
    ]
    return _staged_call(
        core, name=f"ffn_bwd_l{layer}", grid=(nt + 2,),
        in_specs=[tile(D_MODEL, 0), tile(D_MODEL, 2), tile(D_MODEL, 2), tile(2 * D_FF, 1), tile(D_FF, 1), tile(D_FF, 1),
                  _const_spec((1, D_MODEL)), _const_spec((8, D_FF)), ANY, ANY],
        out_specs=[tile(D_MODEL, 2), tile(2 * D_FF, 1), tile(D_MODEL, 0),
                   _const_spec((8, D_MODEL)), _const_spec((8, D_FF)), _const_spec((3, 8, D_FF))],
        out_shape=outs,
        scratch_shapes=[pltpu.VMEM((D_MODEL, 2 * D_FF), BF16), pltpu.VMEM((D_FF, D_MODEL), BF16),
                        pltpu.VMEM((8, D_FF), F32), pltpu.VMEM((tm, D_FF), F32), pltpu.VMEM((tm, 2 * D_FF), BF16),
                        pltpu.SemaphoreType.DMA((8,))],
        args=[dx3, dx3, x2, up, silu, dsilu, g2, wfc, wup_g, wdown_g], stages=[])[0]


def _mixer_bwd(layer, dx2, x, zc, qs, sa, ca, sb, cb, ug, fu, xhs, cv, g1, lng, lnb, wmt, wsc, win_g, wb_g, wout_g):
    t_len = x.shape[0]
    tm = min(TM_MIX, t_len)
    nt = t_len // tm
    nb = tm // GMLP_BLOCK

    def core(dx2_ref, x_ref, zc_ref, q_ref, sa_ref, ca_ref, sb_ref, cb_ref, ug_ref, fu_ref, xh_ref, cv_ref,
             g1_ref, lng_ref, lnb_ref, wmt_ref, wsc_ref, win_hbm, wb_hbm, wout_hbm,
             dx_ref, dz_ref, da_ref, db_ref, dx2b_ref, dg1_ref, dbgate_ref, dlng_ref, dlnb_ref, dwm_ref, dbsf_ref, dwsc_ref,
             win_v, wb_v, wout_v, carry, vn_s, df_s, dvn_s, sems):
        i = pl.program_id(0)

        @pl.when(i == 0)
        def _():
            cps = (_load_col_sharded(win_hbm, win_v, sems, 0) + _load_branch(wb_hbm, wb_v, sems, 4)
                   + _load_row_sharded(wout_hbm, wout_v, sems, 12))
            _start_all(cps)
            for ref in (carry, dg1_ref, dbgate_ref, dlng_ref, dlnb_ref, dwm_ref, dbsf_ref, dwsc_ref):
                ref[...] = jnp.zeros_like(ref)
            _wait_all(cps)

        def kept(k):
            return zc_ref[:, k * D_B:(k + 1) * D_B].astype(F32)

        def dz_cols(c0, n, val):
            dz_ref[:, c0:c0 + n] = val.astype(BF16)
            return _dot_nt(dz_ref[:, c0:c0 + n], win_v[:, c0:c0 + n])

        dx2b_ref[...] = dx2_ref[...].astype(BF16)
        dm = _dot_nt(dx2b_ref[...], wout_v[...])
        da_ref[...] = (dm * sa_ref[...].astype(F32)).astype(BF16)
        dga = dm * ca_ref[...].astype(F32)
        dh = dz_cols(C_GA, D_MODEL, dga)
        dbgate_ref[:, 0:D_MODEL] += _colsum8(dga)
        dya = _dot_nt(da_ref[...], wb_v[0])
        db_ref[...] = (dm * sb_ref[...].astype(F32)).astype(BF16)
        dgb = dm * cb_ref[...].astype(F32)
        dh = dh + dz_cols(C_GB, D_MODEL, dgb)
        dbgate_ref[:, D_MODEL:2 * D_MODEL] += _colsum8(dgb)
        dyb = _dot_nt(db_ref[...], wb_v[1])

        xh = xh_ref[...].astype(F32)
        vn_s[...] = (xh * lng_ref[...] + lnb_ref[...]).astype(BF16)
        df = dya * ug_ref[...].astype(F32)
        df_s[...] = df.astype(BF16)
        dbsf_acc = df[0:128, :]
        for b in range(1, nb):
            dbsf_acc = dbsf_acc + df[b * 128:(b + 1) * 128, :]
        dbsf_ref[...] += dbsf_acc
        for hd in range(A_HEADS):
            cols = slice(hd * 128, (hd + 1) * 128)
            vcat = jnp.concatenate([vn_s[b * 128:(b + 1) * 128, cols] for b in range(nb)], axis=1)
            dcat = jnp.concatenate([df_s[b * 128:(b + 1) * 128, cols] for b in range(nb)], axis=1)
            gcat = _dot(wmt_ref[hd], dcat)
            dwm_ref[hd] += _dot_nt(dcat, vcat)
            for b in range(nb):
                dvn_s[b * 128:(b + 1) * 128, cols] = gcat[:, b * 128:(b + 1) * 128]
        dh = dh + dz_cols(C_U, D_A, dya * fu_ref[...].astype(F32))
        dvn = dvn_s[...]
        dlng_ref[...] += _colsum8(dvn * xh)
        dlnb_ref[...] += _colsum8(dvn)
        dxh = dvn * lng_ref[...]
        dvc = dxh - jnp.mean(dxh, axis=-1, keepdims=True) - xh * jnp.mean(dxh * xh, axis=-1, keepdims=True)
        dh = dh + dz_cols(C_V, D_A, dvc * cv_ref[...].astype(F32))

        cg = kept(1)
        hbv = kept(2)
        p = cg * hbv
        dh = dh + dz_cols(C_BG, D_B, dyb * q_ref[...].astype(F32))
        dq = dyb * kept(0)
        cr = carry[...]
        dq1 = _shift_up(dq, cr, 1)
        dq2 = _shift_up(dq, cr, 2)
        carry[...] = dq[0:8, :]
        dwsc_ref[0] += _colsum8(dq2 * p)
        dwsc_ref[1] += _colsum8(dq1 * p)
        dwsc_ref[2] += _colsum8(dq * p)
        dp = wsc_ref[2:3, :] * dq + wsc_ref[1:2, :] * dq1 + wsc_ref[0:1, :] * dq2
        dh = dh + dz_cols(C_CG, D_B, dp * hbv)
        dh = dh + dz_cols(C_HB, D_B, dp * cg)

        xv = x_ref[...]
        r = lax.rsqrt(jnp.mean(xv * xv, axis=-1, keepdims=True) + RMS_EPS)
        xn = xv * r
        dg1_ref[...] += _colsum8(dh * xn)
        dxn = dh * g1_ref[...]
        dx_ref[...] = dx2_ref[...] + r * (dxn - xn * jnp.mean(dxn * xn, axis=-1, keepdims=True))

    outs = [
        jax.ShapeDtypeStruct((t_len, D_MODEL), F32),
        jax.ShapeDtypeStruct((t_len, D_IN), BF16),
        jax.ShapeDtypeStruct((t_len, D_MODEL), BF16),
        jax.ShapeDtypeStruct((t_len, D_MODEL), BF16),
        jax.ShapeDtypeStruct((t_len, D_MODEL), BF16),
        jax.ShapeDtypeStruct((8, D_MODEL), F32),
        jax.ShapeDtypeStruct((8, 2 * D_MODEL), F32),
        jax.ShapeDtypeStruct((8, D_A), F32),
        jax.ShapeDtypeStruct((8, D_A), F32),
        jax.ShapeDtypeStruct((A_HEADS, 128, 128), F32),
        jax.ShapeDtypeStruct((128, D_A), F32),
        jax.ShapeDtypeStruct((3, 8, D_B), F32),
    ]

    return _staged_call(
        core, name=f"mixer_bwd_l{layer}", grid=(nt,),
        in_specs=[_row_spec(tm, D_MODEL, nt), _row_spec(tm, D_MODEL, nt), _row_spec(tm, 3 * D_B, nt),
                  _row_spec(tm, D_B, nt), _row_spec(tm, D_MODEL, nt), _row_spec(tm, D_MODEL, nt),
                  _row_spec(tm, D_MODEL, nt), _row_spec(tm, D_MODEL, nt), _row_spec(tm, D_A, nt), _row_spec(tm, D_A, nt),
                  _row_spec(tm, D_A, nt), _row_spec(tm, D_A, nt),
                  _const_spec((1, D_MODEL)), _const_spec((1, D_A)), _const_spec((1, D_A)),
                  _const_spec((A_HEADS, 128, 128)), _const_spec((8, D_B)), ANY, ANY, ANY],
        out_specs=[_row_spec(tm, D_MODEL, nt), _row_spec(tm, D_IN, nt), _row_spec(tm, D_MODEL, nt),
                   _row_spec(tm, D_MODEL, nt), _row_spec(tm, D_MODEL, nt),
                   _const_spec((8, D_MODEL)), _const_spec((8, 2 * D_MODEL)), _const_spec((8, D_A)), _const_spec((8, D_A)),
                   _const_spec((A_HEADS, 128, 128)), _const_spec((128, D_A)), _const_spec((3, 8, D_B))],
        out_shape=outs,
        scratch_shapes=[pltpu.VMEM((D_MODEL, D_IN), BF16), pltpu.VMEM((2, D_A, D_MODEL), BF16),
                        pltpu.VMEM((D_MODEL, D_MODEL), BF16), pltpu.VMEM((8, D_B), F32),
                        pltpu.VMEM((tm, D_A), BF16), pltpu.VMEM((tm, D_A), BF16), pltpu.VMEM((tm, D_A), F32),
                        pltpu.SemaphoreType.DMA((16,))],
        args=[dx2, x, zc, qs, sa, ca, sb, cb, ug, fu, xhs, cv, g1, lng, lnb, wmt, wsc, win_g, wb_g, wout_g],
        stages=[])[0]


def _wgrad(name, layer, a, b, rows, cols, row_blk, col_blk, stages):
    t_len, m = a.shape
    n = b.shape[1]
    tk = min(TK_WGRAD, t_len)
    col_sharded = n == N_CHIPS * cols
    grid = (m // row_blk, n // col_blk, t_len // tk)
    shards = col_blk // cols if col_sharded else 1

    if col_sharded:
        out_shape = (N_CHIPS, rows, cols)
        out_spec = pl.BlockSpec((shards, row_blk, cols), lambda i, j, k: (j, i, 0))
    else:
        out_shape = (N_CHIPS * rows, cols)
        out_spec = pl.BlockSpec((row_blk, col_blk), lambda i, j, k: (i, j))

    def core(a_ref, b_ref, o_ref):
        @pl.when(pl.program_id(2) == 0)
        def _():
            o_ref[...] = jnp.zeros_like(o_ref)

        g = _dot_tn(a_ref[...], b_ref[...])
        if col_sharded:
            for q in range(shards):
                o_ref[q] += g[:, q * cols:(q + 1) * cols]
        else:
            o_ref[...] += g

    own, outs = _staged_call(
        core, name=f"wgrad_{name}_l{layer}", grid=grid,
        in_specs=[pl.BlockSpec((tk, row_blk), lambda i, j, k: (k, i)), pl.BlockSpec((tk, col_blk), lambda i, j, k: (k, j))],
        out_specs=[out_spec], out_shape=[jax.ShapeDtypeStruct(out_shape, F32)], scratch_shapes=[],
        args=[a, b], stages=stages)
    return [own[0].reshape(N_CHIPS, rows, cols)], outs


def _wgrad_branch(layer, ya, da, yb, db, stages):
    t_len = ya.shape[0]
    tk = min(TK_WGRAD, t_len)

    cs = D_MODEL // N_CHIPS

    def core(ya_ref, da_ref, yb_ref, db_ref, o_ref):
        @pl.when(pl.program_id(0) == 0)
        def _():
            o_ref[...] = jnp.zeros_like(o_ref)

        ga = _dot_tn(ya_ref[...], da_ref[...])
        gb = _dot_tn(yb_ref[...], db_ref[...])
        for k in range(N_CHIPS):
            o_ref[k, 0:D_A, :] += ga[:, k * cs:(k + 1) * cs]
            o_ref[k, D_A:2 * D_A, :] += gb[:, k * cs:(k + 1) * cs]

    a_spec = pl.BlockSpec((tk, D_A), lambda k: (k, 0))
    d_spec = pl.BlockSpec((tk, D_MODEL), lambda k: (k, 0))
    return _staged_call(
        core, name=f"wgrad_w_branch_l{layer}", grid=(t_len // tk,),
        in_specs=[a_spec, d_spec, a_spec, d_spec],
        out_specs=[pl.BlockSpec((N_CHIPS, 2 * D_A, cs), lambda k: (0, 0, 0))],
        out_shape=[jax.ShapeDtypeStruct((N_CHIPS, 2 * D_A, cs), F32)], scratch_shapes=[],
        args=[ya, da, yb, db], stages=stages)


def _flat_blk(rows, cols):
    blk = rows
    while blk * cols * 4 > 2 * 1024 * 1024 and blk % 16 == 0:
        blk //= 2
    return blk


def _cast_into_slots(name, jobs, chip, stages):
    blks = [_flat_blk(w.shape[1], w.shape[2]) for w, _ in jobs]
    nblks = [w.shape[1] // b for (w, _), b in zip(jobs, blks)]
    n = len(jobs)
    out_shape = [jax.ShapeDtypeStruct((N_CHIPS,) + w.shape[1:], BF16) for w, _ in jobs]

    def core(*refs):
        for w_ref, o_ref in zip(refs[-2 * n:-n], refs[-n:]):
            o_ref[...] = w_ref[...].astype(BF16)

    def slot(*scalars):
        return scalars[0][0] if scalars else 2 * lax.axis_index("x") + lax.axis_index("y")

    in_specs = [pl.BlockSpec((None, b, w.shape[2]), lambda i, *s, la=la, k=k: (la, jnp.minimum(i, k - 1), 0))
                for (w, la), b, k in zip(jobs, blks, nblks)]
    out_specs = [pl.BlockSpec((None, b, w.shape[2]), lambda i, *s, k=k: (slot(*s), jnp.minimum(i, k - 1), 0))
                 for (w, _), b, k in zip(jobs, blks, nblks)]
    args = [w for w, _ in jobs]
    if stages:
        return _staged_call(core, name=f"cast_{name}", grid=(max(nblks),), in_specs=in_specs, out_specs=out_specs,
                            out_shape=out_shape, scratch_shapes=[], args=args, stages=stages)
    own = pl.pallas_call(
        core, name=f"cast_{name}",
        grid_spec=pltpu.PrefetchScalarGridSpec(num_scalar_prefetch=1, grid=(max(nblks),), in_specs=in_specs,
                                               out_specs=out_specs),
        out_shape=out_shape, compiler_params=_params(),
    )(chip, *args)
    return list(own), []


def _reduction_sums(name, jobs, pos):
    in_specs, out_specs, out_shape, args, bodies, counts = [], [], [], [], [], []
    for job in jobs:
        kind, grad, other = job[0], job[1], job[2]
        _, h, cols = other.shape
        blk = _flat_blk(h, cols)
        nblk = h // blk
        if kind == "pair":
            total = N_CHIPS * nblk

            def block(s, total=total, nblk=nblk):
                b = jnp.minimum(s, total - 1)
                return b // nblk, b % nblk

            spec = pl.BlockSpec((None, blk, cols), lambda s, p, block=block: (block(s)[0], block(s)[1], 0))
            in_specs += [pl.BlockSpec((None, blk, cols), lambda s, p, block=block, nblk=nblk:
                                      (block(s)[0], p[1] * nblk + block(s)[1], 0)), spec]
            out_specs.append(spec)
            out_shape.append(jax.ShapeDtypeStruct((N_CHIPS, h, cols), BF16))
            args += [grad, other]
            bodies.append((2, lambda g, o, out: out.__setitem__(..., (g[...] + o[...]).astype(BF16))))
        else:
            total = nblk

            def block(s, total=total):
                return jnp.minimum(s, total - 1)

            in_specs += [pl.BlockSpec((None, blk, cols), lambda s, p, block=block, nblk=nblk:
                                      (p[0], p[1] * nblk + block(s), 0)),
                         pl.BlockSpec((None, blk, cols), lambda s, p, block=block: (p[0], block(s), 0)),
                         pl.BlockSpec((3, blk, cols), lambda s, p, block=block: (0, block(s), 0))]
            out_specs.append(pl.BlockSpec((blk, cols), lambda s, p, block=block, nblk=nblk: (p[1] * nblk + block(s), 0)))
            out_shape.append(jax.ShapeDtypeStruct((2 * h, cols), F32))
            args += [grad, other, job[3]]
            bodies.append((3, lambda g, o, r, out: out.__setitem__(
                ..., (((g[...] + o[...]) + r[0].astype(F32)) + r[1].astype(F32)) + r[2].astype(F32))))
        counts.append(total)

    def body(pos_ref, *refs):
        ins, outs = refs[:len(args)], refs[len(args):]
        k = 0
        for (n_in, fn), out in zip(bodies, outs):
            fn(*ins[k:k + n_in], out)
            k += n_in

    return pl.pallas_call(
        body, name=f"reduction_sums_{name}",
        grid_spec=pltpu.PrefetchScalarGridSpec(num_scalar_prefetch=1, grid=(max(counts),), in_specs=in_specs,
                                               out_specs=out_specs),
        out_shape=out_shape,
        compiler_params=_params(),
    )(pos, *args)


def _sum_slots(name, slots):
    n, rows, _ = slots.shape

    def body(s_ref, o_ref):
        acc = s_ref[0]
        for d in range(1, n):
            acc = acc + s_ref[d]
        o_ref[...] = acc

    return pl.pallas_call(
        body, name=f"sum_slots_{name}", grid=(1,),
        in_specs=[pl.BlockSpec((n, rows, 128), lambda i: (0, 0, 0))],
        out_specs=pl.BlockSpec((rows, 128), lambda i: (0, 0)),
        out_shape=jax.ShapeDtypeStruct((rows, 128), F32),
        compiler_params=_params(),
    )(slots)


def _adamw_math(w, g, m, v):
    m2 = ADAM_B1 * m + (1.0 - ADAM_B1) * g
    v2 = ADAM_B2 * v + (1.0 - ADAM_B2) * (g * g)
    m_hat = m2 / (1.0 - ADAM_B1 ** ADAM_STEP)
    v_hat = v2 / (1.0 - ADAM_B2 ** ADAM_STEP)
    delta = -ADAM_LR * (m_hat / (jnp.sqrt(v_hat) + ADAM_EPS) + ADAM_WD * w)
    return delta, m2, v2


SC_TILES = 32
SC_CHUNK = 5632


def _adamw_sparsecore(name, w, g, m, v):
    n = w.shape[0]
    per_tile = n // SC_TILES
    n_chunks = per_tile // SC_CHUNK
    assert per_tile * SC_TILES == n and n_chunks * SC_CHUNK == per_tile

    def body(w_hbm, g_hbm, m_hbm, v_hbm, d_hbm, m2_hbm, v2_hbm, wb, gb, mb, vb, db):
        tile = lax.axis_index("subcore") * 2 + lax.axis_index("core")

        @pl.loop(0, n_chunks)
        def _(k):
            part = pl.ds(tile * per_tile + k * SC_CHUNK, SC_CHUNK)
            pltpu.sync_copy(w_hbm.at[part], wb)
            pltpu.sync_copy(g_hbm.at[part], gb)
            pltpu.sync_copy(m_hbm.at[part], mb)
            pltpu.sync_copy(v_hbm.at[part], vb)

            @pl.loop(0, SC_CHUNK, step=16)
            def _(i):
                s = pl.ds(i, 16)
                d, m2, v2 = _adamw_math(wb[s], gb[s], mb[s], vb[s])
                db[s] = d
                mb[s] = m2
                vb[s] = v2

            pltpu.sync_copy(db, d_hbm.at[part])
            pltpu.sync_copy(mb, m2_hbm.at[part])
            pltpu.sync_copy(vb, v2_hbm.at[part])

    return pl.kernel(
        body, name=f"adamw_sc_{name}",
        out_type=[jax.ShapeDtypeStruct(w.shape, F32)] * 3,
        mesh=plsc.VectorSubcoreMesh(core_axis_name="core", subcore_axis_name="subcore"),
        scratch_types=[pltpu.VMEM((SC_CHUNK,), F32)] * 5,
    )(w, g, m, v)


def _adamw_big(name, w, g0, g1, m, v):
    _, rows, cols = w.shape
    blk = _flat_blk(rows, cols) // 2

    def body(w_ref, g0_ref, g1_ref, m_ref, v_ref, g_ref, d_ref, m2_ref, v2_ref):
        g = jnp.where(pl.program_id(0) == 0, g0_ref[...], g1_ref[...])
        d, m2, v2 = _adamw_math(w_ref[...], g, m_ref[...], v_ref[...])
        g_ref[...] = g
        d_ref[...] = d
        m2_ref[...] = m2
        v2_ref[...] = v2

    spec = pl.BlockSpec((None, blk, cols), lambda la, i: (la, i, 0))
    return pl.pallas_call(
        body, name=f"adamw_{name}", grid=(N_LAYERS, rows // blk),
        in_specs=[spec, pl.BlockSpec((blk, cols), lambda la, i: (i * (1 - la), 0)),
                  pl.BlockSpec((blk, cols), lambda la, i: (i * la, 0)), spec, spec],
        out_specs=[spec] * 4,
        out_shape=[jax.ShapeDtypeStruct(w.shape, F32)] * 4,
        compiler_params=_params(("parallel", "parallel")),
    )(w, g0, g1, m, v)


def _adamw_small(ws, gs, ms, vs):
    n = len(ws)

    def body(*refs):
        ins, outs = refs[:4 * n], refs[4 * n:]
        for k in range(n):
            d, m2, v2 = _adamw_math(ins[k][...], ins[n + k][...], ins[2 * n + k][...], ins[3 * n + k][...])
            outs[k][...] = d
            outs[n + k][...] = m2
            outs[2 * n + k][...] = v2

    vmem = pl.BlockSpec(memory_space=pltpu.VMEM)
    return pl.pallas_call(
        body, name="adamw_small",
        in_specs=[vmem] * (4 * n), out_specs=[vmem] * (3 * n),
        out_shape=[jax.ShapeDtypeStruct(w.shape, F32) for w in ws] * 3,
        compiler_params=pltpu.CompilerParams(vmem_limit_bytes=V7X_VMEM_LIMIT),
    )(*ws, *gs, *ms, *vs)


SMALL = ("norm1_g", "b_gate", "gmlp_ln_g", "gmlp_ln_b", "w_spatial", "b_spatial", "w_shortconv", "norm2_g",
         "w_ffn_conv", "b_ffn_conv", "final_g")
ALL_WEIGHTS = ("norm1_g", "w_in", "b_gate", "gmlp_ln_g", "gmlp_ln_b", "w_spatial", "b_spatial", "w_shortconv",
               "w_branch", "w_out", "norm2_g", "w_ffn_up", "w_ffn_conv", "b_ffn_conv", "w_ffn_down", "final_g")


def _pack(arrays):
    flat = jnp.concatenate([a.reshape(-1) for a in arrays])
    n = flat.shape[0]
    rows = -(-n // 1024) * 8
    return jnp.pad(flat, (0, rows * 128 - n)).reshape(rows, 128)


def _unpack(packed, like):
    flat = packed.reshape(-1)
    out, off = [], 0
    for a in like:
        out.append(flat[off:off + a.size].reshape(a.shape))
        off += a.size
    return out


def _pad8(w):
    return jnp.pad(w, ((0, 5), (0, 0)))


def kernel(x, norm1_g, w_in, b_gate, gmlp_ln_g, gmlp_ln_b, w_spatial, b_spatial, w_shortconv, w_branch, w_out, norm2_g, w_ffn_up, w_ffn_conv, b_ffn_conv, w_ffn_down, final_g, loss_target, m_norm1_g, m_w_in, m_b_gate, m_gmlp_ln_g, m_gmlp_ln_b, m_w_spatial, m_b_spatial, m_w_shortconv, m_w_branch, m_w_out, m_norm2_g, m_w_ffn_up, m_w_ffn_conv, m_b_ffn_conv, m_w_ffn_down, m_final_g, v_norm1_g, v_w_in, v_b_gate, v_gmlp_ln_g, v_gmlp_ln_b, v_w_spatial, v_b_spatial, v_w_shortconv, v_w_branch, v_w_out, v_norm2_g, v_w_ffn_up, v_w_ffn_conv, v_b_ffn_conv, v_w_ffn_down, v_final_g):
    weights = dict(norm1_g=norm1_g, w_in=w_in, b_gate=b_gate, gmlp_ln_g=gmlp_ln_g, gmlp_ln_b=gmlp_ln_b,
                   w_spatial=w_spatial, b_spatial=b_spatial, w_shortconv=w_shortconv, w_branch=w_branch, w_out=w_out,
                   norm2_g=norm2_g, w_ffn_up=w_ffn_up, w_ffn_conv=w_ffn_conv, b_ffn_conv=b_ffn_conv,
                   w_ffn_down=w_ffn_down, final_g=final_g)
    mom = dict(norm1_g=m_norm1_g, w_in=m_w_in, b_gate=m_b_gate, gmlp_ln_g=m_gmlp_ln_g, gmlp_ln_b=m_gmlp_ln_b,
               w_spatial=m_w_spatial, b_spatial=m_b_spatial, w_shortconv=m_w_shortconv, w_branch=m_w_branch,
               w_out=m_w_out, norm2_g=m_norm2_g, w_ffn_up=m_w_ffn_up, w_ffn_conv=m_w_ffn_conv,
               b_ffn_conv=m_b_ffn_conv, w_ffn_down=m_w_ffn_down, final_g=m_final_g)
    vel = dict(norm1_g=v_norm1_g, w_in=v_w_in, b_gate=v_b_gate, gmlp_ln_g=v_gmlp_ln_g, gmlp_ln_b=v_gmlp_ln_b,
               w_spatial=v_w_spatial, b_spatial=v_b_spatial, w_shortconv=v_w_shortconv, w_branch=v_w_branch,
               w_out=v_w_out, norm2_g=v_norm2_g, w_ffn_up=v_w_ffn_up, w_ffn_conv=v_w_ffn_conv,
               b_ffn_conv=v_b_ffn_conv, w_ffn_down=v_w_ffn_down, final_g=v_final_g)

    cx, cy, cc = _mesh_pos()
    chip = 2 * cx + cy
    pos_arr = jnp.stack([chip, cc]).astype(jnp.int32)
    t_len = x.shape[1]
    xs = x.reshape(t_len, D_MODEL)
    target = loss_target.reshape(t_len, D_MODEL)
    pipe = _Pipe()

    full = {}

    mixer_w = ("w_in", "w_branch", "w_out")
    ffn_w = ("w_ffn_up", "w_ffn_down")
    slots = {}

    def cast(name, keys, stages):
        own, outs = _cast_into_slots(name, [(weights[n].reshape((N_LAYERS,) + BIG[n]), la) for n, la in keys],
                                     chip.astype(jnp.int32).reshape(1), stages)
        slots.update(zip(keys, own))
        return own, outs

    def gather(names, la):
        def then(*bufs):
            full.update(zip([(n, la) for n in names], bufs))

        pipe.add(_gather_stage([slots[(n, la)] for n in names], then))

    first = [(n, 0) for n in mixer_w]
    cast("first", first, [])
    gather(mixer_w, 0)
    tap_slots = {}
    pipe.add(_chip_spread_stage(_pack([w_shortconv, w_ffn_conv]), lambda got: tap_slots.__setitem__("all", got)))
    pipe.carry(lambda st: cast("rest", [(n, la) for la in range(N_LAYERS) for n in BIG_NAMES if (n, la) not in first], st))
    by_chip = [_unpack(tap_slots["all"][k], [w_shortconv, w_ffn_conv]) for k in range(N_CHIPS)]
    wsc_full = jnp.concatenate([t[0] for t in by_chip], axis=-1)
    wfc_full = jnp.concatenate([t[1] for t in by_chip], axis=-1)

    idx = jnp.arange(GMLP_BLOCK) // CHUNK
    mask = idx[None, :] <= idx[:, None]
    wm_all = jnp.where(mask[None, None], w_spatial, 0.0)
    wm_bf = wm_all.astype(BF16)
    wmt_bf = jnp.swapaxes(wm_all, -1, -2).astype(BF16)
    bsf = jnp.repeat(jnp.swapaxes(b_spatial, -1, -2), 128, axis=-1)

    def row(a):
        return a.reshape(1, -1)

    def mixer_args(la):
        return (row(norm1_g[la]), row(b_gate[la]), row(gmlp_ln_g[la]), row(gmlp_ln_b[la]))

    def mixer_weights(la):
        return tuple(full[(n, la)] for n in mixer_w)

    def ffn_weights(la):
        return tuple(full[(n, la)] for n in ffn_w)

    saved = []
    h_in = xs
    for la in range(N_LAYERS):
        gather(ffn_w, la)
        *kept, mg, h1, x2 = pipe.carry(lambda st: _mixer_fwd(
            la, h_in, *mixer_args(la), wm_bf[la], bsf[la], _pad8(wsc_full[la]), *mixer_weights(la), st))
        ya, yb = kept[1], kept[2]
        if la + 1 < N_LAYERS:
            gather(mixer_w, la + 1)
        head = (target, row(final_g)) if la == N_LAYERS - 1 else None
        up, silu, dsilu, act, h2, *rest = pipe.carry(lambda st: _ffn_fwd(
            la, x2, row(norm2_g[la]), _pad8(wfc_full[la]), row(b_ffn_conv[la]), *ffn_weights(la), st, head=head))
        saved.append(dict(x=h_in, ya=ya, yb=yb, mixer=[kept[0]] + kept[3:], mg=mg, h1=h1, x2=x2, up=up, silu=silu,
                          dsilu=dsilu, act=act, h2=h2))
        h_in = rest[0]
    dx, dgf8, loss8 = rest

    reduced_big = {}

    sums_due = []

    def run_sums():
        if sums_due:
            due = list(sums_due)
            sums_due.clear()
            run_sums.calls += 1
            for (_, then), res in zip(due, _reduction_sums(str(run_sums.calls), [job for job, _ in due], pos_arr)):
                then(res)

    run_sums.calls = 0
    pipe.after = run_sums

    def reduce_big(name, la, grad):
        def after_pair(other):
            def after_chips(got):
                sums_due.append((("chip", grad, other, got), lambda final: pipe.add(_pair_fill_stage(
                    final, lambda done: reduced_big.__setitem__((name, la), done)))))

            sums_due.append((("pair", grad, other), lambda psum: pipe.add(_chip_send_stage(psum, after_chips))))

        pipe.add(_pair_send_stage(grad, after_pair))

    small = {n: [None] * N_LAYERS for n in SMALL}
    spread = {}
    wgrad_in = {}
    for la in reversed(range(N_LAYERS)):
        s = saved[la]
        dx3 = dx
        dx2, dup, dx3b, dg2, dbfc, dwfc = _ffn_bwd(
            la, dx3, s["x2"], s["up"], s["silu"], s["dsilu"], row(norm2_g[la]), _pad8(wfc_full[la]),
            *ffn_weights(la))
        dxl, dz, da, db, dx2b, dg1, dbg, dlng, dlnb, dwm, dbsf, dwsc = _mixer_bwd(
            la, dx2, s["x"], *s["mixer"], row(norm1_g[la]), row(gmlp_ln_g[la]), row(gmlp_ln_b[la]), wmt_bf[la],
            _pad8(wsc_full[la]), *mixer_weights(la))
        wgrad_in[la] = dict(s, dup=dup, dx3b=dx3b, dz=dz, da=da, db=db, dx2b=dx2b)
        small["norm1_g"][la] = dg1.sum(0)
        small["b_gate"][la] = dbg.sum(0)
        small["gmlp_ln_g"][la] = dlng.sum(0)
        small["gmlp_ln_b"][la] = dlnb.sum(0)
        small["w_spatial"][la] = jnp.where(mask[None], dwm, 0.0)
        small["b_spatial"][la] = dbsf.reshape(128, A_HEADS, 128).sum(-1).T
        small["w_shortconv"][la] = dwsc.sum(1)
        small["norm2_g"][la] = dg2.sum(0)
        small["w_ffn_conv"][la] = dwfc.sum(1)
        small["b_ffn_conv"][la] = dbfc.sum(0)
        dx = dxl
    grad_x = dx.reshape(x.shape)

    small_local = [jnp.stack(small[n]) for n in SMALL[:-1]] + [dgf8.sum(0), 0.5 * loss8.sum().reshape(1) / D_MODEL]
    mine = _pack(small_local)

    def after_swap(other):
        pair = _sum_slots("small_pair", jnp.stack([mine, other]))
        pipe.add(_chip_spread_stage(pair, lambda slots: spread.__setitem__("slots", slots)))

    pipe.add(_pair_swap_stage(mine, after_swap))

    def wgrad(name, la, st):
        w = wgrad_in[la]
        if name == "w_ffn_up":
            return _wgrad(name, la, w["h2"], w["dup"], 1024, 1408, 512, 2816, st)
        if name == "w_in":
            return _wgrad(name, la, w["h1"], w["dz"], 1024, 1152, 512, 2304, st)
        if name == "w_ffn_down":
            return _wgrad(name, la, w["act"], w["dx3b"], 704, 1024, 1408, 1024, st)
        if name == "w_out":
            return _wgrad(name, la, w["mg"], w["dx2b"], 256, 1024, 1024, 1024, st)
        return _wgrad_branch(la, w["ya"], w["da"], w["yb"], w["db"], st)

    for name in ("w_ffn_up", "w_in", "w_ffn_down", "w_out", "w_branch"):
        for la in reversed(range(N_LAYERS)):
            g, = pipe.carry(lambda st: wgrad(name, la, st), long=name not in ("w_out", "w_branch"))
            reduce_big(name, la, g)
    pipe.flush()

    reduced = _unpack(_sum_slots("small_grads", spread["slots"]), small_local)
    loss = reduced[-1].reshape(())
    grads = dict(zip(SMALL, reduced[:-1]))
    grads["w_shortconv"] = lax.dynamic_slice(grads["w_shortconv"], (0, 0, chip * (D_B // 4)), (N_LAYERS, 3, D_B // 4))
    grads["w_ffn_conv"] = lax.dynamic_slice(grads["w_ffn_conv"], (0, 0, chip * (D_FF // 4)), (N_LAYERS, 3, D_FF // 4))

    delta, new_m, new_v = {}, {}, {}
    for n in BIG_NAMES:
        shape3 = (N_LAYERS,) + BIG[n]
        if n == "w_ffn_down":
            g = jnp.concatenate([reduced_big[(n, 0)].reshape(-1), reduced_big[(n, 1)].reshape(-1)])
            res = (g,) + tuple(_adamw_sparsecore(n, weights[n].reshape(-1), g, mom[n].reshape(-1), vel[n].reshape(-1)))
            grads[n], delta[n], new_m[n], new_v[n] = (a.reshape(weights[n].shape) for a in res)
            continue
        res = _adamw_big(n, weights[n].reshape(shape3), reduced_big[(n, 0)], reduced_big[(n, 1)],
                         mom[n].reshape(shape3), vel[n].reshape(shape3))
        grads[n], delta[n], new_m[n], new_v[n] = (a.reshape(weights[n].shape) for a in res)
    res = _adamw_small(*[[src[n].reshape(-1, src[n].shape[-1]) for n in SMALL] for src in (weights, grads, mom, vel)])
    for k, n in enumerate(SMALL):
        delta[n], new_m[n], new_v[n] = (res[j * len(SMALL) + k].reshape(weights[n].shape) for j in range(3))

    return (loss, grad_x, *[grads[n] for n in ALL_WEIGHTS], *[delta[n] for n in ALL_WEIGHTS],
            *[new_m[n] for n in ALL_WEIGHTS], *[new_v[n] for n in ALL_WEIGHTS])
```

```python
import jax
import jax.numpy as jnp
from jax import lax
from jax.experimental import pallas as pl
from jax.experimental.pallas import tpu as pltpu
from jax.experimental.pallas import tpu_sc as plsc

F32 = jnp.float32
BF16 = jnp.bfloat16
MESH = pl.DeviceIdType.MESH
ANY = pl.BlockSpec(memory_space=pl.ANY)

D_MODEL = 1024
D_A = 512
D_B = 512
D_IN = 4608
D_FF = 2816
GMLP_BLOCK = 128
CHUNK = 64
A_HEADS = 4
N_LAYERS = 2
N_CHIPS = 4
RMS_EPS = 1e-6
LN_EPS = 1e-5
ADAM_LR = 0.001
ADAM_B1 = 0.9
ADAM_B2 = 0.999
ADAM_EPS = 1e-08
ADAM_WD = 0.01
ADAM_STEP = 10

C_U, C_V, C_BG, C_CG, C_HB, C_GA, C_GB = 0, 512, 1024, 1536, 2048, 2560, 3584

V7X_VMEM_LIMIT = 60 * 1024 * 1024
TM_MIX = 256
TM_FFN = 256
TK_WGRAD = 2048
SLOW_COPY_BYTES = 768 * 1024
FF_CHUNKS = ((0, 768), (768, 1536), (1536, 2304), (2304, 2816))
GELU_C0 = 0.7978845608028654
GELU_C1 = 0.044715

BIG = {
    "w_in": (1024, 1152),
    "w_branch": (1024, 256),
    "w_out": (256, 1024),
    "w_ffn_up": (1024, 1408),
    "w_ffn_down": (704, 1024),
}
BIG_NAMES = tuple(BIG)


def _params(sem=("arbitrary",), vmem=V7X_VMEM_LIMIT):
    return pltpu.CompilerParams(dimension_semantics=sem, vmem_limit_bytes=vmem)


def _gelu(x):
    x2 = x * x
    t = jnp.tanh(GELU_C0 * x * (1.0 + GELU_C1 * x2))
    return 0.5 * x * (1.0 + t), t


def _gelu_grad(x, t):
    return 0.5 * (1.0 + t) + 0.5 * x * (1.0 - t * t) * GELU_C0 * (1.0 + 3.0 * GELU_C1 * x * x)


def _colsum8(v):
    r, n = v.shape
    return v.reshape(r // 8, 8, n).sum(axis=0)


def _dot(a, b):
    return jnp.dot(a, b, preferred_element_type=F32)


def _dot_nt(a, b):
    return lax.dot_general(a, b, (((1,), (1,)), ((), ())), preferred_element_type=F32)


def _dot_tn(a, b):
    return lax.dot_general(a, b, (((0,), (0,)), ((), ())), preferred_element_type=F32)


def _shift_down(v, carry, n):
    rows = lax.broadcasted_iota(jnp.int32, (8, v.shape[1]), 0)
    out = pltpu.roll(v, n, 0)
    head = out[0:8, :]
    for r in range(n):
        head = jnp.where(rows == r, carry[8 - n + r:8 - n + r + 1, :], head)
    return jnp.concatenate([head, out[8:, :]], axis=0)


def _shift_up(v, carry, n):
    tm = v.shape[0]
    rows = lax.broadcasted_iota(jnp.int32, (8, v.shape[1]), 0)
    out = pltpu.roll(v, tm - n, 0)
    tail = out[tm - 8:tm, :]
    for r in range(n):
        tail = jnp.where(rows == 8 - n + r, carry[r:r + 1, :], tail)
    return jnp.concatenate([out[0:tm - 8, :], tail], axis=0)


def _sigmoid(x):
    return 0.5 * jnp.tanh(0.5 * x) + 0.5


def _start_all(copies):
    for cp in copies:
        cp.start()


def _wait_all(copies):
    for cp in copies:
        cp.wait()


def _load_col_sharded(src, dst, sems, first):
    cs = src.shape[-1]
    return [pltpu.make_async_copy(src.at[k], dst.at[:, k * cs:(k + 1) * cs], sems.at[first + k])
            for k in range(N_CHIPS)]


def _load_row_sharded(src, dst, sems, first):
    rs = src.shape[-2]
    return [pltpu.make_async_copy(src.at[k], dst.at[k * rs:(k + 1) * rs, :], sems.at[first + k])
            for k in range(N_CHIPS)]


def _load_branch(src, dst, sems, first):
    return [pltpu.make_async_copy(src.at[k, pl.ds(m * D_A, D_A), :], dst.at[m, :, k * 256:(k + 1) * 256],
                                  sems.at[first + 2 * k + m])
            for k in range(N_CHIPS) for m in range(2)]


def _row_spec(tm, n, rev=None):
    if rev is None:
        return pl.BlockSpec((tm, n), lambda i: (i, 0))
    return pl.BlockSpec((tm, n), lambda i: (rev - 1 - i, 0))


def _const_spec(shape):
    nd = len(shape)
    return pl.BlockSpec(shape, lambda i: (0,) * nd)


def _mesh_pos():
    return lax.axis_index("x"), lax.axis_index("y"), lax.axis_index("c")


def _other_chips(x, y):
    return [(1 - x, y, 2 * (1 - x) + y), (x, 1 - y, 2 * x + (1 - y)), (1 - x, 1 - y, 2 * (1 - x) + (1 - y))]


def _remote(src, dst, ssem, rsem, to):
    return pltpu.make_async_remote_copy(src_ref=src, dst_ref=dst, send_sem=ssem, recv_sem=rsem, device_id=to,
                                        device_id_type=MESH)


def _half(ref, which, h):
    start = pl.multiple_of(which * h, 8)
    if len(ref.shape) == 2:
        return ref.at[pl.ds(start, h), :]
    return ref.at[:, pl.ds(start, h), :]


class _Stage:
    def __init__(self, ins=(), inouts=(), outs=(), n_sems=0, start=None, mid=None, finish=None, then=None, slow=False):
        self.ins, self.inouts, self.outs = list(ins), list(inouts), list(outs)
        self.n_sems, self.start, self.mid, self.finish, self.then = n_sems, start, mid, finish, then
        self.slow = slow


def _gather_stage(bufs, then):
    n = len(bufs)

    def copies(io, sem):
        x, y, c = _mesh_pos()
        me = 2 * x + y
        ici, fwd, got = [], [], []
        for w in range(n):
            h = io[w].shape[1] // 2
            for j, (px, py, pk) in enumerate(_other_chips(x, y)):
                mine = _half(io[w].at[me], c, h)
                theirs = _half(io[w].at[pk], c, h)
                ici.append(_remote(mine, mine, sem(12 * w + j), sem(12 * w + 3 + j), (px, py, c)))
                got.append(_remote(theirs, theirs, sem(12 * w + j), sem(12 * w + 3 + j), (px, py, c)))
                fwd.append(_remote(theirs, theirs, sem(12 * w + 6 + j), sem(12 * w + 9 + j), (x, y, 1 - c)))
        return ici, got, fwd

    def start(ins, io, outs, sem):
        _start_all(copies(io, sem)[0])

    def mid(ins, io, outs, sem):
        _, got, fwd = copies(io, sem)
        for g, f in zip(got, fwd):
            g.wait_recv()
            f.start()

    def finish(ins, io, outs, sem):
        x, y, c = _mesh_pos()
        ici, _, fwd = copies(io, sem)
        for w in range(n):
            h = io[w].shape[1] // 2
            for j, (px, py, pk) in enumerate(_other_chips(x, y)):
                other = _half(io[w].at[pk], 1 - c, h)
                _remote(other, other, sem(12 * w + 6 + j), sem(12 * w + 9 + j), (x, y, 1 - c)).wait_recv()
        for cp in ici + fwd:
            cp.wait_send()

    return _Stage(inouts=bufs, n_sems=12 * n, start=start, mid=mid, finish=finish, then=then)


def _pair_send_stage(grad, then):
    h = grad.shape[1] // 2

    def copy(ins, outs, sem):
        x, y, c = _mesh_pos()
        return _remote(_half(ins[0], 1 - c, h), outs[0], sem(0), sem(1), (x, y, 1 - c))

    return _Stage(ins=[grad], outs=[jax.ShapeDtypeStruct((N_CHIPS, h, grad.shape[2]), F32)], n_sems=2,
                  start=lambda ins, io, outs, sem: copy(ins, outs, sem).start(),
                  finish=lambda ins, io, outs, sem: copy(ins, outs, sem).wait(), then=then)


def _chip_send_stage(psum, then):
    def copies(ins, outs, sem):
        x, y, c = _mesh_pos()
        return [_remote(ins[0].at[pk], outs[0].at[j], sem(j), sem(3 + j), (px, py, c))
                for j, (px, py, pk) in enumerate(_other_chips(x, y))]

    return _Stage(ins=[psum], outs=[jax.ShapeDtypeStruct((3,) + psum.shape[1:], BF16)], n_sems=6,
                  start=lambda ins, io, outs, sem: _start_all(copies(ins, outs, sem)),
                  finish=lambda ins, io, outs, sem: _wait_all(copies(ins, outs, sem)), then=then,
                  slow=psum.shape[1] * psum.shape[2] * 2 > SLOW_COPY_BYTES)


def _pair_fill_stage(final, then):
    h = final.shape[0] // 2

    def copy(io, sem):
        x, y, c = _mesh_pos()
        mine = _half(io[0], c, h)
        return _remote(mine, mine, sem(0), sem(1), (x, y, 1 - c))

    return _Stage(inouts=[final], n_sems=2,
                  start=lambda ins, io, outs, sem: copy(io, sem).start(),
                  finish=lambda ins, io, outs, sem: copy(io, sem).wait(), then=then)


def _pair_swap_stage(packed, then):
    def copy(ins, outs, sem):
        x, y, c = _mesh_pos()
        return _remote(ins[0], outs[0], sem(0), sem(1), (x, y, 1 - c))

    return _Stage(ins=[packed], outs=[jax.ShapeDtypeStruct(packed.shape, F32)], n_sems=2,
                  start=lambda ins, io, outs, sem: copy(ins, outs, sem).start(),
                  finish=lambda ins, io, outs, sem: copy(ins, outs, sem).wait(), then=then)


def _chip_spread_stage(psum, then):
    def copies(ins, outs, sem):
        x, y, c = _mesh_pos()
        me = 2 * x + y
        cps = [_remote(ins[0], outs[0].at[me], sem(j), sem(3 + j), (px, py, c))
               for j, (px, py, pk) in enumerate(_other_chips(x, y))]
        return cps, pltpu.make_async_copy(ins[0], outs[0].at[me], sem(6))

    def start(ins, io, outs, sem):
        cps, own = copies(ins, outs, sem)
        own.start()
        _start_all(cps)

    def finish(ins, io, outs, sem):
        cps, own = copies(ins, outs, sem)
        _wait_all(cps)
        own.wait()

    return _Stage(ins=[psum], outs=[jax.ShapeDtypeStruct((N_CHIPS,) + psum.shape, F32)], n_sems=7,
                  start=start, finish=finish, then=then)


def _staged_call(core, *, name, grid, in_specs, out_specs, out_shape, scratch_shapes, args, stages):
    n_in, n_out, n_scr = len(args), len(out_shape), len(scratch_shapes)
    s_args, s_outs, aliases, layout = [], [], {}, []
    n_sems = 0
    for st in stages:
        i0, o0 = len(s_args), len(s_outs)
        s_args += st.ins + st.inouts
        for q in range(len(st.inouts)):
            aliases[n_in + i0 + len(st.ins) + q] = n_out + o0 + q
        s_outs += [jax.ShapeDtypeStruct(a.shape, a.dtype) for a in st.inouts] + st.outs
        layout.append((i0, o0, n_sems))
        n_sems += st.n_sems
    steps = 1
    for g in grid:
        steps *= g

    def body(*refs):
        own_in = refs[:n_in]
        s_in = refs[n_in:n_in + len(s_args)]
        rest = refs[n_in + len(s_args):]
        own_out = rest[:n_out]
        s_out = rest[n_out:n_out + len(s_outs)]
        scr = rest[n_out + len(s_outs):]

        def run(which):
            for st, (i0, o0, s0) in zip(stages, layout):
                fn = getattr(st, which)
                if fn is not None:
                    fn(s_in[i0:i0 + len(st.ins)], s_out[o0:o0 + len(st.inouts)],
                       s_out[o0 + len(st.inouts):o0 + len(st.inouts) + len(st.outs)],
                       lambda k, s0=s0: scr[n_scr].at[s0 + k])

        if not stages:
            core(*own_in, *own_out, *scr[:n_scr])
            return
        step = 0
        for d, g in enumerate(grid):
            step = step * g + pl.program_id(d)
        if steps == 1:
            run("start")
            core(*own_in, *own_out, *scr[:n_scr])
            run("mid")
            run("finish")
            return
        pl.when(step == 0)(lambda: run("start"))
        core(*own_in, *own_out, *scr[:n_scr])
        pl.when(step == (3 * steps) // 4)(lambda: run("mid"))
        pl.when(step == steps - 1)(lambda: run("finish"))

    sem = ("arbitrary",) * len(grid) if stages else ("parallel",) * max(len(grid) - 1, 0) + ("arbitrary",) * min(len(grid), 1)
    res = pl.pallas_call(
        body, name=name, grid=grid,
        in_specs=list(in_specs) + [ANY] * len(s_args),
        out_specs=list(out_specs) + [ANY] * len(s_outs),
        out_shape=list(out_shape) + s_outs,
        input_output_aliases=aliases,
        scratch_shapes=list(scratch_shapes) + ([pltpu.SemaphoreType.DMA((n_sems,))] if stages else []),
        compiler_params=_params(sem) if grid else pltpu.CompilerParams(vmem_limit_bytes=V7X_VMEM_LIMIT),
    )(*args, *s_args)
    return list(res[:n_out]), list(res[n_out:])


class _Pipe:
    def __init__(self):
        self.ready = []
        self.flushes = 0
        self.after = None

    def add(self, stage):
        self.ready.append(stage)

    def carry(self, call, long=True):
        stages = [st for st in self.ready if long or not st.slow]
        self.ready = [st for st in self.ready if not (long or not st.slow)]
        own, outs = call(stages)
        k = 0
        for st in stages:
            n = len(st.inouts) + len(st.outs)
            st.then(*outs[k:k + n])
            k += n
        if self.after is not None:
            self.after()
        return own

    def flush(self):
        while self.ready:
            self.flushes += 1
            self.carry(lambda stages: _staged_call(
                lambda *refs: None, name=f"comm_tail_{self.flushes}", grid=(), in_specs=[], out_specs=[], out_shape=[],
                scratch_shapes=[], args=[], stages=stages))


def _mixer_fwd(layer, x, g1, bgate, lng, lnb, wm, bsf, wsc, win_g, wb_g, wout_g, stages):
    t_len = x.shape[0]
    tm = min(TM_MIX, t_len)
    nt = t_len // tm
    nb = tm // GMLP_BLOCK

    def core(x_ref, x_late_ref, g1_ref, bgate_ref, lng_ref, lnb_ref, wm_ref, bsf_ref, wsc_ref, win_hbm, wb_hbm, wout_hbm,
             zc_ref, ya_ref, yb_ref, q_ref, sa_ref, ca_ref, sb_ref, cb_ref, ug_ref, fu_ref, xh_ref, cv_ref,
             mg_ref, h_ref, x2_ref,
             win_v, wb_v, wout_v, carry, vn_s, f_s, z_s, sems):
        i = pl.program_id(0)

        @pl.when(i == 0)
        def _():
            cps = (_load_col_sharded(win_hbm, win_v, sems, 0) + _load_branch(wb_hbm, wb_v, sems, 4)
                   + _load_row_sharded(wout_hbm, wout_v, sems, 12))
            _start_all(cps)
            carry[...] = jnp.zeros_like(carry)
            z_s[...] = jnp.zeros_like(z_s)
            _wait_all(cps)

        xv = x_ref[...]
        r = lax.rsqrt(jnp.mean(xv * xv, axis=-1, keepdims=True) + RMS_EPS)
        h_ref[...] = (xv * r * g1_ref[...]).astype(BF16)

        def zcols(c0, n, keep=None):
            zv = z_s[:, c0:c0 + n]
            z_s[:, c0:c0 + n] = _dot(h_ref[...], win_v[:, c0:c0 + n])
            if keep is not None:
                zc_ref[:, keep * D_B:(keep + 1) * D_B] = zv.astype(BF16)
            return zv

        v = zcols(C_V, D_A)
        vg, tv = _gelu(v)
        mu = jnp.mean(vg, axis=-1, keepdims=True)
        vc = vg - mu
        rstd = lax.rsqrt(jnp.mean(vc * vc, axis=-1, keepdims=True) + LN_EPS)
        xh = vc * rstd
        xh_ref[...] = xh.astype(BF16)
        cv_ref[...] = (rstd * _gelu_grad(v, tv)).astype(BF16)
        vn_s[...] = (xh * lng_ref[...] + lnb_ref[...]).astype(BF16)
        for hd in range(A_HEADS):
            cols = slice(hd * 128, (hd + 1) * 128)
            vcat = jnp.concatenate([vn_s[b * 128:(b + 1) * 128, cols] for b in range(nb)], axis=1)
            fcat = _dot(wm_ref[hd], vcat)
            for b in range(nb):
                f_s[b * 128:(b + 1) * 128, cols] = fcat[:, b * 128:(b + 1) * 128]
        u = zcols(C_U, D_A)
        ug, tu = _gelu(u)
        ug_ref[...] = ug.astype(BF16)
        fb = f_s[...] + jnp.concatenate([bsf_ref[...]] * nb, axis=0)
        fu_ref[...] = (fb * _gelu_grad(u, tu)).astype(BF16)
        ya_ref[...] = (ug * fb).astype(BF16)

        p = zcols(C_CG, D_B, keep=1) * zcols(C_HB, D_B, keep=2)
        cr = carry[...]
        q = wsc_ref[0:1, :] * _shift_down(p, cr, 2) + wsc_ref[1:2, :] * _shift_down(p, cr, 1) + wsc_ref[2:3, :] * p
        carry[...] = p[tm - 8:tm, :]
        q_ref[...] = q.astype(BF16)
        yb_ref[...] = (zcols(C_BG, D_B, keep=0) * q).astype(BF16)

        av = _dot(ya_ref[...], wb_v[0])
        sa = _sigmoid(zcols(C_GA, D_MODEL) + bgate_ref[:, 0:D_MODEL])
        sa_ref[...] = sa.astype(BF16)
        mg = sa * av
        ca_ref[...] = (mg * (1.0 - sa)).astype(BF16)
        bv = _dot(yb_ref[...], wb_v[1])
        sb = _sigmoid(zcols(C_GB, D_MODEL) + bgate_ref[:, D_MODEL:2 * D_MODEL])
        sb_ref[...] = sb.astype(BF16)
        mb = sb * bv
        cb_ref[...] = (mb * (1.0 - sb)).astype(BF16)
        mg_ref[...] = (mg + mb).astype(BF16)
        x2_ref[...] = x_late_ref[...] + _dot(mg_ref[...], wout_v[...])

    def tile(n, lag):
        return pl.BlockSpec((tm, n), lambda i: (jnp.clip(i - lag, 0, nt - 1), 0))

    outs = [
        jax.ShapeDtypeStruct((t_len, 3 * D_B), BF16),
        jax.ShapeDtypeStruct((t_len, D_A), BF16),
        jax.ShapeDtypeStruct((t_len, D_B), BF16),
        jax.ShapeDtypeStruct((t_len, D_B), BF16),
        jax.ShapeDtypeStruct((t_len, D_MODEL), BF16),
        jax.ShapeDtypeStruct((t_len, D_MODEL), BF16),
        jax.ShapeDtypeStruct((t_len, D_MODEL), BF16),
        jax.ShapeDtypeStruct((t_len, D_MODEL), BF16),
        jax.ShapeDtypeStruct((t_len, D_A), BF16),
        jax.ShapeDtypeStruct((t_len, D_A), BF16),
        jax.ShapeDtypeStruct((t_len, D_A), BF16),
        jax.ShapeDtypeStruct((t_len, D_A), BF16),
        jax.ShapeDtypeStruct((t_len, D_MODEL), BF16),
        jax.ShapeDtypeStruct((t_len, D_MODEL), BF16),
        jax.ShapeDtypeStruct((t_len, D_MODEL), F32),
    ]
    return _staged_call(
        core, name=f"mixer_fwd_l{layer}", grid=(nt + 1,),
        in_specs=[tile(D_MODEL, 0), tile(D_MODEL, 1), _const_spec((1, D_MODEL)), _const_spec((1, 2 * D_MODEL)),
                  _const_spec((1, D_A)), _const_spec((1, D_A)), _const_spec((A_HEADS, 128, 128)),
                  _const_spec((128, D_A)), _const_spec((8, D_B)), ANY, ANY, ANY],
        out_specs=[tile(o.shape[1], 0 if k == len(outs) - 2 else 1) for k, o in enumerate(outs)],
        out_shape=outs,
        scratch_shapes=[pltpu.VMEM((D_MODEL, D_IN), BF16), pltpu.VMEM((2, D_A, D_MODEL), BF16),
                        pltpu.VMEM((D_MODEL, D_MODEL), BF16), pltpu.VMEM((8, D_B), F32),
                        pltpu.VMEM((tm, D_A), BF16), pltpu.VMEM((tm, D_A), F32), pltpu.VMEM((tm, D_IN), F32),
                        pltpu.SemaphoreType.DMA((16,))],
        args=[x, x, g1, bgate, lng, lnb, wm, bsf, wsc, win_g, wb_g, wout_g], stages=stages)


def _ffn_fwd(layer, x2, g2, wfc, bfc, wup_g, wdown_g, stages, head=None):
    t_len = x2.shape[0]
    tm = min(TM_FFN, t_len)
    nt = t_len // tm

    def core(*refs):
        if head is None:
            (x_ref, g2_ref, wfc_ref, bfc_ref, wup_hbm, wdown_hbm, up_ref, silu_ref, dsilu_ref, act_ref, h_ref, x3_ref,
             wup_v, wdown_v, carry, sems) = refs
        else:
            (x_ref, g2_ref, wfc_ref, bfc_ref, t_ref, gf_ref, wup_hbm, wdown_hbm, up_ref, silu_ref, dsilu_ref, act_ref,
             h_ref, dx_ref, dgf_ref, loss_ref, wup_v, wdown_v, carry, sems) = refs
        i = pl.program_id(0)

        @pl.when(i == 0)
        def _():
            cps = _load_col_sharded(wup_hbm, wup_v, sems, 0) + _load_row_sharded(wdown_hbm, wdown_v, sems, 4)
            _start_all(cps)
            carry[...] = jnp.zeros_like(carry)
            if head is not None:
                dgf_ref[...] = jnp.zeros_like(dgf_ref)
                loss_ref[...] = jnp.zeros_like(loss_ref)
            _wait_all(cps)

        xv = x_ref[...]
        r = lax.rsqrt(jnp.mean(xv * xv, axis=-1, keepdims=True) + RMS_EPS)
        h_ref[...] = (xv * r * g2_ref[...]).astype(BF16)
        gate = _dot(h_ref[...], wup_v[:, 0:D_FF])
        up_ref[:, 0:D_FF] = gate.astype(BF16)
        cr = carry[...]
        gc = (wfc_ref[0:1, :] * _shift_down(gate, cr, 2) + wfc_ref[1:2, :] * _shift_down(gate, cr, 1)
              + wfc_ref[2:3, :] * gate + bfc_ref[...])
        carry[...] = gate[tm - 8:tm, :]
        sg = _sigmoid(gc)
        silu = gc * sg
        silu_ref[...] = silu.astype(BF16)
        dsilu_ref[...] = (sg + silu * (1.0 - sg)).astype(BF16)
        val = _dot(h_ref[...], wup_v[:, D_FF:2 * D_FF])
        up_ref[:, D_FF:2 * D_FF] = val.astype(BF16)
        act_ref[...] = (silu * val).astype(BF16)
        x3 = x_ref[...] + _dot(act_ref[...], wdown_v[...])
        if head is None:
            x3_ref[...] = x3
        else:
            r3 = lax.rsqrt(jnp.mean(x3 * x3, axis=-1, keepdims=True) + RMS_EPS)
            xh = x3 * r3
            err = xh * gf_ref[...] - t_ref[...]
            loss_ref[...] += _colsum8(err * err)
            dy = err * (1.0 / D_MODEL)
            dgf_ref[...] += _colsum8(dy * xh)
            dxh = dy * gf_ref[...]
            dx_ref[...] = r3 * (dxh - xh * jnp.mean(dxh * xh, axis=-1, keepdims=True))

    outs = [
        jax.ShapeDtypeStruct((t_len, 2 * D_FF), BF16),
        jax.ShapeDtypeStruct((t_len, D_FF), BF16),
        jax.ShapeDtypeStruct((t_len, D_FF), BF16),
        jax.ShapeDtypeStruct((t_len, D_FF), BF16),
        jax.ShapeDtypeStruct((t_len, D_MODEL), BF16),
        jax.ShapeDtypeStruct((t_len, D_MODEL), F32),
    ]
    in_specs = [_row_spec(tm, D_MODEL), _const_spec((1, D_MODEL)), _const_spec((8, D_FF)), _const_spec((1, D_FF))]
    out_specs = [_row_spec(tm, o.shape[1]) for o in outs]
    args = [x2, g2, wfc, bfc]
    if head is not None:
        in_specs += [_row_spec(tm, D_MODEL), _const_spec((1, D_MODEL))]
        args += list(head)
        outs += [jax.ShapeDtypeStruct((8, D_MODEL), F32)] * 2
        out_specs += [_const_spec((8, D_MODEL))] * 2
    return _staged_call(
        core, name=f"ffn_fwd_l{layer}", grid=(nt,),
        in_specs=in_specs + [ANY, ANY], out_specs=out_specs, out_shape=outs,
        scratch_shapes=[pltpu.VMEM((D_MODEL, 2 * D_FF), BF16), pltpu.VMEM((D_FF, D_MODEL), BF16),
                        pltpu.VMEM((8, D_FF), F32), pltpu.SemaphoreType.DMA((8,))],
        args=args + [wup_g, wdown_g], stages=stages)


def _ffn_bwd(layer, dx3, x2, up, silu, dsilu, g2, wfc, wup_g, wdown_g):
    t_len = x2.shape[0]
    tm = min(TM_FFN, t_len)
    nt = t_len // tm

    def core(dx3_ref, dx3_late_ref, x_ref, up_ref, silu_ref, dsilu_ref, g2_ref, wfc_ref, wup_hbm, wdown_hbm,
             dx2_ref, dup_ref, dx3b_ref, dg2_ref, dbfc_ref, dwfc_ref,
             wup_v, wdown_v, carry, da_s, dup_s, sems):
        i = pl.program_id(0)

        @pl.when(i == 0)
        def _():
            cps = _load_col_sharded(wup_hbm, wup_v, sems, 0) + _load_row_sharded(wdown_hbm, wdown_v, sems, 4)
            _start_all(cps)
            for ref in (carry, da_s, dup_s, dg2_ref, dbfc_ref, dwfc_ref):
                ref[...] = jnp.zeros_like(ref)
            _wait_all(cps)

        live = (i <= nt).astype(F32)
        dx3b_ref[...] = dx3_ref[...].astype(BF16)
        dh = jnp.zeros((tm, D_MODEL), F32)
        for c0, c1 in FF_CHUNKS:
            v0, v1 = D_FF + c0, D_FF + c1
            dh = dh + _dot_nt(dup_s[:, c0:c1], wup_v[:, c0:c1]) + _dot_nt(dup_s[:, v0:v1], wup_v[:, v0:v1])
            da = da_s[:, c0:c1]
            dval = (da * silu_ref[:, c0:c1].astype(F32)).astype(BF16)
            dup_ref[:, v0:v1] = dval
            dup_s[:, v0:v1] = dval
            dgc = da * up_ref[:, v0:v1].astype(F32) * dsilu_ref[:, c0:c1].astype(F32)
            cr = carry[:, c0:c1]
            dgc1 = _shift_up(dgc, cr, 1)
            dgc2 = _shift_up(dgc, cr, 2)
            carry[:, c0:c1] = jnp.where(i < nt, dgc[0:8, :], cr)
            gate = up_ref[:, c0:c1].astype(F32)
            dbfc_ref[:, c0:c1] += live * _colsum8(dgc)
            dwfc_ref[0, :, c0:c1] += live * _colsum8(dgc2 * gate)
            dwfc_ref[1, :, c0:c1] += live * _colsum8(dgc1 * gate)
            dwfc_ref[2, :, c0:c1] += live * _colsum8(dgc * gate)
            dgate = (wfc_ref[2:3, c0:c1] * dgc + wfc_ref[1:2, c0:c1] * dgc1 + wfc_ref[0:1, c0:c1] * dgc2).astype(BF16)
            dup_ref[:, c0:c1] = dgate
            dup_s[:, c0:c1] = dgate
            da_s[:, c0:c1] = _dot_nt(dx3b_ref[...], wdown_v[c0:c1, :])
        xv = x_ref[...]
        r = lax.rsqrt(jnp.mean(xv * xv, axis=-1, keepdims=True) + RMS_EPS)
        xh = xv * r
        dg2_ref[...] += _colsum8(dh * xh)
        dxh = dh * g2_ref[...]
        dx2_ref[...] = dx3_late_ref[...] + r * (dxh - xh * jnp.mean(dxh * xh, axis=-1, keepdims=True))

    def tile(n, lag):
        return pl.BlockSpec((tm, n), lambda i: (nt - 1 - jnp.clip(i - lag, 0, nt - 1), 0))

    outs = [
        jax.ShapeDtypeStruct((t_len, D_MODEL), F32),
        jax.ShapeDtypeStruct((t_len, 2 * D_FF), BF16),
        jax.ShapeDtypeStruct((t_len, D_MODEL), BF16),
        jax.ShapeDtypeStruct((8, D_MODEL), F32),
        jax.ShapeDtypeStruct((8, D_FF), F32),
        jax.ShapeDtypeStruct((3, 8, D_FF), F32),
    ]
    return _staged_call(
        core, name=f"ffn_bwd_l{layer}", grid=(nt + 2,),
        in_specs=[tile(D_MODEL, 0), tile(D_MODEL, 2), tile(D_MODEL, 2), tile(2 * D_FF, 1), tile(D_FF, 1), tile(D_FF, 1),
                  _const_spec((1, D_MODEL)), _const_spec((8, D_FF)), ANY, ANY],
        out_specs=[tile(D_MODEL, 2), tile(2 * D_FF, 1), tile(D_MODEL, 0),
                   _const_spec((8, D_MODEL)), _const_spec((8, D_FF)), _const_spec((3, 8, D_FF))],
        out_shape=outs,
        scratch_shapes=[pltpu.VMEM((D_MODEL, 2 * D_FF), BF16), pltpu.VMEM((D_FF, D_MODEL), BF16),
                        pltpu.VMEM((8, D_FF), F32), pltpu.VMEM((tm, D_FF), F32), pltpu.VMEM((tm, 2 * D_FF), BF16),
                        pltpu.SemaphoreType.DMA((8,))],
        args=[dx3, dx3, x2, up, silu, dsilu, g2, wfc, wup_g, wdown_g], stages=[])[0]


def _mixer_bwd(layer, dx2, x, zc, qs, sa, ca, sb, cb, ug, fu, xhs, cv, g1, lng, lnb, wmt, wsc, win_g, wb_g, wout_g):
    t_len = x.shape[0]
    tm = min(TM_MIX, t_len)
    nt = t_len // tm
    nb = tm // GMLP_BLOCK

    def core(dx2_ref, x_ref, zc_ref, q_ref, sa_ref, ca_ref, sb_ref, cb_ref, ug_ref, fu_ref, xh_ref, cv_ref,
             g1_ref, lng_ref, lnb_ref, wmt_ref, wsc_ref, win_hbm, wb_hbm, wout_hbm,
             dx_ref, dz_ref, da_ref, db_ref, dx2b_ref, dg1_ref, dbgate_ref, dlng_ref, dlnb_ref, dwm_ref, dbsf_ref, dwsc_ref,
             win_v, wb_v, wout_v, carry, vn_s, df_s, dvn_s, sems):
        i = pl.program_id(0)

        @pl.when(i == 0)
        def _():
            cps = (_load_col_sharded(win_hbm, win_v, sems, 0) + _load_branch(wb_hbm, wb_v, sems, 4)
                   + _load_row_sharded(wout_hbm, wout_v, sems, 12))
            _start_all(cps)
            for ref in (carry, dg1_ref, dbgate_ref, dlng_ref, dlnb_ref, dwm_ref, dbsf_ref, dwsc_ref):
                ref[...] = jnp.zeros_like(ref)
            _wait_all(cps)

        def kept(k):
            return zc_ref[:, k * D_B:(k + 1) * D_B].astype(F32)

        def dz_cols(c0, n, val):
            dz_ref[:, c0:c0 + n] = val.astype(BF16)
            return _dot_nt(dz_ref[:, c0:c0 + n], win_v[:, c0:c0 + n])

        dx2b_ref[...] = dx2_ref[...].astype(BF16)
        dm = _dot_nt(dx2b_ref[...], wout_v[...])
        da_ref[...] = (dm * sa_ref[...].astype(F32)).astype(BF16)
        dga = dm * ca_ref[...].astype(F32)
        dh = dz_cols(C_GA, D_MODEL, dga)
        dbgate_ref[:, 0:D_MODEL] += _colsum8(dga)
        dya = _dot_nt(da_ref[...], wb_v[0])
        db_ref[...] = (dm * sb_ref[...].astype(F32)).astype(BF16)
        dgb = dm * cb_ref[...].astype(F32)
        dh = dh + dz_cols(C_GB, D_MODEL, dgb)
        dbgate_ref[:, D_MODEL:2 * D_MODEL] += _colsum8(dgb)
        dyb = _dot_nt(db_ref[...], wb_v[1])

        xh = xh_ref[...].astype(F32)
        vn_s[...] = (xh * lng_ref[...] + lnb_ref[...]).astype(BF16)
        df = dya * ug_ref[...].astype(F32)
        df_s[...] = df.astype(BF16)
        dbsf_acc = df[0:128, :]
        for b in range(1, nb):
            dbsf_acc = dbsf_acc + df[b * 128:(b + 1) * 128, :]
        dbsf_ref[...] += dbsf_acc
        for hd in range(A_HEADS):
            cols = slice(hd * 128, (hd + 1) * 128)
            vcat = jnp.concatenate([vn_s[b * 128:(b + 1) * 128, cols] for b in range(nb)], axis=1)
            dcat = jnp.concatenate([df_s[b * 128:(b + 1) * 128, cols] for b in range(nb)], axis=1)
            gcat = _dot(wmt_ref[hd], dcat)
            dwm_ref[hd] += _dot_nt(dcat, vcat)
            for b in range(nb):
                dvn_s[b * 128:(b + 1) * 128, cols] = gcat[:, b * 128:(b + 1) * 128]
        dh = dh + dz_cols(C_U, D_A, dya * fu_ref[...].astype(F32))
        dvn = dvn_s[...]
        dlng_ref[...] += _colsum8(dvn * xh)
        dlnb_ref[...] += _colsum8(dvn)
        dxh = dvn * lng_ref[...]
        dvc = dxh - jnp.mean(dxh, axis=-1, keepdims=True) - xh * jnp.mean(dxh * xh, axis=-1, keepdims=True)
        dh = dh + dz_cols(C_V, D_A, dvc * cv_ref[...].astype(F32))

        cg = kept(1)
        hbv = kept(2)
        p = cg * hbv
        dh = dh + dz_cols(C_BG, D_B, dyb * q_ref[...].astype(F32))
        dq = dyb * kept(0)
        cr = carry[...]
        dq1 = _shift_up(dq, cr, 1)
        dq2 = _shift_up(dq, cr, 2)
        carry[...] = dq[0:8, :]
        dwsc_ref[0] += _colsum8(dq2 * p)
        dwsc_ref[1] += _colsum8(dq1 * p)
        dwsc_ref[2] += _colsum8(dq * p)
        dp = wsc_ref[2:3, :] * dq + wsc_ref[1:2, :] * dq1 + wsc_ref[0:1, :] * dq2
        dh = dh + dz_cols(C_CG, D_B, dp * hbv)
        dh = dh + dz_cols(C_HB, D_B, dp * cg)

        xv = x_ref[...]
        r = lax.rsqrt(jnp.mean(xv * xv, axis=-1, keepdims=True) + RMS_EPS)
        xn = xv * r
        dg1_ref[...] += _colsum8(dh * xn)
        dxn = dh * g1_ref[...]
        dx_ref[...] = dx2_ref[...] + r * (dxn - xn * jnp.mean(dxn * xn, axis=-1, keepdims=True))

    outs = [
        jax.ShapeDtypeStruct((t_len, D_MODEL), F32),
        jax.ShapeDtypeStruct((t_len, D_IN), BF16),
        jax.ShapeDtypeStruct((t_len, D_MODEL), BF16),
        jax.ShapeDtypeStruct((t_len, D_MODEL), BF16),
        jax.ShapeDtypeStruct((t_len, D_MODEL), BF16),
        jax.ShapeDtypeStruct((8, D_MODEL), F32),
        jax.ShapeDtypeStruct((8, 2 * D_MODEL), F32),
        jax.ShapeDtypeStruct((8, D_A), F32),
        jax.ShapeDtypeStruct((8, D_A), F32),
        jax.ShapeDtypeStruct((A_HEADS, 128, 128), F32),
        jax.ShapeDtypeStruct((128, D_A), F32),
        jax.ShapeDtypeStruct((3, 8, D_B), F32),
    ]

    return _staged_call(
        core, name=f"mixer_bwd_l{layer}", grid=(nt,),
        in_specs=[_row_spec(tm, D_MODEL, nt), _row_spec(tm, D_MODEL, nt), _row_spec(tm, 3 * D_B, nt),
                  _row_spec(tm, D_B, nt), _row_spec(tm, D_MODEL, nt), _row_spec(tm, D_MODEL, nt),
                  _row_spec(tm, D_MODEL, nt), _row_spec(tm, D_MODEL, nt), _row_spec(tm, D_A, nt), _row_spec(tm, D_A, nt),
                  _row_spec(tm, D_A, nt), _row_spec(tm, D_A, nt),
                  _const_spec((1, D_MODEL)), _const_spec((1, D_A)), _const_spec((1, D_A)),
                  _const_spec((A_HEADS, 128, 128)), _const_spec((8, D_B)), ANY, ANY, ANY],
        out_specs=[_row_spec(tm, D_MODEL, nt), _row_spec(tm, D_IN, nt), _row_spec(tm, D_MODEL, nt),
                   _row_spec(tm, D_MODEL, nt), _row_spec(tm, D_MODEL, nt),
                   _const_spec((8, D_MODEL)), _const_spec((8, 2 * D_MODEL)), _const_spec((8, D_A)), _const_spec((8, D_A)),
                   _const_spec((A_HEADS, 128, 128)), _const_spec((128, D_A)), _const_spec((3, 8, D_B))],
        out_shape=outs,
        scratch_shapes=[pltpu.VMEM((D_MODEL, D_IN), BF16), pltpu.VMEM((2, D_A, D_MODEL), BF16),
                        pltpu.VMEM((D_MODEL, D_MODEL), BF16), pltpu.VMEM((8, D_B), F32),
                        pltpu.VMEM((tm, D_A), BF16), pltpu.VMEM((tm, D_A), BF16), pltpu.VMEM((tm, D_A), F32),
                        pltpu.SemaphoreType.DMA((16,))],
        args=[dx2, x, zc, qs, sa, ca, sb, cb, ug, fu, xhs, cv, g1, lng, lnb, wmt, wsc, win_g, wb_g, wout_g],
        stages=[])[0]


def _wgrad(name, layer, a, b, rows, cols, row_blk, col_blk, stages):
    t_len, m = a.shape
    n = b.shape[1]
    tk = min(TK_WGRAD, t_len)
    col_sharded = n == N_CHIPS * cols
    grid = (m // row_blk, n // col_blk, t_len // tk)
    shards = col_blk // cols if col_sharded else 1

    if col_sharded:
        out_shape = (N_CHIPS, rows, cols)
        out_spec = pl.BlockSpec((shards, row_blk, cols), lambda i, j, k: (j, i, 0))
    else:
        out_shape = (N_CHIPS * rows, cols)
        out_spec = pl.BlockSpec((row_blk, col_blk), lambda i, j, k: (i, j))

    def core(a_ref, b_ref, o_ref):
        @pl.when(pl.program_id(2) == 0)
        def _():
            o_ref[...] = jnp.zeros_like(o_ref)

        g = _dot_tn(a_ref[...], b_ref[...])
        if col_sharded:
            for q in range(shards):
                o_ref[q] += g[:, q * cols:(q + 1) * cols]
        else:
            o_ref[...] += g

    own, outs = _staged_call(
        core, name=f"wgrad_{name}_l{layer}", grid=grid,
        in_specs=[pl.BlockSpec((tk, row_blk), lambda i, j, k: (k, i)), pl.BlockSpec((tk, col_blk), lambda i, j, k: (k, j))],
        out_specs=[out_spec], out_shape=[jax.ShapeDtypeStruct(out_shape, F32)], scratch_shapes=[],
        args=[a, b], stages=stages)
    return [own[0].reshape(N_CHIPS, rows, cols)], outs


def _wgrad_branch(layer, ya, da, yb, db, stages):
    t_len = ya.shape[0]
    tk = min(TK_WGRAD, t_len)

    cs = D_MODEL // N_CHIPS

    def core(ya_ref, da_ref, yb_ref, db_ref, o_ref):
        @pl.when(pl.program_id(0) == 0)
        def _():
            o_ref[...] = jnp.zeros_like(o_ref)

        ga = _dot_tn(ya_ref[...], da_ref[...])
        gb = _dot_tn(yb_ref[...], db_ref[...])
        for k in range(N_CHIPS):
            o_ref[k, 0:D_A, :] += ga[:, k * cs:(k + 1) * cs]
            o_ref[k, D_A:2 * D_A, :] += gb[:, k * cs:(k + 1) * cs]

    a_spec = pl.BlockSpec((tk, D_A), lambda k: (k, 0))
    d_spec = pl.BlockSpec((tk, D_MODEL), lambda k: (k, 0))
    return _staged_call(
        core, name=f"wgrad_w_branch_l{layer}", grid=(t_len // tk,),
        in_specs=[a_spec, d_spec, a_spec, d_spec],
        out_specs=[pl.BlockSpec((N_CHIPS, 2 * D_A, cs), lambda k: (0, 0, 0))],
        out_shape=[jax.ShapeDtypeStruct((N_CHIPS, 2 * D_A, cs), F32)], scratch_shapes=[],
        args=[ya, da, yb, db], stages=stages)


def _flat_blk(rows, cols):
    blk = rows
    while blk * cols * 4 > 2 * 1024 * 1024 and blk % 16 == 0:
        blk //= 2
    return blk


def _cast_into_slots(name, jobs, chip, stages):
    blks = [_flat_blk(w.shape[1], w.shape[2]) for w, _ in jobs]
    nblks = [w.shape[1] // b for (w, _), b in zip(jobs, blks)]
    n = len(jobs)
    out_shape = [jax.ShapeDtypeStruct((N_CHIPS,) + w.shape[1:], BF16) for w, _ in jobs]

    def core(*refs):
        for w_ref, o_ref in zip(refs[-2 * n:-n], refs[-n:]):
            o_ref[...] = w_ref[...].astype(BF16)

    def slot(*scalars):
        return scalars[0][0] if scalars else 2 * lax.axis_index("x") + lax.axis_index("y")

    in_specs = [pl.BlockSpec((None, b, w.shape[2]), lambda i, *s, la=la, k=k: (la, jnp.minimum(i, k - 1), 0))
                for (w, la), b, k in zip(jobs, blks, nblks)]
    out_specs = [pl.BlockSpec((None, b, w.shape[2]), lambda i, *s, k=k: (slot(*s), jnp.minimum(i, k - 1), 0))
                 for (w, _), b, k in zip(jobs, blks, nblks)]
    args = [w for w, _ in jobs]
    if stages:
        return _staged_call(core, name=f"cast_{name}", grid=(max(nblks),), in_specs=in_specs, out_specs=out_specs,
                            out_shape=out_shape, scratch_shapes=[], args=args, stages=stages)
    own = pl.pallas_call(
        core, name=f"cast_{name}",
        grid_spec=pltpu.PrefetchScalarGridSpec(num_scalar_prefetch=1, grid=(max(nblks),), in_specs=in_specs,
                                               out_specs=out_specs),
        out_shape=out_shape, compiler_params=_params(),
    )(chip, *args)
    return list(own), []


def _reduction_sums(name, jobs, pos):
    in_specs, out_specs, out_shape, args, bodies, counts = [], [], [], [], [], []
    for job in jobs:
        kind, grad, other = job[0], job[1], job[2]
        _, h, cols = other.shape
        blk = _flat_blk(h, cols)
        nblk = h // blk
        if kind == "pair":
            total = N_CHIPS * nblk

            def block(s, total=total, nblk=nblk):
                b = jnp.minimum(s, total - 1)
                return b // nblk, b % nblk

            spec = pl.BlockSpec((None, blk, cols), lambda s, p, block=block: (block(s)[0], block(s)[1], 0))
            in_specs += [pl.BlockSpec((None, blk, cols), lambda s, p, block=block, nblk=nblk:
                                      (block(s)[0], p[1] * nblk + block(s)[1], 0)), spec]
            out_specs.append(spec)
            out_shape.append(jax.ShapeDtypeStruct((N_CHIPS, h, cols), BF16))
            args += [grad, other]
            bodies.append((2, lambda g, o, out: out.__setitem__(..., (g[...] + o[...]).astype(BF16))))
        else:
            total = nblk

            def block(s, total=total):
                return jnp.minimum(s, total - 1)

            in_specs += [pl.BlockSpec((None, blk, cols), lambda s, p, block=block, nblk=nblk:
                                      (p[0], p[1] * nblk + block(s), 0)),
                         pl.BlockSpec((None, blk, cols), lambda s, p, block=block: (p[0], block(s), 0)),
                         pl.BlockSpec((3, blk, cols), lambda s, p, block=block: (0, block(s), 0))]
            out_specs.append(pl.BlockSpec((blk, cols), lambda s, p, block=block, nblk=nblk: (p[1] * nblk + block(s), 0)))
            out_shape.append(jax.ShapeDtypeStruct((2 * h, cols), F32))
            args += [grad, other, job[3]]
            bodies.append((3, lambda g, o, r, out: out.__setitem__(
                ..., (((g[...] + o[...]) + r[0].astype(F32)) + r[1].astype(F32)) + r[2].astype(F32))))
        counts.append(total)

    def body(pos_ref, *refs):
        ins, outs = refs[:len(args)], refs[len(args):]
        k = 0
        for (n_in, fn), out in zip(bodies, outs):
            fn(*ins[k:k + n_in], out)
            k += n_in

    return pl.pallas_call(
        body, name=f"reduction_sums_{name}",
        grid_spec=pltpu.PrefetchScalarGridSpec(num_scalar_prefetch=1, grid=(max(counts),), in_specs=in_specs,
                                               out_specs=out_specs),
        out_shape=out_shape,
        compiler_params=_params(),
    )(pos, *args)


def _sum_slots(name, slots):
    n, rows, _ = slots.shape

    def body(s_ref, o_ref):
        acc = s_ref[0]
        for d in range(1, n):
            acc = acc + s_ref[d]
        o_ref[...] = acc

    return pl.pallas_call(
        body, name=f"sum_slots_{name}", grid=(1,),
        in_specs=[pl.BlockSpec((n, rows, 128), lambda i: (0, 0, 0))],
        out_specs=pl.BlockSpec((rows, 128), lambda i: (0, 0)),
        out_shape=jax.ShapeDtypeStruct((rows, 128), F32),
        compiler_params=_params(),
    )(slots)


def _adamw_math(w, g, m, v):
    m2 = ADAM_B1 * m + (1.0 - ADAM_B1) * g
    v2 = ADAM_B2 * v + (1.0 - ADAM_B2) * (g * g)
    m_hat = m2 / (1.0 - ADAM_B1 ** ADAM_STEP)
    v_hat = v2 / (1.0 - ADAM_B2 ** ADAM_STEP)
    delta = -ADAM_LR * (m_hat / (jnp.sqrt(v_hat) + ADAM_EPS) + ADAM_WD * w)
    return delta, m2, v2


SC_TILES = 32
SC_COLS = 512


def _adamw_sparsecore(name, w, g, m, v):
    rows, cols = w.shape
    col_parts = cols // SC_COLS
    per_tile = (rows // 8) * col_parts // SC_TILES
    assert per_tile * SC_TILES * 8 * SC_COLS == rows * cols

    def body(w_hbm, g_hbm, m_hbm, v_hbm, d_hbm, m2_hbm, v2_hbm, wb, gb, mb, vb, db):
        tile = lax.axis_index("subcore") * 2 + lax.axis_index("core")

        @pl.loop(0, per_tile)
        def _(k):
            u = tile * per_tile + k
            part = (pl.ds((u // col_parts) * 8, 8), pl.ds((u % col_parts) * SC_COLS, SC_COLS))
            pltpu.sync_copy(w_hbm.at[part], wb)
            pltpu.sync_copy(g_hbm.at[part], gb)
            pltpu.sync_copy(m_hbm.at[part], mb)
            pltpu.sync_copy(v_hbm.at[part], vb)

            @pl.loop(0, 8)
            def _(r):
                @pl.loop(0, SC_COLS, step=16)
                def _(i):
                    s = pl.ds(i, 16)
                    d, m2, v2 = _adamw_math(wb[r, s], gb[r, s], mb[r, s], vb[r, s])
                    db[r, s] = d
                    mb[r, s] = m2
                    vb[r, s] = v2

            pltpu.sync_copy(db, d_hbm.at[part])
            pltpu.sync_copy(mb, m2_hbm.at[part])
            pltpu.sync_copy(vb, v2_hbm.at[part])

    return pl.kernel(
        body, name=f"adamw_sc_{name}",
        out_type=[jax.ShapeDtypeStruct(w.shape, F32)] * 3,
        mesh=plsc.VectorSubcoreMesh(core_axis_name="core", subcore_axis_name="subcore"),
        scratch_types=[pltpu.VMEM((8, SC_COLS), F32)] * 5,
    )(w, g, m, v)


def _adamw_big(name, w, g0, g1, m, v):
    _, rows, cols = w.shape
    blk = _flat_blk(rows, cols) // 2

    def body(w_ref, g0_ref, g1_ref, m_ref, v_ref, g_ref, d_ref, m2_ref, v2_ref):
        g = jnp.where(pl.program_id(0) == 0, g0_ref[...], g1_ref[...])
        d, m2, v2 = _adamw_math(w_ref[...], g, m_ref[...], v_ref[...])
        g_ref[...] = g
        d_ref[...] = d
        m2_ref[...] = m2
        v2_ref[...] = v2

    spec = pl.BlockSpec((None, blk, cols), lambda la, i: (la, i, 0))
    return pl.pallas_call(
        body, name=f"adamw_{name}", grid=(N_LAYERS, rows // blk),
        in_specs=[spec, pl.BlockSpec((blk, cols), lambda la, i: (i * (1 - la), 0)),
                  pl.BlockSpec((blk, cols), lambda la, i: (i * la, 0)), spec, spec],
        out_specs=[spec] * 4,
        out_shape=[jax.ShapeDtypeStruct(w.shape, F32)] * 4,
        compiler_params=_params(("parallel", "parallel")),
    )(w, g0, g1, m, v)


def _adamw_small(ws, gs, ms, vs):
    n = len(ws)

    def body(*refs):
        ins, outs = refs[:4 * n], refs[4 * n:]
        for k in range(n):
            d, m2, v2 = _adamw_math(ins[k][...], ins[n + k][...], ins[2 * n + k][...], ins[3 * n + k][...])
            outs[k][...] = d
            outs[n + k][...] = m2
            outs[2 * n + k][...] = v2

    vmem = pl.BlockSpec(memory_space=pltpu.VMEM)
    return pl.pallas_call(
        body, name="adamw_small",
        in_specs=[vmem] * (4 * n), out_specs=[vmem] * (3 * n),
        out_shape=[jax.ShapeDtypeStruct(w.shape, F32) for w in ws] * 3,
        compiler_params=pltpu.CompilerParams(vmem_limit_bytes=V7X_VMEM_LIMIT),
    )(*ws, *gs, *ms, *vs)


SMALL = ("norm1_g", "b_gate", "gmlp_ln_g", "gmlp_ln_b", "w_spatial", "b_spatial", "w_shortconv", "norm2_g",
         "w_ffn_conv", "b_ffn_conv", "final_g")
ALL_WEIGHTS = ("norm1_g", "w_in", "b_gate", "gmlp_ln_g", "gmlp_ln_b", "w_spatial", "b_spatial", "w_shortconv",
               "w_branch", "w_out", "norm2_g", "w_ffn_up", "w_ffn_conv", "b_ffn_conv", "w_ffn_down", "final_g")


def _pack(arrays):
    flat = jnp.concatenate([a.reshape(-1) for a in arrays])
    n = flat.shape[0]
    rows = -(-n // 1024) * 8
    return jnp.pad(flat, (0, rows * 128 - n)).reshape(rows, 128)


def _unpack(packed, like):
    flat = packed.reshape(-1)
    out, off = [], 0
    for a in like:
        out.append(flat[off:off + a.size].reshape(a.shape))
        off += a.size
    return out


def _pad8(w):
    return jnp.pad(w, ((0, 5), (0, 0)))


def kernel(x, norm1_g, w_in, b_gate, gmlp_ln_g, gmlp_ln_b, w_spatial, b_spatial, w_shortconv, w_branch, w_out, norm2_g, w_ffn_up, w_ffn_conv, b_ffn_conv, w_ffn_down, final_g, loss_target, m_norm1_g, m_w_in, m_b_gate, m_gmlp_ln_g, m_gmlp_ln_b, m_w_spatial, m_b_spatial, m_w_shortconv, m_w_branch, m_w_out, m_norm2_g, m_w_ffn_up, m_w_ffn_conv, m_b_ffn_conv, m_w_ffn_down, m_final_g, v_norm1_g, v_w_in, v_b_gate, v_gmlp_ln_g, v_gmlp_ln_b, v_w_spatial, v_b_spatial, v_w_shortconv, v_w_branch, v_w_out, v_norm2_g, v_w_ffn_up, v_w_ffn_conv, v_b_ffn_conv, v_w_ffn_down, v_final_g):
    weights = dict(norm1_g=norm1_g, w_in=w_in, b_gate=b_gate, gmlp_ln_g=gmlp_ln_g, gmlp_ln_b=gmlp_ln_b,
                   w_spatial=w_spatial, b_spatial=b_spatial, w_shortconv=w_shortconv, w_branch=w_branch, w_out=w_out,
                   norm2_g=norm2_g, w_ffn_up=w_ffn_up, w_ffn_conv=w_ffn_conv, b_ffn_conv=b_ffn_conv,
                   w_ffn_down=w_ffn_down, final_g=final_g)
    mom = dict(norm1_g=m_norm1_g, w_in=m_w_in, b_gate=m_b_gate, gmlp_ln_g=m_gmlp_ln_g, gmlp_ln_b=m_gmlp_ln_b,
               w_spatial=m_w_spatial, b_spatial=m_b_spatial, w_shortconv=m_w_shortconv, w_branch=m_w_branch,
               w_out=m_w_out, norm2_g=m_norm2_g, w_ffn_up=m_w_ffn_up, w_ffn_conv=m_w_ffn_conv,
               b_ffn_conv=m_b_ffn_conv, w_ffn_down=m_w_ffn_down, final_g=m_final_g)
    vel = dict(norm1_g=v_norm1_g, w_in=v_w_in, b_gate=v_b_gate, gmlp_ln_g=v_gmlp_ln_g, gmlp_ln_b=v_gmlp_ln_b,
               w_spatial=v_w_spatial, b_spatial=v_b_spatial, w_shortconv=v_w_shortconv, w_branch=v_w_branch,
               w_out=v_w_out, norm2_g=v_norm2_g, w_ffn_up=v_w_ffn_up, w_ffn_conv=v_w_ffn_conv,
               b_ffn_conv=v_b_ffn_conv, w_ffn_down=v_w_ffn_down, final_g=v_final_g)

    cx, cy, cc = _mesh_pos()
    chip = 2 * cx + cy
    pos_arr = jnp.stack([chip, cc]).astype(jnp.int32)
    t_len = x.shape[1]
    xs = x.reshape(t_len, D_MODEL)
    target = loss_target.reshape(t_len, D_MODEL)
    pipe = _Pipe()

    full = {}

    mixer_w = ("w_in", "w_branch", "w_out")
    ffn_w = ("w_ffn_up", "w_ffn_down")
    slots = {}

    def cast(name, keys, stages):
        own, outs = _cast_into_slots(name, [(weights[n].reshape((N_LAYERS,) + BIG[n]), la) for n, la in keys],
                                     chip.astype(jnp.int32).reshape(1), stages)
        slots.update(zip(keys, own))
        return own, outs

    def gather(names, la):
        def then(*bufs):
            full.update(zip([(n, la) for n in names], bufs))

        pipe.add(_gather_stage([slots[(n, la)] for n in names], then))

    first = [(n, 0) for n in mixer_w]
    cast("first", first, [])
    gather(mixer_w, 0)
    tap_slots = {}
    pipe.add(_chip_spread_stage(_pack([w_shortconv, w_ffn_conv]), lambda got: tap_slots.__setitem__("all", got)))
    pipe.carry(lambda st: cast("rest", [(n, la) for la in range(N_LAYERS) for n in BIG_NAMES if (n, la) not in first], st))
    by_chip = [_unpack(tap_slots["all"][k], [w_shortconv, w_ffn_conv]) for k in range(N_CHIPS)]
    wsc_full = jnp.concatenate([t[0] for t in by_chip], axis=-1)
    wfc_full = jnp.concatenate([t[1] for t in by_chip], axis=-1)

    idx = jnp.arange(GMLP_BLOCK) // CHUNK
    mask = idx[None, :] <= idx[:, None]
    wm_all = jnp.where(mask[None, None], w_spatial, 0.0)
    wm_bf = wm_all.astype(BF16)
    wmt_bf = jnp.swapaxes(wm_all, -1, -2).astype(BF16)
    bsf = jnp.repeat(jnp.swapaxes(b_spatial, -1, -2), 128, axis=-1)

    def row(a):
        return a.reshape(1, -1)

    def mixer_args(la):
        return (row(norm1_g[la]), row(b_gate[la]), row(gmlp_ln_g[la]), row(gmlp_ln_b[la]))

    def mixer_weights(la):
        return tuple(full[(n, la)] for n in mixer_w)

    def ffn_weights(la):
        return tuple(full[(n, la)] for n in ffn_w)

    saved = []
    h_in = xs
    for la in range(N_LAYERS):
        gather(ffn_w, la)
        *kept, mg, h1, x2 = pipe.carry(lambda st: _mixer_fwd(
            la, h_in, *mixer_args(la), wm_bf[la], bsf[la], _pad8(wsc_full[la]), *mixer_weights(la), st))
        ya, yb = kept[1], kept[2]
        if la + 1 < N_LAYERS:
            gather(mixer_w, la + 1)
        head = (target, row(final_g)) if la == N_LAYERS - 1 else None
        up, silu, dsilu, act, h2, *rest = pipe.carry(lambda st: _ffn_fwd(
            la, x2, row(norm2_g[la]), _pad8(wfc_full[la]), row(b_ffn_conv[la]), *ffn_weights(la), st, head=head))
        saved.append(dict(x=h_in, ya=ya, yb=yb, mixer=[kept[0]] + kept[3:], mg=mg, h1=h1, x2=x2, up=up, silu=silu,
                          dsilu=dsilu, act=act, h2=h2))
        h_in = rest[0]
    dx, dgf8, loss8 = rest

    reduced_big = {}

    sums_due = []

    def run_sums():
        if sums_due:
            due = list(sums_due)
            sums_due.clear()
            run_sums.calls += 1
            for (_, then), res in zip(due, _reduction_sums(str(run_sums.calls), [job for job, _ in due], pos_arr)):
                then(res)

    run_sums.calls = 0
    pipe.after = run_sums

    def reduce_big(name, la, grad):
        def after_pair(other):
            def after_chips(got):
                sums_due.append((("chip", grad, other, got), lambda final: pipe.add(_pair_fill_stage(
                    final, lambda done: reduced_big.__setitem__((name, la), done)))))

            sums_due.append((("pair", grad, other), lambda psum: pipe.add(_chip_send_stage(psum, after_chips))))

        pipe.add(_pair_send_stage(grad, after_pair))

    small = {n: [None] * N_LAYERS for n in SMALL}
    spread = {}
    wgrad_in = {}
    for la in reversed(range(N_LAYERS)):
        s = saved[la]
        dx3 = dx
        dx2, dup, dx3b, dg2, dbfc, dwfc = _ffn_bwd(
            la, dx3, s["x2"], s["up"], s["silu"], s["dsilu"], row(norm2_g[la]), _pad8(wfc_full[la]),
            *ffn_weights(la))
        dxl, dz, da, db, dx2b, dg1, dbg, dlng, dlnb, dwm, dbsf, dwsc = _mixer_bwd(
            la, dx2, s["x"], *s["mixer"], row(norm1_g[la]), row(gmlp_ln_g[la]), row(gmlp_ln_b[la]), wmt_bf[la],
            _pad8(wsc_full[la]), *mixer_weights(la))
        wgrad_in[la] = dict(s, dup=dup, dx3b=dx3b, dz=dz, da=da, db=db, dx2b=dx2b)
        small["norm1_g"][la] = dg1.sum(0)
        small["b_gate"][la] = dbg.sum(0)
        small["gmlp_ln_g"][la] = dlng.sum(0)
        small["gmlp_ln_b"][la] = dlnb.sum(0)
        small["w_spatial"][la] = jnp.where(mask[None], dwm, 0.0)
        small["b_spatial"][la] = dbsf.reshape(128, A_HEADS, 128).sum(-1).T
        small["w_shortconv"][la] = dwsc.sum(1)
        small["norm2_g"][la] = dg2.sum(0)
        small["w_ffn_conv"][la] = dwfc.sum(1)
        small["b_ffn_conv"][la] = dbfc.sum(0)
        dx = dxl
    grad_x = dx.reshape(x.shape)

    small_local = [jnp.stack(small[n]) for n in SMALL[:-1]] + [dgf8.sum(0), 0.5 * loss8.sum().reshape(1) / D_MODEL]
    mine = _pack(small_local)

    def after_swap(other):
        pair = _sum_slots("small_pair", jnp.stack([mine, other]))
        pipe.add(_chip_spread_stage(pair, lambda slots: spread.__setitem__("slots", slots)))

    pipe.add(_pair_swap_stage(mine, after_swap))

    def wgrad(name, la, st):
        w = wgrad_in[la]
        if name == "w_ffn_up":
            return _wgrad(name, la, w["h2"], w["dup"], 1024, 1408, 512, 2816, st)
        if name == "w_in":
            return _wgrad(name, la, w["h1"], w["dz"], 1024, 1152, 512, 2304, st)
        if name == "w_ffn_down":
            return _wgrad(name, la, w["act"], w["dx3b"], 704, 1024, 1408, 1024, st)
        if name == "w_out":
            return _wgrad(name, la, w["mg"], w["dx2b"], 256, 1024, 1024, 1024, st)
        return _wgrad_branch(la, w["ya"], w["da"], w["yb"], w["db"], st)

    for name in ("w_ffn_up", "w_in", "w_ffn_down", "w_out", "w_branch"):
        for la in reversed(range(N_LAYERS)):
            g, = pipe.carry(lambda st: wgrad(name, la, st), long=name not in ("w_out", "w_branch"))
            reduce_big(name, la, g)
    pipe.flush()

    reduced = _unpack(_sum_slots("small_grads", spread["slots"]), small_local)
    loss = reduced[-1].reshape(())
    grads = dict(zip(SMALL, reduced[:-1]))
    grads["w_shortconv"] = lax.dynamic_slice(grads["w_shortconv"], (0, 0, chip * (D_B // 4)), (N_LAYERS, 3, D_B // 4))
    grads["w_ffn_conv"] = lax.dynamic_slice(grads["w_ffn_conv"], (0, 0, chip * (D_FF // 4)), (N_LAYERS, 3, D_FF // 4))

    delta, new_m, new_v = {}, {}, {}
    for n in BIG_NAMES:
        shape3 = (N_LAYERS,) + BIG[n]
        if n == "w_ffn_down":
            shape2 = (N_LAYERS * BIG[n][0], BIG[n][1])
            g = jnp.concatenate([reduced_big[(n, 0)], reduced_big[(n, 1)]], axis=0)
            res = (g,) + tuple(_adamw_sparsecore(n, weights[n].reshape(shape2), g, mom[n].reshape(shape2),
                                                 vel[n].reshape(shape2)))
            grads[n], delta[n], new_m[n], new_v[n] = (a.reshape(weights[n].shape) for a in res)
            continue
        res = _adamw_big(n, weights[n].reshape(shape3), reduced_big[(n, 0)], reduced_big[(n, 1)],
                         mom[n].reshape(shape3), vel[n].reshape(shape3))
        grads[n], delta[n], new_m[n], new_v[n] = (a.reshape(weights[n].shape) for a in res)
    res = _adamw_small(*[[src[n].reshape(-1, src[n].shape[-1]) for n in SMALL] for src in (weights, grads, mom, vel)])
    for k, n in enumerate(SMALL):
        delta[n], new_m[n], new_v[n] = (res[j * len(SMALL) + k].reshape(weights[n].shape) for j in range(3))

    return (loss, grad_x, *[grads[n] for n in ALL_WEIGHTS], *[delta[n] for n in ALL_WEIGHTS],
            *[new_m[n] for n in ALL_WEIGHTS], *[new_v[n] for n in ALL_WEIGHTS])
```
